```python
import jax, jax.numpy as jnp
from jax import lax
import numpy as np

D_MODEL = 1024
BATCH = 8
SEQ = 4096
DEPTH = 1

ATTN_HEADS = 8
HEAD_DIM = 64
ATTN_WIDTH = ATTN_HEADS * HEAD_DIM
CONV_GROUPS = 8
CONV_WIDTH = D_MODEL // 2
CONV_K = 3
Q_BLOCK = 128
EPS = 1e-6
IN_SIZES = (ATTN_WIDTH, ATTN_WIDTH, ATTN_WIDTH, ATTN_HEADS, ATTN_WIDTH,
            CONV_WIDTH, CONV_WIDTH, CONV_WIDTH, CONV_WIDTH,
            D_MODEL, D_MODEL)
IN_WIDTH = sum(IN_SIZES)

kernel_name = "fox_shortconv_gated_hybrid"


def _rmsnorm(x, g):
    xf = x.astype(jnp.float32)
    xf = xf * lax.rsqrt(jnp.mean(xf * xf, axis=-1, keepdims=True) + EPS)
    return xf.astype(x.dtype) * g


def _fox_attention(q, k, v, log_f):
    b, s, h, dh = q.shape
    nb = s // Q_BLOCK
    cum = jnp.cumsum(log_f, axis=1).transpose(0, 2, 1)
    q_blocks = q.reshape(b, nb, Q_BLOCK, h, dh).swapaxes(0, 1)
    cum_q_blocks = cum.reshape(b, h, nb, Q_BLOCK).transpose(2, 0, 1, 3)
    k_pos = jnp.arange(s)
    scale = dh ** -0.5

    def one_block(args):
        qb, cq, blk = args
        logits = jnp.einsum('bqhd,bkhd->bhqk', qb, k).astype(jnp.float32) * scale
        logits = logits + cq[..., None] - cum[:, :, None, :]
        q_pos = blk * Q_BLOCK + jnp.arange(Q_BLOCK)
        causal = k_pos[None, :] <= q_pos[:, None]
        logits = jnp.where(causal, logits, -jnp.inf)
        p = jax.nn.softmax(logits, axis=-1).astype(v.dtype)
        return jnp.einsum('bhqk,bkhd->bqhd', p, v)

    out = lax.map(one_block, (q_blocks, cum_q_blocks, jnp.arange(nb)))
    return out.swapaxes(0, 1).reshape(b, s, h * dh)


def _causal_dwconv(v, w):
    return lax.conv_general_dilated(
        v, w[:, None, :].astype(v.dtype), window_strides=(1,),
        padding=[(CONV_K - 1, 0)], dimension_numbers=('NWC', 'WIO', 'NWC'),
        feature_group_count=v.shape[-1])


def _layer(x, c, w_ada, b_ada, norm_g, w_in, b_f, q_norm_g, k_norm_g,
           conv_w, w_attn_out, w_conv_out, w_o):
    b, s, _ = x.shape
    ada = c @ w_ada + b_ada
    shift, scale, gate = jnp.split(ada, 3, axis=-1)
    h = _rmsnorm(x, norm_g) * (1 + scale[:, None, :]) + shift[:, None, :]
    proj = h @ w_in
    split_points = np.cumsum(IN_SIZES)[:-1].tolist()
    q, k, v, f_logit, z_a, gb, gc, u, z_b, g_a, g_b = jnp.split(proj, split_points, axis=-1)

    q = _rmsnorm(q.reshape(b, s, ATTN_HEADS, HEAD_DIM), q_norm_g)
    k = _rmsnorm(k.reshape(b, s, ATTN_HEADS, HEAD_DIM), k_norm_g)
    v = v.reshape(b, s, ATTN_HEADS, HEAD_DIM)
    log_f = jax.nn.log_sigmoid((f_logit + b_f).astype(jnp.float32))
    o_a = _fox_attention(q, k, v, log_f) * jax.nn.silu(z_a)

    o_b = gb * _causal_dwconv(gc * u, conv_w) * jax.nn.silu(z_b)

    merged = jax.nn.sigmoid(g_a) * (o_a @ w_attn_out) + jax.nn.sigmoid(g_b) * (o_b @ w_conv_out)
    return x + gate[:, None, :] * (merged @ w_o)


def _fwd_setup_inputs(seed: int = 0) -> dict:
    key = jax.random.key(seed)
    ks = jax.random.split(key, 14)
    f32 = jnp.float32
    nrm = lambda k, shape, s: jax.random.normal(k, shape, f32) * s
    x = jax.random.normal(ks[0], (BATCH, SEQ, D_MODEL), f32)
    c = jax.random.normal(ks[1], (BATCH, D_MODEL), f32)
    w_ada = nrm(ks[2], (DEPTH, D_MODEL, 3 * D_MODEL), 0.5 * D_MODEL ** -0.5)
    b_ada = nrm(ks[3], (DEPTH, 3 * D_MODEL), 0.02)
    norm_g = 1.0 + nrm(ks[4], (DEPTH, D_MODEL), 0.02)
    w_in = nrm(ks[5], (DEPTH, D_MODEL, IN_WIDTH), D_MODEL ** -0.5)
    b_f = 3.0 + nrm(ks[6], (DEPTH, ATTN_HEADS), 0.5)
    q_norm_g = 1.0 + nrm(ks[7], (DEPTH, HEAD_DIM), 0.02)
    k_norm_g = 1.0 + nrm(ks[8], (DEPTH, HEAD_DIM), 0.02)
    conv_w = nrm(ks[9], (DEPTH, CONV_K, CONV_WIDTH), CONV_K ** -0.5)
    w_attn_out = nrm(ks[10], (DEPTH, ATTN_WIDTH, D_MODEL), ATTN_WIDTH ** -0.5)
    w_conv_out = nrm(ks[11], (DEPTH, CONV_WIDTH, D_MODEL), CONV_WIDTH ** -0.5)
    w_o = nrm(ks[12], (DEPTH, D_MODEL, D_MODEL), D_MODEL ** -0.5)
    return {"x": x, "c": c, "w_ada": w_ada, "b_ada": b_ada, "norm_g": norm_g,
            "w_in": w_in, "b_f": b_f, "q_norm_g": q_norm_g, "k_norm_g": k_norm_g,
            "conv_w": conv_w, "w_attn_out": w_attn_out, "w_conv_out": w_conv_out,
            "w_o": w_o}


def _fwd_reference(x, c, w_ada, b_ada, norm_g, w_in, b_f, q_norm_g, k_norm_g,
              conv_w, w_attn_out, w_conv_out, w_o):
    for i in range(DEPTH):
        x = _layer(x, c, w_ada[i], b_ada[i], norm_g[i], w_in[i], b_f[i],
                   q_norm_g[i], k_norm_g[i], conv_w[i], w_attn_out[i],
                   w_conv_out[i], w_o[i])
    return x


import jax as _jax
import jax.numpy as _jnp

TWIN_FORMAT = 'train_step'
FWD_PARAMS = ['x', 'c', 'w_ada', 'b_ada', 'norm_g', 'w_in', 'b_f', 'q_norm_g', 'k_norm_g', 'conv_w', 'w_attn_out', 'w_conv_out', 'w_o']
TWIN_WEIGHTS = ['w_ada', 'b_ada', 'norm_g', 'w_in', 'b_f', 'q_norm_g', 'k_norm_g', 'conv_w', 'w_attn_out', 'w_conv_out', 'w_o']
TWIN_DIFF_INPUT = 'x'
TWIN_INPUTS = ['x', 'c', 'w_ada', 'b_ada', 'norm_g', 'w_in', 'b_f', 'q_norm_g', 'k_norm_g', 'conv_w', 'w_attn_out', 'w_conv_out', 'w_o', 'loss_target', 'm_w_ada', 'm_b_ada', 'm_norm_g', 'm_w_in', 'm_b_f', 'm_q_norm_g', 'm_k_norm_g', 'm_conv_w', 'm_w_attn_out', 'm_w_conv_out', 'm_w_o', 'v_w_ada', 'v_b_ada', 'v_norm_g', 'v_w_in', 'v_b_f', 'v_q_norm_g', 'v_k_norm_g', 'v_conv_w', 'v_w_attn_out', 'v_w_conv_out', 'v_w_o']
TWIN_OUTPUTS = ['loss', 'grad_x', 'grad_w_ada', 'grad_b_ada', 'grad_norm_g', 'grad_w_in', 'grad_b_f', 'grad_q_norm_g', 'grad_k_norm_g', 'grad_conv_w', 'grad_w_attn_out', 'grad_w_conv_out', 'grad_w_o', 'delta_w_ada', 'delta_b_ada', 'delta_norm_g', 'delta_w_in', 'delta_b_f', 'delta_q_norm_g', 'delta_k_norm_g', 'delta_conv_w', 'delta_w_attn_out', 'delta_w_conv_out', 'delta_w_o', 'new_m_w_ada', 'new_m_b_ada', 'new_m_norm_g', 'new_m_w_in', 'new_m_b_f', 'new_m_q_norm_g', 'new_m_k_norm_g', 'new_m_conv_w', 'new_m_w_attn_out', 'new_m_w_conv_out', 'new_m_w_o', 'new_v_w_ada', 'new_v_b_ada', 'new_v_norm_g', 'new_v_w_in', 'new_v_b_f', 'new_v_q_norm_g', 'new_v_k_norm_g', 'new_v_conv_w', 'new_v_w_attn_out', 'new_v_w_conv_out', 'new_v_w_o']
TWIN_LEAF_KINDS = {'loss': 'loss', 'grad_x': 'grad_x', 'grad_w_ada': 'grad_w', 'grad_b_ada': 'grad_w', 'grad_norm_g': 'grad_w', 'grad_w_in': 'grad_w', 'grad_b_f': 'grad_w', 'grad_q_norm_g': 'grad_w', 'grad_k_norm_g': 'grad_w', 'grad_conv_w': 'grad_w', 'grad_w_attn_out': 'grad_w', 'grad_w_conv_out': 'grad_w', 'grad_w_o': 'grad_w', 'delta_w_ada': 'delta_w', 'delta_b_ada': 'delta_w', 'delta_norm_g': 'delta_w', 'delta_w_in': 'delta_w', 'delta_b_f': 'delta_w', 'delta_q_norm_g': 'delta_w', 'delta_k_norm_g': 'delta_w', 'delta_conv_w': 'delta_w', 'delta_w_attn_out': 'delta_w', 'delta_w_conv_out': 'delta_w', 'delta_w_o': 'delta_w', 'new_m_w_ada': 'new_m', 'new_m_b_ada': 'new_m', 'new_m_norm_g': 'new_m', 'new_m_w_in': 'new_m', 'new_m_b_f': 'new_m', 'new_m_q_norm_g': 'new_m', 'new_m_k_norm_g': 'new_m', 'new_m_conv_w': 'new_m', 'new_m_w_attn_out': 'new_m', 'new_m_w_conv_out': 'new_m', 'new_m_w_o': 'new_m', 'new_v_w_ada': 'new_v', 'new_v_b_ada': 'new_v', 'new_v_norm_g': 'new_v', 'new_v_w_in': 'new_v', 'new_v_b_f': 'new_v', 'new_v_q_norm_g': 'new_v', 'new_v_k_norm_g': 'new_v', 'new_v_conv_w': 'new_v', 'new_v_w_attn_out': 'new_v', 'new_v_w_conv_out': 'new_v', 'new_v_w_o': 'new_v'}


def _forward(args):
    return _fwd_reference(*[args[k] for k in FWD_PARAMS])


def _output_shape():
    out = _jax.eval_shape(lambda: _forward(_fwd_setup_inputs(0)))
    return out.shape, out.dtype

N_MICROBATCH = 1
ADAM_LR = 0.001
ADAM_B1 = 0.9
ADAM_B2 = 0.999
ADAM_EPS = 1e-08
ADAM_WD = 0.01
ADAM_STEP = 10
PER_EXAMPLE_BATCH_AXIS = {'x': 0, 'c': 0, 'loss_target': 0}
SHARED_INPUTS = []
_WEIGHT_DTYPES = {'w_ada': _jnp.float32, 'b_ada': _jnp.float32, 'norm_g': _jnp.float32, 'w_in': _jnp.float32, 'b_f': _jnp.float32, 'q_norm_g': _jnp.float32, 'k_norm_g': _jnp.float32, 'conv_w': _jnp.float32, 'w_attn_out': _jnp.float32, 'w_conv_out': _jnp.float32, 'w_o': _jnp.float32}
MOMENT_SCALE = {'w_ada': 7.821726e+00, 'b_ada': 1.185679e+01, 'norm_g': 2.353481e+01, 'w_in': 1.029161e+00, 'b_f': 7.099078e+00, 'q_norm_g': 9.301556e-01, 'k_norm_g': 9.268263e-01, 'conv_w': 7.060825e+00, 'w_attn_out': 9.357113e-02, 'w_conv_out': 4.368372e-01, 'w_o': 3.808498e-01}


def _to_microbatches(a, axis):
    t = _jnp.moveaxis(a, axis, 0)
    t = t.reshape((N_MICROBATCH, t.shape[0] // N_MICROBATCH) + t.shape[1:])
    return _jnp.moveaxis(t, 1, axis + 1)


def setup_inputs(seed: int = 0) -> dict:
    inp = _fwd_setup_inputs(seed)
    key = _jax.random.fold_in(_jax.random.key(seed), 7919)
    shape, _ = _output_shape()
    out = dict(inp)
    out["loss_target"] = _jax.random.normal(_jax.random.fold_in(key, 0), shape, _jnp.float32)
    for i, name in enumerate(TWIN_WEIGHTS):
        w = inp[name].astype(_jnp.float32)
        if MOMENT_SCALE is None:
            s = _jnp.sqrt(_jnp.mean(_jnp.square(w)) + 1e-30)
        else:
            s = MOMENT_SCALE[name]
        km, kv = _jax.random.split(_jax.random.fold_in(key, i + 1))
        out[name] = w
        out["m_" + name] = s * _jax.random.normal(km, w.shape, _jnp.float32)
        out["v_" + name] = (s * s) * _jax.random.uniform(kv, w.shape, _jnp.float32, 0.5, 1.5)
    if N_MICROBATCH > 1:
        for name, axis in PER_EXAMPLE_BATCH_AXIS.items():
            out[name] = _to_microbatches(out[name], axis)
    return {'x': out['x'], 'c': out['c'], 'w_ada': out['w_ada'], 'b_ada': out['b_ada'], 'norm_g': out['norm_g'], 'w_in': out['w_in'], 'b_f': out['b_f'], 'q_norm_g': out['q_norm_g'], 'k_norm_g': out['k_norm_g'], 'conv_w': out['conv_w'], 'w_attn_out': out['w_attn_out'], 'w_conv_out': out['w_conv_out'], 'w_o': out['w_o'], 'loss_target': out['loss_target'], 'm_w_ada': out['m_w_ada'], 'm_b_ada': out['m_b_ada'], 'm_norm_g': out['m_norm_g'], 'm_w_in': out['m_w_in'], 'm_b_f': out['m_b_f'], 'm_q_norm_g': out['m_q_norm_g'], 'm_k_norm_g': out['m_k_norm_g'], 'm_conv_w': out['m_conv_w'], 'm_w_attn_out': out['m_w_attn_out'], 'm_w_conv_out': out['m_w_conv_out'], 'm_w_o': out['m_w_o'], 'v_w_ada': out['v_w_ada'], 'v_b_ada': out['v_b_ada'], 'v_norm_g': out['v_norm_g'], 'v_w_in': out['v_w_in'], 'v_b_f': out['v_b_f'], 'v_q_norm_g': out['v_q_norm_g'], 'v_k_norm_g': out['v_k_norm_g'], 'v_conv_w': out['v_conv_w'], 'v_w_attn_out': out['v_w_attn_out'], 'v_w_conv_out': out['v_w_conv_out'], 'v_w_o': out['v_w_o']}


def _loss(weights, diff, rest, loss_target):
    with _jax.named_scope("forward"):
        args = {**rest, TWIN_DIFF_INPUT: diff, **{k: w.astype(_WEIGHT_DTYPES[k]) for k, w in weights.items()}}
        y = _forward(args)
    with _jax.named_scope("loss_head"):
        err = _jnp.square(y.astype(_jnp.float32) - loss_target)
        return 0.5 * _jnp.sum(_jnp.mean(err, axis=-1)) if err.ndim else 0.5 * err


def _adamw(w, g, m, v):
    m = ADAM_B1 * m + (1.0 - ADAM_B1) * g
    v = ADAM_B2 * v + (1.0 - ADAM_B2) * _jnp.square(g)
    m_hat = m / (1.0 - ADAM_B1 ** ADAM_STEP)
    v_hat = v / (1.0 - ADAM_B2 ** ADAM_STEP)
    delta = -ADAM_LR * (m_hat / (_jnp.sqrt(v_hat) + ADAM_EPS) + ADAM_WD * w)
    return delta, m, v


def reference(x, c, w_ada, b_ada, norm_g, w_in, b_f, q_norm_g, k_norm_g, conv_w, w_attn_out, w_conv_out, w_o, loss_target, m_w_ada, m_b_ada, m_norm_g, m_w_in, m_b_f, m_q_norm_g, m_k_norm_g, m_conv_w, m_w_attn_out, m_w_conv_out, m_w_o, v_w_ada, v_b_ada, v_norm_g, v_w_in, v_b_f, v_q_norm_g, v_k_norm_g, v_conv_w, v_w_attn_out, v_w_conv_out, v_w_o):
    given = dict(x=x, c=c, w_ada=w_ada, b_ada=b_ada, norm_g=norm_g, w_in=w_in, b_f=b_f, q_norm_g=q_norm_g, k_norm_g=k_norm_g, conv_w=conv_w, w_attn_out=w_attn_out, w_conv_out=w_conv_out, w_o=w_o, loss_target=loss_target, m_w_ada=m_w_ada, m_b_ada=m_b_ada, m_norm_g=m_norm_g, m_w_in=m_w_in, m_b_f=m_b_f, m_q_norm_g=m_q_norm_g, m_k_norm_g=m_k_norm_g, m_conv_w=m_conv_w, m_w_attn_out=m_w_attn_out, m_w_conv_out=m_w_conv_out, m_w_o=m_w_o, v_w_ada=v_w_ada, v_b_ada=v_b_ada, v_norm_g=v_norm_g, v_w_in=v_w_in, v_b_f=v_b_f, v_q_norm_g=v_q_norm_g, v_k_norm_g=v_k_norm_g, v_conv_w=v_conv_w, v_w_attn_out=v_w_attn_out, v_w_conv_out=v_w_conv_out, v_w_o=v_w_o)
    weights = {n: given[n] for n in TWIN_WEIGHTS}
    shared = {n: given[n] for n in SHARED_INPUTS}
    per_example = {n: given[n] for n in ['x', 'c']}
    grad_fn = _jax.value_and_grad(_loss, argnums=(0, 1))

    def one_microbatch(ex, loss_target):
        ex = dict(ex)
        diff = ex.pop(TWIN_DIFF_INPUT)
        return grad_fn(weights, diff, {**shared, **ex}, loss_target)

    if N_MICROBATCH == 1:
        loss, (grad_w, grad_x) = one_microbatch(per_example, given["loss_target"])
    else:
        def body(carry, xs):
            loss_sum, grad_sum = carry
            l_k, (gw_k, gx_k) = one_microbatch(xs[0], xs[1])
            with _jax.named_scope("update"):
                return (loss_sum + l_k, _jax.tree.map(_jnp.add, grad_sum, gw_k)), gx_k

        init = (_jnp.zeros((), _jnp.float32), _jax.tree.map(_jnp.zeros_like, weights))
        (loss, grad_w), grad_x = _jax.lax.scan(body, init, (per_example, given["loss_target"]))
    with _jax.named_scope("update"):
        delta_w, new_m, new_v = {}, {}, {}
        for n in TWIN_WEIGHTS:
            delta_w[n], new_m[n], new_v[n] = _adamw(weights[n], grad_w[n], given["m_" + n], given["v_" + n])
    return (loss, grad_x, *[grad_w[n] for n in TWIN_WEIGHTS], *[delta_w[n] for n in TWIN_WEIGHTS],
            *[new_m[n] for n in TWIN_WEIGHTS], *[new_v[n] for n in TWIN_WEIGHTS])
```

```python
import functools

import numpy as np
import jax
import jax.numpy as jnp
from jax import lax
from jax.experimental import pallas as pl
from jax.experimental.pallas import tpu as pltpu

F32 = jnp.float32
BF16 = jnp.bfloat16

D_MODEL = 1024
HEADS = 8
HEAD_DIM = 64
ATTN_W = 512
CONV_W = 512
N_DEV = 8
IN_WIDTH = 6152
IN_SHARD = IN_WIDTH // N_DEV
N_MAIN = 6144
N_FPAD = 128
N_ALL = N_MAIN + N_FPAD
ADA_SHARD = 3 * D_MODEL // N_DEV
EPS = 1e-6
NEG = -1e30

ADAM_LR = 0.001
ADAM_B1 = 0.9
ADAM_B2 = 0.999
ADAM_EPS = 1e-08
ADAM_WD = 0.01
ADAM_STEP = 10

LANES = 128
VMEM_LIMIT = 56 * 1024 * 1024

TM_PROJ = 512
TN_PROJ = 1024
TM_ELEM = 512
TQ = 512
TM_TAIL = 256
TC_CUM = 256
TK_DW = 512
TN_DW = 896
TM_DH = 512
TK_DH = 896

OFF_Q, OFF_K, OFF_V, OFF_ZA, OFF_CONV, OFF_GA, OFF_GB = 0, 512, 1024, 1536, 2048, 4096, 5120


def _params(sem=None):
    return pltpu.CompilerParams(dimension_semantics=sem, vmem_limit_bytes=VMEM_LIMIT)


def _dot(a, b):
    return jnp.dot(a, b, preferred_element_type=F32)


def _dot_nt(a, b):
    return lax.dot_general(a, b, (((1,), (1,)), ((), ())), preferred_element_type=F32)


def _dot_tn(a, b):
    return lax.dot_general(a, b, (((0,), (0,)), ((), ())), preferred_element_type=F32)


def _sigmoid(x):
    return 1.0 / (1.0 + jnp.exp(-x))


def _lane_lo(shape):
    return lax.broadcasted_iota(jnp.int32, shape, len(shape) - 1) < HEAD_DIM


def _seg_sum(z, lo):
    a = jnp.sum(jnp.where(lo, z, 0.0), axis=-1, keepdims=True)
    b = jnp.sum(jnp.where(lo, 0.0, z), axis=-1, keepdims=True)
    return jnp.where(lo, a, b)


def _lane_col(z, lane):
    idx = lax.broadcasted_iota(jnp.int32, z.shape, 1)
    return jnp.sum(jnp.where(idx == lane, z, 0.0), axis=-1, keepdims=True)


def _sub_row(z, row):
    idx = lax.broadcasted_iota(jnp.int32, z.shape, 0)
    return jnp.sum(jnp.where(idx == row, z, 0.0), axis=0, keepdims=True)


def _mesh_pos():
    x, y, c = lax.axis_index("x"), lax.axis_index("y"), lax.axis_index("c")
    return x, y, c, 4 * x + 2 * y + c


def _peer(k, x, y, c):
    px = 1 - x if (k >> 2) & 1 else x
    py = 1 - y if (k >> 1) & 1 else y
    pc = 1 - c if k & 1 else c
    return (px, py, pc), 4 * px + 2 * py + pc


def _exchange(arrs, gather, name):
    n = len(arrs)
    any_spec = pl.BlockSpec(memory_space=pl.ANY)

    def body(*refs):
        ins, outs = refs[:n], refs[n:2 * n]
        send_sems, recv_sems, local_sems = refs[2 * n:]
        x, y, c, me = _mesh_pos()
        copies = []
        for a in range(n):
            own = ins[a] if gather else ins[a].at[me]
            local = pltpu.make_async_copy(own, outs[a].at[me], local_sems.at[a])
            local.start()
            copies.append(local)
            for k in range(1, N_DEV):
                dev, p = _peer(k, x, y, c)
                cp = pltpu.make_async_remote_copy(
                    src_ref=ins[a] if gather else ins[a].at[p],
                    dst_ref=outs[a].at[me],
                    send_sem=send_sems.at[a * (N_DEV - 1) + k - 1],
                    recv_sem=recv_sems.at[a * (N_DEV - 1) + k - 1],
                    device_id=dev, device_id_type=pl.DeviceIdType.MESH)
                cp.start()
                copies.append(cp)
        for cp in copies:
            cp.wait()

    out_shape = [jax.ShapeDtypeStruct((N_DEV,) + a.shape if gather else a.shape, a.dtype) for a in arrs]
    return pl.pallas_call(
        body, name=name, out_shape=out_shape,
        in_specs=[any_spec] * n, out_specs=[any_spec] * n,
        scratch_shapes=[pltpu.SemaphoreType.DMA((n * (N_DEV - 1),)),
                        pltpu.SemaphoreType.DMA((n * (N_DEV - 1),)),
                        pltpu.SemaphoreType.DMA((n,))],
    )(*arrs)


def _ada_exchange(c_row, w_ada_sh):
    def body(c_ref, w_ref, call_ref, adag_ref, mine_ref, send_sems, recv_sems):
        x, y, c, me = _mesh_pos()

        def copy(phase, k, src, dst):
            dev, _ = _peer(k, x, y, c)
            return pltpu.make_async_remote_copy(
                src_ref=src, dst_ref=dst,
                send_sem=send_sems.at[phase * (N_DEV - 1) + k - 1],
                recv_sem=recv_sems.at[phase * (N_DEV - 1) + k - 1],
                device_id=dev, device_id_type=pl.DeviceIdType.MESH)

        call_ref[me] = c_ref[...]
        first = [copy(0, k, c_ref, call_ref.at[me]) for k in range(1, N_DEV)]
        for cp in first:
            cp.start()
        for cp in first:
            cp.wait()
        wb = w_ref[...].astype(BF16)
        for b in range(N_DEV):
            row = jnp.broadcast_to(call_ref[b], (8, D_MODEL)).astype(BF16)
            mine_ref[b] = _sub_row(_dot(row, wb), 0)
        adag_ref[me] = mine_ref[...]
        second = [copy(1, k, mine_ref, adag_ref.at[me]) for k in range(1, N_DEV)]
        for cp in second:
            cp.start()
        for cp in second:
            cp.wait()

    vm = pl.BlockSpec(memory_space=pltpu.VMEM)
    return pl.pallas_call(
        body, name="ada_exchange",
        out_shape=[jax.ShapeDtypeStruct((N_DEV, 1, D_MODEL), F32),
                   jax.ShapeDtypeStruct((N_DEV, N_DEV, 1, ADA_SHARD), F32)],
        in_specs=[vm, vm], out_specs=[vm, vm],
        scratch_shapes=[pltpu.VMEM((N_DEV, 1, ADA_SHARD), F32),
                        pltpu.SemaphoreType.DMA((2 * (N_DEV - 1),)),
                        pltpu.SemaphoreType.DMA((2 * (N_DEV - 1),))],
        compiler_params=pltpu.CompilerParams(vmem_limit_bytes=VMEM_LIMIT),
    )(c_row, w_ada_sh)


def _proj_fwd(x, ada3, norm_g, w_main, w_f):
    s = x.shape[0]
    tm, tn = min(TM_PROJ, s), TN_PROJ

    def body(x_ref, ada_ref, g_ref, w_ref, wf_ref, proj_ref, fl_ref, ht_ref, h_s):
        @pl.when(pl.program_id(1) == 0)
        def _():
            xv = x_ref[...]
            r = lax.rsqrt(jnp.mean(xv * xv, axis=-1, keepdims=True) + EPS)
            hv = ((xv * r) * g_ref[...]) * (1.0 + ada_ref[1:2, :]) + ada_ref[0:1, :]
            hb = hv.astype(BF16)
            h_s[...] = hb
            ht_ref[...] = hv.T.astype(BF16)
            fl_ref[...] = _dot(hb, wf_ref[...])
        proj_ref[...] = _dot(h_s[...], w_ref[...])

    return pl.pallas_call(
        body, name="proj_fwd", grid=(s // tm, N_MAIN // tn),
        in_specs=[pl.BlockSpec((tm, D_MODEL), lambda i, j: (i, 0)),
                  pl.BlockSpec((3, D_MODEL), lambda i, j: (0, 0)),
                  pl.BlockSpec((1, D_MODEL), lambda i, j: (0, 0)),
                  pl.BlockSpec((D_MODEL, tn), lambda i, j: (0, j)),
                  pl.BlockSpec((D_MODEL, N_FPAD), lambda i, j: (0, 0))],
        out_specs=[pl.BlockSpec((tm, tn), lambda i, j: (i, j)),
                   pl.BlockSpec((tm, N_FPAD), lambda i, j: (i, 0)),
                   pl.BlockSpec((D_MODEL, tm), lambda i, j: (0, i))],
        out_shape=[jax.ShapeDtypeStruct((s, N_MAIN), F32),
                   jax.ShapeDtypeStruct((s, N_FPAD), F32),
                   jax.ShapeDtypeStruct((D_MODEL, s), BF16)],
        scratch_shapes=[pltpu.VMEM((tm, D_MODEL), BF16)],
        compiler_params=_params(("parallel", "arbitrary")),
    )(x, ada3, norm_g, w_main, w_f)


def _qkv_prep(proj, qg, kg):
    s = proj.shape[0]
    tm = min(TM_ELEM, s)
    scale = HEAD_DIM ** -0.5

    def body(p_ref, qg_ref, kg_ref, qs_ref, kn_ref, v_ref):
        lo = _lane_lo((tm, LANES))
        for pr in range(ATTN_W // LANES):
            sl = slice(pr * LANES, (pr + 1) * LANES)
            q2 = p_ref[:, OFF_Q + pr * LANES:OFF_Q + (pr + 1) * LANES]
            k2 = p_ref[:, OFF_K + pr * LANES:OFF_K + (pr + 1) * LANES]
            rq = lax.rsqrt(_seg_sum(q2 * q2, lo) * (1.0 / HEAD_DIM) + EPS)
            rk = lax.rsqrt(_seg_sum(k2 * k2, lo) * (1.0 / HEAD_DIM) + EPS)
            qs_ref[:, sl] = (((q2 * rq) * qg_ref[:, sl]) * scale).astype(BF16)
            kn_ref[:, sl] = ((k2 * rk) * kg_ref[:, sl]).astype(BF16)
        v_ref[...] = p_ref[:, OFF_V:OFF_V + ATTN_W].astype(BF16)

    row = pl.BlockSpec((tm, ATTN_W), lambda i: (i, 0))
    vec = pl.BlockSpec((1, ATTN_W), lambda i: (0, 0))
    return pl.pallas_call(
        body, name="qkv_prep", grid=(s // tm,),
        in_specs=[pl.BlockSpec((tm, 3 * ATTN_W), lambda i: (i, 0)), vec, vec],
        out_specs=[row, row, row],
        out_shape=[jax.ShapeDtypeStruct((s, ATTN_W), BF16)] * 3,
        compiler_params=_params(("parallel",)),
    )(proj, qg, kg)


def _log_forget(fl, bf):
    z = fl + bf
    lf = jnp.minimum(z, 0.0) - jnp.log1p(jnp.exp(-jnp.abs(z)))
    lane = lax.broadcasted_iota(jnp.int32, z.shape, 1)
    return jnp.where(lane < HEADS, lf, 0.0)


def _forget_cumsum(fl, bf_pad):
    s = fl.shape[0]
    tc = min(TC_CUM, s)

    def body(fl_ref, bf_ref, cum_ref, cumt_ref, carry):
        @pl.when(pl.program_id(0) == 0)
        def _():
            carry[...] = jnp.zeros_like(carry)
        lf = _log_forget(fl_ref[...], bf_ref[...])
        r = lax.broadcasted_iota(jnp.int32, (tc, tc), 0)
        cidx = lax.broadcasted_iota(jnp.int32, (tc, tc), 1)
        tri = (cidx <= r).astype(F32)
        cs = jnp.dot(tri, lf, preferred_element_type=F32, precision=lax.Precision.HIGHEST) + carry[...]
        cum_ref[...] = cs
        cumt_ref[...] = cs.T[0:HEADS, :]
        carry[...] = cum_ref[tc - 1:tc, :]

    return pl.pallas_call(
        body, name="forget_cumsum", grid=(s // tc,),
        in_specs=[pl.BlockSpec((tc, LANES), lambda i: (i, 0)),
                  pl.BlockSpec((1, LANES), lambda i: (0, 0))],
        out_specs=[pl.BlockSpec((tc, LANES), lambda i: (i, 0)),
                   pl.BlockSpec((HEADS, tc), lambda i: (0, i))],
        out_shape=[jax.ShapeDtypeStruct((s, LANES), F32),
                   jax.ShapeDtypeStruct((HEADS, s), F32)],
        scratch_shapes=[pltpu.VMEM((1, LANES), F32)],
        compiler_params=_params(("arbitrary",)),
    )(fl, bf_pad)


def _causal(i, j, t):
    rows = i * t + lax.broadcasted_iota(jnp.int32, (t, t), 0)
    cols = j * t + lax.broadcasted_iota(jnp.int32, (t, t), 1)
    return cols <= rows


def _attn_fwd(qs, kn, vb, cum, cumt4, proj):
    s = qs.shape[0]
    t = min(TQ, s)
    nt = s // t
    za_blk = OFF_ZA // LANES

    def body(q_ref, k_ref, v_ref, cum_ref, cumt_ref, za_ref, attn_ref, oa_ref, lse_ref,
             m_s, l_s, fq_s, acc_s):
        pr, i, j = pl.program_id(0), pl.program_id(1), pl.program_id(2)

        @pl.when(j == 0)
        def _():
            m_s[...] = jnp.full_like(m_s, NEG)
            l_s[...] = jnp.zeros_like(l_s)
            acc_s[...] = jnp.zeros_like(acc_s)
            for hh in range(2):
                fq_s[hh] = _lane_col(cum_ref[...], 2 * pr + hh)

        @pl.when(j <= i)
        def _():
            lo_q = _lane_lo((t, LANES))
            q2, k2, v2 = q_ref[...], k_ref[...], v_ref[...]
            keep = _causal(i, j, t)
            zero = jnp.zeros_like(q2)
            alphas, pvs = [], []
            for hh in range(2):
                sel = lo_q if hh == 0 else jnp.logical_not(lo_q)
                sc = _dot_nt(jnp.where(sel, q2, zero), k2)
                sc = sc + fq_s[hh] - cumt_ref[0, hh:hh + 1, :]
                sc = jnp.where(keep, sc, NEG)
                m_prev = m_s[hh]
                m_next = jnp.maximum(m_prev, jnp.max(sc, axis=-1, keepdims=True))
                alpha = jnp.exp(m_prev - m_next)
                p = jnp.exp(sc - m_next)
                l_s[hh] = alpha * l_s[hh] + jnp.sum(p, axis=-1, keepdims=True)
                m_s[hh] = m_next
                alphas.append(alpha)
                pvs.append(_dot(p.astype(BF16), jnp.where(sel, v2, zero)))
            acc_s[...] = acc_s[...] * jnp.where(lo_q, alphas[0], alphas[1]) + pvs[0] + pvs[1]

        @pl.when(j == i)
        def _():
            lo_q = _lane_lo((t, LANES))
            out = acc_s[...] / jnp.where(lo_q, l_s[0], l_s[1])
            attn_ref[...] = out
            lse_ref[...] = jnp.where(lo_q, m_s[0] + jnp.log(l_s[0]), m_s[1] + jnp.log(l_s[1]))
            z = za_ref[...]
            oa_ref[...] = (out * (z * _sigmoid(z))).astype(BF16)

    qspec = pl.BlockSpec((t, LANES), lambda p, i, j: (i, p))
    kspec = pl.BlockSpec((t, LANES), lambda p, i, j: (jnp.minimum(j, i), p))
    return pl.pallas_call(
        body, name="attn_fwd", grid=(ATTN_W // LANES, nt, nt),
        in_specs=[qspec, kspec, kspec,
                  pl.BlockSpec((t, LANES), lambda p, i, j: (i, 0)),
                  pl.BlockSpec((1, 8, t), lambda p, i, j: (p, 0, jnp.minimum(j, i))),
                  pl.BlockSpec((t, LANES), lambda p, i, j: (i, za_blk + p))],
        out_specs=[qspec, qspec, qspec],
        out_shape=[jax.ShapeDtypeStruct((s, ATTN_W), F32),
                   jax.ShapeDtypeStruct((s, ATTN_W), BF16),
                   jax.ShapeDtypeStruct((s, ATTN_W), F32)],
        scratch_shapes=[pltpu.VMEM((2, t, 1), F32), pltpu.VMEM((2, t, 1), F32),
                        pltpu.VMEM((2, t, 1), F32), pltpu.VMEM((t, LANES), F32)],
        compiler_params=_params(("parallel", "parallel", "arbitrary")),
    )(qs, kn, vb, cum, cumt4, proj)


def _conv_parts(blk, halo, first, w_ref, tm):
    gb, gc = blk[:, 0:LANES], blk[:, LANES:2 * LANES]
    u, zb = blk[:, 2 * LANES:3 * LANES], blk[:, 3 * LANES:4 * LANES]
    cu = gc * u
    cu_h = jnp.where(first, 0.0, halo[:, LANES:2 * LANES] * halo[:, 2 * LANES:3 * LANES])
    prev1, prev2 = _sub_row(cu_h, 7), _sub_row(cu_h, 6)
    row = lax.broadcasted_iota(jnp.int32, (tm, LANES), 0)
    r1 = jnp.where(row == 0, prev1, pltpu.roll(cu, 1, 0))
    r2 = jnp.where(row == 0, prev2, jnp.where(row == 1, prev1, pltpu.roll(cu, 2, 0)))
    conv = w_ref[2:3, :] * cu + w_ref[1:2, :] * r1 + w_ref[0:1, :] * r2
    return gb, gc, u, zb, cu, r1, r2, conv


def _conv_fwd(proj, conv_w):
    s = proj.shape[0]
    tm = min(TM_ELEM, s)
    cb = OFF_CONV // (4 * LANES)

    def body(p_ref, halo_ref, w_ref, ob_ref):
        first = pl.program_id(1) == 0
        gb, _, _, zb, _, _, _, conv = _conv_parts(p_ref[...], halo_ref[...], first, w_ref, tm)
        ob_ref[...] = (gb * conv * (zb * _sigmoid(zb))).astype(BF16)

    return pl.pallas_call(
        body, name="conv_fwd", grid=(CONV_W // LANES, s // tm),
        in_specs=[pl.BlockSpec((tm, 4 * LANES), lambda c, i: (i, cb + c)),
                  pl.BlockSpec((8, 4 * LANES), lambda c, i: (jnp.maximum(i * (tm // 8) - 1, 0), cb + c)),
                  pl.BlockSpec((3, LANES), lambda c, i: (0, c))],
        out_specs=pl.BlockSpec((tm, LANES), lambda c, i: (i, c)),
        out_shape=jax.ShapeDtypeStruct((s, CONV_W), BF16),
        compiler_params=_params(("parallel", "parallel")),
    )(proj, proj, conv_w)


def _tail(oa, ob, proj, x, target, ada3, wa, wb, wo):
    s = x.shape[0]
    tm = min(TM_TAIL, s)
    gab_blk = OFF_GA // (2 * D_MODEL)

    def body(oa_ref, ob_ref, gab_ref, x_ref, t_ref, ada_ref, wa_ref, wb_ref, wo_ref,
             dy_ref, dgab_ref, doa_ref, dob_ref, dwo_ref, dwa_ref, dwb_ref, dgate_ref, loss_ref):
        @pl.when(pl.program_id(0) == 0)
        def _():
            dwo_ref[...] = jnp.zeros_like(dwo_ref)
            dwa_ref[...] = jnp.zeros_like(dwa_ref)
            dwb_ref[...] = jnp.zeros_like(dwb_ref)
            dgate_ref[...] = jnp.zeros_like(dgate_ref)
            loss_ref[...] = jnp.zeros_like(loss_ref)

        oa_v, ob_v = oa_ref[...], ob_ref[...]
        wa_v, wb_v, wo_v = wa_ref[...], wb_ref[...], wo_ref[...]
        a2 = _dot(oa_v, wa_v)
        b2 = _dot(ob_v, wb_v)
        sa = _sigmoid(gab_ref[:, 0:D_MODEL])
        sb = _sigmoid(gab_ref[:, D_MODEL:2 * D_MODEL])
        mb = (sa * a2 + sb * b2).astype(BF16)
        mo = _dot(mb, wo_v)
        gate = ada_ref[2:3, :]
        err = (x_ref[...] + gate * mo) - t_ref[...]
        dy = err * (1.0 / D_MODEL)
        dy_ref[...] = dy
        loss_ref[...] += 0.5 * jnp.sum(err * err) * (1.0 / D_MODEL)
        dgate_ref[...] += jnp.sum(dy * mo, axis=0, keepdims=True)
        dmo = (dy * gate).astype(BF16)
        dmerged = _dot_nt(dmo, wo_v)
        dwo_ref[...] += _dot_tn(mb, dmo)
        da2 = (dmerged * sa).astype(BF16)
        db2 = (dmerged * sb).astype(BF16)
        dgab_ref[:, 0:D_MODEL] = (dmerged * a2 * (sa * (1.0 - sa))).astype(BF16)
        dgab_ref[:, D_MODEL:2 * D_MODEL] = (dmerged * b2 * (sb * (1.0 - sb))).astype(BF16)
        doa_ref[...] = _dot_nt(da2, wa_v)
        dob_ref[...] = _dot_nt(db2, wb_v)
        dwa_ref[...] += _dot_tn(oa_v, da2)
        dwb_ref[...] += _dot_tn(ob_v, db2)

    half = pl.BlockSpec((tm, ATTN_W), lambda i: (i, 0))
    full = pl.BlockSpec((tm, D_MODEL), lambda i: (i, 0))

    def const(shape):
        return pl.BlockSpec(shape, lambda i: (0, 0))

    return pl.pallas_call(
        body, name="tail", grid=(s // tm,),
        in_specs=[half, half, pl.BlockSpec((tm, 2 * D_MODEL), lambda i: (i, gab_blk)), full, full,
                  const((3, D_MODEL)), const((ATTN_W, D_MODEL)), const((CONV_W, D_MODEL)),
                  const((D_MODEL, D_MODEL))],
        out_specs=[full, pl.BlockSpec((tm, 2 * D_MODEL), lambda i: (i, 0)), half, half,
                   const((D_MODEL, D_MODEL)), const((ATTN_W, D_MODEL)), const((CONV_W, D_MODEL)),
                   const((1, D_MODEL)), const((1, LANES))],
        out_shape=[jax.ShapeDtypeStruct((s, D_MODEL), F32),
                   jax.ShapeDtypeStruct((s, 2 * D_MODEL), BF16),
                   jax.ShapeDtypeStruct((s, ATTN_W), F32),
                   jax.ShapeDtypeStruct((s, CONV_W), F32),
                   jax.ShapeDtypeStruct((D_MODEL, D_MODEL), F32),
                   jax.ShapeDtypeStruct((ATTN_W, D_MODEL), F32),
                   jax.ShapeDtypeStruct((CONV_W, D_MODEL), F32),
                   jax.ShapeDtypeStruct((1, D_MODEL), F32),
                   jax.ShapeDtypeStruct((1, LANES), F32)],
        compiler_params=_params(("arbitrary",)),
    )(oa, ob, proj, x, target, ada3, wa, wb, wo)


def _attn_bwd_prep(doa, attn, proj):
    s = doa.shape[0]
    tm = min(TM_ELEM, s)
    za_blk = OFF_ZA // ATTN_W

    def body(doa_ref, attn_ref, za_ref, do_ref, delta_ref, dza_ref):
        lo = _lane_lo((tm, LANES))
        for pr in range(ATTN_W // LANES):
            sl = slice(pr * LANES, (pr + 1) * LANES)
            g, a, z = doa_ref[:, sl], attn_ref[:, sl], za_ref[:, sl]
            sg = _sigmoid(z)
            datb = (g * (z * sg)).astype(BF16)
            do_ref[:, sl] = datb
            delta_ref[:, sl] = _seg_sum(datb.astype(F32) * a, lo)
            dza_ref[:, sl] = (g * a * (sg * (1.0 + z * (1.0 - sg)))).astype(BF16)

    row = pl.BlockSpec((tm, ATTN_W), lambda i: (i, 0))
    return pl.pallas_call(
        body, name="attn_bwd_prep", grid=(s // tm,),
        in_specs=[row, row, pl.BlockSpec((tm, ATTN_W), lambda i: (i, za_blk))],
        out_specs=[row, row, row],
        out_shape=[jax.ShapeDtypeStruct((s, ATTN_W), BF16),
                   jax.ShapeDtypeStruct((s, ATTN_W), F32),
                   jax.ShapeDtypeStruct((s, ATTN_W), BF16)],
        compiler_params=_params(("parallel",)),
    )(doa, attn, proj)


def _attn_bwd(qs, kn, vb, do, lse, delta, cum, cumt4):
    s = qs.shape[0]
    t = min(TQ, s)
    nt = s // t

    def body(q_ref, k_ref, v_ref, do_ref, lse_ref, delta_ref, cum_ref, cumt_ref,
             dq_ref, dk_ref, dv_ref, dc_ref, dr_ref):
        pr, j, i = pl.program_id(0), pl.program_id(1), pl.program_id(2)

        @pl.when(jnp.logical_and(j == 0, i == 0))
        def _():
            dq_ref[...] = jnp.zeros_like(dq_ref)
            dr_ref[...] = jnp.zeros_like(dr_ref)

        @pl.when(i == 0)
        def _():
            dk_ref[...] = jnp.zeros_like(dk_ref)
            dv_ref[...] = jnp.zeros_like(dv_ref)
            dc_ref[...] = jnp.zeros_like(dc_ref)

        @pl.when(i >= j)
        def _():
            lo_q = _lane_lo((t, LANES))
            q2, k2, v2, do2 = q_ref[...], k_ref[...], v_ref[...], do_ref[...]
            lse2, delta2, cum2 = lse_ref[...], delta_ref[...], cum_ref[...]
            keep = _causal(i, j, t)
            zero = jnp.zeros_like(q2)
            dq_acc = jnp.zeros((t, LANES), F32)
            row_sums = []
            for hh in range(2):
                sel = lo_q if hh == 0 else jnp.logical_not(lo_q)
                qh, kh = jnp.where(sel, q2, zero), jnp.where(sel, k2, zero)
                doh = jnp.where(sel, do2, zero)
                sc = _dot_nt(qh, k2)
                sc = sc + _lane_col(cum2, 2 * pr + hh) - cumt_ref[0, hh:hh + 1, :]
                sc = jnp.where(keep, sc, NEG)
                p = jnp.exp(sc - _lane_col(lse2, hh * HEAD_DIM))
                dp = _dot_nt(doh, v2)
                ds = p * (dp - _lane_col(delta2, hh * HEAD_DIM))
                pb, dsb = p.astype(BF16), ds.astype(BF16)
                dv_ref[...] += _dot_tn(pb, doh)
                dk_ref[...] += _dot_tn(dsb, qh)
                dq_acc = dq_acc + _dot(dsb, kh)
                dc_ref[0, hh:hh + 1, :] += -jnp.sum(ds, axis=0, keepdims=True)
                row_sums.append(jnp.sum(ds, axis=-1, keepdims=True))
            rows = pl.ds(pl.multiple_of(i * t, t), t)
            dq_ref[rows, :] += dq_acc
            dr_ref[rows, :] += jnp.where(lo_q, row_sums[0], row_sums[1])

    qspec = pl.BlockSpec((t, LANES), lambda p, j, i: (jnp.maximum(i, j), p))
    kspec = pl.BlockSpec((t, LANES), lambda p, j, i: (j, p))
    cspec = pl.BlockSpec((1, 8, t), lambda p, j, i: (p, 0, j))
    return pl.pallas_call(
        body, name="attn_bwd", grid=(ATTN_W // LANES, nt, nt),
        in_specs=[qspec, kspec, kspec, qspec, qspec, qspec,
                  pl.BlockSpec((t, LANES), lambda p, j, i: (jnp.maximum(i, j), 0)), cspec],
        out_specs=[pl.BlockSpec((s, LANES), lambda p, j, i: (0, p)), kspec, kspec, cspec,
                   pl.BlockSpec((s, LANES), lambda p, j, i: (0, p))],
        out_shape=[jax.ShapeDtypeStruct((s, ATTN_W), F32)] * 3
        + [jax.ShapeDtypeStruct((ATTN_W // LANES, 8, s), F32), jax.ShapeDtypeStruct((s, ATTN_W), F32)],
        compiler_params=_params(("parallel", "arbitrary", "arbitrary")),
    )(qs, kn, vb, do, lse, delta, cum, cumt4)


def _forget_bwd(dcum_col, drow_col, fl, bf_pad):
    s = fl.shape[0]
    tc = min(TC_CUM, s)
    n = s // tc

    def body(dc_ref, dr_ref, fl_ref, bf_ref, df_ref, dbf_ref, carry):
        @pl.when(pl.program_id(0) == 0)
        def _():
            carry[...] = jnp.zeros_like(carry)
            dbf_ref[...] = jnp.zeros_like(dbf_ref)
        r = lax.broadcasted_iota(jnp.int32, (tc, tc), 0)
        cidx = lax.broadcasted_iota(jnp.int32, (tc, tc), 1)
        tri = (cidx >= r).astype(F32)
        dc = dc_ref[...] + dr_ref[...]
        dlf = jnp.dot(tri, dc, preferred_element_type=F32, precision=lax.Precision.HIGHEST) + carry[...]
        carry[...] += jnp.sum(dc, axis=0, keepdims=True)
        lane = lax.broadcasted_iota(jnp.int32, (tc, LANES), 1)
        dfl = jnp.where(lane < HEADS, dlf * _sigmoid(-(fl_ref[...] + bf_ref[...])), 0.0)
        df_ref[...] = dfl.astype(BF16)
        dbf_ref[...] += jnp.sum(dfl, axis=0, keepdims=True)

    rev = pl.BlockSpec((tc, LANES), lambda i: (n - 1 - i, 0))
    vec = pl.BlockSpec((1, LANES), lambda i: (0, 0))
    return pl.pallas_call(
        body, name="forget_bwd", grid=(n,),
        in_specs=[rev, rev, rev, vec], out_specs=[rev, vec],
        out_shape=[jax.ShapeDtypeStruct((s, LANES), BF16), jax.ShapeDtypeStruct((1, LANES), F32)],
        scratch_shapes=[pltpu.VMEM((1, LANES), F32)],
        compiler_params=_params(("arbitrary",)),
    )(dcum_col, drow_col, fl, bf_pad)


def _qk_norm_bwd(dq, dk, dv, proj, qg, kg):
    s = dq.shape[0]
    tm = min(TM_ELEM, s)
    scale = HEAD_DIM ** -0.5

    def body(dq_ref, dk_ref, dv_ref, p_ref, qg_ref, kg_ref, out_ref, dqg_ref, dkg_ref):
        @pl.when(pl.program_id(0) == 0)
        def _():
            dqg_ref[...] = jnp.zeros_like(dqg_ref)
            dkg_ref[...] = jnp.zeros_like(dkg_ref)
        lo = _lane_lo((tm, LANES))

        def one(raw, dy, g, dg_ref, sl, off):
            r = lax.rsqrt(_seg_sum(raw * raw, lo) * (1.0 / HEAD_DIM) + EPS)
            xhat = raw * r
            dg_ref[:, sl] += jnp.sum(dy * xhat, axis=0, keepdims=True)
            dxh = dy * g
            dx = r * (dxh - xhat * (_seg_sum(dxh * xhat, lo) * (1.0 / HEAD_DIM)))
            out_ref[:, off + sl.start:off + sl.stop] = dx.astype(BF16)

        for pr in range(ATTN_W // LANES):
            sl = slice(pr * LANES, (pr + 1) * LANES)
            one(p_ref[:, OFF_Q + sl.start:OFF_Q + sl.stop], dq_ref[:, sl] * scale, qg_ref[:, sl], dqg_ref, sl, OFF_Q)
            one(p_ref[:, OFF_K + sl.start:OFF_K + sl.stop], dk_ref[:, sl], kg_ref[:, sl], dkg_ref, sl, OFF_K)
        out_ref[:, OFF_V:OFF_V + ATTN_W] = dv_ref[...].astype(BF16)

    row = pl.BlockSpec((tm, ATTN_W), lambda i: (i, 0))
    vec = pl.BlockSpec((1, ATTN_W), lambda i: (0, 0))
    return pl.pallas_call(
        body, name="qk_norm_bwd", grid=(s // tm,),
        in_specs=[row, row, row, pl.BlockSpec((tm, 2 * ATTN_W), lambda i: (i, 0)), vec, vec],
        out_specs=[pl.BlockSpec((tm, 3 * ATTN_W), lambda i: (i, 0)), vec, vec],
        out_shape=[jax.ShapeDtypeStruct((s, 3 * ATTN_W), BF16),
                   jax.ShapeDtypeStruct((1, ATTN_W), F32), jax.ShapeDtypeStruct((1, ATTN_W), F32)],
        compiler_params=_params(("arbitrary",)),
    )(dq, dk, dv, proj, qg, kg)


def _conv_bwd(dob, proj, conv_w):
    s = dob.shape[0]
    tm = min(TM_ELEM, s)
    cb = OFF_CONV // (4 * LANES)
    nblk8 = s // 8

    def body(dob_ref, p_ref, prev_ref, next_ref, dnext_ref, w_ref, out_ref, dw_ref):
        i = pl.program_id(1)

        @pl.when(i == 0)
        def _():
            dw_ref[...] = jnp.zeros_like(dw_ref)
        gb, gc, u, zb, cu, r1, r2, conv = _conv_parts(p_ref[...], prev_ref[...], i == 0, w_ref, tm)
        g = dob_ref[...]
        sg = _sigmoid(zb)
        sz = zb * sg
        dconv = g * gb * sz
        nxt = next_ref[...]
        zn = nxt[:, 3 * LANES:4 * LANES]
        dcn = jnp.where(i == pl.num_programs(1) - 1, 0.0,
                        dnext_ref[...] * nxt[:, 0:LANES] * (zn * _sigmoid(zn)))
        nxt1, nxt2 = _sub_row(dcn, 0), _sub_row(dcn, 1)
        row = lax.broadcasted_iota(jnp.int32, (tm, LANES), 0)
        f1 = jnp.where(row == tm - 1, nxt1, pltpu.roll(dconv, tm - 1, 0))
        f2 = jnp.where(row == tm - 2, nxt1, jnp.where(row == tm - 1, nxt2, pltpu.roll(dconv, tm - 2, 0)))
        dcu = w_ref[2:3, :] * dconv + w_ref[1:2, :] * f1 + w_ref[0:1, :] * f2
        out_ref[:, 0:LANES] = (g * conv * sz).astype(BF16)
        out_ref[:, LANES:2 * LANES] = (dcu * u).astype(BF16)
        out_ref[:, 2 * LANES:3 * LANES] = (dcu * gc).astype(BF16)
        out_ref[:, 3 * LANES:4 * LANES] = (g * gb * conv * (sg * (1.0 + zb * (1.0 - sg)))).astype(BF16)
        w_row = lax.broadcasted_iota(jnp.int32, (3, LANES), 0)
        dw0 = jnp.sum(dconv * r2, axis=0, keepdims=True)
        dw1 = jnp.sum(dconv * r1, axis=0, keepdims=True)
        dw2 = jnp.sum(dconv * cu, axis=0, keepdims=True)
        dw_ref[...] += jnp.where(w_row == 0, dw0, jnp.where(w_row == 1, dw1, dw2))

    nxt_idx = lambda i: jnp.minimum((i + 1) * (tm // 8), nblk8 - 1)
    return pl.pallas_call(
        body, name="conv_bwd", grid=(CONV_W // LANES, s // tm),
        in_specs=[pl.BlockSpec((tm, LANES), lambda c, i: (i, c)),
                  pl.BlockSpec((tm, 4 * LANES), lambda c, i: (i, cb + c)),
                  pl.BlockSpec((8, 4 * LANES), lambda c, i: (jnp.maximum(i * (tm // 8) - 1, 0), cb + c)),
                  pl.BlockSpec((8, 4 * LANES), lambda c, i: (nxt_idx(i), cb + c)),
                  pl.BlockSpec((8, LANES), lambda c, i: (nxt_idx(i), c)),
                  pl.BlockSpec((3, LANES), lambda c, i: (0, c))],
        out_specs=[pl.BlockSpec((tm, 4 * LANES), lambda c, i: (i, c)),
                   pl.BlockSpec((3, LANES), lambda c, i: (0, c))],
        out_shape=[jax.ShapeDtypeStruct((s, 4 * CONV_W), BF16), jax.ShapeDtypeStruct((3, CONV_W), F32)],
        compiler_params=_params(("parallel", "arbitrary")),
    )(dob, proj, proj, proj, dob, conv_w)


def _dw_in(ht, dproj):
    s = ht.shape[1]
    tk, tn = min(TK_DW, s), TN_DW

    def body(ht_ref, dp_ref, out_ref):
        @pl.when(pl.program_id(1) == 0)
        def _():
            out_ref[...] = jnp.zeros_like(out_ref)
        out_ref[...] += _dot(ht_ref[...], dp_ref[...])

    return pl.pallas_call(
        body, name="dw_in", grid=(N_ALL // tn, s // tk),
        in_specs=[pl.BlockSpec((D_MODEL, tk), lambda n, k: (0, k)),
                  pl.BlockSpec((tk, tn), lambda n, k: (k, n))],
        out_specs=pl.BlockSpec((D_MODEL, tn), lambda n, k: (0, n)),
        out_shape=jax.ShapeDtypeStruct((D_MODEL, N_ALL), F32),
        compiler_params=_params(("parallel", "arbitrary")),
    )(ht, dproj)


def _dh_and_dx(dproj, w_all_t, x, dy, ada3, norm_g):
    s = x.shape[0]
    tm, tk = min(TM_DH, s), TK_DH
    nk = N_ALL // tk

    def body(dp_ref, wt_ref, x_ref, dy_ref, ada_ref, g_ref, gx_ref, dsh_ref, dsc_ref, dg_ref, acc):
        i, k = pl.program_id(0), pl.program_id(1)

        @pl.when(jnp.logical_and(i == 0, k == 0))
        def _():
            dsh_ref[...] = jnp.zeros_like(dsh_ref)
            dsc_ref[...] = jnp.zeros_like(dsc_ref)
            dg_ref[...] = jnp.zeros_like(dg_ref)

        @pl.when(k == 0)
        def _():
            acc[...] = jnp.zeros_like(acc)
        acc[...] += _dot(dp_ref[...], wt_ref[...])

        @pl.when(k == nk - 1)
        def _():
            dh = acc[...]
            xv = x_ref[...]
            r = lax.rsqrt(jnp.mean(xv * xv, axis=-1, keepdims=True) + EPS)
            xhat = xv * r
            g = g_ref[...]
            one_sc = 1.0 + ada_ref[1:2, :]
            dsh_ref[...] += jnp.sum(dh, axis=0, keepdims=True)
            dsc_ref[...] += jnp.sum(dh * (xhat * g), axis=0, keepdims=True)
            dg_ref[...] += jnp.sum(dh * xhat, axis=0, keepdims=True) * one_sc
            dxh = dh * (g * one_sc)
            dx = r * (dxh - xhat * jnp.mean(dxh * xhat, axis=-1, keepdims=True))
            gx_ref[...] = dy_ref[...] + dx

    full = pl.BlockSpec((tm, D_MODEL), lambda i, k: (i, 0))
    vec = pl.BlockSpec((1, D_MODEL), lambda i, k: (0, 0))
    return pl.pallas_call(
        body, name="dh_dx", grid=(s // tm, nk),
        in_specs=[pl.BlockSpec((tm, tk), lambda i, k: (i, k)),
                  pl.BlockSpec((tk, D_MODEL), lambda i, k: (k, 0)),
                  full, full, pl.BlockSpec((3, D_MODEL), lambda i, k: (0, 0)), vec],
        out_specs=[full, vec, vec, vec],
        out_shape=[jax.ShapeDtypeStruct((s, D_MODEL), F32)] + [jax.ShapeDtypeStruct((1, D_MODEL), F32)] * 3,
        scratch_shapes=[pltpu.VMEM((tm, D_MODEL), F32)],
        compiler_params=_params(("arbitrary", "arbitrary")),
    )(dproj, w_all_t, x, dy, ada3, norm_g)


def _sum_small(vec_all, qg_parts, kg_parts):
    def body(v_ref, q_ref, k_ref, tot_ref, gq_ref, gk_ref):
        tot = v_ref[0:1, :]
        for p in range(1, N_DEV):
            tot = tot + v_ref[p:p + 1, :]
        tot_ref[...] = tot
        gq_ref[...] = jnp.sum(q_ref[...], axis=0, keepdims=True)
        gk_ref[...] = jnp.sum(k_ref[...], axis=0, keepdims=True)

    n = vec_all.shape[-1]
    return pl.pallas_call(
        body, name="sum_small",
        out_shape=[jax.ShapeDtypeStruct((1, n), F32),
                   jax.ShapeDtypeStruct((1, HEAD_DIM), F32), jax.ShapeDtypeStruct((1, HEAD_DIM), F32)],
        compiler_params=_params(),
    )(vec_all, qg_parts, kg_parts)


def _grad_w_ada(c_cols, dada_rows):
    def body(c_ref, d_ref, out_ref):
        acc = c_ref[0] * d_ref[0]
        for b in range(1, N_DEV):
            acc = acc + c_ref[b] * d_ref[b]
        out_ref[...] = acc

    return pl.pallas_call(
        body, name="grad_w_ada",
        out_shape=jax.ShapeDtypeStruct((D_MODEL, ADA_SHARD), F32),
        compiler_params=_params(),
    )(c_cols, dada_rows)


def _adamw(w, m, v, g_parts, name):
    rows, cols = w.shape
    n_parts = g_parts.shape[0]
    tr = 256 if rows % 256 == 0 else rows
    c1 = 1.0 / (1.0 - ADAM_B1 ** ADAM_STEP)
    c2 = 1.0 / (1.0 - ADAM_B2 ** ADAM_STEP)

    def body(w_ref, m_ref, v_ref, g_ref, go_ref, d_ref, mo_ref, vo_ref):
        g = g_ref[0].astype(F32)
        for p in range(1, n_parts):
            g = g + g_ref[p].astype(F32)
        m_new = ADAM_B1 * m_ref[...] + (1.0 - ADAM_B1) * g
        v_new = ADAM_B2 * v_ref[...] + (1.0 - ADAM_B2) * (g * g)
        go_ref[...] = g
        mo_ref[...] = m_new
        vo_ref[...] = v_new
        d_ref[...] = -ADAM_LR * ((m_new * c1) / (jnp.sqrt(v_new * c2) + ADAM_EPS) + ADAM_WD * w_ref[...])

    blk = pl.BlockSpec((tr, cols), lambda i: (i, 0))
    return pl.pallas_call(
        body, name=name, grid=(rows // tr,),
        in_specs=[blk, blk, blk, pl.BlockSpec((n_parts, tr, cols), lambda i: (0, i, 0))],
        out_specs=[blk] * 4,
        out_shape=[jax.ShapeDtypeStruct((rows, cols), F32)] * 4,
        compiler_params=_params(("parallel",)),
    )(w, m, v, g_parts)


_O_Q, _O_K, _O_V, _O_F, _O_ZA, _O_GB, _O_GC, _O_U, _O_ZB, _O_GA, _O_GB2 = (
    0, 512, 1024, 1536, 1544, 2056, 2568, 3080, 3592, 4104, 5128)


def _to_internal(w_in_g):
    wf = jnp.transpose(w_in_g, (1, 0, 2)).reshape(D_MODEL, IN_WIDTH)
    cols = lambda a, n: wf[:, a:a + n]
    conv = [cols(base + LANES * c, LANES) for c in range(4) for base in (_O_GB, _O_GC, _O_U, _O_ZB)]
    main = jnp.concatenate([cols(_O_Q, 512), cols(_O_K, 512), cols(_O_V, 512), cols(_O_ZA, 512), *conv,
                            cols(_O_GA, 1024), cols(_O_GB2, 1024)], axis=1)
    f = jnp.pad(cols(_O_F, HEADS), ((0, 0), (0, N_FPAD - HEADS)))
    return main, f


def _from_internal(dw):
    cols = lambda a, n: dw[:, a:a + n]
    conv = lambda k: [cols(OFF_CONV + 4 * LANES * c + LANES * k, LANES) for c in range(4)]
    full = jnp.concatenate([cols(OFF_Q, 512), cols(OFF_K, 512), cols(OFF_V, 512), cols(N_MAIN, HEADS),
                            cols(OFF_ZA, 512), *conv(0), *conv(1), *conv(2), *conv(3),
                            cols(OFF_GA, 1024), cols(OFF_GB, 1024)], axis=1)
    return jnp.transpose(full.reshape(D_MODEL, N_DEV, IN_SHARD), (1, 0, 2))


def kernel(x, c, w_ada, b_ada, norm_g, w_in, b_f, q_norm_g, k_norm_g, conv_w, w_attn_out, w_conv_out, w_o, loss_target, m_w_ada, m_b_ada, m_norm_g, m_w_in, m_b_f, m_q_norm_g, m_k_norm_g, m_conv_w, m_w_attn_out, m_w_conv_out, m_w_o, v_w_ada, v_b_ada, v_norm_g, v_w_in, v_b_f, v_q_norm_g, v_k_norm_g, v_conv_w, v_w_attn_out, v_w_conv_out, v_w_o):
    me = 4 * lax.axis_index("x") + 2 * lax.axis_index("y") + lax.axis_index("c")
    s = x.shape[1]
    x2, t2 = x[0], loss_target[0]

    w_in_g, wa_g, wb_g, wo_g, cw_g = _exchange(
        [w_in[0].astype(BF16), w_attn_out[0].astype(BF16), w_conv_out[0].astype(BF16),
         w_o[0].astype(BF16), conv_w[0]], True, "gather_weights")
    c_all, ada_g = _ada_exchange(c, w_ada[0])
    ada_mine = lax.dynamic_index_in_dim(ada_g[:, :, 0, :], me, axis=1, keepdims=False)
    ada3 = (ada_mine.reshape(1, 3 * D_MODEL) + b_ada).reshape(3, D_MODEL)
    w_main, w_f = _to_internal(w_in_g)
    w_all_t = jnp.concatenate([w_main, w_f], axis=1).T
    wa = jnp.transpose(wa_g, (1, 0, 2)).reshape(ATTN_W, D_MODEL)
    wb = jnp.transpose(wb_g, (1, 0, 2)).reshape(CONV_W, D_MODEL)
    wo = wo_g.reshape(D_MODEL, D_MODEL)
    cw = jnp.transpose(cw_g, (1, 0, 2)).reshape(3, CONV_W)
    qg = jnp.tile(q_norm_g, (1, HEADS))
    kg = jnp.tile(k_norm_g, (1, HEADS))
    bf_pad = jnp.pad(b_f, ((0, 0), (0, LANES - HEADS)))

    proj, fl, ht = _proj_fwd(x2, ada3, norm_g, w_main, w_f)
    qs, kn, vb = _qkv_prep(proj, qg, kg)
    cum, cumt = _forget_cumsum(fl, bf_pad)
    cumt4 = jnp.pad(cumt.reshape(HEADS // 2, 2, s), ((0, 0), (0, 6), (0, 0)))
    attn, oa, lse = _attn_fwd(qs, kn, vb, cum, cumt4, proj)
    ob = _conv_fwd(proj, cw)
    (dy, dgab, doa, dob, dwo, dwa, dwb, dgate, loss_part) = _tail(oa, ob, proj, x2, t2, ada3, wa, wb, wo)

    do, delta, dza = _attn_bwd_prep(doa, attn, proj)
    dq, dk, dv, dcum4, drow = _attn_bwd(qs, kn, vb, do, lse, delta, cum, cumt4)
    dcum_col = jnp.pad(dcum4[:, 0:2, :].reshape(HEADS, s).T, ((0, 0), (0, LANES - HEADS)))
    drow_col = jnp.pad(drow[:, ::HEAD_DIM], ((0, 0), (0, LANES - HEADS)))
    df, dbf = _forget_bwd(dcum_col, drow_col, fl, bf_pad)
    dqkv, dqg, dkg = _qk_norm_bwd(dq, dk, dv, proj, qg, kg)
    dconv4, dcw = _conv_bwd(dob, proj, cw)
    dproj = jnp.concatenate([dqkv, dza, dconv4, dgab, df], axis=1)
    dw_all = _dw_in(ht, dproj)
    grad_x, dshift, dscale, dnormg = _dh_and_dx(dproj, w_all_t, x2, dy, ada3, norm_g)

    slabs = [_from_internal(dw_all).astype(BF16),
             jnp.transpose(dwa.reshape(ATTN_W, N_DEV, LANES), (1, 0, 2)).astype(BF16),
             jnp.transpose(dwb.reshape(CONV_W, N_DEV, LANES), (1, 0, 2)).astype(BF16),
             dwo.reshape(N_DEV, D_MODEL // N_DEV, D_MODEL).astype(BF16)]
    g_in_parts, g_wa_parts, g_wb_parts, g_wo_parts = _exchange(slabs, False, "exchange_grads")
    vec = jnp.concatenate([dshift, dscale, dgate, dnormg, dbf, dcw.reshape(1, 3 * CONV_W), dqg, dkg], axis=1)
    (vec_all,) = _exchange([vec], True, "gather_small")
    vec_all = vec_all.reshape(N_DEV, vec.shape[1])
    n_main = 4 * D_MODEL + LANES + 3 * CONV_W
    tot, g_qg, g_kg = _sum_small(
        vec_all[:, :n_main],
        vec_all[:, n_main:n_main + ATTN_W].reshape(N_DEV * HEADS, HEAD_DIM),
        vec_all[:, n_main + ATTN_W:].reshape(N_DEV * HEADS, HEAD_DIM))
    g_b_ada = tot[:, 0:3 * D_MODEL]
    g_norm_g = tot[:, 3 * D_MODEL:4 * D_MODEL]
    g_b_f = tot[:, 4 * D_MODEL:4 * D_MODEL + HEADS]
    g_cw_full = tot[:, 4 * D_MODEL + LANES:].reshape(3, CONV_W)
    g_cw = lax.dynamic_slice(g_cw_full, (0, me * (CONV_W // N_DEV)), (3, CONV_W // N_DEV))
    dada_mine = lax.dynamic_slice(vec_all[:, 0:3 * D_MODEL], (0, me * ADA_SHARD), (N_DEV, ADA_SHARD))
    g_w_ada = _grad_w_ada(jnp.transpose(c_all, (0, 2, 1)), dada_mine.reshape(N_DEV, 1, ADA_SHARD))

    upd = {}
    upd["w_ada"] = _adamw(w_ada[0], m_w_ada[0], v_w_ada[0], g_w_ada[None], "adamw_w_ada")
    upd["b_ada"] = _adamw(b_ada, m_b_ada, v_b_ada, g_b_ada[None], "adamw_b_ada")
    upd["norm_g"] = _adamw(norm_g, m_norm_g, v_norm_g, g_norm_g[None], "adamw_norm_g")
    upd["w_in"] = _adamw(w_in[0], m_w_in[0], v_w_in[0], g_in_parts, "adamw_w_in")
    upd["b_f"] = _adamw(b_f, m_b_f, v_b_f, g_b_f[None], "adamw_b_f")
    upd["q_norm_g"] = _adamw(q_norm_g, m_q_norm_g, v_q_norm_g, g_qg[None], "adamw_q_norm_g")
    upd["k_norm_g"] = _adamw(k_norm_g, m_k_norm_g, v_k_norm_g, g_kg[None], "adamw_k_norm_g")
    upd["conv_w"] = _adamw(conv_w[0], m_conv_w[0], v_conv_w[0], g_cw[None], "adamw_conv_w")
    upd["w_attn_out"] = _adamw(w_attn_out[0], m_w_attn_out[0], v_w_attn_out[0], g_wa_parts, "adamw_w_attn_out")
    upd["w_conv_out"] = _adamw(w_conv_out[0], m_w_conv_out[0], v_w_conv_out[0], g_wb_parts, "adamw_w_conv_out")
    upd["w_o"] = _adamw(w_o[0], m_w_o[0], v_w_o[0], g_wo_parts, "adamw_w_o")

    names = ["w_ada", "b_ada", "norm_g", "w_in", "b_f", "q_norm_g", "k_norm_g", "conv_w",
             "w_attn_out", "w_conv_out", "w_o"]
    lead = {"w_ada", "w_in", "conv_w", "w_attn_out", "w_conv_out", "w_o"}
    fix = lambda n, a: a[None] if n in lead else a
    loss = lax.psum(loss_part[0, 0], ("x", "y", "c"))
    outs = [loss, grad_x[None]]
    for k in range(4):
        outs += [fix(n, upd[n][k]) for n in names]
    return tuple(outs)
```

```python
import functools

import numpy as np
import jax
import jax.numpy as jnp
from jax import lax
from jax.experimental import pallas as pl
from jax.experimental.pallas import tpu as pltpu

F32 = jnp.float32
BF16 = jnp.bfloat16

D_MODEL = 1024
HEADS = 8
HEAD_DIM = 64
ATTN_W = 512
CONV_W = 512
N_DEV = 8
IN_WIDTH = 6152
IN_SHARD = IN_WIDTH // N_DEV
N_MAIN = 6144
N_FPAD = 128
N_ALL = N_MAIN + N_FPAD
ADA_SHARD = 3 * D_MODEL // N_DEV
EPS = 1e-6
NEG = -1e30

ADAM_LR = 0.001
ADAM_B1 = 0.9
ADAM_B2 = 0.999
ADAM_EPS = 1e-08
ADAM_WD = 0.01
ADAM_STEP = 10

LANES = 128
VMEM_LIMIT = 56 * 1024 * 1024

TM_PROJ = 512
TN_PROJ = 1024
TM_ELEM = 512
TQ = 512
TM_TAIL = 256
TC_CUM = 256
TK_DW = 512
TN_DW = 896
TM_DH = 512
TK_DH = 896

OFF_Q, OFF_K, OFF_V, OFF_ZA, OFF_CONV, OFF_GA, OFF_GB = 0, 512, 1024, 1536, 2048, 4096, 5120


def _params(sem=None):
    return pltpu.CompilerParams(dimension_semantics=sem, vmem_limit_bytes=VMEM_LIMIT)


def _dot(a, b):
    return jnp.dot(a, b, preferred_element_type=F32)


def _dot_nt(a, b):
    return lax.dot_general(a, b, (((1,), (1,)), ((), ())), preferred_element_type=F32)


def _dot_tn(a, b):
    return lax.dot_general(a, b, (((0,), (0,)), ((), ())), preferred_element_type=F32)


def _sigmoid(x):
    return 1.0 / (1.0 + jnp.exp(-x))


def _lane_lo(shape):
    return lax.broadcasted_iota(jnp.int32, shape, len(shape) - 1) < HEAD_DIM


def _seg_sum(z, lo):
    a = jnp.sum(jnp.where(lo, z, 0.0), axis=-1, keepdims=True)
    b = jnp.sum(jnp.where(lo, 0.0, z), axis=-1, keepdims=True)
    return jnp.where(lo, a, b)


def _lane_col(z, lane):
    idx = lax.broadcasted_iota(jnp.int32, z.shape, 1)
    return jnp.sum(jnp.where(idx == lane, z, 0.0), axis=-1, keepdims=True)


def _sub_row(z, row):
    idx = lax.broadcasted_iota(jnp.int32, z.shape, 0)
    return jnp.sum(jnp.where(idx == row, z, 0.0), axis=0, keepdims=True)


def _mesh_pos():
    x, y, c = lax.axis_index("x"), lax.axis_index("y"), lax.axis_index("c")
    return x, y, c, 4 * x + 2 * y + c


def _peer(k, x, y, c):
    px = 1 - x if (k >> 2) & 1 else x
    py = 1 - y if (k >> 1) & 1 else y
    pc = 1 - c if k & 1 else c
    return (px, py, pc), 4 * px + 2 * py + pc


def _exchange(arrs, gather, name):
    n = len(arrs)
    any_spec = pl.BlockSpec(memory_space=pl.ANY)

    def body(*refs):
        ins, outs = refs[:n], refs[n:2 * n]
        send_sems, recv_sems, local_sems = refs[2 * n:]
        x, y, c, me = _mesh_pos()
        copies = []
        for a in range(n):
            own = ins[a] if gather else ins[a].at[me]
            local = pltpu.make_async_copy(own, outs[a].at[me], local_sems.at[a])
            local.start()
            copies.append(local)
            for k in range(1, N_DEV):
                dev, p = _peer(k, x, y, c)
                cp = pltpu.make_async_remote_copy(
                    src_ref=ins[a] if gather else ins[a].at[p],
                    dst_ref=outs[a].at[me],
                    send_sem=send_sems.at[a * (N_DEV - 1) + k - 1],
                    recv_sem=recv_sems.at[a * (N_DEV - 1) + k - 1],
                    device_id=dev, device_id_type=pl.DeviceIdType.MESH)
                cp.start()
                copies.append(cp)
        for cp in copies:
            cp.wait()

    out_shape = [jax.ShapeDtypeStruct((N_DEV,) + a.shape if gather else a.shape, a.dtype) for a in arrs]
    return pl.pallas_call(
        body, name=name, out_shape=out_shape,
        in_specs=[any_spec] * n, out_specs=[any_spec] * n,
        scratch_shapes=[pltpu.SemaphoreType.DMA((n * (N_DEV - 1),)),
                        pltpu.SemaphoreType.DMA((n * (N_DEV - 1),)),
                        pltpu.SemaphoreType.DMA((n,))],
    )(*arrs)


def _ada_exchange(c_row, w_ada_sh):
    def body(c_ref, w_ref, call_ref, adag_ref, mine_ref, send_sems, recv_sems):
        x, y, c, me = _mesh_pos()

        def copy(phase, k, src, dst):
            dev, _ = _peer(k, x, y, c)
            return pltpu.make_async_remote_copy(
                src_ref=src, dst_ref=dst,
                send_sem=send_sems.at[phase * (N_DEV - 1) + k - 1],
                recv_sem=recv_sems.at[phase * (N_DEV - 1) + k - 1],
                device_id=dev, device_id_type=pl.DeviceIdType.MESH)

        call_ref[me] = c_ref[...]
        first = [copy(0, k, c_ref, call_ref.at[me]) for k in range(1, N_DEV)]
        for cp in first:
            cp.start()
        for cp in first:
            cp.wait()
        wb = w_ref[...].astype(BF16)
        for b in range(N_DEV):
            row = jnp.broadcast_to(call_ref[b], (8, D_MODEL)).astype(BF16)
            mine_ref[b] = _sub_row(_dot(row, wb), 0)
        adag_ref[me] = mine_ref[...]
        second = [copy(1, k, mine_ref, adag_ref.at[me]) for k in range(1, N_DEV)]
        for cp in second:
            cp.start()
        for cp in second:
            cp.wait()

    vm = pl.BlockSpec(memory_space=pltpu.VMEM)
    return pl.pallas_call(
        body, name="ada_exchange",
        out_shape=[jax.ShapeDtypeStruct((N_DEV, 1, D_MODEL), F32),
                   jax.ShapeDtypeStruct((N_DEV, N_DEV, 1, ADA_SHARD), F32)],
        in_specs=[vm, vm], out_specs=[vm, vm],
        scratch_shapes=[pltpu.VMEM((N_DEV, 1, ADA_SHARD), F32),
                        pltpu.SemaphoreType.DMA((2 * (N_DEV - 1),)),
                        pltpu.SemaphoreType.DMA((2 * (N_DEV - 1),))],
        compiler_params=pltpu.CompilerParams(vmem_limit_bytes=VMEM_LIMIT),
    )(c_row, w_ada_sh)


def _proj_fwd(x, ada3, norm_g, w_main, w_f):
    s = x.shape[0]
    tm, tn = min(TM_PROJ, s), TN_PROJ

    def body(x_ref, ada_ref, g_ref, w_ref, wf_ref, proj_ref, fl_ref, ht_ref, h_s):
        @pl.when(pl.program_id(1) == 0)
        def _():
            xv = x_ref[...]
            r = lax.rsqrt(jnp.mean(xv * xv, axis=-1, keepdims=True) + EPS)
            hv = ((xv * r) * g_ref[...]) * (1.0 + ada_ref[1:2, :]) + ada_ref[0:1, :]
            hb = hv.astype(BF16)
            h_s[...] = hb
            ht_ref[...] = hv.T.astype(BF16)
            fl_ref[...] = _dot(hb, wf_ref[...])
        proj_ref[...] = _dot(h_s[...], w_ref[...])

    return pl.pallas_call(
        body, name="proj_fwd", grid=(s // tm, N_MAIN // tn),
        in_specs=[pl.BlockSpec((tm, D_MODEL), lambda i, j: (i, 0)),
                  pl.BlockSpec((3, D_MODEL), lambda i, j: (0, 0)),
                  pl.BlockSpec((1, D_MODEL), lambda i, j: (0, 0)),
                  pl.BlockSpec((D_MODEL, tn), lambda i, j: (0, j)),
                  pl.BlockSpec((D_MODEL, N_FPAD), lambda i, j: (0, 0))],
        out_specs=[pl.BlockSpec((tm, tn), lambda i, j: (i, j)),
                   pl.BlockSpec((tm, N_FPAD), lambda i, j: (i, 0)),
                   pl.BlockSpec((D_MODEL, tm), lambda i, j: (0, i))],
        out_shape=[jax.ShapeDtypeStruct((s, N_MAIN), F32),
                   jax.ShapeDtypeStruct((s, N_FPAD), F32),
                   jax.ShapeDtypeStruct((D_MODEL, s), BF16)],
        scratch_shapes=[pltpu.VMEM((tm, D_MODEL), BF16)],
        compiler_params=_params(("parallel", "arbitrary")),
    )(x, ada3, norm_g, w_main, w_f)


L_ONE_Q, L_F_Q, L_LSE_Q, L_END = HEAD_DIM, HEAD_DIM + 3, HEAD_DIM + 6, HEAD_DIM + 9


def _split3(f):
    hi = f.astype(BF16).astype(F32)
    r = f - hi
    mid = r.astype(BF16).astype(F32)
    return hi, mid, r - mid


def _place3(lane, first, parts, otherwise):
    a, b, c = parts
    return jnp.where(lane == first, a, jnp.where(lane == first + 1, b, jnp.where(lane == first + 2, c, otherwise)))


def _qkv_prep(proj, cum, qg, kg):
    s = proj.shape[0]
    tm = min(TM_ELEM, s)
    scale = HEAD_DIM ** -0.5

    def body(p_ref, cum_ref, qg_ref, kg_ref, qa_ref, ka_ref, va_ref):
        lane = lax.broadcasted_iota(jnp.int32, (tm, LANES), 1)
        lo = lane < HEAD_DIM
        cum_v = cum_ref[...]
        for pr in range(ATTN_W // LANES):
            sl = slice(pr * LANES, (pr + 1) * LANES)
            q2 = p_ref[:, OFF_Q + pr * LANES:OFF_Q + (pr + 1) * LANES]
            k2 = p_ref[:, OFF_K + pr * LANES:OFF_K + (pr + 1) * LANES]
            v2 = p_ref[:, OFF_V + pr * LANES:OFF_V + (pr + 1) * LANES]
            rq = lax.rsqrt(_seg_sum(q2 * q2, lo) * (1.0 / HEAD_DIM) + EPS)
            rk = lax.rsqrt(_seg_sum(k2 * k2, lo) * (1.0 / HEAD_DIM) + EPS)
            qn = ((q2 * rq) * qg_ref[:, sl]) * scale
            kn = (k2 * rk) * kg_ref[:, sl]
            for hh in range(2):
                h = 2 * pr + hh
                f3 = _split3(_lane_col(cum_v, h))
                qh = qn if hh == 0 else pltpu.roll(qn, HEAD_DIM, 1)
                kh = kn if hh == 0 else pltpu.roll(kn, HEAD_DIM, 1)
                q_tail = jnp.where(lane < L_F_Q, 1.0, _place3(lane, L_F_Q, f3, 0.0))
                k_tail = _place3(lane, L_ONE_Q, tuple(-f for f in f3), jnp.where(lane < L_END, 1.0, 0.0))
                qa_ref[h] = jnp.where(lo, qh, q_tail).astype(BF16)
                ka_ref[h] = jnp.where(lo, kh, k_tail).astype(BF16)
                va_ref[h] = jnp.where(lo if hh == 0 else jnp.logical_not(lo), v2, 0.0).astype(BF16)

    heads = pl.BlockSpec((HEADS, tm, LANES), lambda i: (0, i, 0))
    vec = pl.BlockSpec((1, ATTN_W), lambda i: (0, 0))
    return pl.pallas_call(
        body, name="qkv_prep", grid=(s // tm,),
        in_specs=[pl.BlockSpec((tm, 3 * ATTN_W), lambda i: (i, 0)),
                  pl.BlockSpec((tm, LANES), lambda i: (i, 0)), vec, vec],
        out_specs=[heads, heads, heads],
        out_shape=[jax.ShapeDtypeStruct((HEADS, s, LANES), BF16)] * 3,
        compiler_params=_params(("parallel",)),
    )(proj, cum, qg, kg)


def _log_forget(fl, bf):
    z = fl + bf
    lf = jnp.minimum(z, 0.0) - jnp.log1p(jnp.exp(-jnp.abs(z)))
    lane = lax.broadcasted_iota(jnp.int32, z.shape, 1)
    return jnp.where(lane < HEADS, lf, 0.0)


def _forget_cumsum(fl, bf_pad):
    s = fl.shape[0]
    tc = min(TC_CUM, s)

    def body(fl_ref, bf_ref, cum_ref, carry):
        @pl.when(pl.program_id(0) == 0)
        def _():
            carry[...] = jnp.zeros_like(carry)
        lf = _log_forget(fl_ref[...], bf_ref[...])
        r = lax.broadcasted_iota(jnp.int32, (tc, tc), 0)
        cidx = lax.broadcasted_iota(jnp.int32, (tc, tc), 1)
        tri = (cidx <= r).astype(F32)
        cs = jnp.dot(tri, lf, preferred_element_type=F32, precision=lax.Precision.HIGHEST) + carry[...]
        cum_ref[...] = cs
        carry[...] = cum_ref[tc - 1:tc, :]

    return pl.pallas_call(
        body, name="forget_cumsum", grid=(s // tc,),
        in_specs=[pl.BlockSpec((tc, LANES), lambda i: (i, 0)),
                  pl.BlockSpec((1, LANES), lambda i: (0, 0))],
        out_specs=pl.BlockSpec((tc, LANES), lambda i: (i, 0)),
        out_shape=jax.ShapeDtypeStruct((s, LANES), F32),
        scratch_shapes=[pltpu.VMEM((1, LANES), F32)],
        compiler_params=_params(("arbitrary",)),
    )(fl, bf_pad)


def _causal_local(t):
    return lax.broadcasted_iota(jnp.int32, (t, t), 1) <= lax.broadcasted_iota(jnp.int32, (t, t), 0)


def _tri_steps(nt, q_major):
    if q_major:
        pairs = [(i, j) for i in range(nt) for j in range(i + 1)]
    else:
        pairs = [(i, j) for j in range(nt) for i in range(j, nt)]
    return (jnp.asarray(np.array([p[0] for p in pairs], np.int32)),
            jnp.asarray(np.array([p[1] for p in pairs], np.int32)))


def _attn_fwd(qa, ka, va, proj):
    s = qa.shape[1]
    t = min(TQ, s)
    it, jt = _tri_steps(s // t, True)
    za_blk = OFF_ZA // LANES

    def body(it_ref, jt_ref, q_ref, k_ref, v_ref, za_ref, attn_ref, oa_ref, qb_ref, m_s, l_s, acc_s):
        step = pl.program_id(1)
        i, j = it_ref[step], jt_ref[step]

        @pl.when(j == 0)
        def _():
            m_s[...] = jnp.full_like(m_s, NEG)
            l_s[...] = jnp.zeros_like(l_s)
            acc_s[...] = jnp.zeros_like(acc_s)

        def update(masked):
            lo_q = _lane_lo((t, LANES))
            alphas, pvs = [], []
            for hh in range(2):
                sc = _dot_nt(q_ref[hh], k_ref[hh])
                if masked:
                    sc = jnp.where(_causal_local(t), sc, NEG)
                m_prev = m_s[hh]
                m_next = jnp.maximum(m_prev, jnp.max(sc, axis=-1, keepdims=True))
                alpha = jnp.exp(m_prev - m_next)
                p = jnp.exp(sc - m_next)
                l_s[hh] = alpha * l_s[hh] + jnp.sum(p, axis=-1, keepdims=True)
                m_s[hh] = m_next
                alphas.append(alpha)
                pvs.append(_dot(p.astype(BF16), v_ref[hh]))
            acc_s[...] = acc_s[...] * jnp.where(lo_q, alphas[0], alphas[1]) + pvs[0] + pvs[1]

        @pl.when(j < i)
        def _():
            update(False)

        @pl.when(j == i)
        def _():
            update(True)
            lane = lax.broadcasted_iota(jnp.int32, (t, LANES), 1)
            lo_q = lane < HEAD_DIM
            out = acc_s[...] / jnp.where(lo_q, l_s[0], l_s[1])
            attn_ref[...] = out
            z = za_ref[...]
            oa_ref[...] = (out * (z * _sigmoid(z))).astype(BF16)
            for hh in range(2):
                lse3 = _split3(m_s[hh] + jnp.log(l_s[hh]))
                qb_ref[hh] = _place3(lane, L_LSE_Q, tuple(-x for x in lse3), q_ref[hh].astype(F32)).astype(BF16)

    pair_q = pl.BlockSpec((2, t, LANES), lambda p, n, it_, jt_: (p, it_[n], 0))
    pair_k = pl.BlockSpec((2, t, LANES), lambda p, n, it_, jt_: (p, jt_[n], 0))
    out_q = pl.BlockSpec((t, LANES), lambda p, n, it_, jt_: (it_[n], p))
    return pl.pallas_call(
        body, name="attn_fwd",
        grid_spec=pltpu.PrefetchScalarGridSpec(
            num_scalar_prefetch=2, grid=(ATTN_W // LANES, it.shape[0]),
            in_specs=[pair_q, pair_k, pair_k,
                      pl.BlockSpec((t, LANES), lambda p, n, it_, jt_: (it_[n], za_blk + p))],
            out_specs=[out_q, out_q, pair_q],
            scratch_shapes=[pltpu.VMEM((2, t, 1), F32), pltpu.VMEM((2, t, 1), F32),
                            pltpu.VMEM((t, LANES), F32)]),
        out_shape=[jax.ShapeDtypeStruct((s, ATTN_W), F32),
                   jax.ShapeDtypeStruct((s, ATTN_W), BF16),
                   jax.ShapeDtypeStruct((HEADS, s, LANES), BF16)],
        compiler_params=_params(("parallel", "arbitrary")),
    )(it, jt, qa, ka, va, proj)


def _conv_parts(blk, halo, first, w_ref, tm):
    gb, gc = blk[:, 0:LANES], blk[:, LANES:2 * LANES]
    u, zb = blk[:, 2 * LANES:3 * LANES], blk[:, 3 * LANES:4 * LANES]
    cu = gc * u
    cu_h = jnp.where(first, 0.0, halo[:, LANES:2 * LANES] * halo[:, 2 * LANES:3 * LANES])
    prev1, prev2 = _sub_row(cu_h, 7), _sub_row(cu_h, 6)
    row = lax.broadcasted_iota(jnp.int32, (tm, LANES), 0)
    r1 = jnp.where(row == 0, prev1, pltpu.roll(cu, 1, 0))
    r2 = jnp.where(row == 0, prev2, jnp.where(row == 1, prev1, pltpu.roll(cu, 2, 0)))
    conv = w_ref[2:3, :] * cu + w_ref[1:2, :] * r1 + w_ref[0:1, :] * r2
    return gb, gc, u, zb, cu, r1, r2, conv


def _conv_fwd(proj, conv_w):
    s = proj.shape[0]
    tm = min(TM_ELEM, s)
    cb = OFF_CONV // (4 * LANES)

    def body(p_ref, halo_ref, w_ref, ob_ref):
        first = pl.program_id(1) == 0
        gb, _, _, zb, _, _, _, conv = _conv_parts(p_ref[...], halo_ref[...], first, w_ref, tm)
        ob_ref[...] = (gb * conv * (zb * _sigmoid(zb))).astype(BF16)

    return pl.pallas_call(
        body, name="conv_fwd", grid=(CONV_W // LANES, s // tm),
        in_specs=[pl.BlockSpec((tm, 4 * LANES), lambda c, i: (i, cb + c)),
                  pl.BlockSpec((8, 4 * LANES), lambda c, i: (jnp.maximum(i * (tm // 8) - 1, 0), cb + c)),
                  pl.BlockSpec((3, LANES), lambda c, i: (0, c))],
        out_specs=pl.BlockSpec((tm, LANES), lambda c, i: (i, c)),
        out_shape=jax.ShapeDtypeStruct((s, CONV_W), BF16),
        compiler_params=_params(("parallel", "parallel")),
    )(proj, proj, conv_w)


def _tail(oa, ob, proj, x, target, ada3, wa, wb, wo):
    s = x.shape[0]
    tm = min(TM_TAIL, s)
    gab_blk = OFF_GA // (2 * D_MODEL)

    def body(oa_ref, ob_ref, gab_ref, x_ref, t_ref, ada_ref, wa_ref, wb_ref, wo_ref,
             dy_ref, dgab_ref, doa_ref, dob_ref, dwo_ref, dwa_ref, dwb_ref, dgate_ref, loss_ref):
        @pl.when(pl.program_id(0) == 0)
        def _():
            dwo_ref[...] = jnp.zeros_like(dwo_ref)
            dwa_ref[...] = jnp.zeros_like(dwa_ref)
            dwb_ref[...] = jnp.zeros_like(dwb_ref)
            dgate_ref[...] = jnp.zeros_like(dgate_ref)
            loss_ref[...] = jnp.zeros_like(loss_ref)

        oa_v, ob_v = oa_ref[...], ob_ref[...]
        wa_v, wb_v, wo_v = wa_ref[...], wb_ref[...], wo_ref[...]
        a2 = _dot(oa_v, wa_v)
        b2 = _dot(ob_v, wb_v)
        sa = _sigmoid(gab_ref[:, 0:D_MODEL])
        sb = _sigmoid(gab_ref[:, D_MODEL:2 * D_MODEL])
        mb = (sa * a2 + sb * b2).astype(BF16)
        mo = _dot(mb, wo_v)
        gate = ada_ref[2:3, :]
        err = (x_ref[...] + gate * mo) - t_ref[...]
        dy = err * (1.0 / D_MODEL)
        dy_ref[...] = dy
        loss_ref[...] += 0.5 * jnp.sum(err * err) * (1.0 / D_MODEL)
        dgate_ref[...] += jnp.sum(dy * mo, axis=0, keepdims=True)
        dmo = (dy * gate).astype(BF16)
        dmerged = _dot_nt(dmo, wo_v)
        dwo_ref[...] += _dot_tn(mb, dmo)
        da2 = (dmerged * sa).astype(BF16)
        db2 = (dmerged * sb).astype(BF16)
        dgab_ref[:, 0:D_MODEL] = (dmerged * a2 * (sa * (1.0 - sa))).astype(BF16)
        dgab_ref[:, D_MODEL:2 * D_MODEL] = (dmerged * b2 * (sb * (1.0 - sb))).astype(BF16)
        doa_ref[...] = _dot_nt(da2, wa_v)
        dob_ref[...] = _dot_nt(db2, wb_v)
        dwa_ref[...] += _dot_tn(oa_v, da2)
        dwb_ref[...] += _dot_tn(ob_v, db2)

    half = pl.BlockSpec((tm, ATTN_W), lambda i: (i, 0))
    full = pl.BlockSpec((tm, D_MODEL), lambda i: (i, 0))

    def const(shape):
        return pl.BlockSpec(shape, lambda i: (0, 0))

    return pl.pallas_call(
        body, name="tail", grid=(s // tm,),
        in_specs=[half, half, pl.BlockSpec((tm, 2 * D_MODEL), lambda i: (i, gab_blk)), full, full,
                  const((3, D_MODEL)), const((ATTN_W, D_MODEL)), const((CONV_W, D_MODEL)),
                  const((D_MODEL, D_MODEL))],
        out_specs=[full, pl.BlockSpec((tm, 2 * D_MODEL), lambda i: (i, 0)), half, half,
                   const((D_MODEL, D_MODEL)), const((ATTN_W, D_MODEL)), const((CONV_W, D_MODEL)),
                   const((1, D_MODEL)), const((1, LANES))],
        out_shape=[jax.ShapeDtypeStruct((s, D_MODEL), F32),
                   jax.ShapeDtypeStruct((s, 2 * D_MODEL), BF16),
                   jax.ShapeDtypeStruct((s, ATTN_W), F32),
                   jax.ShapeDtypeStruct((s, CONV_W), F32),
                   jax.ShapeDtypeStruct((D_MODEL, D_MODEL), F32),
                   jax.ShapeDtypeStruct((ATTN_W, D_MODEL), F32),
                   jax.ShapeDtypeStruct((CONV_W, D_MODEL), F32),
                   jax.ShapeDtypeStruct((1, D_MODEL), F32),
                   jax.ShapeDtypeStruct((1, LANES), F32)],
        compiler_params=_params(("arbitrary",)),
    )(oa, ob, proj, x, target, ada3, wa, wb, wo)


def _attn_bwd_prep(doa, attn, proj):
    s = doa.shape[0]
    tm = min(TM_ELEM, s)
    za_blk = OFF_ZA // ATTN_W

    def body(doa_ref, attn_ref, za_ref, do_ref, delta_ref, dza_ref):
        lo = _lane_lo((tm, LANES))
        for pr in range(ATTN_W // LANES):
            sl = slice(pr * LANES, (pr + 1) * LANES)
            g, a, z = doa_ref[:, sl], attn_ref[:, sl], za_ref[:, sl]
            sg = _sigmoid(z)
            datb = (g * (z * sg)).astype(BF16)
            zero = jnp.zeros_like(datb)
            do_ref[2 * pr] = jnp.where(lo, datb, zero)
            do_ref[2 * pr + 1] = jnp.where(lo, zero, datb)
            delta_ref[:, sl] = _seg_sum(datb.astype(F32) * a, lo)
            dza_ref[:, sl] = (g * a * (sg * (1.0 + z * (1.0 - sg)))).astype(BF16)

    row = pl.BlockSpec((tm, ATTN_W), lambda i: (i, 0))
    return pl.pallas_call(
        body, name="attn_bwd_prep", grid=(s // tm,),
        in_specs=[row, row, pl.BlockSpec((tm, ATTN_W), lambda i: (i, za_blk))],
        out_specs=[pl.BlockSpec((HEADS, tm, LANES), lambda i: (0, i, 0)), row, row],
        out_shape=[jax.ShapeDtypeStruct((HEADS, s, LANES), BF16),
                   jax.ShapeDtypeStruct((s, ATTN_W), F32),
                   jax.ShapeDtypeStruct((s, ATTN_W), BF16)],
        compiler_params=_params(("parallel",)),
    )(doa, attn, proj)


def _attn_bwd(qb, ka, va, do, delta):
    s = qb.shape[1]
    t = min(TQ, s)
    it, jt = _tri_steps(s // t, False)

    def body(it_ref, jt_ref, q_ref, k_ref, v_ref, do_ref, delta_ref, dq_ref, dk_ref, dv_ref, dc_ref, dr_ref):
        step = pl.program_id(1)
        i, j = it_ref[step], jt_ref[step]

        @pl.when(step == 0)
        def _():
            dq_ref[...] = jnp.zeros_like(dq_ref)
            dr_ref[...] = jnp.zeros_like(dr_ref)

        @pl.when(i == j)
        def _():
            dk_ref[...] = jnp.zeros_like(dk_ref)
            dv_ref[...] = jnp.zeros_like(dv_ref)
            dc_ref[...] = jnp.zeros_like(dc_ref)

        def update(masked):
            lo_q = _lane_lo((t, LANES))
            delta2 = delta_ref[...]
            rows = pl.ds(pl.multiple_of(i * t, t), t)
            row_sums = []
            for hh in range(2):
                qh, kh, vh, doh = q_ref[hh], k_ref[hh], v_ref[hh], do_ref[hh]
                sc = _dot_nt(qh, kh)
                if masked:
                    sc = jnp.where(_causal_local(t), sc, NEG)
                p = jnp.exp(sc)
                dp = _dot_nt(doh, vh)
                ds = p * (dp - _lane_col(delta2, hh * HEAD_DIM))
                pb, dsb = p.astype(BF16), ds.astype(BF16)
                dv_ref[...] += _dot_tn(pb, doh)
                dk_ref[hh] += _dot_tn(dsb, qh)
                dq_ref[hh, rows, :] += _dot(dsb, kh)
                dc_ref[0, hh:hh + 1, :] += -jnp.sum(ds, axis=0, keepdims=True)
                row_sums.append(jnp.sum(ds, axis=-1, keepdims=True))
            dr_ref[rows, :] += jnp.where(lo_q, row_sums[0], row_sums[1])

        @pl.when(i > j)
        def _():
            update(False)

        @pl.when(i == j)
        def _():
            update(True)

    pair_q = pl.BlockSpec((2, t, LANES), lambda p, n, it_, jt_: (p, it_[n], 0))
    pair_k = pl.BlockSpec((2, t, LANES), lambda p, n, it_, jt_: (p, jt_[n], 0))
    return pl.pallas_call(
        body, name="attn_bwd",
        grid_spec=pltpu.PrefetchScalarGridSpec(
            num_scalar_prefetch=2, grid=(ATTN_W // LANES, it.shape[0]),
            in_specs=[pair_q, pair_k, pair_k, pair_q,
                      pl.BlockSpec((t, LANES), lambda p, n, it_, jt_: (it_[n], p))],
            out_specs=[pl.BlockSpec((2, s, LANES), lambda p, n, it_, jt_: (p, 0, 0)),
                       pair_k,
                       pl.BlockSpec((t, LANES), lambda p, n, it_, jt_: (jt_[n], p)),
                       pl.BlockSpec((1, 8, t), lambda p, n, it_, jt_: (p, 0, jt_[n])),
                       pl.BlockSpec((s, LANES), lambda p, n, it_, jt_: (0, p))]),
        out_shape=[jax.ShapeDtypeStruct((HEADS, s, LANES), F32),
                   jax.ShapeDtypeStruct((HEADS, s, LANES), F32),
                   jax.ShapeDtypeStruct((s, ATTN_W), F32),
                   jax.ShapeDtypeStruct((ATTN_W // LANES, 8, s), F32),
                   jax.ShapeDtypeStruct((s, ATTN_W), F32)],
        compiler_params=_params(("parallel", "arbitrary")),
    )(it, jt, qb, ka, va, do, delta)


def _forget_bwd(dcum_col, drow_col, fl, bf_pad):
    s = fl.shape[0]
    tc = min(TC_CUM, s)
    n = s // tc

    def body(dc_ref, dr_ref, fl_ref, bf_ref, df_ref, dbf_ref, carry):
        @pl.when(pl.program_id(0) == 0)
        def _():
            carry[...] = jnp.zeros_like(carry)
            dbf_ref[...] = jnp.zeros_like(dbf_ref)
        r = lax.broadcasted_iota(jnp.int32, (tc, tc), 0)
        cidx = lax.broadcasted_iota(jnp.int32, (tc, tc), 1)
        tri = (cidx >= r).astype(F32)
        dc = dc_ref[...] + dr_ref[...]
        dlf = jnp.dot(tri, dc, preferred_element_type=F32, precision=lax.Precision.HIGHEST) + carry[...]
        carry[...] += jnp.sum(dc, axis=0, keepdims=True)
        lane = lax.broadcasted_iota(jnp.int32, (tc, LANES), 1)
        dfl = jnp.where(lane < HEADS, dlf * _sigmoid(-(fl_ref[...] + bf_ref[...])), 0.0)
        df_ref[...] = dfl.astype(BF16)
        dbf_ref[...] += jnp.sum(dfl, axis=0, keepdims=True)

    rev = pl.BlockSpec((tc, LANES), lambda i: (n - 1 - i, 0))
    vec = pl.BlockSpec((1, LANES), lambda i: (0, 0))
    return pl.pallas_call(
        body, name="forget_bwd", grid=(n,),
        in_specs=[rev, rev, rev, vec], out_specs=[rev, vec],
        out_shape=[jax.ShapeDtypeStruct((s, LANES), BF16), jax.ShapeDtypeStruct((1, LANES), F32)],
        scratch_shapes=[pltpu.VMEM((1, LANES), F32)],
        compiler_params=_params(("arbitrary",)),
    )(dcum_col, drow_col, fl, bf_pad)


def _qk_norm_bwd(dq, dk, dv, proj, qg, kg):
    s = dv.shape[0]
    tm = min(TM_ELEM, s)
    scale = HEAD_DIM ** -0.5

    def body(dq_ref, dk_ref, dv_ref, p_ref, qg_ref, kg_ref, out_ref, dqg_ref, dkg_ref):
        @pl.when(pl.program_id(0) == 0)
        def _():
            dqg_ref[...] = jnp.zeros_like(dqg_ref)
            dkg_ref[...] = jnp.zeros_like(dkg_ref)
        lo = _lane_lo((tm, LANES))

        def pair(ref, pr):
            return jnp.where(lo, ref[2 * pr], pltpu.roll(ref[2 * pr + 1], HEAD_DIM, 1))

        def one(raw, dy, g, dg_ref, sl, off):
            r = lax.rsqrt(_seg_sum(raw * raw, lo) * (1.0 / HEAD_DIM) + EPS)
            xhat = raw * r
            dg_ref[:, sl] += jnp.sum(dy * xhat, axis=0, keepdims=True)
            dxh = dy * g
            dx = r * (dxh - xhat * (_seg_sum(dxh * xhat, lo) * (1.0 / HEAD_DIM)))
            out_ref[:, off + sl.start:off + sl.stop] = dx.astype(BF16)

        for pr in range(ATTN_W // LANES):
            sl = slice(pr * LANES, (pr + 1) * LANES)
            one(p_ref[:, OFF_Q + sl.start:OFF_Q + sl.stop], pair(dq_ref, pr) * scale, qg_ref[:, sl], dqg_ref, sl, OFF_Q)
            one(p_ref[:, OFF_K + sl.start:OFF_K + sl.stop], pair(dk_ref, pr), kg_ref[:, sl], dkg_ref, sl, OFF_K)
        out_ref[:, OFF_V:OFF_V + ATTN_W] = dv_ref[...].astype(BF16)

    row = pl.BlockSpec((tm, ATTN_W), lambda i: (i, 0))
    heads = pl.BlockSpec((HEADS, tm, LANES), lambda i: (0, i, 0))
    vec = pl.BlockSpec((1, ATTN_W), lambda i: (0, 0))
    return pl.pallas_call(
        body, name="qk_norm_bwd", grid=(s // tm,),
        in_specs=[heads, heads, row, pl.BlockSpec((tm, 2 * ATTN_W), lambda i: (i, 0)), vec, vec],
        out_specs=[pl.BlockSpec((tm, 3 * ATTN_W), lambda i: (i, 0)), vec, vec],
        out_shape=[jax.ShapeDtypeStruct((s, 3 * ATTN_W), BF16),
                   jax.ShapeDtypeStruct((1, ATTN_W), F32), jax.ShapeDtypeStruct((1, ATTN_W), F32)],
        compiler_params=_params(("arbitrary",)),
    )(dq, dk, dv, proj, qg, kg)


def _conv_bwd(dob, proj, conv_w):
    s = dob.shape[0]
    tm = min(TM_ELEM, s)
    cb = OFF_CONV // (4 * LANES)
    nblk8 = s // 8

    def body(dob_ref, p_ref, prev_ref, next_ref, dnext_ref, w_ref, out_ref, dw_ref):
        i = pl.program_id(1)

        @pl.when(i == 0)
        def _():
            dw_ref[...] = jnp.zeros_like(dw_ref)
        gb, gc, u, zb, cu, r1, r2, conv = _conv_parts(p_ref[...], prev_ref[...], i == 0, w_ref, tm)
        g = dob_ref[...]
        sg = _sigmoid(zb)
        sz = zb * sg
        dconv = g * gb * sz
        nxt = next_ref[...]
        zn = nxt[:, 3 * LANES:4 * LANES]
        dcn = jnp.where(i == pl.num_programs(1) - 1, 0.0,
                        dnext_ref[...] * nxt[:, 0:LANES] * (zn * _sigmoid(zn)))
        nxt1, nxt2 = _sub_row(dcn, 0), _sub_row(dcn, 1)
        row = lax.broadcasted_iota(jnp.int32, (tm, LANES), 0)
        f1 = jnp.where(row == tm - 1, nxt1, pltpu.roll(dconv, tm - 1, 0))
        f2 = jnp.where(row == tm - 2, nxt1, jnp.where(row == tm - 1, nxt2, pltpu.roll(dconv, tm - 2, 0)))
        dcu = w_ref[2:3, :] * dconv + w_ref[1:2, :] * f1 + w_ref[0:1, :] * f2
        out_ref[:, 0:LANES] = (g * conv * sz).astype(BF16)
        out_ref[:, LANES:2 * LANES] = (dcu * u).astype(BF16)
        out_ref[:, 2 * LANES:3 * LANES] = (dcu * gc).astype(BF16)
        out_ref[:, 3 * LANES:4 * LANES] = (g * gb * conv * (sg * (1.0 + zb * (1.0 - sg)))).astype(BF16)
        w_row = lax.broadcasted_iota(jnp.int32, (3, LANES), 0)
        dw0 = jnp.sum(dconv * r2, axis=0, keepdims=True)
        dw1 = jnp.sum(dconv * r1, axis=0, keepdims=True)
        dw2 = jnp.sum(dconv * cu, axis=0, keepdims=True)
        dw_ref[...] += jnp.where(w_row == 0, dw0, jnp.where(w_row == 1, dw1, dw2))

    nxt_idx = lambda i: jnp.minimum((i + 1) * (tm // 8), nblk8 - 1)
    return pl.pallas_call(
        body, name="conv_bwd", grid=(CONV_W // LANES, s // tm),
        in_specs=[pl.BlockSpec((tm, LANES), lambda c, i: (i, c)),
                  pl.BlockSpec((tm, 4 * LANES), lambda c, i: (i, cb + c)),
                  pl.BlockSpec((8, 4 * LANES), lambda c, i: (jnp.maximum(i * (tm // 8) - 1, 0), cb + c)),
                  pl.BlockSpec((8, 4 * LANES), lambda c, i: (nxt_idx(i), cb + c)),
                  pl.BlockSpec((8, LANES), lambda c, i: (nxt_idx(i), c)),
                  pl.BlockSpec((3, LANES), lambda c, i: (0, c))],
        out_specs=[pl.BlockSpec((tm, 4 * LANES), lambda c, i: (i, c)),
                   pl.BlockSpec((3, LANES), lambda c, i: (0, c))],
        out_shape=[jax.ShapeDtypeStruct((s, 4 * CONV_W), BF16), jax.ShapeDtypeStruct((3, CONV_W), F32)],
        compiler_params=_params(("parallel", "arbitrary")),
    )(dob, proj, proj, proj, dob, conv_w)


def _dw_in(ht, dproj):
    s = ht.shape[1]
    tk, tn = min(TK_DW, s), TN_DW

    def body(ht_ref, dp_ref, out_ref):
        @pl.when(pl.program_id(1) == 0)
        def _():
            out_ref[...] = jnp.zeros_like(out_ref)
        out_ref[...] += _dot(ht_ref[...], dp_ref[...])

    return pl.pallas_call(
        body, name="dw_in", grid=(N_ALL // tn, s // tk),
        in_specs=[pl.BlockSpec((D_MODEL, tk), lambda n, k: (0, k)),
                  pl.BlockSpec((tk, tn), lambda n, k: (k, n))],
        out_specs=pl.BlockSpec((D_MODEL, tn), lambda n, k: (0, n)),
        out_shape=jax.ShapeDtypeStruct((D_MODEL, N_ALL), F32),
        compiler_params=_params(("parallel", "arbitrary")),
    )(ht, dproj)


def _dh_and_dx(dproj, w_all_t, x, dy, ada3, norm_g):
    s = x.shape[0]
    tm, tk = min(TM_DH, s), TK_DH
    nk = N_ALL // tk

    def body(dp_ref, wt_ref, x_ref, dy_ref, ada_ref, g_ref, gx_ref, dsh_ref, dsc_ref, dg_ref, acc):
        i, k = pl.program_id(0), pl.program_id(1)

        @pl.when(jnp.logical_and(i == 0, k == 0))
        def _():
            dsh_ref[...] = jnp.zeros_like(dsh_ref)
            dsc_ref[...] = jnp.zeros_like(dsc_ref)
            dg_ref[...] = jnp.zeros_like(dg_ref)

        @pl.when(k == 0)
        def _():
            acc[...] = jnp.zeros_like(acc)
        acc[...] += _dot(dp_ref[...], wt_ref[...])

        @pl.when(k == nk - 1)
        def _():
            dh = acc[...]
            xv = x_ref[...]
            r = lax.rsqrt(jnp.mean(xv * xv, axis=-1, keepdims=True) + EPS)
            xhat = xv * r
            g = g_ref[...]
            one_sc = 1.0 + ada_ref[1:2, :]
            dsh_ref[...] += jnp.sum(dh, axis=0, keepdims=True)
            dsc_ref[...] += jnp.sum(dh * (xhat * g), axis=0, keepdims=True)
            dg_ref[...] += jnp.sum(dh * xhat, axis=0, keepdims=True) * one_sc
            dxh = dh * (g * one_sc)
            dx = r * (dxh - xhat * jnp.mean(dxh * xhat, axis=-1, keepdims=True))
            gx_ref[...] = dy_ref[...] + dx

    full = pl.BlockSpec((tm, D_MODEL), lambda i, k: (i, 0))
    vec = pl.BlockSpec((1, D_MODEL), lambda i, k: (0, 0))
    return pl.pallas_call(
        body, name="dh_dx", grid=(s // tm, nk),
        in_specs=[pl.BlockSpec((tm, tk), lambda i, k: (i, k)),
                  pl.BlockSpec((tk, D_MODEL), lambda i, k: (k, 0)),
                  full, full, pl.BlockSpec((3, D_MODEL), lambda i, k: (0, 0)), vec],
        out_specs=[full, vec, vec, vec],
        out_shape=[jax.ShapeDtypeStruct((s, D_MODEL), F32)] + [jax.ShapeDtypeStruct((1, D_MODEL), F32)] * 3,
        scratch_shapes=[pltpu.VMEM((tm, D_MODEL), F32)],
        compiler_params=_params(("arbitrary", "arbitrary")),
    )(dproj, w_all_t, x, dy, ada3, norm_g)


def _sum_small(vec_all, qg_parts, kg_parts):
    def body(v_ref, q_ref, k_ref, tot_ref, gq_ref, gk_ref):
        tot = v_ref[0:1, :]
        for p in range(1, N_DEV):
            tot = tot + v_ref[p:p + 1, :]
        tot_ref[...] = tot
        gq_ref[...] = jnp.sum(q_ref[...], axis=0, keepdims=True)
        gk_ref[...] = jnp.sum(k_ref[...], axis=0, keepdims=True)

    n = vec_all.shape[-1]
    return pl.pallas_call(
        body, name="sum_small",
        out_shape=[jax.ShapeDtypeStruct((1, n), F32),
                   jax.ShapeDtypeStruct((1, HEAD_DIM), F32), jax.ShapeDtypeStruct((1, HEAD_DIM), F32)],
        compiler_params=_params(),
    )(vec_all, qg_parts, kg_parts)


def _grad_w_ada(c_cols, dada_rows):
    def body(c_ref, d_ref, out_ref):
        acc = c_ref[0] * d_ref[0]
        for b in range(1, N_DEV):
            acc = acc + c_ref[b] * d_ref[b]
        out_ref[...] = acc

    return pl.pallas_call(
        body, name="grad_w_ada",
        out_shape=jax.ShapeDtypeStruct((D_MODEL, ADA_SHARD), F32),
        compiler_params=_params(),
    )(c_cols, dada_rows)


def _adamw(w, m, v, g_parts, name):
    rows, cols = w.shape
    n_parts = g_parts.shape[0]
    tr = 256 if rows % 256 == 0 else rows
    c1 = 1.0 / (1.0 - ADAM_B1 ** ADAM_STEP)
    c2 = 1.0 / (1.0 - ADAM_B2 ** ADAM_STEP)

    def body(w_ref, m_ref, v_ref, g_ref, go_ref, d_ref, mo_ref, vo_ref):
        g = g_ref[0].astype(F32)
        for p in range(1, n_parts):
            g = g + g_ref[p].astype(F32)
        m_new = ADAM_B1 * m_ref[...] + (1.0 - ADAM_B1) * g
        v_new = ADAM_B2 * v_ref[...] + (1.0 - ADAM_B2) * (g * g)
        go_ref[...] = g
        mo_ref[...] = m_new
        vo_ref[...] = v_new
        d_ref[...] = -ADAM_LR * ((m_new * c1) / (jnp.sqrt(v_new * c2) + ADAM_EPS) + ADAM_WD * w_ref[...])

    blk = pl.BlockSpec((tr, cols), lambda i: (i, 0))
    return pl.pallas_call(
        body, name=name, grid=(rows // tr,),
        in_specs=[blk, blk, blk, pl.BlockSpec((n_parts, tr, cols), lambda i: (0, i, 0))],
        out_specs=[blk] * 4,
        out_shape=[jax.ShapeDtypeStruct((rows, cols), F32)] * 4,
        compiler_params=_params(("parallel",)),
    )(w, m, v, g_parts)


_O_Q, _O_K, _O_V, _O_F, _O_ZA, _O_GB, _O_GC, _O_U, _O_ZB, _O_GA, _O_GB2 = (
    0, 512, 1024, 1536, 1544, 2056, 2568, 3080, 3592, 4104, 5128)


def _to_internal(w_in_g):
    wf = jnp.transpose(w_in_g, (1, 0, 2)).reshape(D_MODEL, IN_WIDTH)
    cols = lambda a, n: wf[:, a:a + n]
    conv = [cols(base + LANES * c, LANES) for c in range(4) for base in (_O_GB, _O_GC, _O_U, _O_ZB)]
    main = jnp.concatenate([cols(_O_Q, 512), cols(_O_K, 512), cols(_O_V, 512), cols(_O_ZA, 512), *conv,
                            cols(_O_GA, 1024), cols(_O_GB2, 1024)], axis=1)
    f = jnp.pad(cols(_O_F, HEADS), ((0, 0), (0, N_FPAD - HEADS)))
    return main, f


def _from_internal(dw):
    cols = lambda a, n: dw[:, a:a + n]
    conv = lambda k: [cols(OFF_CONV + 4 * LANES * c + LANES * k, LANES) for c in range(4)]
    full = jnp.concatenate([cols(OFF_Q, 512), cols(OFF_K, 512), cols(OFF_V, 512), cols(N_MAIN, HEADS),
                            cols(OFF_ZA, 512), *conv(0), *conv(1), *conv(2), *conv(3),
                            cols(OFF_GA, 1024), cols(OFF_GB, 1024)], axis=1)
    return jnp.transpose(full.reshape(D_MODEL, N_DEV, IN_SHARD), (1, 0, 2))


def kernel(x, c, w_ada, b_ada, norm_g, w_in, b_f, q_norm_g, k_norm_g, conv_w, w_attn_out, w_conv_out, w_o, loss_target, m_w_ada, m_b_ada, m_norm_g, m_w_in, m_b_f, m_q_norm_g, m_k_norm_g, m_conv_w, m_w_attn_out, m_w_conv_out, m_w_o, v_w_ada, v_b_ada, v_norm_g, v_w_in, v_b_f, v_q_norm_g, v_k_norm_g, v_conv_w, v_w_attn_out, v_w_conv_out, v_w_o):
    me = 4 * lax.axis_index("x") + 2 * lax.axis_index("y") + lax.axis_index("c")
    s = x.shape[1]
    x2, t2 = x[0], loss_target[0]

    w_in_g, wa_g, wb_g, wo_g, cw_g = _exchange(
        [w_in[0].astype(BF16), w_attn_out[0].astype(BF16), w_conv_out[0].astype(BF16),
         w_o[0].astype(BF16), conv_w[0]], True, "gather_weights")
    c_all, ada_g = _ada_exchange(c, w_ada[0])
    ada_mine = lax.dynamic_index_in_dim(ada_g[:, :, 0, :], me, axis=1, keepdims=False)
    ada3 = (ada_mine.reshape(1, 3 * D_MODEL) + b_ada).reshape(3, D_MODEL)
    w_main, w_f = _to_internal(w_in_g)
    w_all_t = jnp.concatenate([w_main, w_f], axis=1).T
    wa = jnp.transpose(wa_g, (1, 0, 2)).reshape(ATTN_W, D_MODEL)
    wb = jnp.transpose(wb_g, (1, 0, 2)).reshape(CONV_W, D_MODEL)
    wo = wo_g.reshape(D_MODEL, D_MODEL)
    cw = jnp.transpose(cw_g, (1, 0, 2)).reshape(3, CONV_W)
    qg = jnp.tile(q_norm_g, (1, HEADS))
    kg = jnp.tile(k_norm_g, (1, HEADS))
    bf_pad = jnp.pad(b_f, ((0, 0), (0, LANES - HEADS)))

    proj, fl, ht = _proj_fwd(x2, ada3, norm_g, w_main, w_f)
    cum = _forget_cumsum(fl, bf_pad)
    qa, ka, va = _qkv_prep(proj, cum, qg, kg)
    attn, oa, qb = _attn_fwd(qa, ka, va, proj)
    ob = _conv_fwd(proj, cw)
    (dy, dgab, doa, dob, dwo, dwa, dwb, dgate, loss_part) = _tail(oa, ob, proj, x2, t2, ada3, wa, wb, wo)

    do, delta, dza = _attn_bwd_prep(doa, attn, proj)
    dq, dk, dv, dcum4, drow = _attn_bwd(qb, ka, va, do, delta)
    dcum_col = jnp.pad(dcum4[:, 0:2, :].reshape(HEADS, s).T, ((0, 0), (0, LANES - HEADS)))
    drow_col = jnp.pad(drow[:, ::HEAD_DIM], ((0, 0), (0, LANES - HEADS)))
    df, dbf = _forget_bwd(dcum_col, drow_col, fl, bf_pad)
    dqkv, dqg, dkg = _qk_norm_bwd(dq, dk, dv, proj, qg, kg)
    dconv4, dcw = _conv_bwd(dob, proj, cw)
    dproj = jnp.concatenate([dqkv, dza, dconv4, dgab, df], axis=1)
    dw_all = _dw_in(ht, dproj)
    grad_x, dshift, dscale, dnormg = _dh_and_dx(dproj, w_all_t, x2, dy, ada3, norm_g)

    slabs = [_from_internal(dw_all).astype(BF16),
             jnp.transpose(dwa.reshape(ATTN_W, N_DEV, LANES), (1, 0, 2)).astype(BF16),
             jnp.transpose(dwb.reshape(CONV_W, N_DEV, LANES), (1, 0, 2)).astype(BF16),
             dwo.reshape(N_DEV, D_MODEL // N_DEV, D_MODEL).astype(BF16)]
    g_in_parts, g_wa_parts, g_wb_parts, g_wo_parts = _exchange(slabs, False, "exchange_grads")
    vec = jnp.concatenate([dshift, dscale, dgate, dnormg, dbf, dcw.reshape(1, 3 * CONV_W), dqg, dkg], axis=1)
    (vec_all,) = _exchange([vec], True, "gather_small")
    vec_all = vec_all.reshape(N_DEV, vec.shape[1])
    n_main = 4 * D_MODEL + LANES + 3 * CONV_W
    tot, g_qg, g_kg = _sum_small(
        vec_all[:, :n_main],
        vec_all[:, n_main:n_main + ATTN_W].reshape(N_DEV * HEADS, HEAD_DIM),
        vec_all[:, n_main + ATTN_W:].reshape(N_DEV * HEADS, HEAD_DIM))
    g_b_ada = tot[:, 0:3 * D_MODEL]
    g_norm_g = tot[:, 3 * D_MODEL:4 * D_MODEL]
    g_b_f = tot[:, 4 * D_MODEL:4 * D_MODEL + HEADS]
    g_cw_full = tot[:, 4 * D_MODEL + LANES:].reshape(3, CONV_W)
    g_cw = lax.dynamic_slice(g_cw_full, (0, me * (CONV_W // N_DEV)), (3, CONV_W // N_DEV))
    dada_mine = lax.dynamic_slice(vec_all[:, 0:3 * D_MODEL], (0, me * ADA_SHARD), (N_DEV, ADA_SHARD))
    g_w_ada = _grad_w_ada(jnp.transpose(c_all, (0, 2, 1)), dada_mine.reshape(N_DEV, 1, ADA_SHARD))

    upd = {}
    upd["w_ada"] = _adamw(w_ada[0], m_w_ada[0], v_w_ada[0], g_w_ada[None], "adamw_w_ada")
    upd["b_ada"] = _adamw(b_ada, m_b_ada, v_b_ada, g_b_ada[None], "adamw_b_ada")
    upd["norm_g"] = _adamw(norm_g, m_norm_g, v_norm_g, g_norm_g[None], "adamw_norm_g")
    upd["w_in"] = _adamw(w_in[0], m_w_in[0], v_w_in[0], g_in_parts, "adamw_w_in")
    upd["b_f"] = _adamw(b_f, m_b_f, v_b_f, g_b_f[None], "adamw_b_f")
    upd["q_norm_g"] = _adamw(q_norm_g, m_q_norm_g, v_q_norm_g, g_qg[None], "adamw_q_norm_g")
    upd["k_norm_g"] = _adamw(k_norm_g, m_k_norm_g, v_k_norm_g, g_kg[None], "adamw_k_norm_g")
    upd["conv_w"] = _adamw(conv_w[0], m_conv_w[0], v_conv_w[0], g_cw[None], "adamw_conv_w")
    upd["w_attn_out"] = _adamw(w_attn_out[0], m_w_attn_out[0], v_w_attn_out[0], g_wa_parts, "adamw_w_attn_out")
    upd["w_conv_out"] = _adamw(w_conv_out[0], m_w_conv_out[0], v_w_conv_out[0], g_wb_parts, "adamw_w_conv_out")
    upd["w_o"] = _adamw(w_o[0], m_w_o[0], v_w_o[0], g_wo_parts, "adamw_w_o")

    names = ["w_ada", "b_ada", "norm_g", "w_in", "b_f", "q_norm_g", "k_norm_g", "conv_w",
             "w_attn_out", "w_conv_out", "w_o"]
    lead = {"w_ada", "w_in", "conv_w", "w_attn_out", "w_conv_out", "w_o"}
    fix = lambda n, a: a[None] if n in lead else a
    loss = lax.psum(loss_part[0, 0], ("x", "y", "c"))
    outs = [loss, grad_x[None]]
    for k in range(4):
        outs += [fix(n, upd[n][k]) for n in names]
    return tuple(outs)
```

```python
import functools

import numpy as np
import jax
import jax.numpy as jnp
from jax import lax
from jax.experimental import pallas as pl
from jax.experimental.pallas import tpu as pltpu

F32 = jnp.float32
BF16 = jnp.bfloat16

D_MODEL = 1024
HEADS = 8
HEAD_DIM = 64
ATTN_W = 512
CONV_W = 512
N_DEV = 8
IN_WIDTH = 6152
IN_SHARD = IN_WIDTH // N_DEV
N_MAIN = 6144
N_FPAD = 128
N_ALL = N_MAIN + N_FPAD
ADA_SHARD = 3 * D_MODEL // N_DEV
EPS = 1e-6
NEG = -1e30

ADAM_LR = 0.001
ADAM_B1 = 0.9
ADAM_B2 = 0.999
ADAM_EPS = 1e-08
ADAM_WD = 0.01
ADAM_STEP = 10

LANES = 128
VMEM_LIMIT = 56 * 1024 * 1024

TM_PROJ = 512
TN_PROJ = 1024
TM_ELEM = 512
TQ = 512
TM_TAIL = 256
TC_CUM = 256
TK_DW = 512
TN_DW = 896
TM_DH = 512
TK_DH = 896

OFF_Q, OFF_K, OFF_V, OFF_ZA, OFF_CONV, OFF_GA, OFF_GB = 0, 512, 1024, 1536, 2048, 4096, 5120


def _params(sem=None):
    return pltpu.CompilerParams(dimension_semantics=sem, vmem_limit_bytes=VMEM_LIMIT)


def _dot(a, b):
    return jnp.dot(a, b, preferred_element_type=F32)


def _dot_nt(a, b):
    return lax.dot_general(a, b, (((1,), (1,)), ((), ())), preferred_element_type=F32)


def _dot_tn(a, b):
    return lax.dot_general(a, b, (((0,), (0,)), ((), ())), preferred_element_type=F32)


def _sigmoid(x):
    return 1.0 / (1.0 + jnp.exp(-x))


def _lane_lo(shape):
    return lax.broadcasted_iota(jnp.int32, shape, len(shape) - 1) < HEAD_DIM


def _seg_sum(z, lo):
    a = jnp.sum(jnp.where(lo, z, 0.0), axis=-1, keepdims=True)
    b = jnp.sum(jnp.where(lo, 0.0, z), axis=-1, keepdims=True)
    return jnp.where(lo, a, b)


def _lane_col(z, lane):
    idx = lax.broadcasted_iota(jnp.int32, z.shape, 1)
    return jnp.sum(jnp.where(idx == lane, z, 0.0), axis=-1, keepdims=True)


def _sub_row(z, row):
    idx = lax.broadcasted_iota(jnp.int32, z.shape, 0)
    return jnp.sum(jnp.where(idx == row, z, 0.0), axis=0, keepdims=True)


def _mesh_pos():
    x, y, c = lax.axis_index("x"), lax.axis_index("y"), lax.axis_index("c")
    return x, y, c, 4 * x + 2 * y + c


def _peer(k, x, y, c):
    px = 1 - x if (k >> 2) & 1 else x
    py = 1 - y if (k >> 1) & 1 else y
    pc = 1 - c if k & 1 else c
    return (px, py, pc), 4 * px + 2 * py + pc


def _exchange(arrs, gather, name):
    n = len(arrs)
    any_spec = pl.BlockSpec(memory_space=pl.ANY)

    def body(*refs):
        ins, outs = refs[:n], refs[n:2 * n]
        send_sems, recv_sems, local_sems = refs[2 * n:]
        x, y, c, me = _mesh_pos()
        copies = []
        for a in range(n):
            own = ins[a] if gather else ins[a].at[me]
            local = pltpu.make_async_copy(own, outs[a].at[me], local_sems.at[a])
            local.start()
            copies.append(local)
            for k in range(1, N_DEV):
                dev, p = _peer(k, x, y, c)
                cp = pltpu.make_async_remote_copy(
                    src_ref=ins[a] if gather else ins[a].at[p],
                    dst_ref=outs[a].at[me],
                    send_sem=send_sems.at[a * (N_DEV - 1) + k - 1],
                    recv_sem=recv_sems.at[a * (N_DEV - 1) + k - 1],
                    device_id=dev, device_id_type=pl.DeviceIdType.MESH)
                cp.start()
                copies.append(cp)
        for cp in copies:
            cp.wait()

    out_shape = [jax.ShapeDtypeStruct((N_DEV,) + a.shape if gather else a.shape, a.dtype) for a in arrs]
    return pl.pallas_call(
        body, name=name, out_shape=out_shape,
        in_specs=[any_spec] * n, out_specs=[any_spec] * n,
        scratch_shapes=[pltpu.SemaphoreType.DMA((n * (N_DEV - 1),)),
                        pltpu.SemaphoreType.DMA((n * (N_DEV - 1),)),
                        pltpu.SemaphoreType.DMA((n,))],
    )(*arrs)


def _ada_exchange(c_row, w_ada_sh):
    def body(c_ref, w_ref, call_ref, adag_ref, mine_ref, send_sems, recv_sems):
        x, y, c, me = _mesh_pos()

        def copy(phase, k, src, dst):
            dev, _ = _peer(k, x, y, c)
            return pltpu.make_async_remote_copy(
                src_ref=src, dst_ref=dst,
                send_sem=send_sems.at[phase * (N_DEV - 1) + k - 1],
                recv_sem=recv_sems.at[phase * (N_DEV - 1) + k - 1],
                device_id=dev, device_id_type=pl.DeviceIdType.MESH)

        call_ref[me] = c_ref[...]
        first = [copy(0, k, c_ref, call_ref.at[me]) for k in range(1, N_DEV)]
        for cp in first:
            cp.start()
        for cp in first:
            cp.wait()
        wb = w_ref[...].astype(BF16)
        for b in range(N_DEV):
            row = jnp.broadcast_to(call_ref[b], (8, D_MODEL)).astype(BF16)
            mine_ref[b] = _sub_row(_dot(row, wb), 0)
        adag_ref[me] = mine_ref[...]
        second = [copy(1, k, mine_ref, adag_ref.at[me]) for k in range(1, N_DEV)]
        for cp in second:
            cp.start()
        for cp in second:
            cp.wait()

    vm = pl.BlockSpec(memory_space=pltpu.VMEM)
    return pl.pallas_call(
        body, name="ada_exchange",
        out_shape=[jax.ShapeDtypeStruct((N_DEV, 1, D_MODEL), F32),
                   jax.ShapeDtypeStruct((N_DEV, N_DEV, 1, ADA_SHARD), F32)],
        in_specs=[vm, vm], out_specs=[vm, vm],
        scratch_shapes=[pltpu.VMEM((N_DEV, 1, ADA_SHARD), F32),
                        pltpu.SemaphoreType.DMA((2 * (N_DEV - 1),)),
                        pltpu.SemaphoreType.DMA((2 * (N_DEV - 1),))],
        compiler_params=pltpu.CompilerParams(vmem_limit_bytes=VMEM_LIMIT),
    )(c_row, w_ada_sh)


def _proj_fwd(x, ada3, norm_g, w_main, w_f):
    s = x.shape[0]
    tm, tn = min(TM_PROJ, s), TN_PROJ

    def body(x_ref, ada_ref, g_ref, w_ref, wf_ref, proj_ref, fl_ref, ht_ref, h_s):
        @pl.when(pl.program_id(1) == 0)
        def _():
            xv = x_ref[...]
            r = lax.rsqrt(jnp.mean(xv * xv, axis=-1, keepdims=True) + EPS)
            hv = ((xv * r) * g_ref[...]) * (1.0 + ada_ref[1:2, :]) + ada_ref[0:1, :]
            hb = hv.astype(BF16)
            h_s[...] = hb
            ht_ref[...] = hv.T.astype(BF16)
            fl_ref[...] = _dot(hb, wf_ref[...])
        proj_ref[...] = _dot(h_s[...], w_ref[...])

    return pl.pallas_call(
        body, name="proj_fwd", grid=(s // tm, N_MAIN // tn),
        in_specs=[pl.BlockSpec((tm, D_MODEL), lambda i, j: (i, 0)),
                  pl.BlockSpec((3, D_MODEL), lambda i, j: (0, 0)),
                  pl.BlockSpec((1, D_MODEL), lambda i, j: (0, 0)),
                  pl.BlockSpec((D_MODEL, tn), lambda i, j: (0, j)),
                  pl.BlockSpec((D_MODEL, N_FPAD), lambda i, j: (0, 0))],
        out_specs=[pl.BlockSpec((tm, tn), lambda i, j: (i, j)),
                   pl.BlockSpec((tm, N_FPAD), lambda i, j: (i, 0)),
                   pl.BlockSpec((D_MODEL, tm), lambda i, j: (0, i))],
        out_shape=[jax.ShapeDtypeStruct((s, N_MAIN), F32),
                   jax.ShapeDtypeStruct((s, N_FPAD), F32),
                   jax.ShapeDtypeStruct((D_MODEL, s), BF16)],
        scratch_shapes=[pltpu.VMEM((tm, D_MODEL), BF16)],
        compiler_params=_params(("parallel", "arbitrary")),
    )(x, ada3, norm_g, w_main, w_f)


L_ONE_Q, L_F_Q, L_LSE_Q, L_END = HEAD_DIM, HEAD_DIM + 3, HEAD_DIM + 6, HEAD_DIM + 9


def _split3(f):
    hi = f.astype(BF16).astype(F32)
    r = f - hi
    mid = r.astype(BF16).astype(F32)
    return hi, mid, r - mid


def _place3(lane, first, parts, otherwise):
    a, b, c = parts
    return jnp.where(lane == first, a, jnp.where(lane == first + 1, b, jnp.where(lane == first + 2, c, otherwise)))


def _qkv_prep(proj, cum, qg, kg):
    s = proj.shape[0]
    tm = min(TM_ELEM, s)
    scale = HEAD_DIM ** -0.5

    def body(p_ref, cum_ref, qg_ref, kg_ref, qa_ref, ka_ref, va_ref, kt_ref, vt_ref):
        lane = lax.broadcasted_iota(jnp.int32, (tm, LANES), 1)
        lo = lane < HEAD_DIM
        cum_v = cum_ref[...]
        v_tail = jnp.where(lane < L_F_Q, 1.0, 0.0)
        for pr in range(ATTN_W // LANES):
            sl = slice(pr * LANES, (pr + 1) * LANES)
            q2 = p_ref[:, OFF_Q + pr * LANES:OFF_Q + (pr + 1) * LANES]
            k2 = p_ref[:, OFF_K + pr * LANES:OFF_K + (pr + 1) * LANES]
            v2 = p_ref[:, OFF_V + pr * LANES:OFF_V + (pr + 1) * LANES]
            rq = lax.rsqrt(_seg_sum(q2 * q2, lo) * (1.0 / HEAD_DIM) + EPS)
            rk = lax.rsqrt(_seg_sum(k2 * k2, lo) * (1.0 / HEAD_DIM) + EPS)
            qn = ((q2 * rq) * qg_ref[:, sl]) * scale
            kn = (k2 * rk) * kg_ref[:, sl]
            for hh in range(2):
                h = 2 * pr + hh
                f3 = _split3(_lane_col(cum_v, h))
                qh = qn if hh == 0 else pltpu.roll(qn, HEAD_DIM, 1)
                kh = kn if hh == 0 else pltpu.roll(kn, HEAD_DIM, 1)
                vh = v2 if hh == 0 else pltpu.roll(v2, HEAD_DIM, 1)
                q_tail = jnp.where(lane < L_F_Q, 1.0, _place3(lane, L_F_Q, f3, 0.0))
                k_tail = _place3(lane, L_ONE_Q, tuple(-f for f in f3), jnp.where(lane < L_END, 1.0, 0.0))
                k_row = jnp.where(lo, kh, k_tail)
                v_row = jnp.where(lo, vh, v_tail)
                qa_ref[h] = jnp.where(lo, qh, q_tail).astype(BF16)
                ka_ref[h] = k_row.astype(BF16)
                va_ref[h] = v_row.astype(BF16)
                kt_ref[h] = k_row.T.astype(BF16)
                vt_ref[h] = v_row.T.astype(BF16)

    heads = pl.BlockSpec((HEADS, tm, LANES), lambda i: (0, i, 0))
    heads_t = pl.BlockSpec((HEADS, LANES, tm), lambda i: (0, 0, i))
    vec = pl.BlockSpec((1, ATTN_W), lambda i: (0, 0))
    return pl.pallas_call(
        body, name="qkv_prep", grid=(s // tm,),
        in_specs=[pl.BlockSpec((tm, 3 * ATTN_W), lambda i: (i, 0)),
                  pl.BlockSpec((tm, LANES), lambda i: (i, 0)), vec, vec],
        out_specs=[heads, heads, heads, heads_t, heads_t],
        out_shape=[jax.ShapeDtypeStruct((HEADS, s, LANES), BF16)] * 3
        + [jax.ShapeDtypeStruct((HEADS, LANES, s), BF16)] * 2,
        compiler_params=_params(("parallel",)),
    )(proj, cum, qg, kg)


def _log_forget(fl, bf):
    z = fl + bf
    lf = jnp.minimum(z, 0.0) - jnp.log1p(jnp.exp(-jnp.abs(z)))
    lane = lax.broadcasted_iota(jnp.int32, z.shape, 1)
    return jnp.where(lane < HEADS, lf, 0.0)


def _forget_cumsum(fl, bf_pad):
    s = fl.shape[0]
    tc = min(TC_CUM, s)

    def body(fl_ref, bf_ref, cum_ref, carry):
        @pl.when(pl.program_id(0) == 0)
        def _():
            carry[...] = jnp.zeros_like(carry)
        lf = _log_forget(fl_ref[...], bf_ref[...])
        r = lax.broadcasted_iota(jnp.int32, (tc, tc), 0)
        cidx = lax.broadcasted_iota(jnp.int32, (tc, tc), 1)
        tri = (cidx <= r).astype(F32)
        cs = jnp.dot(tri, lf, preferred_element_type=F32, precision=lax.Precision.HIGHEST) + carry[...]
        cum_ref[...] = cs
        carry[...] = cum_ref[tc - 1:tc, :]

    return pl.pallas_call(
        body, name="forget_cumsum", grid=(s // tc,),
        in_specs=[pl.BlockSpec((tc, LANES), lambda i: (i, 0)),
                  pl.BlockSpec((1, LANES), lambda i: (0, 0))],
        out_specs=pl.BlockSpec((tc, LANES), lambda i: (i, 0)),
        out_shape=jax.ShapeDtypeStruct((s, LANES), F32),
        scratch_shapes=[pltpu.VMEM((1, LANES), F32)],
        compiler_params=_params(("arbitrary",)),
    )(fl, bf_pad)


def _causal_t(t):
    return lax.broadcasted_iota(jnp.int32, (t, t), 0) <= lax.broadcasted_iota(jnp.int32, (t, t), 1)


def _tri_steps(nt, q_major):
    if q_major:
        pairs = [(i, j) for i in range(nt) for j in range(i + 1)]
    else:
        pairs = [(i, j) for j in range(nt) for i in range(j, nt)]
    return (jnp.asarray(np.array([p[0] for p in pairs], np.int32)),
            jnp.asarray(np.array([p[1] for p in pairs], np.int32)))


def _attn_fwd(qa, ka, vt, proj):
    s = qa.shape[1]
    t = min(TQ, s)
    it, jt = _tri_steps(s // t, True)
    za_blk = OFF_ZA // LANES

    def body(it_ref, jt_ref, q_ref, k_ref, vt_ref, za_ref, attn_ref, oa_ref, qb_ref, m_s, acc_s, pair_s):
        step = pl.program_id(1)
        i, j = it_ref[step], jt_ref[step]

        @pl.when(j == 0)
        def _():
            m_s[...] = jnp.full_like(m_s, NEG)
            acc_s[...] = jnp.zeros_like(acc_s)

        def update(masked):
            for hh in range(2):
                st = _dot_nt(k_ref[hh], q_ref[hh])
                if masked:
                    st = jnp.where(_causal_t(t), st, NEG)
                m_prev = m_s[hh]
                m_next = jnp.maximum(m_prev, jnp.max(st, axis=0, keepdims=True))
                alpha = jnp.exp(m_prev - m_next)
                pt = jnp.exp(st - m_next).astype(BF16)
                acc_s[hh] = acc_s[hh] * alpha + _dot(vt_ref[hh], pt)
                m_s[hh] = m_next

        @pl.when(j < i)
        def _():
            update(False)

        @pl.when(j == i)
        def _():
            update(True)
            row = lax.broadcasted_iota(jnp.int32, (LANES, t), 0)
            lane = lax.broadcasted_iota(jnp.int32, (t, LANES), 1)
            for hh in range(2):
                l_row = acc_s[hh, L_ONE_Q:L_ONE_Q + 1, :]
                pair_s[hh * HEAD_DIM:(hh + 1) * HEAD_DIM, :] = acc_s[hh, 0:HEAD_DIM, :] / l_row
                lse3 = _split3(m_s[hh] + jnp.log(l_row))
                tail_t = _place3(row, L_LSE_Q, tuple(-x for x in lse3), 0.0)
                keep_q = jnp.logical_or(lane < L_LSE_Q, lane >= L_END)
                qb_ref[hh] = jnp.where(keep_q, q_ref[hh].astype(F32), tail_t.T).astype(BF16)
            out = pair_s[...].T
            attn_ref[...] = out
            z = za_ref[...]
            oa_ref[...] = (out * (z * _sigmoid(z))).astype(BF16)

    pair_q = pl.BlockSpec((2, t, LANES), lambda p, n, it_, jt_: (p, it_[n], 0))
    pair_k = pl.BlockSpec((2, t, LANES), lambda p, n, it_, jt_: (p, jt_[n], 0))
    pair_kt = pl.BlockSpec((2, LANES, t), lambda p, n, it_, jt_: (p, 0, jt_[n]))
    out_q = pl.BlockSpec((t, LANES), lambda p, n, it_, jt_: (it_[n], p))
    return pl.pallas_call(
        body, name="attn_fwd",
        grid_spec=pltpu.PrefetchScalarGridSpec(
            num_scalar_prefetch=2, grid=(ATTN_W // LANES, it.shape[0]),
            in_specs=[pair_q, pair_k, pair_kt,
                      pl.BlockSpec((t, LANES), lambda p, n, it_, jt_: (it_[n], za_blk + p))],
            out_specs=[out_q, out_q, pair_q],
            scratch_shapes=[pltpu.VMEM((2, 1, t), F32), pltpu.VMEM((2, LANES, t), F32),
                            pltpu.VMEM((LANES, t), F32)]),
        out_shape=[jax.ShapeDtypeStruct((s, ATTN_W), F32),
                   jax.ShapeDtypeStruct((s, ATTN_W), BF16),
                   jax.ShapeDtypeStruct((HEADS, s, LANES), BF16)],
        compiler_params=_params(("parallel", "arbitrary")),
    )(it, jt, qa, ka, vt, proj)


def _conv_parts(blk, halo, first, w_ref, tm):
    gb, gc = blk[:, 0:LANES], blk[:, LANES:2 * LANES]
    u, zb = blk[:, 2 * LANES:3 * LANES], blk[:, 3 * LANES:4 * LANES]
    cu = gc * u
    cu_h = jnp.where(first, 0.0, halo[:, LANES:2 * LANES] * halo[:, 2 * LANES:3 * LANES])
    prev1, prev2 = _sub_row(cu_h, 7), _sub_row(cu_h, 6)
    row = lax.broadcasted_iota(jnp.int32, (tm, LANES), 0)
    r1 = jnp.where(row == 0, prev1, pltpu.roll(cu, 1, 0))
    r2 = jnp.where(row == 0, prev2, jnp.where(row == 1, prev1, pltpu.roll(cu, 2, 0)))
    conv = w_ref[2:3, :] * cu + w_ref[1:2, :] * r1 + w_ref[0:1, :] * r2
    return gb, gc, u, zb, cu, r1, r2, conv


def _conv_fwd(proj, conv_w):
    s = proj.shape[0]
    tm = min(TM_ELEM, s)
    cb = OFF_CONV // (4 * LANES)

    def body(p_ref, halo_ref, w_ref, ob_ref):
        first = pl.program_id(1) == 0
        gb, _, _, zb, _, _, _, conv = _conv_parts(p_ref[...], halo_ref[...], first, w_ref, tm)
        ob_ref[...] = (gb * conv * (zb * _sigmoid(zb))).astype(BF16)

    return pl.pallas_call(
        body, name="conv_fwd", grid=(CONV_W // LANES, s // tm),
        in_specs=[pl.BlockSpec((tm, 4 * LANES), lambda c, i: (i, cb + c)),
                  pl.BlockSpec((8, 4 * LANES), lambda c, i: (jnp.maximum(i * (tm // 8) - 1, 0), cb + c)),
                  pl.BlockSpec((3, LANES), lambda c, i: (0, c))],
        out_specs=pl.BlockSpec((tm, LANES), lambda c, i: (i, c)),
        out_shape=jax.ShapeDtypeStruct((s, CONV_W), BF16),
        compiler_params=_params(("parallel", "parallel")),
    )(proj, proj, conv_w)


def _tail(oa, ob, proj, x, target, ada3, wa, wb, wo):
    s = x.shape[0]
    tm = min(TM_TAIL, s)
    gab_blk = OFF_GA // (2 * D_MODEL)

    def body(oa_ref, ob_ref, gab_ref, x_ref, t_ref, ada_ref, wa_ref, wb_ref, wo_ref,
             dy_ref, dgab_ref, doa_ref, dob_ref, dwo_ref, dwa_ref, dwb_ref, dgate_ref, loss_ref):
        @pl.when(pl.program_id(0) == 0)
        def _():
            dwo_ref[...] = jnp.zeros_like(dwo_ref)
            dwa_ref[...] = jnp.zeros_like(dwa_ref)
            dwb_ref[...] = jnp.zeros_like(dwb_ref)
            dgate_ref[...] = jnp.zeros_like(dgate_ref)
            loss_ref[...] = jnp.zeros_like(loss_ref)

        oa_v, ob_v = oa_ref[...], ob_ref[...]
        wa_v, wb_v, wo_v = wa_ref[...], wb_ref[...], wo_ref[...]
        a2 = _dot(oa_v, wa_v)
        b2 = _dot(ob_v, wb_v)
        sa = _sigmoid(gab_ref[:, 0:D_MODEL])
        sb = _sigmoid(gab_ref[:, D_MODEL:2 * D_MODEL])
        mb = (sa * a2 + sb * b2).astype(BF16)
        mo = _dot(mb, wo_v)
        gate = ada_ref[2:3, :]
        err = (x_ref[...] + gate * mo) - t_ref[...]
        dy = err * (1.0 / D_MODEL)
        dy_ref[...] = dy
        loss_ref[...] += 0.5 * jnp.sum(err * err) * (1.0 / D_MODEL)
        dgate_ref[...] += jnp.sum(dy * mo, axis=0, keepdims=True)
        dmo = (dy * gate).astype(BF16)
        dmerged = _dot_nt(dmo, wo_v)
        dwo_ref[...] += _dot_tn(mb, dmo)
        da2 = (dmerged * sa).astype(BF16)
        db2 = (dmerged * sb).astype(BF16)
        dgab_ref[:, 0:D_MODEL] = (dmerged * a2 * (sa * (1.0 - sa))).astype(BF16)
        dgab_ref[:, D_MODEL:2 * D_MODEL] = (dmerged * b2 * (sb * (1.0 - sb))).astype(BF16)
        doa_ref[...] = _dot_nt(da2, wa_v)
        dob_ref[...] = _dot_nt(db2, wb_v)
        dwa_ref[...] += _dot_tn(oa_v, da2)
        dwb_ref[...] += _dot_tn(ob_v, db2)

    half = pl.BlockSpec((tm, ATTN_W), lambda i: (i, 0))
    full = pl.BlockSpec((tm, D_MODEL), lambda i: (i, 0))

    def const(shape):
        return pl.BlockSpec(shape, lambda i: (0, 0))

    return pl.pallas_call(
        body, name="tail", grid=(s // tm,),
        in_specs=[half, half, pl.BlockSpec((tm, 2 * D_MODEL), lambda i: (i, gab_blk)), full, full,
                  const((3, D_MODEL)), const((ATTN_W, D_MODEL)), const((CONV_W, D_MODEL)),
                  const((D_MODEL, D_MODEL))],
        out_specs=[full, pl.BlockSpec((tm, 2 * D_MODEL), lambda i: (i, 0)), half, half,
                   const((D_MODEL, D_MODEL)), const((ATTN_W, D_MODEL)), const((CONV_W, D_MODEL)),
                   const((1, D_MODEL)), const((1, LANES))],
        out_shape=[jax.ShapeDtypeStruct((s, D_MODEL), F32),
                   jax.ShapeDtypeStruct((s, 2 * D_MODEL), BF16),
                   jax.ShapeDtypeStruct((s, ATTN_W), F32),
                   jax.ShapeDtypeStruct((s, CONV_W), F32),
                   jax.ShapeDtypeStruct((D_MODEL, D_MODEL), F32),
                   jax.ShapeDtypeStruct((ATTN_W, D_MODEL), F32),
                   jax.ShapeDtypeStruct((CONV_W, D_MODEL), F32),
                   jax.ShapeDtypeStruct((1, D_MODEL), F32),
                   jax.ShapeDtypeStruct((1, LANES), F32)],
        compiler_params=_params(("arbitrary",)),
    )(oa, ob, proj, x, target, ada3, wa, wb, wo)


def _attn_bwd_prep(doa, attn, proj):
    s = doa.shape[0]
    tm = min(TM_ELEM, s)
    za_blk = OFF_ZA // ATTN_W

    def body(doa_ref, attn_ref, za_ref, do_ref, dza_ref):
        lane = lax.broadcasted_iota(jnp.int32, (tm, LANES), 1)
        lo = lane < HEAD_DIM
        for pr in range(ATTN_W // LANES):
            sl = slice(pr * LANES, (pr + 1) * LANES)
            g, a, z = doa_ref[:, sl], attn_ref[:, sl], za_ref[:, sl]
            sg = _sigmoid(z)
            dat = (g * (z * sg)).astype(BF16).astype(F32)
            prod = dat * a
            dza_ref[:, sl] = (g * a * (sg * (1.0 + z * (1.0 - sg)))).astype(BF16)
            for hh in range(2):
                sel = lo if hh == 0 else jnp.logical_not(lo)
                delta3 = _split3(jnp.sum(jnp.where(sel, prod, 0.0), axis=-1, keepdims=True))
                dh = dat if hh == 0 else pltpu.roll(dat, HEAD_DIM, 1)
                tail = _place3(lane, L_ONE_Q, tuple(-d for d in delta3), 0.0)
                do_ref[2 * pr + hh] = jnp.where(lo, dh, tail).astype(BF16)

    row = pl.BlockSpec((tm, ATTN_W), lambda i: (i, 0))
    return pl.pallas_call(
        body, name="attn_bwd_prep", grid=(s // tm,),
        in_specs=[row, row, pl.BlockSpec((tm, ATTN_W), lambda i: (i, za_blk))],
        out_specs=[pl.BlockSpec((HEADS, tm, LANES), lambda i: (0, i, 0)), row],
        out_shape=[jax.ShapeDtypeStruct((HEADS, s, LANES), BF16),
                   jax.ShapeDtypeStruct((s, ATTN_W), BF16)],
        compiler_params=_params(("parallel",)),
    )(doa, attn, proj)


def _attn_bwd(qb, ka, kt, va, do):
    s = qb.shape[1]
    t = min(TQ, s)
    nt = s // t
    it, jt = _tri_steps(nt, False)

    def body(it_ref, jt_ref, q_ref, k_ref, kt_ref, v_ref, do_ref, dqt_ref, dk_ref, dv_ref):
        step = pl.program_id(1)
        i, j = it_ref[step], jt_ref[step]

        @pl.when(step == 0)
        def _():
            dqt_ref[...] = jnp.zeros_like(dqt_ref)

        @pl.when(i == j)
        def _():
            dk_ref[...] = jnp.zeros_like(dk_ref)
            dv_ref[...] = jnp.zeros_like(dv_ref)

        def update(masked):
            for hh in range(2):
                qh, doh = q_ref[hh], do_ref[hh]
                st = _dot_nt(k_ref[hh], qh)
                if masked:
                    st = jnp.where(_causal_t(t), st, NEG)
                pt = jnp.exp(st)
                dst = (pt * _dot_nt(v_ref[hh], doh)).astype(BF16)
                dv_ref[hh] += _dot(pt.astype(BF16), doh)
                dk_ref[hh] += _dot(dst, qh)
                dqt_ref[hh, i] += _dot(kt_ref[hh], dst)

        @pl.when(i > j)
        def _():
            update(False)

        @pl.when(i == j)
        def _():
            update(True)

    pair_q = pl.BlockSpec((2, t, LANES), lambda p, n, it_, jt_: (p, it_[n], 0))
    pair_k = pl.BlockSpec((2, t, LANES), lambda p, n, it_, jt_: (p, jt_[n], 0))
    pair_kt = pl.BlockSpec((2, LANES, t), lambda p, n, it_, jt_: (p, 0, jt_[n]))
    return pl.pallas_call(
        body, name="attn_bwd",
        grid_spec=pltpu.PrefetchScalarGridSpec(
            num_scalar_prefetch=2, grid=(ATTN_W // LANES, it.shape[0]),
            in_specs=[pair_q, pair_k, pair_kt, pair_k, pair_q],
            out_specs=[pl.BlockSpec((2, nt, LANES, t), lambda p, n, it_, jt_: (p, 0, 0, 0)),
                       pair_k, pair_k]),
        out_shape=[jax.ShapeDtypeStruct((HEADS, nt, LANES, t), F32),
                   jax.ShapeDtypeStruct((HEADS, s, LANES), F32),
                   jax.ShapeDtypeStruct((HEADS, s, LANES), F32)],
        compiler_params=_params(("parallel", "arbitrary")),
    )(it, jt, qb, ka, kt, va, do)


def _forget_bwd(dcum_col, drow_col, fl, bf_pad):
    s = fl.shape[0]
    tc = min(TC_CUM, s)
    n = s // tc

    def body(dc_ref, dr_ref, fl_ref, bf_ref, df_ref, dbf_ref, carry):
        @pl.when(pl.program_id(0) == 0)
        def _():
            carry[...] = jnp.zeros_like(carry)
            dbf_ref[...] = jnp.zeros_like(dbf_ref)
        r = lax.broadcasted_iota(jnp.int32, (tc, tc), 0)
        cidx = lax.broadcasted_iota(jnp.int32, (tc, tc), 1)
        tri = (cidx >= r).astype(F32)
        dc = dc_ref[...] + dr_ref[...]
        dlf = jnp.dot(tri, dc, preferred_element_type=F32, precision=lax.Precision.HIGHEST) + carry[...]
        carry[...] += jnp.sum(dc, axis=0, keepdims=True)
        lane = lax.broadcasted_iota(jnp.int32, (tc, LANES), 1)
        dfl = jnp.where(lane < HEADS, dlf * _sigmoid(-(fl_ref[...] + bf_ref[...])), 0.0)
        df_ref[...] = dfl.astype(BF16)
        dbf_ref[...] += jnp.sum(dfl, axis=0, keepdims=True)

    rev = pl.BlockSpec((tc, LANES), lambda i: (n - 1 - i, 0))
    vec = pl.BlockSpec((1, LANES), lambda i: (0, 0))
    return pl.pallas_call(
        body, name="forget_bwd", grid=(n,),
        in_specs=[rev, rev, rev, vec], out_specs=[rev, vec],
        out_shape=[jax.ShapeDtypeStruct((s, LANES), BF16), jax.ShapeDtypeStruct((1, LANES), F32)],
        scratch_shapes=[pltpu.VMEM((1, LANES), F32)],
        compiler_params=_params(("arbitrary",)),
    )(dcum_col, drow_col, fl, bf_pad)


def _qk_norm_bwd(dqt, dk, dv, proj, qg, kg):
    s = dv.shape[1]
    tm = dqt.shape[-1]
    scale = HEAD_DIM ** -0.5

    def body(dqt_ref, dk_ref, dv_ref, p_ref, qg_ref, kg_ref, out_ref, dqg_ref, dkg_ref):
        @pl.when(pl.program_id(0) == 0)
        def _():
            dqg_ref[...] = jnp.zeros_like(dqg_ref)
            dkg_ref[...] = jnp.zeros_like(dkg_ref)
        lo = _lane_lo((tm, LANES))

        def pair(a, b):
            return jnp.where(lo, a, pltpu.roll(b, HEAD_DIM, 1))

        def one(raw, dy, g, dg_ref, sl, off):
            r = lax.rsqrt(_seg_sum(raw * raw, lo) * (1.0 / HEAD_DIM) + EPS)
            xhat = raw * r
            dg_ref[:, sl] += jnp.sum(dy * xhat, axis=0, keepdims=True)
            dxh = dy * g
            dx = r * (dxh - xhat * (_seg_sum(dxh * xhat, lo) * (1.0 / HEAD_DIM)))
            out_ref[:, off + sl.start:off + sl.stop] = dx.astype(BF16)

        for pr in range(ATTN_W // LANES):
            sl = slice(pr * LANES, (pr + 1) * LANES)
            dq2 = pair(dqt_ref[2 * pr, 0].T, dqt_ref[2 * pr + 1, 0].T)
            one(p_ref[:, OFF_Q + sl.start:OFF_Q + sl.stop], dq2 * scale, qg_ref[:, sl], dqg_ref, sl, OFF_Q)
            one(p_ref[:, OFF_K + sl.start:OFF_K + sl.stop], pair(dk_ref[2 * pr], dk_ref[2 * pr + 1]),
                kg_ref[:, sl], dkg_ref, sl, OFF_K)
            out_ref[:, OFF_V + sl.start:OFF_V + sl.stop] = pair(dv_ref[2 * pr], dv_ref[2 * pr + 1]).astype(BF16)

    heads = pl.BlockSpec((HEADS, tm, LANES), lambda i: (0, i, 0))
    vec = pl.BlockSpec((1, ATTN_W), lambda i: (0, 0))
    return pl.pallas_call(
        body, name="qk_norm_bwd", grid=(s // tm,),
        in_specs=[pl.BlockSpec((HEADS, 1, LANES, tm), lambda i: (0, i, 0, 0)), heads, heads,
                  pl.BlockSpec((tm, 2 * ATTN_W), lambda i: (i, 0)), vec, vec],
        out_specs=[pl.BlockSpec((tm, 3 * ATTN_W), lambda i: (i, 0)), vec, vec],
        out_shape=[jax.ShapeDtypeStruct((s, 3 * ATTN_W), BF16),
                   jax.ShapeDtypeStruct((1, ATTN_W), F32), jax.ShapeDtypeStruct((1, ATTN_W), F32)],
        compiler_params=_params(("arbitrary",)),
    )(dqt, dk, dv, proj, qg, kg)


def _conv_bwd(dob, proj, conv_w):
    s = dob.shape[0]
    tm = min(TM_ELEM, s)
    cb = OFF_CONV // (4 * LANES)
    nblk8 = s // 8

    def body(dob_ref, p_ref, prev_ref, next_ref, dnext_ref, w_ref, out_ref, dw_ref):
        i = pl.program_id(1)

        @pl.when(i == 0)
        def _():
            dw_ref[...] = jnp.zeros_like(dw_ref)
        gb, gc, u, zb, cu, r1, r2, conv = _conv_parts(p_ref[...], prev_ref[...], i == 0, w_ref, tm)
        g = dob_ref[...]
        sg = _sigmoid(zb)
        sz = zb * sg
        dconv = g * gb * sz
        nxt = next_ref[...]
        zn = nxt[:, 3 * LANES:4 * LANES]
        dcn = jnp.where(i == pl.num_programs(1) - 1, 0.0,
                        dnext_ref[...] * nxt[:, 0:LANES] * (zn * _sigmoid(zn)))
        nxt1, nxt2 = _sub_row(dcn, 0), _sub_row(dcn, 1)
        row = lax.broadcasted_iota(jnp.int32, (tm, LANES), 0)
        f1 = jnp.where(row == tm - 1, nxt1, pltpu.roll(dconv, tm - 1, 0))
        f2 = jnp.where(row == tm - 2, nxt1, jnp.where(row == tm - 1, nxt2, pltpu.roll(dconv, tm - 2, 0)))
        dcu = w_ref[2:3, :] * dconv + w_ref[1:2, :] * f1 + w_ref[0:1, :] * f2
        out_ref[:, 0:LANES] = (g * conv * sz).astype(BF16)
        out_ref[:, LANES:2 * LANES] = (dcu * u).astype(BF16)
        out_ref[:, 2 * LANES:3 * LANES] = (dcu * gc).astype(BF16)
        out_ref[:, 3 * LANES:4 * LANES] = (g * gb * conv * (sg * (1.0 + zb * (1.0 - sg)))).astype(BF16)
        w_row = lax.broadcasted_iota(jnp.int32, (3, LANES), 0)
        dw0 = jnp.sum(dconv * r2, axis=0, keepdims=True)
        dw1 = jnp.sum(dconv * r1, axis=0, keepdims=True)
        dw2 = jnp.sum(dconv * cu, axis=0, keepdims=True)
        dw_ref[...] += jnp.where(w_row == 0, dw0, jnp.where(w_row == 1, dw1, dw2))

    nxt_idx = lambda i: jnp.minimum((i + 1) * (tm // 8), nblk8 - 1)
    return pl.pallas_call(
        body, name="conv_bwd", grid=(CONV_W // LANES, s // tm),
        in_specs=[pl.BlockSpec((tm, LANES), lambda c, i: (i, c)),
                  pl.BlockSpec((tm, 4 * LANES), lambda c, i: (i, cb + c)),
                  pl.BlockSpec((8, 4 * LANES), lambda c, i: (jnp.maximum(i * (tm // 8) - 1, 0), cb + c)),
                  pl.BlockSpec((8, 4 * LANES), lambda c, i: (nxt_idx(i), cb + c)),
                  pl.BlockSpec((8, LANES), lambda c, i: (nxt_idx(i), c)),
                  pl.BlockSpec((3, LANES), lambda c, i: (0, c))],
        out_specs=[pl.BlockSpec((tm, 4 * LANES), lambda c, i: (i, c)),
                   pl.BlockSpec((3, LANES), lambda c, i: (0, c))],
        out_shape=[jax.ShapeDtypeStruct((s, 4 * CONV_W), BF16), jax.ShapeDtypeStruct((3, CONV_W), F32)],
        compiler_params=_params(("parallel", "arbitrary")),
    )(dob, proj, proj, proj, dob, conv_w)


def _dw_in(ht, dproj):
    s = ht.shape[1]
    tk, tn = min(TK_DW, s), TN_DW

    def body(ht_ref, dp_ref, out_ref):
        @pl.when(pl.program_id(1) == 0)
        def _():
            out_ref[...] = jnp.zeros_like(out_ref)
        out_ref[...] += _dot(ht_ref[...], dp_ref[...])

    return pl.pallas_call(
        body, name="dw_in", grid=(N_ALL // tn, s // tk),
        in_specs=[pl.BlockSpec((D_MODEL, tk), lambda n, k: (0, k)),
                  pl.BlockSpec((tk, tn), lambda n, k: (k, n))],
        out_specs=pl.BlockSpec((D_MODEL, tn), lambda n, k: (0, n)),
        out_shape=jax.ShapeDtypeStruct((D_MODEL, N_ALL), F32),
        compiler_params=_params(("parallel", "arbitrary")),
    )(ht, dproj)


def _dh_and_dx(dproj, w_all_t, x, dy, ada3, norm_g):
    s = x.shape[0]
    tm, tk = min(TM_DH, s), TK_DH
    nk = N_ALL // tk

    def body(dp_ref, wt_ref, x_ref, dy_ref, ada_ref, g_ref, gx_ref, dsh_ref, dsc_ref, dg_ref, acc):
        i, k = pl.program_id(0), pl.program_id(1)

        @pl.when(jnp.logical_and(i == 0, k == 0))
        def _():
            dsh_ref[...] = jnp.zeros_like(dsh_ref)
            dsc_ref[...] = jnp.zeros_like(dsc_ref)
            dg_ref[...] = jnp.zeros_like(dg_ref)

        @pl.when(k == 0)
        def _():
            acc[...] = jnp.zeros_like(acc)
        acc[...] += _dot(dp_ref[...], wt_ref[...])

        @pl.when(k == nk - 1)
        def _():
            dh = acc[...]
            xv = x_ref[...]
            r = lax.rsqrt(jnp.mean(xv * xv, axis=-1, keepdims=True) + EPS)
            xhat = xv * r
            g = g_ref[...]
            one_sc = 1.0 + ada_ref[1:2, :]
            dsh_ref[...] += jnp.sum(dh, axis=0, keepdims=True)
            dsc_ref[...] += jnp.sum(dh * (xhat * g), axis=0, keepdims=True)
            dg_ref[...] += jnp.sum(dh * xhat, axis=0, keepdims=True) * one_sc
            dxh = dh * (g * one_sc)
            dx = r * (dxh - xhat * jnp.mean(dxh * xhat, axis=-1, keepdims=True))
            gx_ref[...] = dy_ref[...] + dx

    full = pl.BlockSpec((tm, D_MODEL), lambda i, k: (i, 0))
    vec = pl.BlockSpec((1, D_MODEL), lambda i, k: (0, 0))
    return pl.pallas_call(
        body, name="dh_dx", grid=(s // tm, nk),
        in_specs=[pl.BlockSpec((tm, tk), lambda i, k: (i, k)),
                  pl.BlockSpec((tk, D_MODEL), lambda i, k: (k, 0)),
                  full, full, pl.BlockSpec((3, D_MODEL), lambda i, k: (0, 0)), vec],
        out_specs=[full, vec, vec, vec],
        out_shape=[jax.ShapeDtypeStruct((s, D_MODEL), F32)] + [jax.ShapeDtypeStruct((1, D_MODEL), F32)] * 3,
        scratch_shapes=[pltpu.VMEM((tm, D_MODEL), F32)],
        compiler_params=_params(("arbitrary", "arbitrary")),
    )(dproj, w_all_t, x, dy, ada3, norm_g)


def _sum_small(vec_all, qg_parts, kg_parts):
    def body(v_ref, q_ref, k_ref, tot_ref, gq_ref, gk_ref):
        tot = v_ref[0:1, :]
        for p in range(1, N_DEV):
            tot = tot + v_ref[p:p + 1, :]
        tot_ref[...] = tot
        gq_ref[...] = jnp.sum(q_ref[...], axis=0, keepdims=True)
        gk_ref[...] = jnp.sum(k_ref[...], axis=0, keepdims=True)

    n = vec_all.shape[-1]
    return pl.pallas_call(
        body, name="sum_small",
        out_shape=[jax.ShapeDtypeStruct((1, n), F32),
                   jax.ShapeDtypeStruct((1, HEAD_DIM), F32), jax.ShapeDtypeStruct((1, HEAD_DIM), F32)],
        compiler_params=_params(),
    )(vec_all, qg_parts, kg_parts)


def _grad_w_ada(c_cols, dada_rows):
    def body(c_ref, d_ref, out_ref):
        acc = c_ref[0] * d_ref[0]
        for b in range(1, N_DEV):
            acc = acc + c_ref[b] * d_ref[b]
        out_ref[...] = acc

    return pl.pallas_call(
        body, name="grad_w_ada",
        out_shape=jax.ShapeDtypeStruct((D_MODEL, ADA_SHARD), F32),
        compiler_params=_params(),
    )(c_cols, dada_rows)


def _adamw(w, m, v, g_parts, name):
    rows, cols = w.shape
    n_parts = g_parts.shape[0]
    tr = 256 if rows % 256 == 0 else rows
    c1 = 1.0 / (1.0 - ADAM_B1 ** ADAM_STEP)
    c2 = 1.0 / (1.0 - ADAM_B2 ** ADAM_STEP)

    def body(w_ref, m_ref, v_ref, g_ref, go_ref, d_ref, mo_ref, vo_ref):
        g = g_ref[0].astype(F32)
        for p in range(1, n_parts):
            g = g + g_ref[p].astype(F32)
        m_new = ADAM_B1 * m_ref[...] + (1.0 - ADAM_B1) * g
        v_new = ADAM_B2 * v_ref[...] + (1.0 - ADAM_B2) * (g * g)
        go_ref[...] = g
        mo_ref[...] = m_new
        vo_ref[...] = v_new
        d_ref[...] = -ADAM_LR * ((m_new * c1) / (jnp.sqrt(v_new * c2) + ADAM_EPS) + ADAM_WD * w_ref[...])

    blk = pl.BlockSpec((tr, cols), lambda i: (i, 0))
    return pl.pallas_call(
        body, name=name, grid=(rows // tr,),
        in_specs=[blk, blk, blk, pl.BlockSpec((n_parts, tr, cols), lambda i: (0, i, 0))],
        out_specs=[blk] * 4,
        out_shape=[jax.ShapeDtypeStruct((rows, cols), F32)] * 4,
        compiler_params=_params(("parallel",)),
    )(w, m, v, g_parts)


_O_Q, _O_K, _O_V, _O_F, _O_ZA, _O_GB, _O_GC, _O_U, _O_ZB, _O_GA, _O_GB2 = (
    0, 512, 1024, 1536, 1544, 2056, 2568, 3080, 3592, 4104, 5128)


def _to_internal(w_in_g):
    wf = jnp.transpose(w_in_g, (1, 0, 2)).reshape(D_MODEL, IN_WIDTH)
    cols = lambda a, n: wf[:, a:a + n]
    conv = [cols(base + LANES * c, LANES) for c in range(4) for base in (_O_GB, _O_GC, _O_U, _O_ZB)]
    main = jnp.concatenate([cols(_O_Q, 512), cols(_O_K, 512), cols(_O_V, 512), cols(_O_ZA, 512), *conv,
                            cols(_O_GA, 1024), cols(_O_GB2, 1024)], axis=1)
    f = jnp.pad(cols(_O_F, HEADS), ((0, 0), (0, N_FPAD - HEADS)))
    return main, f


def _from_internal(dw):
    cols = lambda a, n: dw[:, a:a + n]
    conv = lambda k: [cols(OFF_CONV + 4 * LANES * c + LANES * k, LANES) for c in range(4)]
    full = jnp.concatenate([cols(OFF_Q, 512), cols(OFF_K, 512), cols(OFF_V, 512), cols(N_MAIN, HEADS),
                            cols(OFF_ZA, 512), *conv(0), *conv(1), *conv(2), *conv(3),
                            cols(OFF_GA, 1024), cols(OFF_GB, 1024)], axis=1)
    return jnp.transpose(full.reshape(D_MODEL, N_DEV, IN_SHARD), (1, 0, 2))


def kernel(x, c, w_ada, b_ada, norm_g, w_in, b_f, q_norm_g, k_norm_g, conv_w, w_attn_out, w_conv_out, w_o, loss_target, m_w_ada, m_b_ada, m_norm_g, m_w_in, m_b_f, m_q_norm_g, m_k_norm_g, m_conv_w, m_w_attn_out, m_w_conv_out, m_w_o, v_w_ada, v_b_ada, v_norm_g, v_w_in, v_b_f, v_q_norm_g, v_k_norm_g, v_conv_w, v_w_attn_out, v_w_conv_out, v_w_o):
    me = 4 * lax.axis_index("x") + 2 * lax.axis_index("y") + lax.axis_index("c")
    s = x.shape[1]
    x2, t2 = x[0], loss_target[0]

    w_in_g, wa_g, wb_g, wo_g, cw_g = _exchange(
        [w_in[0].astype(BF16), w_attn_out[0].astype(BF16), w_conv_out[0].astype(BF16),
         w_o[0].astype(BF16), conv_w[0]], True, "gather_weights")
    c_all, ada_g = _ada_exchange(c, w_ada[0])
    ada_mine = lax.dynamic_index_in_dim(ada_g[:, :, 0, :], me, axis=1, keepdims=False)
    ada3 = (ada_mine.reshape(1, 3 * D_MODEL) + b_ada).reshape(3, D_MODEL)
    w_main, w_f = _to_internal(w_in_g)
    w_all_t = jnp.concatenate([w_main, w_f], axis=1).T
    wa = jnp.transpose(wa_g, (1, 0, 2)).reshape(ATTN_W, D_MODEL)
    wb = jnp.transpose(wb_g, (1, 0, 2)).reshape(CONV_W, D_MODEL)
    wo = wo_g.reshape(D_MODEL, D_MODEL)
    cw = jnp.transpose(cw_g, (1, 0, 2)).reshape(3, CONV_W)
    qg = jnp.tile(q_norm_g, (1, HEADS))
    kg = jnp.tile(k_norm_g, (1, HEADS))
    bf_pad = jnp.pad(b_f, ((0, 0), (0, LANES - HEADS)))

    proj, fl, ht = _proj_fwd(x2, ada3, norm_g, w_main, w_f)
    cum = _forget_cumsum(fl, bf_pad)
    qa, ka, va, kt, vt = _qkv_prep(proj, cum, qg, kg)
    attn, oa, qb = _attn_fwd(qa, ka, vt, proj)
    ob = _conv_fwd(proj, cw)
    (dy, dgab, doa, dob, dwo, dwa, dwb, dgate, loss_part) = _tail(oa, ob, proj, x2, t2, ada3, wa, wb, wo)

    do, dza = _attn_bwd_prep(doa, attn, proj)
    dqt, dk, dv = _attn_bwd(qb, ka, kt, va, do)
    dcum_col = jnp.pad(-dk[:, :, L_ONE_Q].T, ((0, 0), (0, LANES - HEADS)))
    drow_col = jnp.pad(dqt[:, :, L_F_Q, :].reshape(HEADS, s).T, ((0, 0), (0, LANES - HEADS)))
    df, dbf = _forget_bwd(dcum_col, drow_col, fl, bf_pad)
    dqkv, dqg, dkg = _qk_norm_bwd(dqt, dk, dv, proj, qg, kg)
    dconv4, dcw = _conv_bwd(dob, proj, cw)
    dproj = jnp.concatenate([dqkv, dza, dconv4, dgab, df], axis=1)
    dw_all = _dw_in(ht, dproj)
    grad_x, dshift, dscale, dnormg = _dh_and_dx(dproj, w_all_t, x2, dy, ada3, norm_g)

    slabs = [_from_internal(dw_all).astype(BF16),
             jnp.transpose(dwa.reshape(ATTN_W, N_DEV, LANES), (1, 0, 2)).astype(BF16),
             jnp.transpose(dwb.reshape(CONV_W, N_DEV, LANES), (1, 0, 2)).astype(BF16),
             dwo.reshape(N_DEV, D_MODEL // N_DEV, D_MODEL).astype(BF16)]
    g_in_parts, g_wa_parts, g_wb_parts, g_wo_parts = _exchange(slabs, False, "exchange_grads")
    vec = jnp.concatenate([dshift, dscale, dgate, dnormg, dbf, dcw.reshape(1, 3 * CONV_W), dqg, dkg], axis=1)
    (vec_all,) = _exchange([vec], True, "gather_small")
    vec_all = vec_all.reshape(N_DEV, vec.shape[1])
    n_main = 4 * D_MODEL + LANES + 3 * CONV_W
    tot, g_qg, g_kg = _sum_small(
        vec_all[:, :n_main],
        vec_all[:, n_main:n_main + ATTN_W].reshape(N_DEV * HEADS, HEAD_DIM),
        vec_all[:, n_main + ATTN_W:].reshape(N_DEV * HEADS, HEAD_DIM))
    g_b_ada = tot[:, 0:3 * D_MODEL]
    g_norm_g = tot[:, 3 * D_MODEL:4 * D_MODEL]
    g_b_f = tot[:, 4 * D_MODEL:4 * D_MODEL + HEADS]
    g_cw_full = tot[:, 4 * D_MODEL + LANES:].reshape(3, CONV_W)
    g_cw = lax.dynamic_slice(g_cw_full, (0, me * (CONV_W // N_DEV)), (3, CONV_W // N_DEV))
    dada_mine = lax.dynamic_slice(vec_all[:, 0:3 * D_MODEL], (0, me * ADA_SHARD), (N_DEV, ADA_SHARD))
    g_w_ada = _grad_w_ada(jnp.transpose(c_all, (0, 2, 1)), dada_mine.reshape(N_DEV, 1, ADA_SHARD))

    upd = {}
    upd["w_ada"] = _adamw(w_ada[0], m_w_ada[0], v_w_ada[0], g_w_ada[None], "adamw_w_ada")
    upd["b_ada"] = _adamw(b_ada, m_b_ada, v_b_ada, g_b_ada[None], "adamw_b_ada")
    upd["norm_g"] = _adamw(norm_g, m_norm_g, v_norm_g, g_norm_g[None], "adamw_norm_g")
    upd["w_in"] = _adamw(w_in[0], m_w_in[0], v_w_in[0], g_in_parts, "adamw_w_in")
    upd["b_f"] = _adamw(b_f, m_b_f, v_b_f, g_b_f[None], "adamw_b_f")
    upd["q_norm_g"] = _adamw(q_norm_g, m_q_norm_g, v_q_norm_g, g_qg[None], "adamw_q_norm_g")
    upd["k_norm_g"] = _adamw(k_norm_g, m_k_norm_g, v_k_norm_g, g_kg[None], "adamw_k_norm_g")
    upd["conv_w"] = _adamw(conv_w[0], m_conv_w[0], v_conv_w[0], g_cw[None], "adamw_conv_w")
    upd["w_attn_out"] = _adamw(w_attn_out[0], m_w_attn_out[0], v_w_attn_out[0], g_wa_parts, "adamw_w_attn_out")
    upd["w_conv_out"] = _adamw(w_conv_out[0], m_w_conv_out[0], v_w_conv_out[0], g_wb_parts, "adamw_w_conv_out")
    upd["w_o"] = _adamw(w_o[0], m_w_o[0], v_w_o[0], g_wo_parts, "adamw_w_o")

    names = ["w_ada", "b_ada", "norm_g", "w_in", "b_f", "q_norm_g", "k_norm_g", "conv_w",
             "w_attn_out", "w_conv_out", "w_o"]
    lead = {"w_ada", "w_in", "conv_w", "w_attn_out", "w_conv_out", "w_o"}
    fix = lambda n, a: a[None] if n in lead else a
    loss = lax.psum(loss_part[0, 0], ("x", "y", "c"))
    outs = [loss, grad_x[None]]
    for k in range(4):
        outs += [fix(n, upd[n][k]) for n in names]
    return tuple(outs)
```

```python
import functools

import numpy as np
import jax
import jax.numpy as jnp
from jax import lax
from jax.experimental import pallas as pl
from jax.experimental.pallas import tpu as pltpu

F32 = jnp.float32
BF16 = jnp.bfloat16

D_MODEL = 1024
HEADS = 8
HEAD_DIM = 64
ATTN_W = 512
CONV_W = 512
N_DEV = 8
IN_WIDTH = 6152
IN_SHARD = IN_WIDTH // N_DEV
N_MAIN = 6144
N_FPAD = 128
N_ALL = N_MAIN + N_FPAD
ADA_SHARD = 3 * D_MODEL // N_DEV
EPS = 1e-6
NEG = -1e30

ADAM_LR = 0.001
ADAM_B1 = 0.9
ADAM_B2 = 0.999
ADAM_EPS = 1e-08
ADAM_WD = 0.01
ADAM_STEP = 10

LANES = 128
VMEM_LIMIT = 56 * 1024 * 1024

TM_PROJ = 512
TN_PROJ = 1024
TM_ELEM = 512
TQ = 512
TM_TAIL = 256
TC_CUM = 256
TK_DW = 512
TN_DW = 896
TM_DH = 512
TK_DH = 896

OFF_Q, OFF_K, OFF_V, OFF_ZA, OFF_CONV, OFF_GA, OFF_GB = 0, 512, 1024, 1536, 2048, 4096, 5120


def _params(sem=None):
    return pltpu.CompilerParams(dimension_semantics=sem, vmem_limit_bytes=VMEM_LIMIT)


def _dot(a, b):
    return jnp.dot(a, b, preferred_element_type=F32)


def _dot_nt(a, b):
    return lax.dot_general(a, b, (((1,), (1,)), ((), ())), preferred_element_type=F32)


def _dot_tn(a, b):
    return lax.dot_general(a, b, (((0,), (0,)), ((), ())), preferred_element_type=F32)


def _sigmoid(x):
    return 1.0 / (1.0 + jnp.exp(-x))


def _lane_lo(shape):
    return lax.broadcasted_iota(jnp.int32, shape, len(shape) - 1) < HEAD_DIM


def _seg_sum(z, lo):
    a = jnp.sum(jnp.where(lo, z, 0.0), axis=-1, keepdims=True)
    b = jnp.sum(jnp.where(lo, 0.0, z), axis=-1, keepdims=True)
    return jnp.where(lo, a, b)


def _lane_col(z, lane):
    idx = lax.broadcasted_iota(jnp.int32, z.shape, 1)
    return jnp.sum(jnp.where(idx == lane, z, 0.0), axis=-1, keepdims=True)


def _sub_row(z, row):
    idx = lax.broadcasted_iota(jnp.int32, z.shape, 0)
    return jnp.sum(jnp.where(idx == row, z, 0.0), axis=0, keepdims=True)


def _mesh_pos():
    x, y, c = lax.axis_index("x"), lax.axis_index("y"), lax.axis_index("c")
    return x, y, c, 4 * x + 2 * y + c


def _peer(k, x, y, c):
    px = 1 - x if (k >> 2) & 1 else x
    py = 1 - y if (k >> 1) & 1 else y
    pc = 1 - c if k & 1 else c
    return (px, py, pc), 4 * px + 2 * py + pc


def _exchange(arrs, gather, name):
    n = len(arrs)
    any_spec = pl.BlockSpec(memory_space=pl.ANY)

    def body(*refs):
        ins, outs = refs[:n], refs[n:2 * n]
        send_sems, recv_sems, local_sems = refs[2 * n:]
        x, y, c, me = _mesh_pos()
        copies = []
        for a in range(n):
            own = ins[a] if gather else ins[a].at[me]
            local = pltpu.make_async_copy(own, outs[a].at[me], local_sems.at[a])
            local.start()
            copies.append(local)
            for k in range(1, N_DEV):
                dev, p = _peer(k, x, y, c)
                cp = pltpu.make_async_remote_copy(
                    src_ref=ins[a] if gather else ins[a].at[p],
                    dst_ref=outs[a].at[me],
                    send_sem=send_sems.at[a * (N_DEV - 1) + k - 1],
                    recv_sem=recv_sems.at[a * (N_DEV - 1) + k - 1],
                    device_id=dev, device_id_type=pl.DeviceIdType.MESH)
                cp.start()
                copies.append(cp)
        for cp in copies:
            cp.wait()

    out_shape = [jax.ShapeDtypeStruct((N_DEV,) + a.shape if gather else a.shape, a.dtype) for a in arrs]
    return pl.pallas_call(
        body, name=name, out_shape=out_shape,
        in_specs=[any_spec] * n, out_specs=[any_spec] * n,
        scratch_shapes=[pltpu.SemaphoreType.DMA((n * (N_DEV - 1),)),
                        pltpu.SemaphoreType.DMA((n * (N_DEV - 1),)),
                        pltpu.SemaphoreType.DMA((n,))],
    )(*arrs)


def _ada_exchange(c_row, w_ada_sh):
    def body(c_ref, w_ref, call_ref, adag_ref, mine_ref, send_sems, recv_sems):
        x, y, c, me = _mesh_pos()

        def copy(phase, k, src, dst):
            dev, _ = _peer(k, x, y, c)
            return pltpu.make_async_remote_copy(
                src_ref=src, dst_ref=dst,
                send_sem=send_sems.at[phase * (N_DEV - 1) + k - 1],
                recv_sem=recv_sems.at[phase * (N_DEV - 1) + k - 1],
                device_id=dev, device_id_type=pl.DeviceIdType.MESH)

        call_ref[me] = c_ref[...]
        first = [copy(0, k, c_ref, call_ref.at[me]) for k in range(1, N_DEV)]
        for cp in first:
            cp.start()
        for cp in first:
            cp.wait()
        wb = w_ref[...].astype(BF16)
        for b in range(N_DEV):
            row = jnp.broadcast_to(call_ref[b], (8, D_MODEL)).astype(BF16)
            mine_ref[b] = _sub_row(_dot(row, wb), 0)
        adag_ref[me] = mine_ref[...]
        second = [copy(1, k, mine_ref, adag_ref.at[me]) for k in range(1, N_DEV)]
        for cp in second:
            cp.start()
        for cp in second:
            cp.wait()

    vm = pl.BlockSpec(memory_space=pltpu.VMEM)
    return pl.pallas_call(
        body, name="ada_exchange",
        out_shape=[jax.ShapeDtypeStruct((N_DEV, 1, D_MODEL), F32),
                   jax.ShapeDtypeStruct((N_DEV, N_DEV, 1, ADA_SHARD), F32)],
        in_specs=[vm, vm], out_specs=[vm, vm],
        scratch_shapes=[pltpu.VMEM((N_DEV, 1, ADA_SHARD), F32),
                        pltpu.SemaphoreType.DMA((2 * (N_DEV - 1),)),
                        pltpu.SemaphoreType.DMA((2 * (N_DEV - 1),))],
        compiler_params=pltpu.CompilerParams(vmem_limit_bytes=VMEM_LIMIT),
    )(c_row, w_ada_sh)


def _proj_fwd(x, ada3, norm_g, w_main, w_f):
    s = x.shape[0]
    tm, tn = min(TM_PROJ, s), TN_PROJ

    def body(x_ref, ada_ref, g_ref, w_ref, wf_ref, proj_ref, fl_ref, ht_ref, h_s):
        @pl.when(pl.program_id(1) == 0)
        def _():
            xv = x_ref[...]
            r = lax.rsqrt(jnp.mean(xv * xv, axis=-1, keepdims=True) + EPS)
            hv = ((xv * r) * g_ref[...]) * (1.0 + ada_ref[1:2, :]) + ada_ref[0:1, :]
            hb = hv.astype(BF16)
            h_s[...] = hb
            ht_ref[...] = hv.T.astype(BF16)
            fl_ref[...] = _dot(hb, wf_ref[...])
        proj_ref[...] = _dot(h_s[...], w_ref[...])

    return pl.pallas_call(
        body, name="proj_fwd", grid=(s // tm, N_MAIN // tn),
        in_specs=[pl.BlockSpec((tm, D_MODEL), lambda i, j: (i, 0)),
                  pl.BlockSpec((3, D_MODEL), lambda i, j: (0, 0)),
                  pl.BlockSpec((1, D_MODEL), lambda i, j: (0, 0)),
                  pl.BlockSpec((D_MODEL, tn), lambda i, j: (0, j)),
                  pl.BlockSpec((D_MODEL, N_FPAD), lambda i, j: (0, 0))],
        out_specs=[pl.BlockSpec((tm, tn), lambda i, j: (i, j)),
                   pl.BlockSpec((tm, N_FPAD), lambda i, j: (i, 0)),
                   pl.BlockSpec((D_MODEL, tm), lambda i, j: (0, i))],
        out_shape=[jax.ShapeDtypeStruct((s, N_MAIN), F32),
                   jax.ShapeDtypeStruct((s, N_FPAD), F32),
                   jax.ShapeDtypeStruct((D_MODEL, s), BF16)],
        scratch_shapes=[pltpu.VMEM((tm, D_MODEL), BF16)],
        compiler_params=_params(("parallel", "arbitrary")),
    )(x, ada3, norm_g, w_main, w_f)


L_ONE_Q, L_F_Q, L_LSE_Q, L_END = HEAD_DIM, HEAD_DIM + 3, HEAD_DIM + 6, HEAD_DIM + 9


def _split3(f):
    hi = f.astype(BF16).astype(F32)
    r = f - hi
    mid = r.astype(BF16).astype(F32)
    return hi, mid, r - mid


def _place3(lane, first, parts, otherwise):
    a, b, c = parts
    return jnp.where(lane == first, a, jnp.where(lane == first + 1, b, jnp.where(lane == first + 2, c, otherwise)))


def _qkv_prep(proj, cum, qg, kg):
    s = proj.shape[0]
    tm = min(TM_ELEM, s)
    scale = HEAD_DIM ** -0.5

    def body(p_ref, cum_ref, qg_ref, kg_ref, qa_ref, ka_ref, va_ref, kt_ref, vt_ref):
        lane = lax.broadcasted_iota(jnp.int32, (tm, LANES), 1)
        lo = lane < HEAD_DIM
        cum_v = cum_ref[...]
        v_tail = jnp.where(lane < L_F_Q, 1.0, 0.0)
        for pr in range(ATTN_W // LANES):
            sl = slice(pr * LANES, (pr + 1) * LANES)
            q2 = p_ref[:, OFF_Q + pr * LANES:OFF_Q + (pr + 1) * LANES]
            k2 = p_ref[:, OFF_K + pr * LANES:OFF_K + (pr + 1) * LANES]
            v2 = p_ref[:, OFF_V + pr * LANES:OFF_V + (pr + 1) * LANES]
            rq = lax.rsqrt(_seg_sum(q2 * q2, lo) * (1.0 / HEAD_DIM) + EPS)
            rk = lax.rsqrt(_seg_sum(k2 * k2, lo) * (1.0 / HEAD_DIM) + EPS)
            qn = ((q2 * rq) * qg_ref[:, sl]) * scale
            kn = (k2 * rk) * kg_ref[:, sl]
            for hh in range(2):
                h = 2 * pr + hh
                f3 = _split3(_lane_col(cum_v, h))
                qh = qn if hh == 0 else pltpu.roll(qn, HEAD_DIM, 1)
                kh = kn if hh == 0 else pltpu.roll(kn, HEAD_DIM, 1)
                vh = v2 if hh == 0 else pltpu.roll(v2, HEAD_DIM, 1)
                q_tail = jnp.where(lane < L_F_Q, 1.0, _place3(lane, L_F_Q, f3, 0.0))
                k_tail = _place3(lane, L_ONE_Q, tuple(-f for f in f3), jnp.where(lane < L_END, 1.0, 0.0))
                k_row = jnp.where(lo, kh, k_tail)
                v_row = jnp.where(lo, vh, v_tail)
                qa_ref[h] = jnp.where(lo, qh, q_tail).astype(BF16)
                ka_ref[h] = k_row.astype(BF16)
                va_ref[h] = v_row.astype(BF16)
                kt_ref[h] = k_row.T.astype(BF16)
                vt_ref[h] = v_row.T.astype(BF16)

    heads = pl.BlockSpec((HEADS, tm, LANES), lambda i: (0, i, 0))
    heads_t = pl.BlockSpec((HEADS, LANES, tm), lambda i: (0, 0, i))
    vec = pl.BlockSpec((1, ATTN_W), lambda i: (0, 0))
    return pl.pallas_call(
        body, name="qkv_prep", grid=(s // tm,),
        in_specs=[pl.BlockSpec((tm, 3 * ATTN_W), lambda i: (i, 0)),
                  pl.BlockSpec((tm, LANES), lambda i: (i, 0)), vec, vec],
        out_specs=[heads, heads, heads, heads_t, heads_t],
        out_shape=[jax.ShapeDtypeStruct((HEADS, s, LANES), BF16)] * 3
        + [jax.ShapeDtypeStruct((HEADS, LANES, s), BF16)] * 2,
        compiler_params=_params(("parallel",)),
    )(proj, cum, qg, kg)


def _log_forget(fl, bf):
    z = fl + bf
    lf = jnp.minimum(z, 0.0) - jnp.log1p(jnp.exp(-jnp.abs(z)))
    lane = lax.broadcasted_iota(jnp.int32, z.shape, 1)
    return jnp.where(lane < HEADS, lf, 0.0)


def _forget_cumsum(fl, bf_pad):
    s = fl.shape[0]
    tc = min(TC_CUM, s)

    def body(fl_ref, bf_ref, cum_ref, carry):
        @pl.when(pl.program_id(0) == 0)
        def _():
            carry[...] = jnp.zeros_like(carry)
        lf = _log_forget(fl_ref[...], bf_ref[...])
        r = lax.broadcasted_iota(jnp.int32, (tc, tc), 0)
        cidx = lax.broadcasted_iota(jnp.int32, (tc, tc), 1)
        tri = (cidx <= r).astype(F32)
        cs = jnp.dot(tri, lf, preferred_element_type=F32, precision=lax.Precision.HIGHEST) + carry[...]
        cum_ref[...] = cs
        carry[...] = cum_ref[tc - 1:tc, :]

    return pl.pallas_call(
        body, name="forget_cumsum", grid=(s // tc,),
        in_specs=[pl.BlockSpec((tc, LANES), lambda i: (i, 0)),
                  pl.BlockSpec((1, LANES), lambda i: (0, 0))],
        out_specs=pl.BlockSpec((tc, LANES), lambda i: (i, 0)),
        out_shape=jax.ShapeDtypeStruct((s, LANES), F32),
        scratch_shapes=[pltpu.VMEM((1, LANES), F32)],
        compiler_params=_params(("arbitrary",)),
    )(fl, bf_pad)


def _causal_t(t):
    return lax.broadcasted_iota(jnp.int32, (t, t), 0) <= lax.broadcasted_iota(jnp.int32, (t, t), 1)


def _tri_steps(nt, q_major):
    if q_major:
        pairs = [(i, j) for i in range(nt) for j in range(i + 1)]
    else:
        pairs = [(i, j) for j in range(nt) for i in range(j, nt)]
    return (jnp.asarray(np.array([p[0] for p in pairs], np.int32)),
            jnp.asarray(np.array([p[1] for p in pairs], np.int32)))


def _attn_fwd(qa, ka, vt, proj):
    s = qa.shape[1]
    t = min(TQ, s)
    it, jt = _tri_steps(s // t, True)
    za_blk = OFF_ZA // LANES

    def body(it_ref, jt_ref, q_ref, k_ref, vt_ref, za_ref, attn_ref, oa_ref, qb_ref, m_s, acc_s, pair_s):
        step = pl.program_id(1)
        i, j = it_ref[step], jt_ref[step]

        @pl.when(j == 0)
        def _():
            m_s[...] = jnp.full_like(m_s, NEG)
            acc_s[...] = jnp.zeros_like(acc_s)

        def update(masked):
            for hh in range(2):
                st = _dot_nt(k_ref[hh], q_ref[hh])
                if masked:
                    st = jnp.where(_causal_t(t), st, NEG)
                m_prev = m_s[hh]
                m_next = jnp.maximum(m_prev, jnp.max(st, axis=0, keepdims=True))
                alpha = jnp.exp(m_prev - m_next)
                pt = jnp.exp(st - m_next).astype(BF16)
                acc_s[hh] = acc_s[hh] * alpha + _dot(vt_ref[hh], pt)
                m_s[hh] = m_next

        @pl.when(j < i)
        def _():
            update(False)

        @pl.when(j == i)
        def _():
            update(True)
            row = lax.broadcasted_iota(jnp.int32, (LANES, t), 0)
            lane = lax.broadcasted_iota(jnp.int32, (t, LANES), 1)
            for hh in range(2):
                l_row = acc_s[hh, L_ONE_Q:L_ONE_Q + 1, :]
                pair_s[hh * HEAD_DIM:(hh + 1) * HEAD_DIM, :] = acc_s[hh, 0:HEAD_DIM, :] / l_row
                lse3 = _split3(m_s[hh] + jnp.log(l_row))
                tail_t = _place3(row, L_LSE_Q, tuple(-x for x in lse3), 0.0)
                keep_q = jnp.logical_or(lane < L_LSE_Q, lane >= L_END)
                qb_ref[hh] = jnp.where(keep_q, q_ref[hh].astype(F32), tail_t.T).astype(BF16)
            out = pair_s[...].T
            attn_ref[...] = out
            z = za_ref[...]
            oa_ref[...] = (out * (z * _sigmoid(z))).astype(BF16)

    pair_q = pl.BlockSpec((2, t, LANES), lambda p, n, it_, jt_: (p, it_[n], 0))
    pair_k = pl.BlockSpec((2, t, LANES), lambda p, n, it_, jt_: (p, jt_[n], 0))
    pair_kt = pl.BlockSpec((2, LANES, t), lambda p, n, it_, jt_: (p, 0, jt_[n]))
    out_q = pl.BlockSpec((t, LANES), lambda p, n, it_, jt_: (it_[n], p))
    return pl.pallas_call(
        body, name="attn_fwd",
        grid_spec=pltpu.PrefetchScalarGridSpec(
            num_scalar_prefetch=2, grid=(ATTN_W // LANES, it.shape[0]),
            in_specs=[pair_q, pair_k, pair_kt,
                      pl.BlockSpec((t, LANES), lambda p, n, it_, jt_: (it_[n], za_blk + p))],
            out_specs=[out_q, out_q, pair_q],
            scratch_shapes=[pltpu.VMEM((2, 1, t), F32), pltpu.VMEM((2, LANES, t), F32),
                            pltpu.VMEM((LANES, t), F32)]),
        out_shape=[jax.ShapeDtypeStruct((s, ATTN_W), F32),
                   jax.ShapeDtypeStruct((s, ATTN_W), BF16),
                   jax.ShapeDtypeStruct((HEADS, s, LANES), BF16)],
        compiler_params=_params(("parallel", "arbitrary")),
    )(it, jt, qa, ka, vt, proj)


def _conv_parts(blk, halo, first, w_ref, tm):
    gb, gc = blk[:, 0:LANES], blk[:, LANES:2 * LANES]
    u, zb = blk[:, 2 * LANES:3 * LANES], blk[:, 3 * LANES:4 * LANES]
    cu = gc * u
    cu_h = jnp.where(first, 0.0, halo[:, LANES:2 * LANES] * halo[:, 2 * LANES:3 * LANES])
    prev1, prev2 = _sub_row(cu_h, 7), _sub_row(cu_h, 6)
    row = lax.broadcasted_iota(jnp.int32, (tm, LANES), 0)
    r1 = jnp.where(row == 0, prev1, pltpu.roll(cu, 1, 0))
    r2 = jnp.where(row == 0, prev2, jnp.where(row == 1, prev1, pltpu.roll(cu, 2, 0)))
    conv = w_ref[2:3, :] * cu + w_ref[1:2, :] * r1 + w_ref[0:1, :] * r2
    return gb, gc, u, zb, cu, r1, r2, conv


def _conv_fwd(proj, conv_w):
    s = proj.shape[0]
    tm = min(TM_ELEM, s)
    cb = OFF_CONV // (4 * LANES)

    def body(p_ref, halo_ref, w_ref, ob_ref):
        first = pl.program_id(1) == 0
        gb, _, _, zb, _, _, _, conv = _conv_parts(p_ref[...], halo_ref[...], first, w_ref, tm)
        ob_ref[...] = (gb * conv * (zb * _sigmoid(zb))).astype(BF16)

    return pl.pallas_call(
        body, name="conv_fwd", grid=(CONV_W // LANES, s // tm),
        in_specs=[pl.BlockSpec((tm, 4 * LANES), lambda c, i: (i, cb + c)),
                  pl.BlockSpec((8, 4 * LANES), lambda c, i: (jnp.maximum(i * (tm // 8) - 1, 0), cb + c)),
                  pl.BlockSpec((3, LANES), lambda c, i: (0, c))],
        out_specs=pl.BlockSpec((tm, LANES), lambda c, i: (i, c)),
        out_shape=jax.ShapeDtypeStruct((s, CONV_W), BF16),
        compiler_params=_params(("parallel", "parallel")),
    )(proj, proj, conv_w)


def _tail(oa, ob, proj, x, target, ada3, wa, wb, wo):
    s = x.shape[0]
    tm = min(TM_TAIL, s)
    gab_blk = OFF_GA // (2 * D_MODEL)

    def body(oa_ref, ob_ref, gab_ref, x_ref, t_ref, ada_ref, wa_ref, wb_ref, wo_ref,
             dy_ref, dgab_ref, doa_ref, dob_ref, dwo_ref, dwa_ref, dwb_ref, dgate_ref, loss_ref):
        @pl.when(pl.program_id(0) == 0)
        def _():
            dwo_ref[...] = jnp.zeros_like(dwo_ref)
            dwa_ref[...] = jnp.zeros_like(dwa_ref)
            dwb_ref[...] = jnp.zeros_like(dwb_ref)
            dgate_ref[...] = jnp.zeros_like(dgate_ref)
            loss_ref[...] = jnp.zeros_like(loss_ref)

        oa_v, ob_v = oa_ref[...], ob_ref[...]
        wa_v, wb_v, wo_v = wa_ref[...], wb_ref[...], wo_ref[...]
        a2 = _dot(oa_v, wa_v)
        b2 = _dot(ob_v, wb_v)
        sa = _sigmoid(gab_ref[:, 0:D_MODEL])
        sb = _sigmoid(gab_ref[:, D_MODEL:2 * D_MODEL])
        mb = (sa * a2 + sb * b2).astype(BF16)
        mo = _dot(mb, wo_v)
        gate = ada_ref[2:3, :]
        err = (x_ref[...] + gate * mo) - t_ref[...]
        dy = err * (1.0 / D_MODEL)
        dy_ref[...] = dy
        loss_ref[...] += 0.5 * jnp.sum(err * err) * (1.0 / D_MODEL)
        dgate_ref[...] += jnp.sum(dy * mo, axis=0, keepdims=True)
        dmo = (dy * gate).astype(BF16)
        dmerged = _dot_nt(dmo, wo_v)
        dwo_ref[...] += _dot_tn(mb, dmo)
        da2 = (dmerged * sa).astype(BF16)
        db2 = (dmerged * sb).astype(BF16)
        dgab_ref[:, 0:D_MODEL] = (dmerged * a2 * (sa * (1.0 - sa))).astype(BF16)
        dgab_ref[:, D_MODEL:2 * D_MODEL] = (dmerged * b2 * (sb * (1.0 - sb))).astype(BF16)
        doa_ref[...] = _dot_nt(da2, wa_v)
        dob_ref[...] = _dot_nt(db2, wb_v)
        dwa_ref[...] += _dot_tn(oa_v, da2)
        dwb_ref[...] += _dot_tn(ob_v, db2)

    half = pl.BlockSpec((tm, ATTN_W), lambda i: (i, 0))
    full = pl.BlockSpec((tm, D_MODEL), lambda i: (i, 0))

    def const(shape):
        return pl.BlockSpec(shape, lambda i: (0, 0))

    return pl.pallas_call(
        body, name="tail", grid=(s // tm,),
        in_specs=[half, half, pl.BlockSpec((tm, 2 * D_MODEL), lambda i: (i, gab_blk)), full, full,
                  const((3, D_MODEL)), const((ATTN_W, D_MODEL)), const((CONV_W, D_MODEL)),
                  const((D_MODEL, D_MODEL))],
        out_specs=[full, pl.BlockSpec((tm, 2 * D_MODEL), lambda i: (i, 0)), half, half,
                   const((D_MODEL, D_MODEL)), const((ATTN_W, D_MODEL)), const((CONV_W, D_MODEL)),
                   const((1, D_MODEL)), const((1, LANES))],
        out_shape=[jax.ShapeDtypeStruct((s, D_MODEL), F32),
                   jax.ShapeDtypeStruct((s, 2 * D_MODEL), BF16),
                   jax.ShapeDtypeStruct((s, ATTN_W), F32),
                   jax.ShapeDtypeStruct((s, CONV_W), F32),
                   jax.ShapeDtypeStruct((D_MODEL, D_MODEL), F32),
                   jax.ShapeDtypeStruct((ATTN_W, D_MODEL), F32),
                   jax.ShapeDtypeStruct((CONV_W, D_MODEL), F32),
                   jax.ShapeDtypeStruct((1, D_MODEL), F32),
                   jax.ShapeDtypeStruct((1, LANES), F32)],
        compiler_params=_params(("arbitrary",)),
    )(oa, ob, proj, x, target, ada3, wa, wb, wo)


def _attn_bwd_prep(doa, attn, proj):
    s = doa.shape[0]
    tm = min(TM_ELEM, s)
    za_blk = OFF_ZA // ATTN_W

    def body(doa_ref, attn_ref, za_ref, do_ref, dza_ref):
        lane = lax.broadcasted_iota(jnp.int32, (tm, LANES), 1)
        lo = lane < HEAD_DIM
        for pr in range(ATTN_W // LANES):
            sl = slice(pr * LANES, (pr + 1) * LANES)
            g, a, z = doa_ref[:, sl], attn_ref[:, sl], za_ref[:, sl]
            sg = _sigmoid(z)
            dat = (g * (z * sg)).astype(BF16).astype(F32)
            prod = dat * a
            dza_ref[:, sl] = (g * a * (sg * (1.0 + z * (1.0 - sg)))).astype(BF16)
            for hh in range(2):
                sel = lo if hh == 0 else jnp.logical_not(lo)
                delta3 = _split3(jnp.sum(jnp.where(sel, prod, 0.0), axis=-1, keepdims=True))
                dh = dat if hh == 0 else pltpu.roll(dat, HEAD_DIM, 1)
                tail = _place3(lane, L_ONE_Q, tuple(-d for d in delta3), 0.0)
                do_ref[2 * pr + hh] = jnp.where(lo, dh, tail).astype(BF16)

    row = pl.BlockSpec((tm, ATTN_W), lambda i: (i, 0))
    return pl.pallas_call(
        body, name="attn_bwd_prep", grid=(s // tm,),
        in_specs=[row, row, pl.BlockSpec((tm, ATTN_W), lambda i: (i, za_blk))],
        out_specs=[pl.BlockSpec((HEADS, tm, LANES), lambda i: (0, i, 0)), row],
        out_shape=[jax.ShapeDtypeStruct((HEADS, s, LANES), BF16),
                   jax.ShapeDtypeStruct((s, ATTN_W), BF16)],
        compiler_params=_params(("parallel",)),
    )(doa, attn, proj)


def _attn_bwd(qb, ka, kt, va, do):
    s = qb.shape[1]
    t = min(TQ, s)
    nt = s // t
    it, jt = _tri_steps(nt, False)

    def body(it_ref, jt_ref, q_ref, k_ref, kt_ref, v_ref, do_ref, dqt_ref, dk_ref, dv_ref):
        step = pl.program_id(1)
        i, j = it_ref[step], jt_ref[step]

        @pl.when(step == 0)
        def _():
            dqt_ref[...] = jnp.zeros_like(dqt_ref)

        @pl.when(i == j)
        def _():
            dk_ref[...] = jnp.zeros_like(dk_ref)
            dv_ref[...] = jnp.zeros_like(dv_ref)

        def update(masked):
            for hh in range(2):
                qh, doh = q_ref[hh], do_ref[hh]
                st = _dot_nt(k_ref[hh], qh)
                if masked:
                    st = jnp.where(_causal_t(t), st, NEG)
                pt = jnp.exp(st)
                dst = (pt * _dot_nt(v_ref[hh], doh)).astype(BF16)
                dv_ref[hh] += _dot(pt.astype(BF16), doh)
                dk_ref[hh] += _dot(dst, qh)
                dqt_ref[hh, i] += _dot(kt_ref[hh], dst)

        @pl.when(i > j)
        def _():
            update(False)

        @pl.when(i == j)
        def _():
            update(True)

    pair_q = pl.BlockSpec((2, t, LANES), lambda p, n, it_, jt_: (p, it_[n], 0))
    pair_k = pl.BlockSpec((2, t, LANES), lambda p, n, it_, jt_: (p, jt_[n], 0))
    pair_kt = pl.BlockSpec((2, LANES, t), lambda p, n, it_, jt_: (p, 0, jt_[n]))
    return pl.pallas_call(
        body, name="attn_bwd",
        grid_spec=pltpu.PrefetchScalarGridSpec(
            num_scalar_prefetch=2, grid=(ATTN_W // LANES, it.shape[0]),
            in_specs=[pair_q, pair_k, pair_kt, pair_k, pair_q],
            out_specs=[pl.BlockSpec((2, nt, LANES, t), lambda p, n, it_, jt_: (p, 0, 0, 0)),
                       pair_k, pair_k]),
        out_shape=[jax.ShapeDtypeStruct((HEADS, nt, LANES, t), F32),
                   jax.ShapeDtypeStruct((HEADS, s, LANES), F32),
                   jax.ShapeDtypeStruct((HEADS, s, LANES), F32)],
        compiler_params=_params(("parallel", "arbitrary")),
    )(it, jt, qb, ka, kt, va, do)


def _forget_bwd(dcum, fl, bf_pad):
    s = fl.shape[0]
    tc = min(TC_CUM, s)
    n = s // tc

    def body(dc_ref, fl_ref, bf_ref, df_ref, dbf_ref, carry):
        @pl.when(pl.program_id(0) == 0)
        def _():
            carry[...] = jnp.zeros_like(carry)
            dbf_ref[...] = jnp.zeros_like(dbf_ref)
        r = lax.broadcasted_iota(jnp.int32, (tc, tc), 0)
        cidx = lax.broadcasted_iota(jnp.int32, (tc, tc), 1)
        tri = (cidx >= r).astype(F32)
        dc = dc_ref[...]
        dlf = jnp.dot(tri, dc, preferred_element_type=F32, precision=lax.Precision.HIGHEST) + carry[...]
        carry[...] += jnp.sum(dc, axis=0, keepdims=True)
        lane = lax.broadcasted_iota(jnp.int32, (tc, LANES), 1)
        dfl = jnp.where(lane < HEADS, dlf * _sigmoid(-(fl_ref[...] + bf_ref[...])), 0.0)
        df_ref[...] = dfl.astype(BF16)
        dbf_ref[...] += jnp.sum(dfl, axis=0, keepdims=True)

    rev = pl.BlockSpec((tc, LANES), lambda i: (n - 1 - i, 0))
    vec = pl.BlockSpec((1, LANES), lambda i: (0, 0))
    return pl.pallas_call(
        body, name="forget_bwd", grid=(n,),
        in_specs=[rev, rev, vec], out_specs=[rev, vec],
        out_shape=[jax.ShapeDtypeStruct((s, LANES), BF16), jax.ShapeDtypeStruct((1, LANES), F32)],
        scratch_shapes=[pltpu.VMEM((1, LANES), F32)],
        compiler_params=_params(("arbitrary",)),
    )(dcum, fl, bf_pad)


def _qk_norm_bwd(dqt, dk, dv, proj, qg, kg):
    s = dv.shape[1]
    tm = dqt.shape[-1]
    scale = HEAD_DIM ** -0.5

    def body(dqt_ref, dk_ref, dv_ref, p_ref, qg_ref, kg_ref, out_ref, dqg_ref, dkg_ref, dcum_ref):
        @pl.when(pl.program_id(0) == 0)
        def _():
            dqg_ref[...] = jnp.zeros_like(dqg_ref)
            dkg_ref[...] = jnp.zeros_like(dkg_ref)
        lane = lax.broadcasted_iota(jnp.int32, (tm, LANES), 1)
        lo = lane < HEAD_DIM
        dq_rows = [dqt_ref[h, 0].T for h in range(HEADS)]
        dcum = jnp.zeros((tm, LANES), F32)
        for h in range(HEADS):
            dcum = jnp.where(lane == h, _lane_col(dq_rows[h], L_F_Q) - _lane_col(dk_ref[h], L_ONE_Q), dcum)
        dcum_ref[...] = dcum

        def pair(a, b):
            return jnp.where(lo, a, pltpu.roll(b, HEAD_DIM, 1))

        def one(raw, dy, g, dg_ref, sl, off):
            r = lax.rsqrt(_seg_sum(raw * raw, lo) * (1.0 / HEAD_DIM) + EPS)
            xhat = raw * r
            dg_ref[:, sl] += jnp.sum(dy * xhat, axis=0, keepdims=True)
            dxh = dy * g
            dx = r * (dxh - xhat * (_seg_sum(dxh * xhat, lo) * (1.0 / HEAD_DIM)))
            out_ref[:, off + sl.start:off + sl.stop] = dx.astype(BF16)

        for pr in range(ATTN_W // LANES):
            sl = slice(pr * LANES, (pr + 1) * LANES)
            dq2 = pair(dq_rows[2 * pr], dq_rows[2 * pr + 1])
            one(p_ref[:, OFF_Q + sl.start:OFF_Q + sl.stop], dq2 * scale, qg_ref[:, sl], dqg_ref, sl, OFF_Q)
            one(p_ref[:, OFF_K + sl.start:OFF_K + sl.stop], pair(dk_ref[2 * pr], dk_ref[2 * pr + 1]),
                kg_ref[:, sl], dkg_ref, sl, OFF_K)
            out_ref[:, OFF_V + sl.start:OFF_V + sl.stop] = pair(dv_ref[2 * pr], dv_ref[2 * pr + 1]).astype(BF16)

    heads = pl.BlockSpec((HEADS, tm, LANES), lambda i: (0, i, 0))
    vec = pl.BlockSpec((1, ATTN_W), lambda i: (0, 0))
    return pl.pallas_call(
        body, name="qk_norm_bwd", grid=(s // tm,),
        in_specs=[pl.BlockSpec((HEADS, 1, LANES, tm), lambda i: (0, i, 0, 0)), heads, heads,
                  pl.BlockSpec((tm, 2 * ATTN_W), lambda i: (i, 0)), vec, vec],
        out_specs=[pl.BlockSpec((tm, 3 * ATTN_W), lambda i: (i, 0)), vec, vec,
                   pl.BlockSpec((tm, LANES), lambda i: (i, 0))],
        out_shape=[jax.ShapeDtypeStruct((s, 3 * ATTN_W), BF16),
                   jax.ShapeDtypeStruct((1, ATTN_W), F32), jax.ShapeDtypeStruct((1, ATTN_W), F32),
                   jax.ShapeDtypeStruct((s, LANES), F32)],
        compiler_params=_params(("arbitrary",)),
    )(dqt, dk, dv, proj, qg, kg)


def _conv_bwd(dob, proj, conv_w):
    s = dob.shape[0]
    tm = min(TM_ELEM, s)
    cb = OFF_CONV // (4 * LANES)
    nblk8 = s // 8

    def body(dob_ref, p_ref, prev_ref, next_ref, dnext_ref, w_ref, out_ref, dw_ref):
        i = pl.program_id(1)

        @pl.when(i == 0)
        def _():
            dw_ref[...] = jnp.zeros_like(dw_ref)
        gb, gc, u, zb, cu, r1, r2, conv = _conv_parts(p_ref[...], prev_ref[...], i == 0, w_ref, tm)
        g = dob_ref[...]
        sg = _sigmoid(zb)
        sz = zb * sg
        dconv = g * gb * sz
        nxt = next_ref[...]
        zn = nxt[:, 3 * LANES:4 * LANES]
        dcn = jnp.where(i == pl.num_programs(1) - 1, 0.0,
                        dnext_ref[...] * nxt[:, 0:LANES] * (zn * _sigmoid(zn)))
        nxt1, nxt2 = _sub_row(dcn, 0), _sub_row(dcn, 1)
        row = lax.broadcasted_iota(jnp.int32, (tm, LANES), 0)
        f1 = jnp.where(row == tm - 1, nxt1, pltpu.roll(dconv, tm - 1, 0))
        f2 = jnp.where(row == tm - 2, nxt1, jnp.where(row == tm - 1, nxt2, pltpu.roll(dconv, tm - 2, 0)))
        dcu = w_ref[2:3, :] * dconv + w_ref[1:2, :] * f1 + w_ref[0:1, :] * f2
        out_ref[:, 0:LANES] = (g * conv * sz).astype(BF16)
        out_ref[:, LANES:2 * LANES] = (dcu * u).astype(BF16)
        out_ref[:, 2 * LANES:3 * LANES] = (dcu * gc).astype(BF16)
        out_ref[:, 3 * LANES:4 * LANES] = (g * gb * conv * (sg * (1.0 + zb * (1.0 - sg)))).astype(BF16)
        w_row = lax.broadcasted_iota(jnp.int32, (3, LANES), 0)
        dw0 = jnp.sum(dconv * r2, axis=0, keepdims=True)
        dw1 = jnp.sum(dconv * r1, axis=0, keepdims=True)
        dw2 = jnp.sum(dconv * cu, axis=0, keepdims=True)
        dw_ref[...] += jnp.where(w_row == 0, dw0, jnp.where(w_row == 1, dw1, dw2))

    nxt_idx = lambda i: jnp.minimum((i + 1) * (tm // 8), nblk8 - 1)
    return pl.pallas_call(
        body, name="conv_bwd", grid=(CONV_W // LANES, s // tm),
        in_specs=[pl.BlockSpec((tm, LANES), lambda c, i: (i, c)),
                  pl.BlockSpec((tm, 4 * LANES), lambda c, i: (i, cb + c)),
                  pl.BlockSpec((8, 4 * LANES), lambda c, i: (jnp.maximum(i * (tm // 8) - 1, 0), cb + c)),
                  pl.BlockSpec((8, 4 * LANES), lambda c, i: (nxt_idx(i), cb + c)),
                  pl.BlockSpec((8, LANES), lambda c, i: (nxt_idx(i), c)),
                  pl.BlockSpec((3, LANES), lambda c, i: (0, c))],
        out_specs=[pl.BlockSpec((tm, 4 * LANES), lambda c, i: (i, c)),
                   pl.BlockSpec((3, LANES), lambda c, i: (0, c))],
        out_shape=[jax.ShapeDtypeStruct((s, 4 * CONV_W), BF16), jax.ShapeDtypeStruct((3, CONV_W), F32)],
        compiler_params=_params(("parallel", "arbitrary")),
    )(dob, proj, proj, proj, dob, conv_w)


def _dw_in(ht, dproj):
    s = ht.shape[1]
    tk, tn = min(TK_DW, s), TN_DW

    def body(ht_ref, dp_ref, out_ref):
        @pl.when(pl.program_id(1) == 0)
        def _():
            out_ref[...] = jnp.zeros_like(out_ref)
        out_ref[...] += _dot(ht_ref[...], dp_ref[...])

    return pl.pallas_call(
        body, name="dw_in", grid=(N_ALL // tn, s // tk),
        in_specs=[pl.BlockSpec((D_MODEL, tk), lambda n, k: (0, k)),
                  pl.BlockSpec((tk, tn), lambda n, k: (k, n))],
        out_specs=pl.BlockSpec((D_MODEL, tn), lambda n, k: (0, n)),
        out_shape=jax.ShapeDtypeStruct((D_MODEL, N_ALL), F32),
        compiler_params=_params(("parallel", "arbitrary")),
    )(ht, dproj)


def _dh_and_dx(dproj, w_all_t, x, dy, ada3, norm_g):
    s = x.shape[0]
    tm, tk = min(TM_DH, s), TK_DH
    nk = N_ALL // tk

    def body(dp_ref, wt_ref, x_ref, dy_ref, ada_ref, g_ref, gx_ref, dsh_ref, dsc_ref, dg_ref, acc):
        i, k = pl.program_id(0), pl.program_id(1)

        @pl.when(jnp.logical_and(i == 0, k == 0))
        def _():
            dsh_ref[...] = jnp.zeros_like(dsh_ref)
            dsc_ref[...] = jnp.zeros_like(dsc_ref)
            dg_ref[...] = jnp.zeros_like(dg_ref)

        @pl.when(k == 0)
        def _():
            acc[...] = jnp.zeros_like(acc)
        acc[...] += _dot(dp_ref[...], wt_ref[...])

        @pl.when(k == nk - 1)
        def _():
            dh = acc[...]
            xv = x_ref[...]
            r = lax.rsqrt(jnp.mean(xv * xv, axis=-1, keepdims=True) + EPS)
            xhat = xv * r
            g = g_ref[...]
            one_sc = 1.0 + ada_ref[1:2, :]
            dsh_ref[...] += jnp.sum(dh, axis=0, keepdims=True)
            dsc_ref[...] += jnp.sum(dh * (xhat * g), axis=0, keepdims=True)
            dg_ref[...] += jnp.sum(dh * xhat, axis=0, keepdims=True) * one_sc
            dxh = dh * (g * one_sc)
            dx = r * (dxh - xhat * jnp.mean(dxh * xhat, axis=-1, keepdims=True))
            gx_ref[...] = dy_ref[...] + dx

    full = pl.BlockSpec((tm, D_MODEL), lambda i, k: (i, 0))
    vec = pl.BlockSpec((1, D_MODEL), lambda i, k: (0, 0))
    return pl.pallas_call(
        body, name="dh_dx", grid=(s // tm, nk),
        in_specs=[pl.BlockSpec((tm, tk), lambda i, k: (i, k)),
                  pl.BlockSpec((tk, D_MODEL), lambda i, k: (k, 0)),
                  full, full, pl.BlockSpec((3, D_MODEL), lambda i, k: (0, 0)), vec],
        out_specs=[full, vec, vec, vec],
        out_shape=[jax.ShapeDtypeStruct((s, D_MODEL), F32)] + [jax.ShapeDtypeStruct((1, D_MODEL), F32)] * 3,
        scratch_shapes=[pltpu.VMEM((tm, D_MODEL), F32)],
        compiler_params=_params(("arbitrary", "arbitrary")),
    )(dproj, w_all_t, x, dy, ada3, norm_g)


def _sum_small(vec_all, qg_parts, kg_parts):
    def body(v_ref, q_ref, k_ref, tot_ref, gq_ref, gk_ref):
        tot = v_ref[0:1, :]
        for p in range(1, N_DEV):
            tot = tot + v_ref[p:p + 1, :]
        tot_ref[...] = tot
        gq_ref[...] = jnp.sum(q_ref[...], axis=0, keepdims=True)
        gk_ref[...] = jnp.sum(k_ref[...], axis=0, keepdims=True)

    n = vec_all.shape[-1]
    return pl.pallas_call(
        body, name="sum_small",
        out_shape=[jax.ShapeDtypeStruct((1, n), F32),
                   jax.ShapeDtypeStruct((1, HEAD_DIM), F32), jax.ShapeDtypeStruct((1, HEAD_DIM), F32)],
        compiler_params=_params(),
    )(vec_all, qg_parts, kg_parts)


def _grad_w_ada(c_cols, dada_rows):
    def body(c_ref, d_ref, out_ref):
        acc = c_ref[0] * d_ref[0]
        for b in range(1, N_DEV):
            acc = acc + c_ref[b] * d_ref[b]
        out_ref[...] = acc

    return pl.pallas_call(
        body, name="grad_w_ada",
        out_shape=jax.ShapeDtypeStruct((D_MODEL, ADA_SHARD), F32),
        compiler_params=_params(),
    )(c_cols, dada_rows)


def _adamw(w, m, v, g_parts, name):
    rows, cols = w.shape
    n_parts = g_parts.shape[0]
    tr = 256 if rows % 256 == 0 else rows
    c1 = 1.0 / (1.0 - ADAM_B1 ** ADAM_STEP)
    c2 = 1.0 / (1.0 - ADAM_B2 ** ADAM_STEP)

    def body(w_ref, m_ref, v_ref, g_ref, go_ref, d_ref, mo_ref, vo_ref):
        g = g_ref[0].astype(F32)
        for p in range(1, n_parts):
            g = g + g_ref[p].astype(F32)
        m_new = ADAM_B1 * m_ref[...] + (1.0 - ADAM_B1) * g
        v_new = ADAM_B2 * v_ref[...] + (1.0 - ADAM_B2) * (g * g)
        go_ref[...] = g
        mo_ref[...] = m_new
        vo_ref[...] = v_new
        d_ref[...] = -ADAM_LR * ((m_new * c1) / (jnp.sqrt(v_new * c2) + ADAM_EPS) + ADAM_WD * w_ref[...])

    blk = pl.BlockSpec((tr, cols), lambda i: (i, 0))
    return pl.pallas_call(
        body, name=name, grid=(rows // tr,),
        in_specs=[blk, blk, blk, pl.BlockSpec((n_parts, tr, cols), lambda i: (0, i, 0))],
        out_specs=[blk] * 4,
        out_shape=[jax.ShapeDtypeStruct((rows, cols), F32)] * 4,
        compiler_params=_params(("parallel",)),
    )(w, m, v, g_parts)


_O_Q, _O_K, _O_V, _O_F, _O_ZA, _O_GB, _O_GC, _O_U, _O_ZB, _O_GA, _O_GB2 = (
    0, 512, 1024, 1536, 1544, 2056, 2568, 3080, 3592, 4104, 5128)


def _to_internal(w_in_g):
    wf = jnp.transpose(w_in_g, (1, 0, 2)).reshape(D_MODEL, IN_WIDTH)
    cols = lambda a, n: wf[:, a:a + n]
    conv = [cols(base + LANES * c, LANES) for c in range(4) for base in (_O_GB, _O_GC, _O_U, _O_ZB)]
    main = jnp.concatenate([cols(_O_Q, 512), cols(_O_K, 512), cols(_O_V, 512), cols(_O_ZA, 512), *conv,
                            cols(_O_GA, 1024), cols(_O_GB2, 1024)], axis=1)
    f = jnp.pad(cols(_O_F, HEADS), ((0, 0), (0, N_FPAD - HEADS)))
    return main, f


def _from_internal(dw):
    cols = lambda a, n: dw[:, a:a + n]
    conv = lambda k: [cols(OFF_CONV + 4 * LANES * c + LANES * k, LANES) for c in range(4)]
    full = jnp.concatenate([cols(OFF_Q, 512), cols(OFF_K, 512), cols(OFF_V, 512), cols(N_MAIN, HEADS),
                            cols(OFF_ZA, 512), *conv(0), *conv(1), *conv(2), *conv(3),
                            cols(OFF_GA, 1024), cols(OFF_GB, 1024)], axis=1)
    return jnp.transpose(full.reshape(D_MODEL, N_DEV, IN_SHARD), (1, 0, 2))


def kernel(x, c, w_ada, b_ada, norm_g, w_in, b_f, q_norm_g, k_norm_g, conv_w, w_attn_out, w_conv_out, w_o, loss_target, m_w_ada, m_b_ada, m_norm_g, m_w_in, m_b_f, m_q_norm_g, m_k_norm_g, m_conv_w, m_w_attn_out, m_w_conv_out, m_w_o, v_w_ada, v_b_ada, v_norm_g, v_w_in, v_b_f, v_q_norm_g, v_k_norm_g, v_conv_w, v_w_attn_out, v_w_conv_out, v_w_o):
    me = 4 * lax.axis_index("x") + 2 * lax.axis_index("y") + lax.axis_index("c")
    s = x.shape[1]
    x2, t2 = x[0], loss_target[0]

    w_in_g, wa_g, wb_g, wo_g, cw_g = _exchange(
        [w_in[0].astype(BF16), w_attn_out[0].astype(BF16), w_conv_out[0].astype(BF16),
         w_o[0].astype(BF16), conv_w[0]], True, "gather_weights")
    c_all, ada_g = _ada_exchange(c, w_ada[0])
    ada_mine = lax.dynamic_index_in_dim(ada_g[:, :, 0, :], me, axis=1, keepdims=False)
    ada3 = (ada_mine.reshape(1, 3 * D_MODEL) + b_ada).reshape(3, D_MODEL)
    w_main, w_f = _to_internal(w_in_g)
    w_all_t = jnp.concatenate([w_main, w_f], axis=1).T
    wa = jnp.transpose(wa_g, (1, 0, 2)).reshape(ATTN_W, D_MODEL)
    wb = jnp.transpose(wb_g, (1, 0, 2)).reshape(CONV_W, D_MODEL)
    wo = wo_g.reshape(D_MODEL, D_MODEL)
    cw = jnp.transpose(cw_g, (1, 0, 2)).reshape(3, CONV_W)
    qg = jnp.tile(q_norm_g, (1, HEADS))
    kg = jnp.tile(k_norm_g, (1, HEADS))
    bf_pad = jnp.pad(b_f, ((0, 0), (0, LANES - HEADS)))

    proj, fl, ht = _proj_fwd(x2, ada3, norm_g, w_main, w_f)
    cum = _forget_cumsum(fl, bf_pad)
    qa, ka, va, kt, vt = _qkv_prep(proj, cum, qg, kg)
    attn, oa, qb = _attn_fwd(qa, ka, vt, proj)
    ob = _conv_fwd(proj, cw)
    (dy, dgab, doa, dob, dwo, dwa, dwb, dgate, loss_part) = _tail(oa, ob, proj, x2, t2, ada3, wa, wb, wo)

    do, dza = _attn_bwd_prep(doa, attn, proj)
    dqt, dk, dv = _attn_bwd(qb, ka, kt, va, do)
    dqkv, dqg, dkg, dcum = _qk_norm_bwd(dqt, dk, dv, proj, qg, kg)
    df, dbf = _forget_bwd(dcum, fl, bf_pad)
    dconv4, dcw = _conv_bwd(dob, proj, cw)
    dproj = jnp.concatenate([dqkv, dza, dconv4, dgab, df], axis=1)
    dw_all = _dw_in(ht, dproj)
    grad_x, dshift, dscale, dnormg = _dh_and_dx(dproj, w_all_t, x2, dy, ada3, norm_g)

    slabs = [_from_internal(dw_all).astype(BF16),
             jnp.transpose(dwa.reshape(ATTN_W, N_DEV, LANES), (1, 0, 2)).astype(BF16),
             jnp.transpose(dwb.reshape(CONV_W, N_DEV, LANES), (1, 0, 2)).astype(BF16),
             dwo.reshape(N_DEV, D_MODEL // N_DEV, D_MODEL).astype(BF16)]
    g_in_parts, g_wa_parts, g_wb_parts, g_wo_parts = _exchange(slabs, False, "exchange_grads")
    vec = jnp.concatenate([dshift, dscale, dgate, dnormg, dbf, dcw.reshape(1, 3 * CONV_W), dqg, dkg], axis=1)
    (vec_all,) = _exchange([vec], True, "gather_small")
    vec_all = vec_all.reshape(N_DEV, vec.shape[1])
    n_main = 4 * D_MODEL + LANES + 3 * CONV_W
    tot, g_qg, g_kg = _sum_small(
        vec_all[:, :n_main],
        vec_all[:, n_main:n_main + ATTN_W].reshape(N_DEV * HEADS, HEAD_DIM),
        vec_all[:, n_main + ATTN_W:].reshape(N_DEV * HEADS, HEAD_DIM))
    g_b_ada = tot[:, 0:3 * D_MODEL]
    g_norm_g = tot[:, 3 * D_MODEL:4 * D_MODEL]
    g_b_f = tot[:, 4 * D_MODEL:4 * D_MODEL + HEADS]
    g_cw_full = tot[:, 4 * D_MODEL + LANES:].reshape(3, CONV_W)
    g_cw = lax.dynamic_slice(g_cw_full, (0, me * (CONV_W // N_DEV)), (3, CONV_W // N_DEV))
    dada_mine = lax.dynamic_slice(vec_all[:, 0:3 * D_MODEL], (0, me * ADA_SHARD), (N_DEV, ADA_SHARD))
    g_w_ada = _grad_w_ada(jnp.transpose(c_all, (0, 2, 1)), dada_mine.reshape(N_DEV, 1, ADA_SHARD))

    upd = {}
    upd["w_ada"] = _adamw(w_ada[0], m_w_ada[0], v_w_ada[0], g_w_ada[None], "adamw_w_ada")
    upd["b_ada"] = _adamw(b_ada, m_b_ada, v_b_ada, g_b_ada[None], "adamw_b_ada")
    upd["norm_g"] = _adamw(norm_g, m_norm_g, v_norm_g, g_norm_g[None], "adamw_norm_g")
    upd["w_in"] = _adamw(w_in[0], m_w_in[0], v_w_in[0], g_in_parts, "adamw_w_in")
    upd["b_f"] = _adamw(b_f, m_b_f, v_b_f, g_b_f[None], "adamw_b_f")
    upd["q_norm_g"] = _adamw(q_norm_g, m_q_norm_g, v_q_norm_g, g_qg[None], "adamw_q_norm_g")
    upd["k_norm_g"] = _adamw(k_norm_g, m_k_norm_g, v_k_norm_g, g_kg[None], "adamw_k_norm_g")
    upd["conv_w"] = _adamw(conv_w[0], m_conv_w[0], v_conv_w[0], g_cw[None], "adamw_conv_w")
    upd["w_attn_out"] = _adamw(w_attn_out[0], m_w_attn_out[0], v_w_attn_out[0], g_wa_parts, "adamw_w_attn_out")
    upd["w_conv_out"] = _adamw(w_conv_out[0], m_w_conv_out[0], v_w_conv_out[0], g_wb_parts, "adamw_w_conv_out")
    upd["w_o"] = _adamw(w_o[0], m_w_o[0], v_w_o[0], g_wo_parts, "adamw_w_o")

    names = ["w_ada", "b_ada", "norm_g", "w_in", "b_f", "q_norm_g", "k_norm_g", "conv_w",
             "w_attn_out", "w_conv_out", "w_o"]
    lead = {"w_ada", "w_in", "conv_w", "w_attn_out", "w_conv_out", "w_o"}
    fix = lambda n, a: a[None] if n in lead else a
    loss = lax.psum(loss_part[0, 0], ("x", "y", "c"))
    outs = [loss, grad_x[None]]
    for k in range(4):
        outs += [fix(n, upd[n][k]) for n in names]
    return tuple(outs)
```

```python
import functools

import numpy as np
import jax
import jax.numpy as jnp
from jax import lax
from jax.experimental import pallas as pl
from jax.experimental.pallas import tpu as pltpu

F32 = jnp.float32
BF16 = jnp.bfloat16

D_MODEL = 1024
HEADS = 8
HEAD_DIM = 64
ATTN_W = 512
CONV_W = 512
N_DEV = 8
IN_WIDTH = 6152
IN_SHARD = IN_WIDTH // N_DEV
N_MAIN = 6144
N_FPAD = 128
N_ALL = N_MAIN + N_FPAD
ADA_SHARD = 3 * D_MODEL // N_DEV
EPS = 1e-6
NEG = -1e30

ADAM_LR = 0.001
ADAM_B1 = 0.9
ADAM_B2 = 0.999
ADAM_EPS = 1e-08
ADAM_WD = 0.01
ADAM_STEP = 10

LANES = 128
VMEM_LIMIT = 56 * 1024 * 1024

TM_PROJ = 512
TN_PROJ = 1024
TM_ELEM = 512
TQ = 512
TM_TAIL = 256
TC_CUM = 256
TK_DW = 512
TN_DW = 896
TM_DH = 512
TK_DH = 896

OFF_Q, OFF_K, OFF_V, OFF_ZA, OFF_CONV, OFF_GA, OFF_GB = 0, 512, 1024, 1536, 2048, 4096, 5120


def _params(sem=None):
    return pltpu.CompilerParams(dimension_semantics=sem, vmem_limit_bytes=VMEM_LIMIT)


def _dot(a, b):
    return jnp.dot(a, b, preferred_element_type=F32)


def _dot_nt(a, b):
    return lax.dot_general(a, b, (((1,), (1,)), ((), ())), preferred_element_type=F32)


def _dot_tn(a, b):
    return lax.dot_general(a, b, (((0,), (0,)), ((), ())), preferred_element_type=F32)


def _sigmoid(x):
    return 1.0 / (1.0 + jnp.exp(-x))


def _lane_lo(shape):
    return lax.broadcasted_iota(jnp.int32, shape, len(shape) - 1) < HEAD_DIM


def _seg_sum(z, lo):
    a = jnp.sum(jnp.where(lo, z, 0.0), axis=-1, keepdims=True)
    b = jnp.sum(jnp.where(lo, 0.0, z), axis=-1, keepdims=True)
    return jnp.where(lo, a, b)


def _lane_col(z, lane):
    idx = lax.broadcasted_iota(jnp.int32, z.shape, 1)
    return jnp.sum(jnp.where(idx == lane, z, 0.0), axis=-1, keepdims=True)


def _sub_row(z, row):
    idx = lax.broadcasted_iota(jnp.int32, z.shape, 0)
    return jnp.sum(jnp.where(idx == row, z, 0.0), axis=0, keepdims=True)


def _mesh_pos():
    x, y, c = lax.axis_index("x"), lax.axis_index("y"), lax.axis_index("c")
    return x, y, c, 4 * x + 2 * y + c


def _peer(k, x, y, c):
    px = 1 - x if (k >> 2) & 1 else x
    py = 1 - y if (k >> 1) & 1 else y
    pc = 1 - c if k & 1 else c
    return (px, py, pc), 4 * px + 2 * py + pc


def _exchange(arrs, gather, name):
    n = len(arrs)
    any_spec = pl.BlockSpec(memory_space=pl.ANY)

    def body(*refs):
        ins, outs = refs[:n], refs[n:2 * n]
        send_sems, recv_sems, local_sems = refs[2 * n:]
        x, y, c, me = _mesh_pos()
        copies = []
        for a in range(n):
            own = ins[a] if gather else ins[a].at[me]
            local = pltpu.make_async_copy(own, outs[a].at[me], local_sems.at[a])
            local.start()
            copies.append(local)
            for k in range(1, N_DEV):
                dev, p = _peer(k, x, y, c)
                cp = pltpu.make_async_remote_copy(
                    src_ref=ins[a] if gather else ins[a].at[p],
                    dst_ref=outs[a].at[me],
                    send_sem=send_sems.at[a * (N_DEV - 1) + k - 1],
                    recv_sem=recv_sems.at[a * (N_DEV - 1) + k - 1],
                    device_id=dev, device_id_type=pl.DeviceIdType.MESH)
                cp.start()
                copies.append(cp)
        for cp in copies:
            cp.wait()

    out_shape = [jax.ShapeDtypeStruct((N_DEV,) + a.shape if gather else a.shape, a.dtype) for a in arrs]
    return pl.pallas_call(
        body, name=name, out_shape=out_shape,
        in_specs=[any_spec] * n, out_specs=[any_spec] * n,
        scratch_shapes=[pltpu.SemaphoreType.DMA((n * (N_DEV - 1),)),
                        pltpu.SemaphoreType.DMA((n * (N_DEV - 1),)),
                        pltpu.SemaphoreType.DMA((n,))],
    )(*arrs)


def _gather_two_level(arrs, name):
    n = len(arrs)
    any_spec = pl.BlockSpec(memory_space=pl.ANY)
    per = N_DEV - 1

    def body(*refs):
        ins, outs = refs[:n], refs[n:2 * n]
        send_sems, recv_sems, local_sems = refs[2 * n:]
        x, y, c, me = _mesh_pos()
        sibling = (x, y, 1 - c)
        chips = [(1 - x, y), (x, 1 - y), (1 - x, 1 - y)]

        def copy(a, k, src, blk, to):
            return pltpu.make_async_remote_copy(
                src_ref=src, dst_ref=outs[a].at[blk],
                send_sem=send_sems.at[a * per + k], recv_sem=recv_sems.at[a * per + k],
                device_id=to, device_id_type=pl.DeviceIdType.MESH)

        local = [pltpu.make_async_copy(ins[a], outs[a].at[me], local_sems.at[a]) for a in range(n)]
        for cp in local:
            cp.start()
        first = []
        for a in range(n):
            first.append(copy(a, 0, ins[a], me, sibling))
            first += [copy(a, 1 + j, ins[a], me, (px, py, c)) for j, (px, py) in enumerate(chips)]
        for cp in first:
            cp.start()
        passed = []
        for j, (px, py) in enumerate(chips):
            blk = 4 * px + 2 * py + c
            for a in range(n):
                copy(a, 1 + j, ins[a], blk, (x, y, c)).wait_recv()
                fwd = copy(a, 4 + j, outs[a].at[blk], blk, sibling)
                fwd.start()
                passed.append(fwd)
        for a in range(n):
            copy(a, 0, ins[a], 4 * x + 2 * y + 1 - c, (x, y, c)).wait_recv()
            for j, (px, py) in enumerate(chips):
                copy(a, 4 + j, ins[a], 4 * px + 2 * py + 1 - c, (x, y, c)).wait_recv()
        for cp in first + passed:
            cp.wait_send()
        for cp in local:
            cp.wait()

    return pl.pallas_call(
        body, name=name,
        out_shape=[jax.ShapeDtypeStruct((N_DEV,) + a.shape, a.dtype) for a in arrs],
        in_specs=[any_spec] * n, out_specs=[any_spec] * n,
        scratch_shapes=[pltpu.SemaphoreType.DMA((n * per,)), pltpu.SemaphoreType.DMA((n * per,)),
                        pltpu.SemaphoreType.DMA((n,))],
    )(*arrs)


def _sibling_swap(arrs, name):
    n = len(arrs)
    any_spec = pl.BlockSpec(memory_space=pl.ANY)

    def body(*refs):
        ins, outs = refs[:n], refs[n:2 * n]
        send_sems, recv_sems = refs[2 * n:]
        x, y, c, _ = _mesh_pos()
        copies = [pltpu.make_async_remote_copy(
            src_ref=ins[a].at[1 - c], dst_ref=outs[a], send_sem=send_sems.at[a], recv_sem=recv_sems.at[a],
            device_id=(x, y, 1 - c), device_id_type=pl.DeviceIdType.MESH) for a in range(n)]
        for cp in copies:
            cp.start()
        for cp in copies:
            cp.wait()

    return pl.pallas_call(
        body, name=name,
        out_shape=[jax.ShapeDtypeStruct(a.shape[1:], a.dtype) for a in arrs],
        in_specs=[any_spec] * n, out_specs=[any_spec] * n,
        scratch_shapes=[pltpu.SemaphoreType.DMA((n,)), pltpu.SemaphoreType.DMA((n,))],
    )(*arrs)


def _pair_sum(mine2, theirs, core, name):
    _, _, rows, cols = mine2.shape
    tr = 256 if rows % 256 == 0 else rows

    def body(core_ref, a_ref, b_ref, out_ref):
        out_ref[...] = (a_ref[...].astype(F32) + b_ref[...].astype(F32)).astype(BF16)

    return pl.pallas_call(
        body, name=name,
        grid_spec=pltpu.PrefetchScalarGridSpec(
            num_scalar_prefetch=1, grid=(4, rows // tr),
            in_specs=[pl.BlockSpec((None, None, tr, cols), lambda ch, i, core_: (core_[0], ch, i, 0)),
                      pl.BlockSpec((None, tr, cols), lambda ch, i, core_: (ch, i, 0))],
            out_specs=pl.BlockSpec((None, tr, cols), lambda ch, i, core_: (ch, i, 0))),
        out_shape=jax.ShapeDtypeStruct(theirs.shape, BF16),
        compiler_params=_params(("parallel", "parallel")),
    )(core, mine2, theirs)


def _chip_exchange(arrs, name):
    n = len(arrs)
    any_spec = pl.BlockSpec(memory_space=pl.ANY)

    def body(*refs):
        ins, outs = refs[:n], refs[n:2 * n]
        send_sems, recv_sems, local_sems = refs[2 * n:]
        x, y, c, _ = _mesh_pos()
        my_chip = 2 * x + y
        chips = [(1 - x, y), (x, 1 - y), (1 - x, 1 - y)]
        copies = []
        for a in range(n):
            local = pltpu.make_async_copy(ins[a].at[my_chip], outs[a].at[my_chip], local_sems.at[a])
            local.start()
            copies.append(local)
            for j, (px, py) in enumerate(chips):
                cp = pltpu.make_async_remote_copy(
                    src_ref=ins[a].at[2 * px + py], dst_ref=outs[a].at[my_chip],
                    send_sem=send_sems.at[a * 3 + j], recv_sem=recv_sems.at[a * 3 + j],
                    device_id=(px, py, c), device_id_type=pl.DeviceIdType.MESH)
                cp.start()
                copies.append(cp)
        for cp in copies:
            cp.wait()

    return pl.pallas_call(
        body, name=name,
        out_shape=[jax.ShapeDtypeStruct(a.shape, a.dtype) for a in arrs],
        in_specs=[any_spec] * n, out_specs=[any_spec] * n,
        scratch_shapes=[pltpu.SemaphoreType.DMA((n * 3,)), pltpu.SemaphoreType.DMA((n * 3,)),
                        pltpu.SemaphoreType.DMA((n,))],
    )(*arrs)


def _ada_exchange(c_row, w_ada_sh):
    def body(c_ref, w_ref, call_ref, adag_ref, mine_ref, send_sems, recv_sems):
        x, y, c, me = _mesh_pos()

        def copy(phase, k, src, dst):
            dev, _ = _peer(k, x, y, c)
            return pltpu.make_async_remote_copy(
                src_ref=src, dst_ref=dst,
                send_sem=send_sems.at[phase * (N_DEV - 1) + k - 1],
                recv_sem=recv_sems.at[phase * (N_DEV - 1) + k - 1],
                device_id=dev, device_id_type=pl.DeviceIdType.MESH)

        call_ref[me] = c_ref[...]
        first = [copy(0, k, c_ref, call_ref.at[me]) for k in range(1, N_DEV)]
        for cp in first:
            cp.start()
        for cp in first:
            cp.wait()
        wb = w_ref[...].astype(BF16)
        for b in range(N_DEV):
            row = jnp.broadcast_to(call_ref[b], (8, D_MODEL)).astype(BF16)
            mine_ref[b] = _sub_row(_dot(row, wb), 0)
        adag_ref[me] = mine_ref[...]
        second = [copy(1, k, mine_ref, adag_ref.at[me]) for k in range(1, N_DEV)]
        for cp in second:
            cp.start()
        for cp in second:
            cp.wait()

    vm = pl.BlockSpec(memory_space=pltpu.VMEM)
    return pl.pallas_call(
        body, name="ada_exchange",
        out_shape=[jax.ShapeDtypeStruct((N_DEV, 1, D_MODEL), F32),
                   jax.ShapeDtypeStruct((N_DEV, N_DEV, 1, ADA_SHARD), F32)],
        in_specs=[vm, vm], out_specs=[vm, vm],
        scratch_shapes=[pltpu.VMEM((N_DEV, 1, ADA_SHARD), F32),
                        pltpu.SemaphoreType.DMA((2 * (N_DEV - 1),)),
                        pltpu.SemaphoreType.DMA((2 * (N_DEV - 1),))],
        compiler_params=pltpu.CompilerParams(vmem_limit_bytes=VMEM_LIMIT),
    )(c_row, w_ada_sh)


def _proj_fwd(x, ada3, norm_g, w_main, w_f):
    s = x.shape[0]
    tm, tn = min(TM_PROJ, s), TN_PROJ

    def body(x_ref, ada_ref, g_ref, w_ref, wf_ref, proj_ref, fl_ref, ht_ref, h_s):
        @pl.when(pl.program_id(1) == 0)
        def _():
            xv = x_ref[...]
            r = lax.rsqrt(jnp.mean(xv * xv, axis=-1, keepdims=True) + EPS)
            hv = ((xv * r) * g_ref[...]) * (1.0 + ada_ref[1:2, :]) + ada_ref[0:1, :]
            hb = hv.astype(BF16)
            h_s[...] = hb
            ht_ref[...] = hv.T.astype(BF16)
            fl_ref[...] = _dot(hb, wf_ref[...])
        proj_ref[...] = _dot(h_s[...], w_ref[...])

    return pl.pallas_call(
        body, name="proj_fwd", grid=(s // tm, N_MAIN // tn),
        in_specs=[pl.BlockSpec((tm, D_MODEL), lambda i, j: (i, 0)),
                  pl.BlockSpec((3, D_MODEL), lambda i, j: (0, 0)),
                  pl.BlockSpec((1, D_MODEL), lambda i, j: (0, 0)),
                  pl.BlockSpec((D_MODEL, tn), lambda i, j: (0, j)),
                  pl.BlockSpec((D_MODEL, N_FPAD), lambda i, j: (0, 0))],
        out_specs=[pl.BlockSpec((tm, tn), lambda i, j: (i, j)),
                   pl.BlockSpec((tm, N_FPAD), lambda i, j: (i, 0)),
                   pl.BlockSpec((D_MODEL, tm), lambda i, j: (0, i))],
        out_shape=[jax.ShapeDtypeStruct((s, N_MAIN), F32),
                   jax.ShapeDtypeStruct((s, N_FPAD), F32),
                   jax.ShapeDtypeStruct((D_MODEL, s), BF16)],
        scratch_shapes=[pltpu.VMEM((tm, D_MODEL), BF16)],
        compiler_params=_params(("parallel", "arbitrary")),
    )(x, ada3, norm_g, w_main, w_f)


L_ONE_Q, L_F_Q, L_LSE_Q, L_END = HEAD_DIM, HEAD_DIM + 3, HEAD_DIM + 6, HEAD_DIM + 9


def _split3(f):
    hi = f.astype(BF16).astype(F32)
    r = f - hi
    mid = r.astype(BF16).astype(F32)
    return hi, mid, r - mid


def _place3(lane, first, parts, otherwise):
    a, b, c = parts
    return jnp.where(lane == first, a, jnp.where(lane == first + 1, b, jnp.where(lane == first + 2, c, otherwise)))


def _qkv_prep(proj, cum, qg, kg):
    s = proj.shape[0]
    tm = min(TM_ELEM, s)
    scale = HEAD_DIM ** -0.5

    def body(p_ref, cum_ref, qg_ref, kg_ref, qa_ref, ka_ref, va_ref, kt_ref, vt_ref):
        lane = lax.broadcasted_iota(jnp.int32, (tm, LANES), 1)
        lo = lane < HEAD_DIM
        cum_v = cum_ref[...]
        v_tail = jnp.where(lane < L_F_Q, 1.0, 0.0)
        for pr in range(ATTN_W // LANES):
            sl = slice(pr * LANES, (pr + 1) * LANES)
            q2 = p_ref[:, OFF_Q + pr * LANES:OFF_Q + (pr + 1) * LANES]
            k2 = p_ref[:, OFF_K + pr * LANES:OFF_K + (pr + 1) * LANES]
            v2 = p_ref[:, OFF_V + pr * LANES:OFF_V + (pr + 1) * LANES]
            rq = lax.rsqrt(_seg_sum(q2 * q2, lo) * (1.0 / HEAD_DIM) + EPS)
            rk = lax.rsqrt(_seg_sum(k2 * k2, lo) * (1.0 / HEAD_DIM) + EPS)
            qn = ((q2 * rq) * qg_ref[:, sl]) * scale
            kn = (k2 * rk) * kg_ref[:, sl]
            for hh in range(2):
                h = 2 * pr + hh
                f3 = _split3(_lane_col(cum_v, h))
                qh = qn if hh == 0 else pltpu.roll(qn, HEAD_DIM, 1)
                kh = kn if hh == 0 else pltpu.roll(kn, HEAD_DIM, 1)
                vh = v2 if hh == 0 else pltpu.roll(v2, HEAD_DIM, 1)
                q_tail = jnp.where(lane < L_F_Q, 1.0, _place3(lane, L_F_Q, f3, 0.0))
                k_tail = _place3(lane, L_ONE_Q, tuple(-f for f in f3), jnp.where(lane < L_END, 1.0, 0.0))
                k_row = jnp.where(lo, kh, k_tail)
                v_row = jnp.where(lo, vh, v_tail)
                qa_ref[h] = jnp.where(lo, qh, q_tail).astype(BF16)
                ka_ref[h] = k_row.astype(BF16)
                va_ref[h] = v_row.astype(BF16)
                kt_ref[h] = k_row.T.astype(BF16)
                vt_ref[h] = v_row.T.astype(BF16)

    heads = pl.BlockSpec((HEADS, tm, LANES), lambda i: (0, i, 0))
    heads_t = pl.BlockSpec((HEADS, LANES, tm), lambda i: (0, 0, i))
    vec = pl.BlockSpec((1, ATTN_W), lambda i: (0, 0))
    return pl.pallas_call(
        body, name="qkv_prep", grid=(s // tm,),
        in_specs=[pl.BlockSpec((tm, 3 * ATTN_W), lambda i: (i, 0)),
                  pl.BlockSpec((tm, LANES), lambda i: (i, 0)), vec, vec],
        out_specs=[heads, heads, heads, heads_t, heads_t],
        out_shape=[jax.ShapeDtypeStruct((HEADS, s, LANES), BF16)] * 3
        + [jax.ShapeDtypeStruct((HEADS, LANES, s), BF16)] * 2,
        compiler_params=_params(("parallel",)),
    )(proj, cum, qg, kg)


def _log_forget(fl, bf):
    z = fl + bf
    lf = jnp.minimum(z, 0.0) - jnp.log1p(jnp.exp(-jnp.abs(z)))
    lane = lax.broadcasted_iota(jnp.int32, z.shape, 1)
    return jnp.where(lane < HEADS, lf, 0.0)


def _forget_cumsum(fl, bf_pad):
    s = fl.shape[0]
    tc = min(TC_CUM, s)

    def body(fl_ref, bf_ref, cum_ref, carry):
        @pl.when(pl.program_id(0) == 0)
        def _():
            carry[...] = jnp.zeros_like(carry)
        lf = _log_forget(fl_ref[...], bf_ref[...])
        r = lax.broadcasted_iota(jnp.int32, (tc, tc), 0)
        cidx = lax.broadcasted_iota(jnp.int32, (tc, tc), 1)
        tri = (cidx <= r).astype(F32)
        cs = jnp.dot(tri, lf, preferred_element_type=F32, precision=lax.Precision.HIGHEST) + carry[...]
        cum_ref[...] = cs
        carry[...] = cum_ref[tc - 1:tc, :]

    return pl.pallas_call(
        body, name="forget_cumsum", grid=(s // tc,),
        in_specs=[pl.BlockSpec((tc, LANES), lambda i: (i, 0)),
                  pl.BlockSpec((1, LANES), lambda i: (0, 0))],
        out_specs=pl.BlockSpec((tc, LANES), lambda i: (i, 0)),
        out_shape=jax.ShapeDtypeStruct((s, LANES), F32),
        scratch_shapes=[pltpu.VMEM((1, LANES), F32)],
        compiler_params=_params(("arbitrary",)),
    )(fl, bf_pad)


def _causal_t(t):
    return lax.broadcasted_iota(jnp.int32, (t, t), 0) <= lax.broadcasted_iota(jnp.int32, (t, t), 1)


def _tri_steps(nt, q_major):
    if q_major:
        pairs = [(i, j) for i in range(nt) for j in range(i + 1)]
    else:
        pairs = [(i, j) for j in range(nt) for i in range(j, nt)]
    return (jnp.asarray(np.array([p[0] for p in pairs], np.int32)),
            jnp.asarray(np.array([p[1] for p in pairs], np.int32)))


def _attn_fwd(qa, ka, vt, proj):
    s = qa.shape[1]
    t = min(TQ, s)
    it, jt = _tri_steps(s // t, True)
    za_blk = OFF_ZA // LANES

    def body(it_ref, jt_ref, q_ref, k_ref, vt_ref, za_ref, attn_ref, oa_ref, qb_ref, m_s, acc_s, pair_s):
        step = pl.program_id(1)
        i, j = it_ref[step], jt_ref[step]

        @pl.when(j == 0)
        def _():
            m_s[...] = jnp.full_like(m_s, NEG)
            acc_s[...] = jnp.zeros_like(acc_s)

        def update(masked):
            for hh in range(2):
                st = _dot_nt(k_ref[hh], q_ref[hh])
                if masked:
                    st = jnp.where(_causal_t(t), st, NEG)
                m_prev = m_s[hh]
                m_next = jnp.maximum(m_prev, jnp.max(st, axis=0, keepdims=True))
                alpha = jnp.exp(m_prev - m_next)
                pt = jnp.exp(st - m_next).astype(BF16)
                acc_s[hh] = acc_s[hh] * alpha + _dot(vt_ref[hh], pt)
                m_s[hh] = m_next

        @pl.when(j < i)
        def _():
            update(False)

        @pl.when(j == i)
        def _():
            update(True)
            row = lax.broadcasted_iota(jnp.int32, (LANES, t), 0)
            lane = lax.broadcasted_iota(jnp.int32, (t, LANES), 1)
            for hh in range(2):
                l_row = acc_s[hh, L_ONE_Q:L_ONE_Q + 1, :]
                pair_s[hh * HEAD_DIM:(hh + 1) * HEAD_DIM, :] = acc_s[hh, 0:HEAD_DIM, :] / l_row
                lse3 = _split3(m_s[hh] + jnp.log(l_row))
                tail_t = _place3(row, L_LSE_Q, tuple(-x for x in lse3), 0.0)
                keep_q = jnp.logical_or(lane < L_LSE_Q, lane >= L_END)
                qb_ref[hh] = jnp.where(keep_q, q_ref[hh].astype(F32), tail_t.T).astype(BF16)
            out = pair_s[...].T
            attn_ref[...] = out
            z = za_ref[...]
            oa_ref[...] = (out * (z * _sigmoid(z))).astype(BF16)

    pair_q = pl.BlockSpec((2, t, LANES), lambda p, n, it_, jt_: (p, it_[n], 0))
    pair_k = pl.BlockSpec((2, t, LANES), lambda p, n, it_, jt_: (p, jt_[n], 0))
    pair_kt = pl.BlockSpec((2, LANES, t), lambda p, n, it_, jt_: (p, 0, jt_[n]))
    out_q = pl.BlockSpec((t, LANES), lambda p, n, it_, jt_: (it_[n], p))
    return pl.pallas_call(
        body, name="attn_fwd",
        grid_spec=pltpu.PrefetchScalarGridSpec(
            num_scalar_prefetch=2, grid=(ATTN_W // LANES, it.shape[0]),
            in_specs=[pair_q, pair_k, pair_kt,
                      pl.BlockSpec((t, LANES), lambda p, n, it_, jt_: (it_[n], za_blk + p))],
            out_specs=[out_q, out_q, pair_q],
            scratch_shapes=[pltpu.VMEM((2, 1, t), F32), pltpu.VMEM((2, LANES, t), F32),
                            pltpu.VMEM((LANES, t), F32)]),
        out_shape=[jax.ShapeDtypeStruct((s, ATTN_W), F32),
                   jax.ShapeDtypeStruct((s, ATTN_W), BF16),
                   jax.ShapeDtypeStruct((HEADS, s, LANES), BF16)],
        compiler_params=_params(("parallel", "arbitrary")),
    )(it, jt, qa, ka, vt, proj)


def _conv_parts(blk, halo, first, w_ref, tm):
    gb, gc = blk[:, 0:LANES], blk[:, LANES:2 * LANES]
    u, zb = blk[:, 2 * LANES:3 * LANES], blk[:, 3 * LANES:4 * LANES]
    cu = gc * u
    cu_h = jnp.where(first, 0.0, halo[:, LANES:2 * LANES] * halo[:, 2 * LANES:3 * LANES])
    prev1, prev2 = _sub_row(cu_h, 7), _sub_row(cu_h, 6)
    row = lax.broadcasted_iota(jnp.int32, (tm, LANES), 0)
    r1 = jnp.where(row == 0, prev1, pltpu.roll(cu, 1, 0))
    r2 = jnp.where(row == 0, prev2, jnp.where(row == 1, prev1, pltpu.roll(cu, 2, 0)))
    conv = w_ref[2:3, :] * cu + w_ref[1:2, :] * r1 + w_ref[0:1, :] * r2
    return gb, gc, u, zb, cu, r1, r2, conv


def _conv_fwd(proj, conv_w):
    s = proj.shape[0]
    tm = min(TM_ELEM, s)
    cb = OFF_CONV // (4 * LANES)

    def body(p_ref, halo_ref, w_ref, ob_ref):
        first = pl.program_id(1) == 0
        gb, _, _, zb, _, _, _, conv = _conv_parts(p_ref[...], halo_ref[...], first, w_ref, tm)
        ob_ref[...] = (gb * conv * (zb * _sigmoid(zb))).astype(BF16)

    return pl.pallas_call(
        body, name="conv_fwd", grid=(CONV_W // LANES, s // tm),
        in_specs=[pl.BlockSpec((tm, 4 * LANES), lambda c, i: (i, cb + c)),
                  pl.BlockSpec((8, 4 * LANES), lambda c, i: (jnp.maximum(i * (tm // 8) - 1, 0), cb + c)),
                  pl.BlockSpec((3, LANES), lambda c, i: (0, c))],
        out_specs=pl.BlockSpec((tm, LANES), lambda c, i: (i, c)),
        out_shape=jax.ShapeDtypeStruct((s, CONV_W), BF16),
        compiler_params=_params(("parallel", "parallel")),
    )(proj, proj, conv_w)


def _tail(oa, ob, proj, x, target, ada3, wa, wb, wo):
    s = x.shape[0]
    tm = min(TM_TAIL, s)
    gab_blk = OFF_GA // (2 * D_MODEL)

    def body(oa_ref, ob_ref, gab_ref, x_ref, t_ref, ada_ref, wa_ref, wb_ref, wo_ref,
             dy_ref, dgab_ref, doa_ref, dob_ref, dwo_ref, dwa_ref, dwb_ref, dgate_ref, loss_ref):
        @pl.when(pl.program_id(0) == 0)
        def _():
            dwo_ref[...] = jnp.zeros_like(dwo_ref)
            dwa_ref[...] = jnp.zeros_like(dwa_ref)
            dwb_ref[...] = jnp.zeros_like(dwb_ref)
            dgate_ref[...] = jnp.zeros_like(dgate_ref)
            loss_ref[...] = jnp.zeros_like(loss_ref)

        oa_v, ob_v = oa_ref[...], ob_ref[...]
        wa_v, wb_v, wo_v = wa_ref[...], wb_ref[...], wo_ref[...]
        a2 = _dot(oa_v, wa_v)
        b2 = _dot(ob_v, wb_v)
        sa = _sigmoid(gab_ref[:, 0:D_MODEL])
        sb = _sigmoid(gab_ref[:, D_MODEL:2 * D_MODEL])
        mb = (sa * a2 + sb * b2).astype(BF16)
        mo = _dot(mb, wo_v)
        gate = ada_ref[2:3, :]
        err = (x_ref[...] + gate * mo) - t_ref[...]
        dy = err * (1.0 / D_MODEL)
        dy_ref[...] = dy
        loss_ref[...] += 0.5 * jnp.sum(err * err) * (1.0 / D_MODEL)
        dgate_ref[...] += jnp.sum(dy * mo, axis=0, keepdims=True)
        dmo = (dy * gate).astype(BF16)
        dmerged = _dot_nt(dmo, wo_v)
        dwo_ref[...] += _dot_tn(mb, dmo)
        da2 = (dmerged * sa).astype(BF16)
        db2 = (dmerged * sb).astype(BF16)
        dgab_ref[:, 0:D_MODEL] = (dmerged * a2 * (sa * (1.0 - sa))).astype(BF16)
        dgab_ref[:, D_MODEL:2 * D_MODEL] = (dmerged * b2 * (sb * (1.0 - sb))).astype(BF16)
        doa_ref[...] = _dot_nt(da2, wa_v)
        dob_ref[...] = _dot_nt(db2, wb_v)
        dwa_ref[...] += _dot_tn(oa_v, da2)
        dwb_ref[...] += _dot_tn(ob_v, db2)

    half = pl.BlockSpec((tm, ATTN_W), lambda i: (i, 0))
    full = pl.BlockSpec((tm, D_MODEL), lambda i: (i, 0))

    def const(shape):
        return pl.BlockSpec(shape, lambda i: (0, 0))

    return pl.pallas_call(
        body, name="tail", grid=(s // tm,),
        in_specs=[half, half, pl.BlockSpec((tm, 2 * D_MODEL), lambda i: (i, gab_blk)), full, full,
                  const((3, D_MODEL)), const((ATTN_W, D_MODEL)), const((CONV_W, D_MODEL)),
                  const((D_MODEL, D_MODEL))],
        out_specs=[full, pl.BlockSpec((tm, 2 * D_MODEL), lambda i: (i, 0)), half, half,
                   const((D_MODEL, D_MODEL)), const((ATTN_W, D_MODEL)), const((CONV_W, D_MODEL)),
                   const((1, D_MODEL)), const((1, LANES))],
        out_shape=[jax.ShapeDtypeStruct((s, D_MODEL), F32),
                   jax.ShapeDtypeStruct((s, 2 * D_MODEL), BF16),
                   jax.ShapeDtypeStruct((s, ATTN_W), F32),
                   jax.ShapeDtypeStruct((s, CONV_W), F32),
                   jax.ShapeDtypeStruct((D_MODEL, D_MODEL), F32),
                   jax.ShapeDtypeStruct((ATTN_W, D_MODEL), F32),
                   jax.ShapeDtypeStruct((CONV_W, D_MODEL), F32),
                   jax.ShapeDtypeStruct((1, D_MODEL), F32),
                   jax.ShapeDtypeStruct((1, LANES), F32)],
        compiler_params=_params(("arbitrary",)),
    )(oa, ob, proj, x, target, ada3, wa, wb, wo)


def _attn_bwd_prep(doa, attn, proj):
    s = doa.shape[0]
    tm = min(TM_ELEM, s)
    za_blk = OFF_ZA // ATTN_W

    def body(doa_ref, attn_ref, za_ref, do_ref, dza_ref):
        lane = lax.broadcasted_iota(jnp.int32, (tm, LANES), 1)
        lo = lane < HEAD_DIM
        for pr in range(ATTN_W // LANES):
            sl = slice(pr * LANES, (pr + 1) * LANES)
            g, a, z = doa_ref[:, sl], attn_ref[:, sl], za_ref[:, sl]
            sg = _sigmoid(z)
            dat = (g * (z * sg)).astype(BF16).astype(F32)
            prod = dat * a
            dza_ref[:, sl] = (g * a * (sg * (1.0 + z * (1.0 - sg)))).astype(BF16)
            for hh in range(2):
                sel = lo if hh == 0 else jnp.logical_not(lo)
                delta3 = _split3(jnp.sum(jnp.where(sel, prod, 0.0), axis=-1, keepdims=True))
                dh = dat if hh == 0 else pltpu.roll(dat, HEAD_DIM, 1)
                tail = _place3(lane, L_ONE_Q, tuple(-d for d in delta3), 0.0)
                do_ref[2 * pr + hh] = jnp.where(lo, dh, tail).astype(BF16)

    row = pl.BlockSpec((tm, ATTN_W), lambda i: (i, 0))
    return pl.pallas_call(
        body, name="attn_bwd_prep", grid=(s // tm,),
        in_specs=[row, row, pl.BlockSpec((tm, ATTN_W), lambda i: (i, za_blk))],
        out_specs=[pl.BlockSpec((HEADS, tm, LANES), lambda i: (0, i, 0)), row],
        out_shape=[jax.ShapeDtypeStruct((HEADS, s, LANES), BF16),
                   jax.ShapeDtypeStruct((s, ATTN_W), BF16)],
        compiler_params=_params(("parallel",)),
    )(doa, attn, proj)


def _attn_bwd(qb, ka, kt, va, do):
    s = qb.shape[1]
    t = min(TQ, s)
    nt = s // t
    it, jt = _tri_steps(nt, False)

    def body(it_ref, jt_ref, q_ref, k_ref, kt_ref, v_ref, do_ref, dqt_ref, dk_ref, dv_ref):
        step = pl.program_id(1)
        i, j = it_ref[step], jt_ref[step]

        @pl.when(step == 0)
        def _():
            dqt_ref[...] = jnp.zeros_like(dqt_ref)

        @pl.when(i == j)
        def _():
            dk_ref[...] = jnp.zeros_like(dk_ref)
            dv_ref[...] = jnp.zeros_like(dv_ref)

        def update(masked):
            for hh in range(2):
                qh, doh = q_ref[hh], do_ref[hh]
                st = _dot_nt(k_ref[hh], qh)
                if masked:
                    st = jnp.where(_causal_t(t), st, NEG)
                pt = jnp.exp(st)
                dst = (pt * _dot_nt(v_ref[hh], doh)).astype(BF16)
                dv_ref[hh] += _dot(pt.astype(BF16), doh)
                dk_ref[hh] += _dot(dst, qh)
                dqt_ref[hh, i] += _dot(kt_ref[hh], dst)

        @pl.when(i > j)
        def _():
            update(False)

        @pl.when(i == j)
        def _():
            update(True)

    pair_q = pl.BlockSpec((2, t, LANES), lambda p, n, it_, jt_: (p, it_[n], 0))
    pair_k = pl.BlockSpec((2, t, LANES), lambda p, n, it_, jt_: (p, jt_[n], 0))
    pair_kt = pl.BlockSpec((2, LANES, t), lambda p, n, it_, jt_: (p, 0, jt_[n]))
    return pl.pallas_call(
        body, name="attn_bwd",
        grid_spec=pltpu.PrefetchScalarGridSpec(
            num_scalar_prefetch=2, grid=(ATTN_W // LANES, it.shape[0]),
            in_specs=[pair_q, pair_k, pair_kt, pair_k, pair_q],
            out_specs=[pl.BlockSpec((2, nt, LANES, t), lambda p, n, it_, jt_: (p, 0, 0, 0)),
                       pair_k, pair_k]),
        out_shape=[jax.ShapeDtypeStruct((HEADS, nt, LANES, t), F32),
                   jax.ShapeDtypeStruct((HEADS, s, LANES), F32),
                   jax.ShapeDtypeStruct((HEADS, s, LANES), F32)],
        compiler_params=_params(("parallel", "arbitrary")),
    )(it, jt, qb, ka, kt, va, do)


def _forget_bwd(dcum, fl, bf_pad):
    s = fl.shape[0]
    tc = min(TC_CUM, s)
    n = s // tc

    def body(dc_ref, fl_ref, bf_ref, df_ref, dbf_ref, carry):
        @pl.when(pl.program_id(0) == 0)
        def _():
            carry[...] = jnp.zeros_like(carry)
            dbf_ref[...] = jnp.zeros_like(dbf_ref)
        r = lax.broadcasted_iota(jnp.int32, (tc, tc), 0)
        cidx = lax.broadcasted_iota(jnp.int32, (tc, tc), 1)
        tri = (cidx >= r).astype(F32)
        dc = dc_ref[...]
        dlf = jnp.dot(tri, dc, preferred_element_type=F32, precision=lax.Precision.HIGHEST) + carry[...]
        carry[...] += jnp.sum(dc, axis=0, keepdims=True)
        lane = lax.broadcasted_iota(jnp.int32, (tc, LANES), 1)
        dfl = jnp.where(lane < HEADS, dlf * _sigmoid(-(fl_ref[...] + bf_ref[...])), 0.0)
        df_ref[...] = dfl.astype(BF16)
        dbf_ref[...] += jnp.sum(dfl, axis=0, keepdims=True)

    rev = pl.BlockSpec((tc, LANES), lambda i: (n - 1 - i, 0))
    vec = pl.BlockSpec((1, LANES), lambda i: (0, 0))
    return pl.pallas_call(
        body, name="forget_bwd", grid=(n,),
        in_specs=[rev, rev, vec], out_specs=[rev, vec],
        out_shape=[jax.ShapeDtypeStruct((s, LANES), BF16), jax.ShapeDtypeStruct((1, LANES), F32)],
        scratch_shapes=[pltpu.VMEM((1, LANES), F32)],
        compiler_params=_params(("arbitrary",)),
    )(dcum, fl, bf_pad)


def _qk_norm_bwd(dqt, dk, dv, proj, qg, kg):
    s = dv.shape[1]
    tm = dqt.shape[-1]
    scale = HEAD_DIM ** -0.5

    def body(dqt_ref, dk_ref, dv_ref, p_ref, qg_ref, kg_ref, out_ref, dqg_ref, dkg_ref, dcum_ref):
        @pl.when(pl.program_id(0) == 0)
        def _():
            dqg_ref[...] = jnp.zeros_like(dqg_ref)
            dkg_ref[...] = jnp.zeros_like(dkg_ref)
        lane = lax.broadcasted_iota(jnp.int32, (tm, LANES), 1)
        lo = lane < HEAD_DIM
        dq_rows = [dqt_ref[h, 0].T for h in range(HEADS)]
        dcum = jnp.zeros((tm, LANES), F32)
        for h in range(HEADS):
            dcum = jnp.where(lane == h, _lane_col(dq_rows[h], L_F_Q) - _lane_col(dk_ref[h], L_ONE_Q), dcum)
        dcum_ref[...] = dcum

        def pair(a, b):
            return jnp.where(lo, a, pltpu.roll(b, HEAD_DIM, 1))

        def one(raw, dy, g, dg_ref, sl, off):
            r = lax.rsqrt(_seg_sum(raw * raw, lo) * (1.0 / HEAD_DIM) + EPS)
            xhat = raw * r
            dg_ref[:, sl] += jnp.sum(dy * xhat, axis=0, keepdims=True)
            dxh = dy * g
            dx = r * (dxh - xhat * (_seg_sum(dxh * xhat, lo) * (1.0 / HEAD_DIM)))
            out_ref[:, off + sl.start:off + sl.stop] = dx.astype(BF16)

        for pr in range(ATTN_W // LANES):
            sl = slice(pr * LANES, (pr + 1) * LANES)
            dq2 = pair(dq_rows[2 * pr], dq_rows[2 * pr + 1])
            one(p_ref[:, OFF_Q + sl.start:OFF_Q + sl.stop], dq2 * scale, qg_ref[:, sl], dqg_ref, sl, OFF_Q)
            one(p_ref[:, OFF_K + sl.start:OFF_K + sl.stop], pair(dk_ref[2 * pr], dk_ref[2 * pr + 1]),
                kg_ref[:, sl], dkg_ref, sl, OFF_K)
            out_ref[:, OFF_V + sl.start:OFF_V + sl.stop] = pair(dv_ref[2 * pr], dv_ref[2 * pr + 1]).astype(BF16)

    heads = pl.BlockSpec((HEADS, tm, LANES), lambda i: (0, i, 0))
    vec = pl.BlockSpec((1, ATTN_W), lambda i: (0, 0))
    return pl.pallas_call(
        body, name="qk_norm_bwd", grid=(s // tm,),
        in_specs=[pl.BlockSpec((HEADS, 1, LANES, tm), lambda i: (0, i, 0, 0)), heads, heads,
                  pl.BlockSpec((tm, 2 * ATTN_W), lambda i: (i, 0)), vec, vec],
        out_specs=[pl.BlockSpec((tm, 3 * ATTN_W), lambda i: (i, 0)), vec, vec,
                   pl.BlockSpec((tm, LANES), lambda i: (i, 0))],
        out_shape=[jax.ShapeDtypeStruct((s, 3 * ATTN_W), BF16),
                   jax.ShapeDtypeStruct((1, ATTN_W), F32), jax.ShapeDtypeStruct((1, ATTN_W), F32),
                   jax.ShapeDtypeStruct((s, LANES), F32)],
        compiler_params=_params(("arbitrary",)),
    )(dqt, dk, dv, proj, qg, kg)


def _conv_bwd(dob, proj, conv_w):
    s = dob.shape[0]
    tm = min(TM_ELEM, s)
    cb = OFF_CONV // (4 * LANES)
    nblk8 = s // 8

    def body(dob_ref, p_ref, prev_ref, next_ref, dnext_ref, w_ref, out_ref, dw_ref):
        i = pl.program_id(1)

        @pl.when(i == 0)
        def _():
            dw_ref[...] = jnp.zeros_like(dw_ref)
        gb, gc, u, zb, cu, r1, r2, conv = _conv_parts(p_ref[...], prev_ref[...], i == 0, w_ref, tm)
        g = dob_ref[...]
        sg = _sigmoid(zb)
        sz = zb * sg
        dconv = g * gb * sz
        nxt = next_ref[...]
        zn = nxt[:, 3 * LANES:4 * LANES]
        dcn = jnp.where(i == pl.num_programs(1) - 1, 0.0,
                        dnext_ref[...] * nxt[:, 0:LANES] * (zn * _sigmoid(zn)))
        nxt1, nxt2 = _sub_row(dcn, 0), _sub_row(dcn, 1)
        row = lax.broadcasted_iota(jnp.int32, (tm, LANES), 0)
        f1 = jnp.where(row == tm - 1, nxt1, pltpu.roll(dconv, tm - 1, 0))
        f2 = jnp.where(row == tm - 2, nxt1, jnp.where(row == tm - 1, nxt2, pltpu.roll(dconv, tm - 2, 0)))
        dcu = w_ref[2:3, :] * dconv + w_ref[1:2, :] * f1 + w_ref[0:1, :] * f2
        out_ref[:, 0:LANES] = (g * conv * sz).astype(BF16)
        out_ref[:, LANES:2 * LANES] = (dcu * u).astype(BF16)
        out_ref[:, 2 * LANES:3 * LANES] = (dcu * gc).astype(BF16)
        out_ref[:, 3 * LANES:4 * LANES] = (g * gb * conv * (sg * (1.0 + zb * (1.0 - sg)))).astype(BF16)
        w_row = lax.broadcasted_iota(jnp.int32, (3, LANES), 0)
        dw0 = jnp.sum(dconv * r2, axis=0, keepdims=True)
        dw1 = jnp.sum(dconv * r1, axis=0, keepdims=True)
        dw2 = jnp.sum(dconv * cu, axis=0, keepdims=True)
        dw_ref[...] += jnp.where(w_row == 0, dw0, jnp.where(w_row == 1, dw1, dw2))

    nxt_idx = lambda i: jnp.minimum((i + 1) * (tm // 8), nblk8 - 1)
    return pl.pallas_call(
        body, name="conv_bwd", grid=(CONV_W // LANES, s // tm),
        in_specs=[pl.BlockSpec((tm, LANES), lambda c, i: (i, c)),
                  pl.BlockSpec((tm, 4 * LANES), lambda c, i: (i, cb + c)),
                  pl.BlockSpec((8, 4 * LANES), lambda c, i: (jnp.maximum(i * (tm // 8) - 1, 0), cb + c)),
                  pl.BlockSpec((8, 4 * LANES), lambda c, i: (nxt_idx(i), cb + c)),
                  pl.BlockSpec((8, LANES), lambda c, i: (nxt_idx(i), c)),
                  pl.BlockSpec((3, LANES), lambda c, i: (0, c))],
        out_specs=[pl.BlockSpec((tm, 4 * LANES), lambda c, i: (i, c)),
                   pl.BlockSpec((3, LANES), lambda c, i: (0, c))],
        out_shape=[jax.ShapeDtypeStruct((s, 4 * CONV_W), BF16), jax.ShapeDtypeStruct((3, CONV_W), F32)],
        compiler_params=_params(("parallel", "arbitrary")),
    )(dob, proj, proj, proj, dob, conv_w)


def _dw_in(ht, dproj):
    s = ht.shape[1]
    tk, tn = min(TK_DW, s), TN_DW

    def body(ht_ref, dp_ref, out_ref):
        @pl.when(pl.program_id(1) == 0)
        def _():
            out_ref[...] = jnp.zeros_like(out_ref)
        out_ref[...] += _dot(ht_ref[...], dp_ref[...])

    return pl.pallas_call(
        body, name="dw_in", grid=(N_ALL // tn, s // tk),
        in_specs=[pl.BlockSpec((D_MODEL, tk), lambda n, k: (0, k)),
                  pl.BlockSpec((tk, tn), lambda n, k: (k, n))],
        out_specs=pl.BlockSpec((D_MODEL, tn), lambda n, k: (0, n)),
        out_shape=jax.ShapeDtypeStruct((D_MODEL, N_ALL), F32),
        compiler_params=_params(("parallel", "arbitrary")),
    )(ht, dproj)


def _dh_and_dx(dproj, w_all_t, x, dy, ada3, norm_g):
    s = x.shape[0]
    tm, tk = min(TM_DH, s), TK_DH
    nk = N_ALL // tk

    def body(dp_ref, wt_ref, x_ref, dy_ref, ada_ref, g_ref, gx_ref, dsh_ref, dsc_ref, dg_ref, acc):
        i, k = pl.program_id(0), pl.program_id(1)

        @pl.when(jnp.logical_and(i == 0, k == 0))
        def _():
            dsh_ref[...] = jnp.zeros_like(dsh_ref)
            dsc_ref[...] = jnp.zeros_like(dsc_ref)
            dg_ref[...] = jnp.zeros_like(dg_ref)

        @pl.when(k == 0)
        def _():
            acc[...] = jnp.zeros_like(acc)
        acc[...] += _dot(dp_ref[...], wt_ref[...])

        @pl.when(k == nk - 1)
        def _():
            dh = acc[...]
            xv = x_ref[...]
            r = lax.rsqrt(jnp.mean(xv * xv, axis=-1, keepdims=True) + EPS)
            xhat = xv * r
            g = g_ref[...]
            one_sc = 1.0 + ada_ref[1:2, :]
            dsh_ref[...] += jnp.sum(dh, axis=0, keepdims=True)
            dsc_ref[...] += jnp.sum(dh * (xhat * g), axis=0, keepdims=True)
            dg_ref[...] += jnp.sum(dh * xhat, axis=0, keepdims=True) * one_sc
            dxh = dh * (g * one_sc)
            dx = r * (dxh - xhat * jnp.mean(dxh * xhat, axis=-1, keepdims=True))
            gx_ref[...] = dy_ref[...] + dx

    full = pl.BlockSpec((tm, D_MODEL), lambda i, k: (i, 0))
    vec = pl.BlockSpec((1, D_MODEL), lambda i, k: (0, 0))
    return pl.pallas_call(
        body, name="dh_dx", grid=(s // tm, nk),
        in_specs=[pl.BlockSpec((tm, tk), lambda i, k: (i, k)),
                  pl.BlockSpec((tk, D_MODEL), lambda i, k: (k, 0)),
                  full, full, pl.BlockSpec((3, D_MODEL), lambda i, k: (0, 0)), vec],
        out_specs=[full, vec, vec, vec],
        out_shape=[jax.ShapeDtypeStruct((s, D_MODEL), F32)] + [jax.ShapeDtypeStruct((1, D_MODEL), F32)] * 3,
        scratch_shapes=[pltpu.VMEM((tm, D_MODEL), F32)],
        compiler_params=_params(("arbitrary", "arbitrary")),
    )(dproj, w_all_t, x, dy, ada3, norm_g)


def _sum_small(vec_all, qg_parts, kg_parts):
    def body(v_ref, q_ref, k_ref, tot_ref, gq_ref, gk_ref):
        tot = v_ref[0:1, :]
        for p in range(1, N_DEV):
            tot = tot + v_ref[p:p + 1, :]
        tot_ref[...] = tot
        gq_ref[...] = jnp.sum(q_ref[...], axis=0, keepdims=True)
        gk_ref[...] = jnp.sum(k_ref[...], axis=0, keepdims=True)

    n = vec_all.shape[-1]
    return pl.pallas_call(
        body, name="sum_small",
        out_shape=[jax.ShapeDtypeStruct((1, n), F32),
                   jax.ShapeDtypeStruct((1, HEAD_DIM), F32), jax.ShapeDtypeStruct((1, HEAD_DIM), F32)],
        compiler_params=_params(),
    )(vec_all, qg_parts, kg_parts)


def _grad_w_ada(c_cols, dada_rows):
    def body(c_ref, d_ref, out_ref):
        acc = c_ref[0] * d_ref[0]
        for b in range(1, N_DEV):
            acc = acc + c_ref[b] * d_ref[b]
        out_ref[...] = acc

    return pl.pallas_call(
        body, name="grad_w_ada",
        out_shape=jax.ShapeDtypeStruct((D_MODEL, ADA_SHARD), F32),
        compiler_params=_params(),
    )(c_cols, dada_rows)


def _adamw(w, m, v, g_parts, name):
    rows, cols = w.shape
    n_parts = g_parts.shape[0]
    tr = 256 if rows % 256 == 0 else rows
    c1 = 1.0 / (1.0 - ADAM_B1 ** ADAM_STEP)
    c2 = 1.0 / (1.0 - ADAM_B2 ** ADAM_STEP)

    def body(w_ref, m_ref, v_ref, g_ref, go_ref, d_ref, mo_ref, vo_ref):
        g = g_ref[0].astype(F32)
        for p in range(1, n_parts):
            g = g + g_ref[p].astype(F32)
        m_new = ADAM_B1 * m_ref[...] + (1.0 - ADAM_B1) * g
        v_new = ADAM_B2 * v_ref[...] + (1.0 - ADAM_B2) * (g * g)
        go_ref[...] = g
        mo_ref[...] = m_new
        vo_ref[...] = v_new
        d_ref[...] = -ADAM_LR * ((m_new * c1) / (jnp.sqrt(v_new * c2) + ADAM_EPS) + ADAM_WD * w_ref[...])

    blk = pl.BlockSpec((tr, cols), lambda i: (i, 0))
    return pl.pallas_call(
        body, name=name, grid=(rows // tr,),
        in_specs=[blk, blk, blk, pl.BlockSpec((n_parts, tr, cols), lambda i: (0, i, 0))],
        out_specs=[blk] * 4,
        out_shape=[jax.ShapeDtypeStruct((rows, cols), F32)] * 4,
        compiler_params=_params(("parallel",)),
    )(w, m, v, g_parts)


_O_Q, _O_K, _O_V, _O_F, _O_ZA, _O_GB, _O_GC, _O_U, _O_ZB, _O_GA, _O_GB2 = (
    0, 512, 1024, 1536, 1544, 2056, 2568, 3080, 3592, 4104, 5128)


def _to_internal(w_in_g):
    wf = jnp.transpose(w_in_g, (1, 0, 2)).reshape(D_MODEL, IN_WIDTH)
    cols = lambda a, n: wf[:, a:a + n]
    conv = [cols(base + LANES * c, LANES) for c in range(4) for base in (_O_GB, _O_GC, _O_U, _O_ZB)]
    main = jnp.concatenate([cols(_O_Q, 512), cols(_O_K, 512), cols(_O_V, 512), cols(_O_ZA, 512), *conv,
                            cols(_O_GA, 1024), cols(_O_GB2, 1024)], axis=1)
    f = jnp.pad(cols(_O_F, HEADS), ((0, 0), (0, N_FPAD - HEADS)))
    return main, f


def _from_internal(dw):
    cols = lambda a, n: dw[:, a:a + n]
    conv = lambda k: [cols(OFF_CONV + 4 * LANES * c + LANES * k, LANES) for c in range(4)]
    full = jnp.concatenate([cols(OFF_Q, 512), cols(OFF_K, 512), cols(OFF_V, 512), cols(N_MAIN, HEADS),
                            cols(OFF_ZA, 512), *conv(0), *conv(1), *conv(2), *conv(3),
                            cols(OFF_GA, 1024), cols(OFF_GB, 1024)], axis=1)
    return jnp.transpose(full.reshape(D_MODEL, N_DEV, IN_SHARD), (1, 0, 2))


def kernel(x, c, w_ada, b_ada, norm_g, w_in, b_f, q_norm_g, k_norm_g, conv_w, w_attn_out, w_conv_out, w_o, loss_target, m_w_ada, m_b_ada, m_norm_g, m_w_in, m_b_f, m_q_norm_g, m_k_norm_g, m_conv_w, m_w_attn_out, m_w_conv_out, m_w_o, v_w_ada, v_b_ada, v_norm_g, v_w_in, v_b_f, v_q_norm_g, v_k_norm_g, v_conv_w, v_w_attn_out, v_w_conv_out, v_w_o):
    me = 4 * lax.axis_index("x") + 2 * lax.axis_index("y") + lax.axis_index("c")
    s = x.shape[1]
    x2, t2 = x[0], loss_target[0]

    cw_g, wa_g, wb_g, wo_g, w_in_g = _gather_two_level(
        [conv_w[0], w_attn_out[0].astype(BF16), w_conv_out[0].astype(BF16), w_o[0].astype(BF16),
         w_in[0].astype(BF16)], "gather_weights")
    c_all, ada_g = _ada_exchange(c, w_ada[0])
    ada_mine = lax.dynamic_index_in_dim(ada_g[:, :, 0, :], me, axis=1, keepdims=False)
    ada3 = (ada_mine.reshape(1, 3 * D_MODEL) + b_ada).reshape(3, D_MODEL)
    w_main, w_f = _to_internal(w_in_g)
    w_all_t = jnp.concatenate([w_main, w_f], axis=1).T
    wa = jnp.transpose(wa_g, (1, 0, 2)).reshape(ATTN_W, D_MODEL)
    wb = jnp.transpose(wb_g, (1, 0, 2)).reshape(CONV_W, D_MODEL)
    wo = wo_g.reshape(D_MODEL, D_MODEL)
    cw = jnp.transpose(cw_g, (1, 0, 2)).reshape(3, CONV_W)
    qg = jnp.tile(q_norm_g, (1, HEADS))
    kg = jnp.tile(k_norm_g, (1, HEADS))
    bf_pad = jnp.pad(b_f, ((0, 0), (0, LANES - HEADS)))

    proj, fl, ht = _proj_fwd(x2, ada3, norm_g, w_main, w_f)
    cum = _forget_cumsum(fl, bf_pad)
    qa, ka, va, kt, vt = _qkv_prep(proj, cum, qg, kg)
    attn, oa, qb = _attn_fwd(qa, ka, vt, proj)
    ob = _conv_fwd(proj, cw)
    (dy, dgab, doa, dob, dwo, dwa, dwb, dgate, loss_part) = _tail(oa, ob, proj, x2, t2, ada3, wa, wb, wo)

    do, dza = _attn_bwd_prep(doa, attn, proj)
    dqt, dk, dv = _attn_bwd(qb, ka, kt, va, do)
    dqkv, dqg, dkg, dcum = _qk_norm_bwd(dqt, dk, dv, proj, qg, kg)
    df, dbf = _forget_bwd(dcum, fl, bf_pad)
    dconv4, dcw = _conv_bwd(dob, proj, cw)
    dproj = jnp.concatenate([dqkv, dza, dconv4, dgab, df], axis=1)
    dw_all = _dw_in(ht, dproj)
    grad_x, dshift, dscale, dnormg = _dh_and_dx(dproj, w_all_t, x2, dy, ada3, norm_g)

    def by_core(slabs8):
        return jnp.swapaxes(slabs8.reshape((4, 2) + slabs8.shape[1:]), 0, 1).astype(BF16)

    slabs = [by_core(jnp.transpose(dwa.reshape(ATTN_W, N_DEV, LANES), (1, 0, 2))),
             by_core(jnp.transpose(dwb.reshape(CONV_W, N_DEV, LANES), (1, 0, 2))),
             by_core(dwo.reshape(N_DEV, D_MODEL // N_DEV, D_MODEL)),
             by_core(_from_internal(dw_all))]
    theirs = _sibling_swap(slabs, "swap_grads")
    core = lax.axis_index("c").astype(jnp.int32).reshape(1)
    chip_sums = [_pair_sum(m2, t4, core, "pair_sum_" + nm)
                 for m2, t4, nm in zip(slabs, theirs, ("wa", "wb", "wo", "w_in"))]
    g_wa_parts, g_wb_parts, g_wo_parts, g_in_parts = _chip_exchange(chip_sums, "exchange_grads")
    vec = jnp.concatenate([dshift, dscale, dgate, dnormg, dbf, dcw.reshape(1, 3 * CONV_W), dqg, dkg], axis=1)
    (vec_all,) = _exchange([vec], True, "gather_small")
    vec_all = vec_all.reshape(N_DEV, vec.shape[1])
    n_main = 4 * D_MODEL + LANES + 3 * CONV_W
    tot, g_qg, g_kg = _sum_small(
        vec_all[:, :n_main],
        vec_all[:, n_main:n_main + ATTN_W].reshape(N_DEV * HEADS, HEAD_DIM),
        vec_all[:, n_main + ATTN_W:].reshape(N_DEV * HEADS, HEAD_DIM))
    g_b_ada = tot[:, 0:3 * D_MODEL]
    g_norm_g = tot[:, 3 * D_MODEL:4 * D_MODEL]
    g_b_f = tot[:, 4 * D_MODEL:4 * D_MODEL + HEADS]
    g_cw_full = tot[:, 4 * D_MODEL + LANES:].reshape(3, CONV_W)
    g_cw = lax.dynamic_slice(g_cw_full, (0, me * (CONV_W // N_DEV)), (3, CONV_W // N_DEV))
    dada_mine = lax.dynamic_slice(vec_all[:, 0:3 * D_MODEL], (0, me * ADA_SHARD), (N_DEV, ADA_SHARD))
    g_w_ada = _grad_w_ada(jnp.transpose(c_all, (0, 2, 1)), dada_mine.reshape(N_DEV, 1, ADA_SHARD))

    upd = {}
    upd["w_ada"] = _adamw(w_ada[0], m_w_ada[0], v_w_ada[0], g_w_ada[None], "adamw_w_ada")
    upd["b_ada"] = _adamw(b_ada, m_b_ada, v_b_ada, g_b_ada[None], "adamw_b_ada")
    upd["norm_g"] = _adamw(norm_g, m_norm_g, v_norm_g, g_norm_g[None], "adamw_norm_g")
    upd["w_in"] = _adamw(w_in[0], m_w_in[0], v_w_in[0], g_in_parts, "adamw_w_in")
    upd["b_f"] = _adamw(b_f, m_b_f, v_b_f, g_b_f[None], "adamw_b_f")
    upd["q_norm_g"] = _adamw(q_norm_g, m_q_norm_g, v_q_norm_g, g_qg[None], "adamw_q_norm_g")
    upd["k_norm_g"] = _adamw(k_norm_g, m_k_norm_g, v_k_norm_g, g_kg[None], "adamw_k_norm_g")
    upd["conv_w"] = _adamw(conv_w[0], m_conv_w[0], v_conv_w[0], g_cw[None], "adamw_conv_w")
    upd["w_attn_out"] = _adamw(w_attn_out[0], m_w_attn_out[0], v_w_attn_out[0], g_wa_parts, "adamw_w_attn_out")
    upd["w_conv_out"] = _adamw(w_conv_out[0], m_w_conv_out[0], v_w_conv_out[0], g_wb_parts, "adamw_w_conv_out")
    upd["w_o"] = _adamw(w_o[0], m_w_o[0], v_w_o[0], g_wo_parts, "adamw_w_o")

    names = ["w_ada", "b_ada", "norm_g", "w_in", "b_f", "q_norm_g", "k_norm_g", "conv_w",
             "w_attn_out", "w_conv_out", "w_o"]
    lead = {"w_ada", "w_in", "conv_w", "w_attn_out", "w_conv_out", "w_o"}
    fix = lambda n, a: a[None] if n in lead else a
    loss = lax.psum(loss_part[0, 0], ("x", "y", "c"))
    outs = [loss, grad_x[None]]
    for k in range(4):
        outs += [fix(n, upd[n][k]) for n in names]
    return tuple(outs)
```

```python
import functools

import numpy as np
import jax
import jax.numpy as jnp
from jax import lax
from jax.experimental import pallas as pl
from jax.experimental.pallas import tpu as pltpu

F32 = jnp.float32
BF16 = jnp.bfloat16

D_MODEL = 1024
HEADS = 8
HEAD_DIM = 64
ATTN_W = 512
CONV_W = 512
N_DEV = 8
IN_WIDTH = 6152
IN_SHARD = IN_WIDTH // N_DEV
N_MAIN = 6144
N_FPAD = 128
N_ALL = N_MAIN + N_FPAD
ADA_SHARD = 3 * D_MODEL // N_DEV
EPS = 1e-6
NEG = -1e30

ADAM_LR = 0.001
ADAM_B1 = 0.9
ADAM_B2 = 0.999
ADAM_EPS = 1e-08
ADAM_WD = 0.01
ADAM_STEP = 10

LANES = 128
VMEM_LIMIT = 56 * 1024 * 1024

TM_PROJ = 512
TN_PROJ = 1024
TM_ELEM = 512
TQ = 512
TM_TAIL = 256
TC_CUM = 256
TK_DW = 512
TN_DW = 896
TM_DH = 512
TK_DH = 896

OFF_Q, OFF_K, OFF_V, OFF_ZA, OFF_CONV, OFF_GA, OFF_GB = 0, 512, 1024, 1536, 2048, 4096, 5120


def _params(sem=None):
    return pltpu.CompilerParams(dimension_semantics=sem, vmem_limit_bytes=VMEM_LIMIT)


def _dot(a, b):
    return jnp.dot(a, b, preferred_element_type=F32)


def _dot_nt(a, b):
    return lax.dot_general(a, b, (((1,), (1,)), ((), ())), preferred_element_type=F32)


def _dot_tn(a, b):
    return lax.dot_general(a, b, (((0,), (0,)), ((), ())), preferred_element_type=F32)


def _sigmoid(x):
    return 1.0 / (1.0 + jnp.exp(-x))


def _lane_lo(shape):
    return lax.broadcasted_iota(jnp.int32, shape, len(shape) - 1) < HEAD_DIM


def _seg_sum(z, lo):
    a = jnp.sum(jnp.where(lo, z, 0.0), axis=-1, keepdims=True)
    b = jnp.sum(jnp.where(lo, 0.0, z), axis=-1, keepdims=True)
    return jnp.where(lo, a, b)


def _lane_col(z, lane):
    idx = lax.broadcasted_iota(jnp.int32, z.shape, 1)
    return jnp.sum(jnp.where(idx == lane, z, 0.0), axis=-1, keepdims=True)


def _sub_row(z, row):
    idx = lax.broadcasted_iota(jnp.int32, z.shape, 0)
    return jnp.sum(jnp.where(idx == row, z, 0.0), axis=0, keepdims=True)


def _mesh_pos():
    x, y, c = lax.axis_index("x"), lax.axis_index("y"), lax.axis_index("c")
    return x, y, c, 4 * x + 2 * y + c


def _peer(k, x, y, c):
    px = 1 - x if (k >> 2) & 1 else x
    py = 1 - y if (k >> 1) & 1 else y
    pc = 1 - c if k & 1 else c
    return (px, py, pc), 4 * px + 2 * py + pc


def _exchange(arrs, gather, name):
    n = len(arrs)
    any_spec = pl.BlockSpec(memory_space=pl.ANY)

    def body(*refs):
        ins, outs = refs[:n], refs[n:2 * n]
        send_sems, recv_sems, local_sems = refs[2 * n:]
        x, y, c, me = _mesh_pos()
        copies = []
        for a in range(n):
            own = ins[a] if gather else ins[a].at[me]
            local = pltpu.make_async_copy(own, outs[a].at[me], local_sems.at[a])
            local.start()
            copies.append(local)
            for k in range(1, N_DEV):
                dev, p = _peer(k, x, y, c)
                cp = pltpu.make_async_remote_copy(
                    src_ref=ins[a] if gather else ins[a].at[p],
                    dst_ref=outs[a].at[me],
                    send_sem=send_sems.at[a * (N_DEV - 1) + k - 1],
                    recv_sem=recv_sems.at[a * (N_DEV - 1) + k - 1],
                    device_id=dev, device_id_type=pl.DeviceIdType.MESH)
                cp.start()
                copies.append(cp)
        for cp in copies:
            cp.wait()

    out_shape = [jax.ShapeDtypeStruct((N_DEV,) + a.shape if gather else a.shape, a.dtype) for a in arrs]
    return pl.pallas_call(
        body, name=name, out_shape=out_shape,
        in_specs=[any_spec] * n, out_specs=[any_spec] * n,
        scratch_shapes=[pltpu.SemaphoreType.DMA((n * (N_DEV - 1),)),
                        pltpu.SemaphoreType.DMA((n * (N_DEV - 1),)),
                        pltpu.SemaphoreType.DMA((n,))],
    )(*arrs)


def _gather_two_level(arrs, name):
    n = len(arrs)
    any_spec = pl.BlockSpec(memory_space=pl.ANY)
    per = N_DEV - 1

    def body(*refs):
        ins, outs = refs[:n], refs[n:2 * n]
        send_sems, recv_sems, local_sems = refs[2 * n:]
        x, y, c, me = _mesh_pos()
        sibling = (x, y, 1 - c)
        chips = [(1 - x, y), (x, 1 - y), (1 - x, 1 - y)]

        def copy(a, k, src, blk, to):
            return pltpu.make_async_remote_copy(
                src_ref=src, dst_ref=outs[a].at[blk],
                send_sem=send_sems.at[a * per + k], recv_sem=recv_sems.at[a * per + k],
                device_id=to, device_id_type=pl.DeviceIdType.MESH)

        local = [pltpu.make_async_copy(ins[a], outs[a].at[me], local_sems.at[a]) for a in range(n)]
        for cp in local:
            cp.start()
        first = []
        for a in range(n):
            first.append(copy(a, 0, ins[a], me, sibling))
            first += [copy(a, 1 + j, ins[a], me, (px, py, c)) for j, (px, py) in enumerate(chips)]
        for cp in first:
            cp.start()
        passed = []
        for j, (px, py) in enumerate(chips):
            blk = 4 * px + 2 * py + c
            for a in range(n):
                copy(a, 1 + j, ins[a], blk, (x, y, c)).wait_recv()
                fwd = copy(a, 4 + j, outs[a].at[blk], blk, sibling)
                fwd.start()
                passed.append(fwd)
        for a in range(n):
            copy(a, 0, ins[a], 4 * x + 2 * y + 1 - c, (x, y, c)).wait_recv()
            for j, (px, py) in enumerate(chips):
                copy(a, 4 + j, ins[a], 4 * px + 2 * py + 1 - c, (x, y, c)).wait_recv()
        for cp in first + passed:
            cp.wait_send()
        for cp in local:
            cp.wait()

    return pl.pallas_call(
        body, name=name,
        out_shape=[jax.ShapeDtypeStruct((N_DEV,) + a.shape, a.dtype) for a in arrs],
        in_specs=[any_spec] * n, out_specs=[any_spec] * n,
        scratch_shapes=[pltpu.SemaphoreType.DMA((n * per,)), pltpu.SemaphoreType.DMA((n * per,)),
                        pltpu.SemaphoreType.DMA((n,))],
    )(*arrs)


def _sibling_swap(arrs, name):
    n = len(arrs)
    any_spec = pl.BlockSpec(memory_space=pl.ANY)

    def body(*refs):
        ins, outs = refs[:n], refs[n:2 * n]
        send_sems, recv_sems = refs[2 * n:]
        x, y, c, _ = _mesh_pos()
        copies = [pltpu.make_async_remote_copy(
            src_ref=ins[a].at[1 - c], dst_ref=outs[a], send_sem=send_sems.at[a], recv_sem=recv_sems.at[a],
            device_id=(x, y, 1 - c), device_id_type=pl.DeviceIdType.MESH) for a in range(n)]
        for cp in copies:
            cp.start()
        for cp in copies:
            cp.wait()

    return pl.pallas_call(
        body, name=name,
        out_shape=[jax.ShapeDtypeStruct(a.shape[1:], a.dtype) for a in arrs],
        in_specs=[any_spec] * n, out_specs=[any_spec] * n,
        scratch_shapes=[pltpu.SemaphoreType.DMA((n,)), pltpu.SemaphoreType.DMA((n,))],
    )(*arrs)


def _pair_sum(mine2, theirs, core, name):
    _, _, rows, cols = mine2.shape
    tr = 256 if rows % 256 == 0 else rows

    def body(core_ref, a_ref, b_ref, out_ref):
        out_ref[...] = (a_ref[...].astype(F32) + b_ref[...].astype(F32)).astype(BF16)

    return pl.pallas_call(
        body, name=name,
        grid_spec=pltpu.PrefetchScalarGridSpec(
            num_scalar_prefetch=1, grid=(4, rows // tr),
            in_specs=[pl.BlockSpec((None, None, tr, cols), lambda ch, i, core_: (core_[0], ch, i, 0)),
                      pl.BlockSpec((None, tr, cols), lambda ch, i, core_: (ch, i, 0))],
            out_specs=pl.BlockSpec((None, tr, cols), lambda ch, i, core_: (ch, i, 0))),
        out_shape=jax.ShapeDtypeStruct(theirs.shape, BF16),
        compiler_params=_params(("parallel", "parallel")),
    )(core, mine2, theirs)


def _chip_exchange(arrs, name):
    n = len(arrs)
    any_spec = pl.BlockSpec(memory_space=pl.ANY)

    def body(*refs):
        ins, outs = refs[:n], refs[n:2 * n]
        send_sems, recv_sems, local_sems = refs[2 * n:]
        x, y, c, _ = _mesh_pos()
        my_chip = 2 * x + y
        chips = [(1 - x, y), (x, 1 - y), (1 - x, 1 - y)]
        copies = []
        for a in range(n):
            local = pltpu.make_async_copy(ins[a].at[my_chip], outs[a].at[my_chip], local_sems.at[a])
            local.start()
            copies.append(local)
            for j, (px, py) in enumerate(chips):
                cp = pltpu.make_async_remote_copy(
                    src_ref=ins[a].at[2 * px + py], dst_ref=outs[a].at[my_chip],
                    send_sem=send_sems.at[a * 3 + j], recv_sem=recv_sems.at[a * 3 + j],
                    device_id=(px, py, c), device_id_type=pl.DeviceIdType.MESH)
                cp.start()
                copies.append(cp)
        for cp in copies:
            cp.wait()

    return pl.pallas_call(
        body, name=name,
        out_shape=[jax.ShapeDtypeStruct(a.shape, a.dtype) for a in arrs],
        in_specs=[any_spec] * n, out_specs=[any_spec] * n,
        scratch_shapes=[pltpu.SemaphoreType.DMA((n * 3,)), pltpu.SemaphoreType.DMA((n * 3,)),
                        pltpu.SemaphoreType.DMA((n,))],
    )(*arrs)


def _ada_exchange(c_row, w_ada_sh):
    def body(c_ref, w_ref, call_ref, adag_ref, mine_ref, send_sems, recv_sems):
        x, y, c, me = _mesh_pos()

        def copy(phase, k, src, dst):
            dev, _ = _peer(k, x, y, c)
            return pltpu.make_async_remote_copy(
                src_ref=src, dst_ref=dst,
                send_sem=send_sems.at[phase * (N_DEV - 1) + k - 1],
                recv_sem=recv_sems.at[phase * (N_DEV - 1) + k - 1],
                device_id=dev, device_id_type=pl.DeviceIdType.MESH)

        call_ref[me] = c_ref[...]
        first = [copy(0, k, c_ref, call_ref.at[me]) for k in range(1, N_DEV)]
        for cp in first:
            cp.start()
        for cp in first:
            cp.wait()
        wb = w_ref[...].astype(BF16)
        for b in range(N_DEV):
            row = jnp.broadcast_to(call_ref[b], (8, D_MODEL)).astype(BF16)
            mine_ref[b] = _sub_row(_dot(row, wb), 0)
        adag_ref[me] = mine_ref[...]
        second = [copy(1, k, mine_ref, adag_ref.at[me]) for k in range(1, N_DEV)]
        for cp in second:
            cp.start()
        for cp in second:
            cp.wait()

    vm = pl.BlockSpec(memory_space=pltpu.VMEM)
    return pl.pallas_call(
        body, name="ada_exchange",
        out_shape=[jax.ShapeDtypeStruct((N_DEV, 1, D_MODEL), F32),
                   jax.ShapeDtypeStruct((N_DEV, N_DEV, 1, ADA_SHARD), F32)],
        in_specs=[vm, vm], out_specs=[vm, vm],
        scratch_shapes=[pltpu.VMEM((N_DEV, 1, ADA_SHARD), F32),
                        pltpu.SemaphoreType.DMA((2 * (N_DEV - 1),)),
                        pltpu.SemaphoreType.DMA((2 * (N_DEV - 1),))],
        compiler_params=pltpu.CompilerParams(vmem_limit_bytes=VMEM_LIMIT),
    )(c_row, w_ada_sh)


def _proj_fwd(x, ada3, norm_g, w_all_t):
    s = x.shape[0]
    tm, tn = min(TM_PROJ, s), TN_PROJ

    def body(x_ref, ada_ref, g_ref, w_ref, wf_ref, proj_ref, fl_ref, h_ref):
        @pl.when(pl.program_id(1) == 0)
        def _():
            xv = x_ref[...]
            r = lax.rsqrt(jnp.mean(xv * xv, axis=-1, keepdims=True) + EPS)
            hv = ((xv * r) * g_ref[...]) * (1.0 + ada_ref[1:2, :]) + ada_ref[0:1, :]
            hb = hv.astype(BF16)
            h_ref[...] = hb
            fl_ref[...] = _dot_nt(hb, wf_ref[...])
        proj_ref[...] = _dot_nt(h_ref[...], w_ref[...])

    return pl.pallas_call(
        body, name="proj_fwd", grid=(s // tm, N_MAIN // tn),
        in_specs=[pl.BlockSpec((tm, D_MODEL), lambda i, j: (i, 0)),
                  pl.BlockSpec((3, D_MODEL), lambda i, j: (0, 0)),
                  pl.BlockSpec((1, D_MODEL), lambda i, j: (0, 0)),
                  pl.BlockSpec((tn, D_MODEL), lambda i, j: (j, 0)),
                  pl.BlockSpec((N_FPAD, D_MODEL), lambda i, j: (N_MAIN // N_FPAD, 0))],
        out_specs=[pl.BlockSpec((tm, tn), lambda i, j: (i, j)),
                   pl.BlockSpec((tm, N_FPAD), lambda i, j: (i, 0)),
                   pl.BlockSpec((tm, D_MODEL), lambda i, j: (i, 0))],
        out_shape=[jax.ShapeDtypeStruct((s, N_MAIN), F32),
                   jax.ShapeDtypeStruct((s, N_FPAD), F32),
                   jax.ShapeDtypeStruct((s, D_MODEL), BF16)],
        compiler_params=_params(("parallel", "arbitrary")),
    )(x, ada3, norm_g, w_all_t, w_all_t)


L_ONE_Q, L_F_Q, L_LSE_Q, L_END = HEAD_DIM, HEAD_DIM + 3, HEAD_DIM + 6, HEAD_DIM + 9


def _split3(f):
    hi = f.astype(BF16).astype(F32)
    r = f - hi
    mid = r.astype(BF16).astype(F32)
    return hi, mid, r - mid


def _place3(lane, first, parts, otherwise):
    a, b, c = parts
    return jnp.where(lane == first, a, jnp.where(lane == first + 1, b, jnp.where(lane == first + 2, c, otherwise)))


def _qkv_prep(proj, cum, qg, kg):
    s = proj.shape[0]
    tm = min(TM_ELEM, s)
    scale = HEAD_DIM ** -0.5

    def body(p_ref, cum_ref, qg_ref, kg_ref, qa_ref, ka_ref, va_ref, kt_ref, vt_ref):
        lane = lax.broadcasted_iota(jnp.int32, (tm, LANES), 1)
        lo = lane < HEAD_DIM
        cum_v = cum_ref[...]
        v_tail = jnp.where(lane < L_F_Q, 1.0, 0.0)
        for pr in range(ATTN_W // LANES):
            sl = slice(pr * LANES, (pr + 1) * LANES)
            q2 = p_ref[:, OFF_Q + pr * LANES:OFF_Q + (pr + 1) * LANES]
            k2 = p_ref[:, OFF_K + pr * LANES:OFF_K + (pr + 1) * LANES]
            v2 = p_ref[:, OFF_V + pr * LANES:OFF_V + (pr + 1) * LANES]
            rq = lax.rsqrt(_seg_sum(q2 * q2, lo) * (1.0 / HEAD_DIM) + EPS)
            rk = lax.rsqrt(_seg_sum(k2 * k2, lo) * (1.0 / HEAD_DIM) + EPS)
            qn = ((q2 * rq) * qg_ref[:, sl]) * scale
            kn = (k2 * rk) * kg_ref[:, sl]
            for hh in range(2):
                h = 2 * pr + hh
                f3 = _split3(_lane_col(cum_v, h))
                qh = qn if hh == 0 else pltpu.roll(qn, HEAD_DIM, 1)
                kh = kn if hh == 0 else pltpu.roll(kn, HEAD_DIM, 1)
                vh = v2 if hh == 0 else pltpu.roll(v2, HEAD_DIM, 1)
                q_tail = jnp.where(lane < L_F_Q, 1.0, _place3(lane, L_F_Q, f3, 0.0))
                k_tail = _place3(lane, L_ONE_Q, tuple(-f for f in f3), jnp.where(lane < L_END, 1.0, 0.0))
                k_row = jnp.where(lo, kh, k_tail)
                v_row = jnp.where(lo, vh, v_tail)
                qa_ref[h] = jnp.where(lo, qh, q_tail).astype(BF16)
                ka_ref[h] = k_row.astype(BF16)
                va_ref[h] = v_row.astype(BF16)
                kt_ref[h] = k_row.T.astype(BF16)
                vt_ref[h] = v_row.T.astype(BF16)

    heads = pl.BlockSpec((HEADS, tm, LANES), lambda i: (0, i, 0))
    heads_t = pl.BlockSpec((HEADS, LANES, tm), lambda i: (0, 0, i))
    vec = pl.BlockSpec((1, ATTN_W), lambda i: (0, 0))
    return pl.pallas_call(
        body, name="qkv_prep", grid=(s // tm,),
        in_specs=[pl.BlockSpec((tm, 3 * ATTN_W), lambda i: (i, 0)),
                  pl.BlockSpec((tm, LANES), lambda i: (i, 0)), vec, vec],
        out_specs=[heads, heads, heads, heads_t, heads_t],
        out_shape=[jax.ShapeDtypeStruct((HEADS, s, LANES), BF16)] * 3
        + [jax.ShapeDtypeStruct((HEADS, LANES, s), BF16)] * 2,
        compiler_params=_params(("parallel",)),
    )(proj, cum, qg, kg)


def _log_forget(fl, bf):
    z = fl + bf
    lf = jnp.minimum(z, 0.0) - jnp.log1p(jnp.exp(-jnp.abs(z)))
    lane = lax.broadcasted_iota(jnp.int32, z.shape, 1)
    return jnp.where(lane < HEADS, lf, 0.0)


def _forget_cumsum(fl, bf_pad):
    s = fl.shape[0]
    tc = min(TC_CUM, s)

    def body(fl_ref, bf_ref, cum_ref, carry):
        @pl.when(pl.program_id(0) == 0)
        def _():
            carry[...] = jnp.zeros_like(carry)
        lf = _log_forget(fl_ref[...], bf_ref[...])
        r = lax.broadcasted_iota(jnp.int32, (tc, tc), 0)
        cidx = lax.broadcasted_iota(jnp.int32, (tc, tc), 1)
        tri = (cidx <= r).astype(F32)
        cs = jnp.dot(tri, lf, preferred_element_type=F32, precision=lax.Precision.HIGHEST) + carry[...]
        cum_ref[...] = cs
        carry[...] = cum_ref[tc - 1:tc, :]

    return pl.pallas_call(
        body, name="forget_cumsum", grid=(s // tc,),
        in_specs=[pl.BlockSpec((tc, LANES), lambda i: (i, 0)),
                  pl.BlockSpec((1, LANES), lambda i: (0, 0))],
        out_specs=pl.BlockSpec((tc, LANES), lambda i: (i, 0)),
        out_shape=jax.ShapeDtypeStruct((s, LANES), F32),
        scratch_shapes=[pltpu.VMEM((1, LANES), F32)],
        compiler_params=_params(("arbitrary",)),
    )(fl, bf_pad)


def _causal_t(t):
    return lax.broadcasted_iota(jnp.int32, (t, t), 0) <= lax.broadcasted_iota(jnp.int32, (t, t), 1)


def _tri_steps(nt, q_major):
    if q_major:
        pairs = [(i, j) for i in range(nt) for j in range(i + 1)]
    else:
        pairs = [(i, j) for j in range(nt) for i in range(j, nt)]
    return (jnp.asarray(np.array([p[0] for p in pairs], np.int32)),
            jnp.asarray(np.array([p[1] for p in pairs], np.int32)))


def _attn_fwd(qa, ka, vt, proj):
    s = qa.shape[1]
    t = min(TQ, s)
    it, jt = _tri_steps(s // t, True)
    za_blk = OFF_ZA // LANES

    def body(it_ref, jt_ref, q_ref, k_ref, vt_ref, za_ref, attn_ref, oa_ref, qb_ref, m_s, acc_s, pair_s):
        step = pl.program_id(1)
        i, j = it_ref[step], jt_ref[step]

        @pl.when(j == 0)
        def _():
            m_s[...] = jnp.full_like(m_s, NEG)
            acc_s[...] = jnp.zeros_like(acc_s)

        def update(masked):
            for hh in range(2):
                st = _dot_nt(k_ref[hh], q_ref[hh])
                if masked:
                    st = jnp.where(_causal_t(t), st, NEG)
                m_prev = m_s[hh]
                m_next = jnp.maximum(m_prev, jnp.max(st, axis=0, keepdims=True))
                alpha = jnp.exp(m_prev - m_next)
                pt = jnp.exp(st - m_next).astype(BF16)
                acc_s[hh] = acc_s[hh] * alpha + _dot(vt_ref[hh], pt)
                m_s[hh] = m_next

        @pl.when(j < i)
        def _():
            update(False)

        @pl.when(j == i)
        def _():
            update(True)
            row = lax.broadcasted_iota(jnp.int32, (LANES, t), 0)
            lane = lax.broadcasted_iota(jnp.int32, (t, LANES), 1)
            for hh in range(2):
                l_row = acc_s[hh, L_ONE_Q:L_ONE_Q + 1, :]
                pair_s[hh * HEAD_DIM:(hh + 1) * HEAD_DIM, :] = acc_s[hh, 0:HEAD_DIM, :] / l_row
                lse3 = _split3(m_s[hh] + jnp.log(l_row))
                tail_t = _place3(row, L_LSE_Q, tuple(-x for x in lse3), 0.0)
                keep_q = jnp.logical_or(lane < L_LSE_Q, lane >= L_END)
                qb_ref[hh] = jnp.where(keep_q, q_ref[hh].astype(F32), tail_t.T).astype(BF16)
            out = pair_s[...].T
            attn_ref[...] = out
            z = za_ref[...]
            oa_ref[...] = (out * (z * _sigmoid(z))).astype(BF16)

    pair_q = pl.BlockSpec((2, t, LANES), lambda p, n, it_, jt_: (p, it_[n], 0))
    pair_k = pl.BlockSpec((2, t, LANES), lambda p, n, it_, jt_: (p, jt_[n], 0))
    pair_kt = pl.BlockSpec((2, LANES, t), lambda p, n, it_, jt_: (p, 0, jt_[n]))
    out_q = pl.BlockSpec((t, LANES), lambda p, n, it_, jt_: (it_[n], p))
    return pl.pallas_call(
        body, name="attn_fwd",
        grid_spec=pltpu.PrefetchScalarGridSpec(
            num_scalar_prefetch=2, grid=(ATTN_W // LANES, it.shape[0]),
            in_specs=[pair_q, pair_k, pair_kt,
                      pl.BlockSpec((t, LANES), lambda p, n, it_, jt_: (it_[n], za_blk + p))],
            out_specs=[out_q, out_q, pair_q],
            scratch_shapes=[pltpu.VMEM((2, 1, t), F32), pltpu.VMEM((2, LANES, t), F32),
                            pltpu.VMEM((LANES, t), F32)]),
        out_shape=[jax.ShapeDtypeStruct((s, ATTN_W), F32),
                   jax.ShapeDtypeStruct((s, ATTN_W), BF16),
                   jax.ShapeDtypeStruct((HEADS, s, LANES), BF16)],
        compiler_params=_params(("parallel", "arbitrary")),
    )(it, jt, qa, ka, vt, proj)


def _conv_parts(blk, halo, first, w_ref, tm):
    gb, gc = blk[:, 0:LANES], blk[:, LANES:2 * LANES]
    u, zb = blk[:, 2 * LANES:3 * LANES], blk[:, 3 * LANES:4 * LANES]
    cu = gc * u
    cu_h = jnp.where(first, 0.0, halo[:, LANES:2 * LANES] * halo[:, 2 * LANES:3 * LANES])
    prev1, prev2 = _sub_row(cu_h, 7), _sub_row(cu_h, 6)
    row = lax.broadcasted_iota(jnp.int32, (tm, LANES), 0)
    r1 = jnp.where(row == 0, prev1, pltpu.roll(cu, 1, 0))
    r2 = jnp.where(row == 0, prev2, jnp.where(row == 1, prev1, pltpu.roll(cu, 2, 0)))
    conv = w_ref[2:3, :] * cu + w_ref[1:2, :] * r1 + w_ref[0:1, :] * r2
    return gb, gc, u, zb, cu, r1, r2, conv


def _conv_fwd(proj, conv_w):
    s = proj.shape[0]
    tm = min(TM_ELEM, s)
    cb = OFF_CONV // (4 * LANES)

    def body(p_ref, halo_ref, w_ref, ob_ref):
        first = pl.program_id(1) == 0
        gb, _, _, zb, _, _, _, conv = _conv_parts(p_ref[...], halo_ref[...], first, w_ref, tm)
        ob_ref[...] = (gb * conv * (zb * _sigmoid(zb))).astype(BF16)

    return pl.pallas_call(
        body, name="conv_fwd", grid=(CONV_W // LANES, s // tm),
        in_specs=[pl.BlockSpec((tm, 4 * LANES), lambda c, i: (i, cb + c)),
                  pl.BlockSpec((8, 4 * LANES), lambda c, i: (jnp.maximum(i * (tm // 8) - 1, 0), cb + c)),
                  pl.BlockSpec((3, LANES), lambda c, i: (0, c))],
        out_specs=pl.BlockSpec((tm, LANES), lambda c, i: (i, c)),
        out_shape=jax.ShapeDtypeStruct((s, CONV_W), BF16),
        compiler_params=_params(("parallel", "parallel")),
    )(proj, proj, conv_w)


def _tail(oa, ob, proj, x, target, ada3, wa, wb, wo):
    s = x.shape[0]
    tm = min(TM_TAIL, s)
    gab_blk = OFF_GA // (2 * D_MODEL)

    def body(oa_ref, ob_ref, gab_ref, x_ref, t_ref, ada_ref, wa_ref, wb_ref, wo_ref,
             dy_ref, dgab_ref, doa_ref, dob_ref, dwo_ref, dwa_ref, dwb_ref, dgate_ref, loss_ref):
        @pl.when(pl.program_id(0) == 0)
        def _():
            dwo_ref[...] = jnp.zeros_like(dwo_ref)
            dwa_ref[...] = jnp.zeros_like(dwa_ref)
            dwb_ref[...] = jnp.zeros_like(dwb_ref)
            dgate_ref[...] = jnp.zeros_like(dgate_ref)
            loss_ref[...] = jnp.zeros_like(loss_ref)

        oa_v, ob_v = oa_ref[...], ob_ref[...]
        wa_v, wb_v, wo_v = wa_ref[...], wb_ref[...], wo_ref[...]
        a2 = _dot(oa_v, wa_v)
        b2 = _dot(ob_v, wb_v)
        sa = _sigmoid(gab_ref[:, 0:D_MODEL])
        sb = _sigmoid(gab_ref[:, D_MODEL:2 * D_MODEL])
        mb = (sa * a2 + sb * b2).astype(BF16)
        mo = _dot(mb, wo_v)
        gate = ada_ref[2:3, :]
        err = (x_ref[...] + gate * mo) - t_ref[...]
        dy = err * (1.0 / D_MODEL)
        dy_ref[...] = dy
        loss_ref[...] += 0.5 * jnp.sum(err * err) * (1.0 / D_MODEL)
        dgate_ref[...] += jnp.sum(dy * mo, axis=0, keepdims=True)
        dmo = (dy * gate).astype(BF16)
        dmerged = _dot_nt(dmo, wo_v)
        dwo_ref[...] += _dot_tn(mb, dmo)
        da2 = (dmerged * sa).astype(BF16)
        db2 = (dmerged * sb).astype(BF16)
        dgab_ref[:, 0:D_MODEL] = (dmerged * a2 * (sa * (1.0 - sa))).astype(BF16)
        dgab_ref[:, D_MODEL:2 * D_MODEL] = (dmerged * b2 * (sb * (1.0 - sb))).astype(BF16)
        doa_ref[...] = _dot_nt(da2, wa_v)
        dob_ref[...] = _dot_nt(db2, wb_v)
        dwa_ref[...] += _dot_tn(oa_v, da2)
        dwb_ref[...] += _dot_tn(ob_v, db2)

    half = pl.BlockSpec((tm, ATTN_W), lambda i: (i, 0))
    full = pl.BlockSpec((tm, D_MODEL), lambda i: (i, 0))

    def const(shape):
        return pl.BlockSpec(shape, lambda i: (0, 0))

    return pl.pallas_call(
        body, name="tail", grid=(s // tm,),
        in_specs=[half, half, pl.BlockSpec((tm, 2 * D_MODEL), lambda i: (i, gab_blk)), full, full,
                  const((3, D_MODEL)), const((ATTN_W, D_MODEL)), const((CONV_W, D_MODEL)),
                  const((D_MODEL, D_MODEL))],
        out_specs=[full, pl.BlockSpec((tm, 2 * D_MODEL), lambda i: (i, 0)), half, half,
                   const((D_MODEL, D_MODEL)), const((ATTN_W, D_MODEL)), const((CONV_W, D_MODEL)),
                   const((1, D_MODEL)), const((1, LANES))],
        out_shape=[jax.ShapeDtypeStruct((s, D_MODEL), F32),
                   jax.ShapeDtypeStruct((s, 2 * D_MODEL), BF16),
                   jax.ShapeDtypeStruct((s, ATTN_W), F32),
                   jax.ShapeDtypeStruct((s, CONV_W), F32),
                   jax.ShapeDtypeStruct((D_MODEL, D_MODEL), F32),
                   jax.ShapeDtypeStruct((ATTN_W, D_MODEL), F32),
                   jax.ShapeDtypeStruct((CONV_W, D_MODEL), F32),
                   jax.ShapeDtypeStruct((1, D_MODEL), F32),
                   jax.ShapeDtypeStruct((1, LANES), F32)],
        compiler_params=_params(("arbitrary",)),
    )(oa, ob, proj, x, target, ada3, wa, wb, wo)


def _attn_bwd_prep(doa, attn, proj):
    s = doa.shape[0]
    tm = min(TM_ELEM, s)
    za_blk = OFF_ZA // ATTN_W

    def body(doa_ref, attn_ref, za_ref, do_ref, dza_ref):
        lane = lax.broadcasted_iota(jnp.int32, (tm, LANES), 1)
        lo = lane < HEAD_DIM
        for pr in range(ATTN_W // LANES):
            sl = slice(pr * LANES, (pr + 1) * LANES)
            g, a, z = doa_ref[:, sl], attn_ref[:, sl], za_ref[:, sl]
            sg = _sigmoid(z)
            dat = (g * (z * sg)).astype(BF16).astype(F32)
            prod = dat * a
            dza_ref[:, sl] = (g * a * (sg * (1.0 + z * (1.0 - sg)))).astype(BF16)
            for hh in range(2):
                sel = lo if hh == 0 else jnp.logical_not(lo)
                delta3 = _split3(jnp.sum(jnp.where(sel, prod, 0.0), axis=-1, keepdims=True))
                dh = dat if hh == 0 else pltpu.roll(dat, HEAD_DIM, 1)
                tail = _place3(lane, L_ONE_Q, tuple(-d for d in delta3), 0.0)
                do_ref[2 * pr + hh] = jnp.where(lo, dh, tail).astype(BF16)

    row = pl.BlockSpec((tm, ATTN_W), lambda i: (i, 0))
    return pl.pallas_call(
        body, name="attn_bwd_prep", grid=(s // tm,),
        in_specs=[row, row, pl.BlockSpec((tm, ATTN_W), lambda i: (i, za_blk))],
        out_specs=[pl.BlockSpec((HEADS, tm, LANES), lambda i: (0, i, 0)), row],
        out_shape=[jax.ShapeDtypeStruct((HEADS, s, LANES), BF16),
                   jax.ShapeDtypeStruct((s, ATTN_W), BF16)],
        compiler_params=_params(("parallel",)),
    )(doa, attn, proj)


def _attn_bwd(qb, ka, kt, va, do):
    s = qb.shape[1]
    t = min(TQ, s)
    nt = s // t
    it, jt = _tri_steps(nt, False)

    def body(it_ref, jt_ref, q_ref, k_ref, kt_ref, v_ref, do_ref, dqt_ref, dk_ref, dv_ref):
        step = pl.program_id(1)
        i, j = it_ref[step], jt_ref[step]

        @pl.when(step == 0)
        def _():
            dqt_ref[...] = jnp.zeros_like(dqt_ref)

        @pl.when(i == j)
        def _():
            dk_ref[...] = jnp.zeros_like(dk_ref)
            dv_ref[...] = jnp.zeros_like(dv_ref)

        def update(masked):
            for hh in range(2):
                qh, doh = q_ref[hh], do_ref[hh]
                st = _dot_nt(k_ref[hh], qh)
                if masked:
                    st = jnp.where(_causal_t(t), st, NEG)
                pt = jnp.exp(st)
                dst = (pt * _dot_nt(v_ref[hh], doh)).astype(BF16)
                dv_ref[hh] += _dot(pt.astype(BF16), doh)
                dk_ref[hh] += _dot(dst, qh)
                dqt_ref[hh, i] += _dot(kt_ref[hh], dst)

        @pl.when(i > j)
        def _():
            update(False)

        @pl.when(i == j)
        def _():
            update(True)

    pair_q = pl.BlockSpec((2, t, LANES), lambda p, n, it_, jt_: (p, it_[n], 0))
    pair_k = pl.BlockSpec((2, t, LANES), lambda p, n, it_, jt_: (p, jt_[n], 0))
    pair_kt = pl.BlockSpec((2, LANES, t), lambda p, n, it_, jt_: (p, 0, jt_[n]))
    return pl.pallas_call(
        body, name="attn_bwd",
        grid_spec=pltpu.PrefetchScalarGridSpec(
            num_scalar_prefetch=2, grid=(ATTN_W // LANES, it.shape[0]),
            in_specs=[pair_q, pair_k, pair_kt, pair_k, pair_q],
            out_specs=[pl.BlockSpec((2, nt, LANES, t), lambda p, n, it_, jt_: (p, 0, 0, 0)),
                       pair_k, pair_k]),
        out_shape=[jax.ShapeDtypeStruct((HEADS, nt, LANES, t), F32),
                   jax.ShapeDtypeStruct((HEADS, s, LANES), F32),
                   jax.ShapeDtypeStruct((HEADS, s, LANES), F32)],
        compiler_params=_params(("parallel", "arbitrary")),
    )(it, jt, qb, ka, kt, va, do)


def _forget_bwd(dcum, fl, bf_pad):
    s = fl.shape[0]
    tc = min(TC_CUM, s)
    n = s // tc

    def body(dc_ref, fl_ref, bf_ref, df_ref, dbf_ref, carry):
        @pl.when(pl.program_id(0) == 0)
        def _():
            carry[...] = jnp.zeros_like(carry)
            dbf_ref[...] = jnp.zeros_like(dbf_ref)
        r = lax.broadcasted_iota(jnp.int32, (tc, tc), 0)
        cidx = lax.broadcasted_iota(jnp.int32, (tc, tc), 1)
        tri = (cidx >= r).astype(F32)
        dc = dc_ref[...]
        dlf = jnp.dot(tri, dc, preferred_element_type=F32, precision=lax.Precision.HIGHEST) + carry[...]
        carry[...] += jnp.sum(dc, axis=0, keepdims=True)
        lane = lax.broadcasted_iota(jnp.int32, (tc, LANES), 1)
        dfl = jnp.where(lane < HEADS, dlf * _sigmoid(-(fl_ref[...] + bf_ref[...])), 0.0)
        df_ref[...] = dfl.astype(BF16)
        dbf_ref[...] += jnp.sum(dfl, axis=0, keepdims=True)

    rev = pl.BlockSpec((tc, LANES), lambda i: (n - 1 - i, 0))
    vec = pl.BlockSpec((1, LANES), lambda i: (0, 0))
    return pl.pallas_call(
        body, name="forget_bwd", grid=(n,),
        in_specs=[rev, rev, vec], out_specs=[rev, vec],
        out_shape=[jax.ShapeDtypeStruct((s, LANES), BF16), jax.ShapeDtypeStruct((1, LANES), F32)],
        scratch_shapes=[pltpu.VMEM((1, LANES), F32)],
        compiler_params=_params(("arbitrary",)),
    )(dcum, fl, bf_pad)


def _qk_norm_bwd(dqt, dk, dv, proj, qg, kg):
    s = dv.shape[1]
    tm = dqt.shape[-1]
    scale = HEAD_DIM ** -0.5

    def body(dqt_ref, dk_ref, dv_ref, p_ref, qg_ref, kg_ref, out_ref, dqg_ref, dkg_ref, dcum_ref):
        @pl.when(pl.program_id(0) == 0)
        def _():
            dqg_ref[...] = jnp.zeros_like(dqg_ref)
            dkg_ref[...] = jnp.zeros_like(dkg_ref)
        lane = lax.broadcasted_iota(jnp.int32, (tm, LANES), 1)
        lo = lane < HEAD_DIM
        dq_rows = [dqt_ref[h, 0].T for h in range(HEADS)]
        dcum = jnp.zeros((tm, LANES), F32)
        for h in range(HEADS):
            dcum = jnp.where(lane == h, _lane_col(dq_rows[h], L_F_Q) - _lane_col(dk_ref[h], L_ONE_Q), dcum)
        dcum_ref[...] = dcum

        def pair(a, b):
            return jnp.where(lo, a, pltpu.roll(b, HEAD_DIM, 1))

        def one(raw, dy, g, dg_ref, sl, off):
            r = lax.rsqrt(_seg_sum(raw * raw, lo) * (1.0 / HEAD_DIM) + EPS)
            xhat = raw * r
            dg_ref[:, sl] += jnp.sum(dy * xhat, axis=0, keepdims=True)
            dxh = dy * g
            dx = r * (dxh - xhat * (_seg_sum(dxh * xhat, lo) * (1.0 / HEAD_DIM)))
            out_ref[:, off + sl.start:off + sl.stop] = dx.astype(BF16)

        for pr in range(ATTN_W // LANES):
            sl = slice(pr * LANES, (pr + 1) * LANES)
            dq2 = pair(dq_rows[2 * pr], dq_rows[2 * pr + 1])
            one(p_ref[:, OFF_Q + sl.start:OFF_Q + sl.stop], dq2 * scale, qg_ref[:, sl], dqg_ref, sl, OFF_Q)
            one(p_ref[:, OFF_K + sl.start:OFF_K + sl.stop], pair(dk_ref[2 * pr], dk_ref[2 * pr + 1]),
                kg_ref[:, sl], dkg_ref, sl, OFF_K)
            out_ref[:, OFF_V + sl.start:OFF_V + sl.stop] = pair(dv_ref[2 * pr], dv_ref[2 * pr + 1]).astype(BF16)

    heads = pl.BlockSpec((HEADS, tm, LANES), lambda i: (0, i, 0))
    vec = pl.BlockSpec((1, ATTN_W), lambda i: (0, 0))
    return pl.pallas_call(
        body, name="qk_norm_bwd", grid=(s // tm,),
        in_specs=[pl.BlockSpec((HEADS, 1, LANES, tm), lambda i: (0, i, 0, 0)), heads, heads,
                  pl.BlockSpec((tm, 2 * ATTN_W), lambda i: (i, 0)), vec, vec],
        out_specs=[pl.BlockSpec((tm, 3 * ATTN_W), lambda i: (i, 0)), vec, vec,
                   pl.BlockSpec((tm, LANES), lambda i: (i, 0))],
        out_shape=[jax.ShapeDtypeStruct((s, 3 * ATTN_W), BF16),
                   jax.ShapeDtypeStruct((1, ATTN_W), F32), jax.ShapeDtypeStruct((1, ATTN_W), F32),
                   jax.ShapeDtypeStruct((s, LANES), F32)],
        compiler_params=_params(("arbitrary",)),
    )(dqt, dk, dv, proj, qg, kg)


def _conv_bwd(dob, proj, conv_w):
    s = dob.shape[0]
    tm = min(TM_ELEM, s)
    cb = OFF_CONV // (4 * LANES)
    nblk8 = s // 8

    def body(dob_ref, p_ref, prev_ref, next_ref, dnext_ref, w_ref, out_ref, dw_ref):
        i = pl.program_id(1)

        @pl.when(i == 0)
        def _():
            dw_ref[...] = jnp.zeros_like(dw_ref)
        gb, gc, u, zb, cu, r1, r2, conv = _conv_parts(p_ref[...], prev_ref[...], i == 0, w_ref, tm)
        g = dob_ref[...]
        sg = _sigmoid(zb)
        sz = zb * sg
        dconv = g * gb * sz
        nxt = next_ref[...]
        zn = nxt[:, 3 * LANES:4 * LANES]
        dcn = jnp.where(i == pl.num_programs(1) - 1, 0.0,
                        dnext_ref[...] * nxt[:, 0:LANES] * (zn * _sigmoid(zn)))
        nxt1, nxt2 = _sub_row(dcn, 0), _sub_row(dcn, 1)
        row = lax.broadcasted_iota(jnp.int32, (tm, LANES), 0)
        f1 = jnp.where(row == tm - 1, nxt1, pltpu.roll(dconv, tm - 1, 0))
        f2 = jnp.where(row == tm - 2, nxt1, jnp.where(row == tm - 1, nxt2, pltpu.roll(dconv, tm - 2, 0)))
        dcu = w_ref[2:3, :] * dconv + w_ref[1:2, :] * f1 + w_ref[0:1, :] * f2
        out_ref[:, 0:LANES] = (g * conv * sz).astype(BF16)
        out_ref[:, LANES:2 * LANES] = (dcu * u).astype(BF16)
        out_ref[:, 2 * LANES:3 * LANES] = (dcu * gc).astype(BF16)
        out_ref[:, 3 * LANES:4 * LANES] = (g * gb * conv * (sg * (1.0 + zb * (1.0 - sg)))).astype(BF16)
        w_row = lax.broadcasted_iota(jnp.int32, (3, LANES), 0)
        dw0 = jnp.sum(dconv * r2, axis=0, keepdims=True)
        dw1 = jnp.sum(dconv * r1, axis=0, keepdims=True)
        dw2 = jnp.sum(dconv * cu, axis=0, keepdims=True)
        dw_ref[...] += jnp.where(w_row == 0, dw0, jnp.where(w_row == 1, dw1, dw2))

    nxt_idx = lambda i: jnp.minimum((i + 1) * (tm // 8), nblk8 - 1)
    return pl.pallas_call(
        body, name="conv_bwd", grid=(CONV_W // LANES, s // tm),
        in_specs=[pl.BlockSpec((tm, LANES), lambda c, i: (i, c)),
                  pl.BlockSpec((tm, 4 * LANES), lambda c, i: (i, cb + c)),
                  pl.BlockSpec((8, 4 * LANES), lambda c, i: (jnp.maximum(i * (tm // 8) - 1, 0), cb + c)),
                  pl.BlockSpec((8, 4 * LANES), lambda c, i: (nxt_idx(i), cb + c)),
                  pl.BlockSpec((8, LANES), lambda c, i: (nxt_idx(i), c)),
                  pl.BlockSpec((3, LANES), lambda c, i: (0, c))],
        out_specs=[pl.BlockSpec((tm, 4 * LANES), lambda c, i: (i, c)),
                   pl.BlockSpec((3, LANES), lambda c, i: (0, c))],
        out_shape=[jax.ShapeDtypeStruct((s, 4 * CONV_W), BF16), jax.ShapeDtypeStruct((3, CONV_W), F32)],
        compiler_params=_params(("parallel", "arbitrary")),
    )(dob, proj, proj, proj, dob, conv_w)


def _dw_in(h, dproj):
    s = h.shape[0]
    tk, tn = min(TK_DW, s), TN_DW

    def body(dp_ref, h_ref, out_ref):
        @pl.when(pl.program_id(1) == 0)
        def _():
            out_ref[...] = jnp.zeros_like(out_ref)
        out_ref[...] += _dot_tn(dp_ref[...], h_ref[...])

    return pl.pallas_call(
        body, name="dw_in", grid=(N_ALL // tn, s // tk),
        in_specs=[pl.BlockSpec((tk, tn), lambda n, k: (k, n)),
                  pl.BlockSpec((tk, D_MODEL), lambda n, k: (k, 0))],
        out_specs=pl.BlockSpec((tn, D_MODEL), lambda n, k: (n, 0)),
        out_shape=jax.ShapeDtypeStruct((N_ALL, D_MODEL), F32),
        compiler_params=_params(("parallel", "arbitrary")),
    )(dproj, h)


def _dh_and_dx(dproj, w_all_t, x, dy, ada3, norm_g):
    s = x.shape[0]
    tm, tk = min(TM_DH, s), TK_DH
    nk = N_ALL // tk

    def body(dp_ref, wt_ref, x_ref, dy_ref, ada_ref, g_ref, gx_ref, dsh_ref, dsc_ref, dg_ref, acc):
        i, k = pl.program_id(0), pl.program_id(1)

        @pl.when(jnp.logical_and(i == 0, k == 0))
        def _():
            dsh_ref[...] = jnp.zeros_like(dsh_ref)
            dsc_ref[...] = jnp.zeros_like(dsc_ref)
            dg_ref[...] = jnp.zeros_like(dg_ref)

        @pl.when(k == 0)
        def _():
            acc[...] = jnp.zeros_like(acc)
        acc[...] += _dot(dp_ref[...], wt_ref[...])

        @pl.when(k == nk - 1)
        def _():
            dh = acc[...]
            xv = x_ref[...]
            r = lax.rsqrt(jnp.mean(xv * xv, axis=-1, keepdims=True) + EPS)
            xhat = xv * r
            g = g_ref[...]
            one_sc = 1.0 + ada_ref[1:2, :]
            dsh_ref[...] += jnp.sum(dh, axis=0, keepdims=True)
            dsc_ref[...] += jnp.sum(dh * (xhat * g), axis=0, keepdims=True)
            dg_ref[...] += jnp.sum(dh * xhat, axis=0, keepdims=True) * one_sc
            dxh = dh * (g * one_sc)
            dx = r * (dxh - xhat * jnp.mean(dxh * xhat, axis=-1, keepdims=True))
            gx_ref[...] = dy_ref[...] + dx

    full = pl.BlockSpec((tm, D_MODEL), lambda i, k: (i, 0))
    vec = pl.BlockSpec((1, D_MODEL), lambda i, k: (0, 0))
    return pl.pallas_call(
        body, name="dh_dx", grid=(s // tm, nk),
        in_specs=[pl.BlockSpec((tm, tk), lambda i, k: (i, k)),
                  pl.BlockSpec((tk, D_MODEL), lambda i, k: (k, 0)),
                  full, full, pl.BlockSpec((3, D_MODEL), lambda i, k: (0, 0)), vec],
        out_specs=[full, vec, vec, vec],
        out_shape=[jax.ShapeDtypeStruct((s, D_MODEL), F32)] + [jax.ShapeDtypeStruct((1, D_MODEL), F32)] * 3,
        scratch_shapes=[pltpu.VMEM((tm, D_MODEL), F32)],
        compiler_params=_params(("arbitrary", "arbitrary")),
    )(dproj, w_all_t, x, dy, ada3, norm_g)


def _sum_small(vec_all, qg_parts, kg_parts):
    def body(v_ref, q_ref, k_ref, tot_ref, gq_ref, gk_ref):
        tot = v_ref[0:1, :]
        for p in range(1, N_DEV):
            tot = tot + v_ref[p:p + 1, :]
        tot_ref[...] = tot
        gq_ref[...] = jnp.sum(q_ref[...], axis=0, keepdims=True)
        gk_ref[...] = jnp.sum(k_ref[...], axis=0, keepdims=True)

    n = vec_all.shape[-1]
    return pl.pallas_call(
        body, name="sum_small",
        out_shape=[jax.ShapeDtypeStruct((1, n), F32),
                   jax.ShapeDtypeStruct((1, HEAD_DIM), F32), jax.ShapeDtypeStruct((1, HEAD_DIM), F32)],
        compiler_params=_params(),
    )(vec_all, qg_parts, kg_parts)


def _grad_w_ada(c_cols, dada_rows):
    def body(c_ref, d_ref, out_ref):
        acc = c_ref[0] * d_ref[0]
        for b in range(1, N_DEV):
            acc = acc + c_ref[b] * d_ref[b]
        out_ref[...] = acc

    return pl.pallas_call(
        body, name="grad_w_ada",
        out_shape=jax.ShapeDtypeStruct((D_MODEL, ADA_SHARD), F32),
        compiler_params=_params(),
    )(c_cols, dada_rows)


def _adamw(w, m, v, g_parts, name):
    rows, cols = w.shape
    n_parts = g_parts.shape[0]
    tr = 256 if rows % 256 == 0 else rows
    tc = 256 if (tr == rows and rows > 256 and cols % 256 == 0) else cols
    c1 = 1.0 / (1.0 - ADAM_B1 ** ADAM_STEP)
    c2 = 1.0 / (1.0 - ADAM_B2 ** ADAM_STEP)

    def body(w_ref, m_ref, v_ref, g_ref, go_ref, d_ref, mo_ref, vo_ref):
        g = g_ref[0].astype(F32)
        for p in range(1, n_parts):
            g = g + g_ref[p].astype(F32)
        m_new = ADAM_B1 * m_ref[...] + (1.0 - ADAM_B1) * g
        v_new = ADAM_B2 * v_ref[...] + (1.0 - ADAM_B2) * (g * g)
        go_ref[...] = g
        mo_ref[...] = m_new
        vo_ref[...] = v_new
        d_ref[...] = -ADAM_LR * ((m_new * c1) / (jnp.sqrt(v_new * c2) + ADAM_EPS) + ADAM_WD * w_ref[...])

    blk = pl.BlockSpec((tr, tc), lambda i, j: (i, j))
    return pl.pallas_call(
        body, name=name, grid=(rows // tr, cols // tc),
        in_specs=[blk, blk, blk, pl.BlockSpec((n_parts, tr, tc), lambda i, j: (0, i, j))],
        out_specs=[blk] * 4,
        out_shape=[jax.ShapeDtypeStruct((rows, cols), F32)] * 4,
        compiler_params=_params(("parallel", "parallel")),
    )(w, m, v, g_parts)


_O_Q, _O_K, _O_V, _O_F, _O_ZA, _O_GB, _O_GC, _O_U, _O_ZB, _O_GA, _O_GB2 = (
    0, 512, 1024, 1536, 1544, 2056, 2568, 3080, 3592, 4104, 5128)


def _to_internal(wt_g):
    wf = wt_g.reshape(IN_WIDTH, D_MODEL)
    rows = lambda a, n: wf[a:a + n]
    conv = [rows(base + LANES * c, LANES) for c in range(4) for base in (_O_GB, _O_GC, _O_U, _O_ZB)]
    f = jnp.pad(rows(_O_F, HEADS), ((0, N_FPAD - HEADS), (0, 0)))
    return jnp.concatenate([rows(_O_Q, 512), rows(_O_K, 512), rows(_O_V, 512), rows(_O_ZA, 512), *conv,
                            rows(_O_GA, 1024), rows(_O_GB2, 1024), f], axis=0)


def _from_internal(dwt):
    rows = lambda a, n: dwt[a:a + n]
    conv = lambda k: [rows(OFF_CONV + 4 * LANES * c + LANES * k, LANES) for c in range(4)]
    full = jnp.concatenate([rows(OFF_Q, 512), rows(OFF_K, 512), rows(OFF_V, 512), rows(N_MAIN, HEADS),
                            rows(OFF_ZA, 512), *conv(0), *conv(1), *conv(2), *conv(3),
                            rows(OFF_GA, 1024), rows(OFF_GB, 1024)], axis=0)
    return full.reshape(N_DEV, IN_SHARD, D_MODEL)


def kernel(x, c, w_ada, b_ada, norm_g, w_in, b_f, q_norm_g, k_norm_g, conv_w, w_attn_out, w_conv_out, w_o, loss_target, m_w_ada, m_b_ada, m_norm_g, m_w_in, m_b_f, m_q_norm_g, m_k_norm_g, m_conv_w, m_w_attn_out, m_w_conv_out, m_w_o, v_w_ada, v_b_ada, v_norm_g, v_w_in, v_b_f, v_q_norm_g, v_k_norm_g, v_conv_w, v_w_attn_out, v_w_conv_out, v_w_o):
    me = 4 * lax.axis_index("x") + 2 * lax.axis_index("y") + lax.axis_index("c")
    s = x.shape[1]
    x2, t2 = x[0], loss_target[0]

    cw_g, wa_g, wb_g, wo_g, w_in_g = _gather_two_level(
        [conv_w[0], w_attn_out[0].astype(BF16), w_conv_out[0].astype(BF16), w_o[0].astype(BF16),
         w_in[0].T.astype(BF16)], "gather_weights")
    c_all, ada_g = _ada_exchange(c, w_ada[0])
    ada_mine = lax.dynamic_index_in_dim(ada_g[:, :, 0, :], me, axis=1, keepdims=False)
    ada3 = (ada_mine.reshape(1, 3 * D_MODEL) + b_ada).reshape(3, D_MODEL)
    w_all_t = _to_internal(w_in_g)
    wa = jnp.transpose(wa_g, (1, 0, 2)).reshape(ATTN_W, D_MODEL)
    wb = jnp.transpose(wb_g, (1, 0, 2)).reshape(CONV_W, D_MODEL)
    wo = wo_g.reshape(D_MODEL, D_MODEL)
    cw = jnp.transpose(cw_g, (1, 0, 2)).reshape(3, CONV_W)
    qg = jnp.tile(q_norm_g, (1, HEADS))
    kg = jnp.tile(k_norm_g, (1, HEADS))
    bf_pad = jnp.pad(b_f, ((0, 0), (0, LANES - HEADS)))

    proj, fl, h = _proj_fwd(x2, ada3, norm_g, w_all_t)
    cum = _forget_cumsum(fl, bf_pad)
    qa, ka, va, kt, vt = _qkv_prep(proj, cum, qg, kg)
    attn, oa, qb = _attn_fwd(qa, ka, vt, proj)
    ob = _conv_fwd(proj, cw)
    (dy, dgab, doa, dob, dwo, dwa, dwb, dgate, loss_part) = _tail(oa, ob, proj, x2, t2, ada3, wa, wb, wo)

    do, dza = _attn_bwd_prep(doa, attn, proj)
    dqt, dk, dv = _attn_bwd(qb, ka, kt, va, do)
    dqkv, dqg, dkg, dcum = _qk_norm_bwd(dqt, dk, dv, proj, qg, kg)
    df, dbf = _forget_bwd(dcum, fl, bf_pad)
    dconv4, dcw = _conv_bwd(dob, proj, cw)
    dproj = jnp.concatenate([dqkv, dza, dconv4, dgab, df], axis=1)
    dw_all = _dw_in(h, dproj)
    grad_x, dshift, dscale, dnormg = _dh_and_dx(dproj, w_all_t, x2, dy, ada3, norm_g)

    def by_core(slabs8):
        return jnp.swapaxes(slabs8.reshape((4, 2) + slabs8.shape[1:]), 0, 1).astype(BF16)

    slabs = [by_core(jnp.transpose(dwa.reshape(ATTN_W, N_DEV, LANES), (1, 0, 2))),
             by_core(jnp.transpose(dwb.reshape(CONV_W, N_DEV, LANES), (1, 0, 2))),
             by_core(dwo.reshape(N_DEV, D_MODEL // N_DEV, D_MODEL)),
             by_core(_from_internal(dw_all))]
    theirs = _sibling_swap(slabs, "swap_grads")
    core = lax.axis_index("c").astype(jnp.int32).reshape(1)
    chip_sums = [_pair_sum(m2, t4, core, "pair_sum_" + nm)
                 for m2, t4, nm in zip(slabs, theirs, ("wa", "wb", "wo", "w_in"))]
    g_wa_parts, g_wb_parts, g_wo_parts, g_in_parts = _chip_exchange(chip_sums, "exchange_grads")
    vec = jnp.concatenate([dshift, dscale, dgate, dnormg, dbf, dcw.reshape(1, 3 * CONV_W), dqg, dkg], axis=1)
    (vec_all,) = _exchange([vec], True, "gather_small")
    vec_all = vec_all.reshape(N_DEV, vec.shape[1])
    n_main = 4 * D_MODEL + LANES + 3 * CONV_W
    tot, g_qg, g_kg = _sum_small(
        vec_all[:, :n_main],
        vec_all[:, n_main:n_main + ATTN_W].reshape(N_DEV * HEADS, HEAD_DIM),
        vec_all[:, n_main + ATTN_W:].reshape(N_DEV * HEADS, HEAD_DIM))
    g_b_ada = tot[:, 0:3 * D_MODEL]
    g_norm_g = tot[:, 3 * D_MODEL:4 * D_MODEL]
    g_b_f = tot[:, 4 * D_MODEL:4 * D_MODEL + HEADS]
    g_cw_full = tot[:, 4 * D_MODEL + LANES:].reshape(3, CONV_W)
    g_cw = lax.dynamic_slice(g_cw_full, (0, me * (CONV_W // N_DEV)), (3, CONV_W // N_DEV))
    dada_mine = lax.dynamic_slice(vec_all[:, 0:3 * D_MODEL], (0, me * ADA_SHARD), (N_DEV, ADA_SHARD))
    g_w_ada = _grad_w_ada(jnp.transpose(c_all, (0, 2, 1)), dada_mine.reshape(N_DEV, 1, ADA_SHARD))

    upd = {}
    upd["w_ada"] = _adamw(w_ada[0], m_w_ada[0], v_w_ada[0], g_w_ada[None], "adamw_w_ada")
    upd["b_ada"] = _adamw(b_ada, m_b_ada, v_b_ada, g_b_ada[None], "adamw_b_ada")
    upd["norm_g"] = _adamw(norm_g, m_norm_g, v_norm_g, g_norm_g[None], "adamw_norm_g")
    upd["w_in"] = [u.T for u in _adamw(w_in[0].T, m_w_in[0].T, v_w_in[0].T, g_in_parts, "adamw_w_in")]
    upd["b_f"] = _adamw(b_f, m_b_f, v_b_f, g_b_f[None], "adamw_b_f")
    upd["q_norm_g"] = _adamw(q_norm_g, m_q_norm_g, v_q_norm_g, g_qg[None], "adamw_q_norm_g")
    upd["k_norm_g"] = _adamw(k_norm_g, m_k_norm_g, v_k_norm_g, g_kg[None], "adamw_k_norm_g")
    upd["conv_w"] = _adamw(conv_w[0], m_conv_w[0], v_conv_w[0], g_cw[None], "adamw_conv_w")
    upd["w_attn_out"] = _adamw(w_attn_out[0], m_w_attn_out[0], v_w_attn_out[0], g_wa_parts, "adamw_w_attn_out")
    upd["w_conv_out"] = _adamw(w_conv_out[0], m_w_conv_out[0], v_w_conv_out[0], g_wb_parts, "adamw_w_conv_out")
    upd["w_o"] = _adamw(w_o[0], m_w_o[0], v_w_o[0], g_wo_parts, "adamw_w_o")

    names = ["w_ada", "b_ada", "norm_g", "w_in", "b_f", "q_norm_g", "k_norm_g", "conv_w",
             "w_attn_out", "w_conv_out", "w_o"]
    lead = {"w_ada", "w_in", "conv_w", "w_attn_out", "w_conv_out", "w_o"}
    fix = lambda n, a: a[None] if n in lead else a
    loss = lax.psum(loss_part[0, 0], ("x", "y", "c"))
    outs = [loss, grad_x[None]]
    for k in range(4):
        outs += [fix(n, upd[n][k]) for n in names]
    return tuple(outs)
```

```python
import functools

import numpy as np
import jax
import jax.numpy as jnp
from jax import lax
from jax.experimental import pallas as pl
from jax.experimental.pallas import tpu as pltpu

F32 = jnp.float32
BF16 = jnp.bfloat16

D_MODEL = 1024
HEADS = 8
HEAD_DIM = 64
ATTN_W = 512
CONV_W = 512
N_DEV = 8
IN_WIDTH = 6152
IN_SHARD = IN_WIDTH // N_DEV
N_MAIN = 6144
N_FPAD = 128
N_ALL = N_MAIN + N_FPAD
ADA_SHARD = 3 * D_MODEL // N_DEV
EPS = 1e-6
NEG = -1e30

ADAM_LR = 0.001
ADAM_B1 = 0.9
ADAM_B2 = 0.999
ADAM_EPS = 1e-08
ADAM_WD = 0.01
ADAM_STEP = 10

LANES = 128
VMEM_LIMIT = 56 * 1024 * 1024

TM_PROJ = 512
TN_PROJ = 1024
TM_ELEM = 512
TQ = 512
TM_TAIL = 256
TC_CUM = 256
TK_DW = 512
TN_DW = 896
TM_DH = 512
TK_DH = 896

OFF_Q, OFF_K, OFF_V, OFF_ZA, OFF_CONV, OFF_GA, OFF_GB = 0, 512, 1024, 1536, 2048, 4096, 5120


def _params(sem=None):
    return pltpu.CompilerParams(dimension_semantics=sem, vmem_limit_bytes=VMEM_LIMIT)


def _dot(a, b):
    return jnp.dot(a, b, preferred_element_type=F32)


def _dot_nt(a, b):
    return lax.dot_general(a, b, (((1,), (1,)), ((), ())), preferred_element_type=F32)


def _dot_tn(a, b):
    return lax.dot_general(a, b, (((0,), (0,)), ((), ())), preferred_element_type=F32)


def _sigmoid(x):
    return 1.0 / (1.0 + jnp.exp(-x))


def _lane_lo(shape):
    return lax.broadcasted_iota(jnp.int32, shape, len(shape) - 1) < HEAD_DIM


def _seg_sum(z, lo):
    a = jnp.sum(jnp.where(lo, z, 0.0), axis=-1, keepdims=True)
    b = jnp.sum(jnp.where(lo, 0.0, z), axis=-1, keepdims=True)
    return jnp.where(lo, a, b)


def _lane_col(z, lane):
    idx = lax.broadcasted_iota(jnp.int32, z.shape, 1)
    return jnp.sum(jnp.where(idx == lane, z, 0.0), axis=-1, keepdims=True)


def _sub_row(z, row):
    idx = lax.broadcasted_iota(jnp.int32, z.shape, 0)
    return jnp.sum(jnp.where(idx == row, z, 0.0), axis=0, keepdims=True)


def _mesh_pos():
    x, y, c = lax.axis_index("x"), lax.axis_index("y"), lax.axis_index("c")
    return x, y, c, 4 * x + 2 * y + c


def _peer(k, x, y, c):
    px = 1 - x if (k >> 2) & 1 else x
    py = 1 - y if (k >> 1) & 1 else y
    pc = 1 - c if k & 1 else c
    return (px, py, pc), 4 * px + 2 * py + pc


def _exchange(arrs, gather, name):
    n = len(arrs)
    any_spec = pl.BlockSpec(memory_space=pl.ANY)

    def body(*refs):
        ins, outs = refs[:n], refs[n:2 * n]
        send_sems, recv_sems, local_sems = refs[2 * n:]
        x, y, c, me = _mesh_pos()
        copies = []
        for a in range(n):
            own = ins[a] if gather else ins[a].at[me]
            local = pltpu.make_async_copy(own, outs[a].at[me], local_sems.at[a])
            local.start()
            copies.append(local)
            for k in range(1, N_DEV):
                dev, p = _peer(k, x, y, c)
                cp = pltpu.make_async_remote_copy(
                    src_ref=ins[a] if gather else ins[a].at[p],
                    dst_ref=outs[a].at[me],
                    send_sem=send_sems.at[a * (N_DEV - 1) + k - 1],
                    recv_sem=recv_sems.at[a * (N_DEV - 1) + k - 1],
                    device_id=dev, device_id_type=pl.DeviceIdType.MESH)
                cp.start()
                copies.append(cp)
        for cp in copies:
            cp.wait()

    out_shape = [jax.ShapeDtypeStruct((N_DEV,) + a.shape if gather else a.shape, a.dtype) for a in arrs]
    return pl.pallas_call(
        body, name=name, out_shape=out_shape,
        in_specs=[any_spec] * n, out_specs=[any_spec] * n,
        scratch_shapes=[pltpu.SemaphoreType.DMA((n * (N_DEV - 1),)),
                        pltpu.SemaphoreType.DMA((n * (N_DEV - 1),)),
                        pltpu.SemaphoreType.DMA((n,))],
    )(*arrs)


def _gather_two_level(arrs, name):
    n = len(arrs)
    any_spec = pl.BlockSpec(memory_space=pl.ANY)
    per = N_DEV - 1

    def body(*refs):
        ins, outs = refs[:n], refs[n:2 * n]
        send_sems, recv_sems, local_sems = refs[2 * n:]
        x, y, c, me = _mesh_pos()
        sibling = (x, y, 1 - c)
        chips = [(1 - x, y), (x, 1 - y), (1 - x, 1 - y)]

        def copy(a, k, src, blk, to):
            return pltpu.make_async_remote_copy(
                src_ref=src, dst_ref=outs[a].at[blk],
                send_sem=send_sems.at[a * per + k], recv_sem=recv_sems.at[a * per + k],
                device_id=to, device_id_type=pl.DeviceIdType.MESH)

        local = [pltpu.make_async_copy(ins[a], outs[a].at[me], local_sems.at[a]) for a in range(n)]
        for cp in local:
            cp.start()
        first = []
        for a in range(n):
            first.append(copy(a, 0, ins[a], me, sibling))
            first += [copy(a, 1 + j, ins[a], me, (px, py, c)) for j, (px, py) in enumerate(chips)]
        for cp in first:
            cp.start()
        passed = []
        for j, (px, py) in enumerate(chips):
            blk = 4 * px + 2 * py + c
            for a in range(n):
                copy(a, 1 + j, ins[a], blk, (x, y, c)).wait_recv()
                fwd = copy(a, 4 + j, outs[a].at[blk], blk, sibling)
                fwd.start()
                passed.append(fwd)
        for a in range(n):
            copy(a, 0, ins[a], 4 * x + 2 * y + 1 - c, (x, y, c)).wait_recv()
            for j, (px, py) in enumerate(chips):
                copy(a, 4 + j, ins[a], 4 * px + 2 * py + 1 - c, (x, y, c)).wait_recv()
        for cp in first + passed:
            cp.wait_send()
        for cp in local:
            cp.wait()

    return pl.pallas_call(
        body, name=name,
        out_shape=[jax.ShapeDtypeStruct((N_DEV,) + a.shape, a.dtype) for a in arrs],
        in_specs=[any_spec] * n, out_specs=[any_spec] * n,
        scratch_shapes=[pltpu.SemaphoreType.DMA((n * per,)), pltpu.SemaphoreType.DMA((n * per,)),
                        pltpu.SemaphoreType.DMA((n,))],
    )(*arrs)


def _sibling_swap(arrs, name):
    n = len(arrs)
    any_spec = pl.BlockSpec(memory_space=pl.ANY)

    def body(*refs):
        ins, outs = refs[:n], refs[n:2 * n]
        send_sems, recv_sems = refs[2 * n:]
        x, y, c, _ = _mesh_pos()
        copies = [pltpu.make_async_remote_copy(
            src_ref=ins[a].at[1 - c], dst_ref=outs[a], send_sem=send_sems.at[a], recv_sem=recv_sems.at[a],
            device_id=(x, y, 1 - c), device_id_type=pl.DeviceIdType.MESH) for a in range(n)]
        for cp in copies:
            cp.start()
        for cp in copies:
            cp.wait()

    return pl.pallas_call(
        body, name=name,
        out_shape=[jax.ShapeDtypeStruct(a.shape[1:], a.dtype) for a in arrs],
        in_specs=[any_spec] * n, out_specs=[any_spec] * n,
        scratch_shapes=[pltpu.SemaphoreType.DMA((n,)), pltpu.SemaphoreType.DMA((n,))],
    )(*arrs)


def _pair_sum(mine2, theirs, core, name):
    _, _, rows, cols = mine2.shape
    tr = 256 if rows % 256 == 0 else rows

    def body(core_ref, a_ref, b_ref, out_ref):
        out_ref[...] = (a_ref[...].astype(F32) + b_ref[...].astype(F32)).astype(BF16)

    return pl.pallas_call(
        body, name=name,
        grid_spec=pltpu.PrefetchScalarGridSpec(
            num_scalar_prefetch=1, grid=(4, rows // tr),
            in_specs=[pl.BlockSpec((None, None, tr, cols), lambda ch, i, core_: (core_[0], ch, i, 0)),
                      pl.BlockSpec((None, tr, cols), lambda ch, i, core_: (ch, i, 0))],
            out_specs=pl.BlockSpec((None, tr, cols), lambda ch, i, core_: (ch, i, 0))),
        out_shape=jax.ShapeDtypeStruct(theirs.shape, BF16),
        compiler_params=_params(("parallel", "parallel")),
    )(core, mine2, theirs)


def _chip_copies(ins, outs, send_sems, recv_sems, local_sems):
    x, y, c, _ = _mesh_pos()
    my_chip = 2 * x + y
    chips = [(1 - x, y), (x, 1 - y), (1 - x, 1 - y)]
    copies = []
    for a in range(len(ins)):
        copies.append(pltpu.make_async_copy(ins[a].at[my_chip], outs[a].at[my_chip], local_sems.at[a]))
        for j, (px, py) in enumerate(chips):
            copies.append(pltpu.make_async_remote_copy(
                src_ref=ins[a].at[2 * px + py], dst_ref=outs[a].at[my_chip],
                send_sem=send_sems.at[a * 3 + j], recv_sem=recv_sems.at[a * 3 + j],
                device_id=(px, py, c), device_id_type=pl.DeviceIdType.MESH))
    return copies


def _ada_exchange(c_row, w_ada_sh):
    def body(c_ref, w_ref, call_ref, adag_ref, mine_ref, send_sems, recv_sems):
        x, y, c, me = _mesh_pos()

        def copy(phase, k, src, dst):
            dev, _ = _peer(k, x, y, c)
            return pltpu.make_async_remote_copy(
                src_ref=src, dst_ref=dst,
                send_sem=send_sems.at[phase * (N_DEV - 1) + k - 1],
                recv_sem=recv_sems.at[phase * (N_DEV - 1) + k - 1],
                device_id=dev, device_id_type=pl.DeviceIdType.MESH)

        call_ref[me] = c_ref[...]
        first = [copy(0, k, c_ref, call_ref.at[me]) for k in range(1, N_DEV)]
        for cp in first:
            cp.start()
        for cp in first:
            cp.wait()
        wb = w_ref[...].astype(BF16)
        for b in range(N_DEV):
            row = jnp.broadcast_to(call_ref[b], (8, D_MODEL)).astype(BF16)
            mine_ref[b] = _sub_row(_dot(row, wb), 0)
        adag_ref[me] = mine_ref[...]
        second = [copy(1, k, mine_ref, adag_ref.at[me]) for k in range(1, N_DEV)]
        for cp in second:
            cp.start()
        for cp in second:
            cp.wait()

    vm = pl.BlockSpec(memory_space=pltpu.VMEM)
    return pl.pallas_call(
        body, name="ada_exchange",
        out_shape=[jax.ShapeDtypeStruct((N_DEV, 1, D_MODEL), F32),
                   jax.ShapeDtypeStruct((N_DEV, N_DEV, 1, ADA_SHARD), F32)],
        in_specs=[vm, vm], out_specs=[vm, vm],
        scratch_shapes=[pltpu.VMEM((N_DEV, 1, ADA_SHARD), F32),
                        pltpu.SemaphoreType.DMA((2 * (N_DEV - 1),)),
                        pltpu.SemaphoreType.DMA((2 * (N_DEV - 1),))],
        compiler_params=pltpu.CompilerParams(vmem_limit_bytes=VMEM_LIMIT),
    )(c_row, w_ada_sh)


def _proj_fwd(x, ada3, norm_g, w_all_t):
    s = x.shape[0]
    tm, tn = min(TM_PROJ, s), TN_PROJ

    def body(x_ref, ada_ref, g_ref, w_ref, wf_ref, proj_ref, fl_ref, h_ref):
        @pl.when(pl.program_id(1) == 0)
        def _():
            xv = x_ref[...]
            r = lax.rsqrt(jnp.mean(xv * xv, axis=-1, keepdims=True) + EPS)
            hv = ((xv * r) * g_ref[...]) * (1.0 + ada_ref[1:2, :]) + ada_ref[0:1, :]
            hb = hv.astype(BF16)
            h_ref[...] = hb
            fl_ref[...] = _dot_nt(hb, wf_ref[...])
        proj_ref[...] = _dot_nt(h_ref[...], w_ref[...])

    return pl.pallas_call(
        body, name="proj_fwd", grid=(s // tm, N_MAIN // tn),
        in_specs=[pl.BlockSpec((tm, D_MODEL), lambda i, j: (i, 0)),
                  pl.BlockSpec((3, D_MODEL), lambda i, j: (0, 0)),
                  pl.BlockSpec((1, D_MODEL), lambda i, j: (0, 0)),
                  pl.BlockSpec((tn, D_MODEL), lambda i, j: (j, 0)),
                  pl.BlockSpec((N_FPAD, D_MODEL), lambda i, j: (N_MAIN // N_FPAD, 0))],
        out_specs=[pl.BlockSpec((tm, tn), lambda i, j: (i, j)),
                   pl.BlockSpec((tm, N_FPAD), lambda i, j: (i, 0)),
                   pl.BlockSpec((tm, D_MODEL), lambda i, j: (i, 0))],
        out_shape=[jax.ShapeDtypeStruct((s, N_MAIN), F32),
                   jax.ShapeDtypeStruct((s, N_FPAD), F32),
                   jax.ShapeDtypeStruct((s, D_MODEL), BF16)],
        compiler_params=_params(("parallel", "arbitrary")),
    )(x, ada3, norm_g, w_all_t, w_all_t)


L_ONE_Q, L_F_Q, L_LSE_Q, L_END = HEAD_DIM, HEAD_DIM + 3, HEAD_DIM + 6, HEAD_DIM + 9


def _split3(f):
    hi = f.astype(BF16).astype(F32)
    r = f - hi
    mid = r.astype(BF16).astype(F32)
    return hi, mid, r - mid


def _place3(lane, first, parts, otherwise):
    a, b, c = parts
    return jnp.where(lane == first, a, jnp.where(lane == first + 1, b, jnp.where(lane == first + 2, c, otherwise)))


def _qkv_prep(proj, cum, qg, kg):
    s = proj.shape[0]
    tm = min(TM_ELEM, s)
    scale = HEAD_DIM ** -0.5

    def body(p_ref, cum_ref, qg_ref, kg_ref, qa_ref, ka_ref, va_ref, kt_ref, vt_ref):
        lane = lax.broadcasted_iota(jnp.int32, (tm, LANES), 1)
        lo = lane < HEAD_DIM
        cum_v = cum_ref[...]
        v_tail = jnp.where(lane < L_F_Q, 1.0, 0.0)
        for pr in range(ATTN_W // LANES):
            sl = slice(pr * LANES, (pr + 1) * LANES)
            q2 = p_ref[:, OFF_Q + pr * LANES:OFF_Q + (pr + 1) * LANES]
            k2 = p_ref[:, OFF_K + pr * LANES:OFF_K + (pr + 1) * LANES]
            v2 = p_ref[:, OFF_V + pr * LANES:OFF_V + (pr + 1) * LANES]
            rq = lax.rsqrt(_seg_sum(q2 * q2, lo) * (1.0 / HEAD_DIM) + EPS)
            rk = lax.rsqrt(_seg_sum(k2 * k2, lo) * (1.0 / HEAD_DIM) + EPS)
            qn = ((q2 * rq) * qg_ref[:, sl]) * scale
            kn = (k2 * rk) * kg_ref[:, sl]
            for hh in range(2):
                h = 2 * pr + hh
                f3 = _split3(_lane_col(cum_v, h))
                qh = qn if hh == 0 else pltpu.roll(qn, HEAD_DIM, 1)
                kh = kn if hh == 0 else pltpu.roll(kn, HEAD_DIM, 1)
                vh = v2 if hh == 0 else pltpu.roll(v2, HEAD_DIM, 1)
                q_tail = jnp.where(lane < L_F_Q, 1.0, _place3(lane, L_F_Q, f3, 0.0))
                k_tail = _place3(lane, L_ONE_Q, tuple(-f for f in f3), jnp.where(lane < L_END, 1.0, 0.0))
                k_row = jnp.where(lo, kh, k_tail)
                v_row = jnp.where(lo, vh, v_tail)
                qa_ref[h] = jnp.where(lo, qh, q_tail).astype(BF16)
                ka_ref[h] = k_row.astype(BF16)
                va_ref[h] = v_row.astype(BF16)
                kt_ref[h] = k_row.T.astype(BF16)
                vt_ref[h] = v_row.T.astype(BF16)

    heads = pl.BlockSpec((HEADS, tm, LANES), lambda i: (0, i, 0))
    heads_t = pl.BlockSpec((HEADS, LANES, tm), lambda i: (0, 0, i))
    vec = pl.BlockSpec((1, ATTN_W), lambda i: (0, 0))
    return pl.pallas_call(
        body, name="qkv_prep", grid=(s // tm,),
        in_specs=[pl.BlockSpec((tm, 3 * ATTN_W), lambda i: (i, 0)),
                  pl.BlockSpec((tm, LANES), lambda i: (i, 0)), vec, vec],
        out_specs=[heads, heads, heads, heads_t, heads_t],
        out_shape=[jax.ShapeDtypeStruct((HEADS, s, LANES), BF16)] * 3
        + [jax.ShapeDtypeStruct((HEADS, LANES, s), BF16)] * 2,
        compiler_params=_params(("parallel",)),
    )(proj, cum, qg, kg)


def _log_forget(fl, bf):
    z = fl + bf
    lf = jnp.minimum(z, 0.0) - jnp.log1p(jnp.exp(-jnp.abs(z)))
    lane = lax.broadcasted_iota(jnp.int32, z.shape, 1)
    return jnp.where(lane < HEADS, lf, 0.0)


def _forget_cumsum(fl, bf_pad):
    s = fl.shape[0]
    tc = min(TC_CUM, s)

    def body(fl_ref, bf_ref, cum_ref, carry):
        @pl.when(pl.program_id(0) == 0)
        def _():
            carry[...] = jnp.zeros_like(carry)
        lf = _log_forget(fl_ref[...], bf_ref[...])
        r = lax.broadcasted_iota(jnp.int32, (tc, tc), 0)
        cidx = lax.broadcasted_iota(jnp.int32, (tc, tc), 1)
        tri = (cidx <= r).astype(F32)
        cs = jnp.dot(tri, lf, preferred_element_type=F32, precision=lax.Precision.HIGHEST) + carry[...]
        cum_ref[...] = cs
        carry[...] = cum_ref[tc - 1:tc, :]

    return pl.pallas_call(
        body, name="forget_cumsum", grid=(s // tc,),
        in_specs=[pl.BlockSpec((tc, LANES), lambda i: (i, 0)),
                  pl.BlockSpec((1, LANES), lambda i: (0, 0))],
        out_specs=pl.BlockSpec((tc, LANES), lambda i: (i, 0)),
        out_shape=jax.ShapeDtypeStruct((s, LANES), F32),
        scratch_shapes=[pltpu.VMEM((1, LANES), F32)],
        compiler_params=_params(("arbitrary",)),
    )(fl, bf_pad)


def _causal_t(t):
    return lax.broadcasted_iota(jnp.int32, (t, t), 0) <= lax.broadcasted_iota(jnp.int32, (t, t), 1)


def _tri_steps(nt, q_major):
    if q_major:
        pairs = [(i, j) for i in range(nt) for j in range(i + 1)]
    else:
        pairs = [(i, j) for j in range(nt) for i in range(j, nt)]
    return (jnp.asarray(np.array([p[0] for p in pairs], np.int32)),
            jnp.asarray(np.array([p[1] for p in pairs], np.int32)))


def _attn_fwd(qa, ka, vt, proj):
    s = qa.shape[1]
    t = min(TQ, s)
    it, jt = _tri_steps(s // t, True)
    za_blk = OFF_ZA // LANES

    def body(it_ref, jt_ref, q_ref, k_ref, vt_ref, za_ref, attn_ref, oa_ref, qb_ref, m_s, acc_s, pair_s):
        step = pl.program_id(1)
        i, j = it_ref[step], jt_ref[step]

        @pl.when(j == 0)
        def _():
            m_s[...] = jnp.full_like(m_s, NEG)
            acc_s[...] = jnp.zeros_like(acc_s)

        def update(masked):
            for hh in range(2):
                st = _dot_nt(k_ref[hh], q_ref[hh])
                if masked:
                    st = jnp.where(_causal_t(t), st, NEG)
                m_prev = m_s[hh]
                m_next = jnp.maximum(m_prev, jnp.max(st, axis=0, keepdims=True))
                alpha = jnp.exp(m_prev - m_next)
                pt = jnp.exp(st - m_next).astype(BF16)
                acc_s[hh] = acc_s[hh] * alpha + _dot(vt_ref[hh], pt)
                m_s[hh] = m_next

        @pl.when(j < i)
        def _():
            update(False)

        @pl.when(j == i)
        def _():
            update(True)
            row = lax.broadcasted_iota(jnp.int32, (LANES, t), 0)
            lane = lax.broadcasted_iota(jnp.int32, (t, LANES), 1)
            for hh in range(2):
                l_row = acc_s[hh, L_ONE_Q:L_ONE_Q + 1, :]
                pair_s[hh * HEAD_DIM:(hh + 1) * HEAD_DIM, :] = acc_s[hh, 0:HEAD_DIM, :] / l_row
                lse3 = _split3(m_s[hh] + jnp.log(l_row))
                tail_t = _place3(row, L_LSE_Q, tuple(-x for x in lse3), 0.0)
                keep_q = jnp.logical_or(lane < L_LSE_Q, lane >= L_END)
                qb_ref[hh] = jnp.where(keep_q, q_ref[hh].astype(F32), tail_t.T).astype(BF16)
            out = pair_s[...].T
            attn_ref[...] = out
            z = za_ref[...]
            oa_ref[...] = (out * (z * _sigmoid(z))).astype(BF16)

    pair_q = pl.BlockSpec((2, t, LANES), lambda p, n, it_, jt_: (p, it_[n], 0))
    pair_k = pl.BlockSpec((2, t, LANES), lambda p, n, it_, jt_: (p, jt_[n], 0))
    pair_kt = pl.BlockSpec((2, LANES, t), lambda p, n, it_, jt_: (p, 0, jt_[n]))
    out_q = pl.BlockSpec((t, LANES), lambda p, n, it_, jt_: (it_[n], p))
    return pl.pallas_call(
        body, name="attn_fwd",
        grid_spec=pltpu.PrefetchScalarGridSpec(
            num_scalar_prefetch=2, grid=(ATTN_W // LANES, it.shape[0]),
            in_specs=[pair_q, pair_k, pair_kt,
                      pl.BlockSpec((t, LANES), lambda p, n, it_, jt_: (it_[n], za_blk + p))],
            out_specs=[out_q, out_q, pair_q],
            scratch_shapes=[pltpu.VMEM((2, 1, t), F32), pltpu.VMEM((2, LANES, t), F32),
                            pltpu.VMEM((LANES, t), F32)]),
        out_shape=[jax.ShapeDtypeStruct((s, ATTN_W), F32),
                   jax.ShapeDtypeStruct((s, ATTN_W), BF16),
                   jax.ShapeDtypeStruct((HEADS, s, LANES), BF16)],
        compiler_params=_params(("parallel", "arbitrary")),
    )(it, jt, qa, ka, vt, proj)


def _conv_parts(blk, halo, first, w_ref, tm):
    gb, gc = blk[:, 0:LANES], blk[:, LANES:2 * LANES]
    u, zb = blk[:, 2 * LANES:3 * LANES], blk[:, 3 * LANES:4 * LANES]
    cu = gc * u
    cu_h = jnp.where(first, 0.0, halo[:, LANES:2 * LANES] * halo[:, 2 * LANES:3 * LANES])
    prev1, prev2 = _sub_row(cu_h, 7), _sub_row(cu_h, 6)
    row = lax.broadcasted_iota(jnp.int32, (tm, LANES), 0)
    r1 = jnp.where(row == 0, prev1, pltpu.roll(cu, 1, 0))
    r2 = jnp.where(row == 0, prev2, jnp.where(row == 1, prev1, pltpu.roll(cu, 2, 0)))
    conv = w_ref[2:3, :] * cu + w_ref[1:2, :] * r1 + w_ref[0:1, :] * r2
    return gb, gc, u, zb, cu, r1, r2, conv


def _conv_fwd(proj, conv_w):
    s = proj.shape[0]
    tm = min(TM_ELEM, s)
    cb = OFF_CONV // (4 * LANES)

    def body(p_ref, halo_ref, w_ref, ob_ref):
        first = pl.program_id(1) == 0
        gb, _, _, zb, _, _, _, conv = _conv_parts(p_ref[...], halo_ref[...], first, w_ref, tm)
        ob_ref[...] = (gb * conv * (zb * _sigmoid(zb))).astype(BF16)

    return pl.pallas_call(
        body, name="conv_fwd", grid=(CONV_W // LANES, s // tm),
        in_specs=[pl.BlockSpec((tm, 4 * LANES), lambda c, i: (i, cb + c)),
                  pl.BlockSpec((8, 4 * LANES), lambda c, i: (jnp.maximum(i * (tm // 8) - 1, 0), cb + c)),
                  pl.BlockSpec((3, LANES), lambda c, i: (0, c))],
        out_specs=pl.BlockSpec((tm, LANES), lambda c, i: (i, c)),
        out_shape=jax.ShapeDtypeStruct((s, CONV_W), BF16),
        compiler_params=_params(("parallel", "parallel")),
    )(proj, proj, conv_w)


def _tail(oa, ob, proj, x, target, ada3, wa, wb, wo):
    s = x.shape[0]
    tm = min(TM_TAIL, s)
    gab_blk = OFF_GA // (2 * D_MODEL)

    def body(oa_ref, ob_ref, gab_ref, x_ref, t_ref, ada_ref, wa_ref, wb_ref, wo_ref,
             dy_ref, dgab_ref, doa_ref, dob_ref, dwo_ref, dwa_ref, dwb_ref, dgate_ref, loss_ref):
        @pl.when(pl.program_id(0) == 0)
        def _():
            dwo_ref[...] = jnp.zeros_like(dwo_ref)
            dwa_ref[...] = jnp.zeros_like(dwa_ref)
            dwb_ref[...] = jnp.zeros_like(dwb_ref)
            dgate_ref[...] = jnp.zeros_like(dgate_ref)
            loss_ref[...] = jnp.zeros_like(loss_ref)

        oa_v, ob_v = oa_ref[...], ob_ref[...]
        wa_v, wb_v, wo_v = wa_ref[...], wb_ref[...], wo_ref[...]
        a2 = _dot(oa_v, wa_v)
        b2 = _dot(ob_v, wb_v)
        sa = _sigmoid(gab_ref[:, 0:D_MODEL])
        sb = _sigmoid(gab_ref[:, D_MODEL:2 * D_MODEL])
        mb = (sa * a2 + sb * b2).astype(BF16)
        mo = _dot(mb, wo_v)
        gate = ada_ref[2:3, :]
        err = (x_ref[...] + gate * mo) - t_ref[...]
        dy = err * (1.0 / D_MODEL)
        dy_ref[...] = dy
        loss_ref[...] += 0.5 * jnp.sum(err * err) * (1.0 / D_MODEL)
        dgate_ref[...] += jnp.sum(dy * mo, axis=0, keepdims=True)
        dmo = (dy * gate).astype(BF16)
        dmerged = _dot_nt(dmo, wo_v)
        dwo_ref[...] += _dot_tn(mb, dmo)
        da2 = (dmerged * sa).astype(BF16)
        db2 = (dmerged * sb).astype(BF16)
        dgab_ref[:, 0:D_MODEL] = (dmerged * a2 * (sa * (1.0 - sa))).astype(BF16)
        dgab_ref[:, D_MODEL:2 * D_MODEL] = (dmerged * b2 * (sb * (1.0 - sb))).astype(BF16)
        doa_ref[...] = _dot_nt(da2, wa_v)
        dob_ref[...] = _dot_nt(db2, wb_v)
        dwa_ref[...] += _dot_tn(oa_v, da2)
        dwb_ref[...] += _dot_tn(ob_v, db2)

    half = pl.BlockSpec((tm, ATTN_W), lambda i: (i, 0))
    full = pl.BlockSpec((tm, D_MODEL), lambda i: (i, 0))

    def const(shape):
        return pl.BlockSpec(shape, lambda i: (0, 0))

    return pl.pallas_call(
        body, name="tail", grid=(s // tm,),
        in_specs=[half, half, pl.BlockSpec((tm, 2 * D_MODEL), lambda i: (i, gab_blk)), full, full,
                  const((3, D_MODEL)), const((ATTN_W, D_MODEL)), const((CONV_W, D_MODEL)),
                  const((D_MODEL, D_MODEL))],
        out_specs=[full, pl.BlockSpec((tm, 2 * D_MODEL), lambda i: (i, 0)), half, half,
                   const((D_MODEL, D_MODEL)), const((ATTN_W, D_MODEL)), const((CONV_W, D_MODEL)),
                   const((1, D_MODEL)), const((1, LANES))],
        out_shape=[jax.ShapeDtypeStruct((s, D_MODEL), F32),
                   jax.ShapeDtypeStruct((s, 2 * D_MODEL), BF16),
                   jax.ShapeDtypeStruct((s, ATTN_W), F32),
                   jax.ShapeDtypeStruct((s, CONV_W), F32),
                   jax.ShapeDtypeStruct((D_MODEL, D_MODEL), F32),
                   jax.ShapeDtypeStruct((ATTN_W, D_MODEL), F32),
                   jax.ShapeDtypeStruct((CONV_W, D_MODEL), F32),
                   jax.ShapeDtypeStruct((1, D_MODEL), F32),
                   jax.ShapeDtypeStruct((1, LANES), F32)],
        compiler_params=_params(("arbitrary",)),
    )(oa, ob, proj, x, target, ada3, wa, wb, wo)


def _attn_bwd_prep(doa, attn, proj):
    s = doa.shape[0]
    tm = min(TM_ELEM, s)
    za_blk = OFF_ZA // ATTN_W

    def body(doa_ref, attn_ref, za_ref, do_ref, dza_ref):
        lane = lax.broadcasted_iota(jnp.int32, (tm, LANES), 1)
        lo = lane < HEAD_DIM
        for pr in range(ATTN_W // LANES):
            sl = slice(pr * LANES, (pr + 1) * LANES)
            g, a, z = doa_ref[:, sl], attn_ref[:, sl], za_ref[:, sl]
            sg = _sigmoid(z)
            dat = (g * (z * sg)).astype(BF16).astype(F32)
            prod = dat * a
            dza_ref[:, sl] = (g * a * (sg * (1.0 + z * (1.0 - sg)))).astype(BF16)
            for hh in range(2):
                sel = lo if hh == 0 else jnp.logical_not(lo)
                delta3 = _split3(jnp.sum(jnp.where(sel, prod, 0.0), axis=-1, keepdims=True))
                dh = dat if hh == 0 else pltpu.roll(dat, HEAD_DIM, 1)
                tail = _place3(lane, L_ONE_Q, tuple(-d for d in delta3), 0.0)
                do_ref[2 * pr + hh] = jnp.where(lo, dh, tail).astype(BF16)

    row = pl.BlockSpec((tm, ATTN_W), lambda i: (i, 0))
    return pl.pallas_call(
        body, name="attn_bwd_prep", grid=(s // tm,),
        in_specs=[row, row, pl.BlockSpec((tm, ATTN_W), lambda i: (i, za_blk))],
        out_specs=[pl.BlockSpec((HEADS, tm, LANES), lambda i: (0, i, 0)), row],
        out_shape=[jax.ShapeDtypeStruct((HEADS, s, LANES), BF16),
                   jax.ShapeDtypeStruct((s, ATTN_W), BF16)],
        compiler_params=_params(("parallel",)),
    )(doa, attn, proj)


def _attn_bwd(qb, ka, kt, va, do):
    s = qb.shape[1]
    t = min(TQ, s)
    nt = s // t
    it, jt = _tri_steps(nt, False)

    def body(it_ref, jt_ref, q_ref, k_ref, kt_ref, v_ref, do_ref, dqt_ref, dk_ref, dv_ref):
        step = pl.program_id(1)
        i, j = it_ref[step], jt_ref[step]

        @pl.when(step == 0)
        def _():
            dqt_ref[...] = jnp.zeros_like(dqt_ref)

        @pl.when(i == j)
        def _():
            dk_ref[...] = jnp.zeros_like(dk_ref)
            dv_ref[...] = jnp.zeros_like(dv_ref)

        def update(masked):
            for hh in range(2):
                qh, doh = q_ref[hh], do_ref[hh]
                st = _dot_nt(k_ref[hh], qh)
                if masked:
                    st = jnp.where(_causal_t(t), st, NEG)
                pt = jnp.exp(st)
                dst = (pt * _dot_nt(v_ref[hh], doh)).astype(BF16)
                dv_ref[hh] += _dot(pt.astype(BF16), doh)
                dk_ref[hh] += _dot(dst, qh)
                dqt_ref[hh, i] += _dot(kt_ref[hh], dst)

        @pl.when(i > j)
        def _():
            update(False)

        @pl.when(i == j)
        def _():
            update(True)

    pair_q = pl.BlockSpec((2, t, LANES), lambda p, n, it_, jt_: (p, it_[n], 0))
    pair_k = pl.BlockSpec((2, t, LANES), lambda p, n, it_, jt_: (p, jt_[n], 0))
    pair_kt = pl.BlockSpec((2, LANES, t), lambda p, n, it_, jt_: (p, 0, jt_[n]))
    return pl.pallas_call(
        body, name="attn_bwd",
        grid_spec=pltpu.PrefetchScalarGridSpec(
            num_scalar_prefetch=2, grid=(ATTN_W // LANES, it.shape[0]),
            in_specs=[pair_q, pair_k, pair_kt, pair_k, pair_q],
            out_specs=[pl.BlockSpec((2, nt, LANES, t), lambda p, n, it_, jt_: (p, 0, 0, 0)),
                       pair_k, pair_k]),
        out_shape=[jax.ShapeDtypeStruct((HEADS, nt, LANES, t), F32),
                   jax.ShapeDtypeStruct((HEADS, s, LANES), F32),
                   jax.ShapeDtypeStruct((HEADS, s, LANES), F32)],
        compiler_params=_params(("parallel", "arbitrary")),
    )(it, jt, qb, ka, kt, va, do)


def _forget_bwd(dcum, fl, bf_pad):
    s = fl.shape[0]
    tc = min(TC_CUM, s)
    n = s // tc

    def body(dc_ref, fl_ref, bf_ref, df_ref, dbf_ref, carry):
        @pl.when(pl.program_id(0) == 0)
        def _():
            carry[...] = jnp.zeros_like(carry)
            dbf_ref[...] = jnp.zeros_like(dbf_ref)
        r = lax.broadcasted_iota(jnp.int32, (tc, tc), 0)
        cidx = lax.broadcasted_iota(jnp.int32, (tc, tc), 1)
        tri = (cidx >= r).astype(F32)
        dc = dc_ref[...]
        dlf = jnp.dot(tri, dc, preferred_element_type=F32, precision=lax.Precision.HIGHEST) + carry[...]
        carry[...] += jnp.sum(dc, axis=0, keepdims=True)
        lane = lax.broadcasted_iota(jnp.int32, (tc, LANES), 1)
        dfl = jnp.where(lane < HEADS, dlf * _sigmoid(-(fl_ref[...] + bf_ref[...])), 0.0)
        df_ref[...] = dfl.astype(BF16)
        dbf_ref[...] += jnp.sum(dfl, axis=0, keepdims=True)

    rev = pl.BlockSpec((tc, LANES), lambda i: (n - 1 - i, 0))
    vec = pl.BlockSpec((1, LANES), lambda i: (0, 0))
    return pl.pallas_call(
        body, name="forget_bwd", grid=(n,),
        in_specs=[rev, rev, vec], out_specs=[rev, vec],
        out_shape=[jax.ShapeDtypeStruct((s, LANES), BF16), jax.ShapeDtypeStruct((1, LANES), F32)],
        scratch_shapes=[pltpu.VMEM((1, LANES), F32)],
        compiler_params=_params(("arbitrary",)),
    )(dcum, fl, bf_pad)


def _qk_norm_bwd(dqt, dk, dv, proj, qg, kg):
    s = dv.shape[1]
    tm = dqt.shape[-1]
    scale = HEAD_DIM ** -0.5

    def body(dqt_ref, dk_ref, dv_ref, p_ref, qg_ref, kg_ref, out_ref, dqg_ref, dkg_ref, dcum_ref):
        @pl.when(pl.program_id(0) == 0)
        def _():
            dqg_ref[...] = jnp.zeros_like(dqg_ref)
            dkg_ref[...] = jnp.zeros_like(dkg_ref)
        lane = lax.broadcasted_iota(jnp.int32, (tm, LANES), 1)
        lo = lane < HEAD_DIM
        dq_rows = [dqt_ref[h, 0].T for h in range(HEADS)]
        dcum = jnp.zeros((tm, LANES), F32)
        for h in range(HEADS):
            dcum = jnp.where(lane == h, _lane_col(dq_rows[h], L_F_Q) - _lane_col(dk_ref[h], L_ONE_Q), dcum)
        dcum_ref[...] = dcum

        def pair(a, b):
            return jnp.where(lo, a, pltpu.roll(b, HEAD_DIM, 1))

        def one(raw, dy, g, dg_ref, sl, off):
            r = lax.rsqrt(_seg_sum(raw * raw, lo) * (1.0 / HEAD_DIM) + EPS)
            xhat = raw * r
            dg_ref[:, sl] += jnp.sum(dy * xhat, axis=0, keepdims=True)
            dxh = dy * g
            dx = r * (dxh - xhat * (_seg_sum(dxh * xhat, lo) * (1.0 / HEAD_DIM)))
            out_ref[:, off + sl.start:off + sl.stop] = dx.astype(BF16)

        for pr in range(ATTN_W // LANES):
            sl = slice(pr * LANES, (pr + 1) * LANES)
            dq2 = pair(dq_rows[2 * pr], dq_rows[2 * pr + 1])
            one(p_ref[:, OFF_Q + sl.start:OFF_Q + sl.stop], dq2 * scale, qg_ref[:, sl], dqg_ref, sl, OFF_Q)
            one(p_ref[:, OFF_K + sl.start:OFF_K + sl.stop], pair(dk_ref[2 * pr], dk_ref[2 * pr + 1]),
                kg_ref[:, sl], dkg_ref, sl, OFF_K)
            out_ref[:, OFF_V + sl.start:OFF_V + sl.stop] = pair(dv_ref[2 * pr], dv_ref[2 * pr + 1]).astype(BF16)

    heads = pl.BlockSpec((HEADS, tm, LANES), lambda i: (0, i, 0))
    vec = pl.BlockSpec((1, ATTN_W), lambda i: (0, 0))
    return pl.pallas_call(
        body, name="qk_norm_bwd", grid=(s // tm,),
        in_specs=[pl.BlockSpec((HEADS, 1, LANES, tm), lambda i: (0, i, 0, 0)), heads, heads,
                  pl.BlockSpec((tm, 2 * ATTN_W), lambda i: (i, 0)), vec, vec],
        out_specs=[pl.BlockSpec((tm, 3 * ATTN_W), lambda i: (i, 0)), vec, vec,
                   pl.BlockSpec((tm, LANES), lambda i: (i, 0))],
        out_shape=[jax.ShapeDtypeStruct((s, 3 * ATTN_W), BF16),
                   jax.ShapeDtypeStruct((1, ATTN_W), F32), jax.ShapeDtypeStruct((1, ATTN_W), F32),
                   jax.ShapeDtypeStruct((s, LANES), F32)],
        compiler_params=_params(("arbitrary",)),
    )(dqt, dk, dv, proj, qg, kg)


def _conv_bwd(dob, proj, conv_w):
    s = dob.shape[0]
    tm = min(TM_ELEM, s)
    cb = OFF_CONV // (4 * LANES)
    nblk8 = s // 8

    def body(dob_ref, p_ref, prev_ref, next_ref, dnext_ref, w_ref, out_ref, dw_ref):
        i = pl.program_id(1)

        @pl.when(i == 0)
        def _():
            dw_ref[...] = jnp.zeros_like(dw_ref)
        gb, gc, u, zb, cu, r1, r2, conv = _conv_parts(p_ref[...], prev_ref[...], i == 0, w_ref, tm)
        g = dob_ref[...]
        sg = _sigmoid(zb)
        sz = zb * sg
        dconv = g * gb * sz
        nxt = next_ref[...]
        zn = nxt[:, 3 * LANES:4 * LANES]
        dcn = jnp.where(i == pl.num_programs(1) - 1, 0.0,
                        dnext_ref[...] * nxt[:, 0:LANES] * (zn * _sigmoid(zn)))
        nxt1, nxt2 = _sub_row(dcn, 0), _sub_row(dcn, 1)
        row = lax.broadcasted_iota(jnp.int32, (tm, LANES), 0)
        f1 = jnp.where(row == tm - 1, nxt1, pltpu.roll(dconv, tm - 1, 0))
        f2 = jnp.where(row == tm - 2, nxt1, jnp.where(row == tm - 1, nxt2, pltpu.roll(dconv, tm - 2, 0)))
        dcu = w_ref[2:3, :] * dconv + w_ref[1:2, :] * f1 + w_ref[0:1, :] * f2
        out_ref[:, 0:LANES] = (g * conv * sz).astype(BF16)
        out_ref[:, LANES:2 * LANES] = (dcu * u).astype(BF16)
        out_ref[:, 2 * LANES:3 * LANES] = (dcu * gc).astype(BF16)
        out_ref[:, 3 * LANES:4 * LANES] = (g * gb * conv * (sg * (1.0 + zb * (1.0 - sg)))).astype(BF16)
        w_row = lax.broadcasted_iota(jnp.int32, (3, LANES), 0)
        dw0 = jnp.sum(dconv * r2, axis=0, keepdims=True)
        dw1 = jnp.sum(dconv * r1, axis=0, keepdims=True)
        dw2 = jnp.sum(dconv * cu, axis=0, keepdims=True)
        dw_ref[...] += jnp.where(w_row == 0, dw0, jnp.where(w_row == 1, dw1, dw2))

    nxt_idx = lambda i: jnp.minimum((i + 1) * (tm // 8), nblk8 - 1)
    return pl.pallas_call(
        body, name="conv_bwd", grid=(CONV_W // LANES, s // tm),
        in_specs=[pl.BlockSpec((tm, LANES), lambda c, i: (i, c)),
                  pl.BlockSpec((tm, 4 * LANES), lambda c, i: (i, cb + c)),
                  pl.BlockSpec((8, 4 * LANES), lambda c, i: (jnp.maximum(i * (tm // 8) - 1, 0), cb + c)),
                  pl.BlockSpec((8, 4 * LANES), lambda c, i: (nxt_idx(i), cb + c)),
                  pl.BlockSpec((8, LANES), lambda c, i: (nxt_idx(i), c)),
                  pl.BlockSpec((3, LANES), lambda c, i: (0, c))],
        out_specs=[pl.BlockSpec((tm, 4 * LANES), lambda c, i: (i, c)),
                   pl.BlockSpec((3, LANES), lambda c, i: (0, c))],
        out_shape=[jax.ShapeDtypeStruct((s, 4 * CONV_W), BF16), jax.ShapeDtypeStruct((3, CONV_W), F32)],
        compiler_params=_params(("parallel", "arbitrary")),
    )(dob, proj, proj, proj, dob, conv_w)


def _dw_in(h, dproj):
    s = h.shape[0]
    tk, tn = min(TK_DW, s), TN_DW

    def body(dp_ref, h_ref, out_ref):
        @pl.when(pl.program_id(1) == 0)
        def _():
            out_ref[...] = jnp.zeros_like(out_ref)
        out_ref[...] += _dot_tn(dp_ref[...], h_ref[...])

    return pl.pallas_call(
        body, name="dw_in", grid=(N_ALL // tn, s // tk),
        in_specs=[pl.BlockSpec((tk, tn), lambda n, k: (k, n)),
                  pl.BlockSpec((tk, D_MODEL), lambda n, k: (k, 0))],
        out_specs=pl.BlockSpec((tn, D_MODEL), lambda n, k: (n, 0)),
        out_shape=jax.ShapeDtypeStruct((N_ALL, D_MODEL), F32),
        compiler_params=_params(("parallel", "arbitrary")),
    )(dproj, h)


def _dh_and_dx(dproj, w_all_t, x, dy, ada3, norm_g, chip_sums):
    s = x.shape[0]
    tm, tk = min(TM_DH, s), TK_DH
    nk = N_ALL // tk
    nt = s // tm
    n = len(chip_sums)

    def body(dp_ref, wt_ref, x_ref, dy_ref, ada_ref, g_ref, *rest):
        ins, rest = rest[:n], rest[n:]
        gx_ref, dsh_ref, dsc_ref, dg_ref = rest[:4]
        outs, (acc, send_sems, recv_sems, local_sems) = rest[4:4 + n], rest[4 + n:]
        i, k = pl.program_id(0), pl.program_id(1)

        @pl.when(jnp.logical_and(i == 0, k == 0))
        def _():
            for cp in _chip_copies(ins, outs, send_sems, recv_sems, local_sems):
                cp.start()
            dsh_ref[...] = jnp.zeros_like(dsh_ref)
            dsc_ref[...] = jnp.zeros_like(dsc_ref)
            dg_ref[...] = jnp.zeros_like(dg_ref)

        @pl.when(k == 0)
        def _():
            acc[...] = jnp.zeros_like(acc)
        acc[...] += _dot(dp_ref[...], wt_ref[...])

        @pl.when(k == nk - 1)
        def _():
            dh = acc[...]
            xv = x_ref[...]
            r = lax.rsqrt(jnp.mean(xv * xv, axis=-1, keepdims=True) + EPS)
            xhat = xv * r
            g = g_ref[...]
            one_sc = 1.0 + ada_ref[1:2, :]
            dsh_ref[...] += jnp.sum(dh, axis=0, keepdims=True)
            dsc_ref[...] += jnp.sum(dh * (xhat * g), axis=0, keepdims=True)
            dg_ref[...] += jnp.sum(dh * xhat, axis=0, keepdims=True) * one_sc
            dxh = dh * (g * one_sc)
            dx = r * (dxh - xhat * jnp.mean(dxh * xhat, axis=-1, keepdims=True))
            gx_ref[...] = dy_ref[...] + dx

        @pl.when(jnp.logical_and(i == nt - 1, k == nk - 1))
        def _():
            for cp in _chip_copies(ins, outs, send_sems, recv_sems, local_sems):
                cp.wait()

    full = pl.BlockSpec((tm, D_MODEL), lambda i, k: (i, 0))
    vec = pl.BlockSpec((1, D_MODEL), lambda i, k: (0, 0))
    any_spec = pl.BlockSpec(memory_space=pl.ANY)
    res = pl.pallas_call(
        body, name="dh_dx", grid=(nt, nk),
        in_specs=[pl.BlockSpec((tm, tk), lambda i, k: (i, k)),
                  pl.BlockSpec((tk, D_MODEL), lambda i, k: (k, 0)),
                  full, full, pl.BlockSpec((3, D_MODEL), lambda i, k: (0, 0)), vec] + [any_spec] * n,
        out_specs=[full, vec, vec, vec] + [any_spec] * n,
        out_shape=[jax.ShapeDtypeStruct((s, D_MODEL), F32)] + [jax.ShapeDtypeStruct((1, D_MODEL), F32)] * 3
        + [jax.ShapeDtypeStruct(a.shape, a.dtype) for a in chip_sums],
        scratch_shapes=[pltpu.VMEM((tm, D_MODEL), F32), pltpu.SemaphoreType.DMA((n * 3,)),
                        pltpu.SemaphoreType.DMA((n * 3,)), pltpu.SemaphoreType.DMA((n,))],
        compiler_params=_params(("arbitrary", "arbitrary")),
    )(dproj, w_all_t, x, dy, ada3, norm_g, *chip_sums)
    return res[:4], res[4:]


def _sum_small(vec_all, qg_parts, kg_parts):
    def body(v_ref, q_ref, k_ref, tot_ref, gq_ref, gk_ref):
        tot = v_ref[0:1, :]
        for p in range(1, N_DEV):
            tot = tot + v_ref[p:p + 1, :]
        tot_ref[...] = tot
        gq_ref[...] = jnp.sum(q_ref[...], axis=0, keepdims=True)
        gk_ref[...] = jnp.sum(k_ref[...], axis=0, keepdims=True)

    n = vec_all.shape[-1]
    return pl.pallas_call(
        body, name="sum_small",
        out_shape=[jax.ShapeDtypeStruct((1, n), F32),
                   jax.ShapeDtypeStruct((1, HEAD_DIM), F32), jax.ShapeDtypeStruct((1, HEAD_DIM), F32)],
        compiler_params=_params(),
    )(vec_all, qg_parts, kg_parts)


def _grad_w_ada(c_cols, dada_rows):
    def body(c_ref, d_ref, out_ref):
        acc = c_ref[0] * d_ref[0]
        for b in range(1, N_DEV):
            acc = acc + c_ref[b] * d_ref[b]
        out_ref[...] = acc

    return pl.pallas_call(
        body, name="grad_w_ada",
        out_shape=jax.ShapeDtypeStruct((D_MODEL, ADA_SHARD), F32),
        compiler_params=_params(),
    )(c_cols, dada_rows)


def _adamw(w, m, v, g_parts, name):
    rows, cols = w.shape
    n_parts = g_parts.shape[0]
    tr = 256 if rows % 256 == 0 else rows
    tc = 256 if (tr == rows and rows > 256 and cols % 256 == 0) else cols
    c1 = 1.0 / (1.0 - ADAM_B1 ** ADAM_STEP)
    c2 = 1.0 / (1.0 - ADAM_B2 ** ADAM_STEP)

    def body(w_ref, m_ref, v_ref, g_ref, go_ref, d_ref, mo_ref, vo_ref):
        g = g_ref[0].astype(F32)
        for p in range(1, n_parts):
            g = g + g_ref[p].astype(F32)
        m_new = ADAM_B1 * m_ref[...] + (1.0 - ADAM_B1) * g
        v_new = ADAM_B2 * v_ref[...] + (1.0 - ADAM_B2) * (g * g)
        go_ref[...] = g
        mo_ref[...] = m_new
        vo_ref[...] = v_new
        d_ref[...] = -ADAM_LR * ((m_new * c1) / (jnp.sqrt(v_new * c2) + ADAM_EPS) + ADAM_WD * w_ref[...])

    blk = pl.BlockSpec((tr, tc), lambda i, j: (i, j))
    return pl.pallas_call(
        body, name=name, grid=(rows // tr, cols // tc),
        in_specs=[blk, blk, blk, pl.BlockSpec((n_parts, tr, tc), lambda i, j: (0, i, j))],
        out_specs=[blk] * 4,
        out_shape=[jax.ShapeDtypeStruct((rows, cols), F32)] * 4,
        compiler_params=_params(("parallel", "parallel")),
    )(w, m, v, g_parts)


_O_Q, _O_K, _O_V, _O_F, _O_ZA, _O_GB, _O_GC, _O_U, _O_ZB, _O_GA, _O_GB2 = (
    0, 512, 1024, 1536, 1544, 2056, 2568, 3080, 3592, 4104, 5128)


def _to_internal(wt_g):
    wf = wt_g.reshape(IN_WIDTH, D_MODEL)
    rows = lambda a, n: wf[a:a + n]
    conv = [rows(base + LANES * c, LANES) for c in range(4) for base in (_O_GB, _O_GC, _O_U, _O_ZB)]
    f = jnp.pad(rows(_O_F, HEADS), ((0, N_FPAD - HEADS), (0, 0)))
    return jnp.concatenate([rows(_O_Q, 512), rows(_O_K, 512), rows(_O_V, 512), rows(_O_ZA, 512), *conv,
                            rows(_O_GA, 1024), rows(_O_GB2, 1024), f], axis=0)


def _from_internal(dwt):
    rows = lambda a, n: dwt[a:a + n]
    conv = lambda k: [rows(OFF_CONV + 4 * LANES * c + LANES * k, LANES) for c in range(4)]
    full = jnp.concatenate([rows(OFF_Q, 512), rows(OFF_K, 512), rows(OFF_V, 512), rows(N_MAIN, HEADS),
                            rows(OFF_ZA, 512), *conv(0), *conv(1), *conv(2), *conv(3),
                            rows(OFF_GA, 1024), rows(OFF_GB, 1024)], axis=0)
    return full.reshape(N_DEV, IN_SHARD, D_MODEL)


def kernel(x, c, w_ada, b_ada, norm_g, w_in, b_f, q_norm_g, k_norm_g, conv_w, w_attn_out, w_conv_out, w_o, loss_target, m_w_ada, m_b_ada, m_norm_g, m_w_in, m_b_f, m_q_norm_g, m_k_norm_g, m_conv_w, m_w_attn_out, m_w_conv_out, m_w_o, v_w_ada, v_b_ada, v_norm_g, v_w_in, v_b_f, v_q_norm_g, v_k_norm_g, v_conv_w, v_w_attn_out, v_w_conv_out, v_w_o):
    me = 4 * lax.axis_index("x") + 2 * lax.axis_index("y") + lax.axis_index("c")
    s = x.shape[1]
    x2, t2 = x[0], loss_target[0]

    cw_g, wa_g, wb_g, wo_g, w_in_g = _gather_two_level(
        [conv_w[0], w_attn_out[0].astype(BF16), w_conv_out[0].astype(BF16), w_o[0].astype(BF16),
         w_in[0].T.astype(BF16)], "gather_weights")
    c_all, ada_g = _ada_exchange(c, w_ada[0])
    ada_mine = lax.dynamic_index_in_dim(ada_g[:, :, 0, :], me, axis=1, keepdims=False)
    ada3 = (ada_mine.reshape(1, 3 * D_MODEL) + b_ada).reshape(3, D_MODEL)
    w_all_t = _to_internal(w_in_g)
    wa = jnp.transpose(wa_g, (1, 0, 2)).reshape(ATTN_W, D_MODEL)
    wb = jnp.transpose(wb_g, (1, 0, 2)).reshape(CONV_W, D_MODEL)
    wo = wo_g.reshape(D_MODEL, D_MODEL)
    cw = jnp.transpose(cw_g, (1, 0, 2)).reshape(3, CONV_W)
    qg = jnp.tile(q_norm_g, (1, HEADS))
    kg = jnp.tile(k_norm_g, (1, HEADS))
    bf_pad = jnp.pad(b_f, ((0, 0), (0, LANES - HEADS)))

    proj, fl, h = _proj_fwd(x2, ada3, norm_g, w_all_t)
    cum = _forget_cumsum(fl, bf_pad)
    qa, ka, va, kt, vt = _qkv_prep(proj, cum, qg, kg)
    attn, oa, qb = _attn_fwd(qa, ka, vt, proj)
    ob = _conv_fwd(proj, cw)
    (dy, dgab, doa, dob, dwo, dwa, dwb, dgate, loss_part) = _tail(oa, ob, proj, x2, t2, ada3, wa, wb, wo)

    do, dza = _attn_bwd_prep(doa, attn, proj)
    dqt, dk, dv = _attn_bwd(qb, ka, kt, va, do)
    dqkv, dqg, dkg, dcum = _qk_norm_bwd(dqt, dk, dv, proj, qg, kg)
    df, dbf = _forget_bwd(dcum, fl, bf_pad)
    dconv4, dcw = _conv_bwd(dob, proj, cw)
    dproj = jnp.concatenate([dqkv, dza, dconv4, dgab, df], axis=1)
    dw_all = _dw_in(h, dproj)

    def by_core(slabs8):
        return jnp.swapaxes(slabs8.reshape((4, 2) + slabs8.shape[1:]), 0, 1).astype(BF16)

    slabs = [by_core(jnp.transpose(dwa.reshape(ATTN_W, N_DEV, LANES), (1, 0, 2))),
             by_core(jnp.transpose(dwb.reshape(CONV_W, N_DEV, LANES), (1, 0, 2))),
             by_core(dwo.reshape(N_DEV, D_MODEL // N_DEV, D_MODEL)),
             by_core(_from_internal(dw_all))]
    theirs = _sibling_swap(slabs, "swap_grads")
    core = lax.axis_index("c").astype(jnp.int32).reshape(1)
    chip_sums = [_pair_sum(m2, t4, core, "pair_sum_" + nm)
                 for m2, t4, nm in zip(slabs, theirs, ("wa", "wb", "wo", "w_in"))]
    (grad_x, dshift, dscale, dnormg), (g_wa_parts, g_wb_parts, g_wo_parts, g_in_parts) = _dh_and_dx(
        dproj, w_all_t, x2, dy, ada3, norm_g, chip_sums)
    vec = jnp.concatenate([dshift, dscale, dgate, dnormg, dbf, dcw.reshape(1, 3 * CONV_W), dqg, dkg], axis=1)
    (vec_all,) = _exchange([vec], True, "gather_small")
    vec_all = vec_all.reshape(N_DEV, vec.shape[1])
    n_main = 4 * D_MODEL + LANES + 3 * CONV_W
    tot, g_qg, g_kg = _sum_small(
        vec_all[:, :n_main],
        vec_all[:, n_main:n_main + ATTN_W].reshape(N_DEV * HEADS, HEAD_DIM),
        vec_all[:, n_main + ATTN_W:].reshape(N_DEV * HEADS, HEAD_DIM))
    g_b_ada = tot[:, 0:3 * D_MODEL]
    g_norm_g = tot[:, 3 * D_MODEL:4 * D_MODEL]
    g_b_f = tot[:, 4 * D_MODEL:4 * D_MODEL + HEADS]
    g_cw_full = tot[:, 4 * D_MODEL + LANES:].reshape(3, CONV_W)
    g_cw = lax.dynamic_slice(g_cw_full, (0, me * (CONV_W // N_DEV)), (3, CONV_W // N_DEV))
    dada_mine = lax.dynamic_slice(vec_all[:, 0:3 * D_MODEL], (0, me * ADA_SHARD), (N_DEV, ADA_SHARD))
    g_w_ada = _grad_w_ada(jnp.transpose(c_all, (0, 2, 1)), dada_mine.reshape(N_DEV, 1, ADA_SHARD))

    upd = {}
    upd["w_ada"] = _adamw(w_ada[0], m_w_ada[0], v_w_ada[0], g_w_ada[None], "adamw_w_ada")
    upd["b_ada"] = _adamw(b_ada, m_b_ada, v_b_ada, g_b_ada[None], "adamw_b_ada")
    upd["norm_g"] = _adamw(norm_g, m_norm_g, v_norm_g, g_norm_g[None], "adamw_norm_g")
    upd["w_in"] = [u.T for u in _adamw(w_in[0].T, m_w_in[0].T, v_w_in[0].T, g_in_parts, "adamw_w_in")]
    upd["b_f"] = _adamw(b_f, m_b_f, v_b_f, g_b_f[None], "adamw_b_f")
    upd["q_norm_g"] = _adamw(q_norm_g, m_q_norm_g, v_q_norm_g, g_qg[None], "adamw_q_norm_g")
    upd["k_norm_g"] = _adamw(k_norm_g, m_k_norm_g, v_k_norm_g, g_kg[None], "adamw_k_norm_g")
    upd["conv_w"] = _adamw(conv_w[0], m_conv_w[0], v_conv_w[0], g_cw[None], "adamw_conv_w")
    upd["w_attn_out"] = _adamw(w_attn_out[0], m_w_attn_out[0], v_w_attn_out[0], g_wa_parts, "adamw_w_attn_out")
    upd["w_conv_out"] = _adamw(w_conv_out[0], m_w_conv_out[0], v_w_conv_out[0], g_wb_parts, "adamw_w_conv_out")
    upd["w_o"] = _adamw(w_o[0], m_w_o[0], v_w_o[0], g_wo_parts, "adamw_w_o")

    names = ["w_ada", "b_ada", "norm_g", "w_in", "b_f", "q_norm_g", "k_norm_g", "conv_w",
             "w_attn_out", "w_conv_out", "w_o"]
    lead = {"w_ada", "w_in", "conv_w", "w_attn_out", "w_conv_out", "w_o"}
    fix = lambda n, a: a[None] if n in lead else a
    loss = lax.psum(loss_part[0, 0], ("x", "y", "c"))
    outs = [loss, grad_x[None]]
    for k in range(4):
        outs += [fix(n, upd[n][k]) for n in names]
    return tuple(outs)
```

```python
import functools

import numpy as np
import jax
import jax.numpy as jnp
from jax import lax
from jax.experimental import pallas as pl
from jax.experimental.pallas import tpu as pltpu

F32 = jnp.float32
BF16 = jnp.bfloat16

D_MODEL = 1024
HEADS = 8
HEAD_DIM = 64
ATTN_W = 512
CONV_W = 512
N_DEV = 8
IN_WIDTH = 6152
IN_SHARD = IN_WIDTH // N_DEV
N_MAIN = 6144
N_FPAD = 128
N_ALL = N_MAIN + N_FPAD
ADA_SHARD = 3 * D_MODEL // N_DEV
EPS = 1e-6
NEG = -1e30

ADAM_LR = 0.001
ADAM_B1 = 0.9
ADAM_B2 = 0.999
ADAM_EPS = 1e-08
ADAM_WD = 0.01
ADAM_STEP = 10

LANES = 128
VMEM_LIMIT = 56 * 1024 * 1024

TM_PROJ = 512
TN_PROJ = 1024
TM_ELEM = 512
TQ = 512
HEADS_PER_STEP = 4
TM_TAIL = 256
TC_CUM = 256
TK_DW = 512
TN_DW = 896
TM_DH = 512
TK_DH = 896

OFF_Q, OFF_K, OFF_V, OFF_ZA, OFF_CONV, OFF_GA, OFF_GB = 0, 512, 1024, 1536, 2048, 4096, 5120


def _params(sem=None):
    return pltpu.CompilerParams(dimension_semantics=sem, vmem_limit_bytes=VMEM_LIMIT)


def _dot(a, b):
    return jnp.dot(a, b, preferred_element_type=F32)


def _dot_nt(a, b):
    return lax.dot_general(a, b, (((1,), (1,)), ((), ())), preferred_element_type=F32)


def _dot_tn(a, b):
    return lax.dot_general(a, b, (((0,), (0,)), ((), ())), preferred_element_type=F32)


def _sigmoid(x):
    return 1.0 / (1.0 + jnp.exp(-x))


def _lane_lo(shape):
    return lax.broadcasted_iota(jnp.int32, shape, len(shape) - 1) < HEAD_DIM


def _seg_sum(z, lo):
    a = jnp.sum(jnp.where(lo, z, 0.0), axis=-1, keepdims=True)
    b = jnp.sum(jnp.where(lo, 0.0, z), axis=-1, keepdims=True)
    return jnp.where(lo, a, b)


def _lane_col(z, lane):
    idx = lax.broadcasted_iota(jnp.int32, z.shape, 1)
    return jnp.sum(jnp.where(idx == lane, z, 0.0), axis=-1, keepdims=True)


def _sub_row(z, row):
    idx = lax.broadcasted_iota(jnp.int32, z.shape, 0)
    return jnp.sum(jnp.where(idx == row, z, 0.0), axis=0, keepdims=True)


def _mesh_pos():
    x, y, c = lax.axis_index("x"), lax.axis_index("y"), lax.axis_index("c")
    return x, y, c, 4 * x + 2 * y + c


def _peer(k, x, y, c):
    px = 1 - x if (k >> 2) & 1 else x
    py = 1 - y if (k >> 1) & 1 else y
    pc = 1 - c if k & 1 else c
    return (px, py, pc), 4 * px + 2 * py + pc


def _exchange(arrs, gather, name):
    n = len(arrs)
    any_spec = pl.BlockSpec(memory_space=pl.ANY)

    def body(*refs):
        ins, outs = refs[:n], refs[n:2 * n]
        send_sems, recv_sems, local_sems = refs[2 * n:]
        x, y, c, me = _mesh_pos()
        copies = []
        for a in range(n):
            own = ins[a] if gather else ins[a].at[me]
            local = pltpu.make_async_copy(own, outs[a].at[me], local_sems.at[a])
            local.start()
            copies.append(local)
            for k in range(1, N_DEV):
                dev, p = _peer(k, x, y, c)
                cp = pltpu.make_async_remote_copy(
                    src_ref=ins[a] if gather else ins[a].at[p],
                    dst_ref=outs[a].at[me],
                    send_sem=send_sems.at[a * (N_DEV - 1) + k - 1],
                    recv_sem=recv_sems.at[a * (N_DEV - 1) + k - 1],
                    device_id=dev, device_id_type=pl.DeviceIdType.MESH)
                cp.start()
                copies.append(cp)
        for cp in copies:
            cp.wait()

    out_shape = [jax.ShapeDtypeStruct((N_DEV,) + a.shape if gather else a.shape, a.dtype) for a in arrs]
    return pl.pallas_call(
        body, name=name, out_shape=out_shape,
        in_specs=[any_spec] * n, out_specs=[any_spec] * n,
        scratch_shapes=[pltpu.SemaphoreType.DMA((n * (N_DEV - 1),)),
                        pltpu.SemaphoreType.DMA((n * (N_DEV - 1),)),
                        pltpu.SemaphoreType.DMA((n,))],
    )(*arrs)


def _gather_two_level(arrs, name):
    n = len(arrs)
    any_spec = pl.BlockSpec(memory_space=pl.ANY)
    per = N_DEV - 1

    def body(*refs):
        ins, outs = refs[:n], refs[n:2 * n]
        send_sems, recv_sems, local_sems = refs[2 * n:]
        x, y, c, me = _mesh_pos()
        sibling = (x, y, 1 - c)
        chips = [(1 - x, y), (x, 1 - y), (1 - x, 1 - y)]

        def copy(a, k, src, blk, to):
            return pltpu.make_async_remote_copy(
                src_ref=src, dst_ref=outs[a].at[blk],
                send_sem=send_sems.at[a * per + k], recv_sem=recv_sems.at[a * per + k],
                device_id=to, device_id_type=pl.DeviceIdType.MESH)

        local = [pltpu.make_async_copy(ins[a], outs[a].at[me], local_sems.at[a]) for a in range(n)]
        for cp in local:
            cp.start()
        first = []
        for a in range(n):
            first.append(copy(a, 0, ins[a], me, sibling))
            first += [copy(a, 1 + j, ins[a], me, (px, py, c)) for j, (px, py) in enumerate(chips)]
        for cp in first:
            cp.start()
        passed = []
        for j, (px, py) in enumerate(chips):
            blk = 4 * px + 2 * py + c
            for a in range(n):
                copy(a, 1 + j, ins[a], blk, (x, y, c)).wait_recv()
                fwd = copy(a, 4 + j, outs[a].at[blk], blk, sibling)
                fwd.start()
                passed.append(fwd)
        for a in range(n):
            copy(a, 0, ins[a], 4 * x + 2 * y + 1 - c, (x, y, c)).wait_recv()
            for j, (px, py) in enumerate(chips):
                copy(a, 4 + j, ins[a], 4 * px + 2 * py + 1 - c, (x, y, c)).wait_recv()
        for cp in first + passed:
            cp.wait_send()
        for cp in local:
            cp.wait()

    return pl.pallas_call(
        body, name=name,
        out_shape=[jax.ShapeDtypeStruct((N_DEV,) + a.shape, a.dtype) for a in arrs],
        in_specs=[any_spec] * n, out_specs=[any_spec] * n,
        scratch_shapes=[pltpu.SemaphoreType.DMA((n * per,)), pltpu.SemaphoreType.DMA((n * per,)),
                        pltpu.SemaphoreType.DMA((n,))],
    )(*arrs)


def _sibling_swap(arrs, name):
    n = len(arrs)
    any_spec = pl.BlockSpec(memory_space=pl.ANY)

    def body(*refs):
        ins, outs = refs[:n], refs[n:2 * n]
        send_sems, recv_sems = refs[2 * n:]
        x, y, c, _ = _mesh_pos()
        copies = [pltpu.make_async_remote_copy(
            src_ref=ins[a].at[1 - c], dst_ref=outs[a], send_sem=send_sems.at[a], recv_sem=recv_sems.at[a],
            device_id=(x, y, 1 - c), device_id_type=pl.DeviceIdType.MESH) for a in range(n)]
        for cp in copies:
            cp.start()
        for cp in copies:
            cp.wait()

    return pl.pallas_call(
        body, name=name,
        out_shape=[jax.ShapeDtypeStruct(a.shape[1:], a.dtype) for a in arrs],
        in_specs=[any_spec] * n, out_specs=[any_spec] * n,
        scratch_shapes=[pltpu.SemaphoreType.DMA((n,)), pltpu.SemaphoreType.DMA((n,))],
    )(*arrs)


def _pair_sum(mine2, theirs, core, name):
    _, _, rows, cols = mine2.shape
    tr = 256 if rows % 256 == 0 else rows

    def body(core_ref, a_ref, b_ref, out_ref):
        out_ref[...] = (a_ref[...].astype(F32) + b_ref[...].astype(F32)).astype(BF16)

    return pl.pallas_call(
        body, name=name,
        grid_spec=pltpu.PrefetchScalarGridSpec(
            num_scalar_prefetch=1, grid=(4, rows // tr),
            in_specs=[pl.BlockSpec((None, None, tr, cols), lambda ch, i, core_: (core_[0], ch, i, 0)),
                      pl.BlockSpec((None, tr, cols), lambda ch, i, core_: (ch, i, 0))],
            out_specs=pl.BlockSpec((None, tr, cols), lambda ch, i, core_: (ch, i, 0))),
        out_shape=jax.ShapeDtypeStruct(theirs.shape, BF16),
        compiler_params=_params(("parallel", "parallel")),
    )(core, mine2, theirs)


def _chip_copies(ins, outs, send_sems, recv_sems, local_sems):
    x, y, c, _ = _mesh_pos()
    my_chip = 2 * x + y
    chips = [(1 - x, y), (x, 1 - y), (1 - x, 1 - y)]
    copies = []
    for a in range(len(ins)):
        copies.append(pltpu.make_async_copy(ins[a].at[my_chip], outs[a].at[my_chip], local_sems.at[a]))
        for j, (px, py) in enumerate(chips):
            copies.append(pltpu.make_async_remote_copy(
                src_ref=ins[a].at[2 * px + py], dst_ref=outs[a].at[my_chip],
                send_sem=send_sems.at[a * 3 + j], recv_sem=recv_sems.at[a * 3 + j],
                device_id=(px, py, c), device_id_type=pl.DeviceIdType.MESH))
    return copies


def _ada_exchange(c_row, w_ada_sh):
    def body(c_ref, w_ref, call_ref, adag_ref, mine_ref, send_sems, recv_sems):
        x, y, c, me = _mesh_pos()

        def copy(phase, k, src, dst):
            dev, _ = _peer(k, x, y, c)
            return pltpu.make_async_remote_copy(
                src_ref=src, dst_ref=dst,
                send_sem=send_sems.at[phase * (N_DEV - 1) + k - 1],
                recv_sem=recv_sems.at[phase * (N_DEV - 1) + k - 1],
                device_id=dev, device_id_type=pl.DeviceIdType.MESH)

        call_ref[me] = c_ref[...]
        first = [copy(0, k, c_ref, call_ref.at[me]) for k in range(1, N_DEV)]
        for cp in first:
            cp.start()
        for cp in first:
            cp.wait()
        wb = w_ref[...].astype(BF16)
        for b in range(N_DEV):
            row = jnp.broadcast_to(call_ref[b], (8, D_MODEL)).astype(BF16)
            mine_ref[b] = _sub_row(_dot(row, wb), 0)
        adag_ref[me] = mine_ref[...]
        second = [copy(1, k, mine_ref, adag_ref.at[me]) for k in range(1, N_DEV)]
        for cp in second:
            cp.start()
        for cp in second:
            cp.wait()

    vm = pl.BlockSpec(memory_space=pltpu.VMEM)
    return pl.pallas_call(
        body, name="ada_exchange",
        out_shape=[jax.ShapeDtypeStruct((N_DEV, 1, D_MODEL), F32),
                   jax.ShapeDtypeStruct((N_DEV, N_DEV, 1, ADA_SHARD), F32)],
        in_specs=[vm, vm], out_specs=[vm, vm],
        scratch_shapes=[pltpu.VMEM((N_DEV, 1, ADA_SHARD), F32),
                        pltpu.SemaphoreType.DMA((2 * (N_DEV - 1),)),
                        pltpu.SemaphoreType.DMA((2 * (N_DEV - 1),))],
        compiler_params=pltpu.CompilerParams(vmem_limit_bytes=VMEM_LIMIT),
    )(c_row, w_ada_sh)


def _proj_fwd(x, ada3, norm_g, w_all_t):
    s = x.shape[0]
    tm, tn = min(TM_PROJ, s), TN_PROJ

    def body(x_ref, ada_ref, g_ref, w_ref, wf_ref, proj_ref, fl_ref, h_ref):
        @pl.when(pl.program_id(1) == 0)
        def _():
            xv = x_ref[...]
            r = lax.rsqrt(jnp.mean(xv * xv, axis=-1, keepdims=True) + EPS)
            hv = ((xv * r) * g_ref[...]) * (1.0 + ada_ref[1:2, :]) + ada_ref[0:1, :]
            hb = hv.astype(BF16)
            h_ref[...] = hb
            fl_ref[...] = _dot_nt(hb, wf_ref[...])
        proj_ref[...] = _dot_nt(h_ref[...], w_ref[...])

    return pl.pallas_call(
        body, name="proj_fwd", grid=(s // tm, N_MAIN // tn),
        in_specs=[pl.BlockSpec((tm, D_MODEL), lambda i, j: (i, 0)),
                  pl.BlockSpec((3, D_MODEL), lambda i, j: (0, 0)),
                  pl.BlockSpec((1, D_MODEL), lambda i, j: (0, 0)),
                  pl.BlockSpec((tn, D_MODEL), lambda i, j: (j, 0)),
                  pl.BlockSpec((N_FPAD, D_MODEL), lambda i, j: (N_MAIN // N_FPAD, 0))],
        out_specs=[pl.BlockSpec((tm, tn), lambda i, j: (i, j)),
                   pl.BlockSpec((tm, N_FPAD), lambda i, j: (i, 0)),
                   pl.BlockSpec((tm, D_MODEL), lambda i, j: (i, 0))],
        out_shape=[jax.ShapeDtypeStruct((s, N_MAIN), F32),
                   jax.ShapeDtypeStruct((s, N_FPAD), F32),
                   jax.ShapeDtypeStruct((s, D_MODEL), BF16)],
        compiler_params=_params(("parallel", "arbitrary")),
    )(x, ada3, norm_g, w_all_t, w_all_t)


L_ONE_Q, L_F_Q, L_LSE_Q, L_END = HEAD_DIM, HEAD_DIM + 3, HEAD_DIM + 6, HEAD_DIM + 9


def _split3(f):
    hi = f.astype(BF16).astype(F32)
    r = f - hi
    mid = r.astype(BF16).astype(F32)
    return hi, mid, r - mid


def _place3(lane, first, parts, otherwise):
    a, b, c = parts
    return jnp.where(lane == first, a, jnp.where(lane == first + 1, b, jnp.where(lane == first + 2, c, otherwise)))


def _qkv_prep(proj, cum, qg, kg):
    s = proj.shape[0]
    tm = min(TM_ELEM, s)
    scale = HEAD_DIM ** -0.5

    def body(p_ref, cum_ref, qg_ref, kg_ref, qa_ref, ka_ref, va_ref, kt_ref, vt_ref):
        lane = lax.broadcasted_iota(jnp.int32, (tm, LANES), 1)
        lo = lane < HEAD_DIM
        cum_v = cum_ref[...]
        v_tail = jnp.where(lane < L_F_Q, 1.0, 0.0)
        for pr in range(ATTN_W // LANES):
            sl = slice(pr * LANES, (pr + 1) * LANES)
            q2 = p_ref[:, OFF_Q + pr * LANES:OFF_Q + (pr + 1) * LANES]
            k2 = p_ref[:, OFF_K + pr * LANES:OFF_K + (pr + 1) * LANES]
            v2 = p_ref[:, OFF_V + pr * LANES:OFF_V + (pr + 1) * LANES]
            rq = lax.rsqrt(_seg_sum(q2 * q2, lo) * (1.0 / HEAD_DIM) + EPS)
            rk = lax.rsqrt(_seg_sum(k2 * k2, lo) * (1.0 / HEAD_DIM) + EPS)
            qn = ((q2 * rq) * qg_ref[:, sl]) * scale
            kn = (k2 * rk) * kg_ref[:, sl]
            for hh in range(2):
                h = 2 * pr + hh
                f3 = _split3(_lane_col(cum_v, h))
                qh = qn if hh == 0 else pltpu.roll(qn, HEAD_DIM, 1)
                kh = kn if hh == 0 else pltpu.roll(kn, HEAD_DIM, 1)
                vh = v2 if hh == 0 else pltpu.roll(v2, HEAD_DIM, 1)
                q_tail = jnp.where(lane < L_F_Q, 1.0, _place3(lane, L_F_Q, f3, 0.0))
                k_tail = _place3(lane, L_ONE_Q, tuple(-f for f in f3), jnp.where(lane < L_END, 1.0, 0.0))
                k_row = jnp.where(lo, kh, k_tail)
                v_row = jnp.where(lo, vh, v_tail)
                qa_ref[h] = jnp.where(lo, qh, q_tail).astype(BF16)
                ka_ref[h] = k_row.astype(BF16)
                va_ref[h] = v_row.astype(BF16)
                kt_ref[h] = k_row.T.astype(BF16)
                vt_ref[h] = v_row.T.astype(BF16)

    heads = pl.BlockSpec((HEADS, tm, LANES), lambda i: (0, i, 0))
    heads_t = pl.BlockSpec((HEADS, LANES, tm), lambda i: (0, 0, i))
    vec = pl.BlockSpec((1, ATTN_W), lambda i: (0, 0))
    return pl.pallas_call(
        body, name="qkv_prep", grid=(s // tm,),
        in_specs=[pl.BlockSpec((tm, 3 * ATTN_W), lambda i: (i, 0)),
                  pl.BlockSpec((tm, LANES), lambda i: (i, 0)), vec, vec],
        out_specs=[heads, heads, heads, heads_t, heads_t],
        out_shape=[jax.ShapeDtypeStruct((HEADS, s, LANES), BF16)] * 3
        + [jax.ShapeDtypeStruct((HEADS, LANES, s), BF16)] * 2,
        compiler_params=_params(("parallel",)),
    )(proj, cum, qg, kg)


def _log_forget(fl, bf):
    z = fl + bf
    lf = jnp.minimum(z, 0.0) - jnp.log1p(jnp.exp(-jnp.abs(z)))
    lane = lax.broadcasted_iota(jnp.int32, z.shape, 1)
    return jnp.where(lane < HEADS, lf, 0.0)


def _forget_cumsum(fl, bf_pad):
    s = fl.shape[0]
    tc = min(TC_CUM, s)

    def body(fl_ref, bf_ref, cum_ref, carry):
        @pl.when(pl.program_id(0) == 0)
        def _():
            carry[...] = jnp.zeros_like(carry)
        lf = _log_forget(fl_ref[...], bf_ref[...])
        r = lax.broadcasted_iota(jnp.int32, (tc, tc), 0)
        cidx = lax.broadcasted_iota(jnp.int32, (tc, tc), 1)
        tri = (cidx <= r).astype(F32)
        cs = jnp.dot(tri, lf, preferred_element_type=F32, precision=lax.Precision.HIGHEST) + carry[...]
        cum_ref[...] = cs
        carry[...] = cum_ref[tc - 1:tc, :]

    return pl.pallas_call(
        body, name="forget_cumsum", grid=(s // tc,),
        in_specs=[pl.BlockSpec((tc, LANES), lambda i: (i, 0)),
                  pl.BlockSpec((1, LANES), lambda i: (0, 0))],
        out_specs=pl.BlockSpec((tc, LANES), lambda i: (i, 0)),
        out_shape=jax.ShapeDtypeStruct((s, LANES), F32),
        scratch_shapes=[pltpu.VMEM((1, LANES), F32)],
        compiler_params=_params(("arbitrary",)),
    )(fl, bf_pad)


def _causal_t(t):
    return lax.broadcasted_iota(jnp.int32, (t, t), 0) <= lax.broadcasted_iota(jnp.int32, (t, t), 1)


def _tri_steps(nt, q_major):
    if q_major:
        pairs = [(i, j) for i in range(nt) for j in range(i + 1)]
    else:
        pairs = [(i, j) for j in range(nt) for i in range(j, nt)]
    return (jnp.asarray(np.array([p[0] for p in pairs], np.int32)),
            jnp.asarray(np.array([p[1] for p in pairs], np.int32)))


def _attn_fwd(qa, ka, vt, proj):
    s = qa.shape[1]
    t = min(TQ, s)
    it, jt = _tri_steps(s // t, True)
    hp = HEADS_PER_STEP
    wide = hp * HEAD_DIM
    za_blk = OFF_ZA // wide

    def body(it_ref, jt_ref, q_ref, k_ref, vt_ref, za_ref, attn_ref, oa_ref, qb_ref, m_s, acc_s, pair_s):
        step = pl.program_id(1)
        i, j = it_ref[step], jt_ref[step]

        @pl.when(j == 0)
        def _():
            m_s[...] = jnp.full_like(m_s, NEG)
            acc_s[...] = jnp.zeros_like(acc_s)

        def update(masked):
            for hh in range(hp):
                st = _dot_nt(k_ref[hh], q_ref[hh])
                if masked:
                    st = jnp.where(_causal_t(t), st, NEG)
                m_prev = m_s[hh]
                m_next = jnp.maximum(m_prev, jnp.max(st, axis=0, keepdims=True))
                alpha = jnp.exp(m_prev - m_next)
                pt = jnp.exp(st - m_next).astype(BF16)
                acc_s[hh] = acc_s[hh] * alpha + _dot(vt_ref[hh], pt)
                m_s[hh] = m_next

        @pl.when(j < i)
        def _():
            update(False)

        @pl.when(j == i)
        def _():
            update(True)
            row = lax.broadcasted_iota(jnp.int32, (LANES, t), 0)
            lane = lax.broadcasted_iota(jnp.int32, (t, LANES), 1)
            for hh in range(hp):
                l_row = acc_s[hh, L_ONE_Q:L_ONE_Q + 1, :]
                pair_s[hh * HEAD_DIM:(hh + 1) * HEAD_DIM, :] = acc_s[hh, 0:HEAD_DIM, :] / l_row
                lse3 = _split3(m_s[hh] + jnp.log(l_row))
                tail_t = _place3(row, L_LSE_Q, tuple(-x for x in lse3), 0.0)
                keep_q = jnp.logical_or(lane < L_LSE_Q, lane >= L_END)
                qb_ref[hh] = jnp.where(keep_q, q_ref[hh].astype(F32), tail_t.T).astype(BF16)
            out = pair_s[...].T
            attn_ref[...] = out
            z = za_ref[...]
            oa_ref[...] = (out * (z * _sigmoid(z))).astype(BF16)

    pair_q = pl.BlockSpec((hp, t, LANES), lambda p, n, it_, jt_: (p, it_[n], 0))
    pair_k = pl.BlockSpec((hp, t, LANES), lambda p, n, it_, jt_: (p, jt_[n], 0))
    pair_kt = pl.BlockSpec((hp, LANES, t), lambda p, n, it_, jt_: (p, 0, jt_[n]))
    out_q = pl.BlockSpec((t, wide), lambda p, n, it_, jt_: (it_[n], p))
    return pl.pallas_call(
        body, name="attn_fwd",
        grid_spec=pltpu.PrefetchScalarGridSpec(
            num_scalar_prefetch=2, grid=(HEADS // hp, it.shape[0]),
            in_specs=[pair_q, pair_k, pair_kt,
                      pl.BlockSpec((t, wide), lambda p, n, it_, jt_: (it_[n], za_blk + p))],
            out_specs=[out_q, out_q, pair_q],
            scratch_shapes=[pltpu.VMEM((hp, 1, t), F32), pltpu.VMEM((hp, LANES, t), F32),
                            pltpu.VMEM((wide, t), F32)]),
        out_shape=[jax.ShapeDtypeStruct((s, ATTN_W), F32),
                   jax.ShapeDtypeStruct((s, ATTN_W), BF16),
                   jax.ShapeDtypeStruct((HEADS, s, LANES), BF16)],
        compiler_params=_params(("parallel", "arbitrary")),
    )(it, jt, qa, ka, vt, proj)


def _conv_parts(blk, halo, first, w_ref, tm):
    gb, gc = blk[:, 0:LANES], blk[:, LANES:2 * LANES]
    u, zb = blk[:, 2 * LANES:3 * LANES], blk[:, 3 * LANES:4 * LANES]
    cu = gc * u
    cu_h = jnp.where(first, 0.0, halo[:, LANES:2 * LANES] * halo[:, 2 * LANES:3 * LANES])
    prev1, prev2 = _sub_row(cu_h, 7), _sub_row(cu_h, 6)
    row = lax.broadcasted_iota(jnp.int32, (tm, LANES), 0)
    r1 = jnp.where(row == 0, prev1, pltpu.roll(cu, 1, 0))
    r2 = jnp.where(row == 0, prev2, jnp.where(row == 1, prev1, pltpu.roll(cu, 2, 0)))
    conv = w_ref[2:3, :] * cu + w_ref[1:2, :] * r1 + w_ref[0:1, :] * r2
    return gb, gc, u, zb, cu, r1, r2, conv


def _conv_fwd(proj, conv_w):
    s = proj.shape[0]
    tm = min(TM_ELEM, s)
    cb = OFF_CONV // (4 * LANES)

    def body(p_ref, halo_ref, w_ref, ob_ref):
        first = pl.program_id(1) == 0
        gb, _, _, zb, _, _, _, conv = _conv_parts(p_ref[...], halo_ref[...], first, w_ref, tm)
        ob_ref[...] = (gb * conv * (zb * _sigmoid(zb))).astype(BF16)

    return pl.pallas_call(
        body, name="conv_fwd", grid=(CONV_W // LANES, s // tm),
        in_specs=[pl.BlockSpec((tm, 4 * LANES), lambda c, i: (i, cb + c)),
                  pl.BlockSpec((8, 4 * LANES), lambda c, i: (jnp.maximum(i * (tm // 8) - 1, 0), cb + c)),
                  pl.BlockSpec((3, LANES), lambda c, i: (0, c))],
        out_specs=pl.BlockSpec((tm, LANES), lambda c, i: (i, c)),
        out_shape=jax.ShapeDtypeStruct((s, CONV_W), BF16),
        compiler_params=_params(("parallel", "parallel")),
    )(proj, proj, conv_w)


def _tail(oa, ob, proj, x, target, ada3, wa, wb, wo):
    s = x.shape[0]
    tm = min(TM_TAIL, s)
    gab_blk = OFF_GA // (2 * D_MODEL)

    def body(oa_ref, ob_ref, gab_ref, x_ref, t_ref, ada_ref, wa_ref, wb_ref, wo_ref,
             dy_ref, dgab_ref, doa_ref, dob_ref, dwo_ref, dwa_ref, dwb_ref, dgate_ref, loss_ref):
        @pl.when(pl.program_id(0) == 0)
        def _():
            dwo_ref[...] = jnp.zeros_like(dwo_ref)
            dwa_ref[...] = jnp.zeros_like(dwa_ref)
            dwb_ref[...] = jnp.zeros_like(dwb_ref)
            dgate_ref[...] = jnp.zeros_like(dgate_ref)
            loss_ref[...] = jnp.zeros_like(loss_ref)

        oa_v, ob_v = oa_ref[...], ob_ref[...]
        wa_v, wb_v, wo_v = wa_ref[...], wb_ref[...], wo_ref[...]
        a2 = _dot(oa_v, wa_v)
        b2 = _dot(ob_v, wb_v)
        sa = _sigmoid(gab_ref[:, 0:D_MODEL])
        sb = _sigmoid(gab_ref[:, D_MODEL:2 * D_MODEL])
        mb = (sa * a2 + sb * b2).astype(BF16)
        mo = _dot(mb, wo_v)
        gate = ada_ref[2:3, :]
        err = (x_ref[...] + gate * mo) - t_ref[...]
        dy = err * (1.0 / D_MODEL)
        dy_ref[...] = dy
        loss_ref[...] += 0.5 * jnp.sum(err * err) * (1.0 / D_MODEL)
        dgate_ref[...] += jnp.sum(dy * mo, axis=0, keepdims=True)
        dmo = (dy * gate).astype(BF16)
        dmerged = _dot_nt(dmo, wo_v)
        dwo_ref[...] += _dot_tn(mb, dmo)
        da2 = (dmerged * sa).astype(BF16)
        db2 = (dmerged * sb).astype(BF16)
        dgab_ref[:, 0:D_MODEL] = (dmerged * a2 * (sa * (1.0 - sa))).astype(BF16)
        dgab_ref[:, D_MODEL:2 * D_MODEL] = (dmerged * b2 * (sb * (1.0 - sb))).astype(BF16)
        doa_ref[...] = _dot_nt(da2, wa_v)
        dob_ref[...] = _dot_nt(db2, wb_v)
        dwa_ref[...] += _dot_tn(oa_v, da2)
        dwb_ref[...] += _dot_tn(ob_v, db2)

    half = pl.BlockSpec((tm, ATTN_W), lambda i: (i, 0))
    full = pl.BlockSpec((tm, D_MODEL), lambda i: (i, 0))

    def const(shape):
        return pl.BlockSpec(shape, lambda i: (0, 0))

    return pl.pallas_call(
        body, name="tail", grid=(s // tm,),
        in_specs=[half, half, pl.BlockSpec((tm, 2 * D_MODEL), lambda i: (i, gab_blk)), full, full,
                  const((3, D_MODEL)), const((ATTN_W, D_MODEL)), const((CONV_W, D_MODEL)),
                  const((D_MODEL, D_MODEL))],
        out_specs=[full, pl.BlockSpec((tm, 2 * D_MODEL), lambda i: (i, 0)), half, half,
                   const((D_MODEL, D_MODEL)), const((ATTN_W, D_MODEL)), const((CONV_W, D_MODEL)),
                   const((1, D_MODEL)), const((1, LANES))],
        out_shape=[jax.ShapeDtypeStruct((s, D_MODEL), F32),
                   jax.ShapeDtypeStruct((s, 2 * D_MODEL), BF16),
                   jax.ShapeDtypeStruct((s, ATTN_W), F32),
                   jax.ShapeDtypeStruct((s, CONV_W), F32),
                   jax.ShapeDtypeStruct((D_MODEL, D_MODEL), F32),
                   jax.ShapeDtypeStruct((ATTN_W, D_MODEL), F32),
                   jax.ShapeDtypeStruct((CONV_W, D_MODEL), F32),
                   jax.ShapeDtypeStruct((1, D_MODEL), F32),
                   jax.ShapeDtypeStruct((1, LANES), F32)],
        compiler_params=_params(("arbitrary",)),
    )(oa, ob, proj, x, target, ada3, wa, wb, wo)


def _attn_bwd_prep(doa, attn, proj):
    s = doa.shape[0]
    tm = min(TM_ELEM, s)
    za_blk = OFF_ZA // ATTN_W

    def body(doa_ref, attn_ref, za_ref, do_ref, dza_ref):
        lane = lax.broadcasted_iota(jnp.int32, (tm, LANES), 1)
        lo = lane < HEAD_DIM
        for pr in range(ATTN_W // LANES):
            sl = slice(pr * LANES, (pr + 1) * LANES)
            g, a, z = doa_ref[:, sl], attn_ref[:, sl], za_ref[:, sl]
            sg = _sigmoid(z)
            dat = (g * (z * sg)).astype(BF16).astype(F32)
            prod = dat * a
            dza_ref[:, sl] = (g * a * (sg * (1.0 + z * (1.0 - sg)))).astype(BF16)
            for hh in range(2):
                sel = lo if hh == 0 else jnp.logical_not(lo)
                delta3 = _split3(jnp.sum(jnp.where(sel, prod, 0.0), axis=-1, keepdims=True))
                dh = dat if hh == 0 else pltpu.roll(dat, HEAD_DIM, 1)
                tail = _place3(lane, L_ONE_Q, tuple(-d for d in delta3), 0.0)
                do_ref[2 * pr + hh] = jnp.where(lo, dh, tail).astype(BF16)

    row = pl.BlockSpec((tm, ATTN_W), lambda i: (i, 0))
    return pl.pallas_call(
        body, name="attn_bwd_prep", grid=(s // tm,),
        in_specs=[row, row, pl.BlockSpec((tm, ATTN_W), lambda i: (i, za_blk))],
        out_specs=[pl.BlockSpec((HEADS, tm, LANES), lambda i: (0, i, 0)), row],
        out_shape=[jax.ShapeDtypeStruct((HEADS, s, LANES), BF16),
                   jax.ShapeDtypeStruct((s, ATTN_W), BF16)],
        compiler_params=_params(("parallel",)),
    )(doa, attn, proj)


def _attn_bwd(qb, ka, kt, va, do):
    s = qb.shape[1]
    t = min(TQ, s)
    nt = s // t
    hp = HEADS_PER_STEP
    it, jt = _tri_steps(nt, False)

    def body(it_ref, jt_ref, q_ref, k_ref, kt_ref, v_ref, do_ref, dqt_ref, dk_ref, dv_ref):
        step = pl.program_id(1)
        i, j = it_ref[step], jt_ref[step]

        @pl.when(step == 0)
        def _():
            dqt_ref[...] = jnp.zeros_like(dqt_ref)

        @pl.when(i == j)
        def _():
            dk_ref[...] = jnp.zeros_like(dk_ref)
            dv_ref[...] = jnp.zeros_like(dv_ref)

        def update(masked):
            for hh in range(hp):
                qh, doh = q_ref[hh], do_ref[hh]
                st = _dot_nt(k_ref[hh], qh)
                if masked:
                    st = jnp.where(_causal_t(t), st, NEG)
                pt = jnp.exp(st)
                dst = (pt * _dot_nt(v_ref[hh], doh)).astype(BF16)
                dv_ref[hh] += _dot(pt.astype(BF16), doh)
                dk_ref[hh] += _dot(dst, qh)
                dqt_ref[hh, i] += _dot(kt_ref[hh], dst)

        @pl.when(i > j)
        def _():
            update(False)

        @pl.when(i == j)
        def _():
            update(True)

    pair_q = pl.BlockSpec((hp, t, LANES), lambda p, n, it_, jt_: (p, it_[n], 0))
    pair_k = pl.BlockSpec((hp, t, LANES), lambda p, n, it_, jt_: (p, jt_[n], 0))
    pair_kt = pl.BlockSpec((hp, LANES, t), lambda p, n, it_, jt_: (p, 0, jt_[n]))
    return pl.pallas_call(
        body, name="attn_bwd",
        grid_spec=pltpu.PrefetchScalarGridSpec(
            num_scalar_prefetch=2, grid=(HEADS // hp, it.shape[0]),
            in_specs=[pair_q, pair_k, pair_kt, pair_k, pair_q],
            out_specs=[pl.BlockSpec((hp, nt, LANES, t), lambda p, n, it_, jt_: (p, 0, 0, 0)),
                       pair_k, pair_k]),
        out_shape=[jax.ShapeDtypeStruct((HEADS, nt, LANES, t), F32),
                   jax.ShapeDtypeStruct((HEADS, s, LANES), F32),
                   jax.ShapeDtypeStruct((HEADS, s, LANES), F32)],
        compiler_params=_params(("parallel", "arbitrary")),
    )(it, jt, qb, ka, kt, va, do)


def _forget_bwd(dcum, fl, bf_pad):
    s = fl.shape[0]
    tc = min(TC_CUM, s)
    n = s // tc

    def body(dc_ref, fl_ref, bf_ref, df_ref, dbf_ref, carry):
        @pl.when(pl.program_id(0) == 0)
        def _():
            carry[...] = jnp.zeros_like(carry)
            dbf_ref[...] = jnp.zeros_like(dbf_ref)
        r = lax.broadcasted_iota(jnp.int32, (tc, tc), 0)
        cidx = lax.broadcasted_iota(jnp.int32, (tc, tc), 1)
        tri = (cidx >= r).astype(F32)
        dc = dc_ref[...]
        dlf = jnp.dot(tri, dc, preferred_element_type=F32, precision=lax.Precision.HIGHEST) + carry[...]
        carry[...] += jnp.sum(dc, axis=0, keepdims=True)
        lane = lax.broadcasted_iota(jnp.int32, (tc, LANES), 1)
        dfl = jnp.where(lane < HEADS, dlf * _sigmoid(-(fl_ref[...] + bf_ref[...])), 0.0)
        df_ref[...] = dfl.astype(BF16)
        dbf_ref[...] += jnp.sum(dfl, axis=0, keepdims=True)

    rev = pl.BlockSpec((tc, LANES), lambda i: (n - 1 - i, 0))
    vec = pl.BlockSpec((1, LANES), lambda i: (0, 0))
    return pl.pallas_call(
        body, name="forget_bwd", grid=(n,),
        in_specs=[rev, rev, vec], out_specs=[rev, vec],
        out_shape=[jax.ShapeDtypeStruct((s, LANES), BF16), jax.ShapeDtypeStruct((1, LANES), F32)],
        scratch_shapes=[pltpu.VMEM((1, LANES), F32)],
        compiler_params=_params(("arbitrary",)),
    )(dcum, fl, bf_pad)


def _qk_norm_bwd(dqt, dk, dv, proj, qg, kg):
    s = dv.shape[1]
    tm = dqt.shape[-1]
    scale = HEAD_DIM ** -0.5

    def body(dqt_ref, dk_ref, dv_ref, p_ref, qg_ref, kg_ref, out_ref, dqg_ref, dkg_ref, dcum_ref):
        @pl.when(pl.program_id(0) == 0)
        def _():
            dqg_ref[...] = jnp.zeros_like(dqg_ref)
            dkg_ref[...] = jnp.zeros_like(dkg_ref)
        lane = lax.broadcasted_iota(jnp.int32, (tm, LANES), 1)
        lo = lane < HEAD_DIM
        dq_rows = [dqt_ref[h, 0].T for h in range(HEADS)]
        dcum = jnp.zeros((tm, LANES), F32)
        for h in range(HEADS):
            dcum = jnp.where(lane == h, _lane_col(dq_rows[h], L_F_Q) - _lane_col(dk_ref[h], L_ONE_Q), dcum)
        dcum_ref[...] = dcum

        def pair(a, b):
            return jnp.where(lo, a, pltpu.roll(b, HEAD_DIM, 1))

        def one(raw, dy, g, dg_ref, sl, off):
            r = lax.rsqrt(_seg_sum(raw * raw, lo) * (1.0 / HEAD_DIM) + EPS)
            xhat = raw * r
            dg_ref[:, sl] += jnp.sum(dy * xhat, axis=0, keepdims=True)
            dxh = dy * g
            dx = r * (dxh - xhat * (_seg_sum(dxh * xhat, lo) * (1.0 / HEAD_DIM)))
            out_ref[:, off + sl.start:off + sl.stop] = dx.astype(BF16)

        for pr in range(ATTN_W // LANES):
            sl = slice(pr * LANES, (pr + 1) * LANES)
            dq2 = pair(dq_rows[2 * pr], dq_rows[2 * pr + 1])
            one(p_ref[:, OFF_Q + sl.start:OFF_Q + sl.stop], dq2 * scale, qg_ref[:, sl], dqg_ref, sl, OFF_Q)
            one(p_ref[:, OFF_K + sl.start:OFF_K + sl.stop], pair(dk_ref[2 * pr], dk_ref[2 * pr + 1]),
                kg_ref[:, sl], dkg_ref, sl, OFF_K)
            out_ref[:, OFF_V + sl.start:OFF_V + sl.stop] = pair(dv_ref[2 * pr], dv_ref[2 * pr + 1]).astype(BF16)

    heads = pl.BlockSpec((HEADS, tm, LANES), lambda i: (0, i, 0))
    vec = pl.BlockSpec((1, ATTN_W), lambda i: (0, 0))
    return pl.pallas_call(
        body, name="qk_norm_bwd", grid=(s // tm,),
        in_specs=[pl.BlockSpec((HEADS, 1, LANES, tm), lambda i: (0, i, 0, 0)), heads, heads,
                  pl.BlockSpec((tm, 2 * ATTN_W), lambda i: (i, 0)), vec, vec],
        out_specs=[pl.BlockSpec((tm, 3 * ATTN_W), lambda i: (i, 0)), vec, vec,
                   pl.BlockSpec((tm, LANES), lambda i: (i, 0))],
        out_shape=[jax.ShapeDtypeStruct((s, 3 * ATTN_W), BF16),
                   jax.ShapeDtypeStruct((1, ATTN_W), F32), jax.ShapeDtypeStruct((1, ATTN_W), F32),
                   jax.ShapeDtypeStruct((s, LANES), F32)],
        compiler_params=_params(("arbitrary",)),
    )(dqt, dk, dv, proj, qg, kg)


def _conv_bwd(dob, proj, conv_w):
    s = dob.shape[0]
    tm = min(TM_ELEM, s)
    cb = OFF_CONV // (4 * LANES)
    nblk8 = s // 8

    def body(dob_ref, p_ref, prev_ref, next_ref, dnext_ref, w_ref, out_ref, dw_ref):
        i = pl.program_id(1)

        @pl.when(i == 0)
        def _():
            dw_ref[...] = jnp.zeros_like(dw_ref)
        gb, gc, u, zb, cu, r1, r2, conv = _conv_parts(p_ref[...], prev_ref[...], i == 0, w_ref, tm)
        g = dob_ref[...]
        sg = _sigmoid(zb)
        sz = zb * sg
        dconv = g * gb * sz
        nxt = next_ref[...]
        zn = nxt[:, 3 * LANES:4 * LANES]
        dcn = jnp.where(i == pl.num_programs(1) - 1, 0.0,
                        dnext_ref[...] * nxt[:, 0:LANES] * (zn * _sigmoid(zn)))
        nxt1, nxt2 = _sub_row(dcn, 0), _sub_row(dcn, 1)
        row = lax.broadcasted_iota(jnp.int32, (tm, LANES), 0)
        f1 = jnp.where(row == tm - 1, nxt1, pltpu.roll(dconv, tm - 1, 0))
        f2 = jnp.where(row == tm - 2, nxt1, jnp.where(row == tm - 1, nxt2, pltpu.roll(dconv, tm - 2, 0)))
        dcu = w_ref[2:3, :] * dconv + w_ref[1:2, :] * f1 + w_ref[0:1, :] * f2
        out_ref[:, 0:LANES] = (g * conv * sz).astype(BF16)
        out_ref[:, LANES:2 * LANES] = (dcu * u).astype(BF16)
        out_ref[:, 2 * LANES:3 * LANES] = (dcu * gc).astype(BF16)
        out_ref[:, 3 * LANES:4 * LANES] = (g * gb * conv * (sg * (1.0 + zb * (1.0 - sg)))).astype(BF16)
        w_row = lax.broadcasted_iota(jnp.int32, (3, LANES), 0)
        dw0 = jnp.sum(dconv * r2, axis=0, keepdims=True)
        dw1 = jnp.sum(dconv * r1, axis=0, keepdims=True)
        dw2 = jnp.sum(dconv * cu, axis=0, keepdims=True)
        dw_ref[...] += jnp.where(w_row == 0, dw0, jnp.where(w_row == 1, dw1, dw2))

    nxt_idx = lambda i: jnp.minimum((i + 1) * (tm // 8), nblk8 - 1)
    return pl.pallas_call(
        body, name="conv_bwd", grid=(CONV_W // LANES, s // tm),
        in_specs=[pl.BlockSpec((tm, LANES), lambda c, i: (i, c)),
                  pl.BlockSpec((tm, 4 * LANES), lambda c, i: (i, cb + c)),
                  pl.BlockSpec((8, 4 * LANES), lambda c, i: (jnp.maximum(i * (tm // 8) - 1, 0), cb + c)),
                  pl.BlockSpec((8, 4 * LANES), lambda c, i: (nxt_idx(i), cb + c)),
                  pl.BlockSpec((8, LANES), lambda c, i: (nxt_idx(i), c)),
                  pl.BlockSpec((3, LANES), lambda c, i: (0, c))],
        out_specs=[pl.BlockSpec((tm, 4 * LANES), lambda c, i: (i, c)),
                   pl.BlockSpec((3, LANES), lambda c, i: (0, c))],
        out_shape=[jax.ShapeDtypeStruct((s, 4 * CONV_W), BF16), jax.ShapeDtypeStruct((3, CONV_W), F32)],
        compiler_params=_params(("parallel", "arbitrary")),
    )(dob, proj, proj, proj, dob, conv_w)


def _dw_in(h, dproj):
    s = h.shape[0]
    tk, tn = min(TK_DW, s), TN_DW

    def body(dp_ref, h_ref, out_ref):
        @pl.when(pl.program_id(1) == 0)
        def _():
            out_ref[...] = jnp.zeros_like(out_ref)
        out_ref[...] += _dot_tn(dp_ref[...], h_ref[...])

    return pl.pallas_call(
        body, name="dw_in", grid=(N_ALL // tn, s // tk),
        in_specs=[pl.BlockSpec((tk, tn), lambda n, k: (k, n)),
                  pl.BlockSpec((tk, D_MODEL), lambda n, k: (k, 0))],
        out_specs=pl.BlockSpec((tn, D_MODEL), lambda n, k: (n, 0)),
        out_shape=jax.ShapeDtypeStruct((N_ALL, D_MODEL), F32),
        compiler_params=_params(("parallel", "arbitrary")),
    )(dproj, h)


def _dh_and_dx(dproj, w_all_t, x, dy, ada3, norm_g, chip_sums):
    s = x.shape[0]
    tm, tk = min(TM_DH, s), TK_DH
    nk = N_ALL // tk
    nt = s // tm
    n = len(chip_sums)

    def body(dp_ref, wt_ref, x_ref, dy_ref, ada_ref, g_ref, *rest):
        ins, rest = rest[:n], rest[n:]
        gx_ref, dsh_ref, dsc_ref, dg_ref = rest[:4]
        outs, (acc, send_sems, recv_sems, local_sems) = rest[4:4 + n], rest[4 + n:]
        i, k = pl.program_id(0), pl.program_id(1)

        @pl.when(jnp.logical_and(i == 0, k == 0))
        def _():
            for cp in _chip_copies(ins, outs, send_sems, recv_sems, local_sems):
                cp.start()
            dsh_ref[...] = jnp.zeros_like(dsh_ref)
            dsc_ref[...] = jnp.zeros_like(dsc_ref)
            dg_ref[...] = jnp.zeros_like(dg_ref)

        @pl.when(k == 0)
        def _():
            acc[...] = jnp.zeros_like(acc)
        acc[...] += _dot(dp_ref[...], wt_ref[...])

        @pl.when(k == nk - 1)
        def _():
            dh = acc[...]
            xv = x_ref[...]
            r = lax.rsqrt(jnp.mean(xv * xv, axis=-1, keepdims=True) + EPS)
            xhat = xv * r
            g = g_ref[...]
            one_sc = 1.0 + ada_ref[1:2, :]
            dsh_ref[...] += jnp.sum(dh, axis=0, keepdims=True)
            dsc_ref[...] += jnp.sum(dh * (xhat * g), axis=0, keepdims=True)
            dg_ref[...] += jnp.sum(dh * xhat, axis=0, keepdims=True) * one_sc
            dxh = dh * (g * one_sc)
            dx = r * (dxh - xhat * jnp.mean(dxh * xhat, axis=-1, keepdims=True))
            gx_ref[...] = dy_ref[...] + dx

        @pl.when(jnp.logical_and(i == nt - 1, k == nk - 1))
        def _():
            for cp in _chip_copies(ins, outs, send_sems, recv_sems, local_sems):
                cp.wait()

    full = pl.BlockSpec((tm, D_MODEL), lambda i, k: (i, 0))
    vec = pl.BlockSpec((1, D_MODEL), lambda i, k: (0, 0))
    any_spec = pl.BlockSpec(memory_space=pl.ANY)
    res = pl.pallas_call(
        body, name="dh_dx", grid=(nt, nk),
        in_specs=[pl.BlockSpec((tm, tk), lambda i, k: (i, k)),
                  pl.BlockSpec((tk, D_MODEL), lambda i, k: (k, 0)),
                  full, full, pl.BlockSpec((3, D_MODEL), lambda i, k: (0, 0)), vec] + [any_spec] * n,
        out_specs=[full, vec, vec, vec] + [any_spec] * n,
        out_shape=[jax.ShapeDtypeStruct((s, D_MODEL), F32)] + [jax.ShapeDtypeStruct((1, D_MODEL), F32)] * 3
        + [jax.ShapeDtypeStruct(a.shape, a.dtype) for a in chip_sums],
        scratch_shapes=[pltpu.VMEM((tm, D_MODEL), F32), pltpu.SemaphoreType.DMA((n * 3,)),
                        pltpu.SemaphoreType.DMA((n * 3,)), pltpu.SemaphoreType.DMA((n,))],
        compiler_params=_params(("arbitrary", "arbitrary")),
    )(dproj, w_all_t, x, dy, ada3, norm_g, *chip_sums)
    return res[:4], res[4:]


def _sum_small(vec_all, qg_parts, kg_parts):
    def body(v_ref, q_ref, k_ref, tot_ref, gq_ref, gk_ref):
        tot = v_ref[0:1, :]
        for p in range(1, N_DEV):
            tot = tot + v_ref[p:p + 1, :]
        tot_ref[...] = tot
        gq_ref[...] = jnp.sum(q_ref[...], axis=0, keepdims=True)
        gk_ref[...] = jnp.sum(k_ref[...], axis=0, keepdims=True)

    n = vec_all.shape[-1]
    return pl.pallas_call(
        body, name="sum_small",
        out_shape=[jax.ShapeDtypeStruct((1, n), F32),
                   jax.ShapeDtypeStruct((1, HEAD_DIM), F32), jax.ShapeDtypeStruct((1, HEAD_DIM), F32)],
        compiler_params=_params(),
    )(vec_all, qg_parts, kg_parts)


def _grad_w_ada(c_cols, dada_rows):
    def body(c_ref, d_ref, out_ref):
        acc = c_ref[0] * d_ref[0]
        for b in range(1, N_DEV):
            acc = acc + c_ref[b] * d_ref[b]
        out_ref[...] = acc

    return pl.pallas_call(
        body, name="grad_w_ada",
        out_shape=jax.ShapeDtypeStruct((D_MODEL, ADA_SHARD), F32),
        compiler_params=_params(),
    )(c_cols, dada_rows)


def _adamw(w, m, v, g_parts, name):
    rows, cols = w.shape
    n_parts = g_parts.shape[0]
    tr = 256 if rows % 256 == 0 else rows
    tc = 256 if (tr == rows and rows > 256 and cols % 256 == 0) else cols
    c1 = 1.0 / (1.0 - ADAM_B1 ** ADAM_STEP)
    c2 = 1.0 / (1.0 - ADAM_B2 ** ADAM_STEP)

    def body(w_ref, m_ref, v_ref, g_ref, go_ref, d_ref, mo_ref, vo_ref):
        g = g_ref[0].astype(F32)
        for p in range(1, n_parts):
            g = g + g_ref[p].astype(F32)
        m_new = ADAM_B1 * m_ref[...] + (1.0 - ADAM_B1) * g
        v_new = ADAM_B2 * v_ref[...] + (1.0 - ADAM_B2) * (g * g)
        go_ref[...] = g
        mo_ref[...] = m_new
        vo_ref[...] = v_new
        d_ref[...] = -ADAM_LR * ((m_new * c1) / (jnp.sqrt(v_new * c2) + ADAM_EPS) + ADAM_WD * w_ref[...])

    blk = pl.BlockSpec((tr, tc), lambda i, j: (i, j))
    return pl.pallas_call(
        body, name=name, grid=(rows // tr, cols // tc),
        in_specs=[blk, blk, blk, pl.BlockSpec((n_parts, tr, tc), lambda i, j: (0, i, j))],
        out_specs=[blk] * 4,
        out_shape=[jax.ShapeDtypeStruct((rows, cols), F32)] * 4,
        compiler_params=_params(("parallel", "parallel")),
    )(w, m, v, g_parts)


_O_Q, _O_K, _O_V, _O_F, _O_ZA, _O_GB, _O_GC, _O_U, _O_ZB, _O_GA, _O_GB2 = (
    0, 512, 1024, 1536, 1544, 2056, 2568, 3080, 3592, 4104, 5128)


def _to_internal(wt_g):
    wf = wt_g.reshape(IN_WIDTH, D_MODEL)
    rows = lambda a, n: wf[a:a + n]
    conv = [rows(base + LANES * c, LANES) for c in range(4) for base in (_O_GB, _O_GC, _O_U, _O_ZB)]
    f = jnp.pad(rows(_O_F, HEADS), ((0, N_FPAD - HEADS), (0, 0)))
    return jnp.concatenate([rows(_O_Q, 512), rows(_O_K, 512), rows(_O_V, 512), rows(_O_ZA, 512), *conv,
                            rows(_O_GA, 1024), rows(_O_GB2, 1024), f], axis=0)


def _from_internal(dwt):
    rows = lambda a, n: dwt[a:a + n]
    conv = lambda k: [rows(OFF_CONV + 4 * LANES * c + LANES * k, LANES) for c in range(4)]
    full = jnp.concatenate([rows(OFF_Q, 512), rows(OFF_K, 512), rows(OFF_V, 512), rows(N_MAIN, HEADS),
                            rows(OFF_ZA, 512), *conv(0), *conv(1), *conv(2), *conv(3),
                            rows(OFF_GA, 1024), rows(OFF_GB, 1024)], axis=0)
    return full.reshape(N_DEV, IN_SHARD, D_MODEL)


def kernel(x, c, w_ada, b_ada, norm_g, w_in, b_f, q_norm_g, k_norm_g, conv_w, w_attn_out, w_conv_out, w_o, loss_target, m_w_ada, m_b_ada, m_norm_g, m_w_in, m_b_f, m_q_norm_g, m_k_norm_g, m_conv_w, m_w_attn_out, m_w_conv_out, m_w_o, v_w_ada, v_b_ada, v_norm_g, v_w_in, v_b_f, v_q_norm_g, v_k_norm_g, v_conv_w, v_w_attn_out, v_w_conv_out, v_w_o):
    me = 4 * lax.axis_index("x") + 2 * lax.axis_index("y") + lax.axis_index("c")
    s = x.shape[1]
    x2, t2 = x[0], loss_target[0]

    cw_g, wa_g, wb_g, wo_g, w_in_g = _gather_two_level(
        [conv_w[0], w_attn_out[0].astype(BF16), w_conv_out[0].astype(BF16), w_o[0].astype(BF16),
         w_in[0].T.astype(BF16)], "gather_weights")
    c_all, ada_g = _ada_exchange(c, w_ada[0])
    ada_mine = lax.dynamic_index_in_dim(ada_g[:, :, 0, :], me, axis=1, keepdims=False)
    ada3 = (ada_mine.reshape(1, 3 * D_MODEL) + b_ada).reshape(3, D_MODEL)
    w_all_t = _to_internal(w_in_g)
    wa = jnp.transpose(wa_g, (1, 0, 2)).reshape(ATTN_W, D_MODEL)
    wb = jnp.transpose(wb_g, (1, 0, 2)).reshape(CONV_W, D_MODEL)
    wo = wo_g.reshape(D_MODEL, D_MODEL)
    cw = jnp.transpose(cw_g, (1, 0, 2)).reshape(3, CONV_W)
    qg = jnp.tile(q_norm_g, (1, HEADS))
    kg = jnp.tile(k_norm_g, (1, HEADS))
    bf_pad = jnp.pad(b_f, ((0, 0), (0, LANES - HEADS)))

    proj, fl, h = _proj_fwd(x2, ada3, norm_g, w_all_t)
    cum = _forget_cumsum(fl, bf_pad)
    qa, ka, va, kt, vt = _qkv_prep(proj, cum, qg, kg)
    attn, oa, qb = _attn_fwd(qa, ka, vt, proj)
    ob = _conv_fwd(proj, cw)
    (dy, dgab, doa, dob, dwo, dwa, dwb, dgate, loss_part) = _tail(oa, ob, proj, x2, t2, ada3, wa, wb, wo)

    do, dza = _attn_bwd_prep(doa, attn, proj)
    dqt, dk, dv = _attn_bwd(qb, ka, kt, va, do)
    dqkv, dqg, dkg, dcum = _qk_norm_bwd(dqt, dk, dv, proj, qg, kg)
    df, dbf = _forget_bwd(dcum, fl, bf_pad)
    dconv4, dcw = _conv_bwd(dob, proj, cw)
    dproj = jnp.concatenate([dqkv, dza, dconv4, dgab, df], axis=1)
    dw_all = _dw_in(h, dproj)

    def by_core(slabs8):
        return jnp.swapaxes(slabs8.reshape((4, 2) + slabs8.shape[1:]), 0, 1).astype(BF16)

    slabs = [by_core(jnp.transpose(dwa.reshape(ATTN_W, N_DEV, LANES), (1, 0, 2))),
             by_core(jnp.transpose(dwb.reshape(CONV_W, N_DEV, LANES), (1, 0, 2))),
             by_core(dwo.reshape(N_DEV, D_MODEL // N_DEV, D_MODEL)),
             by_core(_from_internal(dw_all))]
    theirs = _sibling_swap(slabs, "swap_grads")
    core = lax.axis_index("c").astype(jnp.int32).reshape(1)
    chip_sums = [_pair_sum(m2, t4, core, "pair_sum_" + nm)
                 for m2, t4, nm in zip(slabs, theirs, ("wa", "wb", "wo", "w_in"))]
    (grad_x, dshift, dscale, dnormg), (g_wa_parts, g_wb_parts, g_wo_parts, g_in_parts) = _dh_and_dx(
        dproj, w_all_t, x2, dy, ada3, norm_g, chip_sums)
    vec = jnp.concatenate([dshift, dscale, dgate, dnormg, dbf, dcw.reshape(1, 3 * CONV_W), dqg, dkg], axis=1)
    (vec_all,) = _exchange([vec], True, "gather_small")
    vec_all = vec_all.reshape(N_DEV, vec.shape[1])
    n_main = 4 * D_MODEL + LANES + 3 * CONV_W
    tot, g_qg, g_kg = _sum_small(
        vec_all[:, :n_main],
        vec_all[:, n_main:n_main + ATTN_W].reshape(N_DEV * HEADS, HEAD_DIM),
        vec_all[:, n_main + ATTN_W:].reshape(N_DEV * HEADS, HEAD_DIM))
    g_b_ada = tot[:, 0:3 * D_MODEL]
    g_norm_g = tot[:, 3 * D_MODEL:4 * D_MODEL]
    g_b_f = tot[:, 4 * D_MODEL:4 * D_MODEL + HEADS]
    g_cw_full = tot[:, 4 * D_MODEL + LANES:].reshape(3, CONV_W)
    g_cw = lax.dynamic_slice(g_cw_full, (0, me * (CONV_W // N_DEV)), (3, CONV_W // N_DEV))
    dada_mine = lax.dynamic_slice(vec_all[:, 0:3 * D_MODEL], (0, me * ADA_SHARD), (N_DEV, ADA_SHARD))
    g_w_ada = _grad_w_ada(jnp.transpose(c_all, (0, 2, 1)), dada_mine.reshape(N_DEV, 1, ADA_SHARD))

    upd = {}
    upd["w_ada"] = _adamw(w_ada[0], m_w_ada[0], v_w_ada[0], g_w_ada[None], "adamw_w_ada")
    upd["b_ada"] = _adamw(b_ada, m_b_ada, v_b_ada, g_b_ada[None], "adamw_b_ada")
    upd["norm_g"] = _adamw(norm_g, m_norm_g, v_norm_g, g_norm_g[None], "adamw_norm_g")
    upd["w_in"] = [u.T for u in _adamw(w_in[0].T, m_w_in[0].T, v_w_in[0].T, g_in_parts, "adamw_w_in")]
    upd["b_f"] = _adamw(b_f, m_b_f, v_b_f, g_b_f[None], "adamw_b_f")
    upd["q_norm_g"] = _adamw(q_norm_g, m_q_norm_g, v_q_norm_g, g_qg[None], "adamw_q_norm_g")
    upd["k_norm_g"] = _adamw(k_norm_g, m_k_norm_g, v_k_norm_g, g_kg[None], "adamw_k_norm_g")
    upd["conv_w"] = _adamw(conv_w[0], m_conv_w[0], v_conv_w[0], g_cw[None], "adamw_conv_w")
    upd["w_attn_out"] = _adamw(w_attn_out[0], m_w_attn_out[0], v_w_attn_out[0], g_wa_parts, "adamw_w_attn_out")
    upd["w_conv_out"] = _adamw(w_conv_out[0], m_w_conv_out[0], v_w_conv_out[0], g_wb_parts, "adamw_w_conv_out")
    upd["w_o"] = _adamw(w_o[0], m_w_o[0], v_w_o[0], g_wo_parts, "adamw_w_o")

    names = ["w_ada", "b_ada", "norm_g", "w_in", "b_f", "q_norm_g", "k_norm_g", "conv_w",
             "w_attn_out", "w_conv_out", "w_o"]
    lead = {"w_ada", "w_in", "conv_w", "w_attn_out", "w_conv_out", "w_o"}
    fix = lambda n, a: a[None] if n in lead else a
    loss = lax.psum(loss_part[0, 0], ("x", "y", "c"))
    outs = [loss, grad_x[None]]
    for k in range(4):
        outs += [fix(n, upd[n][k]) for n in names]
    return tuple(outs)
```

```python
import functools

import numpy as np
import jax
import jax.numpy as jnp
from jax import lax
from jax.experimental import pallas as pl
from jax.experimental.pallas import tpu as pltpu

F32 = jnp.float32
BF16 = jnp.bfloat16

D_MODEL = 1024
HEADS = 8
HEAD_DIM = 64
ATTN_W = 512
CONV_W = 512
N_DEV = 8
IN_WIDTH = 6152
IN_SHARD = IN_WIDTH // N_DEV
N_MAIN = 6144
N_FPAD = 128
N_ALL = N_MAIN + N_FPAD
ADA_SHARD = 3 * D_MODEL // N_DEV
EPS = 1e-6
NEG = -1e30

ADAM_LR = 0.001
ADAM_B1 = 0.9
ADAM_B2 = 0.999
ADAM_EPS = 1e-08
ADAM_WD = 0.01
ADAM_STEP = 10

LANES = 128
VMEM_LIMIT = 56 * 1024 * 1024

TM_PROJ = 512
TN_PROJ = 1024
TM_ELEM = 512
TQ = 512
HEADS_PER_STEP = 4
TM_TAIL = 256
TC_CUM = 256
TK_DW = 1024
TN_DW = 512
TM_DH = 256
HALO = 16

OFF_Q, OFF_K, OFF_V, OFF_ZA, OFF_CB, OFF_CC, OFF_CU, OFF_CZ, OFF_GA, OFF_GB = (
    0, 512, 1024, 1536, 2048, 2560, 3072, 3584, 4096, 5120)


def _params(sem=None):
    return pltpu.CompilerParams(dimension_semantics=sem, vmem_limit_bytes=VMEM_LIMIT)


def _dot(a, b):
    return jnp.dot(a, b, preferred_element_type=F32)


def _dot_nt(a, b):
    return lax.dot_general(a, b, (((1,), (1,)), ((), ())), preferred_element_type=F32)


def _dot_tn(a, b):
    return lax.dot_general(a, b, (((0,), (0,)), ((), ())), preferred_element_type=F32)


def _sigmoid(x):
    return 1.0 / (1.0 + jnp.exp(-x))


def _lane_lo(shape):
    return lax.broadcasted_iota(jnp.int32, shape, len(shape) - 1) < HEAD_DIM


def _seg_sum(z, lo):
    a = jnp.sum(jnp.where(lo, z, 0.0), axis=-1, keepdims=True)
    b = jnp.sum(jnp.where(lo, 0.0, z), axis=-1, keepdims=True)
    return jnp.where(lo, a, b)


def _lane_col(z, lane):
    idx = lax.broadcasted_iota(jnp.int32, z.shape, 1)
    return jnp.sum(jnp.where(idx == lane, z, 0.0), axis=-1, keepdims=True)


def _sub_row(z, row):
    idx = lax.broadcasted_iota(jnp.int32, z.shape, 0)
    return jnp.sum(jnp.where(idx == row, z, 0.0), axis=0, keepdims=True)


def _mesh_pos():
    x, y, c = lax.axis_index("x"), lax.axis_index("y"), lax.axis_index("c")
    return x, y, c, 4 * x + 2 * y + c


def _peer(k, x, y, c):
    px = 1 - x if (k >> 2) & 1 else x
    py = 1 - y if (k >> 1) & 1 else y
    pc = 1 - c if k & 1 else c
    return (px, py, pc), 4 * px + 2 * py + pc


def _exchange(arrs, gather, name):
    n = len(arrs)
    any_spec = pl.BlockSpec(memory_space=pl.ANY)

    def body(*refs):
        ins, outs = refs[:n], refs[n:2 * n]
        send_sems, recv_sems, local_sems = refs[2 * n:]
        x, y, c, me = _mesh_pos()
        copies = []
        for a in range(n):
            own = ins[a] if gather else ins[a].at[me]
            local = pltpu.make_async_copy(own, outs[a].at[me], local_sems.at[a])
            local.start()
            copies.append(local)
            for k in range(1, N_DEV):
                dev, p = _peer(k, x, y, c)
                cp = pltpu.make_async_remote_copy(
                    src_ref=ins[a] if gather else ins[a].at[p],
                    dst_ref=outs[a].at[me],
                    send_sem=send_sems.at[a * (N_DEV - 1) + k - 1],
                    recv_sem=recv_sems.at[a * (N_DEV - 1) + k - 1],
                    device_id=dev, device_id_type=pl.DeviceIdType.MESH)
                cp.start()
                copies.append(cp)
        for cp in copies:
            cp.wait()

    out_shape = [jax.ShapeDtypeStruct((N_DEV,) + a.shape if gather else a.shape, a.dtype) for a in arrs]
    return pl.pallas_call(
        body, name=name, out_shape=out_shape,
        in_specs=[any_spec] * n, out_specs=[any_spec] * n,
        scratch_shapes=[pltpu.SemaphoreType.DMA((n * (N_DEV - 1),)),
                        pltpu.SemaphoreType.DMA((n * (N_DEV - 1),)),
                        pltpu.SemaphoreType.DMA((n,))],
    )(*arrs)


def _gather_two_level(arrs, name):
    n = len(arrs)
    any_spec = pl.BlockSpec(memory_space=pl.ANY)
    per = N_DEV - 1

    def body(*refs):
        ins, outs = refs[:n], refs[n:2 * n]
        send_sems, recv_sems, local_sems = refs[2 * n:]
        x, y, c, me = _mesh_pos()
        sibling = (x, y, 1 - c)
        chips = [(1 - x, y), (x, 1 - y), (1 - x, 1 - y)]

        def copy(a, k, src, blk, to):
            return pltpu.make_async_remote_copy(
                src_ref=src, dst_ref=outs[a].at[blk],
                send_sem=send_sems.at[a * per + k], recv_sem=recv_sems.at[a * per + k],
                device_id=to, device_id_type=pl.DeviceIdType.MESH)

        local = [pltpu.make_async_copy(ins[a], outs[a].at[me], local_sems.at[a]) for a in range(n)]
        for cp in local:
            cp.start()
        first = []
        for a in range(n):
            first.append(copy(a, 0, ins[a], me, sibling))
            first += [copy(a, 1 + j, ins[a], me, (px, py, c)) for j, (px, py) in enumerate(chips)]
        for cp in first:
            cp.start()
        passed = []
        for j, (px, py) in enumerate(chips):
            blk = 4 * px + 2 * py + c
            for a in range(n):
                copy(a, 1 + j, ins[a], blk, (x, y, c)).wait_recv()
                fwd = copy(a, 4 + j, outs[a].at[blk], blk, sibling)
                fwd.start()
                passed.append(fwd)
        for a in range(n):
            copy(a, 0, ins[a], 4 * x + 2 * y + 1 - c, (x, y, c)).wait_recv()
            for j, (px, py) in enumerate(chips):
                copy(a, 4 + j, ins[a], 4 * px + 2 * py + 1 - c, (x, y, c)).wait_recv()
        for cp in first + passed:
            cp.wait_send()
        for cp in local:
            cp.wait()

    return pl.pallas_call(
        body, name=name,
        out_shape=[jax.ShapeDtypeStruct((N_DEV,) + a.shape, a.dtype) for a in arrs],
        in_specs=[any_spec] * n, out_specs=[any_spec] * n,
        scratch_shapes=[pltpu.SemaphoreType.DMA((n * per,)), pltpu.SemaphoreType.DMA((n * per,)),
                        pltpu.SemaphoreType.DMA((n,))],
    )(*arrs)


def _sibling_swap(arrs, name):
    n = len(arrs)
    any_spec = pl.BlockSpec(memory_space=pl.ANY)

    def body(*refs):
        ins, outs = refs[:n], refs[n:2 * n]
        send_sems, recv_sems = refs[2 * n:]
        x, y, c, _ = _mesh_pos()
        copies = [pltpu.make_async_remote_copy(
            src_ref=ins[a].at[1 - c], dst_ref=outs[a], send_sem=send_sems.at[a], recv_sem=recv_sems.at[a],
            device_id=(x, y, 1 - c), device_id_type=pl.DeviceIdType.MESH) for a in range(n)]
        for cp in copies:
            cp.start()
        for cp in copies:
            cp.wait()

    return pl.pallas_call(
        body, name=name,
        out_shape=[jax.ShapeDtypeStruct(a.shape[1:], a.dtype) for a in arrs],
        in_specs=[any_spec] * n, out_specs=[any_spec] * n,
        scratch_shapes=[pltpu.SemaphoreType.DMA((n,)), pltpu.SemaphoreType.DMA((n,))],
    )(*arrs)


def _pair_sum(mine2, theirs, core, name):
    _, _, rows, cols = mine2.shape
    tr = 256 if rows % 256 == 0 else rows

    def body(core_ref, a_ref, b_ref, out_ref):
        out_ref[...] = (a_ref[...].astype(F32) + b_ref[...].astype(F32)).astype(BF16)

    return pl.pallas_call(
        body, name=name,
        grid_spec=pltpu.PrefetchScalarGridSpec(
            num_scalar_prefetch=1, grid=(4, rows // tr),
            in_specs=[pl.BlockSpec((None, None, tr, cols), lambda ch, i, core_: (core_[0], ch, i, 0)),
                      pl.BlockSpec((None, tr, cols), lambda ch, i, core_: (ch, i, 0))],
            out_specs=pl.BlockSpec((None, tr, cols), lambda ch, i, core_: (ch, i, 0))),
        out_shape=jax.ShapeDtypeStruct(theirs.shape, BF16),
        compiler_params=_params(("parallel", "parallel")),
    )(core, mine2, theirs)


def _chip_copies(ins, outs, send_sems, recv_sems, local_sems):
    x, y, c, _ = _mesh_pos()
    my_chip = 2 * x + y
    chips = [(1 - x, y), (x, 1 - y), (1 - x, 1 - y)]
    copies = []
    for a in range(len(ins)):
        copies.append(pltpu.make_async_copy(ins[a].at[my_chip], outs[a].at[my_chip], local_sems.at[a]))
        for j, (px, py) in enumerate(chips):
            copies.append(pltpu.make_async_remote_copy(
                src_ref=ins[a].at[2 * px + py], dst_ref=outs[a].at[my_chip],
                send_sem=send_sems.at[a * 3 + j], recv_sem=recv_sems.at[a * 3 + j],
                device_id=(px, py, c), device_id_type=pl.DeviceIdType.MESH))
    return copies


def _ada_exchange(c_row, w_ada_sh):
    def body(c_ref, w_ref, call_ref, adag_ref, mine_ref, send_sems, recv_sems):
        x, y, c, me = _mesh_pos()

        def copy(phase, k, src, dst):
            dev, _ = _peer(k, x, y, c)
            return pltpu.make_async_remote_copy(
                src_ref=src, dst_ref=dst,
                send_sem=send_sems.at[phase * (N_DEV - 1) + k - 1],
                recv_sem=recv_sems.at[phase * (N_DEV - 1) + k - 1],
                device_id=dev, device_id_type=pl.DeviceIdType.MESH)

        call_ref[me] = c_ref[...]
        first = [copy(0, k, c_ref, call_ref.at[me]) for k in range(1, N_DEV)]
        for cp in first:
            cp.start()
        for cp in first:
            cp.wait()
        wb = w_ref[...].astype(BF16)
        for b in range(N_DEV):
            row = jnp.broadcast_to(call_ref[b], (8, D_MODEL)).astype(BF16)
            mine_ref[b] = _sub_row(_dot(row, wb), 0)
        adag_ref[me] = mine_ref[...]
        second = [copy(1, k, mine_ref, adag_ref.at[me]) for k in range(1, N_DEV)]
        for cp in second:
            cp.start()
        for cp in second:
            cp.wait()

    vm = pl.BlockSpec(memory_space=pltpu.VMEM)
    return pl.pallas_call(
        body, name="ada_exchange",
        out_shape=[jax.ShapeDtypeStruct((N_DEV, 1, D_MODEL), F32),
                   jax.ShapeDtypeStruct((N_DEV, N_DEV, 1, ADA_SHARD), F32)],
        in_specs=[vm, vm], out_specs=[vm, vm],
        scratch_shapes=[pltpu.VMEM((N_DEV, 1, ADA_SHARD), F32),
                        pltpu.SemaphoreType.DMA((2 * (N_DEV - 1),)),
                        pltpu.SemaphoreType.DMA((2 * (N_DEV - 1),))],
        compiler_params=pltpu.CompilerParams(vmem_limit_bytes=VMEM_LIMIT),
    )(c_row, w_ada_sh)


def _proj_fwd(x, ada3, norm_g, w_all_t):
    s = x.shape[0]
    tm, tn = min(TM_PROJ, s), TN_PROJ

    def body(x_ref, ada_ref, g_ref, w_ref, wf_ref, proj_ref, fl_ref, h_ref):
        @pl.when(pl.program_id(1) == 0)
        def _():
            xv = x_ref[...]
            r = lax.rsqrt(jnp.mean(xv * xv, axis=-1, keepdims=True) + EPS)
            hv = ((xv * r) * g_ref[...]) * (1.0 + ada_ref[1:2, :]) + ada_ref[0:1, :]
            hb = hv.astype(BF16)
            h_ref[...] = hb
            fl_ref[...] = _dot_nt(hb, wf_ref[...])
        proj_ref[...] = _dot_nt(h_ref[...], w_ref[...]).astype(BF16)

    return pl.pallas_call(
        body, name="proj_fwd", grid=(s // tm, N_MAIN // tn),
        in_specs=[pl.BlockSpec((tm, D_MODEL), lambda i, j: (i, 0)),
                  pl.BlockSpec((3, D_MODEL), lambda i, j: (0, 0)),
                  pl.BlockSpec((1, D_MODEL), lambda i, j: (0, 0)),
                  pl.BlockSpec((tn, D_MODEL), lambda i, j: (j, 0)),
                  pl.BlockSpec((N_FPAD, D_MODEL), lambda i, j: (N_MAIN // N_FPAD, 0))],
        out_specs=[pl.BlockSpec((tm, tn), lambda i, j: (i, j)),
                   pl.BlockSpec((tm, N_FPAD), lambda i, j: (i, 0)),
                   pl.BlockSpec((tm, D_MODEL), lambda i, j: (i, 0))],
        out_shape=[jax.ShapeDtypeStruct((s, N_MAIN), BF16),
                   jax.ShapeDtypeStruct((s, N_FPAD), F32),
                   jax.ShapeDtypeStruct((s, D_MODEL), BF16)],
        compiler_params=_params(("parallel", "arbitrary")),
    )(x, ada3, norm_g, w_all_t, w_all_t)


L_ONE_Q, L_F_Q, L_LSE_Q, L_END = HEAD_DIM, HEAD_DIM + 3, HEAD_DIM + 6, HEAD_DIM + 9


def _split3(f):
    hi = f.astype(BF16).astype(F32)
    r = f - hi
    mid = r.astype(BF16).astype(F32)
    return hi, mid, r - mid


def _place3(lane, first, parts, otherwise):
    a, b, c = parts
    return jnp.where(lane == first, a, jnp.where(lane == first + 1, b, jnp.where(lane == first + 2, c, otherwise)))


def _qkv_prep(proj, cum, qg, kg):
    s = proj.shape[0]
    tm = min(TM_ELEM, s)
    scale = HEAD_DIM ** -0.5

    def body(p_ref, cum_ref, qg_ref, kg_ref, qa_ref, ka_ref, va_ref, kt_ref, vt_ref):
        lane = lax.broadcasted_iota(jnp.int32, (tm, LANES), 1)
        lo = lane < HEAD_DIM
        cum_v = cum_ref[...]
        v_tail = jnp.where(lane < L_F_Q, 1.0, 0.0)
        for pr in range(ATTN_W // LANES):
            sl = slice(pr * LANES, (pr + 1) * LANES)
            q2 = p_ref[:, OFF_Q + pr * LANES:OFF_Q + (pr + 1) * LANES].astype(F32)
            k2 = p_ref[:, OFF_K + pr * LANES:OFF_K + (pr + 1) * LANES].astype(F32)
            v2 = p_ref[:, OFF_V + pr * LANES:OFF_V + (pr + 1) * LANES].astype(F32)
            rq = lax.rsqrt(_seg_sum(q2 * q2, lo) * (1.0 / HEAD_DIM) + EPS)
            rk = lax.rsqrt(_seg_sum(k2 * k2, lo) * (1.0 / HEAD_DIM) + EPS)
            qn = ((q2 * rq) * qg_ref[:, sl]) * scale
            kn = (k2 * rk) * kg_ref[:, sl]
            for hh in range(2):
                h = 2 * pr + hh
                f3 = _split3(_lane_col(cum_v, h))
                qh = qn if hh == 0 else pltpu.roll(qn, HEAD_DIM, 1)
                kh = kn if hh == 0 else pltpu.roll(kn, HEAD_DIM, 1)
                vh = v2 if hh == 0 else pltpu.roll(v2, HEAD_DIM, 1)
                q_tail = jnp.where(lane < L_F_Q, 1.0, _place3(lane, L_F_Q, f3, 0.0))
                k_tail = _place3(lane, L_ONE_Q, tuple(-f for f in f3), jnp.where(lane < L_END, 1.0, 0.0))
                k_row = jnp.where(lo, kh, k_tail)
                v_row = jnp.where(lo, vh, v_tail)
                qa_ref[h] = jnp.where(lo, qh, q_tail).astype(BF16)
                ka_ref[h] = k_row.astype(BF16)
                va_ref[h] = v_row.astype(BF16)
                kt_ref[h] = k_row.T.astype(BF16)
                vt_ref[h] = v_row.T.astype(BF16)

    heads = pl.BlockSpec((HEADS, tm, LANES), lambda i: (0, i, 0))
    heads_t = pl.BlockSpec((HEADS, LANES, tm), lambda i: (0, 0, i))
    vec = pl.BlockSpec((1, ATTN_W), lambda i: (0, 0))
    return pl.pallas_call(
        body, name="qkv_prep", grid=(s // tm,),
        in_specs=[pl.BlockSpec((tm, 3 * ATTN_W), lambda i: (i, 0)),
                  pl.BlockSpec((tm, LANES), lambda i: (i, 0)), vec, vec],
        out_specs=[heads, heads, heads, heads_t, heads_t],
        out_shape=[jax.ShapeDtypeStruct((HEADS, s, LANES), BF16)] * 3
        + [jax.ShapeDtypeStruct((HEADS, LANES, s), BF16)] * 2,
        compiler_params=_params(("parallel",)),
    )(proj, cum, qg, kg)


def _log_forget(fl, bf):
    z = fl + bf
    lf = jnp.minimum(z, 0.0) - jnp.log1p(jnp.exp(-jnp.abs(z)))
    lane = lax.broadcasted_iota(jnp.int32, z.shape, 1)
    return jnp.where(lane < HEADS, lf, 0.0)


def _forget_cumsum(fl, bf_pad):
    s = fl.shape[0]
    tc = min(TC_CUM, s)

    def body(fl_ref, bf_ref, cum_ref, carry):
        @pl.when(pl.program_id(0) == 0)
        def _():
            carry[...] = jnp.zeros_like(carry)
        lf = _log_forget(fl_ref[...], bf_ref[...])
        r = lax.broadcasted_iota(jnp.int32, (tc, tc), 0)
        cidx = lax.broadcasted_iota(jnp.int32, (tc, tc), 1)
        tri = (cidx <= r).astype(F32)
        cs = jnp.dot(tri, lf, preferred_element_type=F32, precision=lax.Precision.HIGHEST) + carry[...]
        cum_ref[...] = cs
        carry[...] = cum_ref[tc - 1:tc, :]

    return pl.pallas_call(
        body, name="forget_cumsum", grid=(s // tc,),
        in_specs=[pl.BlockSpec((tc, LANES), lambda i: (i, 0)),
                  pl.BlockSpec((1, LANES), lambda i: (0, 0))],
        out_specs=pl.BlockSpec((tc, LANES), lambda i: (i, 0)),
        out_shape=jax.ShapeDtypeStruct((s, LANES), F32),
        scratch_shapes=[pltpu.VMEM((1, LANES), F32)],
        compiler_params=_params(("arbitrary",)),
    )(fl, bf_pad)


def _causal_t(t):
    return lax.broadcasted_iota(jnp.int32, (t, t), 0) <= lax.broadcasted_iota(jnp.int32, (t, t), 1)


def _tri_steps(nt, q_major):
    if q_major:
        pairs = [(i, j) for i in range(nt) for j in range(i + 1)]
    else:
        pairs = [(i, j) for j in range(nt) for i in range(j, nt)]
    return (jnp.asarray(np.array([p[0] for p in pairs], np.int32)),
            jnp.asarray(np.array([p[1] for p in pairs], np.int32)))


def _attn_fwd(qa, ka, vt, proj):
    s = qa.shape[1]
    t = min(TQ, s)
    it, jt = _tri_steps(s // t, True)
    hp = HEADS_PER_STEP
    wide = hp * HEAD_DIM
    za_blk = OFF_ZA // wide

    def body(it_ref, jt_ref, q_ref, k_ref, vt_ref, za_ref, attn_ref, oa_ref, qb_ref, m_s, acc_s, pair_s):
        step = pl.program_id(1)
        i, j = it_ref[step], jt_ref[step]

        @pl.when(j == 0)
        def _():
            m_s[...] = jnp.full_like(m_s, NEG)
            acc_s[...] = jnp.zeros_like(acc_s)

        def update(masked):
            for hh in range(hp):
                st = _dot_nt(k_ref[hh], q_ref[hh])
                if masked:
                    st = jnp.where(_causal_t(t), st, NEG)
                m_prev = m_s[hh]
                m_next = jnp.maximum(m_prev, jnp.max(st, axis=0, keepdims=True))
                alpha = jnp.exp(m_prev - m_next)
                pt = jnp.exp(st - m_next).astype(BF16)
                acc_s[hh] = acc_s[hh] * alpha + _dot(vt_ref[hh], pt)
                m_s[hh] = m_next

        @pl.when(j < i)
        def _():
            update(False)

        @pl.when(j == i)
        def _():
            update(True)
            row = lax.broadcasted_iota(jnp.int32, (LANES, t), 0)
            lane = lax.broadcasted_iota(jnp.int32, (t, LANES), 1)
            for hh in range(hp):
                l_row = acc_s[hh, L_ONE_Q:L_ONE_Q + 1, :]
                pair_s[hh * HEAD_DIM:(hh + 1) * HEAD_DIM, :] = acc_s[hh, 0:HEAD_DIM, :] / l_row
                lse3 = _split3(m_s[hh] + jnp.log(l_row))
                tail_t = _place3(row, L_LSE_Q, tuple(-x for x in lse3), 0.0)
                keep_q = jnp.logical_or(lane < L_LSE_Q, lane >= L_END)
                qb_ref[hh] = jnp.where(keep_q, q_ref[hh].astype(F32), tail_t.T).astype(BF16)
            out = pair_s[...].T
            attn_ref[...] = out
            z = za_ref[...].astype(F32)
            oa_ref[...] = (out * (z * _sigmoid(z))).astype(BF16)

    pair_q = pl.BlockSpec((hp, t, LANES), lambda p, n, it_, jt_: (p, it_[n], 0))
    pair_k = pl.BlockSpec((hp, t, LANES), lambda p, n, it_, jt_: (p, jt_[n], 0))
    pair_kt = pl.BlockSpec((hp, LANES, t), lambda p, n, it_, jt_: (p, 0, jt_[n]))
    out_q = pl.BlockSpec((t, wide), lambda p, n, it_, jt_: (it_[n], p))
    return pl.pallas_call(
        body, name="attn_fwd",
        grid_spec=pltpu.PrefetchScalarGridSpec(
            num_scalar_prefetch=2, grid=(HEADS // hp, it.shape[0]),
            in_specs=[pair_q, pair_k, pair_kt,
                      pl.BlockSpec((t, wide), lambda p, n, it_, jt_: (it_[n], za_blk + p))],
            out_specs=[out_q, out_q, pair_q],
            scratch_shapes=[pltpu.VMEM((hp, 1, t), F32), pltpu.VMEM((hp, LANES, t), F32),
                            pltpu.VMEM((wide, t), F32)]),
        out_shape=[jax.ShapeDtypeStruct((s, ATTN_W), F32),
                   jax.ShapeDtypeStruct((s, ATTN_W), BF16),
                   jax.ShapeDtypeStruct((HEADS, s, LANES), BF16)],
        compiler_params=_params(("parallel", "arbitrary")),
    )(it, jt, qa, ka, vt, proj)


def _conv_parts(gb_ref, gc_ref, u_ref, zb_ref, gch_ref, uh_ref, first, w_ref, tm):
    gb, gc = gb_ref[...].astype(F32), gc_ref[...].astype(F32)
    u, zb = u_ref[...].astype(F32), zb_ref[...].astype(F32)
    cu = gc * u
    cu_h = jnp.where(first, 0.0, gch_ref[...].astype(F32) * uh_ref[...].astype(F32))
    prev1, prev2 = _sub_row(cu_h, HALO - 1), _sub_row(cu_h, HALO - 2)
    row = lax.broadcasted_iota(jnp.int32, (tm, LANES), 0)
    r1 = jnp.where(row == 0, prev1, pltpu.roll(cu, 1, 0))
    r2 = jnp.where(row == 0, prev2, jnp.where(row == 1, prev1, pltpu.roll(cu, 2, 0)))
    conv = w_ref[2:3, :] * cu + w_ref[1:2, :] * r1 + w_ref[0:1, :] * r2
    return gb, gc, u, zb, cu, r1, r2, conv


def _conv_specs(tm, s):
    def tile(off):
        return pl.BlockSpec((tm, LANES), lambda c, i: (i, off // LANES + c))

    def before(off):
        return pl.BlockSpec((HALO, LANES), lambda c, i: (jnp.maximum(i * (tm // HALO) - 1, 0), off // LANES + c))

    def after(off):
        return pl.BlockSpec((HALO, LANES),
                            lambda c, i: (jnp.minimum((i + 1) * (tm // HALO), s // HALO - 1), off // LANES + c))

    return ([tile(OFF_CB), tile(OFF_CC), tile(OFF_CU), tile(OFF_CZ)], [before(OFF_CC), before(OFF_CU)],
            [after(OFF_CB), after(OFF_CZ)])


def _conv_fwd(proj, conv_w):
    s = proj.shape[0]
    tm = min(TM_ELEM, s)
    tiles, befores, _ = _conv_specs(tm, s)

    def body(gb_ref, gc_ref, u_ref, zb_ref, gch_ref, uh_ref, w_ref, ob_ref):
        first = pl.program_id(1) == 0
        gb, _, _, zb, _, _, _, conv = _conv_parts(gb_ref, gc_ref, u_ref, zb_ref, gch_ref, uh_ref, first, w_ref, tm)
        ob_ref[...] = (gb * conv * (zb * _sigmoid(zb))).astype(BF16)

    return pl.pallas_call(
        body, name="conv_fwd", grid=(CONV_W // LANES, s // tm),
        in_specs=tiles + befores + [pl.BlockSpec((3, LANES), lambda c, i: (0, c))],
        out_specs=pl.BlockSpec((tm, LANES), lambda c, i: (i, c)),
        out_shape=jax.ShapeDtypeStruct((s, CONV_W), BF16),
        compiler_params=_params(("parallel", "parallel")),
    )(*([proj] * 6), conv_w)


def _tail(oa, ob, proj, x, target, ada3, wa, wb, wo):
    s = x.shape[0]
    tm = min(TM_TAIL, s)
    gab_blk = OFF_GA // (2 * D_MODEL)

    def body(oa_ref, ob_ref, gab_ref, x_ref, t_ref, ada_ref, wa_ref, wb_ref, wo_ref,
             dy_ref, dgab_ref, doa_ref, dob_ref, dwo_ref, dwa_ref, dwb_ref, dgate_ref, loss_ref):
        @pl.when(pl.program_id(0) == 0)
        def _():
            dwo_ref[...] = jnp.zeros_like(dwo_ref)
            dwa_ref[...] = jnp.zeros_like(dwa_ref)
            dwb_ref[...] = jnp.zeros_like(dwb_ref)
            dgate_ref[...] = jnp.zeros_like(dgate_ref)
            loss_ref[...] = jnp.zeros_like(loss_ref)

        oa_v, ob_v = oa_ref[...], ob_ref[...]
        wa_v, wb_v, wo_v = wa_ref[...], wb_ref[...], wo_ref[...]
        a2 = _dot(oa_v, wa_v)
        b2 = _dot(ob_v, wb_v)
        sa = _sigmoid(gab_ref[:, 0:D_MODEL].astype(F32))
        sb = _sigmoid(gab_ref[:, D_MODEL:2 * D_MODEL].astype(F32))
        mb = (sa * a2 + sb * b2).astype(BF16)
        mo = _dot(mb, wo_v)
        gate = ada_ref[2:3, :]
        err = (x_ref[...] + gate * mo) - t_ref[...]
        dy = err * (1.0 / D_MODEL)
        dy_ref[...] = dy
        loss_ref[...] += 0.5 * jnp.sum(err * err) * (1.0 / D_MODEL)
        dgate_ref[...] += jnp.sum(dy * mo, axis=0, keepdims=True)
        dmo = (dy * gate).astype(BF16)
        dmerged = _dot_nt(dmo, wo_v)
        dwo_ref[...] += _dot_tn(mb, dmo)
        da2 = (dmerged * sa).astype(BF16)
        db2 = (dmerged * sb).astype(BF16)
        dgab_ref[:, 0:D_MODEL] = (dmerged * a2 * (sa * (1.0 - sa))).astype(BF16)
        dgab_ref[:, D_MODEL:2 * D_MODEL] = (dmerged * b2 * (sb * (1.0 - sb))).astype(BF16)
        doa_ref[...] = _dot_nt(da2, wa_v)
        dob_ref[...] = _dot_nt(db2, wb_v)
        dwa_ref[...] += _dot_tn(oa_v, da2)
        dwb_ref[...] += _dot_tn(ob_v, db2)

    half = pl.BlockSpec((tm, ATTN_W), lambda i: (i, 0))
    full = pl.BlockSpec((tm, D_MODEL), lambda i: (i, 0))

    def const(shape):
        return pl.BlockSpec(shape, lambda i: (0, 0))

    return pl.pallas_call(
        body, name="tail", grid=(s // tm,),
        in_specs=[half, half, pl.BlockSpec((tm, 2 * D_MODEL), lambda i: (i, gab_blk)), full, full,
                  const((3, D_MODEL)), const((ATTN_W, D_MODEL)), const((CONV_W, D_MODEL)),
                  const((D_MODEL, D_MODEL))],
        out_specs=[full, pl.BlockSpec((tm, 2 * D_MODEL), lambda i: (i, 0)), half, half,
                   const((D_MODEL, D_MODEL)), const((ATTN_W, D_MODEL)), const((CONV_W, D_MODEL)),
                   const((1, D_MODEL)), const((1, LANES))],
        out_shape=[jax.ShapeDtypeStruct((s, D_MODEL), F32),
                   jax.ShapeDtypeStruct((s, 2 * D_MODEL), BF16),
                   jax.ShapeDtypeStruct((s, ATTN_W), F32),
                   jax.ShapeDtypeStruct((s, CONV_W), F32),
                   jax.ShapeDtypeStruct((D_MODEL, D_MODEL), F32),
                   jax.ShapeDtypeStruct((ATTN_W, D_MODEL), F32),
                   jax.ShapeDtypeStruct((CONV_W, D_MODEL), F32),
                   jax.ShapeDtypeStruct((1, D_MODEL), F32),
                   jax.ShapeDtypeStruct((1, LANES), F32)],
        compiler_params=_params(("arbitrary",)),
    )(oa, ob, proj, x, target, ada3, wa, wb, wo)


def _attn_bwd_prep(doa, attn, proj):
    s = doa.shape[0]
    tm = min(TM_ELEM, s)
    za_blk = OFF_ZA // ATTN_W

    def body(doa_ref, attn_ref, za_ref, do_ref, dza_ref):
        lane = lax.broadcasted_iota(jnp.int32, (tm, LANES), 1)
        lo = lane < HEAD_DIM
        for pr in range(ATTN_W // LANES):
            sl = slice(pr * LANES, (pr + 1) * LANES)
            g, a, z = doa_ref[:, sl], attn_ref[:, sl], za_ref[:, sl].astype(F32)
            sg = _sigmoid(z)
            dat = (g * (z * sg)).astype(BF16).astype(F32)
            prod = dat * a
            dza_ref[:, sl] = (g * a * (sg * (1.0 + z * (1.0 - sg)))).astype(BF16)
            for hh in range(2):
                sel = lo if hh == 0 else jnp.logical_not(lo)
                delta3 = _split3(jnp.sum(jnp.where(sel, prod, 0.0), axis=-1, keepdims=True))
                dh = dat if hh == 0 else pltpu.roll(dat, HEAD_DIM, 1)
                tail = _place3(lane, L_ONE_Q, tuple(-d for d in delta3), 0.0)
                do_ref[2 * pr + hh] = jnp.where(lo, dh, tail).astype(BF16)

    row = pl.BlockSpec((tm, ATTN_W), lambda i: (i, 0))
    return pl.pallas_call(
        body, name="attn_bwd_prep", grid=(s // tm,),
        in_specs=[row, row, pl.BlockSpec((tm, ATTN_W), lambda i: (i, za_blk))],
        out_specs=[pl.BlockSpec((HEADS, tm, LANES), lambda i: (0, i, 0)), row],
        out_shape=[jax.ShapeDtypeStruct((HEADS, s, LANES), BF16),
                   jax.ShapeDtypeStruct((s, ATTN_W), BF16)],
        compiler_params=_params(("parallel",)),
    )(doa, attn, proj)


def _attn_bwd(qb, ka, kt, va, do):
    s = qb.shape[1]
    t = min(TQ, s)
    nt = s // t
    hp = HEADS_PER_STEP
    it, jt = _tri_steps(nt, False)

    def body(it_ref, jt_ref, q_ref, k_ref, kt_ref, v_ref, do_ref, dqt_ref, dk_ref, dv_ref):
        step = pl.program_id(1)
        i, j = it_ref[step], jt_ref[step]

        @pl.when(step == 0)
        def _():
            dqt_ref[...] = jnp.zeros_like(dqt_ref)

        @pl.when(i == j)
        def _():
            dk_ref[...] = jnp.zeros_like(dk_ref)
            dv_ref[...] = jnp.zeros_like(dv_ref)

        def update(masked):
            for hh in range(hp):
                qh, doh = q_ref[hh], do_ref[hh]
                st = _dot_nt(k_ref[hh], qh)
                if masked:
                    st = jnp.where(_causal_t(t), st, NEG)
                pt = jnp.exp(st)
                dst = (pt * _dot_nt(v_ref[hh], doh)).astype(BF16)
                dv_ref[hh] += _dot(pt.astype(BF16), doh)
                dk_ref[hh] += _dot(dst, qh)
                dqt_ref[hh, i] += _dot(kt_ref[hh], dst)

        @pl.when(i > j)
        def _():
            update(False)

        @pl.when(i == j)
        def _():
            update(True)

    pair_q = pl.BlockSpec((hp, t, LANES), lambda p, n, it_, jt_: (p, it_[n], 0))
    pair_k = pl.BlockSpec((hp, t, LANES), lambda p, n, it_, jt_: (p, jt_[n], 0))
    pair_kt = pl.BlockSpec((hp, LANES, t), lambda p, n, it_, jt_: (p, 0, jt_[n]))
    return pl.pallas_call(
        body, name="attn_bwd",
        grid_spec=pltpu.PrefetchScalarGridSpec(
            num_scalar_prefetch=2, grid=(HEADS // hp, it.shape[0]),
            in_specs=[pair_q, pair_k, pair_kt, pair_k, pair_q],
            out_specs=[pl.BlockSpec((hp, nt, LANES, t), lambda p, n, it_, jt_: (p, 0, 0, 0)),
                       pair_k, pair_k]),
        out_shape=[jax.ShapeDtypeStruct((HEADS, nt, LANES, t), F32),
                   jax.ShapeDtypeStruct((HEADS, s, LANES), F32),
                   jax.ShapeDtypeStruct((HEADS, s, LANES), F32)],
        compiler_params=_params(("parallel", "arbitrary")),
    )(it, jt, qb, ka, kt, va, do)


def _forget_bwd(dcum, fl, bf_pad):
    s = fl.shape[0]
    tc = min(TC_CUM, s)
    n = s // tc

    def body(dc_ref, fl_ref, bf_ref, df_ref, dbf_ref, carry):
        @pl.when(pl.program_id(0) == 0)
        def _():
            carry[...] = jnp.zeros_like(carry)
            dbf_ref[...] = jnp.zeros_like(dbf_ref)
        r = lax.broadcasted_iota(jnp.int32, (tc, tc), 0)
        cidx = lax.broadcasted_iota(jnp.int32, (tc, tc), 1)
        tri = (cidx >= r).astype(F32)
        dc = dc_ref[...]
        dlf = jnp.dot(tri, dc, preferred_element_type=F32, precision=lax.Precision.HIGHEST) + carry[...]
        carry[...] += jnp.sum(dc, axis=0, keepdims=True)
        lane = lax.broadcasted_iota(jnp.int32, (tc, LANES), 1)
        dfl = jnp.where(lane < HEADS, dlf * _sigmoid(-(fl_ref[...] + bf_ref[...])), 0.0)
        df_ref[...] = dfl.astype(BF16)
        dbf_ref[...] += jnp.sum(dfl, axis=0, keepdims=True)

    rev = pl.BlockSpec((tc, LANES), lambda i: (n - 1 - i, 0))
    vec = pl.BlockSpec((1, LANES), lambda i: (0, 0))
    return pl.pallas_call(
        body, name="forget_bwd", grid=(n,),
        in_specs=[rev, rev, vec], out_specs=[rev, vec],
        out_shape=[jax.ShapeDtypeStruct((s, LANES), BF16), jax.ShapeDtypeStruct((1, LANES), F32)],
        scratch_shapes=[pltpu.VMEM((1, LANES), F32)],
        compiler_params=_params(("arbitrary",)),
    )(dcum, fl, bf_pad)


def _qk_norm_bwd(dqt, dk, dv, proj, qg, kg):
    s = dv.shape[1]
    tm = dqt.shape[-1]
    scale = HEAD_DIM ** -0.5

    def body(dqt_ref, dk_ref, dv_ref, p_ref, qg_ref, kg_ref, out_ref, dqg_ref, dkg_ref, dcum_ref):
        @pl.when(pl.program_id(0) == 0)
        def _():
            dqg_ref[...] = jnp.zeros_like(dqg_ref)
            dkg_ref[...] = jnp.zeros_like(dkg_ref)
        lane = lax.broadcasted_iota(jnp.int32, (tm, LANES), 1)
        lo = lane < HEAD_DIM
        dq_rows = [dqt_ref[h, 0].T for h in range(HEADS)]
        dcum = jnp.zeros((tm, LANES), F32)
        for h in range(HEADS):
            dcum = jnp.where(lane == h, _lane_col(dq_rows[h], L_F_Q) - _lane_col(dk_ref[h], L_ONE_Q), dcum)
        dcum_ref[...] = dcum

        def pair(a, b):
            return jnp.where(lo, a, pltpu.roll(b, HEAD_DIM, 1))

        def one(raw, dy, g, dg_ref, sl, off):
            r = lax.rsqrt(_seg_sum(raw * raw, lo) * (1.0 / HEAD_DIM) + EPS)
            xhat = raw * r
            dg_ref[:, sl] += jnp.sum(dy * xhat, axis=0, keepdims=True)
            dxh = dy * g
            dx = r * (dxh - xhat * (_seg_sum(dxh * xhat, lo) * (1.0 / HEAD_DIM)))
            out_ref[:, off + sl.start:off + sl.stop] = dx.astype(BF16)

        for pr in range(ATTN_W // LANES):
            sl = slice(pr * LANES, (pr + 1) * LANES)
            dq2 = pair(dq_rows[2 * pr], dq_rows[2 * pr + 1])
            one(p_ref[:, OFF_Q + sl.start:OFF_Q + sl.stop].astype(F32), dq2 * scale, qg_ref[:, sl], dqg_ref, sl, OFF_Q)
            one(p_ref[:, OFF_K + sl.start:OFF_K + sl.stop].astype(F32), pair(dk_ref[2 * pr], dk_ref[2 * pr + 1]),
                kg_ref[:, sl], dkg_ref, sl, OFF_K)
            out_ref[:, OFF_V + sl.start:OFF_V + sl.stop] = pair(dv_ref[2 * pr], dv_ref[2 * pr + 1]).astype(BF16)

    heads = pl.BlockSpec((HEADS, tm, LANES), lambda i: (0, i, 0))
    vec = pl.BlockSpec((1, ATTN_W), lambda i: (0, 0))
    return pl.pallas_call(
        body, name="qk_norm_bwd", grid=(s // tm,),
        in_specs=[pl.BlockSpec((HEADS, 1, LANES, tm), lambda i: (0, i, 0, 0)), heads, heads,
                  pl.BlockSpec((tm, 2 * ATTN_W), lambda i: (i, 0)), vec, vec],
        out_specs=[pl.BlockSpec((tm, 3 * ATTN_W), lambda i: (i, 0)), vec, vec,
                   pl.BlockSpec((tm, LANES), lambda i: (i, 0))],
        out_shape=[jax.ShapeDtypeStruct((s, 3 * ATTN_W), BF16),
                   jax.ShapeDtypeStruct((1, ATTN_W), F32), jax.ShapeDtypeStruct((1, ATTN_W), F32),
                   jax.ShapeDtypeStruct((s, LANES), F32)],
        compiler_params=_params(("arbitrary",)),
    )(dqt, dk, dv, proj, qg, kg)


def _conv_bwd(dob, proj, conv_w):
    s = dob.shape[0]
    tm = min(TM_ELEM, s)
    tiles, befores, afters = _conv_specs(tm, s)

    def body(dob_ref, dnext_ref, gb_ref, gc_ref, u_ref, zb_ref, gch_ref, uh_ref, gbn_ref, zbn_ref, w_ref,
             dgb_ref, dgc_ref, du_ref, dzb_ref, dw_ref):
        i = pl.program_id(1)

        @pl.when(i == 0)
        def _():
            dw_ref[...] = jnp.zeros_like(dw_ref)
        gb, gc, u, zb, cu, r1, r2, conv = _conv_parts(gb_ref, gc_ref, u_ref, zb_ref, gch_ref, uh_ref, i == 0, w_ref, tm)
        g = dob_ref[...]
        sg = _sigmoid(zb)
        sz = zb * sg
        dconv = g * gb * sz
        zn = zbn_ref[0:8, :].astype(F32)
        dcn = jnp.where(i == pl.num_programs(1) - 1, 0.0,
                        dnext_ref[...] * gbn_ref[0:8, :].astype(F32) * (zn * _sigmoid(zn)))
        nxt1, nxt2 = _sub_row(dcn, 0), _sub_row(dcn, 1)
        row = lax.broadcasted_iota(jnp.int32, (tm, LANES), 0)
        f1 = jnp.where(row == tm - 1, nxt1, pltpu.roll(dconv, tm - 1, 0))
        f2 = jnp.where(row == tm - 2, nxt1, jnp.where(row == tm - 1, nxt2, pltpu.roll(dconv, tm - 2, 0)))
        dcu = w_ref[2:3, :] * dconv + w_ref[1:2, :] * f1 + w_ref[0:1, :] * f2
        dgb_ref[...] = (g * conv * sz).astype(BF16)
        dgc_ref[...] = (dcu * u).astype(BF16)
        du_ref[...] = (dcu * gc).astype(BF16)
        dzb_ref[...] = (g * gb * conv * (sg * (1.0 + zb * (1.0 - sg)))).astype(BF16)
        w_row = lax.broadcasted_iota(jnp.int32, (3, LANES), 0)
        dw0 = jnp.sum(dconv * r2, axis=0, keepdims=True)
        dw1 = jnp.sum(dconv * r1, axis=0, keepdims=True)
        dw2 = jnp.sum(dconv * cu, axis=0, keepdims=True)
        dw_ref[...] += jnp.where(w_row == 0, dw0, jnp.where(w_row == 1, dw1, dw2))

    blk = pl.BlockSpec((tm, LANES), lambda c, i: (i, c))
    nxt = pl.BlockSpec((8, LANES), lambda c, i: (jnp.minimum((i + 1) * (tm // 8), s // 8 - 1), c))
    wspec = pl.BlockSpec((3, LANES), lambda c, i: (0, c))
    return pl.pallas_call(
        body, name="conv_bwd", grid=(CONV_W // LANES, s // tm),
        in_specs=[blk, nxt] + tiles + befores + afters + [wspec],
        out_specs=[blk, blk, blk, blk, wspec],
        out_shape=[jax.ShapeDtypeStruct((s, CONV_W), BF16)] * 4 + [jax.ShapeDtypeStruct((3, CONV_W), F32)],
        compiler_params=_params(("parallel", "arbitrary")),
    )(dob, dob, *([proj] * 8), conv_w)


def _piece_layout(pieces):
    offs, off = [], 0
    for p in pieces:
        offs.append((off, p.shape[1]))
        off += p.shape[1]
    assert off == N_ALL, off
    return offs


def _dw_in(h, pieces):
    s = h.shape[0]
    tk, tn = min(TK_DW, s), TN_DW
    nk = s // tk
    main, fpiece = pieces[:-1], pieces[-1]
    layout = _piece_layout(pieces)[:-1]
    n_main = len(main)

    def body(*refs):
        p_refs, f_ref, h_ref = refs[:n_main], refs[n_main], refs[n_main + 1]
        out_ref, outf_ref, acc, accf = refs[n_main + 2:]
        n, k = pl.program_id(0), pl.program_id(1)

        @pl.when(k == 0)
        def _():
            acc[...] = jnp.zeros_like(acc)
        hv = h_ref[...]
        for p_ref, (off, width) in zip(p_refs, layout):
            @pl.when(jnp.logical_and(n >= off // tn, n < (off + width) // tn))
            def _():
                acc[...] += _dot_tn(p_ref[...], hv)

        @pl.when(k == nk - 1)
        def _():
            out_ref[...] = acc[...].astype(BF16)

        @pl.when(n == 0)
        def _():
            @pl.when(k == 0)
            def _():
                accf[...] = jnp.zeros_like(accf)
            accf[...] += _dot_tn(f_ref[...], hv)

            @pl.when(k == nk - 1)
            def _():
                outf_ref[...] = accf[...].astype(BF16)

    def piece_spec(off, width):
        lo, hi = off // tn, (off + width) // tn

        def index(n, k):
            active = jnp.logical_and(n >= lo, n < hi)
            return jnp.where(active, k, 0), jnp.clip(n - lo, 0, hi - lo - 1)
        return pl.BlockSpec((tk, tn), index)

    return pl.pallas_call(
        body, name="dw_in", grid=(N_MAIN // tn, nk),
        in_specs=[piece_spec(off, width) for off, width in layout]
        + [pl.BlockSpec((tk, N_FPAD), lambda n, k: (jnp.where(n == 0, k, 0), 0)),
           pl.BlockSpec((tk, D_MODEL), lambda n, k: (k, 0))],
        out_specs=[pl.BlockSpec((tn, D_MODEL), lambda n, k: (n, 0)),
                   pl.BlockSpec((N_FPAD, D_MODEL), lambda n, k: (0, 0))],
        out_shape=[jax.ShapeDtypeStruct((N_MAIN, D_MODEL), BF16), jax.ShapeDtypeStruct((N_FPAD, D_MODEL), BF16)],
        scratch_shapes=[pltpu.VMEM((tn, D_MODEL), F32), pltpu.VMEM((N_FPAD, D_MODEL), F32)],
        compiler_params=_params(("arbitrary", "arbitrary")),
    )(*main, fpiece, h)


def _dh_and_dx(pieces, w_all_t, x, dy, ada3, norm_g, chip_sums):
    s = x.shape[0]
    tm = min(TM_DH, s)
    nt = s // tm
    n = len(chip_sums)
    npc = len(pieces)
    layout = _piece_layout(pieces)

    def body(*refs):
        p_refs, refs = refs[:npc], refs[npc:]
        wt_ref, x_ref, dy_ref, ada_ref, g_ref = refs[:5]
        ins, refs = refs[5:5 + n], refs[5 + n:]
        gx_ref, dsh_ref, dsc_ref, dg_ref = refs[:4]
        outs, (send_sems, recv_sems, local_sems) = refs[4:4 + n], refs[4 + n:]
        i = pl.program_id(0)

        @pl.when(i == 0)
        def _():
            for cp in _chip_copies(ins, outs, send_sems, recv_sems, local_sems):
                cp.start()
            dsh_ref[...] = jnp.zeros_like(dsh_ref)
            dsc_ref[...] = jnp.zeros_like(dsc_ref)
            dg_ref[...] = jnp.zeros_like(dg_ref)

        dh = None
        for p_ref, (off, width) in zip(p_refs, layout):
            part = _dot(p_ref[...], wt_ref[off:off + width, :])
            dh = part if dh is None else dh + part
        xv = x_ref[...]
        r = lax.rsqrt(jnp.mean(xv * xv, axis=-1, keepdims=True) + EPS)
        xhat = xv * r
        g = g_ref[...]
        one_sc = 1.0 + ada_ref[1:2, :]
        dsh_ref[...] += jnp.sum(dh, axis=0, keepdims=True)
        dsc_ref[...] += jnp.sum(dh * (xhat * g), axis=0, keepdims=True)
        dg_ref[...] += jnp.sum(dh * xhat, axis=0, keepdims=True) * one_sc
        dxh = dh * (g * one_sc)
        dx = r * (dxh - xhat * jnp.mean(dxh * xhat, axis=-1, keepdims=True))
        gx_ref[...] = dy_ref[...] + dx

        @pl.when(i == nt - 1)
        def _():
            for cp in _chip_copies(ins, outs, send_sems, recv_sems, local_sems):
                cp.wait()

    full = pl.BlockSpec((tm, D_MODEL), lambda i: (i, 0))
    vec = pl.BlockSpec((1, D_MODEL), lambda i: (0, 0))
    any_spec = pl.BlockSpec(memory_space=pl.ANY)
    res = pl.pallas_call(
        body, name="dh_dx", grid=(nt,),
        in_specs=[pl.BlockSpec((tm, p.shape[1]), lambda i: (i, 0)) for p in pieces]
        + [pl.BlockSpec((N_ALL, D_MODEL), lambda i: (0, 0)), full, full,
           pl.BlockSpec((3, D_MODEL), lambda i: (0, 0)), vec] + [any_spec] * n,
        out_specs=[full, vec, vec, vec] + [any_spec] * n,
        out_shape=[jax.ShapeDtypeStruct((s, D_MODEL), F32)] + [jax.ShapeDtypeStruct((1, D_MODEL), F32)] * 3
        + [jax.ShapeDtypeStruct(a.shape, a.dtype) for a in chip_sums],
        scratch_shapes=[pltpu.SemaphoreType.DMA((n * 3,)), pltpu.SemaphoreType.DMA((n * 3,)),
                        pltpu.SemaphoreType.DMA((n,))],
        compiler_params=_params(("arbitrary",)),
    )(*pieces, w_all_t, x, dy, ada3, norm_g, *chip_sums)
    return res[:4], res[4:]


def _sum_small(vec_all, qg_parts, kg_parts):
    def body(v_ref, q_ref, k_ref, tot_ref, gq_ref, gk_ref):
        tot = v_ref[0:1, :]
        for p in range(1, N_DEV):
            tot = tot + v_ref[p:p + 1, :]
        tot_ref[...] = tot
        gq_ref[...] = jnp.sum(q_ref[...], axis=0, keepdims=True)
        gk_ref[...] = jnp.sum(k_ref[...], axis=0, keepdims=True)

    n = vec_all.shape[-1]
    return pl.pallas_call(
        body, name="sum_small",
        out_shape=[jax.ShapeDtypeStruct((1, n), F32),
                   jax.ShapeDtypeStruct((1, HEAD_DIM), F32), jax.ShapeDtypeStruct((1, HEAD_DIM), F32)],
        compiler_params=_params(),
    )(vec_all, qg_parts, kg_parts)


def _grad_w_ada(c_cols, dada_rows):
    def body(c_ref, d_ref, out_ref):
        acc = c_ref[0] * d_ref[0]
        for b in range(1, N_DEV):
            acc = acc + c_ref[b] * d_ref[b]
        out_ref[...] = acc

    return pl.pallas_call(
        body, name="grad_w_ada",
        out_shape=jax.ShapeDtypeStruct((D_MODEL, ADA_SHARD), F32),
        compiler_params=_params(),
    )(c_cols, dada_rows)


def _adamw(w, m, v, g_parts, name):
    rows, cols = w.shape
    n_parts = g_parts.shape[0]
    tr = 256 if rows % 256 == 0 else rows
    tc = 256 if (tr == rows and rows > 256 and cols % 256 == 0) else cols
    c1 = 1.0 / (1.0 - ADAM_B1 ** ADAM_STEP)
    c2 = 1.0 / (1.0 - ADAM_B2 ** ADAM_STEP)

    def body(w_ref, m_ref, v_ref, g_ref, go_ref, d_ref, mo_ref, vo_ref):
        g = g_ref[0].astype(F32)
        for p in range(1, n_parts):
            g = g + g_ref[p].astype(F32)
        m_new = ADAM_B1 * m_ref[...] + (1.0 - ADAM_B1) * g
        v_new = ADAM_B2 * v_ref[...] + (1.0 - ADAM_B2) * (g * g)
        go_ref[...] = g
        mo_ref[...] = m_new
        vo_ref[...] = v_new
        d_ref[...] = -ADAM_LR * ((m_new * c1) / (jnp.sqrt(v_new * c2) + ADAM_EPS) + ADAM_WD * w_ref[...])

    blk = pl.BlockSpec((tr, tc), lambda i, j: (i, j))
    return pl.pallas_call(
        body, name=name, grid=(rows // tr, cols // tc),
        in_specs=[blk, blk, blk, pl.BlockSpec((n_parts, tr, tc), lambda i, j: (0, i, j))],
        out_specs=[blk] * 4,
        out_shape=[jax.ShapeDtypeStruct((rows, cols), F32)] * 4,
        compiler_params=_params(("parallel", "parallel")),
    )(w, m, v, g_parts)


_O_F = 1536


def _to_internal(wt_g):
    wf = wt_g.reshape(IN_WIDTH, D_MODEL)
    f = jnp.pad(wf[_O_F:_O_F + HEADS], ((0, N_FPAD - HEADS), (0, 0)))
    return jnp.concatenate([wf[:_O_F], wf[_O_F + HEADS:], f], axis=0)


def _from_internal(dwt, dwt_f):
    full = jnp.concatenate([dwt[:_O_F], dwt_f[:HEADS], dwt[_O_F:]], axis=0)
    return full.reshape(N_DEV, IN_SHARD, D_MODEL)


def kernel(x, c, w_ada, b_ada, norm_g, w_in, b_f, q_norm_g, k_norm_g, conv_w, w_attn_out, w_conv_out, w_o, loss_target, m_w_ada, m_b_ada, m_norm_g, m_w_in, m_b_f, m_q_norm_g, m_k_norm_g, m_conv_w, m_w_attn_out, m_w_conv_out, m_w_o, v_w_ada, v_b_ada, v_norm_g, v_w_in, v_b_f, v_q_norm_g, v_k_norm_g, v_conv_w, v_w_attn_out, v_w_conv_out, v_w_o):
    me = 4 * lax.axis_index("x") + 2 * lax.axis_index("y") + lax.axis_index("c")
    s = x.shape[1]
    x2, t2 = x[0], loss_target[0]

    cw_g, wa_g, wb_g, wo_g, w_in_g = _gather_two_level(
        [conv_w[0], w_attn_out[0].astype(BF16), w_conv_out[0].astype(BF16), w_o[0].astype(BF16),
         w_in[0].T.astype(BF16)], "gather_weights")
    c_all, ada_g = _ada_exchange(c, w_ada[0])
    ada_mine = lax.dynamic_index_in_dim(ada_g[:, :, 0, :], me, axis=1, keepdims=False)
    ada3 = (ada_mine.reshape(1, 3 * D_MODEL) + b_ada).reshape(3, D_MODEL)
    w_all_t = _to_internal(w_in_g)
    wa = jnp.transpose(wa_g, (1, 0, 2)).reshape(ATTN_W, D_MODEL)
    wb = jnp.transpose(wb_g, (1, 0, 2)).reshape(CONV_W, D_MODEL)
    wo = wo_g.reshape(D_MODEL, D_MODEL)
    cw = jnp.transpose(cw_g, (1, 0, 2)).reshape(3, CONV_W)
    qg = jnp.tile(q_norm_g, (1, HEADS))
    kg = jnp.tile(k_norm_g, (1, HEADS))
    bf_pad = jnp.pad(b_f, ((0, 0), (0, LANES - HEADS)))

    proj, fl, h = _proj_fwd(x2, ada3, norm_g, w_all_t)
    cum = _forget_cumsum(fl, bf_pad)
    qa, ka, va, kt, vt = _qkv_prep(proj, cum, qg, kg)
    attn, oa, qb = _attn_fwd(qa, ka, vt, proj)
    ob = _conv_fwd(proj, cw)
    (dy, dgab, doa, dob, dwo, dwa, dwb, dgate, loss_part) = _tail(oa, ob, proj, x2, t2, ada3, wa, wb, wo)

    do, dza = _attn_bwd_prep(doa, attn, proj)
    dqt, dk, dv = _attn_bwd(qb, ka, kt, va, do)
    dqkv, dqg, dkg, dcum = _qk_norm_bwd(dqt, dk, dv, proj, qg, kg)
    df, dbf = _forget_bwd(dcum, fl, bf_pad)
    dcb, dcc, dcu, dcz, dcw = _conv_bwd(dob, proj, cw)
    pieces = [dqkv, dza, dcb, dcc, dcu, dcz, dgab, df]
    dw_main, dw_f = _dw_in(h, pieces)

    def by_core(slabs8):
        return jnp.swapaxes(slabs8.reshape((4, 2) + slabs8.shape[1:]), 0, 1).astype(BF16)

    slabs = [by_core(jnp.transpose(dwa.reshape(ATTN_W, N_DEV, LANES), (1, 0, 2))),
             by_core(jnp.transpose(dwb.reshape(CONV_W, N_DEV, LANES), (1, 0, 2))),
             by_core(dwo.reshape(N_DEV, D_MODEL // N_DEV, D_MODEL)),
             by_core(_from_internal(dw_main, dw_f))]
    theirs = _sibling_swap(slabs, "swap_grads")
    core = lax.axis_index("c").astype(jnp.int32).reshape(1)
    chip_sums = [_pair_sum(m2, t4, core, "pair_sum_" + nm)
                 for m2, t4, nm in zip(slabs, theirs, ("wa", "wb", "wo", "w_in"))]
    (grad_x, dshift, dscale, dnormg), (g_wa_parts, g_wb_parts, g_wo_parts, g_in_parts) = _dh_and_dx(
        pieces, w_all_t, x2, dy, ada3, norm_g, chip_sums)
    vec = jnp.concatenate([dshift, dscale, dgate, dnormg, dbf, dcw.reshape(1, 3 * CONV_W), dqg, dkg], axis=1)
    (vec_all,) = _exchange([vec], True, "gather_small")
    vec_all = vec_all.reshape(N_DEV, vec.shape[1])
    n_main = 4 * D_MODEL + LANES + 3 * CONV_W
    tot, g_qg, g_kg = _sum_small(
        vec_all[:, :n_main],
        vec_all[:, n_main:n_main + ATTN_W].reshape(N_DEV * HEADS, HEAD_DIM),
        vec_all[:, n_main + ATTN_W:].reshape(N_DEV * HEADS, HEAD_DIM))
    g_b_ada = tot[:, 0:3 * D_MODEL]
    g_norm_g = tot[:, 3 * D_MODEL:4 * D_MODEL]
    g_b_f = tot[:, 4 * D_MODEL:4 * D_MODEL + HEADS]
    g_cw_full = tot[:, 4 * D_MODEL + LANES:].reshape(3, CONV_W)
    g_cw = lax.dynamic_slice(g_cw_full, (0, me * (CONV_W // N_DEV)), (3, CONV_W // N_DEV))
    dada_mine = lax.dynamic_slice(vec_all[:, 0:3 * D_MODEL], (0, me * ADA_SHARD), (N_DEV, ADA_SHARD))
    g_w_ada = _grad_w_ada(jnp.transpose(c_all, (0, 2, 1)), dada_mine.reshape(N_DEV, 1, ADA_SHARD))

    upd = {}
    upd["w_ada"] = _adamw(w_ada[0], m_w_ada[0], v_w_ada[0], g_w_ada[None], "adamw_w_ada")
    upd["b_ada"] = _adamw(b_ada, m_b_ada, v_b_ada, g_b_ada[None], "adamw_b_ada")
    upd["norm_g"] = _adamw(norm_g, m_norm_g, v_norm_g, g_norm_g[None], "adamw_norm_g")
    upd["w_in"] = [u.T for u in _adamw(w_in[0].T, m_w_in[0].T, v_w_in[0].T, g_in_parts, "adamw_w_in")]
    upd["b_f"] = _adamw(b_f, m_b_f, v_b_f, g_b_f[None], "adamw_b_f")
    upd["q_norm_g"] = _adamw(q_norm_g, m_q_norm_g, v_q_norm_g, g_qg[None], "adamw_q_norm_g")
    upd["k_norm_g"] = _adamw(k_norm_g, m_k_norm_g, v_k_norm_g, g_kg[None], "adamw_k_norm_g")
    upd["conv_w"] = _adamw(conv_w[0], m_conv_w[0], v_conv_w[0], g_cw[None], "adamw_conv_w")
    upd["w_attn_out"] = _adamw(w_attn_out[0], m_w_attn_out[0], v_w_attn_out[0], g_wa_parts, "adamw_w_attn_out")
    upd["w_conv_out"] = _adamw(w_conv_out[0], m_w_conv_out[0], v_w_conv_out[0], g_wb_parts, "adamw_w_conv_out")
    upd["w_o"] = _adamw(w_o[0], m_w_o[0], v_w_o[0], g_wo_parts, "adamw_w_o")

    names = ["w_ada", "b_ada", "norm_g", "w_in", "b_f", "q_norm_g", "k_norm_g", "conv_w",
             "w_attn_out", "w_conv_out", "w_o"]
    lead = {"w_ada", "w_in", "conv_w", "w_attn_out", "w_conv_out", "w_o"}
    fix = lambda n, a: a[None] if n in lead else a
    loss = lax.psum(loss_part[0, 0], ("x", "y", "c"))
    outs = [loss, grad_x[None]]
    for k in range(4):
        outs += [fix(n, upd[n][k]) for n in names]
    return tuple(outs)
```

```python
import functools

import numpy as np
import jax
import jax.numpy as jnp
from jax import lax
from jax.experimental import pallas as pl
from jax.experimental.pallas import tpu as pltpu

F32 = jnp.float32
BF16 = jnp.bfloat16

D_MODEL = 1024
HEADS = 8
HEAD_DIM = 64
ATTN_W = 512
CONV_W = 512
N_DEV = 8
IN_WIDTH = 6152
IN_SHARD = IN_WIDTH // N_DEV
N_MAIN = 6144
N_FPAD = 128
N_ALL = N_MAIN + N_FPAD
ADA_SHARD = 3 * D_MODEL // N_DEV
EPS = 1e-6
NEG = -1e30

ADAM_LR = 0.001
ADAM_B1 = 0.9
ADAM_B2 = 0.999
ADAM_EPS = 1e-08
ADAM_WD = 0.01
ADAM_STEP = 10

LANES = 128
VMEM_LIMIT = 56 * 1024 * 1024

TM_PROJ = 256
TN_PROJ = 1024
TM_ELEM = 512
TQ = 512
HEADS_PER_STEP = 4
HEADS_PER_STEP_FWD = 8
TM_TAIL = 256
TC_CUM = 256
TK_DW = 1024
TN_DW = 512
TM_DH = 256
HALO = 16

OFF_Q, OFF_K, OFF_V, OFF_ZA, OFF_CB, OFF_CC, OFF_CU, OFF_CZ, OFF_GA, OFF_GB = (
    0, 512, 1024, 1536, 2048, 2560, 3072, 3584, 4096, 5120)


def _params(sem=None):
    return pltpu.CompilerParams(dimension_semantics=sem, vmem_limit_bytes=VMEM_LIMIT)


def _dot(a, b):
    return jnp.dot(a, b, preferred_element_type=F32)


def _dot_nt(a, b):
    return lax.dot_general(a, b, (((1,), (1,)), ((), ())), preferred_element_type=F32)


def _dot_tn(a, b):
    return lax.dot_general(a, b, (((0,), (0,)), ((), ())), preferred_element_type=F32)


def _sigmoid(x):
    return 1.0 / (1.0 + jnp.exp(-x))


def _lane_lo(shape):
    return lax.broadcasted_iota(jnp.int32, shape, len(shape) - 1) < HEAD_DIM


def _seg_sum(z, lo):
    a = jnp.sum(jnp.where(lo, z, 0.0), axis=-1, keepdims=True)
    b = jnp.sum(jnp.where(lo, 0.0, z), axis=-1, keepdims=True)
    return jnp.where(lo, a, b)


def _lane_col(z, lane):
    idx = lax.broadcasted_iota(jnp.int32, z.shape, 1)
    return jnp.sum(jnp.where(idx == lane, z, 0.0), axis=-1, keepdims=True)


def _sub_row(z, row):
    idx = lax.broadcasted_iota(jnp.int32, z.shape, 0)
    return jnp.sum(jnp.where(idx == row, z, 0.0), axis=0, keepdims=True)


def _mesh_pos():
    x, y, c = lax.axis_index("x"), lax.axis_index("y"), lax.axis_index("c")
    return x, y, c, 4 * x + 2 * y + c


def _peer(k, x, y, c):
    px = 1 - x if (k >> 2) & 1 else x
    py = 1 - y if (k >> 1) & 1 else y
    pc = 1 - c if k & 1 else c
    return (px, py, pc), 4 * px + 2 * py + pc


def _exchange(arrs, gather, name):
    n = len(arrs)
    any_spec = pl.BlockSpec(memory_space=pl.ANY)

    def body(*refs):
        ins, outs = refs[:n], refs[n:2 * n]
        send_sems, recv_sems, local_sems = refs[2 * n:]
        x, y, c, me = _mesh_pos()
        copies = []
        for a in range(n):
            own = ins[a] if gather else ins[a].at[me]
            local = pltpu.make_async_copy(own, outs[a].at[me], local_sems.at[a])
            local.start()
            copies.append(local)
            for k in range(1, N_DEV):
                dev, p = _peer(k, x, y, c)
                cp = pltpu.make_async_remote_copy(
                    src_ref=ins[a] if gather else ins[a].at[p],
                    dst_ref=outs[a].at[me],
                    send_sem=send_sems.at[a * (N_DEV - 1) + k - 1],
                    recv_sem=recv_sems.at[a * (N_DEV - 1) + k - 1],
                    device_id=dev, device_id_type=pl.DeviceIdType.MESH)
                cp.start()
                copies.append(cp)
        for cp in copies:
            cp.wait()

    out_shape = [jax.ShapeDtypeStruct((N_DEV,) + a.shape if gather else a.shape, a.dtype) for a in arrs]
    return pl.pallas_call(
        body, name=name, out_shape=out_shape,
        in_specs=[any_spec] * n, out_specs=[any_spec] * n,
        scratch_shapes=[pltpu.SemaphoreType.DMA((n * (N_DEV - 1),)),
                        pltpu.SemaphoreType.DMA((n * (N_DEV - 1),)),
                        pltpu.SemaphoreType.DMA((n,))],
    )(*arrs)


def _gather_two_level(arrs, name):
    n = len(arrs)
    any_spec = pl.BlockSpec(memory_space=pl.ANY)
    per = N_DEV - 1

    def body(*refs):
        ins, outs = refs[:n], refs[n:2 * n]
        send_sems, recv_sems, local_sems = refs[2 * n:]
        x, y, c, me = _mesh_pos()
        sibling = (x, y, 1 - c)
        chips = [(1 - x, y), (x, 1 - y), (1 - x, 1 - y)]

        def copy(a, k, src, blk, to):
            return pltpu.make_async_remote_copy(
                src_ref=src, dst_ref=outs[a].at[blk],
                send_sem=send_sems.at[a * per + k], recv_sem=recv_sems.at[a * per + k],
                device_id=to, device_id_type=pl.DeviceIdType.MESH)

        local = [pltpu.make_async_copy(ins[a], outs[a].at[me], local_sems.at[a]) for a in range(n)]
        for cp in local:
            cp.start()
        first = []
        for a in range(n):
            first.append(copy(a, 0, ins[a], me, sibling))
            first += [copy(a, 1 + j, ins[a], me, (px, py, c)) for j, (px, py) in enumerate(chips)]
        for cp in first:
            cp.start()
        passed = []
        for j, (px, py) in enumerate(chips):
            blk = 4 * px + 2 * py + c
            for a in range(n):
                copy(a, 1 + j, ins[a], blk, (x, y, c)).wait_recv()
                fwd = copy(a, 4 + j, outs[a].at[blk], blk, sibling)
                fwd.start()
                passed.append(fwd)
        for a in range(n):
            copy(a, 0, ins[a], 4 * x + 2 * y + 1 - c, (x, y, c)).wait_recv()
            for j, (px, py) in enumerate(chips):
                copy(a, 4 + j, ins[a], 4 * px + 2 * py + 1 - c, (x, y, c)).wait_recv()
        for cp in first + passed:
            cp.wait_send()
        for cp in local:
            cp.wait()

    return pl.pallas_call(
        body, name=name,
        out_shape=[jax.ShapeDtypeStruct((N_DEV,) + a.shape, a.dtype) for a in arrs],
        in_specs=[any_spec] * n, out_specs=[any_spec] * n,
        scratch_shapes=[pltpu.SemaphoreType.DMA((n * per,)), pltpu.SemaphoreType.DMA((n * per,)),
                        pltpu.SemaphoreType.DMA((n,))],
    )(*arrs)


def _sibling_swap(arrs, name):
    n = len(arrs)
    any_spec = pl.BlockSpec(memory_space=pl.ANY)

    def body(*refs):
        ins, outs = refs[:n], refs[n:2 * n]
        send_sems, recv_sems = refs[2 * n:]
        x, y, c, _ = _mesh_pos()
        copies = [pltpu.make_async_remote_copy(
            src_ref=ins[a].at[1 - c], dst_ref=outs[a], send_sem=send_sems.at[a], recv_sem=recv_sems.at[a],
            device_id=(x, y, 1 - c), device_id_type=pl.DeviceIdType.MESH) for a in range(n)]
        for cp in copies:
            cp.start()
        for cp in copies:
            cp.wait()

    return pl.pallas_call(
        body, name=name,
        out_shape=[jax.ShapeDtypeStruct(a.shape[1:], a.dtype) for a in arrs],
        in_specs=[any_spec] * n, out_specs=[any_spec] * n,
        scratch_shapes=[pltpu.SemaphoreType.DMA((n,)), pltpu.SemaphoreType.DMA((n,))],
    )(*arrs)


def _pair_sum(mine2, theirs, core, name):
    _, _, rows, cols = mine2.shape
    tr = 256 if rows % 256 == 0 else rows

    def body(core_ref, a_ref, b_ref, out_ref):
        out_ref[...] = (a_ref[...].astype(F32) + b_ref[...].astype(F32)).astype(BF16)

    return pl.pallas_call(
        body, name=name,
        grid_spec=pltpu.PrefetchScalarGridSpec(
            num_scalar_prefetch=1, grid=(4, rows // tr),
            in_specs=[pl.BlockSpec((None, None, tr, cols), lambda ch, i, core_: (core_[0], ch, i, 0)),
                      pl.BlockSpec((None, tr, cols), lambda ch, i, core_: (ch, i, 0))],
            out_specs=pl.BlockSpec((None, tr, cols), lambda ch, i, core_: (ch, i, 0))),
        out_shape=jax.ShapeDtypeStruct(theirs.shape, BF16),
        compiler_params=_params(("parallel", "parallel")),
    )(core, mine2, theirs)


def _chip_copies(ins, outs, send_sems, recv_sems, local_sems):
    x, y, c, _ = _mesh_pos()
    my_chip = 2 * x + y
    chips = [(1 - x, y), (x, 1 - y), (1 - x, 1 - y)]
    copies = []
    for a in range(len(ins)):
        copies.append(pltpu.make_async_copy(ins[a].at[my_chip], outs[a].at[my_chip], local_sems.at[a]))
        for j, (px, py) in enumerate(chips):
            copies.append(pltpu.make_async_remote_copy(
                src_ref=ins[a].at[2 * px + py], dst_ref=outs[a].at[my_chip],
                send_sem=send_sems.at[a * 3 + j], recv_sem=recv_sems.at[a * 3 + j],
                device_id=(px, py, c), device_id_type=pl.DeviceIdType.MESH))
    return copies


def _ada_exchange(c_row, w_ada_sh):
    def body(c_ref, w_ref, call_ref, adag_ref, mine_ref, send_sems, recv_sems):
        x, y, c, me = _mesh_pos()

        def copy(phase, k, src, dst):
            dev, _ = _peer(k, x, y, c)
            return pltpu.make_async_remote_copy(
                src_ref=src, dst_ref=dst,
                send_sem=send_sems.at[phase * (N_DEV - 1) + k - 1],
                recv_sem=recv_sems.at[phase * (N_DEV - 1) + k - 1],
                device_id=dev, device_id_type=pl.DeviceIdType.MESH)

        call_ref[me] = c_ref[...]
        first = [copy(0, k, c_ref, call_ref.at[me]) for k in range(1, N_DEV)]
        for cp in first:
            cp.start()
        for cp in first:
            cp.wait()
        wb = w_ref[...].astype(BF16)
        for b in range(N_DEV):
            row = jnp.broadcast_to(call_ref[b], (8, D_MODEL)).astype(BF16)
            mine_ref[b] = _sub_row(_dot(row, wb), 0)
        adag_ref[me] = mine_ref[...]
        second = [copy(1, k, mine_ref, adag_ref.at[me]) for k in range(1, N_DEV)]
        for cp in second:
            cp.start()
        for cp in second:
            cp.wait()

    vm = pl.BlockSpec(memory_space=pltpu.VMEM)
    return pl.pallas_call(
        body, name="ada_exchange",
        out_shape=[jax.ShapeDtypeStruct((N_DEV, 1, D_MODEL), F32),
                   jax.ShapeDtypeStruct((N_DEV, N_DEV, 1, ADA_SHARD), F32)],
        in_specs=[vm, vm], out_specs=[vm, vm],
        scratch_shapes=[pltpu.VMEM((N_DEV, 1, ADA_SHARD), F32),
                        pltpu.SemaphoreType.DMA((2 * (N_DEV - 1),)),
                        pltpu.SemaphoreType.DMA((2 * (N_DEV - 1),))],
        compiler_params=pltpu.CompilerParams(vmem_limit_bytes=VMEM_LIMIT),
    )(c_row, w_ada_sh)


def _proj_fwd(x, ada3, norm_g, w_all_t):
    s = x.shape[0]
    tm, tn = min(TM_PROJ, s), TN_PROJ

    def body(x_ref, ada_ref, g_ref, wt_ref, proj_ref, fl_ref, h_ref):
        xv = x_ref[...]
        r = lax.rsqrt(jnp.mean(xv * xv, axis=-1, keepdims=True) + EPS)
        hv = ((xv * r) * g_ref[...]) * (1.0 + ada_ref[1:2, :]) + ada_ref[0:1, :]
        hb = hv.astype(BF16)
        h_ref[...] = hb
        fl_ref[...] = _dot_nt(hb, wt_ref[N_MAIN:N_ALL, :])
        for j in range(N_MAIN // tn):
            proj_ref[:, j * tn:(j + 1) * tn] = _dot_nt(hb, wt_ref[j * tn:(j + 1) * tn, :]).astype(BF16)

    return pl.pallas_call(
        body, name="proj_fwd", grid=(s // tm,),
        in_specs=[pl.BlockSpec((tm, D_MODEL), lambda i: (i, 0)),
                  pl.BlockSpec((3, D_MODEL), lambda i: (0, 0)),
                  pl.BlockSpec((1, D_MODEL), lambda i: (0, 0)),
                  pl.BlockSpec((N_ALL, D_MODEL), lambda i: (0, 0))],
        out_specs=[pl.BlockSpec((tm, N_MAIN), lambda i: (i, 0)),
                   pl.BlockSpec((tm, N_FPAD), lambda i: (i, 0)),
                   pl.BlockSpec((tm, D_MODEL), lambda i: (i, 0))],
        out_shape=[jax.ShapeDtypeStruct((s, N_MAIN), BF16),
                   jax.ShapeDtypeStruct((s, N_FPAD), F32),
                   jax.ShapeDtypeStruct((s, D_MODEL), BF16)],
        compiler_params=_params(("parallel",)),
    )(x, ada3, norm_g, w_all_t)


L_ONE_Q, L_F_Q, L_LSE_Q, L_END = HEAD_DIM, HEAD_DIM + 3, HEAD_DIM + 6, HEAD_DIM + 9


def _split3(f):
    hi = f.astype(BF16).astype(F32)
    r = f - hi
    mid = r.astype(BF16).astype(F32)
    return hi, mid, r - mid


def _place3(lane, first, parts, otherwise):
    a, b, c = parts
    return jnp.where(lane == first, a, jnp.where(lane == first + 1, b, jnp.where(lane == first + 2, c, otherwise)))


def _log_forget(fl, bf):
    z = fl + bf
    lf = jnp.minimum(z, 0.0) - jnp.log1p(jnp.exp(-jnp.abs(z)))
    lane = lax.broadcasted_iota(jnp.int32, z.shape, 1)
    return jnp.where(lane < HEADS, lf, 0.0)


def _qkv_prep(proj, fl, bf_pad, qg, kg):
    s = proj.shape[0]
    tm = min(TM_ELEM, s)
    scale = HEAD_DIM ** -0.5

    def body(p_ref, fl_ref, bf_ref, qg_ref, kg_ref, qa_ref, ka_ref, va_ref, kt_ref, vt_ref, carry):
        @pl.when(pl.program_id(0) == 0)
        def _():
            carry[...] = jnp.zeros_like(carry)
        tri = (lax.broadcasted_iota(jnp.int32, (tm, tm), 1) <= lax.broadcasted_iota(jnp.int32, (tm, tm), 0)).astype(F32)
        cum_v = jnp.dot(tri, _log_forget(fl_ref[...], bf_ref[...]), preferred_element_type=F32,
                        precision=lax.Precision.HIGHEST) + carry[...]
        carry[...] = _sub_row(cum_v, tm - 1)
        lane = lax.broadcasted_iota(jnp.int32, (tm, LANES), 1)
        lo = lane < HEAD_DIM
        v_tail = jnp.where(lane < L_F_Q, 1.0, 0.0)
        for pr in range(ATTN_W // LANES):
            sl = slice(pr * LANES, (pr + 1) * LANES)
            q2 = p_ref[:, OFF_Q + pr * LANES:OFF_Q + (pr + 1) * LANES].astype(F32)
            k2 = p_ref[:, OFF_K + pr * LANES:OFF_K + (pr + 1) * LANES].astype(F32)
            v2 = p_ref[:, OFF_V + pr * LANES:OFF_V + (pr + 1) * LANES].astype(F32)
            rq = lax.rsqrt(_seg_sum(q2 * q2, lo) * (1.0 / HEAD_DIM) + EPS)
            rk = lax.rsqrt(_seg_sum(k2 * k2, lo) * (1.0 / HEAD_DIM) + EPS)
            qn = ((q2 * rq) * qg_ref[:, sl]) * scale
            kn = (k2 * rk) * kg_ref[:, sl]
            for hh in range(2):
                h = 2 * pr + hh
                f3 = _split3(_lane_col(cum_v, h))
                qh = qn if hh == 0 else pltpu.roll(qn, HEAD_DIM, 1)
                kh = kn if hh == 0 else pltpu.roll(kn, HEAD_DIM, 1)
                vh = v2 if hh == 0 else pltpu.roll(v2, HEAD_DIM, 1)
                q_tail = jnp.where(lane < L_F_Q, 1.0, _place3(lane, L_F_Q, f3, 0.0))
                k_tail = _place3(lane, L_ONE_Q, tuple(-f for f in f3), jnp.where(lane < L_END, 1.0, 0.0))
                k_row = jnp.where(lo, kh, k_tail)
                v_row = jnp.where(lo, vh, v_tail)
                qa_ref[h] = jnp.where(lo, qh, q_tail).astype(BF16)
                ka_ref[h] = k_row.astype(BF16)
                va_ref[h] = v_row.astype(BF16)
                kt_ref[h] = k_row.T.astype(BF16)
                vt_ref[h] = v_row.T.astype(BF16)

    heads = pl.BlockSpec((HEADS, tm, LANES), lambda i: (0, i, 0))
    heads_t = pl.BlockSpec((HEADS, LANES, tm), lambda i: (0, 0, i))
    vec = pl.BlockSpec((1, ATTN_W), lambda i: (0, 0))
    return pl.pallas_call(
        body, name="qkv_prep", grid=(s // tm,),
        in_specs=[pl.BlockSpec((tm, 3 * ATTN_W), lambda i: (i, 0)),
                  pl.BlockSpec((tm, LANES), lambda i: (i, 0)),
                  pl.BlockSpec((1, LANES), lambda i: (0, 0)), vec, vec],
        out_specs=[heads, heads, heads, heads_t, heads_t],
        out_shape=[jax.ShapeDtypeStruct((HEADS, s, LANES), BF16)] * 3
        + [jax.ShapeDtypeStruct((HEADS, LANES, s), BF16)] * 2,
        scratch_shapes=[pltpu.VMEM((1, LANES), F32)],
        compiler_params=_params(("arbitrary",)),
    )(proj, fl, bf_pad, qg, kg)


def _causal_t(t):
    return lax.broadcasted_iota(jnp.int32, (t, t), 0) <= lax.broadcasted_iota(jnp.int32, (t, t), 1)


def _tri_steps(nt, q_major):
    if q_major:
        pairs = [(i, j) for i in range(nt) for j in range(i + 1)]
    else:
        pairs = [(i, j) for j in range(nt) for i in range(j, nt)]
    return (jnp.asarray(np.array([p[0] for p in pairs], np.int32)),
            jnp.asarray(np.array([p[1] for p in pairs], np.int32)))


def _attn_fwd(qa, ka, vt, proj):
    s = qa.shape[1]
    t = min(TQ, s)
    it, jt = _tri_steps(s // t, True)
    hp = HEADS_PER_STEP_FWD
    wide = hp * HEAD_DIM
    za_blk = OFF_ZA // wide

    def body(it_ref, jt_ref, q_ref, k_ref, vt_ref, za_ref, attn_ref, oa_ref, qb_ref, m_s, acc_s, pair_s):
        step = pl.program_id(1)
        i, j = it_ref[step], jt_ref[step]

        @pl.when(j == 0)
        def _():
            m_s[...] = jnp.full_like(m_s, NEG)
            acc_s[...] = jnp.zeros_like(acc_s)

        def update(masked):
            for hh in range(hp):
                st = _dot_nt(k_ref[hh], q_ref[hh])
                if masked:
                    st = jnp.where(_causal_t(t), st, NEG)
                m_prev = m_s[hh]
                m_next = jnp.maximum(m_prev, jnp.max(st, axis=0, keepdims=True))
                alpha = jnp.exp(m_prev - m_next)
                pt = jnp.exp(st - m_next).astype(BF16)
                acc_s[hh] = acc_s[hh] * alpha + _dot(vt_ref[hh], pt)
                m_s[hh] = m_next

        @pl.when(j < i)
        def _():
            update(False)

        @pl.when(j == i)
        def _():
            update(True)
            row = lax.broadcasted_iota(jnp.int32, (LANES, t), 0)
            lane = lax.broadcasted_iota(jnp.int32, (t, LANES), 1)
            for hh in range(hp):
                l_row = acc_s[hh, L_ONE_Q:L_ONE_Q + 1, :]
                pair_s[hh * HEAD_DIM:(hh + 1) * HEAD_DIM, :] = acc_s[hh, 0:HEAD_DIM, :] / l_row
                lse3 = _split3(m_s[hh] + jnp.log(l_row))
                tail_t = _place3(row, L_LSE_Q, tuple(-x for x in lse3), 0.0)
                keep_q = jnp.logical_or(lane < L_LSE_Q, lane >= L_END)
                qb_ref[hh] = jnp.where(keep_q, q_ref[hh].astype(F32), tail_t.T).astype(BF16)
            out = pair_s[...].T
            attn_ref[...] = out
            z = za_ref[...].astype(F32)
            oa_ref[...] = (out * (z * _sigmoid(z))).astype(BF16)

    pair_q = pl.BlockSpec((hp, t, LANES), lambda p, n, it_, jt_: (p, it_[n], 0))
    pair_k = pl.BlockSpec((hp, t, LANES), lambda p, n, it_, jt_: (p, jt_[n], 0))
    pair_kt = pl.BlockSpec((hp, LANES, t), lambda p, n, it_, jt_: (p, 0, jt_[n]))
    out_q = pl.BlockSpec((t, wide), lambda p, n, it_, jt_: (it_[n], p))
    return pl.pallas_call(
        body, name="attn_fwd",
        grid_spec=pltpu.PrefetchScalarGridSpec(
            num_scalar_prefetch=2, grid=(HEADS // hp, it.shape[0]),
            in_specs=[pair_q, pair_k, pair_kt,
                      pl.BlockSpec((t, wide), lambda p, n, it_, jt_: (it_[n], za_blk + p))],
            out_specs=[out_q, out_q, pair_q],
            scratch_shapes=[pltpu.VMEM((hp, 1, t), F32), pltpu.VMEM((hp, LANES, t), F32),
                            pltpu.VMEM((wide, t), F32)]),
        out_shape=[jax.ShapeDtypeStruct((s, ATTN_W), F32),
                   jax.ShapeDtypeStruct((s, ATTN_W), BF16),
                   jax.ShapeDtypeStruct((HEADS, s, LANES), BF16)],
        compiler_params=_params(("parallel", "arbitrary")),
    )(it, jt, qa, ka, vt, proj)


def _conv_parts(gb_ref, gc_ref, u_ref, zb_ref, gch_ref, uh_ref, first, w_ref, tm):
    gb, gc = gb_ref[...].astype(F32), gc_ref[...].astype(F32)
    u, zb = u_ref[...].astype(F32), zb_ref[...].astype(F32)
    cu = gc * u
    cu_h = jnp.where(first, 0.0, gch_ref[...].astype(F32) * uh_ref[...].astype(F32))
    prev1, prev2 = _sub_row(cu_h, HALO - 1), _sub_row(cu_h, HALO - 2)
    row = lax.broadcasted_iota(jnp.int32, (tm, LANES), 0)
    r1 = jnp.where(row == 0, prev1, pltpu.roll(cu, 1, 0))
    r2 = jnp.where(row == 0, prev2, jnp.where(row == 1, prev1, pltpu.roll(cu, 2, 0)))
    conv = w_ref[2:3, :] * cu + w_ref[1:2, :] * r1 + w_ref[0:1, :] * r2
    return gb, gc, u, zb, cu, r1, r2, conv


def _conv_specs(tm, s):
    def tile(off):
        return pl.BlockSpec((tm, LANES), lambda c, i: (i, off // LANES + c))

    def before(off):
        return pl.BlockSpec((HALO, LANES), lambda c, i: (jnp.maximum(i * (tm // HALO) - 1, 0), off // LANES + c))

    def after(off):
        return pl.BlockSpec((HALO, LANES),
                            lambda c, i: (jnp.minimum((i + 1) * (tm // HALO), s // HALO - 1), off // LANES + c))

    return ([tile(OFF_CB), tile(OFF_CC), tile(OFF_CU), tile(OFF_CZ)], [before(OFF_CC), before(OFF_CU)],
            [after(OFF_CB), after(OFF_CZ)])


def _conv_fwd(proj, conv_w):
    s = proj.shape[0]
    tm = min(TM_ELEM, s)
    tiles, befores, _ = _conv_specs(tm, s)

    def body(gb_ref, gc_ref, u_ref, zb_ref, gch_ref, uh_ref, w_ref, ob_ref):
        first = pl.program_id(1) == 0
        gb, _, _, zb, _, _, _, conv = _conv_parts(gb_ref, gc_ref, u_ref, zb_ref, gch_ref, uh_ref, first, w_ref, tm)
        ob_ref[...] = (gb * conv * (zb * _sigmoid(zb))).astype(BF16)

    return pl.pallas_call(
        body, name="conv_fwd", grid=(CONV_W // LANES, s // tm),
        in_specs=tiles + befores + [pl.BlockSpec((3, LANES), lambda c, i: (0, c))],
        out_specs=pl.BlockSpec((tm, LANES), lambda c, i: (i, c)),
        out_shape=jax.ShapeDtypeStruct((s, CONV_W), BF16),
        compiler_params=_params(("parallel", "parallel")),
    )(*([proj] * 6), conv_w)


def _tail(oa, ob, proj, x, target, ada3, wa, wb, wo):
    s = x.shape[0]
    tm = min(TM_TAIL, s)
    gab_blk = OFF_GA // (2 * D_MODEL)

    def body(oa_ref, ob_ref, gab_ref, x_ref, t_ref, ada_ref, wa_ref, wb_ref, wo_ref,
             dy_ref, dgab_ref, doa_ref, dob_ref, dwo_ref, dwa_ref, dwb_ref, dgate_ref, loss_ref):
        @pl.when(pl.program_id(0) == 0)
        def _():
            dwo_ref[...] = jnp.zeros_like(dwo_ref)
            dwa_ref[...] = jnp.zeros_like(dwa_ref)
            dwb_ref[...] = jnp.zeros_like(dwb_ref)
            dgate_ref[...] = jnp.zeros_like(dgate_ref)
            loss_ref[...] = jnp.zeros_like(loss_ref)

        oa_v, ob_v = oa_ref[...], ob_ref[...]
        wa_v, wb_v, wo_v = wa_ref[...], wb_ref[...], wo_ref[...]
        a2 = _dot(oa_v, wa_v)
        b2 = _dot(ob_v, wb_v)
        sa = _sigmoid(gab_ref[:, 0:D_MODEL].astype(F32))
        sb = _sigmoid(gab_ref[:, D_MODEL:2 * D_MODEL].astype(F32))
        mb = (sa * a2 + sb * b2).astype(BF16)
        mo = _dot(mb, wo_v)
        gate = ada_ref[2:3, :]
        err = (x_ref[...] + gate * mo) - t_ref[...]
        dy = err * (1.0 / D_MODEL)
        dy_ref[...] = dy
        loss_ref[...] += 0.5 * jnp.sum(err * err) * (1.0 / D_MODEL)
        dgate_ref[...] += jnp.sum(dy * mo, axis=0, keepdims=True)
        dmo = (dy * gate).astype(BF16)
        dmerged = _dot_nt(dmo, wo_v)
        dwo_ref[...] += _dot_tn(mb, dmo)
        da2 = (dmerged * sa).astype(BF16)
        db2 = (dmerged * sb).astype(BF16)
        dgab_ref[:, 0:D_MODEL] = (dmerged * a2 * (sa * (1.0 - sa))).astype(BF16)
        dgab_ref[:, D_MODEL:2 * D_MODEL] = (dmerged * b2 * (sb * (1.0 - sb))).astype(BF16)
        doa_ref[...] = _dot_nt(da2, wa_v)
        dob_ref[...] = _dot_nt(db2, wb_v)
        dwa_ref[...] += _dot_tn(oa_v, da2)
        dwb_ref[...] += _dot_tn(ob_v, db2)

    half = pl.BlockSpec((tm, ATTN_W), lambda i: (i, 0))
    full = pl.BlockSpec((tm, D_MODEL), lambda i: (i, 0))

    def const(shape):
        return pl.BlockSpec(shape, lambda i: (0, 0))

    return pl.pallas_call(
        body, name="tail", grid=(s // tm,),
        in_specs=[half, half, pl.BlockSpec((tm, 2 * D_MODEL), lambda i: (i, gab_blk)), full, full,
                  const((3, D_MODEL)), const((ATTN_W, D_MODEL)), const((CONV_W, D_MODEL)),
                  const((D_MODEL, D_MODEL))],
        out_specs=[full, pl.BlockSpec((tm, 2 * D_MODEL), lambda i: (i, 0)), half, half,
                   const((D_MODEL, D_MODEL)), const((ATTN_W, D_MODEL)), const((CONV_W, D_MODEL)),
                   const((1, D_MODEL)), const((1, LANES))],
        out_shape=[jax.ShapeDtypeStruct((s, D_MODEL), F32),
                   jax.ShapeDtypeStruct((s, 2 * D_MODEL), BF16),
                   jax.ShapeDtypeStruct((s, ATTN_W), F32),
                   jax.ShapeDtypeStruct((s, CONV_W), F32),
                   jax.ShapeDtypeStruct((D_MODEL, D_MODEL), F32),
                   jax.ShapeDtypeStruct((ATTN_W, D_MODEL), F32),
                   jax.ShapeDtypeStruct((CONV_W, D_MODEL), F32),
                   jax.ShapeDtypeStruct((1, D_MODEL), F32),
                   jax.ShapeDtypeStruct((1, LANES), F32)],
        compiler_params=_params(("arbitrary",)),
    )(oa, ob, proj, x, target, ada3, wa, wb, wo)


def _attn_bwd_prep(doa, attn, proj):
    s = doa.shape[0]
    tm = min(TM_ELEM, s)
    za_blk = OFF_ZA // ATTN_W

    def body(doa_ref, attn_ref, za_ref, do_ref, dza_ref):
        lane = lax.broadcasted_iota(jnp.int32, (tm, LANES), 1)
        lo = lane < HEAD_DIM
        for pr in range(ATTN_W // LANES):
            sl = slice(pr * LANES, (pr + 1) * LANES)
            g, a, z = doa_ref[:, sl], attn_ref[:, sl], za_ref[:, sl].astype(F32)
            sg = _sigmoid(z)
            dat = (g * (z * sg)).astype(BF16).astype(F32)
            prod = dat * a
            dza_ref[:, sl] = (g * a * (sg * (1.0 + z * (1.0 - sg)))).astype(BF16)
            for hh in range(2):
                sel = lo if hh == 0 else jnp.logical_not(lo)
                delta3 = _split3(jnp.sum(jnp.where(sel, prod, 0.0), axis=-1, keepdims=True))
                dh = dat if hh == 0 else pltpu.roll(dat, HEAD_DIM, 1)
                tail = _place3(lane, L_ONE_Q, tuple(-d for d in delta3), 0.0)
                do_ref[2 * pr + hh] = jnp.where(lo, dh, tail).astype(BF16)

    row = pl.BlockSpec((tm, ATTN_W), lambda i: (i, 0))
    return pl.pallas_call(
        body, name="attn_bwd_prep", grid=(s // tm,),
        in_specs=[row, row, pl.BlockSpec((tm, ATTN_W), lambda i: (i, za_blk))],
        out_specs=[pl.BlockSpec((HEADS, tm, LANES), lambda i: (0, i, 0)), row],
        out_shape=[jax.ShapeDtypeStruct((HEADS, s, LANES), BF16),
                   jax.ShapeDtypeStruct((s, ATTN_W), BF16)],
        compiler_params=_params(("parallel",)),
    )(doa, attn, proj)


def _attn_bwd(qb, ka, kt, va, do):
    s = qb.shape[1]
    t = min(TQ, s)
    nt = s // t
    hp = HEADS_PER_STEP
    it, jt = _tri_steps(nt, False)

    def body(it_ref, jt_ref, q_ref, k_ref, kt_ref, v_ref, do_ref, dqt_ref, dk_ref, dv_ref):
        step = pl.program_id(1)
        i, j = it_ref[step], jt_ref[step]

        @pl.when(step == 0)
        def _():
            dqt_ref[...] = jnp.zeros_like(dqt_ref)

        @pl.when(i == j)
        def _():
            dk_ref[...] = jnp.zeros_like(dk_ref)
            dv_ref[...] = jnp.zeros_like(dv_ref)

        def update(masked):
            for hh in range(hp):
                qh, doh = q_ref[hh], do_ref[hh]
                st = _dot_nt(k_ref[hh], qh)
                if masked:
                    st = jnp.where(_causal_t(t), st, NEG)
                pt = jnp.exp(st)
                dst = (pt * _dot_nt(v_ref[hh], doh)).astype(BF16)
                dv_ref[hh] += _dot(pt.astype(BF16), doh)
                dk_ref[hh] += _dot(dst, qh)
                dqt_ref[hh, i] += _dot(kt_ref[hh], dst)

        @pl.when(i > j)
        def _():
            update(False)

        @pl.when(i == j)
        def _():
            update(True)

    pair_q = pl.BlockSpec((hp, t, LANES), lambda p, n, it_, jt_: (p, it_[n], 0))
    pair_k = pl.BlockSpec((hp, t, LANES), lambda p, n, it_, jt_: (p, jt_[n], 0))
    pair_kt = pl.BlockSpec((hp, LANES, t), lambda p, n, it_, jt_: (p, 0, jt_[n]))
    return pl.pallas_call(
        body, name="attn_bwd",
        grid_spec=pltpu.PrefetchScalarGridSpec(
            num_scalar_prefetch=2, grid=(HEADS // hp, it.shape[0]),
            in_specs=[pair_q, pair_k, pair_kt, pair_k, pair_q],
            out_specs=[pl.BlockSpec((hp, nt, LANES, t), lambda p, n, it_, jt_: (p, 0, 0, 0)),
                       pair_k, pair_k]),
        out_shape=[jax.ShapeDtypeStruct((HEADS, nt, LANES, t), F32),
                   jax.ShapeDtypeStruct((HEADS, s, LANES), F32),
                   jax.ShapeDtypeStruct((HEADS, s, LANES), F32)],
        compiler_params=_params(("parallel", "arbitrary")),
    )(it, jt, qb, ka, kt, va, do)


def _forget_bwd(dcum, fl, bf_pad):
    s = fl.shape[0]
    tc = min(TC_CUM, s)
    n = s // tc

    def body(dc_ref, fl_ref, bf_ref, df_ref, dbf_ref, carry):
        @pl.when(pl.program_id(0) == 0)
        def _():
            carry[...] = jnp.zeros_like(carry)
            dbf_ref[...] = jnp.zeros_like(dbf_ref)
        r = lax.broadcasted_iota(jnp.int32, (tc, tc), 0)
        cidx = lax.broadcasted_iota(jnp.int32, (tc, tc), 1)
        tri = (cidx >= r).astype(F32)
        dc = dc_ref[...]
        dlf = jnp.dot(tri, dc, preferred_element_type=F32, precision=lax.Precision.HIGHEST) + carry[...]
        carry[...] += jnp.sum(dc, axis=0, keepdims=True)
        lane = lax.broadcasted_iota(jnp.int32, (tc, LANES), 1)
        dfl = jnp.where(lane < HEADS, dlf * _sigmoid(-(fl_ref[...] + bf_ref[...])), 0.0)
        df_ref[...] = dfl.astype(BF16)
        dbf_ref[...] += jnp.sum(dfl, axis=0, keepdims=True)

    rev = pl.BlockSpec((tc, LANES), lambda i: (n - 1 - i, 0))
    vec = pl.BlockSpec((1, LANES), lambda i: (0, 0))
    return pl.pallas_call(
        body, name="forget_bwd", grid=(n,),
        in_specs=[rev, rev, vec], out_specs=[rev, vec],
        out_shape=[jax.ShapeDtypeStruct((s, LANES), BF16), jax.ShapeDtypeStruct((1, LANES), F32)],
        scratch_shapes=[pltpu.VMEM((1, LANES), F32)],
        compiler_params=_params(("arbitrary",)),
    )(dcum, fl, bf_pad)


def _qk_norm_bwd(dqt, dk, dv, proj, qg, kg):
    s = dv.shape[1]
    tm = dqt.shape[-1]
    scale = HEAD_DIM ** -0.5

    def body(dqt_ref, dk_ref, dv_ref, p_ref, qg_ref, kg_ref, out_ref, dqg_ref, dkg_ref, dcum_ref):
        @pl.when(pl.program_id(0) == 0)
        def _():
            dqg_ref[...] = jnp.zeros_like(dqg_ref)
            dkg_ref[...] = jnp.zeros_like(dkg_ref)
        lane = lax.broadcasted_iota(jnp.int32, (tm, LANES), 1)
        lo = lane < HEAD_DIM
        dq_rows = [dqt_ref[h, 0].T for h in range(HEADS)]
        dcum = jnp.zeros((tm, LANES), F32)
        for h in range(HEADS):
            dcum = jnp.where(lane == h, _lane_col(dq_rows[h], L_F_Q) - _lane_col(dk_ref[h], L_ONE_Q), dcum)
        dcum_ref[...] = dcum

        def pair(a, b):
            return jnp.where(lo, a, pltpu.roll(b, HEAD_DIM, 1))

        def one(raw, dy, g, dg_ref, sl, off):
            r = lax.rsqrt(_seg_sum(raw * raw, lo) * (1.0 / HEAD_DIM) + EPS)
            xhat = raw * r
            dg_ref[:, sl] += jnp.sum(dy * xhat, axis=0, keepdims=True)
            dxh = dy * g
            dx = r * (dxh - xhat * (_seg_sum(dxh * xhat, lo) * (1.0 / HEAD_DIM)))
            out_ref[:, off + sl.start:off + sl.stop] = dx.astype(BF16)

        for pr in range(ATTN_W // LANES):
            sl = slice(pr * LANES, (pr + 1) * LANES)
            dq2 = pair(dq_rows[2 * pr], dq_rows[2 * pr + 1])
            one(p_ref[:, OFF_Q + sl.start:OFF_Q + sl.stop].astype(F32), dq2 * scale, qg_ref[:, sl], dqg_ref, sl, OFF_Q)
            one(p_ref[:, OFF_K + sl.start:OFF_K + sl.stop].astype(F32), pair(dk_ref[2 * pr], dk_ref[2 * pr + 1]),
                kg_ref[:, sl], dkg_ref, sl, OFF_K)
            out_ref[:, OFF_V + sl.start:OFF_V + sl.stop] = pair(dv_ref[2 * pr], dv_ref[2 * pr + 1]).astype(BF16)

    heads = pl.BlockSpec((HEADS, tm, LANES), lambda i: (0, i, 0))
    vec = pl.BlockSpec((1, ATTN_W), lambda i: (0, 0))
    return pl.pallas_call(
        body, name="qk_norm_bwd", grid=(s // tm,),
        in_specs=[pl.BlockSpec((HEADS, 1, LANES, tm), lambda i: (0, i, 0, 0)), heads, heads,
                  pl.BlockSpec((tm, 2 * ATTN_W), lambda i: (i, 0)), vec, vec],
        out_specs=[pl.BlockSpec((tm, 3 * ATTN_W), lambda i: (i, 0)), vec, vec,
                   pl.BlockSpec((tm, LANES), lambda i: (i, 0))],
        out_shape=[jax.ShapeDtypeStruct((s, 3 * ATTN_W), BF16),
                   jax.ShapeDtypeStruct((1, ATTN_W), F32), jax.ShapeDtypeStruct((1, ATTN_W), F32),
                   jax.ShapeDtypeStruct((s, LANES), F32)],
        compiler_params=_params(("arbitrary",)),
    )(dqt, dk, dv, proj, qg, kg)


def _conv_bwd(dob, proj, conv_w):
    s = dob.shape[0]
    tm = min(TM_ELEM, s)
    tiles, befores, afters = _conv_specs(tm, s)

    def body(dob_ref, dnext_ref, gb_ref, gc_ref, u_ref, zb_ref, gch_ref, uh_ref, gbn_ref, zbn_ref, w_ref,
             dgb_ref, dgc_ref, du_ref, dzb_ref, dw_ref):
        i = pl.program_id(1)

        @pl.when(i == 0)
        def _():
            dw_ref[...] = jnp.zeros_like(dw_ref)
        gb, gc, u, zb, cu, r1, r2, conv = _conv_parts(gb_ref, gc_ref, u_ref, zb_ref, gch_ref, uh_ref, i == 0, w_ref, tm)
        g = dob_ref[...]
        sg = _sigmoid(zb)
        sz = zb * sg
        dconv = g * gb * sz
        zn = zbn_ref[0:8, :].astype(F32)
        dcn = jnp.where(i == pl.num_programs(1) - 1, 0.0,
                        dnext_ref[...] * gbn_ref[0:8, :].astype(F32) * (zn * _sigmoid(zn)))
        nxt1, nxt2 = _sub_row(dcn, 0), _sub_row(dcn, 1)
        row = lax.broadcasted_iota(jnp.int32, (tm, LANES), 0)
        f1 = jnp.where(row == tm - 1, nxt1, pltpu.roll(dconv, tm - 1, 0))
        f2 = jnp.where(row == tm - 2, nxt1, jnp.where(row == tm - 1, nxt2, pltpu.roll(dconv, tm - 2, 0)))
        dcu = w_ref[2:3, :] * dconv + w_ref[1:2, :] * f1 + w_ref[0:1, :] * f2
        dgb_ref[...] = (g * conv * sz).astype(BF16)
        dgc_ref[...] = (dcu * u).astype(BF16)
        du_ref[...] = (dcu * gc).astype(BF16)
        dzb_ref[...] = (g * gb * conv * (sg * (1.0 + zb * (1.0 - sg)))).astype(BF16)
        w_row = lax.broadcasted_iota(jnp.int32, (3, LANES), 0)
        dw0 = jnp.sum(dconv * r2, axis=0, keepdims=True)
        dw1 = jnp.sum(dconv * r1, axis=0, keepdims=True)
        dw2 = jnp.sum(dconv * cu, axis=0, keepdims=True)
        dw_ref[...] += jnp.where(w_row == 0, dw0, jnp.where(w_row == 1, dw1, dw2))

    blk = pl.BlockSpec((tm, LANES), lambda c, i: (i, c))
    nxt = pl.BlockSpec((8, LANES), lambda c, i: (jnp.minimum((i + 1) * (tm // 8), s // 8 - 1), c))
    wspec = pl.BlockSpec((3, LANES), lambda c, i: (0, c))
    return pl.pallas_call(
        body, name="conv_bwd", grid=(CONV_W // LANES, s // tm),
        in_specs=[blk, nxt] + tiles + befores + afters + [wspec],
        out_specs=[blk, blk, blk, blk, wspec],
        out_shape=[jax.ShapeDtypeStruct((s, CONV_W), BF16)] * 4 + [jax.ShapeDtypeStruct((3, CONV_W), F32)],
        compiler_params=_params(("parallel", "arbitrary")),
    )(dob, dob, *([proj] * 8), conv_w)


def _piece_layout(pieces):
    offs, off = [], 0
    for p in pieces:
        offs.append((off, p.shape[1]))
        off += p.shape[1]
    assert off == N_ALL, off
    return offs


def _dw_in(h, pieces):
    s = h.shape[0]
    tk, tn = min(TK_DW, s), TN_DW
    nk = s // tk
    main, fpiece = pieces[:-1], pieces[-1]
    layout = _piece_layout(pieces)[:-1]
    n_main = len(main)

    def body(*refs):
        p_refs, f_ref, h_ref = refs[:n_main], refs[n_main], refs[n_main + 1]
        out_ref, outf_ref, acc, accf = refs[n_main + 2:]
        n, k = pl.program_id(0), pl.program_id(1)

        @pl.when(k == 0)
        def _():
            acc[...] = jnp.zeros_like(acc)
        hv = h_ref[pl.ds(pl.multiple_of(k * tk, tk), tk), :]
        for p_ref, (off, width) in zip(p_refs, layout):
            @pl.when(jnp.logical_and(n >= off // tn, n < (off + width) // tn))
            def _():
                acc[...] += _dot_tn(p_ref[...], hv)

        @pl.when(k == nk - 1)
        def _():
            out_ref[...] = acc[...].astype(BF16)

        @pl.when(n == 0)
        def _():
            @pl.when(k == 0)
            def _():
                accf[...] = jnp.zeros_like(accf)
            accf[...] += _dot_tn(f_ref[...], hv)

            @pl.when(k == nk - 1)
            def _():
                outf_ref[...] = accf[...].astype(BF16)

    def piece_spec(off, width):
        lo, hi = off // tn, (off + width) // tn

        def index(n, k):
            active = jnp.logical_and(n >= lo, n < hi)
            return jnp.where(active, k, 0), jnp.clip(n - lo, 0, hi - lo - 1)
        return pl.BlockSpec((tk, tn), index)

    return pl.pallas_call(
        body, name="dw_in", grid=(N_MAIN // tn, nk),
        in_specs=[piece_spec(off, width) for off, width in layout]
        + [pl.BlockSpec((tk, N_FPAD), lambda n, k: (jnp.where(n == 0, k, 0), 0)),
           pl.BlockSpec((s, D_MODEL), lambda n, k: (0, 0))],
        out_specs=[pl.BlockSpec((tn, D_MODEL), lambda n, k: (n, 0)),
                   pl.BlockSpec((N_FPAD, D_MODEL), lambda n, k: (0, 0))],
        out_shape=[jax.ShapeDtypeStruct((N_MAIN, D_MODEL), BF16), jax.ShapeDtypeStruct((N_FPAD, D_MODEL), BF16)],
        scratch_shapes=[pltpu.VMEM((tn, D_MODEL), F32), pltpu.VMEM((N_FPAD, D_MODEL), F32)],
        compiler_params=_params(("arbitrary", "arbitrary")),
    )(*main, fpiece, h)


def _dh_and_dx(pieces, w_all_t, x, dy, ada3, norm_g, chip_sums):
    s = x.shape[0]
    tm = min(TM_DH, s)
    nt = s // tm
    n = len(chip_sums)
    npc = len(pieces)
    layout = _piece_layout(pieces)

    def body(*refs):
        p_refs, refs = refs[:npc], refs[npc:]
        wt_ref, x_ref, dy_ref, ada_ref, g_ref = refs[:5]
        ins, refs = refs[5:5 + n], refs[5 + n:]
        gx_ref, dsh_ref, dsc_ref, dg_ref = refs[:4]
        outs, (send_sems, recv_sems, local_sems) = refs[4:4 + n], refs[4 + n:]
        i = pl.program_id(0)

        @pl.when(i == 0)
        def _():
            for cp in _chip_copies(ins, outs, send_sems, recv_sems, local_sems):
                cp.start()
            dsh_ref[...] = jnp.zeros_like(dsh_ref)
            dsc_ref[...] = jnp.zeros_like(dsc_ref)
            dg_ref[...] = jnp.zeros_like(dg_ref)

        dh = None
        for p_ref, (off, width) in zip(p_refs, layout):
            part = _dot(p_ref[...], wt_ref[off:off + width, :])
            dh = part if dh is None else dh + part
        xv = x_ref[...]
        r = lax.rsqrt(jnp.mean(xv * xv, axis=-1, keepdims=True) + EPS)
        xhat = xv * r
        g = g_ref[...]
        one_sc = 1.0 + ada_ref[1:2, :]
        dsh_ref[...] += jnp.sum(dh, axis=0, keepdims=True)
        dsc_ref[...] += jnp.sum(dh * (xhat * g), axis=0, keepdims=True)
        dg_ref[...] += jnp.sum(dh * xhat, axis=0, keepdims=True) * one_sc
        dxh = dh * (g * one_sc)
        dx = r * (dxh - xhat * jnp.mean(dxh * xhat, axis=-1, keepdims=True))
        gx_ref[...] = dy_ref[...] + dx

        @pl.when(i == nt - 1)
        def _():
            for cp in _chip_copies(ins, outs, send_sems, recv_sems, local_sems):
                cp.wait()

    full = pl.BlockSpec((tm, D_MODEL), lambda i: (i, 0))
    vec = pl.BlockSpec((1, D_MODEL), lambda i: (0, 0))
    any_spec = pl.BlockSpec(memory_space=pl.ANY)
    res = pl.pallas_call(
        body, name="dh_dx", grid=(nt,),
        in_specs=[pl.BlockSpec((tm, p.shape[1]), lambda i: (i, 0)) for p in pieces]
        + [pl.BlockSpec((N_ALL, D_MODEL), lambda i: (0, 0)), full, full,
           pl.BlockSpec((3, D_MODEL), lambda i: (0, 0)), vec] + [any_spec] * n,
        out_specs=[full, vec, vec, vec] + [any_spec] * n,
        out_shape=[jax.ShapeDtypeStruct((s, D_MODEL), F32)] + [jax.ShapeDtypeStruct((1, D_MODEL), F32)] * 3
        + [jax.ShapeDtypeStruct(a.shape, a.dtype) for a in chip_sums],
        scratch_shapes=[pltpu.SemaphoreType.DMA((n * 3,)), pltpu.SemaphoreType.DMA((n * 3,)),
                        pltpu.SemaphoreType.DMA((n,))],
        compiler_params=_params(("arbitrary",)),
    )(*pieces, w_all_t, x, dy, ada3, norm_g, *chip_sums)
    return res[:4], res[4:]


def _sum_small(vec_all, qg_parts, kg_parts):
    def body(v_ref, q_ref, k_ref, tot_ref, gq_ref, gk_ref):
        tot = v_ref[0:1, :]
        for p in range(1, N_DEV):
            tot = tot + v_ref[p:p + 1, :]
        tot_ref[...] = tot
        gq_ref[...] = jnp.sum(q_ref[...], axis=0, keepdims=True)
        gk_ref[...] = jnp.sum(k_ref[...], axis=0, keepdims=True)

    n = vec_all.shape[-1]
    return pl.pallas_call(
        body, name="sum_small",
        out_shape=[jax.ShapeDtypeStruct((1, n), F32),
                   jax.ShapeDtypeStruct((1, HEAD_DIM), F32), jax.ShapeDtypeStruct((1, HEAD_DIM), F32)],
        compiler_params=_params(),
    )(vec_all, qg_parts, kg_parts)


def _grad_w_ada(c_cols, dada_rows):
    def body(c_ref, d_ref, out_ref):
        acc = c_ref[0] * d_ref[0]
        for b in range(1, N_DEV):
            acc = acc + c_ref[b] * d_ref[b]
        out_ref[...] = acc

    return pl.pallas_call(
        body, name="grad_w_ada",
        out_shape=jax.ShapeDtypeStruct((D_MODEL, ADA_SHARD), F32),
        compiler_params=_params(),
    )(c_cols, dada_rows)


def _adamw(w, m, v, g_parts, name):
    rows, cols = w.shape
    n_parts = g_parts.shape[0]
    tr = 256 if rows % 256 == 0 else rows
    tc = 256 if (tr == rows and rows > 256 and cols % 256 == 0) else cols
    c1 = 1.0 / (1.0 - ADAM_B1 ** ADAM_STEP)
    c2 = 1.0 / (1.0 - ADAM_B2 ** ADAM_STEP)

    def body(w_ref, m_ref, v_ref, g_ref, go_ref, d_ref, mo_ref, vo_ref):
        g = g_ref[0].astype(F32)
        for p in range(1, n_parts):
            g = g + g_ref[p].astype(F32)
        m_new = ADAM_B1 * m_ref[...] + (1.0 - ADAM_B1) * g
        v_new = ADAM_B2 * v_ref[...] + (1.0 - ADAM_B2) * (g * g)
        go_ref[...] = g
        mo_ref[...] = m_new
        vo_ref[...] = v_new
        d_ref[...] = -ADAM_LR * ((m_new * c1) / (jnp.sqrt(v_new * c2) + ADAM_EPS) + ADAM_WD * w_ref[...])

    blk = pl.BlockSpec((tr, tc), lambda i, j: (i, j))
    return pl.pallas_call(
        body, name=name, grid=(rows // tr, cols // tc),
        in_specs=[blk, blk, blk, pl.BlockSpec((n_parts, tr, tc), lambda i, j: (0, i, j))],
        out_specs=[blk] * 4,
        out_shape=[jax.ShapeDtypeStruct((rows, cols), F32)] * 4,
        compiler_params=_params(("parallel", "parallel")),
    )(w, m, v, g_parts)


_O_F = 1536


def _to_internal(wt_g):
    wf = wt_g.reshape(IN_WIDTH, D_MODEL)
    f = jnp.pad(wf[_O_F:_O_F + HEADS], ((0, N_FPAD - HEADS), (0, 0)))
    return jnp.concatenate([wf[:_O_F], wf[_O_F + HEADS:], f], axis=0)


def _from_internal(dwt, dwt_f):
    full = jnp.concatenate([dwt[:_O_F], dwt_f[:HEADS], dwt[_O_F:]], axis=0)
    return full.reshape(N_DEV, IN_SHARD, D_MODEL)


def kernel(x, c, w_ada, b_ada, norm_g, w_in, b_f, q_norm_g, k_norm_g, conv_w, w_attn_out, w_conv_out, w_o, loss_target, m_w_ada, m_b_ada, m_norm_g, m_w_in, m_b_f, m_q_norm_g, m_k_norm_g, m_conv_w, m_w_attn_out, m_w_conv_out, m_w_o, v_w_ada, v_b_ada, v_norm_g, v_w_in, v_b_f, v_q_norm_g, v_k_norm_g, v_conv_w, v_w_attn_out, v_w_conv_out, v_w_o):
    me = 4 * lax.axis_index("x") + 2 * lax.axis_index("y") + lax.axis_index("c")
    s = x.shape[1]
    x2, t2 = x[0], loss_target[0]

    cw_g, wa_g, wb_g, wo_g, w_in_g = _gather_two_level(
        [conv_w[0], w_attn_out[0].astype(BF16), w_conv_out[0].astype(BF16), w_o[0].astype(BF16),
         w_in[0].T.astype(BF16)], "gather_weights")
    c_all, ada_g = _ada_exchange(c, w_ada[0])
    ada_mine = lax.dynamic_index_in_dim(ada_g[:, :, 0, :], me, axis=1, keepdims=False)
    ada3 = (ada_mine.reshape(1, 3 * D_MODEL) + b_ada).reshape(3, D_MODEL)
    w_all_t = _to_internal(w_in_g)
    wa = jnp.transpose(wa_g, (1, 0, 2)).reshape(ATTN_W, D_MODEL)
    wb = jnp.transpose(wb_g, (1, 0, 2)).reshape(CONV_W, D_MODEL)
    wo = wo_g.reshape(D_MODEL, D_MODEL)
    cw = jnp.transpose(cw_g, (1, 0, 2)).reshape(3, CONV_W)
    qg = jnp.tile(q_norm_g, (1, HEADS))
    kg = jnp.tile(k_norm_g, (1, HEADS))
    bf_pad = jnp.pad(b_f, ((0, 0), (0, LANES - HEADS)))

    proj, fl, h = _proj_fwd(x2, ada3, norm_g, w_all_t)
    qa, ka, va, kt, vt = _qkv_prep(proj, fl, bf_pad, qg, kg)
    attn, oa, qb = _attn_fwd(qa, ka, vt, proj)
    ob = _conv_fwd(proj, cw)
    (dy, dgab, doa, dob, dwo, dwa, dwb, dgate, loss_part) = _tail(oa, ob, proj, x2, t2, ada3, wa, wb, wo)

    do, dza = _attn_bwd_prep(doa, attn, proj)
    dqt, dk, dv = _attn_bwd(qb, ka, kt, va, do)
    dqkv, dqg, dkg, dcum = _qk_norm_bwd(dqt, dk, dv, proj, qg, kg)
    df, dbf = _forget_bwd(dcum, fl, bf_pad)
    dcb, dcc, dcu, dcz, dcw = _conv_bwd(dob, proj, cw)
    pieces = [dqkv, dza, dcb, dcc, dcu, dcz, dgab, df]
    dw_main, dw_f = _dw_in(h, pieces)

    def by_core(slabs8):
        return jnp.swapaxes(slabs8.reshape((4, 2) + slabs8.shape[1:]), 0, 1).astype(BF16)

    slabs = [by_core(jnp.transpose(dwa.reshape(ATTN_W, N_DEV, LANES), (1, 0, 2))),
             by_core(jnp.transpose(dwb.reshape(CONV_W, N_DEV, LANES), (1, 0, 2))),
             by_core(dwo.reshape(N_DEV, D_MODEL // N_DEV, D_MODEL)),
             by_core(_from_internal(dw_main, dw_f))]
    theirs = _sibling_swap(slabs, "swap_grads")
    core = lax.axis_index("c").astype(jnp.int32).reshape(1)
    chip_sums = [_pair_sum(m2, t4, core, "pair_sum_" + nm)
                 for m2, t4, nm in zip(slabs, theirs, ("wa", "wb", "wo", "w_in"))]
    (grad_x, dshift, dscale, dnormg), (g_wa_parts, g_wb_parts, g_wo_parts, g_in_parts) = _dh_and_dx(
        pieces, w_all_t, x2, dy, ada3, norm_g, chip_sums)
    vec = jnp.concatenate([dshift, dscale, dgate, dnormg, dbf, dcw.reshape(1, 3 * CONV_W), dqg, dkg], axis=1)
    (vec_all,) = _exchange([vec], True, "gather_small")
    vec_all = vec_all.reshape(N_DEV, vec.shape[1])
    n_main = 4 * D_MODEL + LANES + 3 * CONV_W
    tot, g_qg, g_kg = _sum_small(
        vec_all[:, :n_main],
        vec_all[:, n_main:n_main + ATTN_W].reshape(N_DEV * HEADS, HEAD_DIM),
        vec_all[:, n_main + ATTN_W:].reshape(N_DEV * HEADS, HEAD_DIM))
    g_b_ada = tot[:, 0:3 * D_MODEL]
    g_norm_g = tot[:, 3 * D_MODEL:4 * D_MODEL]
    g_b_f = tot[:, 4 * D_MODEL:4 * D_MODEL + HEADS]
    g_cw_full = tot[:, 4 * D_MODEL + LANES:].reshape(3, CONV_W)
    g_cw = lax.dynamic_slice(g_cw_full, (0, me * (CONV_W // N_DEV)), (3, CONV_W // N_DEV))
    dada_mine = lax.dynamic_slice(vec_all[:, 0:3 * D_MODEL], (0, me * ADA_SHARD), (N_DEV, ADA_SHARD))
    g_w_ada = _grad_w_ada(jnp.transpose(c_all, (0, 2, 1)), dada_mine.reshape(N_DEV, 1, ADA_SHARD))

    upd = {}
    upd["w_ada"] = _adamw(w_ada[0], m_w_ada[0], v_w_ada[0], g_w_ada[None], "adamw_w_ada")
    upd["b_ada"] = _adamw(b_ada, m_b_ada, v_b_ada, g_b_ada[None], "adamw_b_ada")
    upd["norm_g"] = _adamw(norm_g, m_norm_g, v_norm_g, g_norm_g[None], "adamw_norm_g")
    upd["w_in"] = [u.T for u in _adamw(w_in[0].T, m_w_in[0].T, v_w_in[0].T, g_in_parts, "adamw_w_in")]
    upd["b_f"] = _adamw(b_f, m_b_f, v_b_f, g_b_f[None], "adamw_b_f")
    upd["q_norm_g"] = _adamw(q_norm_g, m_q_norm_g, v_q_norm_g, g_qg[None], "adamw_q_norm_g")
    upd["k_norm_g"] = _adamw(k_norm_g, m_k_norm_g, v_k_norm_g, g_kg[None], "adamw_k_norm_g")
    upd["conv_w"] = _adamw(conv_w[0], m_conv_w[0], v_conv_w[0], g_cw[None], "adamw_conv_w")
    upd["w_attn_out"] = _adamw(w_attn_out[0], m_w_attn_out[0], v_w_attn_out[0], g_wa_parts, "adamw_w_attn_out")
    upd["w_conv_out"] = _adamw(w_conv_out[0], m_w_conv_out[0], v_w_conv_out[0], g_wb_parts, "adamw_w_conv_out")
    upd["w_o"] = _adamw(w_o[0], m_w_o[0], v_w_o[0], g_wo_parts, "adamw_w_o")

    names = ["w_ada", "b_ada", "norm_g", "w_in", "b_f", "q_norm_g", "k_norm_g", "conv_w",
             "w_attn_out", "w_conv_out", "w_o"]
    lead = {"w_ada", "w_in", "conv_w", "w_attn_out", "w_conv_out", "w_o"}
    fix = lambda n, a: a[None] if n in lead else a
    loss = lax.psum(loss_part[0, 0], ("x", "y", "c"))
    outs = [loss, grad_x[None]]
    for k in range(4):
        outs += [fix(n, upd[n][k]) for n in names]
    return tuple(outs)
```

```python
import functools

import numpy as np
import jax
import jax.numpy as jnp
from jax import lax
from jax.experimental import pallas as pl
from jax.experimental.pallas import tpu as pltpu

F32 = jnp.float32
BF16 = jnp.bfloat16

D_MODEL = 1024
HEADS = 8
HEAD_DIM = 64
ATTN_W = 512
CONV_W = 512
N_DEV = 8
IN_WIDTH = 6152
IN_SHARD = IN_WIDTH // N_DEV
N_MAIN = 6144
N_FPAD = 128
N_ALL = N_MAIN + N_FPAD
ADA_SHARD = 3 * D_MODEL // N_DEV
EPS = 1e-6
NEG = -1e30

ADAM_LR = 0.001
ADAM_B1 = 0.9
ADAM_B2 = 0.999
ADAM_EPS = 1e-08
ADAM_WD = 0.01
ADAM_STEP = 10

LANES = 128
VMEM_LIMIT = 56 * 1024 * 1024

TM_PROJ = 256
TN_PROJ = 1024
TM_ELEM = 512
TQ = 512
HEADS_PER_STEP = 4
HEADS_PER_STEP_FWD = 8
TM_TAIL = 256
TC_CUM = 256
TK_DW = 1024
TN_DW = 512
TM_DH = 256
HALO = 16

OFF_Q, OFF_K, OFF_V, OFF_ZA, OFF_CB, OFF_CC, OFF_CU, OFF_CZ, OFF_GA, OFF_GB = (
    0, 512, 1024, 1536, 2048, 2560, 3072, 3584, 4096, 5120)


def _params(sem=None):
    return pltpu.CompilerParams(dimension_semantics=sem, vmem_limit_bytes=VMEM_LIMIT)


def _dot(a, b):
    return jnp.dot(a, b, preferred_element_type=F32)


def _dot_nt(a, b):
    return lax.dot_general(a, b, (((1,), (1,)), ((), ())), preferred_element_type=F32)


def _dot_tn(a, b):
    return lax.dot_general(a, b, (((0,), (0,)), ((), ())), preferred_element_type=F32)


def _sigmoid(x):
    return 1.0 / (1.0 + jnp.exp(-x))


def _lane_lo(shape):
    return lax.broadcasted_iota(jnp.int32, shape, len(shape) - 1) < HEAD_DIM


def _seg_sum(z, lo):
    a = jnp.sum(jnp.where(lo, z, 0.0), axis=-1, keepdims=True)
    b = jnp.sum(jnp.where(lo, 0.0, z), axis=-1, keepdims=True)
    return jnp.where(lo, a, b)


def _lane_col(z, lane):
    idx = lax.broadcasted_iota(jnp.int32, z.shape, 1)
    return jnp.sum(jnp.where(idx == lane, z, 0.0), axis=-1, keepdims=True)


def _sub_row(z, row):
    idx = lax.broadcasted_iota(jnp.int32, z.shape, 0)
    return jnp.sum(jnp.where(idx == row, z, 0.0), axis=0, keepdims=True)


def _mesh_pos():
    x, y, c = lax.axis_index("x"), lax.axis_index("y"), lax.axis_index("c")
    return x, y, c, 4 * x + 2 * y + c


def _peer(k, x, y, c):
    px = 1 - x if (k >> 2) & 1 else x
    py = 1 - y if (k >> 1) & 1 else y
    pc = 1 - c if k & 1 else c
    return (px, py, pc), 4 * px + 2 * py + pc


def _exchange(arrs, gather, name):
    n = len(arrs)
    any_spec = pl.BlockSpec(memory_space=pl.ANY)

    def body(*refs):
        ins, outs = refs[:n], refs[n:2 * n]
        send_sems, recv_sems, local_sems = refs[2 * n:]
        x, y, c, me = _mesh_pos()
        copies = []
        for a in range(n):
            own = ins[a] if gather else ins[a].at[me]
            local = pltpu.make_async_copy(own, outs[a].at[me], local_sems.at[a])
            local.start()
            copies.append(local)
            for k in range(1, N_DEV):
                dev, p = _peer(k, x, y, c)
                cp = pltpu.make_async_remote_copy(
                    src_ref=ins[a] if gather else ins[a].at[p],
                    dst_ref=outs[a].at[me],
                    send_sem=send_sems.at[a * (N_DEV - 1) + k - 1],
                    recv_sem=recv_sems.at[a * (N_DEV - 1) + k - 1],
                    device_id=dev, device_id_type=pl.DeviceIdType.MESH)
                cp.start()
                copies.append(cp)
        for cp in copies:
            cp.wait()

    out_shape = [jax.ShapeDtypeStruct((N_DEV,) + a.shape if gather else a.shape, a.dtype) for a in arrs]
    return pl.pallas_call(
        body, name=name, out_shape=out_shape,
        in_specs=[any_spec] * n, out_specs=[any_spec] * n,
        scratch_shapes=[pltpu.SemaphoreType.DMA((n * (N_DEV - 1),)),
                        pltpu.SemaphoreType.DMA((n * (N_DEV - 1),)),
                        pltpu.SemaphoreType.DMA((n,))],
    )(*arrs)


def _gather_two_level(arrs, name):
    n = len(arrs)
    any_spec = pl.BlockSpec(memory_space=pl.ANY)
    per = N_DEV - 1

    def body(*refs):
        ins, outs = refs[:n], refs[n:2 * n]
        send_sems, recv_sems, local_sems = refs[2 * n:]
        x, y, c, me = _mesh_pos()
        sibling = (x, y, 1 - c)
        chips = [(1 - x, y), (x, 1 - y), (1 - x, 1 - y)]

        def copy(a, k, src, blk, to):
            return pltpu.make_async_remote_copy(
                src_ref=src, dst_ref=outs[a].at[blk],
                send_sem=send_sems.at[a * per + k], recv_sem=recv_sems.at[a * per + k],
                device_id=to, device_id_type=pl.DeviceIdType.MESH)

        local = [pltpu.make_async_copy(ins[a], outs[a].at[me], local_sems.at[a]) for a in range(n)]
        for cp in local:
            cp.start()
        first = []
        for a in range(n):
            first.append(copy(a, 0, ins[a], me, sibling))
            first += [copy(a, 1 + j, ins[a], me, (px, py, c)) for j, (px, py) in enumerate(chips)]
        for cp in first:
            cp.start()
        passed = []
        for j, (px, py) in enumerate(chips):
            blk = 4 * px + 2 * py + c
            for a in range(n):
                copy(a, 1 + j, ins[a], blk, (x, y, c)).wait_recv()
                fwd = copy(a, 4 + j, outs[a].at[blk], blk, sibling)
                fwd.start()
                passed.append(fwd)
        for a in range(n):
            copy(a, 0, ins[a], 4 * x + 2 * y + 1 - c, (x, y, c)).wait_recv()
            for j, (px, py) in enumerate(chips):
                copy(a, 4 + j, ins[a], 4 * px + 2 * py + 1 - c, (x, y, c)).wait_recv()
        for cp in first + passed:
            cp.wait_send()
        for cp in local:
            cp.wait()

    return pl.pallas_call(
        body, name=name,
        out_shape=[jax.ShapeDtypeStruct((N_DEV,) + a.shape, a.dtype) for a in arrs],
        in_specs=[any_spec] * n, out_specs=[any_spec] * n,
        scratch_shapes=[pltpu.SemaphoreType.DMA((n * per,)), pltpu.SemaphoreType.DMA((n * per,)),
                        pltpu.SemaphoreType.DMA((n,))],
    )(*arrs)


def _sibling_swap(arrs, name):
    n = len(arrs)
    any_spec = pl.BlockSpec(memory_space=pl.ANY)

    def body(*refs):
        ins, outs = refs[:n], refs[n:2 * n]
        send_sems, recv_sems = refs[2 * n:]
        x, y, c, _ = _mesh_pos()
        copies = [pltpu.make_async_remote_copy(
            src_ref=ins[a].at[1 - c], dst_ref=outs[a], send_sem=send_sems.at[a], recv_sem=recv_sems.at[a],
            device_id=(x, y, 1 - c), device_id_type=pl.DeviceIdType.MESH) for a in range(n)]
        for cp in copies:
            cp.start()
        for cp in copies:
            cp.wait()

    return pl.pallas_call(
        body, name=name,
        out_shape=[jax.ShapeDtypeStruct(a.shape[1:], a.dtype) for a in arrs],
        in_specs=[any_spec] * n, out_specs=[any_spec] * n,
        scratch_shapes=[pltpu.SemaphoreType.DMA((n,)), pltpu.SemaphoreType.DMA((n,))],
    )(*arrs)


def _pair_sum(mine2, theirs, core, name):
    _, _, rows, cols = mine2.shape
    tr = 256 if rows % 256 == 0 else rows

    def body(core_ref, a_ref, b_ref, out_ref):
        out_ref[...] = (a_ref[...].astype(F32) + b_ref[...].astype(F32)).astype(BF16)

    return pl.pallas_call(
        body, name=name,
        grid_spec=pltpu.PrefetchScalarGridSpec(
            num_scalar_prefetch=1, grid=(4, rows // tr),
            in_specs=[pl.BlockSpec((None, None, tr, cols), lambda ch, i, core_: (core_[0], ch, i, 0)),
                      pl.BlockSpec((None, tr, cols), lambda ch, i, core_: (ch, i, 0))],
            out_specs=pl.BlockSpec((None, tr, cols), lambda ch, i, core_: (ch, i, 0))),
        out_shape=jax.ShapeDtypeStruct(theirs.shape, BF16),
        compiler_params=_params(("parallel", "parallel")),
    )(core, mine2, theirs)


def _chip_copies(ins, outs, send_sems, recv_sems, local_sems):
    x, y, c, _ = _mesh_pos()
    my_chip = 2 * x + y
    chips = [(1 - x, y), (x, 1 - y), (1 - x, 1 - y)]
    copies = []
    for a in range(len(ins)):
        copies.append(pltpu.make_async_copy(ins[a].at[my_chip], outs[a].at[my_chip], local_sems.at[a]))
        for j, (px, py) in enumerate(chips):
            copies.append(pltpu.make_async_remote_copy(
                src_ref=ins[a].at[2 * px + py], dst_ref=outs[a].at[my_chip],
                send_sem=send_sems.at[a * 3 + j], recv_sem=recv_sems.at[a * 3 + j],
                device_id=(px, py, c), device_id_type=pl.DeviceIdType.MESH))
    return copies


def _ada_exchange(c_row, w_ada_sh):
    def body(c_ref, w_ref, call_ref, adag_ref, mine_ref, send_sems, recv_sems):
        x, y, c, me = _mesh_pos()

        def copy(phase, k, src, dst):
            dev, _ = _peer(k, x, y, c)
            return pltpu.make_async_remote_copy(
                src_ref=src, dst_ref=dst,
                send_sem=send_sems.at[phase * (N_DEV - 1) + k - 1],
                recv_sem=recv_sems.at[phase * (N_DEV - 1) + k - 1],
                device_id=dev, device_id_type=pl.DeviceIdType.MESH)

        call_ref[me] = c_ref[...]
        first = [copy(0, k, c_ref, call_ref.at[me]) for k in range(1, N_DEV)]
        for cp in first:
            cp.start()
        for cp in first:
            cp.wait()
        wb = w_ref[...].astype(BF16)
        for b in range(N_DEV):
            row = jnp.broadcast_to(call_ref[b], (8, D_MODEL)).astype(BF16)
            mine_ref[b] = _sub_row(_dot(row, wb), 0)
        adag_ref[me] = mine_ref[...]
        second = [copy(1, k, mine_ref, adag_ref.at[me]) for k in range(1, N_DEV)]
        for cp in second:
            cp.start()
        for cp in second:
            cp.wait()

    vm = pl.BlockSpec(memory_space=pltpu.VMEM)
    return pl.pallas_call(
        body, name="ada_exchange",
        out_shape=[jax.ShapeDtypeStruct((N_DEV, 1, D_MODEL), F32),
                   jax.ShapeDtypeStruct((N_DEV, N_DEV, 1, ADA_SHARD), F32)],
        in_specs=[vm, vm], out_specs=[vm, vm],
        scratch_shapes=[pltpu.VMEM((N_DEV, 1, ADA_SHARD), F32),
                        pltpu.SemaphoreType.DMA((2 * (N_DEV - 1),)),
                        pltpu.SemaphoreType.DMA((2 * (N_DEV - 1),))],
        compiler_params=pltpu.CompilerParams(vmem_limit_bytes=VMEM_LIMIT),
    )(c_row, w_ada_sh)


def _proj_fwd(x, ada3, norm_g, w_all_t):
    s = x.shape[0]
    tm, tn = min(TM_PROJ, s), TN_PROJ

    def body(x_ref, ada_ref, g_ref, wt_ref, proj_ref, fl_ref, h_ref):
        xv = x_ref[...]
        r = lax.rsqrt(jnp.mean(xv * xv, axis=-1, keepdims=True) + EPS)
        hv = ((xv * r) * g_ref[...]) * (1.0 + ada_ref[1:2, :]) + ada_ref[0:1, :]
        hb = hv.astype(BF16)
        h_ref[...] = hb
        fl_ref[...] = _dot_nt(hb, wt_ref[N_MAIN:N_ALL, :])
        for j in range(N_MAIN // tn):
            proj_ref[:, j * tn:(j + 1) * tn] = _dot_nt(hb, wt_ref[j * tn:(j + 1) * tn, :]).astype(BF16)

    return pl.pallas_call(
        body, name="proj_fwd", grid=(s // tm,),
        in_specs=[pl.BlockSpec((tm, D_MODEL), lambda i: (i, 0)),
                  pl.BlockSpec((3, D_MODEL), lambda i: (0, 0)),
                  pl.BlockSpec((1, D_MODEL), lambda i: (0, 0)),
                  pl.BlockSpec((N_ALL, D_MODEL), lambda i: (0, 0))],
        out_specs=[pl.BlockSpec((tm, N_MAIN), lambda i: (i, 0)),
                   pl.BlockSpec((tm, N_FPAD), lambda i: (i, 0)),
                   pl.BlockSpec((tm, D_MODEL), lambda i: (i, 0))],
        out_shape=[jax.ShapeDtypeStruct((s, N_MAIN), BF16),
                   jax.ShapeDtypeStruct((s, N_FPAD), F32),
                   jax.ShapeDtypeStruct((s, D_MODEL), BF16)],
        compiler_params=_params(("parallel",)),
    )(x, ada3, norm_g, w_all_t)


L_ONE_Q, L_F_Q, L_LSE_Q, L_END = HEAD_DIM, HEAD_DIM + 3, HEAD_DIM + 6, HEAD_DIM + 9


def _split3(f):
    hi = f.astype(BF16).astype(F32)
    r = f - hi
    mid = r.astype(BF16).astype(F32)
    return hi, mid, r - mid


def _place3(lane, first, parts, otherwise):
    a, b, c = parts
    return jnp.where(lane == first, a, jnp.where(lane == first + 1, b, jnp.where(lane == first + 2, c, otherwise)))


def _log_forget(fl, bf):
    z = fl + bf
    lf = jnp.minimum(z, 0.0) - jnp.log1p(jnp.exp(-jnp.abs(z)))
    lane = lax.broadcasted_iota(jnp.int32, z.shape, 1)
    return jnp.where(lane < HEADS, lf, 0.0)


def _qkv_prep(proj, fl, bf_pad, qg, kg):
    s = proj.shape[0]
    tm = min(TM_ELEM, s)
    scale = HEAD_DIM ** -0.5

    def body(p_ref, fl_ref, bf_ref, qg_ref, kg_ref, qa_ref, ka_ref, va_ref, kt_ref, vt_ref, carry):
        @pl.when(pl.program_id(0) == 0)
        def _():
            carry[...] = jnp.zeros_like(carry)
        tri = (lax.broadcasted_iota(jnp.int32, (tm, tm), 1) <= lax.broadcasted_iota(jnp.int32, (tm, tm), 0)).astype(F32)
        cum_v = jnp.dot(tri, _log_forget(fl_ref[...], bf_ref[...]), preferred_element_type=F32,
                        precision=lax.Precision.HIGHEST) + carry[...]
        carry[...] = _sub_row(cum_v, tm - 1)
        lane = lax.broadcasted_iota(jnp.int32, (tm, LANES), 1)
        lo = lane < HEAD_DIM
        v_tail = jnp.where(lane < L_F_Q, 1.0, 0.0)
        for pr in range(ATTN_W // LANES):
            sl = slice(pr * LANES, (pr + 1) * LANES)
            q2 = p_ref[:, OFF_Q + pr * LANES:OFF_Q + (pr + 1) * LANES].astype(F32)
            k2 = p_ref[:, OFF_K + pr * LANES:OFF_K + (pr + 1) * LANES].astype(F32)
            v2 = p_ref[:, OFF_V + pr * LANES:OFF_V + (pr + 1) * LANES].astype(F32)
            rq = lax.rsqrt(_seg_sum(q2 * q2, lo) * (1.0 / HEAD_DIM) + EPS)
            rk = lax.rsqrt(_seg_sum(k2 * k2, lo) * (1.0 / HEAD_DIM) + EPS)
            qn = ((q2 * rq) * qg_ref[:, sl]) * scale
            kn = (k2 * rk) * kg_ref[:, sl]
            for hh in range(2):
                h = 2 * pr + hh
                f3 = _split3(_lane_col(cum_v, h))
                qh = qn if hh == 0 else pltpu.roll(qn, HEAD_DIM, 1)
                kh = kn if hh == 0 else pltpu.roll(kn, HEAD_DIM, 1)
                vh = v2 if hh == 0 else pltpu.roll(v2, HEAD_DIM, 1)
                q_tail = jnp.where(lane < L_F_Q, 1.0, _place3(lane, L_F_Q, f3, 0.0))
                k_tail = _place3(lane, L_ONE_Q, tuple(-f for f in f3), jnp.where(lane < L_END, 1.0, 0.0))
                k_row = jnp.where(lo, kh, k_tail)
                v_row = jnp.where(lo, vh, v_tail)
                qa_ref[h] = jnp.where(lo, qh, q_tail).astype(BF16)
                ka_ref[h] = k_row.astype(BF16)
                va_ref[h] = v_row.astype(BF16)
                kt_ref[h] = k_row.T.astype(BF16)
                vt_ref[h] = v_row.T.astype(BF16)

    heads = pl.BlockSpec((HEADS, tm, LANES), lambda i: (0, i, 0))
    heads_t = pl.BlockSpec((HEADS, LANES, tm), lambda i: (0, 0, i))
    vec = pl.BlockSpec((1, ATTN_W), lambda i: (0, 0))
    return pl.pallas_call(
        body, name="qkv_prep", grid=(s // tm,),
        in_specs=[pl.BlockSpec((tm, 3 * ATTN_W), lambda i: (i, 0)),
                  pl.BlockSpec((tm, LANES), lambda i: (i, 0)),
                  pl.BlockSpec((1, LANES), lambda i: (0, 0)), vec, vec],
        out_specs=[heads, heads, heads, heads_t, heads_t],
        out_shape=[jax.ShapeDtypeStruct((HEADS, s, LANES), BF16)] * 3
        + [jax.ShapeDtypeStruct((HEADS, LANES, s), BF16)] * 2,
        scratch_shapes=[pltpu.VMEM((1, LANES), F32)],
        compiler_params=_params(("arbitrary",)),
    )(proj, fl, bf_pad, qg, kg)


def _causal_t(t):
    return lax.broadcasted_iota(jnp.int32, (t, t), 0) <= lax.broadcasted_iota(jnp.int32, (t, t), 1)


def _tri_steps(nt, q_major):
    if q_major:
        pairs = [(i, j) for i in range(nt) for j in range(i + 1)]
    else:
        pairs = [(i, j) for j in range(nt) for i in range(j, nt)]
    return (jnp.asarray(np.array([p[0] for p in pairs], np.int32)),
            jnp.asarray(np.array([p[1] for p in pairs], np.int32)))


def _attn_fwd(qa, ka, vt, proj):
    s = qa.shape[1]
    t = min(TQ, s)
    it, jt = _tri_steps(s // t, True)
    hp = HEADS_PER_STEP_FWD
    wide = hp * HEAD_DIM
    za_blk = OFF_ZA // wide

    def body(it_ref, jt_ref, q_ref, k_ref, vt_ref, za_ref, attn_ref, oa_ref, qb_ref, m_s, acc_s, pair_s):
        step = pl.program_id(1)
        i, j = it_ref[step], jt_ref[step]

        @pl.when(j == 0)
        def _():
            m_s[...] = jnp.full_like(m_s, NEG)
            acc_s[...] = jnp.zeros_like(acc_s)

        def update(masked):
            for hh in range(hp):
                st = _dot_nt(k_ref[hh], q_ref[hh])
                if masked:
                    st = jnp.where(_causal_t(t), st, NEG)
                m_prev = m_s[hh]
                m_next = jnp.maximum(m_prev, jnp.max(st, axis=0, keepdims=True))
                alpha = jnp.exp(m_prev - m_next)
                pt = jnp.exp(st - m_next).astype(BF16)
                acc_s[hh] = acc_s[hh] * alpha + _dot(vt_ref[hh], pt)
                m_s[hh] = m_next

        @pl.when(j < i)
        def _():
            update(False)

        @pl.when(j == i)
        def _():
            update(True)
            row = lax.broadcasted_iota(jnp.int32, (LANES, t), 0)
            lane = lax.broadcasted_iota(jnp.int32, (t, LANES), 1)
            for hh in range(hp):
                l_row = acc_s[hh, L_ONE_Q:L_ONE_Q + 1, :]
                pair_s[hh * HEAD_DIM:(hh + 1) * HEAD_DIM, :] = acc_s[hh, 0:HEAD_DIM, :] / l_row
                lse3 = _split3(m_s[hh] + jnp.log(l_row))
                tail_t = _place3(row, L_LSE_Q, tuple(-x for x in lse3), 0.0)
                keep_q = jnp.logical_or(lane < L_LSE_Q, lane >= L_END)
                qb_ref[hh] = jnp.where(keep_q, q_ref[hh].astype(F32), tail_t.T).astype(BF16)
            out = pair_s[...].T
            attn_ref[...] = out
            z = za_ref[...].astype(F32)
            oa_ref[...] = (out * (z * _sigmoid(z))).astype(BF16)

    pair_q = pl.BlockSpec((hp, t, LANES), lambda p, n, it_, jt_: (p, it_[n], 0))
    pair_k = pl.BlockSpec((hp, t, LANES), lambda p, n, it_, jt_: (p, jt_[n], 0))
    pair_kt = pl.BlockSpec((hp, LANES, t), lambda p, n, it_, jt_: (p, 0, jt_[n]))
    out_q = pl.BlockSpec((t, wide), lambda p, n, it_, jt_: (it_[n], p))
    return pl.pallas_call(
        body, name="attn_fwd",
        grid_spec=pltpu.PrefetchScalarGridSpec(
            num_scalar_prefetch=2, grid=(HEADS // hp, it.shape[0]),
            in_specs=[pair_q, pair_k, pair_kt,
                      pl.BlockSpec((t, wide), lambda p, n, it_, jt_: (it_[n], za_blk + p))],
            out_specs=[out_q, out_q, pair_q],
            scratch_shapes=[pltpu.VMEM((hp, 1, t), F32), pltpu.VMEM((hp, LANES, t), F32),
                            pltpu.VMEM((wide, t), F32)]),
        out_shape=[jax.ShapeDtypeStruct((s, ATTN_W), F32),
                   jax.ShapeDtypeStruct((s, ATTN_W), BF16),
                   jax.ShapeDtypeStruct((HEADS, s, LANES), BF16)],
        compiler_params=_params(("parallel", "arbitrary")),
    )(it, jt, qa, ka, vt, proj)


def _conv_parts(gb_ref, gc_ref, u_ref, zb_ref, gch_ref, uh_ref, first, w_ref, tm):
    gb, gc = gb_ref[...].astype(F32), gc_ref[...].astype(F32)
    u, zb = u_ref[...].astype(F32), zb_ref[...].astype(F32)
    cu = gc * u
    cu_h = jnp.where(first, 0.0, gch_ref[...].astype(F32) * uh_ref[...].astype(F32))
    prev1, prev2 = _sub_row(cu_h, HALO - 1), _sub_row(cu_h, HALO - 2)
    row = lax.broadcasted_iota(jnp.int32, (tm, LANES), 0)
    r1 = jnp.where(row == 0, prev1, pltpu.roll(cu, 1, 0))
    r2 = jnp.where(row == 0, prev2, jnp.where(row == 1, prev1, pltpu.roll(cu, 2, 0)))
    conv = w_ref[2:3, :] * cu + w_ref[1:2, :] * r1 + w_ref[0:1, :] * r2
    return gb, gc, u, zb, cu, r1, r2, conv


def _conv_specs(tm, s):
    def tile(off):
        return pl.BlockSpec((tm, LANES), lambda c, i: (i, off // LANES + c))

    def before(off):
        return pl.BlockSpec((HALO, LANES), lambda c, i: (jnp.maximum(i * (tm // HALO) - 1, 0), off // LANES + c))

    def after(off):
        return pl.BlockSpec((HALO, LANES),
                            lambda c, i: (jnp.minimum((i + 1) * (tm // HALO), s // HALO - 1), off // LANES + c))

    return ([tile(OFF_CB), tile(OFF_CC), tile(OFF_CU), tile(OFF_CZ)], [before(OFF_CC), before(OFF_CU)],
            [after(OFF_CB), after(OFF_CZ)])


def _conv_fwd(proj, conv_w):
    s = proj.shape[0]
    tm = min(TM_ELEM, s)
    tiles, befores, _ = _conv_specs(tm, s)

    def body(gb_ref, gc_ref, u_ref, zb_ref, gch_ref, uh_ref, w_ref, ob_ref):
        first = pl.program_id(1) == 0
        gb, _, _, zb, _, _, _, conv = _conv_parts(gb_ref, gc_ref, u_ref, zb_ref, gch_ref, uh_ref, first, w_ref, tm)
        ob_ref[...] = (gb * conv * (zb * _sigmoid(zb))).astype(BF16)

    return pl.pallas_call(
        body, name="conv_fwd", grid=(CONV_W // LANES, s // tm),
        in_specs=tiles + befores + [pl.BlockSpec((3, LANES), lambda c, i: (0, c))],
        out_specs=pl.BlockSpec((tm, LANES), lambda c, i: (i, c)),
        out_shape=jax.ShapeDtypeStruct((s, CONV_W), BF16),
        compiler_params=_params(("parallel", "parallel")),
    )(*([proj] * 6), conv_w)


def _tail(oa, ob, proj, x, target, ada3, wa, wb, wo):
    s = x.shape[0]
    tm = min(TM_TAIL, s)
    gab_blk = OFF_GA // (2 * D_MODEL)

    def body(oa_ref, ob_ref, gab_ref, x_ref, t_ref, ada_ref, wa_ref, wb_ref, wo_ref,
             dy_ref, dgab_ref, doa_ref, dob_ref, dwo_ref, dwa_ref, dwb_ref, dgate_ref, loss_ref):
        @pl.when(pl.program_id(0) == 0)
        def _():
            dwo_ref[...] = jnp.zeros_like(dwo_ref)
            dwa_ref[...] = jnp.zeros_like(dwa_ref)
            dwb_ref[...] = jnp.zeros_like(dwb_ref)
            dgate_ref[...] = jnp.zeros_like(dgate_ref)
            loss_ref[...] = jnp.zeros_like(loss_ref)

        oa_v, ob_v = oa_ref[...], ob_ref[...]
        wa_v, wb_v, wo_v = wa_ref[...], wb_ref[...], wo_ref[...]
        a2 = _dot(oa_v, wa_v)
        b2 = _dot(ob_v, wb_v)
        sa = _sigmoid(gab_ref[:, 0:D_MODEL].astype(F32))
        sb = _sigmoid(gab_ref[:, D_MODEL:2 * D_MODEL].astype(F32))
        mb = (sa * a2 + sb * b2).astype(BF16)
        mo = _dot(mb, wo_v)
        gate = ada_ref[2:3, :]
        err = (x_ref[...] + gate * mo) - t_ref[...]
        dy = err * (1.0 / D_MODEL)
        dy_ref[...] = dy
        loss_ref[...] += 0.5 * jnp.sum(err * err) * (1.0 / D_MODEL)
        dgate_ref[...] += jnp.sum(dy * mo, axis=0, keepdims=True)
        dmo = (dy * gate).astype(BF16)
        dmerged = _dot_nt(dmo, wo_v)
        dwo_ref[...] += _dot_tn(mb, dmo)
        da2 = (dmerged * sa).astype(BF16)
        db2 = (dmerged * sb).astype(BF16)
        dgab_ref[:, 0:D_MODEL] = (dmerged * a2 * (sa * (1.0 - sa))).astype(BF16)
        dgab_ref[:, D_MODEL:2 * D_MODEL] = (dmerged * b2 * (sb * (1.0 - sb))).astype(BF16)
        doa_ref[...] = _dot_nt(da2, wa_v)
        dob_ref[...] = _dot_nt(db2, wb_v)
        dwa_ref[...] += _dot_tn(oa_v, da2)
        dwb_ref[...] += _dot_tn(ob_v, db2)

    half = pl.BlockSpec((tm, ATTN_W), lambda i: (i, 0))
    full = pl.BlockSpec((tm, D_MODEL), lambda i: (i, 0))

    def const(shape):
        return pl.BlockSpec(shape, lambda i: (0, 0))

    return pl.pallas_call(
        body, name="tail", grid=(s // tm,),
        in_specs=[half, half, pl.BlockSpec((tm, 2 * D_MODEL), lambda i: (i, gab_blk)), full, full,
                  const((3, D_MODEL)), const((ATTN_W, D_MODEL)), const((CONV_W, D_MODEL)),
                  const((D_MODEL, D_MODEL))],
        out_specs=[full, pl.BlockSpec((tm, 2 * D_MODEL), lambda i: (i, 0)), half, half,
                   const((D_MODEL, D_MODEL)), const((ATTN_W, D_MODEL)), const((CONV_W, D_MODEL)),
                   const((1, D_MODEL)), const((1, LANES))],
        out_shape=[jax.ShapeDtypeStruct((s, D_MODEL), F32),
                   jax.ShapeDtypeStruct((s, 2 * D_MODEL), BF16),
                   jax.ShapeDtypeStruct((s, ATTN_W), F32),
                   jax.ShapeDtypeStruct((s, CONV_W), F32),
                   jax.ShapeDtypeStruct((D_MODEL, D_MODEL), F32),
                   jax.ShapeDtypeStruct((ATTN_W, D_MODEL), F32),
                   jax.ShapeDtypeStruct((CONV_W, D_MODEL), F32),
                   jax.ShapeDtypeStruct((1, D_MODEL), F32),
                   jax.ShapeDtypeStruct((1, LANES), F32)],
        compiler_params=_params(("arbitrary",)),
    )(oa, ob, proj, x, target, ada3, wa, wb, wo)


def _attn_bwd_prep(doa, attn, proj):
    s = doa.shape[0]
    tm = min(TM_ELEM, s)
    za_blk = OFF_ZA // ATTN_W

    def body(doa_ref, attn_ref, za_ref, do_ref, dza_ref):
        lane = lax.broadcasted_iota(jnp.int32, (tm, LANES), 1)
        lo = lane < HEAD_DIM
        for pr in range(ATTN_W // LANES):
            sl = slice(pr * LANES, (pr + 1) * LANES)
            g, a, z = doa_ref[:, sl], attn_ref[:, sl], za_ref[:, sl].astype(F32)
            sg = _sigmoid(z)
            dat = (g * (z * sg)).astype(BF16).astype(F32)
            prod = dat * a
            dza_ref[:, sl] = (g * a * (sg * (1.0 + z * (1.0 - sg)))).astype(BF16)
            for hh in range(2):
                sel = lo if hh == 0 else jnp.logical_not(lo)
                delta3 = _split3(jnp.sum(jnp.where(sel, prod, 0.0), axis=-1, keepdims=True))
                dh = dat if hh == 0 else pltpu.roll(dat, HEAD_DIM, 1)
                tail = _place3(lane, L_ONE_Q, tuple(-d for d in delta3), 0.0)
                do_ref[2 * pr + hh] = jnp.where(lo, dh, tail).astype(BF16)

    row = pl.BlockSpec((tm, ATTN_W), lambda i: (i, 0))
    return pl.pallas_call(
        body, name="attn_bwd_prep", grid=(s // tm,),
        in_specs=[row, row, pl.BlockSpec((tm, ATTN_W), lambda i: (i, za_blk))],
        out_specs=[pl.BlockSpec((HEADS, tm, LANES), lambda i: (0, i, 0)), row],
        out_shape=[jax.ShapeDtypeStruct((HEADS, s, LANES), BF16),
                   jax.ShapeDtypeStruct((s, ATTN_W), BF16)],
        compiler_params=_params(("parallel",)),
    )(doa, attn, proj)


def _attn_bwd(qb, ka, kt, va, do):
    s = qb.shape[1]
    t = min(TQ, s)
    nt = s // t
    hp = HEADS_PER_STEP
    it, jt = _tri_steps(nt, False)

    def body(it_ref, jt_ref, q_ref, k_ref, kt_ref, v_ref, do_ref, dqt_ref, dk_ref, dv_ref):
        step = pl.program_id(1)
        i, j = it_ref[step], jt_ref[step]

        @pl.when(step == 0)
        def _():
            dqt_ref[...] = jnp.zeros_like(dqt_ref)

        @pl.when(i == j)
        def _():
            dk_ref[...] = jnp.zeros_like(dk_ref)
            dv_ref[...] = jnp.zeros_like(dv_ref)

        def update(masked):
            for hh in range(hp):
                qh, doh = q_ref[hh], do_ref[hh]
                st = _dot_nt(k_ref[hh], qh)
                if masked:
                    st = jnp.where(_causal_t(t), st, NEG)
                pt = jnp.exp(st)
                dst = (pt * _dot_nt(v_ref[hh], doh)).astype(BF16)
                dv_ref[hh] += _dot(pt.astype(BF16), doh)
                dk_ref[hh] += _dot(dst, qh)
                dqt_ref[hh, i] += _dot(kt_ref[hh], dst)

        @pl.when(i > j)
        def _():
            update(False)

        @pl.when(i == j)
        def _():
            update(True)

    pair_q = pl.BlockSpec((hp, t, LANES), lambda p, n, it_, jt_: (p, it_[n], 0))
    pair_k = pl.BlockSpec((hp, t, LANES), lambda p, n, it_, jt_: (p, jt_[n], 0))
    pair_kt = pl.BlockSpec((hp, LANES, t), lambda p, n, it_, jt_: (p, 0, jt_[n]))
    return pl.pallas_call(
        body, name="attn_bwd",
        grid_spec=pltpu.PrefetchScalarGridSpec(
            num_scalar_prefetch=2, grid=(HEADS // hp, it.shape[0]),
            in_specs=[pair_q, pair_k, pair_kt, pair_k, pair_q],
            out_specs=[pl.BlockSpec((hp, nt, LANES, t), lambda p, n, it_, jt_: (p, 0, 0, 0)),
                       pair_k, pair_k]),
        out_shape=[jax.ShapeDtypeStruct((HEADS, nt, LANES, t), F32),
                   jax.ShapeDtypeStruct((HEADS, s, LANES), F32),
                   jax.ShapeDtypeStruct((HEADS, s, LANES), F32)],
        compiler_params=_params(("parallel", "arbitrary")),
    )(it, jt, qb, ka, kt, va, do)


def _forget_bwd(dcum, fl, bf_pad):
    s = fl.shape[0]
    tc = min(TC_CUM, s)
    n = s // tc

    def body(dc_ref, fl_ref, bf_ref, df_ref, dbf_ref, carry):
        @pl.when(pl.program_id(0) == 0)
        def _():
            carry[...] = jnp.zeros_like(carry)
            dbf_ref[...] = jnp.zeros_like(dbf_ref)
        r = lax.broadcasted_iota(jnp.int32, (tc, tc), 0)
        cidx = lax.broadcasted_iota(jnp.int32, (tc, tc), 1)
        tri = (cidx >= r).astype(F32)
        dc = dc_ref[...]
        dlf = jnp.dot(tri, dc, preferred_element_type=F32, precision=lax.Precision.HIGHEST) + carry[...]
        carry[...] += jnp.sum(dc, axis=0, keepdims=True)
        lane = lax.broadcasted_iota(jnp.int32, (tc, LANES), 1)
        dfl = jnp.where(lane < HEADS, dlf * _sigmoid(-(fl_ref[...] + bf_ref[...])), 0.0)
        df_ref[...] = dfl.astype(BF16)
        dbf_ref[...] += jnp.sum(dfl, axis=0, keepdims=True)

    rev = pl.BlockSpec((tc, LANES), lambda i: (n - 1 - i, 0))
    vec = pl.BlockSpec((1, LANES), lambda i: (0, 0))
    return pl.pallas_call(
        body, name="forget_bwd", grid=(n,),
        in_specs=[rev, rev, vec], out_specs=[rev, vec],
        out_shape=[jax.ShapeDtypeStruct((s, LANES), BF16), jax.ShapeDtypeStruct((1, LANES), F32)],
        scratch_shapes=[pltpu.VMEM((1, LANES), F32)],
        compiler_params=_params(("arbitrary",)),
    )(dcum, fl, bf_pad)


def _qk_norm_bwd(dqt, dk, dv, proj, qg, kg):
    s = dv.shape[1]
    tm = dqt.shape[-1]
    scale = HEAD_DIM ** -0.5

    def body(dqt_ref, dk_ref, dv_ref, p_ref, qg_ref, kg_ref, out_ref, dqg_ref, dkg_ref, dcum_ref):
        @pl.when(pl.program_id(0) == 0)
        def _():
            dqg_ref[...] = jnp.zeros_like(dqg_ref)
            dkg_ref[...] = jnp.zeros_like(dkg_ref)
        lane = lax.broadcasted_iota(jnp.int32, (tm, LANES), 1)
        lo = lane < HEAD_DIM
        dq_rows = [dqt_ref[h, 0].T for h in range(HEADS)]
        dcum = jnp.zeros((tm, LANES), F32)
        for h in range(HEADS):
            dcum = jnp.where(lane == h, _lane_col(dq_rows[h], L_F_Q) - _lane_col(dk_ref[h], L_ONE_Q), dcum)
        dcum_ref[...] = dcum

        def pair(a, b):
            return jnp.where(lo, a, pltpu.roll(b, HEAD_DIM, 1))

        def one(raw, dy, g, dg_ref, sl, off):
            r = lax.rsqrt(_seg_sum(raw * raw, lo) * (1.0 / HEAD_DIM) + EPS)
            xhat = raw * r
            dg_ref[:, sl] += jnp.sum(dy * xhat, axis=0, keepdims=True)
            dxh = dy * g
            dx = r * (dxh - xhat * (_seg_sum(dxh * xhat, lo) * (1.0 / HEAD_DIM)))
            out_ref[:, off + sl.start:off + sl.stop] = dx.astype(BF16)

        for pr in range(ATTN_W // LANES):
            sl = slice(pr * LANES, (pr + 1) * LANES)
            dq2 = pair(dq_rows[2 * pr], dq_rows[2 * pr + 1])
            one(p_ref[:, OFF_Q + sl.start:OFF_Q + sl.stop].astype(F32), dq2 * scale, qg_ref[:, sl], dqg_ref, sl, OFF_Q)
            one(p_ref[:, OFF_K + sl.start:OFF_K + sl.stop].astype(F32), pair(dk_ref[2 * pr], dk_ref[2 * pr + 1]),
                kg_ref[:, sl], dkg_ref, sl, OFF_K)
            out_ref[:, OFF_V + sl.start:OFF_V + sl.stop] = pair(dv_ref[2 * pr], dv_ref[2 * pr + 1]).astype(BF16)

    heads = pl.BlockSpec((HEADS, tm, LANES), lambda i: (0, i, 0))
    vec = pl.BlockSpec((1, ATTN_W), lambda i: (0, 0))
    return pl.pallas_call(
        body, name="qk_norm_bwd", grid=(s // tm,),
        in_specs=[pl.BlockSpec((HEADS, 1, LANES, tm), lambda i: (0, i, 0, 0)), heads, heads,
                  pl.BlockSpec((tm, 2 * ATTN_W), lambda i: (i, 0)), vec, vec],
        out_specs=[pl.BlockSpec((tm, 3 * ATTN_W), lambda i: (i, 0)), vec, vec,
                   pl.BlockSpec((tm, LANES), lambda i: (i, 0))],
        out_shape=[jax.ShapeDtypeStruct((s, 3 * ATTN_W), BF16),
                   jax.ShapeDtypeStruct((1, ATTN_W), F32), jax.ShapeDtypeStruct((1, ATTN_W), F32),
                   jax.ShapeDtypeStruct((s, LANES), F32)],
        compiler_params=_params(("arbitrary",)),
    )(dqt, dk, dv, proj, qg, kg)


def _conv_bwd(dob, proj, conv_w):
    s = dob.shape[0]
    tm = min(TM_ELEM, s)
    tiles, befores, afters = _conv_specs(tm, s)

    def body(dob_ref, dnext_ref, gb_ref, gc_ref, u_ref, zb_ref, gch_ref, uh_ref, gbn_ref, zbn_ref, w_ref,
             dgb_ref, dgc_ref, du_ref, dzb_ref, dw_ref):
        i = pl.program_id(1)

        @pl.when(i == 0)
        def _():
            dw_ref[...] = jnp.zeros_like(dw_ref)
        gb, gc, u, zb, cu, r1, r2, conv = _conv_parts(gb_ref, gc_ref, u_ref, zb_ref, gch_ref, uh_ref, i == 0, w_ref, tm)
        g = dob_ref[...]
        sg = _sigmoid(zb)
        sz = zb * sg
        dconv = g * gb * sz
        zn = zbn_ref[0:8, :].astype(F32)
        dcn = jnp.where(i == pl.num_programs(1) - 1, 0.0,
                        dnext_ref[...] * gbn_ref[0:8, :].astype(F32) * (zn * _sigmoid(zn)))
        nxt1, nxt2 = _sub_row(dcn, 0), _sub_row(dcn, 1)
        row = lax.broadcasted_iota(jnp.int32, (tm, LANES), 0)
        f1 = jnp.where(row == tm - 1, nxt1, pltpu.roll(dconv, tm - 1, 0))
        f2 = jnp.where(row == tm - 2, nxt1, jnp.where(row == tm - 1, nxt2, pltpu.roll(dconv, tm - 2, 0)))
        dcu = w_ref[2:3, :] * dconv + w_ref[1:2, :] * f1 + w_ref[0:1, :] * f2
        dgb_ref[...] = (g * conv * sz).astype(BF16)
        dgc_ref[...] = (dcu * u).astype(BF16)
        du_ref[...] = (dcu * gc).astype(BF16)
        dzb_ref[...] = (g * gb * conv * (sg * (1.0 + zb * (1.0 - sg)))).astype(BF16)
        w_row = lax.broadcasted_iota(jnp.int32, (3, LANES), 0)
        dw0 = jnp.sum(dconv * r2, axis=0, keepdims=True)
        dw1 = jnp.sum(dconv * r1, axis=0, keepdims=True)
        dw2 = jnp.sum(dconv * cu, axis=0, keepdims=True)
        dw_ref[...] += jnp.where(w_row == 0, dw0, jnp.where(w_row == 1, dw1, dw2))

    blk = pl.BlockSpec((tm, LANES), lambda c, i: (i, c))
    nxt = pl.BlockSpec((8, LANES), lambda c, i: (jnp.minimum((i + 1) * (tm // 8), s // 8 - 1), c))
    wspec = pl.BlockSpec((3, LANES), lambda c, i: (0, c))
    return pl.pallas_call(
        body, name="conv_bwd", grid=(CONV_W // LANES, s // tm),
        in_specs=[blk, nxt] + tiles + befores + afters + [wspec],
        out_specs=[blk, blk, blk, blk, wspec],
        out_shape=[jax.ShapeDtypeStruct((s, CONV_W), BF16)] * 4 + [jax.ShapeDtypeStruct((3, CONV_W), F32)],
        compiler_params=_params(("parallel", "arbitrary")),
    )(dob, dob, *([proj] * 8), conv_w)


def _piece_layout(pieces):
    offs, off = [], 0
    for p in pieces:
        offs.append((off, p.shape[1]))
        off += p.shape[1]
    assert off == N_ALL, off
    return offs


def _dw_in(h, pieces):
    s = h.shape[0]
    tk, tn = min(TK_DW, s), TN_DW
    nk = s // tk
    main, fpiece = pieces[:-1], pieces[-1]
    layout = _piece_layout(pieces)[:-1]
    n_main = len(main)

    def body(*refs):
        p_refs, f_ref, h_ref = refs[:n_main], refs[n_main], refs[n_main + 1]
        out_ref, outf_ref, acc, accf = refs[n_main + 2:]
        n, k = pl.program_id(0), pl.program_id(1)

        @pl.when(k == 0)
        def _():
            acc[...] = jnp.zeros_like(acc)
        hv = h_ref[pl.ds(pl.multiple_of(k * tk, tk), tk), :]
        for p_ref, (off, width) in zip(p_refs, layout):
            @pl.when(jnp.logical_and(n >= off // tn, n < (off + width) // tn))
            def _():
                acc[...] += _dot_tn(p_ref[...], hv)

        @pl.when(k == nk - 1)
        def _():
            out_ref[...] = acc[...].astype(BF16)

        @pl.when(n == 0)
        def _():
            @pl.when(k == 0)
            def _():
                accf[...] = jnp.zeros_like(accf)
            accf[...] += _dot_tn(f_ref[...], hv)

            @pl.when(k == nk - 1)
            def _():
                outf_ref[...] = accf[...].astype(BF16)

    def piece_spec(off, width):
        lo, hi = off // tn, (off + width) // tn

        def index(n, k):
            active = jnp.logical_and(n >= lo, n < hi)
            return jnp.where(active, k, 0), jnp.clip(n - lo, 0, hi - lo - 1)
        return pl.BlockSpec((tk, tn), index)

    return pl.pallas_call(
        body, name="dw_in", grid=(N_MAIN // tn, nk),
        in_specs=[piece_spec(off, width) for off, width in layout]
        + [pl.BlockSpec((tk, N_FPAD), lambda n, k: (jnp.where(n == 0, k, 0), 0)),
           pl.BlockSpec((s, D_MODEL), lambda n, k: (0, 0))],
        out_specs=[pl.BlockSpec((tn, D_MODEL), lambda n, k: (n, 0)),
                   pl.BlockSpec((N_FPAD, D_MODEL), lambda n, k: (0, 0))],
        out_shape=[jax.ShapeDtypeStruct((N_MAIN, D_MODEL), BF16), jax.ShapeDtypeStruct((N_FPAD, D_MODEL), BF16)],
        scratch_shapes=[pltpu.VMEM((tn, D_MODEL), F32), pltpu.VMEM((N_FPAD, D_MODEL), F32)],
        compiler_params=_params(("arbitrary", "arbitrary")),
    )(*main, fpiece, h)


def _dh_and_dx(pieces, w_all_t, x, dy, ada3, norm_g, chip_sums):
    s = x.shape[0]
    tm = min(TM_DH, s)
    nt = s // tm
    n = len(chip_sums)
    npc = len(pieces)
    layout = _piece_layout(pieces)

    def body(*refs):
        p_refs, refs = refs[:npc], refs[npc:]
        wt_ref, x_ref, dy_ref, ada_ref, g_ref = refs[:5]
        ins, refs = refs[5:5 + n], refs[5 + n:]
        gx_ref, dsh_ref, dsc_ref, dg_ref = refs[:4]
        outs, (send_sems, recv_sems, local_sems) = refs[4:4 + n], refs[4 + n:]
        i = pl.program_id(0)

        @pl.when(i == 0)
        def _():
            for cp in _chip_copies(ins, outs, send_sems, recv_sems, local_sems):
                cp.start()
            dsh_ref[...] = jnp.zeros_like(dsh_ref)
            dsc_ref[...] = jnp.zeros_like(dsc_ref)
            dg_ref[...] = jnp.zeros_like(dg_ref)

        dh = None
        for p_ref, (off, width) in zip(p_refs, layout):
            part = _dot(p_ref[...], wt_ref[off:off + width, :])
            dh = part if dh is None else dh + part
        xv = x_ref[...]
        r = lax.rsqrt(jnp.mean(xv * xv, axis=-1, keepdims=True) + EPS)
        xhat = xv * r
        g = g_ref[...]
        one_sc = 1.0 + ada_ref[1:2, :]
        dsh_ref[...] += jnp.sum(dh, axis=0, keepdims=True)
        dsc_ref[...] += jnp.sum(dh * (xhat * g), axis=0, keepdims=True)
        dg_ref[...] += jnp.sum(dh * xhat, axis=0, keepdims=True) * one_sc
        dxh = dh * (g * one_sc)
        dx = r * (dxh - xhat * jnp.mean(dxh * xhat, axis=-1, keepdims=True))
        gx_ref[...] = dy_ref[...] + dx

        @pl.when(i == nt - 1)
        def _():
            for cp in _chip_copies(ins, outs, send_sems, recv_sems, local_sems):
                cp.wait()

    full = pl.BlockSpec((tm, D_MODEL), lambda i: (i, 0))
    vec = pl.BlockSpec((1, D_MODEL), lambda i: (0, 0))
    any_spec = pl.BlockSpec(memory_space=pl.ANY)
    res = pl.pallas_call(
        body, name="dh_dx", grid=(nt,),
        in_specs=[pl.BlockSpec((tm, p.shape[1]), lambda i: (i, 0)) for p in pieces]
        + [pl.BlockSpec((N_ALL, D_MODEL), lambda i: (0, 0)), full, full,
           pl.BlockSpec((3, D_MODEL), lambda i: (0, 0)), vec] + [any_spec] * n,
        out_specs=[full, vec, vec, vec] + [any_spec] * n,
        out_shape=[jax.ShapeDtypeStruct((s, D_MODEL), F32)] + [jax.ShapeDtypeStruct((1, D_MODEL), F32)] * 3
        + [jax.ShapeDtypeStruct(a.shape, a.dtype) for a in chip_sums],
        scratch_shapes=[pltpu.SemaphoreType.DMA((n * 3,)), pltpu.SemaphoreType.DMA((n * 3,)),
                        pltpu.SemaphoreType.DMA((n,))],
        compiler_params=_params(("arbitrary",)),
    )(*pieces, w_all_t, x, dy, ada3, norm_g, *chip_sums)
    return res[:4], res[4:]


def _sum_small(vec_all, qg_parts, kg_parts):
    def body(v_ref, q_ref, k_ref, tot_ref, gq_ref, gk_ref):
        tot = v_ref[0:1, :]
        for p in range(1, N_DEV):
            tot = tot + v_ref[p:p + 1, :]
        tot_ref[...] = tot
        gq_ref[...] = jnp.sum(q_ref[...], axis=0, keepdims=True)
        gk_ref[...] = jnp.sum(k_ref[...], axis=0, keepdims=True)

    n = vec_all.shape[-1]
    return pl.pallas_call(
        body, name="sum_small",
        out_shape=[jax.ShapeDtypeStruct((1, n), F32),
                   jax.ShapeDtypeStruct((1, HEAD_DIM), F32), jax.ShapeDtypeStruct((1, HEAD_DIM), F32)],
        compiler_params=_params(),
    )(vec_all, qg_parts, kg_parts)


def _grad_w_ada(c_cols, dada_rows):
    def body(c_ref, d_ref, out_ref):
        acc = c_ref[0] * d_ref[0]
        for b in range(1, N_DEV):
            acc = acc + c_ref[b] * d_ref[b]
        out_ref[...] = acc

    return pl.pallas_call(
        body, name="grad_w_ada",
        out_shape=jax.ShapeDtypeStruct((D_MODEL, ADA_SHARD), F32),
        compiler_params=_params(),
    )(c_cols, dada_rows)


def _adamw(w, m, v, g_parts, name):
    rows, cols = w.shape
    n_parts = g_parts.shape[0]
    tr = 256 if rows % 256 == 0 else rows
    tc = 256 if (tr == rows and rows > 256 and cols % 256 == 0) else cols
    c1 = 1.0 / (1.0 - ADAM_B1 ** ADAM_STEP)
    c2 = 1.0 / (1.0 - ADAM_B2 ** ADAM_STEP)

    def body(w_ref, m_ref, v_ref, g_ref, go_ref, d_ref, mo_ref, vo_ref):
        g = g_ref[0].astype(F32)
        for p in range(1, n_parts):
            g = g + g_ref[p].astype(F32)
        m_new = ADAM_B1 * m_ref[...] + (1.0 - ADAM_B1) * g
        v_new = ADAM_B2 * v_ref[...] + (1.0 - ADAM_B2) * (g * g)
        go_ref[...] = g
        mo_ref[...] = m_new
        vo_ref[...] = v_new
        d_ref[...] = -ADAM_LR * ((m_new * c1) / (jnp.sqrt(v_new * c2) + ADAM_EPS) + ADAM_WD * w_ref[...])

    blk = pl.BlockSpec((tr, tc), lambda i, j: (i, j))
    return pl.pallas_call(
        body, name=name, grid=(rows // tr, cols // tc),
        in_specs=[blk, blk, blk, pl.BlockSpec((n_parts, tr, tc), lambda i, j: (0, i, j))],
        out_specs=[blk] * 4,
        out_shape=[jax.ShapeDtypeStruct((rows, cols), F32)] * 4,
        compiler_params=_params(("parallel", "parallel")),
    )(w, m, v, g_parts)


def _adamw_rows128(w, m, v, g_parts, name):
    rows, _ = w.shape
    n_parts = g_parts.shape[0]
    chunk = 64
    c1 = 1.0 / (1.0 - ADAM_B1 ** ADAM_STEP)
    c2 = 1.0 / (1.0 - ADAM_B2 ** ADAM_STEP)

    def body(w_ref, m_ref, v_ref, g_ref, go_ref, d_ref, mo_ref, vo_ref):
        def update(rs):
            g = g_ref[0, rs, :].astype(F32)
            for p in range(1, n_parts):
                g = g + g_ref[p, rs, :].astype(F32)
            m_new = ADAM_B1 * m_ref[rs, :] + (1.0 - ADAM_B1) * g
            v_new = ADAM_B2 * v_ref[rs, :] + (1.0 - ADAM_B2) * (g * g)
            go_ref[rs, :] = g
            mo_ref[rs, :] = m_new
            vo_ref[rs, :] = v_new
            d_ref[rs, :] = -ADAM_LR * ((m_new * c1) / (jnp.sqrt(v_new * c2) + ADAM_EPS) + ADAM_WD * w_ref[rs, :])

        def step(i, carry):
            update(pl.ds(pl.multiple_of(i * chunk, chunk), chunk))
            return carry
        lax.fori_loop(0, rows // chunk, step, 0)
        if rows % chunk:
            update(slice(rows - rows % chunk, rows))

    vm = pl.BlockSpec(memory_space=pltpu.VMEM)
    return pl.pallas_call(
        body, name=name, in_specs=[vm] * 4, out_specs=[vm] * 4,
        out_shape=[jax.ShapeDtypeStruct(w.shape, F32)] * 4,
        compiler_params=_params(),
    )(w, m, v, g_parts)


_O_F = 1536


def _to_internal(wt_g):
    def rows(lo, hi):
        out = []
        for p in range(N_DEV):
            a, b = max(lo, p * IN_SHARD), min(hi, (p + 1) * IN_SHARD)
            if a < b:
                out.append(wt_g[p, a - p * IN_SHARD:b - p * IN_SHARD])
        return out

    pad = jnp.zeros((N_FPAD - HEADS, D_MODEL), wt_g.dtype)
    return jnp.concatenate(rows(0, _O_F) + rows(_O_F + HEADS, IN_WIDTH) + rows(_O_F, _O_F + HEADS) + [pad], axis=0)


def _slabs_by_core(dwt, dwt_f):
    sources = ((dwt, 0, _O_F, 0), (dwt_f, _O_F, _O_F + HEADS, _O_F), (dwt, _O_F + HEADS, IN_WIDTH, HEADS))

    def slab(p):
        lo, hi = p * IN_SHARD, (p + 1) * IN_SHARD
        parts = []
        for src, o_lo, o_hi, shift in sources:
            a, b = max(lo, o_lo), min(hi, o_hi)
            if a < b:
                parts.append(src[a - shift:b - shift])
        return parts[0] if len(parts) == 1 else jnp.concatenate(parts, axis=0)

    return jnp.stack([jnp.stack([slab(2 * chip + core) for chip in range(4)]) for core in range(2)])


def kernel(x, c, w_ada, b_ada, norm_g, w_in, b_f, q_norm_g, k_norm_g, conv_w, w_attn_out, w_conv_out, w_o, loss_target, m_w_ada, m_b_ada, m_norm_g, m_w_in, m_b_f, m_q_norm_g, m_k_norm_g, m_conv_w, m_w_attn_out, m_w_conv_out, m_w_o, v_w_ada, v_b_ada, v_norm_g, v_w_in, v_b_f, v_q_norm_g, v_k_norm_g, v_conv_w, v_w_attn_out, v_w_conv_out, v_w_o):
    me = 4 * lax.axis_index("x") + 2 * lax.axis_index("y") + lax.axis_index("c")
    s = x.shape[1]
    x2, t2 = x[0], loss_target[0]

    cw_g, wa_g, wb_g, wo_g, w_in_g = _gather_two_level(
        [conv_w[0], w_attn_out[0].astype(BF16), w_conv_out[0].astype(BF16), w_o[0].astype(BF16),
         w_in[0].T.astype(BF16)], "gather_weights")
    c_all, ada_g = _ada_exchange(c, w_ada[0])
    ada_mine = lax.dynamic_index_in_dim(ada_g[:, :, 0, :], me, axis=1, keepdims=False)
    ada3 = (ada_mine.reshape(1, 3 * D_MODEL) + b_ada).reshape(3, D_MODEL)
    w_all_t = _to_internal(w_in_g)
    wa = jnp.transpose(wa_g, (1, 0, 2)).reshape(ATTN_W, D_MODEL)
    wb = jnp.transpose(wb_g, (1, 0, 2)).reshape(CONV_W, D_MODEL)
    wo = wo_g.reshape(D_MODEL, D_MODEL)
    cw = jnp.transpose(cw_g, (1, 0, 2)).reshape(3, CONV_W)
    qg = jnp.tile(q_norm_g, (1, HEADS))
    kg = jnp.tile(k_norm_g, (1, HEADS))
    bf_pad = jnp.pad(b_f, ((0, 0), (0, LANES - HEADS)))

    proj, fl, h = _proj_fwd(x2, ada3, norm_g, w_all_t)
    qa, ka, va, kt, vt = _qkv_prep(proj, fl, bf_pad, qg, kg)
    attn, oa, qb = _attn_fwd(qa, ka, vt, proj)
    ob = _conv_fwd(proj, cw)
    (dy, dgab, doa, dob, dwo, dwa, dwb, dgate, loss_part) = _tail(oa, ob, proj, x2, t2, ada3, wa, wb, wo)

    do, dza = _attn_bwd_prep(doa, attn, proj)
    dqt, dk, dv = _attn_bwd(qb, ka, kt, va, do)
    dqkv, dqg, dkg, dcum = _qk_norm_bwd(dqt, dk, dv, proj, qg, kg)
    df, dbf = _forget_bwd(dcum, fl, bf_pad)
    dcb, dcc, dcu, dcz, dcw = _conv_bwd(dob, proj, cw)
    pieces = [dqkv, dza, dcb, dcc, dcu, dcz, dgab, df]
    dw_main, dw_f = _dw_in(h, pieces)

    def by_core(slabs8):
        return jnp.swapaxes(slabs8.reshape((4, 2) + slabs8.shape[1:]), 0, 1).astype(BF16)

    slabs = [by_core(jnp.transpose(dwa.reshape(ATTN_W, N_DEV, LANES), (1, 0, 2))),
             by_core(jnp.transpose(dwb.reshape(CONV_W, N_DEV, LANES), (1, 0, 2))),
             by_core(dwo.reshape(N_DEV, D_MODEL // N_DEV, D_MODEL)),
             _slabs_by_core(dw_main, dw_f)]
    theirs = _sibling_swap(slabs, "swap_grads")
    core = lax.axis_index("c").astype(jnp.int32).reshape(1)
    chip_sums = [_pair_sum(m2, t4, core, "pair_sum_" + nm)
                 for m2, t4, nm in zip(slabs, theirs, ("wa", "wb", "wo", "w_in"))]
    (grad_x, dshift, dscale, dnormg), (g_wa_parts, g_wb_parts, g_wo_parts, g_in_parts) = _dh_and_dx(
        pieces, w_all_t, x2, dy, ada3, norm_g, chip_sums)
    vec = jnp.concatenate([dshift, dscale, dgate, dnormg, dbf, dcw.reshape(1, 3 * CONV_W), dqg, dkg], axis=1)
    (vec_all,) = _exchange([vec], True, "gather_small")
    vec_all = vec_all.reshape(N_DEV, vec.shape[1])
    n_main = 4 * D_MODEL + LANES + 3 * CONV_W
    tot, g_qg, g_kg = _sum_small(
        vec_all[:, :n_main],
        vec_all[:, n_main:n_main + ATTN_W].reshape(N_DEV * HEADS, HEAD_DIM),
        vec_all[:, n_main + ATTN_W:].reshape(N_DEV * HEADS, HEAD_DIM))
    g_b_ada = tot[:, 0:3 * D_MODEL]
    g_norm_g = tot[:, 3 * D_MODEL:4 * D_MODEL]
    g_b_f = tot[:, 4 * D_MODEL:4 * D_MODEL + HEADS]
    g_cw_full = tot[:, 4 * D_MODEL + LANES:].reshape(3, CONV_W)
    g_cw = lax.dynamic_slice(g_cw_full, (0, me * (CONV_W // N_DEV)), (3, CONV_W // N_DEV))
    dada_mine = lax.dynamic_slice(vec_all[:, 0:3 * D_MODEL], (0, me * ADA_SHARD), (N_DEV, ADA_SHARD))
    g_w_ada = _grad_w_ada(jnp.transpose(c_all, (0, 2, 1)), dada_mine.reshape(N_DEV, 1, ADA_SHARD))

    upd = {}
    upd["w_ada"] = _adamw(w_ada[0], m_w_ada[0], v_w_ada[0], g_w_ada[None], "adamw_w_ada")
    upd["b_ada"] = _adamw(b_ada, m_b_ada, v_b_ada, g_b_ada[None], "adamw_b_ada")
    upd["norm_g"] = _adamw(norm_g, m_norm_g, v_norm_g, g_norm_g[None], "adamw_norm_g")
    flat = lambda a: a.reshape(a.shape[:-2] + (IN_SHARD * D_MODEL // LANES, LANES))
    upd["w_in"] = [u.reshape(IN_SHARD, D_MODEL).T for u in _adamw_rows128(
        flat(w_in[0].T), flat(m_w_in[0].T), flat(v_w_in[0].T), flat(g_in_parts), "adamw_w_in")]
    upd["b_f"] = _adamw(b_f, m_b_f, v_b_f, g_b_f[None], "adamw_b_f")
    upd["q_norm_g"] = _adamw(q_norm_g, m_q_norm_g, v_q_norm_g, g_qg[None], "adamw_q_norm_g")
    upd["k_norm_g"] = _adamw(k_norm_g, m_k_norm_g, v_k_norm_g, g_kg[None], "adamw_k_norm_g")
    upd["conv_w"] = _adamw(conv_w[0], m_conv_w[0], v_conv_w[0], g_cw[None], "adamw_conv_w")
    upd["w_attn_out"] = _adamw(w_attn_out[0], m_w_attn_out[0], v_w_attn_out[0], g_wa_parts, "adamw_w_attn_out")
    upd["w_conv_out"] = _adamw(w_conv_out[0], m_w_conv_out[0], v_w_conv_out[0], g_wb_parts, "adamw_w_conv_out")
    upd["w_o"] = _adamw(w_o[0], m_w_o[0], v_w_o[0], g_wo_parts, "adamw_w_o")

    names = ["w_ada", "b_ada", "norm_g", "w_in", "b_f", "q_norm_g", "k_norm_g", "conv_w",
             "w_attn_out", "w_conv_out", "w_o"]
    lead = {"w_ada", "w_in", "conv_w", "w_attn_out", "w_conv_out", "w_o"}
    fix = lambda n, a: a[None] if n in lead else a
    loss = lax.psum(loss_part[0, 0], ("x", "y", "c"))
    outs = [loss, grad_x[None]]
    for k in range(4):
        outs += [fix(n, upd[n][k]) for n in names]
    return tuple(outs)
```

```python
import functools

import numpy as np
import jax
import jax.numpy as jnp
from jax import lax
from jax.experimental import pallas as pl
from jax.experimental.pallas import tpu as pltpu

F32 = jnp.float32
BF16 = jnp.bfloat16

D_MODEL = 1024
HEADS = 8
HEAD_DIM = 64
ATTN_W = 512
CONV_W = 512
N_DEV = 8
IN_WIDTH = 6152
IN_SHARD = IN_WIDTH // N_DEV
N_MAIN = 6144
N_FPAD = 128
N_ALL = N_MAIN + N_FPAD
ADA_SHARD = 3 * D_MODEL // N_DEV
EPS = 1e-6
NEG = -1e30

ADAM_LR = 0.001
ADAM_B1 = 0.9
ADAM_B2 = 0.999
ADAM_EPS = 1e-08
ADAM_WD = 0.01
ADAM_STEP = 10

LANES = 128
VMEM_LIMIT = 56 * 1024 * 1024

TM_PROJ = 256
TN_PROJ = 1024
TM_ELEM = 512
TQ = 512
HEADS_PER_STEP = 4
HEADS_PER_STEP_FWD = 8
TM_TAIL = 256
TC_CUM = 256
TK_DW = 1024
TN_DW = 512
TM_DH = 256
HALO = 16

OFF_Q, OFF_K, OFF_V, OFF_ZA, OFF_CB, OFF_CC, OFF_CU, OFF_CZ, OFF_GA, OFF_GB = (
    0, 512, 1024, 1536, 2048, 2560, 3072, 3584, 4096, 5120)


def _params(sem=None):
    return pltpu.CompilerParams(dimension_semantics=sem, vmem_limit_bytes=VMEM_LIMIT)


def _dot(a, b):
    return jnp.dot(a, b, preferred_element_type=F32)


def _dot_nt(a, b):
    return lax.dot_general(a, b, (((1,), (1,)), ((), ())), preferred_element_type=F32)


def _dot_tn(a, b):
    return lax.dot_general(a, b, (((0,), (0,)), ((), ())), preferred_element_type=F32)


def _sigmoid(x):
    return 1.0 / (1.0 + jnp.exp(-x))


def _lane_lo(shape):
    return lax.broadcasted_iota(jnp.int32, shape, len(shape) - 1) < HEAD_DIM


def _seg_sum(z, lo):
    a = jnp.sum(jnp.where(lo, z, 0.0), axis=-1, keepdims=True)
    b = jnp.sum(jnp.where(lo, 0.0, z), axis=-1, keepdims=True)
    return jnp.where(lo, a, b)


def _lane_col(z, lane):
    idx = lax.broadcasted_iota(jnp.int32, z.shape, 1)
    return jnp.sum(jnp.where(idx == lane, z, 0.0), axis=-1, keepdims=True)


def _sub_row(z, row):
    idx = lax.broadcasted_iota(jnp.int32, z.shape, 0)
    return jnp.sum(jnp.where(idx == row, z, 0.0), axis=0, keepdims=True)


def _mesh_pos():
    x, y, c = lax.axis_index("x"), lax.axis_index("y"), lax.axis_index("c")
    return x, y, c, 4 * x + 2 * y + c


def _peer(k, x, y, c):
    px = 1 - x if (k >> 2) & 1 else x
    py = 1 - y if (k >> 1) & 1 else y
    pc = 1 - c if k & 1 else c
    return (px, py, pc), 4 * px + 2 * py + pc


def _exchange(arrs, gather, name):
    n = len(arrs)
    any_spec = pl.BlockSpec(memory_space=pl.ANY)

    def body(*refs):
        ins, outs = refs[:n], refs[n:2 * n]
        send_sems, recv_sems, local_sems = refs[2 * n:]
        x, y, c, me = _mesh_pos()
        copies = []
        for a in range(n):
            own = ins[a] if gather else ins[a].at[me]
            local = pltpu.make_async_copy(own, outs[a].at[me], local_sems.at[a])
            local.start()
            copies.append(local)
            for k in range(1, N_DEV):
                dev, p = _peer(k, x, y, c)
                cp = pltpu.make_async_remote_copy(
                    src_ref=ins[a] if gather else ins[a].at[p],
                    dst_ref=outs[a].at[me],
                    send_sem=send_sems.at[a * (N_DEV - 1) + k - 1],
                    recv_sem=recv_sems.at[a * (N_DEV - 1) + k - 1],
                    device_id=dev, device_id_type=pl.DeviceIdType.MESH)
                cp.start()
                copies.append(cp)
        for cp in copies:
            cp.wait()

    out_shape = [jax.ShapeDtypeStruct((N_DEV,) + a.shape if gather else a.shape, a.dtype) for a in arrs]
    return pl.pallas_call(
        body, name=name, out_shape=out_shape,
        in_specs=[any_spec] * n, out_specs=[any_spec] * n,
        scratch_shapes=[pltpu.SemaphoreType.DMA((n * (N_DEV - 1),)),
                        pltpu.SemaphoreType.DMA((n * (N_DEV - 1),)),
                        pltpu.SemaphoreType.DMA((n,))],
    )(*arrs)


def _gather_two_level(arrs, name):
    n = len(arrs)
    any_spec = pl.BlockSpec(memory_space=pl.ANY)
    per = N_DEV - 1

    def body(*refs):
        ins, outs = refs[:n], refs[n:2 * n]
        send_sems, recv_sems, local_sems = refs[2 * n:]
        x, y, c, me = _mesh_pos()
        sibling = (x, y, 1 - c)
        chips = [(1 - x, y), (x, 1 - y), (1 - x, 1 - y)]

        def copy(a, k, src, blk, to):
            return pltpu.make_async_remote_copy(
                src_ref=src, dst_ref=outs[a].at[blk],
                send_sem=send_sems.at[a * per + k], recv_sem=recv_sems.at[a * per + k],
                device_id=to, device_id_type=pl.DeviceIdType.MESH)

        local = [pltpu.make_async_copy(ins[a], outs[a].at[me], local_sems.at[a]) for a in range(n)]
        for cp in local:
            cp.start()
        first = []
        for a in range(n):
            first.append(copy(a, 0, ins[a], me, sibling))
            first += [copy(a, 1 + j, ins[a], me, (px, py, c)) for j, (px, py) in enumerate(chips)]
        for cp in first:
            cp.start()
        passed = []
        for j, (px, py) in enumerate(chips):
            blk = 4 * px + 2 * py + c
            for a in range(n):
                copy(a, 1 + j, ins[a], blk, (x, y, c)).wait_recv()
                fwd = copy(a, 4 + j, outs[a].at[blk], blk, sibling)
                fwd.start()
                passed.append(fwd)
        for a in range(n):
            copy(a, 0, ins[a], 4 * x + 2 * y + 1 - c, (x, y, c)).wait_recv()
            for j, (px, py) in enumerate(chips):
                copy(a, 4 + j, ins[a], 4 * px + 2 * py + 1 - c, (x, y, c)).wait_recv()
        for cp in first + passed:
            cp.wait_send()
        for cp in local:
            cp.wait()

    return pl.pallas_call(
        body, name=name,
        out_shape=[jax.ShapeDtypeStruct((N_DEV,) + a.shape, a.dtype) for a in arrs],
        in_specs=[any_spec] * n, out_specs=[any_spec] * n,
        scratch_shapes=[pltpu.SemaphoreType.DMA((n * per,)), pltpu.SemaphoreType.DMA((n * per,)),
                        pltpu.SemaphoreType.DMA((n,))],
    )(*arrs)


def _sibling_swap(arrs, name):
    n = len(arrs)
    any_spec = pl.BlockSpec(memory_space=pl.ANY)

    def body(*refs):
        ins, outs = refs[:n], refs[n:2 * n]
        send_sems, recv_sems = refs[2 * n:]
        x, y, c, _ = _mesh_pos()
        copies = [pltpu.make_async_remote_copy(
            src_ref=ins[a].at[1 - c], dst_ref=outs[a], send_sem=send_sems.at[a], recv_sem=recv_sems.at[a],
            device_id=(x, y, 1 - c), device_id_type=pl.DeviceIdType.MESH) for a in range(n)]
        for cp in copies:
            cp.start()
        for cp in copies:
            cp.wait()

    return pl.pallas_call(
        body, name=name,
        out_shape=[jax.ShapeDtypeStruct(a.shape[1:], a.dtype) for a in arrs],
        in_specs=[any_spec] * n, out_specs=[any_spec] * n,
        scratch_shapes=[pltpu.SemaphoreType.DMA((n,)), pltpu.SemaphoreType.DMA((n,))],
    )(*arrs)


def _pair_sum(mine2, theirs, core, name):
    _, _, rows, cols = mine2.shape
    tr = 256 if rows % 256 == 0 else rows

    def body(core_ref, a_ref, b_ref, out_ref):
        out_ref[...] = (a_ref[...].astype(F32) + b_ref[...].astype(F32)).astype(BF16)

    return pl.pallas_call(
        body, name=name,
        grid_spec=pltpu.PrefetchScalarGridSpec(
            num_scalar_prefetch=1, grid=(4, rows // tr),
            in_specs=[pl.BlockSpec((None, None, tr, cols), lambda ch, i, core_: (core_[0], ch, i, 0)),
                      pl.BlockSpec((None, tr, cols), lambda ch, i, core_: (ch, i, 0))],
            out_specs=pl.BlockSpec((None, tr, cols), lambda ch, i, core_: (ch, i, 0))),
        out_shape=jax.ShapeDtypeStruct(theirs.shape, BF16),
        compiler_params=_params(("parallel", "parallel")),
    )(core, mine2, theirs)


def _chip_copies(ins, outs, send_sems, recv_sems, local_sems):
    x, y, c, _ = _mesh_pos()
    my_chip = 2 * x + y
    chips = [(1 - x, y), (x, 1 - y), (1 - x, 1 - y)]
    copies = []
    for a in range(len(ins)):
        copies.append(pltpu.make_async_copy(ins[a].at[my_chip], outs[a].at[my_chip], local_sems.at[a]))
        for j, (px, py) in enumerate(chips):
            copies.append(pltpu.make_async_remote_copy(
                src_ref=ins[a].at[2 * px + py], dst_ref=outs[a].at[my_chip],
                send_sem=send_sems.at[a * 3 + j], recv_sem=recv_sems.at[a * 3 + j],
                device_id=(px, py, c), device_id_type=pl.DeviceIdType.MESH))
    return copies


def _ada_exchange(c_row, w_ada_sh):
    def body(c_ref, w_ref, call_ref, adag_ref, mine_ref, send_sems, recv_sems):
        x, y, c, me = _mesh_pos()

        def copy(phase, k, src, dst):
            dev, _ = _peer(k, x, y, c)
            return pltpu.make_async_remote_copy(
                src_ref=src, dst_ref=dst,
                send_sem=send_sems.at[phase * (N_DEV - 1) + k - 1],
                recv_sem=recv_sems.at[phase * (N_DEV - 1) + k - 1],
                device_id=dev, device_id_type=pl.DeviceIdType.MESH)

        call_ref[me] = c_ref[...]
        first = [copy(0, k, c_ref, call_ref.at[me]) for k in range(1, N_DEV)]
        for cp in first:
            cp.start()
        for cp in first:
            cp.wait()
        wb = w_ref[...].astype(BF16)
        for b in range(N_DEV):
            row = jnp.broadcast_to(call_ref[b], (8, D_MODEL)).astype(BF16)
            mine_ref[b] = _sub_row(_dot(row, wb), 0)
        adag_ref[me] = mine_ref[...]
        second = [copy(1, k, mine_ref, adag_ref.at[me]) for k in range(1, N_DEV)]
        for cp in second:
            cp.start()
        for cp in second:
            cp.wait()

    vm = pl.BlockSpec(memory_space=pltpu.VMEM)
    return pl.pallas_call(
        body, name="ada_exchange",
        out_shape=[jax.ShapeDtypeStruct((N_DEV, 1, D_MODEL), F32),
                   jax.ShapeDtypeStruct((N_DEV, N_DEV, 1, ADA_SHARD), F32)],
        in_specs=[vm, vm], out_specs=[vm, vm],
        scratch_shapes=[pltpu.VMEM((N_DEV, 1, ADA_SHARD), F32),
                        pltpu.SemaphoreType.DMA((2 * (N_DEV - 1),)),
                        pltpu.SemaphoreType.DMA((2 * (N_DEV - 1),))],
        compiler_params=pltpu.CompilerParams(vmem_limit_bytes=VMEM_LIMIT),
    )(c_row, w_ada_sh)


def _proj_fwd(x, ada3, norm_g, w_all_t):
    s = x.shape[0]
    tm, tn = min(TM_PROJ, s), TN_PROJ

    def body(x_ref, ada_ref, g_ref, wt_ref, proj_ref, fl_ref, h_ref):
        xv = x_ref[...]
        r = lax.rsqrt(jnp.mean(xv * xv, axis=-1, keepdims=True) + EPS)
        hv = ((xv * r) * g_ref[...]) * (1.0 + ada_ref[1:2, :]) + ada_ref[0:1, :]
        hb = hv.astype(BF16)
        h_ref[...] = hb
        fl_ref[...] = _dot_nt(hb, wt_ref[N_MAIN:N_ALL, :])
        for j in range(N_MAIN // tn):
            proj_ref[:, j * tn:(j + 1) * tn] = _dot_nt(hb, wt_ref[j * tn:(j + 1) * tn, :]).astype(BF16)

    return pl.pallas_call(
        body, name="proj_fwd", grid=(s // tm,),
        in_specs=[pl.BlockSpec((tm, D_MODEL), lambda i: (i, 0)),
                  pl.BlockSpec((3, D_MODEL), lambda i: (0, 0)),
                  pl.BlockSpec((1, D_MODEL), lambda i: (0, 0)),
                  pl.BlockSpec((N_ALL, D_MODEL), lambda i: (0, 0))],
        out_specs=[pl.BlockSpec((tm, N_MAIN), lambda i: (i, 0)),
                   pl.BlockSpec((tm, N_FPAD), lambda i: (i, 0)),
                   pl.BlockSpec((tm, D_MODEL), lambda i: (i, 0))],
        out_shape=[jax.ShapeDtypeStruct((s, N_MAIN), BF16),
                   jax.ShapeDtypeStruct((s, N_FPAD), F32),
                   jax.ShapeDtypeStruct((s, D_MODEL), BF16)],
        compiler_params=_params(("parallel",)),
    )(x, ada3, norm_g, w_all_t)


L_ONE_Q, L_F_Q, L_LSE_Q, L_END = HEAD_DIM, HEAD_DIM + 3, HEAD_DIM + 6, HEAD_DIM + 9


def _split3(f):
    hi = f.astype(BF16).astype(F32)
    r = f - hi
    mid = r.astype(BF16).astype(F32)
    return hi, mid, r - mid


def _place3(lane, first, parts, otherwise):
    a, b, c = parts
    return jnp.where(lane == first, a, jnp.where(lane == first + 1, b, jnp.where(lane == first + 2, c, otherwise)))


def _log_forget(fl, bf):
    z = fl + bf
    lf = jnp.minimum(z, 0.0) - jnp.log1p(jnp.exp(-jnp.abs(z)))
    lane = lax.broadcasted_iota(jnp.int32, z.shape, 1)
    return jnp.where(lane < HEADS, lf, 0.0)


def _qkv_prep(proj, fl, bf_pad, qg, kg):
    s = proj.shape[0]
    tm = min(TM_ELEM, s)
    scale = HEAD_DIM ** -0.5

    def body(p_ref, fl_ref, bf_ref, qg_ref, kg_ref, qa_ref, ka_ref, va_ref, kt_ref, vt_ref, carry):
        @pl.when(pl.program_id(0) == 0)
        def _():
            carry[...] = jnp.zeros_like(carry)
        tri = (lax.broadcasted_iota(jnp.int32, (tm, tm), 1) <= lax.broadcasted_iota(jnp.int32, (tm, tm), 0)).astype(F32)
        cum_v = jnp.dot(tri, _log_forget(fl_ref[...], bf_ref[...]), preferred_element_type=F32,
                        precision=lax.Precision.HIGHEST) + carry[...]
        carry[...] = _sub_row(cum_v, tm - 1)
        lane = lax.broadcasted_iota(jnp.int32, (tm, LANES), 1)
        lo = lane < HEAD_DIM
        v_tail = jnp.where(lane < L_F_Q, 1.0, 0.0)
        for pr in range(ATTN_W // LANES):
            sl = slice(pr * LANES, (pr + 1) * LANES)
            q2 = p_ref[:, OFF_Q + pr * LANES:OFF_Q + (pr + 1) * LANES].astype(F32)
            k2 = p_ref[:, OFF_K + pr * LANES:OFF_K + (pr + 1) * LANES].astype(F32)
            v2 = p_ref[:, OFF_V + pr * LANES:OFF_V + (pr + 1) * LANES].astype(F32)
            rq = lax.rsqrt(_seg_sum(q2 * q2, lo) * (1.0 / HEAD_DIM) + EPS)
            rk = lax.rsqrt(_seg_sum(k2 * k2, lo) * (1.0 / HEAD_DIM) + EPS)
            qn = ((q2 * rq) * qg_ref[:, sl]) * scale
            kn = (k2 * rk) * kg_ref[:, sl]
            for hh in range(2):
                h = 2 * pr + hh
                f3 = _split3(_lane_col(cum_v, h))
                qh = qn if hh == 0 else pltpu.roll(qn, HEAD_DIM, 1)
                kh = kn if hh == 0 else pltpu.roll(kn, HEAD_DIM, 1)
                vh = v2 if hh == 0 else pltpu.roll(v2, HEAD_DIM, 1)
                q_tail = jnp.where(lane < L_F_Q, 1.0, _place3(lane, L_F_Q, f3, 0.0))
                k_tail = _place3(lane, L_ONE_Q, tuple(-f for f in f3), jnp.where(lane < L_END, 1.0, 0.0))
                k_row = jnp.where(lo, kh, k_tail)
                v_row = jnp.where(lo, vh, v_tail)
                qa_ref[h] = jnp.where(lo, qh, q_tail).astype(BF16)
                ka_ref[h] = k_row.astype(BF16)
                va_ref[h] = v_row.astype(BF16)
                kt_ref[h] = k_row.T.astype(BF16)
                vt_ref[h] = v_row.T.astype(BF16)

    heads = pl.BlockSpec((HEADS, tm, LANES), lambda i: (0, i, 0))
    heads_t = pl.BlockSpec((HEADS, LANES, tm), lambda i: (0, 0, i))
    vec = pl.BlockSpec((1, ATTN_W), lambda i: (0, 0))
    return pl.pallas_call(
        body, name="qkv_prep", grid=(s // tm,),
        in_specs=[pl.BlockSpec((tm, 3 * ATTN_W), lambda i: (i, 0)),
                  pl.BlockSpec((tm, LANES), lambda i: (i, 0)),
                  pl.BlockSpec((1, LANES), lambda i: (0, 0)), vec, vec],
        out_specs=[heads, heads, heads, heads_t, heads_t],
        out_shape=[jax.ShapeDtypeStruct((HEADS, s, LANES), BF16)] * 3
        + [jax.ShapeDtypeStruct((HEADS, LANES, s), BF16)] * 2,
        scratch_shapes=[pltpu.VMEM((1, LANES), F32)],
        compiler_params=_params(("arbitrary",)),
    )(proj, fl, bf_pad, qg, kg)


def _causal_t(t):
    return lax.broadcasted_iota(jnp.int32, (t, t), 0) <= lax.broadcasted_iota(jnp.int32, (t, t), 1)


def _tri_steps(nt, q_major):
    if q_major:
        pairs = [(i, j) for i in range(nt) for j in range(i + 1)]
    else:
        pairs = [(i, j) for j in range(nt) for i in range(j, nt)]
    return (jnp.asarray(np.array([p[0] for p in pairs], np.int32)),
            jnp.asarray(np.array([p[1] for p in pairs], np.int32)))


def _attn_fwd(qa, ka, vt, proj):
    s = qa.shape[1]
    t = min(TQ, s)
    it, jt = _tri_steps(s // t, True)
    hp = HEADS_PER_STEP_FWD
    wide = hp * HEAD_DIM
    za_blk = OFF_ZA // wide

    def body(it_ref, jt_ref, q_ref, k_ref, vt_ref, za_ref, attn_ref, oa_ref, qb_ref, m_s, acc_s, pair_s):
        step = pl.program_id(1)
        i, j = it_ref[step], jt_ref[step]

        @pl.when(j == 0)
        def _():
            m_s[...] = jnp.full_like(m_s, NEG)
            acc_s[...] = jnp.zeros_like(acc_s)

        def update(masked):
            for hh in range(hp):
                st = _dot_nt(k_ref[hh], q_ref[hh])
                if masked:
                    st = jnp.where(_causal_t(t), st, NEG)
                m_prev = m_s[hh]
                m_next = jnp.maximum(m_prev, jnp.max(st, axis=0, keepdims=True))
                alpha = jnp.exp(m_prev - m_next)
                pt = jnp.exp(st - m_next).astype(BF16)
                acc_s[hh] = acc_s[hh] * alpha + _dot(vt_ref[hh], pt)
                m_s[hh] = m_next

        @pl.when(j < i)
        def _():
            update(False)

        @pl.when(j == i)
        def _():
            update(True)
            row = lax.broadcasted_iota(jnp.int32, (LANES, t), 0)
            lane = lax.broadcasted_iota(jnp.int32, (t, LANES), 1)
            for hh in range(hp):
                l_row = acc_s[hh, L_ONE_Q:L_ONE_Q + 1, :]
                pair_s[hh * HEAD_DIM:(hh + 1) * HEAD_DIM, :] = acc_s[hh, 0:HEAD_DIM, :] / l_row
                lse3 = _split3(m_s[hh] + jnp.log(l_row))
                tail_t = _place3(row, L_LSE_Q, tuple(-x for x in lse3), 0.0)
                keep_q = jnp.logical_or(lane < L_LSE_Q, lane >= L_END)
                qb_ref[hh] = jnp.where(keep_q, q_ref[hh].astype(F32), tail_t.T).astype(BF16)
            out = pair_s[...].T
            attn_ref[...] = out
            z = za_ref[...].astype(F32)
            oa_ref[...] = (out * (z * _sigmoid(z))).astype(BF16)

    pair_q = pl.BlockSpec((hp, t, LANES), lambda p, n, it_, jt_: (p, it_[n], 0))
    pair_k = pl.BlockSpec((hp, t, LANES), lambda p, n, it_, jt_: (p, jt_[n], 0))
    pair_kt = pl.BlockSpec((hp, LANES, t), lambda p, n, it_, jt_: (p, 0, jt_[n]))
    out_q = pl.BlockSpec((t, wide), lambda p, n, it_, jt_: (it_[n], p))
    return pl.pallas_call(
        body, name="attn_fwd",
        grid_spec=pltpu.PrefetchScalarGridSpec(
            num_scalar_prefetch=2, grid=(HEADS // hp, it.shape[0]),
            in_specs=[pair_q, pair_k, pair_kt,
                      pl.BlockSpec((t, wide), lambda p, n, it_, jt_: (it_[n], za_blk + p))],
            out_specs=[out_q, out_q, pair_q],
            scratch_shapes=[pltpu.VMEM((hp, 1, t), F32), pltpu.VMEM((hp, LANES, t), F32),
                            pltpu.VMEM((wide, t), F32)]),
        out_shape=[jax.ShapeDtypeStruct((s, ATTN_W), F32),
                   jax.ShapeDtypeStruct((s, ATTN_W), BF16),
                   jax.ShapeDtypeStruct((HEADS, s, LANES), BF16)],
        compiler_params=_params(("parallel", "arbitrary")),
    )(it, jt, qa, ka, vt, proj)


def _conv_parts(gb_ref, gc_ref, u_ref, zb_ref, gch_ref, uh_ref, first, w_ref, tm):
    gb, gc = gb_ref[...].astype(F32), gc_ref[...].astype(F32)
    u, zb = u_ref[...].astype(F32), zb_ref[...].astype(F32)
    cu = gc * u
    cu_h = jnp.where(first, 0.0, gch_ref[...].astype(F32) * uh_ref[...].astype(F32))
    prev1, prev2 = _sub_row(cu_h, HALO - 1), _sub_row(cu_h, HALO - 2)
    row = lax.broadcasted_iota(jnp.int32, (tm, LANES), 0)
    r1 = jnp.where(row == 0, prev1, pltpu.roll(cu, 1, 0))
    r2 = jnp.where(row == 0, prev2, jnp.where(row == 1, prev1, pltpu.roll(cu, 2, 0)))
    conv = w_ref[2:3, :] * cu + w_ref[1:2, :] * r1 + w_ref[0:1, :] * r2
    return gb, gc, u, zb, cu, r1, r2, conv


def _conv_specs(tm, s):
    def tile(off):
        return pl.BlockSpec((tm, LANES), lambda c, i: (i, off // LANES + c))

    def before(off):
        return pl.BlockSpec((HALO, LANES), lambda c, i: (jnp.maximum(i * (tm // HALO) - 1, 0), off // LANES + c))

    def after(off):
        return pl.BlockSpec((HALO, LANES),
                            lambda c, i: (jnp.minimum((i + 1) * (tm // HALO), s // HALO - 1), off // LANES + c))

    return ([tile(OFF_CB), tile(OFF_CC), tile(OFF_CU), tile(OFF_CZ)], [before(OFF_CC), before(OFF_CU)],
            [after(OFF_CB), after(OFF_CZ)])


def _conv_fwd(proj, conv_w):
    s = proj.shape[0]
    tm = min(TM_ELEM, s)
    tiles, befores, _ = _conv_specs(tm, s)

    def body(gb_ref, gc_ref, u_ref, zb_ref, gch_ref, uh_ref, w_ref, ob_ref):
        first = pl.program_id(1) == 0
        gb, _, _, zb, _, _, _, conv = _conv_parts(gb_ref, gc_ref, u_ref, zb_ref, gch_ref, uh_ref, first, w_ref, tm)
        ob_ref[...] = (gb * conv * (zb * _sigmoid(zb))).astype(BF16)

    return pl.pallas_call(
        body, name="conv_fwd", grid=(CONV_W // LANES, s // tm),
        in_specs=tiles + befores + [pl.BlockSpec((3, LANES), lambda c, i: (0, c))],
        out_specs=pl.BlockSpec((tm, LANES), lambda c, i: (i, c)),
        out_shape=jax.ShapeDtypeStruct((s, CONV_W), BF16),
        compiler_params=_params(("parallel", "parallel")),
    )(*([proj] * 6), conv_w)


def _tail(oa, ob, proj, x, target, ada3, wa, wb, wo):
    s = x.shape[0]
    tm = min(TM_TAIL, s)
    gab_blk = OFF_GA // (2 * D_MODEL)

    def body(oa_ref, ob_ref, gab_ref, x_ref, t_ref, ada_ref, wa_ref, wb_ref, wo_ref,
             dy_ref, dgab_ref, doa_ref, dob_ref, dwo_ref, dwa_ref, dwb_ref, dgate_ref, loss_ref):
        @pl.when(pl.program_id(0) == 0)
        def _():
            dwo_ref[...] = jnp.zeros_like(dwo_ref)
            dwa_ref[...] = jnp.zeros_like(dwa_ref)
            dwb_ref[...] = jnp.zeros_like(dwb_ref)
            dgate_ref[...] = jnp.zeros_like(dgate_ref)
            loss_ref[...] = jnp.zeros_like(loss_ref)

        oa_v, ob_v = oa_ref[...], ob_ref[...]
        wa_v, wb_v, wo_v = wa_ref[...], wb_ref[...], wo_ref[...]
        a2 = _dot(oa_v, wa_v)
        b2 = _dot(ob_v, wb_v)
        sa = _sigmoid(gab_ref[:, 0:D_MODEL].astype(F32))
        sb = _sigmoid(gab_ref[:, D_MODEL:2 * D_MODEL].astype(F32))
        mb = (sa * a2 + sb * b2).astype(BF16)
        mo = _dot(mb, wo_v)
        gate = ada_ref[2:3, :]
        err = (x_ref[...] + gate * mo) - t_ref[...]
        dy = err * (1.0 / D_MODEL)
        dy_ref[...] = dy
        loss_ref[...] += 0.5 * jnp.sum(err * err) * (1.0 / D_MODEL)
        dgate_ref[...] += jnp.sum(dy * mo, axis=0, keepdims=True)
        dmo = (dy * gate).astype(BF16)
        dmerged = _dot_nt(dmo, wo_v)
        dwo_ref[...] += _dot_tn(mb, dmo)
        da2 = (dmerged * sa).astype(BF16)
        db2 = (dmerged * sb).astype(BF16)
        dgab_ref[:, 0:D_MODEL] = (dmerged * a2 * (sa * (1.0 - sa))).astype(BF16)
        dgab_ref[:, D_MODEL:2 * D_MODEL] = (dmerged * b2 * (sb * (1.0 - sb))).astype(BF16)
        doa_ref[...] = _dot_nt(da2, wa_v)
        dob_ref[...] = _dot_nt(db2, wb_v)
        dwa_ref[...] += _dot_tn(oa_v, da2)
        dwb_ref[...] += _dot_tn(ob_v, db2)

    half = pl.BlockSpec((tm, ATTN_W), lambda i: (i, 0))
    full = pl.BlockSpec((tm, D_MODEL), lambda i: (i, 0))

    def const(shape):
        return pl.BlockSpec(shape, lambda i: (0, 0))

    return pl.pallas_call(
        body, name="tail", grid=(s // tm,),
        in_specs=[half, half, pl.BlockSpec((tm, 2 * D_MODEL), lambda i: (i, gab_blk)), full, full,
                  const((3, D_MODEL)), const((ATTN_W, D_MODEL)), const((CONV_W, D_MODEL)),
                  const((D_MODEL, D_MODEL))],
        out_specs=[full, pl.BlockSpec((tm, 2 * D_MODEL), lambda i: (i, 0)), half, half,
                   const((D_MODEL, D_MODEL)), const((ATTN_W, D_MODEL)), const((CONV_W, D_MODEL)),
                   const((1, D_MODEL)), const((1, LANES))],
        out_shape=[jax.ShapeDtypeStruct((s, D_MODEL), F32),
                   jax.ShapeDtypeStruct((s, 2 * D_MODEL), BF16),
                   jax.ShapeDtypeStruct((s, ATTN_W), F32),
                   jax.ShapeDtypeStruct((s, CONV_W), F32),
                   jax.ShapeDtypeStruct((D_MODEL, D_MODEL), F32),
                   jax.ShapeDtypeStruct((ATTN_W, D_MODEL), F32),
                   jax.ShapeDtypeStruct((CONV_W, D_MODEL), F32),
                   jax.ShapeDtypeStruct((1, D_MODEL), F32),
                   jax.ShapeDtypeStruct((1, LANES), F32)],
        compiler_params=_params(("arbitrary",)),
    )(oa, ob, proj, x, target, ada3, wa, wb, wo)


def _attn_bwd_prep(doa, attn, proj):
    s = doa.shape[0]
    tm = min(TM_ELEM, s)
    za_blk = OFF_ZA // ATTN_W

    def body(doa_ref, attn_ref, za_ref, do_ref, dza_ref):
        lane = lax.broadcasted_iota(jnp.int32, (tm, LANES), 1)
        lo = lane < HEAD_DIM
        for pr in range(ATTN_W // LANES):
            sl = slice(pr * LANES, (pr + 1) * LANES)
            g, a, z = doa_ref[:, sl], attn_ref[:, sl], za_ref[:, sl].astype(F32)
            sg = _sigmoid(z)
            dat = (g * (z * sg)).astype(BF16).astype(F32)
            prod = dat * a
            dza_ref[:, sl] = (g * a * (sg * (1.0 + z * (1.0 - sg)))).astype(BF16)
            for hh in range(2):
                sel = lo if hh == 0 else jnp.logical_not(lo)
                delta3 = _split3(jnp.sum(jnp.where(sel, prod, 0.0), axis=-1, keepdims=True))
                dh = dat if hh == 0 else pltpu.roll(dat, HEAD_DIM, 1)
                tail = _place3(lane, L_ONE_Q, tuple(-d for d in delta3), 0.0)
                do_ref[2 * pr + hh] = jnp.where(lo, dh, tail).astype(BF16)

    row = pl.BlockSpec((tm, ATTN_W), lambda i: (i, 0))
    return pl.pallas_call(
        body, name="attn_bwd_prep", grid=(s // tm,),
        in_specs=[row, row, pl.BlockSpec((tm, ATTN_W), lambda i: (i, za_blk))],
        out_specs=[pl.BlockSpec((HEADS, tm, LANES), lambda i: (0, i, 0)), row],
        out_shape=[jax.ShapeDtypeStruct((HEADS, s, LANES), BF16),
                   jax.ShapeDtypeStruct((s, ATTN_W), BF16)],
        compiler_params=_params(("parallel",)),
    )(doa, attn, proj)


def _attn_bwd(qb, ka, kt, va, do):
    s = qb.shape[1]
    t = min(TQ, s)
    nt = s // t
    hp = HEADS_PER_STEP
    it, jt = _tri_steps(nt, False)

    def body(it_ref, jt_ref, q_ref, k_ref, kt_ref, v_ref, do_ref, dq_ref, dk_ref, dv_ref, dqt_s):
        step = pl.program_id(1)
        i, j = it_ref[step], jt_ref[step]

        @pl.when(step == 0)
        def _():
            dqt_s[...] = jnp.zeros_like(dqt_s)

        @pl.when(i == j)
        def _():
            dk_ref[...] = jnp.zeros_like(dk_ref)
            dv_ref[...] = jnp.zeros_like(dv_ref)

        def update(masked):
            for hh in range(hp):
                qh, doh = q_ref[hh], do_ref[hh]
                st = _dot_nt(k_ref[hh], qh)
                if masked:
                    st = jnp.where(_causal_t(t), st, NEG)
                pt = jnp.exp(st)
                dst = (pt * _dot_nt(v_ref[hh], doh)).astype(BF16)
                dv_ref[hh] += _dot(pt.astype(BF16), doh)
                dk_ref[hh] += _dot(dst, qh)
                dqt_s[hh, i] += _dot(kt_ref[hh], dst)

        @pl.when(i > j)
        def _():
            update(False)

        @pl.when(i == j)
        def _():
            update(True)
            for hh in range(hp):
                dq_ref[hh] = dqt_s[hh, i].T

    pair_q = pl.BlockSpec((hp, t, LANES), lambda p, n, it_, jt_: (p, it_[n], 0))
    pair_k = pl.BlockSpec((hp, t, LANES), lambda p, n, it_, jt_: (p, jt_[n], 0))
    pair_kt = pl.BlockSpec((hp, LANES, t), lambda p, n, it_, jt_: (p, 0, jt_[n]))
    return pl.pallas_call(
        body, name="attn_bwd",
        grid_spec=pltpu.PrefetchScalarGridSpec(
            num_scalar_prefetch=2, grid=(HEADS // hp, it.shape[0]),
            in_specs=[pair_q, pair_k, pair_kt, pair_k, pair_q],
            out_specs=[pair_k, pair_k, pair_k],
            scratch_shapes=[pltpu.VMEM((hp, nt, LANES, t), F32)]),
        out_shape=[jax.ShapeDtypeStruct((HEADS, s, LANES), F32)] * 3,
        compiler_params=_params(("parallel", "arbitrary")),
    )(it, jt, qb, ka, kt, va, do)


def _forget_bwd(dcum, fl, bf_pad):
    s = fl.shape[0]
    tc = min(TC_CUM, s)
    n = s // tc

    def body(dc_ref, fl_ref, bf_ref, df_ref, dbf_ref, carry):
        @pl.when(pl.program_id(0) == 0)
        def _():
            carry[...] = jnp.zeros_like(carry)
            dbf_ref[...] = jnp.zeros_like(dbf_ref)
        r = lax.broadcasted_iota(jnp.int32, (tc, tc), 0)
        cidx = lax.broadcasted_iota(jnp.int32, (tc, tc), 1)
        tri = (cidx >= r).astype(F32)
        dc = dc_ref[...]
        dlf = jnp.dot(tri, dc, preferred_element_type=F32, precision=lax.Precision.HIGHEST) + carry[...]
        carry[...] += jnp.sum(dc, axis=0, keepdims=True)
        lane = lax.broadcasted_iota(jnp.int32, (tc, LANES), 1)
        dfl = jnp.where(lane < HEADS, dlf * _sigmoid(-(fl_ref[...] + bf_ref[...])), 0.0)
        df_ref[...] = dfl.astype(BF16)
        dbf_ref[...] += jnp.sum(dfl, axis=0, keepdims=True)

    rev = pl.BlockSpec((tc, LANES), lambda i: (n - 1 - i, 0))
    vec = pl.BlockSpec((1, LANES), lambda i: (0, 0))
    return pl.pallas_call(
        body, name="forget_bwd", grid=(n,),
        in_specs=[rev, rev, vec], out_specs=[rev, vec],
        out_shape=[jax.ShapeDtypeStruct((s, LANES), BF16), jax.ShapeDtypeStruct((1, LANES), F32)],
        scratch_shapes=[pltpu.VMEM((1, LANES), F32)],
        compiler_params=_params(("arbitrary",)),
    )(dcum, fl, bf_pad)


def _qk_norm_bwd(dq, dk, dv, proj, qg, kg):
    s = dv.shape[1]
    tm = min(TM_ELEM, s)
    scale = HEAD_DIM ** -0.5

    def body(dq_ref, dk_ref, dv_ref, p_ref, qg_ref, kg_ref, out_ref, dqg_ref, dkg_ref, dcum_ref):
        @pl.when(pl.program_id(0) == 0)
        def _():
            dqg_ref[...] = jnp.zeros_like(dqg_ref)
            dkg_ref[...] = jnp.zeros_like(dkg_ref)
        lane = lax.broadcasted_iota(jnp.int32, (tm, LANES), 1)
        lo = lane < HEAD_DIM
        dcum = jnp.zeros((tm, LANES), F32)
        for h in range(HEADS):
            dcum = jnp.where(lane == h, _lane_col(dq_ref[h], L_F_Q) - _lane_col(dk_ref[h], L_ONE_Q), dcum)
        dcum_ref[...] = dcum

        def pair(a, b):
            return jnp.where(lo, a, pltpu.roll(b, HEAD_DIM, 1))

        def one(raw, dy, g, dg_ref, sl, off):
            r = lax.rsqrt(_seg_sum(raw * raw, lo) * (1.0 / HEAD_DIM) + EPS)
            xhat = raw * r
            dg_ref[:, sl] += jnp.sum(dy * xhat, axis=0, keepdims=True)
            dxh = dy * g
            dx = r * (dxh - xhat * (_seg_sum(dxh * xhat, lo) * (1.0 / HEAD_DIM)))
            out_ref[:, off + sl.start:off + sl.stop] = dx.astype(BF16)

        for pr in range(ATTN_W // LANES):
            sl = slice(pr * LANES, (pr + 1) * LANES)
            dq2 = pair(dq_ref[2 * pr], dq_ref[2 * pr + 1])
            one(p_ref[:, OFF_Q + sl.start:OFF_Q + sl.stop].astype(F32), dq2 * scale, qg_ref[:, sl], dqg_ref, sl, OFF_Q)
            one(p_ref[:, OFF_K + sl.start:OFF_K + sl.stop].astype(F32), pair(dk_ref[2 * pr], dk_ref[2 * pr + 1]),
                kg_ref[:, sl], dkg_ref, sl, OFF_K)
            out_ref[:, OFF_V + sl.start:OFF_V + sl.stop] = pair(dv_ref[2 * pr], dv_ref[2 * pr + 1]).astype(BF16)

    heads = pl.BlockSpec((HEADS, tm, LANES), lambda i: (0, i, 0))
    vec = pl.BlockSpec((1, ATTN_W), lambda i: (0, 0))
    return pl.pallas_call(
        body, name="qk_norm_bwd", grid=(s // tm,),
        in_specs=[heads, heads, heads, pl.BlockSpec((tm, 2 * ATTN_W), lambda i: (i, 0)), vec, vec],
        out_specs=[pl.BlockSpec((tm, 3 * ATTN_W), lambda i: (i, 0)), vec, vec,
                   pl.BlockSpec((tm, LANES), lambda i: (i, 0))],
        out_shape=[jax.ShapeDtypeStruct((s, 3 * ATTN_W), BF16),
                   jax.ShapeDtypeStruct((1, ATTN_W), F32), jax.ShapeDtypeStruct((1, ATTN_W), F32),
                   jax.ShapeDtypeStruct((s, LANES), F32)],
        compiler_params=_params(("arbitrary",)),
    )(dq, dk, dv, proj, qg, kg)


def _conv_bwd(dob, proj, conv_w):
    s = dob.shape[0]
    tm = min(TM_ELEM, s)
    tiles, befores, afters = _conv_specs(tm, s)

    def body(dob_ref, dnext_ref, gb_ref, gc_ref, u_ref, zb_ref, gch_ref, uh_ref, gbn_ref, zbn_ref, w_ref,
             dgb_ref, dgc_ref, du_ref, dzb_ref, dw_ref):
        i = pl.program_id(1)

        @pl.when(i == 0)
        def _():
            dw_ref[...] = jnp.zeros_like(dw_ref)
        gb, gc, u, zb, cu, r1, r2, conv = _conv_parts(gb_ref, gc_ref, u_ref, zb_ref, gch_ref, uh_ref, i == 0, w_ref, tm)
        g = dob_ref[...]
        sg = _sigmoid(zb)
        sz = zb * sg
        dconv = g * gb * sz
        zn = zbn_ref[0:8, :].astype(F32)
        dcn = jnp.where(i == pl.num_programs(1) - 1, 0.0,
                        dnext_ref[...] * gbn_ref[0:8, :].astype(F32) * (zn * _sigmoid(zn)))
        nxt1, nxt2 = _sub_row(dcn, 0), _sub_row(dcn, 1)
        row = lax.broadcasted_iota(jnp.int32, (tm, LANES), 0)
        f1 = jnp.where(row == tm - 1, nxt1, pltpu.roll(dconv, tm - 1, 0))
        f2 = jnp.where(row == tm - 2, nxt1, jnp.where(row == tm - 1, nxt2, pltpu.roll(dconv, tm - 2, 0)))
        dcu = w_ref[2:3, :] * dconv + w_ref[1:2, :] * f1 + w_ref[0:1, :] * f2
        dgb_ref[...] = (g * conv * sz).astype(BF16)
        dgc_ref[...] = (dcu * u).astype(BF16)
        du_ref[...] = (dcu * gc).astype(BF16)
        dzb_ref[...] = (g * gb * conv * (sg * (1.0 + zb * (1.0 - sg)))).astype(BF16)
        w_row = lax.broadcasted_iota(jnp.int32, (3, LANES), 0)
        dw0 = jnp.sum(dconv * r2, axis=0, keepdims=True)
        dw1 = jnp.sum(dconv * r1, axis=0, keepdims=True)
        dw2 = jnp.sum(dconv * cu, axis=0, keepdims=True)
        dw_ref[...] += jnp.where(w_row == 0, dw0, jnp.where(w_row == 1, dw1, dw2))

    blk = pl.BlockSpec((tm, LANES), lambda c, i: (i, c))
    nxt = pl.BlockSpec((8, LANES), lambda c, i: (jnp.minimum((i + 1) * (tm // 8), s // 8 - 1), c))
    wspec = pl.BlockSpec((3, LANES), lambda c, i: (0, c))
    return pl.pallas_call(
        body, name="conv_bwd", grid=(CONV_W // LANES, s // tm),
        in_specs=[blk, nxt] + tiles + befores + afters + [wspec],
        out_specs=[blk, blk, blk, blk, wspec],
        out_shape=[jax.ShapeDtypeStruct((s, CONV_W), BF16)] * 4 + [jax.ShapeDtypeStruct((3, CONV_W), F32)],
        compiler_params=_params(("parallel", "arbitrary")),
    )(dob, dob, *([proj] * 8), conv_w)


def _piece_layout(pieces):
    offs, off = [], 0
    for p in pieces:
        offs.append((off, p.shape[1]))
        off += p.shape[1]
    assert off == N_ALL, off
    return offs


def _dw_in(h, pieces):
    s = h.shape[0]
    tk, tn = min(TK_DW, s), TN_DW
    nk = s // tk
    main, fpiece = pieces[:-1], pieces[-1]
    layout = _piece_layout(pieces)[:-1]
    n_main = len(main)

    def body(*refs):
        p_refs, f_ref, h_ref = refs[:n_main], refs[n_main], refs[n_main + 1]
        out_ref, outf_ref, acc, accf = refs[n_main + 2:]
        n, k = pl.program_id(0), pl.program_id(1)

        @pl.when(k == 0)
        def _():
            acc[...] = jnp.zeros_like(acc)
        hv = h_ref[pl.ds(pl.multiple_of(k * tk, tk), tk), :]
        for p_ref, (off, width) in zip(p_refs, layout):
            @pl.when(jnp.logical_and(n >= off // tn, n < (off + width) // tn))
            def _():
                acc[...] += _dot_tn(p_ref[...], hv)

        @pl.when(k == nk - 1)
        def _():
            out_ref[...] = acc[...].astype(BF16)

        @pl.when(n == 0)
        def _():
            @pl.when(k == 0)
            def _():
                accf[...] = jnp.zeros_like(accf)
            accf[...] += _dot_tn(f_ref[...], hv)

            @pl.when(k == nk - 1)
            def _():
                outf_ref[...] = accf[...].astype(BF16)

    def piece_spec(off, width):
        lo, hi = off // tn, (off + width) // tn

        def index(n, k):
            active = jnp.logical_and(n >= lo, n < hi)
            return jnp.where(active, k, 0), jnp.clip(n - lo, 0, hi - lo - 1)
        return pl.BlockSpec((tk, tn), index)

    return pl.pallas_call(
        body, name="dw_in", grid=(N_MAIN // tn, nk),
        in_specs=[piece_spec(off, width) for off, width in layout]
        + [pl.BlockSpec((tk, N_FPAD), lambda n, k: (jnp.where(n == 0, k, 0), 0)),
           pl.BlockSpec((s, D_MODEL), lambda n, k: (0, 0))],
        out_specs=[pl.BlockSpec((tn, D_MODEL), lambda n, k: (n, 0)),
                   pl.BlockSpec((N_FPAD, D_MODEL), lambda n, k: (0, 0))],
        out_shape=[jax.ShapeDtypeStruct((N_MAIN, D_MODEL), BF16), jax.ShapeDtypeStruct((N_FPAD, D_MODEL), BF16)],
        scratch_shapes=[pltpu.VMEM((tn, D_MODEL), F32), pltpu.VMEM((N_FPAD, D_MODEL), F32)],
        compiler_params=_params(("arbitrary", "arbitrary")),
    )(*main, fpiece, h)


def _dh_and_dx(pieces, w_all_t, x, dy, ada3, norm_g, chip_sums):
    s = x.shape[0]
    tm = min(TM_DH, s)
    nt = s // tm
    n = len(chip_sums)
    npc = len(pieces)
    layout = _piece_layout(pieces)

    def body(*refs):
        p_refs, refs = refs[:npc], refs[npc:]
        wt_ref, x_ref, dy_ref, ada_ref, g_ref = refs[:5]
        ins, refs = refs[5:5 + n], refs[5 + n:]
        gx_ref, dsh_ref, dsc_ref, dg_ref = refs[:4]
        outs, (send_sems, recv_sems, local_sems) = refs[4:4 + n], refs[4 + n:]
        i = pl.program_id(0)

        @pl.when(i == 0)
        def _():
            for cp in _chip_copies(ins, outs, send_sems, recv_sems, local_sems):
                cp.start()
            dsh_ref[...] = jnp.zeros_like(dsh_ref)
            dsc_ref[...] = jnp.zeros_like(dsc_ref)
            dg_ref[...] = jnp.zeros_like(dg_ref)

        dh = None
        for p_ref, (off, width) in zip(p_refs, layout):
            part = _dot(p_ref[...], wt_ref[off:off + width, :])
            dh = part if dh is None else dh + part
        xv = x_ref[...]
        r = lax.rsqrt(jnp.mean(xv * xv, axis=-1, keepdims=True) + EPS)
        xhat = xv * r
        g = g_ref[...]
        one_sc = 1.0 + ada_ref[1:2, :]
        dsh_ref[...] += jnp.sum(dh, axis=0, keepdims=True)
        dsc_ref[...] += jnp.sum(dh * (xhat * g), axis=0, keepdims=True)
        dg_ref[...] += jnp.sum(dh * xhat, axis=0, keepdims=True) * one_sc
        dxh = dh * (g * one_sc)
        dx = r * (dxh - xhat * jnp.mean(dxh * xhat, axis=-1, keepdims=True))
        gx_ref[...] = dy_ref[...] + dx

        @pl.when(i == nt - 1)
        def _():
            for cp in _chip_copies(ins, outs, send_sems, recv_sems, local_sems):
                cp.wait()

    full = pl.BlockSpec((tm, D_MODEL), lambda i: (i, 0))
    vec = pl.BlockSpec((1, D_MODEL), lambda i: (0, 0))
    any_spec = pl.BlockSpec(memory_space=pl.ANY)
    res = pl.pallas_call(
        body, name="dh_dx", grid=(nt,),
        in_specs=[pl.BlockSpec((tm, p.shape[1]), lambda i: (i, 0)) for p in pieces]
        + [pl.BlockSpec((N_ALL, D_MODEL), lambda i: (0, 0)), full, full,
           pl.BlockSpec((3, D_MODEL), lambda i: (0, 0)), vec] + [any_spec] * n,
        out_specs=[full, vec, vec, vec] + [any_spec] * n,
        out_shape=[jax.ShapeDtypeStruct((s, D_MODEL), F32)] + [jax.ShapeDtypeStruct((1, D_MODEL), F32)] * 3
        + [jax.ShapeDtypeStruct(a.shape, a.dtype) for a in chip_sums],
        scratch_shapes=[pltpu.SemaphoreType.DMA((n * 3,)), pltpu.SemaphoreType.DMA((n * 3,)),
                        pltpu.SemaphoreType.DMA((n,))],
        compiler_params=_params(("arbitrary",)),
    )(*pieces, w_all_t, x, dy, ada3, norm_g, *chip_sums)
    return res[:4], res[4:]


def _sum_small(vec_all, qg_parts, kg_parts):
    def body(v_ref, q_ref, k_ref, tot_ref, gq_ref, gk_ref):
        tot = v_ref[0:1, :]
        for p in range(1, N_DEV):
            tot = tot + v_ref[p:p + 1, :]
        tot_ref[...] = tot
        gq_ref[...] = jnp.sum(q_ref[...], axis=0, keepdims=True)
        gk_ref[...] = jnp.sum(k_ref[...], axis=0, keepdims=True)

    n = vec_all.shape[-1]
    return pl.pallas_call(
        body, name="sum_small",
        out_shape=[jax.ShapeDtypeStruct((1, n), F32),
                   jax.ShapeDtypeStruct((1, HEAD_DIM), F32), jax.ShapeDtypeStruct((1, HEAD_DIM), F32)],
        compiler_params=_params(),
    )(vec_all, qg_parts, kg_parts)


def _grad_w_ada(c_cols, dada_rows):
    def body(c_ref, d_ref, out_ref):
        acc = c_ref[0] * d_ref[0]
        for b in range(1, N_DEV):
            acc = acc + c_ref[b] * d_ref[b]
        out_ref[...] = acc

    return pl.pallas_call(
        body, name="grad_w_ada",
        out_shape=jax.ShapeDtypeStruct((D_MODEL, ADA_SHARD), F32),
        compiler_params=_params(),
    )(c_cols, dada_rows)


def _adamw(w, m, v, g_parts, name):
    rows, cols = w.shape
    n_parts = g_parts.shape[0]
    tr = 256 if rows % 256 == 0 else rows
    tc = 256 if (tr == rows and rows > 256 and cols % 256 == 0) else cols
    c1 = 1.0 / (1.0 - ADAM_B1 ** ADAM_STEP)
    c2 = 1.0 / (1.0 - ADAM_B2 ** ADAM_STEP)

    def body(w_ref, m_ref, v_ref, g_ref, go_ref, d_ref, mo_ref, vo_ref):
        g = g_ref[0].astype(F32)
        for p in range(1, n_parts):
            g = g + g_ref[p].astype(F32)
        m_new = ADAM_B1 * m_ref[...] + (1.0 - ADAM_B1) * g
        v_new = ADAM_B2 * v_ref[...] + (1.0 - ADAM_B2) * (g * g)
        go_ref[...] = g
        mo_ref[...] = m_new
        vo_ref[...] = v_new
        d_ref[...] = -ADAM_LR * ((m_new * c1) / (jnp.sqrt(v_new * c2) + ADAM_EPS) + ADAM_WD * w_ref[...])

    blk = pl.BlockSpec((tr, tc), lambda i, j: (i, j))
    return pl.pallas_call(
        body, name=name, grid=(rows // tr, cols // tc),
        in_specs=[blk, blk, blk, pl.BlockSpec((n_parts, tr, tc), lambda i, j: (0, i, j))],
        out_specs=[blk] * 4,
        out_shape=[jax.ShapeDtypeStruct((rows, cols), F32)] * 4,
        compiler_params=_params(("parallel", "parallel")),
    )(w, m, v, g_parts)


_O_F = 1536


def _to_internal(wt_g):
    wf = wt_g.reshape(IN_WIDTH, D_MODEL)
    f = jnp.pad(wf[_O_F:_O_F + HEADS], ((0, N_FPAD - HEADS), (0, 0)))
    return jnp.concatenate([wf[:_O_F], wf[_O_F + HEADS:], f], axis=0)


def _slabs_by_core(dwt, dwt_f):
    sources = ((dwt, 0, _O_F, 0), (dwt_f, _O_F, _O_F + HEADS, _O_F), (dwt, _O_F + HEADS, IN_WIDTH, HEADS))

    def slab(p):
        lo, hi = p * IN_SHARD, (p + 1) * IN_SHARD
        parts = []
        for src, o_lo, o_hi, shift in sources:
            a, b = max(lo, o_lo), min(hi, o_hi)
            if a < b:
                parts.append(src[a - shift:b - shift])
        return parts[0] if len(parts) == 1 else jnp.concatenate(parts, axis=0)

    return jnp.stack([jnp.stack([slab(2 * chip + core) for chip in range(4)]) for core in range(2)])


def kernel(x, c, w_ada, b_ada, norm_g, w_in, b_f, q_norm_g, k_norm_g, conv_w, w_attn_out, w_conv_out, w_o, loss_target, m_w_ada, m_b_ada, m_norm_g, m_w_in, m_b_f, m_q_norm_g, m_k_norm_g, m_conv_w, m_w_attn_out, m_w_conv_out, m_w_o, v_w_ada, v_b_ada, v_norm_g, v_w_in, v_b_f, v_q_norm_g, v_k_norm_g, v_conv_w, v_w_attn_out, v_w_conv_out, v_w_o):
    me = 4 * lax.axis_index("x") + 2 * lax.axis_index("y") + lax.axis_index("c")
    s = x.shape[1]
    x2, t2 = x[0], loss_target[0]

    cw_g, wa_g, wb_g, wo_g, w_in_g = _gather_two_level(
        [conv_w[0], w_attn_out[0].astype(BF16), w_conv_out[0].astype(BF16), w_o[0].astype(BF16),
         w_in[0].T.astype(BF16)], "gather_weights")
    c_all, ada_g = _ada_exchange(c, w_ada[0])
    ada_mine = lax.dynamic_index_in_dim(ada_g[:, :, 0, :], me, axis=1, keepdims=False)
    ada3 = (ada_mine.reshape(1, 3 * D_MODEL) + b_ada).reshape(3, D_MODEL)
    w_all_t = _to_internal(w_in_g)
    wa = jnp.transpose(wa_g, (1, 0, 2)).reshape(ATTN_W, D_MODEL)
    wb = jnp.transpose(wb_g, (1, 0, 2)).reshape(CONV_W, D_MODEL)
    wo = wo_g.reshape(D_MODEL, D_MODEL)
    cw = jnp.transpose(cw_g, (1, 0, 2)).reshape(3, CONV_W)
    qg = jnp.tile(q_norm_g, (1, HEADS))
    kg = jnp.tile(k_norm_g, (1, HEADS))
    bf_pad = jnp.pad(b_f, ((0, 0), (0, LANES - HEADS)))

    proj, fl, h = _proj_fwd(x2, ada3, norm_g, w_all_t)
    qa, ka, va, kt, vt = _qkv_prep(proj, fl, bf_pad, qg, kg)
    attn, oa, qb = _attn_fwd(qa, ka, vt, proj)
    ob = _conv_fwd(proj, cw)
    (dy, dgab, doa, dob, dwo, dwa, dwb, dgate, loss_part) = _tail(oa, ob, proj, x2, t2, ada3, wa, wb, wo)

    do, dza = _attn_bwd_prep(doa, attn, proj)
    dq, dk, dv = _attn_bwd(qb, ka, kt, va, do)
    dqkv, dqg, dkg, dcum = _qk_norm_bwd(dq, dk, dv, proj, qg, kg)
    df, dbf = _forget_bwd(dcum, fl, bf_pad)
    dcb, dcc, dcu, dcz, dcw = _conv_bwd(dob, proj, cw)
    pieces = [dqkv, dza, dcb, dcc, dcu, dcz, dgab, df]
    dw_main, dw_f = _dw_in(h, pieces)

    def by_core(slabs8):
        return jnp.swapaxes(slabs8.reshape((4, 2) + slabs8.shape[1:]), 0, 1).astype(BF16)

    slabs = [by_core(jnp.transpose(dwa.reshape(ATTN_W, N_DEV, LANES), (1, 0, 2))),
             by_core(jnp.transpose(dwb.reshape(CONV_W, N_DEV, LANES), (1, 0, 2))),
             by_core(dwo.reshape(N_DEV, D_MODEL // N_DEV, D_MODEL)),
             _slabs_by_core(dw_main, dw_f)]
    theirs = _sibling_swap(slabs, "swap_grads")
    core = lax.axis_index("c").astype(jnp.int32).reshape(1)
    chip_sums = [_pair_sum(m2, t4, core, "pair_sum_" + nm)
                 for m2, t4, nm in zip(slabs, theirs, ("wa", "wb", "wo", "w_in"))]
    (grad_x, dshift, dscale, dnormg), (g_wa_parts, g_wb_parts, g_wo_parts, g_in_parts) = _dh_and_dx(
        pieces, w_all_t, x2, dy, ada3, norm_g, chip_sums)
    vec = jnp.concatenate([dshift, dscale, dgate, dnormg, dbf, dcw.reshape(1, 3 * CONV_W), dqg, dkg], axis=1)
    (vec_all,) = _exchange([vec], True, "gather_small")
    vec_all = vec_all.reshape(N_DEV, vec.shape[1])
    n_main = 4 * D_MODEL + LANES + 3 * CONV_W
    tot, g_qg, g_kg = _sum_small(
        vec_all[:, :n_main],
        vec_all[:, n_main:n_main + ATTN_W].reshape(N_DEV * HEADS, HEAD_DIM),
        vec_all[:, n_main + ATTN_W:].reshape(N_DEV * HEADS, HEAD_DIM))
    g_b_ada = tot[:, 0:3 * D_MODEL]
    g_norm_g = tot[:, 3 * D_MODEL:4 * D_MODEL]
    g_b_f = tot[:, 4 * D_MODEL:4 * D_MODEL + HEADS]
    g_cw_full = tot[:, 4 * D_MODEL + LANES:].reshape(3, CONV_W)
    g_cw = lax.dynamic_slice(g_cw_full, (0, me * (CONV_W // N_DEV)), (3, CONV_W // N_DEV))
    dada_mine = lax.dynamic_slice(vec_all[:, 0:3 * D_MODEL], (0, me * ADA_SHARD), (N_DEV, ADA_SHARD))
    g_w_ada = _grad_w_ada(jnp.transpose(c_all, (0, 2, 1)), dada_mine.reshape(N_DEV, 1, ADA_SHARD))

    upd = {}
    upd["w_ada"] = _adamw(w_ada[0], m_w_ada[0], v_w_ada[0], g_w_ada[None], "adamw_w_ada")
    upd["b_ada"] = _adamw(b_ada, m_b_ada, v_b_ada, g_b_ada[None], "adamw_b_ada")
    upd["norm_g"] = _adamw(norm_g, m_norm_g, v_norm_g, g_norm_g[None], "adamw_norm_g")
    upd["w_in"] = [u.T for u in _adamw(w_in[0].T, m_w_in[0].T, v_w_in[0].T, g_in_parts, "adamw_w_in")]
    upd["b_f"] = _adamw(b_f, m_b_f, v_b_f, g_b_f[None], "adamw_b_f")
    upd["q_norm_g"] = _adamw(q_norm_g, m_q_norm_g, v_q_norm_g, g_qg[None], "adamw_q_norm_g")
    upd["k_norm_g"] = _adamw(k_norm_g, m_k_norm_g, v_k_norm_g, g_kg[None], "adamw_k_norm_g")
    upd["conv_w"] = _adamw(conv_w[0], m_conv_w[0], v_conv_w[0], g_cw[None], "adamw_conv_w")
    upd["w_attn_out"] = _adamw(w_attn_out[0], m_w_attn_out[0], v_w_attn_out[0], g_wa_parts, "adamw_w_attn_out")
    upd["w_conv_out"] = _adamw(w_conv_out[0], m_w_conv_out[0], v_w_conv_out[0], g_wb_parts, "adamw_w_conv_out")
    upd["w_o"] = _adamw(w_o[0], m_w_o[0], v_w_o[0], g_wo_parts, "adamw_w_o")

    names = ["w_ada", "b_ada", "norm_g", "w_in", "b_f", "q_norm_g", "k_norm_g", "conv_w",
             "w_attn_out", "w_conv_out", "w_o"]
    lead = {"w_ada", "w_in", "conv_w", "w_attn_out", "w_conv_out", "w_o"}
    fix = lambda n, a: a[None] if n in lead else a
    loss = lax.psum(loss_part[0, 0], ("x", "y", "c"))
    outs = [loss, grad_x[None]]
    for k in range(4):
        outs += [fix(n, upd[n][k]) for n in names]
    return tuple(outs)
```

```python
import functools

import numpy as np
import jax
import jax.numpy as jnp
from jax import lax
from jax.experimental import pallas as pl
from jax.experimental.pallas import tpu as pltpu

F32 = jnp.float32
BF16 = jnp.bfloat16

D_MODEL = 1024
HEADS = 8
HEAD_DIM = 64
ATTN_W = 512
CONV_W = 512
N_DEV = 8
IN_WIDTH = 6152
IN_SHARD = IN_WIDTH // N_DEV
N_MAIN = 6144
N_FPAD = 128
N_ALL = N_MAIN + N_FPAD
ADA_SHARD = 3 * D_MODEL // N_DEV
EPS = 1e-6
NEG = -1e30

ADAM_LR = 0.001
ADAM_B1 = 0.9
ADAM_B2 = 0.999
ADAM_EPS = 1e-08
ADAM_WD = 0.01
ADAM_STEP = 10

LANES = 128
VMEM_LIMIT = 56 * 1024 * 1024

TM_PROJ = 256
TN_PROJ = 1024
TM_ELEM = 512
TQ = 512
HEADS_PER_STEP = 4
HEADS_PER_STEP_FWD = 8
TM_TAIL = 256
TC_CUM = 256
TK_DW = 1024
TN_DW = 512
TM_DH = 256
HALO = 16

OFF_Q, OFF_K, OFF_V, OFF_ZA, OFF_CB, OFF_CC, OFF_CU, OFF_CZ, OFF_GA, OFF_GB = (
    0, 512, 1024, 1536, 2048, 2560, 3072, 3584, 4096, 5120)


def _params(sem=None):
    return pltpu.CompilerParams(dimension_semantics=sem, vmem_limit_bytes=VMEM_LIMIT)


def _dot(a, b):
    return jnp.dot(a, b, preferred_element_type=F32)


def _dot_nt(a, b):
    return lax.dot_general(a, b, (((1,), (1,)), ((), ())), preferred_element_type=F32)


def _dot_tn(a, b):
    return lax.dot_general(a, b, (((0,), (0,)), ((), ())), preferred_element_type=F32)


def _sigmoid(x):
    return 1.0 / (1.0 + jnp.exp(-x))


def _lane_lo(shape):
    return lax.broadcasted_iota(jnp.int32, shape, len(shape) - 1) < HEAD_DIM


def _seg_sum(z, lo):
    a = jnp.sum(jnp.where(lo, z, 0.0), axis=-1, keepdims=True)
    b = jnp.sum(jnp.where(lo, 0.0, z), axis=-1, keepdims=True)
    return jnp.where(lo, a, b)


def _lane_col(z, lane):
    idx = lax.broadcasted_iota(jnp.int32, z.shape, 1)
    return jnp.sum(jnp.where(idx == lane, z, 0.0), axis=-1, keepdims=True)


def _sub_row(z, row):
    idx = lax.broadcasted_iota(jnp.int32, z.shape, 0)
    return jnp.sum(jnp.where(idx == row, z, 0.0), axis=0, keepdims=True)


def _mesh_pos():
    x, y, c = lax.axis_index("x"), lax.axis_index("y"), lax.axis_index("c")
    return x, y, c, 4 * x + 2 * y + c


def _peer(k, x, y, c):
    px = 1 - x if (k >> 2) & 1 else x
    py = 1 - y if (k >> 1) & 1 else y
    pc = 1 - c if k & 1 else c
    return (px, py, pc), 4 * px + 2 * py + pc


def _exchange(arrs, gather, name):
    n = len(arrs)
    any_spec = pl.BlockSpec(memory_space=pl.ANY)

    def body(*refs):
        ins, outs = refs[:n], refs[n:2 * n]
        send_sems, recv_sems, local_sems = refs[2 * n:]
        x, y, c, me = _mesh_pos()
        copies = []
        for a in range(n):
            own = ins[a] if gather else ins[a].at[me]
            local = pltpu.make_async_copy(own, outs[a].at[me], local_sems.at[a])
            local.start()
            copies.append(local)
            for k in range(1, N_DEV):
                dev, p = _peer(k, x, y, c)
                cp = pltpu.make_async_remote_copy(
                    src_ref=ins[a] if gather else ins[a].at[p],
                    dst_ref=outs[a].at[me],
                    send_sem=send_sems.at[a * (N_DEV - 1) + k - 1],
                    recv_sem=recv_sems.at[a * (N_DEV - 1) + k - 1],
                    device_id=dev, device_id_type=pl.DeviceIdType.MESH)
                cp.start()
                copies.append(cp)
        for cp in copies:
            cp.wait()

    out_shape = [jax.ShapeDtypeStruct((N_DEV,) + a.shape if gather else a.shape, a.dtype) for a in arrs]
    return pl.pallas_call(
        body, name=name, out_shape=out_shape,
        in_specs=[any_spec] * n, out_specs=[any_spec] * n,
        scratch_shapes=[pltpu.SemaphoreType.DMA((n * (N_DEV - 1),)),
                        pltpu.SemaphoreType.DMA((n * (N_DEV - 1),)),
                        pltpu.SemaphoreType.DMA((n,))],
    )(*arrs)


def _gather_two_level(arrs, name):
    n = len(arrs)
    any_spec = pl.BlockSpec(memory_space=pl.ANY)
    per = N_DEV - 1

    def body(*refs):
        ins, outs = refs[:n], refs[n:2 * n]
        send_sems, recv_sems, local_sems = refs[2 * n:]
        x, y, c, me = _mesh_pos()
        sibling = (x, y, 1 - c)
        chips = [(1 - x, y), (x, 1 - y), (1 - x, 1 - y)]

        def copy(a, k, src, blk, to):
            return pltpu.make_async_remote_copy(
                src_ref=src, dst_ref=outs[a].at[blk],
                send_sem=send_sems.at[a * per + k], recv_sem=recv_sems.at[a * per + k],
                device_id=to, device_id_type=pl.DeviceIdType.MESH)

        local = [pltpu.make_async_copy(ins[a], outs[a].at[me], local_sems.at[a]) for a in range(n)]
        for cp in local:
            cp.start()
        first = []
        for a in range(n):
            first.append(copy(a, 0, ins[a], me, sibling))
            first += [copy(a, 1 + j, ins[a], me, (px, py, c)) for j, (px, py) in enumerate(chips)]
        for cp in first:
            cp.start()
        passed = []
        for j, (px, py) in enumerate(chips):
            blk = 4 * px + 2 * py + c
            for a in range(n):
                copy(a, 1 + j, ins[a], blk, (x, y, c)).wait_recv()
                fwd = copy(a, 4 + j, outs[a].at[blk], blk, sibling)
                fwd.start()
                passed.append(fwd)
        for a in range(n):
            copy(a, 0, ins[a], 4 * x + 2 * y + 1 - c, (x, y, c)).wait_recv()
            for j, (px, py) in enumerate(chips):
                copy(a, 4 + j, ins[a], 4 * px + 2 * py + 1 - c, (x, y, c)).wait_recv()
        for cp in first + passed:
            cp.wait_send()
        for cp in local:
            cp.wait()

    return pl.pallas_call(
        body, name=name,
        out_shape=[jax.ShapeDtypeStruct((N_DEV,) + a.shape, a.dtype) for a in arrs],
        in_specs=[any_spec] * n, out_specs=[any_spec] * n,
        scratch_shapes=[pltpu.SemaphoreType.DMA((n * per,)), pltpu.SemaphoreType.DMA((n * per,)),
                        pltpu.SemaphoreType.DMA((n,))],
    )(*arrs)


def _sibling_swap(arrs, name):
    n = len(arrs)
    any_spec = pl.BlockSpec(memory_space=pl.ANY)

    def body(*refs):
        ins, outs = refs[:n], refs[n:2 * n]
        send_sems, recv_sems = refs[2 * n:]
        x, y, c, _ = _mesh_pos()
        copies = [pltpu.make_async_remote_copy(
            src_ref=ins[a].at[1 - c], dst_ref=outs[a], send_sem=send_sems.at[a], recv_sem=recv_sems.at[a],
            device_id=(x, y, 1 - c), device_id_type=pl.DeviceIdType.MESH) for a in range(n)]
        for cp in copies:
            cp.start()
        for cp in copies:
            cp.wait()

    return pl.pallas_call(
        body, name=name,
        out_shape=[jax.ShapeDtypeStruct(a.shape[1:], a.dtype) for a in arrs],
        in_specs=[any_spec] * n, out_specs=[any_spec] * n,
        scratch_shapes=[pltpu.SemaphoreType.DMA((n,)), pltpu.SemaphoreType.DMA((n,))],
    )(*arrs)


def _pair_sum(mine2, theirs, core, name):
    _, _, rows, cols = mine2.shape
    tr = 256 if rows % 256 == 0 else rows

    def body(core_ref, a_ref, b_ref, out_ref):
        out_ref[...] = (a_ref[...].astype(F32) + b_ref[...].astype(F32)).astype(BF16)

    return pl.pallas_call(
        body, name=name,
        grid_spec=pltpu.PrefetchScalarGridSpec(
            num_scalar_prefetch=1, grid=(4, rows // tr),
            in_specs=[pl.BlockSpec((None, None, tr, cols), lambda ch, i, core_: (core_[0], ch, i, 0)),
                      pl.BlockSpec((None, tr, cols), lambda ch, i, core_: (ch, i, 0))],
            out_specs=pl.BlockSpec((None, tr, cols), lambda ch, i, core_: (ch, i, 0))),
        out_shape=jax.ShapeDtypeStruct(theirs.shape, BF16),
        compiler_params=_params(("parallel", "parallel")),
    )(core, mine2, theirs)


def _chip_copies(ins, outs, send_sems, recv_sems, local_sems):
    x, y, c, _ = _mesh_pos()
    my_chip = 2 * x + y
    chips = [(1 - x, y), (x, 1 - y), (1 - x, 1 - y)]
    copies = []
    for a in range(len(ins)):
        copies.append(pltpu.make_async_copy(ins[a].at[my_chip], outs[a].at[my_chip], local_sems.at[a]))
        for j, (px, py) in enumerate(chips):
            copies.append(pltpu.make_async_remote_copy(
                src_ref=ins[a].at[2 * px + py], dst_ref=outs[a].at[my_chip],
                send_sem=send_sems.at[a * 3 + j], recv_sem=recv_sems.at[a * 3 + j],
                device_id=(px, py, c), device_id_type=pl.DeviceIdType.MESH))
    return copies


def _ada_exchange(c_row, w_ada_sh):
    def body(c_ref, w_ref, call_ref, adag_ref, mine_ref, send_sems, recv_sems):
        x, y, c, me = _mesh_pos()

        def copy(phase, k, src, dst):
            dev, _ = _peer(k, x, y, c)
            return pltpu.make_async_remote_copy(
                src_ref=src, dst_ref=dst,
                send_sem=send_sems.at[phase * (N_DEV - 1) + k - 1],
                recv_sem=recv_sems.at[phase * (N_DEV - 1) + k - 1],
                device_id=dev, device_id_type=pl.DeviceIdType.MESH)

        call_ref[me] = c_ref[...]
        first = [copy(0, k, c_ref, call_ref.at[me]) for k in range(1, N_DEV)]
        for cp in first:
            cp.start()
        for cp in first:
            cp.wait()
        wb = w_ref[...].astype(BF16)
        for b in range(N_DEV):
            row = jnp.broadcast_to(call_ref[b], (8, D_MODEL)).astype(BF16)
            mine_ref[b] = _sub_row(_dot(row, wb), 0)
        adag_ref[me] = mine_ref[...]
        second = [copy(1, k, mine_ref, adag_ref.at[me]) for k in range(1, N_DEV)]
        for cp in second:
            cp.start()
        for cp in second:
            cp.wait()

    vm = pl.BlockSpec(memory_space=pltpu.VMEM)
    return pl.pallas_call(
        body, name="ada_exchange",
        out_shape=[jax.ShapeDtypeStruct((N_DEV, 1, D_MODEL), F32),
                   jax.ShapeDtypeStruct((N_DEV, N_DEV, 1, ADA_SHARD), F32)],
        in_specs=[vm, vm], out_specs=[vm, vm],
        scratch_shapes=[pltpu.VMEM((N_DEV, 1, ADA_SHARD), F32),
                        pltpu.SemaphoreType.DMA((2 * (N_DEV - 1),)),
                        pltpu.SemaphoreType.DMA((2 * (N_DEV - 1),))],
        compiler_params=pltpu.CompilerParams(vmem_limit_bytes=VMEM_LIMIT),
    )(c_row, w_ada_sh)


def _proj_fwd(x, ada3, norm_g, w_all_t):
    s = x.shape[0]
    tm, tn = min(TM_PROJ, s), TN_PROJ

    def body(x_ref, ada_ref, g_ref, wt_ref, proj_ref, fl_ref, h_ref):
        xv = x_ref[...]
        r = lax.rsqrt(jnp.mean(xv * xv, axis=-1, keepdims=True) + EPS)
        hv = ((xv * r) * g_ref[...]) * (1.0 + ada_ref[1:2, :]) + ada_ref[0:1, :]
        hb = hv.astype(BF16)
        h_ref[...] = hb
        fl_ref[...] = _dot_nt(hb, wt_ref[N_MAIN:N_ALL, :])
        for j in range(N_MAIN // tn):
            proj_ref[:, j * tn:(j + 1) * tn] = _dot_nt(hb, wt_ref[j * tn:(j + 1) * tn, :]).astype(BF16)

    return pl.pallas_call(
        body, name="proj_fwd", grid=(s // tm,),
        in_specs=[pl.BlockSpec((tm, D_MODEL), lambda i: (i, 0)),
                  pl.BlockSpec((3, D_MODEL), lambda i: (0, 0)),
                  pl.BlockSpec((1, D_MODEL), lambda i: (0, 0)),
                  pl.BlockSpec((N_ALL, D_MODEL), lambda i: (0, 0))],
        out_specs=[pl.BlockSpec((tm, N_MAIN), lambda i: (i, 0)),
                   pl.BlockSpec((tm, N_FPAD), lambda i: (i, 0)),
                   pl.BlockSpec((tm, D_MODEL), lambda i: (i, 0))],
        out_shape=[jax.ShapeDtypeStruct((s, N_MAIN), BF16),
                   jax.ShapeDtypeStruct((s, N_FPAD), F32),
                   jax.ShapeDtypeStruct((s, D_MODEL), BF16)],
        compiler_params=_params(("parallel",)),
    )(x, ada3, norm_g, w_all_t)


L_ONE_Q, L_F_Q, L_LSE_Q, L_END = HEAD_DIM, HEAD_DIM + 3, HEAD_DIM + 6, HEAD_DIM + 9


def _split3(f):
    hi = f.astype(BF16).astype(F32)
    r = f - hi
    mid = r.astype(BF16).astype(F32)
    return hi, mid, r - mid


def _place3(lane, first, parts, otherwise):
    a, b, c = parts
    return jnp.where(lane == first, a, jnp.where(lane == first + 1, b, jnp.where(lane == first + 2, c, otherwise)))


def _log_forget(fl, bf):
    z = fl + bf
    lf = jnp.minimum(z, 0.0) - jnp.log1p(jnp.exp(-jnp.abs(z)))
    lane = lax.broadcasted_iota(jnp.int32, z.shape, 1)
    return jnp.where(lane < HEADS, lf, 0.0)


def _qkv_prep(proj, fl, bf_pad, qg, kg):
    s = proj.shape[0]
    tm = min(TM_ELEM, s)
    scale = HEAD_DIM ** -0.5

    def body(p_ref, fl_ref, bf_ref, qg_ref, kg_ref, qa_ref, ka_ref, va_ref, kt_ref, vt_ref, carry):
        @pl.when(pl.program_id(0) == 0)
        def _():
            carry[...] = jnp.zeros_like(carry)
        tri = (lax.broadcasted_iota(jnp.int32, (tm, tm), 1) <= lax.broadcasted_iota(jnp.int32, (tm, tm), 0)).astype(F32)
        cum_v = jnp.dot(tri, _log_forget(fl_ref[...], bf_ref[...]), preferred_element_type=F32,
                        precision=lax.Precision.HIGHEST) + carry[...]
        carry[...] = _sub_row(cum_v, tm - 1)
        lane = lax.broadcasted_iota(jnp.int32, (tm, LANES), 1)
        lo = lane < HEAD_DIM
        v_tail = jnp.where(lane < L_F_Q, 1.0, 0.0)
        for pr in range(ATTN_W // LANES):
            sl = slice(pr * LANES, (pr + 1) * LANES)
            q2 = p_ref[:, OFF_Q + pr * LANES:OFF_Q + (pr + 1) * LANES].astype(F32)
            k2 = p_ref[:, OFF_K + pr * LANES:OFF_K + (pr + 1) * LANES].astype(F32)
            v2 = p_ref[:, OFF_V + pr * LANES:OFF_V + (pr + 1) * LANES].astype(F32)
            rq = lax.rsqrt(_seg_sum(q2 * q2, lo) * (1.0 / HEAD_DIM) + EPS)
            rk = lax.rsqrt(_seg_sum(k2 * k2, lo) * (1.0 / HEAD_DIM) + EPS)
            qn = ((q2 * rq) * qg_ref[:, sl]) * scale
            kn = (k2 * rk) * kg_ref[:, sl]
            for hh in range(2):
                h = 2 * pr + hh
                f3 = _split3(_lane_col(cum_v, h))
                qh = qn if hh == 0 else pltpu.roll(qn, HEAD_DIM, 1)
                kh = kn if hh == 0 else pltpu.roll(kn, HEAD_DIM, 1)
                vh = v2 if hh == 0 else pltpu.roll(v2, HEAD_DIM, 1)
                q_tail = jnp.where(lane < L_F_Q, 1.0, _place3(lane, L_F_Q, f3, 0.0))
                k_tail = _place3(lane, L_ONE_Q, tuple(-f for f in f3), jnp.where(lane < L_END, 1.0, 0.0))
                k_row = jnp.where(lo, kh, k_tail)
                v_row = jnp.where(lo, vh, v_tail)
                qa_ref[h] = jnp.where(lo, qh, q_tail).astype(BF16)
                ka_ref[h] = k_row.astype(BF16)
                va_ref[h] = v_row.astype(BF16)
                kt_ref[h] = k_row.T.astype(BF16)
                vt_ref[h] = v_row.T.astype(BF16)

    heads = pl.BlockSpec((HEADS, tm, LANES), lambda i: (0, i, 0))
    heads_t = pl.BlockSpec((HEADS, LANES, tm), lambda i: (0, 0, i))
    vec = pl.BlockSpec((1, ATTN_W), lambda i: (0, 0))
    return pl.pallas_call(
        body, name="qkv_prep", grid=(s // tm,),
        in_specs=[pl.BlockSpec((tm, 3 * ATTN_W), lambda i: (i, 0)),
                  pl.BlockSpec((tm, LANES), lambda i: (i, 0)),
                  pl.BlockSpec((1, LANES), lambda i: (0, 0)), vec, vec],
        out_specs=[heads, heads, heads, heads_t, heads_t],
        out_shape=[jax.ShapeDtypeStruct((HEADS, s, LANES), BF16)] * 3
        + [jax.ShapeDtypeStruct((HEADS, LANES, s), BF16)] * 2,
        scratch_shapes=[pltpu.VMEM((1, LANES), F32)],
        compiler_params=_params(("arbitrary",)),
    )(proj, fl, bf_pad, qg, kg)


def _causal_t(t):
    return lax.broadcasted_iota(jnp.int32, (t, t), 0) <= lax.broadcasted_iota(jnp.int32, (t, t), 1)


def _tri_steps(nt, q_major):
    if q_major:
        pairs = [(i, j) for i in range(nt) for j in range(i + 1)]
    else:
        pairs = [(i, j) for j in range(nt) for i in range(j, nt)]
    return (jnp.asarray(np.array([p[0] for p in pairs], np.int32)),
            jnp.asarray(np.array([p[1] for p in pairs], np.int32)))


def _attn_fwd(qa, ka, vt, proj):
    s = qa.shape[1]
    t = min(TQ, s)
    it, jt = _tri_steps(s // t, True)
    hp = HEADS_PER_STEP_FWD
    wide = hp * HEAD_DIM
    za_blk = OFF_ZA // wide

    def body(it_ref, jt_ref, q_ref, k_ref, vt_ref, za_ref, attn_ref, oa_ref, qb_ref, m_s, acc_s, pair_s):
        step = pl.program_id(1)
        i, j = it_ref[step], jt_ref[step]

        @pl.when(j == 0)
        def _():
            m_s[...] = jnp.full_like(m_s, NEG)
            acc_s[...] = jnp.zeros_like(acc_s)

        def update(masked):
            for hh in range(hp):
                st = _dot_nt(k_ref[hh], q_ref[hh])
                if masked:
                    st = jnp.where(_causal_t(t), st, NEG)
                m_prev = m_s[hh]
                m_next = jnp.maximum(m_prev, jnp.max(st, axis=0, keepdims=True))
                alpha = jnp.exp(m_prev - m_next)
                pt = jnp.exp(st - m_next).astype(BF16)
                acc_s[hh] = acc_s[hh] * alpha + _dot(vt_ref[hh], pt)
                m_s[hh] = m_next

        @pl.when(j < i)
        def _():
            update(False)

        @pl.when(j == i)
        def _():
            update(True)
            row = lax.broadcasted_iota(jnp.int32, (LANES, t), 0)
            lane = lax.broadcasted_iota(jnp.int32, (t, LANES), 1)
            for hh in range(hp):
                l_row = acc_s[hh, L_ONE_Q:L_ONE_Q + 1, :]
                pair_s[hh * HEAD_DIM:(hh + 1) * HEAD_DIM, :] = acc_s[hh, 0:HEAD_DIM, :] / l_row
                lse3 = _split3(m_s[hh] + jnp.log(l_row))
                tail_t = _place3(row, L_LSE_Q, tuple(-x for x in lse3), 0.0)
                keep_q = jnp.logical_or(lane < L_LSE_Q, lane >= L_END)
                qb_ref[hh] = jnp.where(keep_q, q_ref[hh].astype(F32), tail_t.T).astype(BF16)
            out = pair_s[...].T
            attn_ref[...] = out
            z = za_ref[...].astype(F32)
            oa_ref[...] = (out * (z * _sigmoid(z))).astype(BF16)

    pair_q = pl.BlockSpec((hp, t, LANES), lambda p, n, it_, jt_: (p, it_[n], 0))
    pair_k = pl.BlockSpec((hp, t, LANES), lambda p, n, it_, jt_: (p, jt_[n], 0))
    pair_kt = pl.BlockSpec((hp, LANES, t), lambda p, n, it_, jt_: (p, 0, jt_[n]))
    out_q = pl.BlockSpec((t, wide), lambda p, n, it_, jt_: (it_[n], p))
    return pl.pallas_call(
        body, name="attn_fwd",
        grid_spec=pltpu.PrefetchScalarGridSpec(
            num_scalar_prefetch=2, grid=(HEADS // hp, it.shape[0]),
            in_specs=[pair_q, pair_k, pair_kt,
                      pl.BlockSpec((t, wide), lambda p, n, it_, jt_: (it_[n], za_blk + p))],
            out_specs=[out_q, out_q, pair_q],
            scratch_shapes=[pltpu.VMEM((hp, 1, t), F32), pltpu.VMEM((hp, LANES, t), F32),
                            pltpu.VMEM((wide, t), F32)]),
        out_shape=[jax.ShapeDtypeStruct((s, ATTN_W), F32),
                   jax.ShapeDtypeStruct((s, ATTN_W), BF16),
                   jax.ShapeDtypeStruct((HEADS, s, LANES), BF16)],
        compiler_params=_params(("parallel", "arbitrary")),
    )(it, jt, qa, ka, vt, proj)


def _conv_parts(gb_ref, gc_ref, u_ref, zb_ref, gch_ref, uh_ref, first, w_ref, tm):
    gb, gc = gb_ref[...].astype(F32), gc_ref[...].astype(F32)
    u, zb = u_ref[...].astype(F32), zb_ref[...].astype(F32)
    cu = gc * u
    cu_h = jnp.where(first, 0.0, gch_ref[...].astype(F32) * uh_ref[...].astype(F32))
    prev1, prev2 = _sub_row(cu_h, HALO - 1), _sub_row(cu_h, HALO - 2)
    row = lax.broadcasted_iota(jnp.int32, cu.shape, 0)
    r1 = jnp.where(row == 0, prev1, pltpu.roll(cu, 1, 0))
    r2 = jnp.where(row == 0, prev2, jnp.where(row == 1, prev1, pltpu.roll(cu, 2, 0)))
    conv = w_ref[2:3, :] * cu + w_ref[1:2, :] * r1 + w_ref[0:1, :] * r2
    return gb, gc, u, zb, cu, r1, r2, conv


def _conv_specs(tm, s, width=LANES):
    def tile(off):
        return pl.BlockSpec((tm, width), lambda c, i: (i, off // width + c))

    def before(off):
        return pl.BlockSpec((HALO, width), lambda c, i: (jnp.maximum(i * (tm // HALO) - 1, 0), off // width + c))

    def after(off):
        return pl.BlockSpec((HALO, width),
                            lambda c, i: (jnp.minimum((i + 1) * (tm // HALO), s // HALO - 1), off // width + c))

    return ([tile(OFF_CB), tile(OFF_CC), tile(OFF_CU), tile(OFF_CZ)], [before(OFF_CC), before(OFF_CU)],
            [after(OFF_CB), after(OFF_CZ)])


def _tail(oa, attn, proj, x, target, ada3, wa, wb, wo, conv_w):
    s = x.shape[0]
    tm = min(TM_TAIL, s)
    gab_blk = OFF_GA // (2 * D_MODEL)
    za_blk = OFF_ZA // ATTN_W
    tiles, befores, _ = _conv_specs(tm, s, CONV_W)

    def body(oa_ref, attn_ref, za_ref, gb_ref, gc_ref, u_ref, zb_ref, gch_ref, uh_ref, cw_ref, gab_ref, x_ref, t_ref,
             ada_ref, wa_ref, wb_ref, wo_ref,
             dy_ref, dgab_ref, do_ref, dza_ref, dob_ref, dwo_ref, dwa_ref, dwb_ref, dgate_ref, loss_ref):
        first = pl.program_id(0) == 0

        @pl.when(first)
        def _():
            dwo_ref[...] = jnp.zeros_like(dwo_ref)
            dwa_ref[...] = jnp.zeros_like(dwa_ref)
            dwb_ref[...] = jnp.zeros_like(dwb_ref)
            dgate_ref[...] = jnp.zeros_like(dgate_ref)
            loss_ref[...] = jnp.zeros_like(loss_ref)

        gb, _, _, zb, _, _, _, conv = _conv_parts(gb_ref, gc_ref, u_ref, zb_ref, gch_ref, uh_ref, first, cw_ref, tm)
        ob_v = (gb * conv * (zb * _sigmoid(zb))).astype(BF16)
        oa_v = oa_ref[...]
        wa_v, wb_v, wo_v = wa_ref[...], wb_ref[...], wo_ref[...]
        a2 = _dot(oa_v, wa_v)
        b2 = _dot(ob_v, wb_v)
        sa = _sigmoid(gab_ref[:, 0:D_MODEL].astype(F32))
        sb = _sigmoid(gab_ref[:, D_MODEL:2 * D_MODEL].astype(F32))
        mb = (sa * a2 + sb * b2).astype(BF16)
        mo = _dot(mb, wo_v)
        gate = ada_ref[2:3, :]
        err = (x_ref[...] + gate * mo) - t_ref[...]
        dy = err * (1.0 / D_MODEL)
        dy_ref[...] = dy
        loss_ref[...] += 0.5 * jnp.sum(err * err) * (1.0 / D_MODEL)
        dgate_ref[...] += jnp.sum(dy * mo, axis=0, keepdims=True)
        dmo = (dy * gate).astype(BF16)
        dmerged = _dot_nt(dmo, wo_v)
        dwo_ref[...] += _dot_tn(mb, dmo)
        da2 = (dmerged * sa).astype(BF16)
        db2 = (dmerged * sb).astype(BF16)
        dgab_ref[:, 0:D_MODEL] = (dmerged * a2 * (sa * (1.0 - sa))).astype(BF16)
        dgab_ref[:, D_MODEL:2 * D_MODEL] = (dmerged * b2 * (sb * (1.0 - sb))).astype(BF16)
        doa = _dot_nt(da2, wa_v)
        dob_ref[...] = _dot_nt(db2, wb_v)
        dwa_ref[...] += _dot_tn(oa_v, da2)
        dwb_ref[...] += _dot_tn(ob_v, db2)

        lane = lax.broadcasted_iota(jnp.int32, (tm, LANES), 1)
        lo = lane < HEAD_DIM
        for pr in range(ATTN_W // LANES):
            sl = slice(pr * LANES, (pr + 1) * LANES)
            g, a, z = doa[:, sl], attn_ref[:, sl], za_ref[:, sl].astype(F32)
            sg = _sigmoid(z)
            dat = (g * (z * sg)).astype(BF16).astype(F32)
            prod = dat * a
            dza_ref[:, sl] = (g * a * (sg * (1.0 + z * (1.0 - sg)))).astype(BF16)
            for hh in range(2):
                sel = lo if hh == 0 else jnp.logical_not(lo)
                delta3 = _split3(jnp.sum(jnp.where(sel, prod, 0.0), axis=-1, keepdims=True))
                dh = dat if hh == 0 else pltpu.roll(dat, HEAD_DIM, 1)
                tail_lanes = _place3(lane, L_ONE_Q, tuple(-d for d in delta3), 0.0)
                do_ref[2 * pr + hh] = jnp.where(lo, dh, tail_lanes).astype(BF16)

    half = pl.BlockSpec((tm, ATTN_W), lambda i: (i, 0))
    full = pl.BlockSpec((tm, D_MODEL), lambda i: (i, 0))

    def const(shape):
        return pl.BlockSpec(shape, lambda i: (0, 0))

    def one_axis(spec):
        return pl.BlockSpec(spec.block_shape, lambda i, f=spec.index_map: f(0, i))

    return pl.pallas_call(
        body, name="tail", grid=(s // tm,),
        in_specs=[half, half, pl.BlockSpec((tm, ATTN_W), lambda i: (i, za_blk))]
        + [one_axis(sp) for sp in tiles + befores]
        + [const((3, CONV_W)), pl.BlockSpec((tm, 2 * D_MODEL), lambda i: (i, gab_blk)), full, full,
           const((3, D_MODEL)), const((ATTN_W, D_MODEL)), const((CONV_W, D_MODEL)), const((D_MODEL, D_MODEL))],
        out_specs=[full, pl.BlockSpec((tm, 2 * D_MODEL), lambda i: (i, 0)),
                   pl.BlockSpec((HEADS, tm, LANES), lambda i: (0, i, 0)), half, half,
                   const((D_MODEL, D_MODEL)), const((ATTN_W, D_MODEL)), const((CONV_W, D_MODEL)),
                   const((1, D_MODEL)), const((1, LANES))],
        out_shape=[jax.ShapeDtypeStruct((s, D_MODEL), F32),
                   jax.ShapeDtypeStruct((s, 2 * D_MODEL), BF16),
                   jax.ShapeDtypeStruct((HEADS, s, LANES), BF16),
                   jax.ShapeDtypeStruct((s, ATTN_W), BF16),
                   jax.ShapeDtypeStruct((s, CONV_W), F32),
                   jax.ShapeDtypeStruct((D_MODEL, D_MODEL), F32),
                   jax.ShapeDtypeStruct((ATTN_W, D_MODEL), F32),
                   jax.ShapeDtypeStruct((CONV_W, D_MODEL), F32),
                   jax.ShapeDtypeStruct((1, D_MODEL), F32),
                   jax.ShapeDtypeStruct((1, LANES), F32)],
        compiler_params=_params(("arbitrary",)),
    )(oa, attn, proj, *([proj] * 6), conv_w, proj, x, target, ada3, wa, wb, wo)


def _attn_bwd(qb, ka, kt, va, do):
    s = qb.shape[1]
    t = min(TQ, s)
    nt = s // t
    hp = HEADS_PER_STEP
    it, jt = _tri_steps(nt, False)

    def body(it_ref, jt_ref, q_ref, k_ref, kt_ref, v_ref, do_ref, dq_ref, dk_ref, dv_ref, dqt_s):
        step = pl.program_id(1)
        i, j = it_ref[step], jt_ref[step]

        @pl.when(step == 0)
        def _():
            dqt_s[...] = jnp.zeros_like(dqt_s)

        @pl.when(i == j)
        def _():
            dk_ref[...] = jnp.zeros_like(dk_ref)
            dv_ref[...] = jnp.zeros_like(dv_ref)

        def update(masked):
            for hh in range(hp):
                qh, doh = q_ref[hh], do_ref[hh]
                st = _dot_nt(k_ref[hh], qh)
                if masked:
                    st = jnp.where(_causal_t(t), st, NEG)
                pt = jnp.exp(st)
                dst = (pt * _dot_nt(v_ref[hh], doh)).astype(BF16)
                dv_ref[hh] += _dot(pt.astype(BF16), doh)
                dk_ref[hh] += _dot(dst, qh)
                dqt_s[hh, i] += _dot(kt_ref[hh], dst)

        @pl.when(i > j)
        def _():
            update(False)

        @pl.when(i == j)
        def _():
            update(True)
            for hh in range(hp):
                dq_ref[hh] = dqt_s[hh, i].T

    pair_q = pl.BlockSpec((hp, t, LANES), lambda p, n, it_, jt_: (p, it_[n], 0))
    pair_k = pl.BlockSpec((hp, t, LANES), lambda p, n, it_, jt_: (p, jt_[n], 0))
    pair_kt = pl.BlockSpec((hp, LANES, t), lambda p, n, it_, jt_: (p, 0, jt_[n]))
    return pl.pallas_call(
        body, name="attn_bwd",
        grid_spec=pltpu.PrefetchScalarGridSpec(
            num_scalar_prefetch=2, grid=(HEADS // hp, it.shape[0]),
            in_specs=[pair_q, pair_k, pair_kt, pair_k, pair_q],
            out_specs=[pair_k, pair_k, pair_k],
            scratch_shapes=[pltpu.VMEM((hp, nt, LANES, t), F32)]),
        out_shape=[jax.ShapeDtypeStruct((HEADS, s, LANES), F32)] * 3,
        compiler_params=_params(("parallel", "arbitrary")),
    )(it, jt, qb, ka, kt, va, do)


def _forget_bwd(dcum, fl, bf_pad):
    s = fl.shape[0]
    tc = min(TC_CUM, s)
    n = s // tc

    def body(dc_ref, fl_ref, bf_ref, df_ref, dbf_ref, carry):
        @pl.when(pl.program_id(0) == 0)
        def _():
            carry[...] = jnp.zeros_like(carry)
            dbf_ref[...] = jnp.zeros_like(dbf_ref)
        r = lax.broadcasted_iota(jnp.int32, (tc, tc), 0)
        cidx = lax.broadcasted_iota(jnp.int32, (tc, tc), 1)
        tri = (cidx >= r).astype(F32)
        dc = dc_ref[...]
        dlf = jnp.dot(tri, dc, preferred_element_type=F32, precision=lax.Precision.HIGHEST) + carry[...]
        carry[...] += jnp.sum(dc, axis=0, keepdims=True)
        lane = lax.broadcasted_iota(jnp.int32, (tc, LANES), 1)
        dfl = jnp.where(lane < HEADS, dlf * _sigmoid(-(fl_ref[...] + bf_ref[...])), 0.0)
        df_ref[...] = dfl.astype(BF16)
        dbf_ref[...] += jnp.sum(dfl, axis=0, keepdims=True)

    rev = pl.BlockSpec((tc, LANES), lambda i: (n - 1 - i, 0))
    vec = pl.BlockSpec((1, LANES), lambda i: (0, 0))
    return pl.pallas_call(
        body, name="forget_bwd", grid=(n,),
        in_specs=[rev, rev, vec], out_specs=[rev, vec],
        out_shape=[jax.ShapeDtypeStruct((s, LANES), BF16), jax.ShapeDtypeStruct((1, LANES), F32)],
        scratch_shapes=[pltpu.VMEM((1, LANES), F32)],
        compiler_params=_params(("arbitrary",)),
    )(dcum, fl, bf_pad)


def _qk_norm_bwd(dq, dk, dv, proj, qg, kg):
    s = dv.shape[1]
    tm = min(TM_ELEM, s)
    scale = HEAD_DIM ** -0.5

    def body(dq_ref, dk_ref, dv_ref, p_ref, qg_ref, kg_ref, out_ref, dqg_ref, dkg_ref, dcum_ref):
        @pl.when(pl.program_id(0) == 0)
        def _():
            dqg_ref[...] = jnp.zeros_like(dqg_ref)
            dkg_ref[...] = jnp.zeros_like(dkg_ref)
        lane = lax.broadcasted_iota(jnp.int32, (tm, LANES), 1)
        lo = lane < HEAD_DIM
        dcum = jnp.zeros((tm, LANES), F32)
        for h in range(HEADS):
            dcum = jnp.where(lane == h, _lane_col(dq_ref[h], L_F_Q) - _lane_col(dk_ref[h], L_ONE_Q), dcum)
        dcum_ref[...] = dcum

        def pair(a, b):
            return jnp.where(lo, a, pltpu.roll(b, HEAD_DIM, 1))

        def one(raw, dy, g, dg_ref, sl, off):
            r = lax.rsqrt(_seg_sum(raw * raw, lo) * (1.0 / HEAD_DIM) + EPS)
            xhat = raw * r
            dg_ref[:, sl] += jnp.sum(dy * xhat, axis=0, keepdims=True)
            dxh = dy * g
            dx = r * (dxh - xhat * (_seg_sum(dxh * xhat, lo) * (1.0 / HEAD_DIM)))
            out_ref[:, off + sl.start:off + sl.stop] = dx.astype(BF16)

        for pr in range(ATTN_W // LANES):
            sl = slice(pr * LANES, (pr + 1) * LANES)
            dq2 = pair(dq_ref[2 * pr], dq_ref[2 * pr + 1])
            one(p_ref[:, OFF_Q + sl.start:OFF_Q + sl.stop].astype(F32), dq2 * scale, qg_ref[:, sl], dqg_ref, sl, OFF_Q)
            one(p_ref[:, OFF_K + sl.start:OFF_K + sl.stop].astype(F32), pair(dk_ref[2 * pr], dk_ref[2 * pr + 1]),
                kg_ref[:, sl], dkg_ref, sl, OFF_K)
            out_ref[:, OFF_V + sl.start:OFF_V + sl.stop] = pair(dv_ref[2 * pr], dv_ref[2 * pr + 1]).astype(BF16)

    heads = pl.BlockSpec((HEADS, tm, LANES), lambda i: (0, i, 0))
    vec = pl.BlockSpec((1, ATTN_W), lambda i: (0, 0))
    return pl.pallas_call(
        body, name="qk_norm_bwd", grid=(s // tm,),
        in_specs=[heads, heads, heads, pl.BlockSpec((tm, 2 * ATTN_W), lambda i: (i, 0)), vec, vec],
        out_specs=[pl.BlockSpec((tm, 3 * ATTN_W), lambda i: (i, 0)), vec, vec,
                   pl.BlockSpec((tm, LANES), lambda i: (i, 0))],
        out_shape=[jax.ShapeDtypeStruct((s, 3 * ATTN_W), BF16),
                   jax.ShapeDtypeStruct((1, ATTN_W), F32), jax.ShapeDtypeStruct((1, ATTN_W), F32),
                   jax.ShapeDtypeStruct((s, LANES), F32)],
        compiler_params=_params(("arbitrary",)),
    )(dq, dk, dv, proj, qg, kg)


def _conv_bwd(dob, proj, conv_w):
    s = dob.shape[0]
    tm = min(TM_ELEM, s)
    tiles, befores, afters = _conv_specs(tm, s)

    def body(dob_ref, dnext_ref, gb_ref, gc_ref, u_ref, zb_ref, gch_ref, uh_ref, gbn_ref, zbn_ref, w_ref,
             dgb_ref, dgc_ref, du_ref, dzb_ref, dw_ref):
        i = pl.program_id(1)

        @pl.when(i == 0)
        def _():
            dw_ref[...] = jnp.zeros_like(dw_ref)
        gb, gc, u, zb, cu, r1, r2, conv = _conv_parts(gb_ref, gc_ref, u_ref, zb_ref, gch_ref, uh_ref, i == 0, w_ref, tm)
        g = dob_ref[...]
        sg = _sigmoid(zb)
        sz = zb * sg
        dconv = g * gb * sz
        zn = zbn_ref[0:8, :].astype(F32)
        dcn = jnp.where(i == pl.num_programs(1) - 1, 0.0,
                        dnext_ref[...] * gbn_ref[0:8, :].astype(F32) * (zn * _sigmoid(zn)))
        nxt1, nxt2 = _sub_row(dcn, 0), _sub_row(dcn, 1)
        row = lax.broadcasted_iota(jnp.int32, (tm, LANES), 0)
        f1 = jnp.where(row == tm - 1, nxt1, pltpu.roll(dconv, tm - 1, 0))
        f2 = jnp.where(row == tm - 2, nxt1, jnp.where(row == tm - 1, nxt2, pltpu.roll(dconv, tm - 2, 0)))
        dcu = w_ref[2:3, :] * dconv + w_ref[1:2, :] * f1 + w_ref[0:1, :] * f2
        dgb_ref[...] = (g * conv * sz).astype(BF16)
        dgc_ref[...] = (dcu * u).astype(BF16)
        du_ref[...] = (dcu * gc).astype(BF16)
        dzb_ref[...] = (g * gb * conv * (sg * (1.0 + zb * (1.0 - sg)))).astype(BF16)
        w_row = lax.broadcasted_iota(jnp.int32, (3, LANES), 0)
        dw0 = jnp.sum(dconv * r2, axis=0, keepdims=True)
        dw1 = jnp.sum(dconv * r1, axis=0, keepdims=True)
        dw2 = jnp.sum(dconv * cu, axis=0, keepdims=True)
        dw_ref[...] += jnp.where(w_row == 0, dw0, jnp.where(w_row == 1, dw1, dw2))

    blk = pl.BlockSpec((tm, LANES), lambda c, i: (i, c))
    nxt = pl.BlockSpec((8, LANES), lambda c, i: (jnp.minimum((i + 1) * (tm // 8), s // 8 - 1), c))
    wspec = pl.BlockSpec((3, LANES), lambda c, i: (0, c))
    return pl.pallas_call(
        body, name="conv_bwd", grid=(CONV_W // LANES, s // tm),
        in_specs=[blk, nxt] + tiles + befores + afters + [wspec],
        out_specs=[blk, blk, blk, blk, wspec],
        out_shape=[jax.ShapeDtypeStruct((s, CONV_W), BF16)] * 4 + [jax.ShapeDtypeStruct((3, CONV_W), F32)],
        compiler_params=_params(("parallel", "arbitrary")),
    )(dob, dob, *([proj] * 8), conv_w)


def _piece_layout(pieces):
    offs, off = [], 0
    for p in pieces:
        offs.append((off, p.shape[1]))
        off += p.shape[1]
    assert off == N_ALL, off
    return offs


def _dw_in(h, pieces):
    s = h.shape[0]
    tk, tn = min(TK_DW, s), TN_DW
    nk = s // tk
    main, fpiece = pieces[:-1], pieces[-1]
    layout = _piece_layout(pieces)[:-1]
    n_main = len(main)

    def body(*refs):
        p_refs, f_ref, h_ref = refs[:n_main], refs[n_main], refs[n_main + 1]
        out_ref, outf_ref, acc, accf = refs[n_main + 2:]
        n, k = pl.program_id(0), pl.program_id(1)

        @pl.when(k == 0)
        def _():
            acc[...] = jnp.zeros_like(acc)
        hv = h_ref[pl.ds(pl.multiple_of(k * tk, tk), tk), :]
        for p_ref, (off, width) in zip(p_refs, layout):
            @pl.when(jnp.logical_and(n >= off // tn, n < (off + width) // tn))
            def _():
                acc[...] += _dot_tn(p_ref[...], hv)

        @pl.when(k == nk - 1)
        def _():
            out_ref[...] = acc[...].astype(BF16)

        @pl.when(n == 0)
        def _():
            @pl.when(k == 0)
            def _():
                accf[...] = jnp.zeros_like(accf)
            accf[...] += _dot_tn(f_ref[...], hv)

            @pl.when(k == nk - 1)
            def _():
                outf_ref[...] = accf[...].astype(BF16)

    def piece_spec(off, width):
        lo, hi = off // tn, (off + width) // tn

        def index(n, k):
            active = jnp.logical_and(n >= lo, n < hi)
            return jnp.where(active, k, 0), jnp.clip(n - lo, 0, hi - lo - 1)
        return pl.BlockSpec((tk, tn), index)

    return pl.pallas_call(
        body, name="dw_in", grid=(N_MAIN // tn, nk),
        in_specs=[piece_spec(off, width) for off, width in layout]
        + [pl.BlockSpec((tk, N_FPAD), lambda n, k: (jnp.where(n == 0, k, 0), 0)),
           pl.BlockSpec((s, D_MODEL), lambda n, k: (0, 0))],
        out_specs=[pl.BlockSpec((tn, D_MODEL), lambda n, k: (n, 0)),
                   pl.BlockSpec((N_FPAD, D_MODEL), lambda n, k: (0, 0))],
        out_shape=[jax.ShapeDtypeStruct((N_MAIN, D_MODEL), BF16), jax.ShapeDtypeStruct((N_FPAD, D_MODEL), BF16)],
        scratch_shapes=[pltpu.VMEM((tn, D_MODEL), F32), pltpu.VMEM((N_FPAD, D_MODEL), F32)],
        compiler_params=_params(("arbitrary", "arbitrary")),
    )(*main, fpiece, h)


def _dh_and_dx(pieces, w_all_t, x, dy, ada3, norm_g, chip_sums):
    s = x.shape[0]
    tm = min(TM_DH, s)
    nt = s // tm
    n = len(chip_sums)
    npc = len(pieces)
    layout = _piece_layout(pieces)

    def body(*refs):
        p_refs, refs = refs[:npc], refs[npc:]
        wt_ref, x_ref, dy_ref, ada_ref, g_ref = refs[:5]
        ins, refs = refs[5:5 + n], refs[5 + n:]
        gx_ref, dsh_ref, dsc_ref, dg_ref = refs[:4]
        outs, (send_sems, recv_sems, local_sems) = refs[4:4 + n], refs[4 + n:]
        i = pl.program_id(0)

        @pl.when(i == 0)
        def _():
            for cp in _chip_copies(ins, outs, send_sems, recv_sems, local_sems):
                cp.start()
            dsh_ref[...] = jnp.zeros_like(dsh_ref)
            dsc_ref[...] = jnp.zeros_like(dsc_ref)
            dg_ref[...] = jnp.zeros_like(dg_ref)

        dh = None
        for p_ref, (off, width) in zip(p_refs, layout):
            part = _dot(p_ref[...], wt_ref[off:off + width, :])
            dh = part if dh is None else dh + part
        xv = x_ref[...]
        r = lax.rsqrt(jnp.mean(xv * xv, axis=-1, keepdims=True) + EPS)
        xhat = xv * r
        g = g_ref[...]
        one_sc = 1.0 + ada_ref[1:2, :]
        dsh_ref[...] += jnp.sum(dh, axis=0, keepdims=True)
        dsc_ref[...] += jnp.sum(dh * (xhat * g), axis=0, keepdims=True)
        dg_ref[...] += jnp.sum(dh * xhat, axis=0, keepdims=True) * one_sc
        dxh = dh * (g * one_sc)
        dx = r * (dxh - xhat * jnp.mean(dxh * xhat, axis=-1, keepdims=True))
        gx_ref[...] = dy_ref[...] + dx

        @pl.when(i == nt - 1)
        def _():
            for cp in _chip_copies(ins, outs, send_sems, recv_sems, local_sems):
                cp.wait()

    full = pl.BlockSpec((tm, D_MODEL), lambda i: (i, 0))
    vec = pl.BlockSpec((1, D_MODEL), lambda i: (0, 0))
    any_spec = pl.BlockSpec(memory_space=pl.ANY)
    res = pl.pallas_call(
        body, name="dh_dx", grid=(nt,),
        in_specs=[pl.BlockSpec((tm, p.shape[1]), lambda i: (i, 0)) for p in pieces]
        + [pl.BlockSpec((N_ALL, D_MODEL), lambda i: (0, 0)), full, full,
           pl.BlockSpec((3, D_MODEL), lambda i: (0, 0)), vec] + [any_spec] * n,
        out_specs=[full, vec, vec, vec] + [any_spec] * n,
        out_shape=[jax.ShapeDtypeStruct((s, D_MODEL), F32)] + [jax.ShapeDtypeStruct((1, D_MODEL), F32)] * 3
        + [jax.ShapeDtypeStruct(a.shape, a.dtype) for a in chip_sums],
        scratch_shapes=[pltpu.SemaphoreType.DMA((n * 3,)), pltpu.SemaphoreType.DMA((n * 3,)),
                        pltpu.SemaphoreType.DMA((n,))],
        compiler_params=_params(("arbitrary",)),
    )(*pieces, w_all_t, x, dy, ada3, norm_g, *chip_sums)
    return res[:4], res[4:]


def _sum_small(vec_all, qg_parts, kg_parts):
    def body(v_ref, q_ref, k_ref, tot_ref, gq_ref, gk_ref):
        tot = v_ref[0:1, :]
        for p in range(1, N_DEV):
            tot = tot + v_ref[p:p + 1, :]
        tot_ref[...] = tot
        gq_ref[...] = jnp.sum(q_ref[...], axis=0, keepdims=True)
        gk_ref[...] = jnp.sum(k_ref[...], axis=0, keepdims=True)

    n = vec_all.shape[-1]
    return pl.pallas_call(
        body, name="sum_small",
        out_shape=[jax.ShapeDtypeStruct((1, n), F32),
                   jax.ShapeDtypeStruct((1, HEAD_DIM), F32), jax.ShapeDtypeStruct((1, HEAD_DIM), F32)],
        compiler_params=_params(),
    )(vec_all, qg_parts, kg_parts)


def _grad_w_ada(c_cols, dada_rows):
    def body(c_ref, d_ref, out_ref):
        acc = c_ref[0] * d_ref[0]
        for b in range(1, N_DEV):
            acc = acc + c_ref[b] * d_ref[b]
        out_ref[...] = acc

    return pl.pallas_call(
        body, name="grad_w_ada",
        out_shape=jax.ShapeDtypeStruct((D_MODEL, ADA_SHARD), F32),
        compiler_params=_params(),
    )(c_cols, dada_rows)


def _adamw(w, m, v, g_parts, name):
    rows, cols = w.shape
    n_parts = g_parts.shape[0]
    tr = 256 if rows % 256 == 0 else rows
    tc = 256 if (tr == rows and rows > 256 and cols % 256 == 0) else cols
    c1 = 1.0 / (1.0 - ADAM_B1 ** ADAM_STEP)
    c2 = 1.0 / (1.0 - ADAM_B2 ** ADAM_STEP)

    def body(w_ref, m_ref, v_ref, g_ref, go_ref, d_ref, mo_ref, vo_ref):
        g = g_ref[0].astype(F32)
        for p in range(1, n_parts):
            g = g + g_ref[p].astype(F32)
        m_new = ADAM_B1 * m_ref[...] + (1.0 - ADAM_B1) * g
        v_new = ADAM_B2 * v_ref[...] + (1.0 - ADAM_B2) * (g * g)
        go_ref[...] = g
        mo_ref[...] = m_new
        vo_ref[...] = v_new
        d_ref[...] = -ADAM_LR * ((m_new * c1) / (jnp.sqrt(v_new * c2) + ADAM_EPS) + ADAM_WD * w_ref[...])

    blk = pl.BlockSpec((tr, tc), lambda i, j: (i, j))
    return pl.pallas_call(
        body, name=name, grid=(rows // tr, cols // tc),
        in_specs=[blk, blk, blk, pl.BlockSpec((n_parts, tr, tc), lambda i, j: (0, i, j))],
        out_specs=[blk] * 4,
        out_shape=[jax.ShapeDtypeStruct((rows, cols), F32)] * 4,
        compiler_params=_params(("parallel", "parallel")),
    )(w, m, v, g_parts)


_O_F = 1536


def _to_internal(wt_g):
    wf = wt_g.reshape(IN_WIDTH, D_MODEL)
    f = jnp.pad(wf[_O_F:_O_F + HEADS], ((0, N_FPAD - HEADS), (0, 0)))
    return jnp.concatenate([wf[:_O_F], wf[_O_F + HEADS:], f], axis=0)


def _slabs_by_core(dwt, dwt_f):
    sources = ((dwt, 0, _O_F, 0), (dwt_f, _O_F, _O_F + HEADS, _O_F), (dwt, _O_F + HEADS, IN_WIDTH, HEADS))

    def slab(p):
        lo, hi = p * IN_SHARD, (p + 1) * IN_SHARD
        parts = []
        for src, o_lo, o_hi, shift in sources:
            a, b = max(lo, o_lo), min(hi, o_hi)
            if a < b:
                parts.append(src[a - shift:b - shift])
        return parts[0] if len(parts) == 1 else jnp.concatenate(parts, axis=0)

    return jnp.stack([jnp.stack([slab(2 * chip + core) for chip in range(4)]) for core in range(2)])


def kernel(x, c, w_ada, b_ada, norm_g, w_in, b_f, q_norm_g, k_norm_g, conv_w, w_attn_out, w_conv_out, w_o, loss_target, m_w_ada, m_b_ada, m_norm_g, m_w_in, m_b_f, m_q_norm_g, m_k_norm_g, m_conv_w, m_w_attn_out, m_w_conv_out, m_w_o, v_w_ada, v_b_ada, v_norm_g, v_w_in, v_b_f, v_q_norm_g, v_k_norm_g, v_conv_w, v_w_attn_out, v_w_conv_out, v_w_o):
    me = 4 * lax.axis_index("x") + 2 * lax.axis_index("y") + lax.axis_index("c")
    s = x.shape[1]
    x2, t2 = x[0], loss_target[0]

    cw_g, wa_g, wb_g, wo_g, w_in_g = _gather_two_level(
        [conv_w[0], w_attn_out[0].astype(BF16), w_conv_out[0].astype(BF16), w_o[0].astype(BF16),
         w_in[0].T.astype(BF16)], "gather_weights")
    c_all, ada_g = _ada_exchange(c, w_ada[0])
    ada_mine = lax.dynamic_index_in_dim(ada_g[:, :, 0, :], me, axis=1, keepdims=False)
    ada3 = (ada_mine.reshape(1, 3 * D_MODEL) + b_ada).reshape(3, D_MODEL)
    w_all_t = _to_internal(w_in_g)
    wa = jnp.transpose(wa_g, (1, 0, 2)).reshape(ATTN_W, D_MODEL)
    wb = jnp.transpose(wb_g, (1, 0, 2)).reshape(CONV_W, D_MODEL)
    wo = wo_g.reshape(D_MODEL, D_MODEL)
    cw = jnp.transpose(cw_g, (1, 0, 2)).reshape(3, CONV_W)
    qg = jnp.tile(q_norm_g, (1, HEADS))
    kg = jnp.tile(k_norm_g, (1, HEADS))
    bf_pad = jnp.pad(b_f, ((0, 0), (0, LANES - HEADS)))

    proj, fl, h = _proj_fwd(x2, ada3, norm_g, w_all_t)
    qa, ka, va, kt, vt = _qkv_prep(proj, fl, bf_pad, qg, kg)
    attn, oa, qb = _attn_fwd(qa, ka, vt, proj)
    (dy, dgab, do, dza, dob, dwo, dwa, dwb, dgate, loss_part) = _tail(oa, attn, proj, x2, t2, ada3, wa, wb, wo, cw)

    dq, dk, dv = _attn_bwd(qb, ka, kt, va, do)
    dqkv, dqg, dkg, dcum = _qk_norm_bwd(dq, dk, dv, proj, qg, kg)
    df, dbf = _forget_bwd(dcum, fl, bf_pad)
    dcb, dcc, dcu, dcz, dcw = _conv_bwd(dob, proj, cw)
    pieces = [dqkv, dza, dcb, dcc, dcu, dcz, dgab, df]
    dw_main, dw_f = _dw_in(h, pieces)

    def by_core(slabs8):
        return jnp.swapaxes(slabs8.reshape((4, 2) + slabs8.shape[1:]), 0, 1).astype(BF16)

    slabs = [by_core(jnp.transpose(dwa.reshape(ATTN_W, N_DEV, LANES), (1, 0, 2))),
             by_core(jnp.transpose(dwb.reshape(CONV_W, N_DEV, LANES), (1, 0, 2))),
             by_core(dwo.reshape(N_DEV, D_MODEL // N_DEV, D_MODEL)),
             _slabs_by_core(dw_main, dw_f)]
    theirs = _sibling_swap(slabs, "swap_grads")
    core = lax.axis_index("c").astype(jnp.int32).reshape(1)
    chip_sums = [_pair_sum(m2, t4, core, "pair_sum_" + nm)
                 for m2, t4, nm in zip(slabs, theirs, ("wa", "wb", "wo", "w_in"))]
    (grad_x, dshift, dscale, dnormg), (g_wa_parts, g_wb_parts, g_wo_parts, g_in_parts) = _dh_and_dx(
        pieces, w_all_t, x2, dy, ada3, norm_g, chip_sums)
    vec = jnp.concatenate([dshift, dscale, dgate, dnormg, dbf, dcw.reshape(1, 3 * CONV_W), dqg, dkg], axis=1)
    (vec_all,) = _exchange([vec], True, "gather_small")
    vec_all = vec_all.reshape(N_DEV, vec.shape[1])
    n_main = 4 * D_MODEL + LANES + 3 * CONV_W
    tot, g_qg, g_kg = _sum_small(
        vec_all[:, :n_main],
        vec_all[:, n_main:n_main + ATTN_W].reshape(N_DEV * HEADS, HEAD_DIM),
        vec_all[:, n_main + ATTN_W:].reshape(N_DEV * HEADS, HEAD_DIM))
    g_b_ada = tot[:, 0:3 * D_MODEL]
    g_norm_g = tot[:, 3 * D_MODEL:4 * D_MODEL]
    g_b_f = tot[:, 4 * D_MODEL:4 * D_MODEL + HEADS]
    g_cw_full = tot[:, 4 * D_MODEL + LANES:].reshape(3, CONV_W)
    g_cw = lax.dynamic_slice(g_cw_full, (0, me * (CONV_W // N_DEV)), (3, CONV_W // N_DEV))
    dada_mine = lax.dynamic_slice(vec_all[:, 0:3 * D_MODEL], (0, me * ADA_SHARD), (N_DEV, ADA_SHARD))
    g_w_ada = _grad_w_ada(jnp.transpose(c_all, (0, 2, 1)), dada_mine.reshape(N_DEV, 1, ADA_SHARD))

    upd = {}
    upd["w_ada"] = _adamw(w_ada[0], m_w_ada[0], v_w_ada[0], g_w_ada[None], "adamw_w_ada")
    upd["b_ada"] = _adamw(b_ada, m_b_ada, v_b_ada, g_b_ada[None], "adamw_b_ada")
    upd["norm_g"] = _adamw(norm_g, m_norm_g, v_norm_g, g_norm_g[None], "adamw_norm_g")
    upd["w_in"] = [u.T for u in _adamw(w_in[0].T, m_w_in[0].T, v_w_in[0].T, g_in_parts, "adamw_w_in")]
    upd["b_f"] = _adamw(b_f, m_b_f, v_b_f, g_b_f[None], "adamw_b_f")
    upd["q_norm_g"] = _adamw(q_norm_g, m_q_norm_g, v_q_norm_g, g_qg[None], "adamw_q_norm_g")
    upd["k_norm_g"] = _adamw(k_norm_g, m_k_norm_g, v_k_norm_g, g_kg[None], "adamw_k_norm_g")
    upd["conv_w"] = _adamw(conv_w[0], m_conv_w[0], v_conv_w[0], g_cw[None], "adamw_conv_w")
    upd["w_attn_out"] = _adamw(w_attn_out[0], m_w_attn_out[0], v_w_attn_out[0], g_wa_parts, "adamw_w_attn_out")
    upd["w_conv_out"] = _adamw(w_conv_out[0], m_w_conv_out[0], v_w_conv_out[0], g_wb_parts, "adamw_w_conv_out")
    upd["w_o"] = _adamw(w_o[0], m_w_o[0], v_w_o[0], g_wo_parts, "adamw_w_o")

    names = ["w_ada", "b_ada", "norm_g", "w_in", "b_f", "q_norm_g", "k_norm_g", "conv_w",
             "w_attn_out", "w_conv_out", "w_o"]
    lead = {"w_ada", "w_in", "conv_w", "w_attn_out", "w_conv_out", "w_o"}
    fix = lambda n, a: a[None] if n in lead else a
    loss = lax.psum(loss_part[0, 0], ("x", "y", "c"))
    outs = [loss, grad_x[None]]
    for k in range(4):
        outs += [fix(n, upd[n][k]) for n in names]
    return tuple(outs)
```

```python
import functools

import numpy as np
import jax
import jax.numpy as jnp
from jax import lax
from jax.experimental import pallas as pl
from jax.experimental.pallas import tpu as pltpu

F32 = jnp.float32
BF16 = jnp.bfloat16

D_MODEL = 1024
HEADS = 8
HEAD_DIM = 64
ATTN_W = 512
CONV_W = 512
N_DEV = 8
IN_WIDTH = 6152
IN_SHARD = IN_WIDTH // N_DEV
N_MAIN = 6144
N_FPAD = 128
N_ALL = N_MAIN + N_FPAD
ADA_SHARD = 3 * D_MODEL // N_DEV
EPS = 1e-6
NEG = -1e30

ADAM_LR = 0.001
ADAM_B1 = 0.9
ADAM_B2 = 0.999
ADAM_EPS = 1e-08
ADAM_WD = 0.01
ADAM_STEP = 10

LANES = 128
VMEM_LIMIT = 56 * 1024 * 1024

TM_PROJ = 256
TN_PROJ = 1024
TM_ELEM = 512
TQ = 512
HEADS_PER_STEP = 4
HEADS_PER_STEP_FWD = 8
TM_TAIL = 256
TC_CUM = 256
TK_DW = 1024
TN_DW = 512
TM_DH = 256
HALO = 16

OFF_Q, OFF_K, OFF_V, OFF_ZA, OFF_CB, OFF_CC, OFF_CU, OFF_CZ, OFF_GA, OFF_GB = (
    0, 512, 1024, 1536, 2048, 2560, 3072, 3584, 4096, 5120)


def _params(sem=None):
    return pltpu.CompilerParams(dimension_semantics=sem, vmem_limit_bytes=VMEM_LIMIT)


def _dot(a, b):
    return jnp.dot(a, b, preferred_element_type=F32)


def _dot_nt(a, b):
    return lax.dot_general(a, b, (((1,), (1,)), ((), ())), preferred_element_type=F32)


def _dot_tn(a, b):
    return lax.dot_general(a, b, (((0,), (0,)), ((), ())), preferred_element_type=F32)


def _sigmoid(x):
    return 1.0 / (1.0 + jnp.exp(-x))


def _lane_lo(shape):
    return lax.broadcasted_iota(jnp.int32, shape, len(shape) - 1) < HEAD_DIM


def _seg_sum(z, lo):
    a = jnp.sum(jnp.where(lo, z, 0.0), axis=-1, keepdims=True)
    b = jnp.sum(jnp.where(lo, 0.0, z), axis=-1, keepdims=True)
    return jnp.where(lo, a, b)


def _lane_col(z, lane):
    idx = lax.broadcasted_iota(jnp.int32, z.shape, 1)
    return jnp.sum(jnp.where(idx == lane, z, 0.0), axis=-1, keepdims=True)


def _sub_row(z, row):
    idx = lax.broadcasted_iota(jnp.int32, z.shape, 0)
    return jnp.sum(jnp.where(idx == row, z, 0.0), axis=0, keepdims=True)


def _mesh_pos():
    x, y, c = lax.axis_index("x"), lax.axis_index("y"), lax.axis_index("c")
    return x, y, c, 4 * x + 2 * y + c


def _peer(k, x, y, c):
    px = 1 - x if (k >> 2) & 1 else x
    py = 1 - y if (k >> 1) & 1 else y
    pc = 1 - c if k & 1 else c
    return (px, py, pc), 4 * px + 2 * py + pc


def _gather_copies(ins, outs, send_sems, recv_sems, local_sems):
    x, y, c, me = _mesh_pos()
    copies = []
    for a in range(len(ins)):
        copies.append(pltpu.make_async_copy(ins[a], outs[a].at[me], local_sems.at[a]))
        for k in range(1, N_DEV):
            dev, _ = _peer(k, x, y, c)
            copies.append(pltpu.make_async_remote_copy(
                src_ref=ins[a], dst_ref=outs[a].at[me],
                send_sem=send_sems.at[a * (N_DEV - 1) + k - 1], recv_sem=recv_sems.at[a * (N_DEV - 1) + k - 1],
                device_id=dev, device_id_type=pl.DeviceIdType.MESH))
    return copies


def _gather_sems(n):
    return [pltpu.SemaphoreType.DMA((n * (N_DEV - 1),)), pltpu.SemaphoreType.DMA((n * (N_DEV - 1),)),
            pltpu.SemaphoreType.DMA((n,))]


def _gather_direct(arrs, name):
    n = len(arrs)
    any_spec = pl.BlockSpec(memory_space=pl.ANY)

    def body(*refs):
        copies = _gather_copies(refs[:n], refs[n:2 * n], *refs[2 * n:])
        for cp in copies:
            cp.start()
        for cp in copies:
            cp.wait()

    return pl.pallas_call(
        body, name=name, out_shape=[jax.ShapeDtypeStruct((N_DEV,) + a.shape, a.dtype) for a in arrs],
        in_specs=[any_spec] * n, out_specs=[any_spec] * n, scratch_shapes=_gather_sems(n),
    )(*arrs)


def _gather_two_level(arrs, name):
    n = len(arrs)
    any_spec = pl.BlockSpec(memory_space=pl.ANY)
    per = N_DEV - 1

    def body(*refs):
        ins, outs = refs[:n], refs[n:2 * n]
        send_sems, recv_sems, local_sems = refs[2 * n:]
        x, y, c, me = _mesh_pos()
        sibling = (x, y, 1 - c)
        chips = [(1 - x, y), (x, 1 - y), (1 - x, 1 - y)]

        def copy(a, k, src, blk, to):
            return pltpu.make_async_remote_copy(
                src_ref=src, dst_ref=outs[a].at[blk],
                send_sem=send_sems.at[a * per + k], recv_sem=recv_sems.at[a * per + k],
                device_id=to, device_id_type=pl.DeviceIdType.MESH)

        local = [pltpu.make_async_copy(ins[a], outs[a].at[me], local_sems.at[a]) for a in range(n)]
        for cp in local:
            cp.start()
        first = []
        for a in range(n):
            first.append(copy(a, 0, ins[a], me, sibling))
            first += [copy(a, 1 + j, ins[a], me, (px, py, c)) for j, (px, py) in enumerate(chips)]
        for cp in first:
            cp.start()
        passed = []
        for j, (px, py) in enumerate(chips):
            blk = 4 * px + 2 * py + c
            for a in range(n):
                copy(a, 1 + j, ins[a], blk, (x, y, c)).wait_recv()
                fwd = copy(a, 4 + j, outs[a].at[blk], blk, sibling)
                fwd.start()
                passed.append(fwd)
        for a in range(n):
            copy(a, 0, ins[a], 4 * x + 2 * y + 1 - c, (x, y, c)).wait_recv()
            for j, (px, py) in enumerate(chips):
                copy(a, 4 + j, ins[a], 4 * px + 2 * py + 1 - c, (x, y, c)).wait_recv()
        for cp in first + passed:
            cp.wait_send()
        for cp in local:
            cp.wait()

    return pl.pallas_call(
        body, name=name,
        out_shape=[jax.ShapeDtypeStruct((N_DEV,) + a.shape, a.dtype) for a in arrs],
        in_specs=[any_spec] * n, out_specs=[any_spec] * n,
        scratch_shapes=[pltpu.SemaphoreType.DMA((n * per,)), pltpu.SemaphoreType.DMA((n * per,)),
                        pltpu.SemaphoreType.DMA((n,))],
    )(*arrs)


def _sibling_swap(arrs, name):
    n = len(arrs)
    any_spec = pl.BlockSpec(memory_space=pl.ANY)

    def body(*refs):
        ins, outs = refs[:n], refs[n:2 * n]
        send_sems, recv_sems = refs[2 * n:]
        x, y, c, _ = _mesh_pos()
        copies = [pltpu.make_async_remote_copy(
            src_ref=ins[a].at[1 - c], dst_ref=outs[a], send_sem=send_sems.at[a], recv_sem=recv_sems.at[a],
            device_id=(x, y, 1 - c), device_id_type=pl.DeviceIdType.MESH) for a in range(n)]
        for cp in copies:
            cp.start()
        for cp in copies:
            cp.wait()

    return pl.pallas_call(
        body, name=name,
        out_shape=[jax.ShapeDtypeStruct(a.shape[1:], a.dtype) for a in arrs],
        in_specs=[any_spec] * n, out_specs=[any_spec] * n,
        scratch_shapes=[pltpu.SemaphoreType.DMA((n,)), pltpu.SemaphoreType.DMA((n,))],
    )(*arrs)


def _pair_sum(mine2, theirs, core, name):
    _, _, rows, cols = mine2.shape
    tr = 256 if rows % 256 == 0 else rows

    def body(core_ref, a_ref, b_ref, out_ref):
        out_ref[...] = (a_ref[...].astype(F32) + b_ref[...].astype(F32)).astype(BF16)

    return pl.pallas_call(
        body, name=name,
        grid_spec=pltpu.PrefetchScalarGridSpec(
            num_scalar_prefetch=1, grid=(4, rows // tr),
            in_specs=[pl.BlockSpec((None, None, tr, cols), lambda ch, i, core_: (core_[0], ch, i, 0)),
                      pl.BlockSpec((None, tr, cols), lambda ch, i, core_: (ch, i, 0))],
            out_specs=pl.BlockSpec((None, tr, cols), lambda ch, i, core_: (ch, i, 0))),
        out_shape=jax.ShapeDtypeStruct(theirs.shape, BF16),
        compiler_params=_params(("parallel", "parallel")),
    )(core, mine2, theirs)


def _chip_copies(ins, outs, send_sems, recv_sems, local_sems):
    x, y, c, _ = _mesh_pos()
    my_chip = 2 * x + y
    chips = [(1 - x, y), (x, 1 - y), (1 - x, 1 - y)]
    copies = []
    for a in range(len(ins)):
        copies.append(pltpu.make_async_copy(ins[a].at[my_chip], outs[a].at[my_chip], local_sems.at[a]))
        for j, (px, py) in enumerate(chips):
            copies.append(pltpu.make_async_remote_copy(
                src_ref=ins[a].at[2 * px + py], dst_ref=outs[a].at[my_chip],
                send_sem=send_sems.at[a * 3 + j], recv_sem=recv_sems.at[a * 3 + j],
                device_id=(px, py, c), device_id_type=pl.DeviceIdType.MESH))
    return copies


def _ada_exchange(c_row, w_ada_sh):
    def body(c_ref, w_ref, call_ref, adag_ref, mine_ref, send_sems, recv_sems):
        x, y, c, me = _mesh_pos()

        def copy(phase, k, src, dst):
            dev, _ = _peer(k, x, y, c)
            return pltpu.make_async_remote_copy(
                src_ref=src, dst_ref=dst,
                send_sem=send_sems.at[phase * (N_DEV - 1) + k - 1],
                recv_sem=recv_sems.at[phase * (N_DEV - 1) + k - 1],
                device_id=dev, device_id_type=pl.DeviceIdType.MESH)

        call_ref[me] = c_ref[...]
        first = [copy(0, k, c_ref, call_ref.at[me]) for k in range(1, N_DEV)]
        for cp in first:
            cp.start()
        for cp in first:
            cp.wait()
        wb = w_ref[...].astype(BF16)
        for b in range(N_DEV):
            row = jnp.broadcast_to(call_ref[b], (8, D_MODEL)).astype(BF16)
            mine_ref[b] = _sub_row(_dot(row, wb), 0)
        adag_ref[me] = mine_ref[...]
        second = [copy(1, k, mine_ref, adag_ref.at[me]) for k in range(1, N_DEV)]
        for cp in second:
            cp.start()
        for cp in second:
            cp.wait()

    vm = pl.BlockSpec(memory_space=pltpu.VMEM)
    return pl.pallas_call(
        body, name="ada_exchange",
        out_shape=[jax.ShapeDtypeStruct((N_DEV, 1, D_MODEL), F32),
                   jax.ShapeDtypeStruct((N_DEV, N_DEV, 1, ADA_SHARD), F32)],
        in_specs=[vm, vm], out_specs=[vm, vm],
        scratch_shapes=[pltpu.VMEM((N_DEV, 1, ADA_SHARD), F32),
                        pltpu.SemaphoreType.DMA((2 * (N_DEV - 1),)),
                        pltpu.SemaphoreType.DMA((2 * (N_DEV - 1),))],
        compiler_params=pltpu.CompilerParams(vmem_limit_bytes=VMEM_LIMIT),
    )(c_row, w_ada_sh)


def _proj_fwd(x, ada3, norm_g, w_all_t, later):
    s = x.shape[0]
    tm, tn = min(TM_PROJ, s), TN_PROJ
    nt = s // tm
    n = len(later)

    def body(x_ref, ada_ref, g_ref, wt_ref, *rest):
        ins, (proj_ref, fl_ref, h_ref), rest = rest[:n], rest[n:n + 3], rest[n + 3:]
        outs, sems = rest[:n], rest[n:]
        i = pl.program_id(0)

        @pl.when(i == 0)
        def _():
            for cp in _gather_copies(ins, outs, *sems):
                cp.start()

        xv = x_ref[...]
        r = lax.rsqrt(jnp.mean(xv * xv, axis=-1, keepdims=True) + EPS)
        hv = ((xv * r) * g_ref[...]) * (1.0 + ada_ref[1:2, :]) + ada_ref[0:1, :]
        hb = hv.astype(BF16)
        h_ref[...] = hb
        fl_ref[...] = _dot_nt(hb, wt_ref[N_MAIN:N_ALL, :])
        for j in range(N_MAIN // tn):
            proj_ref[:, j * tn:(j + 1) * tn] = _dot_nt(hb, wt_ref[j * tn:(j + 1) * tn, :]).astype(BF16)

        @pl.when(i == nt - 1)
        def _():
            for cp in _gather_copies(ins, outs, *sems):
                cp.wait()

    any_spec = pl.BlockSpec(memory_space=pl.ANY)
    res = pl.pallas_call(
        body, name="proj_fwd", grid=(nt,),
        in_specs=[pl.BlockSpec((tm, D_MODEL), lambda i: (i, 0)),
                  pl.BlockSpec((3, D_MODEL), lambda i: (0, 0)),
                  pl.BlockSpec((1, D_MODEL), lambda i: (0, 0)),
                  pl.BlockSpec((N_ALL, D_MODEL), lambda i: (0, 0))] + [any_spec] * n,
        out_specs=[pl.BlockSpec((tm, N_MAIN), lambda i: (i, 0)),
                   pl.BlockSpec((tm, N_FPAD), lambda i: (i, 0)),
                   pl.BlockSpec((tm, D_MODEL), lambda i: (i, 0))] + [any_spec] * n,
        out_shape=[jax.ShapeDtypeStruct((s, N_MAIN), BF16),
                   jax.ShapeDtypeStruct((s, N_FPAD), F32),
                   jax.ShapeDtypeStruct((s, D_MODEL), BF16)]
        + [jax.ShapeDtypeStruct((N_DEV,) + a.shape, a.dtype) for a in later],
        scratch_shapes=_gather_sems(n),
        compiler_params=_params(("arbitrary",)),
    )(x, ada3, norm_g, w_all_t, *later)
    return res[:3], res[3:]


L_ONE_Q, L_F_Q, L_LSE_Q, L_END = HEAD_DIM, HEAD_DIM + 3, HEAD_DIM + 6, HEAD_DIM + 9


def _split3(f):
    hi = f.astype(BF16).astype(F32)
    r = f - hi
    mid = r.astype(BF16).astype(F32)
    return hi, mid, r - mid


def _place3(lane, first, parts, otherwise):
    a, b, c = parts
    return jnp.where(lane == first, a, jnp.where(lane == first + 1, b, jnp.where(lane == first + 2, c, otherwise)))


def _log_forget(fl, bf):
    z = fl + bf
    lf = jnp.minimum(z, 0.0) - jnp.log1p(jnp.exp(-jnp.abs(z)))
    lane = lax.broadcasted_iota(jnp.int32, z.shape, 1)
    return jnp.where(lane < HEADS, lf, 0.0)


def _qkv_prep(proj, fl, bf_pad, qg, kg):
    s = proj.shape[0]
    tm = min(TM_ELEM, s)
    scale = HEAD_DIM ** -0.5

    def body(p_ref, fl_ref, bf_ref, qg_ref, kg_ref, qa_ref, ka_ref, va_ref, kt_ref, vt_ref, carry):
        @pl.when(pl.program_id(0) == 0)
        def _():
            carry[...] = jnp.zeros_like(carry)
        tri = (lax.broadcasted_iota(jnp.int32, (tm, tm), 1) <= lax.broadcasted_iota(jnp.int32, (tm, tm), 0)).astype(F32)
        cum_v = jnp.dot(tri, _log_forget(fl_ref[...], bf_ref[...]), preferred_element_type=F32,
                        precision=lax.Precision.HIGHEST) + carry[...]
        carry[...] = _sub_row(cum_v, tm - 1)
        lane = lax.broadcasted_iota(jnp.int32, (tm, LANES), 1)
        lo = lane < HEAD_DIM
        v_tail = jnp.where(lane < L_F_Q, 1.0, 0.0)
        for pr in range(ATTN_W // LANES):
            sl = slice(pr * LANES, (pr + 1) * LANES)
            q2 = p_ref[:, OFF_Q + pr * LANES:OFF_Q + (pr + 1) * LANES].astype(F32)
            k2 = p_ref[:, OFF_K + pr * LANES:OFF_K + (pr + 1) * LANES].astype(F32)
            v2 = p_ref[:, OFF_V + pr * LANES:OFF_V + (pr + 1) * LANES].astype(F32)
            rq = lax.rsqrt(_seg_sum(q2 * q2, lo) * (1.0 / HEAD_DIM) + EPS)
            rk = lax.rsqrt(_seg_sum(k2 * k2, lo) * (1.0 / HEAD_DIM) + EPS)
            qn = ((q2 * rq) * qg_ref[:, sl]) * scale
            kn = (k2 * rk) * kg_ref[:, sl]
            for hh in range(2):
                h = 2 * pr + hh
                f3 = _split3(_lane_col(cum_v, h))
                qh = qn if hh == 0 else pltpu.roll(qn, HEAD_DIM, 1)
                kh = kn if hh == 0 else pltpu.roll(kn, HEAD_DIM, 1)
                vh = v2 if hh == 0 else pltpu.roll(v2, HEAD_DIM, 1)
                q_tail = jnp.where(lane < L_F_Q, 1.0, _place3(lane, L_F_Q, f3, 0.0))
                k_tail = _place3(lane, L_ONE_Q, tuple(-f for f in f3), jnp.where(lane < L_END, 1.0, 0.0))
                k_row = jnp.where(lo, kh, k_tail)
                v_row = jnp.where(lo, vh, v_tail)
                qa_ref[h] = jnp.where(lo, qh, q_tail).astype(BF16)
                ka_ref[h] = k_row.astype(BF16)
                va_ref[h] = v_row.astype(BF16)
                kt_ref[h] = k_row.T.astype(BF16)
                vt_ref[h] = v_row.T.astype(BF16)

    heads = pl.BlockSpec((HEADS, tm, LANES), lambda i: (0, i, 0))
    heads_t = pl.BlockSpec((HEADS, LANES, tm), lambda i: (0, 0, i))
    vec = pl.BlockSpec((1, ATTN_W), lambda i: (0, 0))
    return pl.pallas_call(
        body, name="qkv_prep", grid=(s // tm,),
        in_specs=[pl.BlockSpec((tm, 3 * ATTN_W), lambda i: (i, 0)),
                  pl.BlockSpec((tm, LANES), lambda i: (i, 0)),
                  pl.BlockSpec((1, LANES), lambda i: (0, 0)), vec, vec],
        out_specs=[heads, heads, heads, heads_t, heads_t],
        out_shape=[jax.ShapeDtypeStruct((HEADS, s, LANES), BF16)] * 3
        + [jax.ShapeDtypeStruct((HEADS, LANES, s), BF16)] * 2,
        scratch_shapes=[pltpu.VMEM((1, LANES), F32)],
        compiler_params=_params(("arbitrary",)),
    )(proj, fl, bf_pad, qg, kg)


def _causal_t(t):
    return lax.broadcasted_iota(jnp.int32, (t, t), 0) <= lax.broadcasted_iota(jnp.int32, (t, t), 1)


def _tri_steps(nt, q_major):
    if q_major:
        pairs = [(i, j) for i in range(nt) for j in range(i + 1)]
    else:
        pairs = [(i, j) for j in range(nt) for i in range(j, nt)]
    return (jnp.asarray(np.array([p[0] for p in pairs], np.int32)),
            jnp.asarray(np.array([p[1] for p in pairs], np.int32)))


def _attn_fwd(qa, ka, vt, proj):
    s = qa.shape[1]
    t = min(TQ, s)
    it, jt = _tri_steps(s // t, True)
    hp = HEADS_PER_STEP_FWD
    wide = hp * HEAD_DIM
    za_blk = OFF_ZA // wide

    def body(it_ref, jt_ref, q_ref, k_ref, vt_ref, za_ref, attn_ref, oa_ref, qb_ref, m_s, acc_s, pair_s):
        step = pl.program_id(1)
        i, j = it_ref[step], jt_ref[step]

        @pl.when(j == 0)
        def _():
            m_s[...] = jnp.full_like(m_s, NEG)
            acc_s[...] = jnp.zeros_like(acc_s)

        def update(masked):
            for hh in range(hp):
                st = _dot_nt(k_ref[hh], q_ref[hh])
                if masked:
                    st = jnp.where(_causal_t(t), st, NEG)
                m_prev = m_s[hh]
                m_next = jnp.maximum(m_prev, jnp.max(st, axis=0, keepdims=True))
                alpha = jnp.exp(m_prev - m_next)
                pt = jnp.exp(st - m_next).astype(BF16)
                acc_s[hh] = acc_s[hh] * alpha + _dot(vt_ref[hh], pt)
                m_s[hh] = m_next

        @pl.when(j < i)
        def _():
            update(False)

        @pl.when(j == i)
        def _():
            update(True)
            row = lax.broadcasted_iota(jnp.int32, (LANES, t), 0)
            lane = lax.broadcasted_iota(jnp.int32, (t, LANES), 1)
            for hh in range(hp):
                l_row = acc_s[hh, L_ONE_Q:L_ONE_Q + 1, :]
                pair_s[hh * HEAD_DIM:(hh + 1) * HEAD_DIM, :] = acc_s[hh, 0:HEAD_DIM, :] / l_row
                lse3 = _split3(m_s[hh] + jnp.log(l_row))
                tail_t = _place3(row, L_LSE_Q, tuple(-x for x in lse3), 0.0)
                keep_q = jnp.logical_or(lane < L_LSE_Q, lane >= L_END)
                qb_ref[hh] = jnp.where(keep_q, q_ref[hh].astype(F32), tail_t.T).astype(BF16)
            out = pair_s[...].T
            attn_ref[...] = out
            z = za_ref[...].astype(F32)
            oa_ref[...] = (out * (z * _sigmoid(z))).astype(BF16)

    pair_q = pl.BlockSpec((hp, t, LANES), lambda p, n, it_, jt_: (p, it_[n], 0))
    pair_k = pl.BlockSpec((hp, t, LANES), lambda p, n, it_, jt_: (p, jt_[n], 0))
    pair_kt = pl.BlockSpec((hp, LANES, t), lambda p, n, it_, jt_: (p, 0, jt_[n]))
    out_q = pl.BlockSpec((t, wide), lambda p, n, it_, jt_: (it_[n], p))
    return pl.pallas_call(
        body, name="attn_fwd",
        grid_spec=pltpu.PrefetchScalarGridSpec(
            num_scalar_prefetch=2, grid=(HEADS // hp, it.shape[0]),
            in_specs=[pair_q, pair_k, pair_kt,
                      pl.BlockSpec((t, wide), lambda p, n, it_, jt_: (it_[n], za_blk + p))],
            out_specs=[out_q, out_q, pair_q],
            scratch_shapes=[pltpu.VMEM((hp, 1, t), F32), pltpu.VMEM((hp, LANES, t), F32),
                            pltpu.VMEM((wide, t), F32)]),
        out_shape=[jax.ShapeDtypeStruct((s, ATTN_W), F32),
                   jax.ShapeDtypeStruct((s, ATTN_W), BF16),
                   jax.ShapeDtypeStruct((HEADS, s, LANES), BF16)],
        compiler_params=_params(("parallel", "arbitrary")),
    )(it, jt, qa, ka, vt, proj)


def _conv_parts(gb_ref, gc_ref, u_ref, zb_ref, gch_ref, uh_ref, first, w_ref, tm):
    gb, gc = gb_ref[...].astype(F32), gc_ref[...].astype(F32)
    u, zb = u_ref[...].astype(F32), zb_ref[...].astype(F32)
    cu = gc * u
    cu_h = jnp.where(first, 0.0, gch_ref[...].astype(F32) * uh_ref[...].astype(F32))
    prev1, prev2 = _sub_row(cu_h, HALO - 1), _sub_row(cu_h, HALO - 2)
    row = lax.broadcasted_iota(jnp.int32, cu.shape, 0)
    r1 = jnp.where(row == 0, prev1, pltpu.roll(cu, 1, 0))
    r2 = jnp.where(row == 0, prev2, jnp.where(row == 1, prev1, pltpu.roll(cu, 2, 0)))
    conv = w_ref[2:3, :] * cu + w_ref[1:2, :] * r1 + w_ref[0:1, :] * r2
    return gb, gc, u, zb, cu, r1, r2, conv


def _conv_specs(tm, s, width=LANES):
    def tile(off):
        return pl.BlockSpec((tm, width), lambda c, i: (i, off // width + c))

    def before(off):
        return pl.BlockSpec((HALO, width), lambda c, i: (jnp.maximum(i * (tm // HALO) - 1, 0), off // width + c))

    def after(off):
        return pl.BlockSpec((HALO, width),
                            lambda c, i: (jnp.minimum((i + 1) * (tm // HALO), s // HALO - 1), off // width + c))

    return ([tile(OFF_CB), tile(OFF_CC), tile(OFF_CU), tile(OFF_CZ)], [before(OFF_CC), before(OFF_CU)],
            [after(OFF_CB), after(OFF_CZ)])


def _tail(oa, attn, proj, x, target, ada3, wa, wb, wo, conv_w):
    s = x.shape[0]
    tm = min(TM_TAIL, s)
    gab_blk = OFF_GA // (2 * D_MODEL)
    za_blk = OFF_ZA // ATTN_W
    tiles, befores, _ = _conv_specs(tm, s, CONV_W)

    def body(oa_ref, attn_ref, za_ref, gb_ref, gc_ref, u_ref, zb_ref, gch_ref, uh_ref, cw_ref, gab_ref, x_ref, t_ref,
             ada_ref, wa_ref, wb_ref, wo_ref,
             dy_ref, dgab_ref, do_ref, dza_ref, dob_ref, dwo_ref, dwa_ref, dwb_ref, dgate_ref, loss_ref):
        first = pl.program_id(0) == 0

        @pl.when(first)
        def _():
            dwo_ref[...] = jnp.zeros_like(dwo_ref)
            dwa_ref[...] = jnp.zeros_like(dwa_ref)
            dwb_ref[...] = jnp.zeros_like(dwb_ref)
            dgate_ref[...] = jnp.zeros_like(dgate_ref)
            loss_ref[...] = jnp.zeros_like(loss_ref)

        gb, _, _, zb, _, _, _, conv = _conv_parts(gb_ref, gc_ref, u_ref, zb_ref, gch_ref, uh_ref, first, cw_ref, tm)
        ob_v = (gb * conv * (zb * _sigmoid(zb))).astype(BF16)
        oa_v = oa_ref[...]
        wa_v, wb_v, wo_v = wa_ref[...], wb_ref[...], wo_ref[...]
        a2 = _dot(oa_v, wa_v)
        b2 = _dot(ob_v, wb_v)
        sa = _sigmoid(gab_ref[:, 0:D_MODEL].astype(F32))
        sb = _sigmoid(gab_ref[:, D_MODEL:2 * D_MODEL].astype(F32))
        mb = (sa * a2 + sb * b2).astype(BF16)
        mo = _dot(mb, wo_v)
        gate = ada_ref[2:3, :]
        err = (x_ref[...] + gate * mo) - t_ref[...]
        dy = err * (1.0 / D_MODEL)
        dy_ref[...] = dy
        loss_ref[...] += 0.5 * jnp.sum(err * err) * (1.0 / D_MODEL)
        dgate_ref[...] += jnp.sum(dy * mo, axis=0, keepdims=True)
        dmo = (dy * gate).astype(BF16)
        dmerged = _dot_nt(dmo, wo_v)
        dwo_ref[...] += _dot_tn(mb, dmo)
        da2 = (dmerged * sa).astype(BF16)
        db2 = (dmerged * sb).astype(BF16)
        dgab_ref[:, 0:D_MODEL] = (dmerged * a2 * (sa * (1.0 - sa))).astype(BF16)
        dgab_ref[:, D_MODEL:2 * D_MODEL] = (dmerged * b2 * (sb * (1.0 - sb))).astype(BF16)
        doa = _dot_nt(da2, wa_v)
        dob_ref[...] = _dot_nt(db2, wb_v)
        dwa_ref[...] += _dot_tn(oa_v, da2)
        dwb_ref[...] += _dot_tn(ob_v, db2)

        lane = lax.broadcasted_iota(jnp.int32, (tm, LANES), 1)
        lo = lane < HEAD_DIM
        for pr in range(ATTN_W // LANES):
            sl = slice(pr * LANES, (pr + 1) * LANES)
            g, a, z = doa[:, sl], attn_ref[:, sl], za_ref[:, sl].astype(F32)
            sg = _sigmoid(z)
            dat = (g * (z * sg)).astype(BF16).astype(F32)
            prod = dat * a
            dza_ref[:, sl] = (g * a * (sg * (1.0 + z * (1.0 - sg)))).astype(BF16)
            for hh in range(2):
                sel = lo if hh == 0 else jnp.logical_not(lo)
                delta3 = _split3(jnp.sum(jnp.where(sel, prod, 0.0), axis=-1, keepdims=True))
                dh = dat if hh == 0 else pltpu.roll(dat, HEAD_DIM, 1)
                tail_lanes = _place3(lane, L_ONE_Q, tuple(-d for d in delta3), 0.0)
                do_ref[2 * pr + hh] = jnp.where(lo, dh, tail_lanes).astype(BF16)

    half = pl.BlockSpec((tm, ATTN_W), lambda i: (i, 0))
    full = pl.BlockSpec((tm, D_MODEL), lambda i: (i, 0))

    def const(shape):
        return pl.BlockSpec(shape, lambda i: (0, 0))

    def one_axis(spec):
        return pl.BlockSpec(spec.block_shape, lambda i, f=spec.index_map: f(0, i))

    return pl.pallas_call(
        body, name="tail", grid=(s // tm,),
        in_specs=[half, half, pl.BlockSpec((tm, ATTN_W), lambda i: (i, za_blk))]
        + [one_axis(sp) for sp in tiles + befores]
        + [const((3, CONV_W)), pl.BlockSpec((tm, 2 * D_MODEL), lambda i: (i, gab_blk)), full, full,
           const((3, D_MODEL)), const((ATTN_W, D_MODEL)), const((CONV_W, D_MODEL)), const((D_MODEL, D_MODEL))],
        out_specs=[full, pl.BlockSpec((tm, 2 * D_MODEL), lambda i: (i, 0)),
                   pl.BlockSpec((HEADS, tm, LANES), lambda i: (0, i, 0)), half, half,
                   const((D_MODEL, D_MODEL)), const((ATTN_W, D_MODEL)), const((CONV_W, D_MODEL)),
                   const((1, D_MODEL)), const((1, LANES))],
        out_shape=[jax.ShapeDtypeStruct((s, D_MODEL), F32),
                   jax.ShapeDtypeStruct((s, 2 * D_MODEL), BF16),
                   jax.ShapeDtypeStruct((HEADS, s, LANES), BF16),
                   jax.ShapeDtypeStruct((s, ATTN_W), BF16),
                   jax.ShapeDtypeStruct((s, CONV_W), F32),
                   jax.ShapeDtypeStruct((D_MODEL, D_MODEL), F32),
                   jax.ShapeDtypeStruct((ATTN_W, D_MODEL), F32),
                   jax.ShapeDtypeStruct((CONV_W, D_MODEL), F32),
                   jax.ShapeDtypeStruct((1, D_MODEL), F32),
                   jax.ShapeDtypeStruct((1, LANES), F32)],
        compiler_params=_params(("arbitrary",)),
    )(oa, attn, proj, *([proj] * 6), conv_w, proj, x, target, ada3, wa, wb, wo)


def _attn_bwd(qb, ka, kt, va, do):
    s = qb.shape[1]
    t = min(TQ, s)
    nt = s // t
    hp = HEADS_PER_STEP
    it, jt = _tri_steps(nt, False)

    def body(it_ref, jt_ref, q_ref, k_ref, kt_ref, v_ref, do_ref, dq_ref, dk_ref, dv_ref, dqt_s):
        step = pl.program_id(1)
        i, j = it_ref[step], jt_ref[step]

        @pl.when(step == 0)
        def _():
            dqt_s[...] = jnp.zeros_like(dqt_s)

        @pl.when(i == j)
        def _():
            dk_ref[...] = jnp.zeros_like(dk_ref)
            dv_ref[...] = jnp.zeros_like(dv_ref)

        def update(masked):
            for hh in range(hp):
                qh, doh = q_ref[hh], do_ref[hh]
                st = _dot_nt(k_ref[hh], qh)
                if masked:
                    st = jnp.where(_causal_t(t), st, NEG)
                pt = jnp.exp(st)
                dst = (pt * _dot_nt(v_ref[hh], doh)).astype(BF16)
                dv_ref[hh] += _dot(pt.astype(BF16), doh)
                dk_ref[hh] += _dot(dst, qh)
                dqt_s[hh, i] += _dot(kt_ref[hh], dst)

        @pl.when(i > j)
        def _():
            update(False)

        @pl.when(i == j)
        def _():
            update(True)
            for hh in range(hp):
                dq_ref[hh] = dqt_s[hh, i].T

    pair_q = pl.BlockSpec((hp, t, LANES), lambda p, n, it_, jt_: (p, it_[n], 0))
    pair_k = pl.BlockSpec((hp, t, LANES), lambda p, n, it_, jt_: (p, jt_[n], 0))
    pair_kt = pl.BlockSpec((hp, LANES, t), lambda p, n, it_, jt_: (p, 0, jt_[n]))
    return pl.pallas_call(
        body, name="attn_bwd",
        grid_spec=pltpu.PrefetchScalarGridSpec(
            num_scalar_prefetch=2, grid=(HEADS // hp, it.shape[0]),
            in_specs=[pair_q, pair_k, pair_kt, pair_k, pair_q],
            out_specs=[pair_k, pair_k, pair_k],
            scratch_shapes=[pltpu.VMEM((hp, nt, LANES, t), F32)]),
        out_shape=[jax.ShapeDtypeStruct((HEADS, s, LANES), F32)] * 3,
        compiler_params=_params(("parallel", "arbitrary")),
    )(it, jt, qb, ka, kt, va, do)


def _forget_bwd(dcum, fl, bf_pad):
    s = fl.shape[0]
    tc = min(TC_CUM, s)
    n = s // tc

    def body(dc_ref, fl_ref, bf_ref, df_ref, dbf_ref, carry):
        @pl.when(pl.program_id(0) == 0)
        def _():
            carry[...] = jnp.zeros_like(carry)
            dbf_ref[...] = jnp.zeros_like(dbf_ref)
        r = lax.broadcasted_iota(jnp.int32, (tc, tc), 0)
        cidx = lax.broadcasted_iota(jnp.int32, (tc, tc), 1)
        tri = (cidx >= r).astype(F32)
        dc = dc_ref[...]
        dlf = jnp.dot(tri, dc, preferred_element_type=F32, precision=lax.Precision.HIGHEST) + carry[...]
        carry[...] += jnp.sum(dc, axis=0, keepdims=True)
        lane = lax.broadcasted_iota(jnp.int32, (tc, LANES), 1)
        dfl = jnp.where(lane < HEADS, dlf * _sigmoid(-(fl_ref[...] + bf_ref[...])), 0.0)
        df_ref[...] = dfl.astype(BF16)
        dbf_ref[...] += jnp.sum(dfl, axis=0, keepdims=True)

    rev = pl.BlockSpec((tc, LANES), lambda i: (n - 1 - i, 0))
    vec = pl.BlockSpec((1, LANES), lambda i: (0, 0))
    return pl.pallas_call(
        body, name="forget_bwd", grid=(n,),
        in_specs=[rev, rev, vec], out_specs=[rev, vec],
        out_shape=[jax.ShapeDtypeStruct((s, LANES), BF16), jax.ShapeDtypeStruct((1, LANES), F32)],
        scratch_shapes=[pltpu.VMEM((1, LANES), F32)],
        compiler_params=_params(("arbitrary",)),
    )(dcum, fl, bf_pad)


def _qk_norm_bwd(dq, dk, dv, proj, qg, kg):
    s = dv.shape[1]
    tm = min(TM_ELEM, s)
    scale = HEAD_DIM ** -0.5

    def body(dq_ref, dk_ref, dv_ref, p_ref, qg_ref, kg_ref, out_ref, dqg_ref, dkg_ref, dcum_ref):
        @pl.when(pl.program_id(0) == 0)
        def _():
            dqg_ref[...] = jnp.zeros_like(dqg_ref)
            dkg_ref[...] = jnp.zeros_like(dkg_ref)
        lane = lax.broadcasted_iota(jnp.int32, (tm, LANES), 1)
        lo = lane < HEAD_DIM
        dcum = jnp.zeros((tm, LANES), F32)
        for h in range(HEADS):
            dcum = jnp.where(lane == h, _lane_col(dq_ref[h], L_F_Q) - _lane_col(dk_ref[h], L_ONE_Q), dcum)
        dcum_ref[...] = dcum

        def pair(a, b):
            return jnp.where(lo, a, pltpu.roll(b, HEAD_DIM, 1))

        def one(raw, dy, g, dg_ref, sl, off):
            r = lax.rsqrt(_seg_sum(raw * raw, lo) * (1.0 / HEAD_DIM) + EPS)
            xhat = raw * r
            dg_ref[:, sl] += jnp.sum(dy * xhat, axis=0, keepdims=True)
            dxh = dy * g
            dx = r * (dxh - xhat * (_seg_sum(dxh * xhat, lo) * (1.0 / HEAD_DIM)))
            out_ref[:, off + sl.start:off + sl.stop] = dx.astype(BF16)

        for pr in range(ATTN_W // LANES):
            sl = slice(pr * LANES, (pr + 1) * LANES)
            dq2 = pair(dq_ref[2 * pr], dq_ref[2 * pr + 1])
            one(p_ref[:, OFF_Q + sl.start:OFF_Q + sl.stop].astype(F32), dq2 * scale, qg_ref[:, sl], dqg_ref, sl, OFF_Q)
            one(p_ref[:, OFF_K + sl.start:OFF_K + sl.stop].astype(F32), pair(dk_ref[2 * pr], dk_ref[2 * pr + 1]),
                kg_ref[:, sl], dkg_ref, sl, OFF_K)
            out_ref[:, OFF_V + sl.start:OFF_V + sl.stop] = pair(dv_ref[2 * pr], dv_ref[2 * pr + 1]).astype(BF16)

    heads = pl.BlockSpec((HEADS, tm, LANES), lambda i: (0, i, 0))
    vec = pl.BlockSpec((1, ATTN_W), lambda i: (0, 0))
    return pl.pallas_call(
        body, name="qk_norm_bwd", grid=(s // tm,),
        in_specs=[heads, heads, heads, pl.BlockSpec((tm, 2 * ATTN_W), lambda i: (i, 0)), vec, vec],
        out_specs=[pl.BlockSpec((tm, 3 * ATTN_W), lambda i: (i, 0)), vec, vec,
                   pl.BlockSpec((tm, LANES), lambda i: (i, 0))],
        out_shape=[jax.ShapeDtypeStruct((s, 3 * ATTN_W), BF16),
                   jax.ShapeDtypeStruct((1, ATTN_W), F32), jax.ShapeDtypeStruct((1, ATTN_W), F32),
                   jax.ShapeDtypeStruct((s, LANES), F32)],
        compiler_params=_params(("arbitrary",)),
    )(dq, dk, dv, proj, qg, kg)


def _conv_bwd(dob, proj, conv_w):
    s = dob.shape[0]
    tm = min(TM_ELEM, s)
    tiles, befores, afters = _conv_specs(tm, s)

    def body(dob_ref, dnext_ref, gb_ref, gc_ref, u_ref, zb_ref, gch_ref, uh_ref, gbn_ref, zbn_ref, w_ref,
             dgb_ref, dgc_ref, du_ref, dzb_ref, dw_ref):
        i = pl.program_id(1)

        @pl.when(i == 0)
        def _():
            dw_ref[...] = jnp.zeros_like(dw_ref)
        gb, gc, u, zb, cu, r1, r2, conv = _conv_parts(gb_ref, gc_ref, u_ref, zb_ref, gch_ref, uh_ref, i == 0, w_ref, tm)
        g = dob_ref[...]
        sg = _sigmoid(zb)
        sz = zb * sg
        dconv = g * gb * sz
        zn = zbn_ref[0:8, :].astype(F32)
        dcn = jnp.where(i == pl.num_programs(1) - 1, 0.0,
                        dnext_ref[...] * gbn_ref[0:8, :].astype(F32) * (zn * _sigmoid(zn)))
        nxt1, nxt2 = _sub_row(dcn, 0), _sub_row(dcn, 1)
        row = lax.broadcasted_iota(jnp.int32, (tm, LANES), 0)
        f1 = jnp.where(row == tm - 1, nxt1, pltpu.roll(dconv, tm - 1, 0))
        f2 = jnp.where(row == tm - 2, nxt1, jnp.where(row == tm - 1, nxt2, pltpu.roll(dconv, tm - 2, 0)))
        dcu = w_ref[2:3, :] * dconv + w_ref[1:2, :] * f1 + w_ref[0:1, :] * f2
        dgb_ref[...] = (g * conv * sz).astype(BF16)
        dgc_ref[...] = (dcu * u).astype(BF16)
        du_ref[...] = (dcu * gc).astype(BF16)
        dzb_ref[...] = (g * gb * conv * (sg * (1.0 + zb * (1.0 - sg)))).astype(BF16)
        w_row = lax.broadcasted_iota(jnp.int32, (3, LANES), 0)
        dw0 = jnp.sum(dconv * r2, axis=0, keepdims=True)
        dw1 = jnp.sum(dconv * r1, axis=0, keepdims=True)
        dw2 = jnp.sum(dconv * cu, axis=0, keepdims=True)
        dw_ref[...] += jnp.where(w_row == 0, dw0, jnp.where(w_row == 1, dw1, dw2))

    blk = pl.BlockSpec((tm, LANES), lambda c, i: (i, c))
    nxt = pl.BlockSpec((8, LANES), lambda c, i: (jnp.minimum((i + 1) * (tm // 8), s // 8 - 1), c))
    wspec = pl.BlockSpec((3, LANES), lambda c, i: (0, c))
    return pl.pallas_call(
        body, name="conv_bwd", grid=(CONV_W // LANES, s // tm),
        in_specs=[blk, nxt] + tiles + befores + afters + [wspec],
        out_specs=[blk, blk, blk, blk, wspec],
        out_shape=[jax.ShapeDtypeStruct((s, CONV_W), BF16)] * 4 + [jax.ShapeDtypeStruct((3, CONV_W), F32)],
        compiler_params=_params(("parallel", "arbitrary")),
    )(dob, dob, *([proj] * 8), conv_w)


def _piece_layout(pieces):
    offs, off = [], 0
    for p in pieces:
        offs.append((off, p.shape[1]))
        off += p.shape[1]
    assert off == N_ALL, off
    return offs


def _dw_in(h, pieces, chip_sums):
    s = h.shape[0]
    tk, tn = min(TK_DW, s), TN_DW
    nk = s // tk
    nn = N_MAIN // tn
    main, fpiece = pieces[:-1], pieces[-1]
    layout = _piece_layout(pieces)[:-1]
    n_main = len(main)
    nx = len(chip_sums)

    def body(*refs):
        p_refs, f_ref, h_ref = refs[:n_main], refs[n_main], refs[n_main + 1]
        ins, refs = refs[n_main + 2:n_main + 2 + nx], refs[n_main + 2 + nx:]
        out_ref, outf_ref = refs[:2]
        outs, (acc, accf, send_sems, recv_sems, local_sems) = refs[2:2 + nx], refs[2 + nx:]
        n, k = pl.program_id(0), pl.program_id(1)

        @pl.when(jnp.logical_and(n == 0, k == 0))
        def _():
            for cp in _chip_copies(ins, outs, send_sems, recv_sems, local_sems):
                cp.start()

        @pl.when(k == 0)
        def _():
            acc[...] = jnp.zeros_like(acc)
        hv = h_ref[pl.ds(pl.multiple_of(k * tk, tk), tk), :]
        for p_ref, (off, width) in zip(p_refs, layout):
            @pl.when(jnp.logical_and(n >= off // tn, n < (off + width) // tn))
            def _():
                acc[...] += _dot_tn(p_ref[...], hv)

        @pl.when(k == nk - 1)
        def _():
            out_ref[...] = acc[...].astype(BF16)

        @pl.when(n == 0)
        def _():
            @pl.when(k == 0)
            def _():
                accf[...] = jnp.zeros_like(accf)
            accf[...] += _dot_tn(f_ref[...], hv)

            @pl.when(k == nk - 1)
            def _():
                outf_ref[...] = accf[...].astype(BF16)

        @pl.when(jnp.logical_and(n == nn - 1, k == nk - 1))
        def _():
            for cp in _chip_copies(ins, outs, send_sems, recv_sems, local_sems):
                cp.wait()

    def piece_spec(off, width):
        lo, hi = off // tn, (off + width) // tn

        def index(n, k):
            active = jnp.logical_and(n >= lo, n < hi)
            return jnp.where(active, k, 0), jnp.clip(n - lo, 0, hi - lo - 1)
        return pl.BlockSpec((tk, tn), index)

    any_spec = pl.BlockSpec(memory_space=pl.ANY)
    res = pl.pallas_call(
        body, name="dw_in", grid=(nn, nk),
        in_specs=[piece_spec(off, width) for off, width in layout]
        + [pl.BlockSpec((tk, N_FPAD), lambda n, k: (jnp.where(n == 0, k, 0), 0)),
           pl.BlockSpec((s, D_MODEL), lambda n, k: (0, 0))] + [any_spec] * nx,
        out_specs=[pl.BlockSpec((tn, D_MODEL), lambda n, k: (n, 0)),
                   pl.BlockSpec((N_FPAD, D_MODEL), lambda n, k: (0, 0))] + [any_spec] * nx,
        out_shape=[jax.ShapeDtypeStruct((N_MAIN, D_MODEL), BF16), jax.ShapeDtypeStruct((N_FPAD, D_MODEL), BF16)]
        + [jax.ShapeDtypeStruct(a.shape, a.dtype) for a in chip_sums],
        scratch_shapes=[pltpu.VMEM((tn, D_MODEL), F32), pltpu.VMEM((N_FPAD, D_MODEL), F32),
                        pltpu.SemaphoreType.DMA((nx * 3,)), pltpu.SemaphoreType.DMA((nx * 3,)),
                        pltpu.SemaphoreType.DMA((nx,))],
        compiler_params=_params(("arbitrary", "arbitrary")),
    )(*main, fpiece, h, *chip_sums)
    return res[:2], res[2:]


def _dh_and_dx(pieces, w_all_t, x, dy, ada3, norm_g, chip_sums):
    s = x.shape[0]
    tm = min(TM_DH, s)
    nt = s // tm
    n = len(chip_sums)
    npc = len(pieces)
    layout = _piece_layout(pieces)

    def body(*refs):
        p_refs, refs = refs[:npc], refs[npc:]
        wt_ref, x_ref, dy_ref, ada_ref, g_ref = refs[:5]
        ins, refs = refs[5:5 + n], refs[5 + n:]
        gx_ref, dsh_ref, dsc_ref, dg_ref = refs[:4]
        outs, (send_sems, recv_sems, local_sems) = refs[4:4 + n], refs[4 + n:]
        i = pl.program_id(0)

        @pl.when(i == 0)
        def _():
            for cp in _chip_copies(ins, outs, send_sems, recv_sems, local_sems):
                cp.start()
            dsh_ref[...] = jnp.zeros_like(dsh_ref)
            dsc_ref[...] = jnp.zeros_like(dsc_ref)
            dg_ref[...] = jnp.zeros_like(dg_ref)

        dh = None
        for p_ref, (off, width) in zip(p_refs, layout):
            part = _dot(p_ref[...], wt_ref[off:off + width, :])
            dh = part if dh is None else dh + part
        xv = x_ref[...]
        r = lax.rsqrt(jnp.mean(xv * xv, axis=-1, keepdims=True) + EPS)
        xhat = xv * r
        g = g_ref[...]
        one_sc = 1.0 + ada_ref[1:2, :]
        dsh_ref[...] += jnp.sum(dh, axis=0, keepdims=True)
        dsc_ref[...] += jnp.sum(dh * (xhat * g), axis=0, keepdims=True)
        dg_ref[...] += jnp.sum(dh * xhat, axis=0, keepdims=True) * one_sc
        dxh = dh * (g * one_sc)
        dx = r * (dxh - xhat * jnp.mean(dxh * xhat, axis=-1, keepdims=True))
        gx_ref[...] = dy_ref[...] + dx

        @pl.when(i == nt - 1)
        def _():
            for cp in _chip_copies(ins, outs, send_sems, recv_sems, local_sems):
                cp.wait()

    full = pl.BlockSpec((tm, D_MODEL), lambda i: (i, 0))
    vec = pl.BlockSpec((1, D_MODEL), lambda i: (0, 0))
    any_spec = pl.BlockSpec(memory_space=pl.ANY)
    res = pl.pallas_call(
        body, name="dh_dx", grid=(nt,),
        in_specs=[pl.BlockSpec((tm, p.shape[1]), lambda i: (i, 0)) for p in pieces]
        + [pl.BlockSpec((N_ALL, D_MODEL), lambda i: (0, 0)), full, full,
           pl.BlockSpec((3, D_MODEL), lambda i: (0, 0)), vec] + [any_spec] * n,
        out_specs=[full, vec, vec, vec] + [any_spec] * n,
        out_shape=[jax.ShapeDtypeStruct((s, D_MODEL), F32)] + [jax.ShapeDtypeStruct((1, D_MODEL), F32)] * 3
        + [jax.ShapeDtypeStruct(a.shape, a.dtype) for a in chip_sums],
        scratch_shapes=[pltpu.SemaphoreType.DMA((n * 3,)), pltpu.SemaphoreType.DMA((n * 3,)),
                        pltpu.SemaphoreType.DMA((n,))],
        compiler_params=_params(("arbitrary",)),
    )(*pieces, w_all_t, x, dy, ada3, norm_g, *chip_sums)
    return res[:4], res[4:]


def _sum_small(vec_all, qg_parts, kg_parts):
    def body(v_ref, q_ref, k_ref, tot_ref, gq_ref, gk_ref):
        tot = v_ref[0:1, :]
        for p in range(1, N_DEV):
            tot = tot + v_ref[p:p + 1, :]
        tot_ref[...] = tot
        gq_ref[...] = jnp.sum(q_ref[...], axis=0, keepdims=True)
        gk_ref[...] = jnp.sum(k_ref[...], axis=0, keepdims=True)

    n = vec_all.shape[-1]
    return pl.pallas_call(
        body, name="sum_small",
        out_shape=[jax.ShapeDtypeStruct((1, n), F32),
                   jax.ShapeDtypeStruct((1, HEAD_DIM), F32), jax.ShapeDtypeStruct((1, HEAD_DIM), F32)],
        compiler_params=_params(),
    )(vec_all, qg_parts, kg_parts)


def _grad_w_ada(c_cols, dada_rows):
    def body(c_ref, d_ref, out_ref):
        acc = c_ref[0] * d_ref[0]
        for b in range(1, N_DEV):
            acc = acc + c_ref[b] * d_ref[b]
        out_ref[...] = acc

    return pl.pallas_call(
        body, name="grad_w_ada",
        out_shape=jax.ShapeDtypeStruct((D_MODEL, ADA_SHARD), F32),
        compiler_params=_params(),
    )(c_cols, dada_rows)


def _adamw(w, m, v, g_parts, name):
    rows, cols = w.shape
    n_parts = g_parts.shape[0]
    tr = 256 if rows % 256 == 0 else rows
    tc = 256 if (tr == rows and rows > 256 and cols % 256 == 0) else cols
    c1 = 1.0 / (1.0 - ADAM_B1 ** ADAM_STEP)
    c2 = 1.0 / (1.0 - ADAM_B2 ** ADAM_STEP)

    def body(w_ref, m_ref, v_ref, g_ref, go_ref, d_ref, mo_ref, vo_ref):
        g = g_ref[0].astype(F32)
        for p in range(1, n_parts):
            g = g + g_ref[p].astype(F32)
        m_new = ADAM_B1 * m_ref[...] + (1.0 - ADAM_B1) * g
        v_new = ADAM_B2 * v_ref[...] + (1.0 - ADAM_B2) * (g * g)
        go_ref[...] = g
        mo_ref[...] = m_new
        vo_ref[...] = v_new
        d_ref[...] = -ADAM_LR * ((m_new * c1) / (jnp.sqrt(v_new * c2) + ADAM_EPS) + ADAM_WD * w_ref[...])

    blk = pl.BlockSpec((tr, tc), lambda i, j: (i, j))
    return pl.pallas_call(
        body, name=name, grid=(rows // tr, cols // tc),
        in_specs=[blk, blk, blk, pl.BlockSpec((n_parts, tr, tc), lambda i, j: (0, i, j))],
        out_specs=[blk] * 4,
        out_shape=[jax.ShapeDtypeStruct((rows, cols), F32)] * 4,
        compiler_params=_params(("parallel", "parallel")),
    )(w, m, v, g_parts)


_O_F = 1536


def _to_internal(wt_g):
    wf = wt_g.reshape(IN_WIDTH, D_MODEL)
    f = jnp.pad(wf[_O_F:_O_F + HEADS], ((0, N_FPAD - HEADS), (0, 0)))
    return jnp.concatenate([wf[:_O_F], wf[_O_F + HEADS:], f], axis=0)


def _slabs_by_core(dwt, dwt_f):
    sources = ((dwt, 0, _O_F, 0), (dwt_f, _O_F, _O_F + HEADS, _O_F), (dwt, _O_F + HEADS, IN_WIDTH, HEADS))

    def slab(p):
        lo, hi = p * IN_SHARD, (p + 1) * IN_SHARD
        parts = []
        for src, o_lo, o_hi, shift in sources:
            a, b = max(lo, o_lo), min(hi, o_hi)
            if a < b:
                parts.append(src[a - shift:b - shift])
        return parts[0] if len(parts) == 1 else jnp.concatenate(parts, axis=0)

    return jnp.stack([jnp.stack([slab(2 * chip + core) for chip in range(4)]) for core in range(2)])


def kernel(x, c, w_ada, b_ada, norm_g, w_in, b_f, q_norm_g, k_norm_g, conv_w, w_attn_out, w_conv_out, w_o, loss_target, m_w_ada, m_b_ada, m_norm_g, m_w_in, m_b_f, m_q_norm_g, m_k_norm_g, m_conv_w, m_w_attn_out, m_w_conv_out, m_w_o, v_w_ada, v_b_ada, v_norm_g, v_w_in, v_b_f, v_q_norm_g, v_k_norm_g, v_conv_w, v_w_attn_out, v_w_conv_out, v_w_o):
    me = 4 * lax.axis_index("x") + 2 * lax.axis_index("y") + lax.axis_index("c")
    s = x.shape[1]
    x2, t2 = x[0], loss_target[0]

    (w_in_g,) = _gather_two_level([w_in[0].T.astype(BF16)], "gather_weights")
    c_all, ada_g = _ada_exchange(c, w_ada[0])
    ada_mine = lax.dynamic_index_in_dim(ada_g[:, :, 0, :], me, axis=1, keepdims=False)
    ada3 = (ada_mine.reshape(1, 3 * D_MODEL) + b_ada).reshape(3, D_MODEL)
    w_all_t = _to_internal(w_in_g)
    qg = jnp.tile(q_norm_g, (1, HEADS))
    kg = jnp.tile(k_norm_g, (1, HEADS))
    bf_pad = jnp.pad(b_f, ((0, 0), (0, LANES - HEADS)))

    (proj, fl, h), (cw_g, wa_g, wb_g, wo_g) = _proj_fwd(
        x2, ada3, norm_g, w_all_t,
        [conv_w[0], w_attn_out[0].astype(BF16), w_conv_out[0].astype(BF16), w_o[0].astype(BF16)])
    wa = jnp.transpose(wa_g, (1, 0, 2)).reshape(ATTN_W, D_MODEL)
    wb = jnp.transpose(wb_g, (1, 0, 2)).reshape(CONV_W, D_MODEL)
    wo = wo_g.reshape(D_MODEL, D_MODEL)
    cw = jnp.transpose(cw_g, (1, 0, 2)).reshape(3, CONV_W)
    qa, ka, va, kt, vt = _qkv_prep(proj, fl, bf_pad, qg, kg)
    attn, oa, qb = _attn_fwd(qa, ka, vt, proj)
    (dy, dgab, do, dza, dob, dwo, dwa, dwb, dgate, loss_part) = _tail(oa, attn, proj, x2, t2, ada3, wa, wb, wo, cw)

    def by_core(slabs8):
        return jnp.swapaxes(slabs8.reshape((4, 2) + slabs8.shape[1:]), 0, 1).astype(BF16)

    core = lax.axis_index("c").astype(jnp.int32).reshape(1)
    small = [by_core(jnp.transpose(dwa.reshape(ATTN_W, N_DEV, LANES), (1, 0, 2))),
             by_core(jnp.transpose(dwb.reshape(CONV_W, N_DEV, LANES), (1, 0, 2))),
             by_core(dwo.reshape(N_DEV, D_MODEL // N_DEV, D_MODEL))]
    small_sums = [_pair_sum(m2, t4, core, "pair_sum_" + nm)
                  for m2, t4, nm in zip(small, _sibling_swap(small, "swap_small"), ("wa", "wb", "wo"))]
    dq, dk, dv = _attn_bwd(qb, ka, kt, va, do)
    dqkv, dqg, dkg, dcum = _qk_norm_bwd(dq, dk, dv, proj, qg, kg)
    df, dbf = _forget_bwd(dcum, fl, bf_pad)
    dcb, dcc, dcu, dcz, dcw = _conv_bwd(dob, proj, cw)
    pieces = [dqkv, dza, dcb, dcc, dcu, dcz, dgab, df]
    (dw_main, dw_f), (g_wa_parts, g_wb_parts, g_wo_parts) = _dw_in(h, pieces, small_sums)

    slabs_in = _slabs_by_core(dw_main, dw_f)
    (theirs_in,) = _sibling_swap([slabs_in], "swap_w_in")
    (grad_x, dshift, dscale, dnormg), (g_in_parts,) = _dh_and_dx(
        pieces, w_all_t, x2, dy, ada3, norm_g, [_pair_sum(slabs_in, theirs_in, core, "pair_sum_w_in")])
    vec = jnp.concatenate([dshift, dscale, dgate, dnormg, dbf, dcw.reshape(1, 3 * CONV_W), dqg, dkg], axis=1)
    (vec_all,) = _gather_direct([vec], "gather_small")
    vec_all = vec_all.reshape(N_DEV, vec.shape[1])
    n_main = 4 * D_MODEL + LANES + 3 * CONV_W
    tot, g_qg, g_kg = _sum_small(
        vec_all[:, :n_main],
        vec_all[:, n_main:n_main + ATTN_W].reshape(N_DEV * HEADS, HEAD_DIM),
        vec_all[:, n_main + ATTN_W:].reshape(N_DEV * HEADS, HEAD_DIM))
    g_b_ada = tot[:, 0:3 * D_MODEL]
    g_norm_g = tot[:, 3 * D_MODEL:4 * D_MODEL]
    g_b_f = tot[:, 4 * D_MODEL:4 * D_MODEL + HEADS]
    g_cw_full = tot[:, 4 * D_MODEL + LANES:].reshape(3, CONV_W)
    g_cw = lax.dynamic_slice(g_cw_full, (0, me * (CONV_W // N_DEV)), (3, CONV_W // N_DEV))
    dada_mine = lax.dynamic_slice(vec_all[:, 0:3 * D_MODEL], (0, me * ADA_SHARD), (N_DEV, ADA_SHARD))
    g_w_ada = _grad_w_ada(jnp.transpose(c_all, (0, 2, 1)), dada_mine.reshape(N_DEV, 1, ADA_SHARD))

    upd = {}
    upd["w_ada"] = _adamw(w_ada[0], m_w_ada[0], v_w_ada[0], g_w_ada[None], "adamw_w_ada")
    upd["b_ada"] = _adamw(b_ada, m_b_ada, v_b_ada, g_b_ada[None], "adamw_b_ada")
    upd["norm_g"] = _adamw(norm_g, m_norm_g, v_norm_g, g_norm_g[None], "adamw_norm_g")
    upd["w_in"] = [u.T for u in _adamw(w_in[0].T, m_w_in[0].T, v_w_in[0].T, g_in_parts, "adamw_w_in")]
    upd["b_f"] = _adamw(b_f, m_b_f, v_b_f, g_b_f[None], "adamw_b_f")
    upd["q_norm_g"] = _adamw(q_norm_g, m_q_norm_g, v_q_norm_g, g_qg[None], "adamw_q_norm_g")
    upd["k_norm_g"] = _adamw(k_norm_g, m_k_norm_g, v_k_norm_g, g_kg[None], "adamw_k_norm_g")
    upd["conv_w"] = _adamw(conv_w[0], m_conv_w[0], v_conv_w[0], g_cw[None], "adamw_conv_w")
    upd["w_attn_out"] = _adamw(w_attn_out[0], m_w_attn_out[0], v_w_attn_out[0], g_wa_parts, "adamw_w_attn_out")
    upd["w_conv_out"] = _adamw(w_conv_out[0], m_w_conv_out[0], v_w_conv_out[0], g_wb_parts, "adamw_w_conv_out")
    upd["w_o"] = _adamw(w_o[0], m_w_o[0], v_w_o[0], g_wo_parts, "adamw_w_o")

    names = ["w_ada", "b_ada", "norm_g", "w_in", "b_f", "q_norm_g", "k_norm_g", "conv_w",
             "w_attn_out", "w_conv_out", "w_o"]
    lead = {"w_ada", "w_in", "conv_w", "w_attn_out", "w_conv_out", "w_o"}
    fix = lambda n, a: a[None] if n in lead else a
    loss = lax.psum(loss_part[0, 0], ("x", "y", "c"))
    outs = [loss, grad_x[None]]
    for k in range(4):
        outs += [fix(n, upd[n][k]) for n in names]
    return tuple(outs)
```

```python
import functools

import numpy as np
import jax
import jax.numpy as jnp
from jax import lax
from jax.experimental import pallas as pl
from jax.experimental.pallas import tpu as pltpu

F32 = jnp.float32
BF16 = jnp.bfloat16

D_MODEL = 1024
HEADS = 8
HEAD_DIM = 64
ATTN_W = 512
CONV_W = 512
N_DEV = 8
IN_WIDTH = 6152
IN_SHARD = IN_WIDTH // N_DEV
N_MAIN = 6144
N_FPAD = 128
N_ALL = N_MAIN + N_FPAD
ADA_SHARD = 3 * D_MODEL // N_DEV
EPS = 1e-6
NEG = -1e30

ADAM_LR = 0.001
ADAM_B1 = 0.9
ADAM_B2 = 0.999
ADAM_EPS = 1e-08
ADAM_WD = 0.01
ADAM_STEP = 10

LANES = 128
VMEM_LIMIT = 56 * 1024 * 1024

TM_PROJ = 256
TN_PROJ = 1024
TM_ELEM = 512
TQ = 512
HEADS_PER_STEP = 4
HEADS_PER_STEP_FWD = 8
TM_TAIL = 256
TC_CUM = 256
TK_DW = 1024
TN_DW = 512
TM_DH = 256
HALO = 16

OFF_Q, OFF_K, OFF_V, OFF_ZA, OFF_CB, OFF_CC, OFF_CU, OFF_CZ, OFF_GA, OFF_GB = (
    0, 512, 1024, 1536, 2048, 2560, 3072, 3584, 4096, 5120)


def _params(sem=None):
    return pltpu.CompilerParams(dimension_semantics=sem, vmem_limit_bytes=VMEM_LIMIT)


def _dot(a, b):
    return jnp.dot(a, b, preferred_element_type=F32)


def _dot_nt(a, b):
    return lax.dot_general(a, b, (((1,), (1,)), ((), ())), preferred_element_type=F32)


def _dot_tn(a, b):
    return lax.dot_general(a, b, (((0,), (0,)), ((), ())), preferred_element_type=F32)


def _sigmoid(x):
    return 1.0 / (1.0 + jnp.exp(-x))


def _lane_lo(shape):
    return lax.broadcasted_iota(jnp.int32, shape, len(shape) - 1) < HEAD_DIM


def _seg_sum(z, lo):
    a = jnp.sum(jnp.where(lo, z, 0.0), axis=-1, keepdims=True)
    b = jnp.sum(jnp.where(lo, 0.0, z), axis=-1, keepdims=True)
    return jnp.where(lo, a, b)


def _lane_col(z, lane):
    idx = lax.broadcasted_iota(jnp.int32, z.shape, 1)
    return jnp.sum(jnp.where(idx == lane, z, 0.0), axis=-1, keepdims=True)


def _sub_row(z, row):
    idx = lax.broadcasted_iota(jnp.int32, z.shape, 0)
    return jnp.sum(jnp.where(idx == row, z, 0.0), axis=0, keepdims=True)


def _mesh_pos():
    x, y, c = lax.axis_index("x"), lax.axis_index("y"), lax.axis_index("c")
    return x, y, c, 4 * x + 2 * y + c


def _peer(k, x, y, c):
    px = 1 - x if (k >> 2) & 1 else x
    py = 1 - y if (k >> 1) & 1 else y
    pc = 1 - c if k & 1 else c
    return (px, py, pc), 4 * px + 2 * py + pc


def _gather_copies(ins, outs, send_sems, recv_sems, local_sems):
    x, y, c, me = _mesh_pos()
    copies = []
    for a in range(len(ins)):
        copies.append(pltpu.make_async_copy(ins[a], outs[a].at[me], local_sems.at[a]))
        for k in range(1, N_DEV):
            dev, _ = _peer(k, x, y, c)
            copies.append(pltpu.make_async_remote_copy(
                src_ref=ins[a], dst_ref=outs[a].at[me],
                send_sem=send_sems.at[a * (N_DEV - 1) + k - 1], recv_sem=recv_sems.at[a * (N_DEV - 1) + k - 1],
                device_id=dev, device_id_type=pl.DeviceIdType.MESH))
    return copies


def _gather_sems(n):
    return [pltpu.SemaphoreType.DMA((n * (N_DEV - 1),)), pltpu.SemaphoreType.DMA((n * (N_DEV - 1),)),
            pltpu.SemaphoreType.DMA((n,))]


def _gather_direct(arrs, name):
    n = len(arrs)
    any_spec = pl.BlockSpec(memory_space=pl.ANY)

    def body(*refs):
        copies = _gather_copies(refs[:n], refs[n:2 * n], *refs[2 * n:])
        for cp in copies:
            cp.start()
        for cp in copies:
            cp.wait()

    return pl.pallas_call(
        body, name=name, out_shape=[jax.ShapeDtypeStruct((N_DEV,) + a.shape, a.dtype) for a in arrs],
        in_specs=[any_spec] * n, out_specs=[any_spec] * n, scratch_shapes=_gather_sems(n),
    )(*arrs)


def _ada_phase(c_ref, w_ref, call_ref, adag_ref, mine_ref, send_sems, recv_sems):
    x, y, c, me = _mesh_pos()

    def copy(phase, k, src, dst):
        dev, _ = _peer(k, x, y, c)
        return pltpu.make_async_remote_copy(
            src_ref=src, dst_ref=dst,
            send_sem=send_sems.at[phase * (N_DEV - 1) + k - 1],
            recv_sem=recv_sems.at[phase * (N_DEV - 1) + k - 1],
            device_id=dev, device_id_type=pl.DeviceIdType.MESH)

    call_ref[me] = c_ref[...]
    first = [copy(0, k, c_ref, call_ref.at[me]) for k in range(1, N_DEV)]
    for cp in first:
        cp.start()
    for cp in first:
        cp.wait()
    wb = w_ref[...].astype(BF16)
    for b in range(N_DEV):
        row = jnp.broadcast_to(call_ref[b], (8, D_MODEL)).astype(BF16)
        mine_ref[b] = _sub_row(_dot(row, wb), 0)
    adag_ref[me] = mine_ref[...]
    second = [copy(1, k, mine_ref, adag_ref.at[me]) for k in range(1, N_DEV)]
    for cp in second:
        cp.start()
    for cp in second:
        cp.wait()


def _gather_weights_and_ada(wt_shard, c_row, w_ada_sh):
    any_spec = pl.BlockSpec(memory_space=pl.ANY)
    vm = pl.BlockSpec(memory_space=pltpu.VMEM)

    def body(w_in_ref, c_ref, wada_ref, out_ref, call_ref, adag_ref, mine_ref, send_sems, recv_sems, local_sem,
             ada_send, ada_recv):
        x, y, c, me = _mesh_pos()
        sibling = (x, y, 1 - c)
        chips = [(1 - x, y), (x, 1 - y), (1 - x, 1 - y)]

        def copy(k, src, blk, to):
            return pltpu.make_async_remote_copy(
                src_ref=src, dst_ref=out_ref.at[blk], send_sem=send_sems.at[k], recv_sem=recv_sems.at[k],
                device_id=to, device_id_type=pl.DeviceIdType.MESH)

        local = pltpu.make_async_copy(w_in_ref, out_ref.at[me], local_sem.at[0])
        local.start()
        first = [copy(0, w_in_ref, me, sibling)]
        first += [copy(1 + j, w_in_ref, me, (px, py, c)) for j, (px, py) in enumerate(chips)]
        for cp in first:
            cp.start()
        _ada_phase(c_ref, wada_ref, call_ref, adag_ref, mine_ref, ada_send, ada_recv)
        passed = []
        for j, (px, py) in enumerate(chips):
            blk = 4 * px + 2 * py + c
            copy(1 + j, w_in_ref, blk, (x, y, c)).wait_recv()
            fwd = copy(4 + j, out_ref.at[blk], blk, sibling)
            fwd.start()
            passed.append(fwd)
        copy(0, w_in_ref, 4 * x + 2 * y + 1 - c, (x, y, c)).wait_recv()
        for j, (px, py) in enumerate(chips):
            copy(4 + j, w_in_ref, 4 * px + 2 * py + 1 - c, (x, y, c)).wait_recv()
        for cp in first + passed:
            cp.wait_send()
        local.wait()

    per = N_DEV - 1
    return pl.pallas_call(
        body, name="gather_weights",
        out_shape=[jax.ShapeDtypeStruct((N_DEV,) + wt_shard.shape, wt_shard.dtype),
                   jax.ShapeDtypeStruct((N_DEV, 1, D_MODEL), F32),
                   jax.ShapeDtypeStruct((N_DEV, N_DEV, 1, ADA_SHARD), F32)],
        in_specs=[any_spec, vm, vm], out_specs=[any_spec, vm, vm],
        scratch_shapes=[pltpu.VMEM((N_DEV, 1, ADA_SHARD), F32),
                        pltpu.SemaphoreType.DMA((per,)), pltpu.SemaphoreType.DMA((per,)),
                        pltpu.SemaphoreType.DMA((1,)),
                        pltpu.SemaphoreType.DMA((2 * per,)), pltpu.SemaphoreType.DMA((2 * per,))],
        compiler_params=pltpu.CompilerParams(vmem_limit_bytes=VMEM_LIMIT),
    )(wt_shard, c_row, w_ada_sh)


def _sibling_swap(arrs, name):
    n = len(arrs)
    any_spec = pl.BlockSpec(memory_space=pl.ANY)

    def body(*refs):
        ins, outs = refs[:n], refs[n:2 * n]
        send_sems, recv_sems = refs[2 * n:]
        x, y, c, _ = _mesh_pos()
        copies = [pltpu.make_async_remote_copy(
            src_ref=ins[a].at[1 - c], dst_ref=outs[a], send_sem=send_sems.at[a], recv_sem=recv_sems.at[a],
            device_id=(x, y, 1 - c), device_id_type=pl.DeviceIdType.MESH) for a in range(n)]
        for cp in copies:
            cp.start()
        for cp in copies:
            cp.wait()

    return pl.pallas_call(
        body, name=name,
        out_shape=[jax.ShapeDtypeStruct(a.shape[1:], a.dtype) for a in arrs],
        in_specs=[any_spec] * n, out_specs=[any_spec] * n,
        scratch_shapes=[pltpu.SemaphoreType.DMA((n,)), pltpu.SemaphoreType.DMA((n,))],
    )(*arrs)


def _pair_sum(mine2, theirs, core, name):
    _, _, rows, cols = mine2.shape
    tr = 256 if rows % 256 == 0 else rows

    def body(core_ref, a_ref, b_ref, out_ref):
        out_ref[...] = (a_ref[...].astype(F32) + b_ref[...].astype(F32)).astype(BF16)

    return pl.pallas_call(
        body, name=name,
        grid_spec=pltpu.PrefetchScalarGridSpec(
            num_scalar_prefetch=1, grid=(4, rows // tr),
            in_specs=[pl.BlockSpec((None, None, tr, cols), lambda ch, i, core_: (core_[0], ch, i, 0)),
                      pl.BlockSpec((None, tr, cols), lambda ch, i, core_: (ch, i, 0))],
            out_specs=pl.BlockSpec((None, tr, cols), lambda ch, i, core_: (ch, i, 0))),
        out_shape=jax.ShapeDtypeStruct(theirs.shape, BF16),
        compiler_params=_params(("parallel", "parallel")),
    )(core, mine2, theirs)


def _chip_copies(ins, outs, send_sems, recv_sems, local_sems):
    x, y, c, _ = _mesh_pos()
    my_chip = 2 * x + y
    chips = [(1 - x, y), (x, 1 - y), (1 - x, 1 - y)]
    copies = []
    for a in range(len(ins)):
        copies.append(pltpu.make_async_copy(ins[a].at[my_chip], outs[a].at[my_chip], local_sems.at[a]))
        for j, (px, py) in enumerate(chips):
            copies.append(pltpu.make_async_remote_copy(
                src_ref=ins[a].at[2 * px + py], dst_ref=outs[a].at[my_chip],
                send_sem=send_sems.at[a * 3 + j], recv_sem=recv_sems.at[a * 3 + j],
                device_id=(px, py, c), device_id_type=pl.DeviceIdType.MESH))
    return copies


def _proj_fwd(x, ada3, norm_g, w_all_t, later):
    s = x.shape[0]
    tm, tn = min(TM_PROJ, s), TN_PROJ
    nt = s // tm
    n = len(later)

    def body(x_ref, ada_ref, g_ref, wt_ref, *rest):
        ins, (proj_ref, fl_ref, h_ref), rest = rest[:n], rest[n:n + 3], rest[n + 3:]
        outs, sems = rest[:n], rest[n:]
        i = pl.program_id(0)

        @pl.when(i == 0)
        def _():
            for cp in _gather_copies(ins, outs, *sems):
                cp.start()

        xv = x_ref[...]
        r = lax.rsqrt(jnp.mean(xv * xv, axis=-1, keepdims=True) + EPS)
        hv = ((xv * r) * g_ref[...]) * (1.0 + ada_ref[1:2, :]) + ada_ref[0:1, :]
        hb = hv.astype(BF16)
        h_ref[...] = hb
        fl_ref[...] = _dot_nt(hb, wt_ref[N_MAIN:N_ALL, :])
        for j in range(N_MAIN // tn):
            proj_ref[:, j * tn:(j + 1) * tn] = _dot_nt(hb, wt_ref[j * tn:(j + 1) * tn, :]).astype(BF16)

        @pl.when(i == nt - 1)
        def _():
            for cp in _gather_copies(ins, outs, *sems):
                cp.wait()

    any_spec = pl.BlockSpec(memory_space=pl.ANY)
    res = pl.pallas_call(
        body, name="proj_fwd", grid=(nt,),
        in_specs=[pl.BlockSpec((tm, D_MODEL), lambda i: (i, 0)),
                  pl.BlockSpec((3, D_MODEL), lambda i: (0, 0)),
                  pl.BlockSpec((1, D_MODEL), lambda i: (0, 0)),
                  pl.BlockSpec((N_ALL, D_MODEL), lambda i: (0, 0))] + [any_spec] * n,
        out_specs=[pl.BlockSpec((tm, N_MAIN), lambda i: (i, 0)),
                   pl.BlockSpec((tm, N_FPAD), lambda i: (i, 0)),
                   pl.BlockSpec((tm, D_MODEL), lambda i: (i, 0))] + [any_spec] * n,
        out_shape=[jax.ShapeDtypeStruct((s, N_MAIN), BF16),
                   jax.ShapeDtypeStruct((s, N_FPAD), F32),
                   jax.ShapeDtypeStruct((s, D_MODEL), BF16)]
        + [jax.ShapeDtypeStruct((N_DEV,) + a.shape, a.dtype) for a in later],
        scratch_shapes=_gather_sems(n),
        compiler_params=_params(("arbitrary",)),
    )(x, ada3, norm_g, w_all_t, *later)
    return res[:3], res[3:]


L_ONE_Q, L_F_Q, L_LSE_Q, L_END = HEAD_DIM, HEAD_DIM + 3, HEAD_DIM + 6, HEAD_DIM + 9


def _split3(f):
    hi = f.astype(BF16).astype(F32)
    r = f - hi
    mid = r.astype(BF16).astype(F32)
    return hi, mid, r - mid


def _place3(lane, first, parts, otherwise):
    a, b, c = parts
    return jnp.where(lane == first, a, jnp.where(lane == first + 1, b, jnp.where(lane == first + 2, c, otherwise)))


def _log_forget(fl, bf):
    z = fl + bf
    lf = jnp.minimum(z, 0.0) - jnp.log1p(jnp.exp(-jnp.abs(z)))
    lane = lax.broadcasted_iota(jnp.int32, z.shape, 1)
    return jnp.where(lane < HEADS, lf, 0.0)


def _qkv_prep(proj, fl, bf_pad, qg, kg):
    s = proj.shape[0]
    tm = min(TM_ELEM, s)
    scale = HEAD_DIM ** -0.5

    def body(p_ref, fl_ref, bf_ref, qg_ref, kg_ref, qa_ref, ka_ref, va_ref, kt_ref, vt_ref, carry):
        @pl.when(pl.program_id(0) == 0)
        def _():
            carry[...] = jnp.zeros_like(carry)
        tri = (lax.broadcasted_iota(jnp.int32, (tm, tm), 1) <= lax.broadcasted_iota(jnp.int32, (tm, tm), 0)).astype(F32)
        cum_v = jnp.dot(tri, _log_forget(fl_ref[...], bf_ref[...]), preferred_element_type=F32,
                        precision=lax.Precision.HIGHEST) + carry[...]
        carry[...] = _sub_row(cum_v, tm - 1)
        lane = lax.broadcasted_iota(jnp.int32, (tm, LANES), 1)
        lo = lane < HEAD_DIM
        v_tail = jnp.where(lane < L_F_Q, 1.0, 0.0)
        for pr in range(ATTN_W // LANES):
            sl = slice(pr * LANES, (pr + 1) * LANES)
            q2 = p_ref[:, OFF_Q + pr * LANES:OFF_Q + (pr + 1) * LANES].astype(F32)
            k2 = p_ref[:, OFF_K + pr * LANES:OFF_K + (pr + 1) * LANES].astype(F32)
            v2 = p_ref[:, OFF_V + pr * LANES:OFF_V + (pr + 1) * LANES].astype(F32)
            rq = lax.rsqrt(_seg_sum(q2 * q2, lo) * (1.0 / HEAD_DIM) + EPS)
            rk = lax.rsqrt(_seg_sum(k2 * k2, lo) * (1.0 / HEAD_DIM) + EPS)
            qn = ((q2 * rq) * qg_ref[:, sl]) * scale
            kn = (k2 * rk) * kg_ref[:, sl]
            for hh in range(2):
                h = 2 * pr + hh
                f3 = _split3(_lane_col(cum_v, h))
                qh = qn if hh == 0 else pltpu.roll(qn, HEAD_DIM, 1)
                kh = kn if hh == 0 else pltpu.roll(kn, HEAD_DIM, 1)
                vh = v2 if hh == 0 else pltpu.roll(v2, HEAD_DIM, 1)
                q_tail = jnp.where(lane < L_F_Q, 1.0, _place3(lane, L_F_Q, f3, 0.0))
                k_tail = _place3(lane, L_ONE_Q, tuple(-f for f in f3), jnp.where(lane < L_END, 1.0, 0.0))
                k_row = jnp.where(lo, kh, k_tail)
                v_row = jnp.where(lo, vh, v_tail)
                qa_ref[h] = jnp.where(lo, qh, q_tail).astype(BF16)
                ka_ref[h] = k_row.astype(BF16)
                va_ref[h] = v_row.astype(BF16)
                kt_ref[h] = k_row.T.astype(BF16)
                vt_ref[h] = v_row.T.astype(BF16)

    heads = pl.BlockSpec((HEADS, tm, LANES), lambda i: (0, i, 0))
    heads_t = pl.BlockSpec((HEADS, LANES, tm), lambda i: (0, 0, i))
    vec = pl.BlockSpec((1, ATTN_W), lambda i: (0, 0))
    return pl.pallas_call(
        body, name="qkv_prep", grid=(s // tm,),
        in_specs=[pl.BlockSpec((tm, 3 * ATTN_W), lambda i: (i, 0)),
                  pl.BlockSpec((tm, LANES), lambda i: (i, 0)),
                  pl.BlockSpec((1, LANES), lambda i: (0, 0)), vec, vec],
        out_specs=[heads, heads, heads, heads_t, heads_t],
        out_shape=[jax.ShapeDtypeStruct((HEADS, s, LANES), BF16)] * 3
        + [jax.ShapeDtypeStruct((HEADS, LANES, s), BF16)] * 2,
        scratch_shapes=[pltpu.VMEM((1, LANES), F32)],
        compiler_params=_params(("arbitrary",)),
    )(proj, fl, bf_pad, qg, kg)


def _causal_t(t):
    return lax.broadcasted_iota(jnp.int32, (t, t), 0) <= lax.broadcasted_iota(jnp.int32, (t, t), 1)


def _tri_steps(nt, q_major):
    if q_major:
        pairs = [(i, j) for i in range(nt) for j in range(i + 1)]
    else:
        pairs = [(i, j) for j in range(nt) for i in range(j, nt)]
    return (jnp.asarray(np.array([p[0] for p in pairs], np.int32)),
            jnp.asarray(np.array([p[1] for p in pairs], np.int32)))


def _attn_fwd(qa, ka, vt, proj):
    s = qa.shape[1]
    t = min(TQ, s)
    it, jt = _tri_steps(s // t, True)
    hp = HEADS_PER_STEP_FWD
    wide = hp * HEAD_DIM
    za_blk = OFF_ZA // wide

    def body(it_ref, jt_ref, q_ref, k_ref, vt_ref, za_ref, attn_ref, oa_ref, qb_ref, m_s, acc_s, pair_s):
        step = pl.program_id(1)
        i, j = it_ref[step], jt_ref[step]

        @pl.when(j == 0)
        def _():
            m_s[...] = jnp.full_like(m_s, NEG)
            acc_s[...] = jnp.zeros_like(acc_s)

        def update(masked):
            for hh in range(hp):
                st = _dot_nt(k_ref[hh], q_ref[hh])
                if masked:
                    st = jnp.where(_causal_t(t), st, NEG)
                m_prev = m_s[hh]
                m_next = jnp.maximum(m_prev, jnp.max(st, axis=0, keepdims=True))
                alpha = jnp.exp(m_prev - m_next)
                pt = jnp.exp(st - m_next).astype(BF16)
                acc_s[hh] = acc_s[hh] * alpha + _dot(vt_ref[hh], pt)
                m_s[hh] = m_next

        @pl.when(j < i)
        def _():
            update(False)

        @pl.when(j == i)
        def _():
            update(True)
            row = lax.broadcasted_iota(jnp.int32, (LANES, t), 0)
            lane = lax.broadcasted_iota(jnp.int32, (t, LANES), 1)
            for hh in range(hp):
                l_row = acc_s[hh, L_ONE_Q:L_ONE_Q + 1, :]
                pair_s[hh * HEAD_DIM:(hh + 1) * HEAD_DIM, :] = acc_s[hh, 0:HEAD_DIM, :] / l_row
                lse3 = _split3(m_s[hh] + jnp.log(l_row))
                tail_t = _place3(row, L_LSE_Q, tuple(-x for x in lse3), 0.0)
                keep_q = jnp.logical_or(lane < L_LSE_Q, lane >= L_END)
                qb_ref[hh] = jnp.where(keep_q, q_ref[hh].astype(F32), tail_t.T).astype(BF16)
            out = pair_s[...].T
            attn_ref[...] = out
            z = za_ref[...].astype(F32)
            oa_ref[...] = (out * (z * _sigmoid(z))).astype(BF16)

    pair_q = pl.BlockSpec((hp, t, LANES), lambda p, n, it_, jt_: (p, it_[n], 0))
    pair_k = pl.BlockSpec((hp, t, LANES), lambda p, n, it_, jt_: (p, jt_[n], 0))
    pair_kt = pl.BlockSpec((hp, LANES, t), lambda p, n, it_, jt_: (p, 0, jt_[n]))
    out_q = pl.BlockSpec((t, wide), lambda p, n, it_, jt_: (it_[n], p))
    return pl.pallas_call(
        body, name="attn_fwd",
        grid_spec=pltpu.PrefetchScalarGridSpec(
            num_scalar_prefetch=2, grid=(HEADS // hp, it.shape[0]),
            in_specs=[pair_q, pair_k, pair_kt,
                      pl.BlockSpec((t, wide), lambda p, n, it_, jt_: (it_[n], za_blk + p))],
            out_specs=[out_q, out_q, pair_q],
            scratch_shapes=[pltpu.VMEM((hp, 1, t), F32), pltpu.VMEM((hp, LANES, t), F32),
                            pltpu.VMEM((wide, t), F32)]),
        out_shape=[jax.ShapeDtypeStruct((s, ATTN_W), F32),
                   jax.ShapeDtypeStruct((s, ATTN_W), BF16),
                   jax.ShapeDtypeStruct((HEADS, s, LANES), BF16)],
        compiler_params=_params(("parallel", "arbitrary")),
    )(it, jt, qa, ka, vt, proj)


def _conv_parts(gb_ref, gc_ref, u_ref, zb_ref, gch_ref, uh_ref, first, w_ref, tm):
    gb, gc = gb_ref[...].astype(F32), gc_ref[...].astype(F32)
    u, zb = u_ref[...].astype(F32), zb_ref[...].astype(F32)
    cu = gc * u
    cu_h = jnp.where(first, 0.0, gch_ref[...].astype(F32) * uh_ref[...].astype(F32))
    prev1, prev2 = _sub_row(cu_h, HALO - 1), _sub_row(cu_h, HALO - 2)
    row = lax.broadcasted_iota(jnp.int32, cu.shape, 0)
    r1 = jnp.where(row == 0, prev1, pltpu.roll(cu, 1, 0))
    r2 = jnp.where(row == 0, prev2, jnp.where(row == 1, prev1, pltpu.roll(cu, 2, 0)))
    conv = w_ref[2:3, :] * cu + w_ref[1:2, :] * r1 + w_ref[0:1, :] * r2
    return gb, gc, u, zb, cu, r1, r2, conv


def _conv_specs(tm, s, width=LANES):
    def tile(off):
        return pl.BlockSpec((tm, width), lambda c, i: (i, off // width + c))

    def before(off):
        return pl.BlockSpec((HALO, width), lambda c, i: (jnp.maximum(i * (tm // HALO) - 1, 0), off // width + c))

    def after(off):
        return pl.BlockSpec((HALO, width),
                            lambda c, i: (jnp.minimum((i + 1) * (tm // HALO), s // HALO - 1), off // width + c))

    return ([tile(OFF_CB), tile(OFF_CC), tile(OFF_CU), tile(OFF_CZ)], [before(OFF_CC), before(OFF_CU)],
            [after(OFF_CB), after(OFF_CZ)])


def _tail(oa, attn, proj, x, target, ada3, wa, wb, wo, conv_w):
    s = x.shape[0]
    tm = min(TM_TAIL, s)
    gab_blk = OFF_GA // (2 * D_MODEL)
    za_blk = OFF_ZA // ATTN_W
    tiles, befores, _ = _conv_specs(tm, s, CONV_W)

    def body(oa_ref, attn_ref, za_ref, gb_ref, gc_ref, u_ref, zb_ref, gch_ref, uh_ref, cw_ref, gab_ref, x_ref, t_ref,
             ada_ref, wa_ref, wb_ref, wo_ref,
             dy_ref, dgab_ref, do_ref, dza_ref, dob_ref, dwo_ref, dwa_ref, dwb_ref, dgate_ref, loss_ref):
        first = pl.program_id(0) == 0

        @pl.when(first)
        def _():
            dwo_ref[...] = jnp.zeros_like(dwo_ref)
            dwa_ref[...] = jnp.zeros_like(dwa_ref)
            dwb_ref[...] = jnp.zeros_like(dwb_ref)
            dgate_ref[...] = jnp.zeros_like(dgate_ref)
            loss_ref[...] = jnp.zeros_like(loss_ref)

        gb, _, _, zb, _, _, _, conv = _conv_parts(gb_ref, gc_ref, u_ref, zb_ref, gch_ref, uh_ref, first, cw_ref, tm)
        ob_v = (gb * conv * (zb * _sigmoid(zb))).astype(BF16)
        oa_v = oa_ref[...]
        wa_v, wb_v, wo_v = wa_ref[...], wb_ref[...], wo_ref[...]
        a2 = _dot(oa_v, wa_v)
        b2 = _dot(ob_v, wb_v)
        sa = _sigmoid(gab_ref[:, 0:D_MODEL].astype(F32))
        sb = _sigmoid(gab_ref[:, D_MODEL:2 * D_MODEL].astype(F32))
        mb = (sa * a2 + sb * b2).astype(BF16)
        mo = _dot(mb, wo_v)
        gate = ada_ref[2:3, :]
        err = (x_ref[...] + gate * mo) - t_ref[...]
        dy = err * (1.0 / D_MODEL)
        dy_ref[...] = dy
        loss_ref[...] += 0.5 * jnp.sum(err * err) * (1.0 / D_MODEL)
        dgate_ref[...] += jnp.sum(dy * mo, axis=0, keepdims=True)
        dmo = (dy * gate).astype(BF16)
        dmerged = _dot_nt(dmo, wo_v)
        dwo_ref[...] += _dot_tn(mb, dmo)
        da2 = (dmerged * sa).astype(BF16)
        db2 = (dmerged * sb).astype(BF16)
        dgab_ref[:, 0:D_MODEL] = (dmerged * a2 * (sa * (1.0 - sa))).astype(BF16)
        dgab_ref[:, D_MODEL:2 * D_MODEL] = (dmerged * b2 * (sb * (1.0 - sb))).astype(BF16)
        doa = _dot_nt(da2, wa_v)
        dob_ref[...] = _dot_nt(db2, wb_v)
        dwa_ref[...] += _dot_tn(oa_v, da2)
        dwb_ref[...] += _dot_tn(ob_v, db2)

        lane = lax.broadcasted_iota(jnp.int32, (tm, LANES), 1)
        lo = lane < HEAD_DIM
        for pr in range(ATTN_W // LANES):
            sl = slice(pr * LANES, (pr + 1) * LANES)
            g, a, z = doa[:, sl], attn_ref[:, sl], za_ref[:, sl].astype(F32)
            sg = _sigmoid(z)
            dat = (g * (z * sg)).astype(BF16).astype(F32)
            prod = dat * a
            dza_ref[:, sl] = (g * a * (sg * (1.0 + z * (1.0 - sg)))).astype(BF16)
            for hh in range(2):
                sel = lo if hh == 0 else jnp.logical_not(lo)
                delta3 = _split3(jnp.sum(jnp.where(sel, prod, 0.0), axis=-1, keepdims=True))
                dh = dat if hh == 0 else pltpu.roll(dat, HEAD_DIM, 1)
                tail_lanes = _place3(lane, L_ONE_Q, tuple(-d for d in delta3), 0.0)
                do_ref[2 * pr + hh] = jnp.where(lo, dh, tail_lanes).astype(BF16)

    half = pl.BlockSpec((tm, ATTN_W), lambda i: (i, 0))
    full = pl.BlockSpec((tm, D_MODEL), lambda i: (i, 0))

    def const(shape):
        return pl.BlockSpec(shape, lambda i: (0, 0))

    def one_axis(spec):
        return pl.BlockSpec(spec.block_shape, lambda i, f=spec.index_map: f(0, i))

    return pl.pallas_call(
        body, name="tail", grid=(s // tm,),
        in_specs=[half, half, pl.BlockSpec((tm, ATTN_W), lambda i: (i, za_blk))]
        + [one_axis(sp) for sp in tiles + befores]
        + [const((3, CONV_W)), pl.BlockSpec((tm, 2 * D_MODEL), lambda i: (i, gab_blk)), full, full,
           const((3, D_MODEL)), const((ATTN_W, D_MODEL)), const((CONV_W, D_MODEL)), const((D_MODEL, D_MODEL))],
        out_specs=[full, pl.BlockSpec((tm, 2 * D_MODEL), lambda i: (i, 0)),
                   pl.BlockSpec((HEADS, tm, LANES), lambda i: (0, i, 0)), half, half,
                   const((D_MODEL, D_MODEL)), const((ATTN_W, D_MODEL)), const((CONV_W, D_MODEL)),
                   const((1, D_MODEL)), const((1, LANES))],
        out_shape=[jax.ShapeDtypeStruct((s, D_MODEL), F32),
                   jax.ShapeDtypeStruct((s, 2 * D_MODEL), BF16),
                   jax.ShapeDtypeStruct((HEADS, s, LANES), BF16),
                   jax.ShapeDtypeStruct((s, ATTN_W), BF16),
                   jax.ShapeDtypeStruct((s, CONV_W), F32),
                   jax.ShapeDtypeStruct((D_MODEL, D_MODEL), F32),
                   jax.ShapeDtypeStruct((ATTN_W, D_MODEL), F32),
                   jax.ShapeDtypeStruct((CONV_W, D_MODEL), F32),
                   jax.ShapeDtypeStruct((1, D_MODEL), F32),
                   jax.ShapeDtypeStruct((1, LANES), F32)],
        compiler_params=_params(("arbitrary",)),
    )(oa, attn, proj, *([proj] * 6), conv_w, proj, x, target, ada3, wa, wb, wo)


def _attn_bwd(qb, ka, kt, va, do):
    s = qb.shape[1]
    t = min(TQ, s)
    nt = s // t
    hp = HEADS_PER_STEP
    it, jt = _tri_steps(nt, False)

    def body(it_ref, jt_ref, q_ref, k_ref, kt_ref, v_ref, do_ref, dq_ref, dk_ref, dv_ref, dqt_s):
        step = pl.program_id(1)
        i, j = it_ref[step], jt_ref[step]

        @pl.when(step == 0)
        def _():
            dqt_s[...] = jnp.zeros_like(dqt_s)

        @pl.when(i == j)
        def _():
            dk_ref[...] = jnp.zeros_like(dk_ref)
            dv_ref[...] = jnp.zeros_like(dv_ref)

        def update(masked):
            for hh in range(hp):
                qh, doh = q_ref[hh], do_ref[hh]
                st = _dot_nt(k_ref[hh], qh)
                if masked:
                    st = jnp.where(_causal_t(t), st, NEG)
                pt = jnp.exp(st)
                dst = (pt * _dot_nt(v_ref[hh], doh)).astype(BF16)
                dv_ref[hh] += _dot(pt.astype(BF16), doh)
                dk_ref[hh] += _dot(dst, qh)
                dqt_s[hh, i] += _dot(kt_ref[hh], dst)

        @pl.when(i > j)
        def _():
            update(False)

        @pl.when(i == j)
        def _():
            update(True)
            for hh in range(hp):
                dq_ref[hh] = dqt_s[hh, i].T

    pair_q = pl.BlockSpec((hp, t, LANES), lambda p, n, it_, jt_: (p, it_[n], 0))
    pair_k = pl.BlockSpec((hp, t, LANES), lambda p, n, it_, jt_: (p, jt_[n], 0))
    pair_kt = pl.BlockSpec((hp, LANES, t), lambda p, n, it_, jt_: (p, 0, jt_[n]))
    return pl.pallas_call(
        body, name="attn_bwd",
        grid_spec=pltpu.PrefetchScalarGridSpec(
            num_scalar_prefetch=2, grid=(HEADS // hp, it.shape[0]),
            in_specs=[pair_q, pair_k, pair_kt, pair_k, pair_q],
            out_specs=[pair_k, pair_k, pair_k],
            scratch_shapes=[pltpu.VMEM((hp, nt, LANES, t), F32)]),
        out_shape=[jax.ShapeDtypeStruct((HEADS, s, LANES), F32)] * 3,
        compiler_params=_params(("parallel", "arbitrary")),
    )(it, jt, qb, ka, kt, va, do)


def _forget_bwd(dcum, fl, bf_pad):
    s = fl.shape[0]
    tc = min(TC_CUM, s)
    n = s // tc

    def body(dc_ref, fl_ref, bf_ref, df_ref, dbf_ref, carry):
        @pl.when(pl.program_id(0) == 0)
        def _():
            carry[...] = jnp.zeros_like(carry)
            dbf_ref[...] = jnp.zeros_like(dbf_ref)
        r = lax.broadcasted_iota(jnp.int32, (tc, tc), 0)
        cidx = lax.broadcasted_iota(jnp.int32, (tc, tc), 1)
        tri = (cidx >= r).astype(F32)
        dc = dc_ref[...]
        dlf = jnp.dot(tri, dc, preferred_element_type=F32, precision=lax.Precision.HIGHEST) + carry[...]
        carry[...] += jnp.sum(dc, axis=0, keepdims=True)
        lane = lax.broadcasted_iota(jnp.int32, (tc, LANES), 1)
        dfl = jnp.where(lane < HEADS, dlf * _sigmoid(-(fl_ref[...] + bf_ref[...])), 0.0)
        df_ref[...] = dfl.astype(BF16)
        dbf_ref[...] += jnp.sum(dfl, axis=0, keepdims=True)

    rev = pl.BlockSpec((tc, LANES), lambda i: (n - 1 - i, 0))
    vec = pl.BlockSpec((1, LANES), lambda i: (0, 0))
    return pl.pallas_call(
        body, name="forget_bwd", grid=(n,),
        in_specs=[rev, rev, vec], out_specs=[rev, vec],
        out_shape=[jax.ShapeDtypeStruct((s, LANES), BF16), jax.ShapeDtypeStruct((1, LANES), F32)],
        scratch_shapes=[pltpu.VMEM((1, LANES), F32)],
        compiler_params=_params(("arbitrary",)),
    )(dcum, fl, bf_pad)


def _qk_norm_bwd(dq, dk, dv, proj, qg, kg):
    s = dv.shape[1]
    tm = min(TM_ELEM, s)
    scale = HEAD_DIM ** -0.5

    def body(dq_ref, dk_ref, dv_ref, p_ref, qg_ref, kg_ref, out_ref, dqg_ref, dkg_ref, dcum_ref):
        @pl.when(pl.program_id(0) == 0)
        def _():
            dqg_ref[...] = jnp.zeros_like(dqg_ref)
            dkg_ref[...] = jnp.zeros_like(dkg_ref)
        lane = lax.broadcasted_iota(jnp.int32, (tm, LANES), 1)
        lo = lane < HEAD_DIM
        dcum = jnp.zeros((tm, LANES), F32)
        for h in range(HEADS):
            dcum = jnp.where(lane == h, _lane_col(dq_ref[h], L_F_Q) - _lane_col(dk_ref[h], L_ONE_Q), dcum)
        dcum_ref[...] = dcum

        def pair(a, b):
            return jnp.where(lo, a, pltpu.roll(b, HEAD_DIM, 1))

        def one(raw, dy, g, dg_ref, sl, off):
            r = lax.rsqrt(_seg_sum(raw * raw, lo) * (1.0 / HEAD_DIM) + EPS)
            xhat = raw * r
            dg_ref[:, sl] += jnp.sum(dy * xhat, axis=0, keepdims=True)
            dxh = dy * g
            dx = r * (dxh - xhat * (_seg_sum(dxh * xhat, lo) * (1.0 / HEAD_DIM)))
            out_ref[:, off + sl.start:off + sl.stop] = dx.astype(BF16)

        for pr in range(ATTN_W // LANES):
            sl = slice(pr * LANES, (pr + 1) * LANES)
            dq2 = pair(dq_ref[2 * pr], dq_ref[2 * pr + 1])
            one(p_ref[:, OFF_Q + sl.start:OFF_Q + sl.stop].astype(F32), dq2 * scale, qg_ref[:, sl], dqg_ref, sl, OFF_Q)
            one(p_ref[:, OFF_K + sl.start:OFF_K + sl.stop].astype(F32), pair(dk_ref[2 * pr], dk_ref[2 * pr + 1]),
                kg_ref[:, sl], dkg_ref, sl, OFF_K)
            out_ref[:, OFF_V + sl.start:OFF_V + sl.stop] = pair(dv_ref[2 * pr], dv_ref[2 * pr + 1]).astype(BF16)

    heads = pl.BlockSpec((HEADS, tm, LANES), lambda i: (0, i, 0))
    vec = pl.BlockSpec((1, ATTN_W), lambda i: (0, 0))
    return pl.pallas_call(
        body, name="qk_norm_bwd", grid=(s // tm,),
        in_specs=[heads, heads, heads, pl.BlockSpec((tm, 2 * ATTN_W), lambda i: (i, 0)), vec, vec],
        out_specs=[pl.BlockSpec((tm, 3 * ATTN_W), lambda i: (i, 0)), vec, vec,
                   pl.BlockSpec((tm, LANES), lambda i: (i, 0))],
        out_shape=[jax.ShapeDtypeStruct((s, 3 * ATTN_W), BF16),
                   jax.ShapeDtypeStruct((1, ATTN_W), F32), jax.ShapeDtypeStruct((1, ATTN_W), F32),
                   jax.ShapeDtypeStruct((s, LANES), F32)],
        compiler_params=_params(("arbitrary",)),
    )(dq, dk, dv, proj, qg, kg)


def _conv_bwd(dob, proj, conv_w):
    s = dob.shape[0]
    tm = min(TM_ELEM, s)
    tiles, befores, afters = _conv_specs(tm, s)

    def body(dob_ref, dnext_ref, gb_ref, gc_ref, u_ref, zb_ref, gch_ref, uh_ref, gbn_ref, zbn_ref, w_ref,
             dgb_ref, dgc_ref, du_ref, dzb_ref, dw_ref):
        i = pl.program_id(1)

        @pl.when(i == 0)
        def _():
            dw_ref[...] = jnp.zeros_like(dw_ref)
        gb, gc, u, zb, cu, r1, r2, conv = _conv_parts(gb_ref, gc_ref, u_ref, zb_ref, gch_ref, uh_ref, i == 0, w_ref, tm)
        g = dob_ref[...]
        sg = _sigmoid(zb)
        sz = zb * sg
        dconv = g * gb * sz
        zn = zbn_ref[0:8, :].astype(F32)
        dcn = jnp.where(i == pl.num_programs(1) - 1, 0.0,
                        dnext_ref[...] * gbn_ref[0:8, :].astype(F32) * (zn * _sigmoid(zn)))
        nxt1, nxt2 = _sub_row(dcn, 0), _sub_row(dcn, 1)
        row = lax.broadcasted_iota(jnp.int32, (tm, LANES), 0)
        f1 = jnp.where(row == tm - 1, nxt1, pltpu.roll(dconv, tm - 1, 0))
        f2 = jnp.where(row == tm - 2, nxt1, jnp.where(row == tm - 1, nxt2, pltpu.roll(dconv, tm - 2, 0)))
        dcu = w_ref[2:3, :] * dconv + w_ref[1:2, :] * f1 + w_ref[0:1, :] * f2
        dgb_ref[...] = (g * conv * sz).astype(BF16)
        dgc_ref[...] = (dcu * u).astype(BF16)
        du_ref[...] = (dcu * gc).astype(BF16)
        dzb_ref[...] = (g * gb * conv * (sg * (1.0 + zb * (1.0 - sg)))).astype(BF16)
        w_row = lax.broadcasted_iota(jnp.int32, (3, LANES), 0)
        dw0 = jnp.sum(dconv * r2, axis=0, keepdims=True)
        dw1 = jnp.sum(dconv * r1, axis=0, keepdims=True)
        dw2 = jnp.sum(dconv * cu, axis=0, keepdims=True)
        dw_ref[...] += jnp.where(w_row == 0, dw0, jnp.where(w_row == 1, dw1, dw2))

    blk = pl.BlockSpec((tm, LANES), lambda c, i: (i, c))
    nxt = pl.BlockSpec((8, LANES), lambda c, i: (jnp.minimum((i + 1) * (tm // 8), s // 8 - 1), c))
    wspec = pl.BlockSpec((3, LANES), lambda c, i: (0, c))
    return pl.pallas_call(
        body, name="conv_bwd", grid=(CONV_W // LANES, s // tm),
        in_specs=[blk, nxt] + tiles + befores + afters + [wspec],
        out_specs=[blk, blk, blk, blk, wspec],
        out_shape=[jax.ShapeDtypeStruct((s, CONV_W), BF16)] * 4 + [jax.ShapeDtypeStruct((3, CONV_W), F32)],
        compiler_params=_params(("parallel", "arbitrary")),
    )(dob, dob, *([proj] * 8), conv_w)


def _piece_layout(pieces):
    offs, off = [], 0
    for p in pieces:
        offs.append((off, p.shape[1]))
        off += p.shape[1]
    assert off == N_ALL, off
    return offs


def _dw_in(h, pieces, chip_sums):
    s = h.shape[0]
    tk, tn = min(TK_DW, s), TN_DW
    nk = s // tk
    nn = N_MAIN // tn
    main, fpiece = pieces[:-1], pieces[-1]
    layout = _piece_layout(pieces)[:-1]
    n_main = len(main)
    nx = len(chip_sums)

    def body(*refs):
        p_refs, f_ref, h_ref = refs[:n_main], refs[n_main], refs[n_main + 1]
        ins, refs = refs[n_main + 2:n_main + 2 + nx], refs[n_main + 2 + nx:]
        out_ref, outf_ref = refs[:2]
        outs, (acc, accf, send_sems, recv_sems, local_sems) = refs[2:2 + nx], refs[2 + nx:]
        n, k = pl.program_id(0), pl.program_id(1)

        @pl.when(jnp.logical_and(n == 0, k == 0))
        def _():
            for cp in _chip_copies(ins, outs, send_sems, recv_sems, local_sems):
                cp.start()

        @pl.when(k == 0)
        def _():
            acc[...] = jnp.zeros_like(acc)
        hv = h_ref[pl.ds(pl.multiple_of(k * tk, tk), tk), :]
        for p_ref, (off, width) in zip(p_refs, layout):
            @pl.when(jnp.logical_and(n >= off // tn, n < (off + width) // tn))
            def _():
                acc[...] += _dot_tn(p_ref[...], hv)

        @pl.when(k == nk - 1)
        def _():
            out_ref[...] = acc[...].astype(BF16)

        @pl.when(n == 0)
        def _():
            @pl.when(k == 0)
            def _():
                accf[...] = jnp.zeros_like(accf)
            accf[...] += _dot_tn(f_ref[...], hv)

            @pl.when(k == nk - 1)
            def _():
                outf_ref[...] = accf[...].astype(BF16)

        @pl.when(jnp.logical_and(n == nn - 1, k == nk - 1))
        def _():
            for cp in _chip_copies(ins, outs, send_sems, recv_sems, local_sems):
                cp.wait()

    def piece_spec(off, width):
        lo, hi = off // tn, (off + width) // tn

        def index(n, k):
            active = jnp.logical_and(n >= lo, n < hi)
            return jnp.where(active, k, 0), jnp.clip(n - lo, 0, hi - lo - 1)
        return pl.BlockSpec((tk, tn), index)

    any_spec = pl.BlockSpec(memory_space=pl.ANY)
    res = pl.pallas_call(
        body, name="dw_in", grid=(nn, nk),
        in_specs=[piece_spec(off, width) for off, width in layout]
        + [pl.BlockSpec((tk, N_FPAD), lambda n, k: (jnp.where(n == 0, k, 0), 0)),
           pl.BlockSpec((s, D_MODEL), lambda n, k: (0, 0))] + [any_spec] * nx,
        out_specs=[pl.BlockSpec((tn, D_MODEL), lambda n, k: (n, 0)),
                   pl.BlockSpec((N_FPAD, D_MODEL), lambda n, k: (0, 0))] + [any_spec] * nx,
        out_shape=[jax.ShapeDtypeStruct((N_MAIN, D_MODEL), BF16), jax.ShapeDtypeStruct((N_FPAD, D_MODEL), BF16)]
        + [jax.ShapeDtypeStruct(a.shape, a.dtype) for a in chip_sums],
        scratch_shapes=[pltpu.VMEM((tn, D_MODEL), F32), pltpu.VMEM((N_FPAD, D_MODEL), F32),
                        pltpu.SemaphoreType.DMA((nx * 3,)), pltpu.SemaphoreType.DMA((nx * 3,)),
                        pltpu.SemaphoreType.DMA((nx,))],
        compiler_params=_params(("arbitrary", "arbitrary")),
    )(*main, fpiece, h, *chip_sums)
    return res[:2], res[2:]


def _dh_and_dx(pieces, w_all_t, x, dy, ada3, norm_g, chip_sums):
    s = x.shape[0]
    tm = min(TM_DH, s)
    nt = s // tm
    n = len(chip_sums)
    npc = len(pieces)
    layout = _piece_layout(pieces)

    def body(*refs):
        p_refs, refs = refs[:npc], refs[npc:]
        wt_ref, x_ref, dy_ref, ada_ref, g_ref = refs[:5]
        ins, refs = refs[5:5 + n], refs[5 + n:]
        gx_ref, dsh_ref, dsc_ref, dg_ref = refs[:4]
        outs, (send_sems, recv_sems, local_sems) = refs[4:4 + n], refs[4 + n:]
        i = pl.program_id(0)

        @pl.when(i == 0)
        def _():
            for cp in _chip_copies(ins, outs, send_sems, recv_sems, local_sems):
                cp.start()
            dsh_ref[...] = jnp.zeros_like(dsh_ref)
            dsc_ref[...] = jnp.zeros_like(dsc_ref)
            dg_ref[...] = jnp.zeros_like(dg_ref)

        dh = None
        for p_ref, (off, width) in zip(p_refs, layout):
            part = _dot(p_ref[...], wt_ref[off:off + width, :])
            dh = part if dh is None else dh + part
        xv = x_ref[...]
        r = lax.rsqrt(jnp.mean(xv * xv, axis=-1, keepdims=True) + EPS)
        xhat = xv * r
        g = g_ref[...]
        one_sc = 1.0 + ada_ref[1:2, :]
        dsh_ref[...] += jnp.sum(dh, axis=0, keepdims=True)
        dsc_ref[...] += jnp.sum(dh * (xhat * g), axis=0, keepdims=True)
        dg_ref[...] += jnp.sum(dh * xhat, axis=0, keepdims=True) * one_sc
        dxh = dh * (g * one_sc)
        dx = r * (dxh - xhat * jnp.mean(dxh * xhat, axis=-1, keepdims=True))
        gx_ref[...] = dy_ref[...] + dx

        @pl.when(i == nt - 1)
        def _():
            for cp in _chip_copies(ins, outs, send_sems, recv_sems, local_sems):
                cp.wait()

    full = pl.BlockSpec((tm, D_MODEL), lambda i: (i, 0))
    vec = pl.BlockSpec((1, D_MODEL), lambda i: (0, 0))
    any_spec = pl.BlockSpec(memory_space=pl.ANY)
    res = pl.pallas_call(
        body, name="dh_dx", grid=(nt,),
        in_specs=[pl.BlockSpec((tm, p.shape[1]), lambda i: (i, 0)) for p in pieces]
        + [pl.BlockSpec((N_ALL, D_MODEL), lambda i: (0, 0)), full, full,
           pl.BlockSpec((3, D_MODEL), lambda i: (0, 0)), vec] + [any_spec] * n,
        out_specs=[full, vec, vec, vec] + [any_spec] * n,
        out_shape=[jax.ShapeDtypeStruct((s, D_MODEL), F32)] + [jax.ShapeDtypeStruct((1, D_MODEL), F32)] * 3
        + [jax.ShapeDtypeStruct(a.shape, a.dtype) for a in chip_sums],
        scratch_shapes=[pltpu.SemaphoreType.DMA((n * 3,)), pltpu.SemaphoreType.DMA((n * 3,)),
                        pltpu.SemaphoreType.DMA((n,))],
        compiler_params=_params(("arbitrary",)),
    )(*pieces, w_all_t, x, dy, ada3, norm_g, *chip_sums)
    return res[:4], res[4:]


def _sum_small(vec_all, qg_parts, kg_parts):
    def body(v_ref, q_ref, k_ref, tot_ref, gq_ref, gk_ref):
        tot = v_ref[0:1, :]
        for p in range(1, N_DEV):
            tot = tot + v_ref[p:p + 1, :]
        tot_ref[...] = tot
        gq_ref[...] = jnp.sum(q_ref[...], axis=0, keepdims=True)
        gk_ref[...] = jnp.sum(k_ref[...], axis=0, keepdims=True)

    n = vec_all.shape[-1]
    return pl.pallas_call(
        body, name="sum_small",
        out_shape=[jax.ShapeDtypeStruct((1, n), F32),
                   jax.ShapeDtypeStruct((1, HEAD_DIM), F32), jax.ShapeDtypeStruct((1, HEAD_DIM), F32)],
        compiler_params=_params(),
    )(vec_all, qg_parts, kg_parts)


def _grad_w_ada(c_cols, dada_rows):
    def body(c_ref, d_ref, out_ref):
        acc = c_ref[0] * d_ref[0]
        for b in range(1, N_DEV):
            acc = acc + c_ref[b] * d_ref[b]
        out_ref[...] = acc

    return pl.pallas_call(
        body, name="grad_w_ada",
        out_shape=jax.ShapeDtypeStruct((D_MODEL, ADA_SHARD), F32),
        compiler_params=_params(),
    )(c_cols, dada_rows)


def _adamw(w, m, v, g_parts, name):
    rows, cols = w.shape
    n_parts = g_parts.shape[0]
    tr = 256 if rows % 256 == 0 else rows
    tc = 256 if (tr == rows and rows > 256 and cols % 256 == 0) else cols
    c1 = 1.0 / (1.0 - ADAM_B1 ** ADAM_STEP)
    c2 = 1.0 / (1.0 - ADAM_B2 ** ADAM_STEP)

    def body(w_ref, m_ref, v_ref, g_ref, go_ref, d_ref, mo_ref, vo_ref):
        g = g_ref[0].astype(F32)
        for p in range(1, n_parts):
            g = g + g_ref[p].astype(F32)
        m_new = ADAM_B1 * m_ref[...] + (1.0 - ADAM_B1) * g
        v_new = ADAM_B2 * v_ref[...] + (1.0 - ADAM_B2) * (g * g)
        go_ref[...] = g
        mo_ref[...] = m_new
        vo_ref[...] = v_new
        d_ref[...] = -ADAM_LR * ((m_new * c1) / (jnp.sqrt(v_new * c2) + ADAM_EPS) + ADAM_WD * w_ref[...])

    blk = pl.BlockSpec((tr, tc), lambda i, j: (i, j))
    return pl.pallas_call(
        body, name=name, grid=(rows // tr, cols // tc),
        in_specs=[blk, blk, blk, pl.BlockSpec((n_parts, tr, tc), lambda i, j: (0, i, j))],
        out_specs=[blk] * 4,
        out_shape=[jax.ShapeDtypeStruct((rows, cols), F32)] * 4,
        compiler_params=_params(("parallel", "parallel")),
    )(w, m, v, g_parts)


_O_F = 1536


def _to_internal(wt_g):
    wf = wt_g.reshape(IN_WIDTH, D_MODEL)
    f = jnp.pad(wf[_O_F:_O_F + HEADS], ((0, N_FPAD - HEADS), (0, 0)))
    return jnp.concatenate([wf[:_O_F], wf[_O_F + HEADS:], f], axis=0)


def _slabs_by_core(dwt, dwt_f):
    sources = ((dwt, 0, _O_F, 0), (dwt_f, _O_F, _O_F + HEADS, _O_F), (dwt, _O_F + HEADS, IN_WIDTH, HEADS))

    def slab(p):
        lo, hi = p * IN_SHARD, (p + 1) * IN_SHARD
        parts = []
        for src, o_lo, o_hi, shift in sources:
            a, b = max(lo, o_lo), min(hi, o_hi)
            if a < b:
                parts.append(src[a - shift:b - shift])
        return parts[0] if len(parts) == 1 else jnp.concatenate(parts, axis=0)

    return jnp.stack([jnp.stack([slab(2 * chip + core) for chip in range(4)]) for core in range(2)])


def kernel(x, c, w_ada, b_ada, norm_g, w_in, b_f, q_norm_g, k_norm_g, conv_w, w_attn_out, w_conv_out, w_o, loss_target, m_w_ada, m_b_ada, m_norm_g, m_w_in, m_b_f, m_q_norm_g, m_k_norm_g, m_conv_w, m_w_attn_out, m_w_conv_out, m_w_o, v_w_ada, v_b_ada, v_norm_g, v_w_in, v_b_f, v_q_norm_g, v_k_norm_g, v_conv_w, v_w_attn_out, v_w_conv_out, v_w_o):
    me = 4 * lax.axis_index("x") + 2 * lax.axis_index("y") + lax.axis_index("c")
    s = x.shape[1]
    x2, t2 = x[0], loss_target[0]

    w_in_g, c_all, ada_g = _gather_weights_and_ada(w_in[0].T.astype(BF16), c, w_ada[0])
    ada_mine = lax.dynamic_index_in_dim(ada_g[:, :, 0, :], me, axis=1, keepdims=False)
    ada3 = (ada_mine.reshape(1, 3 * D_MODEL) + b_ada).reshape(3, D_MODEL)
    w_all_t = _to_internal(w_in_g)
    qg = jnp.tile(q_norm_g, (1, HEADS))
    kg = jnp.tile(k_norm_g, (1, HEADS))
    bf_pad = jnp.pad(b_f, ((0, 0), (0, LANES - HEADS)))

    (proj, fl, h), (cw_g, wa_g, wb_g, wo_g) = _proj_fwd(
        x2, ada3, norm_g, w_all_t,
        [conv_w[0], w_attn_out[0].astype(BF16), w_conv_out[0].astype(BF16), w_o[0].astype(BF16)])
    wa = jnp.transpose(wa_g, (1, 0, 2)).reshape(ATTN_W, D_MODEL)
    wb = jnp.transpose(wb_g, (1, 0, 2)).reshape(CONV_W, D_MODEL)
    wo = wo_g.reshape(D_MODEL, D_MODEL)
    cw = jnp.transpose(cw_g, (1, 0, 2)).reshape(3, CONV_W)
    qa, ka, va, kt, vt = _qkv_prep(proj, fl, bf_pad, qg, kg)
    attn, oa, qb = _attn_fwd(qa, ka, vt, proj)
    (dy, dgab, do, dza, dob, dwo, dwa, dwb, dgate, loss_part) = _tail(oa, attn, proj, x2, t2, ada3, wa, wb, wo, cw)

    def by_core(slabs8):
        return jnp.swapaxes(slabs8.reshape((4, 2) + slabs8.shape[1:]), 0, 1).astype(BF16)

    core = lax.axis_index("c").astype(jnp.int32).reshape(1)
    small = [by_core(jnp.transpose(dwa.reshape(ATTN_W, N_DEV, LANES), (1, 0, 2))),
             by_core(jnp.transpose(dwb.reshape(CONV_W, N_DEV, LANES), (1, 0, 2))),
             by_core(dwo.reshape(N_DEV, D_MODEL // N_DEV, D_MODEL))]
    small_sums = [_pair_sum(m2, t4, core, "pair_sum_" + nm)
                  for m2, t4, nm in zip(small, _sibling_swap(small, "swap_small"), ("wa", "wb", "wo"))]
    dq, dk, dv = _attn_bwd(qb, ka, kt, va, do)
    dqkv, dqg, dkg, dcum = _qk_norm_bwd(dq, dk, dv, proj, qg, kg)
    df, dbf = _forget_bwd(dcum, fl, bf_pad)
    dcb, dcc, dcu, dcz, dcw = _conv_bwd(dob, proj, cw)
    pieces = [dqkv, dza, dcb, dcc, dcu, dcz, dgab, df]
    (dw_main, dw_f), (g_wa_parts, g_wb_parts, g_wo_parts) = _dw_in(h, pieces, small_sums)

    slabs_in = _slabs_by_core(dw_main, dw_f)
    (theirs_in,) = _sibling_swap([slabs_in], "swap_w_in")
    (grad_x, dshift, dscale, dnormg), (g_in_parts,) = _dh_and_dx(
        pieces, w_all_t, x2, dy, ada3, norm_g, [_pair_sum(slabs_in, theirs_in, core, "pair_sum_w_in")])
    vec = jnp.concatenate([dshift, dscale, dgate, dnormg, dbf, dcw.reshape(1, 3 * CONV_W), loss_part, dqg, dkg],
                          axis=1)
    (vec_all,) = _gather_direct([vec], "gather_small")
    vec_all = vec_all.reshape(N_DEV, vec.shape[1])
    n_main = 4 * D_MODEL + LANES + 3 * CONV_W + LANES
    tot, g_qg, g_kg = _sum_small(
        vec_all[:, :n_main],
        vec_all[:, n_main:n_main + ATTN_W].reshape(N_DEV * HEADS, HEAD_DIM),
        vec_all[:, n_main + ATTN_W:].reshape(N_DEV * HEADS, HEAD_DIM))
    g_b_ada = tot[:, 0:3 * D_MODEL]
    g_norm_g = tot[:, 3 * D_MODEL:4 * D_MODEL]
    g_b_f = tot[:, 4 * D_MODEL:4 * D_MODEL + HEADS]
    g_cw_full = tot[:, 4 * D_MODEL + LANES:4 * D_MODEL + LANES + 3 * CONV_W].reshape(3, CONV_W)
    g_cw = lax.dynamic_slice(g_cw_full, (0, me * (CONV_W // N_DEV)), (3, CONV_W // N_DEV))
    dada_mine = lax.dynamic_slice(vec_all[:, 0:3 * D_MODEL], (0, me * ADA_SHARD), (N_DEV, ADA_SHARD))
    g_w_ada = _grad_w_ada(jnp.transpose(c_all, (0, 2, 1)), dada_mine.reshape(N_DEV, 1, ADA_SHARD))

    upd = {}
    upd["w_ada"] = _adamw(w_ada[0], m_w_ada[0], v_w_ada[0], g_w_ada[None], "adamw_w_ada")
    upd["b_ada"] = _adamw(b_ada, m_b_ada, v_b_ada, g_b_ada[None], "adamw_b_ada")
    upd["norm_g"] = _adamw(norm_g, m_norm_g, v_norm_g, g_norm_g[None], "adamw_norm_g")
    upd["w_in"] = [u.T for u in _adamw(w_in[0].T, m_w_in[0].T, v_w_in[0].T, g_in_parts, "adamw_w_in")]
    upd["b_f"] = _adamw(b_f, m_b_f, v_b_f, g_b_f[None], "adamw_b_f")
    upd["q_norm_g"] = _adamw(q_norm_g, m_q_norm_g, v_q_norm_g, g_qg[None], "adamw_q_norm_g")
    upd["k_norm_g"] = _adamw(k_norm_g, m_k_norm_g, v_k_norm_g, g_kg[None], "adamw_k_norm_g")
    upd["conv_w"] = _adamw(conv_w[0], m_conv_w[0], v_conv_w[0], g_cw[None], "adamw_conv_w")
    upd["w_attn_out"] = _adamw(w_attn_out[0], m_w_attn_out[0], v_w_attn_out[0], g_wa_parts, "adamw_w_attn_out")
    upd["w_conv_out"] = _adamw(w_conv_out[0], m_w_conv_out[0], v_w_conv_out[0], g_wb_parts, "adamw_w_conv_out")
    upd["w_o"] = _adamw(w_o[0], m_w_o[0], v_w_o[0], g_wo_parts, "adamw_w_o")

    names = ["w_ada", "b_ada", "norm_g", "w_in", "b_f", "q_norm_g", "k_norm_g", "conv_w",
             "w_attn_out", "w_conv_out", "w_o"]
    lead = {"w_ada", "w_in", "conv_w", "w_attn_out", "w_conv_out", "w_o"}
    fix = lambda n, a: a[None] if n in lead else a
    loss = tot[0, n_main - LANES]
    outs = [loss, grad_x[None]]
    for k in range(4):
        outs += [fix(n, upd[n][k]) for n in names]
    return tuple(outs)
```

```python
import functools

import numpy as np
import jax
import jax.numpy as jnp
from jax import lax
from jax.experimental import pallas as pl
from jax.experimental.pallas import tpu as pltpu

F32 = jnp.float32
BF16 = jnp.bfloat16

D_MODEL = 1024
HEADS = 8
HEAD_DIM = 64
ATTN_W = 512
CONV_W = 512
N_DEV = 8
IN_WIDTH = 6152
IN_SHARD = IN_WIDTH // N_DEV
N_MAIN = 6144
N_FPAD = 128
N_ALL = N_MAIN + N_FPAD
ADA_SHARD = 3 * D_MODEL // N_DEV
EPS = 1e-6
NEG = -1e30

ADAM_LR = 0.001
ADAM_B1 = 0.9
ADAM_B2 = 0.999
ADAM_EPS = 1e-08
ADAM_WD = 0.01
ADAM_STEP = 10

LANES = 128
VMEM_LIMIT = 56 * 1024 * 1024

TM_PROJ = 256
TN_PROJ = 1024
TM_ELEM = 512
TQ = 512
HEADS_PER_STEP = 4
HEADS_PER_STEP_FWD = 8
TM_TAIL = 256
TC_CUM = 256
TK_DW = 1024
TN_DW = 512
TM_DH = 256
HALO = 16

OFF_Q, OFF_K, OFF_V, OFF_ZA, OFF_CB, OFF_CC, OFF_CU, OFF_CZ, OFF_GA, OFF_GB = (
    0, 512, 1024, 1536, 2048, 2560, 3072, 3584, 4096, 5120)


def _params(sem=None):
    return pltpu.CompilerParams(dimension_semantics=sem, vmem_limit_bytes=VMEM_LIMIT)


def _dot(a, b):
    return jnp.dot(a, b, preferred_element_type=F32)


def _dot_nt(a, b):
    return lax.dot_general(a, b, (((1,), (1,)), ((), ())), preferred_element_type=F32)


def _dot_tn(a, b):
    return lax.dot_general(a, b, (((0,), (0,)), ((), ())), preferred_element_type=F32)


def _sigmoid(x):
    return 1.0 / (1.0 + jnp.exp(-x))


def _lane_lo(shape):
    return lax.broadcasted_iota(jnp.int32, shape, len(shape) - 1) < HEAD_DIM


def _seg_sum(z, lo):
    a = jnp.sum(jnp.where(lo, z, 0.0), axis=-1, keepdims=True)
    b = jnp.sum(jnp.where(lo, 0.0, z), axis=-1, keepdims=True)
    return jnp.where(lo, a, b)


def _lane_col(z, lane):
    idx = lax.broadcasted_iota(jnp.int32, z.shape, 1)
    return jnp.sum(jnp.where(idx == lane, z, 0.0), axis=-1, keepdims=True)


def _sub_row(z, row):
    idx = lax.broadcasted_iota(jnp.int32, z.shape, 0)
    return jnp.sum(jnp.where(idx == row, z, 0.0), axis=0, keepdims=True)


def _mesh_pos():
    x, y, c = lax.axis_index("x"), lax.axis_index("y"), lax.axis_index("c")
    return x, y, c, 4 * x + 2 * y + c


def _peer(k, x, y, c):
    px = 1 - x if (k >> 2) & 1 else x
    py = 1 - y if (k >> 1) & 1 else y
    pc = 1 - c if k & 1 else c
    return (px, py, pc), 4 * px + 2 * py + pc


def _gather_copies(ins, outs, send_sems, recv_sems, local_sems):
    x, y, c, me = _mesh_pos()
    copies = []
    for a in range(len(ins)):
        copies.append(pltpu.make_async_copy(ins[a], outs[a].at[me], local_sems.at[a]))
        for k in range(1, N_DEV):
            dev, _ = _peer(k, x, y, c)
            copies.append(pltpu.make_async_remote_copy(
                src_ref=ins[a], dst_ref=outs[a].at[me],
                send_sem=send_sems.at[a * (N_DEV - 1) + k - 1], recv_sem=recv_sems.at[a * (N_DEV - 1) + k - 1],
                device_id=dev, device_id_type=pl.DeviceIdType.MESH))
    return copies


def _gather_sems(n):
    return [pltpu.SemaphoreType.DMA((n * (N_DEV - 1),)), pltpu.SemaphoreType.DMA((n * (N_DEV - 1),)),
            pltpu.SemaphoreType.DMA((n,))]


def _gather_direct(arrs, name):
    n = len(arrs)
    any_spec = pl.BlockSpec(memory_space=pl.ANY)

    def body(*refs):
        copies = _gather_copies(refs[:n], refs[n:2 * n], *refs[2 * n:])
        for cp in copies:
            cp.start()
        for cp in copies:
            cp.wait()

    return pl.pallas_call(
        body, name=name, out_shape=[jax.ShapeDtypeStruct((N_DEV,) + a.shape, a.dtype) for a in arrs],
        in_specs=[any_spec] * n, out_specs=[any_spec] * n, scratch_shapes=_gather_sems(n),
    )(*arrs)


def _ada_phase(c_ref, w_ref, call_ref, adag_ref, mine_ref, send_sems, recv_sems):
    x, y, c, me = _mesh_pos()

    def copy(phase, k, src, dst):
        dev, _ = _peer(k, x, y, c)
        return pltpu.make_async_remote_copy(
            src_ref=src, dst_ref=dst,
            send_sem=send_sems.at[phase * (N_DEV - 1) + k - 1],
            recv_sem=recv_sems.at[phase * (N_DEV - 1) + k - 1],
            device_id=dev, device_id_type=pl.DeviceIdType.MESH)

    call_ref[me] = c_ref[...]
    first = [copy(0, k, c_ref, call_ref.at[me]) for k in range(1, N_DEV)]
    for cp in first:
        cp.start()
    for cp in first:
        cp.wait()
    wb = w_ref[...].astype(BF16)
    for b in range(N_DEV):
        row = jnp.broadcast_to(call_ref[b], (8, D_MODEL)).astype(BF16)
        mine_ref[b] = _sub_row(_dot(row, wb), 0)
    adag_ref[me] = mine_ref[...]
    second = [copy(1, k, mine_ref, adag_ref.at[me]) for k in range(1, N_DEV)]
    for cp in second:
        cp.start()
    for cp in second:
        cp.wait()


def _gather_weights_and_ada(wt_shard, c_row, w_ada_sh):
    any_spec = pl.BlockSpec(memory_space=pl.ANY)
    vm = pl.BlockSpec(memory_space=pltpu.VMEM)

    def body(w_in_ref, c_ref, wada_ref, out_ref, call_ref, adag_ref, mine_ref, send_sems, recv_sems, local_sem,
             ada_send, ada_recv):
        x, y, c, me = _mesh_pos()
        sibling = (x, y, 1 - c)
        chips = [(1 - x, y), (x, 1 - y), (1 - x, 1 - y)]

        def copy(k, src, blk, to):
            return pltpu.make_async_remote_copy(
                src_ref=src, dst_ref=out_ref.at[blk], send_sem=send_sems.at[k], recv_sem=recv_sems.at[k],
                device_id=to, device_id_type=pl.DeviceIdType.MESH)

        local = pltpu.make_async_copy(w_in_ref, out_ref.at[me], local_sem.at[0])
        local.start()
        first = [copy(0, w_in_ref, me, sibling)]
        first += [copy(1 + j, w_in_ref, me, (px, py, c)) for j, (px, py) in enumerate(chips)]
        for cp in first:
            cp.start()
        _ada_phase(c_ref, wada_ref, call_ref, adag_ref, mine_ref, ada_send, ada_recv)
        passed = []
        for j, (px, py) in enumerate(chips):
            blk = 4 * px + 2 * py + c
            copy(1 + j, w_in_ref, blk, (x, y, c)).wait_recv()
            fwd = copy(4 + j, out_ref.at[blk], blk, sibling)
            fwd.start()
            passed.append(fwd)
        copy(0, w_in_ref, 4 * x + 2 * y + 1 - c, (x, y, c)).wait_recv()
        for j, (px, py) in enumerate(chips):
            copy(4 + j, w_in_ref, 4 * px + 2 * py + 1 - c, (x, y, c)).wait_recv()
        for cp in first + passed:
            cp.wait_send()
        local.wait()

    per = N_DEV - 1
    return pl.pallas_call(
        body, name="gather_weights",
        out_shape=[jax.ShapeDtypeStruct((N_DEV,) + wt_shard.shape, wt_shard.dtype),
                   jax.ShapeDtypeStruct((N_DEV, 1, D_MODEL), F32),
                   jax.ShapeDtypeStruct((N_DEV, N_DEV, 1, ADA_SHARD), F32)],
        in_specs=[any_spec, vm, vm], out_specs=[any_spec, vm, vm],
        scratch_shapes=[pltpu.VMEM((N_DEV, 1, ADA_SHARD), F32),
                        pltpu.SemaphoreType.DMA((per,)), pltpu.SemaphoreType.DMA((per,)),
                        pltpu.SemaphoreType.DMA((1,)),
                        pltpu.SemaphoreType.DMA((2 * per,)), pltpu.SemaphoreType.DMA((2 * per,))],
        compiler_params=pltpu.CompilerParams(vmem_limit_bytes=VMEM_LIMIT),
    )(wt_shard, c_row, w_ada_sh)


def _sibling_swap(arrs, name):
    n = len(arrs)
    any_spec = pl.BlockSpec(memory_space=pl.ANY)

    def body(*refs):
        ins, outs = refs[:n], refs[n:2 * n]
        send_sems, recv_sems = refs[2 * n:]
        x, y, c, _ = _mesh_pos()
        copies = [pltpu.make_async_remote_copy(
            src_ref=ins[a].at[1 - c], dst_ref=outs[a], send_sem=send_sems.at[a], recv_sem=recv_sems.at[a],
            device_id=(x, y, 1 - c), device_id_type=pl.DeviceIdType.MESH) for a in range(n)]
        for cp in copies:
            cp.start()
        for cp in copies:
            cp.wait()

    return pl.pallas_call(
        body, name=name,
        out_shape=[jax.ShapeDtypeStruct(a.shape[1:], a.dtype) for a in arrs],
        in_specs=[any_spec] * n, out_specs=[any_spec] * n,
        scratch_shapes=[pltpu.SemaphoreType.DMA((n,)), pltpu.SemaphoreType.DMA((n,))],
    )(*arrs)


def _pair_sum(mine2, theirs, core, name):
    _, _, rows, cols = mine2.shape
    tr = 256 if rows % 256 == 0 else rows

    def body(core_ref, a_ref, b_ref, out_ref):
        out_ref[...] = (a_ref[...].astype(F32) + b_ref[...].astype(F32)).astype(BF16)

    return pl.pallas_call(
        body, name=name,
        grid_spec=pltpu.PrefetchScalarGridSpec(
            num_scalar_prefetch=1, grid=(4, rows // tr),
            in_specs=[pl.BlockSpec((None, None, tr, cols), lambda ch, i, core_: (core_[0], ch, i, 0)),
                      pl.BlockSpec((None, tr, cols), lambda ch, i, core_: (ch, i, 0))],
            out_specs=pl.BlockSpec((None, tr, cols), lambda ch, i, core_: (ch, i, 0))),
        out_shape=jax.ShapeDtypeStruct(theirs.shape, BF16),
        compiler_params=_params(("parallel", "parallel")),
    )(core, mine2, theirs)


def _chip_copies(ins, outs, send_sems, recv_sems, local_sems):
    x, y, c, _ = _mesh_pos()
    my_chip = 2 * x + y
    chips = [(1 - x, y), (x, 1 - y), (1 - x, 1 - y)]
    copies = []
    for a in range(len(ins)):
        copies.append(pltpu.make_async_copy(ins[a].at[my_chip], outs[a].at[my_chip], local_sems.at[a]))
        for j, (px, py) in enumerate(chips):
            copies.append(pltpu.make_async_remote_copy(
                src_ref=ins[a].at[2 * px + py], dst_ref=outs[a].at[my_chip],
                send_sem=send_sems.at[a * 3 + j], recv_sem=recv_sems.at[a * 3 + j],
                device_id=(px, py, c), device_id_type=pl.DeviceIdType.MESH))
    return copies


def _proj_fwd(x, ada3, norm_g, w_all_t, later):
    s = x.shape[0]
    tm, tn = min(TM_PROJ, s), TN_PROJ
    nt = s // tm
    n = len(later)

    def body(x_ref, ada_ref, g_ref, wt_ref, *rest):
        ins, (proj_ref, fl_ref, h_ref), rest = rest[:n], rest[n:n + 3], rest[n + 3:]
        outs, sems = rest[:n], rest[n:]
        i = pl.program_id(0)

        @pl.when(i == 0)
        def _():
            for cp in _gather_copies(ins, outs, *sems):
                cp.start()

        xv = x_ref[...]
        r = lax.rsqrt(jnp.mean(xv * xv, axis=-1, keepdims=True) + EPS)
        hv = ((xv * r) * g_ref[...]) * (1.0 + ada_ref[1:2, :]) + ada_ref[0:1, :]
        hb = hv.astype(BF16)
        h_ref[...] = hb
        fl_ref[...] = _dot_nt(hb, wt_ref[N_MAIN:N_ALL, :])
        for j in range(N_MAIN // tn):
            proj_ref[:, j * tn:(j + 1) * tn] = _dot_nt(hb, wt_ref[j * tn:(j + 1) * tn, :]).astype(BF16)

        @pl.when(i == nt - 1)
        def _():
            for cp in _gather_copies(ins, outs, *sems):
                cp.wait()

    any_spec = pl.BlockSpec(memory_space=pl.ANY)
    res = pl.pallas_call(
        body, name="proj_fwd", grid=(nt,),
        in_specs=[pl.BlockSpec((tm, D_MODEL), lambda i: (i, 0)),
                  pl.BlockSpec((3, D_MODEL), lambda i: (0, 0)),
                  pl.BlockSpec((1, D_MODEL), lambda i: (0, 0)),
                  pl.BlockSpec((N_ALL, D_MODEL), lambda i: (0, 0))] + [any_spec] * n,
        out_specs=[pl.BlockSpec((tm, N_MAIN), lambda i: (i, 0)),
                   pl.BlockSpec((tm, N_FPAD), lambda i: (i, 0)),
                   pl.BlockSpec((tm, D_MODEL), lambda i: (i, 0))] + [any_spec] * n,
        out_shape=[jax.ShapeDtypeStruct((s, N_MAIN), BF16),
                   jax.ShapeDtypeStruct((s, N_FPAD), F32),
                   jax.ShapeDtypeStruct((s, D_MODEL), BF16)]
        + [jax.ShapeDtypeStruct((N_DEV,) + a.shape, a.dtype) for a in later],
        scratch_shapes=_gather_sems(n),
        compiler_params=_params(("arbitrary",)),
    )(x, ada3, norm_g, w_all_t, *later)
    return res[:3], res[3:]


L_ONE_Q, L_F_Q, L_LSE_Q, L_END = HEAD_DIM, HEAD_DIM + 3, HEAD_DIM + 6, HEAD_DIM + 9


def _split3(f):
    hi = f.astype(BF16).astype(F32)
    r = f - hi
    mid = r.astype(BF16).astype(F32)
    return hi, mid, r - mid


def _place3(lane, first, parts, otherwise):
    a, b, c = parts
    return jnp.where(lane == first, a, jnp.where(lane == first + 1, b, jnp.where(lane == first + 2, c, otherwise)))


def _log_forget(fl, bf):
    z = fl + bf
    lf = jnp.minimum(z, 0.0) - jnp.log1p(jnp.exp(-jnp.abs(z)))
    lane = lax.broadcasted_iota(jnp.int32, z.shape, 1)
    return jnp.where(lane < HEADS, lf, 0.0)


def _qkv_prep(proj, fl, bf_pad, qg, kg):
    s = proj.shape[0]
    tm = min(TM_ELEM, s)
    scale = HEAD_DIM ** -0.5

    def body(p_ref, fl_ref, bf_ref, qg_ref, kg_ref, qa_ref, ka_ref, va_ref, kt_ref, vt_ref, carry):
        @pl.when(pl.program_id(0) == 0)
        def _():
            carry[...] = jnp.zeros_like(carry)
        tri = (lax.broadcasted_iota(jnp.int32, (tm, tm), 1) <= lax.broadcasted_iota(jnp.int32, (tm, tm), 0)).astype(F32)
        cum_v = jnp.dot(tri, _log_forget(fl_ref[...], bf_ref[...]), preferred_element_type=F32,
                        precision=lax.Precision.HIGHEST) + carry[...]
        carry[...] = _sub_row(cum_v, tm - 1)
        lane = lax.broadcasted_iota(jnp.int32, (tm, LANES), 1)
        lo = lane < HEAD_DIM
        v_tail = jnp.where(lane < L_F_Q, 1.0, 0.0)
        for pr in range(ATTN_W // LANES):
            sl = slice(pr * LANES, (pr + 1) * LANES)
            q2 = p_ref[:, OFF_Q + pr * LANES:OFF_Q + (pr + 1) * LANES].astype(F32)
            k2 = p_ref[:, OFF_K + pr * LANES:OFF_K + (pr + 1) * LANES].astype(F32)
            v2 = p_ref[:, OFF_V + pr * LANES:OFF_V + (pr + 1) * LANES].astype(F32)
            rq = lax.rsqrt(_seg_sum(q2 * q2, lo) * (1.0 / HEAD_DIM) + EPS)
            rk = lax.rsqrt(_seg_sum(k2 * k2, lo) * (1.0 / HEAD_DIM) + EPS)
            qn = ((q2 * rq) * qg_ref[:, sl]) * scale
            kn = (k2 * rk) * kg_ref[:, sl]
            for hh in range(2):
                h = 2 * pr + hh
                f3 = _split3(_lane_col(cum_v, h))
                qh = qn if hh == 0 else pltpu.roll(qn, HEAD_DIM, 1)
                kh = kn if hh == 0 else pltpu.roll(kn, HEAD_DIM, 1)
                vh = v2 if hh == 0 else pltpu.roll(v2, HEAD_DIM, 1)
                q_tail = jnp.where(lane < L_F_Q, 1.0, _place3(lane, L_F_Q, f3, 0.0))
                k_tail = _place3(lane, L_ONE_Q, tuple(-f for f in f3), jnp.where(lane < L_END, 1.0, 0.0))
                k_row = jnp.where(lo, kh, k_tail)
                v_row = jnp.where(lo, vh, v_tail)
                qa_ref[h] = jnp.where(lo, qh, q_tail).astype(BF16)
                ka_ref[h] = k_row.astype(BF16)
                va_ref[h] = v_row.astype(BF16)
                kt_ref[h] = k_row.T.astype(BF16)
                vt_ref[h] = v_row.T.astype(BF16)

    heads = pl.BlockSpec((HEADS, tm, LANES), lambda i: (0, i, 0))
    heads_t = pl.BlockSpec((HEADS, LANES, tm), lambda i: (0, 0, i))
    vec = pl.BlockSpec((1, ATTN_W), lambda i: (0, 0))
    return pl.pallas_call(
        body, name="qkv_prep", grid=(s // tm,),
        in_specs=[pl.BlockSpec((tm, 3 * ATTN_W), lambda i: (i, 0)),
                  pl.BlockSpec((tm, LANES), lambda i: (i, 0)),
                  pl.BlockSpec((1, LANES), lambda i: (0, 0)), vec, vec],
        out_specs=[heads, heads, heads, heads_t, heads_t],
        out_shape=[jax.ShapeDtypeStruct((HEADS, s, LANES), BF16)] * 3
        + [jax.ShapeDtypeStruct((HEADS, LANES, s), BF16)] * 2,
        scratch_shapes=[pltpu.VMEM((1, LANES), F32)],
        compiler_params=_params(("arbitrary",)),
    )(proj, fl, bf_pad, qg, kg)


def _causal_t(t):
    return lax.broadcasted_iota(jnp.int32, (t, t), 0) <= lax.broadcasted_iota(jnp.int32, (t, t), 1)


def _tri_steps(nt, q_major):
    if q_major:
        pairs = [(i, j) for i in range(nt) for j in range(i + 1)]
    else:
        pairs = [(i, j) for j in range(nt) for i in range(j, nt)]
    return (jnp.asarray(np.array([p[0] for p in pairs], np.int32)),
            jnp.asarray(np.array([p[1] for p in pairs], np.int32)))


def _attn_fwd(qa, ka, vt, proj):
    s = qa.shape[1]
    t = min(TQ, s)
    it, jt = _tri_steps(s // t, True)
    hp = HEADS_PER_STEP_FWD
    wide = hp * HEAD_DIM
    za_blk = OFF_ZA // wide

    def body(it_ref, jt_ref, q_ref, k_ref, vt_ref, za_ref, attn_ref, oa_ref, qb_ref, m_s, acc_s, pair_s):
        step = pl.program_id(1)
        i, j = it_ref[step], jt_ref[step]

        @pl.when(j == 0)
        def _():
            m_s[...] = jnp.full_like(m_s, NEG)
            acc_s[...] = jnp.zeros_like(acc_s)

        def update(masked):
            for hh in range(hp):
                st = _dot_nt(k_ref[hh], q_ref[hh])
                if masked:
                    st = jnp.where(_causal_t(t), st, NEG)
                m_prev = m_s[hh]
                m_next = jnp.maximum(m_prev, jnp.max(st, axis=0, keepdims=True))
                alpha = jnp.exp(m_prev - m_next)
                pt = jnp.exp(st - m_next).astype(BF16)
                acc_s[hh] = acc_s[hh] * alpha + _dot(vt_ref[hh], pt)
                m_s[hh] = m_next

        @pl.when(j < i)
        def _():
            update(False)

        @pl.when(j == i)
        def _():
            update(True)
            row = lax.broadcasted_iota(jnp.int32, (LANES, t), 0)
            lane = lax.broadcasted_iota(jnp.int32, (t, LANES), 1)
            for hh in range(hp):
                l_row = acc_s[hh, L_ONE_Q:L_ONE_Q + 1, :]
                pair_s[hh * HEAD_DIM:(hh + 1) * HEAD_DIM, :] = acc_s[hh, 0:HEAD_DIM, :] / l_row
                lse3 = _split3(m_s[hh] + jnp.log(l_row))
                tail_t = _place3(row, L_LSE_Q, tuple(-x for x in lse3), 0.0)
                keep_q = jnp.logical_or(lane < L_LSE_Q, lane >= L_END)
                qb_ref[hh] = jnp.where(keep_q, q_ref[hh].astype(F32), tail_t.T).astype(BF16)
            out = pair_s[...].T
            attn_ref[...] = out
            z = za_ref[...].astype(F32)
            oa_ref[...] = (out * (z * _sigmoid(z))).astype(BF16)

    pair_q = pl.BlockSpec((hp, t, LANES), lambda p, n, it_, jt_: (p, it_[n], 0))
    pair_k = pl.BlockSpec((hp, t, LANES), lambda p, n, it_, jt_: (p, jt_[n], 0))
    pair_kt = pl.BlockSpec((hp, LANES, t), lambda p, n, it_, jt_: (p, 0, jt_[n]))
    out_q = pl.BlockSpec((t, wide), lambda p, n, it_, jt_: (it_[n], p))
    return pl.pallas_call(
        body, name="attn_fwd",
        grid_spec=pltpu.PrefetchScalarGridSpec(
            num_scalar_prefetch=2, grid=(HEADS // hp, it.shape[0]),
            in_specs=[pair_q, pair_k, pair_kt,
                      pl.BlockSpec((t, wide), lambda p, n, it_, jt_: (it_[n], za_blk + p))],
            out_specs=[out_q, out_q, pair_q],
            scratch_shapes=[pltpu.VMEM((hp, 1, t), F32), pltpu.VMEM((hp, LANES, t), F32),
                            pltpu.VMEM((wide, t), F32)]),
        out_shape=[jax.ShapeDtypeStruct((s, ATTN_W), F32),
                   jax.ShapeDtypeStruct((s, ATTN_W), BF16),
                   jax.ShapeDtypeStruct((HEADS, s, LANES), BF16)],
        compiler_params=_params(("parallel", "arbitrary")),
    )(it, jt, qa, ka, vt, proj)


def _conv_parts(gb_ref, gc_ref, u_ref, zb_ref, gch_ref, uh_ref, first, w_ref, tm):
    gb, gc = gb_ref[...].astype(F32), gc_ref[...].astype(F32)
    u, zb = u_ref[...].astype(F32), zb_ref[...].astype(F32)
    cu = gc * u
    cu_h = jnp.where(first, 0.0, gch_ref[...].astype(F32) * uh_ref[...].astype(F32))
    prev1, prev2 = _sub_row(cu_h, HALO - 1), _sub_row(cu_h, HALO - 2)
    row = lax.broadcasted_iota(jnp.int32, cu.shape, 0)
    r1 = jnp.where(row == 0, prev1, pltpu.roll(cu, 1, 0))
    r2 = jnp.where(row == 0, prev2, jnp.where(row == 1, prev1, pltpu.roll(cu, 2, 0)))
    conv = w_ref[2:3, :] * cu + w_ref[1:2, :] * r1 + w_ref[0:1, :] * r2
    return gb, gc, u, zb, cu, r1, r2, conv


def _conv_specs(tm, s, width=LANES):
    def tile(off):
        return pl.BlockSpec((tm, width), lambda c, i: (i, off // width + c))

    def before(off):
        return pl.BlockSpec((HALO, width), lambda c, i: (jnp.maximum(i * (tm // HALO) - 1, 0), off // width + c))

    def after(off):
        return pl.BlockSpec((HALO, width),
                            lambda c, i: (jnp.minimum((i + 1) * (tm // HALO), s // HALO - 1), off // width + c))

    return ([tile(OFF_CB), tile(OFF_CC), tile(OFF_CU), tile(OFF_CZ)], [before(OFF_CC), before(OFF_CU)],
            [after(OFF_CB), after(OFF_CZ)])


def _tail(oa, attn, proj, x, target, ada3, wa, wb, wo, conv_w):
    s = x.shape[0]
    tm = min(TM_TAIL, s)
    gab_blk = OFF_GA // (2 * D_MODEL)
    za_blk = OFF_ZA // ATTN_W
    tiles, befores, _ = _conv_specs(tm, s, CONV_W)

    def body(oa_ref, attn_ref, za_ref, gb_ref, gc_ref, u_ref, zb_ref, gch_ref, uh_ref, cw_ref, gab_ref, x_ref, t_ref,
             ada_ref, wa_ref, wb_ref, wo_ref,
             dy_ref, dgab_ref, do_ref, dza_ref, dob_ref, dwo_ref, dwa_ref, dwb_ref, dgate_ref, loss_ref):
        first = pl.program_id(0) == 0

        @pl.when(first)
        def _():
            dwo_ref[...] = jnp.zeros_like(dwo_ref)
            dwa_ref[...] = jnp.zeros_like(dwa_ref)
            dwb_ref[...] = jnp.zeros_like(dwb_ref)
            dgate_ref[...] = jnp.zeros_like(dgate_ref)
            loss_ref[...] = jnp.zeros_like(loss_ref)

        gb, _, _, zb, _, _, _, conv = _conv_parts(gb_ref, gc_ref, u_ref, zb_ref, gch_ref, uh_ref, first, cw_ref, tm)
        ob_v = (gb * conv * (zb * _sigmoid(zb))).astype(BF16)
        oa_v = oa_ref[...]
        wa_v, wb_v, wo_v = wa_ref[...], wb_ref[...], wo_ref[...]
        a2 = _dot(oa_v, wa_v)
        b2 = _dot(ob_v, wb_v)
        sa = _sigmoid(gab_ref[:, 0:D_MODEL].astype(F32))
        sb = _sigmoid(gab_ref[:, D_MODEL:2 * D_MODEL].astype(F32))
        mb = (sa * a2 + sb * b2).astype(BF16)
        mo = _dot(mb, wo_v)
        gate = ada_ref[2:3, :]
        err = (x_ref[...] + gate * mo) - t_ref[...]
        dy = err * (1.0 / D_MODEL)
        dy_ref[...] = dy
        loss_ref[...] += 0.5 * jnp.sum(err * err) * (1.0 / D_MODEL)
        dgate_ref[...] += jnp.sum(dy * mo, axis=0, keepdims=True)
        dmo = (dy * gate).astype(BF16)
        dmerged = _dot_nt(dmo, wo_v)
        dwo_ref[...] += _dot_tn(mb, dmo)
        da2 = (dmerged * sa).astype(BF16)
        db2 = (dmerged * sb).astype(BF16)
        dgab_ref[:, 0:D_MODEL] = (dmerged * a2 * (sa * (1.0 - sa))).astype(BF16)
        dgab_ref[:, D_MODEL:2 * D_MODEL] = (dmerged * b2 * (sb * (1.0 - sb))).astype(BF16)
        doa = _dot_nt(da2, wa_v)
        dob_ref[...] = _dot_nt(db2, wb_v)
        dwa_ref[...] += _dot_tn(oa_v, da2)
        dwb_ref[...] += _dot_tn(ob_v, db2)

        lane = lax.broadcasted_iota(jnp.int32, (tm, LANES), 1)
        lo = lane < HEAD_DIM
        for pr in range(ATTN_W // LANES):
            sl = slice(pr * LANES, (pr + 1) * LANES)
            g, a, z = doa[:, sl], attn_ref[:, sl], za_ref[:, sl].astype(F32)
            sg = _sigmoid(z)
            dat = (g * (z * sg)).astype(BF16).astype(F32)
            prod = dat * a
            dza_ref[:, sl] = (g * a * (sg * (1.0 + z * (1.0 - sg)))).astype(BF16)
            for hh in range(2):
                sel = lo if hh == 0 else jnp.logical_not(lo)
                delta3 = _split3(jnp.sum(jnp.where(sel, prod, 0.0), axis=-1, keepdims=True))
                dh = dat if hh == 0 else pltpu.roll(dat, HEAD_DIM, 1)
                tail_lanes = _place3(lane, L_ONE_Q, tuple(-d for d in delta3), 0.0)
                do_ref[2 * pr + hh] = jnp.where(lo, dh, tail_lanes).astype(BF16)

    half = pl.BlockSpec((tm, ATTN_W), lambda i: (i, 0))
    full = pl.BlockSpec((tm, D_MODEL), lambda i: (i, 0))

    def const(shape):
        return pl.BlockSpec(shape, lambda i: (0, 0))

    def one_axis(spec):
        return pl.BlockSpec(spec.block_shape, lambda i, f=spec.index_map: f(0, i))

    return pl.pallas_call(
        body, name="tail", grid=(s // tm,),
        in_specs=[half, half, pl.BlockSpec((tm, ATTN_W), lambda i: (i, za_blk))]
        + [one_axis(sp) for sp in tiles + befores]
        + [const((3, CONV_W)), pl.BlockSpec((tm, 2 * D_MODEL), lambda i: (i, gab_blk)), full, full,
           const((3, D_MODEL)), const((ATTN_W, D_MODEL)), const((CONV_W, D_MODEL)), const((D_MODEL, D_MODEL))],
        out_specs=[full, pl.BlockSpec((tm, 2 * D_MODEL), lambda i: (i, 0)),
                   pl.BlockSpec((HEADS, tm, LANES), lambda i: (0, i, 0)), half, half,
                   const((D_MODEL, D_MODEL)), const((ATTN_W, D_MODEL)), const((CONV_W, D_MODEL)),
                   const((1, D_MODEL)), const((1, LANES))],
        out_shape=[jax.ShapeDtypeStruct((s, D_MODEL), F32),
                   jax.ShapeDtypeStruct((s, 2 * D_MODEL), BF16),
                   jax.ShapeDtypeStruct((HEADS, s, LANES), BF16),
                   jax.ShapeDtypeStruct((s, ATTN_W), BF16),
                   jax.ShapeDtypeStruct((s, CONV_W), F32),
                   jax.ShapeDtypeStruct((D_MODEL, D_MODEL), F32),
                   jax.ShapeDtypeStruct((ATTN_W, D_MODEL), F32),
                   jax.ShapeDtypeStruct((CONV_W, D_MODEL), F32),
                   jax.ShapeDtypeStruct((1, D_MODEL), F32),
                   jax.ShapeDtypeStruct((1, LANES), F32)],
        compiler_params=_params(("arbitrary",)),
    )(oa, attn, proj, *([proj] * 6), conv_w, proj, x, target, ada3, wa, wb, wo)


def _attn_bwd(qb, ka, kt, va, do, proj, qg, kg):
    s = qb.shape[1]
    t = min(TQ, s)
    nt = s // t
    hp = HEADS_PER_STEP
    wide = hp * HEAD_DIM
    scale = HEAD_DIM ** -0.5
    it, jt = _tri_steps(nt, False)

    def body(it_ref, jt_ref, q_ref, k_ref, kt_ref, v_ref, do_ref, qraw_ref, kraw_ref, qg_ref, kg_ref,
             dq_ref, dk_ref, dv_ref, dqg_ref, dkg_ref, dcum_ref, dqt_s, dk_s, dv_s, rows_s):
        grp, step = pl.program_id(0), pl.program_id(1)
        i, j = it_ref[step], jt_ref[step]
        lane = lax.broadcasted_iota(jnp.int32, (t, LANES), 1)
        lo = lane < HEAD_DIM

        @pl.when(step == 0)
        def _():
            dqt_s[...] = jnp.zeros_like(dqt_s)
            dqg_ref[...] = jnp.zeros_like(dqg_ref)
            dkg_ref[...] = jnp.zeros_like(dkg_ref)

        @pl.when(i == j)
        def _():
            dk_s[...] = jnp.zeros_like(dk_s)
            dv_s[...] = jnp.zeros_like(dv_s)

        def update(masked):
            for hh in range(hp):
                qh, doh = q_ref[hh], do_ref[hh]
                st = _dot_nt(k_ref[hh], qh)
                if masked:
                    st = jnp.where(_causal_t(t), st, NEG)
                pt = jnp.exp(st)
                dst = (pt * _dot_nt(v_ref[hh], doh)).astype(BF16)
                dv_s[hh] += _dot(pt.astype(BF16), doh)
                dk_s[hh] += _dot(dst, qh)
                dqt_s[hh, i] += _dot(kt_ref[hh], dst)

        def pair(a, b):
            return jnp.where(lo, a, pltpu.roll(b, HEAD_DIM, 1))

        def norm_bwd(raw, dy, g, dg_ref, out_ref, sl):
            r = lax.rsqrt(_seg_sum(raw * raw, lo) * (1.0 / HEAD_DIM) + EPS)
            xhat = raw * r
            dg_ref[:, sl] += jnp.sum(dy * xhat, axis=0, keepdims=True)
            dxh = dy * g
            dx = r * (dxh - xhat * (_seg_sum(dxh * xhat, lo) * (1.0 / HEAD_DIM)))
            out_ref[:, sl] = dx.astype(BF16)

        @pl.when(i > j)
        def _():
            update(False)

        @pl.when(i == j)
        def _():
            update(True)
            dq_rows = [dqt_s[hh, i].T for hh in range(hp)]
            rows = jnp.zeros((t, LANES), F32)
            for hh in range(hp):
                rows = jnp.where(lane == grp * hp + hh, _lane_col(dq_rows[hh], L_F_Q), rows)
            rows_s[...] = rows
            for pr in range(hp // 2):
                sl = slice(pr * LANES, (pr + 1) * LANES)
                norm_bwd(qraw_ref[:, sl].astype(F32), pair(dq_rows[2 * pr], dq_rows[2 * pr + 1]) * scale,
                         qg_ref[:, sl], dqg_ref, dq_ref, sl)

        @pl.when(i == nt - 1)
        def _():
            dcum = rows_s[...]
            for hh in range(hp):
                dcum = jnp.where(lane == grp * hp + hh, dcum - _lane_col(dk_s[hh], L_ONE_Q), dcum)
            dcum_ref[0] = dcum
            for pr in range(hp // 2):
                sl = slice(pr * LANES, (pr + 1) * LANES)
                norm_bwd(kraw_ref[:, sl].astype(F32), pair(dk_s[2 * pr], dk_s[2 * pr + 1]),
                         kg_ref[:, sl], dkg_ref, dk_ref, sl)
                dv_ref[:, sl] = pair(dv_s[2 * pr], dv_s[2 * pr + 1]).astype(BF16)

    pair_q = pl.BlockSpec((hp, t, LANES), lambda p, n, it_, jt_: (p, it_[n], 0))
    pair_k = pl.BlockSpec((hp, t, LANES), lambda p, n, it_, jt_: (p, jt_[n], 0))
    pair_kt = pl.BlockSpec((hp, LANES, t), lambda p, n, it_, jt_: (p, 0, jt_[n]))
    tok = pl.BlockSpec((t, wide), lambda p, n, it_, jt_: (jt_[n], p))
    gain = pl.BlockSpec((1, wide), lambda p, n, it_, jt_: (0, p))
    return pl.pallas_call(
        body, name="attn_bwd",
        grid_spec=pltpu.PrefetchScalarGridSpec(
            num_scalar_prefetch=2, grid=(HEADS // hp, it.shape[0]),
            in_specs=[pair_q, pair_k, pair_kt, pair_k, pair_q,
                      pl.BlockSpec((t, wide), lambda p, n, it_, jt_: (jt_[n], OFF_Q // wide + p)),
                      pl.BlockSpec((t, wide), lambda p, n, it_, jt_: (jt_[n], OFF_K // wide + p)), gain, gain],
            out_specs=[tok, tok, tok, gain, gain,
                       pl.BlockSpec((1, t, LANES), lambda p, n, it_, jt_: (p, jt_[n], 0))],
            scratch_shapes=[pltpu.VMEM((hp, nt, LANES, t), F32), pltpu.VMEM((hp, t, LANES), F32),
                            pltpu.VMEM((hp, t, LANES), F32), pltpu.VMEM((t, LANES), F32)]),
        out_shape=[jax.ShapeDtypeStruct((s, ATTN_W), BF16)] * 3
        + [jax.ShapeDtypeStruct((1, ATTN_W), F32)] * 2
        + [jax.ShapeDtypeStruct((HEADS // hp, s, LANES), F32)],
        compiler_params=_params(("parallel", "arbitrary")),
    )(it, jt, qb, ka, kt, va, do, proj, proj, qg, kg)


def _forget_bwd(dcum, fl, bf_pad):
    s = fl.shape[0]
    tc = min(TC_CUM, s)
    n = s // tc

    def body(dc_ref, fl_ref, bf_ref, df_ref, dbf_ref, carry):
        @pl.when(pl.program_id(0) == 0)
        def _():
            carry[...] = jnp.zeros_like(carry)
            dbf_ref[...] = jnp.zeros_like(dbf_ref)
        r = lax.broadcasted_iota(jnp.int32, (tc, tc), 0)
        cidx = lax.broadcasted_iota(jnp.int32, (tc, tc), 1)
        tri = (cidx >= r).astype(F32)
        dc = dc_ref[0]
        for grp in range(1, dcum.shape[0]):
            dc = dc + dc_ref[grp]
        dlf = jnp.dot(tri, dc, preferred_element_type=F32, precision=lax.Precision.HIGHEST) + carry[...]
        carry[...] += jnp.sum(dc, axis=0, keepdims=True)
        lane = lax.broadcasted_iota(jnp.int32, (tc, LANES), 1)
        dfl = jnp.where(lane < HEADS, dlf * _sigmoid(-(fl_ref[...] + bf_ref[...])), 0.0)
        df_ref[...] = dfl.astype(BF16)
        dbf_ref[...] += jnp.sum(dfl, axis=0, keepdims=True)

    rev = pl.BlockSpec((tc, LANES), lambda i: (n - 1 - i, 0))
    vec = pl.BlockSpec((1, LANES), lambda i: (0, 0))
    return pl.pallas_call(
        body, name="forget_bwd", grid=(n,),
        in_specs=[pl.BlockSpec((dcum.shape[0], tc, LANES), lambda i: (0, n - 1 - i, 0)), rev, vec],
        out_specs=[rev, vec],
        out_shape=[jax.ShapeDtypeStruct((s, LANES), BF16), jax.ShapeDtypeStruct((1, LANES), F32)],
        scratch_shapes=[pltpu.VMEM((1, LANES), F32)],
        compiler_params=_params(("arbitrary",)),
    )(dcum, fl, bf_pad)


def _conv_bwd(dob, proj, conv_w):
    s = dob.shape[0]
    tm = min(TM_ELEM, s)
    tiles, befores, afters = _conv_specs(tm, s)

    def body(dob_ref, dnext_ref, gb_ref, gc_ref, u_ref, zb_ref, gch_ref, uh_ref, gbn_ref, zbn_ref, w_ref,
             dgb_ref, dgc_ref, du_ref, dzb_ref, dw_ref):
        i = pl.program_id(1)

        @pl.when(i == 0)
        def _():
            dw_ref[...] = jnp.zeros_like(dw_ref)
        gb, gc, u, zb, cu, r1, r2, conv = _conv_parts(gb_ref, gc_ref, u_ref, zb_ref, gch_ref, uh_ref, i == 0, w_ref, tm)
        g = dob_ref[...]
        sg = _sigmoid(zb)
        sz = zb * sg
        dconv = g * gb * sz
        zn = zbn_ref[0:8, :].astype(F32)
        dcn = jnp.where(i == pl.num_programs(1) - 1, 0.0,
                        dnext_ref[...] * gbn_ref[0:8, :].astype(F32) * (zn * _sigmoid(zn)))
        nxt1, nxt2 = _sub_row(dcn, 0), _sub_row(dcn, 1)
        row = lax.broadcasted_iota(jnp.int32, (tm, LANES), 0)
        f1 = jnp.where(row == tm - 1, nxt1, pltpu.roll(dconv, tm - 1, 0))
        f2 = jnp.where(row == tm - 2, nxt1, jnp.where(row == tm - 1, nxt2, pltpu.roll(dconv, tm - 2, 0)))
        dcu = w_ref[2:3, :] * dconv + w_ref[1:2, :] * f1 + w_ref[0:1, :] * f2
        dgb_ref[...] = (g * conv * sz).astype(BF16)
        dgc_ref[...] = (dcu * u).astype(BF16)
        du_ref[...] = (dcu * gc).astype(BF16)
        dzb_ref[...] = (g * gb * conv * (sg * (1.0 + zb * (1.0 - sg)))).astype(BF16)
        w_row = lax.broadcasted_iota(jnp.int32, (3, LANES), 0)
        dw0 = jnp.sum(dconv * r2, axis=0, keepdims=True)
        dw1 = jnp.sum(dconv * r1, axis=0, keepdims=True)
        dw2 = jnp.sum(dconv * cu, axis=0, keepdims=True)
        dw_ref[...] += jnp.where(w_row == 0, dw0, jnp.where(w_row == 1, dw1, dw2))

    blk = pl.BlockSpec((tm, LANES), lambda c, i: (i, c))
    nxt = pl.BlockSpec((8, LANES), lambda c, i: (jnp.minimum((i + 1) * (tm // 8), s // 8 - 1), c))
    wspec = pl.BlockSpec((3, LANES), lambda c, i: (0, c))
    return pl.pallas_call(
        body, name="conv_bwd", grid=(CONV_W // LANES, s // tm),
        in_specs=[blk, nxt] + tiles + befores + afters + [wspec],
        out_specs=[blk, blk, blk, blk, wspec],
        out_shape=[jax.ShapeDtypeStruct((s, CONV_W), BF16)] * 4 + [jax.ShapeDtypeStruct((3, CONV_W), F32)],
        compiler_params=_params(("parallel", "arbitrary")),
    )(dob, dob, *([proj] * 8), conv_w)


def _piece_layout(pieces):
    offs, off = [], 0
    for p in pieces:
        offs.append((off, p.shape[1]))
        off += p.shape[1]
    assert off == N_ALL, off
    return offs


def _dw_in(h, pieces, chip_sums):
    s = h.shape[0]
    tk, tn = min(TK_DW, s), TN_DW
    nk = s // tk
    nn = N_MAIN // tn
    main, fpiece = pieces[:-1], pieces[-1]
    layout = _piece_layout(pieces)[:-1]
    n_main = len(main)
    nx = len(chip_sums)

    def body(*refs):
        p_refs, f_ref, h_ref = refs[:n_main], refs[n_main], refs[n_main + 1]
        ins, refs = refs[n_main + 2:n_main + 2 + nx], refs[n_main + 2 + nx:]
        out_ref, outf_ref = refs[:2]
        outs, (acc, accf, send_sems, recv_sems, local_sems) = refs[2:2 + nx], refs[2 + nx:]
        n, k = pl.program_id(0), pl.program_id(1)

        @pl.when(jnp.logical_and(n == 0, k == 0))
        def _():
            for cp in _chip_copies(ins, outs, send_sems, recv_sems, local_sems):
                cp.start()

        @pl.when(k == 0)
        def _():
            acc[...] = jnp.zeros_like(acc)
        hv = h_ref[pl.ds(pl.multiple_of(k * tk, tk), tk), :]
        for p_ref, (off, width) in zip(p_refs, layout):
            @pl.when(jnp.logical_and(n >= off // tn, n < (off + width) // tn))
            def _():
                acc[...] += _dot_tn(p_ref[...], hv)

        @pl.when(k == nk - 1)
        def _():
            out_ref[...] = acc[...].astype(BF16)

        @pl.when(n == 0)
        def _():
            @pl.when(k == 0)
            def _():
                accf[...] = jnp.zeros_like(accf)
            accf[...] += _dot_tn(f_ref[...], hv)

            @pl.when(k == nk - 1)
            def _():
                outf_ref[...] = accf[...].astype(BF16)

        @pl.when(jnp.logical_and(n == nn - 1, k == nk - 1))
        def _():
            for cp in _chip_copies(ins, outs, send_sems, recv_sems, local_sems):
                cp.wait()

    def piece_spec(off, width):
        lo, hi = off // tn, (off + width) // tn

        def index(n, k):
            active = jnp.logical_and(n >= lo, n < hi)
            return jnp.where(active, k, 0), jnp.clip(n - lo, 0, hi - lo - 1)
        return pl.BlockSpec((tk, tn), index)

    any_spec = pl.BlockSpec(memory_space=pl.ANY)
    res = pl.pallas_call(
        body, name="dw_in", grid=(nn, nk),
        in_specs=[piece_spec(off, width) for off, width in layout]
        + [pl.BlockSpec((tk, N_FPAD), lambda n, k: (jnp.where(n == 0, k, 0), 0)),
           pl.BlockSpec((s, D_MODEL), lambda n, k: (0, 0))] + [any_spec] * nx,
        out_specs=[pl.BlockSpec((tn, D_MODEL), lambda n, k: (n, 0)),
                   pl.BlockSpec((N_FPAD, D_MODEL), lambda n, k: (0, 0))] + [any_spec] * nx,
        out_shape=[jax.ShapeDtypeStruct((N_MAIN, D_MODEL), BF16), jax.ShapeDtypeStruct((N_FPAD, D_MODEL), BF16)]
        + [jax.ShapeDtypeStruct(a.shape, a.dtype) for a in chip_sums],
        scratch_shapes=[pltpu.VMEM((tn, D_MODEL), F32), pltpu.VMEM((N_FPAD, D_MODEL), F32),
                        pltpu.SemaphoreType.DMA((nx * 3,)), pltpu.SemaphoreType.DMA((nx * 3,)),
                        pltpu.SemaphoreType.DMA((nx,))],
        compiler_params=_params(("arbitrary", "arbitrary")),
    )(*main, fpiece, h, *chip_sums)
    return res[:2], res[2:]


def _dh_and_dx(pieces, w_all_t, x, dy, ada3, norm_g, chip_sums):
    s = x.shape[0]
    tm = min(TM_DH, s)
    nt = s // tm
    n = len(chip_sums)
    npc = len(pieces)
    layout = _piece_layout(pieces)

    def body(*refs):
        p_refs, refs = refs[:npc], refs[npc:]
        wt_ref, x_ref, dy_ref, ada_ref, g_ref = refs[:5]
        ins, refs = refs[5:5 + n], refs[5 + n:]
        gx_ref, dsh_ref, dsc_ref, dg_ref = refs[:4]
        outs, (send_sems, recv_sems, local_sems) = refs[4:4 + n], refs[4 + n:]
        i = pl.program_id(0)

        @pl.when(i == 0)
        def _():
            for cp in _chip_copies(ins, outs, send_sems, recv_sems, local_sems):
                cp.start()
            dsh_ref[...] = jnp.zeros_like(dsh_ref)
            dsc_ref[...] = jnp.zeros_like(dsc_ref)
            dg_ref[...] = jnp.zeros_like(dg_ref)

        dh = None
        for p_ref, (off, width) in zip(p_refs, layout):
            part = _dot(p_ref[...], wt_ref[off:off + width, :])
            dh = part if dh is None else dh + part
        xv = x_ref[...]
        r = lax.rsqrt(jnp.mean(xv * xv, axis=-1, keepdims=True) + EPS)
        xhat = xv * r
        g = g_ref[...]
        one_sc = 1.0 + ada_ref[1:2, :]
        dsh_ref[...] += jnp.sum(dh, axis=0, keepdims=True)
        dsc_ref[...] += jnp.sum(dh * (xhat * g), axis=0, keepdims=True)
        dg_ref[...] += jnp.sum(dh * xhat, axis=0, keepdims=True) * one_sc
        dxh = dh * (g * one_sc)
        dx = r * (dxh - xhat * jnp.mean(dxh * xhat, axis=-1, keepdims=True))
        gx_ref[...] = dy_ref[...] + dx

        @pl.when(i == nt - 1)
        def _():
            for cp in _chip_copies(ins, outs, send_sems, recv_sems, local_sems):
                cp.wait()

    full = pl.BlockSpec((tm, D_MODEL), lambda i: (i, 0))
    vec = pl.BlockSpec((1, D_MODEL), lambda i: (0, 0))
    any_spec = pl.BlockSpec(memory_space=pl.ANY)
    res = pl.pallas_call(
        body, name="dh_dx", grid=(nt,),
        in_specs=[pl.BlockSpec((tm, p.shape[1]), lambda i: (i, 0)) for p in pieces]
        + [pl.BlockSpec((N_ALL, D_MODEL), lambda i: (0, 0)), full, full,
           pl.BlockSpec((3, D_MODEL), lambda i: (0, 0)), vec] + [any_spec] * n,
        out_specs=[full, vec, vec, vec] + [any_spec] * n,
        out_shape=[jax.ShapeDtypeStruct((s, D_MODEL), F32)] + [jax.ShapeDtypeStruct((1, D_MODEL), F32)] * 3
        + [jax.ShapeDtypeStruct(a.shape, a.dtype) for a in chip_sums],
        scratch_shapes=[pltpu.SemaphoreType.DMA((n * 3,)), pltpu.SemaphoreType.DMA((n * 3,)),
                        pltpu.SemaphoreType.DMA((n,))],
        compiler_params=_params(("arbitrary",)),
    )(*pieces, w_all_t, x, dy, ada3, norm_g, *chip_sums)
    return res[:4], res[4:]


def _sum_small(vec_all, qg_parts, kg_parts):
    def body(v_ref, q_ref, k_ref, tot_ref, gq_ref, gk_ref):
        tot = v_ref[0:1, :]
        for p in range(1, N_DEV):
            tot = tot + v_ref[p:p + 1, :]
        tot_ref[...] = tot
        gq_ref[...] = jnp.sum(q_ref[...], axis=0, keepdims=True)
        gk_ref[...] = jnp.sum(k_ref[...], axis=0, keepdims=True)

    n = vec_all.shape[-1]
    return pl.pallas_call(
        body, name="sum_small",
        out_shape=[jax.ShapeDtypeStruct((1, n), F32),
                   jax.ShapeDtypeStruct((1, HEAD_DIM), F32), jax.ShapeDtypeStruct((1, HEAD_DIM), F32)],
        compiler_params=_params(),
    )(vec_all, qg_parts, kg_parts)


def _grad_w_ada(c_cols, dada_rows):
    def body(c_ref, d_ref, out_ref):
        acc = c_ref[0] * d_ref[0]
        for b in range(1, N_DEV):
            acc = acc + c_ref[b] * d_ref[b]
        out_ref[...] = acc

    return pl.pallas_call(
        body, name="grad_w_ada",
        out_shape=jax.ShapeDtypeStruct((D_MODEL, ADA_SHARD), F32),
        compiler_params=_params(),
    )(c_cols, dada_rows)


def _adamw(w, m, v, g_parts, name):
    rows, cols = w.shape
    n_parts = g_parts.shape[0]
    tr = 256 if rows % 256 == 0 else rows
    tc = 256 if (tr == rows and rows > 256 and cols % 256 == 0) else cols
    c1 = 1.0 / (1.0 - ADAM_B1 ** ADAM_STEP)
    c2 = 1.0 / (1.0 - ADAM_B2 ** ADAM_STEP)

    def body(w_ref, m_ref, v_ref, g_ref, go_ref, d_ref, mo_ref, vo_ref):
        g = g_ref[0].astype(F32)
        for p in range(1, n_parts):
            g = g + g_ref[p].astype(F32)
        m_new = ADAM_B1 * m_ref[...] + (1.0 - ADAM_B1) * g
        v_new = ADAM_B2 * v_ref[...] + (1.0 - ADAM_B2) * (g * g)
        go_ref[...] = g
        mo_ref[...] = m_new
        vo_ref[...] = v_new
        d_ref[...] = -ADAM_LR * ((m_new * c1) / (jnp.sqrt(v_new * c2) + ADAM_EPS) + ADAM_WD * w_ref[...])

    blk = pl.BlockSpec((tr, tc), lambda i, j: (i, j))
    return pl.pallas_call(
        body, name=name, grid=(rows // tr, cols // tc),
        in_specs=[blk, blk, blk, pl.BlockSpec((n_parts, tr, tc), lambda i, j: (0, i, j))],
        out_specs=[blk] * 4,
        out_shape=[jax.ShapeDtypeStruct((rows, cols), F32)] * 4,
        compiler_params=_params(("parallel", "parallel")),
    )(w, m, v, g_parts)


_O_F = 1536


def _to_internal(wt_g):
    wf = wt_g.reshape(IN_WIDTH, D_MODEL)
    f = jnp.pad(wf[_O_F:_O_F + HEADS], ((0, N_FPAD - HEADS), (0, 0)))
    return jnp.concatenate([wf[:_O_F], wf[_O_F + HEADS:], f], axis=0)


def _slabs_by_core(dwt, dwt_f):
    sources = ((dwt, 0, _O_F, 0), (dwt_f, _O_F, _O_F + HEADS, _O_F), (dwt, _O_F + HEADS, IN_WIDTH, HEADS))

    def slab(p):
        lo, hi = p * IN_SHARD, (p + 1) * IN_SHARD
        parts = []
        for src, o_lo, o_hi, shift in sources:
            a, b = max(lo, o_lo), min(hi, o_hi)
            if a < b:
                parts.append(src[a - shift:b - shift])
        return parts[0] if len(parts) == 1 else jnp.concatenate(parts, axis=0)

    return jnp.stack([jnp.stack([slab(2 * chip + core) for chip in range(4)]) for core in range(2)])


def kernel(x, c, w_ada, b_ada, norm_g, w_in, b_f, q_norm_g, k_norm_g, conv_w, w_attn_out, w_conv_out, w_o, loss_target, m_w_ada, m_b_ada, m_norm_g, m_w_in, m_b_f, m_q_norm_g, m_k_norm_g, m_conv_w, m_w_attn_out, m_w_conv_out, m_w_o, v_w_ada, v_b_ada, v_norm_g, v_w_in, v_b_f, v_q_norm_g, v_k_norm_g, v_conv_w, v_w_attn_out, v_w_conv_out, v_w_o):
    me = 4 * lax.axis_index("x") + 2 * lax.axis_index("y") + lax.axis_index("c")
    s = x.shape[1]
    x2, t2 = x[0], loss_target[0]

    w_in_g, c_all, ada_g = _gather_weights_and_ada(w_in[0].T.astype(BF16), c, w_ada[0])
    ada_mine = lax.dynamic_index_in_dim(ada_g[:, :, 0, :], me, axis=1, keepdims=False)
    ada3 = (ada_mine.reshape(1, 3 * D_MODEL) + b_ada).reshape(3, D_MODEL)
    w_all_t = _to_internal(w_in_g)
    qg = jnp.tile(q_norm_g, (1, HEADS))
    kg = jnp.tile(k_norm_g, (1, HEADS))
    bf_pad = jnp.pad(b_f, ((0, 0), (0, LANES - HEADS)))

    (proj, fl, h), (cw_g, wa_g, wb_g, wo_g) = _proj_fwd(
        x2, ada3, norm_g, w_all_t,
        [conv_w[0], w_attn_out[0].astype(BF16), w_conv_out[0].astype(BF16), w_o[0].astype(BF16)])
    wa = jnp.transpose(wa_g, (1, 0, 2)).reshape(ATTN_W, D_MODEL)
    wb = jnp.transpose(wb_g, (1, 0, 2)).reshape(CONV_W, D_MODEL)
    wo = wo_g.reshape(D_MODEL, D_MODEL)
    cw = jnp.transpose(cw_g, (1, 0, 2)).reshape(3, CONV_W)
    qa, ka, va, kt, vt = _qkv_prep(proj, fl, bf_pad, qg, kg)
    attn, oa, qb = _attn_fwd(qa, ka, vt, proj)
    (dy, dgab, do, dza, dob, dwo, dwa, dwb, dgate, loss_part) = _tail(oa, attn, proj, x2, t2, ada3, wa, wb, wo, cw)

    def by_core(slabs8):
        return jnp.swapaxes(slabs8.reshape((4, 2) + slabs8.shape[1:]), 0, 1).astype(BF16)

    core = lax.axis_index("c").astype(jnp.int32).reshape(1)
    small = [by_core(jnp.transpose(dwa.reshape(ATTN_W, N_DEV, LANES), (1, 0, 2))),
             by_core(jnp.transpose(dwb.reshape(CONV_W, N_DEV, LANES), (1, 0, 2))),
             by_core(dwo.reshape(N_DEV, D_MODEL // N_DEV, D_MODEL))]
    small_sums = [_pair_sum(m2, t4, core, "pair_sum_" + nm)
                  for m2, t4, nm in zip(small, _sibling_swap(small, "swap_small"), ("wa", "wb", "wo"))]
    dq, dk, dv, dqg, dkg, dcum = _attn_bwd(qb, ka, kt, va, do, proj, qg, kg)
    df, dbf = _forget_bwd(dcum, fl, bf_pad)
    dcb, dcc, dcu, dcz, dcw = _conv_bwd(dob, proj, cw)
    pieces = [dq, dk, dv, dza, dcb, dcc, dcu, dcz, dgab, df]
    (dw_main, dw_f), (g_wa_parts, g_wb_parts, g_wo_parts) = _dw_in(h, pieces, small_sums)

    slabs_in = _slabs_by_core(dw_main, dw_f)
    (theirs_in,) = _sibling_swap([slabs_in], "swap_w_in")
    (grad_x, dshift, dscale, dnormg), (g_in_parts,) = _dh_and_dx(
        pieces, w_all_t, x2, dy, ada3, norm_g, [_pair_sum(slabs_in, theirs_in, core, "pair_sum_w_in")])
    vec = jnp.concatenate([dshift, dscale, dgate, dnormg, dbf, dcw.reshape(1, 3 * CONV_W), loss_part, dqg, dkg],
                          axis=1)
    (vec_all,) = _gather_direct([vec], "gather_small")
    vec_all = vec_all.reshape(N_DEV, vec.shape[1])
    n_main = 4 * D_MODEL + LANES + 3 * CONV_W + LANES
    tot, g_qg, g_kg = _sum_small(
        vec_all[:, :n_main],
        vec_all[:, n_main:n_main + ATTN_W].reshape(N_DEV * HEADS, HEAD_DIM),
        vec_all[:, n_main + ATTN_W:].reshape(N_DEV * HEADS, HEAD_DIM))
    g_b_ada = tot[:, 0:3 * D_MODEL]
    g_norm_g = tot[:, 3 * D_MODEL:4 * D_MODEL]
    g_b_f = tot[:, 4 * D_MODEL:4 * D_MODEL + HEADS]
    g_cw_full = tot[:, 4 * D_MODEL + LANES:4 * D_MODEL + LANES + 3 * CONV_W].reshape(3, CONV_W)
    g_cw = lax.dynamic_slice(g_cw_full, (0, me * (CONV_W // N_DEV)), (3, CONV_W // N_DEV))
    dada_mine = lax.dynamic_slice(vec_all[:, 0:3 * D_MODEL], (0, me * ADA_SHARD), (N_DEV, ADA_SHARD))
    g_w_ada = _grad_w_ada(jnp.transpose(c_all, (0, 2, 1)), dada_mine.reshape(N_DEV, 1, ADA_SHARD))

    upd = {}
    upd["w_ada"] = _adamw(w_ada[0], m_w_ada[0], v_w_ada[0], g_w_ada[None], "adamw_w_ada")
    upd["b_ada"] = _adamw(b_ada, m_b_ada, v_b_ada, g_b_ada[None], "adamw_b_ada")
    upd["norm_g"] = _adamw(norm_g, m_norm_g, v_norm_g, g_norm_g[None], "adamw_norm_g")
    upd["w_in"] = [u.T for u in _adamw(w_in[0].T, m_w_in[0].T, v_w_in[0].T, g_in_parts, "adamw_w_in")]
    upd["b_f"] = _adamw(b_f, m_b_f, v_b_f, g_b_f[None], "adamw_b_f")
    upd["q_norm_g"] = _adamw(q_norm_g, m_q_norm_g, v_q_norm_g, g_qg[None], "adamw_q_norm_g")
    upd["k_norm_g"] = _adamw(k_norm_g, m_k_norm_g, v_k_norm_g, g_kg[None], "adamw_k_norm_g")
    upd["conv_w"] = _adamw(conv_w[0], m_conv_w[0], v_conv_w[0], g_cw[None], "adamw_conv_w")
    upd["w_attn_out"] = _adamw(w_attn_out[0], m_w_attn_out[0], v_w_attn_out[0], g_wa_parts, "adamw_w_attn_out")
    upd["w_conv_out"] = _adamw(w_conv_out[0], m_w_conv_out[0], v_w_conv_out[0], g_wb_parts, "adamw_w_conv_out")
    upd["w_o"] = _adamw(w_o[0], m_w_o[0], v_w_o[0], g_wo_parts, "adamw_w_o")

    names = ["w_ada", "b_ada", "norm_g", "w_in", "b_f", "q_norm_g", "k_norm_g", "conv_w",
             "w_attn_out", "w_conv_out", "w_o"]
    lead = {"w_ada", "w_in", "conv_w", "w_attn_out", "w_conv_out", "w_o"}
    fix = lambda n, a: a[None] if n in lead else a
    loss = tot[0, n_main - LANES]
    outs = [loss, grad_x[None]]
    for k in range(4):
        outs += [fix(n, upd[n][k]) for n in names]
    return tuple(outs)
```

```python
import functools

import numpy as np
import jax
import jax.numpy as jnp
from jax import lax
from jax.experimental import pallas as pl
from jax.experimental.pallas import tpu as pltpu

F32 = jnp.float32
BF16 = jnp.bfloat16

D_MODEL = 1024
HEADS = 8
HEAD_DIM = 64
ATTN_W = 512
CONV_W = 512
N_DEV = 8
IN_WIDTH = 6152
IN_SHARD = IN_WIDTH // N_DEV
N_MAIN = 6144
N_FPAD = 128
N_ALL = N_MAIN + N_FPAD
ADA_SHARD = 3 * D_MODEL // N_DEV
EPS = 1e-6
NEG = -1e30

ADAM_LR = 0.001
ADAM_B1 = 0.9
ADAM_B2 = 0.999
ADAM_EPS = 1e-08
ADAM_WD = 0.01
ADAM_STEP = 10

LANES = 128
VMEM_LIMIT = 56 * 1024 * 1024

TM_PROJ = 256
TN_PROJ = 1024
TM_ELEM = 512
TQ = 512
HEADS_PER_STEP = 8
HEADS_PER_STEP_FWD = 8
TM_TAIL = 256
TC_CUM = 256
TK_DW = 2048
TN_DW = 512
TM_DH = 256
HALO = 16

OFF_Q, OFF_K, OFF_V, OFF_ZA, OFF_CB, OFF_CC, OFF_CU, OFF_CZ, OFF_GA, OFF_GB = (
    0, 512, 1024, 1536, 2048, 2560, 3072, 3584, 4096, 5120)


def _params(sem=None):
    return pltpu.CompilerParams(dimension_semantics=sem, vmem_limit_bytes=VMEM_LIMIT)


def _dot(a, b):
    return jnp.dot(a, b, preferred_element_type=F32)


def _dot_nt(a, b):
    return lax.dot_general(a, b, (((1,), (1,)), ((), ())), preferred_element_type=F32)


def _dot_tn(a, b):
    return lax.dot_general(a, b, (((0,), (0,)), ((), ())), preferred_element_type=F32)


def _sigmoid(x):
    return 1.0 / (1.0 + jnp.exp(-x))


def _lane_lo(shape):
    return lax.broadcasted_iota(jnp.int32, shape, len(shape) - 1) < HEAD_DIM


def _seg_sum(z, lo):
    a = jnp.sum(jnp.where(lo, z, 0.0), axis=-1, keepdims=True)
    b = jnp.sum(jnp.where(lo, 0.0, z), axis=-1, keepdims=True)
    return jnp.where(lo, a, b)


def _lane_col(z, lane):
    idx = lax.broadcasted_iota(jnp.int32, z.shape, 1)
    return jnp.sum(jnp.where(idx == lane, z, 0.0), axis=-1, keepdims=True)


def _sub_row(z, row):
    idx = lax.broadcasted_iota(jnp.int32, z.shape, 0)
    return jnp.sum(jnp.where(idx == row, z, 0.0), axis=0, keepdims=True)


def _mesh_pos():
    x, y, c = lax.axis_index("x"), lax.axis_index("y"), lax.axis_index("c")
    return x, y, c, 4 * x + 2 * y + c


def _peer(k, x, y, c):
    px = 1 - x if (k >> 2) & 1 else x
    py = 1 - y if (k >> 1) & 1 else y
    pc = 1 - c if k & 1 else c
    return (px, py, pc), 4 * px + 2 * py + pc


def _gather_copies(ins, outs, send_sems, recv_sems, local_sems):
    x, y, c, me = _mesh_pos()
    copies = []
    for a in range(len(ins)):
        copies.append(pltpu.make_async_copy(ins[a], outs[a].at[me], local_sems.at[a]))
        for k in range(1, N_DEV):
            dev, _ = _peer(k, x, y, c)
            copies.append(pltpu.make_async_remote_copy(
                src_ref=ins[a], dst_ref=outs[a].at[me],
                send_sem=send_sems.at[a * (N_DEV - 1) + k - 1], recv_sem=recv_sems.at[a * (N_DEV - 1) + k - 1],
                device_id=dev, device_id_type=pl.DeviceIdType.MESH))
    return copies


def _gather_sems(n):
    return [pltpu.SemaphoreType.DMA((n * (N_DEV - 1),)), pltpu.SemaphoreType.DMA((n * (N_DEV - 1),)),
            pltpu.SemaphoreType.DMA((n,))]


def _gather_direct(arrs, name):
    n = len(arrs)
    any_spec = pl.BlockSpec(memory_space=pl.ANY)

    def body(*refs):
        copies = _gather_copies(refs[:n], refs[n:2 * n], *refs[2 * n:])
        for cp in copies:
            cp.start()
        for cp in copies:
            cp.wait()

    return pl.pallas_call(
        body, name=name, out_shape=[jax.ShapeDtypeStruct((N_DEV,) + a.shape, a.dtype) for a in arrs],
        in_specs=[any_spec] * n, out_specs=[any_spec] * n, scratch_shapes=_gather_sems(n),
    )(*arrs)


def _ada_phase(c_ref, w_ref, call_ref, adag_ref, mine_ref, send_sems, recv_sems):
    x, y, c, me = _mesh_pos()

    def copy(phase, k, src, dst):
        dev, _ = _peer(k, x, y, c)
        return pltpu.make_async_remote_copy(
            src_ref=src, dst_ref=dst,
            send_sem=send_sems.at[phase * (N_DEV - 1) + k - 1],
            recv_sem=recv_sems.at[phase * (N_DEV - 1) + k - 1],
            device_id=dev, device_id_type=pl.DeviceIdType.MESH)

    call_ref[me] = c_ref[...]
    first = [copy(0, k, c_ref, call_ref.at[me]) for k in range(1, N_DEV)]
    for cp in first:
        cp.start()
    for cp in first:
        cp.wait()
    wb = w_ref[...].astype(BF16)
    for b in range(N_DEV):
        row = jnp.broadcast_to(call_ref[b], (8, D_MODEL)).astype(BF16)
        mine_ref[b] = _sub_row(_dot(row, wb), 0)
    adag_ref[me] = mine_ref[...]
    second = [copy(1, k, mine_ref, adag_ref.at[me]) for k in range(1, N_DEV)]
    for cp in second:
        cp.start()
    for cp in second:
        cp.wait()


def _gather_weights_and_ada(wt_shard, c_row, w_ada_sh):
    any_spec = pl.BlockSpec(memory_space=pl.ANY)
    vm = pl.BlockSpec(memory_space=pltpu.VMEM)

    def body(w_in_ref, c_ref, wada_ref, out_ref, call_ref, adag_ref, mine_ref, send_sems, recv_sems, local_sem,
             ada_send, ada_recv):
        x, y, c, me = _mesh_pos()
        sibling = (x, y, 1 - c)
        chips = [(1 - x, y), (x, 1 - y), (1 - x, 1 - y)]

        def copy(k, src, blk, to):
            return pltpu.make_async_remote_copy(
                src_ref=src, dst_ref=out_ref.at[blk], send_sem=send_sems.at[k], recv_sem=recv_sems.at[k],
                device_id=to, device_id_type=pl.DeviceIdType.MESH)

        local = pltpu.make_async_copy(w_in_ref, out_ref.at[me], local_sem.at[0])
        local.start()
        first = [copy(0, w_in_ref, me, sibling)]
        first += [copy(1 + j, w_in_ref, me, (px, py, c)) for j, (px, py) in enumerate(chips)]
        for cp in first:
            cp.start()
        _ada_phase(c_ref, wada_ref, call_ref, adag_ref, mine_ref, ada_send, ada_recv)
        passed = []
        for j, (px, py) in enumerate(chips):
            blk = 4 * px + 2 * py + c
            copy(1 + j, w_in_ref, blk, (x, y, c)).wait_recv()
            fwd = copy(4 + j, out_ref.at[blk], blk, sibling)
            fwd.start()
            passed.append(fwd)
        copy(0, w_in_ref, 4 * x + 2 * y + 1 - c, (x, y, c)).wait_recv()
        for j, (px, py) in enumerate(chips):
            copy(4 + j, w_in_ref, 4 * px + 2 * py + 1 - c, (x, y, c)).wait_recv()
        for cp in first + passed:
            cp.wait_send()
        local.wait()

    per = N_DEV - 1
    return pl.pallas_call(
        body, name="gather_weights",
        out_shape=[jax.ShapeDtypeStruct((N_DEV,) + wt_shard.shape, wt_shard.dtype),
                   jax.ShapeDtypeStruct((N_DEV, 1, D_MODEL), F32),
                   jax.ShapeDtypeStruct((N_DEV, N_DEV, 1, ADA_SHARD), F32)],
        in_specs=[any_spec, vm, vm], out_specs=[any_spec, vm, vm],
        scratch_shapes=[pltpu.VMEM((N_DEV, 1, ADA_SHARD), F32),
                        pltpu.SemaphoreType.DMA((per,)), pltpu.SemaphoreType.DMA((per,)),
                        pltpu.SemaphoreType.DMA((1,)),
                        pltpu.SemaphoreType.DMA((2 * per,)), pltpu.SemaphoreType.DMA((2 * per,))],
        compiler_params=pltpu.CompilerParams(vmem_limit_bytes=VMEM_LIMIT),
    )(wt_shard, c_row, w_ada_sh)


def _sibling_swap(arrs, name):
    n = len(arrs)
    any_spec = pl.BlockSpec(memory_space=pl.ANY)

    def body(*refs):
        ins, outs = refs[:n], refs[n:2 * n]
        send_sems, recv_sems = refs[2 * n:]
        x, y, c, _ = _mesh_pos()
        copies = [pltpu.make_async_remote_copy(
            src_ref=ins[a].at[1 - c], dst_ref=outs[a], send_sem=send_sems.at[a], recv_sem=recv_sems.at[a],
            device_id=(x, y, 1 - c), device_id_type=pl.DeviceIdType.MESH) for a in range(n)]
        for cp in copies:
            cp.start()
        for cp in copies:
            cp.wait()

    return pl.pallas_call(
        body, name=name,
        out_shape=[jax.ShapeDtypeStruct(a.shape[1:], a.dtype) for a in arrs],
        in_specs=[any_spec] * n, out_specs=[any_spec] * n,
        scratch_shapes=[pltpu.SemaphoreType.DMA((n,)), pltpu.SemaphoreType.DMA((n,))],
    )(*arrs)


def _pair_sum(mine2, theirs, core, name):
    _, _, rows, cols = mine2.shape
    tr = 256 if rows % 256 == 0 else rows

    def body(core_ref, a_ref, b_ref, out_ref):
        out_ref[...] = (a_ref[...].astype(F32) + b_ref[...].astype(F32)).astype(BF16)

    return pl.pallas_call(
        body, name=name,
        grid_spec=pltpu.PrefetchScalarGridSpec(
            num_scalar_prefetch=1, grid=(4, rows // tr),
            in_specs=[pl.BlockSpec((None, None, tr, cols), lambda ch, i, core_: (core_[0], ch, i, 0)),
                      pl.BlockSpec((None, tr, cols), lambda ch, i, core_: (ch, i, 0))],
            out_specs=pl.BlockSpec((None, tr, cols), lambda ch, i, core_: (ch, i, 0))),
        out_shape=jax.ShapeDtypeStruct(theirs.shape, BF16),
        compiler_params=_params(("parallel", "parallel")),
    )(core, mine2, theirs)


def _chip_copies(ins, outs, send_sems, recv_sems, local_sems):
    x, y, c, _ = _mesh_pos()
    my_chip = 2 * x + y
    chips = [(1 - x, y), (x, 1 - y), (1 - x, 1 - y)]
    copies = []
    for a in range(len(ins)):
        copies.append(pltpu.make_async_copy(ins[a].at[my_chip], outs[a].at[my_chip], local_sems.at[a]))
        for j, (px, py) in enumerate(chips):
            copies.append(pltpu.make_async_remote_copy(
                src_ref=ins[a].at[2 * px + py], dst_ref=outs[a].at[my_chip],
                send_sem=send_sems.at[a * 3 + j], recv_sem=recv_sems.at[a * 3 + j],
                device_id=(px, py, c), device_id_type=pl.DeviceIdType.MESH))
    return copies


def _proj_fwd(x, ada3, norm_g, w_all_t, later):
    s = x.shape[0]
    tm, tn = min(TM_PROJ, s), TN_PROJ
    nt = s // tm
    n = len(later)

    def body(x_ref, ada_ref, g_ref, wt_ref, *rest):
        ins, (proj_ref, fl_ref, h_ref), rest = rest[:n], rest[n:n + 3], rest[n + 3:]
        outs, sems = rest[:n], rest[n:]
        i = pl.program_id(0)

        @pl.when(i == 0)
        def _():
            for cp in _gather_copies(ins, outs, *sems):
                cp.start()

        xv = x_ref[...]
        r = lax.rsqrt(jnp.mean(xv * xv, axis=-1, keepdims=True) + EPS)
        hv = ((xv * r) * g_ref[...]) * (1.0 + ada_ref[1:2, :]) + ada_ref[0:1, :]
        hb = hv.astype(BF16)
        h_ref[...] = hb
        fl_ref[...] = _dot_nt(hb, wt_ref[N_MAIN:N_ALL, :])
        for j in range(N_MAIN // tn):
            proj_ref[:, j * tn:(j + 1) * tn] = _dot_nt(hb, wt_ref[j * tn:(j + 1) * tn, :]).astype(BF16)

        @pl.when(i == nt - 1)
        def _():
            for cp in _gather_copies(ins, outs, *sems):
                cp.wait()

    any_spec = pl.BlockSpec(memory_space=pl.ANY)
    res = pl.pallas_call(
        body, name="proj_fwd", grid=(nt,),
        in_specs=[pl.BlockSpec((tm, D_MODEL), lambda i: (i, 0)),
                  pl.BlockSpec((3, D_MODEL), lambda i: (0, 0)),
                  pl.BlockSpec((1, D_MODEL), lambda i: (0, 0)),
                  pl.BlockSpec((N_ALL, D_MODEL), lambda i: (0, 0))] + [any_spec] * n,
        out_specs=[pl.BlockSpec((tm, N_MAIN), lambda i: (i, 0)),
                   pl.BlockSpec((tm, N_FPAD), lambda i: (i, 0)),
                   pl.BlockSpec((tm, D_MODEL), lambda i: (i, 0))] + [any_spec] * n,
        out_shape=[jax.ShapeDtypeStruct((s, N_MAIN), BF16),
                   jax.ShapeDtypeStruct((s, N_FPAD), F32),
                   jax.ShapeDtypeStruct((s, D_MODEL), BF16)]
        + [jax.ShapeDtypeStruct((N_DEV,) + a.shape, a.dtype) for a in later],
        scratch_shapes=_gather_sems(n),
        compiler_params=_params(("arbitrary",)),
    )(x, ada3, norm_g, w_all_t, *later)
    return res[:3], res[3:]


L_ONE_Q, L_F_Q, L_LSE_Q, L_END = HEAD_DIM, HEAD_DIM + 3, HEAD_DIM + 6, HEAD_DIM + 9


def _split3(f):
    hi = f.astype(BF16).astype(F32)
    r = f - hi
    mid = r.astype(BF16).astype(F32)
    return hi, mid, r - mid


def _place3(lane, first, parts, otherwise):
    a, b, c = parts
    return jnp.where(lane == first, a, jnp.where(lane == first + 1, b, jnp.where(lane == first + 2, c, otherwise)))


def _log_forget(fl, bf):
    z = fl + bf
    lf = jnp.minimum(z, 0.0) - jnp.log1p(jnp.exp(-jnp.abs(z)))
    lane = lax.broadcasted_iota(jnp.int32, z.shape, 1)
    return jnp.where(lane < HEADS, lf, 0.0)


def _qkv_prep(proj, fl, bf_pad, qg, kg):
    s = proj.shape[0]
    tm = min(TM_ELEM, s)
    scale = HEAD_DIM ** -0.5

    def body(p_ref, fl_ref, bf_ref, qg_ref, kg_ref, qa_ref, ka_ref, va_ref, kt_ref, vt_ref, carry):
        @pl.when(pl.program_id(0) == 0)
        def _():
            carry[...] = jnp.zeros_like(carry)
        tri = (lax.broadcasted_iota(jnp.int32, (tm, tm), 1) <= lax.broadcasted_iota(jnp.int32, (tm, tm), 0)).astype(F32)
        cum_v = jnp.dot(tri, _log_forget(fl_ref[...], bf_ref[...]), preferred_element_type=F32,
                        precision=lax.Precision.HIGHEST) + carry[...]
        carry[...] = _sub_row(cum_v, tm - 1)
        lane = lax.broadcasted_iota(jnp.int32, (tm, LANES), 1)
        lo = lane < HEAD_DIM
        v_tail = jnp.where(lane < L_F_Q, 1.0, 0.0)
        for pr in range(ATTN_W // LANES):
            sl = slice(pr * LANES, (pr + 1) * LANES)
            q2 = p_ref[:, OFF_Q + pr * LANES:OFF_Q + (pr + 1) * LANES].astype(F32)
            k2 = p_ref[:, OFF_K + pr * LANES:OFF_K + (pr + 1) * LANES].astype(F32)
            v2 = p_ref[:, OFF_V + pr * LANES:OFF_V + (pr + 1) * LANES].astype(F32)
            rq = lax.rsqrt(_seg_sum(q2 * q2, lo) * (1.0 / HEAD_DIM) + EPS)
            rk = lax.rsqrt(_seg_sum(k2 * k2, lo) * (1.0 / HEAD_DIM) + EPS)
            qn = ((q2 * rq) * qg_ref[:, sl]) * scale
            kn = (k2 * rk) * kg_ref[:, sl]
            for hh in range(2):
                h = 2 * pr + hh
                f3 = _split3(_lane_col(cum_v, h))
                qh = qn if hh == 0 else pltpu.roll(qn, HEAD_DIM, 1)
                kh = kn if hh == 0 else pltpu.roll(kn, HEAD_DIM, 1)
                vh = v2 if hh == 0 else pltpu.roll(v2, HEAD_DIM, 1)
                q_tail = jnp.where(lane < L_F_Q, 1.0, _place3(lane, L_F_Q, f3, 0.0))
                k_tail = _place3(lane, L_ONE_Q, tuple(-f for f in f3), jnp.where(lane < L_END, 1.0, 0.0))
                k_row = jnp.where(lo, kh, k_tail)
                v_row = jnp.where(lo, vh, v_tail)
                qa_ref[h] = jnp.where(lo, qh, q_tail).astype(BF16)
                ka_ref[h] = k_row.astype(BF16)
                va_ref[h] = v_row.astype(BF16)
                kt_ref[h] = k_row.T.astype(BF16)
                vt_ref[h] = v_row.T.astype(BF16)

    heads = pl.BlockSpec((HEADS, tm, LANES), lambda i: (0, i, 0))
    heads_t = pl.BlockSpec((HEADS, LANES, tm), lambda i: (0, 0, i))
    vec = pl.BlockSpec((1, ATTN_W), lambda i: (0, 0))
    return pl.pallas_call(
        body, name="qkv_prep", grid=(s // tm,),
        in_specs=[pl.BlockSpec((tm, 3 * ATTN_W), lambda i: (i, 0)),
                  pl.BlockSpec((tm, LANES), lambda i: (i, 0)),
                  pl.BlockSpec((1, LANES), lambda i: (0, 0)), vec, vec],
        out_specs=[heads, heads, heads, heads_t, heads_t],
        out_shape=[jax.ShapeDtypeStruct((HEADS, s, LANES), BF16)] * 3
        + [jax.ShapeDtypeStruct((HEADS, LANES, s), BF16)] * 2,
        scratch_shapes=[pltpu.VMEM((1, LANES), F32)],
        compiler_params=_params(("arbitrary",)),
    )(proj, fl, bf_pad, qg, kg)


def _causal_t(t):
    return lax.broadcasted_iota(jnp.int32, (t, t), 0) <= lax.broadcasted_iota(jnp.int32, (t, t), 1)


def _tri_steps(nt, q_major):
    if q_major:
        pairs = [(i, j) for i in range(nt) for j in range(i + 1)]
    else:
        pairs = [(i, j) for j in range(nt) for i in range(j, nt)]
    return (jnp.asarray(np.array([p[0] for p in pairs], np.int32)),
            jnp.asarray(np.array([p[1] for p in pairs], np.int32)))


def _attn_fwd(qa, ka, vt, proj):
    s = qa.shape[1]
    t = min(TQ, s)
    it, jt = _tri_steps(s // t, True)
    hp = HEADS_PER_STEP_FWD
    wide = hp * HEAD_DIM
    za_blk = OFF_ZA // wide

    def body(it_ref, jt_ref, q_ref, k_ref, vt_ref, za_ref, attn_ref, oa_ref, qb_ref, m_s, acc_s, pair_s):
        step = pl.program_id(1)
        i, j = it_ref[step], jt_ref[step]

        @pl.when(j == 0)
        def _():
            m_s[...] = jnp.full_like(m_s, NEG)
            acc_s[...] = jnp.zeros_like(acc_s)

        def update(masked):
            for hh in range(hp):
                st = _dot_nt(k_ref[hh], q_ref[hh])
                if masked:
                    st = jnp.where(_causal_t(t), st, NEG)
                m_prev = m_s[hh]
                m_next = jnp.maximum(m_prev, jnp.max(st, axis=0, keepdims=True))
                alpha = jnp.exp(m_prev - m_next)
                pt = jnp.exp(st - m_next).astype(BF16)
                acc_s[hh] = acc_s[hh] * alpha + _dot(vt_ref[hh], pt)
                m_s[hh] = m_next

        @pl.when(j < i)
        def _():
            update(False)

        @pl.when(j == i)
        def _():
            update(True)
            row = lax.broadcasted_iota(jnp.int32, (LANES, t), 0)
            lane = lax.broadcasted_iota(jnp.int32, (t, LANES), 1)
            for hh in range(hp):
                l_row = acc_s[hh, L_ONE_Q:L_ONE_Q + 1, :]
                pair_s[hh * HEAD_DIM:(hh + 1) * HEAD_DIM, :] = acc_s[hh, 0:HEAD_DIM, :] / l_row
                lse3 = _split3(m_s[hh] + jnp.log(l_row))
                tail_t = _place3(row, L_LSE_Q, tuple(-x for x in lse3), 0.0)
                keep_q = jnp.logical_or(lane < L_LSE_Q, lane >= L_END)
                qb_ref[hh] = jnp.where(keep_q, q_ref[hh].astype(F32), tail_t.T).astype(BF16)
            out = pair_s[...].T
            attn_ref[...] = out
            z = za_ref[...].astype(F32)
            oa_ref[...] = (out * (z * _sigmoid(z))).astype(BF16)

    pair_q = pl.BlockSpec((hp, t, LANES), lambda p, n, it_, jt_: (p, it_[n], 0))
    pair_k = pl.BlockSpec((hp, t, LANES), lambda p, n, it_, jt_: (p, jt_[n], 0))
    pair_kt = pl.BlockSpec((hp, LANES, t), lambda p, n, it_, jt_: (p, 0, jt_[n]))
    out_q = pl.BlockSpec((t, wide), lambda p, n, it_, jt_: (it_[n], p))
    return pl.pallas_call(
        body, name="attn_fwd",
        grid_spec=pltpu.PrefetchScalarGridSpec(
            num_scalar_prefetch=2, grid=(HEADS // hp, it.shape[0]),
            in_specs=[pair_q, pair_k, pair_kt,
                      pl.BlockSpec((t, wide), lambda p, n, it_, jt_: (it_[n], za_blk + p))],
            out_specs=[out_q, out_q, pair_q],
            scratch_shapes=[pltpu.VMEM((hp, 1, t), F32), pltpu.VMEM((hp, LANES, t), F32),
                            pltpu.VMEM((wide, t), F32)]),
        out_shape=[jax.ShapeDtypeStruct((s, ATTN_W), F32),
                   jax.ShapeDtypeStruct((s, ATTN_W), BF16),
                   jax.ShapeDtypeStruct((HEADS, s, LANES), BF16)],
        compiler_params=_params(("parallel", "arbitrary")),
    )(it, jt, qa, ka, vt, proj)


def _conv_parts(gb_ref, gc_ref, u_ref, zb_ref, gch_ref, uh_ref, first, w_ref, tm):
    gb, gc = gb_ref[...].astype(F32), gc_ref[...].astype(F32)
    u, zb = u_ref[...].astype(F32), zb_ref[...].astype(F32)
    cu = gc * u
    cu_h = jnp.where(first, 0.0, gch_ref[...].astype(F32) * uh_ref[...].astype(F32))
    prev1, prev2 = _sub_row(cu_h, HALO - 1), _sub_row(cu_h, HALO - 2)
    row = lax.broadcasted_iota(jnp.int32, cu.shape, 0)
    r1 = jnp.where(row == 0, prev1, pltpu.roll(cu, 1, 0))
    r2 = jnp.where(row == 0, prev2, jnp.where(row == 1, prev1, pltpu.roll(cu, 2, 0)))
    conv = w_ref[2:3, :] * cu + w_ref[1:2, :] * r1 + w_ref[0:1, :] * r2
    return gb, gc, u, zb, cu, r1, r2, conv


def _conv_specs(tm, s, width=LANES):
    def tile(off):
        return pl.BlockSpec((tm, width), lambda c, i: (i, off // width + c))

    def before(off):
        return pl.BlockSpec((HALO, width), lambda c, i: (jnp.maximum(i * (tm // HALO) - 1, 0), off // width + c))

    def after(off):
        return pl.BlockSpec((HALO, width),
                            lambda c, i: (jnp.minimum((i + 1) * (tm // HALO), s // HALO - 1), off // width + c))

    return ([tile(OFF_CB), tile(OFF_CC), tile(OFF_CU), tile(OFF_CZ)], [before(OFF_CC), before(OFF_CU)],
            [after(OFF_CB), after(OFF_CZ)])


def _tail(oa, attn, proj, x, target, ada3, wa, wb, wo, conv_w):
    s = x.shape[0]
    tm = min(TM_TAIL, s)
    gab_blk = OFF_GA // (2 * D_MODEL)
    za_blk = OFF_ZA // ATTN_W
    tiles, befores, _ = _conv_specs(tm, s, CONV_W)

    def body(oa_ref, attn_ref, za_ref, gb_ref, gc_ref, u_ref, zb_ref, gch_ref, uh_ref, cw_ref, gab_ref, x_ref, t_ref,
             ada_ref, wa_ref, wb_ref, wo_ref,
             dy_ref, dgab_ref, do_ref, dza_ref, dob_ref, dwo_ref, dwa_ref, dwb_ref, dgate_ref, loss_ref):
        first = pl.program_id(0) == 0

        @pl.when(first)
        def _():
            dwo_ref[...] = jnp.zeros_like(dwo_ref)
            dwa_ref[...] = jnp.zeros_like(dwa_ref)
            dwb_ref[...] = jnp.zeros_like(dwb_ref)
            dgate_ref[...] = jnp.zeros_like(dgate_ref)
            loss_ref[...] = jnp.zeros_like(loss_ref)

        gb, _, _, zb, _, _, _, conv = _conv_parts(gb_ref, gc_ref, u_ref, zb_ref, gch_ref, uh_ref, first, cw_ref, tm)
        ob_v = (gb * conv * (zb * _sigmoid(zb))).astype(BF16)
        oa_v = oa_ref[...]
        wa_v, wb_v, wo_v = wa_ref[...], wb_ref[...], wo_ref[...]
        a2 = _dot(oa_v, wa_v)
        b2 = _dot(ob_v, wb_v)
        sa = _sigmoid(gab_ref[:, 0:D_MODEL].astype(F32))
        sb = _sigmoid(gab_ref[:, D_MODEL:2 * D_MODEL].astype(F32))
        mb = (sa * a2 + sb * b2).astype(BF16)
        mo = _dot(mb, wo_v)
        gate = ada_ref[2:3, :]
        err = (x_ref[...] + gate * mo) - t_ref[...]
        dy = err * (1.0 / D_MODEL)
        dy_ref[...] = dy
        loss_ref[...] += 0.5 * jnp.sum(err * err) * (1.0 / D_MODEL)
        dgate_ref[...] += jnp.sum(dy * mo, axis=0, keepdims=True)
        dmo = (dy * gate).astype(BF16)
        dmerged = _dot_nt(dmo, wo_v)
        dwo_ref[...] += _dot_tn(mb, dmo)
        da2 = (dmerged * sa).astype(BF16)
        db2 = (dmerged * sb).astype(BF16)
        dgab_ref[:, 0:D_MODEL] = (dmerged * a2 * (sa * (1.0 - sa))).astype(BF16)
        dgab_ref[:, D_MODEL:2 * D_MODEL] = (dmerged * b2 * (sb * (1.0 - sb))).astype(BF16)
        doa = _dot_nt(da2, wa_v)
        dob_ref[...] = _dot_nt(db2, wb_v)
        dwa_ref[...] += _dot_tn(oa_v, da2)
        dwb_ref[...] += _dot_tn(ob_v, db2)

        lane = lax.broadcasted_iota(jnp.int32, (tm, LANES), 1)
        lo = lane < HEAD_DIM
        for pr in range(ATTN_W // LANES):
            sl = slice(pr * LANES, (pr + 1) * LANES)
            g, a, z = doa[:, sl], attn_ref[:, sl], za_ref[:, sl].astype(F32)
            sg = _sigmoid(z)
            dat = (g * (z * sg)).astype(BF16).astype(F32)
            prod = dat * a
            dza_ref[:, sl] = (g * a * (sg * (1.0 + z * (1.0 - sg)))).astype(BF16)
            for hh in range(2):
                sel = lo if hh == 0 else jnp.logical_not(lo)
                delta3 = _split3(jnp.sum(jnp.where(sel, prod, 0.0), axis=-1, keepdims=True))
                dh = dat if hh == 0 else pltpu.roll(dat, HEAD_DIM, 1)
                tail_lanes = _place3(lane, L_ONE_Q, tuple(-d for d in delta3), 0.0)
                do_ref[2 * pr + hh] = jnp.where(lo, dh, tail_lanes).astype(BF16)

    half = pl.BlockSpec((tm, ATTN_W), lambda i: (i, 0))
    full = pl.BlockSpec((tm, D_MODEL), lambda i: (i, 0))

    def const(shape):
        return pl.BlockSpec(shape, lambda i: (0, 0))

    def one_axis(spec):
        return pl.BlockSpec(spec.block_shape, lambda i, f=spec.index_map: f(0, i))

    return pl.pallas_call(
        body, name="tail", grid=(s // tm,),
        in_specs=[half, half, pl.BlockSpec((tm, ATTN_W), lambda i: (i, za_blk))]
        + [one_axis(sp) for sp in tiles + befores]
        + [const((3, CONV_W)), pl.BlockSpec((tm, 2 * D_MODEL), lambda i: (i, gab_blk)), full, full,
           const((3, D_MODEL)), const((ATTN_W, D_MODEL)), const((CONV_W, D_MODEL)), const((D_MODEL, D_MODEL))],
        out_specs=[full, pl.BlockSpec((tm, 2 * D_MODEL), lambda i: (i, 0)),
                   pl.BlockSpec((HEADS, tm, LANES), lambda i: (0, i, 0)), half, half,
                   const((D_MODEL, D_MODEL)), const((ATTN_W, D_MODEL)), const((CONV_W, D_MODEL)),
                   const((1, D_MODEL)), const((1, LANES))],
        out_shape=[jax.ShapeDtypeStruct((s, D_MODEL), F32),
                   jax.ShapeDtypeStruct((s, 2 * D_MODEL), BF16),
                   jax.ShapeDtypeStruct((HEADS, s, LANES), BF16),
                   jax.ShapeDtypeStruct((s, ATTN_W), BF16),
                   jax.ShapeDtypeStruct((s, CONV_W), F32),
                   jax.ShapeDtypeStruct((D_MODEL, D_MODEL), F32),
                   jax.ShapeDtypeStruct((ATTN_W, D_MODEL), F32),
                   jax.ShapeDtypeStruct((CONV_W, D_MODEL), F32),
                   jax.ShapeDtypeStruct((1, D_MODEL), F32),
                   jax.ShapeDtypeStruct((1, LANES), F32)],
        compiler_params=_params(("arbitrary",)),
    )(oa, attn, proj, *([proj] * 6), conv_w, proj, x, target, ada3, wa, wb, wo)


def _attn_bwd(qb, ka, kt, va, do, proj, qg, kg):
    s = qb.shape[1]
    t = min(TQ, s)
    nt = s // t
    hp = HEADS_PER_STEP
    wide = hp * HEAD_DIM
    scale = HEAD_DIM ** -0.5
    it, jt = _tri_steps(nt, False)

    def body(it_ref, jt_ref, q_ref, k_ref, kt_ref, v_ref, do_ref, qraw_ref, kraw_ref, qg_ref, kg_ref,
             dq_ref, dk_ref, dv_ref, dqg_ref, dkg_ref, dcum_ref, dqt_s, dk_s, dv_s, rows_s):
        grp, step = pl.program_id(0), pl.program_id(1)
        i, j = it_ref[step], jt_ref[step]
        lane = lax.broadcasted_iota(jnp.int32, (t, LANES), 1)
        lo = lane < HEAD_DIM

        @pl.when(step == 0)
        def _():
            dqt_s[...] = jnp.zeros_like(dqt_s)
            dqg_ref[...] = jnp.zeros_like(dqg_ref)
            dkg_ref[...] = jnp.zeros_like(dkg_ref)

        @pl.when(i == j)
        def _():
            dk_s[...] = jnp.zeros_like(dk_s)
            dv_s[...] = jnp.zeros_like(dv_s)

        def update(masked):
            for hh in range(hp):
                qh, doh = q_ref[hh], do_ref[hh]
                st = _dot_nt(k_ref[hh], qh)
                if masked:
                    st = jnp.where(_causal_t(t), st, NEG)
                pt = jnp.exp(st)
                dst = (pt * _dot_nt(v_ref[hh], doh)).astype(BF16)
                dv_s[hh] += _dot(pt.astype(BF16), doh)
                dk_s[hh] += _dot(dst, qh)
                dqt_s[hh, i] += _dot(kt_ref[hh], dst)

        def pair(a, b):
            return jnp.where(lo, a, pltpu.roll(b, HEAD_DIM, 1))

        def norm_bwd(raw, dy, g, dg_ref, out_ref, sl):
            r = lax.rsqrt(_seg_sum(raw * raw, lo) * (1.0 / HEAD_DIM) + EPS)
            xhat = raw * r
            dg_ref[:, sl] += jnp.sum(dy * xhat, axis=0, keepdims=True)
            dxh = dy * g
            dx = r * (dxh - xhat * (_seg_sum(dxh * xhat, lo) * (1.0 / HEAD_DIM)))
            out_ref[:, sl] = dx.astype(BF16)

        @pl.when(i > j)
        def _():
            update(False)

        @pl.when(i == j)
        def _():
            update(True)
            dq_rows = [dqt_s[hh, i].T for hh in range(hp)]
            rows = jnp.zeros((t, LANES), F32)
            for hh in range(hp):
                rows = jnp.where(lane == grp * hp + hh, _lane_col(dq_rows[hh], L_F_Q), rows)
            rows_s[...] = rows
            for pr in range(hp // 2):
                sl = slice(pr * LANES, (pr + 1) * LANES)
                norm_bwd(qraw_ref[:, sl].astype(F32), pair(dq_rows[2 * pr], dq_rows[2 * pr + 1]) * scale,
                         qg_ref[:, sl], dqg_ref, dq_ref, sl)

        @pl.when(i == nt - 1)
        def _():
            dcum = rows_s[...]
            for hh in range(hp):
                dcum = jnp.where(lane == grp * hp + hh, dcum - _lane_col(dk_s[hh], L_ONE_Q), dcum)
            dcum_ref[0] = dcum
            for pr in range(hp // 2):
                sl = slice(pr * LANES, (pr + 1) * LANES)
                norm_bwd(kraw_ref[:, sl].astype(F32), pair(dk_s[2 * pr], dk_s[2 * pr + 1]),
                         kg_ref[:, sl], dkg_ref, dk_ref, sl)
                dv_ref[:, sl] = pair(dv_s[2 * pr], dv_s[2 * pr + 1]).astype(BF16)

    pair_q = pl.BlockSpec((hp, t, LANES), lambda p, n, it_, jt_: (p, it_[n], 0))
    pair_k = pl.BlockSpec((hp, t, LANES), lambda p, n, it_, jt_: (p, jt_[n], 0))
    pair_kt = pl.BlockSpec((hp, LANES, t), lambda p, n, it_, jt_: (p, 0, jt_[n]))
    tok = pl.BlockSpec((t, wide), lambda p, n, it_, jt_: (jt_[n], p))
    gain = pl.BlockSpec((1, wide), lambda p, n, it_, jt_: (0, p))
    return pl.pallas_call(
        body, name="attn_bwd",
        grid_spec=pltpu.PrefetchScalarGridSpec(
            num_scalar_prefetch=2, grid=(HEADS // hp, it.shape[0]),
            in_specs=[pair_q, pair_k, pair_kt, pair_k, pair_q,
                      pl.BlockSpec((t, wide), lambda p, n, it_, jt_: (jt_[n], OFF_Q // wide + p)),
                      pl.BlockSpec((t, wide), lambda p, n, it_, jt_: (jt_[n], OFF_K // wide + p)), gain, gain],
            out_specs=[tok, tok, tok, gain, gain,
                       pl.BlockSpec((1, t, LANES), lambda p, n, it_, jt_: (p, jt_[n], 0))],
            scratch_shapes=[pltpu.VMEM((hp, nt, LANES, t), F32), pltpu.VMEM((hp, t, LANES), F32),
                            pltpu.VMEM((hp, t, LANES), F32), pltpu.VMEM((t, LANES), F32)]),
        out_shape=[jax.ShapeDtypeStruct((s, ATTN_W), BF16)] * 3
        + [jax.ShapeDtypeStruct((1, ATTN_W), F32)] * 2
        + [jax.ShapeDtypeStruct((HEADS // hp, s, LANES), F32)],
        compiler_params=_params(("parallel", "arbitrary")),
    )(it, jt, qb, ka, kt, va, do, proj, proj, qg, kg)


def _forget_bwd(dcum, fl, bf_pad):
    s = fl.shape[0]
    tc = min(TC_CUM, s)
    n = s // tc

    def body(dc_ref, fl_ref, bf_ref, df_ref, dbf_ref, carry):
        @pl.when(pl.program_id(0) == 0)
        def _():
            carry[...] = jnp.zeros_like(carry)
            dbf_ref[...] = jnp.zeros_like(dbf_ref)
        r = lax.broadcasted_iota(jnp.int32, (tc, tc), 0)
        cidx = lax.broadcasted_iota(jnp.int32, (tc, tc), 1)
        tri = (cidx >= r).astype(F32)
        dc = dc_ref[0]
        for grp in range(1, dcum.shape[0]):
            dc = dc + dc_ref[grp]
        dlf = jnp.dot(tri, dc, preferred_element_type=F32, precision=lax.Precision.HIGHEST) + carry[...]
        carry[...] += jnp.sum(dc, axis=0, keepdims=True)
        lane = lax.broadcasted_iota(jnp.int32, (tc, LANES), 1)
        dfl = jnp.where(lane < HEADS, dlf * _sigmoid(-(fl_ref[...] + bf_ref[...])), 0.0)
        df_ref[...] = dfl.astype(BF16)
        dbf_ref[...] += jnp.sum(dfl, axis=0, keepdims=True)

    rev = pl.BlockSpec((tc, LANES), lambda i: (n - 1 - i, 0))
    vec = pl.BlockSpec((1, LANES), lambda i: (0, 0))
    return pl.pallas_call(
        body, name="forget_bwd", grid=(n,),
        in_specs=[pl.BlockSpec((dcum.shape[0], tc, LANES), lambda i: (0, n - 1 - i, 0)), rev, vec],
        out_specs=[rev, vec],
        out_shape=[jax.ShapeDtypeStruct((s, LANES), BF16), jax.ShapeDtypeStruct((1, LANES), F32)],
        scratch_shapes=[pltpu.VMEM((1, LANES), F32)],
        compiler_params=_params(("arbitrary",)),
    )(dcum, fl, bf_pad)


def _conv_bwd(dob, proj, conv_w):
    s = dob.shape[0]
    tm = min(TM_ELEM, s)
    tiles, befores, afters = _conv_specs(tm, s)

    def body(dob_ref, dnext_ref, gb_ref, gc_ref, u_ref, zb_ref, gch_ref, uh_ref, gbn_ref, zbn_ref, w_ref,
             dgb_ref, dgc_ref, du_ref, dzb_ref, dw_ref):
        i = pl.program_id(1)

        @pl.when(i == 0)
        def _():
            dw_ref[...] = jnp.zeros_like(dw_ref)
        gb, gc, u, zb, cu, r1, r2, conv = _conv_parts(gb_ref, gc_ref, u_ref, zb_ref, gch_ref, uh_ref, i == 0, w_ref, tm)
        g = dob_ref[...]
        sg = _sigmoid(zb)
        sz = zb * sg
        dconv = g * gb * sz
        zn = zbn_ref[0:8, :].astype(F32)
        dcn = jnp.where(i == pl.num_programs(1) - 1, 0.0,
                        dnext_ref[...] * gbn_ref[0:8, :].astype(F32) * (zn * _sigmoid(zn)))
        nxt1, nxt2 = _sub_row(dcn, 0), _sub_row(dcn, 1)
        row = lax.broadcasted_iota(jnp.int32, (tm, LANES), 0)
        f1 = jnp.where(row == tm - 1, nxt1, pltpu.roll(dconv, tm - 1, 0))
        f2 = jnp.where(row == tm - 2, nxt1, jnp.where(row == tm - 1, nxt2, pltpu.roll(dconv, tm - 2, 0)))
        dcu = w_ref[2:3, :] * dconv + w_ref[1:2, :] * f1 + w_ref[0:1, :] * f2
        dgb_ref[...] = (g * conv * sz).astype(BF16)
        dgc_ref[...] = (dcu * u).astype(BF16)
        du_ref[...] = (dcu * gc).astype(BF16)
        dzb_ref[...] = (g * gb * conv * (sg * (1.0 + zb * (1.0 - sg)))).astype(BF16)
        w_row = lax.broadcasted_iota(jnp.int32, (3, LANES), 0)
        dw0 = jnp.sum(dconv * r2, axis=0, keepdims=True)
        dw1 = jnp.sum(dconv * r1, axis=0, keepdims=True)
        dw2 = jnp.sum(dconv * cu, axis=0, keepdims=True)
        dw_ref[...] += jnp.where(w_row == 0, dw0, jnp.where(w_row == 1, dw1, dw2))

    blk = pl.BlockSpec((tm, LANES), lambda c, i: (i, c))
    nxt = pl.BlockSpec((8, LANES), lambda c, i: (jnp.minimum((i + 1) * (tm // 8), s // 8 - 1), c))
    wspec = pl.BlockSpec((3, LANES), lambda c, i: (0, c))
    return pl.pallas_call(
        body, name="conv_bwd", grid=(CONV_W // LANES, s // tm),
        in_specs=[blk, nxt] + tiles + befores + afters + [wspec],
        out_specs=[blk, blk, blk, blk, wspec],
        out_shape=[jax.ShapeDtypeStruct((s, CONV_W), BF16)] * 4 + [jax.ShapeDtypeStruct((3, CONV_W), F32)],
        compiler_params=_params(("parallel", "arbitrary")),
    )(dob, dob, *([proj] * 8), conv_w)


def _piece_layout(pieces):
    offs, off = [], 0
    for p in pieces:
        offs.append((off, p.shape[1]))
        off += p.shape[1]
    assert off == N_ALL, off
    return offs


def _dw_in(h, pieces, chip_sums):
    s = h.shape[0]
    tk, tn = min(TK_DW, s), TN_DW
    nk = s // tk
    nn = N_MAIN // tn
    main, fpiece = pieces[:-1], pieces[-1]
    layout = _piece_layout(pieces)[:-1]
    n_main = len(main)
    nx = len(chip_sums)

    def body(*refs):
        p_refs, f_ref, h_ref = refs[:n_main], refs[n_main], refs[n_main + 1]
        ins, refs = refs[n_main + 2:n_main + 2 + nx], refs[n_main + 2 + nx:]
        out_ref, outf_ref = refs[:2]
        outs, (acc, accf, send_sems, recv_sems, local_sems) = refs[2:2 + nx], refs[2 + nx:]
        n, k = pl.program_id(0), pl.program_id(1)

        @pl.when(jnp.logical_and(n == 0, k == 0))
        def _():
            for cp in _chip_copies(ins, outs, send_sems, recv_sems, local_sems):
                cp.start()

        @pl.when(k == 0)
        def _():
            acc[...] = jnp.zeros_like(acc)
        hv = h_ref[pl.ds(pl.multiple_of(k * tk, tk), tk), :]
        for p_ref, (off, width) in zip(p_refs, layout):
            @pl.when(jnp.logical_and(n >= off // tn, n < (off + width) // tn))
            def _():
                acc[...] += _dot_tn(p_ref[...], hv)

        @pl.when(k == nk - 1)
        def _():
            out_ref[...] = acc[...].astype(BF16)

        @pl.when(n == 0)
        def _():
            @pl.when(k == 0)
            def _():
                accf[...] = jnp.zeros_like(accf)
            accf[...] += _dot_tn(f_ref[...], hv)

            @pl.when(k == nk - 1)
            def _():
                outf_ref[...] = accf[...].astype(BF16)

        @pl.when(jnp.logical_and(n == nn - 1, k == nk - 1))
        def _():
            for cp in _chip_copies(ins, outs, send_sems, recv_sems, local_sems):
                cp.wait()

    def piece_spec(off, width):
        lo, hi = off // tn, (off + width) // tn

        def index(n, k):
            active = jnp.logical_and(n >= lo, n < hi)
            return jnp.where(active, k, 0), jnp.clip(n - lo, 0, hi - lo - 1)
        return pl.BlockSpec((tk, tn), index)

    any_spec = pl.BlockSpec(memory_space=pl.ANY)
    res = pl.pallas_call(
        body, name="dw_in", grid=(nn, nk),
        in_specs=[piece_spec(off, width) for off, width in layout]
        + [pl.BlockSpec((tk, N_FPAD), lambda n, k: (jnp.where(n == 0, k, 0), 0)),
           pl.BlockSpec((s, D_MODEL), lambda n, k: (0, 0))] + [any_spec] * nx,
        out_specs=[pl.BlockSpec((tn, D_MODEL), lambda n, k: (n, 0)),
                   pl.BlockSpec((N_FPAD, D_MODEL), lambda n, k: (0, 0))] + [any_spec] * nx,
        out_shape=[jax.ShapeDtypeStruct((N_MAIN, D_MODEL), BF16), jax.ShapeDtypeStruct((N_FPAD, D_MODEL), BF16)]
        + [jax.ShapeDtypeStruct(a.shape, a.dtype) for a in chip_sums],
        scratch_shapes=[pltpu.VMEM((tn, D_MODEL), F32), pltpu.VMEM((N_FPAD, D_MODEL), F32),
                        pltpu.SemaphoreType.DMA((nx * 3,)), pltpu.SemaphoreType.DMA((nx * 3,)),
                        pltpu.SemaphoreType.DMA((nx,))],
        compiler_params=_params(("arbitrary", "arbitrary")),
    )(*main, fpiece, h, *chip_sums)
    return res[:2], res[2:]


def _dh_and_dx(pieces, w_all_t, x, dy, ada3, norm_g, chip_sums):
    s = x.shape[0]
    tm = min(TM_DH, s)
    nt = s // tm
    n = len(chip_sums)
    npc = len(pieces)
    layout = _piece_layout(pieces)

    def body(*refs):
        p_refs, refs = refs[:npc], refs[npc:]
        wt_ref, x_ref, dy_ref, ada_ref, g_ref = refs[:5]
        ins, refs = refs[5:5 + n], refs[5 + n:]
        gx_ref, dsh_ref, dsc_ref, dg_ref = refs[:4]
        outs, (send_sems, recv_sems, local_sems) = refs[4:4 + n], refs[4 + n:]
        i = pl.program_id(0)

        @pl.when(i == 0)
        def _():
            for cp in _chip_copies(ins, outs, send_sems, recv_sems, local_sems):
                cp.start()
            dsh_ref[...] = jnp.zeros_like(dsh_ref)
            dsc_ref[...] = jnp.zeros_like(dsc_ref)
            dg_ref[...] = jnp.zeros_like(dg_ref)

        dh = None
        for p_ref, (off, width) in zip(p_refs, layout):
            part = _dot(p_ref[...], wt_ref[off:off + width, :])
            dh = part if dh is None else dh + part
        xv = x_ref[...]
        r = lax.rsqrt(jnp.mean(xv * xv, axis=-1, keepdims=True) + EPS)
        xhat = xv * r
        g = g_ref[...]
        one_sc = 1.0 + ada_ref[1:2, :]
        dsh_ref[...] += jnp.sum(dh, axis=0, keepdims=True)
        dsc_ref[...] += jnp.sum(dh * (xhat * g), axis=0, keepdims=True)
        dg_ref[...] += jnp.sum(dh * xhat, axis=0, keepdims=True) * one_sc
        dxh = dh * (g * one_sc)
        dx = r * (dxh - xhat * jnp.mean(dxh * xhat, axis=-1, keepdims=True))
        gx_ref[...] = dy_ref[...] + dx

        @pl.when(i == nt - 1)
        def _():
            for cp in _chip_copies(ins, outs, send_sems, recv_sems, local_sems):
                cp.wait()

    full = pl.BlockSpec((tm, D_MODEL), lambda i: (i, 0))
    vec = pl.BlockSpec((1, D_MODEL), lambda i: (0, 0))
    any_spec = pl.BlockSpec(memory_space=pl.ANY)
    res = pl.pallas_call(
        body, name="dh_dx", grid=(nt,),
        in_specs=[pl.BlockSpec((tm, p.shape[1]), lambda i: (i, 0)) for p in pieces]
        + [pl.BlockSpec((N_ALL, D_MODEL), lambda i: (0, 0)), full, full,
           pl.BlockSpec((3, D_MODEL), lambda i: (0, 0)), vec] + [any_spec] * n,
        out_specs=[full, vec, vec, vec] + [any_spec] * n,
        out_shape=[jax.ShapeDtypeStruct((s, D_MODEL), F32)] + [jax.ShapeDtypeStruct((1, D_MODEL), F32)] * 3
        + [jax.ShapeDtypeStruct(a.shape, a.dtype) for a in chip_sums],
        scratch_shapes=[pltpu.SemaphoreType.DMA((n * 3,)), pltpu.SemaphoreType.DMA((n * 3,)),
                        pltpu.SemaphoreType.DMA((n,))],
        compiler_params=_params(("arbitrary",)),
    )(*pieces, w_all_t, x, dy, ada3, norm_g, *chip_sums)
    return res[:4], res[4:]


def _sum_small(vec_all, qg_parts, kg_parts):
    def body(v_ref, q_ref, k_ref, tot_ref, gq_ref, gk_ref):
        tot = v_ref[0:1, :]
        for p in range(1, N_DEV):
            tot = tot + v_ref[p:p + 1, :]
        tot_ref[...] = tot
        gq_ref[...] = jnp.sum(q_ref[...], axis=0, keepdims=True)
        gk_ref[...] = jnp.sum(k_ref[...], axis=0, keepdims=True)

    n = vec_all.shape[-1]
    return pl.pallas_call(
        body, name="sum_small",
        out_shape=[jax.ShapeDtypeStruct((1, n), F32),
                   jax.ShapeDtypeStruct((1, HEAD_DIM), F32), jax.ShapeDtypeStruct((1, HEAD_DIM), F32)],
        compiler_params=_params(),
    )(vec_all, qg_parts, kg_parts)


def _grad_w_ada(c_cols, dada_rows):
    def body(c_ref, d_ref, out_ref):
        acc = c_ref[0] * d_ref[0]
        for b in range(1, N_DEV):
            acc = acc + c_ref[b] * d_ref[b]
        out_ref[...] = acc

    return pl.pallas_call(
        body, name="grad_w_ada",
        out_shape=jax.ShapeDtypeStruct((D_MODEL, ADA_SHARD), F32),
        compiler_params=_params(),
    )(c_cols, dada_rows)


def _adamw(w, m, v, g_parts, name):
    rows, cols = w.shape
    n_parts = g_parts.shape[0]
    tr = 256 if rows % 256 == 0 else rows
    tc = 256 if (tr == rows and rows > 256 and cols % 256 == 0) else cols
    c1 = 1.0 / (1.0 - ADAM_B1 ** ADAM_STEP)
    c2 = 1.0 / (1.0 - ADAM_B2 ** ADAM_STEP)

    def body(w_ref, m_ref, v_ref, g_ref, go_ref, d_ref, mo_ref, vo_ref):
        g = g_ref[0].astype(F32)
        for p in range(1, n_parts):
            g = g + g_ref[p].astype(F32)
        m_new = ADAM_B1 * m_ref[...] + (1.0 - ADAM_B1) * g
        v_new = ADAM_B2 * v_ref[...] + (1.0 - ADAM_B2) * (g * g)
        go_ref[...] = g
        mo_ref[...] = m_new
        vo_ref[...] = v_new
        d_ref[...] = -ADAM_LR * ((m_new * c1) / (jnp.sqrt(v_new * c2) + ADAM_EPS) + ADAM_WD * w_ref[...])

    blk = pl.BlockSpec((tr, tc), lambda i, j: (i, j))
    return pl.pallas_call(
        body, name=name, grid=(rows // tr, cols // tc),
        in_specs=[blk, blk, blk, pl.BlockSpec((n_parts, tr, tc), lambda i, j: (0, i, j))],
        out_specs=[blk] * 4,
        out_shape=[jax.ShapeDtypeStruct((rows, cols), F32)] * 4,
        compiler_params=_params(("parallel", "parallel")),
    )(w, m, v, g_parts)


_O_F = 1536


def _to_internal(wt_g):
    wf = wt_g.reshape(IN_WIDTH, D_MODEL)
    f = jnp.pad(wf[_O_F:_O_F + HEADS], ((0, N_FPAD - HEADS), (0, 0)))
    return jnp.concatenate([wf[:_O_F], wf[_O_F + HEADS:], f], axis=0)


def _slabs_by_core(dwt, dwt_f):
    sources = ((dwt, 0, _O_F, 0), (dwt_f, _O_F, _O_F + HEADS, _O_F), (dwt, _O_F + HEADS, IN_WIDTH, HEADS))

    def slab(p):
        lo, hi = p * IN_SHARD, (p + 1) * IN_SHARD
        parts = []
        for src, o_lo, o_hi, shift in sources:
            a, b = max(lo, o_lo), min(hi, o_hi)
            if a < b:
                parts.append(src[a - shift:b - shift])
        return parts[0] if len(parts) == 1 else jnp.concatenate(parts, axis=0)

    return jnp.stack([jnp.stack([slab(2 * chip + core) for chip in range(4)]) for core in range(2)])


def kernel(x, c, w_ada, b_ada, norm_g, w_in, b_f, q_norm_g, k_norm_g, conv_w, w_attn_out, w_conv_out, w_o, loss_target, m_w_ada, m_b_ada, m_norm_g, m_w_in, m_b_f, m_q_norm_g, m_k_norm_g, m_conv_w, m_w_attn_out, m_w_conv_out, m_w_o, v_w_ada, v_b_ada, v_norm_g, v_w_in, v_b_f, v_q_norm_g, v_k_norm_g, v_conv_w, v_w_attn_out, v_w_conv_out, v_w_o):
    me = 4 * lax.axis_index("x") + 2 * lax.axis_index("y") + lax.axis_index("c")
    s = x.shape[1]
    x2, t2 = x[0], loss_target[0]

    w_in_g, c_all, ada_g = _gather_weights_and_ada(w_in[0].T.astype(BF16), c, w_ada[0])
    ada_mine = lax.dynamic_index_in_dim(ada_g[:, :, 0, :], me, axis=1, keepdims=False)
    ada3 = (ada_mine.reshape(1, 3 * D_MODEL) + b_ada).reshape(3, D_MODEL)
    w_all_t = _to_internal(w_in_g)
    qg = jnp.tile(q_norm_g, (1, HEADS))
    kg = jnp.tile(k_norm_g, (1, HEADS))
    bf_pad = jnp.pad(b_f, ((0, 0), (0, LANES - HEADS)))

    (proj, fl, h), (cw_g, wa_g, wb_g, wo_g) = _proj_fwd(
        x2, ada3, norm_g, w_all_t,
        [conv_w[0], w_attn_out[0].astype(BF16), w_conv_out[0].astype(BF16), w_o[0].astype(BF16)])
    wa = jnp.transpose(wa_g, (1, 0, 2)).reshape(ATTN_W, D_MODEL)
    wb = jnp.transpose(wb_g, (1, 0, 2)).reshape(CONV_W, D_MODEL)
    wo = wo_g.reshape(D_MODEL, D_MODEL)
    cw = jnp.transpose(cw_g, (1, 0, 2)).reshape(3, CONV_W)
    qa, ka, va, kt, vt = _qkv_prep(proj, fl, bf_pad, qg, kg)
    attn, oa, qb = _attn_fwd(qa, ka, vt, proj)
    (dy, dgab, do, dza, dob, dwo, dwa, dwb, dgate, loss_part) = _tail(oa, attn, proj, x2, t2, ada3, wa, wb, wo, cw)

    def by_core(slabs8):
        return jnp.swapaxes(slabs8.reshape((4, 2) + slabs8.shape[1:]), 0, 1).astype(BF16)

    core = lax.axis_index("c").astype(jnp.int32).reshape(1)
    small = [by_core(jnp.transpose(dwa.reshape(ATTN_W, N_DEV, LANES), (1, 0, 2))),
             by_core(jnp.transpose(dwb.reshape(CONV_W, N_DEV, LANES), (1, 0, 2))),
             by_core(dwo.reshape(N_DEV, D_MODEL // N_DEV, D_MODEL))]
    small_sums = [_pair_sum(m2, t4, core, "pair_sum_" + nm)
                  for m2, t4, nm in zip(small, _sibling_swap(small, "swap_small"), ("wa", "wb", "wo"))]
    dq, dk, dv, dqg, dkg, dcum = _attn_bwd(qb, ka, kt, va, do, proj, qg, kg)
    df, dbf = _forget_bwd(dcum, fl, bf_pad)
    dcb, dcc, dcu, dcz, dcw = _conv_bwd(dob, proj, cw)
    pieces = [dq, dk, dv, dza, dcb, dcc, dcu, dcz, dgab, df]
    (dw_main, dw_f), (g_wa_parts, g_wb_parts, g_wo_parts) = _dw_in(h, pieces, small_sums)

    slabs_in = _slabs_by_core(dw_main, dw_f)
    (theirs_in,) = _sibling_swap([slabs_in], "swap_w_in")
    (grad_x, dshift, dscale, dnormg), (g_in_parts,) = _dh_and_dx(
        pieces, w_all_t, x2, dy, ada3, norm_g, [_pair_sum(slabs_in, theirs_in, core, "pair_sum_w_in")])
    vec = jnp.concatenate([dshift, dscale, dgate, dnormg, dbf, dcw.reshape(1, 3 * CONV_W), loss_part, dqg, dkg],
                          axis=1)
    (vec_all,) = _gather_direct([vec], "gather_small")
    vec_all = vec_all.reshape(N_DEV, vec.shape[1])
    n_main = 4 * D_MODEL + LANES + 3 * CONV_W + LANES
    tot, g_qg, g_kg = _sum_small(
        vec_all[:, :n_main],
        vec_all[:, n_main:n_main + ATTN_W].reshape(N_DEV * HEADS, HEAD_DIM),
        vec_all[:, n_main + ATTN_W:].reshape(N_DEV * HEADS, HEAD_DIM))
    g_b_ada = tot[:, 0:3 * D_MODEL]
    g_norm_g = tot[:, 3 * D_MODEL:4 * D_MODEL]
    g_b_f = tot[:, 4 * D_MODEL:4 * D_MODEL + HEADS]
    g_cw_full = tot[:, 4 * D_MODEL + LANES:4 * D_MODEL + LANES + 3 * CONV_W].reshape(3, CONV_W)
    g_cw = lax.dynamic_slice(g_cw_full, (0, me * (CONV_W // N_DEV)), (3, CONV_W // N_DEV))
    dada_mine = lax.dynamic_slice(vec_all[:, 0:3 * D_MODEL], (0, me * ADA_SHARD), (N_DEV, ADA_SHARD))
    g_w_ada = _grad_w_ada(jnp.transpose(c_all, (0, 2, 1)), dada_mine.reshape(N_DEV, 1, ADA_SHARD))

    upd = {}
    upd["w_ada"] = _adamw(w_ada[0], m_w_ada[0], v_w_ada[0], g_w_ada[None], "adamw_w_ada")
    upd["b_ada"] = _adamw(b_ada, m_b_ada, v_b_ada, g_b_ada[None], "adamw_b_ada")
    upd["norm_g"] = _adamw(norm_g, m_norm_g, v_norm_g, g_norm_g[None], "adamw_norm_g")
    upd["w_in"] = [u.T for u in _adamw(w_in[0].T, m_w_in[0].T, v_w_in[0].T, g_in_parts, "adamw_w_in")]
    upd["b_f"] = _adamw(b_f, m_b_f, v_b_f, g_b_f[None], "adamw_b_f")
    upd["q_norm_g"] = _adamw(q_norm_g, m_q_norm_g, v_q_norm_g, g_qg[None], "adamw_q_norm_g")
    upd["k_norm_g"] = _adamw(k_norm_g, m_k_norm_g, v_k_norm_g, g_kg[None], "adamw_k_norm_g")
    upd["conv_w"] = _adamw(conv_w[0], m_conv_w[0], v_conv_w[0], g_cw[None], "adamw_conv_w")
    upd["w_attn_out"] = _adamw(w_attn_out[0], m_w_attn_out[0], v_w_attn_out[0], g_wa_parts, "adamw_w_attn_out")
    upd["w_conv_out"] = _adamw(w_conv_out[0], m_w_conv_out[0], v_w_conv_out[0], g_wb_parts, "adamw_w_conv_out")
    upd["w_o"] = _adamw(w_o[0], m_w_o[0], v_w_o[0], g_wo_parts, "adamw_w_o")

    names = ["w_ada", "b_ada", "norm_g", "w_in", "b_f", "q_norm_g", "k_norm_g", "conv_w",
             "w_attn_out", "w_conv_out", "w_o"]
    lead = {"w_ada", "w_in", "conv_w", "w_attn_out", "w_conv_out", "w_o"}
    fix = lambda n, a: a[None] if n in lead else a
    loss = tot[0, n_main - LANES]
    outs = [loss, grad_x[None]]
    for k in range(4):
        outs += [fix(n, upd[n][k]) for n in names]
    return tuple(outs)
```

```python
import functools

import numpy as np
import jax
import jax.numpy as jnp
from jax import lax
from jax.experimental import pallas as pl
from jax.experimental.pallas import tpu as pltpu

F32 = jnp.float32
BF16 = jnp.bfloat16

D_MODEL = 1024
HEADS = 8
HEAD_DIM = 64
ATTN_W = 512
CONV_W = 512
N_DEV = 8
IN_WIDTH = 6152
IN_SHARD = IN_WIDTH // N_DEV
N_MAIN = 6144
N_FPAD = 128
N_ALL = N_MAIN + N_FPAD
ADA_SHARD = 3 * D_MODEL // N_DEV
EPS = 1e-6
NEG = -1e30
LOG2E = 1.4426950408889634

ADAM_LR = 0.001
ADAM_B1 = 0.9
ADAM_B2 = 0.999
ADAM_EPS = 1e-08
ADAM_WD = 0.01
ADAM_STEP = 10

LANES = 128
VMEM_LIMIT = 56 * 1024 * 1024

TM_PROJ = 256
TN_PROJ = 1024
TM_ELEM = 512
TQ = 512
HEADS_PER_STEP = 8
HEADS_PER_STEP_FWD = 8
TM_TAIL = 256
TC_CUM = 256
TK_DW = 2048
TN_DW = 512
TM_DH = 256
HALO = 16

OFF_Q, OFF_K, OFF_V, OFF_ZA, OFF_CB, OFF_CC, OFF_CU, OFF_CZ, OFF_GA, OFF_GB = (
    0, 512, 1024, 1536, 2048, 2560, 3072, 3584, 4096, 5120)


def _params(sem=None):
    return pltpu.CompilerParams(dimension_semantics=sem, vmem_limit_bytes=VMEM_LIMIT)


def _dot(a, b):
    return jnp.dot(a, b, preferred_element_type=F32)


def _dot_nt(a, b):
    return lax.dot_general(a, b, (((1,), (1,)), ((), ())), preferred_element_type=F32)


def _dot_tn(a, b):
    return lax.dot_general(a, b, (((0,), (0,)), ((), ())), preferred_element_type=F32)


def _sigmoid(x):
    return 1.0 / (1.0 + jnp.exp(-x))


def _lane_lo(shape):
    return lax.broadcasted_iota(jnp.int32, shape, len(shape) - 1) < HEAD_DIM


def _seg_sum(z, lo):
    a = jnp.sum(jnp.where(lo, z, 0.0), axis=-1, keepdims=True)
    b = jnp.sum(jnp.where(lo, 0.0, z), axis=-1, keepdims=True)
    return jnp.where(lo, a, b)


def _lane_col(z, lane):
    idx = lax.broadcasted_iota(jnp.int32, z.shape, 1)
    return jnp.sum(jnp.where(idx == lane, z, 0.0), axis=-1, keepdims=True)


def _sub_row(z, row):
    idx = lax.broadcasted_iota(jnp.int32, z.shape, 0)
    return jnp.sum(jnp.where(idx == row, z, 0.0), axis=0, keepdims=True)


def _mesh_pos():
    x, y, c = lax.axis_index("x"), lax.axis_index("y"), lax.axis_index("c")
    return x, y, c, 4 * x + 2 * y + c


def _peer(k, x, y, c):
    px = 1 - x if (k >> 2) & 1 else x
    py = 1 - y if (k >> 1) & 1 else y
    pc = 1 - c if k & 1 else c
    return (px, py, pc), 4 * px + 2 * py + pc


def _gather_copies(ins, outs, send_sems, recv_sems, local_sems):
    x, y, c, me = _mesh_pos()
    copies = []
    for a in range(len(ins)):
        copies.append(pltpu.make_async_copy(ins[a], outs[a].at[me], local_sems.at[a]))
        for k in range(1, N_DEV):
            dev, _ = _peer(k, x, y, c)
            copies.append(pltpu.make_async_remote_copy(
                src_ref=ins[a], dst_ref=outs[a].at[me],
                send_sem=send_sems.at[a * (N_DEV - 1) + k - 1], recv_sem=recv_sems.at[a * (N_DEV - 1) + k - 1],
                device_id=dev, device_id_type=pl.DeviceIdType.MESH))
    return copies


def _gather_sems(n):
    return [pltpu.SemaphoreType.DMA((n * (N_DEV - 1),)), pltpu.SemaphoreType.DMA((n * (N_DEV - 1),)),
            pltpu.SemaphoreType.DMA((n,))]


def _gather_direct(arrs, name):
    n = len(arrs)
    any_spec = pl.BlockSpec(memory_space=pl.ANY)

    def body(*refs):
        copies = _gather_copies(refs[:n], refs[n:2 * n], *refs[2 * n:])
        for cp in copies:
            cp.start()
        for cp in copies:
            cp.wait()

    return pl.pallas_call(
        body, name=name, out_shape=[jax.ShapeDtypeStruct((N_DEV,) + a.shape, a.dtype) for a in arrs],
        in_specs=[any_spec] * n, out_specs=[any_spec] * n, scratch_shapes=_gather_sems(n),
    )(*arrs)


def _ada_phase(c_ref, w_ref, call_ref, adag_ref, mine_ref, send_sems, recv_sems):
    x, y, c, me = _mesh_pos()

    def copy(phase, k, src, dst):
        dev, _ = _peer(k, x, y, c)
        return pltpu.make_async_remote_copy(
            src_ref=src, dst_ref=dst,
            send_sem=send_sems.at[phase * (N_DEV - 1) + k - 1],
            recv_sem=recv_sems.at[phase * (N_DEV - 1) + k - 1],
            device_id=dev, device_id_type=pl.DeviceIdType.MESH)

    call_ref[me] = c_ref[...]
    first = [copy(0, k, c_ref, call_ref.at[me]) for k in range(1, N_DEV)]
    for cp in first:
        cp.start()
    for cp in first:
        cp.wait()
    wb = w_ref[...].astype(BF16)
    for b in range(N_DEV):
        row = jnp.broadcast_to(call_ref[b], (8, D_MODEL)).astype(BF16)
        mine_ref[b] = _sub_row(_dot(row, wb), 0)
    adag_ref[me] = mine_ref[...]
    second = [copy(1, k, mine_ref, adag_ref.at[me]) for k in range(1, N_DEV)]
    for cp in second:
        cp.start()
    for cp in second:
        cp.wait()


def _gather_weights_and_ada(wt_shard, c_row, w_ada_sh):
    any_spec = pl.BlockSpec(memory_space=pl.ANY)
    vm = pl.BlockSpec(memory_space=pltpu.VMEM)

    def body(w_in_ref, c_ref, wada_ref, out_ref, call_ref, adag_ref, mine_ref, send_sems, recv_sems, local_sem,
             ada_send, ada_recv):
        x, y, c, me = _mesh_pos()
        sibling = (x, y, 1 - c)
        chips = [(1 - x, y), (x, 1 - y), (1 - x, 1 - y)]

        def copy(k, src, blk, to):
            return pltpu.make_async_remote_copy(
                src_ref=src, dst_ref=out_ref.at[blk], send_sem=send_sems.at[k], recv_sem=recv_sems.at[k],
                device_id=to, device_id_type=pl.DeviceIdType.MESH)

        local = pltpu.make_async_copy(w_in_ref, out_ref.at[me], local_sem.at[0])
        local.start()
        first = [copy(0, w_in_ref, me, sibling)]
        first += [copy(1 + j, w_in_ref, me, (px, py, c)) for j, (px, py) in enumerate(chips)]
        for cp in first:
            cp.start()
        _ada_phase(c_ref, wada_ref, call_ref, adag_ref, mine_ref, ada_send, ada_recv)
        passed = []
        for j, (px, py) in enumerate(chips):
            blk = 4 * px + 2 * py + c
            copy(1 + j, w_in_ref, blk, (x, y, c)).wait_recv()
            fwd = copy(4 + j, out_ref.at[blk], blk, sibling)
            fwd.start()
            passed.append(fwd)
        copy(0, w_in_ref, 4 * x + 2 * y + 1 - c, (x, y, c)).wait_recv()
        for j, (px, py) in enumerate(chips):
            copy(4 + j, w_in_ref, 4 * px + 2 * py + 1 - c, (x, y, c)).wait_recv()
        for cp in first + passed:
            cp.wait_send()
        local.wait()

    per = N_DEV - 1
    return pl.pallas_call(
        body, name="gather_weights",
        out_shape=[jax.ShapeDtypeStruct((N_DEV,) + wt_shard.shape, wt_shard.dtype),
                   jax.ShapeDtypeStruct((N_DEV, 1, D_MODEL), F32),
                   jax.ShapeDtypeStruct((N_DEV, N_DEV, 1, ADA_SHARD), F32)],
        in_specs=[any_spec, vm, vm], out_specs=[any_spec, vm, vm],
        scratch_shapes=[pltpu.VMEM((N_DEV, 1, ADA_SHARD), F32),
                        pltpu.SemaphoreType.DMA((per,)), pltpu.SemaphoreType.DMA((per,)),
                        pltpu.SemaphoreType.DMA((1,)),
                        pltpu.SemaphoreType.DMA((2 * per,)), pltpu.SemaphoreType.DMA((2 * per,))],
        compiler_params=pltpu.CompilerParams(vmem_limit_bytes=VMEM_LIMIT),
    )(wt_shard, c_row, w_ada_sh)


def _sibling_swap(arrs, name):
    n = len(arrs)
    any_spec = pl.BlockSpec(memory_space=pl.ANY)

    def body(*refs):
        ins, outs = refs[:n], refs[n:2 * n]
        send_sems, recv_sems = refs[2 * n:]
        x, y, c, _ = _mesh_pos()
        copies = [pltpu.make_async_remote_copy(
            src_ref=ins[a].at[1 - c], dst_ref=outs[a], send_sem=send_sems.at[a], recv_sem=recv_sems.at[a],
            device_id=(x, y, 1 - c), device_id_type=pl.DeviceIdType.MESH) for a in range(n)]
        for cp in copies:
            cp.start()
        for cp in copies:
            cp.wait()

    return pl.pallas_call(
        body, name=name,
        out_shape=[jax.ShapeDtypeStruct(a.shape[1:], a.dtype) for a in arrs],
        in_specs=[any_spec] * n, out_specs=[any_spec] * n,
        scratch_shapes=[pltpu.SemaphoreType.DMA((n,)), pltpu.SemaphoreType.DMA((n,))],
    )(*arrs)


def _pair_sum(mine2, theirs, core, name):
    _, _, rows, cols = mine2.shape
    tr = 256 if rows % 256 == 0 else rows

    def body(core_ref, a_ref, b_ref, out_ref):
        out_ref[...] = (a_ref[...].astype(F32) + b_ref[...].astype(F32)).astype(BF16)

    return pl.pallas_call(
        body, name=name,
        grid_spec=pltpu.PrefetchScalarGridSpec(
            num_scalar_prefetch=1, grid=(4, rows // tr),
            in_specs=[pl.BlockSpec((None, None, tr, cols), lambda ch, i, core_: (core_[0], ch, i, 0)),
                      pl.BlockSpec((None, tr, cols), lambda ch, i, core_: (ch, i, 0))],
            out_specs=pl.BlockSpec((None, tr, cols), lambda ch, i, core_: (ch, i, 0))),
        out_shape=jax.ShapeDtypeStruct(theirs.shape, BF16),
        compiler_params=_params(("parallel", "parallel")),
    )(core, mine2, theirs)


def _chip_copies(ins, outs, send_sems, recv_sems, local_sems):
    x, y, c, _ = _mesh_pos()
    my_chip = 2 * x + y
    chips = [(1 - x, y), (x, 1 - y), (1 - x, 1 - y)]
    copies = []
    for a in range(len(ins)):
        copies.append(pltpu.make_async_copy(ins[a].at[my_chip], outs[a].at[my_chip], local_sems.at[a]))
        for j, (px, py) in enumerate(chips):
            copies.append(pltpu.make_async_remote_copy(
                src_ref=ins[a].at[2 * px + py], dst_ref=outs[a].at[my_chip],
                send_sem=send_sems.at[a * 3 + j], recv_sem=recv_sems.at[a * 3 + j],
                device_id=(px, py, c), device_id_type=pl.DeviceIdType.MESH))
    return copies


def _proj_fwd(x, ada3, norm_g, w_all_t, later):
    s = x.shape[0]
    tm, tn = min(TM_PROJ, s), TN_PROJ
    nt = s // tm
    n = len(later)

    def body(x_ref, ada_ref, g_ref, wt_ref, *rest):
        ins, (proj_ref, fl_ref, h_ref), rest = rest[:n], rest[n:n + 3], rest[n + 3:]
        outs, sems = rest[:n], rest[n:]
        i = pl.program_id(0)

        @pl.when(i == 0)
        def _():
            for cp in _gather_copies(ins, outs, *sems):
                cp.start()

        xv = x_ref[...]
        r = lax.rsqrt(jnp.mean(xv * xv, axis=-1, keepdims=True) + EPS)
        hv = ((xv * r) * g_ref[...]) * (1.0 + ada_ref[1:2, :]) + ada_ref[0:1, :]
        hb = hv.astype(BF16)
        h_ref[...] = hb
        fl_ref[...] = _dot_nt(hb, wt_ref[N_MAIN:N_ALL, :])
        for j in range(N_MAIN // tn):
            proj_ref[:, j * tn:(j + 1) * tn] = _dot_nt(hb, wt_ref[j * tn:(j + 1) * tn, :]).astype(BF16)

        @pl.when(i == nt - 1)
        def _():
            for cp in _gather_copies(ins, outs, *sems):
                cp.wait()

    any_spec = pl.BlockSpec(memory_space=pl.ANY)
    res = pl.pallas_call(
        body, name="proj_fwd", grid=(nt,),
        in_specs=[pl.BlockSpec((tm, D_MODEL), lambda i: (i, 0)),
                  pl.BlockSpec((3, D_MODEL), lambda i: (0, 0)),
                  pl.BlockSpec((1, D_MODEL), lambda i: (0, 0)),
                  pl.BlockSpec((N_ALL, D_MODEL), lambda i: (0, 0))] + [any_spec] * n,
        out_specs=[pl.BlockSpec((tm, N_MAIN), lambda i: (i, 0)),
                   pl.BlockSpec((tm, N_FPAD), lambda i: (i, 0)),
                   pl.BlockSpec((tm, D_MODEL), lambda i: (i, 0))] + [any_spec] * n,
        out_shape=[jax.ShapeDtypeStruct((s, N_MAIN), BF16),
                   jax.ShapeDtypeStruct((s, N_FPAD), F32),
                   jax.ShapeDtypeStruct((s, D_MODEL), BF16)]
        + [jax.ShapeDtypeStruct((N_DEV,) + a.shape, a.dtype) for a in later],
        scratch_shapes=_gather_sems(n),
        compiler_params=_params(("arbitrary",)),
    )(x, ada3, norm_g, w_all_t, *later)
    return res[:3], res[3:]


L_ONE_Q, L_F_Q, L_LSE_Q, L_END = HEAD_DIM, HEAD_DIM + 3, HEAD_DIM + 6, HEAD_DIM + 9


def _split3(f):
    hi = f.astype(BF16).astype(F32)
    r = f - hi
    mid = r.astype(BF16).astype(F32)
    return hi, mid, r - mid


def _place3(lane, first, parts, otherwise):
    a, b, c = parts
    return jnp.where(lane == first, a, jnp.where(lane == first + 1, b, jnp.where(lane == first + 2, c, otherwise)))


def _log_forget(fl, bf):
    z = fl + bf
    lf = jnp.minimum(z, 0.0) - jnp.log1p(jnp.exp(-jnp.abs(z)))
    lane = lax.broadcasted_iota(jnp.int32, z.shape, 1)
    return jnp.where(lane < HEADS, lf, 0.0)


def _qkv_prep(proj, fl, bf_pad, qg, kg):
    s = proj.shape[0]
    tm = min(TM_ELEM, s)
    scale = HEAD_DIM ** -0.5

    def body(p_ref, fl_ref, bf_ref, qg_ref, kg_ref, qa_ref, ka_ref, va_ref, kt_ref, vt_ref, carry):
        @pl.when(pl.program_id(0) == 0)
        def _():
            carry[...] = jnp.zeros_like(carry)
        tri = (lax.broadcasted_iota(jnp.int32, (tm, tm), 1) <= lax.broadcasted_iota(jnp.int32, (tm, tm), 0)).astype(F32)
        cum_v = jnp.dot(tri, _log_forget(fl_ref[...], bf_ref[...]), preferred_element_type=F32,
                        precision=lax.Precision.HIGHEST) + carry[...]
        carry[...] = _sub_row(cum_v, tm - 1)
        lane = lax.broadcasted_iota(jnp.int32, (tm, LANES), 1)
        lo = lane < HEAD_DIM
        v_tail = jnp.where(lane < L_F_Q, 1.0, 0.0)
        for pr in range(ATTN_W // LANES):
            sl = slice(pr * LANES, (pr + 1) * LANES)
            q2 = p_ref[:, OFF_Q + pr * LANES:OFF_Q + (pr + 1) * LANES].astype(F32)
            k2 = p_ref[:, OFF_K + pr * LANES:OFF_K + (pr + 1) * LANES].astype(F32)
            v2 = p_ref[:, OFF_V + pr * LANES:OFF_V + (pr + 1) * LANES].astype(F32)
            rq = lax.rsqrt(_seg_sum(q2 * q2, lo) * (1.0 / HEAD_DIM) + EPS)
            rk = lax.rsqrt(_seg_sum(k2 * k2, lo) * (1.0 / HEAD_DIM) + EPS)
            qn = ((q2 * rq) * qg_ref[:, sl]) * (scale * LOG2E)
            kn = (k2 * rk) * kg_ref[:, sl]
            for hh in range(2):
                h = 2 * pr + hh
                f3 = _split3(_lane_col(cum_v, h) * LOG2E)
                qh = qn if hh == 0 else pltpu.roll(qn, HEAD_DIM, 1)
                kh = kn if hh == 0 else pltpu.roll(kn, HEAD_DIM, 1)
                vh = v2 if hh == 0 else pltpu.roll(v2, HEAD_DIM, 1)
                q_tail = jnp.where(lane < L_F_Q, 1.0, _place3(lane, L_F_Q, f3, 0.0))
                k_tail = _place3(lane, L_ONE_Q, tuple(-f for f in f3), jnp.where(lane < L_END, 1.0, 0.0))
                k_row = jnp.where(lo, kh, k_tail)
                v_row = jnp.where(lo, vh, v_tail)
                qa_ref[h] = jnp.where(lo, qh, q_tail).astype(BF16)
                ka_ref[h] = k_row.astype(BF16)
                va_ref[h] = v_row.astype(BF16)
                kt_ref[h] = k_row.T.astype(BF16)
                vt_ref[h] = v_row.T.astype(BF16)

    heads = pl.BlockSpec((HEADS, tm, LANES), lambda i: (0, i, 0))
    heads_t = pl.BlockSpec((HEADS, LANES, tm), lambda i: (0, 0, i))
    vec = pl.BlockSpec((1, ATTN_W), lambda i: (0, 0))
    return pl.pallas_call(
        body, name="qkv_prep", grid=(s // tm,),
        in_specs=[pl.BlockSpec((tm, 3 * ATTN_W), lambda i: (i, 0)),
                  pl.BlockSpec((tm, LANES), lambda i: (i, 0)),
                  pl.BlockSpec((1, LANES), lambda i: (0, 0)), vec, vec],
        out_specs=[heads, heads, heads, heads_t, heads_t],
        out_shape=[jax.ShapeDtypeStruct((HEADS, s, LANES), BF16)] * 3
        + [jax.ShapeDtypeStruct((HEADS, LANES, s), BF16)] * 2,
        scratch_shapes=[pltpu.VMEM((1, LANES), F32)],
        compiler_params=_params(("arbitrary",)),
    )(proj, fl, bf_pad, qg, kg)


def _causal_t(t):
    return lax.broadcasted_iota(jnp.int32, (t, t), 0) <= lax.broadcasted_iota(jnp.int32, (t, t), 1)


def _tri_steps(nt, q_major):
    if q_major:
        pairs = [(i, j) for i in range(nt) for j in range(i + 1)]
    else:
        pairs = [(i, j) for j in range(nt) for i in range(j, nt)]
    return (jnp.asarray(np.array([p[0] for p in pairs], np.int32)),
            jnp.asarray(np.array([p[1] for p in pairs], np.int32)))


def _attn_fwd(qa, ka, vt, proj):
    s = qa.shape[1]
    t = min(TQ, s)
    it, jt = _tri_steps(s // t, True)
    hp = HEADS_PER_STEP_FWD
    wide = hp * HEAD_DIM
    za_blk = OFF_ZA // wide

    def body(it_ref, jt_ref, q_ref, k_ref, vt_ref, za_ref, attn_ref, oa_ref, qb_ref, m_s, acc_s, pair_s):
        step = pl.program_id(1)
        i, j = it_ref[step], jt_ref[step]

        @pl.when(j == 0)
        def _():
            m_s[...] = jnp.full_like(m_s, NEG)
            acc_s[...] = jnp.zeros_like(acc_s)

        def update(masked):
            for hh in range(hp):
                st = _dot_nt(k_ref[hh], q_ref[hh])
                if masked:
                    st = jnp.where(_causal_t(t), st, NEG)
                m_prev = m_s[hh]
                m_next = jnp.maximum(m_prev, jnp.max(st, axis=0, keepdims=True))
                alpha = jnp.exp2(m_prev - m_next)
                pt = jnp.exp2(st - m_next).astype(BF16)
                acc_s[hh] = acc_s[hh] * alpha + _dot(vt_ref[hh], pt)
                m_s[hh] = m_next

        @pl.when(j < i)
        def _():
            update(False)

        @pl.when(j == i)
        def _():
            update(True)
            row = lax.broadcasted_iota(jnp.int32, (LANES, t), 0)
            lane = lax.broadcasted_iota(jnp.int32, (t, LANES), 1)
            for hh in range(hp):
                l_row = acc_s[hh, L_ONE_Q:L_ONE_Q + 1, :]
                pair_s[hh * HEAD_DIM:(hh + 1) * HEAD_DIM, :] = acc_s[hh, 0:HEAD_DIM, :] / l_row
                lse3 = _split3(m_s[hh] + jnp.log2(l_row))
                tail_t = _place3(row, L_LSE_Q, tuple(-x for x in lse3), 0.0)
                keep_q = jnp.logical_or(lane < L_LSE_Q, lane >= L_END)
                qb_ref[hh] = jnp.where(keep_q, q_ref[hh].astype(F32), tail_t.T).astype(BF16)
            out = pair_s[...].T
            attn_ref[...] = out
            z = za_ref[...].astype(F32)
            oa_ref[...] = (out * (z * _sigmoid(z))).astype(BF16)

    pair_q = pl.BlockSpec((hp, t, LANES), lambda p, n, it_, jt_: (p, it_[n], 0))
    pair_k = pl.BlockSpec((hp, t, LANES), lambda p, n, it_, jt_: (p, jt_[n], 0))
    pair_kt = pl.BlockSpec((hp, LANES, t), lambda p, n, it_, jt_: (p, 0, jt_[n]))
    out_q = pl.BlockSpec((t, wide), lambda p, n, it_, jt_: (it_[n], p))
    return pl.pallas_call(
        body, name="attn_fwd",
        grid_spec=pltpu.PrefetchScalarGridSpec(
            num_scalar_prefetch=2, grid=(HEADS // hp, it.shape[0]),
            in_specs=[pair_q, pair_k, pair_kt,
                      pl.BlockSpec((t, wide), lambda p, n, it_, jt_: (it_[n], za_blk + p))],
            out_specs=[out_q, out_q, pair_q],
            scratch_shapes=[pltpu.VMEM((hp, 1, t), F32), pltpu.VMEM((hp, LANES, t), F32),
                            pltpu.VMEM((wide, t), F32)]),
        out_shape=[jax.ShapeDtypeStruct((s, ATTN_W), F32),
                   jax.ShapeDtypeStruct((s, ATTN_W), BF16),
                   jax.ShapeDtypeStruct((HEADS, s, LANES), BF16)],
        compiler_params=_params(("parallel", "arbitrary")),
    )(it, jt, qa, ka, vt, proj)


def _conv_parts(gb_ref, gc_ref, u_ref, zb_ref, gch_ref, uh_ref, first, w_ref, tm):
    gb, gc = gb_ref[...].astype(F32), gc_ref[...].astype(F32)
    u, zb = u_ref[...].astype(F32), zb_ref[...].astype(F32)
    cu = gc * u
    cu_h = jnp.where(first, 0.0, gch_ref[...].astype(F32) * uh_ref[...].astype(F32))
    prev1, prev2 = _sub_row(cu_h, HALO - 1), _sub_row(cu_h, HALO - 2)
    row = lax.broadcasted_iota(jnp.int32, cu.shape, 0)
    r1 = jnp.where(row == 0, prev1, pltpu.roll(cu, 1, 0))
    r2 = jnp.where(row == 0, prev2, jnp.where(row == 1, prev1, pltpu.roll(cu, 2, 0)))
    conv = w_ref[2:3, :] * cu + w_ref[1:2, :] * r1 + w_ref[0:1, :] * r2
    return gb, gc, u, zb, cu, r1, r2, conv


def _conv_specs(tm, s, width=LANES):
    def tile(off):
        return pl.BlockSpec((tm, width), lambda c, i: (i, off // width + c))

    def before(off):
        return pl.BlockSpec((HALO, width), lambda c, i: (jnp.maximum(i * (tm // HALO) - 1, 0), off // width + c))

    def after(off):
        return pl.BlockSpec((HALO, width),
                            lambda c, i: (jnp.minimum((i + 1) * (tm // HALO), s // HALO - 1), off // width + c))

    return ([tile(OFF_CB), tile(OFF_CC), tile(OFF_CU), tile(OFF_CZ)], [before(OFF_CC), before(OFF_CU)],
            [after(OFF_CB), after(OFF_CZ)])


def _tail(oa, attn, proj, x, target, ada3, wa, wb, wo, conv_w):
    s = x.shape[0]
    tm = min(TM_TAIL, s)
    gab_blk = OFF_GA // (2 * D_MODEL)
    za_blk = OFF_ZA // ATTN_W
    tiles, befores, _ = _conv_specs(tm, s, CONV_W)

    def body(oa_ref, attn_ref, za_ref, gb_ref, gc_ref, u_ref, zb_ref, gch_ref, uh_ref, cw_ref, gab_ref, x_ref, t_ref,
             ada_ref, wa_ref, wb_ref, wo_ref,
             dy_ref, dgab_ref, do_ref, dza_ref, dob_ref, dwo_ref, dwa_ref, dwb_ref, dgate_ref, loss_ref):
        first = pl.program_id(0) == 0

        @pl.when(first)
        def _():
            dwo_ref[...] = jnp.zeros_like(dwo_ref)
            dwa_ref[...] = jnp.zeros_like(dwa_ref)
            dwb_ref[...] = jnp.zeros_like(dwb_ref)
            dgate_ref[...] = jnp.zeros_like(dgate_ref)
            loss_ref[...] = jnp.zeros_like(loss_ref)

        gb, _, _, zb, _, _, _, conv = _conv_parts(gb_ref, gc_ref, u_ref, zb_ref, gch_ref, uh_ref, first, cw_ref, tm)
        ob_v = (gb * conv * (zb * _sigmoid(zb))).astype(BF16)
        oa_v = oa_ref[...]
        wa_v, wb_v, wo_v = wa_ref[...], wb_ref[...], wo_ref[...]
        a2 = _dot(oa_v, wa_v)
        b2 = _dot(ob_v, wb_v)
        sa = _sigmoid(gab_ref[:, 0:D_MODEL].astype(F32))
        sb = _sigmoid(gab_ref[:, D_MODEL:2 * D_MODEL].astype(F32))
        mb = (sa * a2 + sb * b2).astype(BF16)
        mo = _dot(mb, wo_v)
        gate = ada_ref[2:3, :]
        err = (x_ref[...] + gate * mo) - t_ref[...]
        dy = err * (1.0 / D_MODEL)
        dy_ref[...] = dy
        loss_ref[...] += 0.5 * jnp.sum(err * err) * (1.0 / D_MODEL)
        dgate_ref[...] += jnp.sum(dy * mo, axis=0, keepdims=True)
        dmo = (dy * gate).astype(BF16)
        dmerged = _dot_nt(dmo, wo_v)
        dwo_ref[...] += _dot_tn(mb, dmo)
        da2 = (dmerged * sa).astype(BF16)
        db2 = (dmerged * sb).astype(BF16)
        dgab_ref[:, 0:D_MODEL] = (dmerged * a2 * (sa * (1.0 - sa))).astype(BF16)
        dgab_ref[:, D_MODEL:2 * D_MODEL] = (dmerged * b2 * (sb * (1.0 - sb))).astype(BF16)
        doa = _dot_nt(da2, wa_v)
        dob_ref[...] = _dot_nt(db2, wb_v)
        dwa_ref[...] += _dot_tn(oa_v, da2)
        dwb_ref[...] += _dot_tn(ob_v, db2)

        lane = lax.broadcasted_iota(jnp.int32, (tm, LANES), 1)
        lo = lane < HEAD_DIM
        for pr in range(ATTN_W // LANES):
            sl = slice(pr * LANES, (pr + 1) * LANES)
            g, a, z = doa[:, sl], attn_ref[:, sl], za_ref[:, sl].astype(F32)
            sg = _sigmoid(z)
            dat = (g * (z * sg)).astype(BF16).astype(F32)
            prod = dat * a
            dza_ref[:, sl] = (g * a * (sg * (1.0 + z * (1.0 - sg)))).astype(BF16)
            for hh in range(2):
                sel = lo if hh == 0 else jnp.logical_not(lo)
                delta3 = _split3(jnp.sum(jnp.where(sel, prod, 0.0), axis=-1, keepdims=True))
                dh = dat if hh == 0 else pltpu.roll(dat, HEAD_DIM, 1)
                tail_lanes = _place3(lane, L_ONE_Q, tuple(-d for d in delta3), 0.0)
                do_ref[2 * pr + hh] = jnp.where(lo, dh, tail_lanes).astype(BF16)

    half = pl.BlockSpec((tm, ATTN_W), lambda i: (i, 0))
    full = pl.BlockSpec((tm, D_MODEL), lambda i: (i, 0))

    def const(shape):
        return pl.BlockSpec(shape, lambda i: (0, 0))

    def one_axis(spec):
        return pl.BlockSpec(spec.block_shape, lambda i, f=spec.index_map: f(0, i))

    return pl.pallas_call(
        body, name="tail", grid=(s // tm,),
        in_specs=[half, half, pl.BlockSpec((tm, ATTN_W), lambda i: (i, za_blk))]
        + [one_axis(sp) for sp in tiles + befores]
        + [const((3, CONV_W)), pl.BlockSpec((tm, 2 * D_MODEL), lambda i: (i, gab_blk)), full, full,
           const((3, D_MODEL)), const((ATTN_W, D_MODEL)), const((CONV_W, D_MODEL)), const((D_MODEL, D_MODEL))],
        out_specs=[full, pl.BlockSpec((tm, 2 * D_MODEL), lambda i: (i, 0)),
                   pl.BlockSpec((HEADS, tm, LANES), lambda i: (0, i, 0)), half, half,
                   const((D_MODEL, D_MODEL)), const((ATTN_W, D_MODEL)), const((CONV_W, D_MODEL)),
                   const((1, D_MODEL)), const((1, LANES))],
        out_shape=[jax.ShapeDtypeStruct((s, D_MODEL), F32),
                   jax.ShapeDtypeStruct((s, 2 * D_MODEL), BF16),
                   jax.ShapeDtypeStruct((HEADS, s, LANES), BF16),
                   jax.ShapeDtypeStruct((s, ATTN_W), BF16),
                   jax.ShapeDtypeStruct((s, CONV_W), F32),
                   jax.ShapeDtypeStruct((D_MODEL, D_MODEL), F32),
                   jax.ShapeDtypeStruct((ATTN_W, D_MODEL), F32),
                   jax.ShapeDtypeStruct((CONV_W, D_MODEL), F32),
                   jax.ShapeDtypeStruct((1, D_MODEL), F32),
                   jax.ShapeDtypeStruct((1, LANES), F32)],
        compiler_params=_params(("arbitrary",)),
    )(oa, attn, proj, *([proj] * 6), conv_w, proj, x, target, ada3, wa, wb, wo)


def _attn_bwd(qb, ka, kt, va, do, proj, qg, kg):
    s = qb.shape[1]
    t = min(TQ, s)
    nt = s // t
    hp = HEADS_PER_STEP
    wide = hp * HEAD_DIM
    scale = HEAD_DIM ** -0.5
    it, jt = _tri_steps(nt, False)

    def body(it_ref, jt_ref, q_ref, k_ref, kt_ref, v_ref, do_ref, qraw_ref, kraw_ref, qg_ref, kg_ref,
             dq_ref, dk_ref, dv_ref, dqg_ref, dkg_ref, dcum_ref, dqt_s, dk_s, dv_s, rows_s):
        grp, step = pl.program_id(0), pl.program_id(1)
        i, j = it_ref[step], jt_ref[step]
        lane = lax.broadcasted_iota(jnp.int32, (t, LANES), 1)
        lo = lane < HEAD_DIM

        @pl.when(step == 0)
        def _():
            dqt_s[...] = jnp.zeros_like(dqt_s)
            dqg_ref[...] = jnp.zeros_like(dqg_ref)
            dkg_ref[...] = jnp.zeros_like(dkg_ref)

        @pl.when(i == j)
        def _():
            dk_s[...] = jnp.zeros_like(dk_s)
            dv_s[...] = jnp.zeros_like(dv_s)

        def update(masked):
            for hh in range(hp):
                qh, doh = q_ref[hh], do_ref[hh]
                st = _dot_nt(k_ref[hh], qh)
                if masked:
                    st = jnp.where(_causal_t(t), st, NEG)
                pt = jnp.exp2(st)
                dst = (pt * _dot_nt(v_ref[hh], doh)).astype(BF16)
                dv_s[hh] += _dot(pt.astype(BF16), doh)
                dk_s[hh] += _dot(dst, qh)
                dqt_s[hh, i] += _dot(kt_ref[hh], dst)

        def pair(a, b):
            return jnp.where(lo, a, pltpu.roll(b, HEAD_DIM, 1))

        def norm_bwd(raw, dy, g, dg_ref, out_ref, sl):
            r = lax.rsqrt(_seg_sum(raw * raw, lo) * (1.0 / HEAD_DIM) + EPS)
            xhat = raw * r
            dg_ref[:, sl] += jnp.sum(dy * xhat, axis=0, keepdims=True)
            dxh = dy * g
            dx = r * (dxh - xhat * (_seg_sum(dxh * xhat, lo) * (1.0 / HEAD_DIM)))
            out_ref[:, sl] = dx.astype(BF16)

        @pl.when(i > j)
        def _():
            update(False)

        @pl.when(i == j)
        def _():
            update(True)
            dq_rows = [dqt_s[hh, i].T for hh in range(hp)]
            rows = jnp.zeros((t, LANES), F32)
            for hh in range(hp):
                rows = jnp.where(lane == grp * hp + hh, _lane_col(dq_rows[hh], L_F_Q), rows)
            rows_s[...] = rows
            for pr in range(hp // 2):
                sl = slice(pr * LANES, (pr + 1) * LANES)
                norm_bwd(qraw_ref[:, sl].astype(F32), pair(dq_rows[2 * pr], dq_rows[2 * pr + 1]) * scale,
                         qg_ref[:, sl], dqg_ref, dq_ref, sl)

        @pl.when(i == nt - 1)
        def _():
            dcum = rows_s[...]
            for hh in range(hp):
                dcum = jnp.where(lane == grp * hp + hh, dcum - _lane_col(dk_s[hh], L_ONE_Q), dcum)
            dcum_ref[0] = dcum
            for pr in range(hp // 2):
                sl = slice(pr * LANES, (pr + 1) * LANES)
                norm_bwd(kraw_ref[:, sl].astype(F32), pair(dk_s[2 * pr], dk_s[2 * pr + 1]) * (1.0 / LOG2E),
                         kg_ref[:, sl], dkg_ref, dk_ref, sl)
                dv_ref[:, sl] = pair(dv_s[2 * pr], dv_s[2 * pr + 1]).astype(BF16)

    pair_q = pl.BlockSpec((hp, t, LANES), lambda p, n, it_, jt_: (p, it_[n], 0))
    pair_k = pl.BlockSpec((hp, t, LANES), lambda p, n, it_, jt_: (p, jt_[n], 0))
    pair_kt = pl.BlockSpec((hp, LANES, t), lambda p, n, it_, jt_: (p, 0, jt_[n]))
    tok = pl.BlockSpec((t, wide), lambda p, n, it_, jt_: (jt_[n], p))
    gain = pl.BlockSpec((1, wide), lambda p, n, it_, jt_: (0, p))
    return pl.pallas_call(
        body, name="attn_bwd",
        grid_spec=pltpu.PrefetchScalarGridSpec(
            num_scalar_prefetch=2, grid=(HEADS // hp, it.shape[0]),
            in_specs=[pair_q, pair_k, pair_kt, pair_k, pair_q,
                      pl.BlockSpec((t, wide), lambda p, n, it_, jt_: (jt_[n], OFF_Q // wide + p)),
                      pl.BlockSpec((t, wide), lambda p, n, it_, jt_: (jt_[n], OFF_K // wide + p)), gain, gain],
            out_specs=[tok, tok, tok, gain, gain,
                       pl.BlockSpec((1, t, LANES), lambda p, n, it_, jt_: (p, jt_[n], 0))],
            scratch_shapes=[pltpu.VMEM((hp, nt, LANES, t), F32), pltpu.VMEM((hp, t, LANES), F32),
                            pltpu.VMEM((hp, t, LANES), F32), pltpu.VMEM((t, LANES), F32)]),
        out_shape=[jax.ShapeDtypeStruct((s, ATTN_W), BF16)] * 3
        + [jax.ShapeDtypeStruct((1, ATTN_W), F32)] * 2
        + [jax.ShapeDtypeStruct((HEADS // hp, s, LANES), F32)],
        compiler_params=_params(("parallel", "arbitrary")),
    )(it, jt, qb, ka, kt, va, do, proj, proj, qg, kg)


def _forget_bwd(dcum, fl, bf_pad):
    s = fl.shape[0]
    tc = min(TC_CUM, s)
    n = s // tc

    def body(dc_ref, fl_ref, bf_ref, df_ref, dbf_ref, carry):
        @pl.when(pl.program_id(0) == 0)
        def _():
            carry[...] = jnp.zeros_like(carry)
            dbf_ref[...] = jnp.zeros_like(dbf_ref)
        r = lax.broadcasted_iota(jnp.int32, (tc, tc), 0)
        cidx = lax.broadcasted_iota(jnp.int32, (tc, tc), 1)
        tri = (cidx >= r).astype(F32)
        dc = dc_ref[0]
        for grp in range(1, dcum.shape[0]):
            dc = dc + dc_ref[grp]
        dlf = jnp.dot(tri, dc, preferred_element_type=F32, precision=lax.Precision.HIGHEST) + carry[...]
        carry[...] += jnp.sum(dc, axis=0, keepdims=True)
        lane = lax.broadcasted_iota(jnp.int32, (tc, LANES), 1)
        dfl = jnp.where(lane < HEADS, dlf * _sigmoid(-(fl_ref[...] + bf_ref[...])), 0.0)
        df_ref[...] = dfl.astype(BF16)
        dbf_ref[...] += jnp.sum(dfl, axis=0, keepdims=True)

    rev = pl.BlockSpec((tc, LANES), lambda i: (n - 1 - i, 0))
    vec = pl.BlockSpec((1, LANES), lambda i: (0, 0))
    return pl.pallas_call(
        body, name="forget_bwd", grid=(n,),
        in_specs=[pl.BlockSpec((dcum.shape[0], tc, LANES), lambda i: (0, n - 1 - i, 0)), rev, vec],
        out_specs=[rev, vec],
        out_shape=[jax.ShapeDtypeStruct((s, LANES), BF16), jax.ShapeDtypeStruct((1, LANES), F32)],
        scratch_shapes=[pltpu.VMEM((1, LANES), F32)],
        compiler_params=_params(("arbitrary",)),
    )(dcum, fl, bf_pad)


def _conv_bwd(dob, proj, conv_w):
    s = dob.shape[0]
    tm = min(TM_ELEM, s)
    tiles, befores, afters = _conv_specs(tm, s)

    def body(dob_ref, dnext_ref, gb_ref, gc_ref, u_ref, zb_ref, gch_ref, uh_ref, gbn_ref, zbn_ref, w_ref,
             dgb_ref, dgc_ref, du_ref, dzb_ref, dw_ref):
        i = pl.program_id(1)

        @pl.when(i == 0)
        def _():
            dw_ref[...] = jnp.zeros_like(dw_ref)
        gb, gc, u, zb, cu, r1, r2, conv = _conv_parts(gb_ref, gc_ref, u_ref, zb_ref, gch_ref, uh_ref, i == 0, w_ref, tm)
        g = dob_ref[...]
        sg = _sigmoid(zb)
        sz = zb * sg
        dconv = g * gb * sz
        zn = zbn_ref[0:8, :].astype(F32)
        dcn = jnp.where(i == pl.num_programs(1) - 1, 0.0,
                        dnext_ref[...] * gbn_ref[0:8, :].astype(F32) * (zn * _sigmoid(zn)))
        nxt1, nxt2 = _sub_row(dcn, 0), _sub_row(dcn, 1)
        row = lax.broadcasted_iota(jnp.int32, (tm, LANES), 0)
        f1 = jnp.where(row == tm - 1, nxt1, pltpu.roll(dconv, tm - 1, 0))
        f2 = jnp.where(row == tm - 2, nxt1, jnp.where(row == tm - 1, nxt2, pltpu.roll(dconv, tm - 2, 0)))
        dcu = w_ref[2:3, :] * dconv + w_ref[1:2, :] * f1 + w_ref[0:1, :] * f2
        dgb_ref[...] = (g * conv * sz).astype(BF16)
        dgc_ref[...] = (dcu * u).astype(BF16)
        du_ref[...] = (dcu * gc).astype(BF16)
        dzb_ref[...] = (g * gb * conv * (sg * (1.0 + zb * (1.0 - sg)))).astype(BF16)
        w_row = lax.broadcasted_iota(jnp.int32, (3, LANES), 0)
        dw0 = jnp.sum(dconv * r2, axis=0, keepdims=True)
        dw1 = jnp.sum(dconv * r1, axis=0, keepdims=True)
        dw2 = jnp.sum(dconv * cu, axis=0, keepdims=True)
        dw_ref[...] += jnp.where(w_row == 0, dw0, jnp.where(w_row == 1, dw1, dw2))

    blk = pl.BlockSpec((tm, LANES), lambda c, i: (i, c))
    nxt = pl.BlockSpec((8, LANES), lambda c, i: (jnp.minimum((i + 1) * (tm // 8), s // 8 - 1), c))
    wspec = pl.BlockSpec((3, LANES), lambda c, i: (0, c))
    return pl.pallas_call(
        body, name="conv_bwd", grid=(CONV_W // LANES, s // tm),
        in_specs=[blk, nxt] + tiles + befores + afters + [wspec],
        out_specs=[blk, blk, blk, blk, wspec],
        out_shape=[jax.ShapeDtypeStruct((s, CONV_W), BF16)] * 4 + [jax.ShapeDtypeStruct((3, CONV_W), F32)],
        compiler_params=_params(("parallel", "arbitrary")),
    )(dob, dob, *([proj] * 8), conv_w)


def _piece_layout(pieces):
    offs, off = [], 0
    for p in pieces:
        offs.append((off, p.shape[1]))
        off += p.shape[1]
    assert off == N_ALL, off
    return offs


def _dw_in(h, pieces, chip_sums):
    s = h.shape[0]
    tk, tn = min(TK_DW, s), TN_DW
    nk = s // tk
    nn = N_MAIN // tn
    main, fpiece = pieces[:-1], pieces[-1]
    layout = _piece_layout(pieces)[:-1]
    n_main = len(main)
    nx = len(chip_sums)

    def body(*refs):
        p_refs, f_ref, h_ref = refs[:n_main], refs[n_main], refs[n_main + 1]
        ins, refs = refs[n_main + 2:n_main + 2 + nx], refs[n_main + 2 + nx:]
        out_ref, outf_ref = refs[:2]
        outs, (acc, accf, send_sems, recv_sems, local_sems) = refs[2:2 + nx], refs[2 + nx:]
        n, k = pl.program_id(0), pl.program_id(1)

        @pl.when(jnp.logical_and(n == 0, k == 0))
        def _():
            for cp in _chip_copies(ins, outs, send_sems, recv_sems, local_sems):
                cp.start()

        @pl.when(k == 0)
        def _():
            acc[...] = jnp.zeros_like(acc)
        hv = h_ref[pl.ds(pl.multiple_of(k * tk, tk), tk), :]
        for p_ref, (off, width) in zip(p_refs, layout):
            @pl.when(jnp.logical_and(n >= off // tn, n < (off + width) // tn))
            def _():
                acc[...] += _dot_tn(p_ref[...], hv)

        @pl.when(k == nk - 1)
        def _():
            out_ref[...] = acc[...].astype(BF16)

        @pl.when(n == 0)
        def _():
            @pl.when(k == 0)
            def _():
                accf[...] = jnp.zeros_like(accf)
            accf[...] += _dot_tn(f_ref[...], hv)

            @pl.when(k == nk - 1)
            def _():
                outf_ref[...] = accf[...].astype(BF16)

        @pl.when(jnp.logical_and(n == nn - 1, k == nk - 1))
        def _():
            for cp in _chip_copies(ins, outs, send_sems, recv_sems, local_sems):
                cp.wait()

    def piece_spec(off, width):
        lo, hi = off // tn, (off + width) // tn

        def index(n, k):
            active = jnp.logical_and(n >= lo, n < hi)
            return jnp.where(active, k, 0), jnp.clip(n - lo, 0, hi - lo - 1)
        return pl.BlockSpec((tk, tn), index)

    any_spec = pl.BlockSpec(memory_space=pl.ANY)
    res = pl.pallas_call(
        body, name="dw_in", grid=(nn, nk),
        in_specs=[piece_spec(off, width) for off, width in layout]
        + [pl.BlockSpec((tk, N_FPAD), lambda n, k: (jnp.where(n == 0, k, 0), 0)),
           pl.BlockSpec((s, D_MODEL), lambda n, k: (0, 0))] + [any_spec] * nx,
        out_specs=[pl.BlockSpec((tn, D_MODEL), lambda n, k: (n, 0)),
                   pl.BlockSpec((N_FPAD, D_MODEL), lambda n, k: (0, 0))] + [any_spec] * nx,
        out_shape=[jax.ShapeDtypeStruct((N_MAIN, D_MODEL), BF16), jax.ShapeDtypeStruct((N_FPAD, D_MODEL), BF16)]
        + [jax.ShapeDtypeStruct(a.shape, a.dtype) for a in chip_sums],
        scratch_shapes=[pltpu.VMEM((tn, D_MODEL), F32), pltpu.VMEM((N_FPAD, D_MODEL), F32),
                        pltpu.SemaphoreType.DMA((nx * 3,)), pltpu.SemaphoreType.DMA((nx * 3,)),
                        pltpu.SemaphoreType.DMA((nx,))],
        compiler_params=_params(("arbitrary", "arbitrary")),
    )(*main, fpiece, h, *chip_sums)
    return res[:2], res[2:]


def _dh_and_dx(pieces, w_all_t, x, dy, ada3, norm_g, chip_sums):
    s = x.shape[0]
    tm = min(TM_DH, s)
    nt = s // tm
    n = len(chip_sums)
    npc = len(pieces)
    layout = _piece_layout(pieces)

    def body(*refs):
        p_refs, refs = refs[:npc], refs[npc:]
        wt_ref, x_ref, dy_ref, ada_ref, g_ref = refs[:5]
        ins, refs = refs[5:5 + n], refs[5 + n:]
        gx_ref, dsh_ref, dsc_ref, dg_ref = refs[:4]
        outs, (send_sems, recv_sems, local_sems) = refs[4:4 + n], refs[4 + n:]
        i = pl.program_id(0)

        @pl.when(i == 0)
        def _():
            for cp in _chip_copies(ins, outs, send_sems, recv_sems, local_sems):
                cp.start()
            dsh_ref[...] = jnp.zeros_like(dsh_ref)
            dsc_ref[...] = jnp.zeros_like(dsc_ref)
            dg_ref[...] = jnp.zeros_like(dg_ref)

        dh = None
        for p_ref, (off, width) in zip(p_refs, layout):
            part = _dot(p_ref[...], wt_ref[off:off + width, :])
            dh = part if dh is None else dh + part
        xv = x_ref[...]
        r = lax.rsqrt(jnp.mean(xv * xv, axis=-1, keepdims=True) + EPS)
        xhat = xv * r
        g = g_ref[...]
        one_sc = 1.0 + ada_ref[1:2, :]
        dsh_ref[...] += jnp.sum(dh, axis=0, keepdims=True)
        dsc_ref[...] += jnp.sum(dh * (xhat * g), axis=0, keepdims=True)
        dg_ref[...] += jnp.sum(dh * xhat, axis=0, keepdims=True) * one_sc
        dxh = dh * (g * one_sc)
        dx = r * (dxh - xhat * jnp.mean(dxh * xhat, axis=-1, keepdims=True))
        gx_ref[...] = dy_ref[...] + dx

        @pl.when(i == nt - 1)
        def _():
            for cp in _chip_copies(ins, outs, send_sems, recv_sems, local_sems):
                cp.wait()

    full = pl.BlockSpec((tm, D_MODEL), lambda i: (i, 0))
    vec = pl.BlockSpec((1, D_MODEL), lambda i: (0, 0))
    any_spec = pl.BlockSpec(memory_space=pl.ANY)
    res = pl.pallas_call(
        body, name="dh_dx", grid=(nt,),
        in_specs=[pl.BlockSpec((tm, p.shape[1]), lambda i: (i, 0)) for p in pieces]
        + [pl.BlockSpec((N_ALL, D_MODEL), lambda i: (0, 0)), full, full,
           pl.BlockSpec((3, D_MODEL), lambda i: (0, 0)), vec] + [any_spec] * n,
        out_specs=[full, vec, vec, vec] + [any_spec] * n,
        out_shape=[jax.ShapeDtypeStruct((s, D_MODEL), F32)] + [jax.ShapeDtypeStruct((1, D_MODEL), F32)] * 3
        + [jax.ShapeDtypeStruct(a.shape, a.dtype) for a in chip_sums],
        scratch_shapes=[pltpu.SemaphoreType.DMA((n * 3,)), pltpu.SemaphoreType.DMA((n * 3,)),
                        pltpu.SemaphoreType.DMA((n,))],
        compiler_params=_params(("arbitrary",)),
    )(*pieces, w_all_t, x, dy, ada3, norm_g, *chip_sums)
    return res[:4], res[4:]


def _sum_small(vec_all, qg_parts, kg_parts):
    def body(v_ref, q_ref, k_ref, tot_ref, gq_ref, gk_ref):
        tot = v_ref[0:1, :]
        for p in range(1, N_DEV):
            tot = tot + v_ref[p:p + 1, :]
        tot_ref[...] = tot
        gq_ref[...] = jnp.sum(q_ref[...], axis=0, keepdims=True)
        gk_ref[...] = jnp.sum(k_ref[...], axis=0, keepdims=True)

    n = vec_all.shape[-1]
    return pl.pallas_call(
        body, name="sum_small",
        out_shape=[jax.ShapeDtypeStruct((1, n), F32),
                   jax.ShapeDtypeStruct((1, HEAD_DIM), F32), jax.ShapeDtypeStruct((1, HEAD_DIM), F32)],
        compiler_params=_params(),
    )(vec_all, qg_parts, kg_parts)


def _grad_w_ada(c_cols, dada_rows):
    def body(c_ref, d_ref, out_ref):
        acc = c_ref[0] * d_ref[0]
        for b in range(1, N_DEV):
            acc = acc + c_ref[b] * d_ref[b]
        out_ref[...] = acc

    return pl.pallas_call(
        body, name="grad_w_ada",
        out_shape=jax.ShapeDtypeStruct((D_MODEL, ADA_SHARD), F32),
        compiler_params=_params(),
    )(c_cols, dada_rows)


def _adamw(w, m, v, g_parts, name):
    rows, cols = w.shape
    n_parts = g_parts.shape[0]
    tr = 256 if rows % 256 == 0 else rows
    tc = 256 if (tr == rows and rows > 256 and cols % 256 == 0) else cols
    c1 = 1.0 / (1.0 - ADAM_B1 ** ADAM_STEP)
    c2 = 1.0 / (1.0 - ADAM_B2 ** ADAM_STEP)

    def body(w_ref, m_ref, v_ref, g_ref, go_ref, d_ref, mo_ref, vo_ref):
        g = g_ref[0].astype(F32)
        for p in range(1, n_parts):
            g = g + g_ref[p].astype(F32)
        m_new = ADAM_B1 * m_ref[...] + (1.0 - ADAM_B1) * g
        v_new = ADAM_B2 * v_ref[...] + (1.0 - ADAM_B2) * (g * g)
        go_ref[...] = g
        mo_ref[...] = m_new
        vo_ref[...] = v_new
        d_ref[...] = -ADAM_LR * ((m_new * c1) / (jnp.sqrt(v_new * c2) + ADAM_EPS) + ADAM_WD * w_ref[...])

    blk = pl.BlockSpec((tr, tc), lambda i, j: (i, j))
    return pl.pallas_call(
        body, name=name, grid=(rows // tr, cols // tc),
        in_specs=[blk, blk, blk, pl.BlockSpec((n_parts, tr, tc), lambda i, j: (0, i, j))],
        out_specs=[blk] * 4,
        out_shape=[jax.ShapeDtypeStruct((rows, cols), F32)] * 4,
        compiler_params=_params(("parallel", "parallel")),
    )(w, m, v, g_parts)


_O_F = 1536


def _to_internal(wt_g):
    wf = wt_g.reshape(IN_WIDTH, D_MODEL)
    f = jnp.pad(wf[_O_F:_O_F + HEADS], ((0, N_FPAD - HEADS), (0, 0)))
    return jnp.concatenate([wf[:_O_F], wf[_O_F + HEADS:], f], axis=0)


def _slabs_by_core(dwt, dwt_f):
    sources = ((dwt, 0, _O_F, 0), (dwt_f, _O_F, _O_F + HEADS, _O_F), (dwt, _O_F + HEADS, IN_WIDTH, HEADS))

    def slab(p):
        lo, hi = p * IN_SHARD, (p + 1) * IN_SHARD
        parts = []
        for src, o_lo, o_hi, shift in sources:
            a, b = max(lo, o_lo), min(hi, o_hi)
            if a < b:
                parts.append(src[a - shift:b - shift])
        return parts[0] if len(parts) == 1 else jnp.concatenate(parts, axis=0)

    return jnp.stack([jnp.stack([slab(2 * chip + core) for chip in range(4)]) for core in range(2)])


def kernel(x, c, w_ada, b_ada, norm_g, w_in, b_f, q_norm_g, k_norm_g, conv_w, w_attn_out, w_conv_out, w_o, loss_target, m_w_ada, m_b_ada, m_norm_g, m_w_in, m_b_f, m_q_norm_g, m_k_norm_g, m_conv_w, m_w_attn_out, m_w_conv_out, m_w_o, v_w_ada, v_b_ada, v_norm_g, v_w_in, v_b_f, v_q_norm_g, v_k_norm_g, v_conv_w, v_w_attn_out, v_w_conv_out, v_w_o):
    me = 4 * lax.axis_index("x") + 2 * lax.axis_index("y") + lax.axis_index("c")
    s = x.shape[1]
    x2, t2 = x[0], loss_target[0]

    w_in_g, c_all, ada_g = _gather_weights_and_ada(w_in[0].T.astype(BF16), c, w_ada[0])
    ada_mine = lax.dynamic_index_in_dim(ada_g[:, :, 0, :], me, axis=1, keepdims=False)
    ada3 = (ada_mine.reshape(1, 3 * D_MODEL) + b_ada).reshape(3, D_MODEL)
    w_all_t = _to_internal(w_in_g)
    qg = jnp.tile(q_norm_g, (1, HEADS))
    kg = jnp.tile(k_norm_g, (1, HEADS))
    bf_pad = jnp.pad(b_f, ((0, 0), (0, LANES - HEADS)))

    (proj, fl, h), (cw_g, wa_g, wb_g, wo_g) = _proj_fwd(
        x2, ada3, norm_g, w_all_t,
        [conv_w[0], w_attn_out[0].astype(BF16), w_conv_out[0].astype(BF16), w_o[0].astype(BF16)])
    wa = jnp.transpose(wa_g, (1, 0, 2)).reshape(ATTN_W, D_MODEL)
    wb = jnp.transpose(wb_g, (1, 0, 2)).reshape(CONV_W, D_MODEL)
    wo = wo_g.reshape(D_MODEL, D_MODEL)
    cw = jnp.transpose(cw_g, (1, 0, 2)).reshape(3, CONV_W)
    qa, ka, va, kt, vt = _qkv_prep(proj, fl, bf_pad, qg, kg)
    attn, oa, qb = _attn_fwd(qa, ka, vt, proj)
    (dy, dgab, do, dza, dob, dwo, dwa, dwb, dgate, loss_part) = _tail(oa, attn, proj, x2, t2, ada3, wa, wb, wo, cw)

    def by_core(slabs8):
        return jnp.swapaxes(slabs8.reshape((4, 2) + slabs8.shape[1:]), 0, 1).astype(BF16)

    core = lax.axis_index("c").astype(jnp.int32).reshape(1)
    small = [by_core(jnp.transpose(dwa.reshape(ATTN_W, N_DEV, LANES), (1, 0, 2))),
             by_core(jnp.transpose(dwb.reshape(CONV_W, N_DEV, LANES), (1, 0, 2))),
             by_core(dwo.reshape(N_DEV, D_MODEL // N_DEV, D_MODEL))]
    small_sums = [_pair_sum(m2, t4, core, "pair_sum_" + nm)
                  for m2, t4, nm in zip(small, _sibling_swap(small, "swap_small"), ("wa", "wb", "wo"))]
    dq, dk, dv, dqg, dkg, dcum = _attn_bwd(qb, ka, kt, va, do, proj, qg, kg)
    df, dbf = _forget_bwd(dcum, fl, bf_pad)
    dcb, dcc, dcu, dcz, dcw = _conv_bwd(dob, proj, cw)
    pieces = [dq, dk, dv, dza, dcb, dcc, dcu, dcz, dgab, df]
    (dw_main, dw_f), (g_wa_parts, g_wb_parts, g_wo_parts) = _dw_in(h, pieces, small_sums)

    slabs_in = _slabs_by_core(dw_main, dw_f)
    (theirs_in,) = _sibling_swap([slabs_in], "swap_w_in")
    (grad_x, dshift, dscale, dnormg), (g_in_parts,) = _dh_and_dx(
        pieces, w_all_t, x2, dy, ada3, norm_g, [_pair_sum(slabs_in, theirs_in, core, "pair_sum_w_in")])
    vec = jnp.concatenate([dshift, dscale, dgate, dnormg, dbf, dcw.reshape(1, 3 * CONV_W), loss_part, dqg, dkg],
                          axis=1)
    (vec_all,) = _gather_direct([vec], "gather_small")
    vec_all = vec_all.reshape(N_DEV, vec.shape[1])
    n_main = 4 * D_MODEL + LANES + 3 * CONV_W + LANES
    tot, g_qg, g_kg = _sum_small(
        vec_all[:, :n_main],
        vec_all[:, n_main:n_main + ATTN_W].reshape(N_DEV * HEADS, HEAD_DIM),
        vec_all[:, n_main + ATTN_W:].reshape(N_DEV * HEADS, HEAD_DIM))
    g_b_ada = tot[:, 0:3 * D_MODEL]
    g_norm_g = tot[:, 3 * D_MODEL:4 * D_MODEL]
    g_b_f = tot[:, 4 * D_MODEL:4 * D_MODEL + HEADS]
    g_cw_full = tot[:, 4 * D_MODEL + LANES:4 * D_MODEL + LANES + 3 * CONV_W].reshape(3, CONV_W)
    g_cw = lax.dynamic_slice(g_cw_full, (0, me * (CONV_W // N_DEV)), (3, CONV_W // N_DEV))
    dada_mine = lax.dynamic_slice(vec_all[:, 0:3 * D_MODEL], (0, me * ADA_SHARD), (N_DEV, ADA_SHARD))
    g_w_ada = _grad_w_ada(jnp.transpose(c_all, (0, 2, 1)), dada_mine.reshape(N_DEV, 1, ADA_SHARD))

    upd = {}
    upd["w_ada"] = _adamw(w_ada[0], m_w_ada[0], v_w_ada[0], g_w_ada[None], "adamw_w_ada")
    upd["b_ada"] = _adamw(b_ada, m_b_ada, v_b_ada, g_b_ada[None], "adamw_b_ada")
    upd["norm_g"] = _adamw(norm_g, m_norm_g, v_norm_g, g_norm_g[None], "adamw_norm_g")
    upd["w_in"] = [u.T for u in _adamw(w_in[0].T, m_w_in[0].T, v_w_in[0].T, g_in_parts, "adamw_w_in")]
    upd["b_f"] = _adamw(b_f, m_b_f, v_b_f, g_b_f[None], "adamw_b_f")
    upd["q_norm_g"] = _adamw(q_norm_g, m_q_norm_g, v_q_norm_g, g_qg[None], "adamw_q_norm_g")
    upd["k_norm_g"] = _adamw(k_norm_g, m_k_norm_g, v_k_norm_g, g_kg[None], "adamw_k_norm_g")
    upd["conv_w"] = _adamw(conv_w[0], m_conv_w[0], v_conv_w[0], g_cw[None], "adamw_conv_w")
    upd["w_attn_out"] = _adamw(w_attn_out[0], m_w_attn_out[0], v_w_attn_out[0], g_wa_parts, "adamw_w_attn_out")
    upd["w_conv_out"] = _adamw(w_conv_out[0], m_w_conv_out[0], v_w_conv_out[0], g_wb_parts, "adamw_w_conv_out")
    upd["w_o"] = _adamw(w_o[0], m_w_o[0], v_w_o[0], g_wo_parts, "adamw_w_o")

    names = ["w_ada", "b_ada", "norm_g", "w_in", "b_f", "q_norm_g", "k_norm_g", "conv_w",
             "w_attn_out", "w_conv_out", "w_o"]
    lead = {"w_ada", "w_in", "conv_w", "w_attn_out", "w_conv_out", "w_o"}
    fix = lambda n, a: a[None] if n in lead else a
    loss = tot[0, n_main - LANES]
    outs = [loss, grad_x[None]]
    for k in range(4):
        outs += [fix(n, upd[n][k]) for n in names]
    return tuple(outs)
```

```python
import functools

import numpy as np
import jax
import jax.numpy as jnp
from jax import lax
from jax.experimental import pallas as pl
from jax.experimental.pallas import tpu as pltpu

F32 = jnp.float32
BF16 = jnp.bfloat16

D_MODEL = 1024
HEADS = 8
HEAD_DIM = 64
ATTN_W = 512
CONV_W = 512
N_DEV = 8
IN_WIDTH = 6152
IN_SHARD = IN_WIDTH // N_DEV
N_MAIN = 6144
N_FPAD = 128
N_ALL = N_MAIN + N_FPAD
ADA_SHARD = 3 * D_MODEL // N_DEV
EPS = 1e-6
NEG = -1e30

ADAM_LR = 0.001
ADAM_B1 = 0.9
ADAM_B2 = 0.999
ADAM_EPS = 1e-08
ADAM_WD = 0.01
ADAM_STEP = 10

LANES = 128
VMEM_LIMIT = 56 * 1024 * 1024

TM_PROJ = 256
TN_PROJ = 1024
TM_ELEM = 512
TQ = 512
HEADS_PER_STEP = 8
HEADS_PER_STEP_FWD = 8
TM_TAIL = 256
TC_CUM = 256
TK_DW = 2048
TN_DW = 512
TM_DH = 256
HALO = 16

OFF_Q, OFF_K, OFF_V, OFF_ZA, OFF_CB, OFF_CC, OFF_CU, OFF_CZ, OFF_GA, OFF_GB = (
    0, 512, 1024, 1536, 2048, 2560, 3072, 3584, 4096, 5120)


def _params(sem=None):
    return pltpu.CompilerParams(dimension_semantics=sem, vmem_limit_bytes=VMEM_LIMIT)


def _dot(a, b):
    return jnp.dot(a, b, preferred_element_type=F32)


def _dot_nt(a, b):
    return lax.dot_general(a, b, (((1,), (1,)), ((), ())), preferred_element_type=F32)


def _dot_tn(a, b):
    return lax.dot_general(a, b, (((0,), (0,)), ((), ())), preferred_element_type=F32)


def _sigmoid(x):
    return 1.0 / (1.0 + jnp.exp(-x))


def _lane_lo(shape):
    return lax.broadcasted_iota(jnp.int32, shape, len(shape) - 1) < HEAD_DIM


def _seg_sum(z, lo):
    a = jnp.sum(jnp.where(lo, z, 0.0), axis=-1, keepdims=True)
    b = jnp.sum(jnp.where(lo, 0.0, z), axis=-1, keepdims=True)
    return jnp.where(lo, a, b)


def _lane_col(z, lane):
    idx = lax.broadcasted_iota(jnp.int32, z.shape, 1)
    return jnp.sum(jnp.where(idx == lane, z, 0.0), axis=-1, keepdims=True)


def _sub_row(z, row):
    idx = lax.broadcasted_iota(jnp.int32, z.shape, 0)
    return jnp.sum(jnp.where(idx == row, z, 0.0), axis=0, keepdims=True)


def _mesh_pos():
    x, y, c = lax.axis_index("x"), lax.axis_index("y"), lax.axis_index("c")
    return x, y, c, 4 * x + 2 * y + c


def _peer(k, x, y, c):
    px = 1 - x if (k >> 2) & 1 else x
    py = 1 - y if (k >> 1) & 1 else y
    pc = 1 - c if k & 1 else c
    return (px, py, pc), 4 * px + 2 * py + pc


def _gather_copies(ins, outs, send_sems, recv_sems, local_sems):
    x, y, c, me = _mesh_pos()
    copies = []
    for a in range(len(ins)):
        copies.append(pltpu.make_async_copy(ins[a], outs[a].at[me], local_sems.at[a]))
        for k in range(1, N_DEV):
            dev, _ = _peer(k, x, y, c)
            copies.append(pltpu.make_async_remote_copy(
                src_ref=ins[a], dst_ref=outs[a].at[me],
                send_sem=send_sems.at[a * (N_DEV - 1) + k - 1], recv_sem=recv_sems.at[a * (N_DEV - 1) + k - 1],
                device_id=dev, device_id_type=pl.DeviceIdType.MESH))
    return copies


def _gather_sems(n):
    return [pltpu.SemaphoreType.DMA((n * (N_DEV - 1),)), pltpu.SemaphoreType.DMA((n * (N_DEV - 1),)),
            pltpu.SemaphoreType.DMA((n,))]


def _gather_direct(arrs, name):
    n = len(arrs)
    any_spec = pl.BlockSpec(memory_space=pl.ANY)

    def body(*refs):
        copies = _gather_copies(refs[:n], refs[n:2 * n], *refs[2 * n:])
        for cp in copies:
            cp.start()
        for cp in copies:
            cp.wait()

    return pl.pallas_call(
        body, name=name, out_shape=[jax.ShapeDtypeStruct((N_DEV,) + a.shape, a.dtype) for a in arrs],
        in_specs=[any_spec] * n, out_specs=[any_spec] * n, scratch_shapes=_gather_sems(n),
    )(*arrs)


def _ada_phase(c_ref, w_ref, call_ref, adag_ref, mine_ref, send_sems, recv_sems):
    x, y, c, me = _mesh_pos()

    def copy(phase, k, src, dst):
        dev, _ = _peer(k, x, y, c)
        return pltpu.make_async_remote_copy(
            src_ref=src, dst_ref=dst,
            send_sem=send_sems.at[phase * (N_DEV - 1) + k - 1],
            recv_sem=recv_sems.at[phase * (N_DEV - 1) + k - 1],
            device_id=dev, device_id_type=pl.DeviceIdType.MESH)

    call_ref[me] = c_ref[...]
    first = [copy(0, k, c_ref, call_ref.at[me]) for k in range(1, N_DEV)]
    for cp in first:
        cp.start()
    for cp in first:
        cp.wait()
    wb = w_ref[...].astype(BF16)
    for b in range(N_DEV):
        row = jnp.broadcast_to(call_ref[b], (8, D_MODEL)).astype(BF16)
        mine_ref[b] = _sub_row(_dot(row, wb), 0)
    adag_ref[me] = mine_ref[...]
    second = [copy(1, k, mine_ref, adag_ref.at[me]) for k in range(1, N_DEV)]
    for cp in second:
        cp.start()
    for cp in second:
        cp.wait()


def _gather_weights_and_ada(wt_shard, c_row, w_ada_sh):
    any_spec = pl.BlockSpec(memory_space=pl.ANY)
    vm = pl.BlockSpec(memory_space=pltpu.VMEM)

    def body(w_in_ref, c_ref, wada_ref, out_ref, call_ref, adag_ref, mine_ref, send_sems, recv_sems, local_sem,
             ada_send, ada_recv):
        x, y, c, me = _mesh_pos()
        sibling = (x, y, 1 - c)
        chips = [(1 - x, y), (x, 1 - y), (1 - x, 1 - y)]

        def copy(k, src, blk, to):
            return pltpu.make_async_remote_copy(
                src_ref=src, dst_ref=out_ref.at[blk], send_sem=send_sems.at[k], recv_sem=recv_sems.at[k],
                device_id=to, device_id_type=pl.DeviceIdType.MESH)

        local = pltpu.make_async_copy(w_in_ref, out_ref.at[me], local_sem.at[0])
        local.start()
        first = [copy(0, w_in_ref, me, sibling)]
        first += [copy(1 + j, w_in_ref, me, (px, py, c)) for j, (px, py) in enumerate(chips)]
        for cp in first:
            cp.start()
        _ada_phase(c_ref, wada_ref, call_ref, adag_ref, mine_ref, ada_send, ada_recv)
        passed = []
        for j, (px, py) in enumerate(chips):
            blk = 4 * px + 2 * py + c
            copy(1 + j, w_in_ref, blk, (x, y, c)).wait_recv()
            fwd = copy(4 + j, out_ref.at[blk], blk, sibling)
            fwd.start()
            passed.append(fwd)
        copy(0, w_in_ref, 4 * x + 2 * y + 1 - c, (x, y, c)).wait_recv()
        for j, (px, py) in enumerate(chips):
            copy(4 + j, w_in_ref, 4 * px + 2 * py + 1 - c, (x, y, c)).wait_recv()
        for cp in first + passed:
            cp.wait_send()
        local.wait()

    per = N_DEV - 1
    return pl.pallas_call(
        body, name="gather_weights",
        out_shape=[jax.ShapeDtypeStruct((N_DEV,) + wt_shard.shape, wt_shard.dtype),
                   jax.ShapeDtypeStruct((N_DEV, 1, D_MODEL), F32),
                   jax.ShapeDtypeStruct((N_DEV, N_DEV, 1, ADA_SHARD), F32)],
        in_specs=[any_spec, vm, vm], out_specs=[any_spec, vm, vm],
        scratch_shapes=[pltpu.VMEM((N_DEV, 1, ADA_SHARD), F32),
                        pltpu.SemaphoreType.DMA((per,)), pltpu.SemaphoreType.DMA((per,)),
                        pltpu.SemaphoreType.DMA((1,)),
                        pltpu.SemaphoreType.DMA((2 * per,)), pltpu.SemaphoreType.DMA((2 * per,))],
        compiler_params=pltpu.CompilerParams(vmem_limit_bytes=VMEM_LIMIT),
    )(wt_shard, c_row, w_ada_sh)


def _sibling_swap(arrs, name):
    n = len(arrs)
    any_spec = pl.BlockSpec(memory_space=pl.ANY)

    def body(*refs):
        ins, outs = refs[:n], refs[n:2 * n]
        send_sems, recv_sems = refs[2 * n:]
        x, y, c, _ = _mesh_pos()
        copies = [pltpu.make_async_remote_copy(
            src_ref=ins[a].at[1 - c], dst_ref=outs[a], send_sem=send_sems.at[a], recv_sem=recv_sems.at[a],
            device_id=(x, y, 1 - c), device_id_type=pl.DeviceIdType.MESH) for a in range(n)]
        for cp in copies:
            cp.start()
        for cp in copies:
            cp.wait()

    return pl.pallas_call(
        body, name=name,
        out_shape=[jax.ShapeDtypeStruct(a.shape[1:], a.dtype) for a in arrs],
        in_specs=[any_spec] * n, out_specs=[any_spec] * n,
        scratch_shapes=[pltpu.SemaphoreType.DMA((n,)), pltpu.SemaphoreType.DMA((n,))],
    )(*arrs)


def _pair_sum(mine2, theirs, core, name):
    _, _, rows, cols = mine2.shape
    tr = 256 if rows % 256 == 0 else rows

    def body(core_ref, a_ref, b_ref, out_ref):
        out_ref[...] = (a_ref[...].astype(F32) + b_ref[...].astype(F32)).astype(BF16)

    return pl.pallas_call(
        body, name=name,
        grid_spec=pltpu.PrefetchScalarGridSpec(
            num_scalar_prefetch=1, grid=(4, rows // tr),
            in_specs=[pl.BlockSpec((None, None, tr, cols), lambda ch, i, core_: (core_[0], ch, i, 0)),
                      pl.BlockSpec((None, tr, cols), lambda ch, i, core_: (ch, i, 0))],
            out_specs=pl.BlockSpec((None, tr, cols), lambda ch, i, core_: (ch, i, 0))),
        out_shape=jax.ShapeDtypeStruct(theirs.shape, BF16),
        compiler_params=_params(("parallel", "parallel")),
    )(core, mine2, theirs)


def _chip_copies(ins, outs, send_sems, recv_sems, local_sems):
    x, y, c, _ = _mesh_pos()
    my_chip = 2 * x + y
    chips = [(1 - x, y), (x, 1 - y), (1 - x, 1 - y)]
    copies = []
    for a in range(len(ins)):
        copies.append(pltpu.make_async_copy(ins[a].at[my_chip], outs[a].at[my_chip], local_sems.at[a]))
        for j, (px, py) in enumerate(chips):
            copies.append(pltpu.make_async_remote_copy(
                src_ref=ins[a].at[2 * px + py], dst_ref=outs[a].at[my_chip],
                send_sem=send_sems.at[a * 3 + j], recv_sem=recv_sems.at[a * 3 + j],
                device_id=(px, py, c), device_id_type=pl.DeviceIdType.MESH))
    return copies


def _proj_fwd(x, ada3, norm_g, w_all_t, bf_pad, qg, kg, later):
    s = x.shape[0]
    tm, tn = min(TM_PROJ, s), TN_PROJ
    nt = s // tm
    n = len(later)

    def body(x_ref, ada_ref, g_ref, wt_ref, bf_ref, qg_ref, kg_ref, *rest):
        ins, (proj_ref, fl_ref, h_ref), rows_refs, rest = rest[:n], rest[n:n + 3], rest[n + 3:n + 8], rest[n + 8:]
        outs, (carry, send_sems, recv_sems, local_sems) = rest[:n], rest[n:]
        i = pl.program_id(0)

        @pl.when(i == 0)
        def _():
            carry[...] = jnp.zeros_like(carry)
            for cp in _gather_copies(ins, outs, send_sems, recv_sems, local_sems):
                cp.start()

        xv = x_ref[...]
        r = lax.rsqrt(jnp.mean(xv * xv, axis=-1, keepdims=True) + EPS)
        hv = ((xv * r) * g_ref[...]) * (1.0 + ada_ref[1:2, :]) + ada_ref[0:1, :]
        hb = hv.astype(BF16)
        h_ref[...] = hb
        fl = _dot_nt(hb, wt_ref[N_MAIN:N_ALL, :])
        fl_ref[...] = fl
        for j in range(N_MAIN // tn):
            proj_ref[:, j * tn:(j + 1) * tn] = _dot_nt(hb, wt_ref[j * tn:(j + 1) * tn, :]).astype(BF16)
        _attention_rows(proj_ref, fl, bf_ref, qg_ref, kg_ref, carry, *rows_refs)

        @pl.when(i == nt - 1)
        def _():
            for cp in _gather_copies(ins, outs, send_sems, recv_sems, local_sems):
                cp.wait()

    any_spec = pl.BlockSpec(memory_space=pl.ANY)
    heads = pl.BlockSpec((HEADS, tm, LANES), lambda i: (0, i, 0))
    heads_t = pl.BlockSpec((HEADS, LANES, tm), lambda i: (0, 0, i))
    vec = pl.BlockSpec((1, ATTN_W), lambda i: (0, 0))
    res = pl.pallas_call(
        body, name="proj_fwd", grid=(nt,),
        in_specs=[pl.BlockSpec((tm, D_MODEL), lambda i: (i, 0)),
                  pl.BlockSpec((3, D_MODEL), lambda i: (0, 0)),
                  pl.BlockSpec((1, D_MODEL), lambda i: (0, 0)),
                  pl.BlockSpec((N_ALL, D_MODEL), lambda i: (0, 0)),
                  pl.BlockSpec((1, LANES), lambda i: (0, 0)), vec, vec] + [any_spec] * n,
        out_specs=[pl.BlockSpec((tm, N_MAIN), lambda i: (i, 0)),
                   pl.BlockSpec((tm, N_FPAD), lambda i: (i, 0)),
                   pl.BlockSpec((tm, D_MODEL), lambda i: (i, 0)),
                   heads, heads, heads, heads_t, heads_t] + [any_spec] * n,
        out_shape=[jax.ShapeDtypeStruct((s, N_MAIN), BF16),
                   jax.ShapeDtypeStruct((s, N_FPAD), F32),
                   jax.ShapeDtypeStruct((s, D_MODEL), BF16)]
        + [jax.ShapeDtypeStruct((HEADS, s, LANES), BF16)] * 3
        + [jax.ShapeDtypeStruct((HEADS, LANES, s), BF16)] * 2
        + [jax.ShapeDtypeStruct((N_DEV,) + a.shape, a.dtype) for a in later],
        scratch_shapes=[pltpu.VMEM((1, LANES), F32)] + _gather_sems(n),
        compiler_params=_params(("arbitrary",)),
    )(x, ada3, norm_g, w_all_t, bf_pad, qg, kg, *later)
    return res[:3], res[3:8], res[8:]


L_ONE_Q, L_F_Q, L_LSE_Q, L_END = HEAD_DIM, HEAD_DIM + 3, HEAD_DIM + 6, HEAD_DIM + 9


def _split3(f):
    hi = f.astype(BF16).astype(F32)
    r = f - hi
    mid = r.astype(BF16).astype(F32)
    return hi, mid, r - mid


def _place3(lane, first, parts, otherwise):
    a, b, c = parts
    return jnp.where(lane == first, a, jnp.where(lane == first + 1, b, jnp.where(lane == first + 2, c, otherwise)))


def _log_forget(fl, bf):
    z = fl + bf
    lf = jnp.minimum(z, 0.0) - jnp.log1p(jnp.exp(-jnp.abs(z)))
    lane = lax.broadcasted_iota(jnp.int32, z.shape, 1)
    return jnp.where(lane < HEADS, lf, 0.0)


def _attention_rows(p_ref, fl, bf_ref, qg_ref, kg_ref, carry, qa_ref, ka_ref, va_ref, kt_ref, vt_ref):
    tm = fl.shape[0]
    scale = HEAD_DIM ** -0.5
    tri = (lax.broadcasted_iota(jnp.int32, (tm, tm), 1) <= lax.broadcasted_iota(jnp.int32, (tm, tm), 0)).astype(F32)
    cum_v = jnp.dot(tri, _log_forget(fl, bf_ref[...]), preferred_element_type=F32,
                    precision=lax.Precision.HIGHEST) + carry[...]
    carry[...] = _sub_row(cum_v, tm - 1)
    lane = lax.broadcasted_iota(jnp.int32, (tm, LANES), 1)
    lo = lane < HEAD_DIM
    v_tail = jnp.where(lane < L_F_Q, 1.0, 0.0)
    for pr in range(ATTN_W // LANES):
        sl = slice(pr * LANES, (pr + 1) * LANES)
        q2 = p_ref[:, OFF_Q + pr * LANES:OFF_Q + (pr + 1) * LANES].astype(F32)
        k2 = p_ref[:, OFF_K + pr * LANES:OFF_K + (pr + 1) * LANES].astype(F32)
        v2 = p_ref[:, OFF_V + pr * LANES:OFF_V + (pr + 1) * LANES].astype(F32)
        rq = lax.rsqrt(_seg_sum(q2 * q2, lo) * (1.0 / HEAD_DIM) + EPS)
        rk = lax.rsqrt(_seg_sum(k2 * k2, lo) * (1.0 / HEAD_DIM) + EPS)
        qn = ((q2 * rq) * qg_ref[:, sl]) * scale
        kn = (k2 * rk) * kg_ref[:, sl]
        for hh in range(2):
            h = 2 * pr + hh
            f3 = _split3(_lane_col(cum_v, h))
            qh = qn if hh == 0 else pltpu.roll(qn, HEAD_DIM, 1)
            kh = kn if hh == 0 else pltpu.roll(kn, HEAD_DIM, 1)
            vh = v2 if hh == 0 else pltpu.roll(v2, HEAD_DIM, 1)
            q_tail = jnp.where(lane < L_F_Q, 1.0, _place3(lane, L_F_Q, f3, 0.0))
            k_tail = _place3(lane, L_ONE_Q, tuple(-f for f in f3), jnp.where(lane < L_END, 1.0, 0.0))
            k_row = jnp.where(lo, kh, k_tail)
            v_row = jnp.where(lo, vh, v_tail)
            qa_ref[h] = jnp.where(lo, qh, q_tail).astype(BF16)
            ka_ref[h] = k_row.astype(BF16)
            va_ref[h] = v_row.astype(BF16)
            kt_ref[h] = k_row.T.astype(BF16)
            vt_ref[h] = v_row.T.astype(BF16)


def _causal_t(t):
    return lax.broadcasted_iota(jnp.int32, (t, t), 0) <= lax.broadcasted_iota(jnp.int32, (t, t), 1)


def _tri_steps(nt, q_major):
    if q_major:
        pairs = [(i, j) for i in range(nt) for j in range(i + 1)]
    else:
        pairs = [(i, j) for j in range(nt) for i in range(j, nt)]
    return (jnp.asarray(np.array([p[0] for p in pairs], np.int32)),
            jnp.asarray(np.array([p[1] for p in pairs], np.int32)))


def _attn_fwd(qa, ka, vt, proj):
    s = qa.shape[1]
    t = min(TQ, s)
    it, jt = _tri_steps(s // t, True)
    hp = HEADS_PER_STEP_FWD
    wide = hp * HEAD_DIM
    za_blk = OFF_ZA // wide

    def body(it_ref, jt_ref, q_ref, k_ref, vt_ref, za_ref, attn_ref, oa_ref, qb_ref, m_s, acc_s, pair_s):
        step = pl.program_id(1)
        i, j = it_ref[step], jt_ref[step]

        @pl.when(j == 0)
        def _():
            m_s[...] = jnp.full_like(m_s, NEG)
            acc_s[...] = jnp.zeros_like(acc_s)

        def update(masked):
            for hh in range(hp):
                st = _dot_nt(k_ref[hh], q_ref[hh])
                if masked:
                    st = jnp.where(_causal_t(t), st, NEG)
                m_prev = m_s[hh]
                m_next = jnp.maximum(m_prev, jnp.max(st, axis=0, keepdims=True))
                alpha = jnp.exp(m_prev - m_next)
                pt = jnp.exp(st - m_next).astype(BF16)
                acc_s[hh] = acc_s[hh] * alpha + _dot(vt_ref[hh], pt)
                m_s[hh] = m_next

        @pl.when(j < i)
        def _():
            update(False)

        @pl.when(j == i)
        def _():
            update(True)
            row = lax.broadcasted_iota(jnp.int32, (LANES, t), 0)
            lane = lax.broadcasted_iota(jnp.int32, (t, LANES), 1)
            for hh in range(hp):
                l_row = acc_s[hh, L_ONE_Q:L_ONE_Q + 1, :]
                pair_s[hh * HEAD_DIM:(hh + 1) * HEAD_DIM, :] = acc_s[hh, 0:HEAD_DIM, :] / l_row
                lse3 = _split3(m_s[hh] + jnp.log(l_row))
                tail_t = _place3(row, L_LSE_Q, tuple(-x for x in lse3), 0.0)
                keep_q = jnp.logical_or(lane < L_LSE_Q, lane >= L_END)
                qb_ref[hh] = jnp.where(keep_q, q_ref[hh].astype(F32), tail_t.T).astype(BF16)
            out = pair_s[...].T
            attn_ref[...] = out
            z = za_ref[...].astype(F32)
            oa_ref[...] = (out * (z * _sigmoid(z))).astype(BF16)

    pair_q = pl.BlockSpec((hp, t, LANES), lambda p, n, it_, jt_: (p, it_[n], 0))
    pair_k = pl.BlockSpec((hp, t, LANES), lambda p, n, it_, jt_: (p, jt_[n], 0))
    pair_kt = pl.BlockSpec((hp, LANES, t), lambda p, n, it_, jt_: (p, 0, jt_[n]))
    out_q = pl.BlockSpec((t, wide), lambda p, n, it_, jt_: (it_[n], p))
    return pl.pallas_call(
        body, name="attn_fwd",
        grid_spec=pltpu.PrefetchScalarGridSpec(
            num_scalar_prefetch=2, grid=(HEADS // hp, it.shape[0]),
            in_specs=[pair_q, pair_k, pair_kt,
                      pl.BlockSpec((t, wide), lambda p, n, it_, jt_: (it_[n], za_blk + p))],
            out_specs=[out_q, out_q, pair_q],
            scratch_shapes=[pltpu.VMEM((hp, 1, t), F32), pltpu.VMEM((hp, LANES, t), F32),
                            pltpu.VMEM((wide, t), F32)]),
        out_shape=[jax.ShapeDtypeStruct((s, ATTN_W), F32),
                   jax.ShapeDtypeStruct((s, ATTN_W), BF16),
                   jax.ShapeDtypeStruct((HEADS, s, LANES), BF16)],
        compiler_params=_params(("parallel", "arbitrary")),
    )(it, jt, qa, ka, vt, proj)


def _conv_parts(gb_ref, gc_ref, u_ref, zb_ref, gch_ref, uh_ref, first, w_ref, tm):
    gb, gc = gb_ref[...].astype(F32), gc_ref[...].astype(F32)
    u, zb = u_ref[...].astype(F32), zb_ref[...].astype(F32)
    cu = gc * u
    cu_h = jnp.where(first, 0.0, gch_ref[...].astype(F32) * uh_ref[...].astype(F32))
    prev1, prev2 = _sub_row(cu_h, HALO - 1), _sub_row(cu_h, HALO - 2)
    row = lax.broadcasted_iota(jnp.int32, cu.shape, 0)
    r1 = jnp.where(row == 0, prev1, pltpu.roll(cu, 1, 0))
    r2 = jnp.where(row == 0, prev2, jnp.where(row == 1, prev1, pltpu.roll(cu, 2, 0)))
    conv = w_ref[2:3, :] * cu + w_ref[1:2, :] * r1 + w_ref[0:1, :] * r2
    return gb, gc, u, zb, cu, r1, r2, conv


def _conv_specs(tm, s, width=LANES):
    def tile(off):
        return pl.BlockSpec((tm, width), lambda c, i: (i, off // width + c))

    def before(off):
        return pl.BlockSpec((HALO, width), lambda c, i: (jnp.maximum(i * (tm // HALO) - 1, 0), off // width + c))

    def after(off):
        return pl.BlockSpec((HALO, width),
                            lambda c, i: (jnp.minimum((i + 1) * (tm // HALO), s // HALO - 1), off // width + c))

    return ([tile(OFF_CB), tile(OFF_CC), tile(OFF_CU), tile(OFF_CZ)], [before(OFF_CC), before(OFF_CU)],
            [after(OFF_CB), after(OFF_CZ)])


def _tail(oa, attn, proj, x, target, ada3, wa, wb, wo, conv_w):
    s = x.shape[0]
    tm = min(TM_TAIL, s)
    gab_blk = OFF_GA // (2 * D_MODEL)
    za_blk = OFF_ZA // ATTN_W
    tiles, befores, _ = _conv_specs(tm, s, CONV_W)

    def body(oa_ref, attn_ref, za_ref, gb_ref, gc_ref, u_ref, zb_ref, gch_ref, uh_ref, cw_ref, gab_ref, x_ref, t_ref,
             ada_ref, wa_ref, wb_ref, wo_ref,
             dy_ref, dgab_ref, do_ref, dza_ref, dob_ref, dwo_ref, dwa_ref, dwb_ref, dgate_ref, loss_ref):
        first = pl.program_id(0) == 0

        @pl.when(first)
        def _():
            dwo_ref[...] = jnp.zeros_like(dwo_ref)
            dwa_ref[...] = jnp.zeros_like(dwa_ref)
            dwb_ref[...] = jnp.zeros_like(dwb_ref)
            dgate_ref[...] = jnp.zeros_like(dgate_ref)
            loss_ref[...] = jnp.zeros_like(loss_ref)

        gb, _, _, zb, _, _, _, conv = _conv_parts(gb_ref, gc_ref, u_ref, zb_ref, gch_ref, uh_ref, first, cw_ref, tm)
        ob_v = (gb * conv * (zb * _sigmoid(zb))).astype(BF16)
        oa_v = oa_ref[...]
        wa_v, wb_v, wo_v = wa_ref[...], wb_ref[...], wo_ref[...]
        a2 = _dot(oa_v, wa_v)
        b2 = _dot(ob_v, wb_v)
        sa = _sigmoid(gab_ref[:, 0:D_MODEL].astype(F32))
        sb = _sigmoid(gab_ref[:, D_MODEL:2 * D_MODEL].astype(F32))
        mb = (sa * a2 + sb * b2).astype(BF16)
        mo = _dot(mb, wo_v)
        gate = ada_ref[2:3, :]
        err = (x_ref[...] + gate * mo) - t_ref[...]
        dy = err * (1.0 / D_MODEL)
        dy_ref[...] = dy
        loss_ref[...] += 0.5 * jnp.sum(err * err) * (1.0 / D_MODEL)
        dgate_ref[...] += jnp.sum(dy * mo, axis=0, keepdims=True)
        dmo = (dy * gate).astype(BF16)
        dmerged = _dot_nt(dmo, wo_v)
        dwo_ref[...] += _dot_tn(mb, dmo)
        da2 = (dmerged * sa).astype(BF16)
        db2 = (dmerged * sb).astype(BF16)
        dgab_ref[:, 0:D_MODEL] = (dmerged * a2 * (sa * (1.0 - sa))).astype(BF16)
        dgab_ref[:, D_MODEL:2 * D_MODEL] = (dmerged * b2 * (sb * (1.0 - sb))).astype(BF16)
        doa = _dot_nt(da2, wa_v)
        dob_ref[...] = _dot_nt(db2, wb_v)
        dwa_ref[...] += _dot_tn(oa_v, da2)
        dwb_ref[...] += _dot_tn(ob_v, db2)

        lane = lax.broadcasted_iota(jnp.int32, (tm, LANES), 1)
        lo = lane < HEAD_DIM
        for pr in range(ATTN_W // LANES):
            sl = slice(pr * LANES, (pr + 1) * LANES)
            g, a, z = doa[:, sl], attn_ref[:, sl], za_ref[:, sl].astype(F32)
            sg = _sigmoid(z)
            dat = (g * (z * sg)).astype(BF16).astype(F32)
            prod = dat * a
            dza_ref[:, sl] = (g * a * (sg * (1.0 + z * (1.0 - sg)))).astype(BF16)
            for hh in range(2):
                sel = lo if hh == 0 else jnp.logical_not(lo)
                delta3 = _split3(jnp.sum(jnp.where(sel, prod, 0.0), axis=-1, keepdims=True))
                dh = dat if hh == 0 else pltpu.roll(dat, HEAD_DIM, 1)
                tail_lanes = _place3(lane, L_ONE_Q, tuple(-d for d in delta3), 0.0)
                do_ref[2 * pr + hh] = jnp.where(lo, dh, tail_lanes).astype(BF16)

    half = pl.BlockSpec((tm, ATTN_W), lambda i: (i, 0))
    full = pl.BlockSpec((tm, D_MODEL), lambda i: (i, 0))

    def const(shape):
        return pl.BlockSpec(shape, lambda i: (0, 0))

    def one_axis(spec):
        return pl.BlockSpec(spec.block_shape, lambda i, f=spec.index_map: f(0, i))

    return pl.pallas_call(
        body, name="tail", grid=(s // tm,),
        in_specs=[half, half, pl.BlockSpec((tm, ATTN_W), lambda i: (i, za_blk))]
        + [one_axis(sp) for sp in tiles + befores]
        + [const((3, CONV_W)), pl.BlockSpec((tm, 2 * D_MODEL), lambda i: (i, gab_blk)), full, full,
           const((3, D_MODEL)), const((ATTN_W, D_MODEL)), const((CONV_W, D_MODEL)), const((D_MODEL, D_MODEL))],
        out_specs=[full, pl.BlockSpec((tm, 2 * D_MODEL), lambda i: (i, 0)),
                   pl.BlockSpec((HEADS, tm, LANES), lambda i: (0, i, 0)), half, half,
                   const((D_MODEL, D_MODEL)), const((ATTN_W, D_MODEL)), const((CONV_W, D_MODEL)),
                   const((1, D_MODEL)), const((1, LANES))],
        out_shape=[jax.ShapeDtypeStruct((s, D_MODEL), F32),
                   jax.ShapeDtypeStruct((s, 2 * D_MODEL), BF16),
                   jax.ShapeDtypeStruct((HEADS, s, LANES), BF16),
                   jax.ShapeDtypeStruct((s, ATTN_W), BF16),
                   jax.ShapeDtypeStruct((s, CONV_W), F32),
                   jax.ShapeDtypeStruct((D_MODEL, D_MODEL), F32),
                   jax.ShapeDtypeStruct((ATTN_W, D_MODEL), F32),
                   jax.ShapeDtypeStruct((CONV_W, D_MODEL), F32),
                   jax.ShapeDtypeStruct((1, D_MODEL), F32),
                   jax.ShapeDtypeStruct((1, LANES), F32)],
        compiler_params=_params(("arbitrary",)),
    )(oa, attn, proj, *([proj] * 6), conv_w, proj, x, target, ada3, wa, wb, wo)


def _attn_bwd(qb, ka, kt, va, do, proj, qg, kg):
    s = qb.shape[1]
    t = min(TQ, s)
    nt = s // t
    hp = HEADS_PER_STEP
    wide = hp * HEAD_DIM
    scale = HEAD_DIM ** -0.5
    it, jt = _tri_steps(nt, False)

    def body(it_ref, jt_ref, q_ref, k_ref, kt_ref, v_ref, do_ref, qraw_ref, kraw_ref, qg_ref, kg_ref,
             dq_ref, dk_ref, dv_ref, dqg_ref, dkg_ref, dcum_ref, dqt_s, dk_s, dv_s, rows_s):
        grp, step = pl.program_id(0), pl.program_id(1)
        i, j = it_ref[step], jt_ref[step]
        lane = lax.broadcasted_iota(jnp.int32, (t, LANES), 1)
        lo = lane < HEAD_DIM

        @pl.when(step == 0)
        def _():
            dqt_s[...] = jnp.zeros_like(dqt_s)
            dqg_ref[...] = jnp.zeros_like(dqg_ref)
            dkg_ref[...] = jnp.zeros_like(dkg_ref)

        @pl.when(i == j)
        def _():
            dk_s[...] = jnp.zeros_like(dk_s)
            dv_s[...] = jnp.zeros_like(dv_s)

        def update(masked):
            for hh in range(hp):
                qh, doh = q_ref[hh], do_ref[hh]
                st = _dot_nt(k_ref[hh], qh)
                if masked:
                    st = jnp.where(_causal_t(t), st, NEG)
                pt = jnp.exp(st)
                dst = (pt * _dot_nt(v_ref[hh], doh)).astype(BF16)
                dv_s[hh] += _dot(pt.astype(BF16), doh)
                dk_s[hh] += _dot(dst, qh)
                dqt_s[hh, i] += _dot(kt_ref[hh], dst)

        def pair(a, b):
            return jnp.where(lo, a, pltpu.roll(b, HEAD_DIM, 1))

        def norm_bwd(raw, dy, g, dg_ref, out_ref, sl):
            r = lax.rsqrt(_seg_sum(raw * raw, lo) * (1.0 / HEAD_DIM) + EPS)
            xhat = raw * r
            dg_ref[:, sl] += jnp.sum(dy * xhat, axis=0, keepdims=True)
            dxh = dy * g
            dx = r * (dxh - xhat * (_seg_sum(dxh * xhat, lo) * (1.0 / HEAD_DIM)))
            out_ref[:, sl] = dx.astype(BF16)

        @pl.when(i > j)
        def _():
            update(False)

        @pl.when(i == j)
        def _():
            update(True)
            dq_rows = [dqt_s[hh, i].T for hh in range(hp)]
            rows = jnp.zeros((t, LANES), F32)
            for hh in range(hp):
                rows = jnp.where(lane == grp * hp + hh, _lane_col(dq_rows[hh], L_F_Q), rows)
            rows_s[...] = rows
            for pr in range(hp // 2):
                sl = slice(pr * LANES, (pr + 1) * LANES)
                norm_bwd(qraw_ref[:, sl].astype(F32), pair(dq_rows[2 * pr], dq_rows[2 * pr + 1]) * scale,
                         qg_ref[:, sl], dqg_ref, dq_ref, sl)

        @pl.when(i == nt - 1)
        def _():
            dcum = rows_s[...]
            for hh in range(hp):
                dcum = jnp.where(lane == grp * hp + hh, dcum - _lane_col(dk_s[hh], L_ONE_Q), dcum)
            dcum_ref[0] = dcum
            for pr in range(hp // 2):
                sl = slice(pr * LANES, (pr + 1) * LANES)
                norm_bwd(kraw_ref[:, sl].astype(F32), pair(dk_s[2 * pr], dk_s[2 * pr + 1]),
                         kg_ref[:, sl], dkg_ref, dk_ref, sl)
                dv_ref[:, sl] = pair(dv_s[2 * pr], dv_s[2 * pr + 1]).astype(BF16)

    pair_q = pl.BlockSpec((hp, t, LANES), lambda p, n, it_, jt_: (p, it_[n], 0))
    pair_k = pl.BlockSpec((hp, t, LANES), lambda p, n, it_, jt_: (p, jt_[n], 0))
    pair_kt = pl.BlockSpec((hp, LANES, t), lambda p, n, it_, jt_: (p, 0, jt_[n]))
    tok = pl.BlockSpec((t, wide), lambda p, n, it_, jt_: (jt_[n], p))
    gain = pl.BlockSpec((1, wide), lambda p, n, it_, jt_: (0, p))
    return pl.pallas_call(
        body, name="attn_bwd",
        grid_spec=pltpu.PrefetchScalarGridSpec(
            num_scalar_prefetch=2, grid=(HEADS // hp, it.shape[0]),
            in_specs=[pair_q, pair_k, pair_kt, pair_k, pair_q,
                      pl.BlockSpec((t, wide), lambda p, n, it_, jt_: (jt_[n], OFF_Q // wide + p)),
                      pl.BlockSpec((t, wide), lambda p, n, it_, jt_: (jt_[n], OFF_K // wide + p)), gain, gain],
            out_specs=[tok, tok, tok, gain, gain,
                       pl.BlockSpec((1, t, LANES), lambda p, n, it_, jt_: (p, jt_[n], 0))],
            scratch_shapes=[pltpu.VMEM((hp, nt, LANES, t), F32), pltpu.VMEM((hp, t, LANES), F32),
                            pltpu.VMEM((hp, t, LANES), F32), pltpu.VMEM((t, LANES), F32)]),
        out_shape=[jax.ShapeDtypeStruct((s, ATTN_W), BF16)] * 3
        + [jax.ShapeDtypeStruct((1, ATTN_W), F32)] * 2
        + [jax.ShapeDtypeStruct((HEADS // hp, s, LANES), F32)],
        compiler_params=_params(("parallel", "arbitrary")),
    )(it, jt, qb, ka, kt, va, do, proj, proj, qg, kg)


def _forget_bwd(dcum, fl, bf_pad):
    s = fl.shape[0]
    tc = min(TC_CUM, s)
    n = s // tc

    def body(dc_ref, fl_ref, bf_ref, df_ref, dbf_ref, carry):
        @pl.when(pl.program_id(0) == 0)
        def _():
            carry[...] = jnp.zeros_like(carry)
            dbf_ref[...] = jnp.zeros_like(dbf_ref)
        r = lax.broadcasted_iota(jnp.int32, (tc, tc), 0)
        cidx = lax.broadcasted_iota(jnp.int32, (tc, tc), 1)
        tri = (cidx >= r).astype(F32)
        dc = dc_ref[0]
        for grp in range(1, dcum.shape[0]):
            dc = dc + dc_ref[grp]
        dlf = jnp.dot(tri, dc, preferred_element_type=F32, precision=lax.Precision.HIGHEST) + carry[...]
        carry[...] += jnp.sum(dc, axis=0, keepdims=True)
        lane = lax.broadcasted_iota(jnp.int32, (tc, LANES), 1)
        dfl = jnp.where(lane < HEADS, dlf * _sigmoid(-(fl_ref[...] + bf_ref[...])), 0.0)
        df_ref[...] = dfl.astype(BF16)
        dbf_ref[...] += jnp.sum(dfl, axis=0, keepdims=True)

    rev = pl.BlockSpec((tc, LANES), lambda i: (n - 1 - i, 0))
    vec = pl.BlockSpec((1, LANES), lambda i: (0, 0))
    return pl.pallas_call(
        body, name="forget_bwd", grid=(n,),
        in_specs=[pl.BlockSpec((dcum.shape[0], tc, LANES), lambda i: (0, n - 1 - i, 0)), rev, vec],
        out_specs=[rev, vec],
        out_shape=[jax.ShapeDtypeStruct((s, LANES), BF16), jax.ShapeDtypeStruct((1, LANES), F32)],
        scratch_shapes=[pltpu.VMEM((1, LANES), F32)],
        compiler_params=_params(("arbitrary",)),
    )(dcum, fl, bf_pad)


def _conv_bwd(dob, proj, conv_w):
    s = dob.shape[0]
    tm = min(TM_ELEM, s)
    tiles, befores, afters = _conv_specs(tm, s)

    def body(dob_ref, dnext_ref, gb_ref, gc_ref, u_ref, zb_ref, gch_ref, uh_ref, gbn_ref, zbn_ref, w_ref,
             dgb_ref, dgc_ref, du_ref, dzb_ref, dw_ref):
        i = pl.program_id(1)

        @pl.when(i == 0)
        def _():
            dw_ref[...] = jnp.zeros_like(dw_ref)
        gb, gc, u, zb, cu, r1, r2, conv = _conv_parts(gb_ref, gc_ref, u_ref, zb_ref, gch_ref, uh_ref, i == 0, w_ref, tm)
        g = dob_ref[...]
        sg = _sigmoid(zb)
        sz = zb * sg
        dconv = g * gb * sz
        zn = zbn_ref[0:8, :].astype(F32)
        dcn = jnp.where(i == pl.num_programs(1) - 1, 0.0,
                        dnext_ref[...] * gbn_ref[0:8, :].astype(F32) * (zn * _sigmoid(zn)))
        nxt1, nxt2 = _sub_row(dcn, 0), _sub_row(dcn, 1)
        row = lax.broadcasted_iota(jnp.int32, (tm, LANES), 0)
        f1 = jnp.where(row == tm - 1, nxt1, pltpu.roll(dconv, tm - 1, 0))
        f2 = jnp.where(row == tm - 2, nxt1, jnp.where(row == tm - 1, nxt2, pltpu.roll(dconv, tm - 2, 0)))
        dcu = w_ref[2:3, :] * dconv + w_ref[1:2, :] * f1 + w_ref[0:1, :] * f2
        dgb_ref[...] = (g * conv * sz).astype(BF16)
        dgc_ref[...] = (dcu * u).astype(BF16)
        du_ref[...] = (dcu * gc).astype(BF16)
        dzb_ref[...] = (g * gb * conv * (sg * (1.0 + zb * (1.0 - sg)))).astype(BF16)
        w_row = lax.broadcasted_iota(jnp.int32, (3, LANES), 0)
        dw0 = jnp.sum(dconv * r2, axis=0, keepdims=True)
        dw1 = jnp.sum(dconv * r1, axis=0, keepdims=True)
        dw2 = jnp.sum(dconv * cu, axis=0, keepdims=True)
        dw_ref[...] += jnp.where(w_row == 0, dw0, jnp.where(w_row == 1, dw1, dw2))

    blk = pl.BlockSpec((tm, LANES), lambda c, i: (i, c))
    nxt = pl.BlockSpec((8, LANES), lambda c, i: (jnp.minimum((i + 1) * (tm // 8), s // 8 - 1), c))
    wspec = pl.BlockSpec((3, LANES), lambda c, i: (0, c))
    return pl.pallas_call(
        body, name="conv_bwd", grid=(CONV_W // LANES, s // tm),
        in_specs=[blk, nxt] + tiles + befores + afters + [wspec],
        out_specs=[blk, blk, blk, blk, wspec],
        out_shape=[jax.ShapeDtypeStruct((s, CONV_W), BF16)] * 4 + [jax.ShapeDtypeStruct((3, CONV_W), F32)],
        compiler_params=_params(("parallel", "arbitrary")),
    )(dob, dob, *([proj] * 8), conv_w)


def _piece_layout(pieces):
    offs, off = [], 0
    for p in pieces:
        offs.append((off, p.shape[1]))
        off += p.shape[1]
    assert off == N_ALL, off
    return offs


def _dw_in(h, pieces, chip_sums):
    s = h.shape[0]
    tk, tn = min(TK_DW, s), TN_DW
    nk = s // tk
    nn = N_MAIN // tn
    main, fpiece = pieces[:-1], pieces[-1]
    layout = _piece_layout(pieces)[:-1]
    n_main = len(main)
    nx = len(chip_sums)

    def body(*refs):
        p_refs, f_ref, h_ref = refs[:n_main], refs[n_main], refs[n_main + 1]
        ins, refs = refs[n_main + 2:n_main + 2 + nx], refs[n_main + 2 + nx:]
        out_ref, outf_ref = refs[:2]
        outs, (acc, accf, send_sems, recv_sems, local_sems) = refs[2:2 + nx], refs[2 + nx:]
        n, k = pl.program_id(0), pl.program_id(1)

        @pl.when(jnp.logical_and(n == 0, k == 0))
        def _():
            for cp in _chip_copies(ins, outs, send_sems, recv_sems, local_sems):
                cp.start()

        @pl.when(k == 0)
        def _():
            acc[...] = jnp.zeros_like(acc)
        hv = h_ref[pl.ds(pl.multiple_of(k * tk, tk), tk), :]
        for p_ref, (off, width) in zip(p_refs, layout):
            @pl.when(jnp.logical_and(n >= off // tn, n < (off + width) // tn))
            def _():
                acc[...] += _dot_tn(p_ref[...], hv)

        @pl.when(k == nk - 1)
        def _():
            out_ref[...] = acc[...].astype(BF16)

        @pl.when(n == 0)
        def _():
            @pl.when(k == 0)
            def _():
                accf[...] = jnp.zeros_like(accf)
            accf[...] += _dot_tn(f_ref[...], hv)

            @pl.when(k == nk - 1)
            def _():
                outf_ref[...] = accf[...].astype(BF16)

        @pl.when(jnp.logical_and(n == nn - 1, k == nk - 1))
        def _():
            for cp in _chip_copies(ins, outs, send_sems, recv_sems, local_sems):
                cp.wait()

    def piece_spec(off, width):
        lo, hi = off // tn, (off + width) // tn

        def index(n, k):
            active = jnp.logical_and(n >= lo, n < hi)
            return jnp.where(active, k, 0), jnp.clip(n - lo, 0, hi - lo - 1)
        return pl.BlockSpec((tk, tn), index)

    any_spec = pl.BlockSpec(memory_space=pl.ANY)
    res = pl.pallas_call(
        body, name="dw_in", grid=(nn, nk),
        in_specs=[piece_spec(off, width) for off, width in layout]
        + [pl.BlockSpec((tk, N_FPAD), lambda n, k: (jnp.where(n == 0, k, 0), 0)),
           pl.BlockSpec((s, D_MODEL), lambda n, k: (0, 0))] + [any_spec] * nx,
        out_specs=[pl.BlockSpec((tn, D_MODEL), lambda n, k: (n, 0)),
                   pl.BlockSpec((N_FPAD, D_MODEL), lambda n, k: (0, 0))] + [any_spec] * nx,
        out_shape=[jax.ShapeDtypeStruct((N_MAIN, D_MODEL), BF16), jax.ShapeDtypeStruct((N_FPAD, D_MODEL), BF16)]
        + [jax.ShapeDtypeStruct(a.shape, a.dtype) for a in chip_sums],
        scratch_shapes=[pltpu.VMEM((tn, D_MODEL), F32), pltpu.VMEM((N_FPAD, D_MODEL), F32),
                        pltpu.SemaphoreType.DMA((nx * 3,)), pltpu.SemaphoreType.DMA((nx * 3,)),
                        pltpu.SemaphoreType.DMA((nx,))],
        compiler_params=_params(("arbitrary", "arbitrary")),
    )(*main, fpiece, h, *chip_sums)
    return res[:2], res[2:]


def _dh_and_dx(pieces, w_all_t, x, dy, ada3, norm_g, chip_sums):
    s = x.shape[0]
    tm = min(TM_DH, s)
    nt = s // tm
    n = len(chip_sums)
    npc = len(pieces)
    layout = _piece_layout(pieces)

    def body(*refs):
        p_refs, refs = refs[:npc], refs[npc:]
        wt_ref, x_ref, dy_ref, ada_ref, g_ref = refs[:5]
        ins, refs = refs[5:5 + n], refs[5 + n:]
        gx_ref, dsh_ref, dsc_ref, dg_ref = refs[:4]
        outs, (send_sems, recv_sems, local_sems) = refs[4:4 + n], refs[4 + n:]
        i = pl.program_id(0)

        @pl.when(i == 0)
        def _():
            for cp in _chip_copies(ins, outs, send_sems, recv_sems, local_sems):
                cp.start()
            dsh_ref[...] = jnp.zeros_like(dsh_ref)
            dsc_ref[...] = jnp.zeros_like(dsc_ref)
            dg_ref[...] = jnp.zeros_like(dg_ref)

        dh = None
        for p_ref, (off, width) in zip(p_refs, layout):
            part = _dot(p_ref[...], wt_ref[off:off + width, :])
            dh = part if dh is None else dh + part
        xv = x_ref[...]
        r = lax.rsqrt(jnp.mean(xv * xv, axis=-1, keepdims=True) + EPS)
        xhat = xv * r
        g = g_ref[...]
        one_sc = 1.0 + ada_ref[1:2, :]
        dsh_ref[...] += jnp.sum(dh, axis=0, keepdims=True)
        dsc_ref[...] += jnp.sum(dh * (xhat * g), axis=0, keepdims=True)
        dg_ref[...] += jnp.sum(dh * xhat, axis=0, keepdims=True) * one_sc
        dxh = dh * (g * one_sc)
        dx = r * (dxh - xhat * jnp.mean(dxh * xhat, axis=-1, keepdims=True))
        gx_ref[...] = dy_ref[...] + dx

        @pl.when(i == nt - 1)
        def _():
            for cp in _chip_copies(ins, outs, send_sems, recv_sems, local_sems):
                cp.wait()

    full = pl.BlockSpec((tm, D_MODEL), lambda i: (i, 0))
    vec = pl.BlockSpec((1, D_MODEL), lambda i: (0, 0))
    any_spec = pl.BlockSpec(memory_space=pl.ANY)
    res = pl.pallas_call(
        body, name="dh_dx", grid=(nt,),
        in_specs=[pl.BlockSpec((tm, p.shape[1]), lambda i: (i, 0)) for p in pieces]
        + [pl.BlockSpec((N_ALL, D_MODEL), lambda i: (0, 0)), full, full,
           pl.BlockSpec((3, D_MODEL), lambda i: (0, 0)), vec] + [any_spec] * n,
        out_specs=[full, vec, vec, vec] + [any_spec] * n,
        out_shape=[jax.ShapeDtypeStruct((s, D_MODEL), F32)] + [jax.ShapeDtypeStruct((1, D_MODEL), F32)] * 3
        + [jax.ShapeDtypeStruct(a.shape, a.dtype) for a in chip_sums],
        scratch_shapes=[pltpu.SemaphoreType.DMA((n * 3,)), pltpu.SemaphoreType.DMA((n * 3,)),
                        pltpu.SemaphoreType.DMA((n,))],
        compiler_params=_params(("arbitrary",)),
    )(*pieces, w_all_t, x, dy, ada3, norm_g, *chip_sums)
    return res[:4], res[4:]


def _sum_small(vec_all, qg_parts, kg_parts):
    def body(v_ref, q_ref, k_ref, tot_ref, gq_ref, gk_ref):
        tot = v_ref[0:1, :]
        for p in range(1, N_DEV):
            tot = tot + v_ref[p:p + 1, :]
        tot_ref[...] = tot
        gq_ref[...] = jnp.sum(q_ref[...], axis=0, keepdims=True)
        gk_ref[...] = jnp.sum(k_ref[...], axis=0, keepdims=True)

    n = vec_all.shape[-1]
    return pl.pallas_call(
        body, name="sum_small",
        out_shape=[jax.ShapeDtypeStruct((1, n), F32),
                   jax.ShapeDtypeStruct((1, HEAD_DIM), F32), jax.ShapeDtypeStruct((1, HEAD_DIM), F32)],
        compiler_params=_params(),
    )(vec_all, qg_parts, kg_parts)


def _grad_w_ada(c_cols, dada_rows):
    def body(c_ref, d_ref, out_ref):
        acc = c_ref[0] * d_ref[0]
        for b in range(1, N_DEV):
            acc = acc + c_ref[b] * d_ref[b]
        out_ref[...] = acc

    return pl.pallas_call(
        body, name="grad_w_ada",
        out_shape=jax.ShapeDtypeStruct((D_MODEL, ADA_SHARD), F32),
        compiler_params=_params(),
    )(c_cols, dada_rows)


def _adamw(w, m, v, g_parts, name):
    rows, cols = w.shape
    n_parts = g_parts.shape[0]
    tr = 256 if rows % 256 == 0 else rows
    tc = 256 if (tr == rows and rows > 256 and cols % 256 == 0) else cols
    c1 = 1.0 / (1.0 - ADAM_B1 ** ADAM_STEP)
    c2 = 1.0 / (1.0 - ADAM_B2 ** ADAM_STEP)

    def body(w_ref, m_ref, v_ref, g_ref, go_ref, d_ref, mo_ref, vo_ref):
        g = g_ref[0].astype(F32)
        for p in range(1, n_parts):
            g = g + g_ref[p].astype(F32)
        m_new = ADAM_B1 * m_ref[...] + (1.0 - ADAM_B1) * g
        v_new = ADAM_B2 * v_ref[...] + (1.0 - ADAM_B2) * (g * g)
        go_ref[...] = g
        mo_ref[...] = m_new
        vo_ref[...] = v_new
        d_ref[...] = -ADAM_LR * ((m_new * c1) / (jnp.sqrt(v_new * c2) + ADAM_EPS) + ADAM_WD * w_ref[...])

    blk = pl.BlockSpec((tr, tc), lambda i, j: (i, j))
    return pl.pallas_call(
        body, name=name, grid=(rows // tr, cols // tc),
        in_specs=[blk, blk, blk, pl.BlockSpec((n_parts, tr, tc), lambda i, j: (0, i, j))],
        out_specs=[blk] * 4,
        out_shape=[jax.ShapeDtypeStruct((rows, cols), F32)] * 4,
        compiler_params=_params(("parallel", "parallel")),
    )(w, m, v, g_parts)


_O_F = 1536


def _to_internal(wt_g):
    wf = wt_g.reshape(IN_WIDTH, D_MODEL)
    f = jnp.pad(wf[_O_F:_O_F + HEADS], ((0, N_FPAD - HEADS), (0, 0)))
    return jnp.concatenate([wf[:_O_F], wf[_O_F + HEADS:], f], axis=0)


def _slabs_by_core(dwt, dwt_f):
    sources = ((dwt, 0, _O_F, 0), (dwt_f, _O_F, _O_F + HEADS, _O_F), (dwt, _O_F + HEADS, IN_WIDTH, HEADS))

    def slab(p):
        lo, hi = p * IN_SHARD, (p + 1) * IN_SHARD
        parts = []
        for src, o_lo, o_hi, shift in sources:
            a, b = max(lo, o_lo), min(hi, o_hi)
            if a < b:
                parts.append(src[a - shift:b - shift])
        return parts[0] if len(parts) == 1 else jnp.concatenate(parts, axis=0)

    return jnp.stack([jnp.stack([slab(2 * chip + core) for chip in range(4)]) for core in range(2)])


def kernel(x, c, w_ada, b_ada, norm_g, w_in, b_f, q_norm_g, k_norm_g, conv_w, w_attn_out, w_conv_out, w_o, loss_target, m_w_ada, m_b_ada, m_norm_g, m_w_in, m_b_f, m_q_norm_g, m_k_norm_g, m_conv_w, m_w_attn_out, m_w_conv_out, m_w_o, v_w_ada, v_b_ada, v_norm_g, v_w_in, v_b_f, v_q_norm_g, v_k_norm_g, v_conv_w, v_w_attn_out, v_w_conv_out, v_w_o):
    me = 4 * lax.axis_index("x") + 2 * lax.axis_index("y") + lax.axis_index("c")
    s = x.shape[1]
    x2, t2 = x[0], loss_target[0]

    w_in_g, c_all, ada_g = _gather_weights_and_ada(w_in[0].T.astype(BF16), c, w_ada[0])
    ada_mine = lax.dynamic_index_in_dim(ada_g[:, :, 0, :], me, axis=1, keepdims=False)
    ada3 = (ada_mine.reshape(1, 3 * D_MODEL) + b_ada).reshape(3, D_MODEL)
    w_all_t = _to_internal(w_in_g)
    qg = jnp.tile(q_norm_g, (1, HEADS))
    kg = jnp.tile(k_norm_g, (1, HEADS))
    bf_pad = jnp.pad(b_f, ((0, 0), (0, LANES - HEADS)))

    (proj, fl, h), (qa, ka, va, kt, vt), (cw_g, wa_g, wb_g, wo_g) = _proj_fwd(
        x2, ada3, norm_g, w_all_t, bf_pad, qg, kg,
        [conv_w[0], w_attn_out[0].astype(BF16), w_conv_out[0].astype(BF16), w_o[0].astype(BF16)])
    wa = jnp.transpose(wa_g, (1, 0, 2)).reshape(ATTN_W, D_MODEL)
    wb = jnp.transpose(wb_g, (1, 0, 2)).reshape(CONV_W, D_MODEL)
    wo = wo_g.reshape(D_MODEL, D_MODEL)
    cw = jnp.transpose(cw_g, (1, 0, 2)).reshape(3, CONV_W)
    attn, oa, qb = _attn_fwd(qa, ka, vt, proj)
    (dy, dgab, do, dza, dob, dwo, dwa, dwb, dgate, loss_part) = _tail(oa, attn, proj, x2, t2, ada3, wa, wb, wo, cw)

    def by_core(slabs8):
        return jnp.swapaxes(slabs8.reshape((4, 2) + slabs8.shape[1:]), 0, 1).astype(BF16)

    core = lax.axis_index("c").astype(jnp.int32).reshape(1)
    small = [by_core(jnp.transpose(dwa.reshape(ATTN_W, N_DEV, LANES), (1, 0, 2))),
             by_core(jnp.transpose(dwb.reshape(CONV_W, N_DEV, LANES), (1, 0, 2))),
             by_core(dwo.reshape(N_DEV, D_MODEL // N_DEV, D_MODEL))]
    small_sums = [_pair_sum(m2, t4, core, "pair_sum_" + nm)
                  for m2, t4, nm in zip(small, _sibling_swap(small, "swap_small"), ("wa", "wb", "wo"))]
    dq, dk, dv, dqg, dkg, dcum = _attn_bwd(qb, ka, kt, va, do, proj, qg, kg)
    df, dbf = _forget_bwd(dcum, fl, bf_pad)
    dcb, dcc, dcu, dcz, dcw = _conv_bwd(dob, proj, cw)
    pieces = [dq, dk, dv, dza, dcb, dcc, dcu, dcz, dgab, df]
    (dw_main, dw_f), (g_wa_parts, g_wb_parts, g_wo_parts) = _dw_in(h, pieces, small_sums)

    slabs_in = _slabs_by_core(dw_main, dw_f)
    (theirs_in,) = _sibling_swap([slabs_in], "swap_w_in")
    (grad_x, dshift, dscale, dnormg), (g_in_parts,) = _dh_and_dx(
        pieces, w_all_t, x2, dy, ada3, norm_g, [_pair_sum(slabs_in, theirs_in, core, "pair_sum_w_in")])
    vec = jnp.concatenate([dshift, dscale, dgate, dnormg, dbf, dcw.reshape(1, 3 * CONV_W), loss_part, dqg, dkg],
                          axis=1)
    (vec_all,) = _gather_direct([vec], "gather_small")
    vec_all = vec_all.reshape(N_DEV, vec.shape[1])
    n_main = 4 * D_MODEL + LANES + 3 * CONV_W + LANES
    tot, g_qg, g_kg = _sum_small(
        vec_all[:, :n_main],
        vec_all[:, n_main:n_main + ATTN_W].reshape(N_DEV * HEADS, HEAD_DIM),
        vec_all[:, n_main + ATTN_W:].reshape(N_DEV * HEADS, HEAD_DIM))
    g_b_ada = tot[:, 0:3 * D_MODEL]
    g_norm_g = tot[:, 3 * D_MODEL:4 * D_MODEL]
    g_b_f = tot[:, 4 * D_MODEL:4 * D_MODEL + HEADS]
    g_cw_full = tot[:, 4 * D_MODEL + LANES:4 * D_MODEL + LANES + 3 * CONV_W].reshape(3, CONV_W)
    g_cw = lax.dynamic_slice(g_cw_full, (0, me * (CONV_W // N_DEV)), (3, CONV_W // N_DEV))
    dada_mine = lax.dynamic_slice(vec_all[:, 0:3 * D_MODEL], (0, me * ADA_SHARD), (N_DEV, ADA_SHARD))
    g_w_ada = _grad_w_ada(jnp.transpose(c_all, (0, 2, 1)), dada_mine.reshape(N_DEV, 1, ADA_SHARD))

    upd = {}
    upd["w_ada"] = _adamw(w_ada[0], m_w_ada[0], v_w_ada[0], g_w_ada[None], "adamw_w_ada")
    upd["b_ada"] = _adamw(b_ada, m_b_ada, v_b_ada, g_b_ada[None], "adamw_b_ada")
    upd["norm_g"] = _adamw(norm_g, m_norm_g, v_norm_g, g_norm_g[None], "adamw_norm_g")
    upd["w_in"] = [u.T for u in _adamw(w_in[0].T, m_w_in[0].T, v_w_in[0].T, g_in_parts, "adamw_w_in")]
    upd["b_f"] = _adamw(b_f, m_b_f, v_b_f, g_b_f[None], "adamw_b_f")
    upd["q_norm_g"] = _adamw(q_norm_g, m_q_norm_g, v_q_norm_g, g_qg[None], "adamw_q_norm_g")
    upd["k_norm_g"] = _adamw(k_norm_g, m_k_norm_g, v_k_norm_g, g_kg[None], "adamw_k_norm_g")
    upd["conv_w"] = _adamw(conv_w[0], m_conv_w[0], v_conv_w[0], g_cw[None], "adamw_conv_w")
    upd["w_attn_out"] = _adamw(w_attn_out[0], m_w_attn_out[0], v_w_attn_out[0], g_wa_parts, "adamw_w_attn_out")
    upd["w_conv_out"] = _adamw(w_conv_out[0], m_w_conv_out[0], v_w_conv_out[0], g_wb_parts, "adamw_w_conv_out")
    upd["w_o"] = _adamw(w_o[0], m_w_o[0], v_w_o[0], g_wo_parts, "adamw_w_o")

    names = ["w_ada", "b_ada", "norm_g", "w_in", "b_f", "q_norm_g", "k_norm_g", "conv_w",
             "w_attn_out", "w_conv_out", "w_o"]
    lead = {"w_ada", "w_in", "conv_w", "w_attn_out", "w_conv_out", "w_o"}
    fix = lambda n, a: a[None] if n in lead else a
    loss = tot[0, n_main - LANES]
    outs = [loss, grad_x[None]]
    for k in range(4):
        outs += [fix(n, upd[n][k]) for n in names]
    return tuple(outs)
```

```python
import functools

import numpy as np
import jax
import jax.numpy as jnp
from jax import lax
from jax.experimental import pallas as pl
from jax.experimental.pallas import tpu as pltpu

F32 = jnp.float32
BF16 = jnp.bfloat16

D_MODEL = 1024
HEADS = 8
HEAD_DIM = 64
ATTN_W = 512
CONV_W = 512
N_DEV = 8
IN_WIDTH = 6152
IN_SHARD = IN_WIDTH // N_DEV
N_MAIN = 6144
N_FPAD = 128
N_ALL = N_MAIN + N_FPAD
ADA_SHARD = 3 * D_MODEL // N_DEV
EPS = 1e-6
NEG = -1e30

ADAM_LR = 0.001
ADAM_B1 = 0.9
ADAM_B2 = 0.999
ADAM_EPS = 1e-08
ADAM_WD = 0.01
ADAM_STEP = 10

LANES = 128
VMEM_LIMIT = 56 * 1024 * 1024

TM_PROJ = 256
TN_PROJ = 1024
TM_ELEM = 512
TQ = 512
HEADS_PER_STEP = 8
HEADS_PER_STEP_FWD = 8
TM_TAIL = 256
TC_CUM = 512
TK_DW = 2048
TN_DW = 512
TM_DH = 256
HALO = 16

OFF_Q, OFF_K, OFF_V, OFF_ZA, OFF_CB, OFF_CC, OFF_CU, OFF_CZ, OFF_GA, OFF_GB = (
    0, 512, 1024, 1536, 2048, 2560, 3072, 3584, 4096, 5120)


def _params(sem=None):
    return pltpu.CompilerParams(dimension_semantics=sem, vmem_limit_bytes=VMEM_LIMIT)


def _dot(a, b):
    return jnp.dot(a, b, preferred_element_type=F32)


def _dot_nt(a, b):
    return lax.dot_general(a, b, (((1,), (1,)), ((), ())), preferred_element_type=F32)


def _dot_tn(a, b):
    return lax.dot_general(a, b, (((0,), (0,)), ((), ())), preferred_element_type=F32)


def _sigmoid(x):
    return 1.0 / (1.0 + jnp.exp(-x))


def _lane_lo(shape):
    return lax.broadcasted_iota(jnp.int32, shape, len(shape) - 1) < HEAD_DIM


def _seg_sum(z, lo):
    a = jnp.sum(jnp.where(lo, z, 0.0), axis=-1, keepdims=True)
    b = jnp.sum(jnp.where(lo, 0.0, z), axis=-1, keepdims=True)
    return jnp.where(lo, a, b)


def _lane_col(z, lane):
    idx = lax.broadcasted_iota(jnp.int32, z.shape, 1)
    return jnp.sum(jnp.where(idx == lane, z, 0.0), axis=-1, keepdims=True)


def _sub_row(z, row):
    idx = lax.broadcasted_iota(jnp.int32, z.shape, 0)
    return jnp.sum(jnp.where(idx == row, z, 0.0), axis=0, keepdims=True)


def _mesh_pos():
    x, y, c = lax.axis_index("x"), lax.axis_index("y"), lax.axis_index("c")
    return x, y, c, 4 * x + 2 * y + c


def _peer(k, x, y, c):
    px = 1 - x if (k >> 2) & 1 else x
    py = 1 - y if (k >> 1) & 1 else y
    pc = 1 - c if k & 1 else c
    return (px, py, pc), 4 * px + 2 * py + pc


def _gather_copies(ins, outs, send_sems, recv_sems, local_sems):
    x, y, c, me = _mesh_pos()
    copies = []
    for a in range(len(ins)):
        copies.append(pltpu.make_async_copy(ins[a], outs[a].at[me], local_sems.at[a]))
        for k in range(1, N_DEV):
            dev, _ = _peer(k, x, y, c)
            copies.append(pltpu.make_async_remote_copy(
                src_ref=ins[a], dst_ref=outs[a].at[me],
                send_sem=send_sems.at[a * (N_DEV - 1) + k - 1], recv_sem=recv_sems.at[a * (N_DEV - 1) + k - 1],
                device_id=dev, device_id_type=pl.DeviceIdType.MESH))
    return copies


def _gather_sems(n):
    return [pltpu.SemaphoreType.DMA((n * (N_DEV - 1),)), pltpu.SemaphoreType.DMA((n * (N_DEV - 1),)),
            pltpu.SemaphoreType.DMA((n,))]


def _gather_direct(arrs, name):
    n = len(arrs)
    any_spec = pl.BlockSpec(memory_space=pl.ANY)

    def body(*refs):
        copies = _gather_copies(refs[:n], refs[n:2 * n], *refs[2 * n:])
        for cp in copies:
            cp.start()
        for cp in copies:
            cp.wait()

    return pl.pallas_call(
        body, name=name, out_shape=[jax.ShapeDtypeStruct((N_DEV,) + a.shape, a.dtype) for a in arrs],
        in_specs=[any_spec] * n, out_specs=[any_spec] * n, scratch_shapes=_gather_sems(n),
    )(*arrs)


def _ada_phase(c_ref, w_ref, call_ref, adag_ref, mine_ref, send_sems, recv_sems):
    x, y, c, me = _mesh_pos()

    def copy(phase, k, src, dst):
        dev, _ = _peer(k, x, y, c)
        return pltpu.make_async_remote_copy(
            src_ref=src, dst_ref=dst,
            send_sem=send_sems.at[phase * (N_DEV - 1) + k - 1],
            recv_sem=recv_sems.at[phase * (N_DEV - 1) + k - 1],
            device_id=dev, device_id_type=pl.DeviceIdType.MESH)

    call_ref[me] = c_ref[...]
    first = [copy(0, k, c_ref, call_ref.at[me]) for k in range(1, N_DEV)]
    for cp in first:
        cp.start()
    for cp in first:
        cp.wait()
    wb = w_ref[...].astype(BF16)
    for b in range(N_DEV):
        row = jnp.broadcast_to(call_ref[b], (8, D_MODEL)).astype(BF16)
        mine_ref[b] = _sub_row(_dot(row, wb), 0)
    adag_ref[me] = mine_ref[...]
    second = [copy(1, k, mine_ref, adag_ref.at[me]) for k in range(1, N_DEV)]
    for cp in second:
        cp.start()
    for cp in second:
        cp.wait()


def _gather_weights_and_ada(wt_shard, c_row, w_ada_sh):
    any_spec = pl.BlockSpec(memory_space=pl.ANY)
    vm = pl.BlockSpec(memory_space=pltpu.VMEM)

    def body(w_in_ref, c_ref, wada_ref, out_ref, call_ref, adag_ref, mine_ref, send_sems, recv_sems, local_sem,
             ada_send, ada_recv):
        x, y, c, me = _mesh_pos()
        sibling = (x, y, 1 - c)
        chips = [(1 - x, y), (x, 1 - y), (1 - x, 1 - y)]

        def copy(k, src, blk, to):
            return pltpu.make_async_remote_copy(
                src_ref=src, dst_ref=out_ref.at[blk], send_sem=send_sems.at[k], recv_sem=recv_sems.at[k],
                device_id=to, device_id_type=pl.DeviceIdType.MESH)

        local = pltpu.make_async_copy(w_in_ref, out_ref.at[me], local_sem.at[0])
        local.start()
        first = [copy(0, w_in_ref, me, sibling)]
        first += [copy(1 + j, w_in_ref, me, (px, py, c)) for j, (px, py) in enumerate(chips)]
        for cp in first:
            cp.start()
        _ada_phase(c_ref, wada_ref, call_ref, adag_ref, mine_ref, ada_send, ada_recv)
        passed = []
        for j, (px, py) in enumerate(chips):
            blk = 4 * px + 2 * py + c
            copy(1 + j, w_in_ref, blk, (x, y, c)).wait_recv()
            fwd = copy(4 + j, out_ref.at[blk], blk, sibling)
            fwd.start()
            passed.append(fwd)
        copy(0, w_in_ref, 4 * x + 2 * y + 1 - c, (x, y, c)).wait_recv()
        for j, (px, py) in enumerate(chips):
            copy(4 + j, w_in_ref, 4 * px + 2 * py + 1 - c, (x, y, c)).wait_recv()
        for cp in first + passed:
            cp.wait_send()
        local.wait()

    per = N_DEV - 1
    return pl.pallas_call(
        body, name="gather_weights",
        out_shape=[jax.ShapeDtypeStruct((N_DEV,) + wt_shard.shape, wt_shard.dtype),
                   jax.ShapeDtypeStruct((N_DEV, 1, D_MODEL), F32),
                   jax.ShapeDtypeStruct((N_DEV, N_DEV, 1, ADA_SHARD), F32)],
        in_specs=[any_spec, vm, vm], out_specs=[any_spec, vm, vm],
        scratch_shapes=[pltpu.VMEM((N_DEV, 1, ADA_SHARD), F32),
                        pltpu.SemaphoreType.DMA((per,)), pltpu.SemaphoreType.DMA((per,)),
                        pltpu.SemaphoreType.DMA((1,)),
                        pltpu.SemaphoreType.DMA((2 * per,)), pltpu.SemaphoreType.DMA((2 * per,))],
        compiler_params=pltpu.CompilerParams(vmem_limit_bytes=VMEM_LIMIT),
    )(wt_shard, c_row, w_ada_sh)


def _sibling_swap(arrs, name):
    n = len(arrs)
    any_spec = pl.BlockSpec(memory_space=pl.ANY)

    def body(*refs):
        ins, outs = refs[:n], refs[n:2 * n]
        send_sems, recv_sems = refs[2 * n:]
        x, y, c, _ = _mesh_pos()
        copies = [pltpu.make_async_remote_copy(
            src_ref=ins[a].at[1 - c], dst_ref=outs[a], send_sem=send_sems.at[a], recv_sem=recv_sems.at[a],
            device_id=(x, y, 1 - c), device_id_type=pl.DeviceIdType.MESH) for a in range(n)]
        for cp in copies:
            cp.start()
        for cp in copies:
            cp.wait()

    return pl.pallas_call(
        body, name=name,
        out_shape=[jax.ShapeDtypeStruct(a.shape[1:], a.dtype) for a in arrs],
        in_specs=[any_spec] * n, out_specs=[any_spec] * n,
        scratch_shapes=[pltpu.SemaphoreType.DMA((n,)), pltpu.SemaphoreType.DMA((n,))],
    )(*arrs)


def _pair_sum(mine2, theirs, core, name):
    _, _, rows, cols = mine2.shape
    tr = 256 if rows % 256 == 0 else rows

    def body(core_ref, a_ref, b_ref, out_ref):
        out_ref[...] = (a_ref[...].astype(F32) + b_ref[...].astype(F32)).astype(BF16)

    return pl.pallas_call(
        body, name=name,
        grid_spec=pltpu.PrefetchScalarGridSpec(
            num_scalar_prefetch=1, grid=(4, rows // tr),
            in_specs=[pl.BlockSpec((None, None, tr, cols), lambda ch, i, core_: (core_[0], ch, i, 0)),
                      pl.BlockSpec((None, tr, cols), lambda ch, i, core_: (ch, i, 0))],
            out_specs=pl.BlockSpec((None, tr, cols), lambda ch, i, core_: (ch, i, 0))),
        out_shape=jax.ShapeDtypeStruct(theirs.shape, BF16),
        compiler_params=_params(("parallel", "parallel")),
    )(core, mine2, theirs)


def _chip_copies(ins, outs, send_sems, recv_sems, local_sems):
    x, y, c, _ = _mesh_pos()
    my_chip = 2 * x + y
    chips = [(1 - x, y), (x, 1 - y), (1 - x, 1 - y)]
    copies = []
    for a in range(len(ins)):
        copies.append(pltpu.make_async_copy(ins[a].at[my_chip], outs[a].at[my_chip], local_sems.at[a]))
        for j, (px, py) in enumerate(chips):
            copies.append(pltpu.make_async_remote_copy(
                src_ref=ins[a].at[2 * px + py], dst_ref=outs[a].at[my_chip],
                send_sem=send_sems.at[a * 3 + j], recv_sem=recv_sems.at[a * 3 + j],
                device_id=(px, py, c), device_id_type=pl.DeviceIdType.MESH))
    return copies


def _proj_fwd(x, ada3, norm_g, w_all_t, bf_pad, qg, kg, later):
    s = x.shape[0]
    tm, tn = min(TM_PROJ, s), TN_PROJ
    nt = s // tm
    n = len(later)

    def body(x_ref, ada_ref, g_ref, wt_ref, bf_ref, qg_ref, kg_ref, *rest):
        ins, (proj_ref, fl_ref, h_ref), rows_refs, rest = rest[:n], rest[n:n + 3], rest[n + 3:n + 8], rest[n + 8:]
        outs, (carry, send_sems, recv_sems, local_sems) = rest[:n], rest[n:]
        i = pl.program_id(0)

        @pl.when(i == 0)
        def _():
            carry[...] = jnp.zeros_like(carry)
            for cp in _gather_copies(ins, outs, send_sems, recv_sems, local_sems):
                cp.start()

        xv = x_ref[...]
        r = lax.rsqrt(jnp.mean(xv * xv, axis=-1, keepdims=True) + EPS)
        hv = ((xv * r) * g_ref[...]) * (1.0 + ada_ref[1:2, :]) + ada_ref[0:1, :]
        hb = hv.astype(BF16)
        h_ref[...] = hb
        fl = _dot_nt(hb, wt_ref[N_MAIN:N_ALL, :])
        fl_ref[...] = fl
        for j in range(N_MAIN // tn):
            proj_ref[:, j * tn:(j + 1) * tn] = _dot_nt(hb, wt_ref[j * tn:(j + 1) * tn, :]).astype(BF16)
        _attention_rows(proj_ref, fl, bf_ref, qg_ref, kg_ref, carry, *rows_refs)

        @pl.when(i == nt - 1)
        def _():
            for cp in _gather_copies(ins, outs, send_sems, recv_sems, local_sems):
                cp.wait()

    any_spec = pl.BlockSpec(memory_space=pl.ANY)
    heads = pl.BlockSpec((HEADS, tm, LANES), lambda i: (0, i, 0))
    heads_t = pl.BlockSpec((HEADS, LANES, tm), lambda i: (0, 0, i))
    vec = pl.BlockSpec((1, ATTN_W), lambda i: (0, 0))
    res = pl.pallas_call(
        body, name="proj_fwd", grid=(nt,),
        in_specs=[pl.BlockSpec((tm, D_MODEL), lambda i: (i, 0)),
                  pl.BlockSpec((3, D_MODEL), lambda i: (0, 0)),
                  pl.BlockSpec((1, D_MODEL), lambda i: (0, 0)),
                  pl.BlockSpec((N_ALL, D_MODEL), lambda i: (0, 0)),
                  pl.BlockSpec((1, LANES), lambda i: (0, 0)), vec, vec] + [any_spec] * n,
        out_specs=[pl.BlockSpec((tm, N_MAIN), lambda i: (i, 0)),
                   pl.BlockSpec((tm, N_FPAD), lambda i: (i, 0)),
                   pl.BlockSpec((tm, D_MODEL), lambda i: (i, 0)),
                   heads, heads, heads, heads_t, heads_t] + [any_spec] * n,
        out_shape=[jax.ShapeDtypeStruct((s, N_MAIN), BF16),
                   jax.ShapeDtypeStruct((s, N_FPAD), F32),
                   jax.ShapeDtypeStruct((s, D_MODEL), BF16)]
        + [jax.ShapeDtypeStruct((HEADS, s, LANES), BF16)] * 3
        + [jax.ShapeDtypeStruct((HEADS, LANES, s), BF16)] * 2
        + [jax.ShapeDtypeStruct((N_DEV,) + a.shape, a.dtype) for a in later],
        scratch_shapes=[pltpu.VMEM((1, LANES), F32)] + _gather_sems(n),
        compiler_params=_params(("arbitrary",)),
    )(x, ada3, norm_g, w_all_t, bf_pad, qg, kg, *later)
    return res[:3], res[3:8], res[8:]


L_ONE_Q, L_F_Q, L_LSE_Q, L_END = HEAD_DIM, HEAD_DIM + 3, HEAD_DIM + 6, HEAD_DIM + 9


def _split3(f):
    hi = f.astype(BF16).astype(F32)
    r = f - hi
    mid = r.astype(BF16).astype(F32)
    return hi, mid, r - mid


def _place3(lane, first, parts, otherwise):
    a, b, c = parts
    return jnp.where(lane == first, a, jnp.where(lane == first + 1, b, jnp.where(lane == first + 2, c, otherwise)))


def _log_forget(fl, bf):
    z = fl + bf
    lf = jnp.minimum(z, 0.0) - jnp.log1p(jnp.exp(-jnp.abs(z)))
    lane = lax.broadcasted_iota(jnp.int32, z.shape, 1)
    return jnp.where(lane < HEADS, lf, 0.0)


def _attention_rows(p_ref, fl, bf_ref, qg_ref, kg_ref, carry, qa_ref, ka_ref, va_ref, kt_ref, vt_ref):
    tm = fl.shape[0]
    scale = HEAD_DIM ** -0.5
    tri = (lax.broadcasted_iota(jnp.int32, (tm, tm), 1) <= lax.broadcasted_iota(jnp.int32, (tm, tm), 0)).astype(F32)
    cum_v = jnp.dot(tri, _log_forget(fl, bf_ref[...]), preferred_element_type=F32,
                    precision=lax.Precision.HIGHEST) + carry[...]
    carry[...] = _sub_row(cum_v, tm - 1)
    lane = lax.broadcasted_iota(jnp.int32, (tm, LANES), 1)
    lo = lane < HEAD_DIM
    v_tail = jnp.where(lane < L_F_Q, 1.0, 0.0)
    for pr in range(ATTN_W // LANES):
        sl = slice(pr * LANES, (pr + 1) * LANES)
        q2 = p_ref[:, OFF_Q + pr * LANES:OFF_Q + (pr + 1) * LANES].astype(F32)
        k2 = p_ref[:, OFF_K + pr * LANES:OFF_K + (pr + 1) * LANES].astype(F32)
        v2 = p_ref[:, OFF_V + pr * LANES:OFF_V + (pr + 1) * LANES].astype(F32)
        rq = lax.rsqrt(_seg_sum(q2 * q2, lo) * (1.0 / HEAD_DIM) + EPS)
        rk = lax.rsqrt(_seg_sum(k2 * k2, lo) * (1.0 / HEAD_DIM) + EPS)
        qn = ((q2 * rq) * qg_ref[:, sl]) * scale
        kn = (k2 * rk) * kg_ref[:, sl]
        for hh in range(2):
            h = 2 * pr + hh
            f3 = _split3(_lane_col(cum_v, h))
            qh = qn if hh == 0 else pltpu.roll(qn, HEAD_DIM, 1)
            kh = kn if hh == 0 else pltpu.roll(kn, HEAD_DIM, 1)
            vh = v2 if hh == 0 else pltpu.roll(v2, HEAD_DIM, 1)
            q_tail = jnp.where(lane < L_F_Q, 1.0, _place3(lane, L_F_Q, f3, 0.0))
            k_tail = _place3(lane, L_ONE_Q, tuple(-f for f in f3), jnp.where(lane < L_END, 1.0, 0.0))
            k_row = jnp.where(lo, kh, k_tail)
            v_row = jnp.where(lo, vh, v_tail)
            qa_ref[h] = jnp.where(lo, qh, q_tail).astype(BF16)
            ka_ref[h] = k_row.astype(BF16)
            va_ref[h] = v_row.astype(BF16)
            kt_ref[h] = k_row.T.astype(BF16)
            vt_ref[h] = v_row.T.astype(BF16)


def _causal_t(t):
    return lax.broadcasted_iota(jnp.int32, (t, t), 0) <= lax.broadcasted_iota(jnp.int32, (t, t), 1)


def _tri_steps(nt, q_major):
    if q_major:
        pairs = [(i, j) for i in range(nt) for j in range(i + 1)]
    else:
        pairs = [(i, j) for j in range(nt) for i in range(j, nt)]
    return (jnp.asarray(np.array([p[0] for p in pairs], np.int32)),
            jnp.asarray(np.array([p[1] for p in pairs], np.int32)))


def _attn_fwd(qa, ka, vt, proj):
    s = qa.shape[1]
    t = min(TQ, s)
    it, jt = _tri_steps(s // t, True)
    hp = HEADS_PER_STEP_FWD
    wide = hp * HEAD_DIM
    za_blk = OFF_ZA // wide

    def body(it_ref, jt_ref, q_ref, k_ref, vt_ref, za_ref, attn_ref, oa_ref, qb_ref, m_s, acc_s, pair_s):
        step = pl.program_id(1)
        i, j = it_ref[step], jt_ref[step]

        @pl.when(j == 0)
        def _():
            m_s[...] = jnp.full_like(m_s, NEG)
            acc_s[...] = jnp.zeros_like(acc_s)

        def update(masked):
            for hh in range(hp):
                st = _dot_nt(k_ref[hh], q_ref[hh])
                if masked:
                    st = jnp.where(_causal_t(t), st, NEG)
                m_prev = m_s[hh]
                m_next = jnp.maximum(m_prev, jnp.max(st, axis=0, keepdims=True))
                alpha = jnp.exp(m_prev - m_next)
                pt = jnp.exp(st - m_next).astype(BF16)
                acc_s[hh] = acc_s[hh] * alpha + _dot(vt_ref[hh], pt)
                m_s[hh] = m_next

        @pl.when(j < i)
        def _():
            update(False)

        @pl.when(j == i)
        def _():
            update(True)
            row = lax.broadcasted_iota(jnp.int32, (LANES, t), 0)
            lane = lax.broadcasted_iota(jnp.int32, (t, LANES), 1)
            for hh in range(hp):
                l_row = acc_s[hh, L_ONE_Q:L_ONE_Q + 1, :]
                pair_s[hh * HEAD_DIM:(hh + 1) * HEAD_DIM, :] = acc_s[hh, 0:HEAD_DIM, :] / l_row
                lse3 = _split3(m_s[hh] + jnp.log(l_row))
                tail_t = _place3(row, L_LSE_Q, tuple(-x for x in lse3), 0.0)
                keep_q = jnp.logical_or(lane < L_LSE_Q, lane >= L_END)
                qb_ref[hh] = jnp.where(keep_q, q_ref[hh].astype(F32), tail_t.T).astype(BF16)
            out = pair_s[...].T
            attn_ref[...] = out
            z = za_ref[...].astype(F32)
            oa_ref[...] = (out * (z * _sigmoid(z))).astype(BF16)

    pair_q = pl.BlockSpec((hp, t, LANES), lambda p, n, it_, jt_: (p, it_[n], 0))
    pair_k = pl.BlockSpec((hp, t, LANES), lambda p, n, it_, jt_: (p, jt_[n], 0))
    pair_kt = pl.BlockSpec((hp, LANES, t), lambda p, n, it_, jt_: (p, 0, jt_[n]))
    out_q = pl.BlockSpec((t, wide), lambda p, n, it_, jt_: (it_[n], p))
    return pl.pallas_call(
        body, name="attn_fwd",
        grid_spec=pltpu.PrefetchScalarGridSpec(
            num_scalar_prefetch=2, grid=(HEADS // hp, it.shape[0]),
            in_specs=[pair_q, pair_k, pair_kt,
                      pl.BlockSpec((t, wide), lambda p, n, it_, jt_: (it_[n], za_blk + p))],
            out_specs=[out_q, out_q, pair_q],
            scratch_shapes=[pltpu.VMEM((hp, 1, t), F32), pltpu.VMEM((hp, LANES, t), F32),
                            pltpu.VMEM((wide, t), F32)]),
        out_shape=[jax.ShapeDtypeStruct((s, ATTN_W), F32),
                   jax.ShapeDtypeStruct((s, ATTN_W), BF16),
                   jax.ShapeDtypeStruct((HEADS, s, LANES), BF16)],
        compiler_params=_params(("parallel", "arbitrary")),
    )(it, jt, qa, ka, vt, proj)


def _conv_parts(gb_ref, gc_ref, u_ref, zb_ref, gch_ref, uh_ref, first, w_ref, tm):
    gb, gc = gb_ref[...].astype(F32), gc_ref[...].astype(F32)
    u, zb = u_ref[...].astype(F32), zb_ref[...].astype(F32)
    cu = gc * u
    cu_h = jnp.where(first, 0.0, gch_ref[...].astype(F32) * uh_ref[...].astype(F32))
    prev1, prev2 = _sub_row(cu_h, HALO - 1), _sub_row(cu_h, HALO - 2)
    row = lax.broadcasted_iota(jnp.int32, cu.shape, 0)
    r1 = jnp.where(row == 0, prev1, pltpu.roll(cu, 1, 0))
    r2 = jnp.where(row == 0, prev2, jnp.where(row == 1, prev1, pltpu.roll(cu, 2, 0)))
    conv = w_ref[2:3, :] * cu + w_ref[1:2, :] * r1 + w_ref[0:1, :] * r2
    return gb, gc, u, zb, cu, r1, r2, conv


def _conv_specs(tm, s, width=LANES):
    def tile(off):
        return pl.BlockSpec((tm, width), lambda c, i: (i, off // width + c))

    def before(off):
        return pl.BlockSpec((HALO, width), lambda c, i: (jnp.maximum(i * (tm // HALO) - 1, 0), off // width + c))

    def after(off):
        return pl.BlockSpec((HALO, width),
                            lambda c, i: (jnp.minimum((i + 1) * (tm // HALO), s // HALO - 1), off // width + c))

    return ([tile(OFF_CB), tile(OFF_CC), tile(OFF_CU), tile(OFF_CZ)], [before(OFF_CC), before(OFF_CU)],
            [after(OFF_CB), after(OFF_CZ)])


def _tail(oa, attn, proj, x, target, ada3, wa, wb, wo, conv_w):
    s = x.shape[0]
    tm = min(TM_TAIL, s)
    gab_blk = OFF_GA // (2 * D_MODEL)
    za_blk = OFF_ZA // ATTN_W
    tiles, befores, _ = _conv_specs(tm, s, CONV_W)

    def body(oa_ref, attn_ref, za_ref, gb_ref, gc_ref, u_ref, zb_ref, gch_ref, uh_ref, cw_ref, gab_ref, x_ref, t_ref,
             ada_ref, wa_ref, wb_ref, wo_ref,
             dy_ref, dgab_ref, do_ref, dza_ref, dob_ref, dwo_ref, dwa_ref, dwb_ref, dgate_ref, loss_ref):
        first = pl.program_id(0) == 0

        @pl.when(first)
        def _():
            dwo_ref[...] = jnp.zeros_like(dwo_ref)
            dwa_ref[...] = jnp.zeros_like(dwa_ref)
            dwb_ref[...] = jnp.zeros_like(dwb_ref)
            dgate_ref[...] = jnp.zeros_like(dgate_ref)
            loss_ref[...] = jnp.zeros_like(loss_ref)

        gb, _, _, zb, _, _, _, conv = _conv_parts(gb_ref, gc_ref, u_ref, zb_ref, gch_ref, uh_ref, first, cw_ref, tm)
        ob_v = (gb * conv * (zb * _sigmoid(zb))).astype(BF16)
        oa_v = oa_ref[...]
        wa_v, wb_v, wo_v = wa_ref[...], wb_ref[...], wo_ref[...]
        a2 = _dot(oa_v, wa_v)
        b2 = _dot(ob_v, wb_v)
        sa = _sigmoid(gab_ref[:, 0:D_MODEL].astype(F32))
        sb = _sigmoid(gab_ref[:, D_MODEL:2 * D_MODEL].astype(F32))
        mb = (sa * a2 + sb * b2).astype(BF16)
        mo = _dot(mb, wo_v)
        gate = ada_ref[2:3, :]
        err = (x_ref[...] + gate * mo) - t_ref[...]
        dy = err * (1.0 / D_MODEL)
        dy_ref[...] = dy
        loss_ref[...] += 0.5 * jnp.sum(err * err) * (1.0 / D_MODEL)
        dgate_ref[...] += jnp.sum(dy * mo, axis=0, keepdims=True)
        dmo = (dy * gate).astype(BF16)
        dmerged = _dot_nt(dmo, wo_v)
        dwo_ref[...] += _dot_tn(mb, dmo)
        da2 = (dmerged * sa).astype(BF16)
        db2 = (dmerged * sb).astype(BF16)
        dgab_ref[:, 0:D_MODEL] = (dmerged * a2 * (sa * (1.0 - sa))).astype(BF16)
        dgab_ref[:, D_MODEL:2 * D_MODEL] = (dmerged * b2 * (sb * (1.0 - sb))).astype(BF16)
        doa = _dot_nt(da2, wa_v)
        dob_ref[...] = _dot_nt(db2, wb_v)
        dwa_ref[...] += _dot_tn(oa_v, da2)
        dwb_ref[...] += _dot_tn(ob_v, db2)

        lane = lax.broadcasted_iota(jnp.int32, (tm, LANES), 1)
        lo = lane < HEAD_DIM
        for pr in range(ATTN_W // LANES):
            sl = slice(pr * LANES, (pr + 1) * LANES)
            g, a, z = doa[:, sl], attn_ref[:, sl], za_ref[:, sl].astype(F32)
            sg = _sigmoid(z)
            dat = (g * (z * sg)).astype(BF16).astype(F32)
            prod = dat * a
            dza_ref[:, sl] = (g * a * (sg * (1.0 + z * (1.0 - sg)))).astype(BF16)
            for hh in range(2):
                sel = lo if hh == 0 else jnp.logical_not(lo)
                delta3 = _split3(jnp.sum(jnp.where(sel, prod, 0.0), axis=-1, keepdims=True))
                dh = dat if hh == 0 else pltpu.roll(dat, HEAD_DIM, 1)
                tail_lanes = _place3(lane, L_ONE_Q, tuple(-d for d in delta3), 0.0)
                do_ref[2 * pr + hh] = jnp.where(lo, dh, tail_lanes).astype(BF16)

    half = pl.BlockSpec((tm, ATTN_W), lambda i: (i, 0))
    full = pl.BlockSpec((tm, D_MODEL), lambda i: (i, 0))

    def const(shape):
        return pl.BlockSpec(shape, lambda i: (0, 0))

    def one_axis(spec):
        return pl.BlockSpec(spec.block_shape, lambda i, f=spec.index_map: f(0, i))

    return pl.pallas_call(
        body, name="tail", grid=(s // tm,),
        in_specs=[half, half, pl.BlockSpec((tm, ATTN_W), lambda i: (i, za_blk))]
        + [one_axis(sp) for sp in tiles + befores]
        + [const((3, CONV_W)), pl.BlockSpec((tm, 2 * D_MODEL), lambda i: (i, gab_blk)), full, full,
           const((3, D_MODEL)), const((ATTN_W, D_MODEL)), const((CONV_W, D_MODEL)), const((D_MODEL, D_MODEL))],
        out_specs=[full, pl.BlockSpec((tm, 2 * D_MODEL), lambda i: (i, 0)),
                   pl.BlockSpec((HEADS, tm, LANES), lambda i: (0, i, 0)), half, half,
                   const((D_MODEL, D_MODEL)), const((ATTN_W, D_MODEL)), const((CONV_W, D_MODEL)),
                   const((1, D_MODEL)), const((1, LANES))],
        out_shape=[jax.ShapeDtypeStruct((s, D_MODEL), F32),
                   jax.ShapeDtypeStruct((s, 2 * D_MODEL), BF16),
                   jax.ShapeDtypeStruct((HEADS, s, LANES), BF16),
                   jax.ShapeDtypeStruct((s, ATTN_W), BF16),
                   jax.ShapeDtypeStruct((s, CONV_W), F32),
                   jax.ShapeDtypeStruct((D_MODEL, D_MODEL), F32),
                   jax.ShapeDtypeStruct((ATTN_W, D_MODEL), F32),
                   jax.ShapeDtypeStruct((CONV_W, D_MODEL), F32),
                   jax.ShapeDtypeStruct((1, D_MODEL), F32),
                   jax.ShapeDtypeStruct((1, LANES), F32)],
        compiler_params=_params(("arbitrary",)),
    )(oa, attn, proj, *([proj] * 6), conv_w, proj, x, target, ada3, wa, wb, wo)


def _attn_bwd(qb, ka, kt, va, do, proj, qg, kg):
    s = qb.shape[1]
    t = min(TQ, s)
    nt = s // t
    hp = HEADS_PER_STEP
    wide = hp * HEAD_DIM
    scale = HEAD_DIM ** -0.5
    it, jt = _tri_steps(nt, False)

    def body(it_ref, jt_ref, q_ref, k_ref, kt_ref, v_ref, do_ref, qraw_ref, kraw_ref, qg_ref, kg_ref,
             dq_ref, dk_ref, dv_ref, dqg_ref, dkg_ref, dcum_ref, dqt_s, dk_s, dv_s, rows_s):
        grp, step = pl.program_id(0), pl.program_id(1)
        i, j = it_ref[step], jt_ref[step]
        lane = lax.broadcasted_iota(jnp.int32, (t, LANES), 1)
        lo = lane < HEAD_DIM

        @pl.when(step == 0)
        def _():
            dqt_s[...] = jnp.zeros_like(dqt_s)
            dqg_ref[...] = jnp.zeros_like(dqg_ref)
            dkg_ref[...] = jnp.zeros_like(dkg_ref)

        @pl.when(i == j)
        def _():
            dk_s[...] = jnp.zeros_like(dk_s)
            dv_s[...] = jnp.zeros_like(dv_s)

        def update(masked):
            for hh in range(hp):
                qh, doh = q_ref[hh], do_ref[hh]
                st = _dot_nt(k_ref[hh], qh)
                if masked:
                    st = jnp.where(_causal_t(t), st, NEG)
                pt = jnp.exp(st)
                dst = (pt * _dot_nt(v_ref[hh], doh)).astype(BF16)
                dv_s[hh] += _dot(pt.astype(BF16), doh)
                dk_s[hh] += _dot(dst, qh)
                dqt_s[hh, i] += _dot(kt_ref[hh], dst)

        def pair(a, b):
            return jnp.where(lo, a, pltpu.roll(b, HEAD_DIM, 1))

        def norm_bwd(raw, dy, g, dg_ref, out_ref, sl):
            r = lax.rsqrt(_seg_sum(raw * raw, lo) * (1.0 / HEAD_DIM) + EPS)
            xhat = raw * r
            dg_ref[:, sl] += jnp.sum(dy * xhat, axis=0, keepdims=True)
            dxh = dy * g
            dx = r * (dxh - xhat * (_seg_sum(dxh * xhat, lo) * (1.0 / HEAD_DIM)))
            out_ref[:, sl] = dx.astype(BF16)

        @pl.when(i > j)
        def _():
            update(False)

        @pl.when(i == j)
        def _():
            update(True)
            dq_rows = [dqt_s[hh, i].T for hh in range(hp)]
            rows = jnp.zeros((t, LANES), F32)
            for hh in range(hp):
                rows = jnp.where(lane == grp * hp + hh, _lane_col(dq_rows[hh], L_F_Q), rows)
            rows_s[...] = rows
            for pr in range(hp // 2):
                sl = slice(pr * LANES, (pr + 1) * LANES)
                norm_bwd(qraw_ref[:, sl].astype(F32), pair(dq_rows[2 * pr], dq_rows[2 * pr + 1]) * scale,
                         qg_ref[:, sl], dqg_ref, dq_ref, sl)

        @pl.when(i == nt - 1)
        def _():
            dcum = rows_s[...]
            for hh in range(hp):
                dcum = jnp.where(lane == grp * hp + hh, dcum - _lane_col(dk_s[hh], L_ONE_Q), dcum)
            dcum_ref[0] = dcum
            for pr in range(hp // 2):
                sl = slice(pr * LANES, (pr + 1) * LANES)
                norm_bwd(kraw_ref[:, sl].astype(F32), pair(dk_s[2 * pr], dk_s[2 * pr + 1]),
                         kg_ref[:, sl], dkg_ref, dk_ref, sl)
                dv_ref[:, sl] = pair(dv_s[2 * pr], dv_s[2 * pr + 1]).astype(BF16)

    pair_q = pl.BlockSpec((hp, t, LANES), lambda p, n, it_, jt_: (p, it_[n], 0))
    pair_k = pl.BlockSpec((hp, t, LANES), lambda p, n, it_, jt_: (p, jt_[n], 0))
    pair_kt = pl.BlockSpec((hp, LANES, t), lambda p, n, it_, jt_: (p, 0, jt_[n]))
    tok = pl.BlockSpec((t, wide), lambda p, n, it_, jt_: (jt_[n], p))
    gain = pl.BlockSpec((1, wide), lambda p, n, it_, jt_: (0, p))
    return pl.pallas_call(
        body, name="attn_bwd",
        grid_spec=pltpu.PrefetchScalarGridSpec(
            num_scalar_prefetch=2, grid=(HEADS // hp, it.shape[0]),
            in_specs=[pair_q, pair_k, pair_kt, pair_k, pair_q,
                      pl.BlockSpec((t, wide), lambda p, n, it_, jt_: (jt_[n], OFF_Q // wide + p)),
                      pl.BlockSpec((t, wide), lambda p, n, it_, jt_: (jt_[n], OFF_K // wide + p)), gain, gain],
            out_specs=[tok, tok, tok, gain, gain,
                       pl.BlockSpec((1, t, LANES), lambda p, n, it_, jt_: (p, jt_[n], 0))],
            scratch_shapes=[pltpu.VMEM((hp, nt, LANES, t), F32), pltpu.VMEM((hp, t, LANES), F32),
                            pltpu.VMEM((hp, t, LANES), F32), pltpu.VMEM((t, LANES), F32)]),
        out_shape=[jax.ShapeDtypeStruct((s, ATTN_W), BF16)] * 3
        + [jax.ShapeDtypeStruct((1, ATTN_W), F32)] * 2
        + [jax.ShapeDtypeStruct((HEADS // hp, s, LANES), F32)],
        compiler_params=_params(("parallel", "arbitrary")),
    )(it, jt, qb, ka, kt, va, do, proj, proj, qg, kg)


def _forget_bwd(dcum, fl, bf_pad):
    s = fl.shape[0]
    tc = min(TC_CUM, s)
    n = s // tc

    def body(dc_ref, fl_ref, bf_ref, df_ref, dbf_ref, carry):
        @pl.when(pl.program_id(0) == 0)
        def _():
            carry[...] = jnp.zeros_like(carry)
            dbf_ref[...] = jnp.zeros_like(dbf_ref)
        r = lax.broadcasted_iota(jnp.int32, (tc, tc), 0)
        cidx = lax.broadcasted_iota(jnp.int32, (tc, tc), 1)
        tri = (cidx >= r).astype(F32)
        dc = dc_ref[0]
        for grp in range(1, dcum.shape[0]):
            dc = dc + dc_ref[grp]
        dlf = jnp.dot(tri, dc, preferred_element_type=F32, precision=lax.Precision.HIGHEST) + carry[...]
        carry[...] += jnp.sum(dc, axis=0, keepdims=True)
        lane = lax.broadcasted_iota(jnp.int32, (tc, LANES), 1)
        dfl = jnp.where(lane < HEADS, dlf * _sigmoid(-(fl_ref[...] + bf_ref[...])), 0.0)
        df_ref[...] = dfl.astype(BF16)
        dbf_ref[...] += jnp.sum(dfl, axis=0, keepdims=True)

    rev = pl.BlockSpec((tc, LANES), lambda i: (n - 1 - i, 0))
    vec = pl.BlockSpec((1, LANES), lambda i: (0, 0))
    return pl.pallas_call(
        body, name="forget_bwd", grid=(n,),
        in_specs=[pl.BlockSpec((dcum.shape[0], tc, LANES), lambda i: (0, n - 1 - i, 0)), rev, vec],
        out_specs=[rev, vec],
        out_shape=[jax.ShapeDtypeStruct((s, LANES), BF16), jax.ShapeDtypeStruct((1, LANES), F32)],
        scratch_shapes=[pltpu.VMEM((1, LANES), F32)],
        compiler_params=_params(("arbitrary",)),
    )(dcum, fl, bf_pad)


def _conv_bwd(dob, proj, conv_w):
    s = dob.shape[0]
    tm = min(TM_ELEM, s)
    tiles, befores, afters = _conv_specs(tm, s)

    def body(dob_ref, dnext_ref, gb_ref, gc_ref, u_ref, zb_ref, gch_ref, uh_ref, gbn_ref, zbn_ref, w_ref,
             dgb_ref, dgc_ref, du_ref, dzb_ref, dw_ref):
        i = pl.program_id(1)

        @pl.when(i == 0)
        def _():
            dw_ref[...] = jnp.zeros_like(dw_ref)
        gb, gc, u, zb, cu, r1, r2, conv = _conv_parts(gb_ref, gc_ref, u_ref, zb_ref, gch_ref, uh_ref, i == 0, w_ref, tm)
        g = dob_ref[...]
        sg = _sigmoid(zb)
        sz = zb * sg
        dconv = g * gb * sz
        zn = zbn_ref[0:8, :].astype(F32)
        dcn = jnp.where(i == pl.num_programs(1) - 1, 0.0,
                        dnext_ref[...] * gbn_ref[0:8, :].astype(F32) * (zn * _sigmoid(zn)))
        nxt1, nxt2 = _sub_row(dcn, 0), _sub_row(dcn, 1)
        row = lax.broadcasted_iota(jnp.int32, (tm, LANES), 0)
        f1 = jnp.where(row == tm - 1, nxt1, pltpu.roll(dconv, tm - 1, 0))
        f2 = jnp.where(row == tm - 2, nxt1, jnp.where(row == tm - 1, nxt2, pltpu.roll(dconv, tm - 2, 0)))
        dcu = w_ref[2:3, :] * dconv + w_ref[1:2, :] * f1 + w_ref[0:1, :] * f2
        dgb_ref[...] = (g * conv * sz).astype(BF16)
        dgc_ref[...] = (dcu * u).astype(BF16)
        du_ref[...] = (dcu * gc).astype(BF16)
        dzb_ref[...] = (g * gb * conv * (sg * (1.0 + zb * (1.0 - sg)))).astype(BF16)
        w_row = lax.broadcasted_iota(jnp.int32, (3, LANES), 0)
        dw0 = jnp.sum(dconv * r2, axis=0, keepdims=True)
        dw1 = jnp.sum(dconv * r1, axis=0, keepdims=True)
        dw2 = jnp.sum(dconv * cu, axis=0, keepdims=True)
        dw_ref[...] += jnp.where(w_row == 0, dw0, jnp.where(w_row == 1, dw1, dw2))

    blk = pl.BlockSpec((tm, LANES), lambda c, i: (i, c))
    nxt = pl.BlockSpec((8, LANES), lambda c, i: (jnp.minimum((i + 1) * (tm // 8), s // 8 - 1), c))
    wspec = pl.BlockSpec((3, LANES), lambda c, i: (0, c))
    return pl.pallas_call(
        body, name="conv_bwd", grid=(CONV_W // LANES, s // tm),
        in_specs=[blk, nxt] + tiles + befores + afters + [wspec],
        out_specs=[blk, blk, blk, blk, wspec],
        out_shape=[jax.ShapeDtypeStruct((s, CONV_W), BF16)] * 4 + [jax.ShapeDtypeStruct((3, CONV_W), F32)],
        compiler_params=_params(("parallel", "arbitrary")),
    )(dob, dob, *([proj] * 8), conv_w)


def _piece_layout(pieces):
    offs, off = [], 0
    for p in pieces:
        offs.append((off, p.shape[1]))
        off += p.shape[1]
    assert off == N_ALL, off
    return offs


def _dw_in(h, pieces, chip_sums):
    s = h.shape[0]
    tk, tn = min(TK_DW, s), TN_DW
    nk = s // tk
    nn = N_MAIN // tn
    main, fpiece = pieces[:-1], pieces[-1]
    layout = _piece_layout(pieces)[:-1]
    n_main = len(main)
    nx = len(chip_sums)

    def body(*refs):
        p_refs, f_ref, h_ref = refs[:n_main], refs[n_main], refs[n_main + 1]
        ins, refs = refs[n_main + 2:n_main + 2 + nx], refs[n_main + 2 + nx:]
        out_ref, outf_ref = refs[:2]
        outs, (acc, accf, send_sems, recv_sems, local_sems) = refs[2:2 + nx], refs[2 + nx:]
        n, k = pl.program_id(0), pl.program_id(1)

        @pl.when(jnp.logical_and(n == 0, k == 0))
        def _():
            for cp in _chip_copies(ins, outs, send_sems, recv_sems, local_sems):
                cp.start()

        @pl.when(k == 0)
        def _():
            acc[...] = jnp.zeros_like(acc)
        hv = h_ref[pl.ds(pl.multiple_of(k * tk, tk), tk), :]
        for p_ref, (off, width) in zip(p_refs, layout):
            @pl.when(jnp.logical_and(n >= off // tn, n < (off + width) // tn))
            def _():
                acc[...] += _dot_tn(p_ref[...], hv)

        @pl.when(k == nk - 1)
        def _():
            out_ref[...] = acc[...].astype(BF16)

        @pl.when(n == 0)
        def _():
            @pl.when(k == 0)
            def _():
                accf[...] = jnp.zeros_like(accf)
            accf[...] += _dot_tn(f_ref[...], hv)

            @pl.when(k == nk - 1)
            def _():
                outf_ref[...] = accf[...].astype(BF16)

        @pl.when(jnp.logical_and(n == nn - 1, k == nk - 1))
        def _():
            for cp in _chip_copies(ins, outs, send_sems, recv_sems, local_sems):
                cp.wait()

    def piece_spec(off, width):
        lo, hi = off // tn, (off + width) // tn

        def index(n, k):
            active = jnp.logical_and(n >= lo, n < hi)
            return jnp.where(active, k, 0), jnp.clip(n - lo, 0, hi - lo - 1)
        return pl.BlockSpec((tk, tn), index)

    any_spec = pl.BlockSpec(memory_space=pl.ANY)
    res = pl.pallas_call(
        body, name="dw_in", grid=(nn, nk),
        in_specs=[piece_spec(off, width) for off, width in layout]
        + [pl.BlockSpec((tk, N_FPAD), lambda n, k: (jnp.where(n == 0, k, 0), 0)),
           pl.BlockSpec((s, D_MODEL), lambda n, k: (0, 0))] + [any_spec] * nx,
        out_specs=[pl.BlockSpec((tn, D_MODEL), lambda n, k: (n, 0)),
                   pl.BlockSpec((N_FPAD, D_MODEL), lambda n, k: (0, 0))] + [any_spec] * nx,
        out_shape=[jax.ShapeDtypeStruct((N_MAIN, D_MODEL), BF16), jax.ShapeDtypeStruct((N_FPAD, D_MODEL), BF16)]
        + [jax.ShapeDtypeStruct(a.shape, a.dtype) for a in chip_sums],
        scratch_shapes=[pltpu.VMEM((tn, D_MODEL), F32), pltpu.VMEM((N_FPAD, D_MODEL), F32),
                        pltpu.SemaphoreType.DMA((nx * 3,)), pltpu.SemaphoreType.DMA((nx * 3,)),
                        pltpu.SemaphoreType.DMA((nx,))],
        compiler_params=_params(("arbitrary", "arbitrary")),
    )(*main, fpiece, h, *chip_sums)
    return res[:2], res[2:]


def _dh_and_dx(pieces, w_all_t, x, dy, ada3, norm_g, chip_sums):
    s = x.shape[0]
    tm = min(TM_DH, s)
    nt = s // tm
    n = len(chip_sums)
    npc = len(pieces)
    layout = _piece_layout(pieces)

    def body(*refs):
        p_refs, refs = refs[:npc], refs[npc:]
        wt_ref, x_ref, dy_ref, ada_ref, g_ref = refs[:5]
        ins, refs = refs[5:5 + n], refs[5 + n:]
        gx_ref, dsh_ref, dsc_ref, dg_ref = refs[:4]
        outs, (send_sems, recv_sems, local_sems) = refs[4:4 + n], refs[4 + n:]
        i = pl.program_id(0)

        @pl.when(i == 0)
        def _():
            for cp in _chip_copies(ins, outs, send_sems, recv_sems, local_sems):
                cp.start()
            dsh_ref[...] = jnp.zeros_like(dsh_ref)
            dsc_ref[...] = jnp.zeros_like(dsc_ref)
            dg_ref[...] = jnp.zeros_like(dg_ref)

        dh = None
        for p_ref, (off, width) in zip(p_refs, layout):
            part = _dot(p_ref[...], wt_ref[off:off + width, :])
            dh = part if dh is None else dh + part
        xv = x_ref[...]
        r = lax.rsqrt(jnp.mean(xv * xv, axis=-1, keepdims=True) + EPS)
        xhat = xv * r
        g = g_ref[...]
        one_sc = 1.0 + ada_ref[1:2, :]
        dsh_ref[...] += jnp.sum(dh, axis=0, keepdims=True)
        dsc_ref[...] += jnp.sum(dh * (xhat * g), axis=0, keepdims=True)
        dg_ref[...] += jnp.sum(dh * xhat, axis=0, keepdims=True) * one_sc
        dxh = dh * (g * one_sc)
        dx = r * (dxh - xhat * jnp.mean(dxh * xhat, axis=-1, keepdims=True))
        gx_ref[...] = dy_ref[...] + dx

        @pl.when(i == nt - 1)
        def _():
            for cp in _chip_copies(ins, outs, send_sems, recv_sems, local_sems):
                cp.wait()

    full = pl.BlockSpec((tm, D_MODEL), lambda i: (i, 0))
    vec = pl.BlockSpec((1, D_MODEL), lambda i: (0, 0))
    any_spec = pl.BlockSpec(memory_space=pl.ANY)
    res = pl.pallas_call(
        body, name="dh_dx", grid=(nt,),
        in_specs=[pl.BlockSpec((tm, p.shape[1]), lambda i: (i, 0)) for p in pieces]
        + [pl.BlockSpec((N_ALL, D_MODEL), lambda i: (0, 0)), full, full,
           pl.BlockSpec((3, D_MODEL), lambda i: (0, 0)), vec] + [any_spec] * n,
        out_specs=[full, vec, vec, vec] + [any_spec] * n,
        out_shape=[jax.ShapeDtypeStruct((s, D_MODEL), F32)] + [jax.ShapeDtypeStruct((1, D_MODEL), F32)] * 3
        + [jax.ShapeDtypeStruct(a.shape, a.dtype) for a in chip_sums],
        scratch_shapes=[pltpu.SemaphoreType.DMA((n * 3,)), pltpu.SemaphoreType.DMA((n * 3,)),
                        pltpu.SemaphoreType.DMA((n,))],
        compiler_params=_params(("arbitrary",)),
    )(*pieces, w_all_t, x, dy, ada3, norm_g, *chip_sums)
    return res[:4], res[4:]


def _sum_small(vec_all, qg_parts, kg_parts):
    def body(v_ref, q_ref, k_ref, tot_ref, gq_ref, gk_ref):
        tot = v_ref[0:1, :]
        for p in range(1, N_DEV):
            tot = tot + v_ref[p:p + 1, :]
        tot_ref[...] = tot
        gq_ref[...] = jnp.sum(q_ref[...], axis=0, keepdims=True)
        gk_ref[...] = jnp.sum(k_ref[...], axis=0, keepdims=True)

    n = vec_all.shape[-1]
    return pl.pallas_call(
        body, name="sum_small",
        out_shape=[jax.ShapeDtypeStruct((1, n), F32),
                   jax.ShapeDtypeStruct((1, HEAD_DIM), F32), jax.ShapeDtypeStruct((1, HEAD_DIM), F32)],
        compiler_params=_params(),
    )(vec_all, qg_parts, kg_parts)


def _grad_w_ada(c_cols, dada_rows):
    def body(c_ref, d_ref, out_ref):
        acc = c_ref[0] * d_ref[0]
        for b in range(1, N_DEV):
            acc = acc + c_ref[b] * d_ref[b]
        out_ref[...] = acc

    return pl.pallas_call(
        body, name="grad_w_ada",
        out_shape=jax.ShapeDtypeStruct((D_MODEL, ADA_SHARD), F32),
        compiler_params=_params(),
    )(c_cols, dada_rows)


def _adam_step(w, m, v, g):
    c1 = 1.0 / (1.0 - ADAM_B1 ** ADAM_STEP)
    c2 = 1.0 / (1.0 - ADAM_B2 ** ADAM_STEP)
    m_new = ADAM_B1 * m + (1.0 - ADAM_B1) * g
    v_new = ADAM_B2 * v + (1.0 - ADAM_B2) * (g * g)
    return -ADAM_LR * ((m_new * c1) / (jnp.sqrt(v_new * c2) + ADAM_EPS) + ADAM_WD * w), m_new, v_new


def _adamw_small(params, name):
    n = len(params)

    def body(*refs):
        ins, outs = refs[:4 * n], refs[4 * n:]
        for k in range(n):
            w_ref, m_ref, v_ref, g_ref = ins[4 * k:4 * k + 4]
            d_ref, mo_ref, vo_ref = outs[3 * k:3 * k + 3]
            d_ref[...], mo_ref[...], vo_ref[...] = _adam_step(w_ref[...], m_ref[...], v_ref[...], g_ref[...])

    res = pl.pallas_call(
        body, name=name,
        out_shape=[jax.ShapeDtypeStruct(p[0].shape, F32) for p in params for _ in range(3)],
        compiler_params=_params(),
    )(*[a for p in params for a in p])
    return [(p[3],) + tuple(res[3 * k:3 * k + 3]) for k, p in enumerate(params)]


def _adamw(w, m, v, g_parts, name):
    rows, cols = w.shape
    n_parts = g_parts.shape[0]
    tr = 256 if rows % 256 == 0 else rows
    tc = 256 if (tr == rows and rows > 256 and cols % 256 == 0) else cols

    def body(w_ref, m_ref, v_ref, g_ref, go_ref, d_ref, mo_ref, vo_ref):
        g = g_ref[0].astype(F32)
        for p in range(1, n_parts):
            g = g + g_ref[p].astype(F32)
        go_ref[...] = g
        d_ref[...], mo_ref[...], vo_ref[...] = _adam_step(w_ref[...], m_ref[...], v_ref[...], g)

    blk = pl.BlockSpec((tr, tc), lambda i, j: (i, j))
    return pl.pallas_call(
        body, name=name, grid=(rows // tr, cols // tc),
        in_specs=[blk, blk, blk, pl.BlockSpec((n_parts, tr, tc), lambda i, j: (0, i, j))],
        out_specs=[blk] * 4,
        out_shape=[jax.ShapeDtypeStruct((rows, cols), F32)] * 4,
        compiler_params=_params(("parallel", "parallel")),
    )(w, m, v, g_parts)


_O_F = 1536


def _to_internal(wt_g):
    wf = wt_g.reshape(IN_WIDTH, D_MODEL)
    f = jnp.pad(wf[_O_F:_O_F + HEADS], ((0, N_FPAD - HEADS), (0, 0)))
    return jnp.concatenate([wf[:_O_F], wf[_O_F + HEADS:], f], axis=0)


def _slabs_by_core(dwt, dwt_f):
    sources = ((dwt, 0, _O_F, 0), (dwt_f, _O_F, _O_F + HEADS, _O_F), (dwt, _O_F + HEADS, IN_WIDTH, HEADS))

    def slab(p):
        lo, hi = p * IN_SHARD, (p + 1) * IN_SHARD
        parts = []
        for src, o_lo, o_hi, shift in sources:
            a, b = max(lo, o_lo), min(hi, o_hi)
            if a < b:
                parts.append(src[a - shift:b - shift])
        return parts[0] if len(parts) == 1 else jnp.concatenate(parts, axis=0)

    return jnp.stack([jnp.stack([slab(2 * chip + core) for chip in range(4)]) for core in range(2)])


def kernel(x, c, w_ada, b_ada, norm_g, w_in, b_f, q_norm_g, k_norm_g, conv_w, w_attn_out, w_conv_out, w_o, loss_target, m_w_ada, m_b_ada, m_norm_g, m_w_in, m_b_f, m_q_norm_g, m_k_norm_g, m_conv_w, m_w_attn_out, m_w_conv_out, m_w_o, v_w_ada, v_b_ada, v_norm_g, v_w_in, v_b_f, v_q_norm_g, v_k_norm_g, v_conv_w, v_w_attn_out, v_w_conv_out, v_w_o):
    me = 4 * lax.axis_index("x") + 2 * lax.axis_index("y") + lax.axis_index("c")
    s = x.shape[1]
    x2, t2 = x[0], loss_target[0]

    w_in_g, c_all, ada_g = _gather_weights_and_ada(w_in[0].T.astype(BF16), c, w_ada[0])
    ada_mine = lax.dynamic_index_in_dim(ada_g[:, :, 0, :], me, axis=1, keepdims=False)
    ada3 = (ada_mine.reshape(1, 3 * D_MODEL) + b_ada).reshape(3, D_MODEL)
    w_all_t = _to_internal(w_in_g)
    qg = jnp.tile(q_norm_g, (1, HEADS))
    kg = jnp.tile(k_norm_g, (1, HEADS))
    bf_pad = jnp.pad(b_f, ((0, 0), (0, LANES - HEADS)))

    (proj, fl, h), (qa, ka, va, kt, vt), (cw_g, wa_g, wb_g, wo_g) = _proj_fwd(
        x2, ada3, norm_g, w_all_t, bf_pad, qg, kg,
        [conv_w[0], w_attn_out[0].astype(BF16), w_conv_out[0].astype(BF16), w_o[0].astype(BF16)])
    wa = jnp.transpose(wa_g, (1, 0, 2)).reshape(ATTN_W, D_MODEL)
    wb = jnp.transpose(wb_g, (1, 0, 2)).reshape(CONV_W, D_MODEL)
    wo = wo_g.reshape(D_MODEL, D_MODEL)
    cw = jnp.transpose(cw_g, (1, 0, 2)).reshape(3, CONV_W)
    attn, oa, qb = _attn_fwd(qa, ka, vt, proj)
    (dy, dgab, do, dza, dob, dwo, dwa, dwb, dgate, loss_part) = _tail(oa, attn, proj, x2, t2, ada3, wa, wb, wo, cw)

    def by_core(slabs8):
        return jnp.swapaxes(slabs8.reshape((4, 2) + slabs8.shape[1:]), 0, 1).astype(BF16)

    core = lax.axis_index("c").astype(jnp.int32).reshape(1)
    small = [by_core(jnp.transpose(dwa.reshape(ATTN_W, N_DEV, LANES), (1, 0, 2))),
             by_core(jnp.transpose(dwb.reshape(CONV_W, N_DEV, LANES), (1, 0, 2))),
             by_core(dwo.reshape(N_DEV, D_MODEL // N_DEV, D_MODEL))]
    small_sums = [_pair_sum(m2, t4, core, "pair_sum_" + nm)
                  for m2, t4, nm in zip(small, _sibling_swap(small, "swap_small"), ("wa", "wb", "wo"))]
    dq, dk, dv, dqg, dkg, dcum = _attn_bwd(qb, ka, kt, va, do, proj, qg, kg)
    df, dbf = _forget_bwd(dcum, fl, bf_pad)
    dcb, dcc, dcu, dcz, dcw = _conv_bwd(dob, proj, cw)
    pieces = [dq, dk, dv, dza, dcb, dcc, dcu, dcz, dgab, df]
    (dw_main, dw_f), (g_wa_parts, g_wb_parts, g_wo_parts) = _dw_in(h, pieces, small_sums)

    slabs_in = _slabs_by_core(dw_main, dw_f)
    (theirs_in,) = _sibling_swap([slabs_in], "swap_w_in")
    (grad_x, dshift, dscale, dnormg), (g_in_parts,) = _dh_and_dx(
        pieces, w_all_t, x2, dy, ada3, norm_g, [_pair_sum(slabs_in, theirs_in, core, "pair_sum_w_in")])
    vec = jnp.concatenate([dshift, dscale, dgate, dnormg, dbf, dcw.reshape(1, 3 * CONV_W), loss_part, dqg, dkg],
                          axis=1)
    (vec_all,) = _gather_direct([vec], "gather_small")
    vec_all = vec_all.reshape(N_DEV, vec.shape[1])
    n_main = 4 * D_MODEL + LANES + 3 * CONV_W + LANES
    tot, g_qg, g_kg = _sum_small(
        vec_all[:, :n_main],
        vec_all[:, n_main:n_main + ATTN_W].reshape(N_DEV * HEADS, HEAD_DIM),
        vec_all[:, n_main + ATTN_W:].reshape(N_DEV * HEADS, HEAD_DIM))
    g_b_ada = tot[:, 0:3 * D_MODEL]
    g_norm_g = tot[:, 3 * D_MODEL:4 * D_MODEL]
    g_b_f = tot[:, 4 * D_MODEL:4 * D_MODEL + HEADS]
    g_cw_full = tot[:, 4 * D_MODEL + LANES:4 * D_MODEL + LANES + 3 * CONV_W].reshape(3, CONV_W)
    g_cw = lax.dynamic_slice(g_cw_full, (0, me * (CONV_W // N_DEV)), (3, CONV_W // N_DEV))
    dada_mine = lax.dynamic_slice(vec_all[:, 0:3 * D_MODEL], (0, me * ADA_SHARD), (N_DEV, ADA_SHARD))
    g_w_ada = _grad_w_ada(jnp.transpose(c_all, (0, 2, 1)), dada_mine.reshape(N_DEV, 1, ADA_SHARD))

    upd = {}
    upd["w_ada"] = _adamw(w_ada[0], m_w_ada[0], v_w_ada[0], g_w_ada[None], "adamw_w_ada")
    upd["w_in"] = [u.T for u in _adamw(w_in[0].T, m_w_in[0].T, v_w_in[0].T, g_in_parts, "adamw_w_in")]
    small_names = ["b_ada", "norm_g", "b_f", "q_norm_g", "k_norm_g", "conv_w"]
    small_upd = _adamw_small(
        [(b_ada, m_b_ada, v_b_ada, g_b_ada), (norm_g, m_norm_g, v_norm_g, g_norm_g), (b_f, m_b_f, v_b_f, g_b_f),
         (q_norm_g, m_q_norm_g, v_q_norm_g, g_qg), (k_norm_g, m_k_norm_g, v_k_norm_g, g_kg),
         (conv_w[0], m_conv_w[0], v_conv_w[0], g_cw)], "adamw_small")
    upd.update(zip(small_names, small_upd))
    upd["w_attn_out"] = _adamw(w_attn_out[0], m_w_attn_out[0], v_w_attn_out[0], g_wa_parts, "adamw_w_attn_out")
    upd["w_conv_out"] = _adamw(w_conv_out[0], m_w_conv_out[0], v_w_conv_out[0], g_wb_parts, "adamw_w_conv_out")
    upd["w_o"] = _adamw(w_o[0], m_w_o[0], v_w_o[0], g_wo_parts, "adamw_w_o")

    names = ["w_ada", "b_ada", "norm_g", "w_in", "b_f", "q_norm_g", "k_norm_g", "conv_w",
             "w_attn_out", "w_conv_out", "w_o"]
    lead = {"w_ada", "w_in", "conv_w", "w_attn_out", "w_conv_out", "w_o"}
    fix = lambda n, a: a[None] if n in lead else a
    loss = tot[0, n_main - LANES]
    outs = [loss, grad_x[None]]
    for k in range(4):
        outs += [fix(n, upd[n][k]) for n in names]
    return tuple(outs)
```

```python
import functools

import numpy as np
import jax
import jax.numpy as jnp
from jax import lax
from jax.experimental import pallas as pl
from jax.experimental.pallas import tpu as pltpu

F32 = jnp.float32
BF16 = jnp.bfloat16

D_MODEL = 1024
HEADS = 8
HEAD_DIM = 64
ATTN_W = 512
CONV_W = 512
N_DEV = 8
IN_WIDTH = 6152
IN_SHARD = IN_WIDTH // N_DEV
N_MAIN = 6144
N_FPAD = 128
N_ALL = N_MAIN + N_FPAD
ADA_SHARD = 3 * D_MODEL // N_DEV
EPS = 1e-6
NEG = -1e30

ADAM_LR = 0.001
ADAM_B1 = 0.9
ADAM_B2 = 0.999
ADAM_EPS = 1e-08
ADAM_WD = 0.01
ADAM_STEP = 10

LANES = 128
VMEM_LIMIT = 56 * 1024 * 1024

TM_PROJ = 256
TN_PROJ = 1024
TM_ELEM = 512
TQ = 512
HEADS_PER_STEP = 8
HEADS_PER_STEP_FWD = 8
TM_TAIL = 256
TC_CUM = 512
TK_DW = 2048
TN_DW = 512
TM_DH = 256
HALO = 16

OFF_Q, OFF_K, OFF_V, OFF_ZA, OFF_CB, OFF_CC, OFF_CU, OFF_CZ, OFF_GA, OFF_GB = (
    0, 512, 1024, 1536, 2048, 2560, 3072, 3584, 4096, 5120)


def _params(sem=None):
    return pltpu.CompilerParams(dimension_semantics=sem, vmem_limit_bytes=VMEM_LIMIT)


def _dot(a, b):
    return jnp.dot(a, b, preferred_element_type=F32)


def _dot_nt(a, b):
    return lax.dot_general(a, b, (((1,), (1,)), ((), ())), preferred_element_type=F32)


def _dot_tn(a, b):
    return lax.dot_general(a, b, (((0,), (0,)), ((), ())), preferred_element_type=F32)


def _sigmoid(x):
    return 1.0 / (1.0 + jnp.exp(-x))


def _lane_lo(shape):
    return lax.broadcasted_iota(jnp.int32, shape, len(shape) - 1) < HEAD_DIM


def _seg_sum(z, lo):
    a = jnp.sum(jnp.where(lo, z, 0.0), axis=-1, keepdims=True)
    b = jnp.sum(jnp.where(lo, 0.0, z), axis=-1, keepdims=True)
    return jnp.where(lo, a, b)


def _lane_col(z, lane):
    idx = lax.broadcasted_iota(jnp.int32, z.shape, 1)
    return jnp.sum(jnp.where(idx == lane, z, 0.0), axis=-1, keepdims=True)


def _sub_row(z, row):
    idx = lax.broadcasted_iota(jnp.int32, z.shape, 0)
    return jnp.sum(jnp.where(idx == row, z, 0.0), axis=0, keepdims=True)


def _mesh_pos():
    x, y, c = lax.axis_index("x"), lax.axis_index("y"), lax.axis_index("c")
    return x, y, c, 4 * x + 2 * y + c


def _peer(k, x, y, c):
    px = 1 - x if (k >> 2) & 1 else x
    py = 1 - y if (k >> 1) & 1 else y
    pc = 1 - c if k & 1 else c
    return (px, py, pc), 4 * px + 2 * py + pc


def _gather_copies(ins, outs, send_sems, recv_sems, local_sems):
    x, y, c, me = _mesh_pos()
    copies = []
    for a in range(len(ins)):
        copies.append(pltpu.make_async_copy(ins[a], outs[a].at[me], local_sems.at[a]))
        for k in range(1, N_DEV):
            dev, _ = _peer(k, x, y, c)
            copies.append(pltpu.make_async_remote_copy(
                src_ref=ins[a], dst_ref=outs[a].at[me],
                send_sem=send_sems.at[a * (N_DEV - 1) + k - 1], recv_sem=recv_sems.at[a * (N_DEV - 1) + k - 1],
                device_id=dev, device_id_type=pl.DeviceIdType.MESH))
    return copies


def _gather_sems(n):
    return [pltpu.SemaphoreType.DMA((n * (N_DEV - 1),)), pltpu.SemaphoreType.DMA((n * (N_DEV - 1),)),
            pltpu.SemaphoreType.DMA((n,))]


def _gather_direct(arrs, name):
    n = len(arrs)
    any_spec = pl.BlockSpec(memory_space=pl.ANY)

    def body(*refs):
        copies = _gather_copies(refs[:n], refs[n:2 * n], *refs[2 * n:])
        for cp in copies:
            cp.start()
        for cp in copies:
            cp.wait()

    return pl.pallas_call(
        body, name=name, out_shape=[jax.ShapeDtypeStruct((N_DEV,) + a.shape, a.dtype) for a in arrs],
        in_specs=[any_spec] * n, out_specs=[any_spec] * n, scratch_shapes=_gather_sems(n),
    )(*arrs)


def _ada_phase(c_ref, w_ref, call_ref, adag_ref, mine_ref, send_sems, recv_sems):
    x, y, c, me = _mesh_pos()

    def copy(phase, k, src, dst):
        dev, _ = _peer(k, x, y, c)
        return pltpu.make_async_remote_copy(
            src_ref=src, dst_ref=dst,
            send_sem=send_sems.at[phase * (N_DEV - 1) + k - 1],
            recv_sem=recv_sems.at[phase * (N_DEV - 1) + k - 1],
            device_id=dev, device_id_type=pl.DeviceIdType.MESH)

    call_ref[me] = c_ref[...]
    first = [copy(0, k, c_ref, call_ref.at[me]) for k in range(1, N_DEV)]
    for cp in first:
        cp.start()
    for cp in first:
        cp.wait()
    wb = w_ref[...].astype(BF16)
    for b in range(N_DEV):
        row = jnp.broadcast_to(call_ref[b], (8, D_MODEL)).astype(BF16)
        mine_ref[b] = _sub_row(_dot(row, wb), 0)
    adag_ref[me] = mine_ref[...]
    second = [copy(1, k, mine_ref, adag_ref.at[me]) for k in range(1, N_DEV)]
    for cp in second:
        cp.start()
    for cp in second:
        cp.wait()


def _gather_weights_and_ada(wt_shard, c_row, w_ada_sh):
    any_spec = pl.BlockSpec(memory_space=pl.ANY)
    vm = pl.BlockSpec(memory_space=pltpu.VMEM)

    def body(w_in_ref, c_ref, wada_ref, out_ref, call_ref, adag_ref, mine_ref, send_sems, recv_sems, local_sem,
             ada_send, ada_recv):
        x, y, c, me = _mesh_pos()
        sibling = (x, y, 1 - c)
        chips = [(1 - x, y), (x, 1 - y), (1 - x, 1 - y)]

        def copy(k, src, blk, to):
            return pltpu.make_async_remote_copy(
                src_ref=src, dst_ref=out_ref.at[blk], send_sem=send_sems.at[k], recv_sem=recv_sems.at[k],
                device_id=to, device_id_type=pl.DeviceIdType.MESH)

        local = pltpu.make_async_copy(w_in_ref, out_ref.at[me], local_sem.at[0])
        local.start()
        first = [copy(0, w_in_ref, me, sibling)]
        first += [copy(1 + j, w_in_ref, me, (px, py, c)) for j, (px, py) in enumerate(chips)]
        for cp in first:
            cp.start()
        _ada_phase(c_ref, wada_ref, call_ref, adag_ref, mine_ref, ada_send, ada_recv)
        passed = []
        for j, (px, py) in enumerate(chips):
            blk = 4 * px + 2 * py + c
            copy(1 + j, w_in_ref, blk, (x, y, c)).wait_recv()
            fwd = copy(4 + j, out_ref.at[blk], blk, sibling)
            fwd.start()
            passed.append(fwd)
        copy(0, w_in_ref, 4 * x + 2 * y + 1 - c, (x, y, c)).wait_recv()
        for j, (px, py) in enumerate(chips):
            copy(4 + j, w_in_ref, 4 * px + 2 * py + 1 - c, (x, y, c)).wait_recv()
        for cp in first + passed:
            cp.wait_send()
        local.wait()

    per = N_DEV - 1
    return pl.pallas_call(
        body, name="gather_weights",
        out_shape=[jax.ShapeDtypeStruct((N_DEV,) + wt_shard.shape, wt_shard.dtype),
                   jax.ShapeDtypeStruct((N_DEV, 1, D_MODEL), F32),
                   jax.ShapeDtypeStruct((N_DEV, N_DEV, 1, ADA_SHARD), F32)],
        in_specs=[any_spec, vm, vm], out_specs=[any_spec, vm, vm],
        scratch_shapes=[pltpu.VMEM((N_DEV, 1, ADA_SHARD), F32),
                        pltpu.SemaphoreType.DMA((per,)), pltpu.SemaphoreType.DMA((per,)),
                        pltpu.SemaphoreType.DMA((1,)),
                        pltpu.SemaphoreType.DMA((2 * per,)), pltpu.SemaphoreType.DMA((2 * per,))],
        compiler_params=pltpu.CompilerParams(vmem_limit_bytes=VMEM_LIMIT),
    )(wt_shard, c_row, w_ada_sh)


def _sibling_swap(arrs, name):
    n = len(arrs)
    any_spec = pl.BlockSpec(memory_space=pl.ANY)

    def body(*refs):
        ins, outs = refs[:n], refs[n:2 * n]
        send_sems, recv_sems = refs[2 * n:]
        x, y, c, _ = _mesh_pos()
        copies = [pltpu.make_async_remote_copy(
            src_ref=ins[a].at[1 - c], dst_ref=outs[a], send_sem=send_sems.at[a], recv_sem=recv_sems.at[a],
            device_id=(x, y, 1 - c), device_id_type=pl.DeviceIdType.MESH) for a in range(n)]
        for cp in copies:
            cp.start()
        for cp in copies:
            cp.wait()

    return pl.pallas_call(
        body, name=name,
        out_shape=[jax.ShapeDtypeStruct(a.shape[1:], a.dtype) for a in arrs],
        in_specs=[any_spec] * n, out_specs=[any_spec] * n,
        scratch_shapes=[pltpu.SemaphoreType.DMA((n,)), pltpu.SemaphoreType.DMA((n,))],
    )(*arrs)


def _pair_sum(mine2, theirs, core, name):
    _, _, rows, cols = mine2.shape
    tr = 256 if rows % 256 == 0 else rows

    def body(core_ref, a_ref, b_ref, out_ref):
        out_ref[...] = (a_ref[...].astype(F32) + b_ref[...].astype(F32)).astype(BF16)

    return pl.pallas_call(
        body, name=name,
        grid_spec=pltpu.PrefetchScalarGridSpec(
            num_scalar_prefetch=1, grid=(4, rows // tr),
            in_specs=[pl.BlockSpec((None, None, tr, cols), lambda ch, i, core_: (core_[0], ch, i, 0)),
                      pl.BlockSpec((None, tr, cols), lambda ch, i, core_: (ch, i, 0))],
            out_specs=pl.BlockSpec((None, tr, cols), lambda ch, i, core_: (ch, i, 0))),
        out_shape=jax.ShapeDtypeStruct(theirs.shape, BF16),
        compiler_params=_params(("parallel", "parallel")),
    )(core, mine2, theirs)


def _all_to_all_copies(ins, outs, send_sems, recv_sems, local_sems):
    x, y, c, me = _mesh_pos()
    copies = []
    for a in range(len(ins)):
        copies.append(pltpu.make_async_copy(ins[a].at[me], outs[a].at[me], local_sems.at[a]))
        for k in range(1, N_DEV):
            dev, p = _peer(k, x, y, c)
            copies.append(pltpu.make_async_remote_copy(
                src_ref=ins[a].at[p], dst_ref=outs[a].at[me],
                send_sem=send_sems.at[a * (N_DEV - 1) + k - 1], recv_sem=recv_sems.at[a * (N_DEV - 1) + k - 1],
                device_id=dev, device_id_type=pl.DeviceIdType.MESH))
    return copies


def _chip_copies(ins, outs, send_sems, recv_sems, local_sems):
    x, y, c, _ = _mesh_pos()
    my_chip = 2 * x + y
    chips = [(1 - x, y), (x, 1 - y), (1 - x, 1 - y)]
    copies = []
    for a in range(len(ins)):
        copies.append(pltpu.make_async_copy(ins[a].at[my_chip], outs[a].at[my_chip], local_sems.at[a]))
        for j, (px, py) in enumerate(chips):
            copies.append(pltpu.make_async_remote_copy(
                src_ref=ins[a].at[2 * px + py], dst_ref=outs[a].at[my_chip],
                send_sem=send_sems.at[a * 3 + j], recv_sem=recv_sems.at[a * 3 + j],
                device_id=(px, py, c), device_id_type=pl.DeviceIdType.MESH))
    return copies


def _proj_fwd(x, ada3, norm_g, w_all_t, bf_pad, qg, kg, later):
    s = x.shape[0]
    tm, tn = min(TM_PROJ, s), TN_PROJ
    nt = s // tm
    n = len(later)

    def body(x_ref, ada_ref, g_ref, wt_ref, bf_ref, qg_ref, kg_ref, *rest):
        ins, (proj_ref, fl_ref, h_ref), rows_refs, rest = rest[:n], rest[n:n + 3], rest[n + 3:n + 8], rest[n + 8:]
        outs, (carry, send_sems, recv_sems, local_sems) = rest[:n], rest[n:]
        i = pl.program_id(0)

        @pl.when(i == 0)
        def _():
            carry[...] = jnp.zeros_like(carry)
            for cp in _gather_copies(ins, outs, send_sems, recv_sems, local_sems):
                cp.start()

        xv = x_ref[...]
        r = lax.rsqrt(jnp.mean(xv * xv, axis=-1, keepdims=True) + EPS)
        hv = ((xv * r) * g_ref[...]) * (1.0 + ada_ref[1:2, :]) + ada_ref[0:1, :]
        hb = hv.astype(BF16)
        h_ref[...] = hb
        fl = _dot_nt(hb, wt_ref[N_MAIN:N_ALL, :])
        fl_ref[...] = fl
        for j in range(N_MAIN // tn):
            proj_ref[:, j * tn:(j + 1) * tn] = _dot_nt(hb, wt_ref[j * tn:(j + 1) * tn, :]).astype(BF16)
        _attention_rows(proj_ref, fl, bf_ref, qg_ref, kg_ref, carry, *rows_refs)

        @pl.when(i == nt - 1)
        def _():
            for cp in _gather_copies(ins, outs, send_sems, recv_sems, local_sems):
                cp.wait()

    any_spec = pl.BlockSpec(memory_space=pl.ANY)
    heads = pl.BlockSpec((HEADS, tm, LANES), lambda i: (0, i, 0))
    heads_t = pl.BlockSpec((HEADS, LANES, tm), lambda i: (0, 0, i))
    vec = pl.BlockSpec((1, ATTN_W), lambda i: (0, 0))
    res = pl.pallas_call(
        body, name="proj_fwd", grid=(nt,),
        in_specs=[pl.BlockSpec((tm, D_MODEL), lambda i: (i, 0)),
                  pl.BlockSpec((3, D_MODEL), lambda i: (0, 0)),
                  pl.BlockSpec((1, D_MODEL), lambda i: (0, 0)),
                  pl.BlockSpec((N_ALL, D_MODEL), lambda i: (0, 0)),
                  pl.BlockSpec((1, LANES), lambda i: (0, 0)), vec, vec] + [any_spec] * n,
        out_specs=[pl.BlockSpec((tm, N_MAIN), lambda i: (i, 0)),
                   pl.BlockSpec((tm, N_FPAD), lambda i: (i, 0)),
                   pl.BlockSpec((tm, D_MODEL), lambda i: (i, 0)),
                   heads, heads, heads, heads_t, heads_t] + [any_spec] * n,
        out_shape=[jax.ShapeDtypeStruct((s, N_MAIN), BF16),
                   jax.ShapeDtypeStruct((s, N_FPAD), F32),
                   jax.ShapeDtypeStruct((s, D_MODEL), BF16)]
        + [jax.ShapeDtypeStruct((HEADS, s, LANES), BF16)] * 3
        + [jax.ShapeDtypeStruct((HEADS, LANES, s), BF16)] * 2
        + [jax.ShapeDtypeStruct((N_DEV,) + a.shape, a.dtype) for a in later],
        scratch_shapes=[pltpu.VMEM((1, LANES), F32)] + _gather_sems(n),
        compiler_params=_params(("arbitrary",)),
    )(x, ada3, norm_g, w_all_t, bf_pad, qg, kg, *later)
    return res[:3], res[3:8], res[8:]


L_ONE_Q, L_F_Q, L_LSE_Q, L_END = HEAD_DIM, HEAD_DIM + 3, HEAD_DIM + 6, HEAD_DIM + 9


def _split3(f):
    hi = f.astype(BF16).astype(F32)
    r = f - hi
    mid = r.astype(BF16).astype(F32)
    return hi, mid, r - mid


def _place3(lane, first, parts, otherwise):
    a, b, c = parts
    return jnp.where(lane == first, a, jnp.where(lane == first + 1, b, jnp.where(lane == first + 2, c, otherwise)))


def _log_forget(fl, bf):
    z = fl + bf
    lf = jnp.minimum(z, 0.0) - jnp.log1p(jnp.exp(-jnp.abs(z)))
    lane = lax.broadcasted_iota(jnp.int32, z.shape, 1)
    return jnp.where(lane < HEADS, lf, 0.0)


def _attention_rows(p_ref, fl, bf_ref, qg_ref, kg_ref, carry, qa_ref, ka_ref, va_ref, kt_ref, vt_ref):
    tm = fl.shape[0]
    scale = HEAD_DIM ** -0.5
    tri = (lax.broadcasted_iota(jnp.int32, (tm, tm), 1) <= lax.broadcasted_iota(jnp.int32, (tm, tm), 0)).astype(F32)
    cum_v = jnp.dot(tri, _log_forget(fl, bf_ref[...]), preferred_element_type=F32,
                    precision=lax.Precision.HIGHEST) + carry[...]
    carry[...] = _sub_row(cum_v, tm - 1)
    lane = lax.broadcasted_iota(jnp.int32, (tm, LANES), 1)
    lo = lane < HEAD_DIM
    v_tail = jnp.where(lane < L_F_Q, 1.0, 0.0)
    for pr in range(ATTN_W // LANES):
        sl = slice(pr * LANES, (pr + 1) * LANES)
        q2 = p_ref[:, OFF_Q + pr * LANES:OFF_Q + (pr + 1) * LANES].astype(F32)
        k2 = p_ref[:, OFF_K + pr * LANES:OFF_K + (pr + 1) * LANES].astype(F32)
        v2 = p_ref[:, OFF_V + pr * LANES:OFF_V + (pr + 1) * LANES].astype(F32)
        rq = lax.rsqrt(_seg_sum(q2 * q2, lo) * (1.0 / HEAD_DIM) + EPS)
        rk = lax.rsqrt(_seg_sum(k2 * k2, lo) * (1.0 / HEAD_DIM) + EPS)
        qn = ((q2 * rq) * qg_ref[:, sl]) * scale
        kn = (k2 * rk) * kg_ref[:, sl]
        for hh in range(2):
            h = 2 * pr + hh
            f3 = _split3(_lane_col(cum_v, h))
            qh = qn if hh == 0 else pltpu.roll(qn, HEAD_DIM, 1)
            kh = kn if hh == 0 else pltpu.roll(kn, HEAD_DIM, 1)
            vh = v2 if hh == 0 else pltpu.roll(v2, HEAD_DIM, 1)
            q_tail = jnp.where(lane < L_F_Q, 1.0, _place3(lane, L_F_Q, f3, 0.0))
            k_tail = _place3(lane, L_ONE_Q, tuple(-f for f in f3), jnp.where(lane < L_END, 1.0, 0.0))
            k_row = jnp.where(lo, kh, k_tail)
            v_row = jnp.where(lo, vh, v_tail)
            qa_ref[h] = jnp.where(lo, qh, q_tail).astype(BF16)
            ka_ref[h] = k_row.astype(BF16)
            va_ref[h] = v_row.astype(BF16)
            kt_ref[h] = k_row.T.astype(BF16)
            vt_ref[h] = v_row.T.astype(BF16)


def _causal_t(t):
    return lax.broadcasted_iota(jnp.int32, (t, t), 0) <= lax.broadcasted_iota(jnp.int32, (t, t), 1)


def _tri_steps(nt, q_major):
    if q_major:
        pairs = [(i, j) for i in range(nt) for j in range(i + 1)]
    else:
        pairs = [(i, j) for j in range(nt) for i in range(j, nt)]
    return (jnp.asarray(np.array([p[0] for p in pairs], np.int32)),
            jnp.asarray(np.array([p[1] for p in pairs], np.int32)))


def _attn_fwd(qa, ka, vt, proj):
    s = qa.shape[1]
    t = min(TQ, s)
    it, jt = _tri_steps(s // t, True)
    hp = HEADS_PER_STEP_FWD
    wide = hp * HEAD_DIM
    za_blk = OFF_ZA // wide

    def body(it_ref, jt_ref, q_ref, k_ref, vt_ref, za_ref, attn_ref, oa_ref, qb_ref, m_s, acc_s, pair_s):
        step = pl.program_id(1)
        i, j = it_ref[step], jt_ref[step]

        @pl.when(j == 0)
        def _():
            m_s[...] = jnp.full_like(m_s, NEG)
            acc_s[...] = jnp.zeros_like(acc_s)

        def update(masked):
            for hh in range(hp):
                st = _dot_nt(k_ref[hh], q_ref[hh])
                if masked:
                    st = jnp.where(_causal_t(t), st, NEG)
                m_prev = m_s[hh]
                m_next = jnp.maximum(m_prev, jnp.max(st, axis=0, keepdims=True))
                alpha = jnp.exp(m_prev - m_next)
                pt = jnp.exp(st - m_next).astype(BF16)
                acc_s[hh] = acc_s[hh] * alpha + _dot(vt_ref[hh], pt)
                m_s[hh] = m_next

        @pl.when(j < i)
        def _():
            update(False)

        @pl.when(j == i)
        def _():
            update(True)
            row = lax.broadcasted_iota(jnp.int32, (LANES, t), 0)
            lane = lax.broadcasted_iota(jnp.int32, (t, LANES), 1)
            for hh in range(hp):
                l_row = acc_s[hh, L_ONE_Q:L_ONE_Q + 1, :]
                pair_s[hh * HEAD_DIM:(hh + 1) * HEAD_DIM, :] = acc_s[hh, 0:HEAD_DIM, :] / l_row
                lse3 = _split3(m_s[hh] + jnp.log(l_row))
                tail_t = _place3(row, L_LSE_Q, tuple(-x for x in lse3), 0.0)
                keep_q = jnp.logical_or(lane < L_LSE_Q, lane >= L_END)
                qb_ref[hh] = jnp.where(keep_q, q_ref[hh].astype(F32), tail_t.T).astype(BF16)
            out = pair_s[...].T
            attn_ref[...] = out
            z = za_ref[...].astype(F32)
            oa_ref[...] = (out * (z * _sigmoid(z))).astype(BF16)

    pair_q = pl.BlockSpec((hp, t, LANES), lambda p, n, it_, jt_: (p, it_[n], 0))
    pair_k = pl.BlockSpec((hp, t, LANES), lambda p, n, it_, jt_: (p, jt_[n], 0))
    pair_kt = pl.BlockSpec((hp, LANES, t), lambda p, n, it_, jt_: (p, 0, jt_[n]))
    out_q = pl.BlockSpec((t, wide), lambda p, n, it_, jt_: (it_[n], p))
    return pl.pallas_call(
        body, name="attn_fwd",
        grid_spec=pltpu.PrefetchScalarGridSpec(
            num_scalar_prefetch=2, grid=(HEADS // hp, it.shape[0]),
            in_specs=[pair_q, pair_k, pair_kt,
                      pl.BlockSpec((t, wide), lambda p, n, it_, jt_: (it_[n], za_blk + p))],
            out_specs=[out_q, out_q, pair_q],
            scratch_shapes=[pltpu.VMEM((hp, 1, t), F32), pltpu.VMEM((hp, LANES, t), F32),
                            pltpu.VMEM((wide, t), F32)]),
        out_shape=[jax.ShapeDtypeStruct((s, ATTN_W), F32),
                   jax.ShapeDtypeStruct((s, ATTN_W), BF16),
                   jax.ShapeDtypeStruct((HEADS, s, LANES), BF16)],
        compiler_params=_params(("parallel", "arbitrary")),
    )(it, jt, qa, ka, vt, proj)


def _conv_parts(gb_ref, gc_ref, u_ref, zb_ref, gch_ref, uh_ref, first, w_ref, tm):
    gb, gc = gb_ref[...].astype(F32), gc_ref[...].astype(F32)
    u, zb = u_ref[...].astype(F32), zb_ref[...].astype(F32)
    cu = gc * u
    cu_h = jnp.where(first, 0.0, gch_ref[...].astype(F32) * uh_ref[...].astype(F32))
    prev1, prev2 = _sub_row(cu_h, HALO - 1), _sub_row(cu_h, HALO - 2)
    row = lax.broadcasted_iota(jnp.int32, cu.shape, 0)
    r1 = jnp.where(row == 0, prev1, pltpu.roll(cu, 1, 0))
    r2 = jnp.where(row == 0, prev2, jnp.where(row == 1, prev1, pltpu.roll(cu, 2, 0)))
    conv = w_ref[2:3, :] * cu + w_ref[1:2, :] * r1 + w_ref[0:1, :] * r2
    return gb, gc, u, zb, cu, r1, r2, conv


def _conv_specs(tm, s, width=LANES):
    def tile(off):
        return pl.BlockSpec((tm, width), lambda c, i: (i, off // width + c))

    def before(off):
        return pl.BlockSpec((HALO, width), lambda c, i: (jnp.maximum(i * (tm // HALO) - 1, 0), off // width + c))

    def after(off):
        return pl.BlockSpec((HALO, width),
                            lambda c, i: (jnp.minimum((i + 1) * (tm // HALO), s // HALO - 1), off // width + c))

    return ([tile(OFF_CB), tile(OFF_CC), tile(OFF_CU), tile(OFF_CZ)], [before(OFF_CC), before(OFF_CU)],
            [after(OFF_CB), after(OFF_CZ)])


def _tail(oa, attn, proj, x, target, ada3, wa, wb, wo, conv_w):
    s = x.shape[0]
    tm = min(TM_TAIL, s)
    gab_blk = OFF_GA // (2 * D_MODEL)
    za_blk = OFF_ZA // ATTN_W
    tiles, befores, _ = _conv_specs(tm, s, CONV_W)

    def body(oa_ref, attn_ref, za_ref, gb_ref, gc_ref, u_ref, zb_ref, gch_ref, uh_ref, cw_ref, gab_ref, x_ref, t_ref,
             ada_ref, wa_ref, wb_ref, wo_ref,
             dy_ref, dgab_ref, do_ref, dza_ref, dob_ref, dwo_ref, dwa_ref, dwb_ref, dgate_ref, loss_ref):
        first = pl.program_id(0) == 0

        @pl.when(first)
        def _():
            dwo_ref[...] = jnp.zeros_like(dwo_ref)
            dwa_ref[...] = jnp.zeros_like(dwa_ref)
            dwb_ref[...] = jnp.zeros_like(dwb_ref)
            dgate_ref[...] = jnp.zeros_like(dgate_ref)
            loss_ref[...] = jnp.zeros_like(loss_ref)

        gb, _, _, zb, _, _, _, conv = _conv_parts(gb_ref, gc_ref, u_ref, zb_ref, gch_ref, uh_ref, first, cw_ref, tm)
        ob_v = (gb * conv * (zb * _sigmoid(zb))).astype(BF16)
        oa_v = oa_ref[...]
        wa_v, wb_v, wo_v = wa_ref[...], wb_ref[...], wo_ref[...]
        a2 = _dot(oa_v, wa_v)
        b2 = _dot(ob_v, wb_v)
        sa = _sigmoid(gab_ref[:, 0:D_MODEL].astype(F32))
        sb = _sigmoid(gab_ref[:, D_MODEL:2 * D_MODEL].astype(F32))
        mb = (sa * a2 + sb * b2).astype(BF16)
        mo = _dot(mb, wo_v)
        gate = ada_ref[2:3, :]
        err = (x_ref[...] + gate * mo) - t_ref[...]
        dy = err * (1.0 / D_MODEL)
        dy_ref[...] = dy
        loss_ref[...] += 0.5 * jnp.sum(err * err) * (1.0 / D_MODEL)
        dgate_ref[...] += jnp.sum(dy * mo, axis=0, keepdims=True)
        dmo = (dy * gate).astype(BF16)
        dmerged = _dot_nt(dmo, wo_v)
        dwo_ref[...] += _dot_tn(mb, dmo)
        da2 = (dmerged * sa).astype(BF16)
        db2 = (dmerged * sb).astype(BF16)
        dgab_ref[:, 0:D_MODEL] = (dmerged * a2 * (sa * (1.0 - sa))).astype(BF16)
        dgab_ref[:, D_MODEL:2 * D_MODEL] = (dmerged * b2 * (sb * (1.0 - sb))).astype(BF16)
        doa = _dot_nt(da2, wa_v)
        dob_ref[...] = _dot_nt(db2, wb_v)
        dwa_ref[...] += _dot_tn(oa_v, da2)
        dwb_ref[...] += _dot_tn(ob_v, db2)

        lane = lax.broadcasted_iota(jnp.int32, (tm, LANES), 1)
        lo = lane < HEAD_DIM
        for pr in range(ATTN_W // LANES):
            sl = slice(pr * LANES, (pr + 1) * LANES)
            g, a, z = doa[:, sl], attn_ref[:, sl], za_ref[:, sl].astype(F32)
            sg = _sigmoid(z)
            dat = (g * (z * sg)).astype(BF16).astype(F32)
            prod = dat * a
            dza_ref[:, sl] = (g * a * (sg * (1.0 + z * (1.0 - sg)))).astype(BF16)
            for hh in range(2):
                sel = lo if hh == 0 else jnp.logical_not(lo)
                delta3 = _split3(jnp.sum(jnp.where(sel, prod, 0.0), axis=-1, keepdims=True))
                dh = dat if hh == 0 else pltpu.roll(dat, HEAD_DIM, 1)
                tail_lanes = _place3(lane, L_ONE_Q, tuple(-d for d in delta3), 0.0)
                do_ref[2 * pr + hh] = jnp.where(lo, dh, tail_lanes).astype(BF16)

    half = pl.BlockSpec((tm, ATTN_W), lambda i: (i, 0))
    full = pl.BlockSpec((tm, D_MODEL), lambda i: (i, 0))

    def const(shape):
        return pl.BlockSpec(shape, lambda i: (0, 0))

    def one_axis(spec):
        return pl.BlockSpec(spec.block_shape, lambda i, f=spec.index_map: f(0, i))

    return pl.pallas_call(
        body, name="tail", grid=(s // tm,),
        in_specs=[half, half, pl.BlockSpec((tm, ATTN_W), lambda i: (i, za_blk))]
        + [one_axis(sp) for sp in tiles + befores]
        + [const((3, CONV_W)), pl.BlockSpec((tm, 2 * D_MODEL), lambda i: (i, gab_blk)), full, full,
           const((3, D_MODEL)), const((ATTN_W, D_MODEL)), const((CONV_W, D_MODEL)), const((D_MODEL, D_MODEL))],
        out_specs=[full, pl.BlockSpec((tm, 2 * D_MODEL), lambda i: (i, 0)),
                   pl.BlockSpec((HEADS, tm, LANES), lambda i: (0, i, 0)), half, half,
                   const((D_MODEL, D_MODEL)), const((ATTN_W, D_MODEL)), const((CONV_W, D_MODEL)),
                   const((1, D_MODEL)), const((1, LANES))],
        out_shape=[jax.ShapeDtypeStruct((s, D_MODEL), F32),
                   jax.ShapeDtypeStruct((s, 2 * D_MODEL), BF16),
                   jax.ShapeDtypeStruct((HEADS, s, LANES), BF16),
                   jax.ShapeDtypeStruct((s, ATTN_W), BF16),
                   jax.ShapeDtypeStruct((s, CONV_W), F32),
                   jax.ShapeDtypeStruct((D_MODEL, D_MODEL), F32),
                   jax.ShapeDtypeStruct((ATTN_W, D_MODEL), F32),
                   jax.ShapeDtypeStruct((CONV_W, D_MODEL), F32),
                   jax.ShapeDtypeStruct((1, D_MODEL), F32),
                   jax.ShapeDtypeStruct((1, LANES), F32)],
        compiler_params=_params(("arbitrary",)),
    )(oa, attn, proj, *([proj] * 6), conv_w, proj, x, target, ada3, wa, wb, wo)


def _attn_bwd(qb, ka, kt, va, do, proj, qg, kg):
    s = qb.shape[1]
    t = min(TQ, s)
    nt = s // t
    hp = HEADS_PER_STEP
    wide = hp * HEAD_DIM
    scale = HEAD_DIM ** -0.5
    it, jt = _tri_steps(nt, False)

    def body(it_ref, jt_ref, q_ref, k_ref, kt_ref, v_ref, do_ref, qraw_ref, kraw_ref, qg_ref, kg_ref,
             dq_ref, dk_ref, dv_ref, dqg_ref, dkg_ref, dcum_ref, dqt_s, dk_s, dv_s, rows_s):
        grp, step = pl.program_id(0), pl.program_id(1)
        i, j = it_ref[step], jt_ref[step]
        lane = lax.broadcasted_iota(jnp.int32, (t, LANES), 1)
        lo = lane < HEAD_DIM

        @pl.when(step == 0)
        def _():
            dqt_s[...] = jnp.zeros_like(dqt_s)
            dqg_ref[...] = jnp.zeros_like(dqg_ref)
            dkg_ref[...] = jnp.zeros_like(dkg_ref)

        @pl.when(i == j)
        def _():
            dk_s[...] = jnp.zeros_like(dk_s)
            dv_s[...] = jnp.zeros_like(dv_s)

        def update(masked):
            for hh in range(hp):
                qh, doh = q_ref[hh], do_ref[hh]
                st = _dot_nt(k_ref[hh], qh)
                if masked:
                    st = jnp.where(_causal_t(t), st, NEG)
                pt = jnp.exp(st)
                dst = (pt * _dot_nt(v_ref[hh], doh)).astype(BF16)
                dv_s[hh] += _dot(pt.astype(BF16), doh)
                dk_s[hh] += _dot(dst, qh)
                dqt_s[hh, i] += _dot(kt_ref[hh], dst)

        def pair(a, b):
            return jnp.where(lo, a, pltpu.roll(b, HEAD_DIM, 1))

        def norm_bwd(raw, dy, g, dg_ref, out_ref, sl):
            r = lax.rsqrt(_seg_sum(raw * raw, lo) * (1.0 / HEAD_DIM) + EPS)
            xhat = raw * r
            dg_ref[:, sl] += jnp.sum(dy * xhat, axis=0, keepdims=True)
            dxh = dy * g
            dx = r * (dxh - xhat * (_seg_sum(dxh * xhat, lo) * (1.0 / HEAD_DIM)))
            out_ref[:, sl] = dx.astype(BF16)

        @pl.when(i > j)
        def _():
            update(False)

        @pl.when(i == j)
        def _():
            update(True)
            dq_rows = [dqt_s[hh, i].T for hh in range(hp)]
            rows = jnp.zeros((t, LANES), F32)
            for hh in range(hp):
                rows = jnp.where(lane == grp * hp + hh, _lane_col(dq_rows[hh], L_F_Q), rows)
            rows_s[...] = rows
            for pr in range(hp // 2):
                sl = slice(pr * LANES, (pr + 1) * LANES)
                norm_bwd(qraw_ref[:, sl].astype(F32), pair(dq_rows[2 * pr], dq_rows[2 * pr + 1]) * scale,
                         qg_ref[:, sl], dqg_ref, dq_ref, sl)

        @pl.when(i == nt - 1)
        def _():
            dcum = rows_s[...]
            for hh in range(hp):
                dcum = jnp.where(lane == grp * hp + hh, dcum - _lane_col(dk_s[hh], L_ONE_Q), dcum)
            dcum_ref[0] = dcum
            for pr in range(hp // 2):
                sl = slice(pr * LANES, (pr + 1) * LANES)
                norm_bwd(kraw_ref[:, sl].astype(F32), pair(dk_s[2 * pr], dk_s[2 * pr + 1]),
                         kg_ref[:, sl], dkg_ref, dk_ref, sl)
                dv_ref[:, sl] = pair(dv_s[2 * pr], dv_s[2 * pr + 1]).astype(BF16)

    pair_q = pl.BlockSpec((hp, t, LANES), lambda p, n, it_, jt_: (p, it_[n], 0))
    pair_k = pl.BlockSpec((hp, t, LANES), lambda p, n, it_, jt_: (p, jt_[n], 0))
    pair_kt = pl.BlockSpec((hp, LANES, t), lambda p, n, it_, jt_: (p, 0, jt_[n]))
    tok = pl.BlockSpec((t, wide), lambda p, n, it_, jt_: (jt_[n], p))
    gain = pl.BlockSpec((1, wide), lambda p, n, it_, jt_: (0, p))
    return pl.pallas_call(
        body, name="attn_bwd",
        grid_spec=pltpu.PrefetchScalarGridSpec(
            num_scalar_prefetch=2, grid=(HEADS // hp, it.shape[0]),
            in_specs=[pair_q, pair_k, pair_kt, pair_k, pair_q,
                      pl.BlockSpec((t, wide), lambda p, n, it_, jt_: (jt_[n], OFF_Q // wide + p)),
                      pl.BlockSpec((t, wide), lambda p, n, it_, jt_: (jt_[n], OFF_K // wide + p)), gain, gain],
            out_specs=[tok, tok, tok, gain, gain,
                       pl.BlockSpec((1, t, LANES), lambda p, n, it_, jt_: (p, jt_[n], 0))],
            scratch_shapes=[pltpu.VMEM((hp, nt, LANES, t), F32), pltpu.VMEM((hp, t, LANES), F32),
                            pltpu.VMEM((hp, t, LANES), F32), pltpu.VMEM((t, LANES), F32)]),
        out_shape=[jax.ShapeDtypeStruct((s, ATTN_W), BF16)] * 3
        + [jax.ShapeDtypeStruct((1, ATTN_W), F32)] * 2
        + [jax.ShapeDtypeStruct((HEADS // hp, s, LANES), F32)],
        compiler_params=_params(("parallel", "arbitrary")),
    )(it, jt, qb, ka, kt, va, do, proj, proj, qg, kg)


def _forget_bwd(dcum, fl, bf_pad):
    s = fl.shape[0]
    tc = min(TC_CUM, s)
    n = s // tc

    def body(dc_ref, fl_ref, bf_ref, df_ref, dbf_ref, carry):
        @pl.when(pl.program_id(0) == 0)
        def _():
            carry[...] = jnp.zeros_like(carry)
            dbf_ref[...] = jnp.zeros_like(dbf_ref)
        r = lax.broadcasted_iota(jnp.int32, (tc, tc), 0)
        cidx = lax.broadcasted_iota(jnp.int32, (tc, tc), 1)
        tri = (cidx >= r).astype(F32)
        dc = dc_ref[0]
        for grp in range(1, dcum.shape[0]):
            dc = dc + dc_ref[grp]
        dlf = jnp.dot(tri, dc, preferred_element_type=F32, precision=lax.Precision.HIGHEST) + carry[...]
        carry[...] += jnp.sum(dc, axis=0, keepdims=True)
        lane = lax.broadcasted_iota(jnp.int32, (tc, LANES), 1)
        dfl = jnp.where(lane < HEADS, dlf * _sigmoid(-(fl_ref[...] + bf_ref[...])), 0.0)
        df_ref[...] = dfl.astype(BF16)
        dbf_ref[...] += jnp.sum(dfl, axis=0, keepdims=True)

    rev = pl.BlockSpec((tc, LANES), lambda i: (n - 1 - i, 0))
    vec = pl.BlockSpec((1, LANES), lambda i: (0, 0))
    return pl.pallas_call(
        body, name="forget_bwd", grid=(n,),
        in_specs=[pl.BlockSpec((dcum.shape[0], tc, LANES), lambda i: (0, n - 1 - i, 0)), rev, vec],
        out_specs=[rev, vec],
        out_shape=[jax.ShapeDtypeStruct((s, LANES), BF16), jax.ShapeDtypeStruct((1, LANES), F32)],
        scratch_shapes=[pltpu.VMEM((1, LANES), F32)],
        compiler_params=_params(("arbitrary",)),
    )(dcum, fl, bf_pad)


def _conv_bwd(dob, proj, conv_w):
    s = dob.shape[0]
    tm = min(TM_ELEM, s)
    tiles, befores, afters = _conv_specs(tm, s)

    def body(dob_ref, dnext_ref, gb_ref, gc_ref, u_ref, zb_ref, gch_ref, uh_ref, gbn_ref, zbn_ref, w_ref,
             dgb_ref, dgc_ref, du_ref, dzb_ref, dw_ref):
        i = pl.program_id(1)

        @pl.when(i == 0)
        def _():
            dw_ref[...] = jnp.zeros_like(dw_ref)
        gb, gc, u, zb, cu, r1, r2, conv = _conv_parts(gb_ref, gc_ref, u_ref, zb_ref, gch_ref, uh_ref, i == 0, w_ref, tm)
        g = dob_ref[...]
        sg = _sigmoid(zb)
        sz = zb * sg
        dconv = g * gb * sz
        zn = zbn_ref[0:8, :].astype(F32)
        dcn = jnp.where(i == pl.num_programs(1) - 1, 0.0,
                        dnext_ref[...] * gbn_ref[0:8, :].astype(F32) * (zn * _sigmoid(zn)))
        nxt1, nxt2 = _sub_row(dcn, 0), _sub_row(dcn, 1)
        row = lax.broadcasted_iota(jnp.int32, (tm, LANES), 0)
        f1 = jnp.where(row == tm - 1, nxt1, pltpu.roll(dconv, tm - 1, 0))
        f2 = jnp.where(row == tm - 2, nxt1, jnp.where(row == tm - 1, nxt2, pltpu.roll(dconv, tm - 2, 0)))
        dcu = w_ref[2:3, :] * dconv + w_ref[1:2, :] * f1 + w_ref[0:1, :] * f2
        dgb_ref[...] = (g * conv * sz).astype(BF16)
        dgc_ref[...] = (dcu * u).astype(BF16)
        du_ref[...] = (dcu * gc).astype(BF16)
        dzb_ref[...] = (g * gb * conv * (sg * (1.0 + zb * (1.0 - sg)))).astype(BF16)
        w_row = lax.broadcasted_iota(jnp.int32, (3, LANES), 0)
        dw0 = jnp.sum(dconv * r2, axis=0, keepdims=True)
        dw1 = jnp.sum(dconv * r1, axis=0, keepdims=True)
        dw2 = jnp.sum(dconv * cu, axis=0, keepdims=True)
        dw_ref[...] += jnp.where(w_row == 0, dw0, jnp.where(w_row == 1, dw1, dw2))

    blk = pl.BlockSpec((tm, LANES), lambda c, i: (i, c))
    nxt = pl.BlockSpec((8, LANES), lambda c, i: (jnp.minimum((i + 1) * (tm // 8), s // 8 - 1), c))
    wspec = pl.BlockSpec((3, LANES), lambda c, i: (0, c))
    return pl.pallas_call(
        body, name="conv_bwd", grid=(CONV_W // LANES, s // tm),
        in_specs=[blk, nxt] + tiles + befores + afters + [wspec],
        out_specs=[blk, blk, blk, blk, wspec],
        out_shape=[jax.ShapeDtypeStruct((s, CONV_W), BF16)] * 4 + [jax.ShapeDtypeStruct((3, CONV_W), F32)],
        compiler_params=_params(("parallel", "arbitrary")),
    )(dob, dob, *([proj] * 8), conv_w)


def _piece_layout(pieces):
    offs, off = [], 0
    for p in pieces:
        offs.append((off, p.shape[1]))
        off += p.shape[1]
    assert off == N_ALL, off
    return offs


def _dw_in(h, pieces, chip_sums):
    s = h.shape[0]
    tk, tn = min(TK_DW, s), TN_DW
    nk = s // tk
    nn = N_MAIN // tn
    main, fpiece = pieces[:-1], pieces[-1]
    layout = _piece_layout(pieces)[:-1]
    n_main = len(main)
    nx = len(chip_sums)

    def body(*refs):
        p_refs, f_ref, h_ref = refs[:n_main], refs[n_main], refs[n_main + 1]
        ins, refs = refs[n_main + 2:n_main + 2 + nx], refs[n_main + 2 + nx:]
        out_ref, outf_ref = refs[:2]
        outs, (acc, accf, send_sems, recv_sems, local_sems) = refs[2:2 + nx], refs[2 + nx:]
        n, k = pl.program_id(0), pl.program_id(1)

        @pl.when(jnp.logical_and(n == 0, k == 0))
        def _():
            for cp in _all_to_all_copies(ins, outs, send_sems, recv_sems, local_sems):
                cp.start()

        @pl.when(k == 0)
        def _():
            acc[...] = jnp.zeros_like(acc)
        hv = h_ref[pl.ds(pl.multiple_of(k * tk, tk), tk), :]
        for p_ref, (off, width) in zip(p_refs, layout):
            @pl.when(jnp.logical_and(n >= off // tn, n < (off + width) // tn))
            def _():
                acc[...] += _dot_tn(p_ref[...], hv)

        @pl.when(k == nk - 1)
        def _():
            out_ref[...] = acc[...].astype(BF16)

        @pl.when(n == 0)
        def _():
            @pl.when(k == 0)
            def _():
                accf[...] = jnp.zeros_like(accf)
            accf[...] += _dot_tn(f_ref[...], hv)

            @pl.when(k == nk - 1)
            def _():
                outf_ref[...] = accf[...].astype(BF16)

        @pl.when(jnp.logical_and(n == nn - 1, k == nk - 1))
        def _():
            for cp in _all_to_all_copies(ins, outs, send_sems, recv_sems, local_sems):
                cp.wait()

    def piece_spec(off, width):
        lo, hi = off // tn, (off + width) // tn

        def index(n, k):
            active = jnp.logical_and(n >= lo, n < hi)
            return jnp.where(active, k, 0), jnp.clip(n - lo, 0, hi - lo - 1)
        return pl.BlockSpec((tk, tn), index)

    any_spec = pl.BlockSpec(memory_space=pl.ANY)
    res = pl.pallas_call(
        body, name="dw_in", grid=(nn, nk),
        in_specs=[piece_spec(off, width) for off, width in layout]
        + [pl.BlockSpec((tk, N_FPAD), lambda n, k: (jnp.where(n == 0, k, 0), 0)),
           pl.BlockSpec((s, D_MODEL), lambda n, k: (0, 0))] + [any_spec] * nx,
        out_specs=[pl.BlockSpec((tn, D_MODEL), lambda n, k: (n, 0)),
                   pl.BlockSpec((N_FPAD, D_MODEL), lambda n, k: (0, 0))] + [any_spec] * nx,
        out_shape=[jax.ShapeDtypeStruct((N_MAIN, D_MODEL), BF16), jax.ShapeDtypeStruct((N_FPAD, D_MODEL), BF16)]
        + [jax.ShapeDtypeStruct(a.shape, a.dtype) for a in chip_sums],
        scratch_shapes=[pltpu.VMEM((tn, D_MODEL), F32), pltpu.VMEM((N_FPAD, D_MODEL), F32)] + _gather_sems(nx),
        compiler_params=_params(("arbitrary", "arbitrary")),
    )(*main, fpiece, h, *chip_sums)
    return res[:2], res[2:]


def _dh_and_dx(pieces, w_all_t, x, dy, ada3, norm_g, chip_sums):
    s = x.shape[0]
    tm = min(TM_DH, s)
    nt = s // tm
    n = len(chip_sums)
    npc = len(pieces)
    layout = _piece_layout(pieces)

    def body(*refs):
        p_refs, refs = refs[:npc], refs[npc:]
        wt_ref, x_ref, dy_ref, ada_ref, g_ref = refs[:5]
        ins, refs = refs[5:5 + n], refs[5 + n:]
        gx_ref, dsh_ref, dsc_ref, dg_ref = refs[:4]
        outs, (send_sems, recv_sems, local_sems) = refs[4:4 + n], refs[4 + n:]
        i = pl.program_id(0)

        @pl.when(i == 0)
        def _():
            for cp in _chip_copies(ins, outs, send_sems, recv_sems, local_sems):
                cp.start()
            dsh_ref[...] = jnp.zeros_like(dsh_ref)
            dsc_ref[...] = jnp.zeros_like(dsc_ref)
            dg_ref[...] = jnp.zeros_like(dg_ref)

        dh = None
        for p_ref, (off, width) in zip(p_refs, layout):
            part = _dot(p_ref[...], wt_ref[off:off + width, :])
            dh = part if dh is None else dh + part
        xv = x_ref[...]
        r = lax.rsqrt(jnp.mean(xv * xv, axis=-1, keepdims=True) + EPS)
        xhat = xv * r
        g = g_ref[...]
        one_sc = 1.0 + ada_ref[1:2, :]
        dsh_ref[...] += jnp.sum(dh, axis=0, keepdims=True)
        dsc_ref[...] += jnp.sum(dh * (xhat * g), axis=0, keepdims=True)
        dg_ref[...] += jnp.sum(dh * xhat, axis=0, keepdims=True) * one_sc
        dxh = dh * (g * one_sc)
        dx = r * (dxh - xhat * jnp.mean(dxh * xhat, axis=-1, keepdims=True))
        gx_ref[...] = dy_ref[...] + dx

        @pl.when(i == nt - 1)
        def _():
            for cp in _chip_copies(ins, outs, send_sems, recv_sems, local_sems):
                cp.wait()

    full = pl.BlockSpec((tm, D_MODEL), lambda i: (i, 0))
    vec = pl.BlockSpec((1, D_MODEL), lambda i: (0, 0))
    any_spec = pl.BlockSpec(memory_space=pl.ANY)
    res = pl.pallas_call(
        body, name="dh_dx", grid=(nt,),
        in_specs=[pl.BlockSpec((tm, p.shape[1]), lambda i: (i, 0)) for p in pieces]
        + [pl.BlockSpec((N_ALL, D_MODEL), lambda i: (0, 0)), full, full,
           pl.BlockSpec((3, D_MODEL), lambda i: (0, 0)), vec] + [any_spec] * n,
        out_specs=[full, vec, vec, vec] + [any_spec] * n,
        out_shape=[jax.ShapeDtypeStruct((s, D_MODEL), F32)] + [jax.ShapeDtypeStruct((1, D_MODEL), F32)] * 3
        + [jax.ShapeDtypeStruct(a.shape, a.dtype) for a in chip_sums],
        scratch_shapes=[pltpu.SemaphoreType.DMA((n * 3,)), pltpu.SemaphoreType.DMA((n * 3,)),
                        pltpu.SemaphoreType.DMA((n,))],
        compiler_params=_params(("arbitrary",)),
    )(*pieces, w_all_t, x, dy, ada3, norm_g, *chip_sums)
    return res[:4], res[4:]


def _sum_small(vec_all, qg_parts, kg_parts):
    def body(v_ref, q_ref, k_ref, tot_ref, gq_ref, gk_ref):
        tot = v_ref[0:1, :]
        for p in range(1, N_DEV):
            tot = tot + v_ref[p:p + 1, :]
        tot_ref[...] = tot
        gq_ref[...] = jnp.sum(q_ref[...], axis=0, keepdims=True)
        gk_ref[...] = jnp.sum(k_ref[...], axis=0, keepdims=True)

    n = vec_all.shape[-1]
    return pl.pallas_call(
        body, name="sum_small",
        out_shape=[jax.ShapeDtypeStruct((1, n), F32),
                   jax.ShapeDtypeStruct((1, HEAD_DIM), F32), jax.ShapeDtypeStruct((1, HEAD_DIM), F32)],
        compiler_params=_params(),
    )(vec_all, qg_parts, kg_parts)


def _grad_w_ada(c_cols, dada_rows):
    def body(c_ref, d_ref, out_ref):
        acc = c_ref[0] * d_ref[0]
        for b in range(1, N_DEV):
            acc = acc + c_ref[b] * d_ref[b]
        out_ref[...] = acc

    return pl.pallas_call(
        body, name="grad_w_ada",
        out_shape=jax.ShapeDtypeStruct((D_MODEL, ADA_SHARD), F32),
        compiler_params=_params(),
    )(c_cols, dada_rows)


def _adam_step(w, m, v, g):
    c1 = 1.0 / (1.0 - ADAM_B1 ** ADAM_STEP)
    c2 = 1.0 / (1.0 - ADAM_B2 ** ADAM_STEP)
    m_new = ADAM_B1 * m + (1.0 - ADAM_B1) * g
    v_new = ADAM_B2 * v + (1.0 - ADAM_B2) * (g * g)
    return -ADAM_LR * ((m_new * c1) / (jnp.sqrt(v_new * c2) + ADAM_EPS) + ADAM_WD * w), m_new, v_new


def _adamw_small(params, name):
    n = len(params)

    def body(*refs):
        ins, outs = refs[:4 * n], refs[4 * n:]
        for k in range(n):
            w_ref, m_ref, v_ref, g_ref = ins[4 * k:4 * k + 4]
            d_ref, mo_ref, vo_ref = outs[3 * k:3 * k + 3]
            d_ref[...], mo_ref[...], vo_ref[...] = _adam_step(w_ref[...], m_ref[...], v_ref[...], g_ref[...])

    res = pl.pallas_call(
        body, name=name,
        out_shape=[jax.ShapeDtypeStruct(p[0].shape, F32) for p in params for _ in range(3)],
        compiler_params=_params(),
    )(*[a for p in params for a in p])
    return [(p[3],) + tuple(res[3 * k:3 * k + 3]) for k, p in enumerate(params)]


def _adamw(w, m, v, g_parts, name):
    rows, cols = w.shape
    n_parts = g_parts.shape[0]
    tr = 256 if rows % 256 == 0 else rows
    tc = 256 if (tr == rows and rows > 256 and cols % 256 == 0) else cols

    def body(w_ref, m_ref, v_ref, g_ref, go_ref, d_ref, mo_ref, vo_ref):
        g = g_ref[0].astype(F32)
        for p in range(1, n_parts):
            g = g + g_ref[p].astype(F32)
        go_ref[...] = g
        d_ref[...], mo_ref[...], vo_ref[...] = _adam_step(w_ref[...], m_ref[...], v_ref[...], g)

    blk = pl.BlockSpec((tr, tc), lambda i, j: (i, j))
    return pl.pallas_call(
        body, name=name, grid=(rows // tr, cols // tc),
        in_specs=[blk, blk, blk, pl.BlockSpec((n_parts, tr, tc), lambda i, j: (0, i, j))],
        out_specs=[blk] * 4,
        out_shape=[jax.ShapeDtypeStruct((rows, cols), F32)] * 4,
        compiler_params=_params(("parallel", "parallel")),
    )(w, m, v, g_parts)


_O_F = 1536


def _to_internal(wt_g):
    wf = wt_g.reshape(IN_WIDTH, D_MODEL)
    f = jnp.pad(wf[_O_F:_O_F + HEADS], ((0, N_FPAD - HEADS), (0, 0)))
    return jnp.concatenate([wf[:_O_F], wf[_O_F + HEADS:], f], axis=0)


def _slabs_by_core(dwt, dwt_f):
    sources = ((dwt, 0, _O_F, 0), (dwt_f, _O_F, _O_F + HEADS, _O_F), (dwt, _O_F + HEADS, IN_WIDTH, HEADS))

    def slab(p):
        lo, hi = p * IN_SHARD, (p + 1) * IN_SHARD
        parts = []
        for src, o_lo, o_hi, shift in sources:
            a, b = max(lo, o_lo), min(hi, o_hi)
            if a < b:
                parts.append(src[a - shift:b - shift])
        return parts[0] if len(parts) == 1 else jnp.concatenate(parts, axis=0)

    return jnp.stack([jnp.stack([slab(2 * chip + core) for chip in range(4)]) for core in range(2)])


def kernel(x, c, w_ada, b_ada, norm_g, w_in, b_f, q_norm_g, k_norm_g, conv_w, w_attn_out, w_conv_out, w_o, loss_target, m_w_ada, m_b_ada, m_norm_g, m_w_in, m_b_f, m_q_norm_g, m_k_norm_g, m_conv_w, m_w_attn_out, m_w_conv_out, m_w_o, v_w_ada, v_b_ada, v_norm_g, v_w_in, v_b_f, v_q_norm_g, v_k_norm_g, v_conv_w, v_w_attn_out, v_w_conv_out, v_w_o):
    me = 4 * lax.axis_index("x") + 2 * lax.axis_index("y") + lax.axis_index("c")
    s = x.shape[1]
    x2, t2 = x[0], loss_target[0]

    w_in_g, c_all, ada_g = _gather_weights_and_ada(w_in[0].T.astype(BF16), c, w_ada[0])
    ada_mine = lax.dynamic_index_in_dim(ada_g[:, :, 0, :], me, axis=1, keepdims=False)
    ada3 = (ada_mine.reshape(1, 3 * D_MODEL) + b_ada).reshape(3, D_MODEL)
    w_all_t = _to_internal(w_in_g)
    qg = jnp.tile(q_norm_g, (1, HEADS))
    kg = jnp.tile(k_norm_g, (1, HEADS))
    bf_pad = jnp.pad(b_f, ((0, 0), (0, LANES - HEADS)))

    (proj, fl, h), (qa, ka, va, kt, vt), (cw_g, wa_g, wb_g, wo_g) = _proj_fwd(
        x2, ada3, norm_g, w_all_t, bf_pad, qg, kg,
        [conv_w[0], w_attn_out[0].astype(BF16), w_conv_out[0].astype(BF16), w_o[0].astype(BF16)])
    wa = jnp.transpose(wa_g, (1, 0, 2)).reshape(ATTN_W, D_MODEL)
    wb = jnp.transpose(wb_g, (1, 0, 2)).reshape(CONV_W, D_MODEL)
    wo = wo_g.reshape(D_MODEL, D_MODEL)
    cw = jnp.transpose(cw_g, (1, 0, 2)).reshape(3, CONV_W)
    attn, oa, qb = _attn_fwd(qa, ka, vt, proj)
    (dy, dgab, do, dza, dob, dwo, dwa, dwb, dgate, loss_part) = _tail(oa, attn, proj, x2, t2, ada3, wa, wb, wo, cw)

    core = lax.axis_index("c").astype(jnp.int32).reshape(1)
    small = [jnp.transpose(dwa.reshape(ATTN_W, N_DEV, LANES), (1, 0, 2)).astype(BF16),
             jnp.transpose(dwb.reshape(CONV_W, N_DEV, LANES), (1, 0, 2)).astype(BF16),
             dwo.reshape(N_DEV, D_MODEL // N_DEV, D_MODEL).astype(BF16)]
    dq, dk, dv, dqg, dkg, dcum = _attn_bwd(qb, ka, kt, va, do, proj, qg, kg)
    df, dbf = _forget_bwd(dcum, fl, bf_pad)
    dcb, dcc, dcu, dcz, dcw = _conv_bwd(dob, proj, cw)
    pieces = [dq, dk, dv, dza, dcb, dcc, dcu, dcz, dgab, df]
    (dw_main, dw_f), (g_wa_parts, g_wb_parts, g_wo_parts) = _dw_in(h, pieces, small)

    slabs_in = _slabs_by_core(dw_main, dw_f)
    (theirs_in,) = _sibling_swap([slabs_in], "swap_w_in")
    (grad_x, dshift, dscale, dnormg), (g_in_parts,) = _dh_and_dx(
        pieces, w_all_t, x2, dy, ada3, norm_g, [_pair_sum(slabs_in, theirs_in, core, "pair_sum_w_in")])
    vec = jnp.concatenate([dshift, dscale, dgate, dnormg, dbf, dcw.reshape(1, 3 * CONV_W), loss_part, dqg, dkg],
                          axis=1)
    (vec_all,) = _gather_direct([vec], "gather_small")
    vec_all = vec_all.reshape(N_DEV, vec.shape[1])
    n_main = 4 * D_MODEL + LANES + 3 * CONV_W + LANES
    tot, g_qg, g_kg = _sum_small(
        vec_all[:, :n_main],
        vec_all[:, n_main:n_main + ATTN_W].reshape(N_DEV * HEADS, HEAD_DIM),
        vec_all[:, n_main + ATTN_W:].reshape(N_DEV * HEADS, HEAD_DIM))
    g_b_ada = tot[:, 0:3 * D_MODEL]
    g_norm_g = tot[:, 3 * D_MODEL:4 * D_MODEL]
    g_b_f = tot[:, 4 * D_MODEL:4 * D_MODEL + HEADS]
    g_cw_full = tot[:, 4 * D_MODEL + LANES:4 * D_MODEL + LANES + 3 * CONV_W].reshape(3, CONV_W)
    g_cw = lax.dynamic_slice(g_cw_full, (0, me * (CONV_W // N_DEV)), (3, CONV_W // N_DEV))
    dada_mine = lax.dynamic_slice(vec_all[:, 0:3 * D_MODEL], (0, me * ADA_SHARD), (N_DEV, ADA_SHARD))
    g_w_ada = _grad_w_ada(jnp.transpose(c_all, (0, 2, 1)), dada_mine.reshape(N_DEV, 1, ADA_SHARD))

    upd = {}
    upd["w_ada"] = _adamw(w_ada[0], m_w_ada[0], v_w_ada[0], g_w_ada[None], "adamw_w_ada")
    upd["w_in"] = [u.T for u in _adamw(w_in[0].T, m_w_in[0].T, v_w_in[0].T, g_in_parts, "adamw_w_in")]
    small_names = ["b_ada", "norm_g", "b_f", "q_norm_g", "k_norm_g", "conv_w"]
    small_upd = _adamw_small(
        [(b_ada, m_b_ada, v_b_ada, g_b_ada), (norm_g, m_norm_g, v_norm_g, g_norm_g), (b_f, m_b_f, v_b_f, g_b_f),
         (q_norm_g, m_q_norm_g, v_q_norm_g, g_qg), (k_norm_g, m_k_norm_g, v_k_norm_g, g_kg),
         (conv_w[0], m_conv_w[0], v_conv_w[0], g_cw)], "adamw_small")
    upd.update(zip(small_names, small_upd))
    upd["w_attn_out"] = _adamw(w_attn_out[0], m_w_attn_out[0], v_w_attn_out[0], g_wa_parts, "adamw_w_attn_out")
    upd["w_conv_out"] = _adamw(w_conv_out[0], m_w_conv_out[0], v_w_conv_out[0], g_wb_parts, "adamw_w_conv_out")
    upd["w_o"] = _adamw(w_o[0], m_w_o[0], v_w_o[0], g_wo_parts, "adamw_w_o")

    names = ["w_ada", "b_ada", "norm_g", "w_in", "b_f", "q_norm_g", "k_norm_g", "conv_w",
             "w_attn_out", "w_conv_out", "w_o"]
    lead = {"w_ada", "w_in", "conv_w", "w_attn_out", "w_conv_out", "w_o"}
    fix = lambda n, a: a[None] if n in lead else a
    loss = tot[0, n_main - LANES]
    outs = [loss, grad_x[None]]
    for k in range(4):
        outs += [fix(n, upd[n][k]) for n in names]
    return tuple(outs)
```

```python
import functools

import numpy as np
import jax
import jax.numpy as jnp
from jax import lax
from jax.experimental import pallas as pl
from jax.experimental.pallas import tpu as pltpu

F32 = jnp.float32
BF16 = jnp.bfloat16

D_MODEL = 1024
HEADS = 8
HEAD_DIM = 64
ATTN_W = 512
CONV_W = 512
N_DEV = 8
IN_WIDTH = 6152
IN_SHARD = IN_WIDTH // N_DEV
N_MAIN = 6144
N_FPAD = 128
N_ALL = N_MAIN + N_FPAD
ADA_SHARD = 3 * D_MODEL // N_DEV
EPS = 1e-6
NEG = -1e30

ADAM_LR = 0.001
ADAM_B1 = 0.9
ADAM_B2 = 0.999
ADAM_EPS = 1e-08
ADAM_WD = 0.01
ADAM_STEP = 10

LANES = 128
VMEM_LIMIT = 56 * 1024 * 1024

TM_PROJ = 256
TN_PROJ = 1024
TM_ELEM = 512
TQ = 512
HEADS_PER_STEP = 8
HEADS_PER_STEP_FWD = 8
TM_TAIL = 256
TC_CUM = 512
TK_DW = 2048
TN_DW = 512
TM_DH = 256
HALO = 16

OFF_Q, OFF_K, OFF_V, OFF_ZA, OFF_CB, OFF_CC, OFF_CU, OFF_CZ, OFF_GA, OFF_GB = (
    0, 512, 1024, 1536, 2048, 2560, 3072, 3584, 4096, 5120)


def _params(sem=None):
    return pltpu.CompilerParams(dimension_semantics=sem, vmem_limit_bytes=VMEM_LIMIT)


def _dot(a, b):
    return jnp.dot(a, b, preferred_element_type=F32)


def _dot_nt(a, b):
    return lax.dot_general(a, b, (((1,), (1,)), ((), ())), preferred_element_type=F32)


def _dot_tn(a, b):
    return lax.dot_general(a, b, (((0,), (0,)), ((), ())), preferred_element_type=F32)


def _sigmoid(x):
    return 1.0 / (1.0 + jnp.exp(-x))


def _lane_lo(shape):
    return lax.broadcasted_iota(jnp.int32, shape, len(shape) - 1) < HEAD_DIM


def _seg_sum(z, lo):
    a = jnp.sum(jnp.where(lo, z, 0.0), axis=-1, keepdims=True)
    b = jnp.sum(jnp.where(lo, 0.0, z), axis=-1, keepdims=True)
    return jnp.where(lo, a, b)


def _lane_col(z, lane):
    idx = lax.broadcasted_iota(jnp.int32, z.shape, 1)
    return jnp.sum(jnp.where(idx == lane, z, 0.0), axis=-1, keepdims=True)


def _sub_row(z, row):
    idx = lax.broadcasted_iota(jnp.int32, z.shape, 0)
    return jnp.sum(jnp.where(idx == row, z, 0.0), axis=0, keepdims=True)


def _mesh_pos():
    x, y, c = lax.axis_index("x"), lax.axis_index("y"), lax.axis_index("c")
    return x, y, c, 4 * x + 2 * y + c


def _peer(k, x, y, c):
    px = 1 - x if (k >> 2) & 1 else x
    py = 1 - y if (k >> 1) & 1 else y
    pc = 1 - c if k & 1 else c
    return (px, py, pc), 4 * px + 2 * py + pc


def _gather_copies(ins, outs, send_sems, recv_sems, local_sems):
    x, y, c, me = _mesh_pos()
    copies = []
    for a in range(len(ins)):
        copies.append(pltpu.make_async_copy(ins[a], outs[a].at[me], local_sems.at[a]))
        for k in range(1, N_DEV):
            dev, _ = _peer(k, x, y, c)
            copies.append(pltpu.make_async_remote_copy(
                src_ref=ins[a], dst_ref=outs[a].at[me],
                send_sem=send_sems.at[a * (N_DEV - 1) + k - 1], recv_sem=recv_sems.at[a * (N_DEV - 1) + k - 1],
                device_id=dev, device_id_type=pl.DeviceIdType.MESH))
    return copies


def _gather_sems(n):
    return [pltpu.SemaphoreType.DMA((n * (N_DEV - 1),)), pltpu.SemaphoreType.DMA((n * (N_DEV - 1),)),
            pltpu.SemaphoreType.DMA((n,))]


def _gather_direct(arrs, name):
    n = len(arrs)
    any_spec = pl.BlockSpec(memory_space=pl.ANY)

    def body(*refs):
        copies = _gather_copies(refs[:n], refs[n:2 * n], *refs[2 * n:])
        for cp in copies:
            cp.start()
        for cp in copies:
            cp.wait()

    return pl.pallas_call(
        body, name=name, out_shape=[jax.ShapeDtypeStruct((N_DEV,) + a.shape, a.dtype) for a in arrs],
        in_specs=[any_spec] * n, out_specs=[any_spec] * n, scratch_shapes=_gather_sems(n),
    )(*arrs)


def _ada_phase(c_ref, w_ref, call_ref, adag_ref, mine_ref, send_sems, recv_sems):
    x, y, c, me = _mesh_pos()

    def copy(phase, k, src, dst):
        dev, _ = _peer(k, x, y, c)
        return pltpu.make_async_remote_copy(
            src_ref=src, dst_ref=dst,
            send_sem=send_sems.at[phase * (N_DEV - 1) + k - 1],
            recv_sem=recv_sems.at[phase * (N_DEV - 1) + k - 1],
            device_id=dev, device_id_type=pl.DeviceIdType.MESH)

    call_ref[me] = c_ref[...]
    first = [copy(0, k, c_ref, call_ref.at[me]) for k in range(1, N_DEV)]
    for cp in first:
        cp.start()
    for cp in first:
        cp.wait()
    wb = w_ref[...].astype(BF16)
    for b in range(N_DEV):
        row = jnp.broadcast_to(call_ref[b], (8, D_MODEL)).astype(BF16)
        mine_ref[b] = _sub_row(_dot(row, wb), 0)
    adag_ref[me] = mine_ref[...]
    second = [copy(1, k, mine_ref, adag_ref.at[me]) for k in range(1, N_DEV)]
    for cp in second:
        cp.start()
    for cp in second:
        cp.wait()


def _gather_weights_and_ada(wt_shard, c_row, w_ada_sh):
    any_spec = pl.BlockSpec(memory_space=pl.ANY)
    vm = pl.BlockSpec(memory_space=pltpu.VMEM)

    def body(w_in_ref, c_ref, wada_ref, out_ref, call_ref, adag_ref, mine_ref, send_sems, recv_sems, local_sem,
             ada_send, ada_recv):
        x, y, c, me = _mesh_pos()
        sibling = (x, y, 1 - c)
        chips = [(1 - x, y), (x, 1 - y), (1 - x, 1 - y)]

        def copy(k, src, blk, to):
            return pltpu.make_async_remote_copy(
                src_ref=src, dst_ref=out_ref.at[blk], send_sem=send_sems.at[k], recv_sem=recv_sems.at[k],
                device_id=to, device_id_type=pl.DeviceIdType.MESH)

        local = pltpu.make_async_copy(w_in_ref, out_ref.at[me], local_sem.at[0])
        local.start()
        first = [copy(0, w_in_ref, me, sibling)]
        first += [copy(1 + j, w_in_ref, me, (px, py, c)) for j, (px, py) in enumerate(chips)]
        for cp in first:
            cp.start()
        _ada_phase(c_ref, wada_ref, call_ref, adag_ref, mine_ref, ada_send, ada_recv)
        passed = []
        for j, (px, py) in enumerate(chips):
            blk = 4 * px + 2 * py + c
            copy(1 + j, w_in_ref, blk, (x, y, c)).wait_recv()
            fwd = copy(4 + j, out_ref.at[blk], blk, sibling)
            fwd.start()
            passed.append(fwd)
        copy(0, w_in_ref, 4 * x + 2 * y + 1 - c, (x, y, c)).wait_recv()
        for j, (px, py) in enumerate(chips):
            copy(4 + j, w_in_ref, 4 * px + 2 * py + 1 - c, (x, y, c)).wait_recv()
        for cp in first + passed:
            cp.wait_send()
        local.wait()

    per = N_DEV - 1
    return pl.pallas_call(
        body, name="gather_weights",
        out_shape=[jax.ShapeDtypeStruct((N_DEV,) + wt_shard.shape, wt_shard.dtype),
                   jax.ShapeDtypeStruct((N_DEV, 1, D_MODEL), F32),
                   jax.ShapeDtypeStruct((N_DEV, N_DEV, 1, ADA_SHARD), F32)],
        in_specs=[any_spec, vm, vm], out_specs=[any_spec, vm, vm],
        scratch_shapes=[pltpu.VMEM((N_DEV, 1, ADA_SHARD), F32),
                        pltpu.SemaphoreType.DMA((per,)), pltpu.SemaphoreType.DMA((per,)),
                        pltpu.SemaphoreType.DMA((1,)),
                        pltpu.SemaphoreType.DMA((2 * per,)), pltpu.SemaphoreType.DMA((2 * per,))],
        compiler_params=pltpu.CompilerParams(vmem_limit_bytes=VMEM_LIMIT),
    )(wt_shard, c_row, w_ada_sh)


def _sibling_swap(arrs, name):
    n = len(arrs)
    any_spec = pl.BlockSpec(memory_space=pl.ANY)

    def body(*refs):
        ins, outs = refs[:n], refs[n:2 * n]
        send_sems, recv_sems = refs[2 * n:]
        x, y, c, _ = _mesh_pos()
        copies = [pltpu.make_async_remote_copy(
            src_ref=ins[a].at[1 - c], dst_ref=outs[a], send_sem=send_sems.at[a], recv_sem=recv_sems.at[a],
            device_id=(x, y, 1 - c), device_id_type=pl.DeviceIdType.MESH) for a in range(n)]
        for cp in copies:
            cp.start()
        for cp in copies:
            cp.wait()

    return pl.pallas_call(
        body, name=name,
        out_shape=[jax.ShapeDtypeStruct(a.shape[1:], a.dtype) for a in arrs],
        in_specs=[any_spec] * n, out_specs=[any_spec] * n,
        scratch_shapes=[pltpu.SemaphoreType.DMA((n,)), pltpu.SemaphoreType.DMA((n,))],
    )(*arrs)


def _pair_sum(mine2, theirs, core, name):
    _, _, rows, cols = mine2.shape
    tr = 256 if rows % 256 == 0 else rows

    def body(core_ref, a_ref, b_ref, out_ref):
        out_ref[...] = (a_ref[...].astype(F32) + b_ref[...].astype(F32)).astype(BF16)

    return pl.pallas_call(
        body, name=name,
        grid_spec=pltpu.PrefetchScalarGridSpec(
            num_scalar_prefetch=1, grid=(4, rows // tr),
            in_specs=[pl.BlockSpec((None, None, tr, cols), lambda ch, i, core_: (core_[0], ch, i, 0)),
                      pl.BlockSpec((None, tr, cols), lambda ch, i, core_: (ch, i, 0))],
            out_specs=pl.BlockSpec((None, tr, cols), lambda ch, i, core_: (ch, i, 0))),
        out_shape=jax.ShapeDtypeStruct(theirs.shape, BF16),
        compiler_params=_params(("parallel", "parallel")),
    )(core, mine2, theirs)


def _all_to_all_copies(ins, outs, send_sems, recv_sems, local_sems):
    x, y, c, me = _mesh_pos()
    copies = []
    for a in range(len(ins)):
        copies.append(pltpu.make_async_copy(ins[a].at[me], outs[a].at[me], local_sems.at[a]))
        for k in range(1, N_DEV):
            dev, p = _peer(k, x, y, c)
            copies.append(pltpu.make_async_remote_copy(
                src_ref=ins[a].at[p], dst_ref=outs[a].at[me],
                send_sem=send_sems.at[a * (N_DEV - 1) + k - 1], recv_sem=recv_sems.at[a * (N_DEV - 1) + k - 1],
                device_id=dev, device_id_type=pl.DeviceIdType.MESH))
    return copies


def _chip_copies(ins, outs, send_sems, recv_sems, local_sems):
    x, y, c, _ = _mesh_pos()
    my_chip = 2 * x + y
    chips = [(1 - x, y), (x, 1 - y), (1 - x, 1 - y)]
    copies = []
    for a in range(len(ins)):
        copies.append(pltpu.make_async_copy(ins[a].at[my_chip], outs[a].at[my_chip], local_sems.at[a]))
        for j, (px, py) in enumerate(chips):
            copies.append(pltpu.make_async_remote_copy(
                src_ref=ins[a].at[2 * px + py], dst_ref=outs[a].at[my_chip],
                send_sem=send_sems.at[a * 3 + j], recv_sem=recv_sems.at[a * 3 + j],
                device_id=(px, py, c), device_id_type=pl.DeviceIdType.MESH))
    return copies


def _proj_fwd(x, ada3, norm_g, w_all_t, bf_pad, qg, kg, later):
    s = x.shape[0]
    tm, tn = min(TM_PROJ, s), TN_PROJ
    nt = s // tm
    n = len(later)

    def body(x_ref, ada_ref, g_ref, wt_ref, bf_ref, qg_ref, kg_ref, *rest):
        ins, (proj_ref, fl_ref, h_ref), rows_refs, rest = rest[:n], rest[n:n + 3], rest[n + 3:n + 8], rest[n + 8:]
        outs, (carry, send_sems, recv_sems, local_sems) = rest[:n], rest[n:]
        i = pl.program_id(0)

        @pl.when(i == 0)
        def _():
            carry[...] = jnp.zeros_like(carry)
            for cp in _gather_copies(ins, outs, send_sems, recv_sems, local_sems):
                cp.start()

        xv = x_ref[...]
        r = lax.rsqrt(jnp.mean(xv * xv, axis=-1, keepdims=True) + EPS)
        hv = ((xv * r) * g_ref[...]) * (1.0 + ada_ref[1:2, :]) + ada_ref[0:1, :]
        hb = hv.astype(BF16)
        h_ref[...] = hb
        fl = _dot_nt(hb, wt_ref[N_MAIN:N_ALL, :])
        fl_ref[...] = fl
        for j in range(N_MAIN // tn):
            proj_ref[:, j * tn:(j + 1) * tn] = _dot_nt(hb, wt_ref[j * tn:(j + 1) * tn, :]).astype(BF16)
        _attention_rows(proj_ref, fl, bf_ref, qg_ref, kg_ref, carry, *rows_refs)

        @pl.when(i == nt - 1)
        def _():
            for cp in _gather_copies(ins, outs, send_sems, recv_sems, local_sems):
                cp.wait()

    any_spec = pl.BlockSpec(memory_space=pl.ANY)
    heads = pl.BlockSpec((HEADS, tm, LANES), lambda i: (0, i, 0))
    heads_t = pl.BlockSpec((HEADS, LANES, tm), lambda i: (0, 0, i))
    vec = pl.BlockSpec((1, ATTN_W), lambda i: (0, 0))
    res = pl.pallas_call(
        body, name="proj_fwd", grid=(nt,),
        in_specs=[pl.BlockSpec((tm, D_MODEL), lambda i: (i, 0)),
                  pl.BlockSpec((3, D_MODEL), lambda i: (0, 0)),
                  pl.BlockSpec((1, D_MODEL), lambda i: (0, 0)),
                  pl.BlockSpec((N_ALL, D_MODEL), lambda i: (0, 0)),
                  pl.BlockSpec((1, LANES), lambda i: (0, 0)), vec, vec] + [any_spec] * n,
        out_specs=[pl.BlockSpec((tm, N_MAIN), lambda i: (i, 0)),
                   pl.BlockSpec((tm, N_FPAD), lambda i: (i, 0)),
                   pl.BlockSpec((tm, D_MODEL), lambda i: (i, 0)),
                   heads, heads, heads, heads_t, heads_t] + [any_spec] * n,
        out_shape=[jax.ShapeDtypeStruct((s, N_MAIN), BF16),
                   jax.ShapeDtypeStruct((s, N_FPAD), F32),
                   jax.ShapeDtypeStruct((s, D_MODEL), BF16)]
        + [jax.ShapeDtypeStruct((HEADS, s, LANES), BF16)] * 3
        + [jax.ShapeDtypeStruct((HEADS, LANES, s), BF16)] * 2
        + [jax.ShapeDtypeStruct((N_DEV,) + a.shape, a.dtype) for a in later],
        scratch_shapes=[pltpu.VMEM((1, LANES), F32)] + _gather_sems(n),
        compiler_params=_params(("arbitrary",)),
    )(x, ada3, norm_g, w_all_t, bf_pad, qg, kg, *later)
    return res[:3], res[3:8], res[8:]


L_ONE_Q, L_F_Q, L_LSE_Q, L_END = HEAD_DIM, HEAD_DIM + 3, HEAD_DIM + 6, HEAD_DIM + 9


def _split3(f):
    hi = f.astype(BF16).astype(F32)
    r = f - hi
    mid = r.astype(BF16).astype(F32)
    return hi, mid, r - mid


def _place3(lane, first, parts, otherwise):
    a, b, c = parts
    return jnp.where(lane == first, a, jnp.where(lane == first + 1, b, jnp.where(lane == first + 2, c, otherwise)))


def _log_forget(fl, bf):
    z = fl + bf
    lf = jnp.minimum(z, 0.0) - jnp.log1p(jnp.exp(-jnp.abs(z)))
    lane = lax.broadcasted_iota(jnp.int32, z.shape, 1)
    return jnp.where(lane < HEADS, lf, 0.0)


def _attention_rows(p_ref, fl, bf_ref, qg_ref, kg_ref, carry, qa_ref, ka_ref, va_ref, kt_ref, vt_ref):
    tm = fl.shape[0]
    scale = HEAD_DIM ** -0.5
    tri = (lax.broadcasted_iota(jnp.int32, (tm, tm), 1) <= lax.broadcasted_iota(jnp.int32, (tm, tm), 0)).astype(F32)
    cum_v = jnp.dot(tri, _log_forget(fl, bf_ref[...]), preferred_element_type=F32,
                    precision=lax.Precision.HIGHEST) + carry[...]
    carry[...] = _sub_row(cum_v, tm - 1)
    lane = lax.broadcasted_iota(jnp.int32, (tm, LANES), 1)
    lo = lane < HEAD_DIM
    v_tail = jnp.where(lane < L_F_Q, 1.0, 0.0)
    for pr in range(ATTN_W // LANES):
        sl = slice(pr * LANES, (pr + 1) * LANES)
        q2 = p_ref[:, OFF_Q + pr * LANES:OFF_Q + (pr + 1) * LANES].astype(F32)
        k2 = p_ref[:, OFF_K + pr * LANES:OFF_K + (pr + 1) * LANES].astype(F32)
        v2 = p_ref[:, OFF_V + pr * LANES:OFF_V + (pr + 1) * LANES].astype(F32)
        rq = lax.rsqrt(_seg_sum(q2 * q2, lo) * (1.0 / HEAD_DIM) + EPS)
        rk = lax.rsqrt(_seg_sum(k2 * k2, lo) * (1.0 / HEAD_DIM) + EPS)
        qn = ((q2 * rq) * qg_ref[:, sl]) * scale
        kn = (k2 * rk) * kg_ref[:, sl]
        for hh in range(2):
            h = 2 * pr + hh
            f3 = _split3(_lane_col(cum_v, h))
            qh = qn if hh == 0 else pltpu.roll(qn, HEAD_DIM, 1)
            kh = kn if hh == 0 else pltpu.roll(kn, HEAD_DIM, 1)
            vh = v2 if hh == 0 else pltpu.roll(v2, HEAD_DIM, 1)
            q_tail = jnp.where(lane < L_F_Q, 1.0, _place3(lane, L_F_Q, f3, 0.0))
            k_tail = _place3(lane, L_ONE_Q, tuple(-f for f in f3), jnp.where(lane < L_END, 1.0, 0.0))
            k_row = jnp.where(lo, kh, k_tail)
            v_row = jnp.where(lo, vh, v_tail)
            qa_ref[h] = jnp.where(lo, qh, q_tail).astype(BF16)
            ka_ref[h] = k_row.astype(BF16)
            va_ref[h] = v_row.astype(BF16)
            kt_ref[h] = k_row.T.astype(BF16)
            vt_ref[h] = v_row.T.astype(BF16)


def _causal_t(t):
    return lax.broadcasted_iota(jnp.int32, (t, t), 0) <= lax.broadcasted_iota(jnp.int32, (t, t), 1)


def _tri_steps(nt, q_major):
    if q_major:
        pairs = [(i, j) for i in range(nt) for j in range(i + 1)]
    else:
        pairs = [(i, j) for j in range(nt) for i in range(j, nt)]
    return (jnp.asarray(np.array([p[0] for p in pairs], np.int32)),
            jnp.asarray(np.array([p[1] for p in pairs], np.int32)))


def _attn_fwd(qa, ka, vt, proj):
    s = qa.shape[1]
    t = min(TQ, s)
    it, jt = _tri_steps(s // t, True)
    hp = HEADS_PER_STEP_FWD
    wide = hp * HEAD_DIM
    za_blk = OFF_ZA // wide

    def body(it_ref, jt_ref, q_ref, k_ref, vt_ref, za_ref, attn_ref, oa_ref, qb_ref, m_s, acc_s, pair_s):
        step = pl.program_id(1)
        i, j = it_ref[step], jt_ref[step]

        @pl.when(j == 0)
        def _():
            m_s[...] = jnp.full_like(m_s, NEG)
            acc_s[...] = jnp.zeros_like(acc_s)

        def update(masked):
            for hh in range(hp):
                st = _dot_nt(k_ref[hh], q_ref[hh])
                if masked:
                    st = jnp.where(_causal_t(t), st, NEG)
                m_prev = m_s[hh]
                m_next = jnp.maximum(m_prev, jnp.max(st, axis=0, keepdims=True))
                alpha = jnp.exp(m_prev - m_next)
                pt = jnp.exp(st - m_next).astype(BF16)
                acc_s[hh] = acc_s[hh] * alpha + _dot(vt_ref[hh], pt)
                m_s[hh] = m_next

        @pl.when(j < i)
        def _():
            update(False)

        @pl.when(j == i)
        def _():
            update(True)
            row = lax.broadcasted_iota(jnp.int32, (LANES, t), 0)
            lane = lax.broadcasted_iota(jnp.int32, (t, LANES), 1)
            for hh in range(hp):
                l_row = acc_s[hh, L_ONE_Q:L_ONE_Q + 1, :]
                pair_s[hh * HEAD_DIM:(hh + 1) * HEAD_DIM, :] = acc_s[hh, 0:HEAD_DIM, :] / l_row
                lse3 = _split3(m_s[hh] + jnp.log(l_row))
                tail_t = _place3(row, L_LSE_Q, tuple(-x for x in lse3), 0.0)
                keep_q = jnp.logical_or(lane < L_LSE_Q, lane >= L_END)
                qb_ref[hh] = jnp.where(keep_q, q_ref[hh].astype(F32), tail_t.T).astype(BF16)
            out = pair_s[...].T
            attn_ref[...] = out
            z = za_ref[...].astype(F32)
            oa_ref[...] = (out * (z * _sigmoid(z))).astype(BF16)

    pair_q = pl.BlockSpec((hp, t, LANES), lambda p, n, it_, jt_: (p, it_[n], 0))
    pair_k = pl.BlockSpec((hp, t, LANES), lambda p, n, it_, jt_: (p, jt_[n], 0))
    pair_kt = pl.BlockSpec((hp, LANES, t), lambda p, n, it_, jt_: (p, 0, jt_[n]))
    out_q = pl.BlockSpec((t, wide), lambda p, n, it_, jt_: (it_[n], p))
    return pl.pallas_call(
        body, name="attn_fwd",
        grid_spec=pltpu.PrefetchScalarGridSpec(
            num_scalar_prefetch=2, grid=(HEADS // hp, it.shape[0]),
            in_specs=[pair_q, pair_k, pair_kt,
                      pl.BlockSpec((t, wide), lambda p, n, it_, jt_: (it_[n], za_blk + p))],
            out_specs=[out_q, out_q, pair_q],
            scratch_shapes=[pltpu.VMEM((hp, 1, t), F32), pltpu.VMEM((hp, LANES, t), F32),
                            pltpu.VMEM((wide, t), F32)]),
        out_shape=[jax.ShapeDtypeStruct((s, ATTN_W), F32),
                   jax.ShapeDtypeStruct((s, ATTN_W), BF16),
                   jax.ShapeDtypeStruct((HEADS, s, LANES), BF16)],
        compiler_params=_params(("parallel", "arbitrary")),
    )(it, jt, qa, ka, vt, proj)


def _conv_parts(gb_ref, gc_ref, u_ref, zb_ref, gch_ref, uh_ref, first, w_ref, tm):
    gb, gc = gb_ref[...].astype(F32), gc_ref[...].astype(F32)
    u, zb = u_ref[...].astype(F32), zb_ref[...].astype(F32)
    cu = gc * u
    cu_h = jnp.where(first, 0.0, gch_ref[...].astype(F32) * uh_ref[...].astype(F32))
    prev1, prev2 = _sub_row(cu_h, HALO - 1), _sub_row(cu_h, HALO - 2)
    row = lax.broadcasted_iota(jnp.int32, cu.shape, 0)
    r1 = jnp.where(row == 0, prev1, pltpu.roll(cu, 1, 0))
    r2 = jnp.where(row == 0, prev2, jnp.where(row == 1, prev1, pltpu.roll(cu, 2, 0)))
    conv = w_ref[2:3, :] * cu + w_ref[1:2, :] * r1 + w_ref[0:1, :] * r2
    return gb, gc, u, zb, cu, r1, r2, conv


def _conv_specs(tm, s, width=LANES):
    def tile(off):
        return pl.BlockSpec((tm, width), lambda c, i: (i, off // width + c))

    def before(off):
        return pl.BlockSpec((HALO, width), lambda c, i: (jnp.maximum(i * (tm // HALO) - 1, 0), off // width + c))

    def after(off):
        return pl.BlockSpec((HALO, width),
                            lambda c, i: (jnp.minimum((i + 1) * (tm // HALO), s // HALO - 1), off // width + c))

    return ([tile(OFF_CB), tile(OFF_CC), tile(OFF_CU), tile(OFF_CZ)], [before(OFF_CC), before(OFF_CU)],
            [after(OFF_CB), after(OFF_CZ)])


def _tail(oa, attn, proj, x, target, ada3, wa, wb, wo, conv_w):
    s = x.shape[0]
    tm = min(TM_TAIL, s)
    gab_blk = OFF_GA // (2 * D_MODEL)
    za_blk = OFF_ZA // ATTN_W
    tiles, befores, _ = _conv_specs(tm, s, CONV_W)

    def body(oa_ref, attn_ref, za_ref, gb_ref, gc_ref, u_ref, zb_ref, gch_ref, uh_ref, cw_ref, gab_ref, x_ref, t_ref,
             ada_ref, wa_ref, wb_ref, wo_ref,
             dy_ref, dgab_ref, do_ref, dza_ref, dob_ref, dwo_ref, dwa_ref, dwb_ref, dgate_ref, loss_ref):
        first = pl.program_id(0) == 0

        @pl.when(first)
        def _():
            dwo_ref[...] = jnp.zeros_like(dwo_ref)
            dwa_ref[...] = jnp.zeros_like(dwa_ref)
            dwb_ref[...] = jnp.zeros_like(dwb_ref)
            dgate_ref[...] = jnp.zeros_like(dgate_ref)
            loss_ref[...] = jnp.zeros_like(loss_ref)

        gb, _, _, zb, _, _, _, conv = _conv_parts(gb_ref, gc_ref, u_ref, zb_ref, gch_ref, uh_ref, first, cw_ref, tm)
        ob_v = (gb * conv * (zb * _sigmoid(zb))).astype(BF16)
        oa_v = oa_ref[...]
        wa_v, wb_v, wo_v = wa_ref[...], wb_ref[...], wo_ref[...]
        a2 = _dot(oa_v, wa_v)
        b2 = _dot(ob_v, wb_v)
        sa = _sigmoid(gab_ref[:, 0:D_MODEL].astype(F32))
        sb = _sigmoid(gab_ref[:, D_MODEL:2 * D_MODEL].astype(F32))
        mb = (sa * a2 + sb * b2).astype(BF16)
        mo = _dot(mb, wo_v)
        gate = ada_ref[2:3, :]
        err = (x_ref[...] + gate * mo) - t_ref[...]
        dy = err * (1.0 / D_MODEL)
        dy_ref[...] = dy
        loss_ref[...] += 0.5 * jnp.sum(err * err) * (1.0 / D_MODEL)
        dgate_ref[...] += jnp.sum(dy * mo, axis=0, keepdims=True)
        dmo = (dy * gate).astype(BF16)
        dmerged = _dot_nt(dmo, wo_v)
        dwo_ref[...] += _dot_tn(mb, dmo)
        da2 = (dmerged * sa).astype(BF16)
        db2 = (dmerged * sb).astype(BF16)
        dgab_ref[:, 0:D_MODEL] = (dmerged * a2 * (sa * (1.0 - sa))).astype(BF16)
        dgab_ref[:, D_MODEL:2 * D_MODEL] = (dmerged * b2 * (sb * (1.0 - sb))).astype(BF16)
        doa = _dot_nt(da2, wa_v)
        dob_ref[...] = _dot_nt(db2, wb_v)
        dwa_ref[...] += _dot_tn(oa_v, da2)
        dwb_ref[...] += _dot_tn(ob_v, db2)

        lane = lax.broadcasted_iota(jnp.int32, (tm, LANES), 1)
        lo = lane < HEAD_DIM
        for pr in range(ATTN_W // LANES):
            sl = slice(pr * LANES, (pr + 1) * LANES)
            g, a, z = doa[:, sl], attn_ref[:, sl], za_ref[:, sl].astype(F32)
            sg = _sigmoid(z)
            dat = (g * (z * sg)).astype(BF16).astype(F32)
            prod = dat * a
            dza_ref[:, sl] = (g * a * (sg * (1.0 + z * (1.0 - sg)))).astype(BF16)
            for hh in range(2):
                sel = lo if hh == 0 else jnp.logical_not(lo)
                delta3 = _split3(jnp.sum(jnp.where(sel, prod, 0.0), axis=-1, keepdims=True))
                dh = dat if hh == 0 else pltpu.roll(dat, HEAD_DIM, 1)
                tail_lanes = _place3(lane, L_ONE_Q, tuple(-d for d in delta3), 0.0)
                do_ref[2 * pr + hh] = jnp.where(lo, dh, tail_lanes).astype(BF16)

    half = pl.BlockSpec((tm, ATTN_W), lambda i: (i, 0))
    full = pl.BlockSpec((tm, D_MODEL), lambda i: (i, 0))

    def const(shape):
        return pl.BlockSpec(shape, lambda i: (0, 0))

    def one_axis(spec):
        return pl.BlockSpec(spec.block_shape, lambda i, f=spec.index_map: f(0, i))

    return pl.pallas_call(
        body, name="tail", grid=(s // tm,),
        in_specs=[half, half, pl.BlockSpec((tm, ATTN_W), lambda i: (i, za_blk))]
        + [one_axis(sp) for sp in tiles + befores]
        + [const((3, CONV_W)), pl.BlockSpec((tm, 2 * D_MODEL), lambda i: (i, gab_blk)), full, full,
           const((3, D_MODEL)), const((ATTN_W, D_MODEL)), const((CONV_W, D_MODEL)), const((D_MODEL, D_MODEL))],
        out_specs=[full, pl.BlockSpec((tm, 2 * D_MODEL), lambda i: (i, 0)),
                   pl.BlockSpec((HEADS, tm, LANES), lambda i: (0, i, 0)), half, half,
                   const((D_MODEL, D_MODEL)), const((ATTN_W, D_MODEL)), const((CONV_W, D_MODEL)),
                   const((1, D_MODEL)), const((1, LANES))],
        out_shape=[jax.ShapeDtypeStruct((s, D_MODEL), F32),
                   jax.ShapeDtypeStruct((s, 2 * D_MODEL), BF16),
                   jax.ShapeDtypeStruct((HEADS, s, LANES), BF16),
                   jax.ShapeDtypeStruct((s, ATTN_W), BF16),
                   jax.ShapeDtypeStruct((s, CONV_W), F32),
                   jax.ShapeDtypeStruct((D_MODEL, D_MODEL), F32),
                   jax.ShapeDtypeStruct((ATTN_W, D_MODEL), F32),
                   jax.ShapeDtypeStruct((CONV_W, D_MODEL), F32),
                   jax.ShapeDtypeStruct((1, D_MODEL), F32),
                   jax.ShapeDtypeStruct((1, LANES), F32)],
        compiler_params=_params(("arbitrary",)),
    )(oa, attn, proj, *([proj] * 6), conv_w, proj, x, target, ada3, wa, wb, wo)


def _attn_bwd(qb, ka, kt, va, do, proj, qg, kg):
    s = qb.shape[1]
    t = min(TQ, s)
    nt = s // t
    hp = HEADS_PER_STEP
    wide = hp * HEAD_DIM
    scale = HEAD_DIM ** -0.5
    it, jt = _tri_steps(nt, False)

    def body(it_ref, jt_ref, q_ref, k_ref, kt_ref, v_ref, do_ref, qraw_ref, kraw_ref, qg_ref, kg_ref,
             dq_ref, dk_ref, dv_ref, dqg_ref, dkg_ref, dcum_ref, dqt_s, dk_s, dv_s, rows_s):
        grp, step = pl.program_id(0), pl.program_id(1)
        i, j = it_ref[step], jt_ref[step]
        lane = lax.broadcasted_iota(jnp.int32, (t, LANES), 1)
        lo = lane < HEAD_DIM

        @pl.when(step == 0)
        def _():
            dqt_s[...] = jnp.zeros_like(dqt_s)
            dqg_ref[...] = jnp.zeros_like(dqg_ref)
            dkg_ref[...] = jnp.zeros_like(dkg_ref)

        @pl.when(i == j)
        def _():
            dk_s[...] = jnp.zeros_like(dk_s)
            dv_s[...] = jnp.zeros_like(dv_s)

        def update(masked):
            for hh in range(hp):
                qh, doh = q_ref[hh], do_ref[hh]
                st = _dot_nt(k_ref[hh], qh)
                if masked:
                    st = jnp.where(_causal_t(t), st, NEG)
                pt = jnp.exp(st)
                dst = (pt * _dot_nt(v_ref[hh], doh)).astype(BF16)
                dv_s[hh] += _dot(pt.astype(BF16), doh)
                dk_s[hh] += _dot(dst, qh)
                dqt_s[hh, i] += _dot(kt_ref[hh], dst)

        def pair(a, b):
            return jnp.where(lo, a, pltpu.roll(b, HEAD_DIM, 1))

        def norm_bwd(raw, dy, g, dg_ref, out_ref, sl):
            r = lax.rsqrt(_seg_sum(raw * raw, lo) * (1.0 / HEAD_DIM) + EPS)
            xhat = raw * r
            dg_ref[:, sl] += jnp.sum(dy * xhat, axis=0, keepdims=True)
            dxh = dy * g
            dx = r * (dxh - xhat * (_seg_sum(dxh * xhat, lo) * (1.0 / HEAD_DIM)))
            out_ref[:, sl] = dx.astype(BF16)

        @pl.when(i > j)
        def _():
            update(False)

        @pl.when(i == j)
        def _():
            update(True)
            dq_rows = [dqt_s[hh, i].T for hh in range(hp)]
            rows = jnp.zeros((t, LANES), F32)
            for hh in range(hp):
                rows = jnp.where(lane == grp * hp + hh, _lane_col(dq_rows[hh], L_F_Q), rows)
            rows_s[...] = rows
            for pr in range(hp // 2):
                sl = slice(pr * LANES, (pr + 1) * LANES)
                norm_bwd(qraw_ref[:, sl].astype(F32), pair(dq_rows[2 * pr], dq_rows[2 * pr + 1]) * scale,
                         qg_ref[:, sl], dqg_ref, dq_ref, sl)

        @pl.when(i == nt - 1)
        def _():
            dcum = rows_s[...]
            for hh in range(hp):
                dcum = jnp.where(lane == grp * hp + hh, dcum - _lane_col(dk_s[hh], L_ONE_Q), dcum)
            dcum_ref[0] = dcum
            for pr in range(hp // 2):
                sl = slice(pr * LANES, (pr + 1) * LANES)
                norm_bwd(kraw_ref[:, sl].astype(F32), pair(dk_s[2 * pr], dk_s[2 * pr + 1]),
                         kg_ref[:, sl], dkg_ref, dk_ref, sl)
                dv_ref[:, sl] = pair(dv_s[2 * pr], dv_s[2 * pr + 1]).astype(BF16)

    pair_q = pl.BlockSpec((hp, t, LANES), lambda p, n, it_, jt_: (p, it_[n], 0))
    pair_k = pl.BlockSpec((hp, t, LANES), lambda p, n, it_, jt_: (p, jt_[n], 0))
    pair_kt = pl.BlockSpec((hp, LANES, t), lambda p, n, it_, jt_: (p, 0, jt_[n]))
    tok = pl.BlockSpec((t, wide), lambda p, n, it_, jt_: (jt_[n], p))
    gain = pl.BlockSpec((1, wide), lambda p, n, it_, jt_: (0, p))
    return pl.pallas_call(
        body, name="attn_bwd",
        grid_spec=pltpu.PrefetchScalarGridSpec(
            num_scalar_prefetch=2, grid=(HEADS // hp, it.shape[0]),
            in_specs=[pair_q, pair_k, pair_kt, pair_k, pair_q,
                      pl.BlockSpec((t, wide), lambda p, n, it_, jt_: (jt_[n], OFF_Q // wide + p)),
                      pl.BlockSpec((t, wide), lambda p, n, it_, jt_: (jt_[n], OFF_K // wide + p)), gain, gain],
            out_specs=[tok, tok, tok, gain, gain,
                       pl.BlockSpec((1, t, LANES), lambda p, n, it_, jt_: (p, jt_[n], 0))],
            scratch_shapes=[pltpu.VMEM((hp, nt, LANES, t), F32), pltpu.VMEM((hp, t, LANES), F32),
                            pltpu.VMEM((hp, t, LANES), F32), pltpu.VMEM((t, LANES), F32)]),
        out_shape=[jax.ShapeDtypeStruct((s, ATTN_W), BF16)] * 3
        + [jax.ShapeDtypeStruct((1, ATTN_W), F32)] * 2
        + [jax.ShapeDtypeStruct((HEADS // hp, s, LANES), F32)],
        compiler_params=_params(("parallel", "arbitrary")),
    )(it, jt, qb, ka, kt, va, do, proj, proj, qg, kg)


def _forget_bwd(dcum, fl, bf_pad):
    s = fl.shape[0]
    tc = min(TC_CUM, s)
    n = s // tc

    def body(dc_ref, fl_ref, bf_ref, df_ref, dbf_ref, carry):
        @pl.when(pl.program_id(0) == 0)
        def _():
            carry[...] = jnp.zeros_like(carry)
            dbf_ref[...] = jnp.zeros_like(dbf_ref)
        r = lax.broadcasted_iota(jnp.int32, (tc, tc), 0)
        cidx = lax.broadcasted_iota(jnp.int32, (tc, tc), 1)
        tri = (cidx >= r).astype(F32)
        dc = dc_ref[0]
        for grp in range(1, dcum.shape[0]):
            dc = dc + dc_ref[grp]
        dlf = jnp.dot(tri, dc, preferred_element_type=F32, precision=lax.Precision.HIGHEST) + carry[...]
        carry[...] += jnp.sum(dc, axis=0, keepdims=True)
        lane = lax.broadcasted_iota(jnp.int32, (tc, LANES), 1)
        dfl = jnp.where(lane < HEADS, dlf * _sigmoid(-(fl_ref[...] + bf_ref[...])), 0.0)
        df_ref[...] = dfl.astype(BF16)
        dbf_ref[...] += jnp.sum(dfl, axis=0, keepdims=True)

    rev = pl.BlockSpec((tc, LANES), lambda i: (n - 1 - i, 0))
    vec = pl.BlockSpec((1, LANES), lambda i: (0, 0))
    return pl.pallas_call(
        body, name="forget_bwd", grid=(n,),
        in_specs=[pl.BlockSpec((dcum.shape[0], tc, LANES), lambda i: (0, n - 1 - i, 0)), rev, vec],
        out_specs=[rev, vec],
        out_shape=[jax.ShapeDtypeStruct((s, LANES), BF16), jax.ShapeDtypeStruct((1, LANES), F32)],
        scratch_shapes=[pltpu.VMEM((1, LANES), F32)],
        compiler_params=_params(("arbitrary",)),
    )(dcum, fl, bf_pad)


def _conv_bwd(dob, proj, conv_w):
    s = dob.shape[0]
    tm = min(TM_ELEM, s)
    wd = CONV_W
    tiles, befores, afters = _conv_specs(tm, s, wd)

    def body(dob_ref, dnext_ref, gb_ref, gc_ref, u_ref, zb_ref, gch_ref, uh_ref, gbn_ref, zbn_ref, w_ref,
             dgb_ref, dgc_ref, du_ref, dzb_ref, dw_ref):
        i = pl.program_id(1)

        @pl.when(i == 0)
        def _():
            dw_ref[...] = jnp.zeros_like(dw_ref)
        gb, gc, u, zb, cu, r1, r2, conv = _conv_parts(gb_ref, gc_ref, u_ref, zb_ref, gch_ref, uh_ref, i == 0, w_ref, tm)
        g = dob_ref[...]
        sg = _sigmoid(zb)
        sz = zb * sg
        dconv = g * gb * sz
        zn = zbn_ref[0:8, :].astype(F32)
        dcn = jnp.where(i == pl.num_programs(1) - 1, 0.0,
                        dnext_ref[...] * gbn_ref[0:8, :].astype(F32) * (zn * _sigmoid(zn)))
        nxt1, nxt2 = _sub_row(dcn, 0), _sub_row(dcn, 1)
        row = lax.broadcasted_iota(jnp.int32, (tm, wd), 0)
        f1 = jnp.where(row == tm - 1, nxt1, pltpu.roll(dconv, tm - 1, 0))
        f2 = jnp.where(row == tm - 2, nxt1, jnp.where(row == tm - 1, nxt2, pltpu.roll(dconv, tm - 2, 0)))
        dcu = w_ref[2:3, :] * dconv + w_ref[1:2, :] * f1 + w_ref[0:1, :] * f2
        dgb_ref[...] = (g * conv * sz).astype(BF16)
        dgc_ref[...] = (dcu * u).astype(BF16)
        du_ref[...] = (dcu * gc).astype(BF16)
        dzb_ref[...] = (g * gb * conv * (sg * (1.0 + zb * (1.0 - sg)))).astype(BF16)
        w_row = lax.broadcasted_iota(jnp.int32, (3, wd), 0)
        dw0 = jnp.sum(dconv * r2, axis=0, keepdims=True)
        dw1 = jnp.sum(dconv * r1, axis=0, keepdims=True)
        dw2 = jnp.sum(dconv * cu, axis=0, keepdims=True)
        dw_ref[...] += jnp.where(w_row == 0, dw0, jnp.where(w_row == 1, dw1, dw2))

    blk = pl.BlockSpec((tm, wd), lambda c, i: (i, c))
    nxt = pl.BlockSpec((8, wd), lambda c, i: (jnp.minimum((i + 1) * (tm // 8), s // 8 - 1), c))
    wspec = pl.BlockSpec((3, wd), lambda c, i: (0, c))
    return pl.pallas_call(
        body, name="conv_bwd", grid=(CONV_W // wd, s // tm),
        in_specs=[blk, nxt] + tiles + befores + afters + [wspec],
        out_specs=[blk, blk, blk, blk, wspec],
        out_shape=[jax.ShapeDtypeStruct((s, CONV_W), BF16)] * 4 + [jax.ShapeDtypeStruct((3, CONV_W), F32)],
        compiler_params=_params(("parallel", "arbitrary")),
    )(dob, dob, *([proj] * 8), conv_w)


def _piece_layout(pieces):
    offs, off = [], 0
    for p in pieces:
        offs.append((off, p.shape[1]))
        off += p.shape[1]
    assert off == N_ALL, off
    return offs


def _dw_in(h, pieces, chip_sums):
    s = h.shape[0]
    tk, tn = min(TK_DW, s), TN_DW
    nk = s // tk
    nn = N_MAIN // tn
    main, fpiece = pieces[:-1], pieces[-1]
    layout = _piece_layout(pieces)[:-1]
    n_main = len(main)
    nx = len(chip_sums)

    def body(*refs):
        p_refs, f_ref, h_ref = refs[:n_main], refs[n_main], refs[n_main + 1]
        ins, refs = refs[n_main + 2:n_main + 2 + nx], refs[n_main + 2 + nx:]
        out_ref, outf_ref = refs[:2]
        outs, (acc, accf, send_sems, recv_sems, local_sems) = refs[2:2 + nx], refs[2 + nx:]
        n, k = pl.program_id(0), pl.program_id(1)

        @pl.when(jnp.logical_and(n == 0, k == 0))
        def _():
            for cp in _all_to_all_copies(ins, outs, send_sems, recv_sems, local_sems):
                cp.start()

        @pl.when(k == 0)
        def _():
            acc[...] = jnp.zeros_like(acc)
        hv = h_ref[pl.ds(pl.multiple_of(k * tk, tk), tk), :]
        for p_ref, (off, width) in zip(p_refs, layout):
            @pl.when(jnp.logical_and(n >= off // tn, n < (off + width) // tn))
            def _():
                acc[...] += _dot_tn(p_ref[...], hv)

        @pl.when(k == nk - 1)
        def _():
            out_ref[...] = acc[...].astype(BF16)

        @pl.when(n == 0)
        def _():
            @pl.when(k == 0)
            def _():
                accf[...] = jnp.zeros_like(accf)
            accf[...] += _dot_tn(f_ref[...], hv)

            @pl.when(k == nk - 1)
            def _():
                outf_ref[...] = accf[...].astype(BF16)

        @pl.when(jnp.logical_and(n == nn - 1, k == nk - 1))
        def _():
            for cp in _all_to_all_copies(ins, outs, send_sems, recv_sems, local_sems):
                cp.wait()

    def piece_spec(off, width):
        lo, hi = off // tn, (off + width) // tn

        def index(n, k):
            active = jnp.logical_and(n >= lo, n < hi)
            return jnp.where(active, k, 0), jnp.clip(n - lo, 0, hi - lo - 1)
        return pl.BlockSpec((tk, tn), index)

    any_spec = pl.BlockSpec(memory_space=pl.ANY)
    res = pl.pallas_call(
        body, name="dw_in", grid=(nn, nk),
        in_specs=[piece_spec(off, width) for off, width in layout]
        + [pl.BlockSpec((tk, N_FPAD), lambda n, k: (jnp.where(n == 0, k, 0), 0)),
           pl.BlockSpec((s, D_MODEL), lambda n, k: (0, 0))] + [any_spec] * nx,
        out_specs=[pl.BlockSpec((tn, D_MODEL), lambda n, k: (n, 0)),
                   pl.BlockSpec((N_FPAD, D_MODEL), lambda n, k: (0, 0))] + [any_spec] * nx,
        out_shape=[jax.ShapeDtypeStruct((N_MAIN, D_MODEL), BF16), jax.ShapeDtypeStruct((N_FPAD, D_MODEL), BF16)]
        + [jax.ShapeDtypeStruct(a.shape, a.dtype) for a in chip_sums],
        scratch_shapes=[pltpu.VMEM((tn, D_MODEL), F32), pltpu.VMEM((N_FPAD, D_MODEL), F32)] + _gather_sems(nx),
        compiler_params=_params(("arbitrary", "arbitrary")),
    )(*main, fpiece, h, *chip_sums)
    return res[:2], res[2:]


def _dh_and_dx(pieces, w_all_t, x, dy, ada3, norm_g, chip_sums):
    s = x.shape[0]
    tm = min(TM_DH, s)
    nt = s // tm
    n = len(chip_sums)
    npc = len(pieces)
    layout = _piece_layout(pieces)

    def body(*refs):
        p_refs, refs = refs[:npc], refs[npc:]
        wt_ref, x_ref, dy_ref, ada_ref, g_ref = refs[:5]
        ins, refs = refs[5:5 + n], refs[5 + n:]
        gx_ref, dsh_ref, dsc_ref, dg_ref = refs[:4]
        outs, (send_sems, recv_sems, local_sems) = refs[4:4 + n], refs[4 + n:]
        i = pl.program_id(0)

        @pl.when(i == 0)
        def _():
            for cp in _chip_copies(ins, outs, send_sems, recv_sems, local_sems):
                cp.start()
            dsh_ref[...] = jnp.zeros_like(dsh_ref)
            dsc_ref[...] = jnp.zeros_like(dsc_ref)
            dg_ref[...] = jnp.zeros_like(dg_ref)

        dh = None
        for p_ref, (off, width) in zip(p_refs, layout):
            part = _dot(p_ref[...], wt_ref[off:off + width, :])
            dh = part if dh is None else dh + part
        xv = x_ref[...]
        r = lax.rsqrt(jnp.mean(xv * xv, axis=-1, keepdims=True) + EPS)
        xhat = xv * r
        g = g_ref[...]
        one_sc = 1.0 + ada_ref[1:2, :]
        dsh_ref[...] += jnp.sum(dh, axis=0, keepdims=True)
        dsc_ref[...] += jnp.sum(dh * (xhat * g), axis=0, keepdims=True)
        dg_ref[...] += jnp.sum(dh * xhat, axis=0, keepdims=True) * one_sc
        dxh = dh * (g * one_sc)
        dx = r * (dxh - xhat * jnp.mean(dxh * xhat, axis=-1, keepdims=True))
        gx_ref[...] = dy_ref[...] + dx

        @pl.when(i == nt - 1)
        def _():
            for cp in _chip_copies(ins, outs, send_sems, recv_sems, local_sems):
                cp.wait()

    full = pl.BlockSpec((tm, D_MODEL), lambda i: (i, 0))
    vec = pl.BlockSpec((1, D_MODEL), lambda i: (0, 0))
    any_spec = pl.BlockSpec(memory_space=pl.ANY)
    res = pl.pallas_call(
        body, name="dh_dx", grid=(nt,),
        in_specs=[pl.BlockSpec((tm, p.shape[1]), lambda i: (i, 0)) for p in pieces]
        + [pl.BlockSpec((N_ALL, D_MODEL), lambda i: (0, 0)), full, full,
           pl.BlockSpec((3, D_MODEL), lambda i: (0, 0)), vec] + [any_spec] * n,
        out_specs=[full, vec, vec, vec] + [any_spec] * n,
        out_shape=[jax.ShapeDtypeStruct((s, D_MODEL), F32)] + [jax.ShapeDtypeStruct((1, D_MODEL), F32)] * 3
        + [jax.ShapeDtypeStruct(a.shape, a.dtype) for a in chip_sums],
        scratch_shapes=[pltpu.SemaphoreType.DMA((n * 3,)), pltpu.SemaphoreType.DMA((n * 3,)),
                        pltpu.SemaphoreType.DMA((n,))],
        compiler_params=_params(("arbitrary",)),
    )(*pieces, w_all_t, x, dy, ada3, norm_g, *chip_sums)
    return res[:4], res[4:]


def _sum_small(vec_all, qg_parts, kg_parts):
    def body(v_ref, q_ref, k_ref, tot_ref, gq_ref, gk_ref):
        tot = v_ref[0:1, :]
        for p in range(1, N_DEV):
            tot = tot + v_ref[p:p + 1, :]
        tot_ref[...] = tot
        gq_ref[...] = jnp.sum(q_ref[...], axis=0, keepdims=True)
        gk_ref[...] = jnp.sum(k_ref[...], axis=0, keepdims=True)

    n = vec_all.shape[-1]
    return pl.pallas_call(
        body, name="sum_small",
        out_shape=[jax.ShapeDtypeStruct((1, n), F32),
                   jax.ShapeDtypeStruct((1, HEAD_DIM), F32), jax.ShapeDtypeStruct((1, HEAD_DIM), F32)],
        compiler_params=_params(),
    )(vec_all, qg_parts, kg_parts)


def _grad_w_ada(c_cols, dada_rows):
    def body(c_ref, d_ref, out_ref):
        acc = c_ref[0] * d_ref[0]
        for b in range(1, N_DEV):
            acc = acc + c_ref[b] * d_ref[b]
        out_ref[...] = acc

    return pl.pallas_call(
        body, name="grad_w_ada",
        out_shape=jax.ShapeDtypeStruct((D_MODEL, ADA_SHARD), F32),
        compiler_params=_params(),
    )(c_cols, dada_rows)


def _adam_step(w, m, v, g):
    c1 = 1.0 / (1.0 - ADAM_B1 ** ADAM_STEP)
    c2 = 1.0 / (1.0 - ADAM_B2 ** ADAM_STEP)
    m_new = ADAM_B1 * m + (1.0 - ADAM_B1) * g
    v_new = ADAM_B2 * v + (1.0 - ADAM_B2) * (g * g)
    return -ADAM_LR * ((m_new * c1) / (jnp.sqrt(v_new * c2) + ADAM_EPS) + ADAM_WD * w), m_new, v_new


def _adamw_small(params, name):
    n = len(params)
    stacked = [p[3].ndim == p[0].ndim + 1 for p in params]

    def body(*refs):
        ins, outs = refs[:4 * n], refs[4 * n:]
        for k in range(n):
            w_ref, m_ref, v_ref, g_ref = ins[4 * k:4 * k + 4]
            go_ref, d_ref, mo_ref, vo_ref = outs[4 * k:4 * k + 4]
            if stacked[k]:
                g = g_ref[0].astype(F32)
                for p in range(1, g_ref.shape[0]):
                    g = g + g_ref[p].astype(F32)
            else:
                g = g_ref[...]
            go_ref[...] = g
            d_ref[...], mo_ref[...], vo_ref[...] = _adam_step(w_ref[...], m_ref[...], v_ref[...], g)

    res = pl.pallas_call(
        body, name=name,
        out_shape=[jax.ShapeDtypeStruct(p[0].shape, F32) for p in params for _ in range(4)],
        compiler_params=_params(),
    )(*[a for p in params for a in p])
    return [tuple(res[4 * k:4 * k + 4]) for k in range(n)]


def _adamw(w, m, v, g_parts, name):
    rows, cols = w.shape
    n_parts = g_parts.shape[0]
    tr = 256 if rows % 256 == 0 else rows
    tc = 256 if (tr == rows and rows > 256 and cols % 256 == 0) else cols

    def body(w_ref, m_ref, v_ref, g_ref, go_ref, d_ref, mo_ref, vo_ref):
        g = g_ref[0].astype(F32)
        for p in range(1, n_parts):
            g = g + g_ref[p].astype(F32)
        go_ref[...] = g
        d_ref[...], mo_ref[...], vo_ref[...] = _adam_step(w_ref[...], m_ref[...], v_ref[...], g)

    blk = pl.BlockSpec((tr, tc), lambda i, j: (i, j))
    return pl.pallas_call(
        body, name=name, grid=(rows // tr, cols // tc),
        in_specs=[blk, blk, blk, pl.BlockSpec((n_parts, tr, tc), lambda i, j: (0, i, j))],
        out_specs=[blk] * 4,
        out_shape=[jax.ShapeDtypeStruct((rows, cols), F32)] * 4,
        compiler_params=_params(("parallel", "parallel")),
    )(w, m, v, g_parts)


_O_F = 1536


def _to_internal(wt_g):
    wf = wt_g.reshape(IN_WIDTH, D_MODEL)
    f = jnp.pad(wf[_O_F:_O_F + HEADS], ((0, N_FPAD - HEADS), (0, 0)))
    return jnp.concatenate([wf[:_O_F], wf[_O_F + HEADS:], f], axis=0)


def _slabs_by_core(dwt, dwt_f):
    sources = ((dwt, 0, _O_F, 0), (dwt_f, _O_F, _O_F + HEADS, _O_F), (dwt, _O_F + HEADS, IN_WIDTH, HEADS))

    def slab(p):
        lo, hi = p * IN_SHARD, (p + 1) * IN_SHARD
        parts = []
        for src, o_lo, o_hi, shift in sources:
            a, b = max(lo, o_lo), min(hi, o_hi)
            if a < b:
                parts.append(src[a - shift:b - shift])
        return parts[0] if len(parts) == 1 else jnp.concatenate(parts, axis=0)

    return jnp.stack([jnp.stack([slab(2 * chip + core) for chip in range(4)]) for core in range(2)])


def kernel(x, c, w_ada, b_ada, norm_g, w_in, b_f, q_norm_g, k_norm_g, conv_w, w_attn_out, w_conv_out, w_o, loss_target, m_w_ada, m_b_ada, m_norm_g, m_w_in, m_b_f, m_q_norm_g, m_k_norm_g, m_conv_w, m_w_attn_out, m_w_conv_out, m_w_o, v_w_ada, v_b_ada, v_norm_g, v_w_in, v_b_f, v_q_norm_g, v_k_norm_g, v_conv_w, v_w_attn_out, v_w_conv_out, v_w_o):
    me = 4 * lax.axis_index("x") + 2 * lax.axis_index("y") + lax.axis_index("c")
    s = x.shape[1]
    x2, t2 = x[0], loss_target[0]

    w_in_g, c_all, ada_g = _gather_weights_and_ada(w_in[0].T.astype(BF16), c, w_ada[0])
    ada_mine = lax.dynamic_index_in_dim(ada_g[:, :, 0, :], me, axis=1, keepdims=False)
    ada3 = (ada_mine.reshape(1, 3 * D_MODEL) + b_ada).reshape(3, D_MODEL)
    w_all_t = _to_internal(w_in_g)
    qg = jnp.tile(q_norm_g, (1, HEADS))
    kg = jnp.tile(k_norm_g, (1, HEADS))
    bf_pad = jnp.pad(b_f, ((0, 0), (0, LANES - HEADS)))

    (proj, fl, h), (qa, ka, va, kt, vt), (cw_g, wa_g, wb_g, wo_g) = _proj_fwd(
        x2, ada3, norm_g, w_all_t, bf_pad, qg, kg,
        [conv_w[0], w_attn_out[0].astype(BF16), w_conv_out[0].astype(BF16), w_o[0].astype(BF16)])
    wa = jnp.transpose(wa_g, (1, 0, 2)).reshape(ATTN_W, D_MODEL)
    wb = jnp.transpose(wb_g, (1, 0, 2)).reshape(CONV_W, D_MODEL)
    wo = wo_g.reshape(D_MODEL, D_MODEL)
    cw = jnp.transpose(cw_g, (1, 0, 2)).reshape(3, CONV_W)
    attn, oa, qb = _attn_fwd(qa, ka, vt, proj)
    (dy, dgab, do, dza, dob, dwo, dwa, dwb, dgate, loss_part) = _tail(oa, attn, proj, x2, t2, ada3, wa, wb, wo, cw)

    core = lax.axis_index("c").astype(jnp.int32).reshape(1)
    small = [jnp.transpose(dwa.reshape(ATTN_W, N_DEV, LANES), (1, 0, 2)).astype(BF16),
             jnp.transpose(dwb.reshape(CONV_W, N_DEV, LANES), (1, 0, 2)).astype(BF16),
             dwo.reshape(N_DEV, D_MODEL // N_DEV, D_MODEL).astype(BF16)]
    dq, dk, dv, dqg, dkg, dcum = _attn_bwd(qb, ka, kt, va, do, proj, qg, kg)
    df, dbf = _forget_bwd(dcum, fl, bf_pad)
    dcb, dcc, dcu, dcz, dcw = _conv_bwd(dob, proj, cw)
    pieces = [dq, dk, dv, dza, dcb, dcc, dcu, dcz, dgab, df]
    (dw_main, dw_f), (g_wa_parts, g_wb_parts, g_wo_parts) = _dw_in(h, pieces, small)

    slabs_in = _slabs_by_core(dw_main, dw_f)
    (theirs_in,) = _sibling_swap([slabs_in], "swap_w_in")
    (grad_x, dshift, dscale, dnormg), (g_in_parts,) = _dh_and_dx(
        pieces, w_all_t, x2, dy, ada3, norm_g, [_pair_sum(slabs_in, theirs_in, core, "pair_sum_w_in")])
    vec = jnp.concatenate([dshift, dscale, dgate, dnormg, dbf, dcw.reshape(1, 3 * CONV_W), loss_part, dqg, dkg],
                          axis=1)
    (vec_all,) = _gather_direct([vec], "gather_small")
    vec_all = vec_all.reshape(N_DEV, vec.shape[1])
    n_main = 4 * D_MODEL + LANES + 3 * CONV_W + LANES
    tot, g_qg, g_kg = _sum_small(
        vec_all[:, :n_main],
        vec_all[:, n_main:n_main + ATTN_W].reshape(N_DEV * HEADS, HEAD_DIM),
        vec_all[:, n_main + ATTN_W:].reshape(N_DEV * HEADS, HEAD_DIM))
    g_b_ada = tot[:, 0:3 * D_MODEL]
    g_norm_g = tot[:, 3 * D_MODEL:4 * D_MODEL]
    g_b_f = tot[:, 4 * D_MODEL:4 * D_MODEL + HEADS]
    g_cw_full = tot[:, 4 * D_MODEL + LANES:4 * D_MODEL + LANES + 3 * CONV_W].reshape(3, CONV_W)
    g_cw = lax.dynamic_slice(g_cw_full, (0, me * (CONV_W // N_DEV)), (3, CONV_W // N_DEV))
    dada_mine = lax.dynamic_slice(vec_all[:, 0:3 * D_MODEL], (0, me * ADA_SHARD), (N_DEV, ADA_SHARD))
    g_w_ada = _grad_w_ada(jnp.transpose(c_all, (0, 2, 1)), dada_mine.reshape(N_DEV, 1, ADA_SHARD))

    upd = {}
    upd["w_ada"] = _adamw(w_ada[0], m_w_ada[0], v_w_ada[0], g_w_ada[None], "adamw_w_ada")
    upd["w_in"] = [u.T for u in _adamw(w_in[0].T, m_w_in[0].T, v_w_in[0].T, g_in_parts, "adamw_w_in")]
    small_names = ["b_ada", "norm_g", "b_f", "q_norm_g", "k_norm_g", "conv_w", "w_attn_out", "w_conv_out", "w_o"]
    small_upd = _adamw_small(
        [(b_ada, m_b_ada, v_b_ada, g_b_ada), (norm_g, m_norm_g, v_norm_g, g_norm_g), (b_f, m_b_f, v_b_f, g_b_f),
         (q_norm_g, m_q_norm_g, v_q_norm_g, g_qg), (k_norm_g, m_k_norm_g, v_k_norm_g, g_kg),
         (conv_w[0], m_conv_w[0], v_conv_w[0], g_cw),
         (w_attn_out[0], m_w_attn_out[0], v_w_attn_out[0], g_wa_parts),
         (w_conv_out[0], m_w_conv_out[0], v_w_conv_out[0], g_wb_parts),
         (w_o[0], m_w_o[0], v_w_o[0], g_wo_parts)], "adamw_small")
    upd.update(zip(small_names, small_upd))

    names = ["w_ada", "b_ada", "norm_g", "w_in", "b_f", "q_norm_g", "k_norm_g", "conv_w",
             "w_attn_out", "w_conv_out", "w_o"]
    lead = {"w_ada", "w_in", "conv_w", "w_attn_out", "w_conv_out", "w_o"}
    fix = lambda n, a: a[None] if n in lead else a
    loss = tot[0, n_main - LANES]
    outs = [loss, grad_x[None]]
    for k in range(4):
        outs += [fix(n, upd[n][k]) for n in names]
    return tuple(outs)
```

```python
import functools

import numpy as np
import jax
import jax.numpy as jnp
from jax import lax
from jax.experimental import pallas as pl
from jax.experimental.pallas import tpu as pltpu

F32 = jnp.float32
BF16 = jnp.bfloat16

D_MODEL = 1024
HEADS = 8
HEAD_DIM = 64
ATTN_W = 512
CONV_W = 512
N_DEV = 8
IN_WIDTH = 6152
IN_SHARD = IN_WIDTH // N_DEV
N_MAIN = 6144
N_FPAD = 128
N_ALL = N_MAIN + N_FPAD
ADA_SHARD = 3 * D_MODEL // N_DEV
EPS = 1e-6
NEG = -1e30

ADAM_LR = 0.001
ADAM_B1 = 0.9
ADAM_B2 = 0.999
ADAM_EPS = 1e-08
ADAM_WD = 0.01
ADAM_STEP = 10

LANES = 128
VMEM_LIMIT = 56 * 1024 * 1024

TM_PROJ = 512
TN_PROJ = 1024
TM_ELEM = 512
TQ = 512
HEADS_PER_STEP = 8
HEADS_PER_STEP_FWD = 8
TM_TAIL = 256
TC_CUM = 512
TK_DW = 2048
TN_DW = 512
TM_DH = 512
HALO = 16

OFF_Q, OFF_K, OFF_V, OFF_ZA, OFF_CB, OFF_CC, OFF_CU, OFF_CZ, OFF_GA, OFF_GB = (
    0, 512, 1024, 1536, 2048, 2560, 3072, 3584, 4096, 5120)


def _params(sem=None):
    return pltpu.CompilerParams(dimension_semantics=sem, vmem_limit_bytes=VMEM_LIMIT)


def _dot(a, b):
    return jnp.dot(a, b, preferred_element_type=F32)


def _dot_nt(a, b):
    return lax.dot_general(a, b, (((1,), (1,)), ((), ())), preferred_element_type=F32)


def _dot_tn(a, b):
    return lax.dot_general(a, b, (((0,), (0,)), ((), ())), preferred_element_type=F32)


def _sigmoid(x):
    return 1.0 / (1.0 + jnp.exp(-x))


def _lane_lo(shape):
    return lax.broadcasted_iota(jnp.int32, shape, len(shape) - 1) < HEAD_DIM


def _seg_sum(z, lo):
    a = jnp.sum(jnp.where(lo, z, 0.0), axis=-1, keepdims=True)
    b = jnp.sum(jnp.where(lo, 0.0, z), axis=-1, keepdims=True)
    return jnp.where(lo, a, b)


def _lane_col(z, lane):
    idx = lax.broadcasted_iota(jnp.int32, z.shape, 1)
    return jnp.sum(jnp.where(idx == lane, z, 0.0), axis=-1, keepdims=True)


def _sub_row(z, row):
    idx = lax.broadcasted_iota(jnp.int32, z.shape, 0)
    return jnp.sum(jnp.where(idx == row, z, 0.0), axis=0, keepdims=True)


def _mesh_pos():
    x, y, c = lax.axis_index("x"), lax.axis_index("y"), lax.axis_index("c")
    return x, y, c, 4 * x + 2 * y + c


def _peer(k, x, y, c):
    px = 1 - x if (k >> 2) & 1 else x
    py = 1 - y if (k >> 1) & 1 else y
    pc = 1 - c if k & 1 else c
    return (px, py, pc), 4 * px + 2 * py + pc


def _gather_copies(ins, outs, send_sems, recv_sems, local_sems):
    x, y, c, me = _mesh_pos()
    copies = []
    for a in range(len(ins)):
        copies.append(pltpu.make_async_copy(ins[a], outs[a].at[me], local_sems.at[a]))
        for k in range(1, N_DEV):
            dev, _ = _peer(k, x, y, c)
            copies.append(pltpu.make_async_remote_copy(
                src_ref=ins[a], dst_ref=outs[a].at[me],
                send_sem=send_sems.at[a * (N_DEV - 1) + k - 1], recv_sem=recv_sems.at[a * (N_DEV - 1) + k - 1],
                device_id=dev, device_id_type=pl.DeviceIdType.MESH))
    return copies


def _gather_sems(n):
    return [pltpu.SemaphoreType.DMA((n * (N_DEV - 1),)), pltpu.SemaphoreType.DMA((n * (N_DEV - 1),)),
            pltpu.SemaphoreType.DMA((n,))]


def _gather_direct(arrs, name):
    n = len(arrs)
    any_spec = pl.BlockSpec(memory_space=pl.ANY)

    def body(*refs):
        copies = _gather_copies(refs[:n], refs[n:2 * n], *refs[2 * n:])
        for cp in copies:
            cp.start()
        for cp in copies:
            cp.wait()

    return pl.pallas_call(
        body, name=name, out_shape=[jax.ShapeDtypeStruct((N_DEV,) + a.shape, a.dtype) for a in arrs],
        in_specs=[any_spec] * n, out_specs=[any_spec] * n, scratch_shapes=_gather_sems(n),
    )(*arrs)


def _ada_phase(c_ref, w_ref, call_ref, adag_ref, mine_ref, send_sems, recv_sems):
    x, y, c, me = _mesh_pos()

    def copy(phase, k, src, dst):
        dev, _ = _peer(k, x, y, c)
        return pltpu.make_async_remote_copy(
            src_ref=src, dst_ref=dst,
            send_sem=send_sems.at[phase * (N_DEV - 1) + k - 1],
            recv_sem=recv_sems.at[phase * (N_DEV - 1) + k - 1],
            device_id=dev, device_id_type=pl.DeviceIdType.MESH)

    call_ref[me] = c_ref[...]
    first = [copy(0, k, c_ref, call_ref.at[me]) for k in range(1, N_DEV)]
    for cp in first:
        cp.start()
    for cp in first:
        cp.wait()
    wb = w_ref[...].astype(BF16)
    for b in range(N_DEV):
        row = jnp.broadcast_to(call_ref[b], (8, D_MODEL)).astype(BF16)
        mine_ref[b] = _sub_row(_dot(row, wb), 0)
    adag_ref[me] = mine_ref[...]
    second = [copy(1, k, mine_ref, adag_ref.at[me]) for k in range(1, N_DEV)]
    for cp in second:
        cp.start()
    for cp in second:
        cp.wait()


def _gather_weights_and_ada(wt_shard, c_row, w_ada_sh):
    any_spec = pl.BlockSpec(memory_space=pl.ANY)
    vm = pl.BlockSpec(memory_space=pltpu.VMEM)

    def body(w_in_ref, c_ref, wada_ref, out_ref, call_ref, adag_ref, mine_ref, send_sems, recv_sems, local_sem,
             ada_send, ada_recv):
        x, y, c, me = _mesh_pos()
        sibling = (x, y, 1 - c)
        chips = [(1 - x, y), (x, 1 - y), (1 - x, 1 - y)]

        def copy(k, src, blk, to):
            return pltpu.make_async_remote_copy(
                src_ref=src, dst_ref=out_ref.at[blk], send_sem=send_sems.at[k], recv_sem=recv_sems.at[k],
                device_id=to, device_id_type=pl.DeviceIdType.MESH)

        local = pltpu.make_async_copy(w_in_ref, out_ref.at[me], local_sem.at[0])
        local.start()
        first = [copy(0, w_in_ref, me, sibling)]
        first += [copy(1 + j, w_in_ref, me, (px, py, c)) for j, (px, py) in enumerate(chips)]
        for cp in first:
            cp.start()
        _ada_phase(c_ref, wada_ref, call_ref, adag_ref, mine_ref, ada_send, ada_recv)
        passed = []
        for j, (px, py) in enumerate(chips):
            blk = 4 * px + 2 * py + c
            copy(1 + j, w_in_ref, blk, (x, y, c)).wait_recv()
            fwd = copy(4 + j, out_ref.at[blk], blk, sibling)
            fwd.start()
            passed.append(fwd)
        copy(0, w_in_ref, 4 * x + 2 * y + 1 - c, (x, y, c)).wait_recv()
        for j, (px, py) in enumerate(chips):
            copy(4 + j, w_in_ref, 4 * px + 2 * py + 1 - c, (x, y, c)).wait_recv()
        for cp in first + passed:
            cp.wait_send()
        local.wait()

    per = N_DEV - 1
    return pl.pallas_call(
        body, name="gather_weights",
        out_shape=[jax.ShapeDtypeStruct((N_DEV,) + wt_shard.shape, wt_shard.dtype),
                   jax.ShapeDtypeStruct((N_DEV, 1, D_MODEL), F32),
                   jax.ShapeDtypeStruct((N_DEV, N_DEV, 1, ADA_SHARD), F32)],
        in_specs=[any_spec, vm, vm], out_specs=[any_spec, vm, vm],
        scratch_shapes=[pltpu.VMEM((N_DEV, 1, ADA_SHARD), F32),
                        pltpu.SemaphoreType.DMA((per,)), pltpu.SemaphoreType.DMA((per,)),
                        pltpu.SemaphoreType.DMA((1,)),
                        pltpu.SemaphoreType.DMA((2 * per,)), pltpu.SemaphoreType.DMA((2 * per,))],
        compiler_params=pltpu.CompilerParams(vmem_limit_bytes=VMEM_LIMIT),
    )(wt_shard, c_row, w_ada_sh)


def _sibling_swap(arrs, name):
    n = len(arrs)
    any_spec = pl.BlockSpec(memory_space=pl.ANY)

    def body(*refs):
        ins, outs = refs[:n], refs[n:2 * n]
        send_sems, recv_sems = refs[2 * n:]
        x, y, c, _ = _mesh_pos()
        copies = [pltpu.make_async_remote_copy(
            src_ref=ins[a].at[1 - c], dst_ref=outs[a], send_sem=send_sems.at[a], recv_sem=recv_sems.at[a],
            device_id=(x, y, 1 - c), device_id_type=pl.DeviceIdType.MESH) for a in range(n)]
        for cp in copies:
            cp.start()
        for cp in copies:
            cp.wait()

    return pl.pallas_call(
        body, name=name,
        out_shape=[jax.ShapeDtypeStruct(a.shape[1:], a.dtype) for a in arrs],
        in_specs=[any_spec] * n, out_specs=[any_spec] * n,
        scratch_shapes=[pltpu.SemaphoreType.DMA((n,)), pltpu.SemaphoreType.DMA((n,))],
    )(*arrs)


def _pair_sum(mine2, theirs, core, name):
    _, _, rows, cols = mine2.shape
    tr = 256 if rows % 256 == 0 else rows

    def body(core_ref, a_ref, b_ref, out_ref):
        out_ref[...] = (a_ref[...].astype(F32) + b_ref[...].astype(F32)).astype(BF16)

    return pl.pallas_call(
        body, name=name,
        grid_spec=pltpu.PrefetchScalarGridSpec(
            num_scalar_prefetch=1, grid=(4, rows // tr),
            in_specs=[pl.BlockSpec((None, None, tr, cols), lambda ch, i, core_: (core_[0], ch, i, 0)),
                      pl.BlockSpec((None, tr, cols), lambda ch, i, core_: (ch, i, 0))],
            out_specs=pl.BlockSpec((None, tr, cols), lambda ch, i, core_: (ch, i, 0))),
        out_shape=jax.ShapeDtypeStruct(theirs.shape, BF16),
        compiler_params=_params(("parallel", "parallel")),
    )(core, mine2, theirs)


def _all_to_all_copies(ins, outs, send_sems, recv_sems, local_sems):
    x, y, c, me = _mesh_pos()
    copies = []
    for a in range(len(ins)):
        copies.append(pltpu.make_async_copy(ins[a].at[me], outs[a].at[me], local_sems.at[a]))
        for k in range(1, N_DEV):
            dev, p = _peer(k, x, y, c)
            copies.append(pltpu.make_async_remote_copy(
                src_ref=ins[a].at[p], dst_ref=outs[a].at[me],
                send_sem=send_sems.at[a * (N_DEV - 1) + k - 1], recv_sem=recv_sems.at[a * (N_DEV - 1) + k - 1],
                device_id=dev, device_id_type=pl.DeviceIdType.MESH))
    return copies


def _chip_copies(ins, outs, send_sems, recv_sems, local_sems):
    x, y, c, _ = _mesh_pos()
    my_chip = 2 * x + y
    chips = [(1 - x, y), (x, 1 - y), (1 - x, 1 - y)]
    copies = []
    for a in range(len(ins)):
        copies.append(pltpu.make_async_copy(ins[a].at[my_chip], outs[a].at[my_chip], local_sems.at[a]))
        for j, (px, py) in enumerate(chips):
            copies.append(pltpu.make_async_remote_copy(
                src_ref=ins[a].at[2 * px + py], dst_ref=outs[a].at[my_chip],
                send_sem=send_sems.at[a * 3 + j], recv_sem=recv_sems.at[a * 3 + j],
                device_id=(px, py, c), device_id_type=pl.DeviceIdType.MESH))
    return copies


def _proj_fwd(x, ada3, norm_g, w_all_t, bf_pad, qg, kg, later):
    s = x.shape[0]
    tm, tn = min(TM_PROJ, s), TN_PROJ
    nt = s // tm
    n = len(later)

    def body(x_ref, ada_ref, g_ref, wt_ref, bf_ref, qg_ref, kg_ref, *rest):
        ins, (proj_ref, fl_ref, h_ref), rows_refs, rest = rest[:n], rest[n:n + 3], rest[n + 3:n + 8], rest[n + 8:]
        outs, (carry, send_sems, recv_sems, local_sems) = rest[:n], rest[n:]
        i = pl.program_id(0)

        @pl.when(i == 0)
        def _():
            carry[...] = jnp.zeros_like(carry)
            for cp in _gather_copies(ins, outs, send_sems, recv_sems, local_sems):
                cp.start()

        xv = x_ref[...]
        r = lax.rsqrt(jnp.mean(xv * xv, axis=-1, keepdims=True) + EPS)
        hv = ((xv * r) * g_ref[...]) * (1.0 + ada_ref[1:2, :]) + ada_ref[0:1, :]
        hb = hv.astype(BF16)
        h_ref[...] = hb
        fl = _dot_nt(hb, wt_ref[N_MAIN:N_ALL, :])
        fl_ref[...] = fl
        for j in range(N_MAIN // tn):
            proj_ref[:, j * tn:(j + 1) * tn] = _dot_nt(hb, wt_ref[j * tn:(j + 1) * tn, :]).astype(BF16)
        _attention_rows(proj_ref, fl, bf_ref, qg_ref, kg_ref, carry, *rows_refs)

        @pl.when(i == nt - 1)
        def _():
            for cp in _gather_copies(ins, outs, send_sems, recv_sems, local_sems):
                cp.wait()

    any_spec = pl.BlockSpec(memory_space=pl.ANY)
    heads = pl.BlockSpec((HEADS, tm, LANES), lambda i: (0, i, 0))
    heads_t = pl.BlockSpec((HEADS, LANES, tm), lambda i: (0, 0, i))
    vec = pl.BlockSpec((1, ATTN_W), lambda i: (0, 0))
    res = pl.pallas_call(
        body, name="proj_fwd", grid=(nt,),
        in_specs=[pl.BlockSpec((tm, D_MODEL), lambda i: (i, 0)),
                  pl.BlockSpec((3, D_MODEL), lambda i: (0, 0)),
                  pl.BlockSpec((1, D_MODEL), lambda i: (0, 0)),
                  pl.BlockSpec((N_ALL, D_MODEL), lambda i: (0, 0)),
                  pl.BlockSpec((1, LANES), lambda i: (0, 0)), vec, vec] + [any_spec] * n,
        out_specs=[pl.BlockSpec((tm, N_MAIN), lambda i: (i, 0)),
                   pl.BlockSpec((tm, N_FPAD), lambda i: (i, 0)),
                   pl.BlockSpec((tm, D_MODEL), lambda i: (i, 0)),
                   heads, heads, heads, heads_t, heads_t] + [any_spec] * n,
        out_shape=[jax.ShapeDtypeStruct((s, N_MAIN), BF16),
                   jax.ShapeDtypeStruct((s, N_FPAD), F32),
                   jax.ShapeDtypeStruct((s, D_MODEL), BF16)]
        + [jax.ShapeDtypeStruct((HEADS, s, LANES), BF16)] * 3
        + [jax.ShapeDtypeStruct((HEADS, LANES, s), BF16)] * 2
        + [jax.ShapeDtypeStruct((N_DEV,) + a.shape, a.dtype) for a in later],
        scratch_shapes=[pltpu.VMEM((1, LANES), F32)] + _gather_sems(n),
        compiler_params=_params(("arbitrary",)),
    )(x, ada3, norm_g, w_all_t, bf_pad, qg, kg, *later)
    return res[:3], res[3:8], res[8:]


L_ONE_Q, L_F_Q, L_LSE_Q, L_END = HEAD_DIM, HEAD_DIM + 3, HEAD_DIM + 6, HEAD_DIM + 9


def _split3(f):
    hi = f.astype(BF16).astype(F32)
    r = f - hi
    mid = r.astype(BF16).astype(F32)
    return hi, mid, r - mid


def _place3(lane, first, parts, otherwise):
    a, b, c = parts
    return jnp.where(lane == first, a, jnp.where(lane == first + 1, b, jnp.where(lane == first + 2, c, otherwise)))


def _log_forget(fl, bf):
    z = fl + bf
    lf = jnp.minimum(z, 0.0) - jnp.log1p(jnp.exp(-jnp.abs(z)))
    lane = lax.broadcasted_iota(jnp.int32, z.shape, 1)
    return jnp.where(lane < HEADS, lf, 0.0)


def _attention_rows(p_ref, fl, bf_ref, qg_ref, kg_ref, carry, qa_ref, ka_ref, va_ref, kt_ref, vt_ref):
    tm = fl.shape[0]
    scale = HEAD_DIM ** -0.5
    tri = (lax.broadcasted_iota(jnp.int32, (tm, tm), 1) <= lax.broadcasted_iota(jnp.int32, (tm, tm), 0)).astype(F32)
    cum_v = jnp.dot(tri, _log_forget(fl, bf_ref[...]), preferred_element_type=F32,
                    precision=lax.Precision.HIGHEST) + carry[...]
    carry[...] = _sub_row(cum_v, tm - 1)
    lane = lax.broadcasted_iota(jnp.int32, (tm, LANES), 1)
    lo = lane < HEAD_DIM
    v_tail = jnp.where(lane < L_F_Q, 1.0, 0.0)
    for pr in range(ATTN_W // LANES):
        sl = slice(pr * LANES, (pr + 1) * LANES)
        q2 = p_ref[:, OFF_Q + pr * LANES:OFF_Q + (pr + 1) * LANES].astype(F32)
        k2 = p_ref[:, OFF_K + pr * LANES:OFF_K + (pr + 1) * LANES].astype(F32)
        v2 = p_ref[:, OFF_V + pr * LANES:OFF_V + (pr + 1) * LANES].astype(F32)
        rq = lax.rsqrt(_seg_sum(q2 * q2, lo) * (1.0 / HEAD_DIM) + EPS)
        rk = lax.rsqrt(_seg_sum(k2 * k2, lo) * (1.0 / HEAD_DIM) + EPS)
        qn = ((q2 * rq) * qg_ref[:, sl]) * scale
        kn = (k2 * rk) * kg_ref[:, sl]
        for hh in range(2):
            h = 2 * pr + hh
            f3 = _split3(_lane_col(cum_v, h))
            qh = qn if hh == 0 else pltpu.roll(qn, HEAD_DIM, 1)
            kh = kn if hh == 0 else pltpu.roll(kn, HEAD_DIM, 1)
            vh = v2 if hh == 0 else pltpu.roll(v2, HEAD_DIM, 1)
            q_tail = jnp.where(lane < L_F_Q, 1.0, _place3(lane, L_F_Q, f3, 0.0))
            k_tail = _place3(lane, L_ONE_Q, tuple(-f for f in f3), jnp.where(lane < L_END, 1.0, 0.0))
            k_row = jnp.where(lo, kh, k_tail)
            v_row = jnp.where(lo, vh, v_tail)
            qa_ref[h] = jnp.where(lo, qh, q_tail).astype(BF16)
            ka_ref[h] = k_row.astype(BF16)
            va_ref[h] = v_row.astype(BF16)
            kt_ref[h] = k_row.T.astype(BF16)
            vt_ref[h] = v_row.T.astype(BF16)


def _causal_t(t):
    return lax.broadcasted_iota(jnp.int32, (t, t), 0) <= lax.broadcasted_iota(jnp.int32, (t, t), 1)


def _tri_steps(nt, q_major):
    if q_major:
        pairs = [(i, j) for i in range(nt) for j in range(i + 1)]
    else:
        pairs = [(i, j) for j in range(nt) for i in range(j, nt)]
    return (jnp.asarray(np.array([p[0] for p in pairs], np.int32)),
            jnp.asarray(np.array([p[1] for p in pairs], np.int32)))


def _attn_fwd(qa, ka, vt, proj):
    s = qa.shape[1]
    t = min(TQ, s)
    it, jt = _tri_steps(s // t, True)
    hp = HEADS_PER_STEP_FWD
    wide = hp * HEAD_DIM
    za_blk = OFF_ZA // wide

    def body(it_ref, jt_ref, q_ref, k_ref, vt_ref, za_ref, attn_ref, oa_ref, qb_ref, m_s, acc_s, pair_s):
        step = pl.program_id(1)
        i, j = it_ref[step], jt_ref[step]

        @pl.when(j == 0)
        def _():
            m_s[...] = jnp.full_like(m_s, NEG)
            acc_s[...] = jnp.zeros_like(acc_s)

        def update(masked):
            for hh in range(hp):
                st = _dot_nt(k_ref[hh], q_ref[hh])
                if masked:
                    st = jnp.where(_causal_t(t), st, NEG)
                m_prev = m_s[hh]
                m_next = jnp.maximum(m_prev, jnp.max(st, axis=0, keepdims=True))
                alpha = jnp.exp(m_prev - m_next)
                pt = jnp.exp(st - m_next).astype(BF16)
                acc_s[hh] = acc_s[hh] * alpha + _dot(vt_ref[hh], pt)
                m_s[hh] = m_next

        @pl.when(j < i)
        def _():
            update(False)

        @pl.when(j == i)
        def _():
            update(True)
            row = lax.broadcasted_iota(jnp.int32, (LANES, t), 0)
            lane = lax.broadcasted_iota(jnp.int32, (t, LANES), 1)
            for hh in range(hp):
                l_row = acc_s[hh, L_ONE_Q:L_ONE_Q + 1, :]
                pair_s[hh * HEAD_DIM:(hh + 1) * HEAD_DIM, :] = acc_s[hh, 0:HEAD_DIM, :] / l_row
                lse3 = _split3(m_s[hh] + jnp.log(l_row))
                tail_t = _place3(row, L_LSE_Q, tuple(-x for x in lse3), 0.0)
                keep_q = jnp.logical_or(lane < L_LSE_Q, lane >= L_END)
                qb_ref[hh] = jnp.where(keep_q, q_ref[hh].astype(F32), tail_t.T).astype(BF16)
            out = pair_s[...].T
            attn_ref[...] = out
            z = za_ref[...].astype(F32)
            oa_ref[...] = (out * (z * _sigmoid(z))).astype(BF16)

    pair_q = pl.BlockSpec((hp, t, LANES), lambda p, n, it_, jt_: (p, it_[n], 0))
    pair_k = pl.BlockSpec((hp, t, LANES), lambda p, n, it_, jt_: (p, jt_[n], 0))
    pair_kt = pl.BlockSpec((hp, LANES, t), lambda p, n, it_, jt_: (p, 0, jt_[n]))
    out_q = pl.BlockSpec((t, wide), lambda p, n, it_, jt_: (it_[n], p))
    return pl.pallas_call(
        body, name="attn_fwd",
        grid_spec=pltpu.PrefetchScalarGridSpec(
            num_scalar_prefetch=2, grid=(HEADS // hp, it.shape[0]),
            in_specs=[pair_q, pair_k, pair_kt,
                      pl.BlockSpec((t, wide), lambda p, n, it_, jt_: (it_[n], za_blk + p))],
            out_specs=[out_q, out_q, pair_q],
            scratch_shapes=[pltpu.VMEM((hp, 1, t), F32), pltpu.VMEM((hp, LANES, t), F32),
                            pltpu.VMEM((wide, t), F32)]),
        out_shape=[jax.ShapeDtypeStruct((s, ATTN_W), F32),
                   jax.ShapeDtypeStruct((s, ATTN_W), BF16),
                   jax.ShapeDtypeStruct((HEADS, s, LANES), BF16)],
        compiler_params=_params(("parallel", "arbitrary")),
    )(it, jt, qa, ka, vt, proj)


def _conv_parts(gb_ref, gc_ref, u_ref, zb_ref, gch_ref, uh_ref, first, w_ref, tm):
    gb, gc = gb_ref[...].astype(F32), gc_ref[...].astype(F32)
    u, zb = u_ref[...].astype(F32), zb_ref[...].astype(F32)
    cu = gc * u
    cu_h = jnp.where(first, 0.0, gch_ref[...].astype(F32) * uh_ref[...].astype(F32))
    prev1, prev2 = _sub_row(cu_h, HALO - 1), _sub_row(cu_h, HALO - 2)
    row = lax.broadcasted_iota(jnp.int32, cu.shape, 0)
    r1 = jnp.where(row == 0, prev1, pltpu.roll(cu, 1, 0))
    r2 = jnp.where(row == 0, prev2, jnp.where(row == 1, prev1, pltpu.roll(cu, 2, 0)))
    conv = w_ref[2:3, :] * cu + w_ref[1:2, :] * r1 + w_ref[0:1, :] * r2
    return gb, gc, u, zb, cu, r1, r2, conv


def _conv_specs(tm, s, width=LANES):
    def tile(off):
        return pl.BlockSpec((tm, width), lambda c, i: (i, off // width + c))

    def before(off):
        return pl.BlockSpec((HALO, width), lambda c, i: (jnp.maximum(i * (tm // HALO) - 1, 0), off // width + c))

    def after(off):
        return pl.BlockSpec((HALO, width),
                            lambda c, i: (jnp.minimum((i + 1) * (tm // HALO), s // HALO - 1), off // width + c))

    return ([tile(OFF_CB), tile(OFF_CC), tile(OFF_CU), tile(OFF_CZ)], [before(OFF_CC), before(OFF_CU)],
            [after(OFF_CB), after(OFF_CZ)])


def _tail(oa, attn, proj, x, target, ada3, wa, wb, wo, conv_w):
    s = x.shape[0]
    tm = min(TM_TAIL, s)
    gab_blk = OFF_GA // (2 * D_MODEL)
    za_blk = OFF_ZA // ATTN_W
    tiles, befores, _ = _conv_specs(tm, s, CONV_W)

    def body(oa_ref, attn_ref, za_ref, gb_ref, gc_ref, u_ref, zb_ref, gch_ref, uh_ref, cw_ref, gab_ref, x_ref, t_ref,
             ada_ref, wa_ref, wb_ref, wo_ref,
             dy_ref, dgab_ref, do_ref, dza_ref, dob_ref, dwo_ref, dwa_ref, dwb_ref, dgate_ref, loss_ref):
        first = pl.program_id(0) == 0

        @pl.when(first)
        def _():
            dwo_ref[...] = jnp.zeros_like(dwo_ref)
            dwa_ref[...] = jnp.zeros_like(dwa_ref)
            dwb_ref[...] = jnp.zeros_like(dwb_ref)
            dgate_ref[...] = jnp.zeros_like(dgate_ref)
            loss_ref[...] = jnp.zeros_like(loss_ref)

        gb, _, _, zb, _, _, _, conv = _conv_parts(gb_ref, gc_ref, u_ref, zb_ref, gch_ref, uh_ref, first, cw_ref, tm)
        ob_v = (gb * conv * (zb * _sigmoid(zb))).astype(BF16)
        oa_v = oa_ref[...]
        wa_v, wb_v, wo_v = wa_ref[...], wb_ref[...], wo_ref[...]
        a2 = _dot(oa_v, wa_v)
        b2 = _dot(ob_v, wb_v)
        sa = _sigmoid(gab_ref[:, 0:D_MODEL].astype(F32))
        sb = _sigmoid(gab_ref[:, D_MODEL:2 * D_MODEL].astype(F32))
        mb = (sa * a2 + sb * b2).astype(BF16)
        mo = _dot(mb, wo_v)
        gate = ada_ref[2:3, :]
        err = (x_ref[...] + gate * mo) - t_ref[...]
        dy = err * (1.0 / D_MODEL)
        dy_ref[...] = dy
        loss_ref[...] += 0.5 * jnp.sum(err * err) * (1.0 / D_MODEL)
        dgate_ref[...] += jnp.sum(dy * mo, axis=0, keepdims=True)
        dmo = (dy * gate).astype(BF16)
        dmerged = _dot_nt(dmo, wo_v)
        dwo_ref[...] += _dot_tn(mb, dmo)
        da2 = (dmerged * sa).astype(BF16)
        db2 = (dmerged * sb).astype(BF16)
        dgab_ref[:, 0:D_MODEL] = (dmerged * a2 * (sa * (1.0 - sa))).astype(BF16)
        dgab_ref[:, D_MODEL:2 * D_MODEL] = (dmerged * b2 * (sb * (1.0 - sb))).astype(BF16)
        doa = _dot_nt(da2, wa_v)
        dob_ref[...] = _dot_nt(db2, wb_v)
        dwa_ref[...] += _dot_tn(oa_v, da2)
        dwb_ref[...] += _dot_tn(ob_v, db2)

        lane = lax.broadcasted_iota(jnp.int32, (tm, LANES), 1)
        lo = lane < HEAD_DIM
        for pr in range(ATTN_W // LANES):
            sl = slice(pr * LANES, (pr + 1) * LANES)
            g, a, z = doa[:, sl], attn_ref[:, sl], za_ref[:, sl].astype(F32)
            sg = _sigmoid(z)
            dat = (g * (z * sg)).astype(BF16).astype(F32)
            prod = dat * a
            dza_ref[:, sl] = (g * a * (sg * (1.0 + z * (1.0 - sg)))).astype(BF16)
            for hh in range(2):
                sel = lo if hh == 0 else jnp.logical_not(lo)
                delta3 = _split3(jnp.sum(jnp.where(sel, prod, 0.0), axis=-1, keepdims=True))
                dh = dat if hh == 0 else pltpu.roll(dat, HEAD_DIM, 1)
                tail_lanes = _place3(lane, L_ONE_Q, tuple(-d for d in delta3), 0.0)
                do_ref[2 * pr + hh] = jnp.where(lo, dh, tail_lanes).astype(BF16)

    half = pl.BlockSpec((tm, ATTN_W), lambda i: (i, 0))
    full = pl.BlockSpec((tm, D_MODEL), lambda i: (i, 0))

    def const(shape):
        return pl.BlockSpec(shape, lambda i: (0, 0))

    def one_axis(spec):
        return pl.BlockSpec(spec.block_shape, lambda i, f=spec.index_map: f(0, i))

    return pl.pallas_call(
        body, name="tail", grid=(s // tm,),
        in_specs=[half, half, pl.BlockSpec((tm, ATTN_W), lambda i: (i, za_blk))]
        + [one_axis(sp) for sp in tiles + befores]
        + [const((3, CONV_W)), pl.BlockSpec((tm, 2 * D_MODEL), lambda i: (i, gab_blk)), full, full,
           const((3, D_MODEL)), const((ATTN_W, D_MODEL)), const((CONV_W, D_MODEL)), const((D_MODEL, D_MODEL))],
        out_specs=[full, pl.BlockSpec((tm, 2 * D_MODEL), lambda i: (i, 0)),
                   pl.BlockSpec((HEADS, tm, LANES), lambda i: (0, i, 0)), half, half,
                   const((D_MODEL, D_MODEL)), const((ATTN_W, D_MODEL)), const((CONV_W, D_MODEL)),
                   const((1, D_MODEL)), const((1, LANES))],
        out_shape=[jax.ShapeDtypeStruct((s, D_MODEL), F32),
                   jax.ShapeDtypeStruct((s, 2 * D_MODEL), BF16),
                   jax.ShapeDtypeStruct((HEADS, s, LANES), BF16),
                   jax.ShapeDtypeStruct((s, ATTN_W), BF16),
                   jax.ShapeDtypeStruct((s, CONV_W), F32),
                   jax.ShapeDtypeStruct((D_MODEL, D_MODEL), F32),
                   jax.ShapeDtypeStruct((ATTN_W, D_MODEL), F32),
                   jax.ShapeDtypeStruct((CONV_W, D_MODEL), F32),
                   jax.ShapeDtypeStruct((1, D_MODEL), F32),
                   jax.ShapeDtypeStruct((1, LANES), F32)],
        compiler_params=_params(("arbitrary",)),
    )(oa, attn, proj, *([proj] * 6), conv_w, proj, x, target, ada3, wa, wb, wo)


def _attn_bwd(qb, ka, kt, va, do, proj, qg, kg):
    s = qb.shape[1]
    t = min(TQ, s)
    nt = s // t
    hp = HEADS_PER_STEP
    wide = hp * HEAD_DIM
    scale = HEAD_DIM ** -0.5
    it, jt = _tri_steps(nt, False)

    def body(it_ref, jt_ref, q_ref, k_ref, kt_ref, v_ref, do_ref, qraw_ref, kraw_ref, qg_ref, kg_ref,
             dq_ref, dk_ref, dv_ref, dqg_ref, dkg_ref, dcum_ref, dqt_s, dk_s, dv_s, rows_s):
        grp, step = pl.program_id(0), pl.program_id(1)
        i, j = it_ref[step], jt_ref[step]
        lane = lax.broadcasted_iota(jnp.int32, (t, LANES), 1)
        lo = lane < HEAD_DIM

        @pl.when(step == 0)
        def _():
            dqt_s[...] = jnp.zeros_like(dqt_s)
            dqg_ref[...] = jnp.zeros_like(dqg_ref)
            dkg_ref[...] = jnp.zeros_like(dkg_ref)

        @pl.when(i == j)
        def _():
            dk_s[...] = jnp.zeros_like(dk_s)
            dv_s[...] = jnp.zeros_like(dv_s)

        def update(masked):
            for hh in range(hp):
                qh, doh = q_ref[hh], do_ref[hh]
                st = _dot_nt(k_ref[hh], qh)
                if masked:
                    st = jnp.where(_causal_t(t), st, NEG)
                pt = jnp.exp(st)
                dst = (pt * _dot_nt(v_ref[hh], doh)).astype(BF16)
                dv_s[hh] += _dot(pt.astype(BF16), doh)
                dk_s[hh] += _dot(dst, qh)
                dqt_s[hh, i] += _dot(kt_ref[hh], dst)

        def pair(a, b):
            return jnp.where(lo, a, pltpu.roll(b, HEAD_DIM, 1))

        def norm_bwd(raw, dy, g, dg_ref, out_ref, sl):
            r = lax.rsqrt(_seg_sum(raw * raw, lo) * (1.0 / HEAD_DIM) + EPS)
            xhat = raw * r
            dg_ref[:, sl] += jnp.sum(dy * xhat, axis=0, keepdims=True)
            dxh = dy * g
            dx = r * (dxh - xhat * (_seg_sum(dxh * xhat, lo) * (1.0 / HEAD_DIM)))
            out_ref[:, sl] = dx.astype(BF16)

        @pl.when(i > j)
        def _():
            update(False)

        @pl.when(i == j)
        def _():
            update(True)
            dq_rows = [dqt_s[hh, i].T for hh in range(hp)]
            rows = jnp.zeros((t, LANES), F32)
            for hh in range(hp):
                rows = jnp.where(lane == grp * hp + hh, _lane_col(dq_rows[hh], L_F_Q), rows)
            rows_s[...] = rows
            for pr in range(hp // 2):
                sl = slice(pr * LANES, (pr + 1) * LANES)
                norm_bwd(qraw_ref[:, sl].astype(F32), pair(dq_rows[2 * pr], dq_rows[2 * pr + 1]) * scale,
                         qg_ref[:, sl], dqg_ref, dq_ref, sl)

        @pl.when(i == nt - 1)
        def _():
            dcum = rows_s[...]
            for hh in range(hp):
                dcum = jnp.where(lane == grp * hp + hh, dcum - _lane_col(dk_s[hh], L_ONE_Q), dcum)
            dcum_ref[0] = dcum
            for pr in range(hp // 2):
                sl = slice(pr * LANES, (pr + 1) * LANES)
                norm_bwd(kraw_ref[:, sl].astype(F32), pair(dk_s[2 * pr], dk_s[2 * pr + 1]),
                         kg_ref[:, sl], dkg_ref, dk_ref, sl)
                dv_ref[:, sl] = pair(dv_s[2 * pr], dv_s[2 * pr + 1]).astype(BF16)

    pair_q = pl.BlockSpec((hp, t, LANES), lambda p, n, it_, jt_: (p, it_[n], 0))
    pair_k = pl.BlockSpec((hp, t, LANES), lambda p, n, it_, jt_: (p, jt_[n], 0))
    pair_kt = pl.BlockSpec((hp, LANES, t), lambda p, n, it_, jt_: (p, 0, jt_[n]))
    tok = pl.BlockSpec((t, wide), lambda p, n, it_, jt_: (jt_[n], p))
    gain = pl.BlockSpec((1, wide), lambda p, n, it_, jt_: (0, p))
    return pl.pallas_call(
        body, name="attn_bwd",
        grid_spec=pltpu.PrefetchScalarGridSpec(
            num_scalar_prefetch=2, grid=(HEADS // hp, it.shape[0]),
            in_specs=[pair_q, pair_k, pair_kt, pair_k, pair_q,
                      pl.BlockSpec((t, wide), lambda p, n, it_, jt_: (jt_[n], OFF_Q // wide + p)),
                      pl.BlockSpec((t, wide), lambda p, n, it_, jt_: (jt_[n], OFF_K // wide + p)), gain, gain],
            out_specs=[tok, tok, tok, gain, gain,
                       pl.BlockSpec((1, t, LANES), lambda p, n, it_, jt_: (p, jt_[n], 0))],
            scratch_shapes=[pltpu.VMEM((hp, nt, LANES, t), F32), pltpu.VMEM((hp, t, LANES), F32),
                            pltpu.VMEM((hp, t, LANES), F32), pltpu.VMEM((t, LANES), F32)]),
        out_shape=[jax.ShapeDtypeStruct((s, ATTN_W), BF16)] * 3
        + [jax.ShapeDtypeStruct((1, ATTN_W), F32)] * 2
        + [jax.ShapeDtypeStruct((HEADS // hp, s, LANES), F32)],
        compiler_params=_params(("parallel", "arbitrary")),
    )(it, jt, qb, ka, kt, va, do, proj, proj, qg, kg)


def _forget_bwd(dcum, fl, bf_pad):
    s = fl.shape[0]
    tc = min(TC_CUM, s)
    n = s // tc

    def body(dc_ref, fl_ref, bf_ref, df_ref, dbf_ref, carry):
        @pl.when(pl.program_id(0) == 0)
        def _():
            carry[...] = jnp.zeros_like(carry)
            dbf_ref[...] = jnp.zeros_like(dbf_ref)
        r = lax.broadcasted_iota(jnp.int32, (tc, tc), 0)
        cidx = lax.broadcasted_iota(jnp.int32, (tc, tc), 1)
        tri = (cidx >= r).astype(F32)
        dc = dc_ref[0]
        for grp in range(1, dcum.shape[0]):
            dc = dc + dc_ref[grp]
        dlf = jnp.dot(tri, dc, preferred_element_type=F32, precision=lax.Precision.HIGHEST) + carry[...]
        carry[...] += jnp.sum(dc, axis=0, keepdims=True)
        lane = lax.broadcasted_iota(jnp.int32, (tc, LANES), 1)
        dfl = jnp.where(lane < HEADS, dlf * _sigmoid(-(fl_ref[...] + bf_ref[...])), 0.0)
        df_ref[...] = dfl.astype(BF16)
        dbf_ref[...] += jnp.sum(dfl, axis=0, keepdims=True)

    rev = pl.BlockSpec((tc, LANES), lambda i: (n - 1 - i, 0))
    vec = pl.BlockSpec((1, LANES), lambda i: (0, 0))
    return pl.pallas_call(
        body, name="forget_bwd", grid=(n,),
        in_specs=[pl.BlockSpec((dcum.shape[0], tc, LANES), lambda i: (0, n - 1 - i, 0)), rev, vec],
        out_specs=[rev, vec],
        out_shape=[jax.ShapeDtypeStruct((s, LANES), BF16), jax.ShapeDtypeStruct((1, LANES), F32)],
        scratch_shapes=[pltpu.VMEM((1, LANES), F32)],
        compiler_params=_params(("arbitrary",)),
    )(dcum, fl, bf_pad)


def _conv_bwd(dob, proj, conv_w):
    s = dob.shape[0]
    tm = min(TM_ELEM, s)
    wd = CONV_W
    tiles, befores, afters = _conv_specs(tm, s, wd)

    def body(dob_ref, dnext_ref, gb_ref, gc_ref, u_ref, zb_ref, gch_ref, uh_ref, gbn_ref, zbn_ref, w_ref,
             dgb_ref, dgc_ref, du_ref, dzb_ref, dw_ref):
        i = pl.program_id(1)

        @pl.when(i == 0)
        def _():
            dw_ref[...] = jnp.zeros_like(dw_ref)
        gb, gc, u, zb, cu, r1, r2, conv = _conv_parts(gb_ref, gc_ref, u_ref, zb_ref, gch_ref, uh_ref, i == 0, w_ref, tm)
        g = dob_ref[...]
        sg = _sigmoid(zb)
        sz = zb * sg
        dconv = g * gb * sz
        zn = zbn_ref[0:8, :].astype(F32)
        dcn = jnp.where(i == pl.num_programs(1) - 1, 0.0,
                        dnext_ref[...] * gbn_ref[0:8, :].astype(F32) * (zn * _sigmoid(zn)))
        nxt1, nxt2 = _sub_row(dcn, 0), _sub_row(dcn, 1)
        row = lax.broadcasted_iota(jnp.int32, (tm, wd), 0)
        f1 = jnp.where(row == tm - 1, nxt1, pltpu.roll(dconv, tm - 1, 0))
        f2 = jnp.where(row == tm - 2, nxt1, jnp.where(row == tm - 1, nxt2, pltpu.roll(dconv, tm - 2, 0)))
        dcu = w_ref[2:3, :] * dconv + w_ref[1:2, :] * f1 + w_ref[0:1, :] * f2
        dgb_ref[...] = (g * conv * sz).astype(BF16)
        dgc_ref[...] = (dcu * u).astype(BF16)
        du_ref[...] = (dcu * gc).astype(BF16)
        dzb_ref[...] = (g * gb * conv * (sg * (1.0 + zb * (1.0 - sg)))).astype(BF16)
        w_row = lax.broadcasted_iota(jnp.int32, (3, wd), 0)
        dw0 = jnp.sum(dconv * r2, axis=0, keepdims=True)
        dw1 = jnp.sum(dconv * r1, axis=0, keepdims=True)
        dw2 = jnp.sum(dconv * cu, axis=0, keepdims=True)
        dw_ref[...] += jnp.where(w_row == 0, dw0, jnp.where(w_row == 1, dw1, dw2))

    blk = pl.BlockSpec((tm, wd), lambda c, i: (i, c))
    nxt = pl.BlockSpec((8, wd), lambda c, i: (jnp.minimum((i + 1) * (tm // 8), s // 8 - 1), c))
    wspec = pl.BlockSpec((3, wd), lambda c, i: (0, c))
    return pl.pallas_call(
        body, name="conv_bwd", grid=(CONV_W // wd, s // tm),
        in_specs=[blk, nxt] + tiles + befores + afters + [wspec],
        out_specs=[blk, blk, blk, blk, wspec],
        out_shape=[jax.ShapeDtypeStruct((s, CONV_W), BF16)] * 4 + [jax.ShapeDtypeStruct((3, CONV_W), F32)],
        compiler_params=_params(("parallel", "arbitrary")),
    )(dob, dob, *([proj] * 8), conv_w)


def _piece_layout(pieces):
    offs, off = [], 0
    for p in pieces:
        offs.append((off, p.shape[1]))
        off += p.shape[1]
    assert off == N_ALL, off
    return offs


def _dw_in(h, pieces, chip_sums):
    s = h.shape[0]
    tk, tn = min(TK_DW, s), TN_DW
    nk = s // tk
    nn = N_MAIN // tn
    main, fpiece = pieces[:-1], pieces[-1]
    layout = _piece_layout(pieces)[:-1]
    n_main = len(main)
    nx = len(chip_sums)

    def body(*refs):
        p_refs, f_ref, h_ref = refs[:n_main], refs[n_main], refs[n_main + 1]
        ins, refs = refs[n_main + 2:n_main + 2 + nx], refs[n_main + 2 + nx:]
        out_ref, outf_ref = refs[:2]
        outs, (acc, accf, send_sems, recv_sems, local_sems) = refs[2:2 + nx], refs[2 + nx:]
        n, k = pl.program_id(0), pl.program_id(1)

        @pl.when(jnp.logical_and(n == 0, k == 0))
        def _():
            for cp in _all_to_all_copies(ins, outs, send_sems, recv_sems, local_sems):
                cp.start()

        @pl.when(k == 0)
        def _():
            acc[...] = jnp.zeros_like(acc)
        hv = h_ref[pl.ds(pl.multiple_of(k * tk, tk), tk), :]
        for p_ref, (off, width) in zip(p_refs, layout):
            @pl.when(jnp.logical_and(n >= off // tn, n < (off + width) // tn))
            def _():
                acc[...] += _dot_tn(p_ref[...], hv)

        @pl.when(k == nk - 1)
        def _():
            out_ref[...] = acc[...].astype(BF16)

        @pl.when(n == 0)
        def _():
            @pl.when(k == 0)
            def _():
                accf[...] = jnp.zeros_like(accf)
            accf[...] += _dot_tn(f_ref[...], hv)

            @pl.when(k == nk - 1)
            def _():
                outf_ref[...] = accf[...].astype(BF16)

        @pl.when(jnp.logical_and(n == nn - 1, k == nk - 1))
        def _():
            for cp in _all_to_all_copies(ins, outs, send_sems, recv_sems, local_sems):
                cp.wait()

    def piece_spec(off, width):
        lo, hi = off // tn, (off + width) // tn

        def index(n, k):
            active = jnp.logical_and(n >= lo, n < hi)
            return jnp.where(active, k, 0), jnp.clip(n - lo, 0, hi - lo - 1)
        return pl.BlockSpec((tk, tn), index)

    any_spec = pl.BlockSpec(memory_space=pl.ANY)
    res = pl.pallas_call(
        body, name="dw_in", grid=(nn, nk),
        in_specs=[piece_spec(off, width) for off, width in layout]
        + [pl.BlockSpec((tk, N_FPAD), lambda n, k: (jnp.where(n == 0, k, 0), 0)),
           pl.BlockSpec((s, D_MODEL), lambda n, k: (0, 0))] + [any_spec] * nx,
        out_specs=[pl.BlockSpec((tn, D_MODEL), lambda n, k: (n, 0)),
                   pl.BlockSpec((N_FPAD, D_MODEL), lambda n, k: (0, 0))] + [any_spec] * nx,
        out_shape=[jax.ShapeDtypeStruct((N_MAIN, D_MODEL), BF16), jax.ShapeDtypeStruct((N_FPAD, D_MODEL), BF16)]
        + [jax.ShapeDtypeStruct(a.shape, a.dtype) for a in chip_sums],
        scratch_shapes=[pltpu.VMEM((tn, D_MODEL), F32), pltpu.VMEM((N_FPAD, D_MODEL), F32)] + _gather_sems(nx),
        compiler_params=_params(("arbitrary", "arbitrary")),
    )(*main, fpiece, h, *chip_sums)
    return res[:2], res[2:]


def _dh_and_dx(pieces, w_all_t, x, dy, ada3, norm_g, chip_sums):
    s = x.shape[0]
    tm = min(TM_DH, s)
    nt = s // tm
    n = len(chip_sums)
    npc = len(pieces)
    layout = _piece_layout(pieces)

    def body(*refs):
        p_refs, refs = refs[:npc], refs[npc:]
        wt_ref, x_ref, dy_ref, ada_ref, g_ref = refs[:5]
        ins, refs = refs[5:5 + n], refs[5 + n:]
        gx_ref, dsh_ref, dsc_ref, dg_ref = refs[:4]
        outs, (send_sems, recv_sems, local_sems) = refs[4:4 + n], refs[4 + n:]
        i = pl.program_id(0)

        @pl.when(i == 0)
        def _():
            for cp in _chip_copies(ins, outs, send_sems, recv_sems, local_sems):
                cp.start()
            dsh_ref[...] = jnp.zeros_like(dsh_ref)
            dsc_ref[...] = jnp.zeros_like(dsc_ref)
            dg_ref[...] = jnp.zeros_like(dg_ref)

        dh = None
        for p_ref, (off, width) in zip(p_refs, layout):
            part = _dot(p_ref[...], wt_ref[off:off + width, :])
            dh = part if dh is None else dh + part
        xv = x_ref[...]
        r = lax.rsqrt(jnp.mean(xv * xv, axis=-1, keepdims=True) + EPS)
        xhat = xv * r
        g = g_ref[...]
        one_sc = 1.0 + ada_ref[1:2, :]
        dsh_ref[...] += jnp.sum(dh, axis=0, keepdims=True)
        dsc_ref[...] += jnp.sum(dh * (xhat * g), axis=0, keepdims=True)
        dg_ref[...] += jnp.sum(dh * xhat, axis=0, keepdims=True) * one_sc
        dxh = dh * (g * one_sc)
        dx = r * (dxh - xhat * jnp.mean(dxh * xhat, axis=-1, keepdims=True))
        gx_ref[...] = dy_ref[...] + dx

        @pl.when(i == nt - 1)
        def _():
            for cp in _chip_copies(ins, outs, send_sems, recv_sems, local_sems):
                cp.wait()

    full = pl.BlockSpec((tm, D_MODEL), lambda i: (i, 0))
    vec = pl.BlockSpec((1, D_MODEL), lambda i: (0, 0))
    any_spec = pl.BlockSpec(memory_space=pl.ANY)
    res = pl.pallas_call(
        body, name="dh_dx", grid=(nt,),
        in_specs=[pl.BlockSpec((tm, p.shape[1]), lambda i: (i, 0)) for p in pieces]
        + [pl.BlockSpec((N_ALL, D_MODEL), lambda i: (0, 0)), full, full,
           pl.BlockSpec((3, D_MODEL), lambda i: (0, 0)), vec] + [any_spec] * n,
        out_specs=[full, vec, vec, vec] + [any_spec] * n,
        out_shape=[jax.ShapeDtypeStruct((s, D_MODEL), F32)] + [jax.ShapeDtypeStruct((1, D_MODEL), F32)] * 3
        + [jax.ShapeDtypeStruct(a.shape, a.dtype) for a in chip_sums],
        scratch_shapes=[pltpu.SemaphoreType.DMA((n * 3,)), pltpu.SemaphoreType.DMA((n * 3,)),
                        pltpu.SemaphoreType.DMA((n,))],
        compiler_params=_params(("arbitrary",)),
    )(*pieces, w_all_t, x, dy, ada3, norm_g, *chip_sums)
    return res[:4], res[4:]


def _sum_small(vec_all, qg_parts, kg_parts):
    def body(v_ref, q_ref, k_ref, tot_ref, gq_ref, gk_ref):
        tot = v_ref[0:1, :]
        for p in range(1, N_DEV):
            tot = tot + v_ref[p:p + 1, :]
        tot_ref[...] = tot
        gq_ref[...] = jnp.sum(q_ref[...], axis=0, keepdims=True)
        gk_ref[...] = jnp.sum(k_ref[...], axis=0, keepdims=True)

    n = vec_all.shape[-1]
    return pl.pallas_call(
        body, name="sum_small",
        out_shape=[jax.ShapeDtypeStruct((1, n), F32),
                   jax.ShapeDtypeStruct((1, HEAD_DIM), F32), jax.ShapeDtypeStruct((1, HEAD_DIM), F32)],
        compiler_params=_params(),
    )(vec_all, qg_parts, kg_parts)


def _grad_w_ada(c_cols, dada_rows):
    def body(c_ref, d_ref, out_ref):
        acc = c_ref[0] * d_ref[0]
        for b in range(1, N_DEV):
            acc = acc + c_ref[b] * d_ref[b]
        out_ref[...] = acc

    return pl.pallas_call(
        body, name="grad_w_ada",
        out_shape=jax.ShapeDtypeStruct((D_MODEL, ADA_SHARD), F32),
        compiler_params=_params(),
    )(c_cols, dada_rows)


def _adam_step(w, m, v, g):
    c1 = 1.0 / (1.0 - ADAM_B1 ** ADAM_STEP)
    c2 = 1.0 / (1.0 - ADAM_B2 ** ADAM_STEP)
    m_new = ADAM_B1 * m + (1.0 - ADAM_B1) * g
    v_new = ADAM_B2 * v + (1.0 - ADAM_B2) * (g * g)
    return -ADAM_LR * ((m_new * c1) / (jnp.sqrt(v_new * c2) + ADAM_EPS) + ADAM_WD * w), m_new, v_new


def _adamw_small(params, name):
    n = len(params)
    stacked = [p[3].ndim == p[0].ndim + 1 for p in params]

    def body(*refs):
        ins, outs = refs[:4 * n], refs[4 * n:]
        for k in range(n):
            w_ref, m_ref, v_ref, g_ref = ins[4 * k:4 * k + 4]
            go_ref, d_ref, mo_ref, vo_ref = outs[4 * k:4 * k + 4]
            if stacked[k]:
                g = g_ref[0].astype(F32)
                for p in range(1, g_ref.shape[0]):
                    g = g + g_ref[p].astype(F32)
            else:
                g = g_ref[...]
            go_ref[...] = g
            d_ref[...], mo_ref[...], vo_ref[...] = _adam_step(w_ref[...], m_ref[...], v_ref[...], g)

    res = pl.pallas_call(
        body, name=name,
        out_shape=[jax.ShapeDtypeStruct(p[0].shape, F32) for p in params for _ in range(4)],
        compiler_params=_params(),
    )(*[a for p in params for a in p])
    return [tuple(res[4 * k:4 * k + 4]) for k in range(n)]


def _adamw(w, m, v, g_parts, name):
    rows, cols = w.shape
    n_parts = g_parts.shape[0]
    tr = 256 if rows % 256 == 0 else rows
    tc = 256 if (tr == rows and rows > 256 and cols % 256 == 0) else cols

    def body(w_ref, m_ref, v_ref, g_ref, go_ref, d_ref, mo_ref, vo_ref):
        g = g_ref[0].astype(F32)
        for p in range(1, n_parts):
            g = g + g_ref[p].astype(F32)
        go_ref[...] = g
        d_ref[...], mo_ref[...], vo_ref[...] = _adam_step(w_ref[...], m_ref[...], v_ref[...], g)

    blk = pl.BlockSpec((tr, tc), lambda i, j: (i, j))
    return pl.pallas_call(
        body, name=name, grid=(rows // tr, cols // tc),
        in_specs=[blk, blk, blk, pl.BlockSpec((n_parts, tr, tc), lambda i, j: (0, i, j))],
        out_specs=[blk] * 4,
        out_shape=[jax.ShapeDtypeStruct((rows, cols), F32)] * 4,
        compiler_params=_params(("parallel", "parallel")),
    )(w, m, v, g_parts)


_O_F = 1536


def _to_internal(wt_g):
    wf = wt_g.reshape(IN_WIDTH, D_MODEL)
    f = jnp.pad(wf[_O_F:_O_F + HEADS], ((0, N_FPAD - HEADS), (0, 0)))
    return jnp.concatenate([wf[:_O_F], wf[_O_F + HEADS:], f], axis=0)


def _slabs_by_core(dwt, dwt_f):
    sources = ((dwt, 0, _O_F, 0), (dwt_f, _O_F, _O_F + HEADS, _O_F), (dwt, _O_F + HEADS, IN_WIDTH, HEADS))

    def slab(p):
        lo, hi = p * IN_SHARD, (p + 1) * IN_SHARD
        parts = []
        for src, o_lo, o_hi, shift in sources:
            a, b = max(lo, o_lo), min(hi, o_hi)
            if a < b:
                parts.append(src[a - shift:b - shift])
        return parts[0] if len(parts) == 1 else jnp.concatenate(parts, axis=0)

    return jnp.stack([jnp.stack([slab(2 * chip + core) for chip in range(4)]) for core in range(2)])


def kernel(x, c, w_ada, b_ada, norm_g, w_in, b_f, q_norm_g, k_norm_g, conv_w, w_attn_out, w_conv_out, w_o, loss_target, m_w_ada, m_b_ada, m_norm_g, m_w_in, m_b_f, m_q_norm_g, m_k_norm_g, m_conv_w, m_w_attn_out, m_w_conv_out, m_w_o, v_w_ada, v_b_ada, v_norm_g, v_w_in, v_b_f, v_q_norm_g, v_k_norm_g, v_conv_w, v_w_attn_out, v_w_conv_out, v_w_o):
    me = 4 * lax.axis_index("x") + 2 * lax.axis_index("y") + lax.axis_index("c")
    s = x.shape[1]
    x2, t2 = x[0], loss_target[0]

    w_in_g, c_all, ada_g = _gather_weights_and_ada(w_in[0].T.astype(BF16), c, w_ada[0])
    ada_mine = lax.dynamic_index_in_dim(ada_g[:, :, 0, :], me, axis=1, keepdims=False)
    ada3 = (ada_mine.reshape(1, 3 * D_MODEL) + b_ada).reshape(3, D_MODEL)
    w_all_t = _to_internal(w_in_g)
    qg = jnp.tile(q_norm_g, (1, HEADS))
    kg = jnp.tile(k_norm_g, (1, HEADS))
    bf_pad = jnp.pad(b_f, ((0, 0), (0, LANES - HEADS)))

    (proj, fl, h), (qa, ka, va, kt, vt), (cw_g, wa_g, wb_g, wo_g) = _proj_fwd(
        x2, ada3, norm_g, w_all_t, bf_pad, qg, kg,
        [conv_w[0], w_attn_out[0].astype(BF16), w_conv_out[0].astype(BF16), w_o[0].astype(BF16)])
    wa = jnp.transpose(wa_g, (1, 0, 2)).reshape(ATTN_W, D_MODEL)
    wb = jnp.transpose(wb_g, (1, 0, 2)).reshape(CONV_W, D_MODEL)
    wo = wo_g.reshape(D_MODEL, D_MODEL)
    cw = jnp.transpose(cw_g, (1, 0, 2)).reshape(3, CONV_W)
    attn, oa, qb = _attn_fwd(qa, ka, vt, proj)
    (dy, dgab, do, dza, dob, dwo, dwa, dwb, dgate, loss_part) = _tail(oa, attn, proj, x2, t2, ada3, wa, wb, wo, cw)

    core = lax.axis_index("c").astype(jnp.int32).reshape(1)
    small = [jnp.transpose(dwa.reshape(ATTN_W, N_DEV, LANES), (1, 0, 2)).astype(BF16),
             jnp.transpose(dwb.reshape(CONV_W, N_DEV, LANES), (1, 0, 2)).astype(BF16),
             dwo.reshape(N_DEV, D_MODEL // N_DEV, D_MODEL).astype(BF16)]
    dq, dk, dv, dqg, dkg, dcum = _attn_bwd(qb, ka, kt, va, do, proj, qg, kg)
    df, dbf = _forget_bwd(dcum, fl, bf_pad)
    dcb, dcc, dcu, dcz, dcw = _conv_bwd(dob, proj, cw)
    pieces = [dq, dk, dv, dza, dcb, dcc, dcu, dcz, dgab, df]
    (dw_main, dw_f), (g_wa_parts, g_wb_parts, g_wo_parts) = _dw_in(h, pieces, small)

    slabs_in = _slabs_by_core(dw_main, dw_f)
    (theirs_in,) = _sibling_swap([slabs_in], "swap_w_in")
    (grad_x, dshift, dscale, dnormg), (g_in_parts,) = _dh_and_dx(
        pieces, w_all_t, x2, dy, ada3, norm_g, [_pair_sum(slabs_in, theirs_in, core, "pair_sum_w_in")])
    vec = jnp.concatenate([dshift, dscale, dgate, dnormg, dbf, dcw.reshape(1, 3 * CONV_W), loss_part, dqg, dkg],
                          axis=1)
    (vec_all,) = _gather_direct([vec], "gather_small")
    vec_all = vec_all.reshape(N_DEV, vec.shape[1])
    n_main = 4 * D_MODEL + LANES + 3 * CONV_W + LANES
    tot, g_qg, g_kg = _sum_small(
        vec_all[:, :n_main],
        vec_all[:, n_main:n_main + ATTN_W].reshape(N_DEV * HEADS, HEAD_DIM),
        vec_all[:, n_main + ATTN_W:].reshape(N_DEV * HEADS, HEAD_DIM))
    g_b_ada = tot[:, 0:3 * D_MODEL]
    g_norm_g = tot[:, 3 * D_MODEL:4 * D_MODEL]
    g_b_f = tot[:, 4 * D_MODEL:4 * D_MODEL + HEADS]
    g_cw_full = tot[:, 4 * D_MODEL + LANES:4 * D_MODEL + LANES + 3 * CONV_W].reshape(3, CONV_W)
    g_cw = lax.dynamic_slice(g_cw_full, (0, me * (CONV_W // N_DEV)), (3, CONV_W // N_DEV))
    dada_mine = lax.dynamic_slice(vec_all[:, 0:3 * D_MODEL], (0, me * ADA_SHARD), (N_DEV, ADA_SHARD))
    g_w_ada = _grad_w_ada(jnp.transpose(c_all, (0, 2, 1)), dada_mine.reshape(N_DEV, 1, ADA_SHARD))

    upd = {}
    upd["w_ada"] = _adamw(w_ada[0], m_w_ada[0], v_w_ada[0], g_w_ada[None], "adamw_w_ada")
    upd["w_in"] = [u.T for u in _adamw(w_in[0].T, m_w_in[0].T, v_w_in[0].T, g_in_parts, "adamw_w_in")]
    small_names = ["b_ada", "norm_g", "b_f", "q_norm_g", "k_norm_g", "conv_w", "w_attn_out", "w_conv_out", "w_o"]
    small_upd = _adamw_small(
        [(b_ada, m_b_ada, v_b_ada, g_b_ada), (norm_g, m_norm_g, v_norm_g, g_norm_g), (b_f, m_b_f, v_b_f, g_b_f),
         (q_norm_g, m_q_norm_g, v_q_norm_g, g_qg), (k_norm_g, m_k_norm_g, v_k_norm_g, g_kg),
         (conv_w[0], m_conv_w[0], v_conv_w[0], g_cw),
         (w_attn_out[0], m_w_attn_out[0], v_w_attn_out[0], g_wa_parts),
         (w_conv_out[0], m_w_conv_out[0], v_w_conv_out[0], g_wb_parts),
         (w_o[0], m_w_o[0], v_w_o[0], g_wo_parts)], "adamw_small")
    upd.update(zip(small_names, small_upd))

    names = ["w_ada", "b_ada", "norm_g", "w_in", "b_f", "q_norm_g", "k_norm_g", "conv_w",
             "w_attn_out", "w_conv_out", "w_o"]
    lead = {"w_ada", "w_in", "conv_w", "w_attn_out", "w_conv_out", "w_o"}
    fix = lambda n, a: a[None] if n in lead else a
    loss = tot[0, n_main - LANES]
    outs = [loss, grad_x[None]]
    for k in range(4):
        outs += [fix(n, upd[n][k]) for n in names]
    return tuple(outs)
```

```python
import functools

import numpy as np
import jax
import jax.numpy as jnp
from jax import lax
from jax.experimental import pallas as pl
from jax.experimental.pallas import tpu as pltpu

F32 = jnp.float32
BF16 = jnp.bfloat16

D_MODEL = 1024
HEADS = 8
HEAD_DIM = 64
ATTN_W = 512
CONV_W = 512
N_DEV = 8
IN_WIDTH = 6152
IN_SHARD = IN_WIDTH // N_DEV
N_MAIN = 6144
N_FPAD = 128
N_ALL = N_MAIN + N_FPAD
ADA_SHARD = 3 * D_MODEL // N_DEV
EPS = 1e-6
NEG = -1e30

ADAM_LR = 0.001
ADAM_B1 = 0.9
ADAM_B2 = 0.999
ADAM_EPS = 1e-08
ADAM_WD = 0.01
ADAM_STEP = 10

LANES = 128
VMEM_LIMIT = 56 * 1024 * 1024

TM_PROJ = 512
TN_PROJ = 1024
TM_ELEM = 512
TQ = 512
HEADS_PER_STEP = 8
HEADS_PER_STEP_FWD = 8
TM_TAIL = 256
TC_CUM = 512
TK_DW = 2048
TN_DW = 512
TM_DH = 256
HALO = 16

OFF_Q, OFF_K, OFF_V, OFF_ZA, OFF_CB, OFF_CC, OFF_CU, OFF_CZ, OFF_GA, OFF_GB = (
    0, 512, 1024, 1536, 2048, 2560, 3072, 3584, 4096, 5120)


def _params(sem=None):
    return pltpu.CompilerParams(dimension_semantics=sem, vmem_limit_bytes=VMEM_LIMIT)


def _dot(a, b):
    return jnp.dot(a, b, preferred_element_type=F32)


def _dot_nt(a, b):
    return lax.dot_general(a, b, (((1,), (1,)), ((), ())), preferred_element_type=F32)


def _dot_tn(a, b):
    return lax.dot_general(a, b, (((0,), (0,)), ((), ())), preferred_element_type=F32)


def _sigmoid(x):
    return 1.0 / (1.0 + jnp.exp(-x))


def _lane_lo(shape):
    return lax.broadcasted_iota(jnp.int32, shape, len(shape) - 1) < HEAD_DIM


def _seg_sum(z, lo):
    a = jnp.sum(jnp.where(lo, z, 0.0), axis=-1, keepdims=True)
    b = jnp.sum(jnp.where(lo, 0.0, z), axis=-1, keepdims=True)
    return jnp.where(lo, a, b)


def _lane_col(z, lane):
    idx = lax.broadcasted_iota(jnp.int32, z.shape, 1)
    return jnp.sum(jnp.where(idx == lane, z, 0.0), axis=-1, keepdims=True)


def _sub_row(z, row):
    idx = lax.broadcasted_iota(jnp.int32, z.shape, 0)
    return jnp.sum(jnp.where(idx == row, z, 0.0), axis=0, keepdims=True)


def _mesh_pos():
    x, y, c = lax.axis_index("x"), lax.axis_index("y"), lax.axis_index("c")
    return x, y, c, 4 * x + 2 * y + c


def _peer(k, x, y, c):
    px = 1 - x if (k >> 2) & 1 else x
    py = 1 - y if (k >> 1) & 1 else y
    pc = 1 - c if k & 1 else c
    return (px, py, pc), 4 * px + 2 * py + pc


def _gather_copies(ins, outs, send_sems, recv_sems, local_sems):
    x, y, c, me = _mesh_pos()
    copies = []
    for a in range(len(ins)):
        copies.append(pltpu.make_async_copy(ins[a], outs[a].at[me], local_sems.at[a]))
        for k in range(1, N_DEV):
            dev, _ = _peer(k, x, y, c)
            copies.append(pltpu.make_async_remote_copy(
                src_ref=ins[a], dst_ref=outs[a].at[me],
                send_sem=send_sems.at[a * (N_DEV - 1) + k - 1], recv_sem=recv_sems.at[a * (N_DEV - 1) + k - 1],
                device_id=dev, device_id_type=pl.DeviceIdType.MESH))
    return copies


def _gather_sems(n):
    return [pltpu.SemaphoreType.DMA((n * (N_DEV - 1),)), pltpu.SemaphoreType.DMA((n * (N_DEV - 1),)),
            pltpu.SemaphoreType.DMA((n,))]


def _gather_direct(arrs, name):
    n = len(arrs)
    any_spec = pl.BlockSpec(memory_space=pl.ANY)

    def body(*refs):
        copies = _gather_copies(refs[:n], refs[n:2 * n], *refs[2 * n:])
        for cp in copies:
            cp.start()
        for cp in copies:
            cp.wait()

    return pl.pallas_call(
        body, name=name, out_shape=[jax.ShapeDtypeStruct((N_DEV,) + a.shape, a.dtype) for a in arrs],
        in_specs=[any_spec] * n, out_specs=[any_spec] * n, scratch_shapes=_gather_sems(n),
    )(*arrs)


def _ada_phase(c_ref, w_ref, call_ref, adag_ref, mine_ref, send_sems, recv_sems):
    x, y, c, me = _mesh_pos()

    def copy(phase, k, src, dst):
        dev, _ = _peer(k, x, y, c)
        return pltpu.make_async_remote_copy(
            src_ref=src, dst_ref=dst,
            send_sem=send_sems.at[phase * (N_DEV - 1) + k - 1],
            recv_sem=recv_sems.at[phase * (N_DEV - 1) + k - 1],
            device_id=dev, device_id_type=pl.DeviceIdType.MESH)

    call_ref[me] = c_ref[...]
    first = [copy(0, k, c_ref, call_ref.at[me]) for k in range(1, N_DEV)]
    for cp in first:
        cp.start()
    for cp in first:
        cp.wait()
    wb = w_ref[...].astype(BF16)
    for b in range(N_DEV):
        row = jnp.broadcast_to(call_ref[b], (8, D_MODEL)).astype(BF16)
        mine_ref[b] = _sub_row(_dot(row, wb), 0)
    adag_ref[me] = mine_ref[...]
    second = [copy(1, k, mine_ref, adag_ref.at[me]) for k in range(1, N_DEV)]
    for cp in second:
        cp.start()
    for cp in second:
        cp.wait()


def _gather_weights_and_ada(wt_shard, c_row, w_ada_sh):
    any_spec = pl.BlockSpec(memory_space=pl.ANY)
    vm = pl.BlockSpec(memory_space=pltpu.VMEM)

    def body(w_in_ref, c_ref, wada_ref, out_ref, call_ref, adag_ref, mine_ref, send_sems, recv_sems, local_sem,
             ada_send, ada_recv):
        x, y, c, me = _mesh_pos()
        sibling = (x, y, 1 - c)
        chips = [(1 - x, y), (x, 1 - y), (1 - x, 1 - y)]

        def copy(k, src, blk, to):
            return pltpu.make_async_remote_copy(
                src_ref=src, dst_ref=out_ref.at[blk], send_sem=send_sems.at[k], recv_sem=recv_sems.at[k],
                device_id=to, device_id_type=pl.DeviceIdType.MESH)

        local = pltpu.make_async_copy(w_in_ref, out_ref.at[me], local_sem.at[0])
        local.start()
        first = [copy(0, w_in_ref, me, sibling)]
        first += [copy(1 + j, w_in_ref, me, (px, py, c)) for j, (px, py) in enumerate(chips)]
        for cp in first:
            cp.start()
        _ada_phase(c_ref, wada_ref, call_ref, adag_ref, mine_ref, ada_send, ada_recv)
        passed = []
        for j, (px, py) in enumerate(chips):
            blk = 4 * px + 2 * py + c
            copy(1 + j, w_in_ref, blk, (x, y, c)).wait_recv()
            fwd = copy(4 + j, out_ref.at[blk], blk, sibling)
            fwd.start()
            passed.append(fwd)
        copy(0, w_in_ref, 4 * x + 2 * y + 1 - c, (x, y, c)).wait_recv()
        for j, (px, py) in enumerate(chips):
            copy(4 + j, w_in_ref, 4 * px + 2 * py + 1 - c, (x, y, c)).wait_recv()
        for cp in first + passed:
            cp.wait_send()
        local.wait()

    per = N_DEV - 1
    return pl.pallas_call(
        body, name="gather_weights",
        out_shape=[jax.ShapeDtypeStruct((N_DEV,) + wt_shard.shape, wt_shard.dtype),
                   jax.ShapeDtypeStruct((N_DEV, 1, D_MODEL), F32),
                   jax.ShapeDtypeStruct((N_DEV, N_DEV, 1, ADA_SHARD), F32)],
        in_specs=[any_spec, vm, vm], out_specs=[any_spec, vm, vm],
        scratch_shapes=[pltpu.VMEM((N_DEV, 1, ADA_SHARD), F32),
                        pltpu.SemaphoreType.DMA((per,)), pltpu.SemaphoreType.DMA((per,)),
                        pltpu.SemaphoreType.DMA((1,)),
                        pltpu.SemaphoreType.DMA((2 * per,)), pltpu.SemaphoreType.DMA((2 * per,))],
        compiler_params=pltpu.CompilerParams(vmem_limit_bytes=VMEM_LIMIT),
    )(wt_shard, c_row, w_ada_sh)


def _sibling_swap(arrs, name):
    n = len(arrs)
    any_spec = pl.BlockSpec(memory_space=pl.ANY)

    def body(*refs):
        ins, outs = refs[:n], refs[n:2 * n]
        send_sems, recv_sems = refs[2 * n:]
        x, y, c, _ = _mesh_pos()
        copies = [pltpu.make_async_remote_copy(
            src_ref=ins[a].at[1 - c], dst_ref=outs[a], send_sem=send_sems.at[a], recv_sem=recv_sems.at[a],
            device_id=(x, y, 1 - c), device_id_type=pl.DeviceIdType.MESH) for a in range(n)]
        for cp in copies:
            cp.start()
        for cp in copies:
            cp.wait()

    return pl.pallas_call(
        body, name=name,
        out_shape=[jax.ShapeDtypeStruct(a.shape[1:], a.dtype) for a in arrs],
        in_specs=[any_spec] * n, out_specs=[any_spec] * n,
        scratch_shapes=[pltpu.SemaphoreType.DMA((n,)), pltpu.SemaphoreType.DMA((n,))],
    )(*arrs)


def _pair_sum(mine2, theirs, core, name):
    _, _, rows, cols = mine2.shape
    tr = 256 if rows % 256 == 0 else rows

    def body(core_ref, a_ref, b_ref, out_ref):
        out_ref[...] = (a_ref[...].astype(F32) + b_ref[...].astype(F32)).astype(BF16)

    return pl.pallas_call(
        body, name=name,
        grid_spec=pltpu.PrefetchScalarGridSpec(
            num_scalar_prefetch=1, grid=(4, rows // tr),
            in_specs=[pl.BlockSpec((None, None, tr, cols), lambda ch, i, core_: (core_[0], ch, i, 0)),
                      pl.BlockSpec((None, tr, cols), lambda ch, i, core_: (ch, i, 0))],
            out_specs=pl.BlockSpec((None, tr, cols), lambda ch, i, core_: (ch, i, 0))),
        out_shape=jax.ShapeDtypeStruct(theirs.shape, BF16),
        compiler_params=_params(("parallel", "parallel")),
    )(core, mine2, theirs)


def _all_to_all_copies(ins, outs, send_sems, recv_sems, local_sems):
    x, y, c, me = _mesh_pos()
    copies = []
    for a in range(len(ins)):
        copies.append(pltpu.make_async_copy(ins[a].at[me], outs[a].at[me], local_sems.at[a]))
        for k in range(1, N_DEV):
            dev, p = _peer(k, x, y, c)
            copies.append(pltpu.make_async_remote_copy(
                src_ref=ins[a].at[p], dst_ref=outs[a].at[me],
                send_sem=send_sems.at[a * (N_DEV - 1) + k - 1], recv_sem=recv_sems.at[a * (N_DEV - 1) + k - 1],
                device_id=dev, device_id_type=pl.DeviceIdType.MESH))
    return copies


def _chip_copies(ins, outs, send_sems, recv_sems, local_sems):
    x, y, c, _ = _mesh_pos()
    my_chip = 2 * x + y
    chips = [(1 - x, y), (x, 1 - y), (1 - x, 1 - y)]
    copies = []
    for a in range(len(ins)):
        copies.append(pltpu.make_async_copy(ins[a].at[my_chip], outs[a].at[my_chip], local_sems.at[a]))
        for j, (px, py) in enumerate(chips):
            copies.append(pltpu.make_async_remote_copy(
                src_ref=ins[a].at[2 * px + py], dst_ref=outs[a].at[my_chip],
                send_sem=send_sems.at[a * 3 + j], recv_sem=recv_sems.at[a * 3 + j],
                device_id=(px, py, c), device_id_type=pl.DeviceIdType.MESH))
    return copies


def _proj_fwd(x, ada3, norm_g, w_all_t, bf_pad, qg, kg, later):
    s = x.shape[0]
    tm, tn = min(TM_PROJ, s), TN_PROJ
    nt = s // tm
    n = len(later)

    def body(x_ref, ada_ref, g_ref, wt_ref, bf_ref, qg_ref, kg_ref, *rest):
        ins, (proj_ref, fl_ref, h_ref), rows_refs, rest = rest[:n], rest[n:n + 3], rest[n + 3:n + 8], rest[n + 8:]
        outs, (carry, send_sems, recv_sems, local_sems) = rest[:n], rest[n:]
        i = pl.program_id(0)

        @pl.when(i == 0)
        def _():
            carry[...] = jnp.zeros_like(carry)
            for cp in _gather_copies(ins, outs, send_sems, recv_sems, local_sems):
                cp.start()

        xv = x_ref[...]
        r = lax.rsqrt(jnp.mean(xv * xv, axis=-1, keepdims=True) + EPS)
        hv = ((xv * r) * g_ref[...]) * (1.0 + ada_ref[1:2, :]) + ada_ref[0:1, :]
        hb = hv.astype(BF16)
        h_ref[...] = hb
        fl = _dot_nt(hb, wt_ref[N_MAIN:N_ALL, :])
        fl_ref[...] = fl
        for j in range(N_MAIN // tn):
            proj_ref[:, j * tn:(j + 1) * tn] = _dot_nt(hb, wt_ref[j * tn:(j + 1) * tn, :]).astype(BF16)
        _attention_rows(proj_ref, fl, bf_ref, qg_ref, kg_ref, carry, *rows_refs)

        @pl.when(i == nt - 1)
        def _():
            for cp in _gather_copies(ins, outs, send_sems, recv_sems, local_sems):
                cp.wait()

    any_spec = pl.BlockSpec(memory_space=pl.ANY)
    heads = pl.BlockSpec((HEADS, tm, LANES), lambda i: (0, i, 0))
    heads_t = pl.BlockSpec((HEADS, LANES, tm), lambda i: (0, 0, i))
    vec = pl.BlockSpec((1, ATTN_W), lambda i: (0, 0))
    res = pl.pallas_call(
        body, name="proj_fwd", grid=(nt,),
        in_specs=[pl.BlockSpec((tm, D_MODEL), lambda i: (i, 0)),
                  pl.BlockSpec((3, D_MODEL), lambda i: (0, 0)),
                  pl.BlockSpec((1, D_MODEL), lambda i: (0, 0)),
                  pl.BlockSpec((N_ALL, D_MODEL), lambda i: (0, 0)),
                  pl.BlockSpec((1, LANES), lambda i: (0, 0)), vec, vec] + [any_spec] * n,
        out_specs=[pl.BlockSpec((tm, N_MAIN), lambda i: (i, 0)),
                   pl.BlockSpec((tm, N_FPAD), lambda i: (i, 0)),
                   pl.BlockSpec((tm, D_MODEL), lambda i: (i, 0)),
                   heads, heads, heads, heads_t, heads_t] + [any_spec] * n,
        out_shape=[jax.ShapeDtypeStruct((s, N_MAIN), BF16),
                   jax.ShapeDtypeStruct((s, N_FPAD), F32),
                   jax.ShapeDtypeStruct((s, D_MODEL), BF16)]
        + [jax.ShapeDtypeStruct((HEADS, s, LANES), BF16)] * 3
        + [jax.ShapeDtypeStruct((HEADS, LANES, s), BF16)] * 2
        + [jax.ShapeDtypeStruct((N_DEV,) + a.shape, a.dtype) for a in later],
        scratch_shapes=[pltpu.VMEM((1, LANES), F32)] + _gather_sems(n),
        compiler_params=_params(("arbitrary",)),
    )(x, ada3, norm_g, w_all_t, bf_pad, qg, kg, *later)
    return res[:3], res[3:8], res[8:]


L_ONE_Q, L_F_Q, L_LSE_Q, L_END = HEAD_DIM, HEAD_DIM + 3, HEAD_DIM + 6, HEAD_DIM + 9


def _split3(f):
    hi = f.astype(BF16).astype(F32)
    r = f - hi
    mid = r.astype(BF16).astype(F32)
    return hi, mid, r - mid


def _place3(lane, first, parts, otherwise):
    a, b, c = parts
    return jnp.where(lane == first, a, jnp.where(lane == first + 1, b, jnp.where(lane == first + 2, c, otherwise)))


def _log_forget(fl, bf):
    z = fl + bf
    lf = jnp.minimum(z, 0.0) - jnp.log1p(jnp.exp(-jnp.abs(z)))
    lane = lax.broadcasted_iota(jnp.int32, z.shape, 1)
    return jnp.where(lane < HEADS, lf, 0.0)


def _attention_rows(p_ref, fl, bf_ref, qg_ref, kg_ref, carry, qa_ref, ka_ref, va_ref, kt_ref, vt_ref):
    tm = fl.shape[0]
    scale = HEAD_DIM ** -0.5
    tri = (lax.broadcasted_iota(jnp.int32, (tm, tm), 1) <= lax.broadcasted_iota(jnp.int32, (tm, tm), 0)).astype(F32)
    cum_v = jnp.dot(tri, _log_forget(fl, bf_ref[...]), preferred_element_type=F32,
                    precision=lax.Precision.HIGHEST) + carry[...]
    carry[...] = _sub_row(cum_v, tm - 1)
    lane = lax.broadcasted_iota(jnp.int32, (tm, LANES), 1)
    lo = lane < HEAD_DIM
    v_tail = jnp.where(lane < L_F_Q, 1.0, 0.0)
    for pr in range(ATTN_W // LANES):
        sl = slice(pr * LANES, (pr + 1) * LANES)
        q2 = p_ref[:, OFF_Q + pr * LANES:OFF_Q + (pr + 1) * LANES].astype(F32)
        k2 = p_ref[:, OFF_K + pr * LANES:OFF_K + (pr + 1) * LANES].astype(F32)
        v2 = p_ref[:, OFF_V + pr * LANES:OFF_V + (pr + 1) * LANES].astype(F32)
        rq = lax.rsqrt(_seg_sum(q2 * q2, lo) * (1.0 / HEAD_DIM) + EPS)
        rk = lax.rsqrt(_seg_sum(k2 * k2, lo) * (1.0 / HEAD_DIM) + EPS)
        qn = ((q2 * rq) * qg_ref[:, sl]) * scale
        kn = (k2 * rk) * kg_ref[:, sl]
        for hh in range(2):
            h = 2 * pr + hh
            f3 = _split3(_lane_col(cum_v, h))
            qh = qn if hh == 0 else pltpu.roll(qn, HEAD_DIM, 1)
            kh = kn if hh == 0 else pltpu.roll(kn, HEAD_DIM, 1)
            vh = v2 if hh == 0 else pltpu.roll(v2, HEAD_DIM, 1)
            q_tail = jnp.where(lane < L_F_Q, 1.0, _place3(lane, L_F_Q, f3, 0.0))
            k_tail = _place3(lane, L_ONE_Q, tuple(-f for f in f3), jnp.where(lane < L_END, 1.0, 0.0))
            k_row = jnp.where(lo, kh, k_tail)
            v_row = jnp.where(lo, vh, v_tail)
            qa_ref[h] = jnp.where(lo, qh, q_tail).astype(BF16)
            ka_ref[h] = k_row.astype(BF16)
            va_ref[h] = v_row.astype(BF16)
            kt_ref[h] = k_row.T.astype(BF16)
            vt_ref[h] = v_row.T.astype(BF16)


def _causal_t(t):
    return lax.broadcasted_iota(jnp.int32, (t, t), 0) <= lax.broadcasted_iota(jnp.int32, (t, t), 1)


def _tri_steps(nt, q_major):
    if q_major:
        pairs = [(i, j) for i in range(nt) for j in range(i + 1)]
    else:
        pairs = [(i, j) for j in range(nt) for i in range(j, nt)]
    return (jnp.asarray(np.array([p[0] for p in pairs], np.int32)),
            jnp.asarray(np.array([p[1] for p in pairs], np.int32)))


def _attn_fwd(qa, ka, vt, proj):
    s = qa.shape[1]
    t = min(TQ, s)
    it, jt = _tri_steps(s // t, True)
    hp = HEADS_PER_STEP_FWD
    wide = hp * HEAD_DIM
    za_blk = OFF_ZA // wide

    def body(it_ref, jt_ref, q_ref, k_ref, vt_ref, za_ref, attn_ref, oa_ref, qb_ref, m_s, acc_s, pair_s):
        step = pl.program_id(1)
        i, j = it_ref[step], jt_ref[step]

        @pl.when(j == 0)
        def _():
            m_s[...] = jnp.full_like(m_s, NEG)
            acc_s[...] = jnp.zeros_like(acc_s)

        def update(masked):
            for hh in range(hp):
                st = _dot_nt(k_ref[hh], q_ref[hh])
                if masked:
                    st = jnp.where(_causal_t(t), st, NEG)
                m_prev = m_s[hh]
                m_next = jnp.maximum(m_prev, jnp.max(st, axis=0, keepdims=True))
                alpha = jnp.exp(m_prev - m_next)
                pt = jnp.exp(st - m_next).astype(BF16)
                acc_s[hh] = acc_s[hh] * alpha + _dot(vt_ref[hh], pt)
                m_s[hh] = m_next

        @pl.when(j < i)
        def _():
            update(False)

        @pl.when(j == i)
        def _():
            update(True)
            row = lax.broadcasted_iota(jnp.int32, (LANES, t), 0)
            lane = lax.broadcasted_iota(jnp.int32, (t, LANES), 1)
            for hh in range(hp):
                l_row = acc_s[hh, L_ONE_Q:L_ONE_Q + 1, :]
                pair_s[hh * HEAD_DIM:(hh + 1) * HEAD_DIM, :] = acc_s[hh, 0:HEAD_DIM, :] / l_row
                lse3 = _split3(m_s[hh] + jnp.log(l_row))
                tail_t = _place3(row, L_LSE_Q, tuple(-x for x in lse3), 0.0)
                keep_q = jnp.logical_or(lane < L_LSE_Q, lane >= L_END)
                qb_ref[hh] = jnp.where(keep_q, q_ref[hh].astype(F32), tail_t.T).astype(BF16)
            out = pair_s[...].T
            attn_ref[...] = out
            z = za_ref[...].astype(F32)
            oa_ref[...] = (out * (z * _sigmoid(z))).astype(BF16)

    pair_q = pl.BlockSpec((hp, t, LANES), lambda p, n, it_, jt_: (p, it_[n], 0))
    pair_k = pl.BlockSpec((hp, t, LANES), lambda p, n, it_, jt_: (p, jt_[n], 0))
    pair_kt = pl.BlockSpec((hp, LANES, t), lambda p, n, it_, jt_: (p, 0, jt_[n]))
    out_q = pl.BlockSpec((t, wide), lambda p, n, it_, jt_: (it_[n], p))
    return pl.pallas_call(
        body, name="attn_fwd",
        grid_spec=pltpu.PrefetchScalarGridSpec(
            num_scalar_prefetch=2, grid=(HEADS // hp, it.shape[0]),
            in_specs=[pair_q, pair_k, pair_kt,
                      pl.BlockSpec((t, wide), lambda p, n, it_, jt_: (it_[n], za_blk + p))],
            out_specs=[out_q, out_q, pair_q],
            scratch_shapes=[pltpu.VMEM((hp, 1, t), F32), pltpu.VMEM((hp, LANES, t), F32),
                            pltpu.VMEM((wide, t), F32)]),
        out_shape=[jax.ShapeDtypeStruct((s, ATTN_W), F32),
                   jax.ShapeDtypeStruct((s, ATTN_W), BF16),
                   jax.ShapeDtypeStruct((HEADS, s, LANES), BF16)],
        compiler_params=_params(("parallel", "arbitrary")),
    )(it, jt, qa, ka, vt, proj)


def _conv_parts(gb_ref, gc_ref, u_ref, zb_ref, gch_ref, uh_ref, first, w_ref, tm):
    gb, gc = gb_ref[...].astype(F32), gc_ref[...].astype(F32)
    u, zb = u_ref[...].astype(F32), zb_ref[...].astype(F32)
    cu = gc * u
    cu_h = jnp.where(first, 0.0, gch_ref[...].astype(F32) * uh_ref[...].astype(F32))
    prev1, prev2 = _sub_row(cu_h, HALO - 1), _sub_row(cu_h, HALO - 2)
    row = lax.broadcasted_iota(jnp.int32, cu.shape, 0)
    r1 = jnp.where(row == 0, prev1, pltpu.roll(cu, 1, 0))
    r2 = jnp.where(row == 0, prev2, jnp.where(row == 1, prev1, pltpu.roll(cu, 2, 0)))
    conv = w_ref[2:3, :] * cu + w_ref[1:2, :] * r1 + w_ref[0:1, :] * r2
    return gb, gc, u, zb, cu, r1, r2, conv


def _conv_specs(tm, s, width=LANES):
    def tile(off):
        return pl.BlockSpec((tm, width), lambda c, i: (i, off // width + c))

    def before(off):
        return pl.BlockSpec((HALO, width), lambda c, i: (jnp.maximum(i * (tm // HALO) - 1, 0), off // width + c))

    def after(off):
        return pl.BlockSpec((HALO, width),
                            lambda c, i: (jnp.minimum((i + 1) * (tm // HALO), s // HALO - 1), off // width + c))

    return ([tile(OFF_CB), tile(OFF_CC), tile(OFF_CU), tile(OFF_CZ)], [before(OFF_CC), before(OFF_CU)],
            [after(OFF_CB), after(OFF_CZ)])


def _tail(oa, attn, proj, x, target, ada3, wa, wb, wo, conv_w):
    s = x.shape[0]
    tm = min(TM_TAIL, s)
    gab_blk = OFF_GA // (2 * D_MODEL)
    za_blk = OFF_ZA // ATTN_W
    tiles, befores, _ = _conv_specs(tm, s, CONV_W)

    def body(oa_ref, attn_ref, za_ref, gb_ref, gc_ref, u_ref, zb_ref, gch_ref, uh_ref, cw_ref, gab_ref, x_ref, t_ref,
             ada_ref, wa_ref, wb_ref, wo_ref,
             dy_ref, dgab_ref, do_ref, dza_ref, dob_ref, dwo_ref, dwa_ref, dwb_ref, dgate_ref, loss_ref):
        first = pl.program_id(0) == 0

        @pl.when(first)
        def _():
            dwo_ref[...] = jnp.zeros_like(dwo_ref)
            dwa_ref[...] = jnp.zeros_like(dwa_ref)
            dwb_ref[...] = jnp.zeros_like(dwb_ref)
            dgate_ref[...] = jnp.zeros_like(dgate_ref)
            loss_ref[...] = jnp.zeros_like(loss_ref)

        gb, _, _, zb, _, _, _, conv = _conv_parts(gb_ref, gc_ref, u_ref, zb_ref, gch_ref, uh_ref, first, cw_ref, tm)
        ob_v = (gb * conv * (zb * _sigmoid(zb))).astype(BF16)
        oa_v = oa_ref[...]
        wa_v, wb_v, wo_v = wa_ref[...], wb_ref[...], wo_ref[...]
        a2 = _dot(oa_v, wa_v)
        b2 = _dot(ob_v, wb_v)
        sa = _sigmoid(gab_ref[:, 0:D_MODEL].astype(F32))
        sb = _sigmoid(gab_ref[:, D_MODEL:2 * D_MODEL].astype(F32))
        mb = (sa * a2 + sb * b2).astype(BF16)
        mo = _dot(mb, wo_v)
        gate = ada_ref[2:3, :]
        err = (x_ref[...] + gate * mo) - t_ref[...]
        dy = err * (1.0 / D_MODEL)
        dy_ref[...] = dy
        loss_ref[...] += 0.5 * jnp.sum(err * err) * (1.0 / D_MODEL)
        dgate_ref[...] += jnp.sum(dy * mo, axis=0, keepdims=True)
        dmo = (dy * gate).astype(BF16)
        dmerged = _dot_nt(dmo, wo_v)
        dwo_ref[...] += _dot_tn(mb, dmo)
        da2 = (dmerged * sa).astype(BF16)
        db2 = (dmerged * sb).astype(BF16)
        dgab_ref[:, 0:D_MODEL] = (dmerged * a2 * (sa * (1.0 - sa))).astype(BF16)
        dgab_ref[:, D_MODEL:2 * D_MODEL] = (dmerged * b2 * (sb * (1.0 - sb))).astype(BF16)
        doa = _dot_nt(da2, wa_v)
        dob_ref[...] = _dot_nt(db2, wb_v)
        dwa_ref[...] += _dot_tn(oa_v, da2)
        dwb_ref[...] += _dot_tn(ob_v, db2)

        lane = lax.broadcasted_iota(jnp.int32, (tm, LANES), 1)
        lo = lane < HEAD_DIM
        for pr in range(ATTN_W // LANES):
            sl = slice(pr * LANES, (pr + 1) * LANES)
            g, a, z = doa[:, sl], attn_ref[:, sl], za_ref[:, sl].astype(F32)
            sg = _sigmoid(z)
            dat = (g * (z * sg)).astype(BF16).astype(F32)
            prod = dat * a
            dza_ref[:, sl] = (g * a * (sg * (1.0 + z * (1.0 - sg)))).astype(BF16)
            for hh in range(2):
                sel = lo if hh == 0 else jnp.logical_not(lo)
                delta3 = _split3(jnp.sum(jnp.where(sel, prod, 0.0), axis=-1, keepdims=True))
                dh = dat if hh == 0 else pltpu.roll(dat, HEAD_DIM, 1)
                tail_lanes = _place3(lane, L_ONE_Q, tuple(-d for d in delta3), 0.0)
                do_ref[2 * pr + hh] = jnp.where(lo, dh, tail_lanes).astype(BF16)

    half = pl.BlockSpec((tm, ATTN_W), lambda i: (i, 0))
    full = pl.BlockSpec((tm, D_MODEL), lambda i: (i, 0))

    def const(shape):
        return pl.BlockSpec(shape, lambda i: (0, 0))

    def one_axis(spec):
        return pl.BlockSpec(spec.block_shape, lambda i, f=spec.index_map: f(0, i))

    return pl.pallas_call(
        body, name="tail", grid=(s // tm,),
        in_specs=[half, half, pl.BlockSpec((tm, ATTN_W), lambda i: (i, za_blk))]
        + [one_axis(sp) for sp in tiles + befores]
        + [const((3, CONV_W)), pl.BlockSpec((tm, 2 * D_MODEL), lambda i: (i, gab_blk)), full, full,
           const((3, D_MODEL)), const((ATTN_W, D_MODEL)), const((CONV_W, D_MODEL)), const((D_MODEL, D_MODEL))],
        out_specs=[full, pl.BlockSpec((tm, 2 * D_MODEL), lambda i: (i, 0)),
                   pl.BlockSpec((HEADS, tm, LANES), lambda i: (0, i, 0)), half, half,
                   const((D_MODEL, D_MODEL)), const((ATTN_W, D_MODEL)), const((CONV_W, D_MODEL)),
                   const((1, D_MODEL)), const((1, LANES))],
        out_shape=[jax.ShapeDtypeStruct((s, D_MODEL), F32),
                   jax.ShapeDtypeStruct((s, 2 * D_MODEL), BF16),
                   jax.ShapeDtypeStruct((HEADS, s, LANES), BF16),
                   jax.ShapeDtypeStruct((s, ATTN_W), BF16),
                   jax.ShapeDtypeStruct((s, CONV_W), F32),
                   jax.ShapeDtypeStruct((D_MODEL, D_MODEL), F32),
                   jax.ShapeDtypeStruct((ATTN_W, D_MODEL), F32),
                   jax.ShapeDtypeStruct((CONV_W, D_MODEL), F32),
                   jax.ShapeDtypeStruct((1, D_MODEL), F32),
                   jax.ShapeDtypeStruct((1, LANES), F32)],
        compiler_params=_params(("arbitrary",)),
    )(oa, attn, proj, *([proj] * 6), conv_w, proj, x, target, ada3, wa, wb, wo)


def _attn_bwd(qb, ka, kt, va, do, proj, qg, kg):
    s = qb.shape[1]
    t = min(TQ, s)
    nt = s // t
    hp = HEADS_PER_STEP
    wide = hp * HEAD_DIM
    scale = HEAD_DIM ** -0.5
    it, jt = _tri_steps(nt, False)

    def body(it_ref, jt_ref, q_ref, k_ref, kt_ref, v_ref, do_ref, qraw_ref, kraw_ref, qg_ref, kg_ref,
             dq_ref, dk_ref, dv_ref, dqg_ref, dkg_ref, dcum_ref, dqt_s, dk_s, dv_s, rows_s):
        grp, step = pl.program_id(0), pl.program_id(1)
        i, j = it_ref[step], jt_ref[step]
        lane = lax.broadcasted_iota(jnp.int32, (t, LANES), 1)
        lo = lane < HEAD_DIM

        @pl.when(step == 0)
        def _():
            dqt_s[...] = jnp.zeros_like(dqt_s)
            dqg_ref[...] = jnp.zeros_like(dqg_ref)
            dkg_ref[...] = jnp.zeros_like(dkg_ref)

        @pl.when(i == j)
        def _():
            dk_s[...] = jnp.zeros_like(dk_s)
            dv_s[...] = jnp.zeros_like(dv_s)

        def update(masked):
            for hh in range(hp):
                qh, doh = q_ref[hh], do_ref[hh]
                st = _dot_nt(k_ref[hh], qh)
                if masked:
                    st = jnp.where(_causal_t(t), st, NEG)
                pt = jnp.exp(st)
                dst = (pt * _dot_nt(v_ref[hh], doh)).astype(BF16)
                dv_s[hh] += _dot(pt.astype(BF16), doh)
                dk_s[hh] += _dot(dst, qh)
                dqt_s[hh, i] += _dot(kt_ref[hh], dst)

        def pair(a, b):
            return jnp.where(lo, a, pltpu.roll(b, HEAD_DIM, 1))

        def norm_bwd(raw, dy, g, dg_ref, out_ref, sl):
            r = lax.rsqrt(_seg_sum(raw * raw, lo) * (1.0 / HEAD_DIM) + EPS)
            xhat = raw * r
            dg_ref[:, sl] += jnp.sum(dy * xhat, axis=0, keepdims=True)
            dxh = dy * g
            dx = r * (dxh - xhat * (_seg_sum(dxh * xhat, lo) * (1.0 / HEAD_DIM)))
            out_ref[:, sl] = dx.astype(BF16)

        @pl.when(i > j)
        def _():
            update(False)

        @pl.when(i == j)
        def _():
            update(True)
            dq_rows = [dqt_s[hh, i].T for hh in range(hp)]
            rows = jnp.zeros((t, LANES), F32)
            for hh in range(hp):
                rows = jnp.where(lane == grp * hp + hh, _lane_col(dq_rows[hh], L_F_Q), rows)
            rows_s[...] = rows
            for pr in range(hp // 2):
                sl = slice(pr * LANES, (pr + 1) * LANES)
                norm_bwd(qraw_ref[:, sl].astype(F32), pair(dq_rows[2 * pr], dq_rows[2 * pr + 1]) * scale,
                         qg_ref[:, sl], dqg_ref, dq_ref, sl)

        @pl.when(i == nt - 1)
        def _():
            dcum = rows_s[...]
            for hh in range(hp):
                dcum = jnp.where(lane == grp * hp + hh, dcum - _lane_col(dk_s[hh], L_ONE_Q), dcum)
            dcum_ref[0] = dcum
            for pr in range(hp // 2):
                sl = slice(pr * LANES, (pr + 1) * LANES)
                norm_bwd(kraw_ref[:, sl].astype(F32), pair(dk_s[2 * pr], dk_s[2 * pr + 1]),
                         kg_ref[:, sl], dkg_ref, dk_ref, sl)
                dv_ref[:, sl] = pair(dv_s[2 * pr], dv_s[2 * pr + 1]).astype(BF16)

    pair_q = pl.BlockSpec((hp, t, LANES), lambda p, n, it_, jt_: (p, it_[n], 0))
    pair_k = pl.BlockSpec((hp, t, LANES), lambda p, n, it_, jt_: (p, jt_[n], 0))
    pair_kt = pl.BlockSpec((hp, LANES, t), lambda p, n, it_, jt_: (p, 0, jt_[n]))
    tok = pl.BlockSpec((t, wide), lambda p, n, it_, jt_: (jt_[n], p))
    gain = pl.BlockSpec((1, wide), lambda p, n, it_, jt_: (0, p))
    return pl.pallas_call(
        body, name="attn_bwd",
        grid_spec=pltpu.PrefetchScalarGridSpec(
            num_scalar_prefetch=2, grid=(HEADS // hp, it.shape[0]),
            in_specs=[pair_q, pair_k, pair_kt, pair_k, pair_q,
                      pl.BlockSpec((t, wide), lambda p, n, it_, jt_: (jt_[n], OFF_Q // wide + p)),
                      pl.BlockSpec((t, wide), lambda p, n, it_, jt_: (jt_[n], OFF_K // wide + p)), gain, gain],
            out_specs=[tok, tok, tok, gain, gain,
                       pl.BlockSpec((1, t, LANES), lambda p, n, it_, jt_: (p, jt_[n], 0))],
            scratch_shapes=[pltpu.VMEM((hp, nt, LANES, t), F32), pltpu.VMEM((hp, t, LANES), F32),
                            pltpu.VMEM((hp, t, LANES), F32), pltpu.VMEM((t, LANES), F32)]),
        out_shape=[jax.ShapeDtypeStruct((s, ATTN_W), BF16)] * 3
        + [jax.ShapeDtypeStruct((1, ATTN_W), F32)] * 2
        + [jax.ShapeDtypeStruct((HEADS // hp, s, LANES), F32)],
        compiler_params=_params(("parallel", "arbitrary")),
    )(it, jt, qb, ka, kt, va, do, proj, proj, qg, kg)


def _forget_bwd(dcum, fl, bf_pad):
    s = fl.shape[0]
    tc = min(TC_CUM, s)
    n = s // tc

    def body(dc_ref, fl_ref, bf_ref, df_ref, dbf_ref, carry):
        @pl.when(pl.program_id(0) == 0)
        def _():
            carry[...] = jnp.zeros_like(carry)
            dbf_ref[...] = jnp.zeros_like(dbf_ref)
        r = lax.broadcasted_iota(jnp.int32, (tc, tc), 0)
        cidx = lax.broadcasted_iota(jnp.int32, (tc, tc), 1)
        tri = (cidx >= r).astype(F32)
        dc = dc_ref[0]
        for grp in range(1, dcum.shape[0]):
            dc = dc + dc_ref[grp]
        dlf = jnp.dot(tri, dc, preferred_element_type=F32, precision=lax.Precision.HIGHEST) + carry[...]
        carry[...] += jnp.sum(dc, axis=0, keepdims=True)
        lane = lax.broadcasted_iota(jnp.int32, (tc, LANES), 1)
        dfl = jnp.where(lane < HEADS, dlf * _sigmoid(-(fl_ref[...] + bf_ref[...])), 0.0)
        df_ref[...] = dfl.astype(BF16)
        dbf_ref[...] += jnp.sum(dfl, axis=0, keepdims=True)

    rev = pl.BlockSpec((tc, LANES), lambda i: (n - 1 - i, 0))
    vec = pl.BlockSpec((1, LANES), lambda i: (0, 0))
    return pl.pallas_call(
        body, name="forget_bwd", grid=(n,),
        in_specs=[pl.BlockSpec((dcum.shape[0], tc, LANES), lambda i: (0, n - 1 - i, 0)), rev, vec],
        out_specs=[rev, vec],
        out_shape=[jax.ShapeDtypeStruct((s, LANES), BF16), jax.ShapeDtypeStruct((1, LANES), F32)],
        scratch_shapes=[pltpu.VMEM((1, LANES), F32)],
        compiler_params=_params(("arbitrary",)),
    )(dcum, fl, bf_pad)


def _conv_bwd(dob, proj, conv_w):
    s = dob.shape[0]
    tm = min(TM_ELEM, s)
    wd = CONV_W
    tiles, befores, afters = _conv_specs(tm, s, wd)

    def body(dob_ref, dnext_ref, gb_ref, gc_ref, u_ref, zb_ref, gch_ref, uh_ref, gbn_ref, zbn_ref, w_ref,
             dgb_ref, dgc_ref, du_ref, dzb_ref, dw_ref):
        i = pl.program_id(1)

        @pl.when(i == 0)
        def _():
            dw_ref[...] = jnp.zeros_like(dw_ref)
        gb, gc, u, zb, cu, r1, r2, conv = _conv_parts(gb_ref, gc_ref, u_ref, zb_ref, gch_ref, uh_ref, i == 0, w_ref, tm)
        g = dob_ref[...]
        sg = _sigmoid(zb)
        sz = zb * sg
        dconv = g * gb * sz
        zn = zbn_ref[0:8, :].astype(F32)
        dcn = jnp.where(i == pl.num_programs(1) - 1, 0.0,
                        dnext_ref[...] * gbn_ref[0:8, :].astype(F32) * (zn * _sigmoid(zn)))
        nxt1, nxt2 = _sub_row(dcn, 0), _sub_row(dcn, 1)
        row = lax.broadcasted_iota(jnp.int32, (tm, wd), 0)
        f1 = jnp.where(row == tm - 1, nxt1, pltpu.roll(dconv, tm - 1, 0))
        f2 = jnp.where(row == tm - 2, nxt1, jnp.where(row == tm - 1, nxt2, pltpu.roll(dconv, tm - 2, 0)))
        dcu = w_ref[2:3, :] * dconv + w_ref[1:2, :] * f1 + w_ref[0:1, :] * f2
        dgb_ref[...] = (g * conv * sz).astype(BF16)
        dgc_ref[...] = (dcu * u).astype(BF16)
        du_ref[...] = (dcu * gc).astype(BF16)
        dzb_ref[...] = (g * gb * conv * (sg * (1.0 + zb * (1.0 - sg)))).astype(BF16)
        w_row = lax.broadcasted_iota(jnp.int32, (3, wd), 0)
        dw0 = jnp.sum(dconv * r2, axis=0, keepdims=True)
        dw1 = jnp.sum(dconv * r1, axis=0, keepdims=True)
        dw2 = jnp.sum(dconv * cu, axis=0, keepdims=True)
        dw_ref[...] += jnp.where(w_row == 0, dw0, jnp.where(w_row == 1, dw1, dw2))

    blk = pl.BlockSpec((tm, wd), lambda c, i: (i, c))
    nxt = pl.BlockSpec((8, wd), lambda c, i: (jnp.minimum((i + 1) * (tm // 8), s // 8 - 1), c))
    wspec = pl.BlockSpec((3, wd), lambda c, i: (0, c))
    return pl.pallas_call(
        body, name="conv_bwd", grid=(CONV_W // wd, s // tm),
        in_specs=[blk, nxt] + tiles + befores + afters + [wspec],
        out_specs=[blk, blk, blk, blk, wspec],
        out_shape=[jax.ShapeDtypeStruct((s, CONV_W), BF16)] * 4 + [jax.ShapeDtypeStruct((3, CONV_W), F32)],
        compiler_params=_params(("parallel", "arbitrary")),
    )(dob, dob, *([proj] * 8), conv_w)


def _piece_layout(pieces):
    offs, off = [], 0
    for p in pieces:
        offs.append((off, p.shape[1]))
        off += p.shape[1]
    assert off == N_ALL, off
    return offs


def _dw_in(h, pieces, chip_sums):
    s = h.shape[0]
    tk, tn = min(TK_DW, s), TN_DW
    nk = s // tk
    nn = N_MAIN // tn
    main, fpiece = pieces[:-1], pieces[-1]
    layout = _piece_layout(pieces)[:-1]
    n_main = len(main)
    nx = len(chip_sums)

    def body(*refs):
        p_refs, f_ref, h_ref = refs[:n_main], refs[n_main], refs[n_main + 1]
        ins, refs = refs[n_main + 2:n_main + 2 + nx], refs[n_main + 2 + nx:]
        out_ref, outf_ref = refs[:2]
        outs, (acc, accf, send_sems, recv_sems, local_sems) = refs[2:2 + nx], refs[2 + nx:]
        n, k = pl.program_id(0), pl.program_id(1)

        @pl.when(jnp.logical_and(n == 0, k == 0))
        def _():
            for cp in _all_to_all_copies(ins, outs, send_sems, recv_sems, local_sems):
                cp.start()

        @pl.when(k == 0)
        def _():
            acc[...] = jnp.zeros_like(acc)
        hv = h_ref[pl.ds(pl.multiple_of(k * tk, tk), tk), :]
        for p_ref, (off, width) in zip(p_refs, layout):
            @pl.when(jnp.logical_and(n >= off // tn, n < (off + width) // tn))
            def _():
                acc[...] += _dot_tn(p_ref[...], hv)

        @pl.when(k == nk - 1)
        def _():
            out_ref[...] = acc[...].astype(BF16)

        @pl.when(n == 0)
        def _():
            @pl.when(k == 0)
            def _():
                accf[...] = jnp.zeros_like(accf)
            accf[...] += _dot_tn(f_ref[...], hv)

            @pl.when(k == nk - 1)
            def _():
                outf_ref[...] = accf[...].astype(BF16)

        @pl.when(jnp.logical_and(n == nn - 1, k == nk - 1))
        def _():
            for cp in _all_to_all_copies(ins, outs, send_sems, recv_sems, local_sems):
                cp.wait()

    def piece_spec(off, width):
        lo, hi = off // tn, (off + width) // tn

        def index(n, k):
            active = jnp.logical_and(n >= lo, n < hi)
            return jnp.where(active, k, 0), jnp.clip(n - lo, 0, hi - lo - 1)
        return pl.BlockSpec((tk, tn), index)

    any_spec = pl.BlockSpec(memory_space=pl.ANY)
    res = pl.pallas_call(
        body, name="dw_in", grid=(nn, nk),
        in_specs=[piece_spec(off, width) for off, width in layout]
        + [pl.BlockSpec((tk, N_FPAD), lambda n, k: (jnp.where(n == 0, k, 0), 0)),
           pl.BlockSpec((s, D_MODEL), lambda n, k: (0, 0))] + [any_spec] * nx,
        out_specs=[pl.BlockSpec((tn, D_MODEL), lambda n, k: (n, 0)),
                   pl.BlockSpec((N_FPAD, D_MODEL), lambda n, k: (0, 0))] + [any_spec] * nx,
        out_shape=[jax.ShapeDtypeStruct((N_MAIN, D_MODEL), BF16), jax.ShapeDtypeStruct((N_FPAD, D_MODEL), BF16)]
        + [jax.ShapeDtypeStruct(a.shape, a.dtype) for a in chip_sums],
        scratch_shapes=[pltpu.VMEM((tn, D_MODEL), F32), pltpu.VMEM((N_FPAD, D_MODEL), F32)] + _gather_sems(nx),
        compiler_params=_params(("arbitrary", "arbitrary")),
    )(*main, fpiece, h, *chip_sums)
    return res[:2], res[2:]


def _dh_and_dx(pieces, w_all_t, x, dy, ada3, norm_g, chip_sums):
    s = x.shape[0]
    tm = min(TM_DH, s)
    nt = s // tm
    n = len(chip_sums)
    npc = len(pieces)
    layout = _piece_layout(pieces)

    def body(*refs):
        p_refs, refs = refs[:npc], refs[npc:]
        wt_ref, x_ref, dy_ref, ada_ref, g_ref = refs[:5]
        ins, refs = refs[5:5 + n], refs[5 + n:]
        gx_ref, dsh_ref, dsc_ref, dg_ref = refs[:4]
        outs, (send_sems, recv_sems, local_sems) = refs[4:4 + n], refs[4 + n:]
        i = pl.program_id(0)

        @pl.when(i == 0)
        def _():
            for cp in _chip_copies(ins, outs, send_sems, recv_sems, local_sems):
                cp.start()
            dsh_ref[...] = jnp.zeros_like(dsh_ref)
            dsc_ref[...] = jnp.zeros_like(dsc_ref)
            dg_ref[...] = jnp.zeros_like(dg_ref)

        dh = None
        for p_ref, (off, width) in zip(p_refs, layout):
            part = _dot(p_ref[...], wt_ref[off:off + width, :])
            dh = part if dh is None else dh + part
        xv = x_ref[...]
        r = lax.rsqrt(jnp.mean(xv * xv, axis=-1, keepdims=True) + EPS)
        xhat = xv * r
        g = g_ref[...]
        one_sc = 1.0 + ada_ref[1:2, :]
        dsh_ref[...] += jnp.sum(dh, axis=0, keepdims=True)
        dsc_ref[...] += jnp.sum(dh * (xhat * g), axis=0, keepdims=True)
        dg_ref[...] += jnp.sum(dh * xhat, axis=0, keepdims=True) * one_sc
        dxh = dh * (g * one_sc)
        dx = r * (dxh - xhat * jnp.mean(dxh * xhat, axis=-1, keepdims=True))
        gx_ref[...] = dy_ref[...] + dx

        @pl.when(i == nt - 1)
        def _():
            for cp in _chip_copies(ins, outs, send_sems, recv_sems, local_sems):
                cp.wait()

    full = pl.BlockSpec((tm, D_MODEL), lambda i: (i, 0))
    vec = pl.BlockSpec((1, D_MODEL), lambda i: (0, 0))
    any_spec = pl.BlockSpec(memory_space=pl.ANY)
    res = pl.pallas_call(
        body, name="dh_dx", grid=(nt,),
        in_specs=[pl.BlockSpec((tm, p.shape[1]), lambda i: (i, 0)) for p in pieces]
        + [pl.BlockSpec((N_ALL, D_MODEL), lambda i: (0, 0)), full, full,
           pl.BlockSpec((3, D_MODEL), lambda i: (0, 0)), vec] + [any_spec] * n,
        out_specs=[full, vec, vec, vec] + [any_spec] * n,
        out_shape=[jax.ShapeDtypeStruct((s, D_MODEL), F32)] + [jax.ShapeDtypeStruct((1, D_MODEL), F32)] * 3
        + [jax.ShapeDtypeStruct(a.shape, a.dtype) for a in chip_sums],
        scratch_shapes=[pltpu.SemaphoreType.DMA((n * 3,)), pltpu.SemaphoreType.DMA((n * 3,)),
                        pltpu.SemaphoreType.DMA((n,))],
        compiler_params=_params(("arbitrary",)),
    )(*pieces, w_all_t, x, dy, ada3, norm_g, *chip_sums)
    return res[:4], res[4:]


def _sum_small(vec_all, qg_parts, kg_parts):
    def body(v_ref, q_ref, k_ref, tot_ref, gq_ref, gk_ref):
        tot = v_ref[0:1, :]
        for p in range(1, N_DEV):
            tot = tot + v_ref[p:p + 1, :]
        tot_ref[...] = tot
        gq_ref[...] = jnp.sum(q_ref[...], axis=0, keepdims=True)
        gk_ref[...] = jnp.sum(k_ref[...], axis=0, keepdims=True)

    n = vec_all.shape[-1]
    return pl.pallas_call(
        body, name="sum_small",
        out_shape=[jax.ShapeDtypeStruct((1, n), F32),
                   jax.ShapeDtypeStruct((1, HEAD_DIM), F32), jax.ShapeDtypeStruct((1, HEAD_DIM), F32)],
        compiler_params=_params(),
    )(vec_all, qg_parts, kg_parts)


def _grad_w_ada(c_cols, dada_rows):
    def body(c_ref, d_ref, out_ref):
        acc = c_ref[0] * d_ref[0]
        for b in range(1, N_DEV):
            acc = acc + c_ref[b] * d_ref[b]
        out_ref[...] = acc

    return pl.pallas_call(
        body, name="grad_w_ada",
        out_shape=jax.ShapeDtypeStruct((D_MODEL, ADA_SHARD), F32),
        compiler_params=_params(),
    )(c_cols, dada_rows)


def _adam_step(w, m, v, g):
    c1 = 1.0 / (1.0 - ADAM_B1 ** ADAM_STEP)
    c2 = 1.0 / (1.0 - ADAM_B2 ** ADAM_STEP)
    m_new = ADAM_B1 * m + (1.0 - ADAM_B1) * g
    v_new = ADAM_B2 * v + (1.0 - ADAM_B2) * (g * g)
    return -ADAM_LR * ((m_new * c1) / (jnp.sqrt(v_new * c2) + ADAM_EPS) + ADAM_WD * w), m_new, v_new


def _adamw_small(params, name):
    n = len(params)
    stacked = [p[3].ndim == p[0].ndim + 1 for p in params]

    def body(*refs):
        ins, outs = refs[:4 * n], refs[4 * n:]
        for k in range(n):
            w_ref, m_ref, v_ref, g_ref = ins[4 * k:4 * k + 4]
            go_ref, d_ref, mo_ref, vo_ref = outs[4 * k:4 * k + 4]
            if stacked[k]:
                g = g_ref[0].astype(F32)
                for p in range(1, g_ref.shape[0]):
                    g = g + g_ref[p].astype(F32)
            else:
                g = g_ref[...]
            go_ref[...] = g
            d_ref[...], mo_ref[...], vo_ref[...] = _adam_step(w_ref[...], m_ref[...], v_ref[...], g)

    res = pl.pallas_call(
        body, name=name,
        out_shape=[jax.ShapeDtypeStruct(p[0].shape, F32) for p in params for _ in range(4)],
        compiler_params=_params(),
    )(*[a for p in params for a in p])
    return [tuple(res[4 * k:4 * k + 4]) for k in range(n)]


def _adamw(w, m, v, g_parts, name):
    rows, cols = w.shape
    n_parts = g_parts.shape[0]
    tr = 256 if rows % 256 == 0 else rows
    tc = 256 if (tr == rows and rows > 256 and cols % 256 == 0) else cols

    def body(w_ref, m_ref, v_ref, g_ref, go_ref, d_ref, mo_ref, vo_ref):
        g = g_ref[0].astype(F32)
        for p in range(1, n_parts):
            g = g + g_ref[p].astype(F32)
        go_ref[...] = g
        d_ref[...], mo_ref[...], vo_ref[...] = _adam_step(w_ref[...], m_ref[...], v_ref[...], g)

    blk = pl.BlockSpec((tr, tc), lambda i, j: (i, j))
    return pl.pallas_call(
        body, name=name, grid=(rows // tr, cols // tc),
        in_specs=[blk, blk, blk, pl.BlockSpec((n_parts, tr, tc), lambda i, j: (0, i, j))],
        out_specs=[blk] * 4,
        out_shape=[jax.ShapeDtypeStruct((rows, cols), F32)] * 4,
        compiler_params=_params(("parallel", "parallel")),
    )(w, m, v, g_parts)


_O_F = 1536


W_TILE = 16
WIN_ROWS = 784


def _internal_start(p):
    return p * IN_SHARD - (HEADS if p * IN_SHARD > _O_F else 0)


def _shard_window(wt_shard, me):
    o = me * IN_SHARD + lax.broadcasted_iota(jnp.int32, (IN_SHARD, 1), 0)
    is_f = jnp.logical_and(o >= _O_F, o < _O_F + HEADS)
    start = me * IN_SHARD - jnp.where(me * IN_SHARD > _O_F, HEADS, 0)
    main = lax.dynamic_update_slice(jnp.zeros((WIN_ROWS, D_MODEL), BF16),
                                    jnp.where(is_f, 0.0, wt_shard).astype(BF16), (start % W_TILE, 0))
    f_rows = jnp.pad(jnp.where(is_f, wt_shard, 0.0).astype(BF16), ((W_TILE, W_TILE), (0, 0)))
    f_start = jnp.clip(W_TILE + _O_F - me * IN_SHARD, 0, IN_SHARD + W_TILE)
    return jnp.concatenate([main, lax.dynamic_slice(f_rows, (f_start, 0), (W_TILE, D_MODEL))], axis=0)


def _assemble_w(windows):
    chunk = 112

    def body(g_ref, out_ref):
        out_ref[WIN_ROWS:N_MAIN, :] = jnp.zeros((N_MAIN - WIN_ROWS, D_MODEL), BF16)
        for p in range(N_DEV):
            base = _internal_start(p) // W_TILE * W_TILE
            for r in range(0, WIN_ROWS, chunk):
                rows = slice(base + r, base + r + chunk)
                piece = g_ref[p, r:r + chunk, :]
                out_ref[rows, :] = piece if p == 0 else out_ref[rows, :] + piece
        f = g_ref[0, WIN_ROWS:WIN_ROWS + W_TILE, :]
        for p in range(1, N_DEV):
            f = f + g_ref[p, WIN_ROWS:WIN_ROWS + W_TILE, :]
        out_ref[N_MAIN:N_MAIN + W_TILE, :] = f
        out_ref[N_MAIN + W_TILE:N_ALL, :] = jnp.zeros((N_FPAD - W_TILE, D_MODEL), BF16)

    return pl.pallas_call(
        body, name="assemble_w", out_shape=jax.ShapeDtypeStruct((N_ALL, D_MODEL), BF16),
        compiler_params=_params(),
    )(windows)


def _slabs_by_core(dwt, dwt_f):
    sources = ((dwt, 0, _O_F, 0), (dwt_f, _O_F, _O_F + HEADS, _O_F), (dwt, _O_F + HEADS, IN_WIDTH, HEADS))

    def slab(p):
        lo, hi = p * IN_SHARD, (p + 1) * IN_SHARD
        parts = []
        for src, o_lo, o_hi, shift in sources:
            a, b = max(lo, o_lo), min(hi, o_hi)
            if a < b:
                parts.append(src[a - shift:b - shift])
        return parts[0] if len(parts) == 1 else jnp.concatenate(parts, axis=0)

    return jnp.stack([jnp.stack([slab(2 * chip + core) for chip in range(4)]) for core in range(2)])


def kernel(x, c, w_ada, b_ada, norm_g, w_in, b_f, q_norm_g, k_norm_g, conv_w, w_attn_out, w_conv_out, w_o, loss_target, m_w_ada, m_b_ada, m_norm_g, m_w_in, m_b_f, m_q_norm_g, m_k_norm_g, m_conv_w, m_w_attn_out, m_w_conv_out, m_w_o, v_w_ada, v_b_ada, v_norm_g, v_w_in, v_b_f, v_q_norm_g, v_k_norm_g, v_conv_w, v_w_attn_out, v_w_conv_out, v_w_o):
    me = 4 * lax.axis_index("x") + 2 * lax.axis_index("y") + lax.axis_index("c")
    s = x.shape[1]
    x2, t2 = x[0], loss_target[0]

    w_in_g, c_all, ada_g = _gather_weights_and_ada(_shard_window(w_in[0].T, me), c, w_ada[0])
    ada_mine = lax.dynamic_index_in_dim(ada_g[:, :, 0, :], me, axis=1, keepdims=False)
    ada3 = (ada_mine.reshape(1, 3 * D_MODEL) + b_ada).reshape(3, D_MODEL)
    w_all_t = _assemble_w(w_in_g)
    qg = jnp.tile(q_norm_g, (1, HEADS))
    kg = jnp.tile(k_norm_g, (1, HEADS))
    bf_pad = jnp.pad(b_f, ((0, 0), (0, LANES - HEADS)))

    (proj, fl, h), (qa, ka, va, kt, vt), (cw_g, wa_g, wb_g, wo_g) = _proj_fwd(
        x2, ada3, norm_g, w_all_t, bf_pad, qg, kg,
        [conv_w[0], w_attn_out[0].astype(BF16), w_conv_out[0].astype(BF16), w_o[0].astype(BF16)])
    wa = jnp.transpose(wa_g, (1, 0, 2)).reshape(ATTN_W, D_MODEL)
    wb = jnp.transpose(wb_g, (1, 0, 2)).reshape(CONV_W, D_MODEL)
    wo = wo_g.reshape(D_MODEL, D_MODEL)
    cw = jnp.transpose(cw_g, (1, 0, 2)).reshape(3, CONV_W)
    attn, oa, qb = _attn_fwd(qa, ka, vt, proj)
    (dy, dgab, do, dza, dob, dwo, dwa, dwb, dgate, loss_part) = _tail(oa, attn, proj, x2, t2, ada3, wa, wb, wo, cw)

    core = lax.axis_index("c").astype(jnp.int32).reshape(1)
    small = [jnp.transpose(dwa.reshape(ATTN_W, N_DEV, LANES), (1, 0, 2)).astype(BF16),
             jnp.transpose(dwb.reshape(CONV_W, N_DEV, LANES), (1, 0, 2)).astype(BF16),
             dwo.reshape(N_DEV, D_MODEL // N_DEV, D_MODEL).astype(BF16)]
    dq, dk, dv, dqg, dkg, dcum = _attn_bwd(qb, ka, kt, va, do, proj, qg, kg)
    df, dbf = _forget_bwd(dcum, fl, bf_pad)
    dcb, dcc, dcu, dcz, dcw = _conv_bwd(dob, proj, cw)
    pieces = [dq, dk, dv, dza, dcb, dcc, dcu, dcz, dgab, df]
    (dw_main, dw_f), (g_wa_parts, g_wb_parts, g_wo_parts) = _dw_in(h, pieces, small)

    slabs_in = _slabs_by_core(dw_main, dw_f)
    (theirs_in,) = _sibling_swap([slabs_in], "swap_w_in")
    (grad_x, dshift, dscale, dnormg), (g_in_parts,) = _dh_and_dx(
        pieces, w_all_t, x2, dy, ada3, norm_g, [_pair_sum(slabs_in, theirs_in, core, "pair_sum_w_in")])
    vec = jnp.concatenate([dshift, dscale, dgate, dnormg, dbf, dcw.reshape(1, 3 * CONV_W), loss_part, dqg, dkg],
                          axis=1)
    (vec_all,) = _gather_direct([vec], "gather_small")
    vec_all = vec_all.reshape(N_DEV, vec.shape[1])
    n_main = 4 * D_MODEL + LANES + 3 * CONV_W + LANES
    tot, g_qg, g_kg = _sum_small(
        vec_all[:, :n_main],
        vec_all[:, n_main:n_main + ATTN_W].reshape(N_DEV * HEADS, HEAD_DIM),
        vec_all[:, n_main + ATTN_W:].reshape(N_DEV * HEADS, HEAD_DIM))
    g_b_ada = tot[:, 0:3 * D_MODEL]
    g_norm_g = tot[:, 3 * D_MODEL:4 * D_MODEL]
    g_b_f = tot[:, 4 * D_MODEL:4 * D_MODEL + HEADS]
    g_cw_full = tot[:, 4 * D_MODEL + LANES:4 * D_MODEL + LANES + 3 * CONV_W].reshape(3, CONV_W)
    g_cw = lax.dynamic_slice(g_cw_full, (0, me * (CONV_W // N_DEV)), (3, CONV_W // N_DEV))
    dada_mine = lax.dynamic_slice(vec_all[:, 0:3 * D_MODEL], (0, me * ADA_SHARD), (N_DEV, ADA_SHARD))
    g_w_ada = _grad_w_ada(jnp.transpose(c_all, (0, 2, 1)), dada_mine.reshape(N_DEV, 1, ADA_SHARD))

    upd = {}
    upd["w_ada"] = _adamw(w_ada[0], m_w_ada[0], v_w_ada[0], g_w_ada[None], "adamw_w_ada")
    upd["w_in"] = [u.T for u in _adamw(w_in[0].T, m_w_in[0].T, v_w_in[0].T, g_in_parts, "adamw_w_in")]
    small_names = ["b_ada", "norm_g", "b_f", "q_norm_g", "k_norm_g", "conv_w", "w_attn_out", "w_conv_out", "w_o"]
    small_upd = _adamw_small(
        [(b_ada, m_b_ada, v_b_ada, g_b_ada), (norm_g, m_norm_g, v_norm_g, g_norm_g), (b_f, m_b_f, v_b_f, g_b_f),
         (q_norm_g, m_q_norm_g, v_q_norm_g, g_qg), (k_norm_g, m_k_norm_g, v_k_norm_g, g_kg),
         (conv_w[0], m_conv_w[0], v_conv_w[0], g_cw),
         (w_attn_out[0], m_w_attn_out[0], v_w_attn_out[0], g_wa_parts),
         (w_conv_out[0], m_w_conv_out[0], v_w_conv_out[0], g_wb_parts),
         (w_o[0], m_w_o[0], v_w_o[0], g_wo_parts)], "adamw_small")
    upd.update(zip(small_names, small_upd))

    names = ["w_ada", "b_ada", "norm_g", "w_in", "b_f", "q_norm_g", "k_norm_g", "conv_w",
             "w_attn_out", "w_conv_out", "w_o"]
    lead = {"w_ada", "w_in", "conv_w", "w_attn_out", "w_conv_out", "w_o"}
    fix = lambda n, a: a[None] if n in lead else a
    loss = tot[0, n_main - LANES]
    outs = [loss, grad_x[None]]
    for k in range(4):
        outs += [fix(n, upd[n][k]) for n in names]
    return tuple(outs)
```

```python
import functools

import numpy as np
import jax
import jax.numpy as jnp
from jax import lax
from jax.experimental import pallas as pl
from jax.experimental.pallas import tpu as pltpu

F32 = jnp.float32
BF16 = jnp.bfloat16

D_MODEL = 1024
HEADS = 8
HEAD_DIM = 64
ATTN_W = 512
CONV_W = 512
N_DEV = 8
IN_WIDTH = 6152
IN_SHARD = IN_WIDTH // N_DEV
N_MAIN = 6144
N_FPAD = 128
N_ALL = N_MAIN + N_FPAD
ADA_SHARD = 3 * D_MODEL // N_DEV
EPS = 1e-6
NEG = -1e30

ADAM_LR = 0.001
ADAM_B1 = 0.9
ADAM_B2 = 0.999
ADAM_EPS = 1e-08
ADAM_WD = 0.01
ADAM_STEP = 10

LANES = 128
VMEM_LIMIT = 56 * 1024 * 1024

TM_PROJ = 512
TN_PROJ = 1024
TM_ELEM = 512
TQ = 512
HEADS_PER_STEP = 8
HEADS_PER_STEP_FWD = 8
TM_TAIL = 256
TC_CUM = 512
TK_DW = 2048
TN_DW = 512
TM_DH = 256
HALO = 16

OFF_Q, OFF_K, OFF_V, OFF_ZA, OFF_CB, OFF_CC, OFF_CU, OFF_CZ, OFF_GA, OFF_GB = (
    0, 512, 1024, 1536, 2048, 2560, 3072, 3584, 4096, 5120)


def _params(sem=None):
    return pltpu.CompilerParams(dimension_semantics=sem, vmem_limit_bytes=VMEM_LIMIT)


def _dot(a, b):
    return jnp.dot(a, b, preferred_element_type=F32)


def _dot_nt(a, b):
    return lax.dot_general(a, b, (((1,), (1,)), ((), ())), preferred_element_type=F32)


def _dot_tn(a, b):
    return lax.dot_general(a, b, (((0,), (0,)), ((), ())), preferred_element_type=F32)


def _sigmoid(x):
    return 1.0 / (1.0 + jnp.exp(-x))


def _lane_lo(shape):
    return lax.broadcasted_iota(jnp.int32, shape, len(shape) - 1) < HEAD_DIM


def _seg_sum(z, lo):
    a = jnp.sum(jnp.where(lo, z, 0.0), axis=-1, keepdims=True)
    b = jnp.sum(jnp.where(lo, 0.0, z), axis=-1, keepdims=True)
    return jnp.where(lo, a, b)


def _lane_col(z, lane):
    idx = lax.broadcasted_iota(jnp.int32, z.shape, 1)
    return jnp.sum(jnp.where(idx == lane, z, 0.0), axis=-1, keepdims=True)


def _sub_row(z, row):
    idx = lax.broadcasted_iota(jnp.int32, z.shape, 0)
    return jnp.sum(jnp.where(idx == row, z, 0.0), axis=0, keepdims=True)


def _mesh_pos():
    x, y, c = lax.axis_index("x"), lax.axis_index("y"), lax.axis_index("c")
    return x, y, c, 4 * x + 2 * y + c


def _peer(k, x, y, c):
    px = 1 - x if (k >> 2) & 1 else x
    py = 1 - y if (k >> 1) & 1 else y
    pc = 1 - c if k & 1 else c
    return (px, py, pc), 4 * px + 2 * py + pc


def _gather_copies(ins, outs, send_sems, recv_sems, local_sems):
    x, y, c, me = _mesh_pos()
    copies = []
    for a in range(len(ins)):
        copies.append(pltpu.make_async_copy(ins[a], outs[a].at[me], local_sems.at[a]))
        for k in range(1, N_DEV):
            dev, _ = _peer(k, x, y, c)
            copies.append(pltpu.make_async_remote_copy(
                src_ref=ins[a], dst_ref=outs[a].at[me],
                send_sem=send_sems.at[a * (N_DEV - 1) + k - 1], recv_sem=recv_sems.at[a * (N_DEV - 1) + k - 1],
                device_id=dev, device_id_type=pl.DeviceIdType.MESH))
    return copies


def _gather_sems(n):
    return [pltpu.SemaphoreType.DMA((n * (N_DEV - 1),)), pltpu.SemaphoreType.DMA((n * (N_DEV - 1),)),
            pltpu.SemaphoreType.DMA((n,))]


def _gather_direct(arrs, name):
    n = len(arrs)
    any_spec = pl.BlockSpec(memory_space=pl.ANY)

    def body(*refs):
        copies = _gather_copies(refs[:n], refs[n:2 * n], *refs[2 * n:])
        for cp in copies:
            cp.start()
        for cp in copies:
            cp.wait()

    return pl.pallas_call(
        body, name=name, out_shape=[jax.ShapeDtypeStruct((N_DEV,) + a.shape, a.dtype) for a in arrs],
        in_specs=[any_spec] * n, out_specs=[any_spec] * n, scratch_shapes=_gather_sems(n),
    )(*arrs)


def _ada_phase(c_ref, w_ref, call_ref, adag_ref, mine_ref, send_sems, recv_sems):
    x, y, c, me = _mesh_pos()

    def copy(phase, k, src, dst):
        dev, _ = _peer(k, x, y, c)
        return pltpu.make_async_remote_copy(
            src_ref=src, dst_ref=dst,
            send_sem=send_sems.at[phase * (N_DEV - 1) + k - 1],
            recv_sem=recv_sems.at[phase * (N_DEV - 1) + k - 1],
            device_id=dev, device_id_type=pl.DeviceIdType.MESH)

    call_ref[me] = c_ref[...]
    first = [copy(0, k, c_ref, call_ref.at[me]) for k in range(1, N_DEV)]
    for cp in first:
        cp.start()
    for cp in first:
        cp.wait()
    wb = w_ref[...].astype(BF16)
    for b in range(N_DEV):
        row = jnp.broadcast_to(call_ref[b], (8, D_MODEL)).astype(BF16)
        mine_ref[b] = _sub_row(_dot(row, wb), 0)
    adag_ref[me] = mine_ref[...]
    second = [copy(1, k, mine_ref, adag_ref.at[me]) for k in range(1, N_DEV)]
    for cp in second:
        cp.start()
    for cp in second:
        cp.wait()


def _gather_weights_and_ada(wt_shard, c_row, w_ada_sh):
    any_spec = pl.BlockSpec(memory_space=pl.ANY)
    vm = pl.BlockSpec(memory_space=pltpu.VMEM)

    def body(w_in_ref, c_ref, wada_ref, out_ref, call_ref, adag_ref, mine_ref, send_sems, recv_sems, local_sem,
             ada_send, ada_recv):
        x, y, c, me = _mesh_pos()
        sibling = (x, y, 1 - c)
        chips = [(1 - x, y), (x, 1 - y), (1 - x, 1 - y)]

        def copy(k, src, blk, to):
            return pltpu.make_async_remote_copy(
                src_ref=src, dst_ref=out_ref.at[blk], send_sem=send_sems.at[k], recv_sem=recv_sems.at[k],
                device_id=to, device_id_type=pl.DeviceIdType.MESH)

        local = pltpu.make_async_copy(w_in_ref, out_ref.at[me], local_sem.at[0])
        local.start()
        first = [copy(0, w_in_ref, me, sibling)]
        first += [copy(1 + j, w_in_ref, me, (px, py, c)) for j, (px, py) in enumerate(chips)]
        for cp in first:
            cp.start()
        _ada_phase(c_ref, wada_ref, call_ref, adag_ref, mine_ref, ada_send, ada_recv)
        passed = []
        for j, (px, py) in enumerate(chips):
            blk = 4 * px + 2 * py + c
            copy(1 + j, w_in_ref, blk, (x, y, c)).wait_recv()
            fwd = copy(4 + j, out_ref.at[blk], blk, sibling)
            fwd.start()
            passed.append(fwd)
        copy(0, w_in_ref, 4 * x + 2 * y + 1 - c, (x, y, c)).wait_recv()
        for j, (px, py) in enumerate(chips):
            copy(4 + j, w_in_ref, 4 * px + 2 * py + 1 - c, (x, y, c)).wait_recv()
        for cp in first + passed:
            cp.wait_send()
        local.wait()

    per = N_DEV - 1
    return pl.pallas_call(
        body, name="gather_weights",
        out_shape=[jax.ShapeDtypeStruct((N_DEV,) + wt_shard.shape, wt_shard.dtype),
                   jax.ShapeDtypeStruct((N_DEV, 1, D_MODEL), F32),
                   jax.ShapeDtypeStruct((N_DEV, N_DEV, 1, ADA_SHARD), F32)],
        in_specs=[any_spec, vm, vm], out_specs=[any_spec, vm, vm],
        scratch_shapes=[pltpu.VMEM((N_DEV, 1, ADA_SHARD), F32),
                        pltpu.SemaphoreType.DMA((per,)), pltpu.SemaphoreType.DMA((per,)),
                        pltpu.SemaphoreType.DMA((1,)),
                        pltpu.SemaphoreType.DMA((2 * per,)), pltpu.SemaphoreType.DMA((2 * per,))],
        compiler_params=pltpu.CompilerParams(vmem_limit_bytes=VMEM_LIMIT),
    )(wt_shard, c_row, w_ada_sh)


def _sibling_swap(arrs, name):
    n = len(arrs)
    any_spec = pl.BlockSpec(memory_space=pl.ANY)

    def body(*refs):
        ins, outs = refs[:n], refs[n:2 * n]
        send_sems, recv_sems = refs[2 * n:]
        x, y, c, _ = _mesh_pos()
        copies = [pltpu.make_async_remote_copy(
            src_ref=ins[a].at[1 - c], dst_ref=outs[a], send_sem=send_sems.at[a], recv_sem=recv_sems.at[a],
            device_id=(x, y, 1 - c), device_id_type=pl.DeviceIdType.MESH) for a in range(n)]
        for cp in copies:
            cp.start()
        for cp in copies:
            cp.wait()

    return pl.pallas_call(
        body, name=name,
        out_shape=[jax.ShapeDtypeStruct(a.shape[1:], a.dtype) for a in arrs],
        in_specs=[any_spec] * n, out_specs=[any_spec] * n,
        scratch_shapes=[pltpu.SemaphoreType.DMA((n,)), pltpu.SemaphoreType.DMA((n,))],
    )(*arrs)


def _pair_sum(mine2, theirs, core, name):
    _, _, rows, cols = mine2.shape
    tr = 256 if rows % 256 == 0 else rows

    def body(core_ref, a_ref, b_ref, out_ref):
        out_ref[...] = (a_ref[...].astype(F32) + b_ref[...].astype(F32)).astype(BF16)

    return pl.pallas_call(
        body, name=name,
        grid_spec=pltpu.PrefetchScalarGridSpec(
            num_scalar_prefetch=1, grid=(4, rows // tr),
            in_specs=[pl.BlockSpec((None, None, tr, cols), lambda ch, i, core_: (core_[0], ch, i, 0)),
                      pl.BlockSpec((None, tr, cols), lambda ch, i, core_: (ch, i, 0))],
            out_specs=pl.BlockSpec((None, tr, cols), lambda ch, i, core_: (ch, i, 0))),
        out_shape=jax.ShapeDtypeStruct(theirs.shape, BF16),
        compiler_params=_params(("parallel", "parallel")),
    )(core, mine2, theirs)


def _all_to_all_copies(ins, outs, send_sems, recv_sems, local_sems):
    x, y, c, me = _mesh_pos()
    copies = []
    for a in range(len(ins)):
        copies.append(pltpu.make_async_copy(ins[a].at[me], outs[a].at[me], local_sems.at[a]))
        for k in range(1, N_DEV):
            dev, p = _peer(k, x, y, c)
            copies.append(pltpu.make_async_remote_copy(
                src_ref=ins[a].at[p], dst_ref=outs[a].at[me],
                send_sem=send_sems.at[a * (N_DEV - 1) + k - 1], recv_sem=recv_sems.at[a * (N_DEV - 1) + k - 1],
                device_id=dev, device_id_type=pl.DeviceIdType.MESH))
    return copies


def _chip_copies(ins, outs, send_sems, recv_sems, local_sems):
    x, y, c, _ = _mesh_pos()
    my_chip = 2 * x + y
    chips = [(1 - x, y), (x, 1 - y), (1 - x, 1 - y)]
    copies = []
    for a in range(len(ins)):
        copies.append(pltpu.make_async_copy(ins[a].at[my_chip], outs[a].at[my_chip], local_sems.at[a]))
        for j, (px, py) in enumerate(chips):
            copies.append(pltpu.make_async_remote_copy(
                src_ref=ins[a].at[2 * px + py], dst_ref=outs[a].at[my_chip],
                send_sem=send_sems.at[a * 3 + j], recv_sem=recv_sems.at[a * 3 + j],
                device_id=(px, py, c), device_id_type=pl.DeviceIdType.MESH))
    return copies


def _proj_fwd(x, ada3, norm_g, w_all_t, bf_pad, qg, kg, later):
    s = x.shape[0]
    tm, tn = min(TM_PROJ, s), TN_PROJ
    nt = s // tm
    n = len(later)

    def body(x_ref, ada_ref, g_ref, wt_ref, bf_ref, qg_ref, kg_ref, *rest):
        ins, (proj_ref, fl_ref, h_ref), rows_refs, rest = rest[:n], rest[n:n + 3], rest[n + 3:n + 8], rest[n + 8:]
        outs, (carry, send_sems, recv_sems, local_sems) = rest[:n], rest[n:]
        i = pl.program_id(0)

        @pl.when(i == 0)
        def _():
            carry[...] = jnp.zeros_like(carry)
            for cp in _gather_copies(ins, outs, send_sems, recv_sems, local_sems):
                cp.start()

        xv = x_ref[...]
        r = lax.rsqrt(jnp.mean(xv * xv, axis=-1, keepdims=True) + EPS)
        hv = ((xv * r) * g_ref[...]) * (1.0 + ada_ref[1:2, :]) + ada_ref[0:1, :]
        hb = hv.astype(BF16)
        h_ref[...] = hb
        fl = _dot_nt(hb, wt_ref[N_MAIN:N_ALL, :])
        fl_ref[...] = fl
        for j in range(N_MAIN // tn):
            proj_ref[:, j * tn:(j + 1) * tn] = _dot_nt(hb, wt_ref[j * tn:(j + 1) * tn, :]).astype(BF16)
        _attention_rows(proj_ref, fl, bf_ref, qg_ref, kg_ref, carry, *rows_refs)

        @pl.when(i == nt - 1)
        def _():
            for cp in _gather_copies(ins, outs, send_sems, recv_sems, local_sems):
                cp.wait()

    any_spec = pl.BlockSpec(memory_space=pl.ANY)
    heads = pl.BlockSpec((HEADS, tm, LANES), lambda i: (0, i, 0))
    heads_t = pl.BlockSpec((HEADS, LANES, tm), lambda i: (0, 0, i))
    vec = pl.BlockSpec((1, ATTN_W), lambda i: (0, 0))
    res = pl.pallas_call(
        body, name="proj_fwd", grid=(nt,),
        in_specs=[pl.BlockSpec((tm, D_MODEL), lambda i: (i, 0)),
                  pl.BlockSpec((3, D_MODEL), lambda i: (0, 0)),
                  pl.BlockSpec((1, D_MODEL), lambda i: (0, 0)),
                  pl.BlockSpec((N_ALL, D_MODEL), lambda i: (0, 0)),
                  pl.BlockSpec((1, LANES), lambda i: (0, 0)), vec, vec] + [any_spec] * n,
        out_specs=[pl.BlockSpec((tm, N_MAIN), lambda i: (i, 0)),
                   pl.BlockSpec((tm, N_FPAD), lambda i: (i, 0)),
                   pl.BlockSpec((tm, D_MODEL), lambda i: (i, 0)),
                   heads, heads, heads, heads_t, heads_t] + [any_spec] * n,
        out_shape=[jax.ShapeDtypeStruct((s, N_MAIN), BF16),
                   jax.ShapeDtypeStruct((s, N_FPAD), F32),
                   jax.ShapeDtypeStruct((s, D_MODEL), BF16)]
        + [jax.ShapeDtypeStruct((HEADS, s, LANES), BF16)] * 3
        + [jax.ShapeDtypeStruct((HEADS, LANES, s), BF16)] * 2
        + [jax.ShapeDtypeStruct((N_DEV,) + a.shape, a.dtype) for a in later],
        scratch_shapes=[pltpu.VMEM((1, LANES), F32)] + _gather_sems(n),
        compiler_params=_params(("arbitrary",)),
    )(x, ada3, norm_g, w_all_t, bf_pad, qg, kg, *later)
    return res[:3], res[3:8], res[8:]


L_ONE_Q, L_F_Q, L_LSE_Q, L_END = HEAD_DIM, HEAD_DIM + 3, HEAD_DIM + 6, HEAD_DIM + 9


def _split3(f):
    hi = f.astype(BF16).astype(F32)
    r = f - hi
    mid = r.astype(BF16).astype(F32)
    return hi, mid, r - mid


def _place3(lane, first, parts, otherwise):
    a, b, c = parts
    return jnp.where(lane == first, a, jnp.where(lane == first + 1, b, jnp.where(lane == first + 2, c, otherwise)))


def _log_forget(fl, bf):
    z = fl + bf
    lf = jnp.minimum(z, 0.0) - jnp.log1p(jnp.exp(-jnp.abs(z)))
    lane = lax.broadcasted_iota(jnp.int32, z.shape, 1)
    return jnp.where(lane < HEADS, lf, 0.0)


def _attention_rows(p_ref, fl, bf_ref, qg_ref, kg_ref, carry, qa_ref, ka_ref, va_ref, kt_ref, vt_ref):
    tm = fl.shape[0]
    scale = HEAD_DIM ** -0.5
    tri = (lax.broadcasted_iota(jnp.int32, (tm, tm), 1) <= lax.broadcasted_iota(jnp.int32, (tm, tm), 0)).astype(F32)
    cum_v = jnp.dot(tri, _log_forget(fl, bf_ref[...]), preferred_element_type=F32,
                    precision=lax.Precision.HIGHEST) + carry[...]
    carry[...] = _sub_row(cum_v, tm - 1)
    lane = lax.broadcasted_iota(jnp.int32, (tm, LANES), 1)
    lo = lane < HEAD_DIM
    v_tail = jnp.where(lane < L_F_Q, 1.0, 0.0)
    for pr in range(ATTN_W // LANES):
        sl = slice(pr * LANES, (pr + 1) * LANES)
        q2 = p_ref[:, OFF_Q + pr * LANES:OFF_Q + (pr + 1) * LANES].astype(F32)
        k2 = p_ref[:, OFF_K + pr * LANES:OFF_K + (pr + 1) * LANES].astype(F32)
        v2 = p_ref[:, OFF_V + pr * LANES:OFF_V + (pr + 1) * LANES].astype(F32)
        rq = lax.rsqrt(_seg_sum(q2 * q2, lo) * (1.0 / HEAD_DIM) + EPS)
        rk = lax.rsqrt(_seg_sum(k2 * k2, lo) * (1.0 / HEAD_DIM) + EPS)
        qn = ((q2 * rq) * qg_ref[:, sl]) * scale
        kn = (k2 * rk) * kg_ref[:, sl]
        for hh in range(2):
            h = 2 * pr + hh
            f3 = _split3(_lane_col(cum_v, h))
            qh = qn if hh == 0 else pltpu.roll(qn, HEAD_DIM, 1)
            kh = kn if hh == 0 else pltpu.roll(kn, HEAD_DIM, 1)
            vh = v2 if hh == 0 else pltpu.roll(v2, HEAD_DIM, 1)
            q_tail = jnp.where(lane < L_F_Q, 1.0, _place3(lane, L_F_Q, f3, 0.0))
            k_tail = _place3(lane, L_ONE_Q, tuple(-f for f in f3), jnp.where(lane < L_END, 1.0, 0.0))
            k_row = jnp.where(lo, kh, k_tail)
            v_row = jnp.where(lo, vh, v_tail)
            qa_ref[h] = jnp.where(lo, qh, q_tail).astype(BF16)
            ka_ref[h] = k_row.astype(BF16)
            va_ref[h] = v_row.astype(BF16)
            kt_ref[h] = k_row.T.astype(BF16)
            vt_ref[h] = v_row.T.astype(BF16)


def _causal_t(t):
    return lax.broadcasted_iota(jnp.int32, (t, t), 0) <= lax.broadcasted_iota(jnp.int32, (t, t), 1)


def _tri_steps(nt, q_major):
    if q_major:
        pairs = [(i, j) for i in range(nt) for j in range(i + 1)]
    else:
        pairs = [(i, j) for j in range(nt) for i in range(j, nt)]
    return (jnp.asarray(np.array([p[0] for p in pairs], np.int32)),
            jnp.asarray(np.array([p[1] for p in pairs], np.int32)))


def _attn_fwd(qa, ka, vt, proj):
    s = qa.shape[1]
    t = min(TQ, s)
    it, jt = _tri_steps(s // t, True)
    hp = HEADS_PER_STEP_FWD
    wide = hp * HEAD_DIM
    za_blk = OFF_ZA // wide

    def body(it_ref, jt_ref, q_ref, k_ref, vt_ref, za_ref, attn_ref, oa_ref, qb_ref, m_s, acc_s, pair_s):
        step = pl.program_id(1)
        i, j = it_ref[step], jt_ref[step]

        @pl.when(j == 0)
        def _():
            m_s[...] = jnp.full_like(m_s, NEG)
            acc_s[...] = jnp.zeros_like(acc_s)

        def update(masked):
            for hh in range(hp):
                st = _dot_nt(k_ref[hh], q_ref[hh])
                if masked:
                    st = jnp.where(_causal_t(t), st, NEG)
                m_prev = m_s[hh]
                m_next = jnp.maximum(m_prev, jnp.max(st, axis=0, keepdims=True))
                alpha = jnp.exp(m_prev - m_next)
                pt = jnp.exp(st - m_next).astype(BF16)
                acc_s[hh] = acc_s[hh] * alpha + _dot(vt_ref[hh], pt)
                m_s[hh] = m_next

        @pl.when(j < i)
        def _():
            update(False)

        @pl.when(j == i)
        def _():
            update(True)
            row = lax.broadcasted_iota(jnp.int32, (LANES, t), 0)
            lane = lax.broadcasted_iota(jnp.int32, (t, LANES), 1)
            for hh in range(hp):
                l_row = acc_s[hh, L_ONE_Q:L_ONE_Q + 1, :]
                pair_s[hh * HEAD_DIM:(hh + 1) * HEAD_DIM, :] = acc_s[hh, 0:HEAD_DIM, :] / l_row
                lse3 = _split3(m_s[hh] + jnp.log(l_row))
                tail_t = _place3(row, L_LSE_Q, tuple(-x for x in lse3), 0.0)
                keep_q = jnp.logical_or(lane < L_LSE_Q, lane >= L_END)
                qb_ref[hh] = jnp.where(keep_q, q_ref[hh].astype(F32), tail_t.T).astype(BF16)
            out = pair_s[...].T
            attn_ref[...] = out
            z = za_ref[...].astype(F32)
            oa_ref[...] = (out * (z * _sigmoid(z))).astype(BF16)

    pair_q = pl.BlockSpec((hp, t, LANES), lambda p, n, it_, jt_: (p, it_[n], 0))
    pair_k = pl.BlockSpec((hp, t, LANES), lambda p, n, it_, jt_: (p, jt_[n], 0))
    pair_kt = pl.BlockSpec((hp, LANES, t), lambda p, n, it_, jt_: (p, 0, jt_[n]))
    out_q = pl.BlockSpec((t, wide), lambda p, n, it_, jt_: (it_[n], p))
    return pl.pallas_call(
        body, name="attn_fwd",
        grid_spec=pltpu.PrefetchScalarGridSpec(
            num_scalar_prefetch=2, grid=(HEADS // hp, it.shape[0]),
            in_specs=[pair_q, pair_k, pair_kt,
                      pl.BlockSpec((t, wide), lambda p, n, it_, jt_: (it_[n], za_blk + p))],
            out_specs=[out_q, out_q, pair_q],
            scratch_shapes=[pltpu.VMEM((hp, 1, t), F32), pltpu.VMEM((hp, LANES, t), F32),
                            pltpu.VMEM((wide, t), F32)]),
        out_shape=[jax.ShapeDtypeStruct((s, ATTN_W), F32),
                   jax.ShapeDtypeStruct((s, ATTN_W), BF16),
                   jax.ShapeDtypeStruct((HEADS, s, LANES), BF16)],
        compiler_params=_params(("parallel", "arbitrary")),
    )(it, jt, qa, ka, vt, proj)


def _conv_parts(gb_ref, gc_ref, u_ref, zb_ref, gch_ref, uh_ref, first, w_ref, tm):
    gb, gc = gb_ref[...].astype(F32), gc_ref[...].astype(F32)
    u, zb = u_ref[...].astype(F32), zb_ref[...].astype(F32)
    cu = gc * u
    cu_h = jnp.where(first, 0.0, gch_ref[...].astype(F32) * uh_ref[...].astype(F32))
    prev1, prev2 = _sub_row(cu_h, HALO - 1), _sub_row(cu_h, HALO - 2)
    row = lax.broadcasted_iota(jnp.int32, cu.shape, 0)
    r1 = jnp.where(row == 0, prev1, pltpu.roll(cu, 1, 0))
    r2 = jnp.where(row == 0, prev2, jnp.where(row == 1, prev1, pltpu.roll(cu, 2, 0)))
    conv = w_ref[2:3, :] * cu + w_ref[1:2, :] * r1 + w_ref[0:1, :] * r2
    return gb, gc, u, zb, cu, r1, r2, conv


def _conv_specs(tm, s, width=LANES):
    def tile(off):
        return pl.BlockSpec((tm, width), lambda c, i: (i, off // width + c))

    def before(off):
        return pl.BlockSpec((HALO, width), lambda c, i: (jnp.maximum(i * (tm // HALO) - 1, 0), off // width + c))

    def after(off):
        return pl.BlockSpec((HALO, width),
                            lambda c, i: (jnp.minimum((i + 1) * (tm // HALO), s // HALO - 1), off // width + c))

    return ([tile(OFF_CB), tile(OFF_CC), tile(OFF_CU), tile(OFF_CZ)], [before(OFF_CC), before(OFF_CU)],
            [after(OFF_CB), after(OFF_CZ)])


def _tail(oa, attn, proj, x, target, ada3, wa, wb, wo, conv_w):
    s = x.shape[0]
    tm = min(TM_TAIL, s)
    gab_blk = OFF_GA // (2 * D_MODEL)
    za_blk = OFF_ZA // ATTN_W
    tiles, befores, _ = _conv_specs(tm, s, CONV_W)

    def body(oa_ref, attn_ref, za_ref, gb_ref, gc_ref, u_ref, zb_ref, gch_ref, uh_ref, cw_ref, gab_ref, x_ref, t_ref,
             ada_ref, wa_ref, wb_ref, wo_ref,
             dy_ref, dgab_ref, do_ref, dza_ref, dob_ref, dwo_ref, dwa_ref, dwb_ref, dgate_ref, loss_ref):
        first = pl.program_id(0) == 0

        @pl.when(first)
        def _():
            dwo_ref[...] = jnp.zeros_like(dwo_ref)
            dwa_ref[...] = jnp.zeros_like(dwa_ref)
            dwb_ref[...] = jnp.zeros_like(dwb_ref)
            dgate_ref[...] = jnp.zeros_like(dgate_ref)
            loss_ref[...] = jnp.zeros_like(loss_ref)

        gb, _, _, zb, _, _, _, conv = _conv_parts(gb_ref, gc_ref, u_ref, zb_ref, gch_ref, uh_ref, first, cw_ref, tm)
        ob_v = (gb * conv * (zb * _sigmoid(zb))).astype(BF16)
        oa_v = oa_ref[...]
        wa_v, wb_v, wo_v = wa_ref[...], wb_ref[...], wo_ref[...]
        a2 = _dot(oa_v, wa_v)
        b2 = _dot(ob_v, wb_v)
        sa = _sigmoid(gab_ref[:, 0:D_MODEL].astype(F32))
        sb = _sigmoid(gab_ref[:, D_MODEL:2 * D_MODEL].astype(F32))
        mb = (sa * a2 + sb * b2).astype(BF16)
        mo = _dot(mb, wo_v)
        gate = ada_ref[2:3, :]
        err = (x_ref[...] + gate * mo) - t_ref[...]
        dy = err * (1.0 / D_MODEL)
        dy_ref[...] = dy
        loss_ref[...] += 0.5 * jnp.sum(err * err) * (1.0 / D_MODEL)
        dgate_ref[...] += jnp.sum(dy * mo, axis=0, keepdims=True)
        dmo = (dy * gate).astype(BF16)
        dmerged = _dot_nt(dmo, wo_v)
        dwo_ref[...] += _dot_tn(mb, dmo)
        da2 = (dmerged * sa).astype(BF16)
        db2 = (dmerged * sb).astype(BF16)
        dgab_ref[:, 0:D_MODEL] = (dmerged * a2 * (sa * (1.0 - sa))).astype(BF16)
        dgab_ref[:, D_MODEL:2 * D_MODEL] = (dmerged * b2 * (sb * (1.0 - sb))).astype(BF16)
        doa = _dot_nt(da2, wa_v)
        dob_ref[...] = _dot_nt(db2, wb_v)
        dwa_ref[...] += _dot_tn(oa_v, da2)
        dwb_ref[...] += _dot_tn(ob_v, db2)

        lane = lax.broadcasted_iota(jnp.int32, (tm, LANES), 1)
        lo = lane < HEAD_DIM
        for pr in range(ATTN_W // LANES):
            sl = slice(pr * LANES, (pr + 1) * LANES)
            g, a, z = doa[:, sl], attn_ref[:, sl], za_ref[:, sl].astype(F32)
            sg = _sigmoid(z)
            dat = (g * (z * sg)).astype(BF16).astype(F32)
            prod = dat * a
            dza_ref[:, sl] = (g * a * (sg * (1.0 + z * (1.0 - sg)))).astype(BF16)
            for hh in range(2):
                sel = lo if hh == 0 else jnp.logical_not(lo)
                delta3 = _split3(jnp.sum(jnp.where(sel, prod, 0.0), axis=-1, keepdims=True))
                dh = dat if hh == 0 else pltpu.roll(dat, HEAD_DIM, 1)
                tail_lanes = _place3(lane, L_ONE_Q, tuple(-d for d in delta3), 0.0)
                do_ref[2 * pr + hh] = jnp.where(lo, dh, tail_lanes).astype(BF16)

    half = pl.BlockSpec((tm, ATTN_W), lambda i: (i, 0))
    full = pl.BlockSpec((tm, D_MODEL), lambda i: (i, 0))

    def const(shape):
        return pl.BlockSpec(shape, lambda i: (0, 0))

    def one_axis(spec):
        return pl.BlockSpec(spec.block_shape, lambda i, f=spec.index_map: f(0, i))

    return pl.pallas_call(
        body, name="tail", grid=(s // tm,),
        in_specs=[half, half, pl.BlockSpec((tm, ATTN_W), lambda i: (i, za_blk))]
        + [one_axis(sp) for sp in tiles + befores]
        + [const((3, CONV_W)), pl.BlockSpec((tm, 2 * D_MODEL), lambda i: (i, gab_blk)), full, full,
           const((3, D_MODEL)), const((ATTN_W, D_MODEL)), const((CONV_W, D_MODEL)), const((D_MODEL, D_MODEL))],
        out_specs=[full, pl.BlockSpec((tm, 2 * D_MODEL), lambda i: (i, 0)),
                   pl.BlockSpec((HEADS, tm, LANES), lambda i: (0, i, 0)), half, half,
                   const((D_MODEL, D_MODEL)), const((ATTN_W, D_MODEL)), const((CONV_W, D_MODEL)),
                   const((1, D_MODEL)), const((1, LANES))],
        out_shape=[jax.ShapeDtypeStruct((s, D_MODEL), F32),
                   jax.ShapeDtypeStruct((s, 2 * D_MODEL), BF16),
                   jax.ShapeDtypeStruct((HEADS, s, LANES), BF16),
                   jax.ShapeDtypeStruct((s, ATTN_W), BF16),
                   jax.ShapeDtypeStruct((s, CONV_W), F32),
                   jax.ShapeDtypeStruct((D_MODEL, D_MODEL), F32),
                   jax.ShapeDtypeStruct((ATTN_W, D_MODEL), F32),
                   jax.ShapeDtypeStruct((CONV_W, D_MODEL), F32),
                   jax.ShapeDtypeStruct((1, D_MODEL), F32),
                   jax.ShapeDtypeStruct((1, LANES), F32)],
        compiler_params=_params(("arbitrary",)),
    )(oa, attn, proj, *([proj] * 6), conv_w, proj, x, target, ada3, wa, wb, wo)


def _attn_bwd(qb, ka, kt, va, do, proj, qg, kg):
    s = qb.shape[1]
    t = min(TQ, s)
    nt = s // t
    hp = HEADS_PER_STEP
    assert hp == HEADS, "all heads share one (S, 1536) output block"
    wide = hp * HEAD_DIM
    scale = HEAD_DIM ** -0.5
    it, jt = _tri_steps(nt, False)

    def body(it_ref, jt_ref, q_ref, k_ref, kt_ref, v_ref, do_ref, qraw_ref, kraw_ref, qg_ref, kg_ref,
             dqkv_ref, dqg_ref, dkg_ref, dcum_ref, dqt_s, dk_s, dv_s, rows_s):
        grp, step = pl.program_id(0), pl.program_id(1)
        i, j = it_ref[step], jt_ref[step]
        lane = lax.broadcasted_iota(jnp.int32, (t, LANES), 1)
        lo = lane < HEAD_DIM

        @pl.when(step == 0)
        def _():
            dqt_s[...] = jnp.zeros_like(dqt_s)
            dqg_ref[...] = jnp.zeros_like(dqg_ref)
            dkg_ref[...] = jnp.zeros_like(dkg_ref)

        @pl.when(i == j)
        def _():
            dk_s[...] = jnp.zeros_like(dk_s)
            dv_s[...] = jnp.zeros_like(dv_s)

        def update(masked):
            for hh in range(hp):
                qh, doh = q_ref[hh], do_ref[hh]
                st = _dot_nt(k_ref[hh], qh)
                if masked:
                    st = jnp.where(_causal_t(t), st, NEG)
                pt = jnp.exp(st)
                dst = (pt * _dot_nt(v_ref[hh], doh)).astype(BF16)
                dv_s[hh] += _dot(pt.astype(BF16), doh)
                dk_s[hh] += _dot(dst, qh)
                dqt_s[hh, i] += _dot(kt_ref[hh], dst)

        def pair(a, b):
            return jnp.where(lo, a, pltpu.roll(b, HEAD_DIM, 1))

        def norm_bwd(raw, dy, g, dg_ref, off, sl):
            r = lax.rsqrt(_seg_sum(raw * raw, lo) * (1.0 / HEAD_DIM) + EPS)
            xhat = raw * r
            dg_ref[:, sl] += jnp.sum(dy * xhat, axis=0, keepdims=True)
            dxh = dy * g
            dx = r * (dxh - xhat * (_seg_sum(dxh * xhat, lo) * (1.0 / HEAD_DIM)))
            dqkv_ref[:, off + sl.start:off + sl.stop] = dx.astype(BF16)

        @pl.when(i > j)
        def _():
            update(False)

        @pl.when(i == j)
        def _():
            update(True)
            dq_rows = [dqt_s[hh, i].T for hh in range(hp)]
            rows = jnp.zeros((t, LANES), F32)
            for hh in range(hp):
                rows = jnp.where(lane == grp * hp + hh, _lane_col(dq_rows[hh], L_F_Q), rows)
            rows_s[...] = rows
            for pr in range(hp // 2):
                sl = slice(pr * LANES, (pr + 1) * LANES)
                norm_bwd(qraw_ref[:, sl].astype(F32), pair(dq_rows[2 * pr], dq_rows[2 * pr + 1]) * scale,
                         qg_ref[:, sl], dqg_ref, OFF_Q, sl)

        @pl.when(i == nt - 1)
        def _():
            dcum = rows_s[...]
            for hh in range(hp):
                dcum = jnp.where(lane == grp * hp + hh, dcum - _lane_col(dk_s[hh], L_ONE_Q), dcum)
            dcum_ref[0] = dcum
            for pr in range(hp // 2):
                sl = slice(pr * LANES, (pr + 1) * LANES)
                norm_bwd(kraw_ref[:, sl].astype(F32), pair(dk_s[2 * pr], dk_s[2 * pr + 1]),
                         kg_ref[:, sl], dkg_ref, OFF_K, sl)
                dqkv_ref[:, OFF_V + sl.start:OFF_V + sl.stop] = pair(dv_s[2 * pr], dv_s[2 * pr + 1]).astype(BF16)

    pair_q = pl.BlockSpec((hp, t, LANES), lambda p, n, it_, jt_: (p, it_[n], 0))
    pair_k = pl.BlockSpec((hp, t, LANES), lambda p, n, it_, jt_: (p, jt_[n], 0))
    pair_kt = pl.BlockSpec((hp, LANES, t), lambda p, n, it_, jt_: (p, 0, jt_[n]))
    tok3 = pl.BlockSpec((t, 3 * ATTN_W), lambda p, n, it_, jt_: (jt_[n], 0))
    gain = pl.BlockSpec((1, wide), lambda p, n, it_, jt_: (0, p))
    return pl.pallas_call(
        body, name="attn_bwd",
        grid_spec=pltpu.PrefetchScalarGridSpec(
            num_scalar_prefetch=2, grid=(HEADS // hp, it.shape[0]),
            in_specs=[pair_q, pair_k, pair_kt, pair_k, pair_q,
                      pl.BlockSpec((t, wide), lambda p, n, it_, jt_: (jt_[n], OFF_Q // wide + p)),
                      pl.BlockSpec((t, wide), lambda p, n, it_, jt_: (jt_[n], OFF_K // wide + p)), gain, gain],
            out_specs=[tok3, gain, gain,
                       pl.BlockSpec((1, t, LANES), lambda p, n, it_, jt_: (p, jt_[n], 0))],
            scratch_shapes=[pltpu.VMEM((hp, nt, LANES, t), F32), pltpu.VMEM((hp, t, LANES), F32),
                            pltpu.VMEM((hp, t, LANES), F32), pltpu.VMEM((t, LANES), F32)]),
        out_shape=[jax.ShapeDtypeStruct((s, 3 * ATTN_W), BF16)]
        + [jax.ShapeDtypeStruct((1, ATTN_W), F32)] * 2
        + [jax.ShapeDtypeStruct((HEADS // hp, s, LANES), F32)],
        compiler_params=_params(("parallel", "arbitrary")),
    )(it, jt, qb, ka, kt, va, do, proj, proj, qg, kg)


def _forget_bwd(dcum, fl, bf_pad):
    s = fl.shape[0]
    tc = min(TC_CUM, s)
    n = s // tc

    def body(dc_ref, fl_ref, bf_ref, df_ref, dbf_ref, carry):
        @pl.when(pl.program_id(0) == 0)
        def _():
            carry[...] = jnp.zeros_like(carry)
            dbf_ref[...] = jnp.zeros_like(dbf_ref)
        r = lax.broadcasted_iota(jnp.int32, (tc, tc), 0)
        cidx = lax.broadcasted_iota(jnp.int32, (tc, tc), 1)
        tri = (cidx >= r).astype(F32)
        dc = dc_ref[0]
        for grp in range(1, dcum.shape[0]):
            dc = dc + dc_ref[grp]
        dlf = jnp.dot(tri, dc, preferred_element_type=F32, precision=lax.Precision.HIGHEST) + carry[...]
        carry[...] += jnp.sum(dc, axis=0, keepdims=True)
        lane = lax.broadcasted_iota(jnp.int32, (tc, LANES), 1)
        dfl = jnp.where(lane < HEADS, dlf * _sigmoid(-(fl_ref[...] + bf_ref[...])), 0.0)
        df_ref[...] = dfl.astype(BF16)
        dbf_ref[...] += jnp.sum(dfl, axis=0, keepdims=True)

    rev = pl.BlockSpec((tc, LANES), lambda i: (n - 1 - i, 0))
    vec = pl.BlockSpec((1, LANES), lambda i: (0, 0))
    return pl.pallas_call(
        body, name="forget_bwd", grid=(n,),
        in_specs=[pl.BlockSpec((dcum.shape[0], tc, LANES), lambda i: (0, n - 1 - i, 0)), rev, vec],
        out_specs=[rev, vec],
        out_shape=[jax.ShapeDtypeStruct((s, LANES), BF16), jax.ShapeDtypeStruct((1, LANES), F32)],
        scratch_shapes=[pltpu.VMEM((1, LANES), F32)],
        compiler_params=_params(("arbitrary",)),
    )(dcum, fl, bf_pad)


def _conv_bwd(dob, proj, conv_w):
    s = dob.shape[0]
    tm = min(TM_ELEM, s)
    wd = CONV_W
    tiles, befores, afters = _conv_specs(tm, s, wd)

    def body(dob_ref, dnext_ref, gb_ref, gc_ref, u_ref, zb_ref, gch_ref, uh_ref, gbn_ref, zbn_ref, w_ref,
             out_ref, dw_ref):
        i = pl.program_id(1)

        @pl.when(i == 0)
        def _():
            dw_ref[...] = jnp.zeros_like(dw_ref)
        gb, gc, u, zb, cu, r1, r2, conv = _conv_parts(gb_ref, gc_ref, u_ref, zb_ref, gch_ref, uh_ref, i == 0, w_ref, tm)
        g = dob_ref[...]
        sg = _sigmoid(zb)
        sz = zb * sg
        dconv = g * gb * sz
        zn = zbn_ref[0:8, :].astype(F32)
        dcn = jnp.where(i == pl.num_programs(1) - 1, 0.0,
                        dnext_ref[...] * gbn_ref[0:8, :].astype(F32) * (zn * _sigmoid(zn)))
        nxt1, nxt2 = _sub_row(dcn, 0), _sub_row(dcn, 1)
        row = lax.broadcasted_iota(jnp.int32, (tm, wd), 0)
        f1 = jnp.where(row == tm - 1, nxt1, pltpu.roll(dconv, tm - 1, 0))
        f2 = jnp.where(row == tm - 2, nxt1, jnp.where(row == tm - 1, nxt2, pltpu.roll(dconv, tm - 2, 0)))
        dcu = w_ref[2:3, :] * dconv + w_ref[1:2, :] * f1 + w_ref[0:1, :] * f2
        out_ref[:, 0:wd] = (g * conv * sz).astype(BF16)
        out_ref[:, wd:2 * wd] = (dcu * u).astype(BF16)
        out_ref[:, 2 * wd:3 * wd] = (dcu * gc).astype(BF16)
        out_ref[:, 3 * wd:4 * wd] = (g * gb * conv * (sg * (1.0 + zb * (1.0 - sg)))).astype(BF16)
        w_row = lax.broadcasted_iota(jnp.int32, (3, wd), 0)
        dw0 = jnp.sum(dconv * r2, axis=0, keepdims=True)
        dw1 = jnp.sum(dconv * r1, axis=0, keepdims=True)
        dw2 = jnp.sum(dconv * cu, axis=0, keepdims=True)
        dw_ref[...] += jnp.where(w_row == 0, dw0, jnp.where(w_row == 1, dw1, dw2))

    blk = pl.BlockSpec((tm, wd), lambda c, i: (i, c))
    nxt = pl.BlockSpec((8, wd), lambda c, i: (jnp.minimum((i + 1) * (tm // 8), s // 8 - 1), c))
    wspec = pl.BlockSpec((3, wd), lambda c, i: (0, c))
    return pl.pallas_call(
        body, name="conv_bwd", grid=(CONV_W // wd, s // tm),
        in_specs=[blk, nxt] + tiles + befores + afters + [wspec],
        out_specs=[pl.BlockSpec((tm, 4 * wd), lambda c, i: (i, c)), wspec],
        out_shape=[jax.ShapeDtypeStruct((s, 4 * CONV_W), BF16), jax.ShapeDtypeStruct((3, CONV_W), F32)],
        compiler_params=_params(("parallel", "arbitrary")),
    )(dob, dob, *([proj] * 8), conv_w)


def _piece_layout(pieces):
    offs, off = [], 0
    for p in pieces:
        offs.append((off, p.shape[1]))
        off += p.shape[1]
    assert off == N_ALL, off
    return offs


def _dw_in(h, pieces, chip_sums):
    s = h.shape[0]
    tk, tn = min(TK_DW, s), TN_DW
    nk = s // tk
    nn = N_MAIN // tn
    main, fpiece = pieces[:-1], pieces[-1]
    layout = _piece_layout(pieces)[:-1]
    n_main = len(main)
    nx = len(chip_sums)

    def body(*refs):
        p_refs, f_ref, h_ref = refs[:n_main], refs[n_main], refs[n_main + 1]
        ins, refs = refs[n_main + 2:n_main + 2 + nx], refs[n_main + 2 + nx:]
        out_ref, outf_ref = refs[:2]
        outs, (acc, accf, send_sems, recv_sems, local_sems) = refs[2:2 + nx], refs[2 + nx:]
        n, k = pl.program_id(0), pl.program_id(1)

        @pl.when(jnp.logical_and(n == 0, k == 0))
        def _():
            for cp in _all_to_all_copies(ins, outs, send_sems, recv_sems, local_sems):
                cp.start()

        @pl.when(k == 0)
        def _():
            acc[...] = jnp.zeros_like(acc)
        hv = h_ref[pl.ds(pl.multiple_of(k * tk, tk), tk), :]
        for p_ref, (off, width) in zip(p_refs, layout):
            @pl.when(jnp.logical_and(n >= off // tn, n < (off + width) // tn))
            def _():
                acc[...] += _dot_tn(p_ref[...], hv)

        @pl.when(k == nk - 1)
        def _():
            out_ref[...] = acc[...].astype(BF16)

        @pl.when(n == 0)
        def _():
            @pl.when(k == 0)
            def _():
                accf[...] = jnp.zeros_like(accf)
            accf[...] += _dot_tn(f_ref[...], hv)

            @pl.when(k == nk - 1)
            def _():
                outf_ref[...] = accf[...].astype(BF16)

        @pl.when(jnp.logical_and(n == nn - 1, k == nk - 1))
        def _():
            for cp in _all_to_all_copies(ins, outs, send_sems, recv_sems, local_sems):
                cp.wait()

    def piece_spec(off, width):
        lo, hi = off // tn, (off + width) // tn

        def index(n, k):
            active = jnp.logical_and(n >= lo, n < hi)
            return jnp.where(active, k, 0), jnp.clip(n - lo, 0, hi - lo - 1)
        return pl.BlockSpec((tk, tn), index)

    any_spec = pl.BlockSpec(memory_space=pl.ANY)
    res = pl.pallas_call(
        body, name="dw_in", grid=(nn, nk),
        in_specs=[piece_spec(off, width) for off, width in layout]
        + [pl.BlockSpec((tk, N_FPAD), lambda n, k: (jnp.where(n == 0, k, 0), 0)),
           pl.BlockSpec((s, D_MODEL), lambda n, k: (0, 0))] + [any_spec] * nx,
        out_specs=[pl.BlockSpec((tn, D_MODEL), lambda n, k: (n, 0)),
                   pl.BlockSpec((N_FPAD, D_MODEL), lambda n, k: (0, 0))] + [any_spec] * nx,
        out_shape=[jax.ShapeDtypeStruct((N_MAIN, D_MODEL), BF16), jax.ShapeDtypeStruct((N_FPAD, D_MODEL), BF16)]
        + [jax.ShapeDtypeStruct(a.shape, a.dtype) for a in chip_sums],
        scratch_shapes=[pltpu.VMEM((tn, D_MODEL), F32), pltpu.VMEM((N_FPAD, D_MODEL), F32)] + _gather_sems(nx),
        compiler_params=_params(("arbitrary", "arbitrary")),
    )(*main, fpiece, h, *chip_sums)
    return res[:2], res[2:]


def _dh_and_dx(pieces, w_all_t, x, dy, ada3, norm_g, chip_sums):
    s = x.shape[0]
    tm = min(TM_DH, s)
    nt = s // tm
    n = len(chip_sums)
    npc = len(pieces)
    layout = _piece_layout(pieces)

    def body(*refs):
        p_refs, refs = refs[:npc], refs[npc:]
        wt_ref, x_ref, dy_ref, ada_ref, g_ref = refs[:5]
        ins, refs = refs[5:5 + n], refs[5 + n:]
        gx_ref, dsh_ref, dsc_ref, dg_ref = refs[:4]
        outs, (send_sems, recv_sems, local_sems) = refs[4:4 + n], refs[4 + n:]
        i = pl.program_id(0)

        @pl.when(i == 0)
        def _():
            for cp in _chip_copies(ins, outs, send_sems, recv_sems, local_sems):
                cp.start()
            dsh_ref[...] = jnp.zeros_like(dsh_ref)
            dsc_ref[...] = jnp.zeros_like(dsc_ref)
            dg_ref[...] = jnp.zeros_like(dg_ref)

        dh = None
        for p_ref, (off, width) in zip(p_refs, layout):
            part = _dot(p_ref[...], wt_ref[off:off + width, :])
            dh = part if dh is None else dh + part
        xv = x_ref[...]
        r = lax.rsqrt(jnp.mean(xv * xv, axis=-1, keepdims=True) + EPS)
        xhat = xv * r
        g = g_ref[...]
        one_sc = 1.0 + ada_ref[1:2, :]
        dsh_ref[...] += jnp.sum(dh, axis=0, keepdims=True)
        dsc_ref[...] += jnp.sum(dh * (xhat * g), axis=0, keepdims=True)
        dg_ref[...] += jnp.sum(dh * xhat, axis=0, keepdims=True) * one_sc
        dxh = dh * (g * one_sc)
        dx = r * (dxh - xhat * jnp.mean(dxh * xhat, axis=-1, keepdims=True))
        gx_ref[...] = dy_ref[...] + dx

        @pl.when(i == nt - 1)
        def _():
            for cp in _chip_copies(ins, outs, send_sems, recv_sems, local_sems):
                cp.wait()

    full = pl.BlockSpec((tm, D_MODEL), lambda i: (i, 0))
    vec = pl.BlockSpec((1, D_MODEL), lambda i: (0, 0))
    any_spec = pl.BlockSpec(memory_space=pl.ANY)
    res = pl.pallas_call(
        body, name="dh_dx", grid=(nt,),
        in_specs=[pl.BlockSpec((tm, p.shape[1]), lambda i: (i, 0)) for p in pieces]
        + [pl.BlockSpec((N_ALL, D_MODEL), lambda i: (0, 0)), full, full,
           pl.BlockSpec((3, D_MODEL), lambda i: (0, 0)), vec] + [any_spec] * n,
        out_specs=[full, vec, vec, vec] + [any_spec] * n,
        out_shape=[jax.ShapeDtypeStruct((s, D_MODEL), F32)] + [jax.ShapeDtypeStruct((1, D_MODEL), F32)] * 3
        + [jax.ShapeDtypeStruct(a.shape, a.dtype) for a in chip_sums],
        scratch_shapes=[pltpu.SemaphoreType.DMA((n * 3,)), pltpu.SemaphoreType.DMA((n * 3,)),
                        pltpu.SemaphoreType.DMA((n,))],
        compiler_params=_params(("arbitrary",)),
    )(*pieces, w_all_t, x, dy, ada3, norm_g, *chip_sums)
    return res[:4], res[4:]


def _sum_small(vec_all, qg_parts, kg_parts):
    def body(v_ref, q_ref, k_ref, tot_ref, gq_ref, gk_ref):
        tot = v_ref[0:1, :]
        for p in range(1, N_DEV):
            tot = tot + v_ref[p:p + 1, :]
        tot_ref[...] = tot
        gq_ref[...] = jnp.sum(q_ref[...], axis=0, keepdims=True)
        gk_ref[...] = jnp.sum(k_ref[...], axis=0, keepdims=True)

    n = vec_all.shape[-1]
    return pl.pallas_call(
        body, name="sum_small",
        out_shape=[jax.ShapeDtypeStruct((1, n), F32),
                   jax.ShapeDtypeStruct((1, HEAD_DIM), F32), jax.ShapeDtypeStruct((1, HEAD_DIM), F32)],
        compiler_params=_params(),
    )(vec_all, qg_parts, kg_parts)


def _grad_w_ada(c_cols, dada_rows):
    def body(c_ref, d_ref, out_ref):
        acc = c_ref[0] * d_ref[0]
        for b in range(1, N_DEV):
            acc = acc + c_ref[b] * d_ref[b]
        out_ref[...] = acc

    return pl.pallas_call(
        body, name="grad_w_ada",
        out_shape=jax.ShapeDtypeStruct((D_MODEL, ADA_SHARD), F32),
        compiler_params=_params(),
    )(c_cols, dada_rows)


def _adam_step(w, m, v, g):
    c1 = 1.0 / (1.0 - ADAM_B1 ** ADAM_STEP)
    c2 = 1.0 / (1.0 - ADAM_B2 ** ADAM_STEP)
    m_new = ADAM_B1 * m + (1.0 - ADAM_B1) * g
    v_new = ADAM_B2 * v + (1.0 - ADAM_B2) * (g * g)
    return -ADAM_LR * ((m_new * c1) / (jnp.sqrt(v_new * c2) + ADAM_EPS) + ADAM_WD * w), m_new, v_new


def _adamw_small(params, name):
    n = len(params)
    stacked = [p[3].ndim == p[0].ndim + 1 for p in params]

    def body(*refs):
        ins, outs = refs[:4 * n], refs[4 * n:]
        for k in range(n):
            w_ref, m_ref, v_ref, g_ref = ins[4 * k:4 * k + 4]
            go_ref, d_ref, mo_ref, vo_ref = outs[4 * k:4 * k + 4]
            if stacked[k]:
                g = g_ref[0].astype(F32)
                for p in range(1, g_ref.shape[0]):
                    g = g + g_ref[p].astype(F32)
            else:
                g = g_ref[...]
            go_ref[...] = g
            d_ref[...], mo_ref[...], vo_ref[...] = _adam_step(w_ref[...], m_ref[...], v_ref[...], g)

    res = pl.pallas_call(
        body, name=name,
        out_shape=[jax.ShapeDtypeStruct(p[0].shape, F32) for p in params for _ in range(4)],
        compiler_params=_params(),
    )(*[a for p in params for a in p])
    return [tuple(res[4 * k:4 * k + 4]) for k in range(n)]


def _adamw(w, m, v, g_parts, name):
    rows, cols = w.shape
    n_parts = g_parts.shape[0]
    tr = 256 if rows % 256 == 0 else rows
    tc = 256 if (tr == rows and rows > 256 and cols % 256 == 0) else cols

    def body(w_ref, m_ref, v_ref, g_ref, go_ref, d_ref, mo_ref, vo_ref):
        g = g_ref[0].astype(F32)
        for p in range(1, n_parts):
            g = g + g_ref[p].astype(F32)
        go_ref[...] = g
        d_ref[...], mo_ref[...], vo_ref[...] = _adam_step(w_ref[...], m_ref[...], v_ref[...], g)

    blk = pl.BlockSpec((tr, tc), lambda i, j: (i, j))
    return pl.pallas_call(
        body, name=name, grid=(rows // tr, cols // tc),
        in_specs=[blk, blk, blk, pl.BlockSpec((n_parts, tr, tc), lambda i, j: (0, i, j))],
        out_specs=[blk] * 4,
        out_shape=[jax.ShapeDtypeStruct((rows, cols), F32)] * 4,
        compiler_params=_params(("parallel", "parallel")),
    )(w, m, v, g_parts)


_O_F = 1536


W_TILE = 16
WIN_ROWS = 784


def _internal_start(p):
    return p * IN_SHARD - (HEADS if p * IN_SHARD > _O_F else 0)


def _shard_window(wt_shard, me):
    o = me * IN_SHARD + lax.broadcasted_iota(jnp.int32, (IN_SHARD, 1), 0)
    is_f = jnp.logical_and(o >= _O_F, o < _O_F + HEADS)
    start = me * IN_SHARD - jnp.where(me * IN_SHARD > _O_F, HEADS, 0)
    main = lax.dynamic_update_slice(jnp.zeros((WIN_ROWS, D_MODEL), BF16),
                                    jnp.where(is_f, 0.0, wt_shard).astype(BF16), (start % W_TILE, 0))
    f_rows = jnp.pad(jnp.where(is_f, wt_shard, 0.0).astype(BF16), ((W_TILE, W_TILE), (0, 0)))
    f_start = jnp.clip(W_TILE + _O_F - me * IN_SHARD, 0, IN_SHARD + W_TILE)
    return jnp.concatenate([main, lax.dynamic_slice(f_rows, (f_start, 0), (W_TILE, D_MODEL))], axis=0)


def _assemble_w(windows):
    chunk = 112

    def body(g_ref, out_ref):
        out_ref[WIN_ROWS:N_MAIN, :] = jnp.zeros((N_MAIN - WIN_ROWS, D_MODEL), BF16)
        for p in range(N_DEV):
            base = _internal_start(p) // W_TILE * W_TILE
            for r in range(0, WIN_ROWS, chunk):
                rows = slice(base + r, base + r + chunk)
                piece = g_ref[p, r:r + chunk, :]
                out_ref[rows, :] = piece if p == 0 else out_ref[rows, :] + piece
        f = g_ref[0, WIN_ROWS:WIN_ROWS + W_TILE, :]
        for p in range(1, N_DEV):
            f = f + g_ref[p, WIN_ROWS:WIN_ROWS + W_TILE, :]
        out_ref[N_MAIN:N_MAIN + W_TILE, :] = f
        out_ref[N_MAIN + W_TILE:N_ALL, :] = jnp.zeros((N_FPAD - W_TILE, D_MODEL), BF16)

    return pl.pallas_call(
        body, name="assemble_w", out_shape=jax.ShapeDtypeStruct((N_ALL, D_MODEL), BF16),
        compiler_params=_params(),
    )(windows)


def _slabs_by_core(dwt, dwt_f):
    sources = ((dwt, 0, _O_F, 0), (dwt_f, _O_F, _O_F + HEADS, _O_F), (dwt, _O_F + HEADS, IN_WIDTH, HEADS))

    def slab(p):
        lo, hi = p * IN_SHARD, (p + 1) * IN_SHARD
        parts = []
        for src, o_lo, o_hi, shift in sources:
            a, b = max(lo, o_lo), min(hi, o_hi)
            if a < b:
                parts.append(src[a - shift:b - shift])
        return parts[0] if len(parts) == 1 else jnp.concatenate(parts, axis=0)

    return jnp.stack([jnp.stack([slab(2 * chip + core) for chip in range(4)]) for core in range(2)])


def kernel(x, c, w_ada, b_ada, norm_g, w_in, b_f, q_norm_g, k_norm_g, conv_w, w_attn_out, w_conv_out, w_o, loss_target, m_w_ada, m_b_ada, m_norm_g, m_w_in, m_b_f, m_q_norm_g, m_k_norm_g, m_conv_w, m_w_attn_out, m_w_conv_out, m_w_o, v_w_ada, v_b_ada, v_norm_g, v_w_in, v_b_f, v_q_norm_g, v_k_norm_g, v_conv_w, v_w_attn_out, v_w_conv_out, v_w_o):
    me = 4 * lax.axis_index("x") + 2 * lax.axis_index("y") + lax.axis_index("c")
    s = x.shape[1]
    x2, t2 = x[0], loss_target[0]

    w_in_g, c_all, ada_g = _gather_weights_and_ada(_shard_window(w_in[0].T, me), c, w_ada[0])
    ada_mine = lax.dynamic_index_in_dim(ada_g[:, :, 0, :], me, axis=1, keepdims=False)
    ada3 = (ada_mine.reshape(1, 3 * D_MODEL) + b_ada).reshape(3, D_MODEL)
    w_all_t = _assemble_w(w_in_g)
    qg = jnp.tile(q_norm_g, (1, HEADS))
    kg = jnp.tile(k_norm_g, (1, HEADS))
    bf_pad = jnp.pad(b_f, ((0, 0), (0, LANES - HEADS)))

    (proj, fl, h), (qa, ka, va, kt, vt), (cw_g, wa_g, wb_g, wo_g) = _proj_fwd(
        x2, ada3, norm_g, w_all_t, bf_pad, qg, kg,
        [conv_w[0], w_attn_out[0].astype(BF16), w_conv_out[0].astype(BF16), w_o[0].astype(BF16)])
    wa = jnp.transpose(wa_g, (1, 0, 2)).reshape(ATTN_W, D_MODEL)
    wb = jnp.transpose(wb_g, (1, 0, 2)).reshape(CONV_W, D_MODEL)
    wo = wo_g.reshape(D_MODEL, D_MODEL)
    cw = jnp.transpose(cw_g, (1, 0, 2)).reshape(3, CONV_W)
    attn, oa, qb = _attn_fwd(qa, ka, vt, proj)
    (dy, dgab, do, dza, dob, dwo, dwa, dwb, dgate, loss_part) = _tail(oa, attn, proj, x2, t2, ada3, wa, wb, wo, cw)

    core = lax.axis_index("c").astype(jnp.int32).reshape(1)
    small = [jnp.transpose(dwa.reshape(ATTN_W, N_DEV, LANES), (1, 0, 2)).astype(BF16),
             jnp.transpose(dwb.reshape(CONV_W, N_DEV, LANES), (1, 0, 2)).astype(BF16),
             dwo.reshape(N_DEV, D_MODEL // N_DEV, D_MODEL).astype(BF16)]
    dqkv, dqg, dkg, dcum = _attn_bwd(qb, ka, kt, va, do, proj, qg, kg)
    df, dbf = _forget_bwd(dcum, fl, bf_pad)
    dconv, dcw = _conv_bwd(dob, proj, cw)
    pieces = [dqkv, dza, dconv, dgab, df]
    (dw_main, dw_f), (g_wa_parts, g_wb_parts, g_wo_parts) = _dw_in(h, pieces, small)

    slabs_in = _slabs_by_core(dw_main, dw_f)
    (theirs_in,) = _sibling_swap([slabs_in], "swap_w_in")
    (grad_x, dshift, dscale, dnormg), (g_in_parts,) = _dh_and_dx(
        pieces, w_all_t, x2, dy, ada3, norm_g, [_pair_sum(slabs_in, theirs_in, core, "pair_sum_w_in")])
    vec = jnp.concatenate([dshift, dscale, dgate, dnormg, dbf, dcw.reshape(1, 3 * CONV_W), loss_part, dqg, dkg],
                          axis=1)
    (vec_all,) = _gather_direct([vec], "gather_small")
    vec_all = vec_all.reshape(N_DEV, vec.shape[1])
    n_main = 4 * D_MODEL + LANES + 3 * CONV_W + LANES
    tot, g_qg, g_kg = _sum_small(
        vec_all[:, :n_main],
        vec_all[:, n_main:n_main + ATTN_W].reshape(N_DEV * HEADS, HEAD_DIM),
        vec_all[:, n_main + ATTN_W:].reshape(N_DEV * HEADS, HEAD_DIM))
    g_b_ada = tot[:, 0:3 * D_MODEL]
    g_norm_g = tot[:, 3 * D_MODEL:4 * D_MODEL]
    g_b_f = tot[:, 4 * D_MODEL:4 * D_MODEL + HEADS]
    g_cw_full = tot[:, 4 * D_MODEL + LANES:4 * D_MODEL + LANES + 3 * CONV_W].reshape(3, CONV_W)
    g_cw = lax.dynamic_slice(g_cw_full, (0, me * (CONV_W // N_DEV)), (3, CONV_W // N_DEV))
    dada_mine = lax.dynamic_slice(vec_all[:, 0:3 * D_MODEL], (0, me * ADA_SHARD), (N_DEV, ADA_SHARD))
    g_w_ada = _grad_w_ada(jnp.transpose(c_all, (0, 2, 1)), dada_mine.reshape(N_DEV, 1, ADA_SHARD))

    upd = {}
    upd["w_ada"] = _adamw(w_ada[0], m_w_ada[0], v_w_ada[0], g_w_ada[None], "adamw_w_ada")
    upd["w_in"] = [u.T for u in _adamw(w_in[0].T, m_w_in[0].T, v_w_in[0].T, g_in_parts, "adamw_w_in")]
    small_names = ["b_ada", "norm_g", "b_f", "q_norm_g", "k_norm_g", "conv_w", "w_attn_out", "w_conv_out", "w_o"]
    small_upd = _adamw_small(
        [(b_ada, m_b_ada, v_b_ada, g_b_ada), (norm_g, m_norm_g, v_norm_g, g_norm_g), (b_f, m_b_f, v_b_f, g_b_f),
         (q_norm_g, m_q_norm_g, v_q_norm_g, g_qg), (k_norm_g, m_k_norm_g, v_k_norm_g, g_kg),
         (conv_w[0], m_conv_w[0], v_conv_w[0], g_cw),
         (w_attn_out[0], m_w_attn_out[0], v_w_attn_out[0], g_wa_parts),
         (w_conv_out[0], m_w_conv_out[0], v_w_conv_out[0], g_wb_parts),
         (w_o[0], m_w_o[0], v_w_o[0], g_wo_parts)], "adamw_small")
    upd.update(zip(small_names, small_upd))

    names = ["w_ada", "b_ada", "norm_g", "w_in", "b_f", "q_norm_g", "k_norm_g", "conv_w",
             "w_attn_out", "w_conv_out", "w_o"]
    lead = {"w_ada", "w_in", "conv_w", "w_attn_out", "w_conv_out", "w_o"}
    fix = lambda n, a: a[None] if n in lead else a
    loss = tot[0, n_main - LANES]
    outs = [loss, grad_x[None]]
    for k in range(4):
        outs += [fix(n, upd[n][k]) for n in names]
    return tuple(outs)
```

```python
import numpy as np
import jax
import jax.numpy as jnp
from jax import lax
from jax.experimental import pallas as pl
from jax.experimental.pallas import tpu as pltpu

F32 = jnp.float32
BF16 = jnp.bfloat16

D_MODEL = 1024
HEADS = 8
HEAD_DIM = 64
ATTN_W = 512
CONV_W = 512
N_DEV = 8
IN_WIDTH = 6152
IN_SHARD = IN_WIDTH // N_DEV
N_MAIN = 6144
N_FPAD = 128
N_ALL = N_MAIN + N_FPAD
ADA_SHARD = 3 * D_MODEL // N_DEV
EPS = 1e-6
NEG = -1e30

ADAM_LR = 0.001
ADAM_B1 = 0.9
ADAM_B2 = 0.999
ADAM_EPS = 1e-08
ADAM_WD = 0.01
ADAM_STEP = 10

LANES = 128
VMEM_LIMIT = 56 * 1024 * 1024

TM_PROJ = 512
TN_PROJ = 1024
TM_ELEM = 512
TQ = 512
HEADS_PER_STEP = 8
TM_TAIL = 256
TC_CUM = 512
TK_DW = 2048
TN_DW = 512
TM_DH = 256
HALO = 16

OFF_Q, OFF_K, OFF_V, OFF_ZA, OFF_CB, OFF_CC, OFF_CU, OFF_CZ, OFF_GA, OFF_GB = (
    0, 512, 1024, 1536, 2048, 2560, 3072, 3584, 4096, 5120)


def _params(sem=None):
    return pltpu.CompilerParams(dimension_semantics=sem, vmem_limit_bytes=VMEM_LIMIT)


def _dot(a, b):
    return jnp.dot(a, b, preferred_element_type=F32)


def _dot_nt(a, b):
    return lax.dot_general(a, b, (((1,), (1,)), ((), ())), preferred_element_type=F32)


def _dot_tn(a, b):
    return lax.dot_general(a, b, (((0,), (0,)), ((), ())), preferred_element_type=F32)


def _sigmoid(x):
    return 1.0 / (1.0 + jnp.exp(-x))


def _seg_sum(z, lo):
    a = jnp.sum(jnp.where(lo, z, 0.0), axis=-1, keepdims=True)
    b = jnp.sum(jnp.where(lo, 0.0, z), axis=-1, keepdims=True)
    return jnp.where(lo, a, b)


def _lane_col(z, lane):
    idx = lax.broadcasted_iota(jnp.int32, z.shape, 1)
    return jnp.sum(jnp.where(idx == lane, z, 0.0), axis=-1, keepdims=True)


def _sub_row(z, row):
    idx = lax.broadcasted_iota(jnp.int32, z.shape, 0)
    return jnp.sum(jnp.where(idx == row, z, 0.0), axis=0, keepdims=True)


def _mesh_pos():
    x, y, c = lax.axis_index("x"), lax.axis_index("y"), lax.axis_index("c")
    return x, y, c, 4 * x + 2 * y + c


def _peer(k, x, y, c):
    px = 1 - x if (k >> 2) & 1 else x
    py = 1 - y if (k >> 1) & 1 else y
    pc = 1 - c if k & 1 else c
    return (px, py, pc), 4 * px + 2 * py + pc


def _gather_copies(ins, outs, send_sems, recv_sems, local_sems):
    x, y, c, me = _mesh_pos()
    copies = []
    for a in range(len(ins)):
        copies.append(pltpu.make_async_copy(ins[a], outs[a].at[me], local_sems.at[a]))
        for k in range(1, N_DEV):
            dev, _ = _peer(k, x, y, c)
            copies.append(pltpu.make_async_remote_copy(
                src_ref=ins[a], dst_ref=outs[a].at[me],
                send_sem=send_sems.at[a * (N_DEV - 1) + k - 1], recv_sem=recv_sems.at[a * (N_DEV - 1) + k - 1],
                device_id=dev, device_id_type=pl.DeviceIdType.MESH))
    return copies


def _gather_sems(n):
    return [pltpu.SemaphoreType.DMA((n * (N_DEV - 1),)), pltpu.SemaphoreType.DMA((n * (N_DEV - 1),)),
            pltpu.SemaphoreType.DMA((n,))]


def _gather_direct(arrs, name):
    n = len(arrs)
    any_spec = pl.BlockSpec(memory_space=pl.ANY)

    def body(*refs):
        copies = _gather_copies(refs[:n], refs[n:2 * n], *refs[2 * n:])
        for cp in copies:
            cp.start()
        for cp in copies:
            cp.wait()

    return pl.pallas_call(
        body, name=name, out_shape=[jax.ShapeDtypeStruct((N_DEV,) + a.shape, a.dtype) for a in arrs],
        in_specs=[any_spec] * n, out_specs=[any_spec] * n, scratch_shapes=_gather_sems(n),
    )(*arrs)


def _ada_phase(c_ref, w_ref, call_ref, adag_ref, mine_ref, send_sems, recv_sems):
    x, y, c, me = _mesh_pos()

    def copy(phase, k, src, dst):
        dev, _ = _peer(k, x, y, c)
        return pltpu.make_async_remote_copy(
            src_ref=src, dst_ref=dst,
            send_sem=send_sems.at[phase * (N_DEV - 1) + k - 1],
            recv_sem=recv_sems.at[phase * (N_DEV - 1) + k - 1],
            device_id=dev, device_id_type=pl.DeviceIdType.MESH)

    call_ref[me] = c_ref[...]
    first = [copy(0, k, c_ref, call_ref.at[me]) for k in range(1, N_DEV)]
    for cp in first:
        cp.start()
    for cp in first:
        cp.wait()
    wb = w_ref[...].astype(BF16)
    for b in range(N_DEV):
        row = jnp.broadcast_to(call_ref[b], (8, D_MODEL)).astype(BF16)
        mine_ref[b] = _sub_row(_dot(row, wb), 0)
    adag_ref[me] = mine_ref[...]
    second = [copy(1, k, mine_ref, adag_ref.at[me]) for k in range(1, N_DEV)]
    for cp in second:
        cp.start()
    for cp in second:
        cp.wait()


def _gather_weights_and_ada(wt_shard, c_row, w_ada_sh):
    any_spec = pl.BlockSpec(memory_space=pl.ANY)
    vm = pl.BlockSpec(memory_space=pltpu.VMEM)

    def body(w_in_ref, c_ref, wada_ref, out_ref, call_ref, adag_ref, mine_ref, send_sems, recv_sems, local_sem,
             ada_send, ada_recv):
        x, y, c, me = _mesh_pos()
        sibling = (x, y, 1 - c)
        chips = [(1 - x, y), (x, 1 - y), (1 - x, 1 - y)]

        def copy(k, src, blk, to):
            return pltpu.make_async_remote_copy(
                src_ref=src, dst_ref=out_ref.at[blk], send_sem=send_sems.at[k], recv_sem=recv_sems.at[k],
                device_id=to, device_id_type=pl.DeviceIdType.MESH)

        local = pltpu.make_async_copy(w_in_ref, out_ref.at[me], local_sem.at[0])
        local.start()
        first = [copy(0, w_in_ref, me, sibling)]
        first += [copy(1 + j, w_in_ref, me, (px, py, c)) for j, (px, py) in enumerate(chips)]
        for cp in first:
            cp.start()
        _ada_phase(c_ref, wada_ref, call_ref, adag_ref, mine_ref, ada_send, ada_recv)
        passed = []
        for j, (px, py) in enumerate(chips):
            blk = 4 * px + 2 * py + c
            copy(1 + j, w_in_ref, blk, (x, y, c)).wait_recv()
            fwd = copy(4 + j, out_ref.at[blk], blk, sibling)
            fwd.start()
            passed.append(fwd)
        copy(0, w_in_ref, 4 * x + 2 * y + 1 - c, (x, y, c)).wait_recv()
        for j, (px, py) in enumerate(chips):
            copy(4 + j, w_in_ref, 4 * px + 2 * py + 1 - c, (x, y, c)).wait_recv()
        for cp in first + passed:
            cp.wait_send()
        local.wait()

    per = N_DEV - 1
    return pl.pallas_call(
        body, name="gather_weights",
        out_shape=[jax.ShapeDtypeStruct((N_DEV,) + wt_shard.shape, wt_shard.dtype),
                   jax.ShapeDtypeStruct((N_DEV, 1, D_MODEL), F32),
                   jax.ShapeDtypeStruct((N_DEV, N_DEV, 1, ADA_SHARD), F32)],
        in_specs=[any_spec, vm, vm], out_specs=[any_spec, vm, vm],
        scratch_shapes=[pltpu.VMEM((N_DEV, 1, ADA_SHARD), F32),
                        pltpu.SemaphoreType.DMA((per,)), pltpu.SemaphoreType.DMA((per,)),
                        pltpu.SemaphoreType.DMA((1,)),
                        pltpu.SemaphoreType.DMA((2 * per,)), pltpu.SemaphoreType.DMA((2 * per,))],
        compiler_params=pltpu.CompilerParams(vmem_limit_bytes=VMEM_LIMIT),
    )(wt_shard, c_row, w_ada_sh)


def _sibling_swap(arrs, name):
    n = len(arrs)
    any_spec = pl.BlockSpec(memory_space=pl.ANY)

    def body(*refs):
        ins, outs = refs[:n], refs[n:2 * n]
        send_sems, recv_sems = refs[2 * n:]
        x, y, c, _ = _mesh_pos()
        copies = [pltpu.make_async_remote_copy(
            src_ref=ins[a].at[1 - c], dst_ref=outs[a], send_sem=send_sems.at[a], recv_sem=recv_sems.at[a],
            device_id=(x, y, 1 - c), device_id_type=pl.DeviceIdType.MESH) for a in range(n)]
        for cp in copies:
            cp.start()
        for cp in copies:
            cp.wait()

    return pl.pallas_call(
        body, name=name,
        out_shape=[jax.ShapeDtypeStruct(a.shape[1:], a.dtype) for a in arrs],
        in_specs=[any_spec] * n, out_specs=[any_spec] * n,
        scratch_shapes=[pltpu.SemaphoreType.DMA((n,)), pltpu.SemaphoreType.DMA((n,))],
    )(*arrs)


def _pair_sum(mine2, theirs, core, name):
    _, _, rows, cols = mine2.shape
    tr = 256 if rows % 256 == 0 else rows

    def body(core_ref, a_ref, b_ref, out_ref):
        out_ref[...] = (a_ref[...].astype(F32) + b_ref[...].astype(F32)).astype(BF16)

    return pl.pallas_call(
        body, name=name,
        grid_spec=pltpu.PrefetchScalarGridSpec(
            num_scalar_prefetch=1, grid=(4, rows // tr),
            in_specs=[pl.BlockSpec((None, None, tr, cols), lambda ch, i, core_: (core_[0], ch, i, 0)),
                      pl.BlockSpec((None, tr, cols), lambda ch, i, core_: (ch, i, 0))],
            out_specs=pl.BlockSpec((None, tr, cols), lambda ch, i, core_: (ch, i, 0))),
        out_shape=jax.ShapeDtypeStruct(theirs.shape, BF16),
        compiler_params=_params(("parallel", "parallel")),
    )(core, mine2, theirs)


def _all_to_all_copies(ins, outs, send_sems, recv_sems, local_sems):
    x, y, c, me = _mesh_pos()
    copies = []
    for a in range(len(ins)):
        copies.append(pltpu.make_async_copy(ins[a].at[me], outs[a].at[me], local_sems.at[a]))
        for k in range(1, N_DEV):
            dev, p = _peer(k, x, y, c)
            copies.append(pltpu.make_async_remote_copy(
                src_ref=ins[a].at[p], dst_ref=outs[a].at[me],
                send_sem=send_sems.at[a * (N_DEV - 1) + k - 1], recv_sem=recv_sems.at[a * (N_DEV - 1) + k - 1],
                device_id=dev, device_id_type=pl.DeviceIdType.MESH))
    return copies


def _chip_copies(ins, outs, send_sems, recv_sems, local_sems):
    x, y, c, _ = _mesh_pos()
    my_chip = 2 * x + y
    chips = [(1 - x, y), (x, 1 - y), (1 - x, 1 - y)]
    copies = []
    for a in range(len(ins)):
        copies.append(pltpu.make_async_copy(ins[a].at[my_chip], outs[a].at[my_chip], local_sems.at[a]))
        for j, (px, py) in enumerate(chips):
            copies.append(pltpu.make_async_remote_copy(
                src_ref=ins[a].at[2 * px + py], dst_ref=outs[a].at[my_chip],
                send_sem=send_sems.at[a * 3 + j], recv_sem=recv_sems.at[a * 3 + j],
                device_id=(px, py, c), device_id_type=pl.DeviceIdType.MESH))
    return copies


def _proj_fwd(x, ada3, norm_g, w_all_t, bf_pad, qg, kg, later):
    s = x.shape[0]
    tm, tn = min(TM_PROJ, s), TN_PROJ
    nt = s // tm
    n = len(later)

    def body(x_ref, ada_ref, g_ref, wt_ref, bf_ref, qg_ref, kg_ref, *rest):
        ins, (proj_ref, fl_ref, h_ref), rows_refs, rest = rest[:n], rest[n:n + 3], rest[n + 3:n + 8], rest[n + 8:]
        outs, (carry, send_sems, recv_sems, local_sems) = rest[:n], rest[n:]
        i = pl.program_id(0)

        @pl.when(i == 0)
        def _():
            carry[...] = jnp.zeros_like(carry)
            for cp in _gather_copies(ins, outs, send_sems, recv_sems, local_sems):
                cp.start()

        xv = x_ref[...]
        r = lax.rsqrt(jnp.mean(xv * xv, axis=-1, keepdims=True) + EPS)
        hv = ((xv * r) * g_ref[...]) * (1.0 + ada_ref[1:2, :]) + ada_ref[0:1, :]
        hb = hv.astype(BF16)
        h_ref[...] = hb
        fl = _dot_nt(hb, wt_ref[N_MAIN:N_ALL, :])
        fl_ref[...] = fl
        for j in range(N_MAIN // tn):
            proj_ref[:, j * tn:(j + 1) * tn] = _dot_nt(hb, wt_ref[j * tn:(j + 1) * tn, :]).astype(BF16)
        _attention_rows(proj_ref, fl, bf_ref, qg_ref, kg_ref, carry, *rows_refs)

        @pl.when(i == nt - 1)
        def _():
            for cp in _gather_copies(ins, outs, send_sems, recv_sems, local_sems):
                cp.wait()

    any_spec = pl.BlockSpec(memory_space=pl.ANY)
    heads = pl.BlockSpec((HEADS, tm, LANES), lambda i: (0, i, 0))
    heads_t = pl.BlockSpec((HEADS, LANES, tm), lambda i: (0, 0, i))
    vec = pl.BlockSpec((1, ATTN_W), lambda i: (0, 0))
    res = pl.pallas_call(
        body, name="proj_fwd", grid=(nt,),
        in_specs=[pl.BlockSpec((tm, D_MODEL), lambda i: (i, 0)),
                  pl.BlockSpec((3, D_MODEL), lambda i: (0, 0)),
                  pl.BlockSpec((1, D_MODEL), lambda i: (0, 0)),
                  pl.BlockSpec((N_ALL, D_MODEL), lambda i: (0, 0)),
                  pl.BlockSpec((1, LANES), lambda i: (0, 0)), vec, vec] + [any_spec] * n,
        out_specs=[pl.BlockSpec((tm, N_MAIN), lambda i: (i, 0)),
                   pl.BlockSpec((tm, N_FPAD), lambda i: (i, 0)),
                   pl.BlockSpec((tm, D_MODEL), lambda i: (i, 0)),
                   heads, heads, heads, heads_t, heads_t] + [any_spec] * n,
        out_shape=[jax.ShapeDtypeStruct((s, N_MAIN), BF16),
                   jax.ShapeDtypeStruct((s, N_FPAD), F32),
                   jax.ShapeDtypeStruct((s, D_MODEL), BF16)]
        + [jax.ShapeDtypeStruct((HEADS, s, LANES), BF16)] * 3
        + [jax.ShapeDtypeStruct((HEADS, LANES, s), BF16)] * 2
        + [jax.ShapeDtypeStruct((N_DEV,) + a.shape, a.dtype) for a in later],
        scratch_shapes=[pltpu.VMEM((1, LANES), F32)] + _gather_sems(n),
        compiler_params=_params(("arbitrary",)),
    )(x, ada3, norm_g, w_all_t, bf_pad, qg, kg, *later)
    return res[:3], res[3:8], res[8:]


L_ONE_Q, L_F_Q, L_LSE_Q, L_END = HEAD_DIM, HEAD_DIM + 3, HEAD_DIM + 6, HEAD_DIM + 9


def _split3(f):
    hi = f.astype(BF16).astype(F32)
    r = f - hi
    mid = r.astype(BF16).astype(F32)
    return hi, mid, r - mid


def _place3(lane, first, parts, otherwise):
    a, b, c = parts
    return jnp.where(lane == first, a, jnp.where(lane == first + 1, b, jnp.where(lane == first + 2, c, otherwise)))


def _log_forget(fl, bf):
    z = fl + bf
    lf = jnp.minimum(z, 0.0) - jnp.log1p(jnp.exp(-jnp.abs(z)))
    lane = lax.broadcasted_iota(jnp.int32, z.shape, 1)
    return jnp.where(lane < HEADS, lf, 0.0)


def _attention_rows(p_ref, fl, bf_ref, qg_ref, kg_ref, carry, qa_ref, ka_ref, va_ref, kt_ref, vt_ref):
    tm = fl.shape[0]
    scale = HEAD_DIM ** -0.5
    tri = (lax.broadcasted_iota(jnp.int32, (tm, tm), 1) <= lax.broadcasted_iota(jnp.int32, (tm, tm), 0)).astype(F32)
    cum_v = jnp.dot(tri, _log_forget(fl, bf_ref[...]), preferred_element_type=F32,
                    precision=lax.Precision.HIGHEST) + carry[...]
    carry[...] = _sub_row(cum_v, tm - 1)
    lane = lax.broadcasted_iota(jnp.int32, (tm, LANES), 1)
    lo = lane < HEAD_DIM
    v_tail = jnp.where(lane < L_F_Q, 1.0, 0.0)
    for pr in range(ATTN_W // LANES):
        sl = slice(pr * LANES, (pr + 1) * LANES)
        q2 = p_ref[:, OFF_Q + pr * LANES:OFF_Q + (pr + 1) * LANES].astype(F32)
        k2 = p_ref[:, OFF_K + pr * LANES:OFF_K + (pr + 1) * LANES].astype(F32)
        v2 = p_ref[:, OFF_V + pr * LANES:OFF_V + (pr + 1) * LANES].astype(F32)
        rq = lax.rsqrt(_seg_sum(q2 * q2, lo) * (1.0 / HEAD_DIM) + EPS)
        rk = lax.rsqrt(_seg_sum(k2 * k2, lo) * (1.0 / HEAD_DIM) + EPS)
        qn = ((q2 * rq) * qg_ref[:, sl]) * scale
        kn = (k2 * rk) * kg_ref[:, sl]
        for hh in range(2):
            h = 2 * pr + hh
            f3 = _split3(_lane_col(cum_v, h))
            qh = qn if hh == 0 else pltpu.roll(qn, HEAD_DIM, 1)
            kh = kn if hh == 0 else pltpu.roll(kn, HEAD_DIM, 1)
            vh = v2 if hh == 0 else pltpu.roll(v2, HEAD_DIM, 1)
            q_tail = jnp.where(lane < L_F_Q, 1.0, _place3(lane, L_F_Q, f3, 0.0))
            k_tail = _place3(lane, L_ONE_Q, tuple(-f for f in f3), jnp.where(lane < L_END, 1.0, 0.0))
            k_row = jnp.where(lo, kh, k_tail)
            v_row = jnp.where(lo, vh, v_tail)
            qa_ref[h] = jnp.where(lo, qh, q_tail).astype(BF16)
            ka_ref[h] = k_row.astype(BF16)
            va_ref[h] = v_row.astype(BF16)
            kt_ref[h] = k_row.T.astype(BF16)
            vt_ref[h] = v_row.T.astype(BF16)


def _causal_t(t):
    return lax.broadcasted_iota(jnp.int32, (t, t), 0) <= lax.broadcasted_iota(jnp.int32, (t, t), 1)


def _tri_steps(nt, q_major):
    if q_major:
        pairs = [(i, j) for i in range(nt) for j in range(i + 1)]
    else:
        pairs = [(i, j) for j in range(nt) for i in range(j, nt)]
    return (jnp.asarray(np.array([p[0] for p in pairs], np.int32)),
            jnp.asarray(np.array([p[1] for p in pairs], np.int32)))


def _attn_fwd(qa, ka, vt, proj):
    s = qa.shape[1]
    t = min(TQ, s)
    it, jt = _tri_steps(s // t, True)
    hp = HEADS_PER_STEP
    wide = hp * HEAD_DIM
    za_blk = OFF_ZA // wide

    def body(it_ref, jt_ref, q_ref, k_ref, vt_ref, za_ref, attn_ref, oa_ref, qb_ref, m_s, acc_s, pair_s):
        step = pl.program_id(1)
        i, j = it_ref[step], jt_ref[step]

        @pl.when(j == 0)
        def _():
            m_s[...] = jnp.full_like(m_s, NEG)
            acc_s[...] = jnp.zeros_like(acc_s)

        def update(masked):
            for hh in range(hp):
                st = _dot_nt(k_ref[hh], q_ref[hh])
                if masked:
                    st = jnp.where(_causal_t(t), st, NEG)
                m_prev = m_s[hh]
                m_next = jnp.maximum(m_prev, jnp.max(st, axis=0, keepdims=True))
                alpha = jnp.exp(m_prev - m_next)
                pt = jnp.exp(st - m_next).astype(BF16)
                acc_s[hh] = acc_s[hh] * alpha + _dot(vt_ref[hh], pt)
                m_s[hh] = m_next

        @pl.when(j < i)
        def _():
            update(False)

        @pl.when(j == i)
        def _():
            update(True)
            row = lax.broadcasted_iota(jnp.int32, (LANES, t), 0)
            lane = lax.broadcasted_iota(jnp.int32, (t, LANES), 1)
            for hh in range(hp):
                l_row = acc_s[hh, L_ONE_Q:L_ONE_Q + 1, :]
                pair_s[hh * HEAD_DIM:(hh + 1) * HEAD_DIM, :] = acc_s[hh, 0:HEAD_DIM, :] / l_row
                lse3 = _split3(m_s[hh] + jnp.log(l_row))
                tail_t = _place3(row, L_LSE_Q, tuple(-x for x in lse3), 0.0)
                keep_q = jnp.logical_or(lane < L_LSE_Q, lane >= L_END)
                qb_ref[hh] = jnp.where(keep_q, q_ref[hh].astype(F32), tail_t.T).astype(BF16)
            out = pair_s[...].T
            attn_ref[...] = out
            z = za_ref[...].astype(F32)
            oa_ref[...] = (out * (z * _sigmoid(z))).astype(BF16)

    pair_q = pl.BlockSpec((hp, t, LANES), lambda p, n, it_, jt_: (p, it_[n], 0))
    pair_k = pl.BlockSpec((hp, t, LANES), lambda p, n, it_, jt_: (p, jt_[n], 0))
    pair_kt = pl.BlockSpec((hp, LANES, t), lambda p, n, it_, jt_: (p, 0, jt_[n]))
    out_q = pl.BlockSpec((t, wide), lambda p, n, it_, jt_: (it_[n], p))
    return pl.pallas_call(
        body, name="attn_fwd",
        grid_spec=pltpu.PrefetchScalarGridSpec(
            num_scalar_prefetch=2, grid=(HEADS // hp, it.shape[0]),
            in_specs=[pair_q, pair_k, pair_kt,
                      pl.BlockSpec((t, wide), lambda p, n, it_, jt_: (it_[n], za_blk + p))],
            out_specs=[out_q, out_q, pair_q],
            scratch_shapes=[pltpu.VMEM((hp, 1, t), F32), pltpu.VMEM((hp, LANES, t), F32),
                            pltpu.VMEM((wide, t), F32)]),
        out_shape=[jax.ShapeDtypeStruct((s, ATTN_W), F32),
                   jax.ShapeDtypeStruct((s, ATTN_W), BF16),
                   jax.ShapeDtypeStruct((HEADS, s, LANES), BF16)],
        compiler_params=_params(("parallel", "arbitrary")),
    )(it, jt, qa, ka, vt, proj)


def _conv_parts(gb_ref, gc_ref, u_ref, zb_ref, gch_ref, uh_ref, first, w_ref, tm):
    gb, gc = gb_ref[...].astype(F32), gc_ref[...].astype(F32)
    u, zb = u_ref[...].astype(F32), zb_ref[...].astype(F32)
    cu = gc * u
    cu_h = jnp.where(first, 0.0, gch_ref[...].astype(F32) * uh_ref[...].astype(F32))
    prev1, prev2 = _sub_row(cu_h, HALO - 1), _sub_row(cu_h, HALO - 2)
    row = lax.broadcasted_iota(jnp.int32, cu.shape, 0)
    r1 = jnp.where(row == 0, prev1, pltpu.roll(cu, 1, 0))
    r2 = jnp.where(row == 0, prev2, jnp.where(row == 1, prev1, pltpu.roll(cu, 2, 0)))
    conv = w_ref[2:3, :] * cu + w_ref[1:2, :] * r1 + w_ref[0:1, :] * r2
    return gb, gc, u, zb, cu, r1, r2, conv


def _conv_specs(tm, s, width=LANES):
    def tile(off):
        return pl.BlockSpec((tm, width), lambda c, i: (i, off // width + c))

    def before(off):
        return pl.BlockSpec((HALO, width), lambda c, i: (jnp.maximum(i * (tm // HALO) - 1, 0), off // width + c))

    def after(off):
        return pl.BlockSpec((HALO, width),
                            lambda c, i: (jnp.minimum((i + 1) * (tm // HALO), s // HALO - 1), off // width + c))

    return ([tile(OFF_CB), tile(OFF_CC), tile(OFF_CU), tile(OFF_CZ)], [before(OFF_CC), before(OFF_CU)],
            [after(OFF_CB), after(OFF_CZ)])


def _tail(oa, attn, proj, x, target, ada3, wa, wb, wo, conv_w):
    s = x.shape[0]
    tm = min(TM_TAIL, s)
    gab_blk = OFF_GA // (2 * D_MODEL)
    za_blk = OFF_ZA // ATTN_W
    tiles, befores, _ = _conv_specs(tm, s, CONV_W)

    def body(oa_ref, attn_ref, za_ref, gb_ref, gc_ref, u_ref, zb_ref, gch_ref, uh_ref, cw_ref, gab_ref, x_ref, t_ref,
             ada_ref, wa_ref, wb_ref, wo_ref,
             dy_ref, dgab_ref, do_ref, dza_ref, dob_ref, dwo_ref, dwa_ref, dwb_ref, dgate_ref, loss_ref):
        first = pl.program_id(0) == 0

        @pl.when(first)
        def _():
            dwo_ref[...] = jnp.zeros_like(dwo_ref)
            dwa_ref[...] = jnp.zeros_like(dwa_ref)
            dwb_ref[...] = jnp.zeros_like(dwb_ref)
            dgate_ref[...] = jnp.zeros_like(dgate_ref)
            loss_ref[...] = jnp.zeros_like(loss_ref)

        gb, _, _, zb, _, _, _, conv = _conv_parts(gb_ref, gc_ref, u_ref, zb_ref, gch_ref, uh_ref, first, cw_ref, tm)
        ob_v = (gb * conv * (zb * _sigmoid(zb))).astype(BF16)
        oa_v = oa_ref[...]
        wa_v, wb_v, wo_v = wa_ref[...], wb_ref[...], wo_ref[...]
        a2 = _dot(oa_v, wa_v)
        b2 = _dot(ob_v, wb_v)
        sa = _sigmoid(gab_ref[:, 0:D_MODEL].astype(F32))
        sb = _sigmoid(gab_ref[:, D_MODEL:2 * D_MODEL].astype(F32))
        mb = (sa * a2 + sb * b2).astype(BF16)
        mo = _dot(mb, wo_v)
        gate = ada_ref[2:3, :]
        err = (x_ref[...] + gate * mo) - t_ref[...]
        dy = err * (1.0 / D_MODEL)
        dy_ref[...] = dy
        loss_ref[...] += 0.5 * jnp.sum(err * err) * (1.0 / D_MODEL)
        dgate_ref[...] += jnp.sum(dy * mo, axis=0, keepdims=True)
        dmo = (dy * gate).astype(BF16)
        dmerged = _dot_nt(dmo, wo_v)
        dwo_ref[...] += _dot_tn(mb, dmo)
        da2 = (dmerged * sa).astype(BF16)
        db2 = (dmerged * sb).astype(BF16)
        dgab_ref[:, 0:D_MODEL] = (dmerged * a2 * (sa * (1.0 - sa))).astype(BF16)
        dgab_ref[:, D_MODEL:2 * D_MODEL] = (dmerged * b2 * (sb * (1.0 - sb))).astype(BF16)
        doa = _dot_nt(da2, wa_v)
        dob_ref[...] = _dot_nt(db2, wb_v)
        dwa_ref[...] += _dot_tn(oa_v, da2)
        dwb_ref[...] += _dot_tn(ob_v, db2)

        lane = lax.broadcasted_iota(jnp.int32, (tm, LANES), 1)
        lo = lane < HEAD_DIM
        for pr in range(ATTN_W // LANES):
            sl = slice(pr * LANES, (pr + 1) * LANES)
            g, a, z = doa[:, sl], attn_ref[:, sl], za_ref[:, sl].astype(F32)
            sg = _sigmoid(z)
            dat = (g * (z * sg)).astype(BF16).astype(F32)
            prod = dat * a
            dza_ref[:, sl] = (g * a * (sg * (1.0 + z * (1.0 - sg)))).astype(BF16)
            for hh in range(2):
                sel = lo if hh == 0 else jnp.logical_not(lo)
                delta3 = _split3(jnp.sum(jnp.where(sel, prod, 0.0), axis=-1, keepdims=True))
                dh = dat if hh == 0 else pltpu.roll(dat, HEAD_DIM, 1)
                tail_lanes = _place3(lane, L_ONE_Q, tuple(-d for d in delta3), 0.0)
                do_ref[2 * pr + hh] = jnp.where(lo, dh, tail_lanes).astype(BF16)

    half = pl.BlockSpec((tm, ATTN_W), lambda i: (i, 0))
    full = pl.BlockSpec((tm, D_MODEL), lambda i: (i, 0))

    def const(shape):
        return pl.BlockSpec(shape, lambda i: (0, 0))

    def one_axis(spec):
        return pl.BlockSpec(spec.block_shape, lambda i, f=spec.index_map: f(0, i))

    return pl.pallas_call(
        body, name="tail", grid=(s // tm,),
        in_specs=[half, half, pl.BlockSpec((tm, ATTN_W), lambda i: (i, za_blk))]
        + [one_axis(sp) for sp in tiles + befores]
        + [const((3, CONV_W)), pl.BlockSpec((tm, 2 * D_MODEL), lambda i: (i, gab_blk)), full, full,
           const((3, D_MODEL)), const((ATTN_W, D_MODEL)), const((CONV_W, D_MODEL)), const((D_MODEL, D_MODEL))],
        out_specs=[full, pl.BlockSpec((tm, 2 * D_MODEL), lambda i: (i, 0)),
                   pl.BlockSpec((HEADS, tm, LANES), lambda i: (0, i, 0)), half, half,
                   const((D_MODEL, D_MODEL)), const((ATTN_W, D_MODEL)), const((CONV_W, D_MODEL)),
                   const((1, D_MODEL)), const((1, LANES))],
        out_shape=[jax.ShapeDtypeStruct((s, D_MODEL), F32),
                   jax.ShapeDtypeStruct((s, 2 * D_MODEL), BF16),
                   jax.ShapeDtypeStruct((HEADS, s, LANES), BF16),
                   jax.ShapeDtypeStruct((s, ATTN_W), BF16),
                   jax.ShapeDtypeStruct((s, CONV_W), F32),
                   jax.ShapeDtypeStruct((D_MODEL, D_MODEL), F32),
                   jax.ShapeDtypeStruct((ATTN_W, D_MODEL), F32),
                   jax.ShapeDtypeStruct((CONV_W, D_MODEL), F32),
                   jax.ShapeDtypeStruct((1, D_MODEL), F32),
                   jax.ShapeDtypeStruct((1, LANES), F32)],
        compiler_params=_params(("arbitrary",)),
    )(oa, attn, proj, *([proj] * 6), conv_w, proj, x, target, ada3, wa, wb, wo)


def _attn_bwd(qb, ka, kt, va, do, proj, qg, kg):
    s = qb.shape[1]
    t = min(TQ, s)
    nt = s // t
    hp = HEADS_PER_STEP
    assert hp == HEADS, "all heads share one (S, 1536) output block"
    wide = hp * HEAD_DIM
    scale = HEAD_DIM ** -0.5
    it, jt = _tri_steps(nt, False)

    def body(it_ref, jt_ref, q_ref, k_ref, kt_ref, v_ref, do_ref, qraw_ref, kraw_ref, qg_ref, kg_ref,
             dqkv_ref, dqg_ref, dkg_ref, dcum_ref, dqt_s, dk_s, dv_s, rows_s):
        grp, step = pl.program_id(0), pl.program_id(1)
        i, j = it_ref[step], jt_ref[step]
        lane = lax.broadcasted_iota(jnp.int32, (t, LANES), 1)
        lo = lane < HEAD_DIM

        @pl.when(step == 0)
        def _():
            dqt_s[...] = jnp.zeros_like(dqt_s)
            dqg_ref[...] = jnp.zeros_like(dqg_ref)
            dkg_ref[...] = jnp.zeros_like(dkg_ref)

        @pl.when(i == j)
        def _():
            dk_s[...] = jnp.zeros_like(dk_s)
            dv_s[...] = jnp.zeros_like(dv_s)

        def update(masked):
            for hh in range(hp):
                qh, doh = q_ref[hh], do_ref[hh]
                st = _dot_nt(k_ref[hh], qh)
                if masked:
                    st = jnp.where(_causal_t(t), st, NEG)
                pt = jnp.exp(st)
                dst = (pt * _dot_nt(v_ref[hh], doh)).astype(BF16)
                dv_s[hh] += _dot(pt.astype(BF16), doh)
                dk_s[hh] += _dot(dst, qh)
                dqt_s[hh, i] += _dot(kt_ref[hh], dst)

        def pair(a, b):
            return jnp.where(lo, a, pltpu.roll(b, HEAD_DIM, 1))

        def norm_bwd(raw, dy, g, dg_ref, off, sl):
            r = lax.rsqrt(_seg_sum(raw * raw, lo) * (1.0 / HEAD_DIM) + EPS)
            xhat = raw * r
            dg_ref[:, sl] += jnp.sum(dy * xhat, axis=0, keepdims=True)
            dxh = dy * g
            dx = r * (dxh - xhat * (_seg_sum(dxh * xhat, lo) * (1.0 / HEAD_DIM)))
            dqkv_ref[:, off + sl.start:off + sl.stop] = dx.astype(BF16)

        @pl.when(i > j)
        def _():
            update(False)

        @pl.when(i == j)
        def _():
            update(True)
            dq_rows = [dqt_s[hh, i].T for hh in range(hp)]
            rows = jnp.zeros((t, LANES), F32)
            for hh in range(hp):
                rows = jnp.where(lane == grp * hp + hh, _lane_col(dq_rows[hh], L_F_Q), rows)
            rows_s[...] = rows
            for pr in range(hp // 2):
                sl = slice(pr * LANES, (pr + 1) * LANES)
                norm_bwd(qraw_ref[:, sl].astype(F32), pair(dq_rows[2 * pr], dq_rows[2 * pr + 1]) * scale,
                         qg_ref[:, sl], dqg_ref, OFF_Q, sl)

        @pl.when(i == nt - 1)
        def _():
            dcum = rows_s[...]
            for hh in range(hp):
                dcum = jnp.where(lane == grp * hp + hh, dcum - _lane_col(dk_s[hh], L_ONE_Q), dcum)
            dcum_ref[0] = dcum
            for pr in range(hp // 2):
                sl = slice(pr * LANES, (pr + 1) * LANES)
                norm_bwd(kraw_ref[:, sl].astype(F32), pair(dk_s[2 * pr], dk_s[2 * pr + 1]),
                         kg_ref[:, sl], dkg_ref, OFF_K, sl)
                dqkv_ref[:, OFF_V + sl.start:OFF_V + sl.stop] = pair(dv_s[2 * pr], dv_s[2 * pr + 1]).astype(BF16)

    pair_q = pl.BlockSpec((hp, t, LANES), lambda p, n, it_, jt_: (p, it_[n], 0))
    pair_k = pl.BlockSpec((hp, t, LANES), lambda p, n, it_, jt_: (p, jt_[n], 0))
    pair_kt = pl.BlockSpec((hp, LANES, t), lambda p, n, it_, jt_: (p, 0, jt_[n]))
    tok3 = pl.BlockSpec((t, 3 * ATTN_W), lambda p, n, it_, jt_: (jt_[n], 0))
    gain = pl.BlockSpec((1, wide), lambda p, n, it_, jt_: (0, p))
    return pl.pallas_call(
        body, name="attn_bwd",
        grid_spec=pltpu.PrefetchScalarGridSpec(
            num_scalar_prefetch=2, grid=(HEADS // hp, it.shape[0]),
            in_specs=[pair_q, pair_k, pair_kt, pair_k, pair_q,
                      pl.BlockSpec((t, wide), lambda p, n, it_, jt_: (jt_[n], OFF_Q // wide + p)),
                      pl.BlockSpec((t, wide), lambda p, n, it_, jt_: (jt_[n], OFF_K // wide + p)), gain, gain],
            out_specs=[tok3, gain, gain,
                       pl.BlockSpec((1, t, LANES), lambda p, n, it_, jt_: (p, jt_[n], 0))],
            scratch_shapes=[pltpu.VMEM((hp, nt, LANES, t), F32), pltpu.VMEM((hp, t, LANES), F32),
                            pltpu.VMEM((hp, t, LANES), F32), pltpu.VMEM((t, LANES), F32)]),
        out_shape=[jax.ShapeDtypeStruct((s, 3 * ATTN_W), BF16)]
        + [jax.ShapeDtypeStruct((1, ATTN_W), F32)] * 2
        + [jax.ShapeDtypeStruct((HEADS // hp, s, LANES), F32)],
        compiler_params=_params(("parallel", "arbitrary")),
    )(it, jt, qb, ka, kt, va, do, proj, proj, qg, kg)


def _forget_bwd(dcum, fl, bf_pad):
    s = fl.shape[0]
    tc = min(TC_CUM, s)
    n = s // tc

    def body(dc_ref, fl_ref, bf_ref, df_ref, dbf_ref, carry):
        @pl.when(pl.program_id(0) == 0)
        def _():
            carry[...] = jnp.zeros_like(carry)
            dbf_ref[...] = jnp.zeros_like(dbf_ref)
        r = lax.broadcasted_iota(jnp.int32, (tc, tc), 0)
        cidx = lax.broadcasted_iota(jnp.int32, (tc, tc), 1)
        tri = (cidx >= r).astype(F32)
        dc = dc_ref[0]
        for grp in range(1, dcum.shape[0]):
            dc = dc + dc_ref[grp]
        dlf = jnp.dot(tri, dc, preferred_element_type=F32, precision=lax.Precision.HIGHEST) + carry[...]
        carry[...] += jnp.sum(dc, axis=0, keepdims=True)
        lane = lax.broadcasted_iota(jnp.int32, (tc, LANES), 1)
        dfl = jnp.where(lane < HEADS, dlf * _sigmoid(-(fl_ref[...] + bf_ref[...])), 0.0)
        df_ref[...] = dfl.astype(BF16)
        dbf_ref[...] += jnp.sum(dfl, axis=0, keepdims=True)

    rev = pl.BlockSpec((tc, LANES), lambda i: (n - 1 - i, 0))
    vec = pl.BlockSpec((1, LANES), lambda i: (0, 0))
    return pl.pallas_call(
        body, name="forget_bwd", grid=(n,),
        in_specs=[pl.BlockSpec((dcum.shape[0], tc, LANES), lambda i: (0, n - 1 - i, 0)), rev, vec],
        out_specs=[rev, vec],
        out_shape=[jax.ShapeDtypeStruct((s, LANES), BF16), jax.ShapeDtypeStruct((1, LANES), F32)],
        scratch_shapes=[pltpu.VMEM((1, LANES), F32)],
        compiler_params=_params(("arbitrary",)),
    )(dcum, fl, bf_pad)


def _conv_bwd(dob, proj, conv_w):
    s = dob.shape[0]
    tm = min(TM_ELEM, s)
    wd = CONV_W
    tiles, befores, afters = _conv_specs(tm, s, wd)

    def body(dob_ref, dnext_ref, gb_ref, gc_ref, u_ref, zb_ref, gch_ref, uh_ref, gbn_ref, zbn_ref, w_ref,
             out_ref, dw_ref):
        i = pl.program_id(1)

        @pl.when(i == 0)
        def _():
            dw_ref[...] = jnp.zeros_like(dw_ref)
        gb, gc, u, zb, cu, r1, r2, conv = _conv_parts(gb_ref, gc_ref, u_ref, zb_ref, gch_ref, uh_ref, i == 0, w_ref, tm)
        g = dob_ref[...]
        sg = _sigmoid(zb)
        sz = zb * sg
        dconv = g * gb * sz
        zn = zbn_ref[0:8, :].astype(F32)
        dcn = jnp.where(i == pl.num_programs(1) - 1, 0.0,
                        dnext_ref[...] * gbn_ref[0:8, :].astype(F32) * (zn * _sigmoid(zn)))
        nxt1, nxt2 = _sub_row(dcn, 0), _sub_row(dcn, 1)
        row = lax.broadcasted_iota(jnp.int32, (tm, wd), 0)
        f1 = jnp.where(row == tm - 1, nxt1, pltpu.roll(dconv, tm - 1, 0))
        f2 = jnp.where(row == tm - 2, nxt1, jnp.where(row == tm - 1, nxt2, pltpu.roll(dconv, tm - 2, 0)))
        dcu = w_ref[2:3, :] * dconv + w_ref[1:2, :] * f1 + w_ref[0:1, :] * f2
        out_ref[:, 0:wd] = (g * conv * sz).astype(BF16)
        out_ref[:, wd:2 * wd] = (dcu * u).astype(BF16)
        out_ref[:, 2 * wd:3 * wd] = (dcu * gc).astype(BF16)
        out_ref[:, 3 * wd:4 * wd] = (g * gb * conv * (sg * (1.0 + zb * (1.0 - sg)))).astype(BF16)
        w_row = lax.broadcasted_iota(jnp.int32, (3, wd), 0)
        dw0 = jnp.sum(dconv * r2, axis=0, keepdims=True)
        dw1 = jnp.sum(dconv * r1, axis=0, keepdims=True)
        dw2 = jnp.sum(dconv * cu, axis=0, keepdims=True)
        dw_ref[...] += jnp.where(w_row == 0, dw0, jnp.where(w_row == 1, dw1, dw2))

    blk = pl.BlockSpec((tm, wd), lambda c, i: (i, c))
    nxt = pl.BlockSpec((8, wd), lambda c, i: (jnp.minimum((i + 1) * (tm // 8), s // 8 - 1), c))
    wspec = pl.BlockSpec((3, wd), lambda c, i: (0, c))
    return pl.pallas_call(
        body, name="conv_bwd", grid=(CONV_W // wd, s // tm),
        in_specs=[blk, nxt] + tiles + befores + afters + [wspec],
        out_specs=[pl.BlockSpec((tm, 4 * wd), lambda c, i: (i, c)), wspec],
        out_shape=[jax.ShapeDtypeStruct((s, 4 * CONV_W), BF16), jax.ShapeDtypeStruct((3, CONV_W), F32)],
        compiler_params=_params(("parallel", "arbitrary")),
    )(dob, dob, *([proj] * 8), conv_w)


def _piece_layout(pieces):
    offs, off = [], 0
    for p in pieces:
        offs.append((off, p.shape[1]))
        off += p.shape[1]
    assert off == N_ALL, off
    return offs


def _dw_in(h, pieces, chip_sums):
    s = h.shape[0]
    tk, tn = min(TK_DW, s), TN_DW
    nk = s // tk
    nn = N_MAIN // tn
    main, fpiece = pieces[:-1], pieces[-1]
    layout = _piece_layout(pieces)[:-1]
    n_main = len(main)
    nx = len(chip_sums)

    def body(*refs):
        p_refs, f_ref, h_ref = refs[:n_main], refs[n_main], refs[n_main + 1]
        ins, refs = refs[n_main + 2:n_main + 2 + nx], refs[n_main + 2 + nx:]
        out_ref, outf_ref = refs[:2]
        outs, (acc, accf, send_sems, recv_sems, local_sems) = refs[2:2 + nx], refs[2 + nx:]
        n, k = pl.program_id(0), pl.program_id(1)

        @pl.when(jnp.logical_and(n == 0, k == 0))
        def _():
            for cp in _all_to_all_copies(ins, outs, send_sems, recv_sems, local_sems):
                cp.start()

        @pl.when(k == 0)
        def _():
            acc[...] = jnp.zeros_like(acc)
        hv = h_ref[pl.ds(pl.multiple_of(k * tk, tk), tk), :]
        for p_ref, (off, width) in zip(p_refs, layout):
            @pl.when(jnp.logical_and(n >= off // tn, n < (off + width) // tn))
            def _():
                acc[...] += _dot_tn(p_ref[...], hv)

        @pl.when(k == nk - 1)
        def _():
            out_ref[...] = acc[...].astype(BF16)

        @pl.when(n == 0)
        def _():
            @pl.when(k == 0)
            def _():
                accf[...] = jnp.zeros_like(accf)
            accf[...] += _dot_tn(f_ref[...], hv)

            @pl.when(k == nk - 1)
            def _():
                outf_ref[...] = accf[...].astype(BF16)

        @pl.when(jnp.logical_and(n == nn - 1, k == nk - 1))
        def _():
            for cp in _all_to_all_copies(ins, outs, send_sems, recv_sems, local_sems):
                cp.wait()

    def piece_spec(off, width):
        lo, hi = off // tn, (off + width) // tn

        def index(n, k):
            active = jnp.logical_and(n >= lo, n < hi)
            return jnp.where(active, k, 0), jnp.clip(n - lo, 0, hi - lo - 1)
        return pl.BlockSpec((tk, tn), index)

    any_spec = pl.BlockSpec(memory_space=pl.ANY)
    res = pl.pallas_call(
        body, name="dw_in", grid=(nn, nk),
        in_specs=[piece_spec(off, width) for off, width in layout]
        + [pl.BlockSpec((tk, N_FPAD), lambda n, k: (jnp.where(n == 0, k, 0), 0)),
           pl.BlockSpec((s, D_MODEL), lambda n, k: (0, 0))] + [any_spec] * nx,
        out_specs=[pl.BlockSpec((tn, D_MODEL), lambda n, k: (n, 0)),
                   pl.BlockSpec((N_FPAD, D_MODEL), lambda n, k: (0, 0))] + [any_spec] * nx,
        out_shape=[jax.ShapeDtypeStruct((N_MAIN, D_MODEL), BF16), jax.ShapeDtypeStruct((N_FPAD, D_MODEL), BF16)]
        + [jax.ShapeDtypeStruct(a.shape, a.dtype) for a in chip_sums],
        scratch_shapes=[pltpu.VMEM((tn, D_MODEL), F32), pltpu.VMEM((N_FPAD, D_MODEL), F32)] + _gather_sems(nx),
        compiler_params=_params(("arbitrary", "arbitrary")),
    )(*main, fpiece, h, *chip_sums)
    return res[:2], res[2:]


def _dh_and_dx(pieces, w_all_t, x, dy, ada3, norm_g, chip_sums):
    s = x.shape[0]
    tm = min(TM_DH, s)
    nt = s // tm
    n = len(chip_sums)
    npc = len(pieces)
    layout = _piece_layout(pieces)

    def body(*refs):
        p_refs, refs = refs[:npc], refs[npc:]
        wt_ref, x_ref, dy_ref, ada_ref, g_ref = refs[:5]
        ins, refs = refs[5:5 + n], refs[5 + n:]
        gx_ref, dsh_ref, dsc_ref, dg_ref = refs[:4]
        outs, (send_sems, recv_sems, local_sems) = refs[4:4 + n], refs[4 + n:]
        i = pl.program_id(0)

        @pl.when(i == 0)
        def _():
            for cp in _chip_copies(ins, outs, send_sems, recv_sems, local_sems):
                cp.start()
            dsh_ref[...] = jnp.zeros_like(dsh_ref)
            dsc_ref[...] = jnp.zeros_like(dsc_ref)
            dg_ref[...] = jnp.zeros_like(dg_ref)

        dh = None
        for p_ref, (off, width) in zip(p_refs, layout):
            part = _dot(p_ref[...], wt_ref[off:off + width, :])
            dh = part if dh is None else dh + part
        xv = x_ref[...]
        r = lax.rsqrt(jnp.mean(xv * xv, axis=-1, keepdims=True) + EPS)
        xhat = xv * r
        g = g_ref[...]
        one_sc = 1.0 + ada_ref[1:2, :]
        dsh_ref[...] += jnp.sum(dh, axis=0, keepdims=True)
        dsc_ref[...] += jnp.sum(dh * (xhat * g), axis=0, keepdims=True)
        dg_ref[...] += jnp.sum(dh * xhat, axis=0, keepdims=True) * one_sc
        dxh = dh * (g * one_sc)
        dx = r * (dxh - xhat * jnp.mean(dxh * xhat, axis=-1, keepdims=True))
        gx_ref[...] = dy_ref[...] + dx

        @pl.when(i == nt - 1)
        def _():
            for cp in _chip_copies(ins, outs, send_sems, recv_sems, local_sems):
                cp.wait()

    full = pl.BlockSpec((tm, D_MODEL), lambda i: (i, 0))
    vec = pl.BlockSpec((1, D_MODEL), lambda i: (0, 0))
    any_spec = pl.BlockSpec(memory_space=pl.ANY)
    res = pl.pallas_call(
        body, name="dh_dx", grid=(nt,),
        in_specs=[pl.BlockSpec((tm, p.shape[1]), lambda i: (i, 0)) for p in pieces]
        + [pl.BlockSpec((N_ALL, D_MODEL), lambda i: (0, 0)), full, full,
           pl.BlockSpec((3, D_MODEL), lambda i: (0, 0)), vec] + [any_spec] * n,
        out_specs=[full, vec, vec, vec] + [any_spec] * n,
        out_shape=[jax.ShapeDtypeStruct((s, D_MODEL), F32)] + [jax.ShapeDtypeStruct((1, D_MODEL), F32)] * 3
        + [jax.ShapeDtypeStruct(a.shape, a.dtype) for a in chip_sums],
        scratch_shapes=[pltpu.SemaphoreType.DMA((n * 3,)), pltpu.SemaphoreType.DMA((n * 3,)),
                        pltpu.SemaphoreType.DMA((n,))],
        compiler_params=_params(("arbitrary",)),
    )(*pieces, w_all_t, x, dy, ada3, norm_g, *chip_sums)
    return res[:4], res[4:]


def _sum_small(vec_all, qg_parts, kg_parts):
    def body(v_ref, q_ref, k_ref, tot_ref, gq_ref, gk_ref):
        tot = v_ref[0:1, :]
        for p in range(1, N_DEV):
            tot = tot + v_ref[p:p + 1, :]
        tot_ref[...] = tot
        gq_ref[...] = jnp.sum(q_ref[...], axis=0, keepdims=True)
        gk_ref[...] = jnp.sum(k_ref[...], axis=0, keepdims=True)

    n = vec_all.shape[-1]
    return pl.pallas_call(
        body, name="sum_small",
        out_shape=[jax.ShapeDtypeStruct((1, n), F32),
                   jax.ShapeDtypeStruct((1, HEAD_DIM), F32), jax.ShapeDtypeStruct((1, HEAD_DIM), F32)],
        compiler_params=_params(),
    )(vec_all, qg_parts, kg_parts)


def _grad_w_ada(c_cols, dada_rows):
    def body(c_ref, d_ref, out_ref):
        acc = c_ref[0] * d_ref[0]
        for b in range(1, N_DEV):
            acc = acc + c_ref[b] * d_ref[b]
        out_ref[...] = acc

    return pl.pallas_call(
        body, name="grad_w_ada",
        out_shape=jax.ShapeDtypeStruct((D_MODEL, ADA_SHARD), F32),
        compiler_params=_params(),
    )(c_cols, dada_rows)


def _adam_step(w, m, v, g):
    c1 = 1.0 / (1.0 - ADAM_B1 ** ADAM_STEP)
    c2 = 1.0 / (1.0 - ADAM_B2 ** ADAM_STEP)
    m_new = ADAM_B1 * m + (1.0 - ADAM_B1) * g
    v_new = ADAM_B2 * v + (1.0 - ADAM_B2) * (g * g)
    return -ADAM_LR * ((m_new * c1) / (jnp.sqrt(v_new * c2) + ADAM_EPS) + ADAM_WD * w), m_new, v_new


def _adamw_small(params, name):
    n = len(params)
    stacked = [p[3].ndim == p[0].ndim + 1 for p in params]

    def body(*refs):
        ins, outs = refs[:4 * n], refs[4 * n:]
        for k in range(n):
            w_ref, m_ref, v_ref, g_ref = ins[4 * k:4 * k + 4]
            go_ref, d_ref, mo_ref, vo_ref = outs[4 * k:4 * k + 4]
            if stacked[k]:
                g = g_ref[0].astype(F32)
                for p in range(1, g_ref.shape[0]):
                    g = g + g_ref[p].astype(F32)
            else:
                g = g_ref[...]
            go_ref[...] = g
            d_ref[...], mo_ref[...], vo_ref[...] = _adam_step(w_ref[...], m_ref[...], v_ref[...], g)

    res = pl.pallas_call(
        body, name=name,
        out_shape=[jax.ShapeDtypeStruct(p[0].shape, F32) for p in params for _ in range(4)],
        compiler_params=_params(),
    )(*[a for p in params for a in p])
    return [tuple(res[4 * k:4 * k + 4]) for k in range(n)]


def _adamw(w, m, v, g_parts, name):
    rows, cols = w.shape
    n_parts = g_parts.shape[0]
    tr = 256 if rows % 256 == 0 else rows
    tc = 256 if (tr == rows and rows > 256 and cols % 256 == 0) else cols

    def body(w_ref, m_ref, v_ref, g_ref, go_ref, d_ref, mo_ref, vo_ref):
        g = g_ref[0].astype(F32)
        for p in range(1, n_parts):
            g = g + g_ref[p].astype(F32)
        go_ref[...] = g
        d_ref[...], mo_ref[...], vo_ref[...] = _adam_step(w_ref[...], m_ref[...], v_ref[...], g)

    blk = pl.BlockSpec((tr, tc), lambda i, j: (i, j))
    return pl.pallas_call(
        body, name=name, grid=(rows // tr, cols // tc),
        in_specs=[blk, blk, blk, pl.BlockSpec((n_parts, tr, tc), lambda i, j: (0, i, j))],
        out_specs=[blk] * 4,
        out_shape=[jax.ShapeDtypeStruct((rows, cols), F32)] * 4,
        compiler_params=_params(("parallel", "parallel")),
    )(w, m, v, g_parts)


_O_F = 1536


W_TILE = 16
WIN_ROWS = 784


def _internal_start(p):
    return p * IN_SHARD - (HEADS if p * IN_SHARD > _O_F else 0)


def _shard_window(wt_shard, me):
    lo = me * IN_SHARD
    o = lo + lax.broadcasted_iota(jnp.int32, (IN_SHARD, 1), 0)
    is_f = jnp.logical_and(o >= _O_F, o < _O_F + HEADS)
    start = lo - jnp.where(lo > _O_F, HEADS, 0)
    window = lax.dynamic_update_slice(jnp.zeros((WIN_ROWS + W_TILE, D_MODEL), BF16),
                                      jnp.where(is_f, 0.0, wt_shard).astype(BF16), (start % W_TILE, 0))
    first = jnp.clip(_O_F - lo, 0, IN_SHARD - W_TILE)
    near = lax.dynamic_slice(wt_shard, (first, 0), (W_TILE, D_MODEL))
    j = lax.broadcasted_iota(jnp.int32, (W_TILE, 1), 0)
    src = _O_F - lo + j
    ok = jnp.logical_and(j < HEADS, jnp.logical_and(src >= 0, src < IN_SHARD))
    f_tile = jnp.where(ok, jnp.roll(near, first - (_O_F - lo), axis=0), 0.0).astype(BF16)
    return lax.dynamic_update_slice(window, f_tile, (WIN_ROWS, 0))


def _assemble_w(windows):
    chunk = 112

    def body(g_ref, out_ref):
        out_ref[WIN_ROWS:N_MAIN, :] = jnp.zeros((N_MAIN - WIN_ROWS, D_MODEL), BF16)
        for p in range(N_DEV):
            base = _internal_start(p) // W_TILE * W_TILE
            for r in range(0, WIN_ROWS, chunk):
                rows = slice(base + r, base + r + chunk)
                piece = g_ref[p, r:r + chunk, :]
                out_ref[rows, :] = piece if p == 0 else out_ref[rows, :] + piece
        f = g_ref[0, WIN_ROWS:WIN_ROWS + W_TILE, :]
        for p in range(1, N_DEV):
            f = f + g_ref[p, WIN_ROWS:WIN_ROWS + W_TILE, :]
        out_ref[N_MAIN:N_MAIN + W_TILE, :] = f
        out_ref[N_MAIN + W_TILE:N_ALL, :] = jnp.zeros((N_FPAD - W_TILE, D_MODEL), BF16)

    return pl.pallas_call(
        body, name="assemble_w", out_shape=jax.ShapeDtypeStruct((N_ALL, D_MODEL), BF16),
        compiler_params=_params(),
    )(windows)


def _slabs_by_core(dwt, dwt_f):
    sources = ((dwt, 0, _O_F, 0), (dwt_f, _O_F, _O_F + HEADS, _O_F), (dwt, _O_F + HEADS, IN_WIDTH, HEADS))

    def slab(p):
        lo, hi = p * IN_SHARD, (p + 1) * IN_SHARD
        parts = []
        for src, o_lo, o_hi, shift in sources:
            a, b = max(lo, o_lo), min(hi, o_hi)
            if a < b:
                parts.append(src[a - shift:b - shift])
        return parts[0] if len(parts) == 1 else jnp.concatenate(parts, axis=0)

    return jnp.stack([jnp.stack([slab(2 * chip + core) for chip in range(4)]) for core in range(2)])


def kernel(x, c, w_ada, b_ada, norm_g, w_in, b_f, q_norm_g, k_norm_g, conv_w, w_attn_out, w_conv_out, w_o, loss_target, m_w_ada, m_b_ada, m_norm_g, m_w_in, m_b_f, m_q_norm_g, m_k_norm_g, m_conv_w, m_w_attn_out, m_w_conv_out, m_w_o, v_w_ada, v_b_ada, v_norm_g, v_w_in, v_b_f, v_q_norm_g, v_k_norm_g, v_conv_w, v_w_attn_out, v_w_conv_out, v_w_o):
    me = 4 * lax.axis_index("x") + 2 * lax.axis_index("y") + lax.axis_index("c")
    s = x.shape[1]
    x2, t2 = x[0], loss_target[0]

    w_in_g, c_all, ada_g = _gather_weights_and_ada(_shard_window(w_in[0].T, me), c, w_ada[0])
    ada_mine = lax.dynamic_index_in_dim(ada_g[:, :, 0, :], me, axis=1, keepdims=False)
    ada3 = (ada_mine.reshape(1, 3 * D_MODEL) + b_ada).reshape(3, D_MODEL)
    w_all_t = _assemble_w(w_in_g)
    qg = jnp.tile(q_norm_g, (1, HEADS))
    kg = jnp.tile(k_norm_g, (1, HEADS))
    bf_pad = jnp.pad(b_f, ((0, 0), (0, LANES - HEADS)))

    (proj, fl, h), (qa, ka, va, kt, vt), (cw_g, wa_g, wb_g, wo_g) = _proj_fwd(
        x2, ada3, norm_g, w_all_t, bf_pad, qg, kg,
        [conv_w[0], w_attn_out[0].astype(BF16), w_conv_out[0].astype(BF16), w_o[0].astype(BF16)])
    wa = jnp.transpose(wa_g, (1, 0, 2)).reshape(ATTN_W, D_MODEL)
    wb = jnp.transpose(wb_g, (1, 0, 2)).reshape(CONV_W, D_MODEL)
    wo = wo_g.reshape(D_MODEL, D_MODEL)
    cw = jnp.transpose(cw_g, (1, 0, 2)).reshape(3, CONV_W)
    attn, oa, qb = _attn_fwd(qa, ka, vt, proj)
    (dy, dgab, do, dza, dob, dwo, dwa, dwb, dgate, loss_part) = _tail(oa, attn, proj, x2, t2, ada3, wa, wb, wo, cw)

    core = lax.axis_index("c").astype(jnp.int32).reshape(1)
    small = [jnp.transpose(dwa.reshape(ATTN_W, N_DEV, LANES), (1, 0, 2)).astype(BF16),
             jnp.transpose(dwb.reshape(CONV_W, N_DEV, LANES), (1, 0, 2)).astype(BF16),
             dwo.reshape(N_DEV, D_MODEL // N_DEV, D_MODEL).astype(BF16)]
    dqkv, dqg, dkg, dcum = _attn_bwd(qb, ka, kt, va, do, proj, qg, kg)
    df, dbf = _forget_bwd(dcum, fl, bf_pad)
    dconv, dcw = _conv_bwd(dob, proj, cw)
    pieces = [dqkv, dza, dconv, dgab, df]
    (dw_main, dw_f), (g_wa_parts, g_wb_parts, g_wo_parts) = _dw_in(h, pieces, small)

    slabs_in = _slabs_by_core(dw_main, dw_f)
    (theirs_in,) = _sibling_swap([slabs_in], "swap_w_in")
    (grad_x, dshift, dscale, dnormg), (g_in_parts,) = _dh_and_dx(
        pieces, w_all_t, x2, dy, ada3, norm_g, [_pair_sum(slabs_in, theirs_in, core, "pair_sum_w_in")])
    vec = jnp.concatenate([dshift, dscale, dgate, dnormg, dbf, dcw.reshape(1, 3 * CONV_W), loss_part, dqg, dkg],
                          axis=1)
    (vec_all,) = _gather_direct([vec], "gather_small")
    vec_all = vec_all.reshape(N_DEV, vec.shape[1])
    n_main = 4 * D_MODEL + LANES + 3 * CONV_W + LANES
    tot, g_qg, g_kg = _sum_small(
        vec_all[:, :n_main],
        vec_all[:, n_main:n_main + ATTN_W].reshape(N_DEV * HEADS, HEAD_DIM),
        vec_all[:, n_main + ATTN_W:].reshape(N_DEV * HEADS, HEAD_DIM))
    g_b_ada = tot[:, 0:3 * D_MODEL]
    g_norm_g = tot[:, 3 * D_MODEL:4 * D_MODEL]
    g_b_f = tot[:, 4 * D_MODEL:4 * D_MODEL + HEADS]
    g_cw_full = tot[:, 4 * D_MODEL + LANES:4 * D_MODEL + LANES + 3 * CONV_W].reshape(3, CONV_W)
    g_cw = lax.dynamic_slice(g_cw_full, (0, me * (CONV_W // N_DEV)), (3, CONV_W // N_DEV))
    dada_mine = lax.dynamic_slice(vec_all[:, 0:3 * D_MODEL], (0, me * ADA_SHARD), (N_DEV, ADA_SHARD))
    g_w_ada = _grad_w_ada(jnp.transpose(c_all, (0, 2, 1)), dada_mine.reshape(N_DEV, 1, ADA_SHARD))

    upd = {}
    upd["w_ada"] = _adamw(w_ada[0], m_w_ada[0], v_w_ada[0], g_w_ada[None], "adamw_w_ada")
    upd["w_in"] = [u.T for u in _adamw(w_in[0].T, m_w_in[0].T, v_w_in[0].T, g_in_parts, "adamw_w_in")]
    small_names = ["b_ada", "norm_g", "b_f", "q_norm_g", "k_norm_g", "conv_w", "w_attn_out", "w_conv_out", "w_o"]
    small_upd = _adamw_small(
        [(b_ada, m_b_ada, v_b_ada, g_b_ada), (norm_g, m_norm_g, v_norm_g, g_norm_g), (b_f, m_b_f, v_b_f, g_b_f),
         (q_norm_g, m_q_norm_g, v_q_norm_g, g_qg), (k_norm_g, m_k_norm_g, v_k_norm_g, g_kg),
         (conv_w[0], m_conv_w[0], v_conv_w[0], g_cw),
         (w_attn_out[0], m_w_attn_out[0], v_w_attn_out[0], g_wa_parts),
         (w_conv_out[0], m_w_conv_out[0], v_w_conv_out[0], g_wb_parts),
         (w_o[0], m_w_o[0], v_w_o[0], g_wo_parts)], "adamw_small")
    upd.update(zip(small_names, small_upd))

    names = ["w_ada", "b_ada", "norm_g", "w_in", "b_f", "q_norm_g", "k_norm_g", "conv_w",
             "w_attn_out", "w_conv_out", "w_o"]
    lead = {"w_ada", "w_in", "conv_w", "w_attn_out", "w_conv_out", "w_o"}
    fix = lambda n, a: a[None] if n in lead else a
    loss = tot[0, n_main - LANES]
    outs = [loss, grad_x[None]]
    for k in range(4):
        outs += [fix(n, upd[n][k]) for n in names]
    return tuple(outs)
```

```python
import numpy as np
import jax
import jax.numpy as jnp
from jax import lax
from jax.experimental import pallas as pl
from jax.experimental.pallas import tpu as pltpu

F32 = jnp.float32
BF16 = jnp.bfloat16

D_MODEL = 1024
HEADS = 8
HEAD_DIM = 64
ATTN_W = 512
CONV_W = 512
N_DEV = 8
IN_WIDTH = 6152
IN_SHARD = IN_WIDTH // N_DEV
N_MAIN = 6144
N_FPAD = 128
N_ALL = N_MAIN + N_FPAD
ADA_SHARD = 3 * D_MODEL // N_DEV
EPS = 1e-6
NEG = -1e30

ADAM_LR = 0.001
ADAM_B1 = 0.9
ADAM_B2 = 0.999
ADAM_EPS = 1e-08
ADAM_WD = 0.01
ADAM_STEP = 10

LANES = 128
VMEM_LIMIT = 56 * 1024 * 1024

TM_PROJ = 512
TN_PROJ = 1024
TM_ELEM = 512
TQ = 512
HEADS_PER_STEP = 8
TM_TAIL = 256
TC_CUM = 512
TK_DW = 2048
TN_DW = 512
TM_DH = 256
HALO = 16

OFF_Q, OFF_K, OFF_V, OFF_ZA, OFF_CB, OFF_CC, OFF_CU, OFF_CZ, OFF_GA, OFF_GB = (
    0, 512, 1024, 1536, 2048, 2560, 3072, 3584, 4096, 5120)


def _params(sem=None):
    return pltpu.CompilerParams(dimension_semantics=sem, vmem_limit_bytes=VMEM_LIMIT)


def _dot(a, b):
    return jnp.dot(a, b, preferred_element_type=F32)


def _dot_nt(a, b):
    return lax.dot_general(a, b, (((1,), (1,)), ((), ())), preferred_element_type=F32)


def _dot_tn(a, b):
    return lax.dot_general(a, b, (((0,), (0,)), ((), ())), preferred_element_type=F32)


def _sigmoid(x):
    return 1.0 / (1.0 + jnp.exp(-x))


def _seg_sum(z, lo):
    a = jnp.sum(jnp.where(lo, z, 0.0), axis=-1, keepdims=True)
    b = jnp.sum(jnp.where(lo, 0.0, z), axis=-1, keepdims=True)
    return jnp.where(lo, a, b)


def _lane_col(z, lane):
    idx = lax.broadcasted_iota(jnp.int32, z.shape, 1)
    return jnp.sum(jnp.where(idx == lane, z, 0.0), axis=-1, keepdims=True)


def _sub_row(z, row):
    idx = lax.broadcasted_iota(jnp.int32, z.shape, 0)
    return jnp.sum(jnp.where(idx == row, z, 0.0), axis=0, keepdims=True)


def _mesh_pos():
    x, y, c = lax.axis_index("x"), lax.axis_index("y"), lax.axis_index("c")
    return x, y, c, 4 * x + 2 * y + c


def _peer(k, x, y, c):
    px = 1 - x if (k >> 2) & 1 else x
    py = 1 - y if (k >> 1) & 1 else y
    pc = 1 - c if k & 1 else c
    return (px, py, pc), 4 * px + 2 * py + pc


def _gather_copies(ins, outs, send_sems, recv_sems, local_sems):
    x, y, c, me = _mesh_pos()
    copies = []
    for a in range(len(ins)):
        copies.append(pltpu.make_async_copy(ins[a], outs[a].at[me], local_sems.at[a]))
        for k in range(1, N_DEV):
            dev, _ = _peer(k, x, y, c)
            copies.append(pltpu.make_async_remote_copy(
                src_ref=ins[a], dst_ref=outs[a].at[me],
                send_sem=send_sems.at[a * (N_DEV - 1) + k - 1], recv_sem=recv_sems.at[a * (N_DEV - 1) + k - 1],
                device_id=dev, device_id_type=pl.DeviceIdType.MESH))
    return copies


def _gather_sems(n):
    return [pltpu.SemaphoreType.DMA((n * (N_DEV - 1),)), pltpu.SemaphoreType.DMA((n * (N_DEV - 1),)),
            pltpu.SemaphoreType.DMA((n,))]


def _gather_direct(arrs, name):
    n = len(arrs)
    any_spec = pl.BlockSpec(memory_space=pl.ANY)

    def body(*refs):
        copies = _gather_copies(refs[:n], refs[n:2 * n], *refs[2 * n:])
        for cp in copies:
            cp.start()
        for cp in copies:
            cp.wait()

    return pl.pallas_call(
        body, name=name, out_shape=[jax.ShapeDtypeStruct((N_DEV,) + a.shape, a.dtype) for a in arrs],
        in_specs=[any_spec] * n, out_specs=[any_spec] * n, scratch_shapes=_gather_sems(n),
    )(*arrs)


def _ada_phase(c_ref, w_ref, call_ref, adag_ref, mine_ref, send_sems, recv_sems):
    x, y, c, me = _mesh_pos()

    def copy(phase, k, src, dst):
        dev, _ = _peer(k, x, y, c)
        return pltpu.make_async_remote_copy(
            src_ref=src, dst_ref=dst,
            send_sem=send_sems.at[phase * (N_DEV - 1) + k - 1],
            recv_sem=recv_sems.at[phase * (N_DEV - 1) + k - 1],
            device_id=dev, device_id_type=pl.DeviceIdType.MESH)

    call_ref[me] = c_ref[...]
    first = [copy(0, k, c_ref, call_ref.at[me]) for k in range(1, N_DEV)]
    for cp in first:
        cp.start()
    for cp in first:
        cp.wait()
    wb = w_ref[...].astype(BF16)
    for b in range(N_DEV):
        row = jnp.broadcast_to(call_ref[b], (8, D_MODEL)).astype(BF16)
        mine_ref[b] = _sub_row(_dot(row, wb), 0)
    adag_ref[me] = mine_ref[...]
    second = [copy(1, k, mine_ref, adag_ref.at[me]) for k in range(1, N_DEV)]
    for cp in second:
        cp.start()
    for cp in second:
        cp.wait()


def _gather_weights_and_ada(wt_shard, c_row, w_ada_sh):
    any_spec = pl.BlockSpec(memory_space=pl.ANY)
    vm = pl.BlockSpec(memory_space=pltpu.VMEM)

    def body(w_in_ref, c_ref, wada_ref, out_ref, call_ref, adag_ref, mine_ref, send_sems, recv_sems, local_sem,
             ada_send, ada_recv):
        x, y, c, me = _mesh_pos()
        sibling = (x, y, 1 - c)
        chips = [(1 - x, y), (x, 1 - y), (1 - x, 1 - y)]

        def copy(k, src, blk, to):
            return pltpu.make_async_remote_copy(
                src_ref=src, dst_ref=out_ref.at[blk], send_sem=send_sems.at[k], recv_sem=recv_sems.at[k],
                device_id=to, device_id_type=pl.DeviceIdType.MESH)

        local = pltpu.make_async_copy(w_in_ref, out_ref.at[me], local_sem.at[0])
        local.start()
        first = [copy(0, w_in_ref, me, sibling)]
        first += [copy(1 + j, w_in_ref, me, (px, py, c)) for j, (px, py) in enumerate(chips)]
        for cp in first:
            cp.start()
        _ada_phase(c_ref, wada_ref, call_ref, adag_ref, mine_ref, ada_send, ada_recv)
        passed = []
        for j, (px, py) in enumerate(chips):
            blk = 4 * px + 2 * py + c
            copy(1 + j, w_in_ref, blk, (x, y, c)).wait_recv()
            fwd = copy(4 + j, out_ref.at[blk], blk, sibling)
            fwd.start()
            passed.append(fwd)
        copy(0, w_in_ref, 4 * x + 2 * y + 1 - c, (x, y, c)).wait_recv()
        for j, (px, py) in enumerate(chips):
            copy(4 + j, w_in_ref, 4 * px + 2 * py + 1 - c, (x, y, c)).wait_recv()
        for cp in first + passed:
            cp.wait_send()
        local.wait()

    per = N_DEV - 1
    return pl.pallas_call(
        body, name="gather_weights",
        out_shape=[jax.ShapeDtypeStruct((N_DEV,) + wt_shard.shape, wt_shard.dtype),
                   jax.ShapeDtypeStruct((N_DEV, 1, D_MODEL), F32),
                   jax.ShapeDtypeStruct((N_DEV, N_DEV, 1, ADA_SHARD), F32)],
        in_specs=[any_spec, vm, vm], out_specs=[any_spec, vm, vm],
        scratch_shapes=[pltpu.VMEM((N_DEV, 1, ADA_SHARD), F32),
                        pltpu.SemaphoreType.DMA((per,)), pltpu.SemaphoreType.DMA((per,)),
                        pltpu.SemaphoreType.DMA((1,)),
                        pltpu.SemaphoreType.DMA((2 * per,)), pltpu.SemaphoreType.DMA((2 * per,))],
        compiler_params=pltpu.CompilerParams(vmem_limit_bytes=VMEM_LIMIT),
    )(wt_shard, c_row, w_ada_sh)


def _sibling_swap_sum(slabs, name):
    _, nch, rows, cols = slabs.shape
    any_spec = pl.BlockSpec(memory_space=pl.ANY)

    def body(in_ref, out_ref, mine_v, theirs_v, sum_v, send_sems, recv_sems, load_sems, store_sems):
        x, y, c, _ = _mesh_pos()
        swaps = [pltpu.make_async_remote_copy(
            src_ref=in_ref.at[1 - c, ch], dst_ref=theirs_v.at[ch], send_sem=send_sems.at[ch],
            recv_sem=recv_sems.at[ch], device_id=(x, y, 1 - c), device_id_type=pl.DeviceIdType.MESH)
            for ch in range(nch)]
        loads = [pltpu.make_async_copy(in_ref.at[c, ch], mine_v.at[ch], load_sems.at[ch]) for ch in range(nch)]
        stores = [pltpu.make_async_copy(sum_v.at[ch], out_ref.at[ch], store_sems.at[ch]) for ch in range(nch)]
        for cp in swaps + loads:
            cp.start()
        for ch in range(nch):
            loads[ch].wait()
            swaps[ch].wait()
            sum_v[ch] = (mine_v[ch].astype(F32) + theirs_v[ch].astype(F32)).astype(BF16)
            stores[ch].start()
        for cp in stores:
            cp.wait()

    return pl.pallas_call(
        body, name=name,
        out_shape=jax.ShapeDtypeStruct((nch, rows, cols), BF16),
        in_specs=[any_spec], out_specs=any_spec,
        scratch_shapes=[pltpu.VMEM((nch, rows, cols), BF16)] * 3 + [pltpu.SemaphoreType.DMA((nch,))] * 4,
        compiler_params=_params(),
    )(slabs)


def _all_to_all_copies(ins, outs, send_sems, recv_sems, local_sems):
    x, y, c, me = _mesh_pos()
    copies = []
    for a in range(len(ins)):
        copies.append(pltpu.make_async_copy(ins[a].at[me], outs[a].at[me], local_sems.at[a]))
        for k in range(1, N_DEV):
            dev, p = _peer(k, x, y, c)
            copies.append(pltpu.make_async_remote_copy(
                src_ref=ins[a].at[p], dst_ref=outs[a].at[me],
                send_sem=send_sems.at[a * (N_DEV - 1) + k - 1], recv_sem=recv_sems.at[a * (N_DEV - 1) + k - 1],
                device_id=dev, device_id_type=pl.DeviceIdType.MESH))
    return copies


def _chip_copies(ins, outs, send_sems, recv_sems, local_sems):
    x, y, c, _ = _mesh_pos()
    my_chip = 2 * x + y
    chips = [(1 - x, y), (x, 1 - y), (1 - x, 1 - y)]
    copies = []
    for a in range(len(ins)):
        copies.append(pltpu.make_async_copy(ins[a].at[my_chip], outs[a].at[my_chip], local_sems.at[a]))
        for j, (px, py) in enumerate(chips):
            copies.append(pltpu.make_async_remote_copy(
                src_ref=ins[a].at[2 * px + py], dst_ref=outs[a].at[my_chip],
                send_sem=send_sems.at[a * 3 + j], recv_sem=recv_sems.at[a * 3 + j],
                device_id=(px, py, c), device_id_type=pl.DeviceIdType.MESH))
    return copies


def _proj_fwd(x, ada3, norm_g, w_all_t, bf_pad, qg, kg, later):
    s = x.shape[0]
    tm, tn = min(TM_PROJ, s), TN_PROJ
    nt = s // tm
    n = len(later)

    def body(x_ref, ada_ref, g_ref, wt_ref, bf_ref, qg_ref, kg_ref, *rest):
        ins, (proj_ref, fl_ref, h_ref), rows_refs, rest = rest[:n], rest[n:n + 3], rest[n + 3:n + 8], rest[n + 8:]
        outs, (carry, send_sems, recv_sems, local_sems) = rest[:n], rest[n:]
        i = pl.program_id(0)

        @pl.when(i == 0)
        def _():
            carry[...] = jnp.zeros_like(carry)
            for cp in _gather_copies(ins, outs, send_sems, recv_sems, local_sems):
                cp.start()

        xv = x_ref[...]
        r = lax.rsqrt(jnp.mean(xv * xv, axis=-1, keepdims=True) + EPS)
        hv = ((xv * r) * g_ref[...]) * (1.0 + ada_ref[1:2, :]) + ada_ref[0:1, :]
        hb = hv.astype(BF16)
        h_ref[...] = hb
        fl = _dot_nt(hb, wt_ref[N_MAIN:N_ALL, :])
        fl_ref[...] = fl
        for j in range(N_MAIN // tn):
            proj_ref[:, j * tn:(j + 1) * tn] = _dot_nt(hb, wt_ref[j * tn:(j + 1) * tn, :]).astype(BF16)
        _attention_rows(proj_ref, fl, bf_ref, qg_ref, kg_ref, carry, *rows_refs)

        @pl.when(i == nt - 1)
        def _():
            for cp in _gather_copies(ins, outs, send_sems, recv_sems, local_sems):
                cp.wait()

    any_spec = pl.BlockSpec(memory_space=pl.ANY)
    heads = pl.BlockSpec((HEADS, tm, LANES), lambda i: (0, i, 0))
    heads_t = pl.BlockSpec((HEADS, LANES, tm), lambda i: (0, 0, i))
    vec = pl.BlockSpec((1, ATTN_W), lambda i: (0, 0))
    res = pl.pallas_call(
        body, name="proj_fwd", grid=(nt,),
        in_specs=[pl.BlockSpec((tm, D_MODEL), lambda i: (i, 0)),
                  pl.BlockSpec((3, D_MODEL), lambda i: (0, 0)),
                  pl.BlockSpec((1, D_MODEL), lambda i: (0, 0)),
                  pl.BlockSpec((N_ALL, D_MODEL), lambda i: (0, 0)),
                  pl.BlockSpec((1, LANES), lambda i: (0, 0)), vec, vec] + [any_spec] * n,
        out_specs=[pl.BlockSpec((tm, N_MAIN), lambda i: (i, 0)),
                   pl.BlockSpec((tm, N_FPAD), lambda i: (i, 0)),
                   pl.BlockSpec((tm, D_MODEL), lambda i: (i, 0)),
                   heads, heads, heads, heads_t, heads_t] + [any_spec] * n,
        out_shape=[jax.ShapeDtypeStruct((s, N_MAIN), BF16),
                   jax.ShapeDtypeStruct((s, N_FPAD), F32),
                   jax.ShapeDtypeStruct((s, D_MODEL), BF16)]
        + [jax.ShapeDtypeStruct((HEADS, s, LANES), BF16)] * 3
        + [jax.ShapeDtypeStruct((HEADS, LANES, s), BF16)] * 2
        + [jax.ShapeDtypeStruct((N_DEV,) + a.shape, a.dtype) for a in later],
        scratch_shapes=[pltpu.VMEM((1, LANES), F32)] + _gather_sems(n),
        compiler_params=_params(("arbitrary",)),
    )(x, ada3, norm_g, w_all_t, bf_pad, qg, kg, *later)
    return res[:3], res[3:8], res[8:]


L_ONE_Q, L_F_Q, L_LSE_Q, L_END = HEAD_DIM, HEAD_DIM + 3, HEAD_DIM + 6, HEAD_DIM + 9


def _split3(f):
    hi = f.astype(BF16).astype(F32)
    r = f - hi
    mid = r.astype(BF16).astype(F32)
    return hi, mid, r - mid


def _place3(lane, first, parts, otherwise):
    a, b, c = parts
    return jnp.where(lane == first, a, jnp.where(lane == first + 1, b, jnp.where(lane == first + 2, c, otherwise)))


def _log_forget(fl, bf):
    z = fl + bf
    lf = jnp.minimum(z, 0.0) - jnp.log1p(jnp.exp(-jnp.abs(z)))
    lane = lax.broadcasted_iota(jnp.int32, z.shape, 1)
    return jnp.where(lane < HEADS, lf, 0.0)


def _attention_rows(p_ref, fl, bf_ref, qg_ref, kg_ref, carry, qa_ref, ka_ref, va_ref, kt_ref, vt_ref):
    tm = fl.shape[0]
    scale = HEAD_DIM ** -0.5
    tri = (lax.broadcasted_iota(jnp.int32, (tm, tm), 1) <= lax.broadcasted_iota(jnp.int32, (tm, tm), 0)).astype(F32)
    cum_v = jnp.dot(tri, _log_forget(fl, bf_ref[...]), preferred_element_type=F32,
                    precision=lax.Precision.HIGHEST) + carry[...]
    carry[...] = _sub_row(cum_v, tm - 1)
    lane = lax.broadcasted_iota(jnp.int32, (tm, LANES), 1)
    lo = lane < HEAD_DIM
    v_tail = jnp.where(lane < L_F_Q, 1.0, 0.0)
    for pr in range(ATTN_W // LANES):
        sl = slice(pr * LANES, (pr + 1) * LANES)
        q2 = p_ref[:, OFF_Q + pr * LANES:OFF_Q + (pr + 1) * LANES].astype(F32)
        k2 = p_ref[:, OFF_K + pr * LANES:OFF_K + (pr + 1) * LANES].astype(F32)
        v2 = p_ref[:, OFF_V + pr * LANES:OFF_V + (pr + 1) * LANES].astype(F32)
        rq = lax.rsqrt(_seg_sum(q2 * q2, lo) * (1.0 / HEAD_DIM) + EPS)
        rk = lax.rsqrt(_seg_sum(k2 * k2, lo) * (1.0 / HEAD_DIM) + EPS)
        qn = ((q2 * rq) * qg_ref[:, sl]) * scale
        kn = (k2 * rk) * kg_ref[:, sl]
        for hh in range(2):
            h = 2 * pr + hh
            f3 = _split3(_lane_col(cum_v, h))
            qh = qn if hh == 0 else pltpu.roll(qn, HEAD_DIM, 1)
            kh = kn if hh == 0 else pltpu.roll(kn, HEAD_DIM, 1)
            vh = v2 if hh == 0 else pltpu.roll(v2, HEAD_DIM, 1)
            q_tail = jnp.where(lane < L_F_Q, 1.0, _place3(lane, L_F_Q, f3, 0.0))
            k_tail = _place3(lane, L_ONE_Q, tuple(-f for f in f3), jnp.where(lane < L_END, 1.0, 0.0))
            k_row = jnp.where(lo, kh, k_tail)
            v_row = jnp.where(lo, vh, v_tail)
            qa_ref[h] = jnp.where(lo, qh, q_tail).astype(BF16)
            ka_ref[h] = k_row.astype(BF16)
            va_ref[h] = v_row.astype(BF16)
            kt_ref[h] = k_row.T.astype(BF16)
            vt_ref[h] = v_row.T.astype(BF16)


def _causal_t(t):
    return lax.broadcasted_iota(jnp.int32, (t, t), 0) <= lax.broadcasted_iota(jnp.int32, (t, t), 1)


def _tri_steps(nt, q_major):
    if q_major:
        pairs = [(i, j) for i in range(nt) for j in range(i + 1)]
    else:
        pairs = [(i, j) for j in range(nt) for i in range(j, nt)]
    return (jnp.asarray(np.array([p[0] for p in pairs], np.int32)),
            jnp.asarray(np.array([p[1] for p in pairs], np.int32)))


def _attn_fwd(qa, ka, vt, proj):
    s = qa.shape[1]
    t = min(TQ, s)
    it, jt = _tri_steps(s // t, True)
    hp = HEADS_PER_STEP
    wide = hp * HEAD_DIM
    za_blk = OFF_ZA // wide

    def body(it_ref, jt_ref, q_ref, k_ref, vt_ref, za_ref, attn_ref, oa_ref, qb_ref, m_s, acc_s, pair_s):
        step = pl.program_id(1)
        i, j = it_ref[step], jt_ref[step]

        @pl.when(j == 0)
        def _():
            m_s[...] = jnp.full_like(m_s, NEG)
            acc_s[...] = jnp.zeros_like(acc_s)

        def update(masked):
            for hh in range(hp):
                st = _dot_nt(k_ref[hh], q_ref[hh])
                if masked:
                    st = jnp.where(_causal_t(t), st, NEG)
                m_prev = m_s[hh]
                m_next = jnp.maximum(m_prev, jnp.max(st, axis=0, keepdims=True))
                alpha = jnp.exp(m_prev - m_next)
                pt = jnp.exp(st - m_next).astype(BF16)
                acc_s[hh] = acc_s[hh] * alpha + _dot(vt_ref[hh], pt)
                m_s[hh] = m_next

        @pl.when(j < i)
        def _():
            update(False)

        @pl.when(j == i)
        def _():
            update(True)
            row = lax.broadcasted_iota(jnp.int32, (LANES, t), 0)
            lane = lax.broadcasted_iota(jnp.int32, (t, LANES), 1)
            for hh in range(hp):
                l_row = acc_s[hh, L_ONE_Q:L_ONE_Q + 1, :]
                pair_s[hh * HEAD_DIM:(hh + 1) * HEAD_DIM, :] = acc_s[hh, 0:HEAD_DIM, :] / l_row
                lse3 = _split3(m_s[hh] + jnp.log(l_row))
                tail_t = _place3(row, L_LSE_Q, tuple(-x for x in lse3), 0.0)
                keep_q = jnp.logical_or(lane < L_LSE_Q, lane >= L_END)
                qb_ref[hh] = jnp.where(keep_q, q_ref[hh].astype(F32), tail_t.T).astype(BF16)
            out = pair_s[...].T
            attn_ref[...] = out
            z = za_ref[...].astype(F32)
            oa_ref[...] = (out * (z * _sigmoid(z))).astype(BF16)

    pair_q = pl.BlockSpec((hp, t, LANES), lambda p, n, it_, jt_: (p, it_[n], 0))
    pair_k = pl.BlockSpec((hp, t, LANES), lambda p, n, it_, jt_: (p, jt_[n], 0))
    pair_kt = pl.BlockSpec((hp, LANES, t), lambda p, n, it_, jt_: (p, 0, jt_[n]))
    out_q = pl.BlockSpec((t, wide), lambda p, n, it_, jt_: (it_[n], p))
    return pl.pallas_call(
        body, name="attn_fwd",
        grid_spec=pltpu.PrefetchScalarGridSpec(
            num_scalar_prefetch=2, grid=(HEADS // hp, it.shape[0]),
            in_specs=[pair_q, pair_k, pair_kt,
                      pl.BlockSpec((t, wide), lambda p, n, it_, jt_: (it_[n], za_blk + p))],
            out_specs=[out_q, out_q, pair_q],
            scratch_shapes=[pltpu.VMEM((hp, 1, t), F32), pltpu.VMEM((hp, LANES, t), F32),
                            pltpu.VMEM((wide, t), F32)]),
        out_shape=[jax.ShapeDtypeStruct((s, ATTN_W), F32),
                   jax.ShapeDtypeStruct((s, ATTN_W), BF16),
                   jax.ShapeDtypeStruct((HEADS, s, LANES), BF16)],
        compiler_params=_params(("parallel", "arbitrary")),
    )(it, jt, qa, ka, vt, proj)


def _conv_parts(gb_ref, gc_ref, u_ref, zb_ref, gch_ref, uh_ref, first, w_ref, tm):
    gb, gc = gb_ref[...].astype(F32), gc_ref[...].astype(F32)
    u, zb = u_ref[...].astype(F32), zb_ref[...].astype(F32)
    cu = gc * u
    cu_h = jnp.where(first, 0.0, gch_ref[...].astype(F32) * uh_ref[...].astype(F32))
    prev1, prev2 = _sub_row(cu_h, HALO - 1), _sub_row(cu_h, HALO - 2)
    row = lax.broadcasted_iota(jnp.int32, cu.shape, 0)
    r1 = jnp.where(row == 0, prev1, pltpu.roll(cu, 1, 0))
    r2 = jnp.where(row == 0, prev2, jnp.where(row == 1, prev1, pltpu.roll(cu, 2, 0)))
    conv = w_ref[2:3, :] * cu + w_ref[1:2, :] * r1 + w_ref[0:1, :] * r2
    return gb, gc, u, zb, cu, r1, r2, conv


def _conv_specs(tm, s, width=LANES):
    def tile(off):
        return pl.BlockSpec((tm, width), lambda c, i: (i, off // width + c))

    def before(off):
        return pl.BlockSpec((HALO, width), lambda c, i: (jnp.maximum(i * (tm // HALO) - 1, 0), off // width + c))

    def after(off):
        return pl.BlockSpec((HALO, width),
                            lambda c, i: (jnp.minimum((i + 1) * (tm // HALO), s // HALO - 1), off // width + c))

    return ([tile(OFF_CB), tile(OFF_CC), tile(OFF_CU), tile(OFF_CZ)], [before(OFF_CC), before(OFF_CU)],
            [after(OFF_CB), after(OFF_CZ)])


def _tail(oa, attn, proj, x, target, ada3, wa, wb, wo, conv_w):
    s = x.shape[0]
    tm = min(TM_TAIL, s)
    gab_blk = OFF_GA // (2 * D_MODEL)
    za_blk = OFF_ZA // ATTN_W
    tiles, befores, _ = _conv_specs(tm, s, CONV_W)

    def body(oa_ref, attn_ref, za_ref, gb_ref, gc_ref, u_ref, zb_ref, gch_ref, uh_ref, cw_ref, gab_ref, x_ref, t_ref,
             ada_ref, wa_ref, wb_ref, wo_ref,
             dy_ref, dgab_ref, do_ref, dza_ref, dob_ref, dwo_ref, dwa_ref, dwb_ref, dgate_ref, loss_ref):
        first = pl.program_id(0) == 0

        @pl.when(first)
        def _():
            dwo_ref[...] = jnp.zeros_like(dwo_ref)
            dwa_ref[...] = jnp.zeros_like(dwa_ref)
            dwb_ref[...] = jnp.zeros_like(dwb_ref)
            dgate_ref[...] = jnp.zeros_like(dgate_ref)
            loss_ref[...] = jnp.zeros_like(loss_ref)

        gb, _, _, zb, _, _, _, conv = _conv_parts(gb_ref, gc_ref, u_ref, zb_ref, gch_ref, uh_ref, first, cw_ref, tm)
        ob_v = (gb * conv * (zb * _sigmoid(zb))).astype(BF16)
        oa_v = oa_ref[...]
        wa_v, wb_v, wo_v = wa_ref[...], wb_ref[...], wo_ref[...]
        a2 = _dot(oa_v, wa_v)
        b2 = _dot(ob_v, wb_v)
        sa = _sigmoid(gab_ref[:, 0:D_MODEL].astype(F32))
        sb = _sigmoid(gab_ref[:, D_MODEL:2 * D_MODEL].astype(F32))
        mb = (sa * a2 + sb * b2).astype(BF16)
        mo = _dot(mb, wo_v)
        gate = ada_ref[2:3, :]
        err = (x_ref[...] + gate * mo) - t_ref[...]
        dy = err * (1.0 / D_MODEL)
        dy_ref[...] = dy
        loss_ref[...] += 0.5 * jnp.sum(err * err) * (1.0 / D_MODEL)
        dgate_ref[...] += jnp.sum(dy * mo, axis=0, keepdims=True)
        dmo = (dy * gate).astype(BF16)
        dmerged = _dot_nt(dmo, wo_v)
        dwo_ref[...] += _dot_tn(mb, dmo)
        da2 = (dmerged * sa).astype(BF16)
        db2 = (dmerged * sb).astype(BF16)
        dgab_ref[:, 0:D_MODEL] = (dmerged * a2 * (sa * (1.0 - sa))).astype(BF16)
        dgab_ref[:, D_MODEL:2 * D_MODEL] = (dmerged * b2 * (sb * (1.0 - sb))).astype(BF16)
        doa = _dot_nt(da2, wa_v)
        dob_ref[...] = _dot_nt(db2, wb_v)
        dwa_ref[...] += _dot_tn(oa_v, da2)
        dwb_ref[...] += _dot_tn(ob_v, db2)

        lane = lax.broadcasted_iota(jnp.int32, (tm, LANES), 1)
        lo = lane < HEAD_DIM
        for pr in range(ATTN_W // LANES):
            sl = slice(pr * LANES, (pr + 1) * LANES)
            g, a, z = doa[:, sl], attn_ref[:, sl], za_ref[:, sl].astype(F32)
            sg = _sigmoid(z)
            dat = (g * (z * sg)).astype(BF16).astype(F32)
            prod = dat * a
            dza_ref[:, sl] = (g * a * (sg * (1.0 + z * (1.0 - sg)))).astype(BF16)
            for hh in range(2):
                sel = lo if hh == 0 else jnp.logical_not(lo)
                delta3 = _split3(jnp.sum(jnp.where(sel, prod, 0.0), axis=-1, keepdims=True))
                dh = dat if hh == 0 else pltpu.roll(dat, HEAD_DIM, 1)
                tail_lanes = _place3(lane, L_ONE_Q, tuple(-d for d in delta3), 0.0)
                do_ref[2 * pr + hh] = jnp.where(lo, dh, tail_lanes).astype(BF16)

    half = pl.BlockSpec((tm, ATTN_W), lambda i: (i, 0))
    full = pl.BlockSpec((tm, D_MODEL), lambda i: (i, 0))

    def const(shape):
        return pl.BlockSpec(shape, lambda i: (0, 0))

    def one_axis(spec):
        return pl.BlockSpec(spec.block_shape, lambda i, f=spec.index_map: f(0, i))

    return pl.pallas_call(
        body, name="tail", grid=(s // tm,),
        in_specs=[half, half, pl.BlockSpec((tm, ATTN_W), lambda i: (i, za_blk))]
        + [one_axis(sp) for sp in tiles + befores]
        + [const((3, CONV_W)), pl.BlockSpec((tm, 2 * D_MODEL), lambda i: (i, gab_blk)), full, full,
           const((3, D_MODEL)), const((ATTN_W, D_MODEL)), const((CONV_W, D_MODEL)), const((D_MODEL, D_MODEL))],
        out_specs=[full, pl.BlockSpec((tm, 2 * D_MODEL), lambda i: (i, 0)),
                   pl.BlockSpec((HEADS, tm, LANES), lambda i: (0, i, 0)), half, half,
                   const((D_MODEL, D_MODEL)), const((ATTN_W, D_MODEL)), const((CONV_W, D_MODEL)),
                   const((1, D_MODEL)), const((1, LANES))],
        out_shape=[jax.ShapeDtypeStruct((s, D_MODEL), F32),
                   jax.ShapeDtypeStruct((s, 2 * D_MODEL), BF16),
                   jax.ShapeDtypeStruct((HEADS, s, LANES), BF16),
                   jax.ShapeDtypeStruct((s, ATTN_W), BF16),
                   jax.ShapeDtypeStruct((s, CONV_W), F32),
                   jax.ShapeDtypeStruct((D_MODEL, D_MODEL), F32),
                   jax.ShapeDtypeStruct((ATTN_W, D_MODEL), F32),
                   jax.ShapeDtypeStruct((CONV_W, D_MODEL), F32),
                   jax.ShapeDtypeStruct((1, D_MODEL), F32),
                   jax.ShapeDtypeStruct((1, LANES), F32)],
        compiler_params=_params(("arbitrary",)),
    )(oa, attn, proj, *([proj] * 6), conv_w, proj, x, target, ada3, wa, wb, wo)


def _attn_bwd(qb, ka, kt, va, do, proj, qg, kg):
    s = qb.shape[1]
    t = min(TQ, s)
    nt = s // t
    hp = HEADS_PER_STEP
    assert hp == HEADS, "all heads share one (S, 1536) output block"
    wide = hp * HEAD_DIM
    scale = HEAD_DIM ** -0.5
    it, jt = _tri_steps(nt, False)

    def body(it_ref, jt_ref, q_ref, k_ref, kt_ref, v_ref, do_ref, qraw_ref, kraw_ref, qg_ref, kg_ref,
             dqkv_ref, dqg_ref, dkg_ref, dcum_ref, dqt_s, dk_s, dv_s, rows_s):
        grp, step = pl.program_id(0), pl.program_id(1)
        i, j = it_ref[step], jt_ref[step]
        lane = lax.broadcasted_iota(jnp.int32, (t, LANES), 1)
        lo = lane < HEAD_DIM

        @pl.when(step == 0)
        def _():
            dqt_s[...] = jnp.zeros_like(dqt_s)
            dqg_ref[...] = jnp.zeros_like(dqg_ref)
            dkg_ref[...] = jnp.zeros_like(dkg_ref)

        @pl.when(i == j)
        def _():
            dk_s[...] = jnp.zeros_like(dk_s)
            dv_s[...] = jnp.zeros_like(dv_s)

        def update(masked):
            for hh in range(hp):
                qh, doh = q_ref[hh], do_ref[hh]
                st = _dot_nt(k_ref[hh], qh)
                if masked:
                    st = jnp.where(_causal_t(t), st, NEG)
                pt = jnp.exp(st)
                dst = (pt * _dot_nt(v_ref[hh], doh)).astype(BF16)
                dv_s[hh] += _dot(pt.astype(BF16), doh)
                dk_s[hh] += _dot(dst, qh)
                dqt_s[hh, i] += _dot(kt_ref[hh], dst)

        def pair(a, b):
            return jnp.where(lo, a, pltpu.roll(b, HEAD_DIM, 1))

        def norm_bwd(raw, dy, g, dg_ref, off, sl):
            r = lax.rsqrt(_seg_sum(raw * raw, lo) * (1.0 / HEAD_DIM) + EPS)
            xhat = raw * r
            dg_ref[:, sl] += jnp.sum(dy * xhat, axis=0, keepdims=True)
            dxh = dy * g
            dx = r * (dxh - xhat * (_seg_sum(dxh * xhat, lo) * (1.0 / HEAD_DIM)))
            dqkv_ref[:, off + sl.start:off + sl.stop] = dx.astype(BF16)

        @pl.when(i > j)
        def _():
            update(False)

        @pl.when(i == j)
        def _():
            update(True)
            dq_rows = [dqt_s[hh, i].T for hh in range(hp)]
            rows = jnp.zeros((t, LANES), F32)
            for hh in range(hp):
                rows = jnp.where(lane == grp * hp + hh, _lane_col(dq_rows[hh], L_F_Q), rows)
            rows_s[...] = rows
            for pr in range(hp // 2):
                sl = slice(pr * LANES, (pr + 1) * LANES)
                norm_bwd(qraw_ref[:, sl].astype(F32), pair(dq_rows[2 * pr], dq_rows[2 * pr + 1]) * scale,
                         qg_ref[:, sl], dqg_ref, OFF_Q, sl)

        @pl.when(i == nt - 1)
        def _():
            dcum = rows_s[...]
            for hh in range(hp):
                dcum = jnp.where(lane == grp * hp + hh, dcum - _lane_col(dk_s[hh], L_ONE_Q), dcum)
            dcum_ref[0] = dcum
            for pr in range(hp // 2):
                sl = slice(pr * LANES, (pr + 1) * LANES)
                norm_bwd(kraw_ref[:, sl].astype(F32), pair(dk_s[2 * pr], dk_s[2 * pr + 1]),
                         kg_ref[:, sl], dkg_ref, OFF_K, sl)
                dqkv_ref[:, OFF_V + sl.start:OFF_V + sl.stop] = pair(dv_s[2 * pr], dv_s[2 * pr + 1]).astype(BF16)

    pair_q = pl.BlockSpec((hp, t, LANES), lambda p, n, it_, jt_: (p, it_[n], 0))
    pair_k = pl.BlockSpec((hp, t, LANES), lambda p, n, it_, jt_: (p, jt_[n], 0))
    pair_kt = pl.BlockSpec((hp, LANES, t), lambda p, n, it_, jt_: (p, 0, jt_[n]))
    tok3 = pl.BlockSpec((t, 3 * ATTN_W), lambda p, n, it_, jt_: (jt_[n], 0))
    gain = pl.BlockSpec((1, wide), lambda p, n, it_, jt_: (0, p))
    return pl.pallas_call(
        body, name="attn_bwd",
        grid_spec=pltpu.PrefetchScalarGridSpec(
            num_scalar_prefetch=2, grid=(HEADS // hp, it.shape[0]),
            in_specs=[pair_q, pair_k, pair_kt, pair_k, pair_q,
                      pl.BlockSpec((t, wide), lambda p, n, it_, jt_: (jt_[n], OFF_Q // wide + p)),
                      pl.BlockSpec((t, wide), lambda p, n, it_, jt_: (jt_[n], OFF_K // wide + p)), gain, gain],
            out_specs=[tok3, gain, gain,
                       pl.BlockSpec((1, t, LANES), lambda p, n, it_, jt_: (p, jt_[n], 0))],
            scratch_shapes=[pltpu.VMEM((hp, nt, LANES, t), F32), pltpu.VMEM((hp, t, LANES), F32),
                            pltpu.VMEM((hp, t, LANES), F32), pltpu.VMEM((t, LANES), F32)]),
        out_shape=[jax.ShapeDtypeStruct((s, 3 * ATTN_W), BF16)]
        + [jax.ShapeDtypeStruct((1, ATTN_W), F32)] * 2
        + [jax.ShapeDtypeStruct((HEADS // hp, s, LANES), F32)],
        compiler_params=_params(("parallel", "arbitrary")),
    )(it, jt, qb, ka, kt, va, do, proj, proj, qg, kg)


def _forget_bwd(dcum, fl, bf_pad):
    s = fl.shape[0]
    tc = min(TC_CUM, s)
    n = s // tc

    def body(dc_ref, fl_ref, bf_ref, df_ref, dbf_ref, carry):
        @pl.when(pl.program_id(0) == 0)
        def _():
            carry[...] = jnp.zeros_like(carry)
            dbf_ref[...] = jnp.zeros_like(dbf_ref)
        r = lax.broadcasted_iota(jnp.int32, (tc, tc), 0)
        cidx = lax.broadcasted_iota(jnp.int32, (tc, tc), 1)
        tri = (cidx >= r).astype(F32)
        dc = dc_ref[0]
        for grp in range(1, dcum.shape[0]):
            dc = dc + dc_ref[grp]
        dlf = jnp.dot(tri, dc, preferred_element_type=F32, precision=lax.Precision.HIGHEST) + carry[...]
        carry[...] += jnp.sum(dc, axis=0, keepdims=True)
        lane = lax.broadcasted_iota(jnp.int32, (tc, LANES), 1)
        dfl = jnp.where(lane < HEADS, dlf * _sigmoid(-(fl_ref[...] + bf_ref[...])), 0.0)
        df_ref[...] = dfl.astype(BF16)
        dbf_ref[...] += jnp.sum(dfl, axis=0, keepdims=True)

    rev = pl.BlockSpec((tc, LANES), lambda i: (n - 1 - i, 0))
    vec = pl.BlockSpec((1, LANES), lambda i: (0, 0))
    return pl.pallas_call(
        body, name="forget_bwd", grid=(n,),
        in_specs=[pl.BlockSpec((dcum.shape[0], tc, LANES), lambda i: (0, n - 1 - i, 0)), rev, vec],
        out_specs=[rev, vec],
        out_shape=[jax.ShapeDtypeStruct((s, LANES), BF16), jax.ShapeDtypeStruct((1, LANES), F32)],
        scratch_shapes=[pltpu.VMEM((1, LANES), F32)],
        compiler_params=_params(("arbitrary",)),
    )(dcum, fl, bf_pad)


def _conv_bwd(dob, proj, conv_w):
    s = dob.shape[0]
    tm = min(TM_ELEM, s)
    wd = CONV_W
    tiles, befores, afters = _conv_specs(tm, s, wd)

    def body(dob_ref, dnext_ref, gb_ref, gc_ref, u_ref, zb_ref, gch_ref, uh_ref, gbn_ref, zbn_ref, w_ref,
             out_ref, dw_ref):
        i = pl.program_id(1)

        @pl.when(i == 0)
        def _():
            dw_ref[...] = jnp.zeros_like(dw_ref)
        gb, gc, u, zb, cu, r1, r2, conv = _conv_parts(gb_ref, gc_ref, u_ref, zb_ref, gch_ref, uh_ref, i == 0, w_ref, tm)
        g = dob_ref[...]
        sg = _sigmoid(zb)
        sz = zb * sg
        dconv = g * gb * sz
        zn = zbn_ref[0:8, :].astype(F32)
        dcn = jnp.where(i == pl.num_programs(1) - 1, 0.0,
                        dnext_ref[...] * gbn_ref[0:8, :].astype(F32) * (zn * _sigmoid(zn)))
        nxt1, nxt2 = _sub_row(dcn, 0), _sub_row(dcn, 1)
        row = lax.broadcasted_iota(jnp.int32, (tm, wd), 0)
        f1 = jnp.where(row == tm - 1, nxt1, pltpu.roll(dconv, tm - 1, 0))
        f2 = jnp.where(row == tm - 2, nxt1, jnp.where(row == tm - 1, nxt2, pltpu.roll(dconv, tm - 2, 0)))
        dcu = w_ref[2:3, :] * dconv + w_ref[1:2, :] * f1 + w_ref[0:1, :] * f2
        out_ref[:, 0:wd] = (g * conv * sz).astype(BF16)
        out_ref[:, wd:2 * wd] = (dcu * u).astype(BF16)
        out_ref[:, 2 * wd:3 * wd] = (dcu * gc).astype(BF16)
        out_ref[:, 3 * wd:4 * wd] = (g * gb * conv * (sg * (1.0 + zb * (1.0 - sg)))).astype(BF16)
        w_row = lax.broadcasted_iota(jnp.int32, (3, wd), 0)
        dw0 = jnp.sum(dconv * r2, axis=0, keepdims=True)
        dw1 = jnp.sum(dconv * r1, axis=0, keepdims=True)
        dw2 = jnp.sum(dconv * cu, axis=0, keepdims=True)
        dw_ref[...] += jnp.where(w_row == 0, dw0, jnp.where(w_row == 1, dw1, dw2))

    blk = pl.BlockSpec((tm, wd), lambda c, i: (i, c))
    nxt = pl.BlockSpec((8, wd), lambda c, i: (jnp.minimum((i + 1) * (tm // 8), s // 8 - 1), c))
    wspec = pl.BlockSpec((3, wd), lambda c, i: (0, c))
    return pl.pallas_call(
        body, name="conv_bwd", grid=(CONV_W // wd, s // tm),
        in_specs=[blk, nxt] + tiles + befores + afters + [wspec],
        out_specs=[pl.BlockSpec((tm, 4 * wd), lambda c, i: (i, c)), wspec],
        out_shape=[jax.ShapeDtypeStruct((s, 4 * CONV_W), BF16), jax.ShapeDtypeStruct((3, CONV_W), F32)],
        compiler_params=_params(("parallel", "arbitrary")),
    )(dob, dob, *([proj] * 8), conv_w)


def _piece_layout(pieces):
    offs, off = [], 0
    for p in pieces:
        offs.append((off, p.shape[1]))
        off += p.shape[1]
    assert off == N_ALL, off
    return offs


def _dw_in(h, pieces, chip_sums):
    s = h.shape[0]
    tk, tn = min(TK_DW, s), TN_DW
    nk = s // tk
    nn = N_MAIN // tn
    main, fpiece = pieces[:-1], pieces[-1]
    layout = _piece_layout(pieces)[:-1]
    n_main = len(main)
    nx = len(chip_sums)

    def body(*refs):
        p_refs, f_ref, h_ref = refs[:n_main], refs[n_main], refs[n_main + 1]
        ins, refs = refs[n_main + 2:n_main + 2 + nx], refs[n_main + 2 + nx:]
        out_ref, outf_ref = refs[:2]
        outs, (acc, accf, send_sems, recv_sems, local_sems) = refs[2:2 + nx], refs[2 + nx:]
        n, k = pl.program_id(0), pl.program_id(1)

        @pl.when(jnp.logical_and(n == 0, k == 0))
        def _():
            for cp in _all_to_all_copies(ins, outs, send_sems, recv_sems, local_sems):
                cp.start()

        @pl.when(k == 0)
        def _():
            acc[...] = jnp.zeros_like(acc)
        hv = h_ref[pl.ds(pl.multiple_of(k * tk, tk), tk), :]
        for p_ref, (off, width) in zip(p_refs, layout):
            @pl.when(jnp.logical_and(n >= off // tn, n < (off + width) // tn))
            def _():
                acc[...] += _dot_tn(p_ref[...], hv)

        @pl.when(k == nk - 1)
        def _():
            out_ref[...] = acc[...].astype(BF16)

        @pl.when(n == 0)
        def _():
            @pl.when(k == 0)
            def _():
                accf[...] = jnp.zeros_like(accf)
            accf[...] += _dot_tn(f_ref[...], hv)

            @pl.when(k == nk - 1)
            def _():
                outf_ref[...] = accf[...].astype(BF16)

        @pl.when(jnp.logical_and(n == nn - 1, k == nk - 1))
        def _():
            for cp in _all_to_all_copies(ins, outs, send_sems, recv_sems, local_sems):
                cp.wait()

    def piece_spec(off, width):
        lo, hi = off // tn, (off + width) // tn

        def index(n, k):
            active = jnp.logical_and(n >= lo, n < hi)
            return jnp.where(active, k, 0), jnp.clip(n - lo, 0, hi - lo - 1)
        return pl.BlockSpec((tk, tn), index)

    any_spec = pl.BlockSpec(memory_space=pl.ANY)
    res = pl.pallas_call(
        body, name="dw_in", grid=(nn, nk),
        in_specs=[piece_spec(off, width) for off, width in layout]
        + [pl.BlockSpec((tk, N_FPAD), lambda n, k: (jnp.where(n == 0, k, 0), 0)),
           pl.BlockSpec((s, D_MODEL), lambda n, k: (0, 0))] + [any_spec] * nx,
        out_specs=[pl.BlockSpec((tn, D_MODEL), lambda n, k: (n, 0)),
                   pl.BlockSpec((N_FPAD, D_MODEL), lambda n, k: (0, 0))] + [any_spec] * nx,
        out_shape=[jax.ShapeDtypeStruct((N_MAIN, D_MODEL), BF16), jax.ShapeDtypeStruct((N_FPAD, D_MODEL), BF16)]
        + [jax.ShapeDtypeStruct(a.shape, a.dtype) for a in chip_sums],
        scratch_shapes=[pltpu.VMEM((tn, D_MODEL), F32), pltpu.VMEM((N_FPAD, D_MODEL), F32)] + _gather_sems(nx),
        compiler_params=_params(("arbitrary", "arbitrary")),
    )(*main, fpiece, h, *chip_sums)
    return res[:2], res[2:]


def _dh_and_dx(pieces, w_all_t, x, dy, ada3, norm_g, chip_sums):
    s = x.shape[0]
    tm = min(TM_DH, s)
    nt = s // tm
    n = len(chip_sums)
    npc = len(pieces)
    layout = _piece_layout(pieces)

    def body(*refs):
        p_refs, refs = refs[:npc], refs[npc:]
        wt_ref, x_ref, dy_ref, ada_ref, g_ref = refs[:5]
        ins, refs = refs[5:5 + n], refs[5 + n:]
        gx_ref, dsh_ref, dsc_ref, dg_ref = refs[:4]
        outs, (send_sems, recv_sems, local_sems) = refs[4:4 + n], refs[4 + n:]
        i = pl.program_id(0)

        @pl.when(i == 0)
        def _():
            for cp in _chip_copies(ins, outs, send_sems, recv_sems, local_sems):
                cp.start()
            dsh_ref[...] = jnp.zeros_like(dsh_ref)
            dsc_ref[...] = jnp.zeros_like(dsc_ref)
            dg_ref[...] = jnp.zeros_like(dg_ref)

        dh = None
        for p_ref, (off, width) in zip(p_refs, layout):
            part = _dot(p_ref[...], wt_ref[off:off + width, :])
            dh = part if dh is None else dh + part
        xv = x_ref[...]
        r = lax.rsqrt(jnp.mean(xv * xv, axis=-1, keepdims=True) + EPS)
        xhat = xv * r
        g = g_ref[...]
        one_sc = 1.0 + ada_ref[1:2, :]
        dsh_ref[...] += jnp.sum(dh, axis=0, keepdims=True)
        dsc_ref[...] += jnp.sum(dh * (xhat * g), axis=0, keepdims=True)
        dg_ref[...] += jnp.sum(dh * xhat, axis=0, keepdims=True) * one_sc
        dxh = dh * (g * one_sc)
        dx = r * (dxh - xhat * jnp.mean(dxh * xhat, axis=-1, keepdims=True))
        gx_ref[...] = dy_ref[...] + dx

        @pl.when(i == nt - 1)
        def _():
            for cp in _chip_copies(ins, outs, send_sems, recv_sems, local_sems):
                cp.wait()

    full = pl.BlockSpec((tm, D_MODEL), lambda i: (i, 0))
    vec = pl.BlockSpec((1, D_MODEL), lambda i: (0, 0))
    any_spec = pl.BlockSpec(memory_space=pl.ANY)
    res = pl.pallas_call(
        body, name="dh_dx", grid=(nt,),
        in_specs=[pl.BlockSpec((tm, p.shape[1]), lambda i: (i, 0)) for p in pieces]
        + [pl.BlockSpec((N_ALL, D_MODEL), lambda i: (0, 0)), full, full,
           pl.BlockSpec((3, D_MODEL), lambda i: (0, 0)), vec] + [any_spec] * n,
        out_specs=[full, vec, vec, vec] + [any_spec] * n,
        out_shape=[jax.ShapeDtypeStruct((s, D_MODEL), F32)] + [jax.ShapeDtypeStruct((1, D_MODEL), F32)] * 3
        + [jax.ShapeDtypeStruct(a.shape, a.dtype) for a in chip_sums],
        scratch_shapes=[pltpu.SemaphoreType.DMA((n * 3,)), pltpu.SemaphoreType.DMA((n * 3,)),
                        pltpu.SemaphoreType.DMA((n,))],
        compiler_params=_params(("arbitrary",)),
    )(*pieces, w_all_t, x, dy, ada3, norm_g, *chip_sums)
    return res[:4], res[4:]


def _sum_small(vec_all, qg_parts, kg_parts):
    def body(v_ref, q_ref, k_ref, tot_ref, gq_ref, gk_ref):
        tot = v_ref[0:1, :]
        for p in range(1, N_DEV):
            tot = tot + v_ref[p:p + 1, :]
        tot_ref[...] = tot
        gq_ref[...] = jnp.sum(q_ref[...], axis=0, keepdims=True)
        gk_ref[...] = jnp.sum(k_ref[...], axis=0, keepdims=True)

    n = vec_all.shape[-1]
    return pl.pallas_call(
        body, name="sum_small",
        out_shape=[jax.ShapeDtypeStruct((1, n), F32),
                   jax.ShapeDtypeStruct((1, HEAD_DIM), F32), jax.ShapeDtypeStruct((1, HEAD_DIM), F32)],
        compiler_params=_params(),
    )(vec_all, qg_parts, kg_parts)


def _grad_w_ada(c_cols, dada_rows):
    def body(c_ref, d_ref, out_ref):
        acc = c_ref[0] * d_ref[0]
        for b in range(1, N_DEV):
            acc = acc + c_ref[b] * d_ref[b]
        out_ref[...] = acc

    return pl.pallas_call(
        body, name="grad_w_ada",
        out_shape=jax.ShapeDtypeStruct((D_MODEL, ADA_SHARD), F32),
        compiler_params=_params(),
    )(c_cols, dada_rows)


def _adam_step(w, m, v, g):
    c1 = 1.0 / (1.0 - ADAM_B1 ** ADAM_STEP)
    c2 = 1.0 / (1.0 - ADAM_B2 ** ADAM_STEP)
    m_new = ADAM_B1 * m + (1.0 - ADAM_B1) * g
    v_new = ADAM_B2 * v + (1.0 - ADAM_B2) * (g * g)
    return -ADAM_LR * ((m_new * c1) / (jnp.sqrt(v_new * c2) + ADAM_EPS) + ADAM_WD * w), m_new, v_new


def _adamw_small(params, name):
    n = len(params)
    stacked = [p[3].ndim == p[0].ndim + 1 for p in params]

    def body(*refs):
        ins, outs = refs[:4 * n], refs[4 * n:]
        for k in range(n):
            w_ref, m_ref, v_ref, g_ref = ins[4 * k:4 * k + 4]
            go_ref, d_ref, mo_ref, vo_ref = outs[4 * k:4 * k + 4]
            if stacked[k]:
                g = g_ref[0].astype(F32)
                for p in range(1, g_ref.shape[0]):
                    g = g + g_ref[p].astype(F32)
            else:
                g = g_ref[...]
            go_ref[...] = g
            d_ref[...], mo_ref[...], vo_ref[...] = _adam_step(w_ref[...], m_ref[...], v_ref[...], g)

    res = pl.pallas_call(
        body, name=name,
        out_shape=[jax.ShapeDtypeStruct(p[0].shape, F32) for p in params for _ in range(4)],
        compiler_params=_params(),
    )(*[a for p in params for a in p])
    return [tuple(res[4 * k:4 * k + 4]) for k in range(n)]


def _adamw(w, m, v, g_parts, name):
    rows, cols = w.shape
    n_parts = g_parts.shape[0]
    tr = 256 if rows % 256 == 0 else rows
    tc = 256 if (tr == rows and rows > 256 and cols % 256 == 0) else cols

    def body(w_ref, m_ref, v_ref, g_ref, go_ref, d_ref, mo_ref, vo_ref):
        g = g_ref[0].astype(F32)
        for p in range(1, n_parts):
            g = g + g_ref[p].astype(F32)
        go_ref[...] = g
        d_ref[...], mo_ref[...], vo_ref[...] = _adam_step(w_ref[...], m_ref[...], v_ref[...], g)

    blk = pl.BlockSpec((tr, tc), lambda i, j: (i, j))
    return pl.pallas_call(
        body, name=name, grid=(rows // tr, cols // tc),
        in_specs=[blk, blk, blk, pl.BlockSpec((n_parts, tr, tc), lambda i, j: (0, i, j))],
        out_specs=[blk] * 4,
        out_shape=[jax.ShapeDtypeStruct((rows, cols), F32)] * 4,
        compiler_params=_params(("parallel", "parallel")),
    )(w, m, v, g_parts)


_O_F = 1536


W_TILE = 16
WIN_ROWS = 784


def _internal_start(p):
    return p * IN_SHARD - (HEADS if p * IN_SHARD > _O_F else 0)


def _shard_window(wt_shard, me):
    lo = me * IN_SHARD
    o = lo + lax.broadcasted_iota(jnp.int32, (IN_SHARD, 1), 0)
    is_f = jnp.logical_and(o >= _O_F, o < _O_F + HEADS)
    start = lo - jnp.where(lo > _O_F, HEADS, 0)
    window = lax.dynamic_update_slice(jnp.zeros((WIN_ROWS + W_TILE, D_MODEL), BF16),
                                      jnp.where(is_f, 0.0, wt_shard).astype(BF16), (start % W_TILE, 0))
    first = jnp.clip(_O_F - lo, 0, IN_SHARD - W_TILE)
    near = lax.dynamic_slice(wt_shard, (first, 0), (W_TILE, D_MODEL))
    j = lax.broadcasted_iota(jnp.int32, (W_TILE, 1), 0)
    src = _O_F - lo + j
    ok = jnp.logical_and(j < HEADS, jnp.logical_and(src >= 0, src < IN_SHARD))
    f_tile = jnp.where(ok, jnp.roll(near, first - (_O_F - lo), axis=0), 0.0).astype(BF16)
    return lax.dynamic_update_slice(window, f_tile, (WIN_ROWS, 0))


def _assemble_w(windows):
    chunk = 112

    def body(g_ref, out_ref):
        out_ref[WIN_ROWS:N_MAIN, :] = jnp.zeros((N_MAIN - WIN_ROWS, D_MODEL), BF16)
        for p in range(N_DEV):
            base = _internal_start(p) // W_TILE * W_TILE
            for r in range(0, WIN_ROWS, chunk):
                rows = slice(base + r, base + r + chunk)
                piece = g_ref[p, r:r + chunk, :]
                out_ref[rows, :] = piece if p == 0 else out_ref[rows, :] + piece
        f = g_ref[0, WIN_ROWS:WIN_ROWS + W_TILE, :]
        for p in range(1, N_DEV):
            f = f + g_ref[p, WIN_ROWS:WIN_ROWS + W_TILE, :]
        out_ref[N_MAIN:N_MAIN + W_TILE, :] = f
        out_ref[N_MAIN + W_TILE:N_ALL, :] = jnp.zeros((N_FPAD - W_TILE, D_MODEL), BF16)

    return pl.pallas_call(
        body, name="assemble_w", out_shape=jax.ShapeDtypeStruct((N_ALL, D_MODEL), BF16),
        compiler_params=_params(),
    )(windows)


def _slabs_by_core(dwt, dwt_f):
    sources = ((dwt, 0, _O_F, 0), (dwt_f, _O_F, _O_F + HEADS, _O_F), (dwt, _O_F + HEADS, IN_WIDTH, HEADS))

    def slab(p):
        lo, hi = p * IN_SHARD, (p + 1) * IN_SHARD
        parts = []
        for src, o_lo, o_hi, shift in sources:
            a, b = max(lo, o_lo), min(hi, o_hi)
            if a < b:
                parts.append(src[a - shift:b - shift])
        return parts[0] if len(parts) == 1 else jnp.concatenate(parts, axis=0)

    return jnp.stack([jnp.stack([slab(2 * chip + core) for chip in range(4)]) for core in range(2)])


def kernel(x, c, w_ada, b_ada, norm_g, w_in, b_f, q_norm_g, k_norm_g, conv_w, w_attn_out, w_conv_out, w_o, loss_target, m_w_ada, m_b_ada, m_norm_g, m_w_in, m_b_f, m_q_norm_g, m_k_norm_g, m_conv_w, m_w_attn_out, m_w_conv_out, m_w_o, v_w_ada, v_b_ada, v_norm_g, v_w_in, v_b_f, v_q_norm_g, v_k_norm_g, v_conv_w, v_w_attn_out, v_w_conv_out, v_w_o):
    me = 4 * lax.axis_index("x") + 2 * lax.axis_index("y") + lax.axis_index("c")
    s = x.shape[1]
    x2, t2 = x[0], loss_target[0]

    w_in_g, c_all, ada_g = _gather_weights_and_ada(_shard_window(w_in[0].T, me), c, w_ada[0])
    ada_mine = lax.dynamic_index_in_dim(ada_g[:, :, 0, :], me, axis=1, keepdims=False)
    ada3 = (ada_mine.reshape(1, 3 * D_MODEL) + b_ada).reshape(3, D_MODEL)
    w_all_t = _assemble_w(w_in_g)
    qg = jnp.tile(q_norm_g, (1, HEADS))
    kg = jnp.tile(k_norm_g, (1, HEADS))
    bf_pad = jnp.pad(b_f, ((0, 0), (0, LANES - HEADS)))

    (proj, fl, h), (qa, ka, va, kt, vt), (cw_g, wa_g, wb_g, wo_g) = _proj_fwd(
        x2, ada3, norm_g, w_all_t, bf_pad, qg, kg,
        [conv_w[0], w_attn_out[0].astype(BF16), w_conv_out[0].astype(BF16), w_o[0].astype(BF16)])
    wa = jnp.transpose(wa_g, (1, 0, 2)).reshape(ATTN_W, D_MODEL)
    wb = jnp.transpose(wb_g, (1, 0, 2)).reshape(CONV_W, D_MODEL)
    wo = wo_g.reshape(D_MODEL, D_MODEL)
    cw = jnp.transpose(cw_g, (1, 0, 2)).reshape(3, CONV_W)
    attn, oa, qb = _attn_fwd(qa, ka, vt, proj)
    (dy, dgab, do, dza, dob, dwo, dwa, dwb, dgate, loss_part) = _tail(oa, attn, proj, x2, t2, ada3, wa, wb, wo, cw)

    small = [jnp.transpose(dwa.reshape(ATTN_W, N_DEV, LANES), (1, 0, 2)).astype(BF16),
             jnp.transpose(dwb.reshape(CONV_W, N_DEV, LANES), (1, 0, 2)).astype(BF16),
             dwo.reshape(N_DEV, D_MODEL // N_DEV, D_MODEL).astype(BF16)]
    dqkv, dqg, dkg, dcum = _attn_bwd(qb, ka, kt, va, do, proj, qg, kg)
    df, dbf = _forget_bwd(dcum, fl, bf_pad)
    dconv, dcw = _conv_bwd(dob, proj, cw)
    pieces = [dqkv, dza, dconv, dgab, df]
    (dw_main, dw_f), (g_wa_parts, g_wb_parts, g_wo_parts) = _dw_in(h, pieces, small)

    pair_in = _sibling_swap_sum(_slabs_by_core(dw_main, dw_f), "swap_sum_w_in")
    (grad_x, dshift, dscale, dnormg), (g_in_parts,) = _dh_and_dx(
        pieces, w_all_t, x2, dy, ada3, norm_g, [pair_in])
    vec = jnp.concatenate([dshift, dscale, dgate, dnormg, dbf, dcw.reshape(1, 3 * CONV_W), loss_part, dqg, dkg],
                          axis=1)
    (vec_all,) = _gather_direct([vec], "gather_small")
    vec_all = vec_all.reshape(N_DEV, vec.shape[1])
    n_main = 4 * D_MODEL + LANES + 3 * CONV_W + LANES
    tot, g_qg, g_kg = _sum_small(
        vec_all[:, :n_main],
        vec_all[:, n_main:n_main + ATTN_W].reshape(N_DEV * HEADS, HEAD_DIM),
        vec_all[:, n_main + ATTN_W:].reshape(N_DEV * HEADS, HEAD_DIM))
    g_b_ada = tot[:, 0:3 * D_MODEL]
    g_norm_g = tot[:, 3 * D_MODEL:4 * D_MODEL]
    g_b_f = tot[:, 4 * D_MODEL:4 * D_MODEL + HEADS]
    g_cw_full = tot[:, 4 * D_MODEL + LANES:4 * D_MODEL + LANES + 3 * CONV_W].reshape(3, CONV_W)
    g_cw = lax.dynamic_slice(g_cw_full, (0, me * (CONV_W // N_DEV)), (3, CONV_W // N_DEV))
    dada_mine = lax.dynamic_slice(vec_all[:, 0:3 * D_MODEL], (0, me * ADA_SHARD), (N_DEV, ADA_SHARD))
    g_w_ada = _grad_w_ada(jnp.transpose(c_all, (0, 2, 1)), dada_mine.reshape(N_DEV, 1, ADA_SHARD))

    upd = {}
    upd["w_ada"] = _adamw(w_ada[0], m_w_ada[0], v_w_ada[0], g_w_ada[None], "adamw_w_ada")
    upd["w_in"] = [u.T for u in _adamw(w_in[0].T, m_w_in[0].T, v_w_in[0].T, g_in_parts, "adamw_w_in")]
    small_names = ["b_ada", "norm_g", "b_f", "q_norm_g", "k_norm_g", "conv_w", "w_attn_out", "w_conv_out", "w_o"]
    small_upd = _adamw_small(
        [(b_ada, m_b_ada, v_b_ada, g_b_ada), (norm_g, m_norm_g, v_norm_g, g_norm_g), (b_f, m_b_f, v_b_f, g_b_f),
         (q_norm_g, m_q_norm_g, v_q_norm_g, g_qg), (k_norm_g, m_k_norm_g, v_k_norm_g, g_kg),
         (conv_w[0], m_conv_w[0], v_conv_w[0], g_cw),
         (w_attn_out[0], m_w_attn_out[0], v_w_attn_out[0], g_wa_parts),
         (w_conv_out[0], m_w_conv_out[0], v_w_conv_out[0], g_wb_parts),
         (w_o[0], m_w_o[0], v_w_o[0], g_wo_parts)], "adamw_small")
    upd.update(zip(small_names, small_upd))

    names = ["w_ada", "b_ada", "norm_g", "w_in", "b_f", "q_norm_g", "k_norm_g", "conv_w",
             "w_attn_out", "w_conv_out", "w_o"]
    lead = {"w_ada", "w_in", "conv_w", "w_attn_out", "w_conv_out", "w_o"}
    fix = lambda n, a: a[None] if n in lead else a
    loss = tot[0, n_main - LANES]
    outs = [loss, grad_x[None]]
    for k in range(4):
        outs += [fix(n, upd[n][k]) for n in names]
    return tuple(outs)
```

```python
import numpy as np
import jax
import jax.numpy as jnp
from jax import lax
from jax.experimental import pallas as pl
from jax.experimental.pallas import tpu as pltpu

F32 = jnp.float32
BF16 = jnp.bfloat16

D_MODEL = 1024
HEADS = 8
HEAD_DIM = 64
ATTN_W = 512
CONV_W = 512
N_DEV = 8
IN_WIDTH = 6152
IN_SHARD = IN_WIDTH // N_DEV
N_MAIN = 6144
N_FPAD = 128
N_ALL = N_MAIN + N_FPAD
ADA_SHARD = 3 * D_MODEL // N_DEV
EPS = 1e-6
NEG = -1e30

ADAM_LR = 0.001
ADAM_B1 = 0.9
ADAM_B2 = 0.999
ADAM_EPS = 1e-08
ADAM_WD = 0.01
ADAM_STEP = 10

LANES = 128
VMEM_LIMIT = 56 * 1024 * 1024

TM_PROJ = 512
TN_PROJ = 1024
TM_ELEM = 512
TQ = 512
HEADS_PER_STEP = 8
TM_TAIL = 256
TC_CUM = 512
TK_DW = 2048
TN_DW = 512
TM_DH = 256
HALO = 16

OFF_Q, OFF_K, OFF_V, OFF_ZA, OFF_CB, OFF_CC, OFF_CU, OFF_CZ, OFF_GA, OFF_GB = (
    0, 512, 1024, 1536, 2048, 2560, 3072, 3584, 4096, 5120)


def _params(sem=None):
    return pltpu.CompilerParams(dimension_semantics=sem, vmem_limit_bytes=VMEM_LIMIT)


def _dot(a, b):
    return jnp.dot(a, b, preferred_element_type=F32)


def _dot_nt(a, b):
    return lax.dot_general(a, b, (((1,), (1,)), ((), ())), preferred_element_type=F32)


def _dot_tn(a, b):
    return lax.dot_general(a, b, (((0,), (0,)), ((), ())), preferred_element_type=F32)


def _sigmoid(x):
    return 1.0 / (1.0 + jnp.exp(-x))


def _seg_sum(z, lo):
    a = jnp.sum(jnp.where(lo, z, 0.0), axis=-1, keepdims=True)
    b = jnp.sum(jnp.where(lo, 0.0, z), axis=-1, keepdims=True)
    return jnp.where(lo, a, b)


def _lane_col(z, lane):
    idx = lax.broadcasted_iota(jnp.int32, z.shape, 1)
    return jnp.sum(jnp.where(idx == lane, z, 0.0), axis=-1, keepdims=True)


def _sub_row(z, row):
    idx = lax.broadcasted_iota(jnp.int32, z.shape, 0)
    return jnp.sum(jnp.where(idx == row, z, 0.0), axis=0, keepdims=True)


def _mesh_pos():
    x, y, c = lax.axis_index("x"), lax.axis_index("y"), lax.axis_index("c")
    return x, y, c, 4 * x + 2 * y + c


def _peer(k, x, y, c):
    px = 1 - x if (k >> 2) & 1 else x
    py = 1 - y if (k >> 1) & 1 else y
    pc = 1 - c if k & 1 else c
    return (px, py, pc), 4 * px + 2 * py + pc


def _gather_copies(ins, outs, send_sems, recv_sems, local_sems):
    x, y, c, me = _mesh_pos()
    copies = []
    for a in range(len(ins)):
        copies.append(pltpu.make_async_copy(ins[a], outs[a].at[me], local_sems.at[a]))
        for k in range(1, N_DEV):
            dev, _ = _peer(k, x, y, c)
            copies.append(pltpu.make_async_remote_copy(
                src_ref=ins[a], dst_ref=outs[a].at[me],
                send_sem=send_sems.at[a * (N_DEV - 1) + k - 1], recv_sem=recv_sems.at[a * (N_DEV - 1) + k - 1],
                device_id=dev, device_id_type=pl.DeviceIdType.MESH))
    return copies


def _gather_sems(n):
    return [pltpu.SemaphoreType.DMA((n * (N_DEV - 1),)), pltpu.SemaphoreType.DMA((n * (N_DEV - 1),)),
            pltpu.SemaphoreType.DMA((n,))]


def _gather_direct(arrs, name):
    n = len(arrs)
    any_spec = pl.BlockSpec(memory_space=pl.ANY)

    def body(*refs):
        copies = _gather_copies(refs[:n], refs[n:2 * n], *refs[2 * n:])
        for cp in copies:
            cp.start()
        for cp in copies:
            cp.wait()

    return pl.pallas_call(
        body, name=name, out_shape=[jax.ShapeDtypeStruct((N_DEV,) + a.shape, a.dtype) for a in arrs],
        in_specs=[any_spec] * n, out_specs=[any_spec] * n, scratch_shapes=_gather_sems(n),
    )(*arrs)


def _ada_phase(c_ref, w_ref, call_ref, adag_ref, mine_ref, send_sems, recv_sems):
    x, y, c, me = _mesh_pos()

    def copy(phase, k, src, dst):
        dev, _ = _peer(k, x, y, c)
        return pltpu.make_async_remote_copy(
            src_ref=src, dst_ref=dst,
            send_sem=send_sems.at[phase * (N_DEV - 1) + k - 1],
            recv_sem=recv_sems.at[phase * (N_DEV - 1) + k - 1],
            device_id=dev, device_id_type=pl.DeviceIdType.MESH)

    call_ref[me] = c_ref[...]
    first = [copy(0, k, c_ref, call_ref.at[me]) for k in range(1, N_DEV)]
    for cp in first:
        cp.start()
    for cp in first:
        cp.wait()
    wb = w_ref[...].astype(BF16)
    for b in range(N_DEV):
        row = jnp.broadcast_to(call_ref[b], (8, D_MODEL)).astype(BF16)
        mine_ref[b] = _sub_row(_dot(row, wb), 0)
    adag_ref[me] = mine_ref[...]
    second = [copy(1, k, mine_ref, adag_ref.at[me]) for k in range(1, N_DEV)]
    for cp in second:
        cp.start()
    for cp in second:
        cp.wait()


def _gather_weights_and_ada(wt_shard, c_row, w_ada_sh):
    any_spec = pl.BlockSpec(memory_space=pl.ANY)
    vm = pl.BlockSpec(memory_space=pltpu.VMEM)

    def body(w_in_ref, c_ref, wada_ref, out_ref, call_ref, adag_ref, mine_ref, send_sems, recv_sems, local_sem,
             ada_send, ada_recv):
        x, y, c, me = _mesh_pos()
        sibling = (x, y, 1 - c)
        chips = [(1 - x, y), (x, 1 - y), (1 - x, 1 - y)]

        def copy(k, src, blk, to):
            return pltpu.make_async_remote_copy(
                src_ref=src, dst_ref=out_ref.at[blk], send_sem=send_sems.at[k], recv_sem=recv_sems.at[k],
                device_id=to, device_id_type=pl.DeviceIdType.MESH)

        local = pltpu.make_async_copy(w_in_ref, out_ref.at[me], local_sem.at[0])
        local.start()
        first = [copy(0, w_in_ref, me, sibling)]
        first += [copy(1 + j, w_in_ref, me, (px, py, c)) for j, (px, py) in enumerate(chips)]
        for cp in first:
            cp.start()
        _ada_phase(c_ref, wada_ref, call_ref, adag_ref, mine_ref, ada_send, ada_recv)
        passed = []
        for j, (px, py) in enumerate(chips):
            blk = 4 * px + 2 * py + c
            copy(1 + j, w_in_ref, blk, (x, y, c)).wait_recv()
            fwd = copy(4 + j, out_ref.at[blk], blk, sibling)
            fwd.start()
            passed.append(fwd)
        copy(0, w_in_ref, 4 * x + 2 * y + 1 - c, (x, y, c)).wait_recv()
        for j, (px, py) in enumerate(chips):
            copy(4 + j, w_in_ref, 4 * px + 2 * py + 1 - c, (x, y, c)).wait_recv()
        for cp in first + passed:
            cp.wait_send()
        local.wait()

    per = N_DEV - 1
    return pl.pallas_call(
        body, name="gather_weights",
        out_shape=[jax.ShapeDtypeStruct((N_DEV,) + wt_shard.shape, wt_shard.dtype),
                   jax.ShapeDtypeStruct((N_DEV, 1, D_MODEL), F32),
                   jax.ShapeDtypeStruct((N_DEV, N_DEV, 1, ADA_SHARD), F32)],
        in_specs=[any_spec, vm, vm], out_specs=[any_spec, vm, vm],
        scratch_shapes=[pltpu.VMEM((N_DEV, 1, ADA_SHARD), F32),
                        pltpu.SemaphoreType.DMA((per,)), pltpu.SemaphoreType.DMA((per,)),
                        pltpu.SemaphoreType.DMA((1,)),
                        pltpu.SemaphoreType.DMA((2 * per,)), pltpu.SemaphoreType.DMA((2 * per,))],
        compiler_params=pltpu.CompilerParams(vmem_limit_bytes=VMEM_LIMIT),
    )(wt_shard, c_row, w_ada_sh)


def _sibling_swap_sum(slabs, name):
    _, nch, rows, cols = slabs.shape
    any_spec = pl.BlockSpec(memory_space=pl.ANY)

    def body(in_ref, out_ref, mine_v, theirs_v, sum_v, send_sems, recv_sems, load_sems, store_sems):
        x, y, c, _ = _mesh_pos()
        swaps = [pltpu.make_async_remote_copy(
            src_ref=in_ref.at[1 - c, ch], dst_ref=theirs_v.at[ch], send_sem=send_sems.at[ch],
            recv_sem=recv_sems.at[ch], device_id=(x, y, 1 - c), device_id_type=pl.DeviceIdType.MESH)
            for ch in range(nch)]
        loads = [pltpu.make_async_copy(in_ref.at[c, ch], mine_v.at[ch], load_sems.at[ch]) for ch in range(nch)]
        stores = [pltpu.make_async_copy(sum_v.at[ch], out_ref.at[ch], store_sems.at[ch]) for ch in range(nch)]
        for cp in swaps + loads:
            cp.start()
        for ch in range(nch):
            loads[ch].wait()
            swaps[ch].wait()
            sum_v[ch] = (mine_v[ch].astype(F32) + theirs_v[ch].astype(F32)).astype(BF16)
            stores[ch].start()
        for cp in stores:
            cp.wait()

    return pl.pallas_call(
        body, name=name,
        out_shape=jax.ShapeDtypeStruct((nch, rows, cols), BF16),
        in_specs=[any_spec], out_specs=any_spec,
        scratch_shapes=[pltpu.VMEM((nch, rows, cols), BF16)] * 3 + [pltpu.SemaphoreType.DMA((nch,))] * 4,
        compiler_params=_params(),
    )(slabs)


def _all_to_all_copies(ins, outs, send_sems, recv_sems, local_sems):
    x, y, c, me = _mesh_pos()
    copies = []
    for a in range(len(ins)):
        copies.append(pltpu.make_async_copy(ins[a].at[me], outs[a].at[me], local_sems.at[a]))
        for k in range(1, N_DEV):
            dev, p = _peer(k, x, y, c)
            copies.append(pltpu.make_async_remote_copy(
                src_ref=ins[a].at[p], dst_ref=outs[a].at[me],
                send_sem=send_sems.at[a * (N_DEV - 1) + k - 1], recv_sem=recv_sems.at[a * (N_DEV - 1) + k - 1],
                device_id=dev, device_id_type=pl.DeviceIdType.MESH))
    return copies


def _chip_copies(ins, outs, send_sems, recv_sems, local_sems):
    x, y, c, _ = _mesh_pos()
    my_chip = 2 * x + y
    chips = [(1 - x, y), (x, 1 - y), (1 - x, 1 - y)]
    copies = []
    for a in range(len(ins)):
        copies.append(pltpu.make_async_copy(ins[a].at[my_chip], outs[a].at[my_chip], local_sems.at[a]))
        for j, (px, py) in enumerate(chips):
            copies.append(pltpu.make_async_remote_copy(
                src_ref=ins[a].at[2 * px + py], dst_ref=outs[a].at[my_chip],
                send_sem=send_sems.at[a * 3 + j], recv_sem=recv_sems.at[a * 3 + j],
                device_id=(px, py, c), device_id_type=pl.DeviceIdType.MESH))
    return copies


def _proj_fwd(x, ada3, norm_g, w_all_t, bf_pad, qg, kg, later):
    s = x.shape[0]
    tm, tn = min(TM_PROJ, s), TN_PROJ
    nt = s // tm
    n = len(later)

    def body(x_ref, ada_ref, g_ref, wt_ref, bf_ref, qg_ref, kg_ref, *rest):
        ins, (proj_ref, fl_ref, h_ref), rows_refs, rest = rest[:n], rest[n:n + 3], rest[n + 3:n + 8], rest[n + 8:]
        outs, (carry, send_sems, recv_sems, local_sems) = rest[:n], rest[n:]
        i = pl.program_id(0)

        @pl.when(i == 0)
        def _():
            carry[...] = jnp.zeros_like(carry)
            for cp in _gather_copies(ins, outs, send_sems, recv_sems, local_sems):
                cp.start()

        xv = x_ref[...]
        r = lax.rsqrt(jnp.mean(xv * xv, axis=-1, keepdims=True) + EPS)
        hv = ((xv * r) * g_ref[...]) * (1.0 + ada_ref[1:2, :]) + ada_ref[0:1, :]
        hb = hv.astype(BF16)
        h_ref[...] = hb
        fl = _dot_nt(hb, wt_ref[N_MAIN:N_ALL, :])
        fl_ref[...] = fl
        for j in range(N_MAIN // tn):
            proj_ref[:, j * tn:(j + 1) * tn] = _dot_nt(hb, wt_ref[j * tn:(j + 1) * tn, :]).astype(BF16)
        _attention_rows(proj_ref, fl, bf_ref, qg_ref, kg_ref, carry, *rows_refs)

        @pl.when(i == nt - 1)
        def _():
            for cp in _gather_copies(ins, outs, send_sems, recv_sems, local_sems):
                cp.wait()

    any_spec = pl.BlockSpec(memory_space=pl.ANY)
    heads = pl.BlockSpec((HEADS, tm, LANES), lambda i: (0, i, 0))
    heads_t = pl.BlockSpec((HEADS, LANES, tm), lambda i: (0, 0, i))
    vec = pl.BlockSpec((1, ATTN_W), lambda i: (0, 0))
    res = pl.pallas_call(
        body, name="proj_fwd", grid=(nt,),
        in_specs=[pl.BlockSpec((tm, D_MODEL), lambda i: (i, 0)),
                  pl.BlockSpec((3, D_MODEL), lambda i: (0, 0)),
                  pl.BlockSpec((1, D_MODEL), lambda i: (0, 0)),
                  pl.BlockSpec((N_ALL, D_MODEL), lambda i: (0, 0)),
                  pl.BlockSpec((1, LANES), lambda i: (0, 0)), vec, vec] + [any_spec] * n,
        out_specs=[pl.BlockSpec((tm, N_MAIN), lambda i: (i, 0)),
                   pl.BlockSpec((tm, N_FPAD), lambda i: (i, 0)),
                   pl.BlockSpec((tm, D_MODEL), lambda i: (i, 0)),
                   heads, heads, heads, heads_t, heads_t] + [any_spec] * n,
        out_shape=[jax.ShapeDtypeStruct((s, N_MAIN), BF16),
                   jax.ShapeDtypeStruct((s, N_FPAD), F32),
                   jax.ShapeDtypeStruct((s, D_MODEL), BF16)]
        + [jax.ShapeDtypeStruct((HEADS, s, LANES), BF16)] * 3
        + [jax.ShapeDtypeStruct((HEADS, LANES, s), BF16)] * 2
        + [jax.ShapeDtypeStruct((N_DEV,) + a.shape, a.dtype) for a in later],
        scratch_shapes=[pltpu.VMEM((1, LANES), F32)] + _gather_sems(n),
        compiler_params=_params(("arbitrary",)),
    )(x, ada3, norm_g, w_all_t, bf_pad, qg, kg, *later)
    return res[:3], res[3:8], res[8:]


L_ONE_Q, L_F_Q, L_LSE_Q, L_END = HEAD_DIM, HEAD_DIM + 3, HEAD_DIM + 6, HEAD_DIM + 9


def _split3(f):
    hi = f.astype(BF16).astype(F32)
    r = f - hi
    mid = r.astype(BF16).astype(F32)
    return hi, mid, r - mid


def _place3(lane, first, parts, otherwise):
    a, b, c = parts
    return jnp.where(lane == first, a, jnp.where(lane == first + 1, b, jnp.where(lane == first + 2, c, otherwise)))


def _log_forget(fl, bf):
    z = fl + bf
    lf = jnp.minimum(z, 0.0) - jnp.log1p(jnp.exp(-jnp.abs(z)))
    lane = lax.broadcasted_iota(jnp.int32, z.shape, 1)
    return jnp.where(lane < HEADS, lf, 0.0)


def _attention_rows(p_ref, fl, bf_ref, qg_ref, kg_ref, carry, qa_ref, ka_ref, va_ref, kt_ref, vt_ref):
    tm = fl.shape[0]
    scale = HEAD_DIM ** -0.5
    tri = (lax.broadcasted_iota(jnp.int32, (tm, tm), 1) <= lax.broadcasted_iota(jnp.int32, (tm, tm), 0)).astype(F32)
    cum_v = jnp.dot(tri, _log_forget(fl, bf_ref[...]), preferred_element_type=F32,
                    precision=lax.Precision.HIGHEST) + carry[...]
    carry[...] = _sub_row(cum_v, tm - 1)
    lane = lax.broadcasted_iota(jnp.int32, (tm, LANES), 1)
    lo = lane < HEAD_DIM
    v_tail = jnp.where(lane < L_F_Q, 1.0, 0.0)
    for pr in range(ATTN_W // LANES):
        sl = slice(pr * LANES, (pr + 1) * LANES)
        q2 = p_ref[:, OFF_Q + pr * LANES:OFF_Q + (pr + 1) * LANES].astype(F32)
        k2 = p_ref[:, OFF_K + pr * LANES:OFF_K + (pr + 1) * LANES].astype(F32)
        v2 = p_ref[:, OFF_V + pr * LANES:OFF_V + (pr + 1) * LANES].astype(F32)
        rq = lax.rsqrt(_seg_sum(q2 * q2, lo) * (1.0 / HEAD_DIM) + EPS)
        rk = lax.rsqrt(_seg_sum(k2 * k2, lo) * (1.0 / HEAD_DIM) + EPS)
        qn = ((q2 * rq) * qg_ref[:, sl]) * scale
        kn = (k2 * rk) * kg_ref[:, sl]
        for hh in range(2):
            h = 2 * pr + hh
            f3 = _split3(_lane_col(cum_v, h))
            qh = qn if hh == 0 else pltpu.roll(qn, HEAD_DIM, 1)
            kh = kn if hh == 0 else pltpu.roll(kn, HEAD_DIM, 1)
            vh = v2 if hh == 0 else pltpu.roll(v2, HEAD_DIM, 1)
            q_tail = jnp.where(lane < L_F_Q, 1.0, _place3(lane, L_F_Q, f3, 0.0))
            k_tail = _place3(lane, L_ONE_Q, tuple(-f for f in f3), jnp.where(lane < L_END, 1.0, 0.0))
            k_row = jnp.where(lo, kh, k_tail)
            v_row = jnp.where(lo, vh, v_tail)
            qa_ref[h] = jnp.where(lo, qh, q_tail).astype(BF16)
            ka_ref[h] = k_row.astype(BF16)
            va_ref[h] = v_row.astype(BF16)
            kt_ref[h] = k_row.T.astype(BF16)
            vt_ref[h] = v_row.T.astype(BF16)


def _causal_t(t):
    return lax.broadcasted_iota(jnp.int32, (t, t), 0) <= lax.broadcasted_iota(jnp.int32, (t, t), 1)


def _tri_steps(nt, q_major):
    if q_major:
        pairs = [(i, j) for i in range(nt) for j in range(i + 1)]
    else:
        pairs = [(i, j) for j in range(nt) for i in range(j, nt)]
    return (jnp.asarray(np.array([p[0] for p in pairs], np.int32)),
            jnp.asarray(np.array([p[1] for p in pairs], np.int32)))


def _attn_fwd(qa, ka, vt, proj):
    s = qa.shape[1]
    t = min(TQ, s)
    it, jt = _tri_steps(s // t, True)
    hp = HEADS_PER_STEP
    wide = hp * HEAD_DIM
    za_blk = OFF_ZA // wide

    def body(it_ref, jt_ref, q_ref, k_ref, vt_ref, za_ref, attn_ref, oa_ref, qb_ref, m_s, acc_s, pair_s):
        step = pl.program_id(1)
        i, j = it_ref[step], jt_ref[step]

        @pl.when(j == 0)
        def _():
            m_s[...] = jnp.full_like(m_s, NEG)
            acc_s[...] = jnp.zeros_like(acc_s)

        def update(masked):
            for hh in range(hp):
                st = _dot_nt(k_ref[hh], q_ref[hh])
                if masked:
                    st = jnp.where(_causal_t(t), st, NEG)
                m_prev = m_s[hh]
                m_next = jnp.maximum(m_prev, jnp.max(st, axis=0, keepdims=True))
                alpha = jnp.exp(m_prev - m_next)
                pt = jnp.exp(st - m_next).astype(BF16)
                acc_s[hh] = acc_s[hh] * alpha + _dot(vt_ref[hh], pt)
                m_s[hh] = m_next

        @pl.when(j < i)
        def _():
            update(False)

        @pl.when(j == i)
        def _():
            update(True)
            row = lax.broadcasted_iota(jnp.int32, (LANES, t), 0)
            lane = lax.broadcasted_iota(jnp.int32, (t, LANES), 1)
            for hh in range(hp):
                l_row = acc_s[hh, L_ONE_Q:L_ONE_Q + 1, :]
                pair_s[hh * HEAD_DIM:(hh + 1) * HEAD_DIM, :] = acc_s[hh, 0:HEAD_DIM, :] / l_row
                lse3 = _split3(m_s[hh] + jnp.log(l_row))
                tail_t = _place3(row, L_LSE_Q, tuple(-x for x in lse3), 0.0)
                keep_q = jnp.logical_or(lane < L_LSE_Q, lane >= L_END)
                qb_ref[hh] = jnp.where(keep_q, q_ref[hh].astype(F32), tail_t.T).astype(BF16)
            out = pair_s[...].T
            attn_ref[...] = out
            z = za_ref[...].astype(F32)
            oa_ref[...] = (out * (z * _sigmoid(z))).astype(BF16)

    pair_q = pl.BlockSpec((hp, t, LANES), lambda p, n, it_, jt_: (p, it_[n], 0))
    pair_k = pl.BlockSpec((hp, t, LANES), lambda p, n, it_, jt_: (p, jt_[n], 0))
    pair_kt = pl.BlockSpec((hp, LANES, t), lambda p, n, it_, jt_: (p, 0, jt_[n]))
    out_q = pl.BlockSpec((t, wide), lambda p, n, it_, jt_: (it_[n], p))
    return pl.pallas_call(
        body, name="attn_fwd",
        grid_spec=pltpu.PrefetchScalarGridSpec(
            num_scalar_prefetch=2, grid=(HEADS // hp, it.shape[0]),
            in_specs=[pair_q, pair_k, pair_kt,
                      pl.BlockSpec((t, wide), lambda p, n, it_, jt_: (it_[n], za_blk + p))],
            out_specs=[out_q, out_q, pair_q],
            scratch_shapes=[pltpu.VMEM((hp, 1, t), F32), pltpu.VMEM((hp, LANES, t), F32),
                            pltpu.VMEM((wide, t), F32)]),
        out_shape=[jax.ShapeDtypeStruct((s, ATTN_W), F32),
                   jax.ShapeDtypeStruct((s, ATTN_W), BF16),
                   jax.ShapeDtypeStruct((HEADS, s, LANES), BF16)],
        compiler_params=_params(("parallel", "arbitrary")),
    )(it, jt, qa, ka, vt, proj)


def _conv_parts(gb_ref, gc_ref, u_ref, zb_ref, gch_ref, uh_ref, first, w_ref, tm):
    gb, gc = gb_ref[...].astype(F32), gc_ref[...].astype(F32)
    u, zb = u_ref[...].astype(F32), zb_ref[...].astype(F32)
    cu = gc * u
    cu_h = jnp.where(first, 0.0, gch_ref[...].astype(F32) * uh_ref[...].astype(F32))
    prev1, prev2 = _sub_row(cu_h, HALO - 1), _sub_row(cu_h, HALO - 2)
    row = lax.broadcasted_iota(jnp.int32, cu.shape, 0)
    r1 = jnp.where(row == 0, prev1, pltpu.roll(cu, 1, 0))
    r2 = jnp.where(row == 0, prev2, jnp.where(row == 1, prev1, pltpu.roll(cu, 2, 0)))
    conv = w_ref[2:3, :] * cu + w_ref[1:2, :] * r1 + w_ref[0:1, :] * r2
    return gb, gc, u, zb, cu, r1, r2, conv


def _conv_specs(tm, s, width=LANES):
    def tile(off):
        return pl.BlockSpec((tm, width), lambda c, i: (i, off // width + c))

    def before(off):
        return pl.BlockSpec((HALO, width), lambda c, i: (jnp.maximum(i * (tm // HALO) - 1, 0), off // width + c))

    def after(off):
        return pl.BlockSpec((HALO, width),
                            lambda c, i: (jnp.minimum((i + 1) * (tm // HALO), s // HALO - 1), off // width + c))

    return ([tile(OFF_CB), tile(OFF_CC), tile(OFF_CU), tile(OFF_CZ)], [before(OFF_CC), before(OFF_CU)],
            [after(OFF_CB), after(OFF_CZ)])


def _tail(oa, attn, proj, x, target, ada3, wa, wb, wo, conv_w):
    s = x.shape[0]
    tm = min(TM_TAIL, s)
    gab_blk = OFF_GA // (2 * D_MODEL)
    za_blk = OFF_ZA // ATTN_W
    tiles, befores, _ = _conv_specs(tm, s, CONV_W)

    def body(oa_ref, attn_ref, za_ref, gb_ref, gc_ref, u_ref, zb_ref, gch_ref, uh_ref, cw_ref, gab_ref, x_ref, t_ref,
             ada_ref, wa_ref, wb_ref, wo_ref,
             dy_ref, dgab_ref, do_ref, dza_ref, dob_ref, dwo_out, dwa_out, dwb_out, dgate_ref, loss_ref,
             dwo_ref, dwa_ref, dwb_ref):
        first = pl.program_id(0) == 0

        @pl.when(first)
        def _():
            dwo_ref[...] = jnp.zeros_like(dwo_ref)
            dwa_ref[...] = jnp.zeros_like(dwa_ref)
            dwb_ref[...] = jnp.zeros_like(dwb_ref)
            dgate_ref[...] = jnp.zeros_like(dgate_ref)
            loss_ref[...] = jnp.zeros_like(loss_ref)

        gb, _, _, zb, _, _, _, conv = _conv_parts(gb_ref, gc_ref, u_ref, zb_ref, gch_ref, uh_ref, first, cw_ref, tm)
        ob_v = (gb * conv * (zb * _sigmoid(zb))).astype(BF16)
        oa_v = oa_ref[...]
        wa_v, wb_v, wo_v = wa_ref[...], wb_ref[...], wo_ref[...]
        a2 = _dot(oa_v, wa_v)
        b2 = _dot(ob_v, wb_v)
        sa = _sigmoid(gab_ref[:, 0:D_MODEL].astype(F32))
        sb = _sigmoid(gab_ref[:, D_MODEL:2 * D_MODEL].astype(F32))
        mb = (sa * a2 + sb * b2).astype(BF16)
        mo = _dot(mb, wo_v)
        gate = ada_ref[2:3, :]
        err = (x_ref[...] + gate * mo) - t_ref[...]
        dy = err * (1.0 / D_MODEL)
        dy_ref[...] = dy
        loss_ref[...] += 0.5 * jnp.sum(err * err) * (1.0 / D_MODEL)
        dgate_ref[...] += jnp.sum(dy * mo, axis=0, keepdims=True)
        dmo = (dy * gate).astype(BF16)
        dmerged = _dot_nt(dmo, wo_v)
        dwo_ref[...] += _dot_tn(mb, dmo)
        da2 = (dmerged * sa).astype(BF16)
        db2 = (dmerged * sb).astype(BF16)
        dgab_ref[:, 0:D_MODEL] = (dmerged * a2 * (sa * (1.0 - sa))).astype(BF16)
        dgab_ref[:, D_MODEL:2 * D_MODEL] = (dmerged * b2 * (sb * (1.0 - sb))).astype(BF16)
        doa = _dot_nt(da2, wa_v)
        dob_ref[...] = _dot_nt(db2, wb_v)
        dwa_ref[...] += _dot_tn(oa_v, da2)
        dwb_ref[...] += _dot_tn(ob_v, db2)

        lane = lax.broadcasted_iota(jnp.int32, (tm, LANES), 1)
        lo = lane < HEAD_DIM
        for pr in range(ATTN_W // LANES):
            sl = slice(pr * LANES, (pr + 1) * LANES)
            g, a, z = doa[:, sl], attn_ref[:, sl], za_ref[:, sl].astype(F32)
            sg = _sigmoid(z)
            dat = (g * (z * sg)).astype(BF16).astype(F32)
            prod = dat * a
            dza_ref[:, sl] = (g * a * (sg * (1.0 + z * (1.0 - sg)))).astype(BF16)
            for hh in range(2):
                sel = lo if hh == 0 else jnp.logical_not(lo)
                delta3 = _split3(jnp.sum(jnp.where(sel, prod, 0.0), axis=-1, keepdims=True))
                dh = dat if hh == 0 else pltpu.roll(dat, HEAD_DIM, 1)
                tail_lanes = _place3(lane, L_ONE_Q, tuple(-d for d in delta3), 0.0)
                do_ref[2 * pr + hh] = jnp.where(lo, dh, tail_lanes).astype(BF16)

        @pl.when(pl.program_id(0) == pl.num_programs(0) - 1)
        def _():
            dwo_out[...] = dwo_ref[...].astype(BF16)
            for p in range(N_DEV):
                dwa_out[p] = dwa_ref[:, p * LANES:(p + 1) * LANES].astype(BF16)
                dwb_out[p] = dwb_ref[:, p * LANES:(p + 1) * LANES].astype(BF16)

    half = pl.BlockSpec((tm, ATTN_W), lambda i: (i, 0))
    full = pl.BlockSpec((tm, D_MODEL), lambda i: (i, 0))

    def const(shape):
        return pl.BlockSpec(shape, lambda i: (0, 0))

    slabs = pl.BlockSpec((N_DEV, ATTN_W, LANES), lambda i: (0, 0, 0))

    def one_axis(spec):
        return pl.BlockSpec(spec.block_shape, lambda i, f=spec.index_map: f(0, i))

    return pl.pallas_call(
        body, name="tail", grid=(s // tm,),
        in_specs=[half, half, pl.BlockSpec((tm, ATTN_W), lambda i: (i, za_blk))]
        + [one_axis(sp) for sp in tiles + befores]
        + [const((3, CONV_W)), pl.BlockSpec((tm, 2 * D_MODEL), lambda i: (i, gab_blk)), full, full,
           const((3, D_MODEL)), const((ATTN_W, D_MODEL)), const((CONV_W, D_MODEL)), const((D_MODEL, D_MODEL))],
        out_specs=[full, pl.BlockSpec((tm, 2 * D_MODEL), lambda i: (i, 0)),
                   pl.BlockSpec((HEADS, tm, LANES), lambda i: (0, i, 0)), half, half,
                   const((D_MODEL, D_MODEL)), slabs, slabs, const((1, D_MODEL)), const((1, LANES))],
        out_shape=[jax.ShapeDtypeStruct((s, D_MODEL), F32),
                   jax.ShapeDtypeStruct((s, 2 * D_MODEL), BF16),
                   jax.ShapeDtypeStruct((HEADS, s, LANES), BF16),
                   jax.ShapeDtypeStruct((s, ATTN_W), BF16),
                   jax.ShapeDtypeStruct((s, CONV_W), F32),
                   jax.ShapeDtypeStruct((D_MODEL, D_MODEL), BF16),
                   jax.ShapeDtypeStruct((N_DEV, ATTN_W, LANES), BF16),
                   jax.ShapeDtypeStruct((N_DEV, CONV_W, LANES), BF16),
                   jax.ShapeDtypeStruct((1, D_MODEL), F32),
                   jax.ShapeDtypeStruct((1, LANES), F32)],
        scratch_shapes=[pltpu.VMEM((D_MODEL, D_MODEL), F32), pltpu.VMEM((ATTN_W, D_MODEL), F32),
                        pltpu.VMEM((CONV_W, D_MODEL), F32)],
        compiler_params=_params(("arbitrary",)),
    )(oa, attn, proj, *([proj] * 6), conv_w, proj, x, target, ada3, wa, wb, wo)


def _attn_bwd(qb, ka, kt, va, do, proj, qg, kg):
    s = qb.shape[1]
    t = min(TQ, s)
    nt = s // t
    hp = HEADS_PER_STEP
    assert hp == HEADS, "all heads share one (S, 1536) output block"
    wide = hp * HEAD_DIM
    scale = HEAD_DIM ** -0.5
    it, jt = _tri_steps(nt, False)

    def body(it_ref, jt_ref, q_ref, k_ref, kt_ref, v_ref, do_ref, qraw_ref, kraw_ref, qg_ref, kg_ref,
             dqkv_ref, dqg_ref, dkg_ref, dcum_ref, dqt_s, dk_s, dv_s, rows_s):
        grp, step = pl.program_id(0), pl.program_id(1)
        i, j = it_ref[step], jt_ref[step]
        lane = lax.broadcasted_iota(jnp.int32, (t, LANES), 1)
        lo = lane < HEAD_DIM

        @pl.when(step == 0)
        def _():
            dqt_s[...] = jnp.zeros_like(dqt_s)
            dqg_ref[...] = jnp.zeros_like(dqg_ref)
            dkg_ref[...] = jnp.zeros_like(dkg_ref)

        @pl.when(i == j)
        def _():
            dk_s[...] = jnp.zeros_like(dk_s)
            dv_s[...] = jnp.zeros_like(dv_s)

        def update(masked):
            for hh in range(hp):
                qh, doh = q_ref[hh], do_ref[hh]
                st = _dot_nt(k_ref[hh], qh)
                if masked:
                    st = jnp.where(_causal_t(t), st, NEG)
                pt = jnp.exp(st)
                dst = (pt * _dot_nt(v_ref[hh], doh)).astype(BF16)
                dv_s[hh] += _dot(pt.astype(BF16), doh)
                dk_s[hh] += _dot(dst, qh)
                dqt_s[hh, i] += _dot(kt_ref[hh], dst)

        def pair(a, b):
            return jnp.where(lo, a, pltpu.roll(b, HEAD_DIM, 1))

        def norm_bwd(raw, dy, g, dg_ref, off, sl):
            r = lax.rsqrt(_seg_sum(raw * raw, lo) * (1.0 / HEAD_DIM) + EPS)
            xhat = raw * r
            dg_ref[:, sl] += jnp.sum(dy * xhat, axis=0, keepdims=True)
            dxh = dy * g
            dx = r * (dxh - xhat * (_seg_sum(dxh * xhat, lo) * (1.0 / HEAD_DIM)))
            dqkv_ref[:, off + sl.start:off + sl.stop] = dx.astype(BF16)

        @pl.when(i > j)
        def _():
            update(False)

        @pl.when(i == j)
        def _():
            update(True)
            dq_rows = [dqt_s[hh, i].T for hh in range(hp)]
            rows = jnp.zeros((t, LANES), F32)
            for hh in range(hp):
                rows = jnp.where(lane == grp * hp + hh, _lane_col(dq_rows[hh], L_F_Q), rows)
            rows_s[...] = rows
            for pr in range(hp // 2):
                sl = slice(pr * LANES, (pr + 1) * LANES)
                norm_bwd(qraw_ref[:, sl].astype(F32), pair(dq_rows[2 * pr], dq_rows[2 * pr + 1]) * scale,
                         qg_ref[:, sl], dqg_ref, OFF_Q, sl)

        @pl.when(i == nt - 1)
        def _():
            dcum = rows_s[...]
            for hh in range(hp):
                dcum = jnp.where(lane == grp * hp + hh, dcum - _lane_col(dk_s[hh], L_ONE_Q), dcum)
            dcum_ref[0] = dcum
            for pr in range(hp // 2):
                sl = slice(pr * LANES, (pr + 1) * LANES)
                norm_bwd(kraw_ref[:, sl].astype(F32), pair(dk_s[2 * pr], dk_s[2 * pr + 1]),
                         kg_ref[:, sl], dkg_ref, OFF_K, sl)
                dqkv_ref[:, OFF_V + sl.start:OFF_V + sl.stop] = pair(dv_s[2 * pr], dv_s[2 * pr + 1]).astype(BF16)

    pair_q = pl.BlockSpec((hp, t, LANES), lambda p, n, it_, jt_: (p, it_[n], 0))
    pair_k = pl.BlockSpec((hp, t, LANES), lambda p, n, it_, jt_: (p, jt_[n], 0))
    pair_kt = pl.BlockSpec((hp, LANES, t), lambda p, n, it_, jt_: (p, 0, jt_[n]))
    tok3 = pl.BlockSpec((t, 3 * ATTN_W), lambda p, n, it_, jt_: (jt_[n], 0))
    gain = pl.BlockSpec((1, wide), lambda p, n, it_, jt_: (0, p))
    return pl.pallas_call(
        body, name="attn_bwd",
        grid_spec=pltpu.PrefetchScalarGridSpec(
            num_scalar_prefetch=2, grid=(HEADS // hp, it.shape[0]),
            in_specs=[pair_q, pair_k, pair_kt, pair_k, pair_q,
                      pl.BlockSpec((t, wide), lambda p, n, it_, jt_: (jt_[n], OFF_Q // wide + p)),
                      pl.BlockSpec((t, wide), lambda p, n, it_, jt_: (jt_[n], OFF_K // wide + p)), gain, gain],
            out_specs=[tok3, gain, gain,
                       pl.BlockSpec((1, t, LANES), lambda p, n, it_, jt_: (p, jt_[n], 0))],
            scratch_shapes=[pltpu.VMEM((hp, nt, LANES, t), F32), pltpu.VMEM((hp, t, LANES), F32),
                            pltpu.VMEM((hp, t, LANES), F32), pltpu.VMEM((t, LANES), F32)]),
        out_shape=[jax.ShapeDtypeStruct((s, 3 * ATTN_W), BF16)]
        + [jax.ShapeDtypeStruct((1, ATTN_W), F32)] * 2
        + [jax.ShapeDtypeStruct((HEADS // hp, s, LANES), F32)],
        compiler_params=_params(("parallel", "arbitrary")),
    )(it, jt, qb, ka, kt, va, do, proj, proj, qg, kg)


def _forget_bwd(dcum, fl, bf_pad):
    s = fl.shape[0]
    tc = min(TC_CUM, s)
    n = s // tc

    def body(dc_ref, fl_ref, bf_ref, df_ref, dbf_ref, carry):
        @pl.when(pl.program_id(0) == 0)
        def _():
            carry[...] = jnp.zeros_like(carry)
            dbf_ref[...] = jnp.zeros_like(dbf_ref)
        r = lax.broadcasted_iota(jnp.int32, (tc, tc), 0)
        cidx = lax.broadcasted_iota(jnp.int32, (tc, tc), 1)
        tri = (cidx >= r).astype(F32)
        dc = dc_ref[0]
        for grp in range(1, dcum.shape[0]):
            dc = dc + dc_ref[grp]
        dlf = jnp.dot(tri, dc, preferred_element_type=F32, precision=lax.Precision.HIGHEST) + carry[...]
        carry[...] += jnp.sum(dc, axis=0, keepdims=True)
        lane = lax.broadcasted_iota(jnp.int32, (tc, LANES), 1)
        dfl = jnp.where(lane < HEADS, dlf * _sigmoid(-(fl_ref[...] + bf_ref[...])), 0.0)
        df_ref[...] = dfl.astype(BF16)
        dbf_ref[...] += jnp.sum(dfl, axis=0, keepdims=True)

    rev = pl.BlockSpec((tc, LANES), lambda i: (n - 1 - i, 0))
    vec = pl.BlockSpec((1, LANES), lambda i: (0, 0))
    return pl.pallas_call(
        body, name="forget_bwd", grid=(n,),
        in_specs=[pl.BlockSpec((dcum.shape[0], tc, LANES), lambda i: (0, n - 1 - i, 0)), rev, vec],
        out_specs=[rev, vec],
        out_shape=[jax.ShapeDtypeStruct((s, LANES), BF16), jax.ShapeDtypeStruct((1, LANES), F32)],
        scratch_shapes=[pltpu.VMEM((1, LANES), F32)],
        compiler_params=_params(("arbitrary",)),
    )(dcum, fl, bf_pad)


def _conv_bwd(dob, proj, conv_w):
    s = dob.shape[0]
    tm = min(TM_ELEM, s)
    wd = CONV_W
    tiles, befores, afters = _conv_specs(tm, s, wd)

    def body(dob_ref, dnext_ref, gb_ref, gc_ref, u_ref, zb_ref, gch_ref, uh_ref, gbn_ref, zbn_ref, w_ref,
             out_ref, dw_ref):
        i = pl.program_id(1)

        @pl.when(i == 0)
        def _():
            dw_ref[...] = jnp.zeros_like(dw_ref)
        gb, gc, u, zb, cu, r1, r2, conv = _conv_parts(gb_ref, gc_ref, u_ref, zb_ref, gch_ref, uh_ref, i == 0, w_ref, tm)
        g = dob_ref[...]
        sg = _sigmoid(zb)
        sz = zb * sg
        dconv = g * gb * sz
        zn = zbn_ref[0:8, :].astype(F32)
        dcn = jnp.where(i == pl.num_programs(1) - 1, 0.0,
                        dnext_ref[...] * gbn_ref[0:8, :].astype(F32) * (zn * _sigmoid(zn)))
        nxt1, nxt2 = _sub_row(dcn, 0), _sub_row(dcn, 1)
        row = lax.broadcasted_iota(jnp.int32, (tm, wd), 0)
        f1 = jnp.where(row == tm - 1, nxt1, pltpu.roll(dconv, tm - 1, 0))
        f2 = jnp.where(row == tm - 2, nxt1, jnp.where(row == tm - 1, nxt2, pltpu.roll(dconv, tm - 2, 0)))
        dcu = w_ref[2:3, :] * dconv + w_ref[1:2, :] * f1 + w_ref[0:1, :] * f2
        out_ref[:, 0:wd] = (g * conv * sz).astype(BF16)
        out_ref[:, wd:2 * wd] = (dcu * u).astype(BF16)
        out_ref[:, 2 * wd:3 * wd] = (dcu * gc).astype(BF16)
        out_ref[:, 3 * wd:4 * wd] = (g * gb * conv * (sg * (1.0 + zb * (1.0 - sg)))).astype(BF16)
        w_row = lax.broadcasted_iota(jnp.int32, (3, wd), 0)
        dw0 = jnp.sum(dconv * r2, axis=0, keepdims=True)
        dw1 = jnp.sum(dconv * r1, axis=0, keepdims=True)
        dw2 = jnp.sum(dconv * cu, axis=0, keepdims=True)
        dw_ref[...] += jnp.where(w_row == 0, dw0, jnp.where(w_row == 1, dw1, dw2))

    blk = pl.BlockSpec((tm, wd), lambda c, i: (i, c))
    nxt = pl.BlockSpec((8, wd), lambda c, i: (jnp.minimum((i + 1) * (tm // 8), s // 8 - 1), c))
    wspec = pl.BlockSpec((3, wd), lambda c, i: (0, c))
    return pl.pallas_call(
        body, name="conv_bwd", grid=(CONV_W // wd, s // tm),
        in_specs=[blk, nxt] + tiles + befores + afters + [wspec],
        out_specs=[pl.BlockSpec((tm, 4 * wd), lambda c, i: (i, c)), wspec],
        out_shape=[jax.ShapeDtypeStruct((s, 4 * CONV_W), BF16), jax.ShapeDtypeStruct((3, CONV_W), F32)],
        compiler_params=_params(("parallel", "arbitrary")),
    )(dob, dob, *([proj] * 8), conv_w)


def _piece_layout(pieces):
    offs, off = [], 0
    for p in pieces:
        offs.append((off, p.shape[1]))
        off += p.shape[1]
    assert off == N_ALL, off
    return offs


def _dw_in(h, pieces, chip_sums):
    s = h.shape[0]
    tk, tn = min(TK_DW, s), TN_DW
    nk = s // tk
    nn = N_MAIN // tn
    main, fpiece = pieces[:-1], pieces[-1]
    layout = _piece_layout(pieces)[:-1]
    n_main = len(main)
    nx = len(chip_sums)

    def body(*refs):
        p_refs, f_ref, h_ref = refs[:n_main], refs[n_main], refs[n_main + 1]
        ins, refs = refs[n_main + 2:n_main + 2 + nx], refs[n_main + 2 + nx:]
        out_ref, outf_ref = refs[:2]
        outs, (acc, accf, send_sems, recv_sems, local_sems) = refs[2:2 + nx], refs[2 + nx:]
        n, k = pl.program_id(0), pl.program_id(1)

        @pl.when(jnp.logical_and(n == 0, k == 0))
        def _():
            for cp in _all_to_all_copies(ins, outs, send_sems, recv_sems, local_sems):
                cp.start()

        @pl.when(k == 0)
        def _():
            acc[...] = jnp.zeros_like(acc)
        hv = h_ref[pl.ds(pl.multiple_of(k * tk, tk), tk), :]
        for p_ref, (off, width) in zip(p_refs, layout):
            @pl.when(jnp.logical_and(n >= off // tn, n < (off + width) // tn))
            def _():
                acc[...] += _dot_tn(p_ref[...], hv)

        @pl.when(k == nk - 1)
        def _():
            out_ref[...] = acc[...].astype(BF16)

        @pl.when(n == 0)
        def _():
            @pl.when(k == 0)
            def _():
                accf[...] = jnp.zeros_like(accf)
            accf[...] += _dot_tn(f_ref[...], hv)

            @pl.when(k == nk - 1)
            def _():
                outf_ref[...] = accf[...].astype(BF16)

        @pl.when(jnp.logical_and(n == nn - 1, k == nk - 1))
        def _():
            for cp in _all_to_all_copies(ins, outs, send_sems, recv_sems, local_sems):
                cp.wait()

    def piece_spec(off, width):
        lo, hi = off // tn, (off + width) // tn

        def index(n, k):
            active = jnp.logical_and(n >= lo, n < hi)
            return jnp.where(active, k, 0), jnp.clip(n - lo, 0, hi - lo - 1)
        return pl.BlockSpec((tk, tn), index)

    any_spec = pl.BlockSpec(memory_space=pl.ANY)
    res = pl.pallas_call(
        body, name="dw_in", grid=(nn, nk),
        in_specs=[piece_spec(off, width) for off, width in layout]
        + [pl.BlockSpec((tk, N_FPAD), lambda n, k: (jnp.where(n == 0, k, 0), 0)),
           pl.BlockSpec((s, D_MODEL), lambda n, k: (0, 0))] + [any_spec] * nx,
        out_specs=[pl.BlockSpec((tn, D_MODEL), lambda n, k: (n, 0)),
                   pl.BlockSpec((N_FPAD, D_MODEL), lambda n, k: (0, 0))] + [any_spec] * nx,
        out_shape=[jax.ShapeDtypeStruct((N_MAIN, D_MODEL), BF16), jax.ShapeDtypeStruct((N_FPAD, D_MODEL), BF16)]
        + [jax.ShapeDtypeStruct(a.shape, a.dtype) for a in chip_sums],
        scratch_shapes=[pltpu.VMEM((tn, D_MODEL), F32), pltpu.VMEM((N_FPAD, D_MODEL), F32)] + _gather_sems(nx),
        compiler_params=_params(("arbitrary", "arbitrary")),
    )(*main, fpiece, h, *chip_sums)
    return res[:2], res[2:]


def _dh_and_dx(pieces, w_all_t, x, dy, ada3, norm_g, chip_sums):
    s = x.shape[0]
    tm = min(TM_DH, s)
    nt = s // tm
    n = len(chip_sums)
    npc = len(pieces)
    layout = _piece_layout(pieces)

    def body(*refs):
        p_refs, refs = refs[:npc], refs[npc:]
        wt_ref, x_ref, dy_ref, ada_ref, g_ref = refs[:5]
        ins, refs = refs[5:5 + n], refs[5 + n:]
        gx_ref, dsh_ref, dsc_ref, dg_ref = refs[:4]
        outs, (send_sems, recv_sems, local_sems) = refs[4:4 + n], refs[4 + n:]
        i = pl.program_id(0)

        @pl.when(i == 0)
        def _():
            for cp in _chip_copies(ins, outs, send_sems, recv_sems, local_sems):
                cp.start()
            dsh_ref[...] = jnp.zeros_like(dsh_ref)
            dsc_ref[...] = jnp.zeros_like(dsc_ref)
            dg_ref[...] = jnp.zeros_like(dg_ref)

        dh = None
        for p_ref, (off, width) in zip(p_refs, layout):
            part = _dot(p_ref[...], wt_ref[off:off + width, :])
            dh = part if dh is None else dh + part
        xv = x_ref[...]
        r = lax.rsqrt(jnp.mean(xv * xv, axis=-1, keepdims=True) + EPS)
        xhat = xv * r
        g = g_ref[...]
        one_sc = 1.0 + ada_ref[1:2, :]
        dsh_ref[...] += jnp.sum(dh, axis=0, keepdims=True)
        dsc_ref[...] += jnp.sum(dh * (xhat * g), axis=0, keepdims=True)
        dg_ref[...] += jnp.sum(dh * xhat, axis=0, keepdims=True) * one_sc
        dxh = dh * (g * one_sc)
        dx = r * (dxh - xhat * jnp.mean(dxh * xhat, axis=-1, keepdims=True))
        gx_ref[...] = dy_ref[...] + dx

        @pl.when(i == nt - 1)
        def _():
            for cp in _chip_copies(ins, outs, send_sems, recv_sems, local_sems):
                cp.wait()

    full = pl.BlockSpec((tm, D_MODEL), lambda i: (i, 0))
    vec = pl.BlockSpec((1, D_MODEL), lambda i: (0, 0))
    any_spec = pl.BlockSpec(memory_space=pl.ANY)
    res = pl.pallas_call(
        body, name="dh_dx", grid=(nt,),
        in_specs=[pl.BlockSpec((tm, p.shape[1]), lambda i: (i, 0)) for p in pieces]
        + [pl.BlockSpec((N_ALL, D_MODEL), lambda i: (0, 0)), full, full,
           pl.BlockSpec((3, D_MODEL), lambda i: (0, 0)), vec] + [any_spec] * n,
        out_specs=[full, vec, vec, vec] + [any_spec] * n,
        out_shape=[jax.ShapeDtypeStruct((s, D_MODEL), F32)] + [jax.ShapeDtypeStruct((1, D_MODEL), F32)] * 3
        + [jax.ShapeDtypeStruct(a.shape, a.dtype) for a in chip_sums],
        scratch_shapes=[pltpu.SemaphoreType.DMA((n * 3,)), pltpu.SemaphoreType.DMA((n * 3,)),
                        pltpu.SemaphoreType.DMA((n,))],
        compiler_params=_params(("arbitrary",)),
    )(*pieces, w_all_t, x, dy, ada3, norm_g, *chip_sums)
    return res[:4], res[4:]


def _sum_small(vec_all, qg_parts, kg_parts):
    def body(v_ref, q_ref, k_ref, tot_ref, gq_ref, gk_ref):
        tot = v_ref[0:1, :]
        for p in range(1, N_DEV):
            tot = tot + v_ref[p:p + 1, :]
        tot_ref[...] = tot
        gq_ref[...] = jnp.sum(q_ref[...], axis=0, keepdims=True)
        gk_ref[...] = jnp.sum(k_ref[...], axis=0, keepdims=True)

    n = vec_all.shape[-1]
    return pl.pallas_call(
        body, name="sum_small",
        out_shape=[jax.ShapeDtypeStruct((1, n), F32),
                   jax.ShapeDtypeStruct((1, HEAD_DIM), F32), jax.ShapeDtypeStruct((1, HEAD_DIM), F32)],
        compiler_params=_params(),
    )(vec_all, qg_parts, kg_parts)


def _grad_w_ada(c_cols, dada_rows):
    def body(c_ref, d_ref, out_ref):
        acc = c_ref[0] * d_ref[0]
        for b in range(1, N_DEV):
            acc = acc + c_ref[b] * d_ref[b]
        out_ref[...] = acc

    return pl.pallas_call(
        body, name="grad_w_ada",
        out_shape=jax.ShapeDtypeStruct((D_MODEL, ADA_SHARD), F32),
        compiler_params=_params(),
    )(c_cols, dada_rows)


def _adam_step(w, m, v, g):
    c1 = 1.0 / (1.0 - ADAM_B1 ** ADAM_STEP)
    c2 = 1.0 / (1.0 - ADAM_B2 ** ADAM_STEP)
    m_new = ADAM_B1 * m + (1.0 - ADAM_B1) * g
    v_new = ADAM_B2 * v + (1.0 - ADAM_B2) * (g * g)
    return -ADAM_LR * ((m_new * c1) / (jnp.sqrt(v_new * c2) + ADAM_EPS) + ADAM_WD * w), m_new, v_new


def _adamw_small(params, name):
    n = len(params)
    stacked = [p[3].ndim == p[0].ndim + 1 for p in params]

    def body(*refs):
        ins, outs = refs[:4 * n], refs[4 * n:]
        for k in range(n):
            w_ref, m_ref, v_ref, g_ref = ins[4 * k:4 * k + 4]
            go_ref, d_ref, mo_ref, vo_ref = outs[4 * k:4 * k + 4]
            if stacked[k]:
                g = g_ref[0].astype(F32)
                for p in range(1, g_ref.shape[0]):
                    g = g + g_ref[p].astype(F32)
            else:
                g = g_ref[...]
            go_ref[...] = g
            d_ref[...], mo_ref[...], vo_ref[...] = _adam_step(w_ref[...], m_ref[...], v_ref[...], g)

    res = pl.pallas_call(
        body, name=name,
        out_shape=[jax.ShapeDtypeStruct(p[0].shape, F32) for p in params for _ in range(4)],
        compiler_params=_params(),
    )(*[a for p in params for a in p])
    return [tuple(res[4 * k:4 * k + 4]) for k in range(n)]


def _adamw(w, m, v, g_parts, name):
    rows, cols = w.shape
    n_parts = g_parts.shape[0]
    tr = 256 if rows % 256 == 0 else rows
    tc = 256 if (tr == rows and rows > 256 and cols % 256 == 0) else cols

    def body(w_ref, m_ref, v_ref, g_ref, go_ref, d_ref, mo_ref, vo_ref):
        g = g_ref[0].astype(F32)
        for p in range(1, n_parts):
            g = g + g_ref[p].astype(F32)
        go_ref[...] = g
        d_ref[...], mo_ref[...], vo_ref[...] = _adam_step(w_ref[...], m_ref[...], v_ref[...], g)

    blk = pl.BlockSpec((tr, tc), lambda i, j: (i, j))
    return pl.pallas_call(
        body, name=name, grid=(rows // tr, cols // tc),
        in_specs=[blk, blk, blk, pl.BlockSpec((n_parts, tr, tc), lambda i, j: (0, i, j))],
        out_specs=[blk] * 4,
        out_shape=[jax.ShapeDtypeStruct((rows, cols), F32)] * 4,
        compiler_params=_params(("parallel", "parallel")),
    )(w, m, v, g_parts)


_O_F = 1536


W_TILE = 16
WIN_ROWS = 784


def _internal_start(p):
    return p * IN_SHARD - (HEADS if p * IN_SHARD > _O_F else 0)


def _shard_window(wt_shard, me):
    lo = me * IN_SHARD
    o = lo + lax.broadcasted_iota(jnp.int32, (IN_SHARD, 1), 0)
    is_f = jnp.logical_and(o >= _O_F, o < _O_F + HEADS)
    start = lo - jnp.where(lo > _O_F, HEADS, 0)
    window = lax.dynamic_update_slice(jnp.zeros((WIN_ROWS + W_TILE, D_MODEL), BF16),
                                      jnp.where(is_f, 0.0, wt_shard).astype(BF16), (start % W_TILE, 0))
    first = jnp.clip(_O_F - lo, 0, IN_SHARD - W_TILE)
    near = lax.dynamic_slice(wt_shard, (first, 0), (W_TILE, D_MODEL))
    j = lax.broadcasted_iota(jnp.int32, (W_TILE, 1), 0)
    src = _O_F - lo + j
    ok = jnp.logical_and(j < HEADS, jnp.logical_and(src >= 0, src < IN_SHARD))
    f_tile = jnp.where(ok, jnp.roll(near, first - (_O_F - lo), axis=0), 0.0).astype(BF16)
    return lax.dynamic_update_slice(window, f_tile, (WIN_ROWS, 0))


def _assemble_w(windows):
    chunk = 112

    def body(g_ref, out_ref):
        out_ref[WIN_ROWS:N_MAIN, :] = jnp.zeros((N_MAIN - WIN_ROWS, D_MODEL), BF16)
        for p in range(N_DEV):
            base = _internal_start(p) // W_TILE * W_TILE
            for r in range(0, WIN_ROWS, chunk):
                rows = slice(base + r, base + r + chunk)
                piece = g_ref[p, r:r + chunk, :]
                out_ref[rows, :] = piece if p == 0 else out_ref[rows, :] + piece
        f = g_ref[0, WIN_ROWS:WIN_ROWS + W_TILE, :]
        for p in range(1, N_DEV):
            f = f + g_ref[p, WIN_ROWS:WIN_ROWS + W_TILE, :]
        out_ref[N_MAIN:N_MAIN + W_TILE, :] = f
        out_ref[N_MAIN + W_TILE:N_ALL, :] = jnp.zeros((N_FPAD - W_TILE, D_MODEL), BF16)

    return pl.pallas_call(
        body, name="assemble_w", out_shape=jax.ShapeDtypeStruct((N_ALL, D_MODEL), BF16),
        compiler_params=_params(),
    )(windows)


def _slabs_by_core(dwt, dwt_f):
    sources = ((dwt, 0, _O_F, 0), (dwt_f, _O_F, _O_F + HEADS, _O_F), (dwt, _O_F + HEADS, IN_WIDTH, HEADS))

    def slab(p):
        lo, hi = p * IN_SHARD, (p + 1) * IN_SHARD
        parts = []
        for src, o_lo, o_hi, shift in sources:
            a, b = max(lo, o_lo), min(hi, o_hi)
            if a < b:
                parts.append(src[a - shift:b - shift])
        return parts[0] if len(parts) == 1 else jnp.concatenate(parts, axis=0)

    return jnp.stack([jnp.stack([slab(2 * chip + core) for chip in range(4)]) for core in range(2)])


def kernel(x, c, w_ada, b_ada, norm_g, w_in, b_f, q_norm_g, k_norm_g, conv_w, w_attn_out, w_conv_out, w_o, loss_target, m_w_ada, m_b_ada, m_norm_g, m_w_in, m_b_f, m_q_norm_g, m_k_norm_g, m_conv_w, m_w_attn_out, m_w_conv_out, m_w_o, v_w_ada, v_b_ada, v_norm_g, v_w_in, v_b_f, v_q_norm_g, v_k_norm_g, v_conv_w, v_w_attn_out, v_w_conv_out, v_w_o):
    me = 4 * lax.axis_index("x") + 2 * lax.axis_index("y") + lax.axis_index("c")
    s = x.shape[1]
    x2, t2 = x[0], loss_target[0]

    w_in_g, c_all, ada_g = _gather_weights_and_ada(_shard_window(w_in[0].T, me), c, w_ada[0])
    ada_mine = lax.dynamic_index_in_dim(ada_g[:, :, 0, :], me, axis=1, keepdims=False)
    ada3 = (ada_mine.reshape(1, 3 * D_MODEL) + b_ada).reshape(3, D_MODEL)
    w_all_t = _assemble_w(w_in_g)
    qg = jnp.tile(q_norm_g, (1, HEADS))
    kg = jnp.tile(k_norm_g, (1, HEADS))
    bf_pad = jnp.pad(b_f, ((0, 0), (0, LANES - HEADS)))

    (proj, fl, h), (qa, ka, va, kt, vt), (cw_g, wa_g, wb_g, wo_g) = _proj_fwd(
        x2, ada3, norm_g, w_all_t, bf_pad, qg, kg,
        [conv_w[0], w_attn_out[0].astype(BF16), w_conv_out[0].astype(BF16), w_o[0].astype(BF16)])
    wa = jnp.transpose(wa_g, (1, 0, 2)).reshape(ATTN_W, D_MODEL)
    wb = jnp.transpose(wb_g, (1, 0, 2)).reshape(CONV_W, D_MODEL)
    wo = wo_g.reshape(D_MODEL, D_MODEL)
    cw = jnp.transpose(cw_g, (1, 0, 2)).reshape(3, CONV_W)
    attn, oa, qb = _attn_fwd(qa, ka, vt, proj)
    (dy, dgab, do, dza, dob, dwo, dwa, dwb, dgate, loss_part) = _tail(oa, attn, proj, x2, t2, ada3, wa, wb, wo, cw)

    small = [dwa, dwb, dwo.reshape(N_DEV, D_MODEL // N_DEV, D_MODEL)]
    dqkv, dqg, dkg, dcum = _attn_bwd(qb, ka, kt, va, do, proj, qg, kg)
    df, dbf = _forget_bwd(dcum, fl, bf_pad)
    dconv, dcw = _conv_bwd(dob, proj, cw)
    pieces = [dqkv, dza, dconv, dgab, df]
    (dw_main, dw_f), (g_wa_parts, g_wb_parts, g_wo_parts) = _dw_in(h, pieces, small)

    pair_in = _sibling_swap_sum(_slabs_by_core(dw_main, dw_f), "swap_sum_w_in")
    (grad_x, dshift, dscale, dnormg), (g_in_parts,) = _dh_and_dx(
        pieces, w_all_t, x2, dy, ada3, norm_g, [pair_in])
    vec = jnp.concatenate([dshift, dscale, dgate, dnormg, dbf, dcw.reshape(1, 3 * CONV_W), loss_part, dqg, dkg],
                          axis=1)
    (vec_all,) = _gather_direct([vec], "gather_small")
    vec_all = vec_all.reshape(N_DEV, vec.shape[1])
    n_main = 4 * D_MODEL + LANES + 3 * CONV_W + LANES
    tot, g_qg, g_kg = _sum_small(
        vec_all[:, :n_main],
        vec_all[:, n_main:n_main + ATTN_W].reshape(N_DEV * HEADS, HEAD_DIM),
        vec_all[:, n_main + ATTN_W:].reshape(N_DEV * HEADS, HEAD_DIM))
    g_b_ada = tot[:, 0:3 * D_MODEL]
    g_norm_g = tot[:, 3 * D_MODEL:4 * D_MODEL]
    g_b_f = tot[:, 4 * D_MODEL:4 * D_MODEL + HEADS]
    g_cw_full = tot[:, 4 * D_MODEL + LANES:4 * D_MODEL + LANES + 3 * CONV_W].reshape(3, CONV_W)
    g_cw = lax.dynamic_slice(g_cw_full, (0, me * (CONV_W // N_DEV)), (3, CONV_W // N_DEV))
    dada_mine = lax.dynamic_slice(vec_all[:, 0:3 * D_MODEL], (0, me * ADA_SHARD), (N_DEV, ADA_SHARD))
    g_w_ada = _grad_w_ada(jnp.transpose(c_all, (0, 2, 1)), dada_mine.reshape(N_DEV, 1, ADA_SHARD))

    upd = {}
    upd["w_ada"] = _adamw(w_ada[0], m_w_ada[0], v_w_ada[0], g_w_ada[None], "adamw_w_ada")
    upd["w_in"] = [u.T for u in _adamw(w_in[0].T, m_w_in[0].T, v_w_in[0].T, g_in_parts, "adamw_w_in")]
    small_names = ["b_ada", "norm_g", "b_f", "q_norm_g", "k_norm_g", "conv_w", "w_attn_out", "w_conv_out", "w_o"]
    small_upd = _adamw_small(
        [(b_ada, m_b_ada, v_b_ada, g_b_ada), (norm_g, m_norm_g, v_norm_g, g_norm_g), (b_f, m_b_f, v_b_f, g_b_f),
         (q_norm_g, m_q_norm_g, v_q_norm_g, g_qg), (k_norm_g, m_k_norm_g, v_k_norm_g, g_kg),
         (conv_w[0], m_conv_w[0], v_conv_w[0], g_cw),
         (w_attn_out[0], m_w_attn_out[0], v_w_attn_out[0], g_wa_parts),
         (w_conv_out[0], m_w_conv_out[0], v_w_conv_out[0], g_wb_parts),
         (w_o[0], m_w_o[0], v_w_o[0], g_wo_parts)], "adamw_small")
    upd.update(zip(small_names, small_upd))

    names = ["w_ada", "b_ada", "norm_g", "w_in", "b_f", "q_norm_g", "k_norm_g", "conv_w",
             "w_attn_out", "w_conv_out", "w_o"]
    lead = {"w_ada", "w_in", "conv_w", "w_attn_out", "w_conv_out", "w_o"}
    fix = lambda n, a: a[None] if n in lead else a
    loss = tot[0, n_main - LANES]
    outs = [loss, grad_x[None]]
    for k in range(4):
        outs += [fix(n, upd[n][k]) for n in names]
    return tuple(outs)
```

```python
import numpy as np
import jax
import jax.numpy as jnp
from jax import lax
from jax.experimental import pallas as pl
from jax.experimental.pallas import tpu as pltpu

F32 = jnp.float32
BF16 = jnp.bfloat16

D_MODEL = 1024
HEADS = 8
HEAD_DIM = 64
ATTN_W = 512
CONV_W = 512
N_DEV = 8
IN_WIDTH = 6152
IN_SHARD = IN_WIDTH // N_DEV
N_MAIN = 6144
N_FPAD = 128
N_ALL = N_MAIN + N_FPAD
ADA_SHARD = 3 * D_MODEL // N_DEV
EPS = 1e-6
NEG = -1e30

ADAM_LR = 0.001
ADAM_B1 = 0.9
ADAM_B2 = 0.999
ADAM_EPS = 1e-08
ADAM_WD = 0.01
ADAM_STEP = 10

LANES = 128
VMEM_LIMIT = 56 * 1024 * 1024

TM_PROJ = 512
TN_PROJ = 1024
TM_ELEM = 512
TQ = 512
HEADS_PER_STEP = 8
TM_TAIL = 256
TC_CUM = 512
TK_DW = 2048
TN_DW = 512
TM_DH = 256
HALO = 16

OFF_Q, OFF_K, OFF_V, OFF_ZA, OFF_CB, OFF_CC, OFF_CU, OFF_CZ, OFF_GA, OFF_GB = (
    0, 512, 1024, 1536, 2048, 2560, 3072, 3584, 4096, 5120)


def _params(sem=None):
    return pltpu.CompilerParams(dimension_semantics=sem, vmem_limit_bytes=VMEM_LIMIT)


def _dot(a, b):
    return jnp.dot(a, b, preferred_element_type=F32)


def _dot_nt(a, b):
    return lax.dot_general(a, b, (((1,), (1,)), ((), ())), preferred_element_type=F32)


def _dot_tn(a, b):
    return lax.dot_general(a, b, (((0,), (0,)), ((), ())), preferred_element_type=F32)


def _sigmoid(x):
    return 1.0 / (1.0 + jnp.exp(-x))


def _seg_sum(z, lo):
    a = jnp.sum(jnp.where(lo, z, 0.0), axis=-1, keepdims=True)
    b = jnp.sum(jnp.where(lo, 0.0, z), axis=-1, keepdims=True)
    return jnp.where(lo, a, b)


def _lane_col(z, lane):
    idx = lax.broadcasted_iota(jnp.int32, z.shape, 1)
    return jnp.sum(jnp.where(idx == lane, z, 0.0), axis=-1, keepdims=True)


def _sub_row(z, row):
    idx = lax.broadcasted_iota(jnp.int32, z.shape, 0)
    return jnp.sum(jnp.where(idx == row, z, 0.0), axis=0, keepdims=True)


def _mesh_pos():
    x, y, c = lax.axis_index("x"), lax.axis_index("y"), lax.axis_index("c")
    return x, y, c, 4 * x + 2 * y + c


def _peer(k, x, y, c):
    px = 1 - x if (k >> 2) & 1 else x
    py = 1 - y if (k >> 1) & 1 else y
    pc = 1 - c if k & 1 else c
    return (px, py, pc), 4 * px + 2 * py + pc


def _gather_copies(ins, outs, send_sems, recv_sems, local_sems):
    x, y, c, me = _mesh_pos()
    copies = []
    for a in range(len(ins)):
        copies.append(pltpu.make_async_copy(ins[a], outs[a].at[me], local_sems.at[a]))
        for k in range(1, N_DEV):
            dev, _ = _peer(k, x, y, c)
            copies.append(pltpu.make_async_remote_copy(
                src_ref=ins[a], dst_ref=outs[a].at[me],
                send_sem=send_sems.at[a * (N_DEV - 1) + k - 1], recv_sem=recv_sems.at[a * (N_DEV - 1) + k - 1],
                device_id=dev, device_id_type=pl.DeviceIdType.MESH))
    return copies


def _gather_sems(n):
    return [pltpu.SemaphoreType.DMA((n * (N_DEV - 1),)), pltpu.SemaphoreType.DMA((n * (N_DEV - 1),)),
            pltpu.SemaphoreType.DMA((n,))]


def _gather_direct(arrs, name):
    n = len(arrs)
    any_spec = pl.BlockSpec(memory_space=pl.ANY)

    def body(*refs):
        copies = _gather_copies(refs[:n], refs[n:2 * n], *refs[2 * n:])
        for cp in copies:
            cp.start()
        for cp in copies:
            cp.wait()

    return pl.pallas_call(
        body, name=name, out_shape=[jax.ShapeDtypeStruct((N_DEV,) + a.shape, a.dtype) for a in arrs],
        in_specs=[any_spec] * n, out_specs=[any_spec] * n, scratch_shapes=_gather_sems(n),
    )(*arrs)


def _ada_phase(c_ref, w_ref, call_ref, adag_ref, mine_ref, send_sems, recv_sems):
    x, y, c, me = _mesh_pos()

    def copy(phase, k, src, dst):
        dev, _ = _peer(k, x, y, c)
        return pltpu.make_async_remote_copy(
            src_ref=src, dst_ref=dst,
            send_sem=send_sems.at[phase * (N_DEV - 1) + k - 1],
            recv_sem=recv_sems.at[phase * (N_DEV - 1) + k - 1],
            device_id=dev, device_id_type=pl.DeviceIdType.MESH)

    call_ref[me] = c_ref[...]
    first = [copy(0, k, c_ref, call_ref.at[me]) for k in range(1, N_DEV)]
    for cp in first:
        cp.start()
    for cp in first:
        cp.wait()
    wb = w_ref[...].astype(BF16)
    for b in range(N_DEV):
        row = jnp.broadcast_to(call_ref[b], (8, D_MODEL)).astype(BF16)
        mine_ref[b] = _sub_row(_dot(row, wb), 0)
    adag_ref[me] = mine_ref[...]
    second = [copy(1, k, mine_ref, adag_ref.at[me]) for k in range(1, N_DEV)]
    for cp in second:
        cp.start()
    for cp in second:
        cp.wait()


def _gather_weights_and_ada(wt_shard, c_row, w_ada_sh):
    any_spec = pl.BlockSpec(memory_space=pl.ANY)
    vm = pl.BlockSpec(memory_space=pltpu.VMEM)

    def body(w_in_ref, c_ref, wada_ref, out_ref, call_ref, adag_ref, mine_ref, send_sems, recv_sems, local_sem,
             ada_send, ada_recv):
        x, y, c, me = _mesh_pos()
        sibling = (x, y, 1 - c)
        chips = [(1 - x, y), (x, 1 - y), (1 - x, 1 - y)]

        def copy(k, src, blk, to):
            return pltpu.make_async_remote_copy(
                src_ref=src, dst_ref=out_ref.at[blk], send_sem=send_sems.at[k], recv_sem=recv_sems.at[k],
                device_id=to, device_id_type=pl.DeviceIdType.MESH)

        local = pltpu.make_async_copy(w_in_ref, out_ref.at[me], local_sem.at[0])
        local.start()
        first = [copy(0, w_in_ref, me, sibling)]
        first += [copy(1 + j, w_in_ref, me, (px, py, c)) for j, (px, py) in enumerate(chips)]
        for cp in first:
            cp.start()
        _ada_phase(c_ref, wada_ref, call_ref, adag_ref, mine_ref, ada_send, ada_recv)
        passed = []
        for j, (px, py) in enumerate(chips):
            blk = 4 * px + 2 * py + c
            copy(1 + j, w_in_ref, blk, (x, y, c)).wait_recv()
            fwd = copy(4 + j, out_ref.at[blk], blk, sibling)
            fwd.start()
            passed.append(fwd)
        copy(0, w_in_ref, 4 * x + 2 * y + 1 - c, (x, y, c)).wait_recv()
        for j, (px, py) in enumerate(chips):
            copy(4 + j, w_in_ref, 4 * px + 2 * py + 1 - c, (x, y, c)).wait_recv()
        for cp in first + passed:
            cp.wait_send()
        local.wait()

    per = N_DEV - 1
    return pl.pallas_call(
        body, name="gather_weights",
        out_shape=[jax.ShapeDtypeStruct((N_DEV,) + wt_shard.shape, wt_shard.dtype),
                   jax.ShapeDtypeStruct((N_DEV, 1, D_MODEL), F32),
                   jax.ShapeDtypeStruct((N_DEV, N_DEV, 1, ADA_SHARD), F32)],
        in_specs=[any_spec, vm, vm], out_specs=[any_spec, vm, vm],
        scratch_shapes=[pltpu.VMEM((N_DEV, 1, ADA_SHARD), F32),
                        pltpu.SemaphoreType.DMA((per,)), pltpu.SemaphoreType.DMA((per,)),
                        pltpu.SemaphoreType.DMA((1,)),
                        pltpu.SemaphoreType.DMA((2 * per,)), pltpu.SemaphoreType.DMA((2 * per,))],
        compiler_params=pltpu.CompilerParams(vmem_limit_bytes=VMEM_LIMIT),
    )(wt_shard, c_row, w_ada_sh)


def _sibling_swap_sum(dwt, dwt_f, name):
    nch, rows = N_DEV // 2, WIN_ROWS + W_TILE
    any_spec = pl.BlockSpec(memory_space=pl.ANY)

    def body(main_ref, f_ref, out_ref, mine_v, theirs_v, sum_v, send_sems, recv_sems, load_sems, store_sems):
        x, y, c, _ = _mesh_pos()

        def parts(core, ch, buf):
            b0, b1 = (_internal_start(2 * ch + k) // W_TILE * W_TILE for k in range(2))
            base = pl.multiple_of(jnp.where(core == 0, b0, b1), W_TILE)
            return [(main_ref.at[pl.ds(base, WIN_ROWS)], buf.at[ch, pl.ds(0, WIN_ROWS)]),
                    (f_ref.at[pl.ds(0, W_TILE)], buf.at[ch, pl.ds(WIN_ROWS, W_TILE)])]

        swaps, loads, stores = [], [], []
        for ch in range(nch):
            for k, (src, dst) in enumerate(parts(1 - c, ch, theirs_v)):
                swaps.append(pltpu.make_async_remote_copy(
                    src_ref=src, dst_ref=dst, send_sem=send_sems.at[2 * ch + k], recv_sem=recv_sems.at[2 * ch + k],
                    device_id=(x, y, 1 - c), device_id_type=pl.DeviceIdType.MESH))
            for k, (src, dst) in enumerate(parts(c, ch, mine_v)):
                loads.append(pltpu.make_async_copy(src, dst, load_sems.at[2 * ch + k]))
            stores.append(pltpu.make_async_copy(sum_v.at[ch], out_ref.at[ch], store_sems.at[ch]))
        for cp in swaps + loads:
            cp.start()
        for ch in range(nch):
            for cp in loads[2 * ch:2 * ch + 2] + swaps[2 * ch:2 * ch + 2]:
                cp.wait()
            sum_v[ch] = (mine_v[ch].astype(F32) + theirs_v[ch].astype(F32)).astype(BF16)
            stores[ch].start()
        for cp in stores:
            cp.wait()

    return pl.pallas_call(
        body, name=name,
        out_shape=jax.ShapeDtypeStruct((nch, rows, D_MODEL), BF16),
        in_specs=[any_spec, any_spec], out_specs=any_spec,
        scratch_shapes=[pltpu.VMEM((nch, rows, D_MODEL), BF16)] * 3
        + [pltpu.SemaphoreType.DMA((2 * nch,))] * 3 + [pltpu.SemaphoreType.DMA((nch,))],
        compiler_params=_params(),
    )(dwt, dwt_f)


def _all_to_all_copies(ins, outs, send_sems, recv_sems, local_sems):
    x, y, c, me = _mesh_pos()
    copies = []
    for a in range(len(ins)):
        copies.append(pltpu.make_async_copy(ins[a].at[me], outs[a].at[me], local_sems.at[a]))
        for k in range(1, N_DEV):
            dev, p = _peer(k, x, y, c)
            copies.append(pltpu.make_async_remote_copy(
                src_ref=ins[a].at[p], dst_ref=outs[a].at[me],
                send_sem=send_sems.at[a * (N_DEV - 1) + k - 1], recv_sem=recv_sems.at[a * (N_DEV - 1) + k - 1],
                device_id=dev, device_id_type=pl.DeviceIdType.MESH))
    return copies


def _chip_copies(ins, outs, send_sems, recv_sems, local_sems):
    x, y, c, _ = _mesh_pos()
    my_chip = 2 * x + y
    chips = [(1 - x, y), (x, 1 - y), (1 - x, 1 - y)]
    copies = []
    for a in range(len(ins)):
        copies.append(pltpu.make_async_copy(ins[a].at[my_chip], outs[a].at[my_chip], local_sems.at[a]))
        for j, (px, py) in enumerate(chips):
            copies.append(pltpu.make_async_remote_copy(
                src_ref=ins[a].at[2 * px + py], dst_ref=outs[a].at[my_chip],
                send_sem=send_sems.at[a * 3 + j], recv_sem=recv_sems.at[a * 3 + j],
                device_id=(px, py, c), device_id_type=pl.DeviceIdType.MESH))
    return copies


def _proj_fwd(x, ada3, norm_g, w_all_t, bf_pad, qg, kg, later):
    s = x.shape[0]
    tm, tn = min(TM_PROJ, s), TN_PROJ
    nt = s // tm
    n = len(later)

    def body(x_ref, ada_ref, g_ref, wt_ref, bf_ref, qg_ref, kg_ref, *rest):
        ins, (proj_ref, fl_ref, h_ref), rows_refs, rest = rest[:n], rest[n:n + 3], rest[n + 3:n + 8], rest[n + 8:]
        outs, (carry, send_sems, recv_sems, local_sems) = rest[:n], rest[n:]
        i = pl.program_id(0)

        @pl.when(i == 0)
        def _():
            carry[...] = jnp.zeros_like(carry)
            for cp in _gather_copies(ins, outs, send_sems, recv_sems, local_sems):
                cp.start()

        xv = x_ref[...]
        r = lax.rsqrt(jnp.mean(xv * xv, axis=-1, keepdims=True) + EPS)
        hv = ((xv * r) * g_ref[...]) * (1.0 + ada_ref[1:2, :]) + ada_ref[0:1, :]
        hb = hv.astype(BF16)
        h_ref[...] = hb
        fl = _dot_nt(hb, wt_ref[N_MAIN:N_ALL, :])
        fl_ref[...] = fl
        for j in range(N_MAIN // tn):
            proj_ref[:, j * tn:(j + 1) * tn] = _dot_nt(hb, wt_ref[j * tn:(j + 1) * tn, :]).astype(BF16)
        _attention_rows(proj_ref, fl, bf_ref, qg_ref, kg_ref, carry, *rows_refs)

        @pl.when(i == nt - 1)
        def _():
            for cp in _gather_copies(ins, outs, send_sems, recv_sems, local_sems):
                cp.wait()

    any_spec = pl.BlockSpec(memory_space=pl.ANY)
    heads = pl.BlockSpec((HEADS, tm, LANES), lambda i: (0, i, 0))
    heads_t = pl.BlockSpec((HEADS, LANES, tm), lambda i: (0, 0, i))
    vec = pl.BlockSpec((1, ATTN_W), lambda i: (0, 0))
    res = pl.pallas_call(
        body, name="proj_fwd", grid=(nt,),
        in_specs=[pl.BlockSpec((tm, D_MODEL), lambda i: (i, 0)),
                  pl.BlockSpec((3, D_MODEL), lambda i: (0, 0)),
                  pl.BlockSpec((1, D_MODEL), lambda i: (0, 0)),
                  pl.BlockSpec((N_ALL, D_MODEL), lambda i: (0, 0)),
                  pl.BlockSpec((1, LANES), lambda i: (0, 0)), vec, vec] + [any_spec] * n,
        out_specs=[pl.BlockSpec((tm, N_MAIN), lambda i: (i, 0)),
                   pl.BlockSpec((tm, N_FPAD), lambda i: (i, 0)),
                   pl.BlockSpec((tm, D_MODEL), lambda i: (i, 0)),
                   heads, heads, heads, heads_t, heads_t] + [any_spec] * n,
        out_shape=[jax.ShapeDtypeStruct((s, N_MAIN), BF16),
                   jax.ShapeDtypeStruct((s, N_FPAD), F32),
                   jax.ShapeDtypeStruct((s, D_MODEL), BF16)]
        + [jax.ShapeDtypeStruct((HEADS, s, LANES), BF16)] * 3
        + [jax.ShapeDtypeStruct((HEADS, LANES, s), BF16)] * 2
        + [jax.ShapeDtypeStruct((N_DEV,) + a.shape, a.dtype) for a in later],
        scratch_shapes=[pltpu.VMEM((1, LANES), F32)] + _gather_sems(n),
        compiler_params=_params(("arbitrary",)),
    )(x, ada3, norm_g, w_all_t, bf_pad, qg, kg, *later)
    return res[:3], res[3:8], res[8:]


L_ONE_Q, L_F_Q, L_LSE_Q, L_END = HEAD_DIM, HEAD_DIM + 3, HEAD_DIM + 6, HEAD_DIM + 9


def _split3(f):
    hi = f.astype(BF16).astype(F32)
    r = f - hi
    mid = r.astype(BF16).astype(F32)
    return hi, mid, r - mid


def _place3(lane, first, parts, otherwise):
    a, b, c = parts
    return jnp.where(lane == first, a, jnp.where(lane == first + 1, b, jnp.where(lane == first + 2, c, otherwise)))


def _log_forget(fl, bf):
    z = fl + bf
    lf = jnp.minimum(z, 0.0) - jnp.log1p(jnp.exp(-jnp.abs(z)))
    lane = lax.broadcasted_iota(jnp.int32, z.shape, 1)
    return jnp.where(lane < HEADS, lf, 0.0)


def _attention_rows(p_ref, fl, bf_ref, qg_ref, kg_ref, carry, qa_ref, ka_ref, va_ref, kt_ref, vt_ref):
    tm = fl.shape[0]
    scale = HEAD_DIM ** -0.5
    tri = (lax.broadcasted_iota(jnp.int32, (tm, tm), 1) <= lax.broadcasted_iota(jnp.int32, (tm, tm), 0)).astype(F32)
    cum_v = jnp.dot(tri, _log_forget(fl, bf_ref[...]), preferred_element_type=F32,
                    precision=lax.Precision.HIGHEST) + carry[...]
    carry[...] = _sub_row(cum_v, tm - 1)
    lane = lax.broadcasted_iota(jnp.int32, (tm, LANES), 1)
    lo = lane < HEAD_DIM
    v_tail = jnp.where(lane < L_F_Q, 1.0, 0.0)
    for pr in range(ATTN_W // LANES):
        sl = slice(pr * LANES, (pr + 1) * LANES)
        q2 = p_ref[:, OFF_Q + pr * LANES:OFF_Q + (pr + 1) * LANES].astype(F32)
        k2 = p_ref[:, OFF_K + pr * LANES:OFF_K + (pr + 1) * LANES].astype(F32)
        v2 = p_ref[:, OFF_V + pr * LANES:OFF_V + (pr + 1) * LANES].astype(F32)
        rq = lax.rsqrt(_seg_sum(q2 * q2, lo) * (1.0 / HEAD_DIM) + EPS)
        rk = lax.rsqrt(_seg_sum(k2 * k2, lo) * (1.0 / HEAD_DIM) + EPS)
        qn = ((q2 * rq) * qg_ref[:, sl]) * scale
        kn = (k2 * rk) * kg_ref[:, sl]
        for hh in range(2):
            h = 2 * pr + hh
            f3 = _split3(_lane_col(cum_v, h))
            qh = qn if hh == 0 else pltpu.roll(qn, HEAD_DIM, 1)
            kh = kn if hh == 0 else pltpu.roll(kn, HEAD_DIM, 1)
            vh = v2 if hh == 0 else pltpu.roll(v2, HEAD_DIM, 1)
            q_tail = jnp.where(lane < L_F_Q, 1.0, _place3(lane, L_F_Q, f3, 0.0))
            k_tail = _place3(lane, L_ONE_Q, tuple(-f for f in f3), jnp.where(lane < L_END, 1.0, 0.0))
            k_row = jnp.where(lo, kh, k_tail)
            v_row = jnp.where(lo, vh, v_tail)
            qa_ref[h] = jnp.where(lo, qh, q_tail).astype(BF16)
            ka_ref[h] = k_row.astype(BF16)
            va_ref[h] = v_row.astype(BF16)
            kt_ref[h] = k_row.T.astype(BF16)
            vt_ref[h] = v_row.T.astype(BF16)


def _causal_t(t):
    return lax.broadcasted_iota(jnp.int32, (t, t), 0) <= lax.broadcasted_iota(jnp.int32, (t, t), 1)


def _tri_steps(nt, q_major):
    if q_major:
        pairs = [(i, j) for i in range(nt) for j in range(i + 1)]
    else:
        pairs = [(i, j) for j in range(nt) for i in range(j, nt)]
    return (jnp.asarray(np.array([p[0] for p in pairs], np.int32)),
            jnp.asarray(np.array([p[1] for p in pairs], np.int32)))


def _attn_fwd(qa, ka, vt, proj):
    s = qa.shape[1]
    t = min(TQ, s)
    it, jt = _tri_steps(s // t, True)
    hp = HEADS_PER_STEP
    wide = hp * HEAD_DIM
    za_blk = OFF_ZA // wide

    def body(it_ref, jt_ref, q_ref, k_ref, vt_ref, za_ref, attn_ref, oa_ref, qb_ref, m_s, acc_s, pair_s):
        step = pl.program_id(1)
        i, j = it_ref[step], jt_ref[step]

        @pl.when(j == 0)
        def _():
            m_s[...] = jnp.full_like(m_s, NEG)
            acc_s[...] = jnp.zeros_like(acc_s)

        def update(masked):
            for hh in range(hp):
                st = _dot_nt(k_ref[hh], q_ref[hh])
                if masked:
                    st = jnp.where(_causal_t(t), st, NEG)
                m_prev = m_s[hh]
                m_next = jnp.maximum(m_prev, jnp.max(st, axis=0, keepdims=True))
                alpha = jnp.exp(m_prev - m_next)
                pt = jnp.exp(st - m_next).astype(BF16)
                acc_s[hh] = acc_s[hh] * alpha + _dot(vt_ref[hh], pt)
                m_s[hh] = m_next

        @pl.when(j < i)
        def _():
            update(False)

        @pl.when(j == i)
        def _():
            update(True)
            row = lax.broadcasted_iota(jnp.int32, (LANES, t), 0)
            lane = lax.broadcasted_iota(jnp.int32, (t, LANES), 1)
            for hh in range(hp):
                l_row = acc_s[hh, L_ONE_Q:L_ONE_Q + 1, :]
                pair_s[hh * HEAD_DIM:(hh + 1) * HEAD_DIM, :] = acc_s[hh, 0:HEAD_DIM, :] / l_row
                lse3 = _split3(m_s[hh] + jnp.log(l_row))
                tail_t = _place3(row, L_LSE_Q, tuple(-x for x in lse3), 0.0)
                keep_q = jnp.logical_or(lane < L_LSE_Q, lane >= L_END)
                qb_ref[hh] = jnp.where(keep_q, q_ref[hh].astype(F32), tail_t.T).astype(BF16)
            out = pair_s[...].T
            attn_ref[...] = out
            z = za_ref[...].astype(F32)
            oa_ref[...] = (out * (z * _sigmoid(z))).astype(BF16)

    pair_q = pl.BlockSpec((hp, t, LANES), lambda p, n, it_, jt_: (p, it_[n], 0))
    pair_k = pl.BlockSpec((hp, t, LANES), lambda p, n, it_, jt_: (p, jt_[n], 0))
    pair_kt = pl.BlockSpec((hp, LANES, t), lambda p, n, it_, jt_: (p, 0, jt_[n]))
    out_q = pl.BlockSpec((t, wide), lambda p, n, it_, jt_: (it_[n], p))
    return pl.pallas_call(
        body, name="attn_fwd",
        grid_spec=pltpu.PrefetchScalarGridSpec(
            num_scalar_prefetch=2, grid=(HEADS // hp, it.shape[0]),
            in_specs=[pair_q, pair_k, pair_kt,
                      pl.BlockSpec((t, wide), lambda p, n, it_, jt_: (it_[n], za_blk + p))],
            out_specs=[out_q, out_q, pair_q],
            scratch_shapes=[pltpu.VMEM((hp, 1, t), F32), pltpu.VMEM((hp, LANES, t), F32),
                            pltpu.VMEM((wide, t), F32)]),
        out_shape=[jax.ShapeDtypeStruct((s, ATTN_W), F32),
                   jax.ShapeDtypeStruct((s, ATTN_W), BF16),
                   jax.ShapeDtypeStruct((HEADS, s, LANES), BF16)],
        compiler_params=_params(("parallel", "arbitrary")),
    )(it, jt, qa, ka, vt, proj)


def _conv_parts(gb_ref, gc_ref, u_ref, zb_ref, gch_ref, uh_ref, first, w_ref, tm):
    gb, gc = gb_ref[...].astype(F32), gc_ref[...].astype(F32)
    u, zb = u_ref[...].astype(F32), zb_ref[...].astype(F32)
    cu = gc * u
    cu_h = jnp.where(first, 0.0, gch_ref[...].astype(F32) * uh_ref[...].astype(F32))
    prev1, prev2 = _sub_row(cu_h, HALO - 1), _sub_row(cu_h, HALO - 2)
    row = lax.broadcasted_iota(jnp.int32, cu.shape, 0)
    r1 = jnp.where(row == 0, prev1, pltpu.roll(cu, 1, 0))
    r2 = jnp.where(row == 0, prev2, jnp.where(row == 1, prev1, pltpu.roll(cu, 2, 0)))
    conv = w_ref[2:3, :] * cu + w_ref[1:2, :] * r1 + w_ref[0:1, :] * r2
    return gb, gc, u, zb, cu, r1, r2, conv


def _conv_specs(tm, s, width=LANES):
    def tile(off):
        return pl.BlockSpec((tm, width), lambda c, i: (i, off // width + c))

    def before(off):
        return pl.BlockSpec((HALO, width), lambda c, i: (jnp.maximum(i * (tm // HALO) - 1, 0), off // width + c))

    def after(off):
        return pl.BlockSpec((HALO, width),
                            lambda c, i: (jnp.minimum((i + 1) * (tm // HALO), s // HALO - 1), off // width + c))

    return ([tile(OFF_CB), tile(OFF_CC), tile(OFF_CU), tile(OFF_CZ)], [before(OFF_CC), before(OFF_CU)],
            [after(OFF_CB), after(OFF_CZ)])


def _tail(oa, attn, proj, x, target, ada3, wa, wb, wo, conv_w):
    s = x.shape[0]
    tm = min(TM_TAIL, s)
    gab_blk = OFF_GA // (2 * D_MODEL)
    za_blk = OFF_ZA // ATTN_W
    tiles, befores, _ = _conv_specs(tm, s, CONV_W)

    def body(oa_ref, attn_ref, za_ref, gb_ref, gc_ref, u_ref, zb_ref, gch_ref, uh_ref, cw_ref, gab_ref, x_ref, t_ref,
             ada_ref, wa_ref, wb_ref, wo_ref,
             dy_ref, dgab_ref, do_ref, dza_ref, dob_ref, dwo_out, dwa_out, dwb_out, dgate_ref, loss_ref,
             dwo_ref, dwa_ref, dwb_ref):
        first = pl.program_id(0) == 0

        @pl.when(first)
        def _():
            dwo_ref[...] = jnp.zeros_like(dwo_ref)
            dwa_ref[...] = jnp.zeros_like(dwa_ref)
            dwb_ref[...] = jnp.zeros_like(dwb_ref)
            dgate_ref[...] = jnp.zeros_like(dgate_ref)
            loss_ref[...] = jnp.zeros_like(loss_ref)

        gb, _, _, zb, _, _, _, conv = _conv_parts(gb_ref, gc_ref, u_ref, zb_ref, gch_ref, uh_ref, first, cw_ref, tm)
        ob_v = (gb * conv * (zb * _sigmoid(zb))).astype(BF16)
        oa_v = oa_ref[...]
        wa_v, wb_v, wo_v = wa_ref[...], wb_ref[...], wo_ref[...]
        a2 = _dot(oa_v, wa_v)
        b2 = _dot(ob_v, wb_v)
        sa = _sigmoid(gab_ref[:, 0:D_MODEL].astype(F32))
        sb = _sigmoid(gab_ref[:, D_MODEL:2 * D_MODEL].astype(F32))
        mb = (sa * a2 + sb * b2).astype(BF16)
        mo = _dot(mb, wo_v)
        gate = ada_ref[2:3, :]
        err = (x_ref[...] + gate * mo) - t_ref[...]
        dy = err * (1.0 / D_MODEL)
        dy_ref[...] = dy
        loss_ref[...] += 0.5 * jnp.sum(err * err) * (1.0 / D_MODEL)
        dgate_ref[...] += jnp.sum(dy * mo, axis=0, keepdims=True)
        dmo = (dy * gate).astype(BF16)
        dmerged = _dot_nt(dmo, wo_v)
        dwo_ref[...] += _dot_tn(mb, dmo)
        da2 = (dmerged * sa).astype(BF16)
        db2 = (dmerged * sb).astype(BF16)
        dgab_ref[:, 0:D_MODEL] = (dmerged * a2 * (sa * (1.0 - sa))).astype(BF16)
        dgab_ref[:, D_MODEL:2 * D_MODEL] = (dmerged * b2 * (sb * (1.0 - sb))).astype(BF16)
        doa = _dot_nt(da2, wa_v)
        dob_ref[...] = _dot_nt(db2, wb_v)
        dwa_ref[...] += _dot_tn(oa_v, da2)
        dwb_ref[...] += _dot_tn(ob_v, db2)

        lane = lax.broadcasted_iota(jnp.int32, (tm, LANES), 1)
        lo = lane < HEAD_DIM
        for pr in range(ATTN_W // LANES):
            sl = slice(pr * LANES, (pr + 1) * LANES)
            g, a, z = doa[:, sl], attn_ref[:, sl], za_ref[:, sl].astype(F32)
            sg = _sigmoid(z)
            dat = (g * (z * sg)).astype(BF16).astype(F32)
            prod = dat * a
            dza_ref[:, sl] = (g * a * (sg * (1.0 + z * (1.0 - sg)))).astype(BF16)
            for hh in range(2):
                sel = lo if hh == 0 else jnp.logical_not(lo)
                delta3 = _split3(jnp.sum(jnp.where(sel, prod, 0.0), axis=-1, keepdims=True))
                dh = dat if hh == 0 else pltpu.roll(dat, HEAD_DIM, 1)
                tail_lanes = _place3(lane, L_ONE_Q, tuple(-d for d in delta3), 0.0)
                do_ref[2 * pr + hh] = jnp.where(lo, dh, tail_lanes).astype(BF16)

        @pl.when(pl.program_id(0) == pl.num_programs(0) - 1)
        def _():
            dwo_out[...] = dwo_ref[...].astype(BF16)
            for p in range(N_DEV):
                dwa_out[p] = dwa_ref[:, p * LANES:(p + 1) * LANES].astype(BF16)
                dwb_out[p] = dwb_ref[:, p * LANES:(p + 1) * LANES].astype(BF16)

    half = pl.BlockSpec((tm, ATTN_W), lambda i: (i, 0))
    full = pl.BlockSpec((tm, D_MODEL), lambda i: (i, 0))

    def const(shape):
        return pl.BlockSpec(shape, lambda i: (0, 0))

    slabs = pl.BlockSpec((N_DEV, ATTN_W, LANES), lambda i: (0, 0, 0))

    def one_axis(spec):
        return pl.BlockSpec(spec.block_shape, lambda i, f=spec.index_map: f(0, i))

    return pl.pallas_call(
        body, name="tail", grid=(s // tm,),
        in_specs=[half, half, pl.BlockSpec((tm, ATTN_W), lambda i: (i, za_blk))]
        + [one_axis(sp) for sp in tiles + befores]
        + [const((3, CONV_W)), pl.BlockSpec((tm, 2 * D_MODEL), lambda i: (i, gab_blk)), full, full,
           const((3, D_MODEL)), const((ATTN_W, D_MODEL)), const((CONV_W, D_MODEL)), const((D_MODEL, D_MODEL))],
        out_specs=[full, pl.BlockSpec((tm, 2 * D_MODEL), lambda i: (i, 0)),
                   pl.BlockSpec((HEADS, tm, LANES), lambda i: (0, i, 0)), half, half,
                   const((D_MODEL, D_MODEL)), slabs, slabs, const((1, D_MODEL)), const((1, LANES))],
        out_shape=[jax.ShapeDtypeStruct((s, D_MODEL), F32),
                   jax.ShapeDtypeStruct((s, 2 * D_MODEL), BF16),
                   jax.ShapeDtypeStruct((HEADS, s, LANES), BF16),
                   jax.ShapeDtypeStruct((s, ATTN_W), BF16),
                   jax.ShapeDtypeStruct((s, CONV_W), F32),
                   jax.ShapeDtypeStruct((D_MODEL, D_MODEL), BF16),
                   jax.ShapeDtypeStruct((N_DEV, ATTN_W, LANES), BF16),
                   jax.ShapeDtypeStruct((N_DEV, CONV_W, LANES), BF16),
                   jax.ShapeDtypeStruct((1, D_MODEL), F32),
                   jax.ShapeDtypeStruct((1, LANES), F32)],
        scratch_shapes=[pltpu.VMEM((D_MODEL, D_MODEL), F32), pltpu.VMEM((ATTN_W, D_MODEL), F32),
                        pltpu.VMEM((CONV_W, D_MODEL), F32)],
        compiler_params=_params(("arbitrary",)),
    )(oa, attn, proj, *([proj] * 6), conv_w, proj, x, target, ada3, wa, wb, wo)


def _attn_bwd(qb, ka, kt, va, do, proj, qg, kg):
    s = qb.shape[1]
    t = min(TQ, s)
    nt = s // t
    hp = HEADS_PER_STEP
    assert hp == HEADS, "all heads share one (S, 1536) output block"
    wide = hp * HEAD_DIM
    scale = HEAD_DIM ** -0.5
    it, jt = _tri_steps(nt, False)

    def body(it_ref, jt_ref, q_ref, k_ref, kt_ref, v_ref, do_ref, qraw_ref, kraw_ref, qg_ref, kg_ref,
             dqkv_ref, dqg_ref, dkg_ref, dcum_ref, dqt_s, dk_s, dv_s, rows_s):
        grp, step = pl.program_id(0), pl.program_id(1)
        i, j = it_ref[step], jt_ref[step]
        lane = lax.broadcasted_iota(jnp.int32, (t, LANES), 1)
        lo = lane < HEAD_DIM

        @pl.when(step == 0)
        def _():
            dqt_s[...] = jnp.zeros_like(dqt_s)
            dqg_ref[...] = jnp.zeros_like(dqg_ref)
            dkg_ref[...] = jnp.zeros_like(dkg_ref)

        @pl.when(i == j)
        def _():
            dk_s[...] = jnp.zeros_like(dk_s)
            dv_s[...] = jnp.zeros_like(dv_s)

        def update(masked):
            for hh in range(hp):
                qh, doh = q_ref[hh], do_ref[hh]
                st = _dot_nt(k_ref[hh], qh)
                if masked:
                    st = jnp.where(_causal_t(t), st, NEG)
                pt = jnp.exp(st)
                dst = (pt * _dot_nt(v_ref[hh], doh)).astype(BF16)
                dv_s[hh] += _dot(pt.astype(BF16), doh)
                dk_s[hh] += _dot(dst, qh)
                dqt_s[hh, i] += _dot(kt_ref[hh], dst)

        def pair(a, b):
            return jnp.where(lo, a, pltpu.roll(b, HEAD_DIM, 1))

        def norm_bwd(raw, dy, g, dg_ref, off, sl):
            r = lax.rsqrt(_seg_sum(raw * raw, lo) * (1.0 / HEAD_DIM) + EPS)
            xhat = raw * r
            dg_ref[:, sl] += jnp.sum(dy * xhat, axis=0, keepdims=True)
            dxh = dy * g
            dx = r * (dxh - xhat * (_seg_sum(dxh * xhat, lo) * (1.0 / HEAD_DIM)))
            dqkv_ref[:, off + sl.start:off + sl.stop] = dx.astype(BF16)

        @pl.when(i > j)
        def _():
            update(False)

        @pl.when(i == j)
        def _():
            update(True)
            dq_rows = [dqt_s[hh, i].T for hh in range(hp)]
            rows = jnp.zeros((t, LANES), F32)
            for hh in range(hp):
                rows = jnp.where(lane == grp * hp + hh, _lane_col(dq_rows[hh], L_F_Q), rows)
            rows_s[...] = rows
            for pr in range(hp // 2):
                sl = slice(pr * LANES, (pr + 1) * LANES)
                norm_bwd(qraw_ref[:, sl].astype(F32), pair(dq_rows[2 * pr], dq_rows[2 * pr + 1]) * scale,
                         qg_ref[:, sl], dqg_ref, OFF_Q, sl)

        @pl.when(i == nt - 1)
        def _():
            dcum = rows_s[...]
            for hh in range(hp):
                dcum = jnp.where(lane == grp * hp + hh, dcum - _lane_col(dk_s[hh], L_ONE_Q), dcum)
            dcum_ref[0] = dcum
            for pr in range(hp // 2):
                sl = slice(pr * LANES, (pr + 1) * LANES)
                norm_bwd(kraw_ref[:, sl].astype(F32), pair(dk_s[2 * pr], dk_s[2 * pr + 1]),
                         kg_ref[:, sl], dkg_ref, OFF_K, sl)
                dqkv_ref[:, OFF_V + sl.start:OFF_V + sl.stop] = pair(dv_s[2 * pr], dv_s[2 * pr + 1]).astype(BF16)

    pair_q = pl.BlockSpec((hp, t, LANES), lambda p, n, it_, jt_: (p, it_[n], 0))
    pair_k = pl.BlockSpec((hp, t, LANES), lambda p, n, it_, jt_: (p, jt_[n], 0))
    pair_kt = pl.BlockSpec((hp, LANES, t), lambda p, n, it_, jt_: (p, 0, jt_[n]))
    tok3 = pl.BlockSpec((t, 3 * ATTN_W), lambda p, n, it_, jt_: (jt_[n], 0))
    gain = pl.BlockSpec((1, wide), lambda p, n, it_, jt_: (0, p))
    return pl.pallas_call(
        body, name="attn_bwd",
        grid_spec=pltpu.PrefetchScalarGridSpec(
            num_scalar_prefetch=2, grid=(HEADS // hp, it.shape[0]),
            in_specs=[pair_q, pair_k, pair_kt, pair_k, pair_q,
                      pl.BlockSpec((t, wide), lambda p, n, it_, jt_: (jt_[n], OFF_Q // wide + p)),
                      pl.BlockSpec((t, wide), lambda p, n, it_, jt_: (jt_[n], OFF_K // wide + p)), gain, gain],
            out_specs=[tok3, gain, gain,
                       pl.BlockSpec((1, t, LANES), lambda p, n, it_, jt_: (p, jt_[n], 0))],
            scratch_shapes=[pltpu.VMEM((hp, nt, LANES, t), F32), pltpu.VMEM((hp, t, LANES), F32),
                            pltpu.VMEM((hp, t, LANES), F32), pltpu.VMEM((t, LANES), F32)]),
        out_shape=[jax.ShapeDtypeStruct((s, 3 * ATTN_W), BF16)]
        + [jax.ShapeDtypeStruct((1, ATTN_W), F32)] * 2
        + [jax.ShapeDtypeStruct((HEADS // hp, s, LANES), F32)],
        compiler_params=_params(("parallel", "arbitrary")),
    )(it, jt, qb, ka, kt, va, do, proj, proj, qg, kg)


def _forget_bwd(dcum, fl, bf_pad):
    s = fl.shape[0]
    tc = min(TC_CUM, s)
    n = s // tc

    def body(dc_ref, fl_ref, bf_ref, df_ref, dbf_ref, carry):
        @pl.when(pl.program_id(0) == 0)
        def _():
            carry[...] = jnp.zeros_like(carry)
            dbf_ref[...] = jnp.zeros_like(dbf_ref)
        r = lax.broadcasted_iota(jnp.int32, (tc, tc), 0)
        cidx = lax.broadcasted_iota(jnp.int32, (tc, tc), 1)
        tri = (cidx >= r).astype(F32)
        dc = dc_ref[0]
        for grp in range(1, dcum.shape[0]):
            dc = dc + dc_ref[grp]
        dlf = jnp.dot(tri, dc, preferred_element_type=F32, precision=lax.Precision.HIGHEST) + carry[...]
        carry[...] += jnp.sum(dc, axis=0, keepdims=True)
        lane = lax.broadcasted_iota(jnp.int32, (tc, LANES), 1)
        dfl = jnp.where(lane < HEADS, dlf * _sigmoid(-(fl_ref[...] + bf_ref[...])), 0.0)
        df_ref[...] = dfl.astype(BF16)
        dbf_ref[...] += jnp.sum(dfl, axis=0, keepdims=True)

    rev = pl.BlockSpec((tc, LANES), lambda i: (n - 1 - i, 0))
    vec = pl.BlockSpec((1, LANES), lambda i: (0, 0))
    return pl.pallas_call(
        body, name="forget_bwd", grid=(n,),
        in_specs=[pl.BlockSpec((dcum.shape[0], tc, LANES), lambda i: (0, n - 1 - i, 0)), rev, vec],
        out_specs=[rev, vec],
        out_shape=[jax.ShapeDtypeStruct((s, LANES), BF16), jax.ShapeDtypeStruct((1, LANES), F32)],
        scratch_shapes=[pltpu.VMEM((1, LANES), F32)],
        compiler_params=_params(("arbitrary",)),
    )(dcum, fl, bf_pad)


def _conv_bwd(dob, proj, conv_w):
    s = dob.shape[0]
    tm = min(TM_ELEM, s)
    wd = CONV_W
    tiles, befores, afters = _conv_specs(tm, s, wd)

    def body(dob_ref, dnext_ref, gb_ref, gc_ref, u_ref, zb_ref, gch_ref, uh_ref, gbn_ref, zbn_ref, w_ref,
             out_ref, dw_ref):
        i = pl.program_id(1)

        @pl.when(i == 0)
        def _():
            dw_ref[...] = jnp.zeros_like(dw_ref)
        gb, gc, u, zb, cu, r1, r2, conv = _conv_parts(gb_ref, gc_ref, u_ref, zb_ref, gch_ref, uh_ref, i == 0, w_ref, tm)
        g = dob_ref[...]
        sg = _sigmoid(zb)
        sz = zb * sg
        dconv = g * gb * sz
        zn = zbn_ref[0:8, :].astype(F32)
        dcn = jnp.where(i == pl.num_programs(1) - 1, 0.0,
                        dnext_ref[...] * gbn_ref[0:8, :].astype(F32) * (zn * _sigmoid(zn)))
        nxt1, nxt2 = _sub_row(dcn, 0), _sub_row(dcn, 1)
        row = lax.broadcasted_iota(jnp.int32, (tm, wd), 0)
        f1 = jnp.where(row == tm - 1, nxt1, pltpu.roll(dconv, tm - 1, 0))
        f2 = jnp.where(row == tm - 2, nxt1, jnp.where(row == tm - 1, nxt2, pltpu.roll(dconv, tm - 2, 0)))
        dcu = w_ref[2:3, :] * dconv + w_ref[1:2, :] * f1 + w_ref[0:1, :] * f2
        out_ref[:, 0:wd] = (g * conv * sz).astype(BF16)
        out_ref[:, wd:2 * wd] = (dcu * u).astype(BF16)
        out_ref[:, 2 * wd:3 * wd] = (dcu * gc).astype(BF16)
        out_ref[:, 3 * wd:4 * wd] = (g * gb * conv * (sg * (1.0 + zb * (1.0 - sg)))).astype(BF16)
        w_row = lax.broadcasted_iota(jnp.int32, (3, wd), 0)
        dw0 = jnp.sum(dconv * r2, axis=0, keepdims=True)
        dw1 = jnp.sum(dconv * r1, axis=0, keepdims=True)
        dw2 = jnp.sum(dconv * cu, axis=0, keepdims=True)
        dw_ref[...] += jnp.where(w_row == 0, dw0, jnp.where(w_row == 1, dw1, dw2))

    blk = pl.BlockSpec((tm, wd), lambda c, i: (i, c))
    nxt = pl.BlockSpec((8, wd), lambda c, i: (jnp.minimum((i + 1) * (tm // 8), s // 8 - 1), c))
    wspec = pl.BlockSpec((3, wd), lambda c, i: (0, c))
    return pl.pallas_call(
        body, name="conv_bwd", grid=(CONV_W // wd, s // tm),
        in_specs=[blk, nxt] + tiles + befores + afters + [wspec],
        out_specs=[pl.BlockSpec((tm, 4 * wd), lambda c, i: (i, c)), wspec],
        out_shape=[jax.ShapeDtypeStruct((s, 4 * CONV_W), BF16), jax.ShapeDtypeStruct((3, CONV_W), F32)],
        compiler_params=_params(("parallel", "arbitrary")),
    )(dob, dob, *([proj] * 8), conv_w)


def _piece_layout(pieces):
    offs, off = [], 0
    for p in pieces:
        offs.append((off, p.shape[1]))
        off += p.shape[1]
    assert off == N_ALL, off
    return offs


def _dw_in(h, pieces, chip_sums):
    s = h.shape[0]
    tk, tn = min(TK_DW, s), TN_DW
    nk = s // tk
    nn = N_MAIN // tn
    main, fpiece = pieces[:-1], pieces[-1]
    layout = _piece_layout(pieces)[:-1]
    n_main = len(main)
    nx = len(chip_sums)

    def body(*refs):
        p_refs, f_ref, h_ref = refs[:n_main], refs[n_main], refs[n_main + 1]
        ins, refs = refs[n_main + 2:n_main + 2 + nx], refs[n_main + 2 + nx:]
        out_ref, outf_ref = refs[:2]
        outs, (acc, accf, send_sems, recv_sems, local_sems) = refs[2:2 + nx], refs[2 + nx:]
        n, k = pl.program_id(0), pl.program_id(1)

        @pl.when(jnp.logical_and(n == 0, k == 0))
        def _():
            for cp in _all_to_all_copies(ins, outs, send_sems, recv_sems, local_sems):
                cp.start()

        @pl.when(k == 0)
        def _():
            acc[...] = jnp.zeros_like(acc)
        hv = h_ref[pl.ds(pl.multiple_of(k * tk, tk), tk), :]
        for p_ref, (off, width) in zip(p_refs, layout):
            @pl.when(jnp.logical_and(n >= off // tn, n < (off + width) // tn))
            def _():
                acc[...] += _dot_tn(p_ref[...], hv)

        @pl.when(k == nk - 1)
        def _():
            out_ref[...] = acc[...].astype(BF16)

        @pl.when(n == 0)
        def _():
            @pl.when(k == 0)
            def _():
                accf[...] = jnp.zeros_like(accf)
            accf[...] += _dot_tn(f_ref[...], hv)

            @pl.when(k == nk - 1)
            def _():
                outf_ref[...] = accf[...].astype(BF16)

        @pl.when(jnp.logical_and(n == nn - 1, k == nk - 1))
        def _():
            for cp in _all_to_all_copies(ins, outs, send_sems, recv_sems, local_sems):
                cp.wait()

    def piece_spec(off, width):
        lo, hi = off // tn, (off + width) // tn

        def index(n, k):
            active = jnp.logical_and(n >= lo, n < hi)
            return jnp.where(active, k, 0), jnp.clip(n - lo, 0, hi - lo - 1)
        return pl.BlockSpec((tk, tn), index)

    any_spec = pl.BlockSpec(memory_space=pl.ANY)
    res = pl.pallas_call(
        body, name="dw_in", grid=(nn, nk),
        in_specs=[piece_spec(off, width) for off, width in layout]
        + [pl.BlockSpec((tk, N_FPAD), lambda n, k: (jnp.where(n == 0, k, 0), 0)),
           pl.BlockSpec((s, D_MODEL), lambda n, k: (0, 0))] + [any_spec] * nx,
        out_specs=[pl.BlockSpec((tn, D_MODEL), lambda n, k: (n, 0)),
                   pl.BlockSpec((N_FPAD, D_MODEL), lambda n, k: (0, 0))] + [any_spec] * nx,
        out_shape=[jax.ShapeDtypeStruct((N_MAIN, D_MODEL), BF16), jax.ShapeDtypeStruct((N_FPAD, D_MODEL), BF16)]
        + [jax.ShapeDtypeStruct(a.shape, a.dtype) for a in chip_sums],
        scratch_shapes=[pltpu.VMEM((tn, D_MODEL), F32), pltpu.VMEM((N_FPAD, D_MODEL), F32)] + _gather_sems(nx),
        compiler_params=_params(("arbitrary", "arbitrary")),
    )(*main, fpiece, h, *chip_sums)
    return res[:2], res[2:]


def _dh_and_dx(pieces, w_all_t, x, dy, ada3, norm_g, chip_sums):
    s = x.shape[0]
    tm = min(TM_DH, s)
    nt = s // tm
    n = len(chip_sums)
    npc = len(pieces)
    layout = _piece_layout(pieces)

    def body(*refs):
        p_refs, refs = refs[:npc], refs[npc:]
        wt_ref, x_ref, dy_ref, ada_ref, g_ref = refs[:5]
        ins, refs = refs[5:5 + n], refs[5 + n:]
        gx_ref, dsh_ref, dsc_ref, dg_ref = refs[:4]
        outs, (send_sems, recv_sems, local_sems) = refs[4:4 + n], refs[4 + n:]
        i = pl.program_id(0)

        @pl.when(i == 0)
        def _():
            for cp in _chip_copies(ins, outs, send_sems, recv_sems, local_sems):
                cp.start()
            dsh_ref[...] = jnp.zeros_like(dsh_ref)
            dsc_ref[...] = jnp.zeros_like(dsc_ref)
            dg_ref[...] = jnp.zeros_like(dg_ref)

        dh = None
        for p_ref, (off, width) in zip(p_refs, layout):
            part = _dot(p_ref[...], wt_ref[off:off + width, :])
            dh = part if dh is None else dh + part
        xv = x_ref[...]
        r = lax.rsqrt(jnp.mean(xv * xv, axis=-1, keepdims=True) + EPS)
        xhat = xv * r
        g = g_ref[...]
        one_sc = 1.0 + ada_ref[1:2, :]
        dsh_ref[...] += jnp.sum(dh, axis=0, keepdims=True)
        dsc_ref[...] += jnp.sum(dh * (xhat * g), axis=0, keepdims=True)
        dg_ref[...] += jnp.sum(dh * xhat, axis=0, keepdims=True) * one_sc
        dxh = dh * (g * one_sc)
        dx = r * (dxh - xhat * jnp.mean(dxh * xhat, axis=-1, keepdims=True))
        gx_ref[...] = dy_ref[...] + dx

        @pl.when(i == nt - 1)
        def _():
            for cp in _chip_copies(ins, outs, send_sems, recv_sems, local_sems):
                cp.wait()

    full = pl.BlockSpec((tm, D_MODEL), lambda i: (i, 0))
    vec = pl.BlockSpec((1, D_MODEL), lambda i: (0, 0))
    any_spec = pl.BlockSpec(memory_space=pl.ANY)
    res = pl.pallas_call(
        body, name="dh_dx", grid=(nt,),
        in_specs=[pl.BlockSpec((tm, p.shape[1]), lambda i: (i, 0)) for p in pieces]
        + [pl.BlockSpec((N_ALL, D_MODEL), lambda i: (0, 0)), full, full,
           pl.BlockSpec((3, D_MODEL), lambda i: (0, 0)), vec] + [any_spec] * n,
        out_specs=[full, vec, vec, vec] + [any_spec] * n,
        out_shape=[jax.ShapeDtypeStruct((s, D_MODEL), F32)] + [jax.ShapeDtypeStruct((1, D_MODEL), F32)] * 3
        + [jax.ShapeDtypeStruct(a.shape, a.dtype) for a in chip_sums],
        scratch_shapes=[pltpu.SemaphoreType.DMA((n * 3,)), pltpu.SemaphoreType.DMA((n * 3,)),
                        pltpu.SemaphoreType.DMA((n,))],
        compiler_params=_params(("arbitrary",)),
    )(*pieces, w_all_t, x, dy, ada3, norm_g, *chip_sums)
    return res[:4], res[4:]


def _sum_small(vec_all, qg_parts, kg_parts):
    def body(v_ref, q_ref, k_ref, tot_ref, gq_ref, gk_ref):
        tot = v_ref[0:1, :]
        for p in range(1, N_DEV):
            tot = tot + v_ref[p:p + 1, :]
        tot_ref[...] = tot
        gq_ref[...] = jnp.sum(q_ref[...], axis=0, keepdims=True)
        gk_ref[...] = jnp.sum(k_ref[...], axis=0, keepdims=True)

    n = vec_all.shape[-1]
    return pl.pallas_call(
        body, name="sum_small",
        out_shape=[jax.ShapeDtypeStruct((1, n), F32),
                   jax.ShapeDtypeStruct((1, HEAD_DIM), F32), jax.ShapeDtypeStruct((1, HEAD_DIM), F32)],
        compiler_params=_params(),
    )(vec_all, qg_parts, kg_parts)


def _grad_w_ada(c_cols, dada_rows):
    def body(c_ref, d_ref, out_ref):
        acc = c_ref[0] * d_ref[0]
        for b in range(1, N_DEV):
            acc = acc + c_ref[b] * d_ref[b]
        out_ref[...] = acc

    return pl.pallas_call(
        body, name="grad_w_ada",
        out_shape=jax.ShapeDtypeStruct((D_MODEL, ADA_SHARD), F32),
        compiler_params=_params(),
    )(c_cols, dada_rows)


def _adam_step(w, m, v, g):
    c1 = 1.0 / (1.0 - ADAM_B1 ** ADAM_STEP)
    c2 = 1.0 / (1.0 - ADAM_B2 ** ADAM_STEP)
    m_new = ADAM_B1 * m + (1.0 - ADAM_B1) * g
    v_new = ADAM_B2 * v + (1.0 - ADAM_B2) * (g * g)
    return -ADAM_LR * ((m_new * c1) / (jnp.sqrt(v_new * c2) + ADAM_EPS) + ADAM_WD * w), m_new, v_new


def _adamw_small(params, name):
    n = len(params)
    stacked = [p[3].ndim == p[0].ndim + 1 for p in params]

    def body(*refs):
        ins, outs = refs[:4 * n], refs[4 * n:]
        for k in range(n):
            w_ref, m_ref, v_ref, g_ref = ins[4 * k:4 * k + 4]
            go_ref, d_ref, mo_ref, vo_ref = outs[4 * k:4 * k + 4]
            if stacked[k]:
                g = g_ref[0].astype(F32)
                for p in range(1, g_ref.shape[0]):
                    g = g + g_ref[p].astype(F32)
            else:
                g = g_ref[...]
            go_ref[...] = g
            d_ref[...], mo_ref[...], vo_ref[...] = _adam_step(w_ref[...], m_ref[...], v_ref[...], g)

    res = pl.pallas_call(
        body, name=name,
        out_shape=[jax.ShapeDtypeStruct(p[0].shape, F32) for p in params for _ in range(4)],
        compiler_params=_params(),
    )(*[a for p in params for a in p])
    return [tuple(res[4 * k:4 * k + 4]) for k in range(n)]


def _adamw(w, m, v, g_parts, name):
    rows, cols = w.shape
    n_parts = g_parts.shape[0]
    tr = 256 if rows % 256 == 0 else rows
    tc = 256 if (tr == rows and rows > 256 and cols % 256 == 0) else cols

    def body(w_ref, m_ref, v_ref, g_ref, go_ref, d_ref, mo_ref, vo_ref):
        g = g_ref[0].astype(F32)
        for p in range(1, n_parts):
            g = g + g_ref[p].astype(F32)
        go_ref[...] = g
        d_ref[...], mo_ref[...], vo_ref[...] = _adam_step(w_ref[...], m_ref[...], v_ref[...], g)

    blk = pl.BlockSpec((tr, tc), lambda i, j: (i, j))
    return pl.pallas_call(
        body, name=name, grid=(rows // tr, cols // tc),
        in_specs=[blk, blk, blk, pl.BlockSpec((n_parts, tr, tc), lambda i, j: (0, i, j))],
        out_specs=[blk] * 4,
        out_shape=[jax.ShapeDtypeStruct((rows, cols), F32)] * 4,
        compiler_params=_params(("parallel", "parallel")),
    )(w, m, v, g_parts)


_O_F = 1536


W_TILE = 16
WIN_ROWS = 784


def _internal_start(p):
    return p * IN_SHARD - (HEADS if p * IN_SHARD > _O_F else 0)


def _shard_window(wt_shard, me):
    lo = me * IN_SHARD
    o = lo + lax.broadcasted_iota(jnp.int32, (IN_SHARD, 1), 0)
    is_f = jnp.logical_and(o >= _O_F, o < _O_F + HEADS)
    start = lo - jnp.where(lo > _O_F, HEADS, 0)
    window = lax.dynamic_update_slice(jnp.zeros((WIN_ROWS + W_TILE, D_MODEL), BF16),
                                      jnp.where(is_f, 0.0, wt_shard).astype(BF16), (start % W_TILE, 0))
    first = jnp.clip(_O_F - lo, 0, IN_SHARD - W_TILE)
    near = lax.dynamic_slice(wt_shard, (first, 0), (W_TILE, D_MODEL))
    j = lax.broadcasted_iota(jnp.int32, (W_TILE, 1), 0)
    src = _O_F - lo + j
    ok = jnp.logical_and(j < HEADS, jnp.logical_and(src >= 0, src < IN_SHARD))
    f_tile = jnp.where(ok, jnp.roll(near, first - (_O_F - lo), axis=0), 0.0).astype(BF16)
    return lax.dynamic_update_slice(window, f_tile, (WIN_ROWS, 0))


def _assemble_w(windows):
    chunk = 112

    def body(g_ref, out_ref):
        out_ref[WIN_ROWS:N_MAIN, :] = jnp.zeros((N_MAIN - WIN_ROWS, D_MODEL), BF16)
        for p in range(N_DEV):
            base = _internal_start(p) // W_TILE * W_TILE
            for r in range(0, WIN_ROWS, chunk):
                rows = slice(base + r, base + r + chunk)
                piece = g_ref[p, r:r + chunk, :]
                out_ref[rows, :] = piece if p == 0 else out_ref[rows, :] + piece
        f = g_ref[0, WIN_ROWS:WIN_ROWS + W_TILE, :]
        for p in range(1, N_DEV):
            f = f + g_ref[p, WIN_ROWS:WIN_ROWS + W_TILE, :]
        out_ref[N_MAIN:N_MAIN + W_TILE, :] = f
        out_ref[N_MAIN + W_TILE:N_ALL, :] = jnp.zeros((N_FPAD - W_TILE, D_MODEL), BF16)

    return pl.pallas_call(
        body, name="assemble_w", out_shape=jax.ShapeDtypeStruct((N_ALL, D_MODEL), BF16),
        compiler_params=_params(),
    )(windows)


def _shard_rows(windows, me):
    n = windows.shape[0]
    lo = me * IN_SHARD
    start = lo - jnp.where(lo > _O_F, HEADS, 0)
    main = lax.dynamic_slice(windows, (0, start % W_TILE, 0), (n, IN_SHARD, D_MODEL))
    first = jnp.clip(_O_F - lo, 0, IN_SHARD - W_TILE)
    near = lax.dynamic_slice(main, (0, first, 0), (n, W_TILE, D_MODEL))
    j = first - (_O_F - lo) + lax.broadcasted_iota(jnp.int32, (1, W_TILE, 1), 1)
    f_rows = jnp.roll(windows[:, WIN_ROWS:], -(first - (_O_F - lo)), axis=1)
    patch = jnp.where(jnp.logical_and(j >= 0, j < HEADS), f_rows, near)
    return lax.dynamic_update_slice(main, patch, (0, first, 0))


def kernel(x, c, w_ada, b_ada, norm_g, w_in, b_f, q_norm_g, k_norm_g, conv_w, w_attn_out, w_conv_out, w_o, loss_target, m_w_ada, m_b_ada, m_norm_g, m_w_in, m_b_f, m_q_norm_g, m_k_norm_g, m_conv_w, m_w_attn_out, m_w_conv_out, m_w_o, v_w_ada, v_b_ada, v_norm_g, v_w_in, v_b_f, v_q_norm_g, v_k_norm_g, v_conv_w, v_w_attn_out, v_w_conv_out, v_w_o):
    me = 4 * lax.axis_index("x") + 2 * lax.axis_index("y") + lax.axis_index("c")
    s = x.shape[1]
    x2, t2 = x[0], loss_target[0]

    w_in_g, c_all, ada_g = _gather_weights_and_ada(_shard_window(w_in[0].T, me), c, w_ada[0])
    ada_mine = lax.dynamic_index_in_dim(ada_g[:, :, 0, :], me, axis=1, keepdims=False)
    ada3 = (ada_mine.reshape(1, 3 * D_MODEL) + b_ada).reshape(3, D_MODEL)
    w_all_t = _assemble_w(w_in_g)
    qg = jnp.tile(q_norm_g, (1, HEADS))
    kg = jnp.tile(k_norm_g, (1, HEADS))
    bf_pad = jnp.pad(b_f, ((0, 0), (0, LANES - HEADS)))

    (proj, fl, h), (qa, ka, va, kt, vt), (cw_g, wa_g, wb_g, wo_g) = _proj_fwd(
        x2, ada3, norm_g, w_all_t, bf_pad, qg, kg,
        [conv_w[0], w_attn_out[0].astype(BF16), w_conv_out[0].astype(BF16), w_o[0].astype(BF16)])
    wa = jnp.transpose(wa_g, (1, 0, 2)).reshape(ATTN_W, D_MODEL)
    wb = jnp.transpose(wb_g, (1, 0, 2)).reshape(CONV_W, D_MODEL)
    wo = wo_g.reshape(D_MODEL, D_MODEL)
    cw = jnp.transpose(cw_g, (1, 0, 2)).reshape(3, CONV_W)
    attn, oa, qb = _attn_fwd(qa, ka, vt, proj)
    (dy, dgab, do, dza, dob, dwo, dwa, dwb, dgate, loss_part) = _tail(oa, attn, proj, x2, t2, ada3, wa, wb, wo, cw)

    small = [dwa, dwb, dwo.reshape(N_DEV, D_MODEL // N_DEV, D_MODEL)]
    dqkv, dqg, dkg, dcum = _attn_bwd(qb, ka, kt, va, do, proj, qg, kg)
    df, dbf = _forget_bwd(dcum, fl, bf_pad)
    dconv, dcw = _conv_bwd(dob, proj, cw)
    pieces = [dqkv, dza, dconv, dgab, df]
    (dw_main, dw_f), (g_wa_parts, g_wb_parts, g_wo_parts) = _dw_in(h, pieces, small)

    pair_in = _sibling_swap_sum(dw_main, dw_f, "swap_sum_w_in")
    (grad_x, dshift, dscale, dnormg), (g_in_windows,) = _dh_and_dx(
        pieces, w_all_t, x2, dy, ada3, norm_g, [pair_in])
    g_in_parts = _shard_rows(g_in_windows, me)
    vec = jnp.concatenate([dshift, dscale, dgate, dnormg, dbf, dcw.reshape(1, 3 * CONV_W), loss_part, dqg, dkg],
                          axis=1)
    (vec_all,) = _gather_direct([vec], "gather_small")
    vec_all = vec_all.reshape(N_DEV, vec.shape[1])
    n_main = 4 * D_MODEL + LANES + 3 * CONV_W + LANES
    tot, g_qg, g_kg = _sum_small(
        vec_all[:, :n_main],
        vec_all[:, n_main:n_main + ATTN_W].reshape(N_DEV * HEADS, HEAD_DIM),
        vec_all[:, n_main + ATTN_W:].reshape(N_DEV * HEADS, HEAD_DIM))
    g_b_ada = tot[:, 0:3 * D_MODEL]
    g_norm_g = tot[:, 3 * D_MODEL:4 * D_MODEL]
    g_b_f = tot[:, 4 * D_MODEL:4 * D_MODEL + HEADS]
    g_cw_full = tot[:, 4 * D_MODEL + LANES:4 * D_MODEL + LANES + 3 * CONV_W].reshape(3, CONV_W)
    g_cw = lax.dynamic_slice(g_cw_full, (0, me * (CONV_W // N_DEV)), (3, CONV_W // N_DEV))
    dada_mine = lax.dynamic_slice(vec_all[:, 0:3 * D_MODEL], (0, me * ADA_SHARD), (N_DEV, ADA_SHARD))
    g_w_ada = _grad_w_ada(jnp.transpose(c_all, (0, 2, 1)), dada_mine.reshape(N_DEV, 1, ADA_SHARD))

    upd = {}
    upd["w_ada"] = _adamw(w_ada[0], m_w_ada[0], v_w_ada[0], g_w_ada[None], "adamw_w_ada")
    upd["w_in"] = [u.T for u in _adamw(w_in[0].T, m_w_in[0].T, v_w_in[0].T, g_in_parts, "adamw_w_in")]
    small_names = ["b_ada", "norm_g", "b_f", "q_norm_g", "k_norm_g", "conv_w", "w_attn_out", "w_conv_out", "w_o"]
    small_upd = _adamw_small(
        [(b_ada, m_b_ada, v_b_ada, g_b_ada), (norm_g, m_norm_g, v_norm_g, g_norm_g), (b_f, m_b_f, v_b_f, g_b_f),
         (q_norm_g, m_q_norm_g, v_q_norm_g, g_qg), (k_norm_g, m_k_norm_g, v_k_norm_g, g_kg),
         (conv_w[0], m_conv_w[0], v_conv_w[0], g_cw),
         (w_attn_out[0], m_w_attn_out[0], v_w_attn_out[0], g_wa_parts),
         (w_conv_out[0], m_w_conv_out[0], v_w_conv_out[0], g_wb_parts),
         (w_o[0], m_w_o[0], v_w_o[0], g_wo_parts)], "adamw_small")
    upd.update(zip(small_names, small_upd))

    names = ["w_ada", "b_ada", "norm_g", "w_in", "b_f", "q_norm_g", "k_norm_g", "conv_w",
             "w_attn_out", "w_conv_out", "w_o"]
    lead = {"w_ada", "w_in", "conv_w", "w_attn_out", "w_conv_out", "w_o"}
    fix = lambda n, a: a[None] if n in lead else a
    loss = tot[0, n_main - LANES]
    outs = [loss, grad_x[None]]
    for k in range(4):
        outs += [fix(n, upd[n][k]) for n in names]
    return tuple(outs)
```

```python
import numpy as np
import jax
import jax.numpy as jnp
from jax import lax
from jax.experimental import pallas as pl
from jax.experimental.pallas import tpu as pltpu

F32 = jnp.float32
BF16 = jnp.bfloat16

D_MODEL = 1024
HEADS = 8
HEAD_DIM = 64
ATTN_W = 512
CONV_W = 512
N_DEV = 8
IN_WIDTH = 6152
IN_SHARD = IN_WIDTH // N_DEV
N_MAIN = 6144
N_FPAD = 128
N_ALL = N_MAIN + N_FPAD
ADA_SHARD = 3 * D_MODEL // N_DEV
EPS = 1e-6
NEG = -1e30

ADAM_LR = 0.001
ADAM_B1 = 0.9
ADAM_B2 = 0.999
ADAM_EPS = 1e-08
ADAM_WD = 0.01
ADAM_STEP = 10

LANES = 128
VMEM_LIMIT = 56 * 1024 * 1024

TM_PROJ = 512
TN_PROJ = 1024
TM_ELEM = 512
TQ = 512
HEADS_PER_STEP = 8
TM_TAIL = 256
TC_CUM = 512
TK_DW = 2048
TN_DW = 512
TM_DH = 256
HALO = 16

OFF_Q, OFF_K, OFF_V, OFF_ZA, OFF_CB, OFF_CC, OFF_CU, OFF_CZ, OFF_GA, OFF_GB = (
    0, 512, 1024, 1536, 2048, 2560, 3072, 3584, 4096, 5120)


def _params(sem=None):
    return pltpu.CompilerParams(dimension_semantics=sem, vmem_limit_bytes=VMEM_LIMIT)


def _dot(a, b):
    return jnp.dot(a, b, preferred_element_type=F32)


def _dot_nt(a, b):
    return lax.dot_general(a, b, (((1,), (1,)), ((), ())), preferred_element_type=F32)


def _dot_tn(a, b):
    return lax.dot_general(a, b, (((0,), (0,)), ((), ())), preferred_element_type=F32)


def _sigmoid(x):
    return 1.0 / (1.0 + jnp.exp(-x))


def _seg_sum(z, lo):
    a = jnp.sum(jnp.where(lo, z, 0.0), axis=-1, keepdims=True)
    b = jnp.sum(jnp.where(lo, 0.0, z), axis=-1, keepdims=True)
    return jnp.where(lo, a, b)


def _lane_col(z, lane):
    idx = lax.broadcasted_iota(jnp.int32, z.shape, 1)
    return jnp.sum(jnp.where(idx == lane, z, 0.0), axis=-1, keepdims=True)


def _sub_row(z, row):
    idx = lax.broadcasted_iota(jnp.int32, z.shape, 0)
    return jnp.sum(jnp.where(idx == row, z, 0.0), axis=0, keepdims=True)


def _mesh_pos():
    x, y, c = lax.axis_index("x"), lax.axis_index("y"), lax.axis_index("c")
    return x, y, c, 4 * x + 2 * y + c


def _peer(k, x, y, c):
    px = 1 - x if (k >> 2) & 1 else x
    py = 1 - y if (k >> 1) & 1 else y
    pc = 1 - c if k & 1 else c
    return (px, py, pc), 4 * px + 2 * py + pc


def _gather_copies(ins, outs, send_sems, recv_sems, local_sems):
    x, y, c, me = _mesh_pos()
    copies = []
    for a in range(len(ins)):
        copies.append(pltpu.make_async_copy(ins[a], outs[a].at[me], local_sems.at[a]))
        for k in range(1, N_DEV):
            dev, _ = _peer(k, x, y, c)
            copies.append(pltpu.make_async_remote_copy(
                src_ref=ins[a], dst_ref=outs[a].at[me],
                send_sem=send_sems.at[a * (N_DEV - 1) + k - 1], recv_sem=recv_sems.at[a * (N_DEV - 1) + k - 1],
                device_id=dev, device_id_type=pl.DeviceIdType.MESH))
    return copies


def _gather_sems(n):
    return [pltpu.SemaphoreType.DMA((n * (N_DEV - 1),)), pltpu.SemaphoreType.DMA((n * (N_DEV - 1),)),
            pltpu.SemaphoreType.DMA((n,))]


def _gather_direct(arrs, name):
    n = len(arrs)
    any_spec = pl.BlockSpec(memory_space=pl.ANY)

    def body(*refs):
        copies = _gather_copies(refs[:n], refs[n:2 * n], *refs[2 * n:])
        for cp in copies:
            cp.start()
        for cp in copies:
            cp.wait()

    return pl.pallas_call(
        body, name=name, out_shape=[jax.ShapeDtypeStruct((N_DEV,) + a.shape, a.dtype) for a in arrs],
        in_specs=[any_spec] * n, out_specs=[any_spec] * n, scratch_shapes=_gather_sems(n),
    )(*arrs)


def _ada_phase(c_ref, w_ref, call_ref, adag_ref, mine_ref, send_sems, recv_sems):
    x, y, c, me = _mesh_pos()

    def copy(phase, k, src, dst):
        dev, _ = _peer(k, x, y, c)
        return pltpu.make_async_remote_copy(
            src_ref=src, dst_ref=dst,
            send_sem=send_sems.at[phase * (N_DEV - 1) + k - 1],
            recv_sem=recv_sems.at[phase * (N_DEV - 1) + k - 1],
            device_id=dev, device_id_type=pl.DeviceIdType.MESH)

    call_ref[me] = c_ref[...]
    first = [copy(0, k, c_ref, call_ref.at[me]) for k in range(1, N_DEV)]
    for cp in first:
        cp.start()
    for cp in first:
        cp.wait()
    wb = w_ref[...].astype(BF16)
    for b in range(N_DEV):
        row = jnp.broadcast_to(call_ref[b], (8, D_MODEL)).astype(BF16)
        mine_ref[b] = _sub_row(_dot(row, wb), 0)
    adag_ref[me] = mine_ref[...]
    second = [copy(1, k, mine_ref, adag_ref.at[me]) for k in range(1, N_DEV)]
    for cp in second:
        cp.start()
    for cp in second:
        cp.wait()


def _gather_weights_and_ada(wt_shard, c_row, w_ada_sh):
    any_spec = pl.BlockSpec(memory_space=pl.ANY)
    vm = pl.BlockSpec(memory_space=pltpu.VMEM)

    def body(w_in_ref, c_ref, wada_ref, out_ref, call_ref, adag_ref, mine_ref, send_sems, recv_sems, local_sem,
             ada_send, ada_recv):
        x, y, c, me = _mesh_pos()
        sibling = (x, y, 1 - c)
        chips = [(1 - x, y), (x, 1 - y), (1 - x, 1 - y)]

        def copy(k, src, blk, to):
            return pltpu.make_async_remote_copy(
                src_ref=src, dst_ref=out_ref.at[blk], send_sem=send_sems.at[k], recv_sem=recv_sems.at[k],
                device_id=to, device_id_type=pl.DeviceIdType.MESH)

        local = pltpu.make_async_copy(w_in_ref, out_ref.at[me], local_sem.at[0])
        local.start()
        first = [copy(0, w_in_ref, me, sibling)]
        first += [copy(1 + j, w_in_ref, me, (px, py, c)) for j, (px, py) in enumerate(chips)]
        for cp in first:
            cp.start()
        _ada_phase(c_ref, wada_ref, call_ref, adag_ref, mine_ref, ada_send, ada_recv)
        passed = []
        for j, (px, py) in enumerate(chips):
            blk = 4 * px + 2 * py + c
            copy(1 + j, w_in_ref, blk, (x, y, c)).wait_recv()
            fwd = copy(4 + j, out_ref.at[blk], blk, sibling)
            fwd.start()
            passed.append(fwd)
        copy(0, w_in_ref, 4 * x + 2 * y + 1 - c, (x, y, c)).wait_recv()
        for j, (px, py) in enumerate(chips):
            copy(4 + j, w_in_ref, 4 * px + 2 * py + 1 - c, (x, y, c)).wait_recv()
        for cp in first + passed:
            cp.wait_send()
        local.wait()

    per = N_DEV - 1
    return pl.pallas_call(
        body, name="gather_weights",
        out_shape=[jax.ShapeDtypeStruct((N_DEV,) + wt_shard.shape, wt_shard.dtype),
                   jax.ShapeDtypeStruct((N_DEV, 1, D_MODEL), F32),
                   jax.ShapeDtypeStruct((N_DEV, N_DEV, 1, ADA_SHARD), F32)],
        in_specs=[any_spec, vm, vm], out_specs=[any_spec, vm, vm],
        scratch_shapes=[pltpu.VMEM((N_DEV, 1, ADA_SHARD), F32),
                        pltpu.SemaphoreType.DMA((per,)), pltpu.SemaphoreType.DMA((per,)),
                        pltpu.SemaphoreType.DMA((1,)),
                        pltpu.SemaphoreType.DMA((2 * per,)), pltpu.SemaphoreType.DMA((2 * per,))],
        compiler_params=pltpu.CompilerParams(vmem_limit_bytes=VMEM_LIMIT),
    )(wt_shard, c_row, w_ada_sh)


def _sibling_swap_sum(dwt, dwt_f, name):
    nch, rows = N_DEV // 2, WIN_ROWS + W_TILE
    any_spec = pl.BlockSpec(memory_space=pl.ANY)

    def body(main_ref, f_ref, out_ref, mine_v, theirs_v, sum_v, send_sems, recv_sems, load_sems, store_sems):
        x, y, c, _ = _mesh_pos()

        def parts(core, ch, buf):
            b0, b1 = (_internal_start(2 * ch + k) // W_TILE * W_TILE for k in range(2))
            base = pl.multiple_of(jnp.where(core == 0, b0, b1), W_TILE)
            return [(main_ref.at[pl.ds(base, WIN_ROWS)], buf.at[ch, pl.ds(0, WIN_ROWS)]),
                    (f_ref.at[pl.ds(0, W_TILE)], buf.at[ch, pl.ds(WIN_ROWS, W_TILE)])]

        swaps, loads, stores = [], [], []
        for ch in range(nch):
            for k, (src, dst) in enumerate(parts(1 - c, ch, theirs_v)):
                swaps.append(pltpu.make_async_remote_copy(
                    src_ref=src, dst_ref=dst, send_sem=send_sems.at[2 * ch + k], recv_sem=recv_sems.at[2 * ch + k],
                    device_id=(x, y, 1 - c), device_id_type=pl.DeviceIdType.MESH))
            for k, (src, dst) in enumerate(parts(c, ch, mine_v)):
                loads.append(pltpu.make_async_copy(src, dst, load_sems.at[2 * ch + k]))
            stores.append(pltpu.make_async_copy(sum_v.at[ch], out_ref.at[ch], store_sems.at[ch]))
        for cp in swaps + loads:
            cp.start()
        for ch in range(nch):
            for cp in loads[2 * ch:2 * ch + 2] + swaps[2 * ch:2 * ch + 2]:
                cp.wait()
            sum_v[ch] = (mine_v[ch].astype(F32) + theirs_v[ch].astype(F32)).astype(BF16)
            stores[ch].start()
        for cp in stores:
            cp.wait()

    return pl.pallas_call(
        body, name=name,
        out_shape=jax.ShapeDtypeStruct((nch, rows, D_MODEL), BF16),
        in_specs=[any_spec, any_spec], out_specs=any_spec,
        scratch_shapes=[pltpu.VMEM((nch, rows, D_MODEL), BF16)] * 3
        + [pltpu.SemaphoreType.DMA((2 * nch,))] * 3 + [pltpu.SemaphoreType.DMA((nch,))],
        compiler_params=_params(),
    )(dwt, dwt_f)


def _all_to_all_copies(ins, outs, send_sems, recv_sems, local_sems):
    x, y, c, me = _mesh_pos()
    copies = []
    for a in range(len(ins)):
        copies.append(pltpu.make_async_copy(ins[a].at[me], outs[a].at[me], local_sems.at[a]))
        for k in range(1, N_DEV):
            dev, p = _peer(k, x, y, c)
            copies.append(pltpu.make_async_remote_copy(
                src_ref=ins[a].at[p], dst_ref=outs[a].at[me],
                send_sem=send_sems.at[a * (N_DEV - 1) + k - 1], recv_sem=recv_sems.at[a * (N_DEV - 1) + k - 1],
                device_id=dev, device_id_type=pl.DeviceIdType.MESH))
    return copies


def _chip_copies(ins, outs, send_sems, recv_sems, local_sems):
    x, y, c, _ = _mesh_pos()
    my_chip = 2 * x + y
    chips = [(1 - x, y), (x, 1 - y), (1 - x, 1 - y)]
    copies = []
    for a in range(len(ins)):
        copies.append(pltpu.make_async_copy(ins[a].at[my_chip], outs[a].at[my_chip], local_sems.at[a]))
        for j, (px, py) in enumerate(chips):
            copies.append(pltpu.make_async_remote_copy(
                src_ref=ins[a].at[2 * px + py], dst_ref=outs[a].at[my_chip],
                send_sem=send_sems.at[a * 3 + j], recv_sem=recv_sems.at[a * 3 + j],
                device_id=(px, py, c), device_id_type=pl.DeviceIdType.MESH))
    return copies


def _proj_fwd(x, ada3, norm_g, w_all_t, bf_pad, qg, kg, later):
    s = x.shape[0]
    tm, tn = min(TM_PROJ, s), TN_PROJ
    nt = s // tm
    n = len(later)

    def body(x_ref, ada_ref, g_ref, wt_ref, bf_ref, qg_ref, kg_ref, *rest):
        ins, (proj_ref, fl_ref, h_ref), rows_refs, rest = rest[:n], rest[n:n + 3], rest[n + 3:n + 8], rest[n + 8:]
        outs, (carry, send_sems, recv_sems, local_sems) = rest[:n], rest[n:]
        i = pl.program_id(0)

        @pl.when(i == 0)
        def _():
            carry[...] = jnp.zeros_like(carry)
            for cp in _gather_copies(ins, outs, send_sems, recv_sems, local_sems):
                cp.start()

        xv = x_ref[...]
        r = lax.rsqrt(jnp.mean(xv * xv, axis=-1, keepdims=True) + EPS)
        hv = ((xv * r) * g_ref[...]) * (1.0 + ada_ref[1:2, :]) + ada_ref[0:1, :]
        hb = hv.astype(BF16)
        h_ref[...] = hb
        fl = _dot_nt(hb, wt_ref[N_MAIN:N_ALL, :])
        fl_ref[...] = fl
        for j in range(N_MAIN // tn):
            proj_ref[:, j * tn:(j + 1) * tn] = _dot_nt(hb, wt_ref[j * tn:(j + 1) * tn, :]).astype(BF16)
        _attention_rows(proj_ref, fl, bf_ref, qg_ref, kg_ref, carry, *rows_refs)

        @pl.when(i == nt - 1)
        def _():
            for cp in _gather_copies(ins, outs, send_sems, recv_sems, local_sems):
                cp.wait()

    any_spec = pl.BlockSpec(memory_space=pl.ANY)
    heads = pl.BlockSpec((HEADS, tm, LANES), lambda i: (0, i, 0))
    heads_t = pl.BlockSpec((HEADS, LANES, tm), lambda i: (0, 0, i))
    vec = pl.BlockSpec((1, ATTN_W), lambda i: (0, 0))
    res = pl.pallas_call(
        body, name="proj_fwd", grid=(nt,),
        in_specs=[pl.BlockSpec((tm, D_MODEL), lambda i: (i, 0)),
                  pl.BlockSpec((3, D_MODEL), lambda i: (0, 0)),
                  pl.BlockSpec((1, D_MODEL), lambda i: (0, 0)),
                  pl.BlockSpec((N_ALL, D_MODEL), lambda i: (0, 0)),
                  pl.BlockSpec((1, LANES), lambda i: (0, 0)), vec, vec] + [any_spec] * n,
        out_specs=[pl.BlockSpec((tm, N_MAIN), lambda i: (i, 0)),
                   pl.BlockSpec((tm, N_FPAD), lambda i: (i, 0)),
                   pl.BlockSpec((tm, D_MODEL), lambda i: (i, 0)),
                   heads, heads, heads, heads_t, heads_t] + [any_spec] * n,
        out_shape=[jax.ShapeDtypeStruct((s, N_MAIN), BF16),
                   jax.ShapeDtypeStruct((s, N_FPAD), F32),
                   jax.ShapeDtypeStruct((s, D_MODEL), BF16)]
        + [jax.ShapeDtypeStruct((HEADS, s, LANES), BF16)] * 3
        + [jax.ShapeDtypeStruct((HEADS, LANES, s), BF16)] * 2
        + [jax.ShapeDtypeStruct((N_DEV,) + a.shape, a.dtype) for a in later],
        scratch_shapes=[pltpu.VMEM((1, LANES), F32)] + _gather_sems(n),
        compiler_params=_params(("arbitrary",)),
    )(x, ada3, norm_g, w_all_t, bf_pad, qg, kg, *later)
    return res[:3], res[3:8], res[8:]


L_ONE_Q, L_F_Q, L_LSE_Q, L_END = HEAD_DIM, HEAD_DIM + 3, HEAD_DIM + 6, HEAD_DIM + 9


def _split3(f):
    hi = f.astype(BF16).astype(F32)
    r = f - hi
    mid = r.astype(BF16).astype(F32)
    return hi, mid, r - mid


def _place3(lane, first, parts, otherwise):
    a, b, c = parts
    return jnp.where(lane == first, a, jnp.where(lane == first + 1, b, jnp.where(lane == first + 2, c, otherwise)))


def _log_forget(fl, bf):
    z = fl + bf
    lf = jnp.minimum(z, 0.0) - jnp.log1p(jnp.exp(-jnp.abs(z)))
    lane = lax.broadcasted_iota(jnp.int32, z.shape, 1)
    return jnp.where(lane < HEADS, lf, 0.0)


def _attention_rows(p_ref, fl, bf_ref, qg_ref, kg_ref, carry, qa_ref, ka_ref, va_ref, kt_ref, vt_ref):
    tm = fl.shape[0]
    scale = HEAD_DIM ** -0.5
    tri = (lax.broadcasted_iota(jnp.int32, (tm, tm), 1) <= lax.broadcasted_iota(jnp.int32, (tm, tm), 0)).astype(F32)
    cum_v = jnp.dot(tri, _log_forget(fl, bf_ref[...]), preferred_element_type=F32,
                    precision=lax.Precision.HIGHEST) + carry[...]
    carry[...] = _sub_row(cum_v, tm - 1)
    lane = lax.broadcasted_iota(jnp.int32, (tm, LANES), 1)
    lo = lane < HEAD_DIM
    v_tail = jnp.where(lane < L_F_Q, 1.0, 0.0)
    for pr in range(ATTN_W // LANES):
        sl = slice(pr * LANES, (pr + 1) * LANES)
        q2 = p_ref[:, OFF_Q + pr * LANES:OFF_Q + (pr + 1) * LANES].astype(F32)
        k2 = p_ref[:, OFF_K + pr * LANES:OFF_K + (pr + 1) * LANES].astype(F32)
        v2 = p_ref[:, OFF_V + pr * LANES:OFF_V + (pr + 1) * LANES].astype(F32)
        rq = lax.rsqrt(_seg_sum(q2 * q2, lo) * (1.0 / HEAD_DIM) + EPS)
        rk = lax.rsqrt(_seg_sum(k2 * k2, lo) * (1.0 / HEAD_DIM) + EPS)
        qn = ((q2 * rq) * qg_ref[:, sl]) * scale
        kn = (k2 * rk) * kg_ref[:, sl]
        for hh in range(2):
            h = 2 * pr + hh
            f3 = _split3(_lane_col(cum_v, h))
            qh = qn if hh == 0 else pltpu.roll(qn, HEAD_DIM, 1)
            kh = kn if hh == 0 else pltpu.roll(kn, HEAD_DIM, 1)
            vh = v2 if hh == 0 else pltpu.roll(v2, HEAD_DIM, 1)
            q_tail = jnp.where(lane < L_F_Q, 1.0, _place3(lane, L_F_Q, f3, 0.0))
            k_tail = _place3(lane, L_ONE_Q, tuple(-f for f in f3), jnp.where(lane < L_END, 1.0, 0.0))
            k_row = jnp.where(lo, kh, k_tail)
            v_row = jnp.where(lo, vh, v_tail)
            qa_ref[h] = jnp.where(lo, qh, q_tail).astype(BF16)
            ka_ref[h] = k_row.astype(BF16)
            va_ref[h] = v_row.astype(BF16)
            kt_ref[h] = k_row.T.astype(BF16)
            vt_ref[h] = v_row.T.astype(BF16)


def _causal_t(t):
    return lax.broadcasted_iota(jnp.int32, (t, t), 0) <= lax.broadcasted_iota(jnp.int32, (t, t), 1)


def _tri_steps(nt, q_major):
    if q_major:
        pairs = [(i, j) for i in range(nt) for j in range(i + 1)]
    else:
        pairs = [(i, j) for j in range(nt) for i in range(j, nt)]
    return (jnp.asarray(np.array([p[0] for p in pairs], np.int32)),
            jnp.asarray(np.array([p[1] for p in pairs], np.int32)))


def _attn_fwd(qa, ka, vt, proj):
    s = qa.shape[1]
    t = min(TQ, s)
    it, jt = _tri_steps(s // t, True)
    hp = HEADS_PER_STEP
    wide = hp * HEAD_DIM
    za_blk = OFF_ZA // wide

    def body(it_ref, jt_ref, q_ref, k_ref, vt_ref, za_ref, attn_ref, oa_ref, qb_ref, m_s, acc_s, pair_s):
        step = pl.program_id(1)
        i, j = it_ref[step], jt_ref[step]

        @pl.when(j == 0)
        def _():
            m_s[...] = jnp.full_like(m_s, NEG)
            acc_s[...] = jnp.zeros_like(acc_s)

        def update(masked):
            for hh in range(hp):
                st = _dot_nt(k_ref[hh], q_ref[hh])
                if masked:
                    st = jnp.where(_causal_t(t), st, NEG)
                m_prev = m_s[hh]
                m_next = jnp.maximum(m_prev, jnp.max(st, axis=0, keepdims=True))
                alpha = jnp.exp(m_prev - m_next)
                pt = jnp.exp(st - m_next).astype(BF16)
                acc_s[hh] = acc_s[hh] * alpha + _dot(vt_ref[hh], pt)
                m_s[hh] = m_next

        @pl.when(j < i)
        def _():
            update(False)

        @pl.when(j == i)
        def _():
            update(True)
            row = lax.broadcasted_iota(jnp.int32, (LANES, t), 0)
            lane = lax.broadcasted_iota(jnp.int32, (t, LANES), 1)
            for hh in range(hp):
                l_row = acc_s[hh, L_ONE_Q:L_ONE_Q + 1, :]
                pair_s[hh * HEAD_DIM:(hh + 1) * HEAD_DIM, :] = acc_s[hh, 0:HEAD_DIM, :] / l_row
                lse3 = _split3(m_s[hh] + jnp.log(l_row))
                tail_t = _place3(row, L_LSE_Q, tuple(-x for x in lse3), 0.0)
                keep_q = jnp.logical_or(lane < L_LSE_Q, lane >= L_END)
                qb_ref[hh] = jnp.where(keep_q, q_ref[hh].astype(F32), tail_t.T).astype(BF16)
            out = pair_s[...].T
            attn_ref[...] = out
            z = za_ref[...].astype(F32)
            oa_ref[...] = (out * (z * _sigmoid(z))).astype(BF16)

    pair_q = pl.BlockSpec((hp, t, LANES), lambda p, n, it_, jt_: (p, it_[n], 0))
    pair_k = pl.BlockSpec((hp, t, LANES), lambda p, n, it_, jt_: (p, jt_[n], 0))
    pair_kt = pl.BlockSpec((hp, LANES, t), lambda p, n, it_, jt_: (p, 0, jt_[n]))
    out_q = pl.BlockSpec((t, wide), lambda p, n, it_, jt_: (it_[n], p))
    return pl.pallas_call(
        body, name="attn_fwd",
        grid_spec=pltpu.PrefetchScalarGridSpec(
            num_scalar_prefetch=2, grid=(HEADS // hp, it.shape[0]),
            in_specs=[pair_q, pair_k, pair_kt,
                      pl.BlockSpec((t, wide), lambda p, n, it_, jt_: (it_[n], za_blk + p))],
            out_specs=[out_q, out_q, pair_q],
            scratch_shapes=[pltpu.VMEM((hp, 1, t), F32), pltpu.VMEM((hp, LANES, t), F32),
                            pltpu.VMEM((wide, t), F32)]),
        out_shape=[jax.ShapeDtypeStruct((s, ATTN_W), F32),
                   jax.ShapeDtypeStruct((s, ATTN_W), BF16),
                   jax.ShapeDtypeStruct((HEADS, s, LANES), BF16)],
        compiler_params=_params(("parallel", "arbitrary")),
    )(it, jt, qa, ka, vt, proj)


def _conv_parts(gb_ref, gc_ref, u_ref, zb_ref, gch_ref, uh_ref, first, w_ref, tm):
    gb, gc = gb_ref[...].astype(F32), gc_ref[...].astype(F32)
    u, zb = u_ref[...].astype(F32), zb_ref[...].astype(F32)
    cu = gc * u
    cu_h = jnp.where(first, 0.0, gch_ref[...].astype(F32) * uh_ref[...].astype(F32))
    prev1, prev2 = _sub_row(cu_h, HALO - 1), _sub_row(cu_h, HALO - 2)
    row = lax.broadcasted_iota(jnp.int32, cu.shape, 0)
    r1 = jnp.where(row == 0, prev1, pltpu.roll(cu, 1, 0))
    r2 = jnp.where(row == 0, prev2, jnp.where(row == 1, prev1, pltpu.roll(cu, 2, 0)))
    conv = w_ref[2:3, :] * cu + w_ref[1:2, :] * r1 + w_ref[0:1, :] * r2
    return gb, gc, u, zb, cu, r1, r2, conv


def _conv_specs(tm, s, width=LANES):
    def tile(off):
        return pl.BlockSpec((tm, width), lambda c, i: (i, off // width + c))

    def before(off):
        return pl.BlockSpec((HALO, width), lambda c, i: (jnp.maximum(i * (tm // HALO) - 1, 0), off // width + c))

    def after(off):
        return pl.BlockSpec((HALO, width),
                            lambda c, i: (jnp.minimum((i + 1) * (tm // HALO), s // HALO - 1), off // width + c))

    return ([tile(OFF_CB), tile(OFF_CC), tile(OFF_CU), tile(OFF_CZ)], [before(OFF_CC), before(OFF_CU)],
            [after(OFF_CB), after(OFF_CZ)])


def _tail(oa, attn, proj, x, target, ada3, wa, wb, wo, conv_w):
    s = x.shape[0]
    tm = min(TM_TAIL, s)
    gab_blk = OFF_GA // (2 * D_MODEL)
    za_blk = OFF_ZA // ATTN_W
    tiles, befores, _ = _conv_specs(tm, s, CONV_W)

    def body(oa_ref, attn_ref, za_ref, gb_ref, gc_ref, u_ref, zb_ref, gch_ref, uh_ref, cw_ref, gab_ref, x_ref, t_ref,
             ada_ref, wag_ref, wbg_ref, wo_ref,
             dy_ref, dgab_ref, do_ref, dza_ref, dob_ref, dwo_out, dwa_out, dwb_out, dgate_ref, loss_ref,
             dwo_ref, dwa_ref, dwb_ref, wa_ref, wb_ref):
        first = pl.program_id(0) == 0

        @pl.when(first)
        def _():
            for p in range(N_DEV):
                wa_ref[:, p * LANES:(p + 1) * LANES] = wag_ref[p]
                wb_ref[:, p * LANES:(p + 1) * LANES] = wbg_ref[p]
            dwo_ref[...] = jnp.zeros_like(dwo_ref)
            dwa_ref[...] = jnp.zeros_like(dwa_ref)
            dwb_ref[...] = jnp.zeros_like(dwb_ref)
            dgate_ref[...] = jnp.zeros_like(dgate_ref)
            loss_ref[...] = jnp.zeros_like(loss_ref)

        gb, _, _, zb, _, _, _, conv = _conv_parts(gb_ref, gc_ref, u_ref, zb_ref, gch_ref, uh_ref, first, cw_ref, tm)
        ob_v = (gb * conv * (zb * _sigmoid(zb))).astype(BF16)
        oa_v = oa_ref[...]
        wa_v, wb_v, wo_v = wa_ref[...], wb_ref[...], wo_ref[...]
        a2 = _dot(oa_v, wa_v)
        b2 = _dot(ob_v, wb_v)
        sa = _sigmoid(gab_ref[:, 0:D_MODEL].astype(F32))
        sb = _sigmoid(gab_ref[:, D_MODEL:2 * D_MODEL].astype(F32))
        mb = (sa * a2 + sb * b2).astype(BF16)
        mo = _dot(mb, wo_v)
        gate = ada_ref[2:3, :]
        err = (x_ref[...] + gate * mo) - t_ref[...]
        dy = err * (1.0 / D_MODEL)
        dy_ref[...] = dy
        loss_ref[...] += 0.5 * jnp.sum(err * err) * (1.0 / D_MODEL)
        dgate_ref[...] += jnp.sum(dy * mo, axis=0, keepdims=True)
        dmo = (dy * gate).astype(BF16)
        dmerged = _dot_nt(dmo, wo_v)
        dwo_ref[...] += _dot_tn(mb, dmo)
        da2 = (dmerged * sa).astype(BF16)
        db2 = (dmerged * sb).astype(BF16)
        dgab_ref[:, 0:D_MODEL] = (dmerged * a2 * (sa * (1.0 - sa))).astype(BF16)
        dgab_ref[:, D_MODEL:2 * D_MODEL] = (dmerged * b2 * (sb * (1.0 - sb))).astype(BF16)
        doa = _dot_nt(da2, wa_v)
        dob_ref[...] = _dot_nt(db2, wb_v)
        dwa_ref[...] += _dot_tn(oa_v, da2)
        dwb_ref[...] += _dot_tn(ob_v, db2)

        lane = lax.broadcasted_iota(jnp.int32, (tm, LANES), 1)
        lo = lane < HEAD_DIM
        for pr in range(ATTN_W // LANES):
            sl = slice(pr * LANES, (pr + 1) * LANES)
            g, a, z = doa[:, sl], attn_ref[:, sl], za_ref[:, sl].astype(F32)
            sg = _sigmoid(z)
            dat = (g * (z * sg)).astype(BF16).astype(F32)
            prod = dat * a
            dza_ref[:, sl] = (g * a * (sg * (1.0 + z * (1.0 - sg)))).astype(BF16)
            for hh in range(2):
                sel = lo if hh == 0 else jnp.logical_not(lo)
                delta3 = _split3(jnp.sum(jnp.where(sel, prod, 0.0), axis=-1, keepdims=True))
                dh = dat if hh == 0 else pltpu.roll(dat, HEAD_DIM, 1)
                tail_lanes = _place3(lane, L_ONE_Q, tuple(-d for d in delta3), 0.0)
                do_ref[2 * pr + hh] = jnp.where(lo, dh, tail_lanes).astype(BF16)

        @pl.when(pl.program_id(0) == pl.num_programs(0) - 1)
        def _():
            dwo_out[...] = dwo_ref[...].astype(BF16)
            for p in range(N_DEV):
                dwa_out[p] = dwa_ref[:, p * LANES:(p + 1) * LANES].astype(BF16)
                dwb_out[p] = dwb_ref[:, p * LANES:(p + 1) * LANES].astype(BF16)

    half = pl.BlockSpec((tm, ATTN_W), lambda i: (i, 0))
    full = pl.BlockSpec((tm, D_MODEL), lambda i: (i, 0))

    def const(shape):
        return pl.BlockSpec(shape, lambda i: (0, 0))

    slabs = pl.BlockSpec((N_DEV, ATTN_W, LANES), lambda i: (0, 0, 0))

    def one_axis(spec):
        return pl.BlockSpec(spec.block_shape, lambda i, f=spec.index_map: f(0, i))

    return pl.pallas_call(
        body, name="tail", grid=(s // tm,),
        in_specs=[half, half, pl.BlockSpec((tm, ATTN_W), lambda i: (i, za_blk))]
        + [one_axis(sp) for sp in tiles + befores]
        + [const((3, CONV_W)), pl.BlockSpec((tm, 2 * D_MODEL), lambda i: (i, gab_blk)), full, full,
           const((3, D_MODEL)), slabs, slabs, const((D_MODEL, D_MODEL))],
        out_specs=[full, pl.BlockSpec((tm, 2 * D_MODEL), lambda i: (i, 0)),
                   pl.BlockSpec((HEADS, tm, LANES), lambda i: (0, i, 0)), half, half,
                   const((D_MODEL, D_MODEL)), slabs, slabs, const((1, D_MODEL)), const((1, LANES))],
        out_shape=[jax.ShapeDtypeStruct((s, D_MODEL), F32),
                   jax.ShapeDtypeStruct((s, 2 * D_MODEL), BF16),
                   jax.ShapeDtypeStruct((HEADS, s, LANES), BF16),
                   jax.ShapeDtypeStruct((s, ATTN_W), BF16),
                   jax.ShapeDtypeStruct((s, CONV_W), F32),
                   jax.ShapeDtypeStruct((D_MODEL, D_MODEL), BF16),
                   jax.ShapeDtypeStruct((N_DEV, ATTN_W, LANES), BF16),
                   jax.ShapeDtypeStruct((N_DEV, CONV_W, LANES), BF16),
                   jax.ShapeDtypeStruct((1, D_MODEL), F32),
                   jax.ShapeDtypeStruct((1, LANES), F32)],
        scratch_shapes=[pltpu.VMEM((D_MODEL, D_MODEL), F32), pltpu.VMEM((ATTN_W, D_MODEL), F32),
                        pltpu.VMEM((CONV_W, D_MODEL), F32), pltpu.VMEM((ATTN_W, D_MODEL), BF16),
                        pltpu.VMEM((CONV_W, D_MODEL), BF16)],
        compiler_params=_params(("arbitrary",)),
    )(oa, attn, proj, *([proj] * 6), conv_w, proj, x, target, ada3, wa, wb, wo)


def _attn_bwd(qb, ka, kt, va, do, proj, qg, kg):
    s = qb.shape[1]
    t = min(TQ, s)
    nt = s // t
    hp = HEADS_PER_STEP
    assert hp == HEADS, "all heads share one (S, 1536) output block"
    wide = hp * HEAD_DIM
    scale = HEAD_DIM ** -0.5
    it, jt = _tri_steps(nt, False)

    def body(it_ref, jt_ref, q_ref, k_ref, kt_ref, v_ref, do_ref, qraw_ref, kraw_ref, qg_ref, kg_ref,
             dqkv_ref, dqg_ref, dkg_ref, dcum_ref, dqt_s, dk_s, dv_s, rows_s):
        grp, step = pl.program_id(0), pl.program_id(1)
        i, j = it_ref[step], jt_ref[step]
        lane = lax.broadcasted_iota(jnp.int32, (t, LANES), 1)
        lo = lane < HEAD_DIM

        @pl.when(step == 0)
        def _():
            dqt_s[...] = jnp.zeros_like(dqt_s)
            dqg_ref[...] = jnp.zeros_like(dqg_ref)
            dkg_ref[...] = jnp.zeros_like(dkg_ref)

        @pl.when(i == j)
        def _():
            dk_s[...] = jnp.zeros_like(dk_s)
            dv_s[...] = jnp.zeros_like(dv_s)

        def update(masked):
            for hh in range(hp):
                qh, doh = q_ref[hh], do_ref[hh]
                st = _dot_nt(k_ref[hh], qh)
                if masked:
                    st = jnp.where(_causal_t(t), st, NEG)
                pt = jnp.exp(st)
                dst = (pt * _dot_nt(v_ref[hh], doh)).astype(BF16)
                dv_s[hh] += _dot(pt.astype(BF16), doh)
                dk_s[hh] += _dot(dst, qh)
                dqt_s[hh, i] += _dot(kt_ref[hh], dst)

        def pair(a, b):
            return jnp.where(lo, a, pltpu.roll(b, HEAD_DIM, 1))

        def norm_bwd(raw, dy, g, dg_ref, off, sl):
            r = lax.rsqrt(_seg_sum(raw * raw, lo) * (1.0 / HEAD_DIM) + EPS)
            xhat = raw * r
            dg_ref[:, sl] += jnp.sum(dy * xhat, axis=0, keepdims=True)
            dxh = dy * g
            dx = r * (dxh - xhat * (_seg_sum(dxh * xhat, lo) * (1.0 / HEAD_DIM)))
            dqkv_ref[:, off + sl.start:off + sl.stop] = dx.astype(BF16)

        @pl.when(i > j)
        def _():
            update(False)

        @pl.when(i == j)
        def _():
            update(True)
            dq_rows = [dqt_s[hh, i].T for hh in range(hp)]
            rows = jnp.zeros((t, LANES), F32)
            for hh in range(hp):
                rows = jnp.where(lane == grp * hp + hh, _lane_col(dq_rows[hh], L_F_Q), rows)
            rows_s[...] = rows
            for pr in range(hp // 2):
                sl = slice(pr * LANES, (pr + 1) * LANES)
                norm_bwd(qraw_ref[:, sl].astype(F32), pair(dq_rows[2 * pr], dq_rows[2 * pr + 1]) * scale,
                         qg_ref[:, sl], dqg_ref, OFF_Q, sl)

        @pl.when(i == nt - 1)
        def _():
            dcum = rows_s[...]
            for hh in range(hp):
                dcum = jnp.where(lane == grp * hp + hh, dcum - _lane_col(dk_s[hh], L_ONE_Q), dcum)
            dcum_ref[0] = dcum
            for pr in range(hp // 2):
                sl = slice(pr * LANES, (pr + 1) * LANES)
                norm_bwd(kraw_ref[:, sl].astype(F32), pair(dk_s[2 * pr], dk_s[2 * pr + 1]),
                         kg_ref[:, sl], dkg_ref, OFF_K, sl)
                dqkv_ref[:, OFF_V + sl.start:OFF_V + sl.stop] = pair(dv_s[2 * pr], dv_s[2 * pr + 1]).astype(BF16)

    pair_q = pl.BlockSpec((hp, t, LANES), lambda p, n, it_, jt_: (p, it_[n], 0))
    pair_k = pl.BlockSpec((hp, t, LANES), lambda p, n, it_, jt_: (p, jt_[n], 0))
    pair_kt = pl.BlockSpec((hp, LANES, t), lambda p, n, it_, jt_: (p, 0, jt_[n]))
    tok3 = pl.BlockSpec((t, 3 * ATTN_W), lambda p, n, it_, jt_: (jt_[n], 0))
    gain = pl.BlockSpec((1, wide), lambda p, n, it_, jt_: (0, p))
    return pl.pallas_call(
        body, name="attn_bwd",
        grid_spec=pltpu.PrefetchScalarGridSpec(
            num_scalar_prefetch=2, grid=(HEADS // hp, it.shape[0]),
            in_specs=[pair_q, pair_k, pair_kt, pair_k, pair_q,
                      pl.BlockSpec((t, wide), lambda p, n, it_, jt_: (jt_[n], OFF_Q // wide + p)),
                      pl.BlockSpec((t, wide), lambda p, n, it_, jt_: (jt_[n], OFF_K // wide + p)), gain, gain],
            out_specs=[tok3, gain, gain,
                       pl.BlockSpec((1, t, LANES), lambda p, n, it_, jt_: (p, jt_[n], 0))],
            scratch_shapes=[pltpu.VMEM((hp, nt, LANES, t), F32), pltpu.VMEM((hp, t, LANES), F32),
                            pltpu.VMEM((hp, t, LANES), F32), pltpu.VMEM((t, LANES), F32)]),
        out_shape=[jax.ShapeDtypeStruct((s, 3 * ATTN_W), BF16)]
        + [jax.ShapeDtypeStruct((1, ATTN_W), F32)] * 2
        + [jax.ShapeDtypeStruct((HEADS // hp, s, LANES), F32)],
        compiler_params=_params(("parallel", "arbitrary")),
    )(it, jt, qb, ka, kt, va, do, proj, proj, qg, kg)


def _forget_bwd(dcum, fl, bf_pad):
    s = fl.shape[0]
    tc = min(TC_CUM, s)
    n = s // tc

    def body(dc_ref, fl_ref, bf_ref, df_ref, dbf_ref, carry):
        @pl.when(pl.program_id(0) == 0)
        def _():
            carry[...] = jnp.zeros_like(carry)
            dbf_ref[...] = jnp.zeros_like(dbf_ref)
        r = lax.broadcasted_iota(jnp.int32, (tc, tc), 0)
        cidx = lax.broadcasted_iota(jnp.int32, (tc, tc), 1)
        tri = (cidx >= r).astype(F32)
        dc = dc_ref[0]
        for grp in range(1, dcum.shape[0]):
            dc = dc + dc_ref[grp]
        dlf = jnp.dot(tri, dc, preferred_element_type=F32, precision=lax.Precision.HIGHEST) + carry[...]
        carry[...] += jnp.sum(dc, axis=0, keepdims=True)
        lane = lax.broadcasted_iota(jnp.int32, (tc, LANES), 1)
        dfl = jnp.where(lane < HEADS, dlf * _sigmoid(-(fl_ref[...] + bf_ref[...])), 0.0)
        df_ref[...] = dfl.astype(BF16)
        dbf_ref[...] += jnp.sum(dfl, axis=0, keepdims=True)

    rev = pl.BlockSpec((tc, LANES), lambda i: (n - 1 - i, 0))
    vec = pl.BlockSpec((1, LANES), lambda i: (0, 0))
    return pl.pallas_call(
        body, name="forget_bwd", grid=(n,),
        in_specs=[pl.BlockSpec((dcum.shape[0], tc, LANES), lambda i: (0, n - 1 - i, 0)), rev, vec],
        out_specs=[rev, vec],
        out_shape=[jax.ShapeDtypeStruct((s, LANES), BF16), jax.ShapeDtypeStruct((1, LANES), F32)],
        scratch_shapes=[pltpu.VMEM((1, LANES), F32)],
        compiler_params=_params(("arbitrary",)),
    )(dcum, fl, bf_pad)


def _conv_bwd(dob, proj, conv_w):
    s = dob.shape[0]
    tm = min(TM_ELEM, s)
    wd = CONV_W
    tiles, befores, afters = _conv_specs(tm, s, wd)

    def body(dob_ref, dnext_ref, gb_ref, gc_ref, u_ref, zb_ref, gch_ref, uh_ref, gbn_ref, zbn_ref, w_ref,
             out_ref, dw_ref):
        i = pl.program_id(1)

        @pl.when(i == 0)
        def _():
            dw_ref[...] = jnp.zeros_like(dw_ref)
        gb, gc, u, zb, cu, r1, r2, conv = _conv_parts(gb_ref, gc_ref, u_ref, zb_ref, gch_ref, uh_ref, i == 0, w_ref, tm)
        g = dob_ref[...]
        sg = _sigmoid(zb)
        sz = zb * sg
        dconv = g * gb * sz
        zn = zbn_ref[0:8, :].astype(F32)
        dcn = jnp.where(i == pl.num_programs(1) - 1, 0.0,
                        dnext_ref[...] * gbn_ref[0:8, :].astype(F32) * (zn * _sigmoid(zn)))
        nxt1, nxt2 = _sub_row(dcn, 0), _sub_row(dcn, 1)
        row = lax.broadcasted_iota(jnp.int32, (tm, wd), 0)
        f1 = jnp.where(row == tm - 1, nxt1, pltpu.roll(dconv, tm - 1, 0))
        f2 = jnp.where(row == tm - 2, nxt1, jnp.where(row == tm - 1, nxt2, pltpu.roll(dconv, tm - 2, 0)))
        dcu = w_ref[2:3, :] * dconv + w_ref[1:2, :] * f1 + w_ref[0:1, :] * f2
        out_ref[:, 0:wd] = (g * conv * sz).astype(BF16)
        out_ref[:, wd:2 * wd] = (dcu * u).astype(BF16)
        out_ref[:, 2 * wd:3 * wd] = (dcu * gc).astype(BF16)
        out_ref[:, 3 * wd:4 * wd] = (g * gb * conv * (sg * (1.0 + zb * (1.0 - sg)))).astype(BF16)
        w_row = lax.broadcasted_iota(jnp.int32, (3, wd), 0)
        dw0 = jnp.sum(dconv * r2, axis=0, keepdims=True)
        dw1 = jnp.sum(dconv * r1, axis=0, keepdims=True)
        dw2 = jnp.sum(dconv * cu, axis=0, keepdims=True)
        dw_ref[...] += jnp.where(w_row == 0, dw0, jnp.where(w_row == 1, dw1, dw2))

    blk = pl.BlockSpec((tm, wd), lambda c, i: (i, c))
    nxt = pl.BlockSpec((8, wd), lambda c, i: (jnp.minimum((i + 1) * (tm // 8), s // 8 - 1), c))
    wspec = pl.BlockSpec((3, wd), lambda c, i: (0, c))
    return pl.pallas_call(
        body, name="conv_bwd", grid=(CONV_W // wd, s // tm),
        in_specs=[blk, nxt] + tiles + befores + afters + [wspec],
        out_specs=[pl.BlockSpec((tm, 4 * wd), lambda c, i: (i, c)), wspec],
        out_shape=[jax.ShapeDtypeStruct((s, 4 * CONV_W), BF16), jax.ShapeDtypeStruct((3, CONV_W), F32)],
        compiler_params=_params(("parallel", "arbitrary")),
    )(dob, dob, *([proj] * 8), conv_w)


def _piece_layout(pieces):
    offs, off = [], 0
    for p in pieces:
        offs.append((off, p.shape[1]))
        off += p.shape[1]
    assert off == N_ALL, off
    return offs


def _dw_in(h, pieces, chip_sums):
    s = h.shape[0]
    tk, tn = min(TK_DW, s), TN_DW
    nk = s // tk
    nn = N_MAIN // tn
    main, fpiece = pieces[:-1], pieces[-1]
    layout = _piece_layout(pieces)[:-1]
    n_main = len(main)
    nx = len(chip_sums)

    def body(*refs):
        p_refs, f_ref, h_ref = refs[:n_main], refs[n_main], refs[n_main + 1]
        ins, refs = refs[n_main + 2:n_main + 2 + nx], refs[n_main + 2 + nx:]
        out_ref, outf_ref = refs[:2]
        outs, (acc, accf, send_sems, recv_sems, local_sems) = refs[2:2 + nx], refs[2 + nx:]
        n, k = pl.program_id(0), pl.program_id(1)

        @pl.when(jnp.logical_and(n == 0, k == 0))
        def _():
            for cp in _all_to_all_copies(ins, outs, send_sems, recv_sems, local_sems):
                cp.start()

        @pl.when(k == 0)
        def _():
            acc[...] = jnp.zeros_like(acc)
        hv = h_ref[pl.ds(pl.multiple_of(k * tk, tk), tk), :]
        for p_ref, (off, width) in zip(p_refs, layout):
            @pl.when(jnp.logical_and(n >= off // tn, n < (off + width) // tn))
            def _():
                acc[...] += _dot_tn(p_ref[...], hv)

        @pl.when(k == nk - 1)
        def _():
            out_ref[...] = acc[...].astype(BF16)

        @pl.when(n == 0)
        def _():
            @pl.when(k == 0)
            def _():
                accf[...] = jnp.zeros_like(accf)
            accf[...] += _dot_tn(f_ref[...], hv)

            @pl.when(k == nk - 1)
            def _():
                outf_ref[...] = accf[...].astype(BF16)

        @pl.when(jnp.logical_and(n == nn - 1, k == nk - 1))
        def _():
            for cp in _all_to_all_copies(ins, outs, send_sems, recv_sems, local_sems):
                cp.wait()

    def piece_spec(off, width):
        lo, hi = off // tn, (off + width) // tn

        def index(n, k):
            active = jnp.logical_and(n >= lo, n < hi)
            return jnp.where(active, k, 0), jnp.clip(n - lo, 0, hi - lo - 1)
        return pl.BlockSpec((tk, tn), index)

    any_spec = pl.BlockSpec(memory_space=pl.ANY)
    res = pl.pallas_call(
        body, name="dw_in", grid=(nn, nk),
        in_specs=[piece_spec(off, width) for off, width in layout]
        + [pl.BlockSpec((tk, N_FPAD), lambda n, k: (jnp.where(n == 0, k, 0), 0)),
           pl.BlockSpec((s, D_MODEL), lambda n, k: (0, 0))] + [any_spec] * nx,
        out_specs=[pl.BlockSpec((tn, D_MODEL), lambda n, k: (n, 0)),
                   pl.BlockSpec((N_FPAD, D_MODEL), lambda n, k: (0, 0))] + [any_spec] * nx,
        out_shape=[jax.ShapeDtypeStruct((N_MAIN, D_MODEL), BF16), jax.ShapeDtypeStruct((N_FPAD, D_MODEL), BF16)]
        + [jax.ShapeDtypeStruct(a.shape, a.dtype) for a in chip_sums],
        scratch_shapes=[pltpu.VMEM((tn, D_MODEL), F32), pltpu.VMEM((N_FPAD, D_MODEL), F32)] + _gather_sems(nx),
        compiler_params=_params(("arbitrary", "arbitrary")),
    )(*main, fpiece, h, *chip_sums)
    return res[:2], res[2:]


def _dh_and_dx(pieces, w_all_t, x, dy, ada3, norm_g, chip_sums):
    s = x.shape[0]
    tm = min(TM_DH, s)
    nt = s // tm
    n = len(chip_sums)
    npc = len(pieces)
    layout = _piece_layout(pieces)

    def body(*refs):
        p_refs, refs = refs[:npc], refs[npc:]
        wt_ref, x_ref, dy_ref, ada_ref, g_ref = refs[:5]
        ins, refs = refs[5:5 + n], refs[5 + n:]
        gx_ref, dsh_ref, dsc_ref, dg_ref = refs[:4]
        outs, (send_sems, recv_sems, local_sems) = refs[4:4 + n], refs[4 + n:]
        i = pl.program_id(0)

        @pl.when(i == 0)
        def _():
            for cp in _chip_copies(ins, outs, send_sems, recv_sems, local_sems):
                cp.start()
            dsh_ref[...] = jnp.zeros_like(dsh_ref)
            dsc_ref[...] = jnp.zeros_like(dsc_ref)
            dg_ref[...] = jnp.zeros_like(dg_ref)

        dh = None
        for p_ref, (off, width) in zip(p_refs, layout):
            part = _dot(p_ref[...], wt_ref[off:off + width, :])
            dh = part if dh is None else dh + part
        xv = x_ref[...]
        r = lax.rsqrt(jnp.mean(xv * xv, axis=-1, keepdims=True) + EPS)
        xhat = xv * r
        g = g_ref[...]
        one_sc = 1.0 + ada_ref[1:2, :]
        dsh_ref[...] += jnp.sum(dh, axis=0, keepdims=True)
        dsc_ref[...] += jnp.sum(dh * (xhat * g), axis=0, keepdims=True)
        dg_ref[...] += jnp.sum(dh * xhat, axis=0, keepdims=True) * one_sc
        dxh = dh * (g * one_sc)
        dx = r * (dxh - xhat * jnp.mean(dxh * xhat, axis=-1, keepdims=True))
        gx_ref[...] = dy_ref[...] + dx

        @pl.when(i == nt - 1)
        def _():
            for cp in _chip_copies(ins, outs, send_sems, recv_sems, local_sems):
                cp.wait()

    full = pl.BlockSpec((tm, D_MODEL), lambda i: (i, 0))
    vec = pl.BlockSpec((1, D_MODEL), lambda i: (0, 0))
    any_spec = pl.BlockSpec(memory_space=pl.ANY)
    res = pl.pallas_call(
        body, name="dh_dx", grid=(nt,),
        in_specs=[pl.BlockSpec((tm, p.shape[1]), lambda i: (i, 0)) for p in pieces]
        + [pl.BlockSpec((N_ALL, D_MODEL), lambda i: (0, 0)), full, full,
           pl.BlockSpec((3, D_MODEL), lambda i: (0, 0)), vec] + [any_spec] * n,
        out_specs=[full, vec, vec, vec] + [any_spec] * n,
        out_shape=[jax.ShapeDtypeStruct((s, D_MODEL), F32)] + [jax.ShapeDtypeStruct((1, D_MODEL), F32)] * 3
        + [jax.ShapeDtypeStruct(a.shape, a.dtype) for a in chip_sums],
        scratch_shapes=[pltpu.SemaphoreType.DMA((n * 3,)), pltpu.SemaphoreType.DMA((n * 3,)),
                        pltpu.SemaphoreType.DMA((n,))],
        compiler_params=_params(("arbitrary",)),
    )(*pieces, w_all_t, x, dy, ada3, norm_g, *chip_sums)
    return res[:4], res[4:]


def _sum_small(vec_all, qg_parts, kg_parts):
    def body(v_ref, q_ref, k_ref, tot_ref, gq_ref, gk_ref):
        tot = v_ref[0:1, :]
        for p in range(1, N_DEV):
            tot = tot + v_ref[p:p + 1, :]
        tot_ref[...] = tot
        gq_ref[...] = jnp.sum(q_ref[...], axis=0, keepdims=True)
        gk_ref[...] = jnp.sum(k_ref[...], axis=0, keepdims=True)

    n = vec_all.shape[-1]
    return pl.pallas_call(
        body, name="sum_small",
        out_shape=[jax.ShapeDtypeStruct((1, n), F32),
                   jax.ShapeDtypeStruct((1, HEAD_DIM), F32), jax.ShapeDtypeStruct((1, HEAD_DIM), F32)],
        compiler_params=_params(),
    )(vec_all, qg_parts, kg_parts)


def _grad_w_ada(c_cols, dada_rows):
    def body(c_ref, d_ref, out_ref):
        acc = c_ref[0] * d_ref[0]
        for b in range(1, N_DEV):
            acc = acc + c_ref[b] * d_ref[b]
        out_ref[...] = acc

    return pl.pallas_call(
        body, name="grad_w_ada",
        out_shape=jax.ShapeDtypeStruct((D_MODEL, ADA_SHARD), F32),
        compiler_params=_params(),
    )(c_cols, dada_rows)


def _adam_step(w, m, v, g):
    c1 = 1.0 / (1.0 - ADAM_B1 ** ADAM_STEP)
    c2 = 1.0 / (1.0 - ADAM_B2 ** ADAM_STEP)
    m_new = ADAM_B1 * m + (1.0 - ADAM_B1) * g
    v_new = ADAM_B2 * v + (1.0 - ADAM_B2) * (g * g)
    return -ADAM_LR * ((m_new * c1) / (jnp.sqrt(v_new * c2) + ADAM_EPS) + ADAM_WD * w), m_new, v_new


def _adamw_small(params, name):
    n = len(params)
    stacked = [p[3].ndim == p[0].ndim + 1 for p in params]

    def body(*refs):
        ins, outs = refs[:4 * n], refs[4 * n:]
        for k in range(n):
            w_ref, m_ref, v_ref, g_ref = ins[4 * k:4 * k + 4]
            go_ref, d_ref, mo_ref, vo_ref = outs[4 * k:4 * k + 4]
            if stacked[k]:
                g = g_ref[0].astype(F32)
                for p in range(1, g_ref.shape[0]):
                    g = g + g_ref[p].astype(F32)
            else:
                g = g_ref[...]
            go_ref[...] = g
            d_ref[...], mo_ref[...], vo_ref[...] = _adam_step(w_ref[...], m_ref[...], v_ref[...], g)

    res = pl.pallas_call(
        body, name=name,
        out_shape=[jax.ShapeDtypeStruct(p[0].shape, F32) for p in params for _ in range(4)],
        compiler_params=_params(),
    )(*[a for p in params for a in p])
    return [tuple(res[4 * k:4 * k + 4]) for k in range(n)]


def _adamw(w, m, v, g_parts, name):
    rows, cols = w.shape
    n_parts = g_parts.shape[0]
    tr = 256 if rows % 256 == 0 else rows
    tc = 256 if (tr == rows and rows > 256 and cols % 256 == 0) else cols

    def body(w_ref, m_ref, v_ref, g_ref, go_ref, d_ref, mo_ref, vo_ref):
        g = g_ref[0].astype(F32)
        for p in range(1, n_parts):
            g = g + g_ref[p].astype(F32)
        go_ref[...] = g
        d_ref[...], mo_ref[...], vo_ref[...] = _adam_step(w_ref[...], m_ref[...], v_ref[...], g)

    blk = pl.BlockSpec((tr, tc), lambda i, j: (i, j))
    return pl.pallas_call(
        body, name=name, grid=(rows // tr, cols // tc),
        in_specs=[blk, blk, blk, pl.BlockSpec((n_parts, tr, tc), lambda i, j: (0, i, j))],
        out_specs=[blk] * 4,
        out_shape=[jax.ShapeDtypeStruct((rows, cols), F32)] * 4,
        compiler_params=_params(("parallel", "parallel")),
    )(w, m, v, g_parts)


_O_F = 1536


W_TILE = 16
WIN_ROWS = 784


def _internal_start(p):
    return p * IN_SHARD - (HEADS if p * IN_SHARD > _O_F else 0)


def _shard_window(wt_shard, me):
    lo = me * IN_SHARD
    o = lo + lax.broadcasted_iota(jnp.int32, (IN_SHARD, 1), 0)
    is_f = jnp.logical_and(o >= _O_F, o < _O_F + HEADS)
    start = lo - jnp.where(lo > _O_F, HEADS, 0)
    window = lax.dynamic_update_slice(jnp.zeros((WIN_ROWS + W_TILE, D_MODEL), BF16),
                                      jnp.where(is_f, 0.0, wt_shard).astype(BF16), (start % W_TILE, 0))
    first = jnp.clip(_O_F - lo, 0, IN_SHARD - W_TILE)
    near = lax.dynamic_slice(wt_shard, (first, 0), (W_TILE, D_MODEL))
    j = lax.broadcasted_iota(jnp.int32, (W_TILE, 1), 0)
    src = _O_F - lo + j
    ok = jnp.logical_and(j < HEADS, jnp.logical_and(src >= 0, src < IN_SHARD))
    f_tile = jnp.where(ok, jnp.roll(near, first - (_O_F - lo), axis=0), 0.0).astype(BF16)
    return lax.dynamic_update_slice(window, f_tile, (WIN_ROWS, 0))


def _assemble_w(windows):
    chunk = 112

    def body(g_ref, out_ref):
        out_ref[WIN_ROWS:N_MAIN, :] = jnp.zeros((N_MAIN - WIN_ROWS, D_MODEL), BF16)
        for p in range(N_DEV):
            base = _internal_start(p) // W_TILE * W_TILE
            for r in range(0, WIN_ROWS, chunk):
                rows = slice(base + r, base + r + chunk)
                piece = g_ref[p, r:r + chunk, :]
                out_ref[rows, :] = piece if p == 0 else out_ref[rows, :] + piece
        f = g_ref[0, WIN_ROWS:WIN_ROWS + W_TILE, :]
        for p in range(1, N_DEV):
            f = f + g_ref[p, WIN_ROWS:WIN_ROWS + W_TILE, :]
        out_ref[N_MAIN:N_MAIN + W_TILE, :] = f
        out_ref[N_MAIN + W_TILE:N_ALL, :] = jnp.zeros((N_FPAD - W_TILE, D_MODEL), BF16)

    return pl.pallas_call(
        body, name="assemble_w", out_shape=jax.ShapeDtypeStruct((N_ALL, D_MODEL), BF16),
        compiler_params=_params(),
    )(windows)


def _shard_rows(windows, me):
    n = windows.shape[0]
    lo = me * IN_SHARD
    start = lo - jnp.where(lo > _O_F, HEADS, 0)
    main = lax.dynamic_slice(windows, (0, start % W_TILE, 0), (n, IN_SHARD, D_MODEL))
    first = jnp.clip(_O_F - lo, 0, IN_SHARD - W_TILE)
    near = lax.dynamic_slice(main, (0, first, 0), (n, W_TILE, D_MODEL))
    j = first - (_O_F - lo) + lax.broadcasted_iota(jnp.int32, (1, W_TILE, 1), 1)
    f_rows = jnp.roll(windows[:, WIN_ROWS:], -(first - (_O_F - lo)), axis=1)
    patch = jnp.where(jnp.logical_and(j >= 0, j < HEADS), f_rows, near)
    return lax.dynamic_update_slice(main, patch, (0, first, 0))


def kernel(x, c, w_ada, b_ada, norm_g, w_in, b_f, q_norm_g, k_norm_g, conv_w, w_attn_out, w_conv_out, w_o, loss_target, m_w_ada, m_b_ada, m_norm_g, m_w_in, m_b_f, m_q_norm_g, m_k_norm_g, m_conv_w, m_w_attn_out, m_w_conv_out, m_w_o, v_w_ada, v_b_ada, v_norm_g, v_w_in, v_b_f, v_q_norm_g, v_k_norm_g, v_conv_w, v_w_attn_out, v_w_conv_out, v_w_o):
    me = 4 * lax.axis_index("x") + 2 * lax.axis_index("y") + lax.axis_index("c")
    s = x.shape[1]
    x2, t2 = x[0], loss_target[0]

    w_in_g, c_all, ada_g = _gather_weights_and_ada(_shard_window(w_in[0].T, me), c, w_ada[0])
    ada_mine = lax.dynamic_index_in_dim(ada_g[:, :, 0, :], me, axis=1, keepdims=False)
    ada3 = (ada_mine.reshape(1, 3 * D_MODEL) + b_ada).reshape(3, D_MODEL)
    w_all_t = _assemble_w(w_in_g)
    qg = jnp.tile(q_norm_g, (1, HEADS))
    kg = jnp.tile(k_norm_g, (1, HEADS))
    bf_pad = jnp.pad(b_f, ((0, 0), (0, LANES - HEADS)))

    (proj, fl, h), (qa, ka, va, kt, vt), (cw_g, wa_g, wb_g, wo_g) = _proj_fwd(
        x2, ada3, norm_g, w_all_t, bf_pad, qg, kg,
        [conv_w[0], w_attn_out[0].astype(BF16), w_conv_out[0].astype(BF16), w_o[0].astype(BF16)])
    wo = wo_g.reshape(D_MODEL, D_MODEL)
    cw = jnp.transpose(cw_g, (1, 0, 2)).reshape(3, CONV_W)
    attn, oa, qb = _attn_fwd(qa, ka, vt, proj)
    (dy, dgab, do, dza, dob, dwo, dwa, dwb, dgate, loss_part) = _tail(oa, attn, proj, x2, t2, ada3, wa_g, wb_g, wo, cw)

    small = [dwa, dwb, dwo.reshape(N_DEV, D_MODEL // N_DEV, D_MODEL)]
    dqkv, dqg, dkg, dcum = _attn_bwd(qb, ka, kt, va, do, proj, qg, kg)
    df, dbf = _forget_bwd(dcum, fl, bf_pad)
    dconv, dcw = _conv_bwd(dob, proj, cw)
    pieces = [dqkv, dza, dconv, dgab, df]
    (dw_main, dw_f), (g_wa_parts, g_wb_parts, g_wo_parts) = _dw_in(h, pieces, small)

    pair_in = _sibling_swap_sum(dw_main, dw_f, "swap_sum_w_in")
    (grad_x, dshift, dscale, dnormg), (g_in_windows,) = _dh_and_dx(
        pieces, w_all_t, x2, dy, ada3, norm_g, [pair_in])
    g_in_parts = _shard_rows(g_in_windows, me)
    vec = jnp.concatenate([dshift, dscale, dgate, dnormg, dbf, dcw.reshape(1, 3 * CONV_W), loss_part, dqg, dkg],
                          axis=1)
    (vec_all,) = _gather_direct([vec], "gather_small")
    vec_all = vec_all.reshape(N_DEV, vec.shape[1])
    n_main = 4 * D_MODEL + LANES + 3 * CONV_W + LANES
    tot, g_qg, g_kg = _sum_small(
        vec_all[:, :n_main],
        vec_all[:, n_main:n_main + ATTN_W].reshape(N_DEV * HEADS, HEAD_DIM),
        vec_all[:, n_main + ATTN_W:].reshape(N_DEV * HEADS, HEAD_DIM))
    g_b_ada = tot[:, 0:3 * D_MODEL]
    g_norm_g = tot[:, 3 * D_MODEL:4 * D_MODEL]
    g_b_f = tot[:, 4 * D_MODEL:4 * D_MODEL + HEADS]
    g_cw_full = tot[:, 4 * D_MODEL + LANES:4 * D_MODEL + LANES + 3 * CONV_W].reshape(3, CONV_W)
    g_cw = lax.dynamic_slice(g_cw_full, (0, me * (CONV_W // N_DEV)), (3, CONV_W // N_DEV))
    dada_mine = lax.dynamic_slice(vec_all[:, 0:3 * D_MODEL], (0, me * ADA_SHARD), (N_DEV, ADA_SHARD))
    g_w_ada = _grad_w_ada(jnp.transpose(c_all, (0, 2, 1)), dada_mine.reshape(N_DEV, 1, ADA_SHARD))

    upd = {}
    upd["w_ada"] = _adamw(w_ada[0], m_w_ada[0], v_w_ada[0], g_w_ada[None], "adamw_w_ada")
    upd["w_in"] = [u.T for u in _adamw(w_in[0].T, m_w_in[0].T, v_w_in[0].T, g_in_parts, "adamw_w_in")]
    small_names = ["b_ada", "norm_g", "b_f", "q_norm_g", "k_norm_g", "conv_w", "w_attn_out", "w_conv_out", "w_o"]
    small_upd = _adamw_small(
        [(b_ada, m_b_ada, v_b_ada, g_b_ada), (norm_g, m_norm_g, v_norm_g, g_norm_g), (b_f, m_b_f, v_b_f, g_b_f),
         (q_norm_g, m_q_norm_g, v_q_norm_g, g_qg), (k_norm_g, m_k_norm_g, v_k_norm_g, g_kg),
         (conv_w[0], m_conv_w[0], v_conv_w[0], g_cw),
         (w_attn_out[0], m_w_attn_out[0], v_w_attn_out[0], g_wa_parts),
         (w_conv_out[0], m_w_conv_out[0], v_w_conv_out[0], g_wb_parts),
         (w_o[0], m_w_o[0], v_w_o[0], g_wo_parts)], "adamw_small")
    upd.update(zip(small_names, small_upd))

    names = ["w_ada", "b_ada", "norm_g", "w_in", "b_f", "q_norm_g", "k_norm_g", "conv_w",
             "w_attn_out", "w_conv_out", "w_o"]
    lead = {"w_ada", "w_in", "conv_w", "w_attn_out", "w_conv_out", "w_o"}
    fix = lambda n, a: a[None] if n in lead else a
    loss = tot[0, n_main - LANES]
    outs = [loss, grad_x[None]]
    for k in range(4):
        outs += [fix(n, upd[n][k]) for n in names]
    return tuple(outs)
```

```python
import numpy as np
import jax
import jax.numpy as jnp
from jax import lax
from jax.experimental import pallas as pl
from jax.experimental.pallas import tpu as pltpu

F32 = jnp.float32
BF16 = jnp.bfloat16

D_MODEL = 1024
HEADS = 8
HEAD_DIM = 64
ATTN_W = 512
CONV_W = 512
N_DEV = 8
IN_WIDTH = 6152
IN_SHARD = IN_WIDTH // N_DEV
N_MAIN = 6144
N_FPAD = 128
N_ALL = N_MAIN + N_FPAD
ADA_SHARD = 3 * D_MODEL // N_DEV
EPS = 1e-6
NEG = -1e30

ADAM_LR = 0.001
ADAM_B1 = 0.9
ADAM_B2 = 0.999
ADAM_EPS = 1e-08
ADAM_WD = 0.01
ADAM_STEP = 10

LANES = 128
VMEM_LIMIT = 56 * 1024 * 1024

TM_PROJ = 512
TN_PROJ = 1024
TM_ELEM = 512
TQ = 512
HEADS_PER_STEP = 8
TM_TAIL = 256
TC_CUM = 512
TK_DW = 2048
TN_DW = 512
TM_DH = 256
HALO = 16

OFF_Q, OFF_K, OFF_V, OFF_ZA, OFF_CB, OFF_CC, OFF_CU, OFF_CZ, OFF_GA, OFF_GB = (
    0, 512, 1024, 1536, 2048, 2560, 3072, 3584, 4096, 5120)


def _params(sem=None):
    return pltpu.CompilerParams(dimension_semantics=sem, vmem_limit_bytes=VMEM_LIMIT)


def _dot(a, b):
    return jnp.dot(a, b, preferred_element_type=F32)


def _dot_nt(a, b):
    return lax.dot_general(a, b, (((1,), (1,)), ((), ())), preferred_element_type=F32)


def _dot_tn(a, b):
    return lax.dot_general(a, b, (((0,), (0,)), ((), ())), preferred_element_type=F32)


def _sigmoid(x):
    return 1.0 / (1.0 + jnp.exp(-x))


def _seg_sum(z, lo):
    a = jnp.sum(jnp.where(lo, z, 0.0), axis=-1, keepdims=True)
    b = jnp.sum(jnp.where(lo, 0.0, z), axis=-1, keepdims=True)
    return jnp.where(lo, a, b)


def _lane_col(z, lane):
    idx = lax.broadcasted_iota(jnp.int32, z.shape, 1)
    return jnp.sum(jnp.where(idx == lane, z, 0.0), axis=-1, keepdims=True)


def _sub_row(z, row):
    idx = lax.broadcasted_iota(jnp.int32, z.shape, 0)
    return jnp.sum(jnp.where(idx == row, z, 0.0), axis=0, keepdims=True)


def _mesh_pos():
    x, y, c = lax.axis_index("x"), lax.axis_index("y"), lax.axis_index("c")
    return x, y, c, 4 * x + 2 * y + c


def _peer(k, x, y, c):
    px = 1 - x if (k >> 2) & 1 else x
    py = 1 - y if (k >> 1) & 1 else y
    pc = 1 - c if k & 1 else c
    return (px, py, pc), 4 * px + 2 * py + pc


def _gather_copies(ins, outs, send_sems, recv_sems, local_sems):
    x, y, c, me = _mesh_pos()
    copies = []
    for a in range(len(ins)):
        copies.append(pltpu.make_async_copy(ins[a], outs[a].at[me], local_sems.at[a]))
        for k in range(1, N_DEV):
            dev, _ = _peer(k, x, y, c)
            copies.append(pltpu.make_async_remote_copy(
                src_ref=ins[a], dst_ref=outs[a].at[me],
                send_sem=send_sems.at[a * (N_DEV - 1) + k - 1], recv_sem=recv_sems.at[a * (N_DEV - 1) + k - 1],
                device_id=dev, device_id_type=pl.DeviceIdType.MESH))
    return copies


def _gather_sems(n):
    return [pltpu.SemaphoreType.DMA((n * (N_DEV - 1),)), pltpu.SemaphoreType.DMA((n * (N_DEV - 1),)),
            pltpu.SemaphoreType.DMA((n,))]


def _gather_direct(arrs, name):
    n = len(arrs)
    any_spec = pl.BlockSpec(memory_space=pl.ANY)

    def body(*refs):
        copies = _gather_copies(refs[:n], refs[n:2 * n], *refs[2 * n:])
        for cp in copies:
            cp.start()
        for cp in copies:
            cp.wait()

    return pl.pallas_call(
        body, name=name, out_shape=[jax.ShapeDtypeStruct((N_DEV,) + a.shape, a.dtype) for a in arrs],
        in_specs=[any_spec] * n, out_specs=[any_spec] * n, scratch_shapes=_gather_sems(n),
    )(*arrs)


def _ada_phase(c_ref, w_ref, call_ref, adag_ref, mine_ref, send_sems, recv_sems):
    x, y, c, me = _mesh_pos()

    def copy(phase, k, src, dst):
        dev, _ = _peer(k, x, y, c)
        return pltpu.make_async_remote_copy(
            src_ref=src, dst_ref=dst,
            send_sem=send_sems.at[phase * (N_DEV - 1) + k - 1],
            recv_sem=recv_sems.at[phase * (N_DEV - 1) + k - 1],
            device_id=dev, device_id_type=pl.DeviceIdType.MESH)

    call_ref[me] = c_ref[...]
    first = [copy(0, k, c_ref, call_ref.at[me]) for k in range(1, N_DEV)]
    for cp in first:
        cp.start()
    for cp in first:
        cp.wait()
    wb = w_ref[...].astype(BF16)
    for b in range(N_DEV):
        row = jnp.broadcast_to(call_ref[b], (8, D_MODEL)).astype(BF16)
        mine_ref[b] = _sub_row(_dot(row, wb), 0)
    adag_ref[me] = mine_ref[...]
    second = [copy(1, k, mine_ref, adag_ref.at[me]) for k in range(1, N_DEV)]
    for cp in second:
        cp.start()
    for cp in second:
        cp.wait()


def _gather_weights_and_ada(wt_shard, c_row, w_ada_sh):
    any_spec = pl.BlockSpec(memory_space=pl.ANY)
    vm = pl.BlockSpec(memory_space=pltpu.VMEM)

    def body(w_in_ref, c_ref, wada_ref, out_ref, call_ref, adag_ref, mine_ref, send_sems, recv_sems, local_sem,
             ada_send, ada_recv):
        x, y, c, me = _mesh_pos()
        sibling = (x, y, 1 - c)
        chips = [(1 - x, y), (x, 1 - y), (1 - x, 1 - y)]

        def copy(k, src, blk, to):
            return pltpu.make_async_remote_copy(
                src_ref=src, dst_ref=out_ref.at[blk], send_sem=send_sems.at[k], recv_sem=recv_sems.at[k],
                device_id=to, device_id_type=pl.DeviceIdType.MESH)

        local = pltpu.make_async_copy(w_in_ref, out_ref.at[me], local_sem.at[0])
        local.start()
        first = [copy(0, w_in_ref, me, sibling)]
        first += [copy(1 + j, w_in_ref, me, (px, py, c)) for j, (px, py) in enumerate(chips)]
        for cp in first:
            cp.start()
        _ada_phase(c_ref, wada_ref, call_ref, adag_ref, mine_ref, ada_send, ada_recv)
        passed = []
        for j, (px, py) in enumerate(chips):
            blk = 4 * px + 2 * py + c
            copy(1 + j, w_in_ref, blk, (x, y, c)).wait_recv()
            fwd = copy(4 + j, out_ref.at[blk], blk, sibling)
            fwd.start()
            passed.append(fwd)
        copy(0, w_in_ref, 4 * x + 2 * y + 1 - c, (x, y, c)).wait_recv()
        for j, (px, py) in enumerate(chips):
            copy(4 + j, w_in_ref, 4 * px + 2 * py + 1 - c, (x, y, c)).wait_recv()
        for cp in first + passed:
            cp.wait_send()
        local.wait()

    per = N_DEV - 1
    return pl.pallas_call(
        body, name="gather_weights",
        out_shape=[jax.ShapeDtypeStruct((N_DEV,) + wt_shard.shape, wt_shard.dtype),
                   jax.ShapeDtypeStruct((N_DEV, 1, D_MODEL), F32),
                   jax.ShapeDtypeStruct((N_DEV, N_DEV, 1, ADA_SHARD), F32)],
        in_specs=[any_spec, vm, vm], out_specs=[any_spec, vm, vm],
        scratch_shapes=[pltpu.VMEM((N_DEV, 1, ADA_SHARD), F32),
                        pltpu.SemaphoreType.DMA((per,)), pltpu.SemaphoreType.DMA((per,)),
                        pltpu.SemaphoreType.DMA((1,)),
                        pltpu.SemaphoreType.DMA((2 * per,)), pltpu.SemaphoreType.DMA((2 * per,))],
        compiler_params=pltpu.CompilerParams(vmem_limit_bytes=VMEM_LIMIT),
    )(wt_shard, c_row, w_ada_sh)


def _sibling_swap_sum(dwt, dwt_f, name):
    nch, rows = N_DEV // 2, WIN_ROWS + W_TILE
    any_spec = pl.BlockSpec(memory_space=pl.ANY)

    def body(main_ref, f_ref, out_ref, mine_v, theirs_v, sum_v, send_sems, recv_sems, load_sems, store_sems):
        x, y, c, _ = _mesh_pos()

        def parts(core, ch, buf):
            b0, b1 = (_internal_start(2 * ch + k) // W_TILE * W_TILE for k in range(2))
            base = pl.multiple_of(jnp.where(core == 0, b0, b1), W_TILE)
            return [(main_ref.at[pl.ds(base, WIN_ROWS)], buf.at[ch, pl.ds(0, WIN_ROWS)]),
                    (f_ref.at[pl.ds(0, W_TILE)], buf.at[ch, pl.ds(WIN_ROWS, W_TILE)])]

        swaps, loads, stores = [], [], []
        for ch in range(nch):
            for k, (src, dst) in enumerate(parts(1 - c, ch, theirs_v)):
                swaps.append(pltpu.make_async_remote_copy(
                    src_ref=src, dst_ref=dst, send_sem=send_sems.at[2 * ch + k], recv_sem=recv_sems.at[2 * ch + k],
                    device_id=(x, y, 1 - c), device_id_type=pl.DeviceIdType.MESH))
            for k, (src, dst) in enumerate(parts(c, ch, mine_v)):
                loads.append(pltpu.make_async_copy(src, dst, load_sems.at[2 * ch + k]))
            stores.append(pltpu.make_async_copy(sum_v.at[ch], out_ref.at[ch], store_sems.at[ch]))
        for cp in swaps + loads:
            cp.start()
        for ch in range(nch):
            for cp in loads[2 * ch:2 * ch + 2] + swaps[2 * ch:2 * ch + 2]:
                cp.wait()
            sum_v[ch] = (mine_v[ch].astype(F32) + theirs_v[ch].astype(F32)).astype(BF16)
            stores[ch].start()
        for cp in stores:
            cp.wait()

    return pl.pallas_call(
        body, name=name,
        out_shape=jax.ShapeDtypeStruct((nch, rows, D_MODEL), BF16),
        in_specs=[any_spec, any_spec], out_specs=any_spec,
        scratch_shapes=[pltpu.VMEM((nch, rows, D_MODEL), BF16)] * 3
        + [pltpu.SemaphoreType.DMA((2 * nch,))] * 3 + [pltpu.SemaphoreType.DMA((nch,))],
        compiler_params=_params(),
    )(dwt, dwt_f)


def _all_to_all_copies(ins, outs, send_sems, recv_sems, local_sems):
    x, y, c, me = _mesh_pos()
    copies = []
    for a in range(len(ins)):
        copies.append(pltpu.make_async_copy(ins[a].at[me], outs[a].at[me], local_sems.at[a]))
        for k in range(1, N_DEV):
            dev, p = _peer(k, x, y, c)
            copies.append(pltpu.make_async_remote_copy(
                src_ref=ins[a].at[p], dst_ref=outs[a].at[me],
                send_sem=send_sems.at[a * (N_DEV - 1) + k - 1], recv_sem=recv_sems.at[a * (N_DEV - 1) + k - 1],
                device_id=dev, device_id_type=pl.DeviceIdType.MESH))
    return copies


def _chip_copies(ins, outs, send_sems, recv_sems, local_sems):
    x, y, c, _ = _mesh_pos()
    my_chip = 2 * x + y
    chips = [(1 - x, y), (x, 1 - y), (1 - x, 1 - y)]
    copies = []
    for a in range(len(ins)):
        copies.append(pltpu.make_async_copy(ins[a].at[my_chip], outs[a].at[my_chip], local_sems.at[a]))
        for j, (px, py) in enumerate(chips):
            copies.append(pltpu.make_async_remote_copy(
                src_ref=ins[a].at[2 * px + py], dst_ref=outs[a].at[my_chip],
                send_sem=send_sems.at[a * 3 + j], recv_sem=recv_sems.at[a * 3 + j],
                device_id=(px, py, c), device_id_type=pl.DeviceIdType.MESH))
    return copies


def _proj_fwd(x, ada3, norm_g, w_all_t, bf_pad, qg, kg, later):
    s = x.shape[0]
    tm, tn = min(TM_PROJ, s), TN_PROJ
    nt = s // tm
    n = len(later)

    def body(x_ref, ada_ref, g_ref, wt_ref, bf_ref, qg_ref, kg_ref, *rest):
        ins, (proj_ref, fl_ref, h_ref), rows_refs, rest = rest[:n], rest[n:n + 3], rest[n + 3:n + 8], rest[n + 8:]
        outs, (carry, send_sems, recv_sems, local_sems) = rest[:n], rest[n:]
        i = pl.program_id(0)

        @pl.when(i == 0)
        def _():
            carry[...] = jnp.zeros_like(carry)
            for cp in _gather_copies(ins, outs, send_sems, recv_sems, local_sems):
                cp.start()

        xv = x_ref[...]
        r = lax.rsqrt(jnp.mean(xv * xv, axis=-1, keepdims=True) + EPS)
        hv = ((xv * r) * g_ref[...]) * (1.0 + ada_ref[1:2, :]) + ada_ref[0:1, :]
        hb = hv.astype(BF16)
        h_ref[...] = hb
        fl = _dot_nt(hb, wt_ref[N_MAIN:N_ALL, :])
        fl_ref[...] = fl
        for j in range(N_MAIN // tn):
            proj_ref[:, j * tn:(j + 1) * tn] = _dot_nt(hb, wt_ref[j * tn:(j + 1) * tn, :]).astype(BF16)
        _attention_rows(proj_ref, fl, bf_ref, qg_ref, kg_ref, carry, *rows_refs)

        @pl.when(i == nt - 1)
        def _():
            for cp in _gather_copies(ins, outs, send_sems, recv_sems, local_sems):
                cp.wait()

    any_spec = pl.BlockSpec(memory_space=pl.ANY)
    heads = pl.BlockSpec((HEADS, tm, LANES), lambda i: (0, i, 0))
    heads_t = pl.BlockSpec((HEADS, LANES, tm), lambda i: (0, 0, i))
    vec = pl.BlockSpec((1, ATTN_W), lambda i: (0, 0))
    res = pl.pallas_call(
        body, name="proj_fwd", grid=(nt,),
        in_specs=[pl.BlockSpec((tm, D_MODEL), lambda i: (i, 0)),
                  pl.BlockSpec((3, D_MODEL), lambda i: (0, 0)),
                  pl.BlockSpec((1, D_MODEL), lambda i: (0, 0)),
                  pl.BlockSpec((N_ALL, D_MODEL), lambda i: (0, 0)),
                  pl.BlockSpec((1, LANES), lambda i: (0, 0)), vec, vec] + [any_spec] * n,
        out_specs=[pl.BlockSpec((tm, N_MAIN), lambda i: (i, 0)),
                   pl.BlockSpec((tm, N_FPAD), lambda i: (i, 0)),
                   pl.BlockSpec((tm, D_MODEL), lambda i: (i, 0)),
                   heads, heads, heads, heads_t, heads_t] + [any_spec] * n,
        out_shape=[jax.ShapeDtypeStruct((s, N_MAIN), BF16),
                   jax.ShapeDtypeStruct((s, N_FPAD), F32),
                   jax.ShapeDtypeStruct((s, D_MODEL), BF16)]
        + [jax.ShapeDtypeStruct((HEADS, s, LANES), BF16)] * 3
        + [jax.ShapeDtypeStruct((HEADS, LANES, s), BF16)] * 2
        + [jax.ShapeDtypeStruct((N_DEV,) + a.shape, a.dtype) for a in later],
        scratch_shapes=[pltpu.VMEM((1, LANES), F32)] + _gather_sems(n),
        compiler_params=_params(("arbitrary",)),
    )(x, ada3, norm_g, w_all_t, bf_pad, qg, kg, *later)
    return res[:3], res[3:8], res[8:]


L_ONE_Q, L_F_Q, L_LSE_Q, L_END = HEAD_DIM, HEAD_DIM + 3, HEAD_DIM + 6, HEAD_DIM + 9


def _split3(f):
    hi = f.astype(BF16).astype(F32)
    r = f - hi
    mid = r.astype(BF16).astype(F32)
    return hi, mid, r - mid


def _place3(lane, first, parts, otherwise):
    a, b, c = parts
    return jnp.where(lane == first, a, jnp.where(lane == first + 1, b, jnp.where(lane == first + 2, c, otherwise)))


def _log_forget(fl, bf):
    z = fl + bf
    lf = jnp.minimum(z, 0.0) - jnp.log1p(jnp.exp(-jnp.abs(z)))
    lane = lax.broadcasted_iota(jnp.int32, z.shape, 1)
    return jnp.where(lane < HEADS, lf, 0.0)


def _attention_rows(p_ref, fl, bf_ref, qg_ref, kg_ref, carry, qa_ref, ka_ref, va_ref, kt_ref, vt_ref):
    tm = fl.shape[0]
    scale = HEAD_DIM ** -0.5
    tri = (lax.broadcasted_iota(jnp.int32, (tm, tm), 1) <= lax.broadcasted_iota(jnp.int32, (tm, tm), 0)).astype(F32)
    cum_v = jnp.dot(tri, _log_forget(fl, bf_ref[...]), preferred_element_type=F32,
                    precision=lax.Precision.HIGHEST) + carry[...]
    carry[...] = _sub_row(cum_v, tm - 1)
    lane = lax.broadcasted_iota(jnp.int32, (tm, LANES), 1)
    lo = lane < HEAD_DIM
    v_tail = jnp.where(lane < L_F_Q, 1.0, 0.0)
    for pr in range(ATTN_W // LANES):
        sl = slice(pr * LANES, (pr + 1) * LANES)
        q2 = p_ref[:, OFF_Q + pr * LANES:OFF_Q + (pr + 1) * LANES].astype(F32)
        k2 = p_ref[:, OFF_K + pr * LANES:OFF_K + (pr + 1) * LANES].astype(F32)
        v2 = p_ref[:, OFF_V + pr * LANES:OFF_V + (pr + 1) * LANES].astype(F32)
        rq = lax.rsqrt(_seg_sum(q2 * q2, lo) * (1.0 / HEAD_DIM) + EPS)
        rk = lax.rsqrt(_seg_sum(k2 * k2, lo) * (1.0 / HEAD_DIM) + EPS)
        qn = ((q2 * rq) * qg_ref[:, sl]) * scale
        kn = (k2 * rk) * kg_ref[:, sl]
        for hh in range(2):
            h = 2 * pr + hh
            f3 = _split3(_lane_col(cum_v, h))
            qh = qn if hh == 0 else pltpu.roll(qn, HEAD_DIM, 1)
            kh = kn if hh == 0 else pltpu.roll(kn, HEAD_DIM, 1)
            vh = v2 if hh == 0 else pltpu.roll(v2, HEAD_DIM, 1)
            q_tail = jnp.where(lane < L_F_Q, 1.0, _place3(lane, L_F_Q, f3, 0.0))
            k_tail = _place3(lane, L_ONE_Q, tuple(-f for f in f3), jnp.where(lane < L_END, 1.0, 0.0))
            k_row = jnp.where(lo, kh, k_tail)
            v_row = jnp.where(lo, vh, v_tail)
            qa_ref[h] = jnp.where(lo, qh, q_tail).astype(BF16)
            ka_ref[h] = k_row.astype(BF16)
            va_ref[h] = v_row.astype(BF16)
            kt_ref[h] = k_row.T.astype(BF16)
            vt_ref[h] = v_row.T.astype(BF16)


def _causal_t(t):
    return lax.broadcasted_iota(jnp.int32, (t, t), 0) <= lax.broadcasted_iota(jnp.int32, (t, t), 1)


def _tri_steps(nt, q_major):
    if q_major:
        pairs = [(i, j) for i in range(nt) for j in range(i + 1)]
    else:
        pairs = [(i, j) for j in range(nt) for i in range(j, nt)]
    return (jnp.asarray(np.array([p[0] for p in pairs], np.int32)),
            jnp.asarray(np.array([p[1] for p in pairs], np.int32)))


def _attn_fwd(qa, ka, vt, proj):
    s = qa.shape[1]
    t = min(TQ, s)
    it, jt = _tri_steps(s // t, True)
    hp = HEADS_PER_STEP
    wide = hp * HEAD_DIM
    za_blk = OFF_ZA // wide

    def body(it_ref, jt_ref, q_ref, k_ref, vt_ref, za_ref, attn_ref, oa_ref, qb_ref, m_s, acc_s, pair_s):
        step = pl.program_id(1)
        i, j = it_ref[step], jt_ref[step]

        @pl.when(j == 0)
        def _():
            m_s[...] = jnp.full_like(m_s, NEG)
            acc_s[...] = jnp.zeros_like(acc_s)

        def update(masked):
            for hh in range(hp):
                st = _dot_nt(k_ref[hh], q_ref[hh])
                if masked:
                    st = jnp.where(_causal_t(t), st, NEG)
                m_prev = m_s[hh]
                m_next = jnp.maximum(m_prev, jnp.max(st, axis=0, keepdims=True))
                alpha = jnp.exp(m_prev - m_next)
                pt = jnp.exp(st - m_next).astype(BF16)
                acc_s[hh] = acc_s[hh] * alpha + _dot(vt_ref[hh], pt)
                m_s[hh] = m_next

        @pl.when(j < i)
        def _():
            update(False)

        @pl.when(j == i)
        def _():
            update(True)
            row = lax.broadcasted_iota(jnp.int32, (LANES, t), 0)
            lane = lax.broadcasted_iota(jnp.int32, (t, LANES), 1)
            for hh in range(hp):
                l_row = acc_s[hh, L_ONE_Q:L_ONE_Q + 1, :]
                pair_s[hh * HEAD_DIM:(hh + 1) * HEAD_DIM, :] = acc_s[hh, 0:HEAD_DIM, :] / l_row
                lse3 = _split3(m_s[hh] + jnp.log(l_row))
                tail_t = _place3(row, L_LSE_Q, tuple(-x for x in lse3), 0.0)
                keep_q = jnp.logical_or(lane < L_LSE_Q, lane >= L_END)
                qb_ref[hh] = jnp.where(keep_q, q_ref[hh].astype(F32), tail_t.T).astype(BF16)
            out = pair_s[...].T
            attn_ref[...] = out
            z = za_ref[...].astype(F32)
            oa_ref[...] = (out * (z * _sigmoid(z))).astype(BF16)

    pair_q = pl.BlockSpec((hp, t, LANES), lambda p, n, it_, jt_: (p, it_[n], 0))
    pair_k = pl.BlockSpec((hp, t, LANES), lambda p, n, it_, jt_: (p, jt_[n], 0))
    pair_kt = pl.BlockSpec((hp, LANES, t), lambda p, n, it_, jt_: (p, 0, jt_[n]))
    out_q = pl.BlockSpec((t, wide), lambda p, n, it_, jt_: (it_[n], p))
    return pl.pallas_call(
        body, name="attn_fwd",
        grid_spec=pltpu.PrefetchScalarGridSpec(
            num_scalar_prefetch=2, grid=(HEADS // hp, it.shape[0]),
            in_specs=[pair_q, pair_k, pair_kt,
                      pl.BlockSpec((t, wide), lambda p, n, it_, jt_: (it_[n], za_blk + p))],
            out_specs=[out_q, out_q, pair_q],
            scratch_shapes=[pltpu.VMEM((hp, 1, t), F32), pltpu.VMEM((hp, LANES, t), F32),
                            pltpu.VMEM((wide, t), F32)]),
        out_shape=[jax.ShapeDtypeStruct((s, ATTN_W), F32),
                   jax.ShapeDtypeStruct((s, ATTN_W), BF16),
                   jax.ShapeDtypeStruct((HEADS, s, LANES), BF16)],
        compiler_params=_params(("parallel", "arbitrary")),
    )(it, jt, qa, ka, vt, proj)


def _conv_parts(gb_ref, gc_ref, u_ref, zb_ref, gch_ref, uh_ref, first, w_ref, tm):
    gb, gc = gb_ref[...].astype(F32), gc_ref[...].astype(F32)
    u, zb = u_ref[...].astype(F32), zb_ref[...].astype(F32)
    cu = gc * u
    cu_h = jnp.where(first, 0.0, gch_ref[...].astype(F32) * uh_ref[...].astype(F32))
    prev1, prev2 = _sub_row(cu_h, HALO - 1), _sub_row(cu_h, HALO - 2)
    row = lax.broadcasted_iota(jnp.int32, cu.shape, 0)
    r1 = jnp.where(row == 0, prev1, pltpu.roll(cu, 1, 0))
    r2 = jnp.where(row == 0, prev2, jnp.where(row == 1, prev1, pltpu.roll(cu, 2, 0)))
    conv = w_ref[2:3, :] * cu + w_ref[1:2, :] * r1 + w_ref[0:1, :] * r2
    return gb, gc, u, zb, cu, r1, r2, conv


def _conv_specs(tm, s, width=LANES):
    def tile(off):
        return pl.BlockSpec((tm, width), lambda c, i: (i, off // width + c))

    def before(off):
        return pl.BlockSpec((HALO, width), lambda c, i: (jnp.maximum(i * (tm // HALO) - 1, 0), off // width + c))

    def after(off):
        return pl.BlockSpec((HALO, width),
                            lambda c, i: (jnp.minimum((i + 1) * (tm // HALO), s // HALO - 1), off // width + c))

    return ([tile(OFF_CB), tile(OFF_CC), tile(OFF_CU), tile(OFF_CZ)], [before(OFF_CC), before(OFF_CU)],
            [after(OFF_CB), after(OFF_CZ)])


def _tail(oa, attn, proj, x, target, ada3, wa, wb, wo, conv_w):
    s = x.shape[0]
    tm = min(TM_TAIL, s)
    gab_blk = OFF_GA // (2 * D_MODEL)
    za_blk = OFF_ZA // ATTN_W
    tiles, befores, _ = _conv_specs(tm, s, CONV_W)

    def body(oa_ref, attn_ref, za_ref, gb_ref, gc_ref, u_ref, zb_ref, gch_ref, uh_ref, cw_ref, gab_ref, x_ref, t_ref,
             ada_ref, wag_ref, wbg_ref, wo_ref,
             dy_ref, dgab_ref, do_ref, dza_ref, dob_ref, dwo_out, dwa_out, dwb_out, dgate_ref, loss_ref,
             dwo_ref, dwa_ref, dwb_ref, wa_ref, wb_ref):
        first = pl.program_id(0) == 0

        @pl.when(first)
        def _():
            for p in range(N_DEV):
                wa_ref[:, p * LANES:(p + 1) * LANES] = wag_ref[p]
                wb_ref[:, p * LANES:(p + 1) * LANES] = wbg_ref[p]
            dwo_ref[...] = jnp.zeros_like(dwo_ref)
            dwa_ref[...] = jnp.zeros_like(dwa_ref)
            dwb_ref[...] = jnp.zeros_like(dwb_ref)
            dgate_ref[...] = jnp.zeros_like(dgate_ref)
            loss_ref[...] = jnp.zeros_like(loss_ref)

        gb, _, _, zb, _, _, _, conv = _conv_parts(gb_ref, gc_ref, u_ref, zb_ref, gch_ref, uh_ref, first, cw_ref, tm)
        ob_v = (gb * conv * (zb * _sigmoid(zb))).astype(BF16)
        oa_v = oa_ref[...]
        wa_v, wb_v, wo_v = wa_ref[...], wb_ref[...], wo_ref[...]
        a2 = _dot(oa_v, wa_v)
        b2 = _dot(ob_v, wb_v)
        sa = _sigmoid(gab_ref[:, 0:D_MODEL].astype(F32))
        sb = _sigmoid(gab_ref[:, D_MODEL:2 * D_MODEL].astype(F32))
        mb = (sa * a2 + sb * b2).astype(BF16)
        mo = _dot(mb, wo_v)
        gate = ada_ref[2:3, :]
        err = (x_ref[...] + gate * mo) - t_ref[...]
        dy = err * (1.0 / D_MODEL)
        dy_ref[...] = dy
        loss_ref[...] += 0.5 * jnp.sum(err * err) * (1.0 / D_MODEL)
        dgate_ref[...] += jnp.sum(dy * mo, axis=0, keepdims=True)
        dmo = (dy * gate).astype(BF16)
        dmerged = _dot_nt(dmo, wo_v)
        dwo_ref[...] += _dot_tn(mb, dmo)
        da2 = (dmerged * sa).astype(BF16)
        db2 = (dmerged * sb).astype(BF16)
        dgab_ref[:, 0:D_MODEL] = (dmerged * a2 * (sa * (1.0 - sa))).astype(BF16)
        dgab_ref[:, D_MODEL:2 * D_MODEL] = (dmerged * b2 * (sb * (1.0 - sb))).astype(BF16)
        doa = _dot_nt(da2, wa_v)
        dob_ref[...] = _dot_nt(db2, wb_v)
        dwa_ref[...] += _dot_tn(oa_v, da2)
        dwb_ref[...] += _dot_tn(ob_v, db2)

        lane = lax.broadcasted_iota(jnp.int32, (tm, LANES), 1)
        lo = lane < HEAD_DIM
        for pr in range(ATTN_W // LANES):
            sl = slice(pr * LANES, (pr + 1) * LANES)
            g, a, z = doa[:, sl], attn_ref[:, sl], za_ref[:, sl].astype(F32)
            sg = _sigmoid(z)
            dat = (g * (z * sg)).astype(BF16).astype(F32)
            prod = dat * a
            dza_ref[:, sl] = (g * a * (sg * (1.0 + z * (1.0 - sg)))).astype(BF16)
            for hh in range(2):
                sel = lo if hh == 0 else jnp.logical_not(lo)
                delta3 = _split3(jnp.sum(jnp.where(sel, prod, 0.0), axis=-1, keepdims=True))
                dh = dat if hh == 0 else pltpu.roll(dat, HEAD_DIM, 1)
                tail_lanes = _place3(lane, L_ONE_Q, tuple(-d for d in delta3), 0.0)
                do_ref[2 * pr + hh] = jnp.where(lo, dh, tail_lanes).astype(BF16)

        @pl.when(pl.program_id(0) == pl.num_programs(0) - 1)
        def _():
            dwo_out[...] = dwo_ref[...].astype(BF16)
            for p in range(N_DEV):
                dwa_out[p] = dwa_ref[:, p * LANES:(p + 1) * LANES].astype(BF16)
                dwb_out[p] = dwb_ref[:, p * LANES:(p + 1) * LANES].astype(BF16)

    half = pl.BlockSpec((tm, ATTN_W), lambda i: (i, 0))
    full = pl.BlockSpec((tm, D_MODEL), lambda i: (i, 0))

    def const(shape):
        return pl.BlockSpec(shape, lambda i: (0, 0))

    slabs = pl.BlockSpec((N_DEV, ATTN_W, LANES), lambda i: (0, 0, 0))

    def one_axis(spec):
        return pl.BlockSpec(spec.block_shape, lambda i, f=spec.index_map: f(0, i))

    return pl.pallas_call(
        body, name="tail", grid=(s // tm,),
        in_specs=[half, half, pl.BlockSpec((tm, ATTN_W), lambda i: (i, za_blk))]
        + [one_axis(sp) for sp in tiles + befores]
        + [const((3, CONV_W)), pl.BlockSpec((tm, 2 * D_MODEL), lambda i: (i, gab_blk)), full, full,
           const((3, D_MODEL)), slabs, slabs, const((D_MODEL, D_MODEL))],
        out_specs=[full, pl.BlockSpec((tm, 2 * D_MODEL), lambda i: (i, 0)),
                   pl.BlockSpec((HEADS, tm, LANES), lambda i: (0, i, 0)), half, half,
                   const((D_MODEL, D_MODEL)), slabs, slabs, const((1, D_MODEL)), const((1, LANES))],
        out_shape=[jax.ShapeDtypeStruct((s, D_MODEL), F32),
                   jax.ShapeDtypeStruct((s, 2 * D_MODEL), BF16),
                   jax.ShapeDtypeStruct((HEADS, s, LANES), BF16),
                   jax.ShapeDtypeStruct((s, ATTN_W), BF16),
                   jax.ShapeDtypeStruct((s, CONV_W), F32),
                   jax.ShapeDtypeStruct((D_MODEL, D_MODEL), BF16),
                   jax.ShapeDtypeStruct((N_DEV, ATTN_W, LANES), BF16),
                   jax.ShapeDtypeStruct((N_DEV, CONV_W, LANES), BF16),
                   jax.ShapeDtypeStruct((1, D_MODEL), F32),
                   jax.ShapeDtypeStruct((1, LANES), F32)],
        scratch_shapes=[pltpu.VMEM((D_MODEL, D_MODEL), F32), pltpu.VMEM((ATTN_W, D_MODEL), F32),
                        pltpu.VMEM((CONV_W, D_MODEL), F32), pltpu.VMEM((ATTN_W, D_MODEL), BF16),
                        pltpu.VMEM((CONV_W, D_MODEL), BF16)],
        compiler_params=_params(("arbitrary",)),
    )(oa, attn, proj, *([proj] * 6), conv_w, proj, x, target, ada3, wa, wb, wo)


def _attn_bwd(qb, ka, kt, va, do, proj, qg, kg):
    s = qb.shape[1]
    t = min(TQ, s)
    nt = s // t
    hp = HEADS_PER_STEP
    assert hp == HEADS, "all heads share one (S, 1536) output block"
    wide = hp * HEAD_DIM
    scale = HEAD_DIM ** -0.5
    it, jt = _tri_steps(nt, False)

    def body(it_ref, jt_ref, q_ref, k_ref, kt_ref, v_ref, do_ref, qraw_ref, kraw_ref, qg_ref, kg_ref,
             dqkv_ref, dqg_ref, dkg_ref, dcum_ref, dqt_s, dk_s, dv_s, rows_s):
        grp, step = pl.program_id(0), pl.program_id(1)
        i, j = it_ref[step], jt_ref[step]
        lane = lax.broadcasted_iota(jnp.int32, (t, LANES), 1)
        lo = lane < HEAD_DIM

        @pl.when(step == 0)
        def _():
            dqt_s[...] = jnp.zeros_like(dqt_s)
            dqg_ref[...] = jnp.zeros_like(dqg_ref)
            dkg_ref[...] = jnp.zeros_like(dkg_ref)

        @pl.when(i == j)
        def _():
            dk_s[...] = jnp.zeros_like(dk_s)
            dv_s[...] = jnp.zeros_like(dv_s)

        def update(masked):
            for hh in range(hp):
                qh, doh = q_ref[hh], do_ref[hh]
                st = _dot_nt(k_ref[hh], qh)
                if masked:
                    st = jnp.where(_causal_t(t), st, NEG)
                pt = jnp.exp(st)
                dst = (pt * _dot_nt(v_ref[hh], doh)).astype(BF16)
                dv_s[hh] += _dot(pt.astype(BF16), doh)
                dk_s[hh] += _dot(dst, qh)
                dqt_s[hh, i] += _dot(kt_ref[hh], dst)

        def pair(a, b):
            return jnp.where(lo, a, pltpu.roll(b, HEAD_DIM, 1))

        def norm_bwd(raw, dy, g, dg_ref, off, sl):
            r = lax.rsqrt(_seg_sum(raw * raw, lo) * (1.0 / HEAD_DIM) + EPS)
            xhat = raw * r
            dg_ref[:, sl] += jnp.sum(dy * xhat, axis=0, keepdims=True)
            dxh = dy * g
            dx = r * (dxh - xhat * (_seg_sum(dxh * xhat, lo) * (1.0 / HEAD_DIM)))
            dqkv_ref[:, off + sl.start:off + sl.stop] = dx.astype(BF16)

        @pl.when(i > j)
        def _():
            update(False)

        @pl.when(i == j)
        def _():
            update(True)
            dq_rows = [dqt_s[hh, i].T for hh in range(hp)]
            rows = jnp.zeros((t, LANES), F32)
            for hh in range(hp):
                rows = jnp.where(lane == grp * hp + hh, _lane_col(dq_rows[hh], L_F_Q), rows)
            rows_s[...] = rows
            for pr in range(hp // 2):
                sl = slice(pr * LANES, (pr + 1) * LANES)
                norm_bwd(qraw_ref[:, sl].astype(F32), pair(dq_rows[2 * pr], dq_rows[2 * pr + 1]) * scale,
                         qg_ref[:, sl], dqg_ref, OFF_Q, sl)

        @pl.when(i == nt - 1)
        def _():
            dcum = rows_s[...]
            for hh in range(hp):
                dcum = jnp.where(lane == grp * hp + hh, dcum - _lane_col(dk_s[hh], L_ONE_Q), dcum)
            dcum_ref[0] = dcum
            for pr in range(hp // 2):
                sl = slice(pr * LANES, (pr + 1) * LANES)
                norm_bwd(kraw_ref[:, sl].astype(F32), pair(dk_s[2 * pr], dk_s[2 * pr + 1]),
                         kg_ref[:, sl], dkg_ref, OFF_K, sl)
                dqkv_ref[:, OFF_V + sl.start:OFF_V + sl.stop] = pair(dv_s[2 * pr], dv_s[2 * pr + 1]).astype(BF16)

    pair_q = pl.BlockSpec((hp, t, LANES), lambda p, n, it_, jt_: (p, it_[n], 0))
    pair_k = pl.BlockSpec((hp, t, LANES), lambda p, n, it_, jt_: (p, jt_[n], 0))
    pair_kt = pl.BlockSpec((hp, LANES, t), lambda p, n, it_, jt_: (p, 0, jt_[n]))
    tok3 = pl.BlockSpec((t, 3 * ATTN_W), lambda p, n, it_, jt_: (jt_[n], 0))
    gain = pl.BlockSpec((1, wide), lambda p, n, it_, jt_: (0, p))
    return pl.pallas_call(
        body, name="attn_bwd",
        grid_spec=pltpu.PrefetchScalarGridSpec(
            num_scalar_prefetch=2, grid=(HEADS // hp, it.shape[0]),
            in_specs=[pair_q, pair_k, pair_kt, pair_k, pair_q,
                      pl.BlockSpec((t, wide), lambda p, n, it_, jt_: (jt_[n], OFF_Q // wide + p)),
                      pl.BlockSpec((t, wide), lambda p, n, it_, jt_: (jt_[n], OFF_K // wide + p)), gain, gain],
            out_specs=[tok3, gain, gain,
                       pl.BlockSpec((1, t, LANES), lambda p, n, it_, jt_: (p, jt_[n], 0))],
            scratch_shapes=[pltpu.VMEM((hp, nt, LANES, t), F32), pltpu.VMEM((hp, t, LANES), F32),
                            pltpu.VMEM((hp, t, LANES), F32), pltpu.VMEM((t, LANES), F32)]),
        out_shape=[jax.ShapeDtypeStruct((s, 3 * ATTN_W), BF16)]
        + [jax.ShapeDtypeStruct((1, ATTN_W), F32)] * 2
        + [jax.ShapeDtypeStruct((HEADS // hp, s, LANES), F32)],
        compiler_params=_params(("parallel", "arbitrary")),
    )(it, jt, qb, ka, kt, va, do, proj, proj, qg, kg)


def _forget_bwd(dcum, fl, bf_pad):
    s = fl.shape[0]
    tc = min(TC_CUM, s)
    n = s // tc

    def body(dc_ref, fl_ref, bf_ref, df_ref, dbf_ref, carry):
        @pl.when(pl.program_id(0) == 0)
        def _():
            carry[...] = jnp.zeros_like(carry)
            dbf_ref[...] = jnp.zeros_like(dbf_ref)
        r = lax.broadcasted_iota(jnp.int32, (tc, tc), 0)
        cidx = lax.broadcasted_iota(jnp.int32, (tc, tc), 1)
        tri = (cidx >= r).astype(F32)
        dc = dc_ref[0]
        for grp in range(1, dcum.shape[0]):
            dc = dc + dc_ref[grp]
        dlf = jnp.dot(tri, dc, preferred_element_type=F32, precision=lax.Precision.HIGHEST) + carry[...]
        carry[...] += jnp.sum(dc, axis=0, keepdims=True)
        lane = lax.broadcasted_iota(jnp.int32, (tc, LANES), 1)
        dfl = jnp.where(lane < HEADS, dlf * _sigmoid(-(fl_ref[...] + bf_ref[...])), 0.0)
        df_ref[...] = dfl.astype(BF16)
        dbf_ref[...] += jnp.sum(dfl, axis=0, keepdims=True)

    rev = pl.BlockSpec((tc, LANES), lambda i: (n - 1 - i, 0))
    vec = pl.BlockSpec((1, LANES), lambda i: (0, 0))
    return pl.pallas_call(
        body, name="forget_bwd", grid=(n,),
        in_specs=[pl.BlockSpec((dcum.shape[0], tc, LANES), lambda i: (0, n - 1 - i, 0)), rev, vec],
        out_specs=[rev, vec],
        out_shape=[jax.ShapeDtypeStruct((s, LANES), BF16), jax.ShapeDtypeStruct((1, LANES), F32)],
        scratch_shapes=[pltpu.VMEM((1, LANES), F32)],
        compiler_params=_params(("arbitrary",)),
    )(dcum, fl, bf_pad)


def _conv_bwd(dob, proj, conv_w):
    s = dob.shape[0]
    tm = min(TM_ELEM, s)
    wd = CONV_W
    tiles, befores, afters = _conv_specs(tm, s, wd)

    def body(dob_ref, dnext_ref, gb_ref, gc_ref, u_ref, zb_ref, gch_ref, uh_ref, gbn_ref, zbn_ref, w_ref,
             out_ref, dw_ref):
        i = pl.program_id(1)

        @pl.when(i == 0)
        def _():
            dw_ref[...] = jnp.zeros_like(dw_ref)
        gb, gc, u, zb, cu, r1, r2, conv = _conv_parts(gb_ref, gc_ref, u_ref, zb_ref, gch_ref, uh_ref, i == 0, w_ref, tm)
        g = dob_ref[...]
        sg = _sigmoid(zb)
        sz = zb * sg
        dconv = g * gb * sz
        zn = zbn_ref[0:8, :].astype(F32)
        dcn = jnp.where(i == pl.num_programs(1) - 1, 0.0,
                        dnext_ref[...] * gbn_ref[0:8, :].astype(F32) * (zn * _sigmoid(zn)))
        nxt1, nxt2 = _sub_row(dcn, 0), _sub_row(dcn, 1)
        row = lax.broadcasted_iota(jnp.int32, (tm, wd), 0)
        f1 = jnp.where(row == tm - 1, nxt1, pltpu.roll(dconv, tm - 1, 0))
        f2 = jnp.where(row == tm - 2, nxt1, jnp.where(row == tm - 1, nxt2, pltpu.roll(dconv, tm - 2, 0)))
        dcu = w_ref[2:3, :] * dconv + w_ref[1:2, :] * f1 + w_ref[0:1, :] * f2
        out_ref[:, 0:wd] = (g * conv * sz).astype(BF16)
        out_ref[:, wd:2 * wd] = (dcu * u).astype(BF16)
        out_ref[:, 2 * wd:3 * wd] = (dcu * gc).astype(BF16)
        out_ref[:, 3 * wd:4 * wd] = (g * gb * conv * (sg * (1.0 + zb * (1.0 - sg)))).astype(BF16)
        w_row = lax.broadcasted_iota(jnp.int32, (3, wd), 0)
        dw0 = jnp.sum(dconv * r2, axis=0, keepdims=True)
        dw1 = jnp.sum(dconv * r1, axis=0, keepdims=True)
        dw2 = jnp.sum(dconv * cu, axis=0, keepdims=True)
        dw_ref[...] += jnp.where(w_row == 0, dw0, jnp.where(w_row == 1, dw1, dw2))

    blk = pl.BlockSpec((tm, wd), lambda c, i: (i, c))
    nxt = pl.BlockSpec((8, wd), lambda c, i: (jnp.minimum((i + 1) * (tm // 8), s // 8 - 1), c))
    wspec = pl.BlockSpec((3, wd), lambda c, i: (0, c))
    return pl.pallas_call(
        body, name="conv_bwd", grid=(CONV_W // wd, s // tm),
        in_specs=[blk, nxt] + tiles + befores + afters + [wspec],
        out_specs=[pl.BlockSpec((tm, 4 * wd), lambda c, i: (i, c)), wspec],
        out_shape=[jax.ShapeDtypeStruct((s, 4 * CONV_W), BF16), jax.ShapeDtypeStruct((3, CONV_W), F32)],
        compiler_params=_params(("parallel", "arbitrary")),
    )(dob, dob, *([proj] * 8), conv_w)


def _piece_layout(pieces):
    offs, off = [], 0
    for p in pieces:
        offs.append((off, p.shape[1]))
        off += p.shape[1]
    assert off == N_ALL, off
    return offs


def _dw_in(h, pieces, chip_sums):
    s = h.shape[0]
    tk, tn = min(TK_DW, s), TN_DW
    nk = s // tk
    nn = N_MAIN // tn
    main, fpiece = pieces[:-1], pieces[-1]
    layout = _piece_layout(pieces)[:-1]
    n_main = len(main)
    nx = len(chip_sums)

    def body(*refs):
        p_refs, f_ref, h_ref = refs[:n_main], refs[n_main], refs[n_main + 1]
        ins, refs = refs[n_main + 2:n_main + 2 + nx], refs[n_main + 2 + nx:]
        out_ref, outf_ref = refs[:2]
        outs, (acc, accf, send_sems, recv_sems, local_sems) = refs[2:2 + nx], refs[2 + nx:]
        n, k = pl.program_id(0), pl.program_id(1)

        @pl.when(jnp.logical_and(n == 0, k == 0))
        def _():
            for cp in _all_to_all_copies(ins, outs, send_sems, recv_sems, local_sems):
                cp.start()

        @pl.when(k == 0)
        def _():
            acc[...] = jnp.zeros_like(acc)
        hv = h_ref[pl.ds(pl.multiple_of(k * tk, tk), tk), :]
        for p_ref, (off, width) in zip(p_refs, layout):
            @pl.when(jnp.logical_and(n >= off // tn, n < (off + width) // tn))
            def _():
                acc[...] += _dot_tn(p_ref[...], hv)

        @pl.when(k == nk - 1)
        def _():
            out_ref[...] = acc[...].astype(BF16)

        @pl.when(n == 0)
        def _():
            @pl.when(k == 0)
            def _():
                accf[...] = jnp.zeros_like(accf)
            accf[...] += _dot_tn(f_ref[...], hv)

            @pl.when(k == nk - 1)
            def _():
                outf_ref[...] = accf[...].astype(BF16)

        @pl.when(jnp.logical_and(n == nn - 1, k == nk - 1))
        def _():
            for cp in _all_to_all_copies(ins, outs, send_sems, recv_sems, local_sems):
                cp.wait()

    def piece_spec(off, width):
        lo, hi = off // tn, (off + width) // tn

        def index(n, k):
            active = jnp.logical_and(n >= lo, n < hi)
            return jnp.where(active, k, 0), jnp.clip(n - lo, 0, hi - lo - 1)
        return pl.BlockSpec((tk, tn), index)

    any_spec = pl.BlockSpec(memory_space=pl.ANY)
    res = pl.pallas_call(
        body, name="dw_in", grid=(nn, nk),
        in_specs=[piece_spec(off, width) for off, width in layout]
        + [pl.BlockSpec((tk, N_FPAD), lambda n, k: (jnp.where(n == 0, k, 0), 0)),
           pl.BlockSpec((s, D_MODEL), lambda n, k: (0, 0))] + [any_spec] * nx,
        out_specs=[pl.BlockSpec((tn, D_MODEL), lambda n, k: (n, 0)),
                   pl.BlockSpec((N_FPAD, D_MODEL), lambda n, k: (0, 0))] + [any_spec] * nx,
        out_shape=[jax.ShapeDtypeStruct((N_MAIN, D_MODEL), BF16), jax.ShapeDtypeStruct((N_FPAD, D_MODEL), BF16)]
        + [jax.ShapeDtypeStruct(a.shape, a.dtype) for a in chip_sums],
        scratch_shapes=[pltpu.VMEM((tn, D_MODEL), F32), pltpu.VMEM((N_FPAD, D_MODEL), F32)] + _gather_sems(nx),
        compiler_params=_params(("arbitrary", "arbitrary")),
    )(*main, fpiece, h, *chip_sums)
    return res[:2], res[2:]


def _dh_and_dx(pieces, w_all_t, x, dy, ada3, norm_g, chip_sums):
    s = x.shape[0]
    tm = min(TM_DH, s)
    nt = s // tm
    n = len(chip_sums)
    npc = len(pieces)
    layout = _piece_layout(pieces)

    def body(*refs):
        p_refs, refs = refs[:npc], refs[npc:]
        wt_ref, x_ref, dy_ref, ada_ref, g_ref = refs[:5]
        ins, refs = refs[5:5 + n], refs[5 + n:]
        gx_ref, dsh_ref, dsc_ref, dg_ref = refs[:4]
        outs, (send_sems, recv_sems, local_sems) = refs[4:4 + n], refs[4 + n:]
        i = pl.program_id(0)

        @pl.when(i == 0)
        def _():
            for cp in _chip_copies(ins, outs, send_sems, recv_sems, local_sems):
                cp.start()
            dsh_ref[...] = jnp.zeros_like(dsh_ref)
            dsc_ref[...] = jnp.zeros_like(dsc_ref)
            dg_ref[...] = jnp.zeros_like(dg_ref)

        dh = None
        for p_ref, (off, width) in zip(p_refs, layout):
            part = _dot(p_ref[...], wt_ref[off:off + width, :])
            dh = part if dh is None else dh + part
        xv = x_ref[...]
        r = lax.rsqrt(jnp.mean(xv * xv, axis=-1, keepdims=True) + EPS)
        xhat = xv * r
        g = g_ref[...]
        one_sc = 1.0 + ada_ref[1:2, :]
        dsh_ref[...] += jnp.sum(dh, axis=0, keepdims=True)
        dsc_ref[...] += jnp.sum(dh * (xhat * g), axis=0, keepdims=True)
        dg_ref[...] += jnp.sum(dh * xhat, axis=0, keepdims=True) * one_sc
        dxh = dh * (g * one_sc)
        dx = r * (dxh - xhat * jnp.mean(dxh * xhat, axis=-1, keepdims=True))
        gx_ref[...] = dy_ref[...] + dx

        @pl.when(i == nt - 1)
        def _():
            for cp in _chip_copies(ins, outs, send_sems, recv_sems, local_sems):
                cp.wait()

    full = pl.BlockSpec((tm, D_MODEL), lambda i: (i, 0))
    vec = pl.BlockSpec((1, D_MODEL), lambda i: (0, 0))
    any_spec = pl.BlockSpec(memory_space=pl.ANY)
    res = pl.pallas_call(
        body, name="dh_dx", grid=(nt,),
        in_specs=[pl.BlockSpec((tm, p.shape[1]), lambda i: (i, 0)) for p in pieces]
        + [pl.BlockSpec((N_ALL, D_MODEL), lambda i: (0, 0)), full, full,
           pl.BlockSpec((3, D_MODEL), lambda i: (0, 0)), vec] + [any_spec] * n,
        out_specs=[full, vec, vec, vec] + [any_spec] * n,
        out_shape=[jax.ShapeDtypeStruct((s, D_MODEL), F32)] + [jax.ShapeDtypeStruct((1, D_MODEL), F32)] * 3
        + [jax.ShapeDtypeStruct(a.shape, a.dtype) for a in chip_sums],
        scratch_shapes=[pltpu.SemaphoreType.DMA((n * 3,)), pltpu.SemaphoreType.DMA((n * 3,)),
                        pltpu.SemaphoreType.DMA((n,))],
        compiler_params=_params(("arbitrary",)),
    )(*pieces, w_all_t, x, dy, ada3, norm_g, *chip_sums)
    return res[:4], res[4:]


def _sum_small(vec_all, qg_parts, kg_parts):
    def body(v_ref, q_ref, k_ref, tot_ref, gq_ref, gk_ref):
        tot = v_ref[0:1, :]
        for p in range(1, N_DEV):
            tot = tot + v_ref[p:p + 1, :]
        tot_ref[...] = tot
        gq_ref[...] = jnp.sum(q_ref[...], axis=0, keepdims=True)
        gk_ref[...] = jnp.sum(k_ref[...], axis=0, keepdims=True)

    n = vec_all.shape[-1]
    return pl.pallas_call(
        body, name="sum_small",
        out_shape=[jax.ShapeDtypeStruct((1, n), F32),
                   jax.ShapeDtypeStruct((1, HEAD_DIM), F32), jax.ShapeDtypeStruct((1, HEAD_DIM), F32)],
        compiler_params=_params(),
    )(vec_all, qg_parts, kg_parts)


def _grad_w_ada(c_rows, dada_rows):
    def body(c_ref, d_ref, out_ref):
        out_ref[...] = jnp.dot(c_ref[...].T, d_ref[...], preferred_element_type=F32,
                               precision=lax.Precision.HIGHEST)

    return pl.pallas_call(
        body, name="grad_w_ada",
        out_shape=jax.ShapeDtypeStruct((D_MODEL, ADA_SHARD), F32),
        compiler_params=_params(),
    )(c_rows, dada_rows)


def _adam_step(w, m, v, g):
    c1 = 1.0 / (1.0 - ADAM_B1 ** ADAM_STEP)
    c2 = 1.0 / (1.0 - ADAM_B2 ** ADAM_STEP)
    m_new = ADAM_B1 * m + (1.0 - ADAM_B1) * g
    v_new = ADAM_B2 * v + (1.0 - ADAM_B2) * (g * g)
    return -ADAM_LR * ((m_new * c1) / (jnp.sqrt(v_new * c2) + ADAM_EPS) + ADAM_WD * w), m_new, v_new


def _adamw_small(params, name):
    n = len(params)
    stacked = [p[3].ndim == p[0].ndim + 1 for p in params]

    def body(*refs):
        ins, outs = refs[:4 * n], refs[4 * n:]
        for k in range(n):
            w_ref, m_ref, v_ref, g_ref = ins[4 * k:4 * k + 4]
            go_ref, d_ref, mo_ref, vo_ref = outs[4 * k:4 * k + 4]
            if stacked[k]:
                g = g_ref[0].astype(F32)
                for p in range(1, g_ref.shape[0]):
                    g = g + g_ref[p].astype(F32)
            else:
                g = g_ref[...]
            go_ref[...] = g
            d_ref[...], mo_ref[...], vo_ref[...] = _adam_step(w_ref[...], m_ref[...], v_ref[...], g)

    res = pl.pallas_call(
        body, name=name,
        out_shape=[jax.ShapeDtypeStruct(p[0].shape, F32) for p in params for _ in range(4)],
        compiler_params=_params(),
    )(*[a for p in params for a in p])
    return [tuple(res[4 * k:4 * k + 4]) for k in range(n)]


def _adamw(w, m, v, g_parts, name):
    rows, cols = w.shape
    n_parts = g_parts.shape[0]
    tr = 256 if rows % 256 == 0 else rows
    tc = 256 if (tr == rows and rows > 256 and cols % 256 == 0) else cols

    def body(w_ref, m_ref, v_ref, g_ref, go_ref, d_ref, mo_ref, vo_ref):
        g = g_ref[0].astype(F32)
        for p in range(1, n_parts):
            g = g + g_ref[p].astype(F32)
        go_ref[...] = g
        d_ref[...], mo_ref[...], vo_ref[...] = _adam_step(w_ref[...], m_ref[...], v_ref[...], g)

    blk = pl.BlockSpec((tr, tc), lambda i, j: (i, j))
    return pl.pallas_call(
        body, name=name, grid=(rows // tr, cols // tc),
        in_specs=[blk, blk, blk, pl.BlockSpec((n_parts, tr, tc), lambda i, j: (0, i, j))],
        out_specs=[blk] * 4,
        out_shape=[jax.ShapeDtypeStruct((rows, cols), F32)] * 4,
        compiler_params=_params(("parallel", "parallel")),
    )(w, m, v, g_parts)


_O_F = 1536


W_TILE = 16
WIN_ROWS = 784


def _internal_start(p):
    return p * IN_SHARD - (HEADS if p * IN_SHARD > _O_F else 0)


def _shard_window(wt_shard, me):
    lo = me * IN_SHARD
    o = lo + lax.broadcasted_iota(jnp.int32, (IN_SHARD, 1), 0)
    is_f = jnp.logical_and(o >= _O_F, o < _O_F + HEADS)
    start = lo - jnp.where(lo > _O_F, HEADS, 0)
    window = lax.dynamic_update_slice(jnp.zeros((WIN_ROWS + W_TILE, D_MODEL), BF16),
                                      jnp.where(is_f, 0.0, wt_shard).astype(BF16), (start % W_TILE, 0))
    first = jnp.clip(_O_F - lo, 0, IN_SHARD - W_TILE)
    near = lax.dynamic_slice(wt_shard, (first, 0), (W_TILE, D_MODEL))
    j = lax.broadcasted_iota(jnp.int32, (W_TILE, 1), 0)
    src = _O_F - lo + j
    ok = jnp.logical_and(j < HEADS, jnp.logical_and(src >= 0, src < IN_SHARD))
    f_tile = jnp.where(ok, jnp.roll(near, first - (_O_F - lo), axis=0), 0.0).astype(BF16)
    return lax.dynamic_update_slice(window, f_tile, (WIN_ROWS, 0))


def _assemble_w(windows):
    chunk = 112

    def body(g_ref, out_ref):
        out_ref[WIN_ROWS:N_MAIN, :] = jnp.zeros((N_MAIN - WIN_ROWS, D_MODEL), BF16)
        for p in range(N_DEV):
            base = _internal_start(p) // W_TILE * W_TILE
            for r in range(0, WIN_ROWS, chunk):
                rows = slice(base + r, base + r + chunk)
                piece = g_ref[p, r:r + chunk, :]
                out_ref[rows, :] = piece if p == 0 else out_ref[rows, :] + piece
        f = g_ref[0, WIN_ROWS:WIN_ROWS + W_TILE, :]
        for p in range(1, N_DEV):
            f = f + g_ref[p, WIN_ROWS:WIN_ROWS + W_TILE, :]
        out_ref[N_MAIN:N_MAIN + W_TILE, :] = f
        out_ref[N_MAIN + W_TILE:N_ALL, :] = jnp.zeros((N_FPAD - W_TILE, D_MODEL), BF16)

    return pl.pallas_call(
        body, name="assemble_w", out_shape=jax.ShapeDtypeStruct((N_ALL, D_MODEL), BF16),
        compiler_params=_params(),
    )(windows)


def _shard_rows(windows, me):
    n = windows.shape[0]
    lo = me * IN_SHARD
    start = lo - jnp.where(lo > _O_F, HEADS, 0)
    main = lax.dynamic_slice(windows, (0, start % W_TILE, 0), (n, IN_SHARD, D_MODEL))
    first = jnp.clip(_O_F - lo, 0, IN_SHARD - W_TILE)
    near = lax.dynamic_slice(main, (0, first, 0), (n, W_TILE, D_MODEL))
    j = first - (_O_F - lo) + lax.broadcasted_iota(jnp.int32, (1, W_TILE, 1), 1)
    f_rows = jnp.roll(windows[:, WIN_ROWS:], -(first - (_O_F - lo)), axis=1)
    patch = jnp.where(jnp.logical_and(j >= 0, j < HEADS), f_rows, near)
    return lax.dynamic_update_slice(main, patch, (0, first, 0))


def kernel(x, c, w_ada, b_ada, norm_g, w_in, b_f, q_norm_g, k_norm_g, conv_w, w_attn_out, w_conv_out, w_o, loss_target, m_w_ada, m_b_ada, m_norm_g, m_w_in, m_b_f, m_q_norm_g, m_k_norm_g, m_conv_w, m_w_attn_out, m_w_conv_out, m_w_o, v_w_ada, v_b_ada, v_norm_g, v_w_in, v_b_f, v_q_norm_g, v_k_norm_g, v_conv_w, v_w_attn_out, v_w_conv_out, v_w_o):
    me = 4 * lax.axis_index("x") + 2 * lax.axis_index("y") + lax.axis_index("c")
    s = x.shape[1]
    x2, t2 = x[0], loss_target[0]

    w_in_g, c_all, ada_g = _gather_weights_and_ada(_shard_window(w_in[0].T, me), c, w_ada[0])
    ada_mine = lax.dynamic_index_in_dim(ada_g[:, :, 0, :], me, axis=1, keepdims=False)
    ada3 = (ada_mine.reshape(1, 3 * D_MODEL) + b_ada).reshape(3, D_MODEL)
    w_all_t = _assemble_w(w_in_g)
    qg = jnp.tile(q_norm_g, (1, HEADS))
    kg = jnp.tile(k_norm_g, (1, HEADS))
    bf_pad = jnp.pad(b_f, ((0, 0), (0, LANES - HEADS)))

    (proj, fl, h), (qa, ka, va, kt, vt), (cw_g, wa_g, wb_g, wo_g) = _proj_fwd(
        x2, ada3, norm_g, w_all_t, bf_pad, qg, kg,
        [conv_w[0], w_attn_out[0].astype(BF16), w_conv_out[0].astype(BF16), w_o[0].astype(BF16)])
    wo = wo_g.reshape(D_MODEL, D_MODEL)
    cw = jnp.transpose(cw_g, (1, 0, 2)).reshape(3, CONV_W)
    attn, oa, qb = _attn_fwd(qa, ka, vt, proj)
    (dy, dgab, do, dza, dob, dwo, dwa, dwb, dgate, loss_part) = _tail(oa, attn, proj, x2, t2, ada3, wa_g, wb_g, wo, cw)

    small = [dwa, dwb, dwo.reshape(N_DEV, D_MODEL // N_DEV, D_MODEL)]
    dqkv, dqg, dkg, dcum = _attn_bwd(qb, ka, kt, va, do, proj, qg, kg)
    df, dbf = _forget_bwd(dcum, fl, bf_pad)
    dconv, dcw = _conv_bwd(dob, proj, cw)
    pieces = [dqkv, dza, dconv, dgab, df]
    (dw_main, dw_f), (g_wa_parts, g_wb_parts, g_wo_parts) = _dw_in(h, pieces, small)

    pair_in = _sibling_swap_sum(dw_main, dw_f, "swap_sum_w_in")
    (grad_x, dshift, dscale, dnormg), (g_in_windows,) = _dh_and_dx(
        pieces, w_all_t, x2, dy, ada3, norm_g, [pair_in])
    g_in_parts = _shard_rows(g_in_windows, me)
    vec = jnp.concatenate([dshift, dscale, dgate, dnormg, dbf, dcw.reshape(1, 3 * CONV_W), loss_part, dqg, dkg],
                          axis=1)
    (vec_all,) = _gather_direct([vec], "gather_small")
    vec_all = vec_all.reshape(N_DEV, vec.shape[1])
    n_main = 4 * D_MODEL + LANES + 3 * CONV_W + LANES
    tot, g_qg, g_kg = _sum_small(
        vec_all[:, :n_main],
        vec_all[:, n_main:n_main + ATTN_W].reshape(N_DEV * HEADS, HEAD_DIM),
        vec_all[:, n_main + ATTN_W:].reshape(N_DEV * HEADS, HEAD_DIM))
    g_b_ada = tot[:, 0:3 * D_MODEL]
    g_norm_g = tot[:, 3 * D_MODEL:4 * D_MODEL]
    g_b_f = tot[:, 4 * D_MODEL:4 * D_MODEL + HEADS]
    g_cw_full = tot[:, 4 * D_MODEL + LANES:4 * D_MODEL + LANES + 3 * CONV_W].reshape(3, CONV_W)
    g_cw = lax.dynamic_slice(g_cw_full, (0, me * (CONV_W // N_DEV)), (3, CONV_W // N_DEV))
    dada_mine = lax.dynamic_slice(vec_all[:, 0:3 * D_MODEL], (0, me * ADA_SHARD), (N_DEV, ADA_SHARD))
    g_w_ada = _grad_w_ada(c_all.reshape(N_DEV, D_MODEL), dada_mine)

    upd = {}
    upd["w_ada"] = _adamw(w_ada[0], m_w_ada[0], v_w_ada[0], g_w_ada[None], "adamw_w_ada")
    upd["w_in"] = [u.T for u in _adamw(w_in[0].T, m_w_in[0].T, v_w_in[0].T, g_in_parts, "adamw_w_in")]
    small_names = ["b_ada", "norm_g", "b_f", "q_norm_g", "k_norm_g", "conv_w", "w_attn_out", "w_conv_out", "w_o"]
    small_upd = _adamw_small(
        [(b_ada, m_b_ada, v_b_ada, g_b_ada), (norm_g, m_norm_g, v_norm_g, g_norm_g), (b_f, m_b_f, v_b_f, g_b_f),
         (q_norm_g, m_q_norm_g, v_q_norm_g, g_qg), (k_norm_g, m_k_norm_g, v_k_norm_g, g_kg),
         (conv_w[0], m_conv_w[0], v_conv_w[0], g_cw),
         (w_attn_out[0], m_w_attn_out[0], v_w_attn_out[0], g_wa_parts),
         (w_conv_out[0], m_w_conv_out[0], v_w_conv_out[0], g_wb_parts),
         (w_o[0], m_w_o[0], v_w_o[0], g_wo_parts)], "adamw_small")
    upd.update(zip(small_names, small_upd))

    names = ["w_ada", "b_ada", "norm_g", "w_in", "b_f", "q_norm_g", "k_norm_g", "conv_w",
             "w_attn_out", "w_conv_out", "w_o"]
    lead = {"w_ada", "w_in", "conv_w", "w_attn_out", "w_conv_out", "w_o"}
    fix = lambda n, a: a[None] if n in lead else a
    loss = tot[0, n_main - LANES]
    outs = [loss, grad_x[None]]
    for k in range(4):
        outs += [fix(n, upd[n][k]) for n in names]
    return tuple(outs)
```

```python
import numpy as np
import jax
import jax.numpy as jnp
from jax import lax
from jax.experimental import pallas as pl
from jax.experimental.pallas import tpu as pltpu

F32 = jnp.float32
BF16 = jnp.bfloat16

D_MODEL = 1024
HEADS = 8
HEAD_DIM = 64
ATTN_W = 512
CONV_W = 512
N_DEV = 8
IN_WIDTH = 6152
IN_SHARD = IN_WIDTH // N_DEV
N_MAIN = 6144
N_FPAD = 128
N_ALL = N_MAIN + N_FPAD
ADA_SHARD = 3 * D_MODEL // N_DEV
EPS = 1e-6
NEG = -1e30

ADAM_LR = 0.001
ADAM_B1 = 0.9
ADAM_B2 = 0.999
ADAM_EPS = 1e-08
ADAM_WD = 0.01
ADAM_STEP = 10

LANES = 128
VMEM_LIMIT = 56 * 1024 * 1024

TM_PROJ = 512
TN_PROJ = 1024
TM_ELEM = 512
TQ = 512
HEADS_PER_STEP = 8
TM_TAIL = 256
TC_TAIL = 512
TC_CUM = 512
TK_DW = 2048
TN_DW = 512
TM_DH = 256
HALO = 16

OFF_Q, OFF_K, OFF_V, OFF_ZA, OFF_CB, OFF_CC, OFF_CU, OFF_CZ, OFF_GA, OFF_GB = (
    0, 512, 1024, 1536, 2048, 2560, 3072, 3584, 4096, 5120)


def _params(sem=None):
    return pltpu.CompilerParams(dimension_semantics=sem, vmem_limit_bytes=VMEM_LIMIT)


def _dot(a, b):
    return jnp.dot(a, b, preferred_element_type=F32)


def _dot_nt(a, b):
    return lax.dot_general(a, b, (((1,), (1,)), ((), ())), preferred_element_type=F32)


def _dot_tn(a, b):
    return lax.dot_general(a, b, (((0,), (0,)), ((), ())), preferred_element_type=F32)


def _sigmoid(x):
    return 1.0 / (1.0 + jnp.exp(-x))


def _seg_sum(z, lo):
    a = jnp.sum(jnp.where(lo, z, 0.0), axis=-1, keepdims=True)
    b = jnp.sum(jnp.where(lo, 0.0, z), axis=-1, keepdims=True)
    return jnp.where(lo, a, b)


def _lane_col(z, lane):
    idx = lax.broadcasted_iota(jnp.int32, z.shape, 1)
    return jnp.sum(jnp.where(idx == lane, z, 0.0), axis=-1, keepdims=True)


def _sub_row(z, row):
    idx = lax.broadcasted_iota(jnp.int32, z.shape, 0)
    return jnp.sum(jnp.where(idx == row, z, 0.0), axis=0, keepdims=True)


def _mesh_pos():
    x, y, c = lax.axis_index("x"), lax.axis_index("y"), lax.axis_index("c")
    return x, y, c, 4 * x + 2 * y + c


def _peer(k, x, y, c):
    px = 1 - x if (k >> 2) & 1 else x
    py = 1 - y if (k >> 1) & 1 else y
    pc = 1 - c if k & 1 else c
    return (px, py, pc), 4 * px + 2 * py + pc


def _gather_copies(ins, outs, send_sems, recv_sems, local_sems):
    x, y, c, me = _mesh_pos()
    copies = []
    for a in range(len(ins)):
        copies.append(pltpu.make_async_copy(ins[a], outs[a].at[me], local_sems.at[a]))
        for k in range(1, N_DEV):
            dev, _ = _peer(k, x, y, c)
            copies.append(pltpu.make_async_remote_copy(
                src_ref=ins[a], dst_ref=outs[a].at[me],
                send_sem=send_sems.at[a * (N_DEV - 1) + k - 1], recv_sem=recv_sems.at[a * (N_DEV - 1) + k - 1],
                device_id=dev, device_id_type=pl.DeviceIdType.MESH))
    return copies


def _gather_sems(n):
    return [pltpu.SemaphoreType.DMA((n * (N_DEV - 1),)), pltpu.SemaphoreType.DMA((n * (N_DEV - 1),)),
            pltpu.SemaphoreType.DMA((n,))]


def _gather_direct(arrs, name):
    n = len(arrs)
    any_spec = pl.BlockSpec(memory_space=pl.ANY)

    def body(*refs):
        copies = _gather_copies(refs[:n], refs[n:2 * n], *refs[2 * n:])
        for cp in copies:
            cp.start()
        for cp in copies:
            cp.wait()

    return pl.pallas_call(
        body, name=name, out_shape=[jax.ShapeDtypeStruct((N_DEV,) + a.shape, a.dtype) for a in arrs],
        in_specs=[any_spec] * n, out_specs=[any_spec] * n, scratch_shapes=_gather_sems(n),
    )(*arrs)


def _ada_phase(c_ref, w_ref, call_ref, adag_ref, mine_ref, send_sems, recv_sems):
    x, y, c, me = _mesh_pos()

    def copy(phase, k, src, dst):
        dev, _ = _peer(k, x, y, c)
        return pltpu.make_async_remote_copy(
            src_ref=src, dst_ref=dst,
            send_sem=send_sems.at[phase * (N_DEV - 1) + k - 1],
            recv_sem=recv_sems.at[phase * (N_DEV - 1) + k - 1],
            device_id=dev, device_id_type=pl.DeviceIdType.MESH)

    call_ref[me] = c_ref[...]
    first = [copy(0, k, c_ref, call_ref.at[me]) for k in range(1, N_DEV)]
    for cp in first:
        cp.start()
    for cp in first:
        cp.wait()
    wb = w_ref[...].astype(BF16)
    for b in range(N_DEV):
        row = jnp.broadcast_to(call_ref[b], (8, D_MODEL)).astype(BF16)
        mine_ref[b] = _sub_row(_dot(row, wb), 0)
    adag_ref[me] = mine_ref[...]
    second = [copy(1, k, mine_ref, adag_ref.at[me]) for k in range(1, N_DEV)]
    for cp in second:
        cp.start()
    for cp in second:
        cp.wait()


def _gather_weights_and_ada(wt_shard, c_row, w_ada_sh):
    any_spec = pl.BlockSpec(memory_space=pl.ANY)
    vm = pl.BlockSpec(memory_space=pltpu.VMEM)

    def body(w_in_ref, c_ref, wada_ref, out_ref, call_ref, adag_ref, mine_ref, send_sems, recv_sems, local_sem,
             ada_send, ada_recv):
        x, y, c, me = _mesh_pos()
        sibling = (x, y, 1 - c)
        chips = [(1 - x, y), (x, 1 - y), (1 - x, 1 - y)]

        def copy(k, src, blk, to):
            return pltpu.make_async_remote_copy(
                src_ref=src, dst_ref=out_ref.at[blk], send_sem=send_sems.at[k], recv_sem=recv_sems.at[k],
                device_id=to, device_id_type=pl.DeviceIdType.MESH)

        local = pltpu.make_async_copy(w_in_ref, out_ref.at[me], local_sem.at[0])
        local.start()
        first = [copy(0, w_in_ref, me, sibling)]
        first += [copy(1 + j, w_in_ref, me, (px, py, c)) for j, (px, py) in enumerate(chips)]
        for cp in first:
            cp.start()
        _ada_phase(c_ref, wada_ref, call_ref, adag_ref, mine_ref, ada_send, ada_recv)
        passed = []
        for j, (px, py) in enumerate(chips):
            blk = 4 * px + 2 * py + c
            copy(1 + j, w_in_ref, blk, (x, y, c)).wait_recv()
            fwd = copy(4 + j, out_ref.at[blk], blk, sibling)
            fwd.start()
            passed.append(fwd)
        copy(0, w_in_ref, 4 * x + 2 * y + 1 - c, (x, y, c)).wait_recv()
        for j, (px, py) in enumerate(chips):
            copy(4 + j, w_in_ref, 4 * px + 2 * py + 1 - c, (x, y, c)).wait_recv()
        for cp in first + passed:
            cp.wait_send()
        local.wait()

    per = N_DEV - 1
    return pl.pallas_call(
        body, name="gather_weights",
        out_shape=[jax.ShapeDtypeStruct((N_DEV,) + wt_shard.shape, wt_shard.dtype),
                   jax.ShapeDtypeStruct((N_DEV, 1, D_MODEL), F32),
                   jax.ShapeDtypeStruct((N_DEV, N_DEV, 1, ADA_SHARD), F32)],
        in_specs=[any_spec, vm, vm], out_specs=[any_spec, vm, vm],
        scratch_shapes=[pltpu.VMEM((N_DEV, 1, ADA_SHARD), F32),
                        pltpu.SemaphoreType.DMA((per,)), pltpu.SemaphoreType.DMA((per,)),
                        pltpu.SemaphoreType.DMA((1,)),
                        pltpu.SemaphoreType.DMA((2 * per,)), pltpu.SemaphoreType.DMA((2 * per,))],
        compiler_params=pltpu.CompilerParams(vmem_limit_bytes=VMEM_LIMIT),
    )(wt_shard, c_row, w_ada_sh)


def _sibling_swap_sum(dwt, dwt_f, name):
    nch, rows = N_DEV // 2, WIN_ROWS + W_TILE
    any_spec = pl.BlockSpec(memory_space=pl.ANY)

    def body(main_ref, f_ref, out_ref, mine_v, theirs_v, sum_v, send_sems, recv_sems, load_sems, store_sems):
        x, y, c, _ = _mesh_pos()

        def parts(core, ch, buf):
            b0, b1 = (_internal_start(2 * ch + k) // W_TILE * W_TILE for k in range(2))
            base = pl.multiple_of(jnp.where(core == 0, b0, b1), W_TILE)
            return [(main_ref.at[pl.ds(base, WIN_ROWS)], buf.at[ch, pl.ds(0, WIN_ROWS)]),
                    (f_ref.at[pl.ds(0, W_TILE)], buf.at[ch, pl.ds(WIN_ROWS, W_TILE)])]

        swaps, loads, stores = [], [], []
        for ch in range(nch):
            for k, (src, dst) in enumerate(parts(1 - c, ch, theirs_v)):
                swaps.append(pltpu.make_async_remote_copy(
                    src_ref=src, dst_ref=dst, send_sem=send_sems.at[2 * ch + k], recv_sem=recv_sems.at[2 * ch + k],
                    device_id=(x, y, 1 - c), device_id_type=pl.DeviceIdType.MESH))
            for k, (src, dst) in enumerate(parts(c, ch, mine_v)):
                loads.append(pltpu.make_async_copy(src, dst, load_sems.at[2 * ch + k]))
            stores.append(pltpu.make_async_copy(sum_v.at[ch], out_ref.at[ch], store_sems.at[ch]))
        for cp in swaps + loads:
            cp.start()
        for ch in range(nch):
            for cp in loads[2 * ch:2 * ch + 2] + swaps[2 * ch:2 * ch + 2]:
                cp.wait()
            sum_v[ch] = (mine_v[ch].astype(F32) + theirs_v[ch].astype(F32)).astype(BF16)
            stores[ch].start()
        for cp in stores:
            cp.wait()

    return pl.pallas_call(
        body, name=name,
        out_shape=jax.ShapeDtypeStruct((nch, rows, D_MODEL), BF16),
        in_specs=[any_spec, any_spec], out_specs=any_spec,
        scratch_shapes=[pltpu.VMEM((nch, rows, D_MODEL), BF16)] * 3
        + [pltpu.SemaphoreType.DMA((2 * nch,))] * 3 + [pltpu.SemaphoreType.DMA((nch,))],
        compiler_params=_params(),
    )(dwt, dwt_f)


def _all_to_all_copies(ins, outs, send_sems, recv_sems, local_sems):
    x, y, c, me = _mesh_pos()
    copies = []
    for a in range(len(ins)):
        copies.append(pltpu.make_async_copy(ins[a].at[me], outs[a].at[me], local_sems.at[a]))
        for k in range(1, N_DEV):
            dev, p = _peer(k, x, y, c)
            copies.append(pltpu.make_async_remote_copy(
                src_ref=ins[a].at[p], dst_ref=outs[a].at[me],
                send_sem=send_sems.at[a * (N_DEV - 1) + k - 1], recv_sem=recv_sems.at[a * (N_DEV - 1) + k - 1],
                device_id=dev, device_id_type=pl.DeviceIdType.MESH))
    return copies


def _chip_copies(ins, outs, send_sems, recv_sems, local_sems):
    x, y, c, _ = _mesh_pos()
    my_chip = 2 * x + y
    chips = [(1 - x, y), (x, 1 - y), (1 - x, 1 - y)]
    copies = []
    for a in range(len(ins)):
        copies.append(pltpu.make_async_copy(ins[a].at[my_chip], outs[a].at[my_chip], local_sems.at[a]))
        for j, (px, py) in enumerate(chips):
            copies.append(pltpu.make_async_remote_copy(
                src_ref=ins[a].at[2 * px + py], dst_ref=outs[a].at[my_chip],
                send_sem=send_sems.at[a * 3 + j], recv_sem=recv_sems.at[a * 3 + j],
                device_id=(px, py, c), device_id_type=pl.DeviceIdType.MESH))
    return copies


def _proj_fwd(x, ada3, norm_g, w_all_t, bf_pad, qg, kg, later):
    s = x.shape[0]
    tm, tn = min(TM_PROJ, s), TN_PROJ
    nt = s // tm
    n = len(later)

    def body(x_ref, ada_ref, g_ref, wt_ref, bf_ref, qg_ref, kg_ref, *rest):
        ins, (proj_ref, fl_ref, h_ref), rows_refs, rest = rest[:n], rest[n:n + 3], rest[n + 3:n + 8], rest[n + 8:]
        outs, (carry, send_sems, recv_sems, local_sems) = rest[:n], rest[n:]
        i = pl.program_id(0)

        @pl.when(i == 0)
        def _():
            carry[...] = jnp.zeros_like(carry)
            for cp in _gather_copies(ins, outs, send_sems, recv_sems, local_sems):
                cp.start()

        xv = x_ref[...]
        r = lax.rsqrt(jnp.mean(xv * xv, axis=-1, keepdims=True) + EPS)
        hv = ((xv * r) * g_ref[...]) * (1.0 + ada_ref[1:2, :]) + ada_ref[0:1, :]
        hb = hv.astype(BF16)
        h_ref[...] = hb
        fl = _dot_nt(hb, wt_ref[N_MAIN:N_ALL, :])
        fl_ref[...] = fl
        for j in range(N_MAIN // tn):
            proj_ref[:, j * tn:(j + 1) * tn] = _dot_nt(hb, wt_ref[j * tn:(j + 1) * tn, :]).astype(BF16)
        _attention_rows(proj_ref, fl, bf_ref, qg_ref, kg_ref, carry, *rows_refs)

        @pl.when(i == nt - 1)
        def _():
            for cp in _gather_copies(ins, outs, send_sems, recv_sems, local_sems):
                cp.wait()

    any_spec = pl.BlockSpec(memory_space=pl.ANY)
    heads = pl.BlockSpec((HEADS, tm, LANES), lambda i: (0, i, 0))
    heads_t = pl.BlockSpec((HEADS, LANES, tm), lambda i: (0, 0, i))
    vec = pl.BlockSpec((1, ATTN_W), lambda i: (0, 0))
    res = pl.pallas_call(
        body, name="proj_fwd", grid=(nt,),
        in_specs=[pl.BlockSpec((tm, D_MODEL), lambda i: (i, 0)),
                  pl.BlockSpec((3, D_MODEL), lambda i: (0, 0)),
                  pl.BlockSpec((1, D_MODEL), lambda i: (0, 0)),
                  pl.BlockSpec((N_ALL, D_MODEL), lambda i: (0, 0)),
                  pl.BlockSpec((1, LANES), lambda i: (0, 0)), vec, vec] + [any_spec] * n,
        out_specs=[pl.BlockSpec((tm, N_MAIN), lambda i: (i, 0)),
                   pl.BlockSpec((tm, N_FPAD), lambda i: (i, 0)),
                   pl.BlockSpec((tm, D_MODEL), lambda i: (i, 0)),
                   heads, heads, heads, heads_t, heads_t] + [any_spec] * n,
        out_shape=[jax.ShapeDtypeStruct((s, N_MAIN), BF16),
                   jax.ShapeDtypeStruct((s, N_FPAD), F32),
                   jax.ShapeDtypeStruct((s, D_MODEL), BF16)]
        + [jax.ShapeDtypeStruct((HEADS, s, LANES), BF16)] * 3
        + [jax.ShapeDtypeStruct((HEADS, LANES, s), BF16)] * 2
        + [jax.ShapeDtypeStruct((N_DEV,) + a.shape, a.dtype) for a in later],
        scratch_shapes=[pltpu.VMEM((1, LANES), F32)] + _gather_sems(n),
        compiler_params=_params(("arbitrary",)),
    )(x, ada3, norm_g, w_all_t, bf_pad, qg, kg, *later)
    return res[:3], res[3:8], res[8:]


L_ONE_Q, L_F_Q, L_LSE_Q, L_END = HEAD_DIM, HEAD_DIM + 3, HEAD_DIM + 6, HEAD_DIM + 9


def _split3(f):
    hi = f.astype(BF16).astype(F32)
    r = f - hi
    mid = r.astype(BF16).astype(F32)
    return hi, mid, r - mid


def _place3(lane, first, parts, otherwise):
    a, b, c = parts
    return jnp.where(lane == first, a, jnp.where(lane == first + 1, b, jnp.where(lane == first + 2, c, otherwise)))


def _log_forget(fl, bf):
    z = fl + bf
    lf = jnp.minimum(z, 0.0) - jnp.log1p(jnp.exp(-jnp.abs(z)))
    lane = lax.broadcasted_iota(jnp.int32, z.shape, 1)
    return jnp.where(lane < HEADS, lf, 0.0)


def _attention_rows(p_ref, fl, bf_ref, qg_ref, kg_ref, carry, qa_ref, ka_ref, va_ref, kt_ref, vt_ref):
    tm = fl.shape[0]
    scale = HEAD_DIM ** -0.5
    tri = (lax.broadcasted_iota(jnp.int32, (tm, tm), 1) <= lax.broadcasted_iota(jnp.int32, (tm, tm), 0)).astype(F32)
    cum_v = jnp.dot(tri, _log_forget(fl, bf_ref[...]), preferred_element_type=F32,
                    precision=lax.Precision.HIGHEST) + carry[...]
    carry[...] = _sub_row(cum_v, tm - 1)
    lane = lax.broadcasted_iota(jnp.int32, (tm, LANES), 1)
    lo = lane < HEAD_DIM
    v_tail = jnp.where(lane < L_F_Q, 1.0, 0.0)
    for pr in range(ATTN_W // LANES):
        sl = slice(pr * LANES, (pr + 1) * LANES)
        q2 = p_ref[:, OFF_Q + pr * LANES:OFF_Q + (pr + 1) * LANES].astype(F32)
        k2 = p_ref[:, OFF_K + pr * LANES:OFF_K + (pr + 1) * LANES].astype(F32)
        v2 = p_ref[:, OFF_V + pr * LANES:OFF_V + (pr + 1) * LANES].astype(F32)
        rq = lax.rsqrt(_seg_sum(q2 * q2, lo) * (1.0 / HEAD_DIM) + EPS)
        rk = lax.rsqrt(_seg_sum(k2 * k2, lo) * (1.0 / HEAD_DIM) + EPS)
        qn = ((q2 * rq) * qg_ref[:, sl]) * scale
        kn = (k2 * rk) * kg_ref[:, sl]
        for hh in range(2):
            h = 2 * pr + hh
            f3 = _split3(_lane_col(cum_v, h))
            qh = qn if hh == 0 else pltpu.roll(qn, HEAD_DIM, 1)
            kh = kn if hh == 0 else pltpu.roll(kn, HEAD_DIM, 1)
            vh = v2 if hh == 0 else pltpu.roll(v2, HEAD_DIM, 1)
            q_tail = jnp.where(lane < L_F_Q, 1.0, _place3(lane, L_F_Q, f3, 0.0))
            k_tail = _place3(lane, L_ONE_Q, tuple(-f for f in f3), jnp.where(lane < L_END, 1.0, 0.0))
            k_row = jnp.where(lo, kh, k_tail)
            v_row = jnp.where(lo, vh, v_tail)
            qa_ref[h] = jnp.where(lo, qh, q_tail).astype(BF16)
            ka_ref[h] = k_row.astype(BF16)
            va_ref[h] = v_row.astype(BF16)
            kt_ref[h] = k_row.T.astype(BF16)
            vt_ref[h] = v_row.T.astype(BF16)


def _causal_t(t):
    return lax.broadcasted_iota(jnp.int32, (t, t), 0) <= lax.broadcasted_iota(jnp.int32, (t, t), 1)


def _tri_steps(nt, q_major):
    if q_major:
        pairs = [(i, j) for i in range(nt) for j in range(i + 1)]
    else:
        pairs = [(i, j) for j in range(nt) for i in range(j, nt)]
    return (jnp.asarray(np.array([p[0] for p in pairs], np.int32)),
            jnp.asarray(np.array([p[1] for p in pairs], np.int32)))


def _attn_fwd(qa, ka, vt, proj):
    s = qa.shape[1]
    t = min(TQ, s)
    it, jt = _tri_steps(s // t, True)
    hp = HEADS_PER_STEP
    wide = hp * HEAD_DIM
    za_blk = OFF_ZA // wide

    def body(it_ref, jt_ref, q_ref, k_ref, vt_ref, za_ref, attn_ref, oa_ref, qb_ref, m_s, acc_s, pair_s):
        step = pl.program_id(1)
        i, j = it_ref[step], jt_ref[step]

        @pl.when(j == 0)
        def _():
            m_s[...] = jnp.full_like(m_s, NEG)
            acc_s[...] = jnp.zeros_like(acc_s)

        def update(masked):
            for hh in range(hp):
                st = _dot_nt(k_ref[hh], q_ref[hh])
                if masked:
                    st = jnp.where(_causal_t(t), st, NEG)
                m_prev = m_s[hh]
                m_next = jnp.maximum(m_prev, jnp.max(st, axis=0, keepdims=True))
                alpha = jnp.exp(m_prev - m_next)
                pt = jnp.exp(st - m_next).astype(BF16)
                acc_s[hh] = acc_s[hh] * alpha + _dot(vt_ref[hh], pt)
                m_s[hh] = m_next

        @pl.when(j < i)
        def _():
            update(False)

        @pl.when(j == i)
        def _():
            update(True)
            row = lax.broadcasted_iota(jnp.int32, (LANES, t), 0)
            lane = lax.broadcasted_iota(jnp.int32, (t, LANES), 1)
            for hh in range(hp):
                l_row = acc_s[hh, L_ONE_Q:L_ONE_Q + 1, :]
                pair_s[hh * HEAD_DIM:(hh + 1) * HEAD_DIM, :] = acc_s[hh, 0:HEAD_DIM, :] / l_row
                lse3 = _split3(m_s[hh] + jnp.log(l_row))
                tail_t = _place3(row, L_LSE_Q, tuple(-x for x in lse3), 0.0)
                keep_q = jnp.logical_or(lane < L_LSE_Q, lane >= L_END)
                qb_ref[hh] = jnp.where(keep_q, q_ref[hh].astype(F32), tail_t.T).astype(BF16)
            out = pair_s[...].T
            attn_ref[...] = out
            z = za_ref[...].astype(F32)
            oa_ref[...] = (out * (z * _sigmoid(z))).astype(BF16)

    pair_q = pl.BlockSpec((hp, t, LANES), lambda p, n, it_, jt_: (p, it_[n], 0))
    pair_k = pl.BlockSpec((hp, t, LANES), lambda p, n, it_, jt_: (p, jt_[n], 0))
    pair_kt = pl.BlockSpec((hp, LANES, t), lambda p, n, it_, jt_: (p, 0, jt_[n]))
    out_q = pl.BlockSpec((t, wide), lambda p, n, it_, jt_: (it_[n], p))
    return pl.pallas_call(
        body, name="attn_fwd",
        grid_spec=pltpu.PrefetchScalarGridSpec(
            num_scalar_prefetch=2, grid=(HEADS // hp, it.shape[0]),
            in_specs=[pair_q, pair_k, pair_kt,
                      pl.BlockSpec((t, wide), lambda p, n, it_, jt_: (it_[n], za_blk + p))],
            out_specs=[out_q, out_q, pair_q],
            scratch_shapes=[pltpu.VMEM((hp, 1, t), F32), pltpu.VMEM((hp, LANES, t), F32),
                            pltpu.VMEM((wide, t), F32)]),
        out_shape=[jax.ShapeDtypeStruct((s, ATTN_W), F32),
                   jax.ShapeDtypeStruct((s, ATTN_W), BF16),
                   jax.ShapeDtypeStruct((HEADS, s, LANES), BF16)],
        compiler_params=_params(("parallel", "arbitrary")),
    )(it, jt, qa, ka, vt, proj)


def _conv_parts(gb_ref, gc_ref, u_ref, zb_ref, gch_ref, uh_ref, first, w_ref, tm):
    gb, gc = gb_ref[...].astype(F32), gc_ref[...].astype(F32)
    u, zb = u_ref[...].astype(F32), zb_ref[...].astype(F32)
    cu = gc * u
    cu_h = jnp.where(first, 0.0, gch_ref[...].astype(F32) * uh_ref[...].astype(F32))
    prev1, prev2 = _sub_row(cu_h, HALO - 1), _sub_row(cu_h, HALO - 2)
    row = lax.broadcasted_iota(jnp.int32, cu.shape, 0)
    r1 = jnp.where(row == 0, prev1, pltpu.roll(cu, 1, 0))
    r2 = jnp.where(row == 0, prev2, jnp.where(row == 1, prev1, pltpu.roll(cu, 2, 0)))
    conv = w_ref[2:3, :] * cu + w_ref[1:2, :] * r1 + w_ref[0:1, :] * r2
    return gb, gc, u, zb, cu, r1, r2, conv


def _conv_specs(tm, s, width=LANES):
    def tile(off):
        return pl.BlockSpec((tm, width), lambda c, i: (i, off // width + c))

    def before(off):
        return pl.BlockSpec((HALO, width), lambda c, i: (jnp.maximum(i * (tm // HALO) - 1, 0), off // width + c))

    def after(off):
        return pl.BlockSpec((HALO, width),
                            lambda c, i: (jnp.minimum((i + 1) * (tm // HALO), s // HALO - 1), off // width + c))

    return ([tile(OFF_CB), tile(OFF_CC), tile(OFF_CU), tile(OFF_CZ)], [before(OFF_CC), before(OFF_CU)],
            [after(OFF_CB), after(OFF_CZ)])


def _tail(oa, attn, proj, x, target, ada3, wa, wb, wo, conv_w):
    s = x.shape[0]
    tm = min(TM_TAIL, s)
    gab_blk = OFF_GA // (2 * D_MODEL)
    za_blk = OFF_ZA // ATTN_W
    tiles, befores, _ = _conv_specs(tm, s, CONV_W)

    def body(oa_ref, attn_ref, za_ref, gb_ref, gc_ref, u_ref, zb_ref, gch_ref, uh_ref, cw_ref, gab_ref, x_ref, t_ref,
             ada_ref, wag_ref, wbg_ref, wo_ref,
             dy_ref, dgab_ref, do_ref, dza_ref, dob_ref, dwo_out, dwa_out, dwb_out, dgate_ref, loss_ref,
             dwo_ref, dwa_ref, dwb_ref, wa_ref, wb_ref, a2_s, b2_s, sa_s, sb_s, mb_s, da2_s, db2_s):
        first = pl.program_id(0) == 0

        @pl.when(first)
        def _():
            for p in range(N_DEV):
                wa_ref[:, p * LANES:(p + 1) * LANES] = wag_ref[p]
                wb_ref[:, p * LANES:(p + 1) * LANES] = wbg_ref[p]
            dwo_ref[...] = jnp.zeros_like(dwo_ref)
            dwa_ref[...] = jnp.zeros_like(dwa_ref)
            dwb_ref[...] = jnp.zeros_like(dwb_ref)
            dgate_ref[...] = jnp.zeros_like(dgate_ref)
            loss_ref[...] = jnp.zeros_like(loss_ref)

        gb, _, _, zb, _, _, _, conv = _conv_parts(gb_ref, gc_ref, u_ref, zb_ref, gch_ref, uh_ref, first, cw_ref, tm)
        ob_v = (gb * conv * (zb * _sigmoid(zb))).astype(BF16)
        oa_v = oa_ref[...]
        wa_v, wb_v, wo_v = wa_ref[...], wb_ref[...], wo_ref[...]
        blocks = [slice(j * TC_TAIL, (j + 1) * TC_TAIL) for j in range(D_MODEL // TC_TAIL)]
        for cs in blocks:
            a2 = _dot(oa_v, wa_ref[:, cs])
            b2 = _dot(ob_v, wb_ref[:, cs])
            sa = _sigmoid(gab_ref[:, cs].astype(F32))
            sb = _sigmoid(gab_ref[:, D_MODEL + cs.start:D_MODEL + cs.stop].astype(F32))
            a2_s[:, cs] = a2
            b2_s[:, cs] = b2
            sa_s[:, cs] = sa
            sb_s[:, cs] = sb
            mb_s[:, cs] = (sa * a2 + sb * b2).astype(BF16)
        mb = mb_s[...]
        mo = _dot(mb, wo_v)
        gate = ada_ref[2:3, :]
        err = (x_ref[...] + gate * mo) - t_ref[...]
        dy = err * (1.0 / D_MODEL)
        dy_ref[...] = dy
        loss_ref[...] += 0.5 * jnp.sum(err * err) * (1.0 / D_MODEL)
        dgate_ref[...] += jnp.sum(dy * mo, axis=0, keepdims=True)
        dmo = (dy * gate).astype(BF16)
        dwo_ref[...] += _dot_tn(mb, dmo)
        for cs in blocks:
            dmerged = _dot_nt(dmo, wo_ref[cs, :])
            sa, sb = sa_s[:, cs], sb_s[:, cs]
            da2_s[:, cs] = (dmerged * sa).astype(BF16)
            db2_s[:, cs] = (dmerged * sb).astype(BF16)
            dgab_ref[:, cs] = (dmerged * a2_s[:, cs] * (sa * (1.0 - sa))).astype(BF16)
            dgab_ref[:, D_MODEL + cs.start:D_MODEL + cs.stop] = (dmerged * b2_s[:, cs] * (sb * (1.0 - sb))).astype(BF16)
        da2, db2 = da2_s[...], db2_s[...]
        doa = _dot_nt(da2, wa_v)
        dob_ref[...] = _dot_nt(db2, wb_v)
        dwa_ref[...] += _dot_tn(oa_v, da2)
        dwb_ref[...] += _dot_tn(ob_v, db2)

        lane = lax.broadcasted_iota(jnp.int32, (tm, LANES), 1)
        lo = lane < HEAD_DIM
        for pr in range(ATTN_W // LANES):
            sl = slice(pr * LANES, (pr + 1) * LANES)
            g, a, z = doa[:, sl], attn_ref[:, sl], za_ref[:, sl].astype(F32)
            sg = _sigmoid(z)
            dat = (g * (z * sg)).astype(BF16).astype(F32)
            prod = dat * a
            dza_ref[:, sl] = (g * a * (sg * (1.0 + z * (1.0 - sg)))).astype(BF16)
            for hh in range(2):
                sel = lo if hh == 0 else jnp.logical_not(lo)
                delta3 = _split3(jnp.sum(jnp.where(sel, prod, 0.0), axis=-1, keepdims=True))
                dh = dat if hh == 0 else pltpu.roll(dat, HEAD_DIM, 1)
                tail_lanes = _place3(lane, L_ONE_Q, tuple(-d for d in delta3), 0.0)
                do_ref[2 * pr + hh] = jnp.where(lo, dh, tail_lanes).astype(BF16)

        @pl.when(pl.program_id(0) == pl.num_programs(0) - 1)
        def _():
            dwo_out[...] = dwo_ref[...].astype(BF16)
            for p in range(N_DEV):
                dwa_out[p] = dwa_ref[:, p * LANES:(p + 1) * LANES].astype(BF16)
                dwb_out[p] = dwb_ref[:, p * LANES:(p + 1) * LANES].astype(BF16)

    half = pl.BlockSpec((tm, ATTN_W), lambda i: (i, 0))
    full = pl.BlockSpec((tm, D_MODEL), lambda i: (i, 0))

    def const(shape):
        return pl.BlockSpec(shape, lambda i: (0, 0))

    slabs = pl.BlockSpec((N_DEV, ATTN_W, LANES), lambda i: (0, 0, 0))

    def one_axis(spec):
        return pl.BlockSpec(spec.block_shape, lambda i, f=spec.index_map: f(0, i))

    return pl.pallas_call(
        body, name="tail", grid=(s // tm,),
        in_specs=[half, half, pl.BlockSpec((tm, ATTN_W), lambda i: (i, za_blk))]
        + [one_axis(sp) for sp in tiles + befores]
        + [const((3, CONV_W)), pl.BlockSpec((tm, 2 * D_MODEL), lambda i: (i, gab_blk)), full, full,
           const((3, D_MODEL)), slabs, slabs, const((D_MODEL, D_MODEL))],
        out_specs=[full, pl.BlockSpec((tm, 2 * D_MODEL), lambda i: (i, 0)),
                   pl.BlockSpec((HEADS, tm, LANES), lambda i: (0, i, 0)), half, half,
                   const((D_MODEL, D_MODEL)), slabs, slabs, const((1, D_MODEL)), const((1, LANES))],
        out_shape=[jax.ShapeDtypeStruct((s, D_MODEL), F32),
                   jax.ShapeDtypeStruct((s, 2 * D_MODEL), BF16),
                   jax.ShapeDtypeStruct((HEADS, s, LANES), BF16),
                   jax.ShapeDtypeStruct((s, ATTN_W), BF16),
                   jax.ShapeDtypeStruct((s, CONV_W), F32),
                   jax.ShapeDtypeStruct((D_MODEL, D_MODEL), BF16),
                   jax.ShapeDtypeStruct((N_DEV, ATTN_W, LANES), BF16),
                   jax.ShapeDtypeStruct((N_DEV, CONV_W, LANES), BF16),
                   jax.ShapeDtypeStruct((1, D_MODEL), F32),
                   jax.ShapeDtypeStruct((1, LANES), F32)],
        scratch_shapes=[pltpu.VMEM((D_MODEL, D_MODEL), F32), pltpu.VMEM((ATTN_W, D_MODEL), F32),
                        pltpu.VMEM((CONV_W, D_MODEL), F32), pltpu.VMEM((ATTN_W, D_MODEL), BF16),
                        pltpu.VMEM((CONV_W, D_MODEL), BF16),
                        pltpu.VMEM((tm, D_MODEL), F32)] + [pltpu.VMEM((tm, D_MODEL), F32)] * 3
        + [pltpu.VMEM((tm, D_MODEL), BF16)] * 3,
        compiler_params=_params(("arbitrary",)),
    )(oa, attn, proj, *([proj] * 6), conv_w, proj, x, target, ada3, wa, wb, wo)


def _attn_bwd(qb, ka, kt, va, do, proj, qg, kg):
    s = qb.shape[1]
    t = min(TQ, s)
    nt = s // t
    hp = HEADS_PER_STEP
    assert hp == HEADS, "all heads share one (S, 1536) output block"
    wide = hp * HEAD_DIM
    scale = HEAD_DIM ** -0.5
    it, jt = _tri_steps(nt, False)

    def body(it_ref, jt_ref, q_ref, k_ref, kt_ref, v_ref, do_ref, qraw_ref, kraw_ref, qg_ref, kg_ref,
             dqkv_ref, dqg_ref, dkg_ref, dcum_ref, dqt_s, dk_s, dv_s, rows_s):
        grp, step = pl.program_id(0), pl.program_id(1)
        i, j = it_ref[step], jt_ref[step]
        lane = lax.broadcasted_iota(jnp.int32, (t, LANES), 1)
        lo = lane < HEAD_DIM

        @pl.when(step == 0)
        def _():
            dqt_s[...] = jnp.zeros_like(dqt_s)
            dqg_ref[...] = jnp.zeros_like(dqg_ref)
            dkg_ref[...] = jnp.zeros_like(dkg_ref)

        @pl.when(i == j)
        def _():
            dk_s[...] = jnp.zeros_like(dk_s)
            dv_s[...] = jnp.zeros_like(dv_s)

        def update(masked):
            for hh in range(hp):
                qh, doh = q_ref[hh], do_ref[hh]
                st = _dot_nt(k_ref[hh], qh)
                if masked:
                    st = jnp.where(_causal_t(t), st, NEG)
                pt = jnp.exp(st)
                dst = (pt * _dot_nt(v_ref[hh], doh)).astype(BF16)
                dv_s[hh] += _dot(pt.astype(BF16), doh)
                dk_s[hh] += _dot(dst, qh)
                dqt_s[hh, i] += _dot(kt_ref[hh], dst)

        def pair(a, b):
            return jnp.where(lo, a, pltpu.roll(b, HEAD_DIM, 1))

        def norm_bwd(raw, dy, g, dg_ref, off, sl):
            r = lax.rsqrt(_seg_sum(raw * raw, lo) * (1.0 / HEAD_DIM) + EPS)
            xhat = raw * r
            dg_ref[:, sl] += jnp.sum(dy * xhat, axis=0, keepdims=True)
            dxh = dy * g
            dx = r * (dxh - xhat * (_seg_sum(dxh * xhat, lo) * (1.0 / HEAD_DIM)))
            dqkv_ref[:, off + sl.start:off + sl.stop] = dx.astype(BF16)

        @pl.when(i > j)
        def _():
            update(False)

        @pl.when(i == j)
        def _():
            update(True)
            dq_rows = [dqt_s[hh, i].T for hh in range(hp)]
            rows = jnp.zeros((t, LANES), F32)
            for hh in range(hp):
                rows = jnp.where(lane == grp * hp + hh, _lane_col(dq_rows[hh], L_F_Q), rows)
            rows_s[...] = rows
            for pr in range(hp // 2):
                sl = slice(pr * LANES, (pr + 1) * LANES)
                norm_bwd(qraw_ref[:, sl].astype(F32), pair(dq_rows[2 * pr], dq_rows[2 * pr + 1]) * scale,
                         qg_ref[:, sl], dqg_ref, OFF_Q, sl)

        @pl.when(i == nt - 1)
        def _():
            dcum = rows_s[...]
            for hh in range(hp):
                dcum = jnp.where(lane == grp * hp + hh, dcum - _lane_col(dk_s[hh], L_ONE_Q), dcum)
            dcum_ref[0] = dcum
            for pr in range(hp // 2):
                sl = slice(pr * LANES, (pr + 1) * LANES)
                norm_bwd(kraw_ref[:, sl].astype(F32), pair(dk_s[2 * pr], dk_s[2 * pr + 1]),
                         kg_ref[:, sl], dkg_ref, OFF_K, sl)
                dqkv_ref[:, OFF_V + sl.start:OFF_V + sl.stop] = pair(dv_s[2 * pr], dv_s[2 * pr + 1]).astype(BF16)

    pair_q = pl.BlockSpec((hp, t, LANES), lambda p, n, it_, jt_: (p, it_[n], 0))
    pair_k = pl.BlockSpec((hp, t, LANES), lambda p, n, it_, jt_: (p, jt_[n], 0))
    pair_kt = pl.BlockSpec((hp, LANES, t), lambda p, n, it_, jt_: (p, 0, jt_[n]))
    tok3 = pl.BlockSpec((t, 3 * ATTN_W), lambda p, n, it_, jt_: (jt_[n], 0))
    gain = pl.BlockSpec((1, wide), lambda p, n, it_, jt_: (0, p))
    return pl.pallas_call(
        body, name="attn_bwd",
        grid_spec=pltpu.PrefetchScalarGridSpec(
            num_scalar_prefetch=2, grid=(HEADS // hp, it.shape[0]),
            in_specs=[pair_q, pair_k, pair_kt, pair_k, pair_q,
                      pl.BlockSpec((t, wide), lambda p, n, it_, jt_: (jt_[n], OFF_Q // wide + p)),
                      pl.BlockSpec((t, wide), lambda p, n, it_, jt_: (jt_[n], OFF_K // wide + p)), gain, gain],
            out_specs=[tok3, gain, gain,
                       pl.BlockSpec((1, t, LANES), lambda p, n, it_, jt_: (p, jt_[n], 0))],
            scratch_shapes=[pltpu.VMEM((hp, nt, LANES, t), F32), pltpu.VMEM((hp, t, LANES), F32),
                            pltpu.VMEM((hp, t, LANES), F32), pltpu.VMEM((t, LANES), F32)]),
        out_shape=[jax.ShapeDtypeStruct((s, 3 * ATTN_W), BF16)]
        + [jax.ShapeDtypeStruct((1, ATTN_W), F32)] * 2
        + [jax.ShapeDtypeStruct((HEADS // hp, s, LANES), F32)],
        compiler_params=_params(("parallel", "arbitrary")),
    )(it, jt, qb, ka, kt, va, do, proj, proj, qg, kg)


def _forget_bwd(dcum, fl, bf_pad):
    s = fl.shape[0]
    tc = min(TC_CUM, s)
    n = s // tc

    def body(dc_ref, fl_ref, bf_ref, df_ref, dbf_ref, carry):
        @pl.when(pl.program_id(0) == 0)
        def _():
            carry[...] = jnp.zeros_like(carry)
            dbf_ref[...] = jnp.zeros_like(dbf_ref)
        r = lax.broadcasted_iota(jnp.int32, (tc, tc), 0)
        cidx = lax.broadcasted_iota(jnp.int32, (tc, tc), 1)
        tri = (cidx >= r).astype(F32)
        dc = dc_ref[0]
        for grp in range(1, dcum.shape[0]):
            dc = dc + dc_ref[grp]
        dlf = jnp.dot(tri, dc, preferred_element_type=F32, precision=lax.Precision.HIGHEST) + carry[...]
        carry[...] += jnp.sum(dc, axis=0, keepdims=True)
        lane = lax.broadcasted_iota(jnp.int32, (tc, LANES), 1)
        dfl = jnp.where(lane < HEADS, dlf * _sigmoid(-(fl_ref[...] + bf_ref[...])), 0.0)
        df_ref[...] = dfl.astype(BF16)
        dbf_ref[...] += jnp.sum(dfl, axis=0, keepdims=True)

    rev = pl.BlockSpec((tc, LANES), lambda i: (n - 1 - i, 0))
    vec = pl.BlockSpec((1, LANES), lambda i: (0, 0))
    return pl.pallas_call(
        body, name="forget_bwd", grid=(n,),
        in_specs=[pl.BlockSpec((dcum.shape[0], tc, LANES), lambda i: (0, n - 1 - i, 0)), rev, vec],
        out_specs=[rev, vec],
        out_shape=[jax.ShapeDtypeStruct((s, LANES), BF16), jax.ShapeDtypeStruct((1, LANES), F32)],
        scratch_shapes=[pltpu.VMEM((1, LANES), F32)],
        compiler_params=_params(("arbitrary",)),
    )(dcum, fl, bf_pad)


def _conv_bwd(dob, proj, conv_w):
    s = dob.shape[0]
    tm = min(TM_ELEM, s)
    wd = CONV_W
    tiles, befores, afters = _conv_specs(tm, s, wd)

    def body(dob_ref, dnext_ref, gb_ref, gc_ref, u_ref, zb_ref, gch_ref, uh_ref, gbn_ref, zbn_ref, w_ref,
             out_ref, dw_ref):
        i = pl.program_id(1)

        @pl.when(i == 0)
        def _():
            dw_ref[...] = jnp.zeros_like(dw_ref)
        gb, gc, u, zb, cu, r1, r2, conv = _conv_parts(gb_ref, gc_ref, u_ref, zb_ref, gch_ref, uh_ref, i == 0, w_ref, tm)
        g = dob_ref[...]
        sg = _sigmoid(zb)
        sz = zb * sg
        dconv = g * gb * sz
        zn = zbn_ref[0:8, :].astype(F32)
        dcn = jnp.where(i == pl.num_programs(1) - 1, 0.0,
                        dnext_ref[...] * gbn_ref[0:8, :].astype(F32) * (zn * _sigmoid(zn)))
        nxt1, nxt2 = _sub_row(dcn, 0), _sub_row(dcn, 1)
        row = lax.broadcasted_iota(jnp.int32, (tm, wd), 0)
        f1 = jnp.where(row == tm - 1, nxt1, pltpu.roll(dconv, tm - 1, 0))
        f2 = jnp.where(row == tm - 2, nxt1, jnp.where(row == tm - 1, nxt2, pltpu.roll(dconv, tm - 2, 0)))
        dcu = w_ref[2:3, :] * dconv + w_ref[1:2, :] * f1 + w_ref[0:1, :] * f2
        out_ref[:, 0:wd] = (g * conv * sz).astype(BF16)
        out_ref[:, wd:2 * wd] = (dcu * u).astype(BF16)
        out_ref[:, 2 * wd:3 * wd] = (dcu * gc).astype(BF16)
        out_ref[:, 3 * wd:4 * wd] = (g * gb * conv * (sg * (1.0 + zb * (1.0 - sg)))).astype(BF16)
        w_row = lax.broadcasted_iota(jnp.int32, (3, wd), 0)
        dw0 = jnp.sum(dconv * r2, axis=0, keepdims=True)
        dw1 = jnp.sum(dconv * r1, axis=0, keepdims=True)
        dw2 = jnp.sum(dconv * cu, axis=0, keepdims=True)
        dw_ref[...] += jnp.where(w_row == 0, dw0, jnp.where(w_row == 1, dw1, dw2))

    blk = pl.BlockSpec((tm, wd), lambda c, i: (i, c))
    nxt = pl.BlockSpec((8, wd), lambda c, i: (jnp.minimum((i + 1) * (tm // 8), s // 8 - 1), c))
    wspec = pl.BlockSpec((3, wd), lambda c, i: (0, c))
    return pl.pallas_call(
        body, name="conv_bwd", grid=(CONV_W // wd, s // tm),
        in_specs=[blk, nxt] + tiles + befores + afters + [wspec],
        out_specs=[pl.BlockSpec((tm, 4 * wd), lambda c, i: (i, c)), wspec],
        out_shape=[jax.ShapeDtypeStruct((s, 4 * CONV_W), BF16), jax.ShapeDtypeStruct((3, CONV_W), F32)],
        compiler_params=_params(("parallel", "arbitrary")),
    )(dob, dob, *([proj] * 8), conv_w)


def _piece_layout(pieces):
    offs, off = [], 0
    for p in pieces:
        offs.append((off, p.shape[1]))
        off += p.shape[1]
    assert off == N_ALL, off
    return offs


def _dw_in(h, pieces, chip_sums):
    s = h.shape[0]
    tk, tn = min(TK_DW, s), TN_DW
    nk = s // tk
    nn = N_MAIN // tn
    main, fpiece = pieces[:-1], pieces[-1]
    layout = _piece_layout(pieces)[:-1]
    n_main = len(main)
    nx = len(chip_sums)

    def body(*refs):
        p_refs, f_ref, h_ref = refs[:n_main], refs[n_main], refs[n_main + 1]
        ins, refs = refs[n_main + 2:n_main + 2 + nx], refs[n_main + 2 + nx:]
        out_ref, outf_ref = refs[:2]
        outs, (acc, accf, send_sems, recv_sems, local_sems) = refs[2:2 + nx], refs[2 + nx:]
        n, k = pl.program_id(0), pl.program_id(1)

        @pl.when(jnp.logical_and(n == 0, k == 0))
        def _():
            for cp in _all_to_all_copies(ins, outs, send_sems, recv_sems, local_sems):
                cp.start()

        @pl.when(k == 0)
        def _():
            acc[...] = jnp.zeros_like(acc)
        hv = h_ref[pl.ds(pl.multiple_of(k * tk, tk), tk), :]
        for p_ref, (off, width) in zip(p_refs, layout):
            @pl.when(jnp.logical_and(n >= off // tn, n < (off + width) // tn))
            def _():
                acc[...] += _dot_tn(p_ref[...], hv)

        @pl.when(k == nk - 1)
        def _():
            out_ref[...] = acc[...].astype(BF16)

        @pl.when(n == 0)
        def _():
            @pl.when(k == 0)
            def _():
                accf[...] = jnp.zeros_like(accf)
            accf[...] += _dot_tn(f_ref[...], hv)

            @pl.when(k == nk - 1)
            def _():
                outf_ref[...] = accf[...].astype(BF16)

        @pl.when(jnp.logical_and(n == nn - 1, k == nk - 1))
        def _():
            for cp in _all_to_all_copies(ins, outs, send_sems, recv_sems, local_sems):
                cp.wait()

    def piece_spec(off, width):
        lo, hi = off // tn, (off + width) // tn

        def index(n, k):
            active = jnp.logical_and(n >= lo, n < hi)
            return jnp.where(active, k, 0), jnp.clip(n - lo, 0, hi - lo - 1)
        return pl.BlockSpec((tk, tn), index)

    any_spec = pl.BlockSpec(memory_space=pl.ANY)
    res = pl.pallas_call(
        body, name="dw_in", grid=(nn, nk),
        in_specs=[piece_spec(off, width) for off, width in layout]
        + [pl.BlockSpec((tk, N_FPAD), lambda n, k: (jnp.where(n == 0, k, 0), 0)),
           pl.BlockSpec((s, D_MODEL), lambda n, k: (0, 0))] + [any_spec] * nx,
        out_specs=[pl.BlockSpec((tn, D_MODEL), lambda n, k: (n, 0)),
                   pl.BlockSpec((N_FPAD, D_MODEL), lambda n, k: (0, 0))] + [any_spec] * nx,
        out_shape=[jax.ShapeDtypeStruct((N_MAIN, D_MODEL), BF16), jax.ShapeDtypeStruct((N_FPAD, D_MODEL), BF16)]
        + [jax.ShapeDtypeStruct(a.shape, a.dtype) for a in chip_sums],
        scratch_shapes=[pltpu.VMEM((tn, D_MODEL), F32), pltpu.VMEM((N_FPAD, D_MODEL), F32)] + _gather_sems(nx),
        compiler_params=_params(("arbitrary", "arbitrary")),
    )(*main, fpiece, h, *chip_sums)
    return res[:2], res[2:]


def _dh_and_dx(pieces, w_all_t, x, dy, ada3, norm_g, chip_sums):
    s = x.shape[0]
    tm = min(TM_DH, s)
    nt = s // tm
    n = len(chip_sums)
    npc = len(pieces)
    layout = _piece_layout(pieces)

    def body(*refs):
        p_refs, refs = refs[:npc], refs[npc:]
        wt_ref, x_ref, dy_ref, ada_ref, g_ref = refs[:5]
        ins, refs = refs[5:5 + n], refs[5 + n:]
        gx_ref, dsh_ref, dsc_ref, dg_ref = refs[:4]
        outs, (send_sems, recv_sems, local_sems) = refs[4:4 + n], refs[4 + n:]
        i = pl.program_id(0)

        @pl.when(i == 0)
        def _():
            for cp in _chip_copies(ins, outs, send_sems, recv_sems, local_sems):
                cp.start()
            dsh_ref[...] = jnp.zeros_like(dsh_ref)
            dsc_ref[...] = jnp.zeros_like(dsc_ref)
            dg_ref[...] = jnp.zeros_like(dg_ref)

        dh = None
        for p_ref, (off, width) in zip(p_refs, layout):
            part = _dot(p_ref[...], wt_ref[off:off + width, :])
            dh = part if dh is None else dh + part
        xv = x_ref[...]
        r = lax.rsqrt(jnp.mean(xv * xv, axis=-1, keepdims=True) + EPS)
        xhat = xv * r
        g = g_ref[...]
        one_sc = 1.0 + ada_ref[1:2, :]
        dsh_ref[...] += jnp.sum(dh, axis=0, keepdims=True)
        dsc_ref[...] += jnp.sum(dh * (xhat * g), axis=0, keepdims=True)
        dg_ref[...] += jnp.sum(dh * xhat, axis=0, keepdims=True) * one_sc
        dxh = dh * (g * one_sc)
        dx = r * (dxh - xhat * jnp.mean(dxh * xhat, axis=-1, keepdims=True))
        gx_ref[...] = dy_ref[...] + dx

        @pl.when(i == nt - 1)
        def _():
            for cp in _chip_copies(ins, outs, send_sems, recv_sems, local_sems):
                cp.wait()

    full = pl.BlockSpec((tm, D_MODEL), lambda i: (i, 0))
    vec = pl.BlockSpec((1, D_MODEL), lambda i: (0, 0))
    any_spec = pl.BlockSpec(memory_space=pl.ANY)
    res = pl.pallas_call(
        body, name="dh_dx", grid=(nt,),
        in_specs=[pl.BlockSpec((tm, p.shape[1]), lambda i: (i, 0)) for p in pieces]
        + [pl.BlockSpec((N_ALL, D_MODEL), lambda i: (0, 0)), full, full,
           pl.BlockSpec((3, D_MODEL), lambda i: (0, 0)), vec] + [any_spec] * n,
        out_specs=[full, vec, vec, vec] + [any_spec] * n,
        out_shape=[jax.ShapeDtypeStruct((s, D_MODEL), F32)] + [jax.ShapeDtypeStruct((1, D_MODEL), F32)] * 3
        + [jax.ShapeDtypeStruct(a.shape, a.dtype) for a in chip_sums],
        scratch_shapes=[pltpu.SemaphoreType.DMA((n * 3,)), pltpu.SemaphoreType.DMA((n * 3,)),
                        pltpu.SemaphoreType.DMA((n,))],
        compiler_params=_params(("arbitrary",)),
    )(*pieces, w_all_t, x, dy, ada3, norm_g, *chip_sums)
    return res[:4], res[4:]


def _sum_small(vec_all, qg_parts, kg_parts):
    def body(v_ref, q_ref, k_ref, tot_ref, gq_ref, gk_ref):
        tot = v_ref[0:1, :]
        for p in range(1, N_DEV):
            tot = tot + v_ref[p:p + 1, :]
        tot_ref[...] = tot
        gq_ref[...] = jnp.sum(q_ref[...], axis=0, keepdims=True)
        gk_ref[...] = jnp.sum(k_ref[...], axis=0, keepdims=True)

    n = vec_all.shape[-1]
    return pl.pallas_call(
        body, name="sum_small",
        out_shape=[jax.ShapeDtypeStruct((1, n), F32),
                   jax.ShapeDtypeStruct((1, HEAD_DIM), F32), jax.ShapeDtypeStruct((1, HEAD_DIM), F32)],
        compiler_params=_params(),
    )(vec_all, qg_parts, kg_parts)


def _grad_w_ada(c_rows, dada_rows):
    def body(c_ref, d_ref, out_ref):
        out_ref[...] = jnp.dot(c_ref[...].T, d_ref[...], preferred_element_type=F32,
                               precision=lax.Precision.HIGHEST)

    return pl.pallas_call(
        body, name="grad_w_ada",
        out_shape=jax.ShapeDtypeStruct((D_MODEL, ADA_SHARD), F32),
        compiler_params=_params(),
    )(c_rows, dada_rows)


def _adam_step(w, m, v, g):
    c1 = 1.0 / (1.0 - ADAM_B1 ** ADAM_STEP)
    c2 = 1.0 / (1.0 - ADAM_B2 ** ADAM_STEP)
    m_new = ADAM_B1 * m + (1.0 - ADAM_B1) * g
    v_new = ADAM_B2 * v + (1.0 - ADAM_B2) * (g * g)
    return -ADAM_LR * ((m_new * c1) / (jnp.sqrt(v_new * c2) + ADAM_EPS) + ADAM_WD * w), m_new, v_new


def _adamw_small(params, name):
    n = len(params)
    stacked = [p[3].ndim == p[0].ndim + 1 for p in params]

    def body(*refs):
        ins, outs = refs[:4 * n], refs[4 * n:]
        for k in range(n):
            w_ref, m_ref, v_ref, g_ref = ins[4 * k:4 * k + 4]
            go_ref, d_ref, mo_ref, vo_ref = outs[4 * k:4 * k + 4]
            if stacked[k]:
                g = g_ref[0].astype(F32)
                for p in range(1, g_ref.shape[0]):
                    g = g + g_ref[p].astype(F32)
            else:
                g = g_ref[...]
            go_ref[...] = g
            d_ref[...], mo_ref[...], vo_ref[...] = _adam_step(w_ref[...], m_ref[...], v_ref[...], g)

    res = pl.pallas_call(
        body, name=name,
        out_shape=[jax.ShapeDtypeStruct(p[0].shape, F32) for p in params for _ in range(4)],
        compiler_params=_params(),
    )(*[a for p in params for a in p])
    return [tuple(res[4 * k:4 * k + 4]) for k in range(n)]


def _adamw(w, m, v, g_parts, name):
    rows, cols = w.shape
    n_parts = g_parts.shape[0]
    tr = 256 if rows % 256 == 0 else rows
    tc = 256 if (tr == rows and rows > 256 and cols % 256 == 0) else cols

    def body(w_ref, m_ref, v_ref, g_ref, go_ref, d_ref, mo_ref, vo_ref):
        g = g_ref[0].astype(F32)
        for p in range(1, n_parts):
            g = g + g_ref[p].astype(F32)
        go_ref[...] = g
        d_ref[...], mo_ref[...], vo_ref[...] = _adam_step(w_ref[...], m_ref[...], v_ref[...], g)

    blk = pl.BlockSpec((tr, tc), lambda i, j: (i, j))
    return pl.pallas_call(
        body, name=name, grid=(rows // tr, cols // tc),
        in_specs=[blk, blk, blk, pl.BlockSpec((n_parts, tr, tc), lambda i, j: (0, i, j))],
        out_specs=[blk] * 4,
        out_shape=[jax.ShapeDtypeStruct((rows, cols), F32)] * 4,
        compiler_params=_params(("parallel", "parallel")),
    )(w, m, v, g_parts)


_O_F = 1536


W_TILE = 16
WIN_ROWS = 784


def _internal_start(p):
    return p * IN_SHARD - (HEADS if p * IN_SHARD > _O_F else 0)


def _shard_window(wt_shard, me):
    lo = me * IN_SHARD
    o = lo + lax.broadcasted_iota(jnp.int32, (IN_SHARD, 1), 0)
    is_f = jnp.logical_and(o >= _O_F, o < _O_F + HEADS)
    start = lo - jnp.where(lo > _O_F, HEADS, 0)
    window = lax.dynamic_update_slice(jnp.zeros((WIN_ROWS + W_TILE, D_MODEL), BF16),
                                      jnp.where(is_f, 0.0, wt_shard).astype(BF16), (start % W_TILE, 0))
    first = jnp.clip(_O_F - lo, 0, IN_SHARD - W_TILE)
    near = lax.dynamic_slice(wt_shard, (first, 0), (W_TILE, D_MODEL))
    j = lax.broadcasted_iota(jnp.int32, (W_TILE, 1), 0)
    src = _O_F - lo + j
    ok = jnp.logical_and(j < HEADS, jnp.logical_and(src >= 0, src < IN_SHARD))
    f_tile = jnp.where(ok, jnp.roll(near, first - (_O_F - lo), axis=0), 0.0).astype(BF16)
    return lax.dynamic_update_slice(window, f_tile, (WIN_ROWS, 0))


def _assemble_w(windows):
    chunk = 112

    def body(g_ref, out_ref):
        out_ref[WIN_ROWS:N_MAIN, :] = jnp.zeros((N_MAIN - WIN_ROWS, D_MODEL), BF16)
        for p in range(N_DEV):
            base = _internal_start(p) // W_TILE * W_TILE
            for r in range(0, WIN_ROWS, chunk):
                rows = slice(base + r, base + r + chunk)
                piece = g_ref[p, r:r + chunk, :]
                out_ref[rows, :] = piece if p == 0 else out_ref[rows, :] + piece
        f = g_ref[0, WIN_ROWS:WIN_ROWS + W_TILE, :]
        for p in range(1, N_DEV):
            f = f + g_ref[p, WIN_ROWS:WIN_ROWS + W_TILE, :]
        out_ref[N_MAIN:N_MAIN + W_TILE, :] = f
        out_ref[N_MAIN + W_TILE:N_ALL, :] = jnp.zeros((N_FPAD - W_TILE, D_MODEL), BF16)

    return pl.pallas_call(
        body, name="assemble_w", out_shape=jax.ShapeDtypeStruct((N_ALL, D_MODEL), BF16),
        compiler_params=_params(),
    )(windows)


def _shard_rows(windows, me):
    n = windows.shape[0]
    lo = me * IN_SHARD
    start = lo - jnp.where(lo > _O_F, HEADS, 0)
    main = lax.dynamic_slice(windows, (0, start % W_TILE, 0), (n, IN_SHARD, D_MODEL))
    first = jnp.clip(_O_F - lo, 0, IN_SHARD - W_TILE)
    near = lax.dynamic_slice(main, (0, first, 0), (n, W_TILE, D_MODEL))
    j = first - (_O_F - lo) + lax.broadcasted_iota(jnp.int32, (1, W_TILE, 1), 1)
    f_rows = jnp.roll(windows[:, WIN_ROWS:], -(first - (_O_F - lo)), axis=1)
    patch = jnp.where(jnp.logical_and(j >= 0, j < HEADS), f_rows, near)
    return lax.dynamic_update_slice(main, patch, (0, first, 0))


def kernel(x, c, w_ada, b_ada, norm_g, w_in, b_f, q_norm_g, k_norm_g, conv_w, w_attn_out, w_conv_out, w_o, loss_target, m_w_ada, m_b_ada, m_norm_g, m_w_in, m_b_f, m_q_norm_g, m_k_norm_g, m_conv_w, m_w_attn_out, m_w_conv_out, m_w_o, v_w_ada, v_b_ada, v_norm_g, v_w_in, v_b_f, v_q_norm_g, v_k_norm_g, v_conv_w, v_w_attn_out, v_w_conv_out, v_w_o):
    me = 4 * lax.axis_index("x") + 2 * lax.axis_index("y") + lax.axis_index("c")
    s = x.shape[1]
    x2, t2 = x[0], loss_target[0]

    w_in_g, c_all, ada_g = _gather_weights_and_ada(_shard_window(w_in[0].T, me), c, w_ada[0])
    ada_mine = lax.dynamic_index_in_dim(ada_g[:, :, 0, :], me, axis=1, keepdims=False)
    ada3 = (ada_mine.reshape(1, 3 * D_MODEL) + b_ada).reshape(3, D_MODEL)
    w_all_t = _assemble_w(w_in_g)
    qg = jnp.tile(q_norm_g, (1, HEADS))
    kg = jnp.tile(k_norm_g, (1, HEADS))
    bf_pad = jnp.pad(b_f, ((0, 0), (0, LANES - HEADS)))

    (proj, fl, h), (qa, ka, va, kt, vt), (cw_g, wa_g, wb_g, wo_g) = _proj_fwd(
        x2, ada3, norm_g, w_all_t, bf_pad, qg, kg,
        [conv_w[0], w_attn_out[0].astype(BF16), w_conv_out[0].astype(BF16), w_o[0].astype(BF16)])
    wo = wo_g.reshape(D_MODEL, D_MODEL)
    cw = jnp.transpose(cw_g, (1, 0, 2)).reshape(3, CONV_W)
    attn, oa, qb = _attn_fwd(qa, ka, vt, proj)
    (dy, dgab, do, dza, dob, dwo, dwa, dwb, dgate, loss_part) = _tail(oa, attn, proj, x2, t2, ada3, wa_g, wb_g, wo, cw)

    small = [dwa, dwb, dwo.reshape(N_DEV, D_MODEL // N_DEV, D_MODEL)]
    dqkv, dqg, dkg, dcum = _attn_bwd(qb, ka, kt, va, do, proj, qg, kg)
    df, dbf = _forget_bwd(dcum, fl, bf_pad)
    dconv, dcw = _conv_bwd(dob, proj, cw)
    pieces = [dqkv, dza, dconv, dgab, df]
    (dw_main, dw_f), (g_wa_parts, g_wb_parts, g_wo_parts) = _dw_in(h, pieces, small)

    pair_in = _sibling_swap_sum(dw_main, dw_f, "swap_sum_w_in")
    (grad_x, dshift, dscale, dnormg), (g_in_windows,) = _dh_and_dx(
        pieces, w_all_t, x2, dy, ada3, norm_g, [pair_in])
    g_in_parts = _shard_rows(g_in_windows, me)
    vec = jnp.concatenate([dshift, dscale, dgate, dnormg, dbf, dcw.reshape(1, 3 * CONV_W), loss_part, dqg, dkg],
                          axis=1)
    (vec_all,) = _gather_direct([vec], "gather_small")
    vec_all = vec_all.reshape(N_DEV, vec.shape[1])
    n_main = 4 * D_MODEL + LANES + 3 * CONV_W + LANES
    tot, g_qg, g_kg = _sum_small(
        vec_all[:, :n_main],
        vec_all[:, n_main:n_main + ATTN_W].reshape(N_DEV * HEADS, HEAD_DIM),
        vec_all[:, n_main + ATTN_W:].reshape(N_DEV * HEADS, HEAD_DIM))
    g_b_ada = tot[:, 0:3 * D_MODEL]
    g_norm_g = tot[:, 3 * D_MODEL:4 * D_MODEL]
    g_b_f = tot[:, 4 * D_MODEL:4 * D_MODEL + HEADS]
    g_cw_full = tot[:, 4 * D_MODEL + LANES:4 * D_MODEL + LANES + 3 * CONV_W].reshape(3, CONV_W)
    g_cw = lax.dynamic_slice(g_cw_full, (0, me * (CONV_W // N_DEV)), (3, CONV_W // N_DEV))
    dada_mine = lax.dynamic_slice(vec_all[:, 0:3 * D_MODEL], (0, me * ADA_SHARD), (N_DEV, ADA_SHARD))
    g_w_ada = _grad_w_ada(c_all.reshape(N_DEV, D_MODEL), dada_mine)

    upd = {}
    upd["w_ada"] = _adamw(w_ada[0], m_w_ada[0], v_w_ada[0], g_w_ada[None], "adamw_w_ada")
    upd["w_in"] = [u.T for u in _adamw(w_in[0].T, m_w_in[0].T, v_w_in[0].T, g_in_parts, "adamw_w_in")]
    small_names = ["b_ada", "norm_g", "b_f", "q_norm_g", "k_norm_g", "conv_w", "w_attn_out", "w_conv_out", "w_o"]
    small_upd = _adamw_small(
        [(b_ada, m_b_ada, v_b_ada, g_b_ada), (norm_g, m_norm_g, v_norm_g, g_norm_g), (b_f, m_b_f, v_b_f, g_b_f),
         (q_norm_g, m_q_norm_g, v_q_norm_g, g_qg), (k_norm_g, m_k_norm_g, v_k_norm_g, g_kg),
         (conv_w[0], m_conv_w[0], v_conv_w[0], g_cw),
         (w_attn_out[0], m_w_attn_out[0], v_w_attn_out[0], g_wa_parts),
         (w_conv_out[0], m_w_conv_out[0], v_w_conv_out[0], g_wb_parts),
         (w_o[0], m_w_o[0], v_w_o[0], g_wo_parts)], "adamw_small")
    upd.update(zip(small_names, small_upd))

    names = ["w_ada", "b_ada", "norm_g", "w_in", "b_f", "q_norm_g", "k_norm_g", "conv_w",
             "w_attn_out", "w_conv_out", "w_o"]
    lead = {"w_ada", "w_in", "conv_w", "w_attn_out", "w_conv_out", "w_o"}
    fix = lambda n, a: a[None] if n in lead else a
    loss = tot[0, n_main - LANES]
    outs = [loss, grad_x[None]]
    for k in range(4):
        outs += [fix(n, upd[n][k]) for n in names]
    return tuple(outs)
```
